```python
import jax, jax.numpy as jnp
from jax import lax
import numpy as np

D_MODEL = 1024
BATCH = 8
SEQ = 2048
DEPTH = 1

CHUNK = 64
EPS = 1e-6
POOL_WIDTH = 512
POOL_WINDOWS = (2, 4, 8, 16)
POOL_GROUPS = len(POOL_WINDOWS)
POOL_GROUP_DIM = POOL_WIDTH // POOL_GROUPS
POOL_MAX_WIN = max(POOL_WINDOWS)
GLA_HEADS = 4
GLA_DK = D_MODEL // 2
GLA_DV = D_MODEL
GLA_HK = GLA_DK // GLA_HEADS
GLA_HV = GLA_DV // GLA_HEADS
GLA_GATE_RANK = 16
GLA_GATE_NORM = 16.0
N_BRANCH = 2
D_FF = 2816
CONV_W = 3

IN_WIDTHS = (POOL_WIDTH, GLA_DK, GLA_DK, GLA_DV, GLA_DV, GLA_GATE_RANK, N_BRANCH * D_MODEL)
IN_TOTAL = sum(IN_WIDTHS)
IN_OFFSETS = tuple(int(o) for o in np.cumsum(IN_WIDTHS)[:-1])

kernel_name = "hybrid_pool_gla_convffn"


def rmsnorm(x, g):
    xf = x.astype(jnp.float32)
    y = xf * lax.rsqrt(jnp.mean(xf * xf, axis=-1, keepdims=True) + EPS)
    return (y * g.astype(jnp.float32)).astype(x.dtype)


def pool_mixer(u, w_grp, scale):
    B, S, _ = u.shape
    uf = u.astype(jnp.float32)
    cs = jnp.pad(jnp.cumsum(uf, axis=1), ((0, 0), (POOL_MAX_WIN, 0), (0, 0)))
    pos = jnp.arange(S, dtype=jnp.float32)
    outs = []
    for g, w in enumerate(POOL_WINDOWS):
        sl = slice(g * POOL_GROUP_DIM, (g + 1) * POOL_GROUP_DIM)
        win_sum = cs[:, POOL_MAX_WIN:, sl] - cs[:, POOL_MAX_WIN - w:POOL_MAX_WIN - w + S, sl]
        count = jnp.minimum(pos + 1.0, float(w))[None, :, None]
        outs.append(win_sum / count - uf[..., sl])
    p = jnp.stack(outs, axis=2).astype(u.dtype)
    p = jnp.einsum('bsgc,gcd->bsgd', p, w_grp).reshape(B, S, POOL_WIDTH)
    return p * scale


def gla_chunk_causal(q, k, v, log_a):
    B, S, H, K = q.shape
    V = v.shape[-1]
    N = S // CHUNK
    f32 = jnp.float32
    qf = (q.astype(f32) * (K ** -0.5)).reshape(B, N, CHUNK, H, K)
    kf = k.astype(f32).reshape(B, N, CHUNK, H, K)
    vf = v.astype(f32).reshape(B, N, CHUNK, H, V)
    bc = jnp.cumsum(log_a.astype(f32).reshape(B, N, CHUNK, H, K), axis=2)
    e_pos, e_neg = jnp.exp(bc), jnp.exp(-bc)
    q_fw, k_fw = qf * e_pos, kf * e_neg
    q_bw, k_bw = qf * e_neg, kf * e_pos
    s_fw = jnp.einsum('bnthk,bnshk->bnhts', q_fw, k_fw)
    s_bw = jnp.einsum('bnthk,bnshk->bnhts', q_bw, k_bw)
    lower = jnp.tril(jnp.ones((CHUNK, CHUNK), dtype=bool))
    scores = jnp.where(lower, s_fw, s_bw)
    intra = jnp.einsum('bnhts,bnshv->bnthv', scores, vf)
    b_last = bc[:, :, -1]
    k_dec = kf * jnp.exp(b_last[:, :, None] - bc)
    kv = jnp.einsum('bnshk,bnshv->bnhkv', k_dec, vf)

    def step(state, inp):
        decay, kv_c = inp
        return decay[..., None] * state + kv_c, state

    xs = (jnp.moveaxis(jnp.exp(b_last), 1, 0), jnp.moveaxis(kv, 1, 0))
    _, s_prev = lax.scan(step, jnp.zeros((B, H, K, V), f32), xs)
    inter = jnp.einsum('bnthk,nbhkv->bnthv', q_fw, s_prev)
    return (intra + inter).reshape(B, S, H, V)


def conv_ffn(h, w_up, w_conv, b_conv, w_down):
    S = h.shape[1]
    u = h @ w_up
    up = jnp.pad(u, ((0, 0), (CONV_W - 1, 0), (0, 0)))
    c = b_conv + sum(up[:, j:j + S] * w_conv[j] for j in range(CONV_W))
    gate, val = jnp.split(c, 2, axis=-1)
    return (jax.nn.silu(gate) * val) @ w_down


def _fwd_setup_inputs(seed: int = 0) -> dict:
    key = jax.random.key(seed)
    ks = jax.random.split(key, 20)
    f32 = jnp.float32
    L = DEPTH

    def nrm(k, shape, scale):
        return jax.random.normal(k, shape, f32) * scale

    return {
        "x": nrm(ks[0], (BATCH, SEQ, D_MODEL), 1.0),
        "g_mix": 1.0 + nrm(ks[1], (L, D_MODEL), 0.02),
        "w_in": nrm(ks[2], (L, D_MODEL, IN_TOTAL), D_MODEL ** -0.5),
        "b_gate": nrm(ks[3], (L, N_BRANCH * D_MODEL), 0.01),
        "w_gk_up": nrm(ks[4], (L, GLA_GATE_RANK, GLA_DK), GLA_GATE_RANK ** -0.5),
        "b_gk": nrm(ks[5], (L, GLA_DK), 0.1),
        "w_pool_grp": nrm(ks[6], (L, POOL_GROUPS, POOL_GROUP_DIM, POOL_GROUP_DIM), POOL_GROUP_DIM ** -0.5),
        "pool_scale": 1.0 + nrm(ks[7], (L, POOL_WIDTH), 0.02),
        "g_gla_head": 1.0 + nrm(ks[8], (L, GLA_HV), 0.02),
        "w_pool_proj": nrm(ks[9], (L, POOL_WIDTH, D_MODEL), POOL_WIDTH ** -0.5),
        "w_gla_proj": nrm(ks[10], (L, GLA_DV, D_MODEL), GLA_DV ** -0.5),
        "w_out": nrm(ks[11], (L, D_MODEL, D_MODEL), D_MODEL ** -0.5),
        "g_ffn": 1.0 + nrm(ks[12], (L, D_MODEL), 0.02),
        "w_up": nrm(ks[13], (L, D_MODEL, 2 * D_FF), D_MODEL ** -0.5),
        "w_conv": nrm(ks[14], (L, CONV_W, 2 * D_FF), CONV_W ** -0.5),
        "b_conv": nrm(ks[15], (L, 2 * D_FF), 0.01),
        "w_down": nrm(ks[16], (L, D_FF, D_MODEL), D_FF ** -0.5),
        "g_final": 1.0 + nrm(ks[17], (D_MODEL,), 0.02),
    }


def _fwd_reference(x, g_mix, w_in, b_gate, w_gk_up, b_gk, w_pool_grp, pool_scale, g_gla_head,
              w_pool_proj, w_gla_proj, w_out, g_ffn, w_up, w_conv, b_conv, w_down, g_final):
    B, S, _ = x.shape
    for l in range(DEPTH):
        h = rmsnorm(x, g_mix[l])
        z = h @ w_in[l]
        z_pool, z_q, z_k, z_v, z_og, z_gk, z_gate = jnp.split(z, IN_OFFSETS, axis=-1)
        y_pool = pool_mixer(z_pool, w_pool_grp[l], pool_scale[l]) @ w_pool_proj[l]
        log_a = jax.nn.log_sigmoid((z_gk @ w_gk_up[l] + b_gk[l]).astype(jnp.float32)) / GLA_GATE_NORM
        o = gla_chunk_causal(
            z_q.reshape(B, S, GLA_HEADS, GLA_HK),
            z_k.reshape(B, S, GLA_HEADS, GLA_HK),
            z_v.reshape(B, S, GLA_HEADS, GLA_HV),
            log_a.reshape(B, S, GLA_HEADS, GLA_HK))
        o = rmsnorm(o, g_gla_head[l]).reshape(B, S, GLA_DV).astype(x.dtype) * jax.nn.silu(z_og)
        y_gla = o @ w_gla_proj[l]
        gates = jax.nn.sigmoid(z_gate + b_gate[l]).reshape(B, S, N_BRANCH, D_MODEL)
        mixed = gates[:, :, 0] * y_pool + gates[:, :, 1] * y_gla
        x = x + mixed @ w_out[l]
        h2 = rmsnorm(x, g_ffn[l])
        x = x + conv_ffn(h2, w_up[l], w_conv[l], b_conv[l], w_down[l])
    return rmsnorm(x, g_final)


import jax as _jax
import jax.numpy as _jnp

TWIN_FORMAT = 'train_step'
FWD_PARAMS = ['x', 'g_mix', 'w_in', 'b_gate', 'w_gk_up', 'b_gk', 'w_pool_grp', 'pool_scale', 'g_gla_head', 'w_pool_proj', 'w_gla_proj', 'w_out', 'g_ffn', 'w_up', 'w_conv', 'b_conv', 'w_down', 'g_final']
TWIN_WEIGHTS = ['g_mix', 'w_in', 'b_gate', 'w_gk_up', 'b_gk', 'w_pool_grp', 'pool_scale', 'g_gla_head', 'w_pool_proj', 'w_gla_proj', 'w_out', 'g_ffn', 'w_up', 'w_conv', 'b_conv', 'w_down', 'g_final']
TWIN_DIFF_INPUT = 'x'
TWIN_INPUTS = ['x', 'g_mix', 'w_in', 'b_gate', 'w_gk_up', 'b_gk', 'w_pool_grp', 'pool_scale', 'g_gla_head', 'w_pool_proj', 'w_gla_proj', 'w_out', 'g_ffn', 'w_up', 'w_conv', 'b_conv', 'w_down', 'g_final', 'loss_target', 'm_g_mix', 'm_w_in', 'm_b_gate', 'm_w_gk_up', 'm_b_gk', 'm_w_pool_grp', 'm_pool_scale', 'm_g_gla_head', 'm_w_pool_proj', 'm_w_gla_proj', 'm_w_out', 'm_g_ffn', 'm_w_up', 'm_w_conv', 'm_b_conv', 'm_w_down', 'm_g_final', 'v_g_mix', 'v_w_in', 'v_b_gate', 'v_w_gk_up', 'v_b_gk', 'v_w_pool_grp', 'v_pool_scale', 'v_g_gla_head', 'v_w_pool_proj', 'v_w_gla_proj', 'v_w_out', 'v_g_ffn', 'v_w_up', 'v_w_conv', 'v_b_conv', 'v_w_down', 'v_g_final']
TWIN_OUTPUTS = ['loss', 'grad_x', 'grad_g_mix', 'grad_w_in', 'grad_b_gate', 'grad_w_gk_up', 'grad_b_gk', 'grad_w_pool_grp', 'grad_pool_scale', 'grad_g_gla_head', 'grad_w_pool_proj', 'grad_w_gla_proj', 'grad_w_out', 'grad_g_ffn', 'grad_w_up', 'grad_w_conv', 'grad_b_conv', 'grad_w_down', 'grad_g_final', 'delta_g_mix', 'delta_w_in', 'delta_b_gate', 'delta_w_gk_up', 'delta_b_gk', 'delta_w_pool_grp', 'delta_pool_scale', 'delta_g_gla_head', 'delta_w_pool_proj', 'delta_w_gla_proj', 'delta_w_out', 'delta_g_ffn', 'delta_w_up', 'delta_w_conv', 'delta_b_conv', 'delta_w_down', 'delta_g_final', 'new_m_g_mix', 'new_m_w_in', 'new_m_b_gate', 'new_m_w_gk_up', 'new_m_b_gk', 'new_m_w_pool_grp', 'new_m_pool_scale', 'new_m_g_gla_head', 'new_m_w_pool_proj', 'new_m_w_gla_proj', 'new_m_w_out', 'new_m_g_ffn', 'new_m_w_up', 'new_m_w_conv', 'new_m_b_conv', 'new_m_w_down', 'new_m_g_final', 'new_v_g_mix', 'new_v_w_in', 'new_v_b_gate', 'new_v_w_gk_up', 'new_v_b_gk', 'new_v_w_pool_grp', 'new_v_pool_scale', 'new_v_g_gla_head', 'new_v_w_pool_proj', 'new_v_w_gla_proj', 'new_v_w_out', 'new_v_g_ffn', 'new_v_w_up', 'new_v_w_conv', 'new_v_b_conv', 'new_v_w_down', 'new_v_g_final']
TWIN_LEAF_KINDS = {'loss': 'loss', 'grad_x': 'grad_x', 'grad_g_mix': 'grad_w', 'grad_w_in': 'grad_w', 'grad_b_gate': 'grad_w', 'grad_w_gk_up': 'grad_w', 'grad_b_gk': 'grad_w', 'grad_w_pool_grp': 'grad_w', 'grad_pool_scale': 'grad_w', 'grad_g_gla_head': 'grad_w', 'grad_w_pool_proj': 'grad_w', 'grad_w_gla_proj': 'grad_w', 'grad_w_out': 'grad_w', 'grad_g_ffn': 'grad_w', 'grad_w_up': 'grad_w', 'grad_w_conv': 'grad_w', 'grad_b_conv': 'grad_w', 'grad_w_down': 'grad_w', 'grad_g_final': 'grad_w', 'delta_g_mix': 'delta_w', 'delta_w_in': 'delta_w', 'delta_b_gate': 'delta_w', 'delta_w_gk_up': 'delta_w', 'delta_b_gk': 'delta_w', 'delta_w_pool_grp': 'delta_w', 'delta_pool_scale': 'delta_w', 'delta_g_gla_head': 'delta_w', 'delta_w_pool_proj': 'delta_w', 'delta_w_gla_proj': 'delta_w', 'delta_w_out': 'delta_w', 'delta_g_ffn': 'delta_w', 'delta_w_up': 'delta_w', 'delta_w_conv': 'delta_w', 'delta_b_conv': 'delta_w', 'delta_w_down': 'delta_w', 'delta_g_final': 'delta_w', 'new_m_g_mix': 'new_m', 'new_m_w_in': 'new_m', 'new_m_b_gate': 'new_m', 'new_m_w_gk_up': 'new_m', 'new_m_b_gk': 'new_m', 'new_m_w_pool_grp': 'new_m', 'new_m_pool_scale': 'new_m', 'new_m_g_gla_head': 'new_m', 'new_m_w_pool_proj': 'new_m', 'new_m_w_gla_proj': 'new_m', 'new_m_w_out': 'new_m', 'new_m_g_ffn': 'new_m', 'new_m_w_up': 'new_m', 'new_m_w_conv': 'new_m', 'new_m_b_conv': 'new_m', 'new_m_w_down': 'new_m', 'new_m_g_final': 'new_m', 'new_v_g_mix': 'new_v', 'new_v_w_in': 'new_v', 'new_v_b_gate': 'new_v', 'new_v_w_gk_up': 'new_v', 'new_v_b_gk': 'new_v', 'new_v_w_pool_grp': 'new_v', 'new_v_pool_scale': 'new_v', 'new_v_g_gla_head': 'new_v', 'new_v_w_pool_proj': 'new_v', 'new_v_w_gla_proj': 'new_v', 'new_v_w_out': 'new_v', 'new_v_g_ffn': 'new_v', 'new_v_w_up': 'new_v', 'new_v_w_conv': 'new_v', 'new_v_b_conv': 'new_v', 'new_v_w_down': 'new_v', 'new_v_g_final': 'new_v'}


def _forward(args):
    return _fwd_reference(*[args[k] for k in FWD_PARAMS])


def _output_shape():
    out = _jax.eval_shape(lambda: _forward(_fwd_setup_inputs(0)))
    return out.shape, out.dtype

N_MICROBATCH = 1
ADAM_LR = 0.001
ADAM_B1 = 0.9
ADAM_B2 = 0.999
ADAM_EPS = 1e-08
ADAM_WD = 0.01
ADAM_STEP = 10
PER_EXAMPLE_BATCH_AXIS = {'x': 0, 'loss_target': 0}
SHARED_INPUTS = []
_WEIGHT_DTYPES = {'g_mix': _jnp.float32, 'w_in': _jnp.float32, 'b_gate': _jnp.float32, 'w_gk_up': _jnp.float32, 'b_gk': _jnp.float32, 'w_pool_grp': _jnp.float32, 'pool_scale': _jnp.float32, 'g_gla_head': _jnp.float32, 'w_pool_proj': _jnp.float32, 'w_gla_proj': _jnp.float32, 'w_out': _jnp.float32, 'g_ffn': _jnp.float32, 'w_up': _jnp.float32, 'w_conv': _jnp.float32, 'b_conv': _jnp.float32, 'w_down': _jnp.float32, 'g_final': _jnp.float32}
MOMENT_SCALE = {'g_mix': 1.011553e-01, 'w_in': 4.329200e-02, 'b_gate': 1.935881e-02, 'w_gk_up': 5.920371e-03, 'b_gk': 2.302789e-02, 'w_pool_grp': 8.122815e-02, 'pool_scale': 8.054112e-02, 'g_gla_head': 7.977454e-02, 'w_pool_proj': 5.685033e-02, 'w_gla_proj': 3.891040e-02, 'w_out': 6.921361e-02, 'g_ffn': 8.719730e-02, 'w_up': 3.567811e-02, 'w_conv': 3.677439e-02, 'b_conv': 3.652119e-02, 'w_down': 5.829762e-02, 'g_final': 1.601219e+01}


def _to_microbatches(a, axis):
    t = _jnp.moveaxis(a, axis, 0)
    t = t.reshape((N_MICROBATCH, t.shape[0] // N_MICROBATCH) + t.shape[1:])
    return _jnp.moveaxis(t, 1, axis + 1)


def setup_inputs(seed: int = 0) -> dict:
    inp = _fwd_setup_inputs(seed)
    key = _jax.random.fold_in(_jax.random.key(seed), 7919)
    shape, _ = _output_shape()
    out = dict(inp)
    out["loss_target"] = _jax.random.normal(_jax.random.fold_in(key, 0), shape, _jnp.float32)
    for i, name in enumerate(TWIN_WEIGHTS):
        w = inp[name].astype(_jnp.float32)
        if MOMENT_SCALE is None:
            s = _jnp.sqrt(_jnp.mean(_jnp.square(w)) + 1e-30)
        else:
            s = MOMENT_SCALE[name]
        km, kv = _jax.random.split(_jax.random.fold_in(key, i + 1))
        out[name] = w
        out["m_" + name] = s * _jax.random.normal(km, w.shape, _jnp.float32)
        out["v_" + name] = (s * s) * _jax.random.uniform(kv, w.shape, _jnp.float32, 0.5, 1.5)
    if N_MICROBATCH > 1:
        for name, axis in PER_EXAMPLE_BATCH_AXIS.items():
            out[name] = _to_microbatches(out[name], axis)
    return {'x': out['x'], 'g_mix': out['g_mix'], 'w_in': out['w_in'], 'b_gate': out['b_gate'], 'w_gk_up': out['w_gk_up'], 'b_gk': out['b_gk'], 'w_pool_grp': out['w_pool_grp'], 'pool_scale': out['pool_scale'], 'g_gla_head': out['g_gla_head'], 'w_pool_proj': out['w_pool_proj'], 'w_gla_proj': out['w_gla_proj'], 'w_out': out['w_out'], 'g_ffn': out['g_ffn'], 'w_up': out['w_up'], 'w_conv': out['w_conv'], 'b_conv': out['b_conv'], 'w_down': out['w_down'], 'g_final': out['g_final'], 'loss_target': out['loss_target'], 'm_g_mix': out['m_g_mix'], 'm_w_in': out['m_w_in'], 'm_b_gate': out['m_b_gate'], 'm_w_gk_up': out['m_w_gk_up'], 'm_b_gk': out['m_b_gk'], 'm_w_pool_grp': out['m_w_pool_grp'], 'm_pool_scale': out['m_pool_scale'], 'm_g_gla_head': out['m_g_gla_head'], 'm_w_pool_proj': out['m_w_pool_proj'], 'm_w_gla_proj': out['m_w_gla_proj'], 'm_w_out': out['m_w_out'], 'm_g_ffn': out['m_g_ffn'], 'm_w_up': out['m_w_up'], 'm_w_conv': out['m_w_conv'], 'm_b_conv': out['m_b_conv'], 'm_w_down': out['m_w_down'], 'm_g_final': out['m_g_final'], 'v_g_mix': out['v_g_mix'], 'v_w_in': out['v_w_in'], 'v_b_gate': out['v_b_gate'], 'v_w_gk_up': out['v_w_gk_up'], 'v_b_gk': out['v_b_gk'], 'v_w_pool_grp': out['v_w_pool_grp'], 'v_pool_scale': out['v_pool_scale'], 'v_g_gla_head': out['v_g_gla_head'], 'v_w_pool_proj': out['v_w_pool_proj'], 'v_w_gla_proj': out['v_w_gla_proj'], 'v_w_out': out['v_w_out'], 'v_g_ffn': out['v_g_ffn'], 'v_w_up': out['v_w_up'], 'v_w_conv': out['v_w_conv'], 'v_b_conv': out['v_b_conv'], 'v_w_down': out['v_w_down'], 'v_g_final': out['v_g_final']}


def _loss(weights, diff, rest, loss_target):
    with _jax.named_scope("forward"):
        args = {**rest, TWIN_DIFF_INPUT: diff, **{k: w.astype(_WEIGHT_DTYPES[k]) for k, w in weights.items()}}
        y = _forward(args)
    with _jax.named_scope("loss_head"):
        err = _jnp.square(y.astype(_jnp.float32) - loss_target)
        return 0.5 * _jnp.sum(_jnp.mean(err, axis=-1)) if err.ndim else 0.5 * err


def _adamw(w, g, m, v):
    m = ADAM_B1 * m + (1.0 - ADAM_B1) * g
    v = ADAM_B2 * v + (1.0 - ADAM_B2) * _jnp.square(g)
    m_hat = m / (1.0 - ADAM_B1 ** ADAM_STEP)
    v_hat = v / (1.0 - ADAM_B2 ** ADAM_STEP)
    delta = -ADAM_LR * (m_hat / (_jnp.sqrt(v_hat) + ADAM_EPS) + ADAM_WD * w)
    return delta, m, v


def reference(x, g_mix, w_in, b_gate, w_gk_up, b_gk, w_pool_grp, pool_scale, g_gla_head, w_pool_proj, w_gla_proj, w_out, g_ffn, w_up, w_conv, b_conv, w_down, g_final, loss_target, m_g_mix, m_w_in, m_b_gate, m_w_gk_up, m_b_gk, m_w_pool_grp, m_pool_scale, m_g_gla_head, m_w_pool_proj, m_w_gla_proj, m_w_out, m_g_ffn, m_w_up, m_w_conv, m_b_conv, m_w_down, m_g_final, v_g_mix, v_w_in, v_b_gate, v_w_gk_up, v_b_gk, v_w_pool_grp, v_pool_scale, v_g_gla_head, v_w_pool_proj, v_w_gla_proj, v_w_out, v_g_ffn, v_w_up, v_w_conv, v_b_conv, v_w_down, v_g_final):
    given = dict(x=x, g_mix=g_mix, w_in=w_in, b_gate=b_gate, w_gk_up=w_gk_up, b_gk=b_gk, w_pool_grp=w_pool_grp, pool_scale=pool_scale, g_gla_head=g_gla_head, w_pool_proj=w_pool_proj, w_gla_proj=w_gla_proj, w_out=w_out, g_ffn=g_ffn, w_up=w_up, w_conv=w_conv, b_conv=b_conv, w_down=w_down, g_final=g_final, loss_target=loss_target, m_g_mix=m_g_mix, m_w_in=m_w_in, m_b_gate=m_b_gate, m_w_gk_up=m_w_gk_up, m_b_gk=m_b_gk, m_w_pool_grp=m_w_pool_grp, m_pool_scale=m_pool_scale, m_g_gla_head=m_g_gla_head, m_w_pool_proj=m_w_pool_proj, m_w_gla_proj=m_w_gla_proj, m_w_out=m_w_out, m_g_ffn=m_g_ffn, m_w_up=m_w_up, m_w_conv=m_w_conv, m_b_conv=m_b_conv, m_w_down=m_w_down, m_g_final=m_g_final, v_g_mix=v_g_mix, v_w_in=v_w_in, v_b_gate=v_b_gate, v_w_gk_up=v_w_gk_up, v_b_gk=v_b_gk, v_w_pool_grp=v_w_pool_grp, v_pool_scale=v_pool_scale, v_g_gla_head=v_g_gla_head, v_w_pool_proj=v_w_pool_proj, v_w_gla_proj=v_w_gla_proj, v_w_out=v_w_out, v_g_ffn=v_g_ffn, v_w_up=v_w_up, v_w_conv=v_w_conv, v_b_conv=v_b_conv, v_w_down=v_w_down, v_g_final=v_g_final)
    weights = {n: given[n] for n in TWIN_WEIGHTS}
    shared = {n: given[n] for n in SHARED_INPUTS}
    per_example = {n: given[n] for n in ['x']}
    grad_fn = _jax.value_and_grad(_loss, argnums=(0, 1))

    def one_microbatch(ex, loss_target):
        ex = dict(ex)
        diff = ex.pop(TWIN_DIFF_INPUT)
        return grad_fn(weights, diff, {**shared, **ex}, loss_target)

    if N_MICROBATCH == 1:
        loss, (grad_w, grad_x) = one_microbatch(per_example, given["loss_target"])
    else:
        def body(carry, xs):
            loss_sum, grad_sum = carry
            l_k, (gw_k, gx_k) = one_microbatch(xs[0], xs[1])
            with _jax.named_scope("update"):
                return (loss_sum + l_k, _jax.tree.map(_jnp.add, grad_sum, gw_k)), gx_k

        init = (_jnp.zeros((), _jnp.float32), _jax.tree.map(_jnp.zeros_like, weights))
        (loss, grad_w), grad_x = _jax.lax.scan(body, init, (per_example, given["loss_target"]))
    with _jax.named_scope("update"):
        delta_w, new_m, new_v = {}, {}, {}
        for n in TWIN_WEIGHTS:
            delta_w[n], new_m[n], new_v[n] = _adamw(weights[n], grad_w[n], given["m_" + n], given["v_" + n])
    return (loss, grad_x, *[grad_w[n] for n in TWIN_WEIGHTS], *[delta_w[n] for n in TWIN_WEIGHTS],
            *[new_m[n] for n in TWIN_WEIGHTS], *[new_v[n] for n in TWIN_WEIGHTS])
```

```python
import functools

import jax
import jax.numpy as jnp
from jax import lax
from jax.experimental import pallas as pl
from jax.experimental.pallas import tpu as pltpu

F32 = jnp.float32
BF = jnp.bfloat16
HIGHEST = lax.Precision.HIGHEST
MESH = pl.DeviceIdType.MESH

N_DEV = 8
SEQ = 2048
D_MODEL = 1024
CHUNK = 64
EPS = 1e-6
POOL_WIDTH = 512
POOL_WINDOWS = (2, 4, 8, 16)
POOL_GD = 128
POOL_HALO = 16
HEADS = 4
HK = 128
HV = 256
GLA_DK = 512
GATE_RANK = 16
GATE_NORM = 16.0
D_FF = 2816
FF_BLK = 704
IN_TOTAL = 5648
IN_SHARD = 706
C_QKV, C_GATE, C_OG, C_POOL, C_GK = 0, 2048, 4096, 5120, 5632
N_CAT = 5760
GK_PAD = 128

ADAM_LR, ADAM_B1, ADAM_B2, ADAM_EPS, ADAM_WD, ADAM_STEP = 0.001, 0.9, 0.999, 1e-08, 0.01, 10
ADAM_C1 = 1.0 - ADAM_B1 ** ADAM_STEP
ADAM_C2 = 1.0 - ADAM_B2 ** ADAM_STEP

VMEM_BYTES_V7X = 64 * 1024 * 1024
VMEM_LIMIT = 48 * 1024 * 1024

TOK_TILE = 256
HALO = 8
GLA_CPS = 4


def _params(*sem):
    return pltpu.CompilerParams(dimension_semantics=sem, vmem_limit_bytes=VMEM_LIMIT)


def _const_spec(shape):
    nd = len(shape)
    return pl.BlockSpec(shape, lambda *_: (0,) * nd)


def _dot(a, b, ta=False, tb=False):
    dims = (((0 if ta else 1,), (1 if tb else 0,)), ((), ()))
    return lax.dot_general(a.astype(BF), b.astype(BF), dims, preferred_element_type=F32)


def _dot_exact(a, b):
    return jnp.dot(a, b, precision=HIGHEST, preferred_element_type=F32)


def _sigmoid(x):
    return 1.0 / (1.0 + jnp.exp(-x))


def _mm(a, b, *, out_shape, out_dtype, grid, blk_a, blk_b, blk_o, map_a, map_b, map_o, ta=False, tb=False,
        res=None, name):
    gk = grid[2]

    def body(*refs):
        if res is None:
            a_ref, b_ref, o_ref = refs[:3]
            r_ref = None
            scr = refs[3:]
        else:
            a_ref, b_ref, r_ref, o_ref = refs[:4]
            scr = refs[4:]
        prod = _dot(a_ref[...], b_ref[...], ta, tb)

        def finish(total):
            if r_ref is not None:
                total = total + r_ref[...]
            o_ref[...] = total.astype(out_dtype)

        if gk == 1:
            finish(prod)
        else:
            acc = scr[0]
            k = pl.program_id(2)

            @pl.when(k == 0)
            def _():
                acc[...] = prod

            @pl.when(k > 0)
            def _():
                acc[...] += prod

            @pl.when(k == gk - 1)
            def _():
                finish(acc[...])

    in_specs = [pl.BlockSpec(blk_a, map_a), pl.BlockSpec(blk_b, map_b)]
    args = [a, b]
    if res is not None:
        in_specs.append(pl.BlockSpec(blk_o, map_o))
        args.append(res)
    return pl.pallas_call(
        body, name=name, grid=grid, in_specs=in_specs, out_specs=pl.BlockSpec(blk_o, map_o),
        out_shape=jax.ShapeDtypeStruct(out_shape, out_dtype),
        scratch_shapes=[] if gk == 1 else [pltpu.VMEM(blk_o, F32)],
        compiler_params=_params("parallel", "parallel", "arbitrary"),
    )(*args)


def _rms_fwd(x, g, name):
    def body(x_ref, g_ref, o_ref):
        xv = x_ref[...]
        r = lax.rsqrt(jnp.mean(xv * xv, axis=-1, keepdims=True) + EPS)
        o_ref[...] = (xv * r * g_ref[...]).astype(BF)

    tile = pl.BlockSpec((TOK_TILE, D_MODEL), lambda i: (i, 0))
    return pl.pallas_call(
        body, name=name, grid=(SEQ // TOK_TILE,), in_specs=[tile, _const_spec((1, D_MODEL))], out_specs=tile,
        out_shape=jax.ShapeDtypeStruct((SEQ, D_MODEL), BF), compiler_params=_params("parallel"),
    )(x, g)


def _rms_bwd(dy, x, g, dres, name):
    def body(dy_ref, x_ref, g_ref, dres_ref, dx_ref, dg_ref):
        xv = x_ref[...]
        r = lax.rsqrt(jnp.mean(xv * xv, axis=-1, keepdims=True) + EPS)
        xn = xv * r
        dyv = dy_ref[...]
        dxn = dyv * g_ref[...]
        dx_ref[...] = dres_ref[...] + r * (dxn - xn * jnp.mean(dxn * xn, axis=-1, keepdims=True))
        part = jnp.sum(dyv * xn, axis=0, keepdims=True)

        @pl.when(pl.program_id(0) == 0)
        def _():
            dg_ref[...] = part

        @pl.when(pl.program_id(0) > 0)
        def _():
            dg_ref[...] += part

    tile = pl.BlockSpec((TOK_TILE, D_MODEL), lambda i: (i, 0))
    vec = _const_spec((1, D_MODEL))
    return pl.pallas_call(
        body, name=name, grid=(SEQ // TOK_TILE,), in_specs=[tile, tile, vec, tile], out_specs=[tile, vec],
        out_shape=[jax.ShapeDtypeStruct((SEQ, D_MODEL), F32), jax.ShapeDtypeStruct((1, D_MODEL), F32)],
        compiler_params=_params("arbitrary"),
    )(dy, x, g, dres)


def _final_loss(x2, g, target):
    def body(x_ref, g_ref, t_ref, loss_ref, dx_ref, dg_ref):
        xv = x_ref[...]
        r = lax.rsqrt(jnp.mean(xv * xv, axis=-1, keepdims=True) + EPS)
        xn = xv * r
        gv = g_ref[...]
        err = xn * gv - t_ref[...]
        lpart = jnp.full((1, 128), 0.5 * jnp.sum(jnp.mean(err * err, axis=-1, keepdims=True)), F32)
        dyv = err * (1.0 / D_MODEL)
        dxn = dyv * gv
        dx_ref[...] = r * (dxn - xn * jnp.mean(dxn * xn, axis=-1, keepdims=True))
        gpart = jnp.sum(dyv * xn, axis=0, keepdims=True)

        @pl.when(pl.program_id(0) == 0)
        def _():
            loss_ref[...] = lpart
            dg_ref[...] = gpart

        @pl.when(pl.program_id(0) > 0)
        def _():
            loss_ref[...] += lpart
            dg_ref[...] += gpart

    tile = pl.BlockSpec((TOK_TILE, D_MODEL), lambda i: (i, 0))
    vec = _const_spec((1, D_MODEL))
    return pl.pallas_call(
        body, name="final_loss", grid=(SEQ // TOK_TILE,), in_specs=[tile, vec, tile],
        out_specs=[_const_spec((1, 128)), tile, vec],
        out_shape=[jax.ShapeDtypeStruct((1, 128), F32), jax.ShapeDtypeStruct((SEQ, D_MODEL), F32),
                   jax.ShapeDtypeStruct((1, D_MODEL), F32)],
        compiler_params=_params("arbitrary"),
    )(x2, g, target)


def _pool_counts(w):
    pos = lax.broadcasted_iota(jnp.int32, (SEQ, 1), 0).astype(F32)
    return jnp.minimum(pos + 1.0, float(w))


def _pool_window(u, w, ext):
    ext[pl.ds(POOL_HALO, SEQ), :] = u
    win = u
    for j in range(1, w):
        win = win + ext[pl.ds(POOL_HALO - j, SEQ), :]
    return win / _pool_counts(w) - u


def _pool_fwd(zcat, w_grp, scale):
    def body(z_ref, w_ref, s_ref, o_ref, ext):
        ext[pl.ds(0, POOL_HALO), :] = jnp.zeros((POOL_HALO, POOL_GD), F32)
        for g, w in enumerate(POOL_WINDOWS):
            cols = slice(g * POOL_GD, (g + 1) * POOL_GD)
            p = _pool_window(z_ref[:, cols], w, ext)
            o_ref[:, cols] = (_dot(p, w_ref[g]) * s_ref[:, cols]).astype(BF)

    return pl.pallas_call(
        body, name="pool_fwd", grid=(1,),
        in_specs=[pl.BlockSpec((SEQ, POOL_WIDTH), lambda i: (0, C_POOL // POOL_WIDTH)),
                  _const_spec((4, POOL_GD, POOL_GD)), _const_spec((1, POOL_WIDTH))],
        out_specs=_const_spec((SEQ, POOL_WIDTH)), out_shape=jax.ShapeDtypeStruct((SEQ, POOL_WIDTH), BF),
        scratch_shapes=[pltpu.VMEM((POOL_HALO + SEQ, POOL_GD), F32)], compiler_params=_params("arbitrary"),
    )(zcat, w_grp, scale)


def _pool_bwd(dzcat, zcat, dps, w_grp, scale):
    def body(dz_in, z_ref, dps_ref, w_ref, s_ref, dz_ref, dw_ref, dsc_ref, ext, ext2):
        del dz_in
        ext[pl.ds(0, POOL_HALO), :] = jnp.zeros((POOL_HALO, POOL_GD), F32)
        ext2[pl.ds(SEQ, POOL_HALO), :] = jnp.zeros((POOL_HALO, POOL_GD), F32)
        for g, w in enumerate(POOL_WINDOWS):
            cols = slice(g * POOL_GD, (g + 1) * POOL_GD)
            p = _pool_window(z_ref[:, cols], w, ext)
            wg = w_ref[g]
            pg = _dot(p, wg)
            dpsv = dps_ref[:, cols]
            dsc_ref[:, cols] = jnp.sum(dpsv * pg, axis=0, keepdims=True)
            dpg = dpsv * s_ref[:, cols]
            dw_ref[g] = _dot(p, dpg, ta=True)
            dp = _dot(dpg, wg, tb=True)
            dpc = dp / _pool_counts(w)
            ext2[pl.ds(0, SEQ), :] = dpc
            du = dpc
            for j in range(1, w):
                du = du + ext2[pl.ds(j, SEQ), :]
            dz_ref[:, cols] = (du - dp).astype(BF)

    return pl.pallas_call(
        body, name="pool_bwd", grid=(1,),
        in_specs=[pl.BlockSpec(memory_space=pl.ANY),
                  pl.BlockSpec((SEQ, POOL_WIDTH), lambda i: (0, C_POOL // POOL_WIDTH)),
                  _const_spec((SEQ, POOL_WIDTH)), _const_spec((4, POOL_GD, POOL_GD)), _const_spec((1, POOL_WIDTH))],
        out_specs=[pl.BlockSpec((SEQ, POOL_WIDTH), lambda i: (0, C_POOL // POOL_WIDTH)),
                   _const_spec((4, POOL_GD, POOL_GD)), _const_spec((1, POOL_WIDTH))],
        out_shape=[jax.ShapeDtypeStruct((SEQ, N_CAT), BF), jax.ShapeDtypeStruct((4, POOL_GD, POOL_GD), F32),
                   jax.ShapeDtypeStruct((1, POOL_WIDTH), F32)],
        scratch_shapes=[pltpu.VMEM((POOL_HALO + SEQ, POOL_GD), F32), pltpu.VMEM((SEQ + POOL_HALO, POOL_GD), F32)],
        input_output_aliases={0: 0}, compiler_params=_params("arbitrary"),
    )(dzcat, zcat, dps, w_grp, scale)


GK_TILE = 512


def _gk_fwd(zcat, wgk_pad, b_gk):
    def body(z_ref, w_ref, b_ref, la_ref):
        pre = _dot(z_ref[...], w_ref[...]) + b_ref[...]
        la_ref[...] = (jnp.minimum(pre, 0.0) - jnp.log(1.0 + jnp.exp(-jnp.abs(pre)))) * (1.0 / GATE_NORM)

    return pl.pallas_call(
        body, name="gk_fwd", grid=(SEQ // GK_TILE,),
        in_specs=[pl.BlockSpec((GK_TILE, GK_PAD), lambda i: (i, C_GK // GK_PAD)), _const_spec((GK_PAD, GLA_DK)),
                  _const_spec((1, GLA_DK))],
        out_specs=pl.BlockSpec((GK_TILE, GLA_DK), lambda i: (i, 0)),
        out_shape=jax.ShapeDtypeStruct((SEQ, GLA_DK), F32), compiler_params=_params("parallel"),
    )(zcat, wgk_pad, b_gk)


def _gk_bwd(dzcat, zcat, dla, wgk_pad, b_gk):
    def body(dz_in, z_ref, dla_ref, w_ref, b_ref, dz_ref, dw_ref, db_ref):
        del dz_in
        zv = z_ref[...]
        wv = w_ref[...]
        pre = _dot(zv, wv) + b_ref[...]
        dpre = dla_ref[...] * (1.0 / GATE_NORM) * (1.0 - _sigmoid(pre))
        dz_ref[...] = _dot(dpre, wv, tb=True).astype(BF)
        dwp = _dot(zv, dpre, ta=True)
        dbp = jnp.sum(dpre, axis=0, keepdims=True)

        @pl.when(pl.program_id(0) == 0)
        def _():
            dw_ref[...] = dwp
            db_ref[...] = dbp

        @pl.when(pl.program_id(0) > 0)
        def _():
            dw_ref[...] += dwp
            db_ref[...] += dbp

    zspec = pl.BlockSpec((GK_TILE, GK_PAD), lambda i: (i, C_GK // GK_PAD))
    return pl.pallas_call(
        body, name="gk_bwd", grid=(SEQ // GK_TILE,),
        in_specs=[pl.BlockSpec(memory_space=pl.ANY), zspec, pl.BlockSpec((GK_TILE, GLA_DK), lambda i: (i, 0)),
                  _const_spec((GK_PAD, GLA_DK)), _const_spec((1, GLA_DK))],
        out_specs=[zspec, _const_spec((GK_PAD, GLA_DK)), _const_spec((1, GLA_DK))],
        out_shape=[jax.ShapeDtypeStruct((SEQ, N_CAT), BF), jax.ShapeDtypeStruct((GK_PAD, GLA_DK), F32),
                   jax.ShapeDtypeStruct((1, GLA_DK), F32)],
        input_output_aliases={0: 0}, compiler_params=_params("arbitrary"),
    )(dzcat, zcat, dla, wgk_pad, b_gk)


GLA_ROWS = GLA_CPS * CHUNK
GLA_STEPS = SEQ // GLA_ROWS
QKV_W = 2048


def _gla_chunk(qkv_ref, la_ref, rows, h):
    tri = lax.broadcasted_iota(jnp.int32, (CHUNK, CHUNK), 0) >= lax.broadcasted_iota(jnp.int32, (CHUNK, CHUNK), 1)
    q = qkv_ref[rows, h * HK:(h + 1) * HK] * (HK ** -0.5)
    k = qkv_ref[rows, GLA_DK + h * HK:GLA_DK + (h + 1) * HK]
    v = qkv_ref[rows, 2 * GLA_DK + h * HV:2 * GLA_DK + (h + 1) * HV]
    la = la_ref[rows, h * HK:(h + 1) * HK]
    bc = _dot_exact(tri.astype(F32), la)
    e_pos, e_neg = jnp.exp(bc), jnp.exp(-bc)
    dl = jnp.exp(jnp.sum(la, axis=0, keepdims=True))
    q_fw, q_bw, k_fw, k_bw = q * e_pos, q * e_neg, k * e_neg, k * e_pos
    scores = jnp.where(tri, _dot(q_fw, k_fw, tb=True), _dot(q_bw, k_bw, tb=True))
    return tri, v, e_pos, e_neg, dl, q_fw, q_bw, k_fw, k_bw, scores


def _gla_fwd(zcat, la):
    def body(qkv_ref, la_ref, o_ref, st_ref, state):
        @pl.when(pl.program_id(0) == 0)
        def _():
            state[...] = jnp.zeros_like(state)

        for c in range(GLA_CPS):
            rows = slice(c * CHUNK, (c + 1) * CHUNK)
            for h in range(HEADS):
                _, v, _, _, dl, q_fw, _, k_fw, _, scores = _gla_chunk(qkv_ref, la_ref, rows, h)
                st = state[h]
                st_ref[c, h] = st
                o_ref[rows, h * HV:(h + 1) * HV] = _dot(scores, v) + _dot(q_fw, st, tb=True)
                state[h] = st * dl + _dot(v, k_fw * dl, ta=True)

    return pl.pallas_call(
        body, name="gla_fwd", grid=(GLA_STEPS,),
        in_specs=[pl.BlockSpec((GLA_ROWS, QKV_W), lambda i: (i, 0)), pl.BlockSpec((GLA_ROWS, GLA_DK), lambda i: (i, 0))],
        out_specs=[pl.BlockSpec((GLA_ROWS, D_MODEL), lambda i: (i, 0)),
                   pl.BlockSpec((GLA_CPS, HEADS, HV, HK), lambda i: (i, 0, 0, 0))],
        out_shape=[jax.ShapeDtypeStruct((SEQ, D_MODEL), F32),
                   jax.ShapeDtypeStruct((SEQ // CHUNK, HEADS, HV, HK), F32)],
        scratch_shapes=[pltpu.VMEM((HEADS, HV, HK), F32)], compiler_params=_params("arbitrary"),
    )(zcat, la)


def _gla_bwd(dzcat, zcat, la, d_o, states):
    def body(dz_in, qkv_ref, la_ref, do_ref, st_ref, dqkv_ref, dla_ref, dstate):
        del dz_in

        @pl.when(pl.program_id(0) == 0)
        def _():
            dstate[...] = jnp.zeros_like(dstate)

        last_row = lax.broadcasted_iota(jnp.int32, (CHUNK, HK), 0) == CHUNK - 1
        upper = (lax.broadcasted_iota(jnp.int32, (CHUNK, CHUNK), 0)
                 <= lax.broadcasted_iota(jnp.int32, (CHUNK, CHUNK), 1)).astype(F32)
        for c in reversed(range(GLA_CPS)):
            rows = slice(c * CHUNK, (c + 1) * CHUNK)
            for h in range(HEADS):
                tri, v, e_pos, e_neg, dl, q_fw, q_bw, k_fw, k_bw, scores = _gla_chunk(qkv_ref, la_ref, rows, h)
                st = st_ref[c, h]
                dst = dstate[h]
                d_out = do_ref[rows, h * HV:(h + 1) * HV]
                k_dec = k_fw * dl
                dp = _dot(d_out, v, tb=True)
                dp_fw = jnp.where(tri, dp, 0.0)
                dp_bw = jnp.where(tri, 0.0, dp)
                dv = _dot(scores, d_out, ta=True) + _dot(k_dec, dst, tb=True)
                dk_dec = _dot(v, dst)
                dq_fw = _dot(dp_fw, k_fw) + _dot(d_out, st)
                dk_fw = _dot(dp_fw, q_fw, ta=True) + dk_dec * dl
                dq_bw = _dot(dp_bw, k_bw)
                dk_bw = _dot(dp_bw, q_bw, ta=True)
                ddl = jnp.sum(st * dst, axis=0, keepdims=True) + jnp.sum(k_fw * dk_dec, axis=0, keepdims=True)
                dstate[h] = dst * dl + _dot(d_out, q_fw, ta=True)
                dq = (dq_fw * e_pos + dq_bw * e_neg) * (HK ** -0.5)
                dk = dk_fw * e_neg + dk_bw * e_pos
                db = dq_fw * q_fw - dk_fw * k_fw - dq_bw * q_bw + dk_bw * k_bw + jnp.where(last_row, ddl * dl, 0.0)
                dla_ref[rows, h * HK:(h + 1) * HK] = _dot_exact(upper, db)
                dqkv_ref[rows, h * HK:(h + 1) * HK] = dq.astype(BF)
                dqkv_ref[rows, GLA_DK + h * HK:GLA_DK + (h + 1) * HK] = dk.astype(BF)
                dqkv_ref[rows, 2 * GLA_DK + h * HV:2 * GLA_DK + (h + 1) * HV] = dv.astype(BF)

    rev = lambda i: (GLA_STEPS - 1 - i, 0)
    return pl.pallas_call(
        body, name="gla_bwd", grid=(GLA_STEPS,),
        in_specs=[pl.BlockSpec(memory_space=pl.ANY), pl.BlockSpec((GLA_ROWS, QKV_W), rev),
                  pl.BlockSpec((GLA_ROWS, GLA_DK), rev), pl.BlockSpec((GLA_ROWS, D_MODEL), rev),
                  pl.BlockSpec((GLA_CPS, HEADS, HV, HK), lambda i: (GLA_STEPS - 1 - i, 0, 0, 0))],
        out_specs=[pl.BlockSpec((GLA_ROWS, QKV_W), rev), pl.BlockSpec((GLA_ROWS, GLA_DK), rev)],
        out_shape=[jax.ShapeDtypeStruct((SEQ, N_CAT), BF), jax.ShapeDtypeStruct((SEQ, GLA_DK), F32)],
        scratch_shapes=[pltpu.VMEM((HEADS, HV, HK), F32)], input_output_aliases={0: 0},
        compiler_params=_params("arbitrary"),
    )(dzcat, zcat, la, d_o, states)


def _silu_parts(x):
    s = _sigmoid(x)
    return x * s, s * (1.0 + x * (1.0 - s))


def _post_gla_fwd(o, zcat, g_head):
    def body(o_ref, zog_ref, g_ref, out_ref):
        for h in range(HEADS):
            cols = slice(h * HV, (h + 1) * HV)
            ov = o_ref[:, cols]
            r = lax.rsqrt(jnp.mean(ov * ov, axis=-1, keepdims=True) + EPS)
            act, _ = _silu_parts(zog_ref[:, cols])
            out_ref[:, cols] = (ov * r * g_ref[...] * act).astype(BF)

    tile = pl.BlockSpec((TOK_TILE, D_MODEL), lambda i: (i, 0))
    return pl.pallas_call(
        body, name="post_gla_fwd", grid=(SEQ // TOK_TILE,),
        in_specs=[tile, pl.BlockSpec((TOK_TILE, D_MODEL), lambda i: (i, C_OG // D_MODEL)), _const_spec((1, HV))],
        out_specs=tile, out_shape=jax.ShapeDtypeStruct((SEQ, D_MODEL), BF), compiler_params=_params("parallel"),
    )(o, zcat, g_head)


def _post_gla_bwd(dzcat, d_og, o, zcat, g_head):
    def body(dz_in, dog_ref, o_ref, zog_ref, g_ref, dz_ref, do_ref, dg_ref):
        del dz_in
        gpart = jnp.zeros((1, HV), F32)
        gv = g_ref[...]
        for h in range(HEADS):
            cols = slice(h * HV, (h + 1) * HV)
            ov = o_ref[:, cols]
            r = lax.rsqrt(jnp.mean(ov * ov, axis=-1, keepdims=True) + EPS)
            on = ov * r
            act, dact = _silu_parts(zog_ref[:, cols])
            dogv = dog_ref[:, cols]
            dz_ref[:, cols] = (dogv * on * gv * dact).astype(BF)
            d_on_g = dogv * act
            gpart = gpart + jnp.sum(d_on_g * on, axis=0, keepdims=True)
            dxn = d_on_g * gv
            do_ref[:, cols] = r * (dxn - on * jnp.mean(dxn * on, axis=-1, keepdims=True))

        @pl.when(pl.program_id(0) == 0)
        def _():
            dg_ref[...] = gpart

        @pl.when(pl.program_id(0) > 0)
        def _():
            dg_ref[...] += gpart

    tile = pl.BlockSpec((TOK_TILE, D_MODEL), lambda i: (i, 0))
    ogspec = pl.BlockSpec((TOK_TILE, D_MODEL), lambda i: (i, C_OG // D_MODEL))
    return pl.pallas_call(
        body, name="post_gla_bwd", grid=(SEQ // TOK_TILE,),
        in_specs=[pl.BlockSpec(memory_space=pl.ANY), tile, tile, ogspec, _const_spec((1, HV))],
        out_specs=[ogspec, tile, _const_spec((1, HV))],
        out_shape=[jax.ShapeDtypeStruct((SEQ, N_CAT), BF), jax.ShapeDtypeStruct((SEQ, D_MODEL), F32),
                   jax.ShapeDtypeStruct((1, HV), F32)],
        input_output_aliases={0: 0}, compiler_params=_params("arbitrary"),
    )(dzcat, d_og, o, zcat, g_head)


GATE_W = 2 * D_MODEL


def _mix_fwd(zcat, b_gate, y_pool, y_gla):
    def body(zg_ref, b_ref, yp_ref, yg_ref, out_ref):
        g0 = _sigmoid(zg_ref[:, :D_MODEL] + b_ref[:, :D_MODEL])
        g1 = _sigmoid(zg_ref[:, D_MODEL:] + b_ref[:, D_MODEL:])
        out_ref[...] = (g0 * yp_ref[...] + g1 * yg_ref[...]).astype(BF)

    tile = pl.BlockSpec((TOK_TILE, D_MODEL), lambda i: (i, 0))
    return pl.pallas_call(
        body, name="mix_fwd", grid=(SEQ // TOK_TILE,),
        in_specs=[pl.BlockSpec((TOK_TILE, GATE_W), lambda i: (i, C_GATE // GATE_W)), _const_spec((1, GATE_W)), tile, tile],
        out_specs=tile, out_shape=jax.ShapeDtypeStruct((SEQ, D_MODEL), BF), compiler_params=_params("parallel"),
    )(zcat, b_gate, y_pool, y_gla)


def _mix_bwd(dmixed, zcat, b_gate, y_pool, y_gla):
    def body(dm_ref, zg_ref, b_ref, yp_ref, yg_ref, dz_ref, dyp_ref, dyg_ref, db_ref):
        dm = dm_ref[...]
        g0 = _sigmoid(zg_ref[:, :D_MODEL] + b_ref[:, :D_MODEL])
        g1 = _sigmoid(zg_ref[:, D_MODEL:] + b_ref[:, D_MODEL:])
        dyp_ref[...] = (dm * g0).astype(BF)
        dyg_ref[...] = (dm * g1).astype(BF)
        dz0 = dm * yp_ref[...] * g0 * (1.0 - g0)
        dz1 = dm * yg_ref[...] * g1 * (1.0 - g1)
        dz_ref[:, :D_MODEL] = dz0.astype(BF)
        dz_ref[:, D_MODEL:] = dz1.astype(BF)
        b0 = jnp.sum(dz0, axis=0, keepdims=True)
        b1 = jnp.sum(dz1, axis=0, keepdims=True)

        @pl.when(pl.program_id(0) == 0)
        def _():
            db_ref[:, :D_MODEL] = b0
            db_ref[:, D_MODEL:] = b1

        @pl.when(pl.program_id(0) > 0)
        def _():
            db_ref[:, :D_MODEL] += b0
            db_ref[:, D_MODEL:] += b1

    tile = pl.BlockSpec((TOK_TILE, D_MODEL), lambda i: (i, 0))
    gspec = pl.BlockSpec((TOK_TILE, GATE_W), lambda i: (i, C_GATE // GATE_W))
    return pl.pallas_call(
        body, name="mix_bwd", grid=(SEQ // TOK_TILE,),
        in_specs=[tile, gspec, _const_spec((1, GATE_W)), tile, tile],
        out_specs=[gspec, tile, tile, _const_spec((1, GATE_W))],
        out_shape=[jax.ShapeDtypeStruct((SEQ, N_CAT), BF), jax.ShapeDtypeStruct((SEQ, D_MODEL), BF),
                   jax.ShapeDtypeStruct((SEQ, D_MODEL), BF), jax.ShapeDtypeStruct((1, GATE_W), F32)],
        compiler_params=_params("arbitrary"),
    )(dmixed, zcat, b_gate, y_pool, y_gla)


N_TOK_TILES = SEQ // TOK_TILE
HALO_PER_TILE = TOK_TILE // HALO


def _conv_rows(ext, n, half, wc_ref, bc_ref):
    return (bc_ref[half] + ext[pl.ds(HALO - 2, n), :] * wc_ref[half, 0:1, :]
            + ext[pl.ds(HALO - 1, n), :] * wc_ref[half, 1:2, :] + ext[pl.ds(HALO, n), :] * wc_ref[half, 2:3, :])


def _pair_specs(pairs):
    tile = pl.BlockSpec((pairs, None, TOK_TILE, FF_BLK), lambda b, i: (0, b, i, 0))
    before = pl.BlockSpec((pairs, None, HALO, FF_BLK), lambda b, i: (0, b, jnp.maximum(i * HALO_PER_TILE - 1, 0), 0))
    after = pl.BlockSpec((pairs, None, HALO, FF_BLK),
                         lambda b, i: (0, b, jnp.minimum((i + 1) * HALO_PER_TILE, SEQ // HALO - 1), 0))

    def vec(rows):
        return pl.BlockSpec((2, None, rows, FF_BLK), lambda b, i: (0, b, 0, 0))

    return tile, before, after, vec


def _fill_ext(ext, half, before_ref, tile_ref, after_ref=None):
    i = pl.program_id(1)
    ext[pl.ds(0, HALO), :] = jnp.where(i > 0, before_ref[half], 0.0)
    ext[pl.ds(HALO, TOK_TILE), :] = tile_ref[half]
    if after_ref is not None:
        ext[pl.ds(HALO + TOK_TILE, HALO), :] = after_ref[half]


def _conv_fwd(u, w_conv, b_conv):
    def body(u_ref, ub_ref, w_ref, b_ref, a_ref, ext_g, ext_v):
        _fill_ext(ext_g, 0, ub_ref, u_ref)
        _fill_ext(ext_v, 1, ub_ref, u_ref)
        cg = _conv_rows(ext_g, TOK_TILE, 0, w_ref, b_ref)
        cv = _conv_rows(ext_v, TOK_TILE, 1, w_ref, b_ref)
        a_ref[0] = (cg * _sigmoid(cg) * cv).astype(BF)

    tile, before, _, vec = _pair_specs(2)
    out_tile, _, _, _ = _pair_specs(1)
    return pl.pallas_call(
        body, name="conv_fwd", grid=(4, N_TOK_TILES), in_specs=[tile, before, vec(3), vec(1)],
        out_specs=out_tile, out_shape=jax.ShapeDtypeStruct((1, 4, SEQ, FF_BLK), BF),
        scratch_shapes=[pltpu.VMEM((HALO + TOK_TILE, FF_BLK), F32)] * 2, compiler_params=_params("parallel", "parallel"),
    )(u, u, w_conv, b_conv)


def _conv_bwd(u, da, w_conv, b_conv):
    n_ext = TOK_TILE + HALO

    def body(u_ref, ub_ref, ua_ref, da_ref, daa_ref, w_ref, b_ref, du_ref, dw_ref, db_ref, ext_g, ext_v, ext_d):
        i = pl.program_id(1)
        _fill_ext(ext_g, 0, ub_ref, u_ref, ua_ref)
        _fill_ext(ext_v, 1, ub_ref, u_ref, ua_ref)
        cg = _conv_rows(ext_g, n_ext, 0, w_ref, b_ref)
        cv = _conv_rows(ext_v, n_ext, 1, w_ref, b_ref)
        act, dact = _silu_parts(cg)
        da_after = jnp.where(i < N_TOK_TILES - 1, daa_ref[0].astype(F32), 0.0)
        dav = jnp.concatenate([da_ref[0].astype(F32), da_after], axis=0)
        for half, dc, ext_u in ((0, dav * cv * dact, ext_g), (1, dav * act, ext_v)):
            ext_d[pl.ds(0, n_ext), :] = dc
            du = (ext_d[pl.ds(0, TOK_TILE), :] * w_ref[half, 2:3, :] + ext_d[pl.ds(1, TOK_TILE), :] * w_ref[half, 1:2, :]
                  + ext_d[pl.ds(2, TOK_TILE), :] * w_ref[half, 0:1, :])
            du_ref[half] = du.astype(BF)
            dct = ext_d[pl.ds(0, TOK_TILE), :]
            dwp = [jnp.sum(dct * ext_u[pl.ds(HALO - 2 + j, TOK_TILE), :], axis=0, keepdims=True) for j in range(3)]
            dbp = jnp.sum(dct, axis=0, keepdims=True)

            @pl.when(i == 0)
            def _():
                for j in range(3):
                    dw_ref[half, j:j + 1, :] = dwp[j]
                db_ref[half] = dbp

            @pl.when(i > 0)
            def _():
                for j in range(3):
                    dw_ref[half, j:j + 1, :] += dwp[j]
                db_ref[half] += dbp

    tile, before, after, vec = _pair_specs(2)
    da_tile, _, da_after_spec, _ = _pair_specs(1)
    return pl.pallas_call(
        body, name="conv_bwd", grid=(4, N_TOK_TILES),
        in_specs=[tile, before, after, da_tile, da_after_spec, vec(3), vec(1)],
        out_specs=[tile, vec(3), vec(1)],
        out_shape=[jax.ShapeDtypeStruct((2, 4, SEQ, FF_BLK), BF), jax.ShapeDtypeStruct((2, 4, 3, FF_BLK), F32),
                   jax.ShapeDtypeStruct((2, 4, 1, FF_BLK), F32)],
        scratch_shapes=[pltpu.VMEM((2 * HALO + TOK_TILE, FF_BLK), F32)] * 3,
        compiler_params=_params("parallel", "arbitrary"),
    )(u, u, u, da, da, w_conv, b_conv)


ANY = pl.BlockSpec(memory_space=pl.ANY)


def _place():
    x, y, c = lax.axis_index("x"), lax.axis_index("y"), lax.axis_index("c")
    other_chips = [(1 - x, y), (x, 1 - y), (1 - x, 1 - y)]
    return x, y, c, other_chips


def _all_gather(shards, name):
    n = len(shards)

    def body(*refs):
        src, out = refs[:n], refs[n:2 * n]
        send_sems, recv_sems, local_sems = refs[2 * n:]
        x, y, c, chips = _place()
        me, sibling = (x, y, c), (x, y, 1 - c)

        def copy(a, k, block, to, own=False):
            dst = out[a].at[4 * block[0] + 2 * block[1] + block[2]]
            return pltpu.make_async_remote_copy(src_ref=src[a] if own else dst, dst_ref=dst, send_sem=send_sems.at[a, k],
                                                recv_sem=recv_sems.at[a, k], device_id=to, device_id_type=MESH)

        mine = [pltpu.make_async_copy(src[a], out[a].at[4 * x + 2 * y + c], local_sems.at[a]) for a in range(n)]
        first = []
        for a in range(n):
            mine[a].start()
            first.append(copy(a, 0, me, sibling, own=True))
            first += [copy(a, 1 + j, me, (*chip, c), own=True) for j, chip in enumerate(chips)]
        for cp in first:
            cp.start()
        passed = []
        for j, chip in enumerate(chips):
            for a in range(n):
                copy(a, 1 + j, (*chip, c), me).wait_recv()
                passed.append(copy(a, 4 + j, (*chip, c), sibling))
                passed[-1].start()
        for a in range(n):
            copy(a, 0, sibling, me).wait_recv()
            for j, chip in enumerate(chips):
                copy(a, 4 + j, (*chip, 1 - c), me).wait_recv()
        for cp in first + passed:
            cp.wait_send()
        for cp in mine:
            cp.wait()

    return pl.pallas_call(
        body, name=name, in_specs=[ANY] * n, out_specs=[ANY] * n,
        out_shape=[jax.ShapeDtypeStruct((N_DEV,) + s.shape, s.dtype) for s in shards],
        scratch_shapes=[pltpu.SemaphoreType.DMA((n, 7)), pltpu.SemaphoreType.DMA((n, 7)), pltpu.SemaphoreType.DMA((n,))],
    )(*shards)


def _exchange_sibling(parts, name):
    n = len(parts)

    def body(*refs):
        src, out = refs[:n], refs[n:2 * n]
        send_sems, recv_sems = refs[2 * n:]
        x, y, c, _ = _place()
        copies = [pltpu.make_async_remote_copy(src_ref=src[a].at[2 * j + 1 - c], dst_ref=out[a].at[j],
                                               send_sem=send_sems.at[a, j], recv_sem=recv_sems.at[a, j],
                                               device_id=(x, y, 1 - c), device_id_type=MESH)
                  for a in range(n) for j in range(4)]
        for cp in copies:
            cp.start()
        for cp in copies:
            cp.wait()

    return pl.pallas_call(
        body, name=name, in_specs=[ANY] * n, out_specs=[ANY] * n,
        out_shape=[jax.ShapeDtypeStruct((4,) + p.shape[1:], p.dtype) for p in parts],
        scratch_shapes=[pltpu.SemaphoreType.DMA((n, 4)), pltpu.SemaphoreType.DMA((n, 4))],
    )(*parts)


def _exchange_chips(sums, name):
    n = len(sums)

    def body(*refs):
        src, out = refs[:n], refs[n:2 * n]
        send_sems, recv_sems = refs[2 * n:]
        _, _, c, chips = _place()
        copies = [pltpu.make_async_remote_copy(src_ref=src[a].at[2 * px + py], dst_ref=out[a].at[k],
                                               send_sem=send_sems.at[a, k], recv_sem=recv_sems.at[a, k],
                                               device_id=(px, py, c), device_id_type=MESH)
                  for a in range(n) for k, (px, py) in enumerate(chips)]
        for cp in copies:
            cp.start()
        for cp in copies:
            cp.wait()

    return pl.pallas_call(
        body, name=name, in_specs=[ANY] * n, out_specs=[ANY] * n,
        out_shape=[jax.ShapeDtypeStruct((3,) + s.shape[1:], s.dtype) for s in sums],
        scratch_shapes=[pltpu.SemaphoreType.DMA((n, 3)), pltpu.SemaphoreType.DMA((n, 3))],
    )(*sums)


def _row_tile(rows):
    for t in (256, 176, 128):
        if rows % t == 0:
            return t
    raise ValueError(rows)


def _pair_sum(part, recv, core, name):
    _, rows, cols = recv.shape
    tr = _row_tile(rows)

    def body(c_ref, p_ref, r_ref, o_ref):
        del c_ref
        o_ref[...] = (p_ref[...].astype(F32) + r_ref[...].astype(F32)).astype(BF)

    grid_spec = pltpu.PrefetchScalarGridSpec(
        num_scalar_prefetch=1, grid=(4, rows // tr),
        in_specs=[pl.BlockSpec((None, None, tr, cols), lambda j, i, c_ref: (j, c_ref[0], i, 0)),
                  pl.BlockSpec((None, tr, cols), lambda j, i, c_ref: (j, i, 0))],
        out_specs=pl.BlockSpec((None, tr, cols), lambda j, i, c_ref: (j, i, 0)))
    return pl.pallas_call(
        body, name=name, grid_spec=grid_spec, out_shape=jax.ShapeDtypeStruct(recv.shape, BF),
        compiler_params=_params("parallel", "parallel"),
    )(core, part.reshape(4, 2, rows, cols), recv)


def _adamw(w, g, m, v):
    m = ADAM_B1 * m + (1.0 - ADAM_B1) * g
    v = ADAM_B2 * v + (1.0 - ADAM_B2) * (g * g)
    delta = -ADAM_LR * ((m / ADAM_C1) / (jnp.sqrt(v / ADAM_C2) + ADAM_EPS) + ADAM_WD * w)
    return delta, m, v


def _chip_sum_adamw(sums, recv, w, m, v, chip, name):
    rows, cols = w.shape
    tr = _row_tile(rows)

    def body(chip_ref, s_ref, r_ref, w_ref, m_ref, v_ref, g_out, d_out, m_out, v_out):
        del chip_ref
        g = s_ref[...].astype(F32)
        for k in range(3):
            g = g + r_ref[k].astype(F32)
        g_out[...] = g
        d_out[...], m_out[...], v_out[...] = _adamw(w_ref[...], g, m_ref[...], v_ref[...])

    tile = pl.BlockSpec((tr, cols), lambda i, chip_ref: (i, 0))
    grid_spec = pltpu.PrefetchScalarGridSpec(
        num_scalar_prefetch=1, grid=(rows // tr,),
        in_specs=[pl.BlockSpec((None, tr, cols), lambda i, chip_ref: (chip_ref[0], i, 0)),
                  pl.BlockSpec((3, tr, cols), lambda i, chip_ref: (0, i, 0)), tile, tile, tile],
        out_specs=[tile] * 4)
    return pl.pallas_call(
        body, name=name, grid_spec=grid_spec, out_shape=[jax.ShapeDtypeStruct((rows, cols), F32)] * 4,
        compiler_params=_params("parallel"),
    )(chip, sums, recv, w, m, v)


def _sum8_adamw(parts, w, m, v):
    rows = w.shape[0]

    def body(p_ref, w_ref, m_ref, v_ref, g_out, d_out, m_out, v_out):
        g = p_ref[0]
        for d in range(1, N_DEV):
            g = g + p_ref[d]
        g_out[...] = g
        d_out[...], m_out[...], v_out[...] = _adamw(w_ref[...], g, m_ref[...], v_ref[...])

    full = _const_spec((rows, 128))
    return pl.pallas_call(
        body, name="small_sum_adamw", grid=(1,), in_specs=[_const_spec((N_DEV, rows, 128)), full, full, full],
        out_specs=[full] * 4, out_shape=[jax.ShapeDtypeStruct((rows, 128), F32)] * 4,
        compiler_params=_params("arbitrary"),
    )(parts, w, m, v)


def _plain_adamw(g, w, m, v):
    rows = w.shape[0]

    def body(g_ref, w_ref, m_ref, v_ref, d_out, m_out, v_out):
        d_out[...], m_out[...], v_out[...] = _adamw(w_ref[...], g_ref[...], m_ref[...], v_ref[...])

    full = _const_spec((rows, 128))
    return pl.pallas_call(
        body, name="shard_adamw", grid=(1,), in_specs=[full] * 4, out_specs=[full] * 3,
        out_shape=[jax.ShapeDtypeStruct((rows, 128), F32)] * 3, compiler_params=_params("arbitrary"),
    )(g, w, m, v)


def _pack(arrays, rows):
    flat = jnp.concatenate([a.reshape(-1) for a in arrays])
    return jnp.pad(flat, (0, rows * 128 - flat.shape[0])).reshape(rows, 128)


def _unpack(packed, shapes):
    flat = packed.reshape(-1)
    out, at = [], 0
    for s in shapes:
        size = 1
        for d in s:
            size *= d
        out.append(flat[at:at + size].reshape(s))
        at += size
    return out


MM_TILE = 512
N_MM_TILES = SEQ // MM_TILE
CAT_TILE = 1152
N_CAT_TILES = N_CAT // CAT_TILE
SMALL_ROWS = 808
SHARD_ROWS = 32


def kernel(x, g_mix, w_in, b_gate, w_gk_up, b_gk, w_pool_grp, pool_scale, g_gla_head, w_pool_proj, w_gla_proj, w_out, g_ffn, w_up, w_conv, b_conv, w_down, g_final, loss_target, m_g_mix, m_w_in, m_b_gate, m_w_gk_up, m_b_gk, m_w_pool_grp, m_pool_scale, m_g_gla_head, m_w_pool_proj, m_w_gla_proj, m_w_out, m_g_ffn, m_w_up, m_w_conv, m_b_conv, m_w_down, m_g_final, v_g_mix, v_w_in, v_b_gate, v_w_gk_up, v_b_gk, v_w_pool_grp, v_pool_scale, v_g_gla_head, v_w_pool_proj, v_w_gla_proj, v_w_out, v_g_ffn, v_w_up, v_w_conv, v_b_conv, v_w_down, v_g_final):
    xi, yi, ci = lax.axis_index("x"), lax.axis_index("y"), lax.axis_index("c")
    me = 4 * xi + 2 * yi + ci
    core = jnp.reshape(ci, (1,)).astype(jnp.int32)
    chip = jnp.reshape(2 * xi + yi, (1,)).astype(jnp.int32)
    xs, target = x[0], loss_target[0]

    big = dict(w_in=w_in[0], w_pool_proj=w_pool_proj[0], w_gla_proj=w_gla_proj[0], w_out=w_out[0], w_up=w_up[0],
               w_down=w_down[0])
    names = list(big)
    gathered = _all_gather([big[k].astype(BF) for k in names] + [w_gk_up[0], w_conv[0]], "gather_weights")
    wg = dict(zip(names, gathered[:6]))
    wgk_all, wconv_all = gathered[6], gathered[7]
    win = wg["w_in"].transpose(1, 0, 2).reshape(D_MODEL, IN_TOTAL)
    wcat = jnp.concatenate([win[:, 512:2560], win[:, 3600:5648], win[:, 2560:3584], win[:, 0:512],
                            jnp.pad(win[:, 3584:3600], ((0, 0), (0, GK_PAD - GATE_RANK)))], axis=1)
    wpp = wg["w_pool_proj"].transpose(1, 0, 2).reshape(POOL_WIDTH, D_MODEL)
    wgp = wg["w_gla_proj"].reshape(D_MODEL, D_MODEL)
    wout = wg["w_out"].reshape(D_MODEL, D_MODEL)
    wup = wg["w_up"].reshape(N_DEV * D_MODEL, FF_BLK)
    wdown = wg["w_down"].reshape(D_FF, D_MODEL)
    wgk_pad = jnp.pad(wgk_all.transpose(1, 0, 2).reshape(GATE_RANK, GLA_DK), ((0, GK_PAD - GATE_RANK), (0, 0)))
    wconv4 = wconv_all.reshape(2, 4, 3, FF_BLK)
    bconv4 = b_conv.reshape(2, 4, 1, FF_BLK)

    tok = lambda i, j, k: (i, 0)
    whole = lambda i, j, k: (0, 0)
    kblk = lambda i, j, k: (k, 0)

    h = _rms_fwd(xs, g_mix, "rms_mix")
    zcat = _mm(h, wcat, out_shape=(SEQ, N_CAT), out_dtype=F32, grid=(N_CAT_TILES, N_MM_TILES, 1),
               blk_a=(MM_TILE, D_MODEL), blk_b=(D_MODEL, CAT_TILE), blk_o=(MM_TILE, CAT_TILE),
               map_a=lambda j, i, k: (i, 0), map_b=lambda j, i, k: (0, j), map_o=lambda j, i, k: (i, j), name="mm_in")
    ps = _pool_fwd(zcat, w_pool_grp[0], pool_scale)
    y_pool = _mm(ps, wpp, out_shape=(SEQ, D_MODEL), out_dtype=F32, grid=(N_MM_TILES, 1, 1),
                 blk_a=(MM_TILE, POOL_WIDTH), blk_b=(POOL_WIDTH, D_MODEL), blk_o=(MM_TILE, D_MODEL),
                 map_a=tok, map_b=whole, map_o=tok, name="mm_pool_proj")
    la = _gk_fwd(zcat, wgk_pad, b_gk)
    o, states = _gla_fwd(zcat, la)
    og = _post_gla_fwd(o, zcat, g_gla_head)
    sq = dict(out_shape=(SEQ, D_MODEL), grid=(N_MM_TILES, 1, 1), blk_a=(MM_TILE, D_MODEL), blk_b=(D_MODEL, D_MODEL),
              blk_o=(MM_TILE, D_MODEL), map_a=tok, map_b=whole, map_o=tok)
    y_gla = _mm(og, wgp, out_dtype=F32, name="mm_gla_proj", **sq)
    mixed = _mix_fwd(zcat, b_gate, y_pool, y_gla)
    x1 = _mm(mixed, wout, out_dtype=F32, res=xs, name="mm_out", **sq)
    h2 = _rms_fwd(x1, g_ffn, "rms_ffn")
    u = _mm(h2, wup, out_shape=(N_DEV * SEQ, FF_BLK), out_dtype=F32, grid=(N_DEV, N_MM_TILES, 1),
            blk_a=(MM_TILE, D_MODEL), blk_b=(D_MODEL, FF_BLK), blk_o=(MM_TILE, FF_BLK),
            map_a=lambda b, i, k: (i, 0), map_b=lambda b, i, k: (b, 0), map_o=lambda b, i, k: (b * N_MM_TILES + i, 0),
            name="mm_up")
    u4 = u.reshape(2, 4, SEQ, FF_BLK)
    act = _conv_fwd(u4, wconv4, bconv4).reshape(4 * SEQ, FF_BLK)
    x2 = _mm(act, wdown, out_shape=(SEQ, D_MODEL), out_dtype=F32, grid=(N_MM_TILES, 1, 4),
             blk_a=(MM_TILE, FF_BLK), blk_b=(FF_BLK, D_MODEL), blk_o=(MM_TILE, D_MODEL),
             map_a=lambda i, j, k: (k * N_MM_TILES + i, 0), map_b=kblk, map_o=tok, res=x1, name="mm_down")
    loss_part, dx2, dg_final = _final_loss(x2, g_final.reshape(1, D_MODEL), target)

    da = _mm(dx2, wdown, out_shape=(4 * SEQ, FF_BLK), out_dtype=BF, grid=(4, N_MM_TILES, 1),
             blk_a=(MM_TILE, D_MODEL), blk_b=(FF_BLK, D_MODEL), blk_o=(MM_TILE, FF_BLK),
             map_a=lambda b, i, k: (i, 0), map_b=lambda b, i, k: (b, 0), map_o=lambda b, i, k: (b * N_MM_TILES + i, 0),
             tb=True, name="mm_d_act")
    d_wdown = _mm(act, dx2, out_shape=(D_FF, D_MODEL), out_dtype=BF, grid=(4, 1, N_MM_TILES),
                  blk_a=(MM_TILE, FF_BLK), blk_b=(MM_TILE, D_MODEL), blk_o=(FF_BLK, D_MODEL),
                  map_a=lambda b, j, k: (b * N_MM_TILES + k, 0), map_b=kblk, map_o=lambda b, j, k: (b, 0),
                  ta=True, name="mm_d_wdown")
    du4, d_wconv, d_bconv = _conv_bwd(u4, da.reshape(1, 4, SEQ, FF_BLK), wconv4, bconv4)
    du = du4.reshape(N_DEV * SEQ, FF_BLK)
    dh2 = _mm(du, wup, out_shape=(SEQ, D_MODEL), out_dtype=F32, grid=(N_MM_TILES, 1, N_DEV),
              blk_a=(MM_TILE, FF_BLK), blk_b=(D_MODEL, FF_BLK), blk_o=(MM_TILE, D_MODEL),
              map_a=lambda i, j, k: (k * N_MM_TILES + i, 0), map_b=kblk, map_o=tok, tb=True, name="mm_d_h2")
    d_wup = _mm(h2, du, out_shape=(N_DEV * D_MODEL, FF_BLK), out_dtype=BF, grid=(N_DEV, 1, N_MM_TILES),
                blk_a=(MM_TILE, D_MODEL), blk_b=(MM_TILE, FF_BLK), blk_o=(D_MODEL, FF_BLK),
                map_a=kblk, map_b=lambda b, j, k: (b * N_MM_TILES + k, 0), map_o=lambda b, j, k: (b, 0),
                ta=True, name="mm_d_wup")
    dx1, dg_ffn = _rms_bwd(dh2, x1, g_ffn, dx2, "rms_ffn_bwd")

    sq_t = dict(out_shape=(D_MODEL, D_MODEL), grid=(1, 1, N_MM_TILES), blk_a=(MM_TILE, D_MODEL),
                blk_b=(MM_TILE, D_MODEL), blk_o=(D_MODEL, D_MODEL), map_a=kblk, map_b=kblk, map_o=whole, ta=True)
    dmixed = _mm(dx1, wout, out_dtype=F32, tb=True, name="mm_d_mixed", **sq)
    d_wout = _mm(mixed, dx1, out_dtype=BF, name="mm_d_wout", **sq_t)
    dzcat, dy_pool, dy_gla, db_gate = _mix_bwd(dmixed, zcat, b_gate, y_pool, y_gla)
    d_og = _mm(dy_gla, wgp, out_dtype=F32, tb=True, name="mm_d_og", **sq)
    d_wgp = _mm(og, dy_gla, out_dtype=BF, name="mm_d_wgp", **sq_t)
    dzcat, d_o, dg_head = _post_gla_bwd(dzcat, d_og, o, zcat, g_gla_head)
    dzcat, dla = _gla_bwd(dzcat, zcat, la, d_o, states)
    dzcat, d_wgk, db_gk = _gk_bwd(dzcat, zcat, dla, wgk_pad, b_gk)
    dps = _mm(dy_pool, wpp, out_shape=(SEQ, POOL_WIDTH), out_dtype=F32, grid=(N_MM_TILES, 1, 1),
              blk_a=(MM_TILE, D_MODEL), blk_b=(POOL_WIDTH, D_MODEL), blk_o=(MM_TILE, POOL_WIDTH),
              map_a=tok, map_b=whole, map_o=tok, tb=True, name="mm_d_ps")
    d_wpp = _mm(ps, dy_pool, out_shape=(POOL_WIDTH, D_MODEL), out_dtype=F32, grid=(1, 1, N_MM_TILES),
                blk_a=(MM_TILE, POOL_WIDTH), blk_b=(MM_TILE, D_MODEL), blk_o=(POOL_WIDTH, D_MODEL),
                map_a=kblk, map_b=kblk, map_o=whole, ta=True, name="mm_d_wpp")
    dzcat, d_wgrp, d_scale = _pool_bwd(dzcat, zcat, dps, w_pool_grp[0], pool_scale)
    dh = _mm(dzcat, wcat, out_shape=(SEQ, D_MODEL), out_dtype=F32, grid=(N_MM_TILES, 1, N_CAT_TILES),
             blk_a=(MM_TILE, CAT_TILE), blk_b=(D_MODEL, CAT_TILE), blk_o=(MM_TILE, D_MODEL),
             map_a=lambda i, j, k: (i, k), map_b=lambda i, j, k: (0, k), map_o=tok, tb=True, name="mm_d_h")
    d_wcat = _mm(h, dzcat, out_shape=(D_MODEL, N_CAT), out_dtype=F32, grid=(N_CAT_TILES, 1, N_MM_TILES),
                 blk_a=(MM_TILE, D_MODEL), blk_b=(MM_TILE, CAT_TILE), blk_o=(D_MODEL, CAT_TILE),
                 map_a=lambda j, _, k: (k, 0), map_b=lambda j, _, k: (k, j), map_o=lambda j, _, k: (0, j),
                 ta=True, name="mm_d_wcat")
    grad_x, dg_mix = _rms_bwd(dh, xs, g_mix, dx1, "rms_mix_bwd")

    d_win = jnp.concatenate([d_wcat[:, C_POOL:C_POOL + POOL_WIDTH], d_wcat[:, C_QKV:C_QKV + QKV_W],
                             d_wcat[:, C_OG:C_OG + D_MODEL], d_wcat[:, C_GK:C_GK + GATE_RANK],
                             d_wcat[:, C_GATE:C_GATE + GATE_W]], axis=1)
    parts = dict(
        w_in=d_win.reshape(D_MODEL, N_DEV, IN_SHARD).transpose(1, 0, 2).astype(BF),
        w_pool_proj=d_wpp.reshape(POOL_WIDTH, N_DEV, D_MODEL // N_DEV).transpose(1, 0, 2).astype(BF),
        w_gla_proj=d_wgp.reshape(N_DEV, D_MODEL // N_DEV, D_MODEL),
        w_out=d_wout.reshape(N_DEV, D_MODEL // N_DEV, D_MODEL),
        w_up=d_wup.reshape(N_DEV, D_MODEL, FF_BLK),
        w_down=d_wdown.reshape(N_DEV, D_FF // N_DEV, D_MODEL))
    from_sibling = _exchange_sibling([parts[k] for k in names], "grads_to_sibling")
    chip_sums = [_pair_sum(parts[k], r, core, "pair_sum_" + k) for k, r in zip(names, from_sibling)]
    from_chips = _exchange_chips(chip_sums, "grads_to_chips")
    moments = dict(w_in=(m_w_in, v_w_in), w_pool_proj=(m_w_pool_proj, v_w_pool_proj),
                   w_gla_proj=(m_w_gla_proj, v_w_gla_proj), w_out=(m_w_out, v_w_out), w_up=(m_w_up, v_w_up),
                   w_down=(m_w_down, v_w_down))
    res = {}
    for k, s, r in zip(names, chip_sums, from_chips):
        outs = _chip_sum_adamw(s, r, big[k], moments[k][0][0], moments[k][1][0], chip, "adamw_" + k)
        res[k] = [t[None] for t in outs]

    small = [("g_mix", dg_mix, g_mix, m_g_mix, v_g_mix), ("b_gate", db_gate, b_gate, m_b_gate, v_b_gate),
             ("w_gk_up", d_wgk[:GATE_RANK], None, None, None), ("b_gk", db_gk, b_gk, m_b_gk, v_b_gk),
             ("w_pool_grp", d_wgrp, w_pool_grp, m_w_pool_grp, v_w_pool_grp),
             ("pool_scale", d_scale, pool_scale, m_pool_scale, v_pool_scale),
             ("g_gla_head", dg_head, g_gla_head, m_g_gla_head, v_g_gla_head), ("g_ffn", dg_ffn, g_ffn, m_g_ffn, v_g_ffn),
             ("w_conv", d_wconv, None, None, None), ("b_conv", d_bconv, b_conv, m_b_conv, v_b_conv),
             ("g_final", dg_final, g_final, m_g_final, v_g_final)]
    zeros_like_part = lambda t: jnp.zeros(t[1].shape, F32)
    g_all = _all_gather([_pack([t[1] for t in small], SMALL_ROWS)], "gather_small_grads")[0]
    packed = _sum8_adamw(g_all, *[_pack([zeros_like_part(t) if t[i] is None else t[i] for t in small], SMALL_ROWS)
                                  for i in (2, 3, 4)])
    shapes = [t[1].shape if t[2] is None else t[2].shape for t in small]
    unpacked = [_unpack(p, shapes) for p in packed]
    for idx, t in enumerate(small):
        if t[2] is not None:
            res[t[0]] = [unpacked[q][idx] for q in range(4)]
    g_wgk = lax.dynamic_slice(unpacked[0][2], (0, me * (GLA_DK // N_DEV)), (GATE_RANK, GLA_DK // N_DEV))
    g_wconv = lax.dynamic_index_in_dim(unpacked[0][8].reshape(N_DEV, 3, FF_BLK), me, axis=0, keepdims=False)
    shard_shapes = [(1, GATE_RANK, GLA_DK // N_DEV), (1, 3, FF_BLK)]
    shard_out = _plain_adamw(_pack([g_wgk, g_wconv], SHARD_ROWS), _pack([w_gk_up, w_conv], SHARD_ROWS),
                             _pack([m_w_gk_up, m_w_conv], SHARD_ROWS), _pack([v_w_gk_up, v_w_conv], SHARD_ROWS))
    shard_un = [_unpack(p, shard_shapes) for p in shard_out]
    res["w_gk_up"] = [g_wgk[None]] + [s[0] for s in shard_un]
    res["w_conv"] = [g_wconv[None]] + [s[1] for s in shard_un]

    loss = lax.psum(loss_part[0, 0], ("x", "y", "c"))
    order = ["g_mix", "w_in", "b_gate", "w_gk_up", "b_gk", "w_pool_grp", "pool_scale", "g_gla_head", "w_pool_proj",
             "w_gla_proj", "w_out", "g_ffn", "w_up", "w_conv", "b_conv", "w_down", "g_final"]
    return (loss, grad_x[None], *[res[k][0] for k in order], *[res[k][1] for k in order],
            *[res[k][2] for k in order], *[res[k][3] for k in order])
```

```python
import functools

import jax
import jax.numpy as jnp
from jax import lax
from jax.experimental import pallas as pl
from jax.experimental.pallas import tpu as pltpu

F32 = jnp.float32
BF = jnp.bfloat16
HIGHEST = lax.Precision.HIGHEST
MESH = pl.DeviceIdType.MESH

N_DEV = 8
SEQ = 2048
D_MODEL = 1024
CHUNK = 64
EPS = 1e-6
POOL_WIDTH = 512
POOL_WINDOWS = (2, 4, 8, 16)
POOL_GD = 128
POOL_HALO = 16
HEADS = 4
HK = 128
HV = 256
GLA_DK = 512
GATE_RANK = 16
GATE_NORM = 16.0
D_FF = 2816
FF_BLK = 704
IN_TOTAL = 5648
IN_SHARD = 706
C_QKV, C_GATE, C_OG, C_POOL = 0, 2048, 4096, 5120
N_CAT = 5632
R_POOL, R_QKV, R_OG, R_GK, R_GATE = 0, 512, 2560, 3584, 3600
GK_PAD = 128

ADAM_LR, ADAM_B1, ADAM_B2, ADAM_EPS, ADAM_WD, ADAM_STEP = 0.001, 0.9, 0.999, 1e-08, 0.01, 10
ADAM_C1 = 1.0 - ADAM_B1 ** ADAM_STEP
ADAM_C2 = 1.0 - ADAM_B2 ** ADAM_STEP

VMEM_BYTES_V7X = 64 * 1024 * 1024
VMEM_LIMIT = 48 * 1024 * 1024

TOK_TILE = 256
HALO = 8
GLA_CPS = 4


def _params(*sem):
    return pltpu.CompilerParams(dimension_semantics=sem, vmem_limit_bytes=VMEM_LIMIT)


def _const_spec(shape):
    nd = len(shape)
    return pl.BlockSpec(shape, lambda *_: (0,) * nd)


def _dot(a, b, ta=False, tb=False):
    dims = (((0 if ta else 1,), (1 if tb else 0,)), ((), ()))
    return lax.dot_general(a.astype(BF), b.astype(BF), dims, preferred_element_type=F32)


def _dot_exact(a, b):
    return jnp.dot(a, b, precision=HIGHEST, preferred_element_type=F32)


def _sigmoid(x):
    return 1.0 / (1.0 + jnp.exp(-x))


def _mm(a, b, *, out_shape, out_dtype, grid, blk_a, blk_b, blk_o, map_a, map_b, map_o, ta=False, tb=False,
        res=None, name):
    gk = grid[2]

    def body(*refs):
        if res is None:
            a_ref, b_ref, o_ref = refs[:3]
            r_ref = None
            scr = refs[3:]
        else:
            a_ref, b_ref, r_ref, o_ref = refs[:4]
            scr = refs[4:]
        prod = _dot(a_ref[...], b_ref[...], ta, tb)

        def finish(total):
            if r_ref is not None:
                total = total + r_ref[...]
            o_ref[...] = total.astype(out_dtype)

        if gk == 1:
            finish(prod)
        else:
            acc = scr[0]
            k = pl.program_id(2)

            @pl.when(k == 0)
            def _():
                acc[...] = prod

            @pl.when(k > 0)
            def _():
                acc[...] += prod

            @pl.when(k == gk - 1)
            def _():
                finish(acc[...])

    in_specs = [pl.BlockSpec(blk_a, map_a), pl.BlockSpec(blk_b, map_b)]
    args = [a, b]
    if res is not None:
        in_specs.append(pl.BlockSpec(blk_o, map_o))
        args.append(res)
    return pl.pallas_call(
        body, name=name, grid=grid, in_specs=in_specs, out_specs=pl.BlockSpec(blk_o, map_o),
        out_shape=jax.ShapeDtypeStruct(out_shape, out_dtype),
        scratch_shapes=[] if gk == 1 else [pltpu.VMEM(tuple(d for d in blk_o if d is not None), F32)],
        compiler_params=_params("parallel", "parallel", "arbitrary"),
    )(*args)


def _rms_fwd(x, g, name):
    def body(x_ref, g_ref, o_ref):
        xv = x_ref[...]
        r = lax.rsqrt(jnp.mean(xv * xv, axis=-1, keepdims=True) + EPS)
        o_ref[...] = (xv * r * g_ref[...]).astype(BF)

    tile = pl.BlockSpec((TOK_TILE, D_MODEL), lambda i: (i, 0))
    return pl.pallas_call(
        body, name=name, grid=(SEQ // TOK_TILE,), in_specs=[tile, _const_spec((1, D_MODEL))], out_specs=tile,
        out_shape=jax.ShapeDtypeStruct((SEQ, D_MODEL), BF), compiler_params=_params("parallel"),
    )(x, g)


def _rms_bwd(dy, x, g, dres, name):
    def body(dy_ref, x_ref, g_ref, dres_ref, dx_ref, dg_ref):
        xv = x_ref[...]
        r = lax.rsqrt(jnp.mean(xv * xv, axis=-1, keepdims=True) + EPS)
        xn = xv * r
        dyv = dy_ref[...]
        dxn = dyv * g_ref[...]
        dx_ref[...] = dres_ref[...] + r * (dxn - xn * jnp.mean(dxn * xn, axis=-1, keepdims=True))
        part = jnp.sum(dyv * xn, axis=0, keepdims=True)

        @pl.when(pl.program_id(0) == 0)
        def _():
            dg_ref[...] = part

        @pl.when(pl.program_id(0) > 0)
        def _():
            dg_ref[...] += part

    tile = pl.BlockSpec((TOK_TILE, D_MODEL), lambda i: (i, 0))
    vec = _const_spec((1, D_MODEL))
    return pl.pallas_call(
        body, name=name, grid=(SEQ // TOK_TILE,), in_specs=[tile, tile, vec, tile], out_specs=[tile, vec],
        out_shape=[jax.ShapeDtypeStruct((SEQ, D_MODEL), F32), jax.ShapeDtypeStruct((1, D_MODEL), F32)],
        compiler_params=_params("arbitrary"),
    )(dy, x, g, dres)


def _final_loss(x2, g, target):
    def body(x_ref, g_ref, t_ref, loss_ref, dx_ref, dg_ref):
        xv = x_ref[...]
        r = lax.rsqrt(jnp.mean(xv * xv, axis=-1, keepdims=True) + EPS)
        xn = xv * r
        gv = g_ref[...]
        err = xn * gv - t_ref[...]
        lpart = jnp.full((1, 128), 0.5 * jnp.sum(jnp.mean(err * err, axis=-1, keepdims=True)), F32)
        dyv = err * (1.0 / D_MODEL)
        dxn = dyv * gv
        dx_ref[...] = r * (dxn - xn * jnp.mean(dxn * xn, axis=-1, keepdims=True))
        gpart = jnp.sum(dyv * xn, axis=0, keepdims=True)

        @pl.when(pl.program_id(0) == 0)
        def _():
            loss_ref[...] = lpart
            dg_ref[...] = gpart

        @pl.when(pl.program_id(0) > 0)
        def _():
            loss_ref[...] += lpart
            dg_ref[...] += gpart

    tile = pl.BlockSpec((TOK_TILE, D_MODEL), lambda i: (i, 0))
    vec = _const_spec((1, D_MODEL))
    return pl.pallas_call(
        body, name="final_loss", grid=(SEQ // TOK_TILE,), in_specs=[tile, vec, tile],
        out_specs=[_const_spec((1, 128)), tile, vec],
        out_shape=[jax.ShapeDtypeStruct((1, 128), F32), jax.ShapeDtypeStruct((SEQ, D_MODEL), F32),
                   jax.ShapeDtypeStruct((1, D_MODEL), F32)],
        compiler_params=_params("arbitrary"),
    )(x2, g, target)


def _pool_counts(w):
    pos = lax.broadcasted_iota(jnp.int32, (SEQ, 1), 0).astype(F32)
    return jnp.minimum(pos + 1.0, float(w))


def _pool_window(u, w, ext):
    ext[pl.ds(POOL_HALO, SEQ), :] = u
    win = u
    for j in range(1, w):
        win = win + ext[pl.ds(POOL_HALO - j, SEQ), :]
    return win / _pool_counts(w) - u


def _pool_fwd(zcat, w_grp, scale):
    def body(z_ref, w_ref, s_ref, o_ref, ext):
        ext[pl.ds(0, POOL_HALO), :] = jnp.zeros((POOL_HALO, POOL_GD), F32)
        for g, w in enumerate(POOL_WINDOWS):
            cols = slice(g * POOL_GD, (g + 1) * POOL_GD)
            p = _pool_window(z_ref[:, cols], w, ext)
            o_ref[:, cols] = (_dot(p, w_ref[g]) * s_ref[:, cols]).astype(BF)

    return pl.pallas_call(
        body, name="pool_fwd", grid=(1,),
        in_specs=[pl.BlockSpec((SEQ, POOL_WIDTH), lambda i: (0, C_POOL // POOL_WIDTH)),
                  _const_spec((4, POOL_GD, POOL_GD)), _const_spec((1, POOL_WIDTH))],
        out_specs=_const_spec((SEQ, POOL_WIDTH)), out_shape=jax.ShapeDtypeStruct((SEQ, POOL_WIDTH), BF),
        scratch_shapes=[pltpu.VMEM((POOL_HALO + SEQ, POOL_GD), F32)], compiler_params=_params("arbitrary"),
    )(zcat, w_grp, scale)


def _pool_bwd(dzcat, zcat, dps, w_grp, scale):
    def body(dz_in, z_ref, dps_ref, w_ref, s_ref, dz_ref, dw_ref, dsc_ref, ext, ext2):
        del dz_in
        ext[pl.ds(0, POOL_HALO), :] = jnp.zeros((POOL_HALO, POOL_GD), F32)
        ext2[pl.ds(SEQ, POOL_HALO), :] = jnp.zeros((POOL_HALO, POOL_GD), F32)
        for g, w in enumerate(POOL_WINDOWS):
            cols = slice(g * POOL_GD, (g + 1) * POOL_GD)
            p = _pool_window(z_ref[:, cols], w, ext)
            wg = w_ref[g]
            pg = _dot(p, wg)
            dpsv = dps_ref[:, cols]
            dsc_ref[:, cols] = jnp.sum(dpsv * pg, axis=0, keepdims=True)
            dpg = dpsv * s_ref[:, cols]
            dw_ref[g] = _dot(p, dpg, ta=True)
            dp = _dot(dpg, wg, tb=True)
            dpc = dp / _pool_counts(w)
            ext2[pl.ds(0, SEQ), :] = dpc
            du = dpc
            for j in range(1, w):
                du = du + ext2[pl.ds(j, SEQ), :]
            dz_ref[:, cols] = (du - dp).astype(BF)

    return pl.pallas_call(
        body, name="pool_bwd", grid=(1,),
        in_specs=[pl.BlockSpec(memory_space=pl.ANY),
                  pl.BlockSpec((SEQ, POOL_WIDTH), lambda i: (0, C_POOL // POOL_WIDTH)),
                  _const_spec((SEQ, POOL_WIDTH)), _const_spec((4, POOL_GD, POOL_GD)), _const_spec((1, POOL_WIDTH))],
        out_specs=[pl.BlockSpec((SEQ, POOL_WIDTH), lambda i: (0, C_POOL // POOL_WIDTH)),
                   _const_spec((4, POOL_GD, POOL_GD)), _const_spec((1, POOL_WIDTH))],
        out_shape=[jax.ShapeDtypeStruct((SEQ, N_CAT), BF), jax.ShapeDtypeStruct((4, POOL_GD, POOL_GD), F32),
                   jax.ShapeDtypeStruct((1, POOL_WIDTH), F32)],
        scratch_shapes=[pltpu.VMEM((POOL_HALO + SEQ, POOL_GD), F32), pltpu.VMEM((SEQ + POOL_HALO, POOL_GD), F32)],
        input_output_aliases={0: 0}, compiler_params=_params("arbitrary"),
    )(dzcat, zcat, dps, w_grp, scale)


GK_TILE = 512


def _gk_fwd(h, wt_gk, wgk_pad, b_gk):
    def body(h_ref, wt_ref, w_ref, b_ref, la_ref):
        z_gk = _dot(h_ref[...], wt_ref[...], tb=True)
        pre = _dot(z_gk, w_ref[...]) + b_ref[...]
        la_ref[...] = (jnp.minimum(pre, 0.0) - jnp.log(1.0 + jnp.exp(-jnp.abs(pre)))) * (1.0 / GATE_NORM)

    return pl.pallas_call(
        body, name="gk_fwd", grid=(SEQ // GK_TILE,),
        in_specs=[pl.BlockSpec((GK_TILE, D_MODEL), lambda i: (i, 0)), _const_spec((GK_PAD, D_MODEL)),
                  _const_spec((GK_PAD, GLA_DK)), _const_spec((1, GLA_DK))],
        out_specs=pl.BlockSpec((GK_TILE, GLA_DK), lambda i: (i, 0)),
        out_shape=jax.ShapeDtypeStruct((SEQ, GLA_DK), F32), compiler_params=_params("parallel"),
    )(h, wt_gk, wgk_pad, b_gk)


def _gk_bwd(dla, h, wt_gk, wgk_pad, b_gk):
    def body(dla_ref, h_ref, wt_ref, w_ref, b_ref, dh_ref, dwt_ref, dw_ref, db_ref):
        hv = h_ref[...]
        wtv = wt_ref[...]
        wv = w_ref[...]
        z_gk = _dot(hv, wtv, tb=True)
        pre = _dot(z_gk, wv) + b_ref[...]
        dpre = dla_ref[...] * (1.0 / GATE_NORM) * (1.0 - _sigmoid(pre))
        dz_gk = _dot(dpre, wv, tb=True)
        dh_ref[...] = _dot(dz_gk, wtv)
        dwtp = _dot(dz_gk, hv, ta=True)
        dwp = _dot(z_gk, dpre, ta=True)
        dbp = jnp.sum(dpre, axis=0, keepdims=True)

        @pl.when(pl.program_id(0) == 0)
        def _():
            dwt_ref[...] = dwtp
            dw_ref[...] = dwp
            db_ref[...] = dbp

        @pl.when(pl.program_id(0) > 0)
        def _():
            dwt_ref[...] += dwtp
            dw_ref[...] += dwp
            db_ref[...] += dbp

    tile = pl.BlockSpec((GK_TILE, D_MODEL), lambda i: (i, 0))
    return pl.pallas_call(
        body, name="gk_bwd", grid=(SEQ // GK_TILE,),
        in_specs=[pl.BlockSpec((GK_TILE, GLA_DK), lambda i: (i, 0)), tile, _const_spec((GK_PAD, D_MODEL)),
                  _const_spec((GK_PAD, GLA_DK)), _const_spec((1, GLA_DK))],
        out_specs=[tile, _const_spec((GK_PAD, D_MODEL)), _const_spec((GK_PAD, GLA_DK)), _const_spec((1, GLA_DK))],
        out_shape=[jax.ShapeDtypeStruct((SEQ, D_MODEL), F32), jax.ShapeDtypeStruct((GK_PAD, D_MODEL), F32),
                   jax.ShapeDtypeStruct((GK_PAD, GLA_DK), F32), jax.ShapeDtypeStruct((1, GLA_DK), F32)],
        compiler_params=_params("arbitrary"),
    )(dla, h, wt_gk, wgk_pad, b_gk)


GLA_ROWS = GLA_CPS * CHUNK
GLA_STEPS = SEQ // GLA_ROWS
QKV_W = 2048


def _gla_chunk(qkv_ref, la_ref, rows, h):
    tri = lax.broadcasted_iota(jnp.int32, (CHUNK, CHUNK), 0) >= lax.broadcasted_iota(jnp.int32, (CHUNK, CHUNK), 1)
    q = qkv_ref[rows, h * HK:(h + 1) * HK] * (HK ** -0.5)
    k = qkv_ref[rows, GLA_DK + h * HK:GLA_DK + (h + 1) * HK]
    v = qkv_ref[rows, 2 * GLA_DK + h * HV:2 * GLA_DK + (h + 1) * HV]
    la = la_ref[rows, h * HK:(h + 1) * HK]
    bc = _dot_exact(tri.astype(F32), la)
    e_pos, e_neg = jnp.exp(bc), jnp.exp(-bc)
    dl = jnp.exp(jnp.sum(la, axis=0, keepdims=True))
    q_fw, q_bw, k_fw, k_bw = q * e_pos, q * e_neg, k * e_neg, k * e_pos
    scores = jnp.where(tri, _dot(q_fw, k_fw, tb=True), _dot(q_bw, k_bw, tb=True))
    return tri, v, e_pos, e_neg, dl, q_fw, q_bw, k_fw, k_bw, scores


def _gla_fwd(zcat, la):
    def body(qkv_ref, la_ref, o_ref, st_ref, state):
        @pl.when(pl.program_id(0) == 0)
        def _():
            state[...] = jnp.zeros_like(state)

        for c in range(GLA_CPS):
            rows = slice(c * CHUNK, (c + 1) * CHUNK)
            for h in range(HEADS):
                _, v, _, _, dl, q_fw, _, k_fw, _, scores = _gla_chunk(qkv_ref, la_ref, rows, h)
                st = state[h]
                st_ref[c, h] = st
                o_ref[rows, h * HV:(h + 1) * HV] = _dot(scores, v) + _dot(q_fw, st, tb=True)
                state[h] = st * dl + _dot(v, k_fw * dl, ta=True)

    return pl.pallas_call(
        body, name="gla_fwd", grid=(GLA_STEPS,),
        in_specs=[pl.BlockSpec((GLA_ROWS, QKV_W), lambda i: (i, 0)), pl.BlockSpec((GLA_ROWS, GLA_DK), lambda i: (i, 0))],
        out_specs=[pl.BlockSpec((GLA_ROWS, D_MODEL), lambda i: (i, 0)),
                   pl.BlockSpec((GLA_CPS, HEADS, HV, HK), lambda i: (i, 0, 0, 0))],
        out_shape=[jax.ShapeDtypeStruct((SEQ, D_MODEL), F32),
                   jax.ShapeDtypeStruct((SEQ // CHUNK, HEADS, HV, HK), F32)],
        scratch_shapes=[pltpu.VMEM((HEADS, HV, HK), F32)], compiler_params=_params("arbitrary"),
    )(zcat, la)


def _gla_bwd(dzcat, zcat, la, d_o, states):
    def body(dz_in, qkv_ref, la_ref, do_ref, st_ref, dqkv_ref, dla_ref, dstate):
        del dz_in

        @pl.when(pl.program_id(0) == 0)
        def _():
            dstate[...] = jnp.zeros_like(dstate)

        last_row = lax.broadcasted_iota(jnp.int32, (CHUNK, HK), 0) == CHUNK - 1
        upper = (lax.broadcasted_iota(jnp.int32, (CHUNK, CHUNK), 0)
                 <= lax.broadcasted_iota(jnp.int32, (CHUNK, CHUNK), 1)).astype(F32)
        for c in reversed(range(GLA_CPS)):
            rows = slice(c * CHUNK, (c + 1) * CHUNK)
            for h in range(HEADS):
                tri, v, e_pos, e_neg, dl, q_fw, q_bw, k_fw, k_bw, scores = _gla_chunk(qkv_ref, la_ref, rows, h)
                st = st_ref[c, h]
                dst = dstate[h]
                d_out = do_ref[rows, h * HV:(h + 1) * HV]
                k_dec = k_fw * dl
                dp = _dot(d_out, v, tb=True)
                dp_fw = jnp.where(tri, dp, 0.0)
                dp_bw = jnp.where(tri, 0.0, dp)
                dv = _dot(scores, d_out, ta=True) + _dot(k_dec, dst, tb=True)
                dk_dec = _dot(v, dst)
                dq_fw = _dot(dp_fw, k_fw) + _dot(d_out, st)
                dk_fw = _dot(dp_fw, q_fw, ta=True) + dk_dec * dl
                dq_bw = _dot(dp_bw, k_bw)
                dk_bw = _dot(dp_bw, q_bw, ta=True)
                ddl = jnp.sum(st * dst, axis=0, keepdims=True) + jnp.sum(k_fw * dk_dec, axis=0, keepdims=True)
                dstate[h] = dst * dl + _dot(d_out, q_fw, ta=True)
                dq = (dq_fw * e_pos + dq_bw * e_neg) * (HK ** -0.5)
                dk = dk_fw * e_neg + dk_bw * e_pos
                db = dq_fw * q_fw - dk_fw * k_fw - dq_bw * q_bw + dk_bw * k_bw + jnp.where(last_row, ddl * dl, 0.0)
                dla_ref[rows, h * HK:(h + 1) * HK] = _dot_exact(upper, db)
                dqkv_ref[rows, h * HK:(h + 1) * HK] = dq.astype(BF)
                dqkv_ref[rows, GLA_DK + h * HK:GLA_DK + (h + 1) * HK] = dk.astype(BF)
                dqkv_ref[rows, 2 * GLA_DK + h * HV:2 * GLA_DK + (h + 1) * HV] = dv.astype(BF)

    rev = lambda i: (GLA_STEPS - 1 - i, 0)
    return pl.pallas_call(
        body, name="gla_bwd", grid=(GLA_STEPS,),
        in_specs=[pl.BlockSpec(memory_space=pl.ANY), pl.BlockSpec((GLA_ROWS, QKV_W), rev),
                  pl.BlockSpec((GLA_ROWS, GLA_DK), rev), pl.BlockSpec((GLA_ROWS, D_MODEL), rev),
                  pl.BlockSpec((GLA_CPS, HEADS, HV, HK), lambda i: (GLA_STEPS - 1 - i, 0, 0, 0))],
        out_specs=[pl.BlockSpec((GLA_ROWS, QKV_W), rev), pl.BlockSpec((GLA_ROWS, GLA_DK), rev)],
        out_shape=[jax.ShapeDtypeStruct((SEQ, N_CAT), BF), jax.ShapeDtypeStruct((SEQ, GLA_DK), F32)],
        scratch_shapes=[pltpu.VMEM((HEADS, HV, HK), F32)], input_output_aliases={0: 0},
        compiler_params=_params("arbitrary"),
    )(dzcat, zcat, la, d_o, states)


def _silu_parts(x):
    s = _sigmoid(x)
    return x * s, s * (1.0 + x * (1.0 - s))


def _post_gla_fwd(o, zcat, g_head):
    def body(o_ref, zog_ref, g_ref, out_ref):
        for h in range(HEADS):
            cols = slice(h * HV, (h + 1) * HV)
            ov = o_ref[:, cols]
            r = lax.rsqrt(jnp.mean(ov * ov, axis=-1, keepdims=True) + EPS)
            act, _ = _silu_parts(zog_ref[:, cols])
            out_ref[:, cols] = (ov * r * g_ref[...] * act).astype(BF)

    tile = pl.BlockSpec((TOK_TILE, D_MODEL), lambda i: (i, 0))
    return pl.pallas_call(
        body, name="post_gla_fwd", grid=(SEQ // TOK_TILE,),
        in_specs=[tile, pl.BlockSpec((TOK_TILE, D_MODEL), lambda i: (i, C_OG // D_MODEL)), _const_spec((1, HV))],
        out_specs=tile, out_shape=jax.ShapeDtypeStruct((SEQ, D_MODEL), BF), compiler_params=_params("parallel"),
    )(o, zcat, g_head)


def _post_gla_bwd(dzcat, d_og, o, zcat, g_head):
    def body(dz_in, dog_ref, o_ref, zog_ref, g_ref, dz_ref, do_ref, dg_ref):
        del dz_in
        gpart = jnp.zeros((1, HV), F32)
        gv = g_ref[...]
        for h in range(HEADS):
            cols = slice(h * HV, (h + 1) * HV)
            ov = o_ref[:, cols]
            r = lax.rsqrt(jnp.mean(ov * ov, axis=-1, keepdims=True) + EPS)
            on = ov * r
            act, dact = _silu_parts(zog_ref[:, cols])
            dogv = dog_ref[:, cols]
            dz_ref[:, cols] = (dogv * on * gv * dact).astype(BF)
            d_on_g = dogv * act
            gpart = gpart + jnp.sum(d_on_g * on, axis=0, keepdims=True)
            dxn = d_on_g * gv
            do_ref[:, cols] = r * (dxn - on * jnp.mean(dxn * on, axis=-1, keepdims=True))

        @pl.when(pl.program_id(0) == 0)
        def _():
            dg_ref[...] = gpart

        @pl.when(pl.program_id(0) > 0)
        def _():
            dg_ref[...] += gpart

    tile = pl.BlockSpec((TOK_TILE, D_MODEL), lambda i: (i, 0))
    ogspec = pl.BlockSpec((TOK_TILE, D_MODEL), lambda i: (i, C_OG // D_MODEL))
    return pl.pallas_call(
        body, name="post_gla_bwd", grid=(SEQ // TOK_TILE,),
        in_specs=[pl.BlockSpec(memory_space=pl.ANY), tile, tile, ogspec, _const_spec((1, HV))],
        out_specs=[ogspec, tile, _const_spec((1, HV))],
        out_shape=[jax.ShapeDtypeStruct((SEQ, N_CAT), BF), jax.ShapeDtypeStruct((SEQ, D_MODEL), F32),
                   jax.ShapeDtypeStruct((1, HV), F32)],
        input_output_aliases={0: 0}, compiler_params=_params("arbitrary"),
    )(dzcat, d_og, o, zcat, g_head)


GATE_W = 2 * D_MODEL


def _mix_fwd(zcat, b_gate, y_pool, y_gla):
    def body(zg_ref, b_ref, yp_ref, yg_ref, out_ref):
        g0 = _sigmoid(zg_ref[:, :D_MODEL] + b_ref[:, :D_MODEL])
        g1 = _sigmoid(zg_ref[:, D_MODEL:] + b_ref[:, D_MODEL:])
        out_ref[...] = (g0 * yp_ref[...] + g1 * yg_ref[...]).astype(BF)

    tile = pl.BlockSpec((TOK_TILE, D_MODEL), lambda i: (i, 0))
    return pl.pallas_call(
        body, name="mix_fwd", grid=(SEQ // TOK_TILE,),
        in_specs=[pl.BlockSpec((TOK_TILE, GATE_W), lambda i: (i, C_GATE // GATE_W)), _const_spec((1, GATE_W)), tile, tile],
        out_specs=tile, out_shape=jax.ShapeDtypeStruct((SEQ, D_MODEL), BF), compiler_params=_params("parallel"),
    )(zcat, b_gate, y_pool, y_gla)


def _mix_bwd(dmixed, zcat, b_gate, y_pool, y_gla):
    def body(dm_ref, zg_ref, b_ref, yp_ref, yg_ref, dz_ref, dyp_ref, dyg_ref, db_ref):
        dm = dm_ref[...]
        g0 = _sigmoid(zg_ref[:, :D_MODEL] + b_ref[:, :D_MODEL])
        g1 = _sigmoid(zg_ref[:, D_MODEL:] + b_ref[:, D_MODEL:])
        dyp_ref[...] = (dm * g0).astype(BF)
        dyg_ref[...] = (dm * g1).astype(BF)
        dz0 = dm * yp_ref[...] * g0 * (1.0 - g0)
        dz1 = dm * yg_ref[...] * g1 * (1.0 - g1)
        dz_ref[:, :D_MODEL] = dz0.astype(BF)
        dz_ref[:, D_MODEL:] = dz1.astype(BF)
        b0 = jnp.sum(dz0, axis=0, keepdims=True)
        b1 = jnp.sum(dz1, axis=0, keepdims=True)

        @pl.when(pl.program_id(0) == 0)
        def _():
            db_ref[:, :D_MODEL] = b0
            db_ref[:, D_MODEL:] = b1

        @pl.when(pl.program_id(0) > 0)
        def _():
            db_ref[:, :D_MODEL] += b0
            db_ref[:, D_MODEL:] += b1

    tile = pl.BlockSpec((TOK_TILE, D_MODEL), lambda i: (i, 0))
    gspec = pl.BlockSpec((TOK_TILE, GATE_W), lambda i: (i, C_GATE // GATE_W))
    return pl.pallas_call(
        body, name="mix_bwd", grid=(SEQ // TOK_TILE,),
        in_specs=[tile, gspec, _const_spec((1, GATE_W)), tile, tile],
        out_specs=[gspec, tile, tile, _const_spec((1, GATE_W))],
        out_shape=[jax.ShapeDtypeStruct((SEQ, N_CAT), BF), jax.ShapeDtypeStruct((SEQ, D_MODEL), BF),
                   jax.ShapeDtypeStruct((SEQ, D_MODEL), BF), jax.ShapeDtypeStruct((1, GATE_W), F32)],
        compiler_params=_params("arbitrary"),
    )(dmixed, zcat, b_gate, y_pool, y_gla)


N_TOK_TILES = SEQ // TOK_TILE
HALO_PER_TILE = TOK_TILE // HALO


LANE_TILES = tuple((lo, min(128, FF_BLK - lo)) for lo in range(0, FF_BLK, 128))


def _taps(w_ref, b_ref, half, lanes, rows):
    shape = (rows, lanes.stop - lanes.start)
    return ([jnp.broadcast_to(w_ref[half, j:j + 1, lanes], shape) for j in range(3)],
            jnp.broadcast_to(b_ref[half, :, lanes], shape))


def _shifted(ext, row, n, lanes):
    return [ext[pl.ds(row - 2 + j, n), lanes] for j in range(3)]


def _conv_strip(u3, taps, bias):
    return bias + u3[0] * taps[0] + u3[1] * taps[1] + u3[2] * taps[2]


def _pair_specs(pairs):
    tile = pl.BlockSpec((pairs, None, TOK_TILE, FF_BLK), lambda b, i: (0, b, i, 0))
    before = pl.BlockSpec((pairs, None, HALO, FF_BLK), lambda b, i: (0, b, jnp.maximum(i * HALO_PER_TILE - 1, 0), 0))
    after = pl.BlockSpec((pairs, None, HALO, FF_BLK),
                         lambda b, i: (0, b, jnp.minimum((i + 1) * HALO_PER_TILE, SEQ // HALO - 1), 0))

    def vec(rows):
        return pl.BlockSpec((2, None, rows, FF_BLK), lambda b, i: (0, b, 0, 0))

    return tile, before, after, vec


def _fill_ext(ext, half, before_ref, tile_ref, after_ref=None):
    i = pl.program_id(1)
    ext[pl.ds(0, HALO), :] = jnp.where(i > 0, before_ref[half], 0.0)
    ext[pl.ds(HALO, TOK_TILE), :] = tile_ref[half]
    if after_ref is not None:
        ext[pl.ds(HALO + TOK_TILE, HALO), :] = after_ref[half]


def _conv_fwd(u, w_conv, b_conv):
    strip = 16

    def body(u_ref, ub_ref, w_ref, b_ref, a_ref, ext_g, ext_v):
        _fill_ext(ext_g, 0, ub_ref, u_ref)
        _fill_ext(ext_v, 1, ub_ref, u_ref)
        for lo, width in LANE_TILES:
            lanes = slice(lo, lo + width)
            taps_g, bias_g = _taps(w_ref, b_ref, 0, lanes, strip)
            taps_v, bias_v = _taps(w_ref, b_ref, 1, lanes, strip)
            for s in range(TOK_TILE // strip):
                cg = _conv_strip(_shifted(ext_g, HALO + s * strip, strip, lanes), taps_g, bias_g)
                cv = _conv_strip(_shifted(ext_v, HALO + s * strip, strip, lanes), taps_v, bias_v)
                a_ref[0, s * strip:(s + 1) * strip, lanes] = (cg * _sigmoid(cg) * cv).astype(BF)

    tile, before, _, vec = _pair_specs(2)
    out_tile, _, _, _ = _pair_specs(1)
    return pl.pallas_call(
        body, name="conv_fwd", grid=(4, N_TOK_TILES), in_specs=[tile, before, vec(3), vec(1)],
        out_specs=out_tile, out_shape=jax.ShapeDtypeStruct((1, 4, SEQ, FF_BLK), BF),
        scratch_shapes=[pltpu.VMEM((HALO + TOK_TILE, FF_BLK), F32)] * 2, compiler_params=_params("parallel", "parallel"),
    )(u, u, w_conv, b_conv)


def _conv_bwd(u, da, w_conv, b_conv):
    strip = 8
    n_strips = TOK_TILE // strip

    def body(u_ref, ub_ref, ua_ref, da_ref, daa_ref, w_ref, b_ref, du_ref, dw_ref, db_ref, ext_g, ext_v, ext_dg, ext_dv):
        i = pl.program_id(1)
        _fill_ext(ext_g, 0, ub_ref, u_ref, ua_ref)
        _fill_ext(ext_v, 1, ub_ref, u_ref, ua_ref)

        @pl.when(i == 0)
        def _():
            dw_ref[...] = jnp.zeros_like(dw_ref)
            db_ref[...] = jnp.zeros_like(db_ref)

        for lo, width in LANE_TILES:
            lanes = slice(lo, lo + width)
            taps = [_taps(w_ref, b_ref, half, lanes, strip) for half in range(2)]
            exts_u, exts_d = (ext_g, ext_v), (ext_dg, ext_dv)
            acc_w = [[jnp.zeros((strip, width), F32) for _ in range(3)] for _ in range(2)]
            acc_b = [jnp.zeros((strip, width), F32) for _ in range(2)]
            da_pair, pending = None, [None, None]
            for s in range(n_strips + 1):
                u3 = [_shifted(exts_u[half], HALO + s * strip, strip, lanes) for half in range(2)]
                cg = _conv_strip(u3[0], *taps[0])
                cv = _conv_strip(u3[1], *taps[1])
                act, dact = _silu_parts(cg)
                if s == n_strips:
                    da = jnp.where(i < N_TOK_TILES - 1, daa_ref[0, :, lanes].astype(F32), 0.0)
                elif s % 2 == 0:
                    da_pair = da_ref[0, s * strip:(s + 2) * strip, lanes].astype(F32)
                    da = da_pair[:strip]
                else:
                    da = da_pair[strip:]
                dc = (da * cv * dact, da * act)
                for half in range(2):
                    exts_d[half][pl.ds(s * strip, strip), lanes] = dc[half]
                    if s < n_strips:
                        for j in range(3):
                            acc_w[half][j] = acc_w[half][j] + dc[half] * u3[half][j]
                        acc_b[half] = acc_b[half] + dc[half]
                    if s >= 1:
                        p = (s - 1) * strip
                        w3 = taps[half][0]
                        du = (exts_d[half][pl.ds(p, strip), lanes] * w3[2] + exts_d[half][pl.ds(p + 1, strip), lanes] * w3[1]
                              + exts_d[half][pl.ds(p + 2, strip), lanes] * w3[0])
                        if (s - 1) % 2 == 0:
                            pending[half] = du
                        else:
                            du_ref[half, p - strip:p + strip, lanes] = jnp.concatenate([pending[half], du], axis=0).astype(BF)
            for half in range(2):
                for j in range(3):
                    dw_ref[half, j:j + 1, lanes] += jnp.sum(acc_w[half][j], axis=0, keepdims=True)
                db_ref[half, :, lanes] += jnp.sum(acc_b[half], axis=0, keepdims=True)

    tile, before, after, vec = _pair_specs(2)
    da_tile, _, da_after_spec, _ = _pair_specs(1)
    return pl.pallas_call(
        body, name="conv_bwd", grid=(4, N_TOK_TILES),
        in_specs=[tile, before, after, da_tile, da_after_spec, vec(3), vec(1)],
        out_specs=[tile, vec(3), vec(1)],
        out_shape=[jax.ShapeDtypeStruct((2, 4, SEQ, FF_BLK), BF), jax.ShapeDtypeStruct((2, 4, 3, FF_BLK), F32),
                   jax.ShapeDtypeStruct((2, 4, 1, FF_BLK), F32)],
        scratch_shapes=[pltpu.VMEM((2 * HALO + TOK_TILE, FF_BLK), F32)] * 4,
        compiler_params=_params("parallel", "arbitrary"),
    )(u, u, u, da, da, w_conv, b_conv)


ANY = pl.BlockSpec(memory_space=pl.ANY)


def _place():
    x, y, c = lax.axis_index("x"), lax.axis_index("y"), lax.axis_index("c")
    other_chips = [(1 - x, y), (x, 1 - y), (1 - x, 1 - y)]
    return x, y, c, other_chips


def _all_gather(shards, name):
    n = len(shards)

    def body(*refs):
        src, out = refs[:n], refs[n:2 * n]
        send_sems, recv_sems, local_sems = refs[2 * n:]
        x, y, c, chips = _place()
        me, sibling = (x, y, c), (x, y, 1 - c)

        def copy(a, k, block, to, own=False):
            dst = out[a].at[4 * block[0] + 2 * block[1] + block[2]]
            return pltpu.make_async_remote_copy(src_ref=src[a] if own else dst, dst_ref=dst, send_sem=send_sems.at[a, k],
                                                recv_sem=recv_sems.at[a, k], device_id=to, device_id_type=MESH)

        mine = [pltpu.make_async_copy(src[a], out[a].at[4 * x + 2 * y + c], local_sems.at[a]) for a in range(n)]
        first = []
        for a in range(n):
            mine[a].start()
            first.append(copy(a, 0, me, sibling, own=True))
            first += [copy(a, 1 + j, me, (*chip, c), own=True) for j, chip in enumerate(chips)]
        for cp in first:
            cp.start()
        passed = []
        for j, chip in enumerate(chips):
            for a in range(n):
                copy(a, 1 + j, (*chip, c), me).wait_recv()
                passed.append(copy(a, 4 + j, (*chip, c), sibling))
                passed[-1].start()
        for a in range(n):
            copy(a, 0, sibling, me).wait_recv()
            for j, chip in enumerate(chips):
                copy(a, 4 + j, (*chip, 1 - c), me).wait_recv()
        for cp in first + passed:
            cp.wait_send()
        for cp in mine:
            cp.wait()

    return pl.pallas_call(
        body, name=name, in_specs=[ANY] * n, out_specs=[ANY] * n,
        out_shape=[jax.ShapeDtypeStruct((N_DEV,) + s.shape, s.dtype) for s in shards],
        scratch_shapes=[pltpu.SemaphoreType.DMA((n, 7)), pltpu.SemaphoreType.DMA((n, 7)), pltpu.SemaphoreType.DMA((n,))],
    )(*shards)


def _exchange_sibling(parts, name):
    n = len(parts)

    def body(*refs):
        src, out = refs[:n], refs[n:2 * n]
        send_sems, recv_sems = refs[2 * n:]
        x, y, c, _ = _place()
        copies = [pltpu.make_async_remote_copy(src_ref=src[a].at[2 * j + 1 - c], dst_ref=out[a].at[j],
                                               send_sem=send_sems.at[a, j], recv_sem=recv_sems.at[a, j],
                                               device_id=(x, y, 1 - c), device_id_type=MESH)
                  for a in range(n) for j in range(4)]
        for cp in copies:
            cp.start()
        for cp in copies:
            cp.wait()

    return pl.pallas_call(
        body, name=name, in_specs=[ANY] * n, out_specs=[ANY] * n,
        out_shape=[jax.ShapeDtypeStruct((4,) + p.shape[1:], p.dtype) for p in parts],
        scratch_shapes=[pltpu.SemaphoreType.DMA((n, 4)), pltpu.SemaphoreType.DMA((n, 4))],
    )(*parts)


def _exchange_chips(sums, name):
    n = len(sums)

    def body(*refs):
        src, out = refs[:n], refs[n:2 * n]
        send_sems, recv_sems = refs[2 * n:]
        _, _, c, chips = _place()
        copies = [pltpu.make_async_remote_copy(src_ref=src[a].at[2 * px + py], dst_ref=out[a].at[k],
                                               send_sem=send_sems.at[a, k], recv_sem=recv_sems.at[a, k],
                                               device_id=(px, py, c), device_id_type=MESH)
                  for a in range(n) for k, (px, py) in enumerate(chips)]
        for cp in copies:
            cp.start()
        for cp in copies:
            cp.wait()

    return pl.pallas_call(
        body, name=name, in_specs=[ANY] * n, out_specs=[ANY] * n,
        out_shape=[jax.ShapeDtypeStruct((3,) + s.shape[1:], s.dtype) for s in sums],
        scratch_shapes=[pltpu.SemaphoreType.DMA((n, 3)), pltpu.SemaphoreType.DMA((n, 3))],
    )(*sums)


def _tile_2d(rows, cols):
    for t in (256, 176, 128):
        if rows % t == 0:
            return t, cols
    return rows, 256


def _pair_sum(part, recv, core, name):
    _, rows, cols = recv.shape
    tr, tc = _tile_2d(rows, cols)

    def body(c_ref, p_ref, r_ref, o_ref):
        del c_ref
        o_ref[...] = (p_ref[...].astype(F32) + r_ref[...].astype(F32)).astype(BF)

    grid_spec = pltpu.PrefetchScalarGridSpec(
        num_scalar_prefetch=1, grid=(4, rows // tr, cols // tc),
        in_specs=[pl.BlockSpec((None, None, tr, tc), lambda j, i, k, c_ref: (j, c_ref[0], i, k)),
                  pl.BlockSpec((None, tr, tc), lambda j, i, k, c_ref: (j, i, k))],
        out_specs=pl.BlockSpec((None, tr, tc), lambda j, i, k, c_ref: (j, i, k)))
    return pl.pallas_call(
        body, name=name, grid_spec=grid_spec, out_shape=jax.ShapeDtypeStruct(recv.shape, BF),
        compiler_params=_params("parallel", "parallel", "parallel"),
    )(core, part.reshape(4, 2, rows, cols), recv)


def _adamw(w, g, m, v):
    m = ADAM_B1 * m + (1.0 - ADAM_B1) * g
    v = ADAM_B2 * v + (1.0 - ADAM_B2) * (g * g)
    delta = -ADAM_LR * ((m / ADAM_C1) / (jnp.sqrt(v / ADAM_C2) + ADAM_EPS) + ADAM_WD * w)
    return delta, m, v


def _chip_sum_adamw(sums, recv, w, m, v, chip, name):
    rows, cols = w.shape
    tr, tc = _tile_2d(rows, cols)

    def body(chip_ref, s_ref, r_ref, w_ref, m_ref, v_ref, g_out, d_out, m_out, v_out):
        del chip_ref
        g = s_ref[...].astype(F32)
        for k in range(3):
            g = g + r_ref[k].astype(F32)
        g_out[...] = g
        d_out[...], m_out[...], v_out[...] = _adamw(w_ref[...], g, m_ref[...], v_ref[...])

    tile = pl.BlockSpec((tr, tc), lambda i, k, chip_ref: (i, k))
    grid_spec = pltpu.PrefetchScalarGridSpec(
        num_scalar_prefetch=1, grid=(rows // tr, cols // tc),
        in_specs=[pl.BlockSpec((None, tr, tc), lambda i, k, chip_ref: (chip_ref[0], i, k)),
                  pl.BlockSpec((3, tr, tc), lambda i, k, chip_ref: (0, i, k)), tile, tile, tile],
        out_specs=[tile] * 4)
    return pl.pallas_call(
        body, name=name, grid_spec=grid_spec, out_shape=[jax.ShapeDtypeStruct((rows, cols), F32)] * 4,
        compiler_params=_params("parallel", "parallel"),
    )(chip, sums, recv, w, m, v)


def _sum8_adamw(parts, w, m, v):
    rows = w.shape[0]

    def body(p_ref, w_ref, m_ref, v_ref, g_out, d_out, m_out, v_out):
        g = p_ref[0]
        for d in range(1, N_DEV):
            g = g + p_ref[d]
        g_out[...] = g
        d_out[...], m_out[...], v_out[...] = _adamw(w_ref[...], g, m_ref[...], v_ref[...])

    full = _const_spec((rows, 128))
    return pl.pallas_call(
        body, name="small_sum_adamw", grid=(1,), in_specs=[_const_spec((N_DEV, rows, 128)), full, full, full],
        out_specs=[full] * 4, out_shape=[jax.ShapeDtypeStruct((rows, 128), F32)] * 4,
        compiler_params=_params("arbitrary"),
    )(parts, w, m, v)


def _plain_adamw(g, w, m, v):
    rows = w.shape[0]

    def body(g_ref, w_ref, m_ref, v_ref, d_out, m_out, v_out):
        d_out[...], m_out[...], v_out[...] = _adamw(w_ref[...], g_ref[...], m_ref[...], v_ref[...])

    full = _const_spec((rows, 128))
    return pl.pallas_call(
        body, name="shard_adamw", grid=(1,), in_specs=[full] * 4, out_specs=[full] * 3,
        out_shape=[jax.ShapeDtypeStruct((rows, 128), F32)] * 3, compiler_params=_params("arbitrary"),
    )(g, w, m, v)


def _pack(arrays, rows):
    flat = jnp.concatenate([a.reshape(-1) for a in arrays])
    return jnp.pad(flat, (0, rows * 128 - flat.shape[0])).reshape(rows, 128)


def _unpack(packed, shapes):
    flat = packed.reshape(-1)
    out, at = [], 0
    for s in shapes:
        size = 1
        for d in s:
            size *= d
        out.append(flat[at:at + size].reshape(s))
        at += size
    return out


MM_TILE = 512
N_MM_TILES = SEQ // MM_TILE
CAT_TILE = 512
N_CAT_TILES = N_CAT // CAT_TILE
SMALL_ROWS = 808
SHARD_ROWS = 32


def kernel(x, g_mix, w_in, b_gate, w_gk_up, b_gk, w_pool_grp, pool_scale, g_gla_head, w_pool_proj, w_gla_proj, w_out, g_ffn, w_up, w_conv, b_conv, w_down, g_final, loss_target, m_g_mix, m_w_in, m_b_gate, m_w_gk_up, m_b_gk, m_w_pool_grp, m_pool_scale, m_g_gla_head, m_w_pool_proj, m_w_gla_proj, m_w_out, m_g_ffn, m_w_up, m_w_conv, m_b_conv, m_w_down, m_g_final, v_g_mix, v_w_in, v_b_gate, v_w_gk_up, v_b_gk, v_w_pool_grp, v_pool_scale, v_g_gla_head, v_w_pool_proj, v_w_gla_proj, v_w_out, v_g_ffn, v_w_up, v_w_conv, v_b_conv, v_w_down, v_g_final):
    xi, yi, ci = lax.axis_index("x"), lax.axis_index("y"), lax.axis_index("c")
    me = 4 * xi + 2 * yi + ci
    core = jnp.reshape(ci, (1,)).astype(jnp.int32)
    chip = jnp.reshape(2 * xi + yi, (1,)).astype(jnp.int32)
    xs, target = x[0], loss_target[0]

    big = dict(w_in=w_in[0].T, w_pool_proj=w_pool_proj[0], w_gla_proj=w_gla_proj[0], w_out=w_out[0], w_up=w_up[0].T,
               w_down=w_down[0])
    moments = dict(w_in=(m_w_in[0].T, v_w_in[0].T), w_pool_proj=(m_w_pool_proj[0], v_w_pool_proj[0]),
                   w_gla_proj=(m_w_gla_proj[0], v_w_gla_proj[0]), w_out=(m_w_out[0], v_w_out[0]),
                   w_up=(m_w_up[0].T, v_w_up[0].T), w_down=(m_w_down[0], v_w_down[0]))
    names = list(big)
    gathered = _all_gather([big[k].astype(BF) for k in names] + [w_gk_up[0], w_conv[0]], "gather_weights")
    wg = dict(zip(names, gathered[:6]))
    wgk_all, wconv_all = gathered[6], gathered[7]
    wt_in = wg["w_in"].reshape(IN_TOTAL, D_MODEL)
    wt_cat = jnp.concatenate([wt_in[R_QKV:R_OG], wt_in[R_GATE:], wt_in[R_OG:R_GK], wt_in[R_POOL:R_QKV]], axis=0)
    wt_gk = jnp.pad(wt_in[R_GK:R_GATE], ((0, GK_PAD - GATE_RANK), (0, 0)))
    wpp = wg["w_pool_proj"].transpose(1, 0, 2).reshape(POOL_WIDTH, D_MODEL)
    wgp = wg["w_gla_proj"].reshape(D_MODEL, D_MODEL)
    wout = wg["w_out"].reshape(D_MODEL, D_MODEL)
    wt_up = wg["w_up"].reshape(2 * D_FF, D_MODEL)
    wdown = wg["w_down"].reshape(D_FF, D_MODEL)
    wgk_pad = jnp.pad(wgk_all.transpose(1, 0, 2).reshape(GATE_RANK, GLA_DK), ((0, GK_PAD - GATE_RANK), (0, 0)))
    wconv4 = wconv_all.reshape(2, 4, 3, FF_BLK)
    bconv4 = b_conv.reshape(2, 4, 1, FF_BLK)

    tok = lambda i, j, k: (i, 0)
    whole = lambda i, j, k: (0, 0)
    kblk = lambda i, j, k: (k, 0)
    ff_tile = (None, None, MM_TILE, FF_BLK)
    ff_seq = (None, None, SEQ, FF_BLK)

    h = _rms_fwd(xs, g_mix, "rms_mix")
    zcat = _mm(h, wt_cat, out_shape=(SEQ, N_CAT), out_dtype=F32, grid=(N_CAT_TILES, 1, 1),
               blk_a=(SEQ, D_MODEL), blk_b=(CAT_TILE, D_MODEL), blk_o=(SEQ, CAT_TILE),
               map_a=whole, map_b=lambda j, i, k: (j, 0), map_o=lambda j, i, k: (0, j), tb=True, name="mm_in")
    ps = _pool_fwd(zcat, w_pool_grp[0], pool_scale)
    y_pool = _mm(ps, wpp, out_shape=(SEQ, D_MODEL), out_dtype=F32, grid=(N_MM_TILES, 1, 1),
                 blk_a=(MM_TILE, POOL_WIDTH), blk_b=(POOL_WIDTH, D_MODEL), blk_o=(MM_TILE, D_MODEL),
                 map_a=tok, map_b=whole, map_o=tok, name="mm_pool_proj")
    la = _gk_fwd(h, wt_gk, wgk_pad, b_gk)
    o, states = _gla_fwd(zcat, la)
    og = _post_gla_fwd(o, zcat, g_gla_head)
    sq = dict(out_shape=(SEQ, D_MODEL), grid=(N_MM_TILES, 1, 1), blk_a=(MM_TILE, D_MODEL), blk_b=(D_MODEL, D_MODEL),
              blk_o=(MM_TILE, D_MODEL), map_a=tok, map_b=whole, map_o=tok)
    y_gla = _mm(og, wgp, out_dtype=F32, name="mm_gla_proj", **sq)
    mixed = _mix_fwd(zcat, b_gate, y_pool, y_gla)
    x1 = _mm(mixed, wout, out_dtype=F32, res=xs, name="mm_out", **sq)
    h2 = _rms_fwd(x1, g_ffn, "rms_ffn")
    blk4 = lambda b, i, k: (b // 4, b % 4, 0, 0)
    u4 = _mm(h2, wt_up, out_shape=(2, 4, SEQ, FF_BLK), out_dtype=F32, grid=(N_DEV, 1, 1),
             blk_a=(SEQ, D_MODEL), blk_b=(FF_BLK, D_MODEL), blk_o=ff_seq,
             map_a=whole, map_b=lambda b, i, k: (b, 0), map_o=blk4, tb=True, name="mm_up")
    act = _conv_fwd(u4, wconv4, bconv4)
    x2 = _mm(act, wdown, out_shape=(SEQ, D_MODEL), out_dtype=F32, grid=(N_MM_TILES, 1, 4),
             blk_a=ff_tile, blk_b=(FF_BLK, D_MODEL), blk_o=(MM_TILE, D_MODEL),
             map_a=lambda i, j, k: (0, k, i, 0), map_b=kblk, map_o=tok, res=x1, name="mm_down")
    loss_part, dx2, dg_final = _final_loss(x2, g_final.reshape(1, D_MODEL), target)

    da = _mm(dx2, wdown, out_shape=(1, 4, SEQ, FF_BLK), out_dtype=BF, grid=(4, N_MM_TILES, 1),
             blk_a=(MM_TILE, D_MODEL), blk_b=(FF_BLK, D_MODEL), blk_o=ff_tile,
             map_a=lambda b, i, k: (i, 0), map_b=lambda b, i, k: (b, 0), map_o=lambda b, i, k: (0, b, i, 0),
             tb=True, name="mm_d_act")
    d_wdown = _mm(act, dx2, out_shape=(D_FF, D_MODEL), out_dtype=BF, grid=(4, 1, N_MM_TILES),
                  blk_a=ff_tile, blk_b=(MM_TILE, D_MODEL), blk_o=(FF_BLK, D_MODEL),
                  map_a=lambda b, j, k: (0, b, k, 0), map_b=kblk, map_o=lambda b, j, k: (b, 0),
                  ta=True, name="mm_d_wdown")
    du4, d_wconv, d_bconv = _conv_bwd(u4, da, wconv4, bconv4)
    dh2 = _mm(du4, wt_up, out_shape=(SEQ, D_MODEL), out_dtype=F32, grid=(1, 1, N_DEV),
              blk_a=ff_seq, blk_b=(FF_BLK, D_MODEL), blk_o=(SEQ, D_MODEL),
              map_a=lambda i, j, k: (k // 4, k % 4, 0, 0), map_b=kblk, map_o=whole, name="mm_d_h2")
    d_wt_up = _mm(du4, h2, out_shape=(2 * D_FF, D_MODEL), out_dtype=BF, grid=(N_DEV, 1, 1),
                  blk_a=ff_seq, blk_b=(SEQ, D_MODEL), blk_o=(FF_BLK, D_MODEL),
                  map_a=blk4, map_b=whole, map_o=lambda b, i, k: (b, 0), ta=True, name="mm_d_wup")
    dx1, dg_ffn = _rms_bwd(dh2, x1, g_ffn, dx2, "rms_ffn_bwd")

    sq_t = dict(out_shape=(D_MODEL, D_MODEL), grid=(1, 1, N_MM_TILES), blk_a=(MM_TILE, D_MODEL),
                blk_b=(MM_TILE, D_MODEL), blk_o=(D_MODEL, D_MODEL), map_a=kblk, map_b=kblk, map_o=whole, ta=True)
    dmixed = _mm(dx1, wout, out_dtype=F32, tb=True, name="mm_d_mixed", **sq)
    d_wout = _mm(mixed, dx1, out_dtype=BF, name="mm_d_wout", **sq_t)
    dzcat, dy_pool, dy_gla, db_gate = _mix_bwd(dmixed, zcat, b_gate, y_pool, y_gla)
    d_og = _mm(dy_gla, wgp, out_dtype=F32, tb=True, name="mm_d_og", **sq)
    d_wgp = _mm(og, dy_gla, out_dtype=BF, name="mm_d_wgp", **sq_t)
    dzcat, d_o, dg_head = _post_gla_bwd(dzcat, d_og, o, zcat, g_gla_head)
    dzcat, dla = _gla_bwd(dzcat, zcat, la, d_o, states)
    dh_gk, d_wt_gk, d_wgk, db_gk = _gk_bwd(dla, h, wt_gk, wgk_pad, b_gk)
    dps = _mm(dy_pool, wpp, out_shape=(SEQ, POOL_WIDTH), out_dtype=F32, grid=(N_MM_TILES, 1, 1),
              blk_a=(MM_TILE, D_MODEL), blk_b=(POOL_WIDTH, D_MODEL), blk_o=(MM_TILE, POOL_WIDTH),
              map_a=tok, map_b=whole, map_o=tok, tb=True, name="mm_d_ps")
    d_wpp = _mm(ps, dy_pool, out_shape=(POOL_WIDTH, D_MODEL), out_dtype=F32, grid=(1, 1, N_MM_TILES),
                blk_a=(MM_TILE, POOL_WIDTH), blk_b=(MM_TILE, D_MODEL), blk_o=(POOL_WIDTH, D_MODEL),
                map_a=kblk, map_b=kblk, map_o=whole, ta=True, name="mm_d_wpp")
    dzcat, d_wgrp, d_scale = _pool_bwd(dzcat, zcat, dps, w_pool_grp[0], pool_scale)
    dh = _mm(dzcat, wt_cat, out_shape=(SEQ, D_MODEL), out_dtype=F32, grid=(1, 1, N_CAT_TILES),
             blk_a=(SEQ, CAT_TILE), blk_b=(CAT_TILE, D_MODEL), blk_o=(SEQ, D_MODEL),
             map_a=lambda i, j, k: (0, k), map_b=kblk, map_o=whole, res=dh_gk, name="mm_d_h")
    d_wt_cat = _mm(dzcat, h, out_shape=(N_CAT, D_MODEL), out_dtype=BF, grid=(N_CAT_TILES, 1, 1),
                   blk_a=(SEQ, CAT_TILE), blk_b=(SEQ, D_MODEL), blk_o=(CAT_TILE, D_MODEL),
                   map_a=lambda j, i, k: (0, j), map_b=whole, map_o=lambda j, i, k: (j, 0), ta=True, name="mm_d_wcat")
    grad_x, dg_mix = _rms_bwd(dh, xs, g_mix, dx1, "rms_mix_bwd")

    d_wt_in = jnp.concatenate([d_wt_cat[C_POOL:], d_wt_cat[C_QKV:C_GATE], d_wt_cat[C_OG:C_POOL],
                               d_wt_gk[:GATE_RANK].astype(BF), d_wt_cat[C_GATE:C_OG]], axis=0)
    parts = dict(
        w_in=d_wt_in.reshape(N_DEV, IN_SHARD, D_MODEL),
        w_pool_proj=d_wpp.reshape(POOL_WIDTH, N_DEV, D_MODEL // N_DEV).transpose(1, 0, 2).astype(BF),
        w_gla_proj=d_wgp.reshape(N_DEV, D_MODEL // N_DEV, D_MODEL),
        w_out=d_wout.reshape(N_DEV, D_MODEL // N_DEV, D_MODEL),
        w_up=d_wt_up.reshape(N_DEV, FF_BLK, D_MODEL),
        w_down=d_wdown.reshape(N_DEV, D_FF // N_DEV, D_MODEL))
    from_sibling = _exchange_sibling([parts[k] for k in names], "grads_to_sibling")
    chip_sums = [_pair_sum(parts[k], r, core, "pair_sum_" + k) for k, r in zip(names, from_sibling)]
    from_chips = _exchange_chips(chip_sums, "grads_to_chips")
    res = {}
    for k, s, r in zip(names, chip_sums, from_chips):
        outs = _chip_sum_adamw(s, r, big[k], moments[k][0], moments[k][1], chip, "adamw_" + k)
        res[k] = [(t.T if k in ("w_in", "w_up") else t)[None] for t in outs]

    small = [("g_mix", dg_mix, g_mix, m_g_mix, v_g_mix), ("b_gate", db_gate, b_gate, m_b_gate, v_b_gate),
             ("w_gk_up", d_wgk[:GATE_RANK], None, None, None), ("b_gk", db_gk, b_gk, m_b_gk, v_b_gk),
             ("w_pool_grp", d_wgrp, w_pool_grp, m_w_pool_grp, v_w_pool_grp),
             ("pool_scale", d_scale, pool_scale, m_pool_scale, v_pool_scale),
             ("g_gla_head", dg_head, g_gla_head, m_g_gla_head, v_g_gla_head), ("g_ffn", dg_ffn, g_ffn, m_g_ffn, v_g_ffn),
             ("w_conv", d_wconv, None, None, None), ("b_conv", d_bconv, b_conv, m_b_conv, v_b_conv),
             ("g_final", dg_final, g_final, m_g_final, v_g_final), ("loss", loss_part, None, None, None)]
    zeros_like_part = lambda t: jnp.zeros(t[1].shape, F32)
    g_all = _all_gather([_pack([t[1] for t in small], SMALL_ROWS)], "gather_small_grads")[0]
    packed = _sum8_adamw(g_all, *[_pack([zeros_like_part(t) if t[i] is None else t[i] for t in small], SMALL_ROWS)
                                  for i in (2, 3, 4)])
    shapes = [t[1].shape if t[2] is None else t[2].shape for t in small]
    unpacked = [_unpack(p, shapes) for p in packed]
    for idx, t in enumerate(small):
        if t[2] is not None:
            res[t[0]] = [unpacked[q][idx] for q in range(4)]
    g_wgk = lax.dynamic_slice(unpacked[0][2], (0, me * (GLA_DK // N_DEV)), (GATE_RANK, GLA_DK // N_DEV))
    g_wconv = lax.dynamic_index_in_dim(unpacked[0][8].reshape(N_DEV, 3, FF_BLK), me, axis=0, keepdims=False)
    shard_shapes = [(1, GATE_RANK, GLA_DK // N_DEV), (1, 3, FF_BLK)]
    shard_out = _plain_adamw(_pack([g_wgk, g_wconv], SHARD_ROWS), _pack([w_gk_up, w_conv], SHARD_ROWS),
                             _pack([m_w_gk_up, m_w_conv], SHARD_ROWS), _pack([v_w_gk_up, v_w_conv], SHARD_ROWS))
    shard_un = [_unpack(p, shard_shapes) for p in shard_out]
    res["w_gk_up"] = [g_wgk[None]] + [s[0] for s in shard_un]
    res["w_conv"] = [g_wconv[None]] + [s[1] for s in shard_un]

    loss = unpacked[0][11][0, 0]
    order =["g_mix", "w_in", "b_gate", "w_gk_up", "b_gk", "w_pool_grp", "pool_scale", "g_gla_head", "w_pool_proj",
             "w_gla_proj", "w_out", "g_ffn", "w_up", "w_conv", "b_conv", "w_down", "g_final"]
    return (loss, grad_x[None], *[res[k][0] for k in order], *[res[k][1] for k in order],
            *[res[k][2] for k in order], *[res[k][3] for k in order])
```

```python
import functools

import jax
import jax.numpy as jnp
from jax import lax
from jax.experimental import pallas as pl
from jax.experimental.pallas import tpu as pltpu

F32 = jnp.float32
BF = jnp.bfloat16
HIGHEST = lax.Precision.HIGHEST
MESH = pl.DeviceIdType.MESH

N_DEV = 8
SEQ = 2048
D_MODEL = 1024
CHUNK = 64
EPS = 1e-6
POOL_WIDTH = 512
POOL_WINDOWS = (2, 4, 8, 16)
POOL_GD = 128
POOL_HALO = 16
HEADS = 4
HK = 128
HV = 256
GLA_DK = 512
GATE_RANK = 16
GATE_NORM = 16.0
D_FF = 2816
FF_BLK = 704
IN_TOTAL = 5648
IN_SHARD = 706
C_QKV, C_GATE, C_OG, C_POOL = 0, 2048, 4096, 5120
N_CAT = 5632
R_POOL, R_QKV, R_OG, R_GK, R_GATE = 0, 512, 2560, 3584, 3600
GK_PAD = 128

ADAM_LR, ADAM_B1, ADAM_B2, ADAM_EPS, ADAM_WD, ADAM_STEP = 0.001, 0.9, 0.999, 1e-08, 0.01, 10
ADAM_C1 = 1.0 - ADAM_B1 ** ADAM_STEP
ADAM_C2 = 1.0 - ADAM_B2 ** ADAM_STEP

VMEM_BYTES_V7X = 64 * 1024 * 1024
VMEM_LIMIT = 48 * 1024 * 1024

TOK_TILE = 256
HALO = 8
GLA_CPS = 4


def _params(*sem):
    return pltpu.CompilerParams(dimension_semantics=sem, vmem_limit_bytes=VMEM_LIMIT)


def _const_spec(shape):
    nd = len(shape)
    return pl.BlockSpec(shape, lambda *_: (0,) * nd)


def _dot(a, b, ta=False, tb=False):
    dims = (((0 if ta else 1,), (1 if tb else 0,)), ((), ()))
    return lax.dot_general(a.astype(BF), b.astype(BF), dims, preferred_element_type=F32)


def _dot_exact(a, b):
    return jnp.dot(a, b, precision=HIGHEST, preferred_element_type=F32)


def _sigmoid(x):
    return 1.0 / (1.0 + jnp.exp(-x))


def _mm(a, b, *, out_shape, out_dtype, grid, blk_a, blk_b, blk_o, map_a, map_b, map_o, ta=False, tb=False,
        res=None, name):
    gk = grid[2]

    def body(*refs):
        if res is None:
            a_ref, b_ref, o_ref = refs[:3]
            r_ref = None
            scr = refs[3:]
        else:
            a_ref, b_ref, r_ref, o_ref = refs[:4]
            scr = refs[4:]
        prod = _dot(a_ref[...], b_ref[...], ta, tb)

        def finish(total):
            if r_ref is not None:
                total = total + r_ref[...]
            o_ref[...] = total.astype(out_dtype)

        if gk == 1:
            finish(prod)
        else:
            acc = scr[0]
            k = pl.program_id(2)

            @pl.when(k == 0)
            def _():
                acc[...] = prod

            @pl.when(k > 0)
            def _():
                acc[...] += prod

            @pl.when(k == gk - 1)
            def _():
                finish(acc[...])

    in_specs = [pl.BlockSpec(blk_a, map_a), pl.BlockSpec(blk_b, map_b)]
    args = [a, b]
    if res is not None:
        in_specs.append(pl.BlockSpec(blk_o, map_o))
        args.append(res)
    return pl.pallas_call(
        body, name=name, grid=grid, in_specs=in_specs, out_specs=pl.BlockSpec(blk_o, map_o),
        out_shape=jax.ShapeDtypeStruct(out_shape, out_dtype),
        scratch_shapes=[] if gk == 1 else [pltpu.VMEM(tuple(d for d in blk_o if d is not None), F32)],
        compiler_params=_params("parallel", "parallel", "arbitrary"),
    )(*args)


def _rms_fwd(x, g, name):
    def body(x_ref, g_ref, o_ref):
        xv = x_ref[...]
        r = lax.rsqrt(jnp.mean(xv * xv, axis=-1, keepdims=True) + EPS)
        o_ref[...] = (xv * r * g_ref[...]).astype(BF)

    tile = pl.BlockSpec((TOK_TILE, D_MODEL), lambda i: (i, 0))
    return pl.pallas_call(
        body, name=name, grid=(SEQ // TOK_TILE,), in_specs=[tile, _const_spec((1, D_MODEL))], out_specs=tile,
        out_shape=jax.ShapeDtypeStruct((SEQ, D_MODEL), BF), compiler_params=_params("parallel"),
    )(x, g)


def _rms_bwd(dy, x, g, dres, name):
    def body(dy_ref, x_ref, g_ref, dres_ref, dx_ref, dg_ref):
        xv = x_ref[...]
        r = lax.rsqrt(jnp.mean(xv * xv, axis=-1, keepdims=True) + EPS)
        xn = xv * r
        dyv = dy_ref[...]
        dxn = dyv * g_ref[...]
        dx_ref[...] = dres_ref[...] + r * (dxn - xn * jnp.mean(dxn * xn, axis=-1, keepdims=True))
        part = jnp.sum(dyv * xn, axis=0, keepdims=True)

        @pl.when(pl.program_id(0) == 0)
        def _():
            dg_ref[...] = part

        @pl.when(pl.program_id(0) > 0)
        def _():
            dg_ref[...] += part

    tile = pl.BlockSpec((TOK_TILE, D_MODEL), lambda i: (i, 0))
    vec = _const_spec((1, D_MODEL))
    return pl.pallas_call(
        body, name=name, grid=(SEQ // TOK_TILE,), in_specs=[tile, tile, vec, tile], out_specs=[tile, vec],
        out_shape=[jax.ShapeDtypeStruct((SEQ, D_MODEL), F32), jax.ShapeDtypeStruct((1, D_MODEL), F32)],
        compiler_params=_params("arbitrary"),
    )(dy, x, g, dres)


def _final_loss(x2, g, target):
    def body(x_ref, g_ref, t_ref, loss_ref, dx_ref, dg_ref):
        xv = x_ref[...]
        r = lax.rsqrt(jnp.mean(xv * xv, axis=-1, keepdims=True) + EPS)
        xn = xv * r
        gv = g_ref[...]
        err = xn * gv - t_ref[...]
        lpart = jnp.full((1, 128), 0.5 * jnp.sum(jnp.mean(err * err, axis=-1, keepdims=True)), F32)
        dyv = err * (1.0 / D_MODEL)
        dxn = dyv * gv
        dx_ref[...] = r * (dxn - xn * jnp.mean(dxn * xn, axis=-1, keepdims=True))
        gpart = jnp.sum(dyv * xn, axis=0, keepdims=True)

        @pl.when(pl.program_id(0) == 0)
        def _():
            loss_ref[...] = lpart
            dg_ref[...] = gpart

        @pl.when(pl.program_id(0) > 0)
        def _():
            loss_ref[...] += lpart
            dg_ref[...] += gpart

    tile = pl.BlockSpec((TOK_TILE, D_MODEL), lambda i: (i, 0))
    vec = _const_spec((1, D_MODEL))
    return pl.pallas_call(
        body, name="final_loss", grid=(SEQ // TOK_TILE,), in_specs=[tile, vec, tile],
        out_specs=[_const_spec((1, 128)), tile, vec],
        out_shape=[jax.ShapeDtypeStruct((1, 128), F32), jax.ShapeDtypeStruct((SEQ, D_MODEL), F32),
                   jax.ShapeDtypeStruct((1, D_MODEL), F32)],
        compiler_params=_params("arbitrary"),
    )(x2, g, target)


def _pool_counts(w):
    pos = lax.broadcasted_iota(jnp.int32, (SEQ, 1), 0).astype(F32)
    return jnp.minimum(pos + 1.0, float(w))


def _pool_window(u, w, ext):
    ext[pl.ds(POOL_HALO, SEQ), :] = u
    win = u
    for j in range(1, w):
        win = win + ext[pl.ds(POOL_HALO - j, SEQ), :]
    return win / _pool_counts(w) - u


def _pool_fwd(zcat, w_grp, scale):
    def body(z_ref, w_ref, s_ref, o_ref, ext):
        ext[pl.ds(0, POOL_HALO), :] = jnp.zeros((POOL_HALO, POOL_GD), F32)
        for g, w in enumerate(POOL_WINDOWS):
            cols = slice(g * POOL_GD, (g + 1) * POOL_GD)
            p = _pool_window(z_ref[:, cols], w, ext)
            o_ref[:, cols] = (_dot(p, w_ref[g]) * s_ref[:, cols]).astype(BF)

    return pl.pallas_call(
        body, name="pool_fwd", grid=(1,),
        in_specs=[pl.BlockSpec((SEQ, POOL_WIDTH), lambda i: (0, C_POOL // POOL_WIDTH)),
                  _const_spec((4, POOL_GD, POOL_GD)), _const_spec((1, POOL_WIDTH))],
        out_specs=_const_spec((SEQ, POOL_WIDTH)), out_shape=jax.ShapeDtypeStruct((SEQ, POOL_WIDTH), BF),
        scratch_shapes=[pltpu.VMEM((POOL_HALO + SEQ, POOL_GD), F32)], compiler_params=_params("arbitrary"),
    )(zcat, w_grp, scale)


def _pool_bwd(dzcat, zcat, dps, w_grp, scale):
    def body(dz_in, z_ref, dps_ref, w_ref, s_ref, dz_ref, dw_ref, dsc_ref, ext, ext2):
        del dz_in
        ext[pl.ds(0, POOL_HALO), :] = jnp.zeros((POOL_HALO, POOL_GD), F32)
        ext2[pl.ds(SEQ, POOL_HALO), :] = jnp.zeros((POOL_HALO, POOL_GD), F32)
        for g, w in enumerate(POOL_WINDOWS):
            cols = slice(g * POOL_GD, (g + 1) * POOL_GD)
            p = _pool_window(z_ref[:, cols], w, ext)
            wg = w_ref[g]
            pg = _dot(p, wg)
            dpsv = dps_ref[:, cols]
            dsc_ref[:, cols] = jnp.sum(dpsv * pg, axis=0, keepdims=True)
            dpg = dpsv * s_ref[:, cols]
            dw_ref[g] = _dot(p, dpg, ta=True)
            dp = _dot(dpg, wg, tb=True)
            dpc = dp / _pool_counts(w)
            ext2[pl.ds(0, SEQ), :] = dpc
            du = dpc
            for j in range(1, w):
                du = du + ext2[pl.ds(j, SEQ), :]
            dz_ref[:, cols] = (du - dp).astype(BF)

    return pl.pallas_call(
        body, name="pool_bwd", grid=(1,),
        in_specs=[pl.BlockSpec(memory_space=pl.ANY),
                  pl.BlockSpec((SEQ, POOL_WIDTH), lambda i: (0, C_POOL // POOL_WIDTH)),
                  _const_spec((SEQ, POOL_WIDTH)), _const_spec((4, POOL_GD, POOL_GD)), _const_spec((1, POOL_WIDTH))],
        out_specs=[pl.BlockSpec((SEQ, POOL_WIDTH), lambda i: (0, C_POOL // POOL_WIDTH)),
                   _const_spec((4, POOL_GD, POOL_GD)), _const_spec((1, POOL_WIDTH))],
        out_shape=[jax.ShapeDtypeStruct((SEQ, N_CAT), BF), jax.ShapeDtypeStruct((4, POOL_GD, POOL_GD), F32),
                   jax.ShapeDtypeStruct((1, POOL_WIDTH), F32)],
        scratch_shapes=[pltpu.VMEM((POOL_HALO + SEQ, POOL_GD), F32), pltpu.VMEM((SEQ + POOL_HALO, POOL_GD), F32)],
        input_output_aliases={0: 0}, compiler_params=_params("arbitrary"),
    )(dzcat, zcat, dps, w_grp, scale)


GK_TILE = 512


def _gk_fwd(h, wt_gk, wgk_pad, b_gk):
    def body(h_ref, wt_ref, w_ref, b_ref, la_ref):
        z_gk = _dot(h_ref[...], wt_ref[...], tb=True)
        pre = _dot(z_gk, w_ref[...]) + b_ref[...]
        la_ref[...] = (jnp.minimum(pre, 0.0) - jnp.log(1.0 + jnp.exp(-jnp.abs(pre)))) * (1.0 / GATE_NORM)

    return pl.pallas_call(
        body, name="gk_fwd", grid=(SEQ // GK_TILE,),
        in_specs=[pl.BlockSpec((GK_TILE, D_MODEL), lambda i: (i, 0)), _const_spec((GK_PAD, D_MODEL)),
                  _const_spec((GK_PAD, GLA_DK)), _const_spec((1, GLA_DK))],
        out_specs=pl.BlockSpec((GK_TILE, GLA_DK), lambda i: (i, 0)),
        out_shape=jax.ShapeDtypeStruct((SEQ, GLA_DK), F32), compiler_params=_params("parallel"),
    )(h, wt_gk, wgk_pad, b_gk)


def _gk_bwd(dla, h, wt_gk, wgk_pad, b_gk):
    def body(dla_ref, h_ref, wt_ref, w_ref, b_ref, dh_ref, dwt_ref, dw_ref, db_ref):
        hv = h_ref[...]
        wtv = wt_ref[...]
        wv = w_ref[...]
        z_gk = _dot(hv, wtv, tb=True)
        pre = _dot(z_gk, wv) + b_ref[...]
        dpre = dla_ref[...] * (1.0 / GATE_NORM) * (1.0 - _sigmoid(pre))
        dz_gk = _dot(dpre, wv, tb=True)
        dh_ref[...] = _dot(dz_gk, wtv)
        dwtp = _dot(dz_gk, hv, ta=True)
        dwp = _dot(z_gk, dpre, ta=True)
        dbp = jnp.sum(dpre, axis=0, keepdims=True)

        @pl.when(pl.program_id(0) == 0)
        def _():
            dwt_ref[...] = dwtp
            dw_ref[...] = dwp
            db_ref[...] = dbp

        @pl.when(pl.program_id(0) > 0)
        def _():
            dwt_ref[...] += dwtp
            dw_ref[...] += dwp
            db_ref[...] += dbp

    tile = pl.BlockSpec((GK_TILE, D_MODEL), lambda i: (i, 0))
    return pl.pallas_call(
        body, name="gk_bwd", grid=(SEQ // GK_TILE,),
        in_specs=[pl.BlockSpec((GK_TILE, GLA_DK), lambda i: (i, 0)), tile, _const_spec((GK_PAD, D_MODEL)),
                  _const_spec((GK_PAD, GLA_DK)), _const_spec((1, GLA_DK))],
        out_specs=[tile, _const_spec((GK_PAD, D_MODEL)), _const_spec((GK_PAD, GLA_DK)), _const_spec((1, GLA_DK))],
        out_shape=[jax.ShapeDtypeStruct((SEQ, D_MODEL), F32), jax.ShapeDtypeStruct((GK_PAD, D_MODEL), F32),
                   jax.ShapeDtypeStruct((GK_PAD, GLA_DK), F32), jax.ShapeDtypeStruct((1, GLA_DK), F32)],
        compiler_params=_params("arbitrary"),
    )(dla, h, wt_gk, wgk_pad, b_gk)


GLA_ROWS = GLA_CPS * CHUNK
GLA_STEPS = SEQ // GLA_ROWS
QKV_W = 2048


def _gla_chunk(qkv_ref, la_ref, rows, h):
    tri = lax.broadcasted_iota(jnp.int32, (CHUNK, CHUNK), 0) >= lax.broadcasted_iota(jnp.int32, (CHUNK, CHUNK), 1)
    q = qkv_ref[rows, h * HK:(h + 1) * HK] * (HK ** -0.5)
    k = qkv_ref[rows, GLA_DK + h * HK:GLA_DK + (h + 1) * HK]
    v = qkv_ref[rows, 2 * GLA_DK + h * HV:2 * GLA_DK + (h + 1) * HV]
    la = la_ref[rows, h * HK:(h + 1) * HK]
    bc = _dot_exact(tri.astype(F32), la)
    e_pos, e_neg = jnp.exp(bc), jnp.exp(-bc)
    dl = jnp.exp(jnp.sum(la, axis=0, keepdims=True))
    q_fw, q_bw, k_fw, k_bw = q * e_pos, q * e_neg, k * e_neg, k * e_pos
    scores = jnp.where(tri, _dot(q_fw, k_fw, tb=True), _dot(q_bw, k_bw, tb=True))
    return tri, v, e_pos, e_neg, dl, q_fw, q_bw, k_fw, k_bw, scores


def _gla_fwd(zcat, la):
    def body(qkv_ref, la_ref, o_ref, st_ref, state):
        @pl.when(pl.program_id(0) == 0)
        def _():
            state[...] = jnp.zeros_like(state)

        for c in range(GLA_CPS):
            rows = slice(c * CHUNK, (c + 1) * CHUNK)
            for h in range(HEADS):
                _, v, _, _, dl, q_fw, _, k_fw, _, scores = _gla_chunk(qkv_ref, la_ref, rows, h)
                st = state[h]
                st_ref[c, h] = st
                o_ref[rows, h * HV:(h + 1) * HV] = _dot(scores, v) + _dot(q_fw, st, tb=True)
                state[h] = st * dl + _dot(v, k_fw * dl, ta=True)

    return pl.pallas_call(
        body, name="gla_fwd", grid=(GLA_STEPS,),
        in_specs=[pl.BlockSpec((GLA_ROWS, QKV_W), lambda i: (i, 0)), pl.BlockSpec((GLA_ROWS, GLA_DK), lambda i: (i, 0))],
        out_specs=[pl.BlockSpec((GLA_ROWS, D_MODEL), lambda i: (i, 0)),
                   pl.BlockSpec((GLA_CPS, HEADS, HV, HK), lambda i: (i, 0, 0, 0))],
        out_shape=[jax.ShapeDtypeStruct((SEQ, D_MODEL), F32),
                   jax.ShapeDtypeStruct((SEQ // CHUNK, HEADS, HV, HK), F32)],
        scratch_shapes=[pltpu.VMEM((HEADS, HV, HK), F32)], compiler_params=_params("arbitrary"),
    )(zcat, la)


def _gla_bwd(dzcat, zcat, la, d_o, states):
    def body(dz_in, qkv_ref, la_ref, do_ref, st_ref, dqkv_ref, dla_ref, dstate):
        del dz_in

        @pl.when(pl.program_id(0) == 0)
        def _():
            dstate[...] = jnp.zeros_like(dstate)

        last_row = lax.broadcasted_iota(jnp.int32, (CHUNK, HK), 0) == CHUNK - 1
        upper = (lax.broadcasted_iota(jnp.int32, (CHUNK, CHUNK), 0)
                 <= lax.broadcasted_iota(jnp.int32, (CHUNK, CHUNK), 1)).astype(F32)
        for c in reversed(range(GLA_CPS)):
            rows = slice(c * CHUNK, (c + 1) * CHUNK)
            for h in range(HEADS):
                tri, v, e_pos, e_neg, dl, q_fw, q_bw, k_fw, k_bw, scores = _gla_chunk(qkv_ref, la_ref, rows, h)
                st = st_ref[c, h]
                dst = dstate[h]
                d_out = do_ref[rows, h * HV:(h + 1) * HV]
                k_dec = k_fw * dl
                dp = _dot(d_out, v, tb=True)
                dp_fw = jnp.where(tri, dp, 0.0)
                dp_bw = jnp.where(tri, 0.0, dp)
                dv = _dot(scores, d_out, ta=True) + _dot(k_dec, dst, tb=True)
                dk_dec = _dot(v, dst)
                dq_fw = _dot(dp_fw, k_fw) + _dot(d_out, st)
                dk_fw = _dot(dp_fw, q_fw, ta=True) + dk_dec * dl
                dq_bw = _dot(dp_bw, k_bw)
                dk_bw = _dot(dp_bw, q_bw, ta=True)
                ddl = jnp.sum(st * dst, axis=0, keepdims=True) + jnp.sum(k_fw * dk_dec, axis=0, keepdims=True)
                dstate[h] = dst * dl + _dot(d_out, q_fw, ta=True)
                dq = (dq_fw * e_pos + dq_bw * e_neg) * (HK ** -0.5)
                dk = dk_fw * e_neg + dk_bw * e_pos
                db = dq_fw * q_fw - dk_fw * k_fw - dq_bw * q_bw + dk_bw * k_bw + jnp.where(last_row, ddl * dl, 0.0)
                dla_ref[rows, h * HK:(h + 1) * HK] = _dot_exact(upper, db)
                dqkv_ref[rows, h * HK:(h + 1) * HK] = dq.astype(BF)
                dqkv_ref[rows, GLA_DK + h * HK:GLA_DK + (h + 1) * HK] = dk.astype(BF)
                dqkv_ref[rows, 2 * GLA_DK + h * HV:2 * GLA_DK + (h + 1) * HV] = dv.astype(BF)

    rev = lambda i: (GLA_STEPS - 1 - i, 0)
    return pl.pallas_call(
        body, name="gla_bwd", grid=(GLA_STEPS,),
        in_specs=[pl.BlockSpec(memory_space=pl.ANY), pl.BlockSpec((GLA_ROWS, QKV_W), rev),
                  pl.BlockSpec((GLA_ROWS, GLA_DK), rev), pl.BlockSpec((GLA_ROWS, D_MODEL), rev),
                  pl.BlockSpec((GLA_CPS, HEADS, HV, HK), lambda i: (GLA_STEPS - 1 - i, 0, 0, 0))],
        out_specs=[pl.BlockSpec((GLA_ROWS, QKV_W), rev), pl.BlockSpec((GLA_ROWS, GLA_DK), rev)],
        out_shape=[jax.ShapeDtypeStruct((SEQ, N_CAT), BF), jax.ShapeDtypeStruct((SEQ, GLA_DK), F32)],
        scratch_shapes=[pltpu.VMEM((HEADS, HV, HK), F32)], input_output_aliases={0: 0},
        compiler_params=_params("arbitrary"),
    )(dzcat, zcat, la, d_o, states)


def _silu_parts(x):
    s = _sigmoid(x)
    return x * s, s * (1.0 + x * (1.0 - s))


def _post_gla_fwd(o, zcat, g_head):
    def body(o_ref, zog_ref, g_ref, out_ref):
        for h in range(HEADS):
            cols = slice(h * HV, (h + 1) * HV)
            ov = o_ref[:, cols]
            r = lax.rsqrt(jnp.mean(ov * ov, axis=-1, keepdims=True) + EPS)
            act, _ = _silu_parts(zog_ref[:, cols])
            out_ref[:, cols] = (ov * r * g_ref[...] * act).astype(BF)

    tile = pl.BlockSpec((TOK_TILE, D_MODEL), lambda i: (i, 0))
    return pl.pallas_call(
        body, name="post_gla_fwd", grid=(SEQ // TOK_TILE,),
        in_specs=[tile, pl.BlockSpec((TOK_TILE, D_MODEL), lambda i: (i, C_OG // D_MODEL)), _const_spec((1, HV))],
        out_specs=tile, out_shape=jax.ShapeDtypeStruct((SEQ, D_MODEL), BF), compiler_params=_params("parallel"),
    )(o, zcat, g_head)


def _post_gla_bwd(dzcat, d_og, o, zcat, g_head):
    def body(dz_in, dog_ref, o_ref, zog_ref, g_ref, dz_ref, do_ref, dg_ref):
        del dz_in
        gpart = jnp.zeros((1, HV), F32)
        gv = g_ref[...]
        for h in range(HEADS):
            cols = slice(h * HV, (h + 1) * HV)
            ov = o_ref[:, cols]
            r = lax.rsqrt(jnp.mean(ov * ov, axis=-1, keepdims=True) + EPS)
            on = ov * r
            act, dact = _silu_parts(zog_ref[:, cols])
            dogv = dog_ref[:, cols]
            dz_ref[:, cols] = (dogv * on * gv * dact).astype(BF)
            d_on_g = dogv * act
            gpart = gpart + jnp.sum(d_on_g * on, axis=0, keepdims=True)
            dxn = d_on_g * gv
            do_ref[:, cols] = r * (dxn - on * jnp.mean(dxn * on, axis=-1, keepdims=True))

        @pl.when(pl.program_id(0) == 0)
        def _():
            dg_ref[...] = gpart

        @pl.when(pl.program_id(0) > 0)
        def _():
            dg_ref[...] += gpart

    tile = pl.BlockSpec((TOK_TILE, D_MODEL), lambda i: (i, 0))
    ogspec = pl.BlockSpec((TOK_TILE, D_MODEL), lambda i: (i, C_OG // D_MODEL))
    return pl.pallas_call(
        body, name="post_gla_bwd", grid=(SEQ // TOK_TILE,),
        in_specs=[pl.BlockSpec(memory_space=pl.ANY), tile, tile, ogspec, _const_spec((1, HV))],
        out_specs=[ogspec, tile, _const_spec((1, HV))],
        out_shape=[jax.ShapeDtypeStruct((SEQ, N_CAT), BF), jax.ShapeDtypeStruct((SEQ, D_MODEL), F32),
                   jax.ShapeDtypeStruct((1, HV), F32)],
        input_output_aliases={0: 0}, compiler_params=_params("arbitrary"),
    )(dzcat, d_og, o, zcat, g_head)


GATE_W = 2 * D_MODEL


def _mix_fwd(zcat, b_gate, y_pool, y_gla):
    def body(zg_ref, b_ref, yp_ref, yg_ref, out_ref):
        g0 = _sigmoid(zg_ref[:, :D_MODEL] + b_ref[:, :D_MODEL])
        g1 = _sigmoid(zg_ref[:, D_MODEL:] + b_ref[:, D_MODEL:])
        out_ref[...] = (g0 * yp_ref[...] + g1 * yg_ref[...]).astype(BF)

    tile = pl.BlockSpec((TOK_TILE, D_MODEL), lambda i: (i, 0))
    return pl.pallas_call(
        body, name="mix_fwd", grid=(SEQ // TOK_TILE,),
        in_specs=[pl.BlockSpec((TOK_TILE, GATE_W), lambda i: (i, C_GATE // GATE_W)), _const_spec((1, GATE_W)), tile, tile],
        out_specs=tile, out_shape=jax.ShapeDtypeStruct((SEQ, D_MODEL), BF), compiler_params=_params("parallel"),
    )(zcat, b_gate, y_pool, y_gla)


def _mix_bwd(dmixed, zcat, b_gate, y_pool, y_gla):
    def body(dm_ref, zg_ref, b_ref, yp_ref, yg_ref, dz_ref, dyp_ref, dyg_ref, db_ref):
        dm = dm_ref[...]
        g0 = _sigmoid(zg_ref[:, :D_MODEL] + b_ref[:, :D_MODEL])
        g1 = _sigmoid(zg_ref[:, D_MODEL:] + b_ref[:, D_MODEL:])
        dyp_ref[...] = (dm * g0).astype(BF)
        dyg_ref[...] = (dm * g1).astype(BF)
        dz0 = dm * yp_ref[...] * g0 * (1.0 - g0)
        dz1 = dm * yg_ref[...] * g1 * (1.0 - g1)
        dz_ref[:, :D_MODEL] = dz0.astype(BF)
        dz_ref[:, D_MODEL:] = dz1.astype(BF)
        b0 = jnp.sum(dz0, axis=0, keepdims=True)
        b1 = jnp.sum(dz1, axis=0, keepdims=True)

        @pl.when(pl.program_id(0) == 0)
        def _():
            db_ref[:, :D_MODEL] = b0
            db_ref[:, D_MODEL:] = b1

        @pl.when(pl.program_id(0) > 0)
        def _():
            db_ref[:, :D_MODEL] += b0
            db_ref[:, D_MODEL:] += b1

    tile = pl.BlockSpec((TOK_TILE, D_MODEL), lambda i: (i, 0))
    gspec = pl.BlockSpec((TOK_TILE, GATE_W), lambda i: (i, C_GATE // GATE_W))
    return pl.pallas_call(
        body, name="mix_bwd", grid=(SEQ // TOK_TILE,),
        in_specs=[tile, gspec, _const_spec((1, GATE_W)), tile, tile],
        out_specs=[gspec, tile, tile, _const_spec((1, GATE_W))],
        out_shape=[jax.ShapeDtypeStruct((SEQ, N_CAT), BF), jax.ShapeDtypeStruct((SEQ, D_MODEL), BF),
                   jax.ShapeDtypeStruct((SEQ, D_MODEL), BF), jax.ShapeDtypeStruct((1, GATE_W), F32)],
        compiler_params=_params("arbitrary"),
    )(dmixed, zcat, b_gate, y_pool, y_gla)


N_TOK_TILES = SEQ // TOK_TILE
HALO_PER_TILE = TOK_TILE // HALO


LANE_TILES = tuple((lo, min(128, FF_BLK - lo)) for lo in range(0, FF_BLK, 128))


def _taps(w_ref, b_ref, half, lanes, rows):
    shape = (rows, lanes.stop - lanes.start)
    return ([jnp.broadcast_to(w_ref[half, j:j + 1, lanes], shape) for j in range(3)],
            jnp.broadcast_to(b_ref[half, :, lanes], shape))


def _shifted(ext, row, n, lanes):
    return [ext[pl.ds(row - 2 + j, n), lanes] for j in range(3)]


def _conv_strip(u3, taps, bias):
    return bias + u3[0] * taps[0] + u3[1] * taps[1] + u3[2] * taps[2]


def _pair_specs(pairs):
    tile = pl.BlockSpec((pairs, None, TOK_TILE, FF_BLK), lambda b, i: (0, b, i, 0))
    before = pl.BlockSpec((pairs, None, HALO, FF_BLK), lambda b, i: (0, b, jnp.maximum(i * HALO_PER_TILE - 1, 0), 0))
    after = pl.BlockSpec((pairs, None, HALO, FF_BLK),
                         lambda b, i: (0, b, jnp.minimum((i + 1) * HALO_PER_TILE, SEQ // HALO - 1), 0))

    def vec(rows):
        return pl.BlockSpec((2, None, rows, FF_BLK), lambda b, i: (0, b, 0, 0))

    return tile, before, after, vec


def _fill_ext(ext, half, before_ref, tile_ref, after_ref=None):
    i = pl.program_id(1)
    ext[pl.ds(0, HALO), :] = jnp.where(i > 0, before_ref[half], 0.0)
    ext[pl.ds(HALO, TOK_TILE), :] = tile_ref[half]
    if after_ref is not None:
        ext[pl.ds(HALO + TOK_TILE, HALO), :] = after_ref[half]


def _conv_fwd(u, w_conv, b_conv):
    strip = 16

    def body(u_ref, ub_ref, w_ref, b_ref, a_ref, ext_g, ext_v):
        _fill_ext(ext_g, 0, ub_ref, u_ref)
        _fill_ext(ext_v, 1, ub_ref, u_ref)
        for lo, width in LANE_TILES:
            lanes = slice(lo, lo + width)
            taps_g, bias_g = _taps(w_ref, b_ref, 0, lanes, strip)
            taps_v, bias_v = _taps(w_ref, b_ref, 1, lanes, strip)
            for s in range(TOK_TILE // strip):
                cg = _conv_strip(_shifted(ext_g, HALO + s * strip, strip, lanes), taps_g, bias_g)
                cv = _conv_strip(_shifted(ext_v, HALO + s * strip, strip, lanes), taps_v, bias_v)
                a_ref[0, s * strip:(s + 1) * strip, lanes] = (cg * _sigmoid(cg) * cv).astype(BF)

    tile, before, _, vec = _pair_specs(2)
    out_tile, _, _, _ = _pair_specs(1)
    return pl.pallas_call(
        body, name="conv_fwd", grid=(4, N_TOK_TILES), in_specs=[tile, before, vec(3), vec(1)],
        out_specs=out_tile, out_shape=jax.ShapeDtypeStruct((1, 4, SEQ, FF_BLK), BF),
        scratch_shapes=[pltpu.VMEM((HALO + TOK_TILE, FF_BLK), F32)] * 2, compiler_params=_params("parallel", "parallel"),
    )(u, u, w_conv, b_conv)


def _conv_bwd(u, da, w_conv, b_conv):
    strip = 8
    n_strips = TOK_TILE // strip

    def body(u_ref, ub_ref, ua_ref, da_ref, daa_ref, w_ref, b_ref, du_ref, dw_ref, db_ref, ext_g, ext_v, ext_dg, ext_dv):
        i = pl.program_id(1)
        _fill_ext(ext_g, 0, ub_ref, u_ref, ua_ref)
        _fill_ext(ext_v, 1, ub_ref, u_ref, ua_ref)

        @pl.when(i == 0)
        def _():
            dw_ref[...] = jnp.zeros_like(dw_ref)
            db_ref[...] = jnp.zeros_like(db_ref)

        for lo, width in LANE_TILES:
            lanes = slice(lo, lo + width)
            taps = [_taps(w_ref, b_ref, half, lanes, strip) for half in range(2)]
            exts_u, exts_d = (ext_g, ext_v), (ext_dg, ext_dv)
            acc_w = [[jnp.zeros((strip, width), F32) for _ in range(3)] for _ in range(2)]
            acc_b = [jnp.zeros((strip, width), F32) for _ in range(2)]
            da_pair, pending = None, [None, None]
            for s in range(n_strips + 1):
                u3 = [_shifted(exts_u[half], HALO + s * strip, strip, lanes) for half in range(2)]
                cg = _conv_strip(u3[0], *taps[0])
                cv = _conv_strip(u3[1], *taps[1])
                act, dact = _silu_parts(cg)
                if s == n_strips:
                    da = jnp.where(i < N_TOK_TILES - 1, daa_ref[0, :, lanes].astype(F32), 0.0)
                elif s % 2 == 0:
                    da_pair = da_ref[0, s * strip:(s + 2) * strip, lanes].astype(F32)
                    da = da_pair[:strip]
                else:
                    da = da_pair[strip:]
                dc = (da * cv * dact, da * act)
                for half in range(2):
                    exts_d[half][pl.ds(s * strip, strip), lanes] = dc[half]
                    if s < n_strips:
                        for j in range(3):
                            acc_w[half][j] = acc_w[half][j] + dc[half] * u3[half][j]
                        acc_b[half] = acc_b[half] + dc[half]
                    if s >= 1:
                        p = (s - 1) * strip
                        w3 = taps[half][0]
                        du = (exts_d[half][pl.ds(p, strip), lanes] * w3[2] + exts_d[half][pl.ds(p + 1, strip), lanes] * w3[1]
                              + exts_d[half][pl.ds(p + 2, strip), lanes] * w3[0])
                        if (s - 1) % 2 == 0:
                            pending[half] = du
                        else:
                            du_ref[half, p - strip:p + strip, lanes] = jnp.concatenate([pending[half], du], axis=0).astype(BF)
            for half in range(2):
                for j in range(3):
                    dw_ref[half, j:j + 1, lanes] += jnp.sum(acc_w[half][j], axis=0, keepdims=True)
                db_ref[half, :, lanes] += jnp.sum(acc_b[half], axis=0, keepdims=True)

    tile, before, after, vec = _pair_specs(2)
    da_tile, _, da_after_spec, _ = _pair_specs(1)
    return pl.pallas_call(
        body, name="conv_bwd", grid=(4, N_TOK_TILES),
        in_specs=[tile, before, after, da_tile, da_after_spec, vec(3), vec(1)],
        out_specs=[tile, vec(3), vec(1)],
        out_shape=[jax.ShapeDtypeStruct((2, 4, SEQ, FF_BLK), BF), jax.ShapeDtypeStruct((2, 4, 3, FF_BLK), F32),
                   jax.ShapeDtypeStruct((2, 4, 1, FF_BLK), F32)],
        scratch_shapes=[pltpu.VMEM((2 * HALO + TOK_TILE, FF_BLK), F32)] * 4,
        compiler_params=_params("parallel", "arbitrary"),
    )(u, u, u, da, da, w_conv, b_conv)


ANY = pl.BlockSpec(memory_space=pl.ANY)


def _place():
    x, y, c = lax.axis_index("x"), lax.axis_index("y"), lax.axis_index("c")
    other_chips = [(1 - x, y), (x, 1 - y), (1 - x, 1 - y)]
    return x, y, c, other_chips


def _all_gather(shards, name):
    n = len(shards)

    def body(*refs):
        src, out = refs[:n], refs[n:2 * n]
        send_sems, recv_sems, local_sems = refs[2 * n:]
        x, y, c, chips = _place()
        me, sibling = (x, y, c), (x, y, 1 - c)

        def copy(a, k, block, to, own=False):
            dst = out[a].at[4 * block[0] + 2 * block[1] + block[2]]
            return pltpu.make_async_remote_copy(src_ref=src[a] if own else dst, dst_ref=dst, send_sem=send_sems.at[a, k],
                                                recv_sem=recv_sems.at[a, k], device_id=to, device_id_type=MESH)

        mine = [pltpu.make_async_copy(src[a], out[a].at[4 * x + 2 * y + c], local_sems.at[a]) for a in range(n)]
        first = []
        for a in range(n):
            mine[a].start()
            first.append(copy(a, 0, me, sibling, own=True))
            first += [copy(a, 1 + j, me, (*chip, c), own=True) for j, chip in enumerate(chips)]
        for cp in first:
            cp.start()
        passed = []
        for j, chip in enumerate(chips):
            for a in range(n):
                copy(a, 1 + j, (*chip, c), me).wait_recv()
                passed.append(copy(a, 4 + j, (*chip, c), sibling))
                passed[-1].start()
        for a in range(n):
            copy(a, 0, sibling, me).wait_recv()
            for j, chip in enumerate(chips):
                copy(a, 4 + j, (*chip, 1 - c), me).wait_recv()
        for cp in first + passed:
            cp.wait_send()
        for cp in mine:
            cp.wait()

    return pl.pallas_call(
        body, name=name, in_specs=[ANY] * n, out_specs=[ANY] * n,
        out_shape=[jax.ShapeDtypeStruct((N_DEV,) + s.shape, s.dtype) for s in shards],
        scratch_shapes=[pltpu.SemaphoreType.DMA((n, 7)), pltpu.SemaphoreType.DMA((n, 7)), pltpu.SemaphoreType.DMA((n,))],
    )(*shards)


SEM = pl.BlockSpec(memory_space=pltpu.SEMAPHORE)
IN_HBM = pl.BlockSpec(memory_space=pltpu.HBM)
SPLIT_PARAMS = pltpu.CompilerParams(has_side_effects=pltpu.SideEffectType.DATAFLOW_SIDE_EFFECTING)


def _gather_first(refs, send_sems, recv_sems):
    x, y, c, chips = _place()
    targets = [(x, y, 1 - c)] + [(px, py, c) for px, py in chips]
    return [pltpu.make_async_remote_copy(src_ref=refs[2 * a], dst_ref=refs[2 * a + 1].at[4 * x + 2 * y + c],
                                         send_sem=send_sems.at[4 * a + k], recv_sem=recv_sems.at[4 * a + k],
                                         device_id=to, device_id_type=MESH)
            for a in range(len(refs) // 2) for k, to in enumerate(targets)]


def _gather_second(refs, send_sems, recv_sems):
    x, y, c, chips = _place()
    copies = []
    for a, land in enumerate(refs):
        for j, (px, py) in enumerate(chips):
            block = land.at[4 * px + 2 * py + c]
            copies.append(pltpu.make_async_remote_copy(src_ref=block, dst_ref=block, send_sem=send_sems.at[3 * a + j],
                                                       recv_sem=recv_sems.at[3 * a + j], device_id=(x, y, 1 - c),
                                                       device_id_type=MESH))
    return copies


def _reduce_first(refs, send_sems, recv_sems):
    x, y, c, _ = _place()
    return [pltpu.make_async_remote_copy(src_ref=refs[2 * a].at[j, 1 - c], dst_ref=refs[2 * a + 1].at[j],
                                         send_sem=send_sems.at[4 * a + j], recv_sem=recv_sems.at[4 * a + j],
                                         device_id=(x, y, 1 - c), device_id_type=MESH)
            for a in range(len(refs) // 2) for j in range(4)]


def _reduce_second(refs, send_sems, recv_sems):
    _, _, c, chips = _place()
    return [pltpu.make_async_remote_copy(src_ref=refs[2 * a].at[2 * px + py], dst_ref=refs[2 * a + 1].at[k],
                                         send_sem=send_sems.at[3 * a + k], recv_sem=recv_sems.at[3 * a + k],
                                         device_id=(px, py, c), device_id_type=MESH)
            for a in range(len(refs) // 2) for k, (px, py) in enumerate(chips)]


def _split_start(name, groups):
    arrays = [a for g in groups for a in g[0]]
    n = len(arrays)

    def body(*refs):
        sems = refs[n:n + 2 * len(groups)]
        at = 0
        for gi, (members, _, build) in enumerate(groups):
            for cp in build(refs[at:at + len(members)], sems[2 * gi], sems[2 * gi + 1]):
                cp.start()
            at += len(members)
        refs[-1][...] = jnp.zeros_like(refs[-1])

    sem_shapes = [pltpu.SemaphoreType.DMA((g[1],)) for g in groups for _ in range(2)]
    outs = pl.pallas_call(
        body, name=name, in_specs=[IN_HBM] * n,
        out_shape=(*sem_shapes, *[pltpu.HBM(a.shape, a.dtype) for a in arrays], jax.ShapeDtypeStruct((8, 128), F32)),
        out_specs=(*[SEM] * len(sem_shapes), *[IN_HBM] * n, pl.BlockSpec(memory_space=pltpu.VMEM)),
        input_output_aliases={i: len(sem_shapes) + i for i in range(n)}, compiler_params=SPLIT_PARAMS,
    )(*[pltpu.with_memory_space_constraint(a, pltpu.HBM) for a in arrays])
    per_group, at = [], len(sem_shapes)
    for gi, (members, _, _) in enumerate(groups):
        per_group.append((outs[2 * gi], outs[2 * gi + 1], list(outs[at:at + len(members)])))
        at += len(members)
    return per_group, outs[-1]


def _split_wait(name, started, build, after):
    send_sems, recv_sems, arrays = started
    n = len(arrays)

    def body(*refs):
        for cp in build(refs[:n], refs[n], refs[n + 1]):
            cp.wait_send()
            cp.wait_recv()

    return pl.pallas_call(
        body, name=name, in_specs=[IN_HBM] * n + [SEM, SEM, ANY],
        out_shape=tuple(pltpu.HBM(a.shape, a.dtype) for a in arrays), out_specs=tuple([IN_HBM] * n),
        input_output_aliases={i: i for i in range(n)}, compiler_params=SPLIT_PARAMS,
    )(*arrays, send_sems, recv_sems, after)


def _gather_landing(shard, me):
    return lax.dynamic_update_slice(lax.empty((N_DEV,) + shard.shape, shard.dtype), shard[None],
                                    (me,) + (0,) * shard.ndim)


def _tile_2d(rows, cols):
    for t in (256, 176, 128):
        if rows % t == 0:
            return t, cols
    return rows, 256


def _pair_sum(part, recv, core, name):
    _, rows, cols = recv.shape
    tr, tc = _tile_2d(rows, cols)

    def body(c_ref, p_ref, r_ref, o_ref):
        del c_ref
        o_ref[...] = (p_ref[...].astype(F32) + r_ref[...].astype(F32)).astype(BF)

    grid_spec = pltpu.PrefetchScalarGridSpec(
        num_scalar_prefetch=1, grid=(4, rows // tr, cols // tc),
        in_specs=[pl.BlockSpec((None, None, tr, tc), lambda j, i, k, c_ref: (j, c_ref[0], i, k)),
                  pl.BlockSpec((None, tr, tc), lambda j, i, k, c_ref: (j, i, k))],
        out_specs=pl.BlockSpec((None, tr, tc), lambda j, i, k, c_ref: (j, i, k)))
    return pl.pallas_call(
        body, name=name, grid_spec=grid_spec, out_shape=jax.ShapeDtypeStruct(recv.shape, BF),
        compiler_params=_params("parallel", "parallel", "parallel"),
    )(core, part, recv)


def _adamw(w, g, m, v):
    m = ADAM_B1 * m + (1.0 - ADAM_B1) * g
    v = ADAM_B2 * v + (1.0 - ADAM_B2) * (g * g)
    delta = -ADAM_LR * ((m / ADAM_C1) / (jnp.sqrt(v / ADAM_C2) + ADAM_EPS) + ADAM_WD * w)
    return delta, m, v


def _chip_sum_adamw(sums, recv, w, m, v, chip, name):
    rows, cols = w.shape
    tr, tc = _tile_2d(rows, cols)

    def body(chip_ref, s_ref, r_ref, w_ref, m_ref, v_ref, g_out, d_out, m_out, v_out):
        del chip_ref
        g = s_ref[...].astype(F32)
        for k in range(3):
            g = g + r_ref[k].astype(F32)
        g_out[...] = g
        d_out[...], m_out[...], v_out[...] = _adamw(w_ref[...], g, m_ref[...], v_ref[...])

    tile = pl.BlockSpec((tr, tc), lambda i, k, chip_ref: (i, k))
    grid_spec = pltpu.PrefetchScalarGridSpec(
        num_scalar_prefetch=1, grid=(rows // tr, cols // tc),
        in_specs=[pl.BlockSpec((None, tr, tc), lambda i, k, chip_ref: (chip_ref[0], i, k)),
                  pl.BlockSpec((3, tr, tc), lambda i, k, chip_ref: (0, i, k)), tile, tile, tile],
        out_specs=[tile] * 4)
    return pl.pallas_call(
        body, name=name, grid_spec=grid_spec, out_shape=[jax.ShapeDtypeStruct((rows, cols), F32)] * 4,
        compiler_params=_params("parallel", "parallel"),
    )(chip, sums, recv, w, m, v)


def _sum8_adamw(parts, w, m, v):
    rows = w.shape[0]

    def body(p_ref, w_ref, m_ref, v_ref, g_out, d_out, m_out, v_out):
        g = p_ref[0]
        for d in range(1, N_DEV):
            g = g + p_ref[d]
        g_out[...] = g
        d_out[...], m_out[...], v_out[...] = _adamw(w_ref[...], g, m_ref[...], v_ref[...])

    full = _const_spec((rows, 128))
    return pl.pallas_call(
        body, name="small_sum_adamw", grid=(1,), in_specs=[_const_spec((N_DEV, rows, 128)), full, full, full],
        out_specs=[full] * 4, out_shape=[jax.ShapeDtypeStruct((rows, 128), F32)] * 4,
        compiler_params=_params("arbitrary"),
    )(parts, w, m, v)


def _plain_adamw(g, w, m, v):
    rows = w.shape[0]

    def body(g_ref, w_ref, m_ref, v_ref, d_out, m_out, v_out):
        d_out[...], m_out[...], v_out[...] = _adamw(w_ref[...], g_ref[...], m_ref[...], v_ref[...])

    full = _const_spec((rows, 128))
    return pl.pallas_call(
        body, name="shard_adamw", grid=(1,), in_specs=[full] * 4, out_specs=[full] * 3,
        out_shape=[jax.ShapeDtypeStruct((rows, 128), F32)] * 3, compiler_params=_params("arbitrary"),
    )(g, w, m, v)


def _pack(arrays, rows):
    flat = jnp.concatenate([a.reshape(-1) for a in arrays])
    return jnp.pad(flat, (0, rows * 128 - flat.shape[0])).reshape(rows, 128)


def _unpack(packed, shapes):
    flat = packed.reshape(-1)
    out, at = [], 0
    for s in shapes:
        size = 1
        for d in s:
            size *= d
        out.append(flat[at:at + size].reshape(s))
        at += size
    return out


MM_TILE = 512
N_MM_TILES = SEQ // MM_TILE
CAT_TILE = 512
N_CAT_TILES = N_CAT // CAT_TILE
SMALL_ROWS = 808
SHARD_ROWS = 32


def kernel(x, g_mix, w_in, b_gate, w_gk_up, b_gk, w_pool_grp, pool_scale, g_gla_head, w_pool_proj, w_gla_proj, w_out, g_ffn, w_up, w_conv, b_conv, w_down, g_final, loss_target, m_g_mix, m_w_in, m_b_gate, m_w_gk_up, m_b_gk, m_w_pool_grp, m_pool_scale, m_g_gla_head, m_w_pool_proj, m_w_gla_proj, m_w_out, m_g_ffn, m_w_up, m_w_conv, m_b_conv, m_w_down, m_g_final, v_g_mix, v_w_in, v_b_gate, v_w_gk_up, v_b_gk, v_w_pool_grp, v_pool_scale, v_g_gla_head, v_w_pool_proj, v_w_gla_proj, v_w_out, v_g_ffn, v_w_up, v_w_conv, v_b_conv, v_w_down, v_g_final):
    xi, yi, ci = lax.axis_index("x"), lax.axis_index("y"), lax.axis_index("c")
    me = 4 * xi + 2 * yi + ci
    core = jnp.reshape(ci, (1,)).astype(jnp.int32)
    chip = jnp.reshape(2 * xi + yi, (1,)).astype(jnp.int32)
    xs, target = x[0], loss_target[0]

    big = dict(w_in=w_in[0].T, w_pool_proj=w_pool_proj[0], w_gla_proj=w_gla_proj[0], w_out=w_out[0], w_up=w_up[0].T,
               w_down=w_down[0])
    moments = dict(w_in=(m_w_in[0].T, v_w_in[0].T), w_pool_proj=(m_w_pool_proj[0], v_w_pool_proj[0]),
                   w_gla_proj=(m_w_gla_proj[0], v_w_gla_proj[0]), w_out=(m_w_out[0], v_w_out[0]),
                   w_up=(m_w_up[0].T, v_w_up[0].T), w_down=(m_w_down[0], v_w_down[0]))
    names = list(big)
    shards = {k: big[k].astype(BF) for k in names}
    shards["w_gk_up"], shards["w_conv"] = w_gk_up[0], w_conv[0]
    gather_groups = (("w_in", "w_gk_up"), ("w_pool_proj", "w_gla_proj", "w_out"), ("w_up", "w_down", "w_conv"))
    started, token = _split_start("gather_start", [
        ([t for k in g for t in (shards[k], _gather_landing(shards[k], me))], 4 * len(g), _gather_first)
        for g in gather_groups])

    def gather_pass(gi, after):
        lands = list(_split_wait(f"gather_wait_{gi}", started[gi], _gather_first, after)[1::2])
        passed, _ = _split_start(f"gather_pass_{gi}", [(lands, 3 * len(lands), _gather_second)])
        return passed[0]

    def gather_done(gi, passed, after):
        return dict(zip(gather_groups[gi], _split_wait(f"gather_pass_wait_{gi}", passed, _gather_second, after)))

    tok = lambda i, j, k: (i, 0)
    whole = lambda i, j, k: (0, 0)
    kblk = lambda i, j, k: (k, 0)
    ff_tile = (None, None, MM_TILE, FF_BLK)
    ff_seq = (None, None, SEQ, FF_BLK)

    h = _rms_fwd(xs, g_mix + token[:1, :1], "rms_mix")
    wg = gather_done(0, gather_pass(0, h), h)
    wt_in = wg["w_in"].reshape(IN_TOTAL, D_MODEL)
    wt_cat = jnp.concatenate([wt_in[R_QKV:R_OG], wt_in[R_GATE:], wt_in[R_OG:R_GK], wt_in[R_POOL:R_QKV]], axis=0)
    wt_gk = jnp.pad(wt_in[R_GK:R_GATE], ((0, GK_PAD - GATE_RANK), (0, 0)))
    wgk_pad = jnp.pad(wg["w_gk_up"].transpose(1, 0, 2).reshape(GATE_RANK, GLA_DK), ((0, GK_PAD - GATE_RANK), (0, 0)))
    zcat = _mm(h, wt_cat, out_shape=(SEQ, N_CAT), out_dtype=F32, grid=(N_CAT_TILES, 1, 1),
               blk_a=(SEQ, D_MODEL), blk_b=(CAT_TILE, D_MODEL), blk_o=(SEQ, CAT_TILE),
               map_a=whole, map_b=lambda j, i, k: (j, 0), map_o=lambda j, i, k: (0, j), tb=True, name="mm_in")
    la = _gk_fwd(h, wt_gk, wgk_pad, b_gk)
    passed = gather_pass(1, la)
    o, states = _gla_fwd(zcat, la)
    wg = gather_done(1, passed, o)
    wpp = wg["w_pool_proj"].transpose(1, 0, 2).reshape(POOL_WIDTH, D_MODEL)
    wgp = wg["w_gla_proj"].reshape(D_MODEL, D_MODEL)
    wout = wg["w_out"].reshape(D_MODEL, D_MODEL)
    og = _post_gla_fwd(o, zcat, g_gla_head)
    passed = gather_pass(2, og)
    ps = _pool_fwd(zcat, w_pool_grp[0], pool_scale)
    y_pool = _mm(ps, wpp, out_shape=(SEQ, D_MODEL), out_dtype=F32, grid=(N_MM_TILES, 1, 1),
                 blk_a=(MM_TILE, POOL_WIDTH), blk_b=(POOL_WIDTH, D_MODEL), blk_o=(MM_TILE, D_MODEL),
                 map_a=tok, map_b=whole, map_o=tok, name="mm_pool_proj")
    sq = dict(out_shape=(SEQ, D_MODEL), grid=(N_MM_TILES, 1, 1), blk_a=(MM_TILE, D_MODEL), blk_b=(D_MODEL, D_MODEL),
              blk_o=(MM_TILE, D_MODEL), map_a=tok, map_b=whole, map_o=tok)
    y_gla = _mm(og, wgp, out_dtype=F32, name="mm_gla_proj", **sq)
    mixed = _mix_fwd(zcat, b_gate, y_pool, y_gla)
    x1 = _mm(mixed, wout, out_dtype=F32, res=xs, name="mm_out", **sq)
    h2 = _rms_fwd(x1, g_ffn, "rms_ffn")
    wg = gather_done(2, passed, h2)
    wt_up = wg["w_up"].reshape(2 * D_FF, D_MODEL)
    wdown = wg["w_down"].reshape(D_FF, D_MODEL)
    wconv4 = wg["w_conv"].reshape(2, 4, 3, FF_BLK)
    bconv4 = b_conv.reshape(2, 4, 1, FF_BLK)
    blk4 = lambda b, i, k: (b // 4, b % 4, 0, 0)
    u4 = _mm(h2, wt_up, out_shape=(2, 4, SEQ, FF_BLK), out_dtype=F32, grid=(N_DEV, 1, 1),
             blk_a=(SEQ, D_MODEL), blk_b=(FF_BLK, D_MODEL), blk_o=ff_seq,
             map_a=whole, map_b=lambda b, i, k: (b, 0), map_o=blk4, tb=True, name="mm_up")
    act = _conv_fwd(u4, wconv4, bconv4)
    x2 = _mm(act, wdown, out_shape=(SEQ, D_MODEL), out_dtype=F32, grid=(N_MM_TILES, 1, 4),
             blk_a=ff_tile, blk_b=(FF_BLK, D_MODEL), blk_o=(MM_TILE, D_MODEL),
             map_a=lambda i, j, k: (0, k, i, 0), map_b=kblk, map_o=tok, res=x1, name="mm_down")
    loss_part, dx2, dg_final = _final_loss(x2, g_final.reshape(1, D_MODEL), target)

    da = _mm(dx2, wdown, out_shape=(1, 4, SEQ, FF_BLK), out_dtype=BF, grid=(4, N_MM_TILES, 1),
             blk_a=(MM_TILE, D_MODEL), blk_b=(FF_BLK, D_MODEL), blk_o=ff_tile,
             map_a=lambda b, i, k: (i, 0), map_b=lambda b, i, k: (b, 0), map_o=lambda b, i, k: (0, b, i, 0),
             tb=True, name="mm_d_act")
    d_wdown = _mm(act, dx2, out_shape=(D_FF, D_MODEL), out_dtype=BF, grid=(4, 1, N_MM_TILES),
                  blk_a=ff_tile, blk_b=(MM_TILE, D_MODEL), blk_o=(FF_BLK, D_MODEL),
                  map_a=lambda b, j, k: (0, b, k, 0), map_b=kblk, map_o=lambda b, j, k: (b, 0),
                  ta=True, name="mm_d_wdown")
    du4, d_wconv, d_bconv = _conv_bwd(u4, da, wconv4, bconv4)
    dh2 = _mm(du4, wt_up, out_shape=(SEQ, D_MODEL), out_dtype=F32, grid=(1, 1, N_DEV),
              blk_a=ff_seq, blk_b=(FF_BLK, D_MODEL), blk_o=(SEQ, D_MODEL),
              map_a=lambda i, j, k: (k // 4, k % 4, 0, 0), map_b=kblk, map_o=whole, name="mm_d_h2")
    d_wt_up = _mm(du4, h2, out_shape=(2 * D_FF, D_MODEL), out_dtype=BF, grid=(N_DEV, 1, 1),
                  blk_a=ff_seq, blk_b=(SEQ, D_MODEL), blk_o=(FF_BLK, D_MODEL),
                  map_a=blk4, map_b=whole, map_o=lambda b, i, k: (b, 0), ta=True, name="mm_d_wup")
    res = {}

    def reduce_start(keys, parts):
        arrays = [t for k in keys for t in (parts[k], lax.empty((4,) + parts[k].shape[2:], BF))]
        st, tkn = _split_start("reduce_start_" + keys[0], [(arrays, 4 * len(keys), _reduce_first)])
        return st[0], tkn

    def reduce_cross(keys, st, after):
        arrays = _split_wait("reduce_wait_" + keys[0], st, _reduce_first, after)
        sums = [_pair_sum(p, r, core, "pair_sum_" + k) for k, p, r in zip(keys, arrays[0::2], arrays[1::2])]
        arrays = [t for s in sums for t in (s, lax.empty((3,) + s.shape[1:], BF))]
        st2, tkn = _split_start("reduce_cross_" + keys[0], [(arrays, 3 * len(keys), _reduce_second)])
        return st2[0], tkn

    def reduce_done(keys, st2, after):
        arrays = _split_wait("reduce_cross_wait_" + keys[0], st2, _reduce_second, after)
        for k, s, r in zip(keys, arrays[0::2], arrays[1::2]):
            outs = _chip_sum_adamw(s, r, big[k], moments[k][0], moments[k][1], chip, "adamw_" + k)
            res[k] = [(t.T if k in ("w_in", "w_up") else t)[None] for t in outs]

    ffn_keys = ("w_down", "w_up")
    ffn_red, tkn = reduce_start(ffn_keys, dict(w_down=d_wdown.reshape(4, 2, D_FF // N_DEV, D_MODEL),
                                               w_up=d_wt_up.reshape(4, 2, FF_BLK, D_MODEL)))
    dx1, dg_ffn = _rms_bwd(dh2, x1, g_ffn + tkn[:1, :1], dx2, "rms_ffn_bwd")

    sq_t = dict(out_shape=(D_MODEL, D_MODEL), grid=(1, 1, N_MM_TILES), blk_a=(MM_TILE, D_MODEL),
                blk_b=(MM_TILE, D_MODEL), blk_o=(D_MODEL, D_MODEL), map_a=kblk, map_b=kblk, map_o=whole, ta=True)
    dmixed = _mm(dx1, wout, out_dtype=F32, tb=True, name="mm_d_mixed", **sq)
    d_wout = _mm(mixed, dx1, out_dtype=BF, name="mm_d_wout", **sq_t)
    dzcat, dy_pool, dy_gla, db_gate = _mix_bwd(dmixed, zcat, b_gate, y_pool, y_gla)
    ffn_red, _ = reduce_cross(ffn_keys, ffn_red, db_gate)
    d_og =_mm(dy_gla, wgp, out_dtype=F32, tb=True, name="mm_d_og", **sq)
    d_wgp = _mm(og, dy_gla, out_dtype=BF, name="mm_d_wgp", **sq_t)
    dzcat, d_o, dg_head = _post_gla_bwd(dzcat, d_og, o, zcat, g_gla_head)
    dzcat, dla = _gla_bwd(dzcat, zcat, la, d_o, states)
    dh_gk, d_wt_gk, d_wgk, db_gk = _gk_bwd(dla, h, wt_gk, wgk_pad, b_gk)
    dps = _mm(dy_pool, wpp, out_shape=(SEQ, POOL_WIDTH), out_dtype=F32, grid=(N_MM_TILES, 1, 1),
              blk_a=(MM_TILE, D_MODEL), blk_b=(POOL_WIDTH, D_MODEL), blk_o=(MM_TILE, POOL_WIDTH),
              map_a=tok, map_b=whole, map_o=tok, tb=True, name="mm_d_ps")
    d_wpp = _mm(ps, dy_pool, out_shape=(POOL_WIDTH, D_MODEL), out_dtype=F32, grid=(1, 1, N_MM_TILES),
                blk_a=(MM_TILE, POOL_WIDTH), blk_b=(MM_TILE, D_MODEL), blk_o=(POOL_WIDTH, D_MODEL),
                map_a=kblk, map_b=kblk, map_o=whole, ta=True, name="mm_d_wpp")
    dzcat, d_wgrp, d_scale = _pool_bwd(dzcat, zcat, dps, w_pool_grp[0], pool_scale)
    mix_keys = ("w_out", "w_gla_proj", "w_pool_proj")
    mix_red, _ = reduce_start(mix_keys, dict(
        w_out=d_wout.reshape(4, 2, D_MODEL // N_DEV, D_MODEL), w_gla_proj=d_wgp.reshape(4, 2, D_MODEL // N_DEV, D_MODEL),
        w_pool_proj=d_wpp.reshape(POOL_WIDTH, N_DEV, D_MODEL // N_DEV).transpose(1, 0, 2).astype(BF)
        .reshape(4, 2, POOL_WIDTH, D_MODEL // N_DEV)))
    dh = _mm(dzcat, wt_cat, out_shape=(SEQ, D_MODEL), out_dtype=F32, grid=(1, 1, N_CAT_TILES),
             blk_a=(SEQ, CAT_TILE), blk_b=(CAT_TILE, D_MODEL), blk_o=(SEQ, D_MODEL),
             map_a=lambda i, j, k: (0, k), map_b=kblk, map_o=whole, res=dh_gk, name="mm_d_h")
    mix_red, _ = reduce_cross(mix_keys, mix_red, dh)
    d_wt_cat = _mm(dzcat, h, out_shape=(N_CAT, D_MODEL), out_dtype=BF, grid=(N_CAT_TILES, 1, 1),
                   blk_a=(SEQ, CAT_TILE), blk_b=(SEQ, D_MODEL), blk_o=(CAT_TILE, D_MODEL),
                   map_a=lambda j, i, k: (0, j), map_b=whole, map_o=lambda j, i, k: (j, 0), ta=True, name="mm_d_wcat")
    d_wt_in = jnp.concatenate([d_wt_cat[C_POOL:], d_wt_cat[C_QKV:C_GATE], d_wt_cat[C_OG:C_POOL],
                               d_wt_gk[:GATE_RANK].astype(BF), d_wt_cat[C_GATE:C_OG]], axis=0)
    in_red, _ = reduce_start(("w_in",), dict(w_in=d_wt_in.reshape(4, 2, IN_SHARD, D_MODEL)))
    grad_x, dg_mix = _rms_bwd(dh, xs, g_mix, dx1, "rms_mix_bwd")
    reduce_done(ffn_keys, ffn_red, grad_x)
    in_red, _ = reduce_cross(("w_in",), in_red, res["w_up"][0])
    reduce_done(mix_keys, mix_red, res["w_down"][0])

    small = [("g_mix", dg_mix, g_mix, m_g_mix, v_g_mix), ("b_gate", db_gate, b_gate, m_b_gate, v_b_gate),
             ("w_gk_up", d_wgk[:GATE_RANK], None, None, None), ("b_gk", db_gk, b_gk, m_b_gk, v_b_gk),
             ("w_pool_grp", d_wgrp, w_pool_grp, m_w_pool_grp, v_w_pool_grp),
             ("pool_scale", d_scale, pool_scale, m_pool_scale, v_pool_scale),
             ("g_gla_head", dg_head, g_gla_head, m_g_gla_head, v_g_gla_head), ("g_ffn", dg_ffn, g_ffn, m_g_ffn, v_g_ffn),
             ("w_conv", d_wconv, None, None, None), ("b_conv", d_bconv, b_conv, m_b_conv, v_b_conv),
             ("g_final", dg_final, g_final, m_g_final, v_g_final), ("loss", loss_part, None, None, None)]
    zeros_like_part = lambda t: jnp.zeros(t[1].shape, F32)
    g_all = _all_gather([_pack([t[1] for t in small], SMALL_ROWS)], "gather_small_grads")[0]
    packed = _sum8_adamw(g_all, *[_pack([zeros_like_part(t) if t[i] is None else t[i] for t in small], SMALL_ROWS)
                                  for i in (2, 3, 4)])
    shapes = [t[1].shape if t[2] is None else t[2].shape for t in small]
    unpacked = [_unpack(p, shapes) for p in packed]
    for idx, t in enumerate(small):
        if t[2] is not None:
            res[t[0]] = [unpacked[q][idx] for q in range(4)]
    g_wgk = lax.dynamic_slice(unpacked[0][2], (0, me * (GLA_DK // N_DEV)), (GATE_RANK, GLA_DK // N_DEV))
    g_wconv = lax.dynamic_index_in_dim(unpacked[0][8].reshape(N_DEV, 3, FF_BLK), me, axis=0, keepdims=False)
    shard_shapes = [(1, GATE_RANK, GLA_DK // N_DEV), (1, 3, FF_BLK)]
    shard_out = _plain_adamw(_pack([g_wgk, g_wconv], SHARD_ROWS), _pack([w_gk_up, w_conv], SHARD_ROWS),
                             _pack([m_w_gk_up, m_w_conv], SHARD_ROWS), _pack([v_w_gk_up, v_w_conv], SHARD_ROWS))
    shard_un = [_unpack(p, shard_shapes) for p in shard_out]
    res["w_gk_up"] = [g_wgk[None]] + [s[0] for s in shard_un]
    res["w_conv"] = [g_wconv[None]] + [s[1] for s in shard_un]

    reduce_done(("w_in",), in_red, shard_out[0])
    loss = unpacked[0][11][0, 0]
    order =["g_mix", "w_in", "b_gate", "w_gk_up", "b_gk", "w_pool_grp", "pool_scale", "g_gla_head", "w_pool_proj",
             "w_gla_proj", "w_out", "g_ffn", "w_up", "w_conv", "b_conv", "w_down", "g_final"]
    return (loss, grad_x[None], *[res[k][0] for k in order], *[res[k][1] for k in order],
            *[res[k][2] for k in order], *[res[k][3] for k in order])
```

```python
import functools

import jax
import jax.numpy as jnp
from jax import lax
from jax.experimental import pallas as pl
from jax.experimental.pallas import tpu as pltpu

F32 = jnp.float32
BF = jnp.bfloat16
HIGHEST = lax.Precision.HIGHEST
MESH = pl.DeviceIdType.MESH

N_DEV = 8
SEQ = 2048
D_MODEL = 1024
CHUNK = 64
EPS = 1e-6
POOL_WIDTH = 512
POOL_WINDOWS = (2, 4, 8, 16)
POOL_GD = 128
POOL_HALO = 16
HEADS = 4
HK = 128
HV = 256
GLA_DK = 512
GATE_RANK = 16
GATE_NORM = 16.0
D_FF = 2816
FF_BLK = 704
IN_TOTAL = 5648
IN_SHARD = 706
C_QKV, C_GATE, C_OG, C_POOL = 0, 2048, 4096, 5120
N_CAT = 5632
R_POOL, R_QKV, R_OG, R_GK, R_GATE = 0, 512, 2560, 3584, 3600
GK_PAD = 128

ADAM_LR, ADAM_B1, ADAM_B2, ADAM_EPS, ADAM_WD, ADAM_STEP = 0.001, 0.9, 0.999, 1e-08, 0.01, 10
ADAM_C1 = 1.0 - ADAM_B1 ** ADAM_STEP
ADAM_C2 = 1.0 - ADAM_B2 ** ADAM_STEP

VMEM_BYTES_V7X = 64 * 1024 * 1024
VMEM_LIMIT = 48 * 1024 * 1024

TOK_TILE = 256
HALO = 8
GLA_CPS = 4


def _params(*sem):
    return pltpu.CompilerParams(dimension_semantics=sem, vmem_limit_bytes=VMEM_LIMIT)


def _const_spec(shape):
    nd = len(shape)
    return pl.BlockSpec(shape, lambda *_: (0,) * nd)


def _dot(a, b, ta=False, tb=False):
    dims = (((0 if ta else 1,), (1 if tb else 0,)), ((), ()))
    return lax.dot_general(a.astype(BF), b.astype(BF), dims, preferred_element_type=F32)


def _dot_exact(a, b):
    return jnp.dot(a, b, precision=HIGHEST, preferred_element_type=F32)


def _sigmoid(x):
    return 1.0 / (1.0 + jnp.exp(-x))


def _mm(a, b, *, out_shape, out_dtype, grid, blk_a, blk_b, blk_o, map_a, map_b, map_o, ta=False, tb=False,
        res=None, name):
    gk = grid[2]

    def body(*refs):
        if res is None:
            a_ref, b_ref, o_ref = refs[:3]
            r_ref = None
            scr = refs[3:]
        else:
            a_ref, b_ref, r_ref, o_ref = refs[:4]
            scr = refs[4:]
        prod = _dot(a_ref[...], b_ref[...], ta, tb)

        def finish(total):
            if r_ref is not None:
                total = total + r_ref[...]
            o_ref[...] = total.astype(out_dtype)

        if gk == 1:
            finish(prod)
        else:
            acc = scr[0]
            k = pl.program_id(2)

            @pl.when(k == 0)
            def _():
                acc[...] = prod

            @pl.when(k > 0)
            def _():
                acc[...] += prod

            @pl.when(k == gk - 1)
            def _():
                finish(acc[...])

    in_specs = [pl.BlockSpec(blk_a, map_a), pl.BlockSpec(blk_b, map_b)]
    args = [a, b]
    if res is not None:
        in_specs.append(pl.BlockSpec(blk_o, map_o))
        args.append(res)
    return pl.pallas_call(
        body, name=name, grid=grid, in_specs=in_specs, out_specs=pl.BlockSpec(blk_o, map_o),
        out_shape=jax.ShapeDtypeStruct(out_shape, out_dtype),
        scratch_shapes=[] if gk == 1 else [pltpu.VMEM(tuple(d for d in blk_o if d is not None), F32)],
        compiler_params=_params("parallel", "parallel", "arbitrary"),
    )(*args)


def _rms_fwd(x, g, name):
    def body(x_ref, g_ref, o_ref):
        xv = x_ref[...]
        r = lax.rsqrt(jnp.mean(xv * xv, axis=-1, keepdims=True) + EPS)
        o_ref[...] = (xv * r * g_ref[...]).astype(BF)

    tile = pl.BlockSpec((TOK_TILE, D_MODEL), lambda i: (i, 0))
    return pl.pallas_call(
        body, name=name, grid=(SEQ // TOK_TILE,), in_specs=[tile, _const_spec((1, D_MODEL))], out_specs=tile,
        out_shape=jax.ShapeDtypeStruct((SEQ, D_MODEL), BF), compiler_params=_params("parallel"),
    )(x, g)


def _rms_bwd(dy, x, g, dres, name):
    def body(dy_ref, x_ref, g_ref, dres_ref, dx_ref, dg_ref):
        xv = x_ref[...]
        r = lax.rsqrt(jnp.mean(xv * xv, axis=-1, keepdims=True) + EPS)
        xn = xv * r
        dyv = dy_ref[...]
        dxn = dyv * g_ref[...]
        dx_ref[...] = dres_ref[...] + r * (dxn - xn * jnp.mean(dxn * xn, axis=-1, keepdims=True))
        part = jnp.sum(dyv * xn, axis=0, keepdims=True)

        @pl.when(pl.program_id(0) == 0)
        def _():
            dg_ref[...] = part

        @pl.when(pl.program_id(0) > 0)
        def _():
            dg_ref[...] += part

    tile = pl.BlockSpec((TOK_TILE, D_MODEL), lambda i: (i, 0))
    vec = _const_spec((1, D_MODEL))
    return pl.pallas_call(
        body, name=name, grid=(SEQ // TOK_TILE,), in_specs=[tile, tile, vec, tile], out_specs=[tile, vec],
        out_shape=[jax.ShapeDtypeStruct((SEQ, D_MODEL), F32), jax.ShapeDtypeStruct((1, D_MODEL), F32)],
        compiler_params=_params("arbitrary"),
    )(dy, x, g, dres)


def _final_loss(x2, g, target):
    def body(x_ref, g_ref, t_ref, loss_ref, dx_ref, dg_ref):
        xv = x_ref[...]
        r = lax.rsqrt(jnp.mean(xv * xv, axis=-1, keepdims=True) + EPS)
        xn = xv * r
        gv = g_ref[...]
        err = xn * gv - t_ref[...]
        lpart = jnp.full((1, 128), 0.5 * jnp.sum(jnp.mean(err * err, axis=-1, keepdims=True)), F32)
        dyv = err * (1.0 / D_MODEL)
        dxn = dyv * gv
        dx_ref[...] = r * (dxn - xn * jnp.mean(dxn * xn, axis=-1, keepdims=True))
        gpart = jnp.sum(dyv * xn, axis=0, keepdims=True)

        @pl.when(pl.program_id(0) == 0)
        def _():
            loss_ref[...] = lpart
            dg_ref[...] = gpart

        @pl.when(pl.program_id(0) > 0)
        def _():
            loss_ref[...] += lpart
            dg_ref[...] += gpart

    tile = pl.BlockSpec((TOK_TILE, D_MODEL), lambda i: (i, 0))
    vec = _const_spec((1, D_MODEL))
    return pl.pallas_call(
        body, name="final_loss", grid=(SEQ // TOK_TILE,), in_specs=[tile, vec, tile],
        out_specs=[_const_spec((1, 128)), tile, vec],
        out_shape=[jax.ShapeDtypeStruct((1, 128), F32), jax.ShapeDtypeStruct((SEQ, D_MODEL), F32),
                   jax.ShapeDtypeStruct((1, D_MODEL), F32)],
        compiler_params=_params("arbitrary"),
    )(x2, g, target)


def _pool_counts(w):
    pos = lax.broadcasted_iota(jnp.int32, (SEQ, 1), 0).astype(F32)
    return jnp.minimum(pos + 1.0, float(w))


def _pool_window(u, w, ext):
    ext[pl.ds(POOL_HALO, SEQ), :] = u
    win = u
    for j in range(1, w):
        win = win + ext[pl.ds(POOL_HALO - j, SEQ), :]
    return win / _pool_counts(w) - u


def _pool_fwd(zcat, w_grp, scale):
    def body(z_ref, w_ref, s_ref, o_ref, ext):
        ext[pl.ds(0, POOL_HALO), :] = jnp.zeros((POOL_HALO, POOL_GD), F32)
        for g, w in enumerate(POOL_WINDOWS):
            cols = slice(g * POOL_GD, (g + 1) * POOL_GD)
            p = _pool_window(z_ref[:, cols], w, ext)
            o_ref[:, cols] = (_dot(p, w_ref[g]) * s_ref[:, cols]).astype(BF)

    return pl.pallas_call(
        body, name="pool_fwd", grid=(1,),
        in_specs=[pl.BlockSpec((SEQ, POOL_WIDTH), lambda i: (0, C_POOL // POOL_WIDTH)),
                  _const_spec((4, POOL_GD, POOL_GD)), _const_spec((1, POOL_WIDTH))],
        out_specs=_const_spec((SEQ, POOL_WIDTH)), out_shape=jax.ShapeDtypeStruct((SEQ, POOL_WIDTH), BF),
        scratch_shapes=[pltpu.VMEM((POOL_HALO + SEQ, POOL_GD), F32)], compiler_params=_params("arbitrary"),
    )(zcat, w_grp, scale)


def _pool_bwd(dzcat, zcat, dps, w_grp, scale):
    def body(dz_in, z_ref, dps_ref, w_ref, s_ref, dz_ref, dw_ref, dsc_ref, ext, ext2):
        del dz_in
        ext[pl.ds(0, POOL_HALO), :] = jnp.zeros((POOL_HALO, POOL_GD), F32)
        ext2[pl.ds(SEQ, POOL_HALO), :] = jnp.zeros((POOL_HALO, POOL_GD), F32)
        for g, w in enumerate(POOL_WINDOWS):
            cols = slice(g * POOL_GD, (g + 1) * POOL_GD)
            p = _pool_window(z_ref[:, cols], w, ext)
            wg = w_ref[g]
            pg = _dot(p, wg)
            dpsv = dps_ref[:, cols]
            dsc_ref[:, cols] = jnp.sum(dpsv * pg, axis=0, keepdims=True)
            dpg = dpsv * s_ref[:, cols]
            dw_ref[g] = _dot(p, dpg, ta=True)
            dp = _dot(dpg, wg, tb=True)
            dpc = dp / _pool_counts(w)
            ext2[pl.ds(0, SEQ), :] = dpc
            du = dpc
            for j in range(1, w):
                du = du + ext2[pl.ds(j, SEQ), :]
            dz_ref[:, cols] = (du - dp).astype(BF)

    return pl.pallas_call(
        body, name="pool_bwd", grid=(1,),
        in_specs=[pl.BlockSpec(memory_space=pl.ANY),
                  pl.BlockSpec((SEQ, POOL_WIDTH), lambda i: (0, C_POOL // POOL_WIDTH)),
                  _const_spec((SEQ, POOL_WIDTH)), _const_spec((4, POOL_GD, POOL_GD)), _const_spec((1, POOL_WIDTH))],
        out_specs=[pl.BlockSpec((SEQ, POOL_WIDTH), lambda i: (0, C_POOL // POOL_WIDTH)),
                   _const_spec((4, POOL_GD, POOL_GD)), _const_spec((1, POOL_WIDTH))],
        out_shape=[jax.ShapeDtypeStruct((SEQ, N_CAT), BF), jax.ShapeDtypeStruct((4, POOL_GD, POOL_GD), F32),
                   jax.ShapeDtypeStruct((1, POOL_WIDTH), F32)],
        scratch_shapes=[pltpu.VMEM((POOL_HALO + SEQ, POOL_GD), F32), pltpu.VMEM((SEQ + POOL_HALO, POOL_GD), F32)],
        input_output_aliases={0: 0}, compiler_params=_params("arbitrary"),
    )(dzcat, zcat, dps, w_grp, scale)


GK_TILE = 512


def _gk_fwd(h, wt_gk, wgk_pad, b_gk):
    def body(h_ref, wt_ref, w_ref, b_ref, la_ref):
        z_gk = _dot(h_ref[...], wt_ref[...], tb=True)
        pre = _dot(z_gk, w_ref[...]) + b_ref[...]
        la_ref[...] = (jnp.minimum(pre, 0.0) - jnp.log(1.0 + jnp.exp(-jnp.abs(pre)))) * (1.0 / GATE_NORM)

    return pl.pallas_call(
        body, name="gk_fwd", grid=(SEQ // GK_TILE,),
        in_specs=[pl.BlockSpec((GK_TILE, D_MODEL), lambda i: (i, 0)), _const_spec((GK_PAD, D_MODEL)),
                  _const_spec((GK_PAD, GLA_DK)), _const_spec((1, GLA_DK))],
        out_specs=pl.BlockSpec((GK_TILE, GLA_DK), lambda i: (i, 0)),
        out_shape=jax.ShapeDtypeStruct((SEQ, GLA_DK), F32), compiler_params=_params("parallel"),
    )(h, wt_gk, wgk_pad, b_gk)


def _gk_bwd(dla, h, wt_gk, wgk_pad, b_gk):
    def body(dla_ref, h_ref, wt_ref, w_ref, b_ref, dh_ref, dwt_ref, dw_ref, db_ref):
        hv = h_ref[...]
        wtv = wt_ref[...]
        wv = w_ref[...]
        z_gk = _dot(hv, wtv, tb=True)
        pre = _dot(z_gk, wv) + b_ref[...]
        dpre = dla_ref[...] * (1.0 / GATE_NORM) * (1.0 - _sigmoid(pre))
        dz_gk = _dot(dpre, wv, tb=True)
        dh_ref[...] = _dot(dz_gk, wtv)
        dwtp = _dot(dz_gk, hv, ta=True)
        dwp = _dot(z_gk, dpre, ta=True)
        dbp = jnp.sum(dpre, axis=0, keepdims=True)

        @pl.when(pl.program_id(0) == 0)
        def _():
            dwt_ref[...] = dwtp
            dw_ref[...] = dwp
            db_ref[...] = dbp

        @pl.when(pl.program_id(0) > 0)
        def _():
            dwt_ref[...] += dwtp
            dw_ref[...] += dwp
            db_ref[...] += dbp

    tile = pl.BlockSpec((GK_TILE, D_MODEL), lambda i: (i, 0))
    return pl.pallas_call(
        body, name="gk_bwd", grid=(SEQ // GK_TILE,),
        in_specs=[pl.BlockSpec((GK_TILE, GLA_DK), lambda i: (i, 0)), tile, _const_spec((GK_PAD, D_MODEL)),
                  _const_spec((GK_PAD, GLA_DK)), _const_spec((1, GLA_DK))],
        out_specs=[tile, _const_spec((GK_PAD, D_MODEL)), _const_spec((GK_PAD, GLA_DK)), _const_spec((1, GLA_DK))],
        out_shape=[jax.ShapeDtypeStruct((SEQ, D_MODEL), F32), jax.ShapeDtypeStruct((GK_PAD, D_MODEL), F32),
                   jax.ShapeDtypeStruct((GK_PAD, GLA_DK), F32), jax.ShapeDtypeStruct((1, GLA_DK), F32)],
        compiler_params=_params("arbitrary"),
    )(dla, h, wt_gk, wgk_pad, b_gk)


GLA_ROWS = GLA_CPS * CHUNK
GLA_STEPS = SEQ // GLA_ROWS
QKV_W = 2048


def _gla_chunk(qkv_ref, la_ref, rows, h):
    tri = lax.broadcasted_iota(jnp.int32, (CHUNK, CHUNK), 0) >= lax.broadcasted_iota(jnp.int32, (CHUNK, CHUNK), 1)
    q = qkv_ref[rows, h * HK:(h + 1) * HK] * (HK ** -0.5)
    k = qkv_ref[rows, GLA_DK + h * HK:GLA_DK + (h + 1) * HK]
    v = qkv_ref[rows, 2 * GLA_DK + h * HV:2 * GLA_DK + (h + 1) * HV]
    la = la_ref[rows, h * HK:(h + 1) * HK]
    bc = _dot_exact(tri.astype(F32), la)
    e_pos, e_neg = jnp.exp(bc), jnp.exp(-bc)
    dl = jnp.exp(jnp.sum(la, axis=0, keepdims=True))
    q_fw, q_bw, k_fw, k_bw = q * e_pos, q * e_neg, k * e_neg, k * e_pos
    scores = jnp.where(tri, _dot(q_fw, k_fw, tb=True), _dot(q_bw, k_bw, tb=True))
    return tri, v, e_pos, e_neg, dl, q_fw, q_bw, k_fw, k_bw, scores


def _gla_fwd(zcat, la):
    def body(qkv_ref, la_ref, o_ref, st_ref, state):
        @pl.when(pl.program_id(0) == 0)
        def _():
            state[...] = jnp.zeros_like(state)

        for c in range(GLA_CPS):
            rows = slice(c * CHUNK, (c + 1) * CHUNK)
            for h in range(HEADS):
                _, v, _, _, dl, q_fw, _, k_fw, _, scores = _gla_chunk(qkv_ref, la_ref, rows, h)
                st = state[h]
                st_ref[c, h] = st
                o_ref[rows, h * HV:(h + 1) * HV] = _dot(scores, v) + _dot(q_fw, st, tb=True)
                state[h] = st * dl + _dot(v, k_fw * dl, ta=True)

    return pl.pallas_call(
        body, name="gla_fwd", grid=(GLA_STEPS,),
        in_specs=[pl.BlockSpec((GLA_ROWS, QKV_W), lambda i: (i, 0)), pl.BlockSpec((GLA_ROWS, GLA_DK), lambda i: (i, 0))],
        out_specs=[pl.BlockSpec((GLA_ROWS, D_MODEL), lambda i: (i, 0)),
                   pl.BlockSpec((GLA_CPS, HEADS, HV, HK), lambda i: (i, 0, 0, 0))],
        out_shape=[jax.ShapeDtypeStruct((SEQ, D_MODEL), F32),
                   jax.ShapeDtypeStruct((SEQ // CHUNK, HEADS, HV, HK), F32)],
        scratch_shapes=[pltpu.VMEM((HEADS, HV, HK), F32)], compiler_params=_params("arbitrary"),
    )(zcat, la)


def _gla_bwd(dzcat, zcat, la, d_o, states):
    def body(dz_in, qkv_ref, la_ref, do_ref, st_ref, dqkv_ref, dla_ref, dstate):
        del dz_in

        @pl.when(pl.program_id(0) == 0)
        def _():
            dstate[...] = jnp.zeros_like(dstate)

        last_row = lax.broadcasted_iota(jnp.int32, (CHUNK, HK), 0) == CHUNK - 1
        upper = (lax.broadcasted_iota(jnp.int32, (CHUNK, CHUNK), 0)
                 <= lax.broadcasted_iota(jnp.int32, (CHUNK, CHUNK), 1)).astype(F32)
        for c in reversed(range(GLA_CPS)):
            rows = slice(c * CHUNK, (c + 1) * CHUNK)
            for h in range(HEADS):
                tri, v, e_pos, e_neg, dl, q_fw, q_bw, k_fw, k_bw, scores = _gla_chunk(qkv_ref, la_ref, rows, h)
                st = st_ref[c, h]
                dst = dstate[h]
                d_out = do_ref[rows, h * HV:(h + 1) * HV]
                k_dec = k_fw * dl
                dp = _dot(d_out, v, tb=True)
                dp_fw = jnp.where(tri, dp, 0.0)
                dp_bw = jnp.where(tri, 0.0, dp)
                dv = _dot(scores, d_out, ta=True) + _dot(k_dec, dst, tb=True)
                dk_dec = _dot(v, dst)
                dq_fw = _dot(dp_fw, k_fw) + _dot(d_out, st)
                dk_fw = _dot(dp_fw, q_fw, ta=True) + dk_dec * dl
                dq_bw = _dot(dp_bw, k_bw)
                dk_bw = _dot(dp_bw, q_bw, ta=True)
                ddl = jnp.sum(st * dst, axis=0, keepdims=True) + jnp.sum(k_fw * dk_dec, axis=0, keepdims=True)
                dstate[h] = dst * dl + _dot(d_out, q_fw, ta=True)
                dq = (dq_fw * e_pos + dq_bw * e_neg) * (HK ** -0.5)
                dk = dk_fw * e_neg + dk_bw * e_pos
                db = dq_fw * q_fw - dk_fw * k_fw - dq_bw * q_bw + dk_bw * k_bw + jnp.where(last_row, ddl * dl, 0.0)
                dla_ref[rows, h * HK:(h + 1) * HK] = _dot_exact(upper, db)
                dqkv_ref[rows, h * HK:(h + 1) * HK] = dq.astype(BF)
                dqkv_ref[rows, GLA_DK + h * HK:GLA_DK + (h + 1) * HK] = dk.astype(BF)
                dqkv_ref[rows, 2 * GLA_DK + h * HV:2 * GLA_DK + (h + 1) * HV] = dv.astype(BF)

    rev = lambda i: (GLA_STEPS - 1 - i, 0)
    return pl.pallas_call(
        body, name="gla_bwd", grid=(GLA_STEPS,),
        in_specs=[pl.BlockSpec(memory_space=pl.ANY), pl.BlockSpec((GLA_ROWS, QKV_W), rev),
                  pl.BlockSpec((GLA_ROWS, GLA_DK), rev), pl.BlockSpec((GLA_ROWS, D_MODEL), rev),
                  pl.BlockSpec((GLA_CPS, HEADS, HV, HK), lambda i: (GLA_STEPS - 1 - i, 0, 0, 0))],
        out_specs=[pl.BlockSpec((GLA_ROWS, QKV_W), rev), pl.BlockSpec((GLA_ROWS, GLA_DK), rev)],
        out_shape=[jax.ShapeDtypeStruct((SEQ, N_CAT), BF), jax.ShapeDtypeStruct((SEQ, GLA_DK), F32)],
        scratch_shapes=[pltpu.VMEM((HEADS, HV, HK), F32)], input_output_aliases={0: 0},
        compiler_params=_params("arbitrary"),
    )(dzcat, zcat, la, d_o, states)


def _silu_parts(x):
    s = _sigmoid(x)
    return x * s, s * (1.0 + x * (1.0 - s))


def _post_gla_fwd(o, zcat, g_head):
    def body(o_ref, zog_ref, g_ref, out_ref):
        for h in range(HEADS):
            cols = slice(h * HV, (h + 1) * HV)
            ov = o_ref[:, cols]
            r = lax.rsqrt(jnp.mean(ov * ov, axis=-1, keepdims=True) + EPS)
            act, _ = _silu_parts(zog_ref[:, cols])
            out_ref[:, cols] = (ov * r * g_ref[...] * act).astype(BF)

    tile = pl.BlockSpec((TOK_TILE, D_MODEL), lambda i: (i, 0))
    return pl.pallas_call(
        body, name="post_gla_fwd", grid=(SEQ // TOK_TILE,),
        in_specs=[tile, pl.BlockSpec((TOK_TILE, D_MODEL), lambda i: (i, C_OG // D_MODEL)), _const_spec((1, HV))],
        out_specs=tile, out_shape=jax.ShapeDtypeStruct((SEQ, D_MODEL), BF), compiler_params=_params("parallel"),
    )(o, zcat, g_head)


def _post_gla_bwd(dzcat, d_og, o, zcat, g_head):
    def body(dz_in, dog_ref, o_ref, zog_ref, g_ref, dz_ref, do_ref, dg_ref):
        del dz_in
        gpart = jnp.zeros((1, HV), F32)
        gv = g_ref[...]
        for h in range(HEADS):
            cols = slice(h * HV, (h + 1) * HV)
            ov = o_ref[:, cols]
            r = lax.rsqrt(jnp.mean(ov * ov, axis=-1, keepdims=True) + EPS)
            on = ov * r
            act, dact = _silu_parts(zog_ref[:, cols])
            dogv = dog_ref[:, cols]
            dz_ref[:, cols] = (dogv * on * gv * dact).astype(BF)
            d_on_g = dogv * act
            gpart = gpart + jnp.sum(d_on_g * on, axis=0, keepdims=True)
            dxn = d_on_g * gv
            do_ref[:, cols] = r * (dxn - on * jnp.mean(dxn * on, axis=-1, keepdims=True))

        @pl.when(pl.program_id(0) == 0)
        def _():
            dg_ref[...] = gpart

        @pl.when(pl.program_id(0) > 0)
        def _():
            dg_ref[...] += gpart

    tile = pl.BlockSpec((TOK_TILE, D_MODEL), lambda i: (i, 0))
    ogspec = pl.BlockSpec((TOK_TILE, D_MODEL), lambda i: (i, C_OG // D_MODEL))
    return pl.pallas_call(
        body, name="post_gla_bwd", grid=(SEQ // TOK_TILE,),
        in_specs=[pl.BlockSpec(memory_space=pl.ANY), tile, tile, ogspec, _const_spec((1, HV))],
        out_specs=[ogspec, tile, _const_spec((1, HV))],
        out_shape=[jax.ShapeDtypeStruct((SEQ, N_CAT), BF), jax.ShapeDtypeStruct((SEQ, D_MODEL), F32),
                   jax.ShapeDtypeStruct((1, HV), F32)],
        input_output_aliases={0: 0}, compiler_params=_params("arbitrary"),
    )(dzcat, d_og, o, zcat, g_head)


GATE_W = 2 * D_MODEL


def _mix_fwd(zcat, b_gate, y_pool, y_gla):
    def body(zg_ref, b_ref, yp_ref, yg_ref, out_ref):
        g0 = _sigmoid(zg_ref[:, :D_MODEL] + b_ref[:, :D_MODEL])
        g1 = _sigmoid(zg_ref[:, D_MODEL:] + b_ref[:, D_MODEL:])
        out_ref[...] = (g0 * yp_ref[...] + g1 * yg_ref[...]).astype(BF)

    tile = pl.BlockSpec((TOK_TILE, D_MODEL), lambda i: (i, 0))
    return pl.pallas_call(
        body, name="mix_fwd", grid=(SEQ // TOK_TILE,),
        in_specs=[pl.BlockSpec((TOK_TILE, GATE_W), lambda i: (i, C_GATE // GATE_W)), _const_spec((1, GATE_W)), tile, tile],
        out_specs=tile, out_shape=jax.ShapeDtypeStruct((SEQ, D_MODEL), BF), compiler_params=_params("parallel"),
    )(zcat, b_gate, y_pool, y_gla)


def _mix_bwd(dmixed, zcat, b_gate, y_pool, y_gla):
    def body(dm_ref, zg_ref, b_ref, yp_ref, yg_ref, dz_ref, dyp_ref, dyg_ref, db_ref):
        dm = dm_ref[...]
        g0 = _sigmoid(zg_ref[:, :D_MODEL] + b_ref[:, :D_MODEL])
        g1 = _sigmoid(zg_ref[:, D_MODEL:] + b_ref[:, D_MODEL:])
        dyp_ref[...] = (dm * g0).astype(BF)
        dyg_ref[...] = (dm * g1).astype(BF)
        dz0 = dm * yp_ref[...] * g0 * (1.0 - g0)
        dz1 = dm * yg_ref[...] * g1 * (1.0 - g1)
        dz_ref[:, :D_MODEL] = dz0.astype(BF)
        dz_ref[:, D_MODEL:] = dz1.astype(BF)
        b0 = jnp.sum(dz0, axis=0, keepdims=True)
        b1 = jnp.sum(dz1, axis=0, keepdims=True)

        @pl.when(pl.program_id(0) == 0)
        def _():
            db_ref[:, :D_MODEL] = b0
            db_ref[:, D_MODEL:] = b1

        @pl.when(pl.program_id(0) > 0)
        def _():
            db_ref[:, :D_MODEL] += b0
            db_ref[:, D_MODEL:] += b1

    tile = pl.BlockSpec((TOK_TILE, D_MODEL), lambda i: (i, 0))
    gspec = pl.BlockSpec((TOK_TILE, GATE_W), lambda i: (i, C_GATE // GATE_W))
    return pl.pallas_call(
        body, name="mix_bwd", grid=(SEQ // TOK_TILE,),
        in_specs=[tile, gspec, _const_spec((1, GATE_W)), tile, tile],
        out_specs=[gspec, tile, tile, _const_spec((1, GATE_W))],
        out_shape=[jax.ShapeDtypeStruct((SEQ, N_CAT), BF), jax.ShapeDtypeStruct((SEQ, D_MODEL), BF),
                   jax.ShapeDtypeStruct((SEQ, D_MODEL), BF), jax.ShapeDtypeStruct((1, GATE_W), F32)],
        compiler_params=_params("arbitrary"),
    )(dmixed, zcat, b_gate, y_pool, y_gla)


N_TOK_TILES = SEQ // TOK_TILE
HALO_PER_TILE = TOK_TILE // HALO


LANE_TILES = tuple((lo, min(128, FF_BLK - lo)) for lo in range(0, FF_BLK, 128))


def _taps(w_ref, b_ref, half, lanes, rows):
    shape = (rows, lanes.stop - lanes.start)
    return ([jnp.broadcast_to(w_ref[half, j:j + 1, lanes], shape) for j in range(3)],
            jnp.broadcast_to(b_ref[half, :, lanes], shape))


def _shifted(ext, row, n, lanes):
    return [ext[pl.ds(row - 2 + j, n), lanes] for j in range(3)]


def _conv_strip(u3, taps, bias):
    return bias + u3[0] * taps[0] + u3[1] * taps[1] + u3[2] * taps[2]


def _pair_specs(pairs):
    tile = pl.BlockSpec((pairs, None, TOK_TILE, FF_BLK), lambda b, i: (0, b, i, 0))
    before = pl.BlockSpec((pairs, None, HALO, FF_BLK), lambda b, i: (0, b, jnp.maximum(i * HALO_PER_TILE - 1, 0), 0))
    after = pl.BlockSpec((pairs, None, HALO, FF_BLK),
                         lambda b, i: (0, b, jnp.minimum((i + 1) * HALO_PER_TILE, SEQ // HALO - 1), 0))

    def vec(rows):
        return pl.BlockSpec((2, None, rows, FF_BLK), lambda b, i: (0, b, 0, 0))

    return tile, before, after, vec


def _fill_ext(ext, half, before_ref, tile_ref, after_ref=None):
    i = pl.program_id(1)
    ext[pl.ds(0, HALO), :] = jnp.where(i > 0, before_ref[half], 0.0)
    ext[pl.ds(HALO, TOK_TILE), :] = tile_ref[half]
    if after_ref is not None:
        ext[pl.ds(HALO + TOK_TILE, HALO), :] = after_ref[half]


def _conv_fwd(u, w_conv, b_conv):
    strip = 16

    def body(u_ref, ub_ref, w_ref, b_ref, a_ref, ext_g, ext_v):
        _fill_ext(ext_g, 0, ub_ref, u_ref)
        _fill_ext(ext_v, 1, ub_ref, u_ref)
        for lo, width in LANE_TILES:
            lanes = slice(lo, lo + width)
            taps_g, bias_g = _taps(w_ref, b_ref, 0, lanes, strip)
            taps_v, bias_v = _taps(w_ref, b_ref, 1, lanes, strip)
            for s in range(TOK_TILE // strip):
                cg = _conv_strip(_shifted(ext_g, HALO + s * strip, strip, lanes), taps_g, bias_g)
                cv = _conv_strip(_shifted(ext_v, HALO + s * strip, strip, lanes), taps_v, bias_v)
                a_ref[0, s * strip:(s + 1) * strip, lanes] = (cg * _sigmoid(cg) * cv).astype(BF)

    tile, before, _, vec = _pair_specs(2)
    out_tile, _, _, _ = _pair_specs(1)
    return pl.pallas_call(
        body, name="conv_fwd", grid=(4, N_TOK_TILES), in_specs=[tile, before, vec(3), vec(1)],
        out_specs=out_tile, out_shape=jax.ShapeDtypeStruct((1, 4, SEQ, FF_BLK), BF),
        scratch_shapes=[pltpu.VMEM((HALO + TOK_TILE, FF_BLK), F32)] * 2, compiler_params=_params("parallel", "parallel"),
    )(u, u, w_conv, b_conv)


def _conv_bwd(u, da, w_conv, b_conv):
    strip = 8
    n_strips = TOK_TILE // strip

    def body(u_ref, ub_ref, ua_ref, da_ref, daa_ref, w_ref, b_ref, du_ref, dw_ref, db_ref, ext_g, ext_v, ext_dg, ext_dv):
        i = pl.program_id(1)
        _fill_ext(ext_g, 0, ub_ref, u_ref, ua_ref)
        _fill_ext(ext_v, 1, ub_ref, u_ref, ua_ref)

        @pl.when(i == 0)
        def _():
            dw_ref[...] = jnp.zeros_like(dw_ref)
            db_ref[...] = jnp.zeros_like(db_ref)

        for lo, width in LANE_TILES:
            lanes = slice(lo, lo + width)
            taps = [_taps(w_ref, b_ref, half, lanes, strip) for half in range(2)]
            exts_u, exts_d = (ext_g, ext_v), (ext_dg, ext_dv)
            acc_w = [[jnp.zeros((strip, width), F32) for _ in range(3)] for _ in range(2)]
            acc_b = [jnp.zeros((strip, width), F32) for _ in range(2)]
            da_pair, pending = None, [None, None]
            for s in range(n_strips + 1):
                u3 = [_shifted(exts_u[half], HALO + s * strip, strip, lanes) for half in range(2)]
                cg = _conv_strip(u3[0], *taps[0])
                cv = _conv_strip(u3[1], *taps[1])
                act, dact = _silu_parts(cg)
                if s == n_strips:
                    da = jnp.where(i < N_TOK_TILES - 1, daa_ref[0, :, lanes].astype(F32), 0.0)
                elif s % 2 == 0:
                    da_pair = da_ref[0, s * strip:(s + 2) * strip, lanes].astype(F32)
                    da = da_pair[:strip]
                else:
                    da = da_pair[strip:]
                dc = (da * cv * dact, da * act)
                for half in range(2):
                    exts_d[half][pl.ds(s * strip, strip), lanes] = dc[half]
                    if s < n_strips:
                        for j in range(3):
                            acc_w[half][j] = acc_w[half][j] + dc[half] * u3[half][j]
                        acc_b[half] = acc_b[half] + dc[half]
                    if s >= 1:
                        p = (s - 1) * strip
                        w3 = taps[half][0]
                        du = (exts_d[half][pl.ds(p, strip), lanes] * w3[2] + exts_d[half][pl.ds(p + 1, strip), lanes] * w3[1]
                              + exts_d[half][pl.ds(p + 2, strip), lanes] * w3[0])
                        if (s - 1) % 2 == 0:
                            pending[half] = du
                        else:
                            du_ref[half, p - strip:p + strip, lanes] = jnp.concatenate([pending[half], du], axis=0).astype(BF)
            for half in range(2):
                for j in range(3):
                    dw_ref[half, j:j + 1, lanes] += jnp.sum(acc_w[half][j], axis=0, keepdims=True)
                db_ref[half, :, lanes] += jnp.sum(acc_b[half], axis=0, keepdims=True)

    tile, before, after, vec = _pair_specs(2)
    da_tile, _, da_after_spec, _ = _pair_specs(1)
    return pl.pallas_call(
        body, name="conv_bwd", grid=(4, N_TOK_TILES),
        in_specs=[tile, before, after, da_tile, da_after_spec, vec(3), vec(1)],
        out_specs=[tile, vec(3), vec(1)],
        out_shape=[jax.ShapeDtypeStruct((2, 4, SEQ, FF_BLK), BF), jax.ShapeDtypeStruct((2, 4, 3, FF_BLK), F32),
                   jax.ShapeDtypeStruct((2, 4, 1, FF_BLK), F32)],
        scratch_shapes=[pltpu.VMEM((2 * HALO + TOK_TILE, FF_BLK), F32)] * 4,
        compiler_params=_params("parallel", "arbitrary"),
    )(u, u, u, da, da, w_conv, b_conv)


ANY = pl.BlockSpec(memory_space=pl.ANY)


def _place():
    x, y, c = lax.axis_index("x"), lax.axis_index("y"), lax.axis_index("c")
    other_chips = [(1 - x, y), (x, 1 - y), (1 - x, 1 - y)]
    return x, y, c, other_chips


def _all_gather(shards, name):
    n = len(shards)

    def body(*refs):
        src, out = refs[:n], refs[n:2 * n]
        send_sems, recv_sems, local_sems = refs[2 * n:]
        x, y, c, chips = _place()
        me, sibling = (x, y, c), (x, y, 1 - c)

        def copy(a, k, block, to, own=False):
            dst = out[a].at[4 * block[0] + 2 * block[1] + block[2]]
            return pltpu.make_async_remote_copy(src_ref=src[a] if own else dst, dst_ref=dst, send_sem=send_sems.at[a, k],
                                                recv_sem=recv_sems.at[a, k], device_id=to, device_id_type=MESH)

        mine = [pltpu.make_async_copy(src[a], out[a].at[4 * x + 2 * y + c], local_sems.at[a]) for a in range(n)]
        first = []
        for a in range(n):
            mine[a].start()
            first.append(copy(a, 0, me, sibling, own=True))
            first += [copy(a, 1 + j, me, (*chip, c), own=True) for j, chip in enumerate(chips)]
        for cp in first:
            cp.start()
        passed = []
        for j, chip in enumerate(chips):
            for a in range(n):
                copy(a, 1 + j, (*chip, c), me).wait_recv()
                passed.append(copy(a, 4 + j, (*chip, c), sibling))
                passed[-1].start()
        for a in range(n):
            copy(a, 0, sibling, me).wait_recv()
            for j, chip in enumerate(chips):
                copy(a, 4 + j, (*chip, 1 - c), me).wait_recv()
        for cp in first + passed:
            cp.wait_send()
        for cp in mine:
            cp.wait()

    return pl.pallas_call(
        body, name=name, in_specs=[ANY] * n, out_specs=[ANY] * n,
        out_shape=[jax.ShapeDtypeStruct((N_DEV,) + s.shape, s.dtype) for s in shards],
        scratch_shapes=[pltpu.SemaphoreType.DMA((n, 7)), pltpu.SemaphoreType.DMA((n, 7)), pltpu.SemaphoreType.DMA((n,))],
    )(*shards)


SEM = pl.BlockSpec(memory_space=pltpu.SEMAPHORE)
IN_HBM = pl.BlockSpec(memory_space=pltpu.HBM)
SPLIT_PARAMS = pltpu.CompilerParams(has_side_effects=pltpu.SideEffectType.DATAFLOW_SIDE_EFFECTING)


def _gather_first(refs, send_sems, recv_sems):
    x, y, c, chips = _place()
    targets = [(x, y, 1 - c)] + [(px, py, c) for px, py in chips]
    return [pltpu.make_async_remote_copy(src_ref=refs[2 * a], dst_ref=refs[2 * a + 1].at[4 * x + 2 * y + c],
                                         send_sem=send_sems.at[4 * a + k], recv_sem=recv_sems.at[4 * a + k],
                                         device_id=to, device_id_type=MESH)
            for a in range(len(refs) // 2) for k, to in enumerate(targets)]


def _gather_second(refs, send_sems, recv_sems):
    x, y, c, chips = _place()
    copies = []
    for a, land in enumerate(refs):
        for j, (px, py) in enumerate(chips):
            block = land.at[4 * px + 2 * py + c]
            copies.append(pltpu.make_async_remote_copy(src_ref=block, dst_ref=block, send_sem=send_sems.at[3 * a + j],
                                                       recv_sem=recv_sems.at[3 * a + j], device_id=(x, y, 1 - c),
                                                       device_id_type=MESH))
    return copies


def _reduce_first(refs, send_sems, recv_sems):
    x, y, c, _ = _place()
    return [pltpu.make_async_remote_copy(src_ref=refs[2 * a].at[j, 1 - c], dst_ref=refs[2 * a + 1].at[j],
                                         send_sem=send_sems.at[4 * a + j], recv_sem=recv_sems.at[4 * a + j],
                                         device_id=(x, y, 1 - c), device_id_type=MESH)
            for a in range(len(refs) // 2) for j in range(4)]


def _reduce_second(refs, send_sems, recv_sems):
    _, _, c, chips = _place()
    return [pltpu.make_async_remote_copy(src_ref=refs[2 * a].at[2 * px + py], dst_ref=refs[2 * a + 1].at[k],
                                         send_sem=send_sems.at[3 * a + k], recv_sem=recv_sems.at[3 * a + k],
                                         device_id=(px, py, c), device_id_type=MESH)
            for a in range(len(refs) // 2) for k, (px, py) in enumerate(chips)]


def _split_start(name, groups):
    arrays = [a for g in groups for a in g[0]]
    n = len(arrays)

    def body(*refs):
        sems = refs[n:n + 2 * len(groups)]
        at = 0
        for gi, (members, _, build) in enumerate(groups):
            for cp in build(refs[at:at + len(members)], sems[2 * gi], sems[2 * gi + 1]):
                cp.start()
            at += len(members)
        refs[-1][...] = jnp.zeros_like(refs[-1])

    sem_shapes = [pltpu.SemaphoreType.DMA((g[1],)) for g in groups for _ in range(2)]
    outs = pl.pallas_call(
        body, name=name, in_specs=[IN_HBM] * n,
        out_shape=(*sem_shapes, *[pltpu.HBM(a.shape, a.dtype) for a in arrays], jax.ShapeDtypeStruct((8, 128), F32)),
        out_specs=(*[SEM] * len(sem_shapes), *[IN_HBM] * n, pl.BlockSpec(memory_space=pltpu.VMEM)),
        input_output_aliases={i: len(sem_shapes) + i for i in range(n)}, compiler_params=SPLIT_PARAMS,
    )(*[pltpu.with_memory_space_constraint(a, pltpu.HBM) for a in arrays])
    per_group, at = [], len(sem_shapes)
    for gi, (members, _, _) in enumerate(groups):
        per_group.append((outs[2 * gi], outs[2 * gi + 1], list(outs[at:at + len(members)])))
        at += len(members)
    return per_group, outs[-1]


def _split_wait(name, started, build, after):
    send_sems, recv_sems, arrays = started
    n = len(arrays)

    def body(*refs):
        for cp in build(refs[:n], refs[n], refs[n + 1]):
            cp.wait_send()
            cp.wait_recv()

    return pl.pallas_call(
        body, name=name, in_specs=[IN_HBM] * n + [SEM, SEM, ANY],
        out_shape=tuple(pltpu.HBM(a.shape, a.dtype) for a in arrays), out_specs=tuple([IN_HBM] * n),
        input_output_aliases={i: i for i in range(n)}, compiler_params=SPLIT_PARAMS,
    )(*arrays, send_sems, recv_sems, after)


def _gather_landing(shard, me):
    return lax.dynamic_update_slice(lax.empty((N_DEV,) + shard.shape, shard.dtype), shard[None],
                                    (me,) + (0,) * shard.ndim)


def _tile_2d(rows, cols):
    for t in (256, 176, 128):
        if rows % t == 0:
            return t, cols
    return rows, 256


def _pair_sum(part, recv, core, name):
    _, rows, cols = recv.shape
    tr, tc = _tile_2d(rows, cols)

    def body(c_ref, p_ref, r_ref, o_ref):
        del c_ref
        o_ref[...] = (p_ref[...].astype(F32) + r_ref[...].astype(F32)).astype(BF)

    grid_spec = pltpu.PrefetchScalarGridSpec(
        num_scalar_prefetch=1, grid=(4, rows // tr, cols // tc),
        in_specs=[pl.BlockSpec((None, None, tr, tc), lambda j, i, k, c_ref: (j, c_ref[0], i, k)),
                  pl.BlockSpec((None, tr, tc), lambda j, i, k, c_ref: (j, i, k))],
        out_specs=pl.BlockSpec((None, tr, tc), lambda j, i, k, c_ref: (j, i, k)))
    return pl.pallas_call(
        body, name=name, grid_spec=grid_spec, out_shape=jax.ShapeDtypeStruct(recv.shape, BF),
        compiler_params=_params("parallel", "parallel", "parallel"),
    )(core, part, recv)


def _adamw(w, g, m, v):
    m = ADAM_B1 * m + (1.0 - ADAM_B1) * g
    v = ADAM_B2 * v + (1.0 - ADAM_B2) * (g * g)
    delta = -ADAM_LR * ((m / ADAM_C1) / (jnp.sqrt(v / ADAM_C2) + ADAM_EPS) + ADAM_WD * w)
    return delta, m, v


def _chip_sum_adamw(sums, recv, w, m, v, chip, name):
    rows, cols = w.shape
    tr, tc = _tile_2d(rows, cols)

    def body(chip_ref, s_ref, r_ref, w_ref, m_ref, v_ref, g_out, d_out, m_out, v_out):
        del chip_ref
        g = s_ref[...].astype(F32)
        for k in range(3):
            g = g + r_ref[k].astype(F32)
        g_out[...] = g
        d_out[...], m_out[...], v_out[...] = _adamw(w_ref[...], g, m_ref[...], v_ref[...])

    tile = pl.BlockSpec((tr, tc), lambda i, k, chip_ref: (i, k))
    grid_spec = pltpu.PrefetchScalarGridSpec(
        num_scalar_prefetch=1, grid=(rows // tr, cols // tc),
        in_specs=[pl.BlockSpec((None, tr, tc), lambda i, k, chip_ref: (chip_ref[0], i, k)),
                  pl.BlockSpec((3, tr, tc), lambda i, k, chip_ref: (0, i, k)), tile, tile, tile],
        out_specs=[tile] * 4)
    return pl.pallas_call(
        body, name=name, grid_spec=grid_spec, out_shape=[jax.ShapeDtypeStruct((rows, cols), F32)] * 4,
        compiler_params=_params("parallel", "parallel"),
    )(chip, sums, recv, w, m, v)


def _sum8_adamw(parts, w, m, v):
    rows = w.shape[0]

    def body(p_ref, w_ref, m_ref, v_ref, g_out, d_out, m_out, v_out):
        g = p_ref[0]
        for d in range(1, N_DEV):
            g = g + p_ref[d]
        g_out[...] = g
        d_out[...], m_out[...], v_out[...] = _adamw(w_ref[...], g, m_ref[...], v_ref[...])

    full = _const_spec((rows, 128))
    return pl.pallas_call(
        body, name="small_sum_adamw", grid=(1,), in_specs=[_const_spec((N_DEV, rows, 128)), full, full, full],
        out_specs=[full] * 4, out_shape=[jax.ShapeDtypeStruct((rows, 128), F32)] * 4,
        compiler_params=_params("arbitrary"),
    )(parts, w, m, v)


def _plain_adamw(g, w, m, v):
    rows = w.shape[0]

    def body(g_ref, w_ref, m_ref, v_ref, d_out, m_out, v_out):
        d_out[...], m_out[...], v_out[...] = _adamw(w_ref[...], g_ref[...], m_ref[...], v_ref[...])

    full = _const_spec((rows, 128))
    return pl.pallas_call(
        body, name="shard_adamw", grid=(1,), in_specs=[full] * 4, out_specs=[full] * 3,
        out_shape=[jax.ShapeDtypeStruct((rows, 128), F32)] * 3, compiler_params=_params("arbitrary"),
    )(g, w, m, v)


def _pack(arrays, rows):
    flat = jnp.concatenate([a.reshape(-1) for a in arrays])
    return jnp.pad(flat, (0, rows * 128 - flat.shape[0])).reshape(rows, 128)


def _unpack(packed, shapes):
    flat = packed.reshape(-1)
    out, at = [], 0
    for s in shapes:
        size = 1
        for d in s:
            size *= d
        out.append(flat[at:at + size].reshape(s))
        at += size
    return out


MM_TILE = 512
N_MM_TILES = SEQ // MM_TILE
CAT_TILE = 512
N_CAT_TILES = N_CAT // CAT_TILE
SMALL_ROWS = 808
SHARD_ROWS = 32


def kernel(x, g_mix, w_in, b_gate, w_gk_up, b_gk, w_pool_grp, pool_scale, g_gla_head, w_pool_proj, w_gla_proj, w_out, g_ffn, w_up, w_conv, b_conv, w_down, g_final, loss_target, m_g_mix, m_w_in, m_b_gate, m_w_gk_up, m_b_gk, m_w_pool_grp, m_pool_scale, m_g_gla_head, m_w_pool_proj, m_w_gla_proj, m_w_out, m_g_ffn, m_w_up, m_w_conv, m_b_conv, m_w_down, m_g_final, v_g_mix, v_w_in, v_b_gate, v_w_gk_up, v_b_gk, v_w_pool_grp, v_pool_scale, v_g_gla_head, v_w_pool_proj, v_w_gla_proj, v_w_out, v_g_ffn, v_w_up, v_w_conv, v_b_conv, v_w_down, v_g_final):
    xi, yi, ci = lax.axis_index("x"), lax.axis_index("y"), lax.axis_index("c")
    me = 4 * xi + 2 * yi + ci
    core = jnp.reshape(ci, (1,)).astype(jnp.int32)
    chip = jnp.reshape(2 * xi + yi, (1,)).astype(jnp.int32)
    xs, target = x[0], loss_target[0]

    big = dict(w_in=w_in[0].T, w_pool_proj=w_pool_proj[0], w_gla_proj=w_gla_proj[0], w_out=w_out[0], w_up=w_up[0].T,
               w_down=w_down[0])
    moments = dict(w_in=(m_w_in[0].T, v_w_in[0].T), w_pool_proj=(m_w_pool_proj[0], v_w_pool_proj[0]),
                   w_gla_proj=(m_w_gla_proj[0], v_w_gla_proj[0]), w_out=(m_w_out[0], v_w_out[0]),
                   w_up=(m_w_up[0].T, v_w_up[0].T), w_down=(m_w_down[0], v_w_down[0]))
    names = list(big)
    shards = {k: big[k].astype(BF) for k in names}
    shards["w_gk_up"], shards["w_conv"] = w_gk_up[0], w_conv[0]
    gather_groups = (("w_in", "w_gk_up"), ("w_pool_proj", "w_gla_proj", "w_out"), ("w_up", "w_down", "w_conv"))
    started, token = _split_start("gather_start", [
        ([t for k in g for t in (shards[k], _gather_landing(shards[k], me))], 4 * len(g), _gather_first)
        for g in gather_groups])

    def gather_pass(gi, after):
        lands = list(_split_wait(f"gather_wait_{gi}", started[gi], _gather_first, after)[1::2])
        passed, _ = _split_start(f"gather_pass_{gi}", [(lands, 3 * len(lands), _gather_second)])
        return passed[0]

    def gather_done(gi, passed, after):
        return dict(zip(gather_groups[gi], _split_wait(f"gather_pass_wait_{gi}", passed, _gather_second, after)))

    tok = lambda i, j, k: (i, 0)
    whole = lambda i, j, k: (0, 0)
    kblk = lambda i, j, k: (k, 0)
    ff_tile = (None, None, MM_TILE, FF_BLK)
    ff_seq = (None, None, SEQ, FF_BLK)

    h = _rms_fwd(xs, g_mix + token[:1, :1], "rms_mix")
    wg = gather_done(0, gather_pass(0, h), h)
    wt_in = wg["w_in"].reshape(IN_TOTAL, D_MODEL)
    wt_cat = jnp.concatenate([wt_in[R_QKV:R_OG], wt_in[R_GATE:], wt_in[R_OG:R_GK], wt_in[R_POOL:R_QKV]], axis=0)
    wt_gk = jnp.pad(wt_in[R_GK:R_GATE], ((0, GK_PAD - GATE_RANK), (0, 0)))
    wgk_pad = jnp.pad(wg["w_gk_up"].transpose(1, 0, 2).reshape(GATE_RANK, GLA_DK), ((0, GK_PAD - GATE_RANK), (0, 0)))
    zcat = _mm(h, wt_cat, out_shape=(SEQ, N_CAT), out_dtype=F32, grid=(N_CAT_TILES, 1, 1),
               blk_a=(SEQ, D_MODEL), blk_b=(CAT_TILE, D_MODEL), blk_o=(SEQ, CAT_TILE),
               map_a=whole, map_b=lambda j, i, k: (j, 0), map_o=lambda j, i, k: (0, j), tb=True, name="mm_in")
    la = _gk_fwd(h, wt_gk, wgk_pad, b_gk)
    passed = gather_pass(1, la)
    o, states = _gla_fwd(zcat, la)
    wg = gather_done(1, passed, o)
    wpp = wg["w_pool_proj"].transpose(1, 0, 2).reshape(POOL_WIDTH, D_MODEL)
    wgp = wg["w_gla_proj"].reshape(D_MODEL, D_MODEL)
    wout = wg["w_out"].reshape(D_MODEL, D_MODEL)
    og = _post_gla_fwd(o, zcat, g_gla_head)
    passed = gather_pass(2, og)
    ps = _pool_fwd(zcat, w_pool_grp[0], pool_scale)
    y_pool = _mm(ps, wpp, out_shape=(SEQ, D_MODEL), out_dtype=F32, grid=(N_MM_TILES, 1, 1),
                 blk_a=(MM_TILE, POOL_WIDTH), blk_b=(POOL_WIDTH, D_MODEL), blk_o=(MM_TILE, D_MODEL),
                 map_a=tok, map_b=whole, map_o=tok, name="mm_pool_proj")
    sq = dict(out_shape=(SEQ, D_MODEL), grid=(N_MM_TILES, 1, 1), blk_a=(MM_TILE, D_MODEL), blk_b=(D_MODEL, D_MODEL),
              blk_o=(MM_TILE, D_MODEL), map_a=tok, map_b=whole, map_o=tok)
    y_gla = _mm(og, wgp, out_dtype=F32, name="mm_gla_proj", **sq)
    mixed = _mix_fwd(zcat, b_gate, y_pool, y_gla)
    x1 = _mm(mixed, wout, out_dtype=F32, res=xs, name="mm_out", **sq)
    h2 = _rms_fwd(x1, g_ffn, "rms_ffn")
    wg = gather_done(2, passed, h2)
    wt_up = wg["w_up"].reshape(2 * D_FF, D_MODEL)
    wdown = wg["w_down"].reshape(D_FF, D_MODEL)
    wconv4 = wg["w_conv"].reshape(2, 4, 3, FF_BLK)
    bconv4 = b_conv.reshape(2, 4, 1, FF_BLK)
    blk4 = lambda b, i, k: (b // 4, b % 4, 0, 0)
    u4 = _mm(h2, wt_up, out_shape=(2, 4, SEQ, FF_BLK), out_dtype=F32, grid=(N_DEV, 1, 1),
             blk_a=(SEQ, D_MODEL), blk_b=(FF_BLK, D_MODEL), blk_o=ff_seq,
             map_a=whole, map_b=lambda b, i, k: (b, 0), map_o=blk4, tb=True, name="mm_up")
    act = _conv_fwd(u4, wconv4, bconv4)
    x2 = _mm(act, wdown, out_shape=(SEQ, D_MODEL), out_dtype=F32, grid=(N_MM_TILES, 1, 4),
             blk_a=ff_tile, blk_b=(FF_BLK, D_MODEL), blk_o=(MM_TILE, D_MODEL),
             map_a=lambda i, j, k: (0, k, i, 0), map_b=kblk, map_o=tok, res=x1, name="mm_down")
    loss_part, dx2, dg_final = _final_loss(x2, g_final.reshape(1, D_MODEL), target)

    da = _mm(dx2, wdown, out_shape=(1, 4, SEQ, FF_BLK), out_dtype=BF, grid=(4, N_MM_TILES, 1),
             blk_a=(MM_TILE, D_MODEL), blk_b=(FF_BLK, D_MODEL), blk_o=ff_tile,
             map_a=lambda b, i, k: (i, 0), map_b=lambda b, i, k: (b, 0), map_o=lambda b, i, k: (0, b, i, 0),
             tb=True, name="mm_d_act")
    d_wdown = _mm(act, dx2, out_shape=(D_FF, D_MODEL), out_dtype=BF, grid=(4, 1, N_MM_TILES),
                  blk_a=ff_tile, blk_b=(MM_TILE, D_MODEL), blk_o=(FF_BLK, D_MODEL),
                  map_a=lambda b, j, k: (0, b, k, 0), map_b=kblk, map_o=lambda b, j, k: (b, 0),
                  ta=True, name="mm_d_wdown")
    du4, d_wconv, d_bconv = _conv_bwd(u4, da, wconv4, bconv4)
    dh2 = _mm(du4, wt_up, out_shape=(SEQ, D_MODEL), out_dtype=F32, grid=(1, 1, N_DEV),
              blk_a=ff_seq, blk_b=(FF_BLK, D_MODEL), blk_o=(SEQ, D_MODEL),
              map_a=lambda i, j, k: (k // 4, k % 4, 0, 0), map_b=kblk, map_o=whole, name="mm_d_h2")
    d_wt_up = _mm(du4, h2, out_shape=(2 * D_FF, D_MODEL), out_dtype=BF, grid=(N_DEV, 1, 1),
                  blk_a=ff_seq, blk_b=(SEQ, D_MODEL), blk_o=(FF_BLK, D_MODEL),
                  map_a=blk4, map_b=whole, map_o=lambda b, i, k: (b, 0), ta=True, name="mm_d_wup")
    res = {}

    def reduce_start(keys, parts):
        arrays = [t for k in keys for t in (parts[k], lax.empty((4,) + parts[k].shape[2:], BF))]
        st, tkn = _split_start("reduce_start_" + keys[0], [(arrays, 4 * len(keys), _reduce_first)])
        return st[0], tkn

    def reduce_cross(keys, st, after):
        arrays = _split_wait("reduce_wait_" + keys[0], st, _reduce_first, after)
        sums = [_pair_sum(p, r, core, "pair_sum_" + k) for k, p, r in zip(keys, arrays[0::2], arrays[1::2])]
        arrays = [t for s in sums for t in (s, lax.empty((3,) + s.shape[1:], BF))]
        st2, tkn = _split_start("reduce_cross_" + keys[0], [(arrays, 3 * len(keys), _reduce_second)])
        return st2[0], tkn

    def reduce_done(keys, st2, after):
        arrays = _split_wait("reduce_cross_wait_" + keys[0], st2, _reduce_second, after)
        for k, s, r in zip(keys, arrays[0::2], arrays[1::2]):
            outs = _chip_sum_adamw(s, r, big[k], moments[k][0], moments[k][1], chip, "adamw_" + k)
            res[k] = [(t.T if k in ("w_in", "w_up") else t)[None] for t in outs]

    ffn_keys = ("w_down", "w_up")
    ffn_red, tkn = reduce_start(ffn_keys, dict(w_down=d_wdown.reshape(4, 2, D_FF // N_DEV, D_MODEL),
                                               w_up=d_wt_up.reshape(4, 2, FF_BLK, D_MODEL)))
    dx1, dg_ffn = _rms_bwd(dh2, x1, g_ffn + tkn[:1, :1], dx2, "rms_ffn_bwd")

    sq_t = dict(out_shape=(D_MODEL, D_MODEL), grid=(1, 1, N_MM_TILES), blk_a=(MM_TILE, D_MODEL),
                blk_b=(MM_TILE, D_MODEL), blk_o=(D_MODEL, D_MODEL), map_a=kblk, map_b=kblk, map_o=whole, ta=True)
    dmixed = _mm(dx1, wout, out_dtype=F32, tb=True, name="mm_d_mixed", **sq)
    d_wout = _mm(mixed, dx1, out_dtype=BF, name="mm_d_wout", **sq_t)
    dzcat, dy_pool, dy_gla, db_gate = _mix_bwd(dmixed, zcat, b_gate, y_pool, y_gla)
    ffn_red, _ = reduce_cross(ffn_keys, ffn_red, db_gate)
    d_og =_mm(dy_gla, wgp, out_dtype=F32, tb=True, name="mm_d_og", **sq)
    d_wgp = _mm(og, dy_gla, out_dtype=BF, name="mm_d_wgp", **sq_t)
    mix_keys = ("w_out", "w_gla_proj")
    mix_red, tkn = reduce_start(mix_keys, dict(w_out=d_wout.reshape(4, 2, D_MODEL // N_DEV, D_MODEL),
                                               w_gla_proj=d_wgp.reshape(4, 2, D_MODEL // N_DEV, D_MODEL)))
    dzcat, d_o, dg_head = _post_gla_bwd(dzcat, d_og, o, zcat, g_gla_head + tkn[:1, :1])
    dzcat, dla = _gla_bwd(dzcat, zcat, la, d_o, states)
    mix_red, _ = reduce_cross(mix_keys, mix_red, dla)
    dh_gk, d_wt_gk, d_wgk, db_gk = _gk_bwd(dla, h, wt_gk, wgk_pad, b_gk)
    dps = _mm(dy_pool, wpp, out_shape=(SEQ, POOL_WIDTH), out_dtype=F32, grid=(N_MM_TILES, 1, 1),
              blk_a=(MM_TILE, D_MODEL), blk_b=(POOL_WIDTH, D_MODEL), blk_o=(MM_TILE, POOL_WIDTH),
              map_a=tok, map_b=whole, map_o=tok, tb=True, name="mm_d_ps")
    d_wpp = _mm(ps, dy_pool, out_shape=(POOL_WIDTH, D_MODEL), out_dtype=F32, grid=(1, 1, N_MM_TILES),
                blk_a=(MM_TILE, POOL_WIDTH), blk_b=(MM_TILE, D_MODEL), blk_o=(POOL_WIDTH, D_MODEL),
                map_a=kblk, map_b=kblk, map_o=whole, ta=True, name="mm_d_wpp")
    dzcat, d_wgrp, d_scale = _pool_bwd(dzcat, zcat, dps, w_pool_grp[0], pool_scale)
    d_wt_cat = _mm(dzcat, h, out_shape=(N_CAT, D_MODEL), out_dtype=BF, grid=(N_CAT_TILES, 1, 1),
                   blk_a=(SEQ, CAT_TILE), blk_b=(SEQ, D_MODEL), blk_o=(CAT_TILE, D_MODEL),
                   map_a=lambda j, i, k: (0, j), map_b=whole, map_o=lambda j, i, k: (j, 0), ta=True, name="mm_d_wcat")
    d_wt_in = jnp.concatenate([d_wt_cat[C_POOL:], d_wt_cat[C_QKV:C_GATE], d_wt_cat[C_OG:C_POOL],
                               d_wt_gk[:GATE_RANK].astype(BF), d_wt_cat[C_GATE:C_OG]], axis=0)
    in_keys = ("w_in", "w_pool_proj")
    in_red, tkn = reduce_start(in_keys, dict(
        w_in=d_wt_in.reshape(4, 2, IN_SHARD, D_MODEL),
        w_pool_proj=d_wpp.reshape(POOL_WIDTH, N_DEV, D_MODEL // N_DEV).transpose(1, 0, 2).astype(BF)
        .reshape(4, 2, POOL_WIDTH, D_MODEL // N_DEV)))
    dh = _mm(dzcat, wt_cat, out_shape=(SEQ, D_MODEL), out_dtype=F32, grid=(1, 1, N_CAT_TILES),
             blk_a=(SEQ, CAT_TILE), blk_b=(CAT_TILE, D_MODEL), blk_o=(SEQ, D_MODEL),
             map_a=lambda i, j, k: (0, k), map_b=kblk, map_o=whole, res=dh_gk, name="mm_d_h")
    in_red, tkn = reduce_cross(in_keys, in_red, dh)
    grad_x, dg_mix = _rms_bwd(dh, xs, g_mix + tkn[:1, :1], dx1, "rms_mix_bwd")
    reduce_done(ffn_keys, ffn_red, grad_x)
    reduce_done(mix_keys, mix_red, res["w_down"][0])

    small = [("g_mix", dg_mix, g_mix, m_g_mix, v_g_mix), ("b_gate", db_gate, b_gate, m_b_gate, v_b_gate),
             ("w_gk_up", d_wgk[:GATE_RANK], None, None, None), ("b_gk", db_gk, b_gk, m_b_gk, v_b_gk),
             ("w_pool_grp", d_wgrp, w_pool_grp, m_w_pool_grp, v_w_pool_grp),
             ("pool_scale", d_scale, pool_scale, m_pool_scale, v_pool_scale),
             ("g_gla_head", dg_head, g_gla_head, m_g_gla_head, v_g_gla_head), ("g_ffn", dg_ffn, g_ffn, m_g_ffn, v_g_ffn),
             ("w_conv", d_wconv, None, None, None), ("b_conv", d_bconv, b_conv, m_b_conv, v_b_conv),
             ("g_final", dg_final, g_final, m_g_final, v_g_final), ("loss", loss_part, None, None, None)]
    zeros_like_part = lambda t: jnp.zeros(t[1].shape, F32)
    g_all = _all_gather([_pack([t[1] for t in small], SMALL_ROWS)], "gather_small_grads")[0]
    packed = _sum8_adamw(g_all, *[_pack([zeros_like_part(t) if t[i] is None else t[i] for t in small], SMALL_ROWS)
                                  for i in (2, 3, 4)])
    shapes = [t[1].shape if t[2] is None else t[2].shape for t in small]
    unpacked = [_unpack(p, shapes) for p in packed]
    for idx, t in enumerate(small):
        if t[2] is not None:
            res[t[0]] = [unpacked[q][idx] for q in range(4)]
    g_wgk = lax.dynamic_slice(unpacked[0][2], (0, me * (GLA_DK // N_DEV)), (GATE_RANK, GLA_DK // N_DEV))
    g_wconv = lax.dynamic_index_in_dim(unpacked[0][8].reshape(N_DEV, 3, FF_BLK), me, axis=0, keepdims=False)
    shard_shapes = [(1, GATE_RANK, GLA_DK // N_DEV), (1, 3, FF_BLK)]
    shard_out = _plain_adamw(_pack([g_wgk, g_wconv], SHARD_ROWS), _pack([w_gk_up, w_conv], SHARD_ROWS),
                             _pack([m_w_gk_up, m_w_conv], SHARD_ROWS), _pack([v_w_gk_up, v_w_conv], SHARD_ROWS))
    shard_un = [_unpack(p, shard_shapes) for p in shard_out]
    res["w_gk_up"] = [g_wgk[None]] + [s[0] for s in shard_un]
    res["w_conv"] = [g_wconv[None]] + [s[1] for s in shard_un]

    reduce_done(in_keys, in_red, shard_out[0])
    loss = unpacked[0][11][0, 0]
    order =["g_mix", "w_in", "b_gate", "w_gk_up", "b_gk", "w_pool_grp", "pool_scale", "g_gla_head", "w_pool_proj",
             "w_gla_proj", "w_out", "g_ffn", "w_up", "w_conv", "b_conv", "w_down", "g_final"]
    return (loss, grad_x[None], *[res[k][0] for k in order], *[res[k][1] for k in order],
            *[res[k][2] for k in order], *[res[k][3] for k in order])
```

```python
import functools

import jax
import jax.numpy as jnp
from jax import lax
from jax.experimental import pallas as pl
from jax.experimental.pallas import tpu as pltpu

F32 = jnp.float32
BF = jnp.bfloat16
HIGHEST = lax.Precision.HIGHEST
MESH = pl.DeviceIdType.MESH

N_DEV = 8
SEQ = 2048
D_MODEL = 1024
CHUNK = 64
EPS = 1e-6
POOL_WIDTH = 512
POOL_WINDOWS = (2, 4, 8, 16)
POOL_GD = 128
POOL_HALO = 16
HEADS = 4
HK = 128
HV = 256
GLA_DK = 512
GATE_RANK = 16
GATE_NORM = 16.0
D_FF = 2816
FF_BLK = 704
IN_TOTAL = 5648
IN_SHARD = 706
C_QKV, C_GATE, C_OG, C_POOL = 0, 2048, 4096, 5120
N_CAT = 5632
R_POOL, R_QKV, R_OG, R_GK, R_GATE = 0, 512, 2560, 3584, 3600
GK_PAD = 128

ADAM_LR, ADAM_B1, ADAM_B2, ADAM_EPS, ADAM_WD, ADAM_STEP = 0.001, 0.9, 0.999, 1e-08, 0.01, 10
ADAM_C1 = 1.0 - ADAM_B1 ** ADAM_STEP
ADAM_C2 = 1.0 - ADAM_B2 ** ADAM_STEP

VMEM_BYTES_V7X = 64 * 1024 * 1024
VMEM_LIMIT = 48 * 1024 * 1024

TOK_TILE = 256
HALO = 8
GLA_CPS = 4


def _params(*sem):
    return pltpu.CompilerParams(dimension_semantics=sem, vmem_limit_bytes=VMEM_LIMIT)


def _const_spec(shape):
    nd = len(shape)
    return pl.BlockSpec(shape, lambda *_: (0,) * nd)


def _dot(a, b, ta=False, tb=False):
    dims = (((0 if ta else 1,), (1 if tb else 0,)), ((), ()))
    return lax.dot_general(a.astype(BF), b.astype(BF), dims, preferred_element_type=F32)


def _dot_exact(a, b):
    return jnp.dot(a, b, precision=HIGHEST, preferred_element_type=F32)


def _sigmoid(x):
    return 0.5 * jnp.tanh(0.5 * x) + 0.5


def _mm(a, b, *, out_shape, out_dtype, grid, blk_a, blk_b, blk_o, map_a, map_b, map_o, ta=False, tb=False,
        res=None, name):
    gk = grid[2]

    def body(*refs):
        if res is None:
            a_ref, b_ref, o_ref = refs[:3]
            r_ref = None
            scr = refs[3:]
        else:
            a_ref, b_ref, r_ref, o_ref = refs[:4]
            scr = refs[4:]
        prod = _dot(a_ref[...], b_ref[...], ta, tb)

        def finish(total):
            if r_ref is not None:
                total = total + r_ref[...]
            o_ref[...] = total.astype(out_dtype)

        if gk == 1:
            finish(prod)
        else:
            acc = scr[0]
            k = pl.program_id(2)

            @pl.when(k == 0)
            def _():
                acc[...] = prod

            @pl.when(k > 0)
            def _():
                acc[...] += prod

            @pl.when(k == gk - 1)
            def _():
                finish(acc[...])

    in_specs = [pl.BlockSpec(blk_a, map_a), pl.BlockSpec(blk_b, map_b)]
    args = [a, b]
    if res is not None:
        in_specs.append(pl.BlockSpec(blk_o, map_o))
        args.append(res)
    return pl.pallas_call(
        body, name=name, grid=grid, in_specs=in_specs, out_specs=pl.BlockSpec(blk_o, map_o),
        out_shape=jax.ShapeDtypeStruct(out_shape, out_dtype),
        scratch_shapes=[] if gk == 1 else [pltpu.VMEM(tuple(d for d in blk_o if d is not None), F32)],
        compiler_params=_params("parallel", "parallel", "arbitrary"),
    )(*args)


TOK_MM_TILE = 256


def _mm_tokens(a, w, *, blk_a, map_a, pieces, res=None, name):
    def body(*refs):
        a_ref, w_ref = refs[:2]
        o_ref = refs[-1]
        total = None
        for idx, row, n in pieces:
            av = a_ref[...] if idx is None else a_ref[idx]
            prod = _dot(av, w_ref[row:row + n, :])
            total = prod if total is None else total + prod
        if res is not None:
            total = total + refs[2][...]
        o_ref[...] = total

    tile = pl.BlockSpec((TOK_MM_TILE, D_MODEL), lambda i: (i, 0))
    in_specs = [pl.BlockSpec(blk_a, map_a), _const_spec(w.shape)]
    args = [a, w]
    if res is not None:
        in_specs.append(tile)
        args.append(res)
    return pl.pallas_call(
        body, name=name, grid=(SEQ // TOK_MM_TILE,), in_specs=in_specs, out_specs=tile,
        out_shape=jax.ShapeDtypeStruct((SEQ, D_MODEL), F32), compiler_params=_params("parallel"),
    )(*args)


def _rms_fwd(x, g, name):
    def body(x_ref, g_ref, o_ref):
        xv = x_ref[...]
        r = lax.rsqrt(jnp.mean(xv * xv, axis=-1, keepdims=True) + EPS)
        o_ref[...] = (xv * r * g_ref[...]).astype(BF)

    tile = pl.BlockSpec((TOK_TILE, D_MODEL), lambda i: (i, 0))
    return pl.pallas_call(
        body, name=name, grid=(SEQ // TOK_TILE,), in_specs=[tile, _const_spec((1, D_MODEL))], out_specs=tile,
        out_shape=jax.ShapeDtypeStruct((SEQ, D_MODEL), BF), compiler_params=_params("parallel"),
    )(x, g)


def _rms_bwd(dy, x, g, dres, name):
    def body(dy_ref, x_ref, g_ref, dres_ref, dx_ref, dg_ref):
        xv = x_ref[...]
        r = lax.rsqrt(jnp.mean(xv * xv, axis=-1, keepdims=True) + EPS)
        xn = xv * r
        dyv = dy_ref[...]
        dxn = dyv * g_ref[...]
        dx_ref[...] = dres_ref[...] + r * (dxn - xn * jnp.mean(dxn * xn, axis=-1, keepdims=True))
        part = jnp.sum(dyv * xn, axis=0, keepdims=True)

        @pl.when(pl.program_id(0) == 0)
        def _():
            dg_ref[...] = part

        @pl.when(pl.program_id(0) > 0)
        def _():
            dg_ref[...] += part

    tile = pl.BlockSpec((TOK_TILE, D_MODEL), lambda i: (i, 0))
    vec = _const_spec((1, D_MODEL))
    return pl.pallas_call(
        body, name=name, grid=(SEQ // TOK_TILE,), in_specs=[tile, tile, vec, tile], out_specs=[tile, vec],
        out_shape=[jax.ShapeDtypeStruct((SEQ, D_MODEL), F32), jax.ShapeDtypeStruct((1, D_MODEL), F32)],
        compiler_params=_params("arbitrary"),
    )(dy, x, g, dres)


def _final_loss(x2, g, target):
    def body(x_ref, g_ref, t_ref, loss_ref, dx_ref, dxb_ref, dg_ref):
        xv = x_ref[...]
        r = lax.rsqrt(jnp.mean(xv * xv, axis=-1, keepdims=True) + EPS)
        xn = xv * r
        gv = g_ref[...]
        err = xn * gv - t_ref[...]
        lpart = jnp.full((1, 128), 0.5 * jnp.sum(jnp.mean(err * err, axis=-1, keepdims=True)), F32)
        dyv = err * (1.0 / D_MODEL)
        dxn = dyv * gv
        dxv = r * (dxn - xn * jnp.mean(dxn * xn, axis=-1, keepdims=True))
        dx_ref[...] = dxv
        dxb_ref[...] = dxv.astype(BF)
        gpart = jnp.sum(dyv * xn, axis=0, keepdims=True)

        @pl.when(pl.program_id(0) == 0)
        def _():
            loss_ref[...] = lpart
            dg_ref[...] = gpart

        @pl.when(pl.program_id(0) > 0)
        def _():
            loss_ref[...] += lpart
            dg_ref[...] += gpart

    tile = pl.BlockSpec((TOK_TILE, D_MODEL), lambda i: (i, 0))
    vec = _const_spec((1, D_MODEL))
    return pl.pallas_call(
        body, name="final_loss", grid=(SEQ // TOK_TILE,), in_specs=[tile, vec, tile],
        out_specs=[_const_spec((1, 128)), tile, tile, vec],
        out_shape=[jax.ShapeDtypeStruct((1, 128), F32), jax.ShapeDtypeStruct((SEQ, D_MODEL), F32),
                   jax.ShapeDtypeStruct((SEQ, D_MODEL), BF), jax.ShapeDtypeStruct((1, D_MODEL), F32)],
        compiler_params=_params("arbitrary"),
    )(x2, g, target)


def _pool_counts(w):
    pos = lax.broadcasted_iota(jnp.int32, (SEQ, 1), 0).astype(F32)
    return jnp.minimum(pos + 1.0, float(w))


def _pool_window(u, w, ext):
    ext[pl.ds(POOL_HALO, SEQ), :] = u
    win = u
    for j in range(1, w):
        win = win + ext[pl.ds(POOL_HALO - j, SEQ), :]
    return win / _pool_counts(w) - u


def _pool_fwd(zcat, w_grp, scale):
    def body(z_ref, w_ref, s_ref, o_ref, ext):
        ext[pl.ds(0, POOL_HALO), :] = jnp.zeros((POOL_HALO, POOL_GD), F32)
        for g, w in enumerate(POOL_WINDOWS):
            cols = slice(g * POOL_GD, (g + 1) * POOL_GD)
            p = _pool_window(z_ref[:, cols], w, ext)
            o_ref[:, cols] = (_dot(p, w_ref[g]) * s_ref[:, cols]).astype(BF)

    return pl.pallas_call(
        body, name="pool_fwd", grid=(1,),
        in_specs=[pl.BlockSpec((SEQ, POOL_WIDTH), lambda i: (0, C_POOL // POOL_WIDTH)),
                  _const_spec((4, POOL_GD, POOL_GD)), _const_spec((1, POOL_WIDTH))],
        out_specs=_const_spec((SEQ, POOL_WIDTH)), out_shape=jax.ShapeDtypeStruct((SEQ, POOL_WIDTH), BF),
        scratch_shapes=[pltpu.VMEM((POOL_HALO + SEQ, POOL_GD), F32)], compiler_params=_params("arbitrary"),
    )(zcat, w_grp, scale)


def _pool_bwd(dzcat, zcat, dps, w_grp, scale):
    def body(dz_in, z_ref, dps_ref, w_ref, s_ref, dz_ref, dw_ref, dsc_ref, ext, ext2):
        del dz_in
        ext[pl.ds(0, POOL_HALO), :] = jnp.zeros((POOL_HALO, POOL_GD), F32)
        ext2[pl.ds(SEQ, POOL_HALO), :] = jnp.zeros((POOL_HALO, POOL_GD), F32)
        for g, w in enumerate(POOL_WINDOWS):
            cols = slice(g * POOL_GD, (g + 1) * POOL_GD)
            p = _pool_window(z_ref[:, cols], w, ext)
            wg = w_ref[g]
            pg = _dot(p, wg)
            dpsv = dps_ref[:, cols]
            dsc_ref[:, cols] = jnp.sum(dpsv * pg, axis=0, keepdims=True)
            dpg = dpsv * s_ref[:, cols]
            dw_ref[g] = _dot(p, dpg, ta=True)
            dp = _dot(dpg, wg, tb=True)
            dpc = dp / _pool_counts(w)
            ext2[pl.ds(0, SEQ), :] = dpc
            du = dpc
            for j in range(1, w):
                du = du + ext2[pl.ds(j, SEQ), :]
            dz_ref[:, cols] = (du - dp).astype(BF)

    return pl.pallas_call(
        body, name="pool_bwd", grid=(1,),
        in_specs=[pl.BlockSpec(memory_space=pl.ANY),
                  pl.BlockSpec((SEQ, POOL_WIDTH), lambda i: (0, C_POOL // POOL_WIDTH)),
                  _const_spec((SEQ, POOL_WIDTH)), _const_spec((4, POOL_GD, POOL_GD)), _const_spec((1, POOL_WIDTH))],
        out_specs=[pl.BlockSpec((SEQ, POOL_WIDTH), lambda i: (0, C_POOL // POOL_WIDTH)),
                   _const_spec((4, POOL_GD, POOL_GD)), _const_spec((1, POOL_WIDTH))],
        out_shape=[jax.ShapeDtypeStruct((SEQ, N_CAT), BF), jax.ShapeDtypeStruct((4, POOL_GD, POOL_GD), F32),
                   jax.ShapeDtypeStruct((1, POOL_WIDTH), F32)],
        scratch_shapes=[pltpu.VMEM((POOL_HALO + SEQ, POOL_GD), F32), pltpu.VMEM((SEQ + POOL_HALO, POOL_GD), F32)],
        input_output_aliases={0: 0}, compiler_params=_params("arbitrary"),
    )(dzcat, zcat, dps, w_grp, scale)


GK_TILE = 512


def _gk_fwd(h, wt_gk, wgk_pad, b_gk):
    def body(h_ref, wt_ref, w_ref, b_ref, la_ref):
        z_gk = _dot(h_ref[...], wt_ref[...], tb=True)
        pre = _dot(z_gk, w_ref[...]) + b_ref[...]
        la_ref[...] = (jnp.minimum(pre, 0.0) - jnp.log(1.0 + jnp.exp(-jnp.abs(pre)))) * (1.0 / GATE_NORM)

    return pl.pallas_call(
        body, name="gk_fwd", grid=(SEQ // GK_TILE,),
        in_specs=[pl.BlockSpec((GK_TILE, D_MODEL), lambda i: (i, 0)), _const_spec((GK_PAD, D_MODEL)),
                  _const_spec((GK_PAD, GLA_DK)), _const_spec((1, GLA_DK))],
        out_specs=pl.BlockSpec((GK_TILE, GLA_DK), lambda i: (i, 0)),
        out_shape=jax.ShapeDtypeStruct((SEQ, GLA_DK), F32), compiler_params=_params("parallel"),
    )(h, wt_gk, wgk_pad, b_gk)


def _gk_bwd(dla, h, wt_gk, wgk_pad, b_gk):
    def body(dla_ref, h_ref, wt_ref, w_ref, b_ref, dh_ref, dwt_ref, dw_ref, db_ref):
        hv = h_ref[...]
        wtv = wt_ref[...]
        wv = w_ref[...]
        z_gk = _dot(hv, wtv, tb=True)
        pre = _dot(z_gk, wv) + b_ref[...]
        dpre = dla_ref[...] * (1.0 / GATE_NORM) * (1.0 - _sigmoid(pre))
        dz_gk = _dot(dpre, wv, tb=True)
        dh_ref[...] = _dot(dz_gk, wtv)
        dwtp = _dot(dz_gk, hv, ta=True)
        dwp = _dot(z_gk, dpre, ta=True)
        dbp = jnp.sum(dpre, axis=0, keepdims=True)

        @pl.when(pl.program_id(0) == 0)
        def _():
            dwt_ref[...] = dwtp
            dw_ref[...] = dwp
            db_ref[...] = dbp

        @pl.when(pl.program_id(0) > 0)
        def _():
            dwt_ref[...] += dwtp
            dw_ref[...] += dwp
            db_ref[...] += dbp

    tile = pl.BlockSpec((GK_TILE, D_MODEL), lambda i: (i, 0))
    return pl.pallas_call(
        body, name="gk_bwd", grid=(SEQ // GK_TILE,),
        in_specs=[pl.BlockSpec((GK_TILE, GLA_DK), lambda i: (i, 0)), tile, _const_spec((GK_PAD, D_MODEL)),
                  _const_spec((GK_PAD, GLA_DK)), _const_spec((1, GLA_DK))],
        out_specs=[tile, _const_spec((GK_PAD, D_MODEL)), _const_spec((GK_PAD, GLA_DK)), _const_spec((1, GLA_DK))],
        out_shape=[jax.ShapeDtypeStruct((SEQ, D_MODEL), F32), jax.ShapeDtypeStruct((GK_PAD, D_MODEL), F32),
                   jax.ShapeDtypeStruct((GK_PAD, GLA_DK), F32), jax.ShapeDtypeStruct((1, GLA_DK), F32)],
        compiler_params=_params("arbitrary"),
    )(dla, h, wt_gk, wgk_pad, b_gk)


GLA_ROWS = GLA_CPS * CHUNK
GLA_STEPS = SEQ // GLA_ROWS
QKV_W = 2048


def _gla_chunk(qkv_ref, la_ref, rows, h):
    tri = lax.broadcasted_iota(jnp.int32, (CHUNK, CHUNK), 0) >= lax.broadcasted_iota(jnp.int32, (CHUNK, CHUNK), 1)
    q = qkv_ref[rows, h * HK:(h + 1) * HK] * (HK ** -0.5)
    k = qkv_ref[rows, GLA_DK + h * HK:GLA_DK + (h + 1) * HK]
    v = qkv_ref[rows, 2 * GLA_DK + h * HV:2 * GLA_DK + (h + 1) * HV]
    la = la_ref[rows, h * HK:(h + 1) * HK]
    bc = _dot_exact(tri.astype(F32), la)
    e_pos, e_neg = jnp.exp(bc), jnp.exp(-bc)
    dl = jnp.exp(jnp.sum(la, axis=0, keepdims=True))
    q_fw, q_bw, k_fw, k_bw = q * e_pos, q * e_neg, k * e_neg, k * e_pos
    scores = jnp.where(tri, _dot(q_fw, k_fw, tb=True), _dot(q_bw, k_bw, tb=True))
    return tri, v, e_pos, e_neg, dl, q_fw, q_bw, k_fw, k_bw, scores


def _gla_fwd(zcat, la):
    def body(qkv_ref, la_ref, o_ref, st_ref, state):
        @pl.when(pl.program_id(0) == 0)
        def _():
            state[...] = jnp.zeros_like(state)

        for c in range(GLA_CPS):
            rows = slice(c * CHUNK, (c + 1) * CHUNK)
            for h in range(HEADS):
                _, v, _, _, dl, q_fw, _, k_fw, _, scores = _gla_chunk(qkv_ref, la_ref, rows, h)
                st = state[h]
                st_ref[c, h] = st
                o_ref[rows, h * HV:(h + 1) * HV] = _dot(scores, v) + _dot(q_fw, st, tb=True)
                state[h] = st * dl + _dot(v, k_fw * dl, ta=True)

    return pl.pallas_call(
        body, name="gla_fwd", grid=(GLA_STEPS,),
        in_specs=[pl.BlockSpec((GLA_ROWS, QKV_W), lambda i: (i, 0)), pl.BlockSpec((GLA_ROWS, GLA_DK), lambda i: (i, 0))],
        out_specs=[pl.BlockSpec((GLA_ROWS, D_MODEL), lambda i: (i, 0)),
                   pl.BlockSpec((GLA_CPS, HEADS, HV, HK), lambda i: (i, 0, 0, 0))],
        out_shape=[jax.ShapeDtypeStruct((SEQ, D_MODEL), F32),
                   jax.ShapeDtypeStruct((SEQ // CHUNK, HEADS, HV, HK), F32)],
        scratch_shapes=[pltpu.VMEM((HEADS, HV, HK), F32)], compiler_params=_params("arbitrary"),
    )(zcat, la)


def _gla_bwd(dzcat, zcat, la, d_o, states):
    def body(dz_in, qkv_ref, la_ref, do_ref, st_ref, dqkv_ref, dla_ref, dstate):
        del dz_in

        @pl.when(pl.program_id(0) == 0)
        def _():
            dstate[...] = jnp.zeros_like(dstate)

        last_row = lax.broadcasted_iota(jnp.int32, (CHUNK, HK), 0) == CHUNK - 1
        upper = (lax.broadcasted_iota(jnp.int32, (CHUNK, CHUNK), 0)
                 <= lax.broadcasted_iota(jnp.int32, (CHUNK, CHUNK), 1)).astype(F32)
        for c in reversed(range(GLA_CPS)):
            rows = slice(c * CHUNK, (c + 1) * CHUNK)
            for h in range(HEADS):
                tri, v, e_pos, e_neg, dl, q_fw, q_bw, k_fw, k_bw, scores = _gla_chunk(qkv_ref, la_ref, rows, h)
                st = st_ref[c, h]
                dst = dstate[h]
                d_out = do_ref[rows, h * HV:(h + 1) * HV]
                k_dec = k_fw * dl
                dp = _dot(d_out, v, tb=True)
                dp_fw = jnp.where(tri, dp, 0.0)
                dp_bw = jnp.where(tri, 0.0, dp)
                dv = _dot(scores, d_out, ta=True) + _dot(k_dec, dst, tb=True)
                dk_dec = _dot(v, dst)
                dq_fw = _dot(dp_fw, k_fw) + _dot(d_out, st)
                dk_fw = _dot(dp_fw, q_fw, ta=True) + dk_dec * dl
                dq_bw = _dot(dp_bw, k_bw)
                dk_bw = _dot(dp_bw, q_bw, ta=True)
                ddl = jnp.sum(st * dst, axis=0, keepdims=True) + jnp.sum(k_fw * dk_dec, axis=0, keepdims=True)
                dstate[h] = dst * dl + _dot(d_out, q_fw, ta=True)
                dq = (dq_fw * e_pos + dq_bw * e_neg) * (HK ** -0.5)
                dk = dk_fw * e_neg + dk_bw * e_pos
                db = dq_fw * q_fw - dk_fw * k_fw - dq_bw * q_bw + dk_bw * k_bw + jnp.where(last_row, ddl * dl, 0.0)
                dla_ref[rows, h * HK:(h + 1) * HK] = _dot_exact(upper, db)
                dqkv_ref[rows, h * HK:(h + 1) * HK] = dq.astype(BF)
                dqkv_ref[rows, GLA_DK + h * HK:GLA_DK + (h + 1) * HK] = dk.astype(BF)
                dqkv_ref[rows, 2 * GLA_DK + h * HV:2 * GLA_DK + (h + 1) * HV] = dv.astype(BF)

    rev = lambda i: (GLA_STEPS - 1 - i, 0)
    return pl.pallas_call(
        body, name="gla_bwd", grid=(GLA_STEPS,),
        in_specs=[pl.BlockSpec(memory_space=pl.ANY), pl.BlockSpec((GLA_ROWS, QKV_W), rev),
                  pl.BlockSpec((GLA_ROWS, GLA_DK), rev), pl.BlockSpec((GLA_ROWS, D_MODEL), rev),
                  pl.BlockSpec((GLA_CPS, HEADS, HV, HK), lambda i: (GLA_STEPS - 1 - i, 0, 0, 0))],
        out_specs=[pl.BlockSpec((GLA_ROWS, QKV_W), rev), pl.BlockSpec((GLA_ROWS, GLA_DK), rev)],
        out_shape=[jax.ShapeDtypeStruct((SEQ, N_CAT), BF), jax.ShapeDtypeStruct((SEQ, GLA_DK), F32)],
        scratch_shapes=[pltpu.VMEM((HEADS, HV, HK), F32)], input_output_aliases={0: 0},
        compiler_params=_params("arbitrary"),
    )(dzcat, zcat, la, d_o, states)


def _silu_parts(x):
    s = _sigmoid(x)
    return x * s, s * (1.0 + x * (1.0 - s))


def _post_gla_fwd(o, zcat, g_head):
    def body(o_ref, zog_ref, g_ref, out_ref):
        for h in range(HEADS):
            cols = slice(h * HV, (h + 1) * HV)
            ov = o_ref[:, cols]
            r = lax.rsqrt(jnp.mean(ov * ov, axis=-1, keepdims=True) + EPS)
            act, _ = _silu_parts(zog_ref[:, cols])
            out_ref[:, cols] = (ov * r * g_ref[...] * act).astype(BF)

    tile = pl.BlockSpec((TOK_TILE, D_MODEL), lambda i: (i, 0))
    return pl.pallas_call(
        body, name="post_gla_fwd", grid=(SEQ // TOK_TILE,),
        in_specs=[tile, pl.BlockSpec((TOK_TILE, D_MODEL), lambda i: (i, C_OG // D_MODEL)), _const_spec((1, HV))],
        out_specs=tile, out_shape=jax.ShapeDtypeStruct((SEQ, D_MODEL), BF), compiler_params=_params("parallel"),
    )(o, zcat, g_head)


def _post_gla_bwd(dzcat, d_og, o, zcat, g_head):
    def body(dz_in, dog_ref, o_ref, zog_ref, g_ref, dz_ref, do_ref, dg_ref):
        del dz_in
        gpart = jnp.zeros((1, HV), F32)
        gv = g_ref[...]
        for h in range(HEADS):
            cols = slice(h * HV, (h + 1) * HV)
            ov = o_ref[:, cols]
            r = lax.rsqrt(jnp.mean(ov * ov, axis=-1, keepdims=True) + EPS)
            on = ov * r
            act, dact = _silu_parts(zog_ref[:, cols])
            dogv = dog_ref[:, cols]
            dz_ref[:, cols] = (dogv * on * gv * dact).astype(BF)
            d_on_g = dogv * act
            gpart = gpart + jnp.sum(d_on_g * on, axis=0, keepdims=True)
            dxn = d_on_g * gv
            do_ref[:, cols] = r * (dxn - on * jnp.mean(dxn * on, axis=-1, keepdims=True))

        @pl.when(pl.program_id(0) == 0)
        def _():
            dg_ref[...] = gpart

        @pl.when(pl.program_id(0) > 0)
        def _():
            dg_ref[...] += gpart

    tile = pl.BlockSpec((TOK_TILE, D_MODEL), lambda i: (i, 0))
    ogspec = pl.BlockSpec((TOK_TILE, D_MODEL), lambda i: (i, C_OG // D_MODEL))
    return pl.pallas_call(
        body, name="post_gla_bwd", grid=(SEQ // TOK_TILE,),
        in_specs=[pl.BlockSpec(memory_space=pl.ANY), tile, tile, ogspec, _const_spec((1, HV))],
        out_specs=[ogspec, tile, _const_spec((1, HV))],
        out_shape=[jax.ShapeDtypeStruct((SEQ, N_CAT), BF), jax.ShapeDtypeStruct((SEQ, D_MODEL), F32),
                   jax.ShapeDtypeStruct((1, HV), F32)],
        input_output_aliases={0: 0}, compiler_params=_params("arbitrary"),
    )(dzcat, d_og, o, zcat, g_head)


GATE_W = 2 * D_MODEL


def _mix_fwd(zcat, b_gate, y_pool, y_gla):
    def body(zg_ref, b_ref, yp_ref, yg_ref, out_ref):
        g0 = _sigmoid(zg_ref[:, :D_MODEL] + b_ref[:, :D_MODEL])
        g1 = _sigmoid(zg_ref[:, D_MODEL:] + b_ref[:, D_MODEL:])
        out_ref[...] = (g0 * yp_ref[...] + g1 * yg_ref[...]).astype(BF)

    tile = pl.BlockSpec((TOK_TILE, D_MODEL), lambda i: (i, 0))
    return pl.pallas_call(
        body, name="mix_fwd", grid=(SEQ // TOK_TILE,),
        in_specs=[pl.BlockSpec((TOK_TILE, GATE_W), lambda i: (i, C_GATE // GATE_W)), _const_spec((1, GATE_W)), tile, tile],
        out_specs=tile, out_shape=jax.ShapeDtypeStruct((SEQ, D_MODEL), BF), compiler_params=_params("parallel"),
    )(zcat, b_gate, y_pool, y_gla)


def _mix_bwd(dmixed, zcat, b_gate, y_pool, y_gla):
    def body(dm_ref, zg_ref, b_ref, yp_ref, yg_ref, dz_ref, dyp_ref, dyg_ref, db_ref):
        dm = dm_ref[...]
        g0 = _sigmoid(zg_ref[:, :D_MODEL] + b_ref[:, :D_MODEL])
        g1 = _sigmoid(zg_ref[:, D_MODEL:] + b_ref[:, D_MODEL:])
        dyp_ref[...] = (dm * g0).astype(BF)
        dyg_ref[...] = (dm * g1).astype(BF)
        dz0 = dm * yp_ref[...] * g0 * (1.0 - g0)
        dz1 = dm * yg_ref[...] * g1 * (1.0 - g1)
        dz_ref[:, :D_MODEL] = dz0.astype(BF)
        dz_ref[:, D_MODEL:] = dz1.astype(BF)
        b0 = jnp.sum(dz0, axis=0, keepdims=True)
        b1 = jnp.sum(dz1, axis=0, keepdims=True)

        @pl.when(pl.program_id(0) == 0)
        def _():
            db_ref[:, :D_MODEL] = b0
            db_ref[:, D_MODEL:] = b1

        @pl.when(pl.program_id(0) > 0)
        def _():
            db_ref[:, :D_MODEL] += b0
            db_ref[:, D_MODEL:] += b1

    tile = pl.BlockSpec((TOK_TILE, D_MODEL), lambda i: (i, 0))
    gspec = pl.BlockSpec((TOK_TILE, GATE_W), lambda i: (i, C_GATE // GATE_W))
    return pl.pallas_call(
        body, name="mix_bwd", grid=(SEQ // TOK_TILE,),
        in_specs=[tile, gspec, _const_spec((1, GATE_W)), tile, tile],
        out_specs=[gspec, tile, tile, _const_spec((1, GATE_W))],
        out_shape=[jax.ShapeDtypeStruct((SEQ, N_CAT), BF), jax.ShapeDtypeStruct((SEQ, D_MODEL), BF),
                   jax.ShapeDtypeStruct((SEQ, D_MODEL), BF), jax.ShapeDtypeStruct((1, GATE_W), F32)],
        compiler_params=_params("arbitrary"),
    )(dmixed, zcat, b_gate, y_pool, y_gla)


N_TOK_TILES = SEQ // TOK_TILE
HALO_PER_TILE = TOK_TILE // HALO


LANE_TILES = tuple((lo, min(128, FF_BLK - lo)) for lo in range(0, FF_BLK, 128))


def _taps(w_ref, b_ref, half, lanes, rows):
    shape = (rows, lanes.stop - lanes.start)
    return ([jnp.broadcast_to(w_ref[half, j:j + 1, lanes], shape) for j in range(3)],
            jnp.broadcast_to(b_ref[half, :, lanes], shape))


def _conv_strips(u_ref, ub_ref, ua_ref, taps, lanes, width, n_strips):
    first = pl.program_id(1) == 0
    row = lax.broadcasted_iota(jnp.int32, (HALO, width), 0)
    prev = [[pltpu.roll(jnp.where(first, 0.0, ub_ref[half, :, lanes]), k, 0) for k in (1, 2)] for half in range(2)]
    for s in range(n_strips + (ua_ref is not None)):
        u3, conv = [], []
        for half in range(2):
            cur = u_ref[half, s * HALO:(s + 1) * HALO, lanes] if s < n_strips else ua_ref[half, :, lanes]
            rolled = [pltpu.roll(cur, k, 0) for k in (1, 2)]
            frames = [jnp.where(row >= 2, rolled[1], prev[half][1]), jnp.where(row >= 1, rolled[0], prev[half][0]), cur]
            prev[half] = rolled
            w3, bias = taps[half]
            u3.append(frames)
            conv.append(bias + frames[0] * w3[0] + frames[1] * w3[1] + frames[2] * w3[2])
        yield s, u3, conv


def _pair_specs(pairs):
    tile = pl.BlockSpec((pairs, None, TOK_TILE, FF_BLK), lambda b, i: (0, b, i, 0))
    before = pl.BlockSpec((pairs, None, HALO, FF_BLK), lambda b, i: (0, b, jnp.maximum(i * HALO_PER_TILE - 1, 0), 0))
    after = pl.BlockSpec((pairs, None, HALO, FF_BLK),
                         lambda b, i: (0, b, jnp.minimum((i + 1) * HALO_PER_TILE, SEQ // HALO - 1), 0))

    def vec(rows):
        return pl.BlockSpec((2, None, rows, FF_BLK), lambda b, i: (0, b, 0, 0))

    return tile, before, after, vec


N_STRIPS = TOK_TILE // HALO


def _conv_fwd(u, w_conv, b_conv):
    def body(u_ref, ub_ref, w_ref, b_ref, a_ref):
        for lo, width in LANE_TILES:
            lanes = slice(lo, lo + width)
            taps = [_taps(w_ref, b_ref, half, lanes, HALO) for half in range(2)]
            pending = None
            for s, _, (cg, cv) in _conv_strips(u_ref, ub_ref, None, taps, lanes, width, N_STRIPS):
                act = cg * _sigmoid(cg) * cv
                if s % 2 == 0:
                    pending = act
                else:
                    a_ref[0, (s - 1) * HALO:(s + 1) * HALO, lanes] = jnp.concatenate([pending, act], axis=0).astype(BF)

    tile, before, _, vec = _pair_specs(2)
    out_tile, _, _, _ = _pair_specs(1)
    return pl.pallas_call(
        body, name="conv_fwd", grid=(4, N_TOK_TILES), in_specs=[tile, before, vec(3), vec(1)],
        out_specs=out_tile, out_shape=jax.ShapeDtypeStruct((1, 4, SEQ, FF_BLK), BF),
        compiler_params=_params("parallel", "parallel"),
    )(u, u, w_conv, b_conv)


def _conv_bwd(u, da, w_conv, b_conv):
    def body(u_ref, ub_ref, ua_ref, da_ref, daa_ref, w_ref, b_ref, du_ref, dw_ref, db_ref):
        i = pl.program_id(1)

        @pl.when(i == 0)
        def _():
            dw_ref[...] = jnp.zeros_like(dw_ref)
            db_ref[...] = jnp.zeros_like(db_ref)

        for lo, width in LANE_TILES:
            lanes = slice(lo, lo + width)
            row = lax.broadcasted_iota(jnp.int32, (HALO, width), 0)
            taps = [_taps(w_ref, b_ref, half, lanes, HALO) for half in range(2)]
            acc_w = [[jnp.zeros((HALO, width), F32) for _ in range(3)] for _ in range(2)]
            acc_b = [jnp.zeros((HALO, width), F32) for _ in range(2)]
            da_pair, pending = None, [None, None]
            dc_prev, up_prev = [None, None], [None, None]
            for s, u3, (cg, cv) in _conv_strips(u_ref, ub_ref, ua_ref, taps, lanes, width, N_STRIPS):
                act, dact = _silu_parts(cg)
                if s == N_STRIPS:
                    da = jnp.where(i < N_TOK_TILES - 1, daa_ref[0, :, lanes].astype(F32), 0.0)
                elif s % 2 == 0:
                    da_pair = da_ref[0, s * HALO:(s + 2) * HALO, lanes].astype(F32)
                    da = da_pair[:HALO]
                else:
                    da = da_pair[HALO:]
                dc = (da * cv * dact, da * act)
                for half in range(2):
                    up = [pltpu.roll(dc[half], HALO - k, 0) for k in (1, 2)]
                    if s < N_STRIPS:
                        for j in range(3):
                            acc_w[half][j] = acc_w[half][j] + dc[half] * u3[half][j]
                        acc_b[half] = acc_b[half] + dc[half]
                    if s >= 1:
                        w3 = taps[half][0]
                        du = (dc_prev[half] * w3[2] + jnp.where(row < HALO - 1, up_prev[half][0], up[0]) * w3[1]
                              + jnp.where(row < HALO - 2, up_prev[half][1], up[1]) * w3[0])
                        if (s - 1) % 2 == 0:
                            pending[half] = du
                        else:
                            du_ref[half, (s - 2) * HALO:s * HALO, lanes] = jnp.concatenate([pending[half], du],
                                                                                           axis=0).astype(BF)
                    dc_prev[half], up_prev[half] = dc[half], up
            for half in range(2):
                for j in range(3):
                    dw_ref[half, j:j + 1, lanes] += jnp.sum(acc_w[half][j], axis=0, keepdims=True)
                db_ref[half, :, lanes] += jnp.sum(acc_b[half], axis=0, keepdims=True)

    tile, before, after, vec = _pair_specs(2)
    da_tile, _, da_after_spec, _ = _pair_specs(1)
    return pl.pallas_call(
        body, name="conv_bwd", grid=(4, N_TOK_TILES),
        in_specs=[tile, before, after, da_tile, da_after_spec, vec(3), vec(1)],
        out_specs=[tile, vec(3), vec(1)],
        out_shape=[jax.ShapeDtypeStruct((2, 4, SEQ, FF_BLK), BF), jax.ShapeDtypeStruct((2, 4, 3, FF_BLK), F32),
                   jax.ShapeDtypeStruct((2, 4, 1, FF_BLK), F32)],
        compiler_params=_params("parallel", "arbitrary"),
    )(u, u, u, da, da, w_conv, b_conv)


ANY = pl.BlockSpec(memory_space=pl.ANY)


def _place():
    x, y, c = lax.axis_index("x"), lax.axis_index("y"), lax.axis_index("c")
    other_chips = [(1 - x, y), (x, 1 - y), (1 - x, 1 - y)]
    return x, y, c, other_chips


def _all_gather(shards, name):
    n = len(shards)

    def body(*refs):
        src, out = refs[:n], refs[n:2 * n]
        send_sems, recv_sems, local_sems = refs[2 * n:]
        x, y, c, chips = _place()
        me, sibling = (x, y, c), (x, y, 1 - c)

        def copy(a, k, block, to, own=False):
            dst = out[a].at[4 * block[0] + 2 * block[1] + block[2]]
            return pltpu.make_async_remote_copy(src_ref=src[a] if own else dst, dst_ref=dst, send_sem=send_sems.at[a, k],
                                                recv_sem=recv_sems.at[a, k], device_id=to, device_id_type=MESH)

        mine = [pltpu.make_async_copy(src[a], out[a].at[4 * x + 2 * y + c], local_sems.at[a]) for a in range(n)]
        first = []
        for a in range(n):
            mine[a].start()
            first.append(copy(a, 0, me, sibling, own=True))
            first += [copy(a, 1 + j, me, (*chip, c), own=True) for j, chip in enumerate(chips)]
        for cp in first:
            cp.start()
        passed = []
        for j, chip in enumerate(chips):
            for a in range(n):
                copy(a, 1 + j, (*chip, c), me).wait_recv()
                passed.append(copy(a, 4 + j, (*chip, c), sibling))
                passed[-1].start()
        for a in range(n):
            copy(a, 0, sibling, me).wait_recv()
            for j, chip in enumerate(chips):
                copy(a, 4 + j, (*chip, 1 - c), me).wait_recv()
        for cp in first + passed:
            cp.wait_send()
        for cp in mine:
            cp.wait()

    return pl.pallas_call(
        body, name=name, in_specs=[ANY] * n, out_specs=[ANY] * n,
        out_shape=[jax.ShapeDtypeStruct((N_DEV,) + s.shape, s.dtype) for s in shards],
        scratch_shapes=[pltpu.SemaphoreType.DMA((n, 7)), pltpu.SemaphoreType.DMA((n, 7)), pltpu.SemaphoreType.DMA((n,))],
    )(*shards)


SEM = pl.BlockSpec(memory_space=pltpu.SEMAPHORE)
IN_HBM = pl.BlockSpec(memory_space=pltpu.HBM)
SPLIT_PARAMS = pltpu.CompilerParams(has_side_effects=pltpu.SideEffectType.DATAFLOW_SIDE_EFFECTING)


def _gather_first(refs, send_sems, recv_sems):
    x, y, c, chips = _place()
    targets = [(x, y, 1 - c)] + [(px, py, c) for px, py in chips]
    return [pltpu.make_async_remote_copy(src_ref=refs[2 * a], dst_ref=refs[2 * a + 1].at[4 * x + 2 * y + c],
                                         send_sem=send_sems.at[4 * a + k], recv_sem=recv_sems.at[4 * a + k],
                                         device_id=to, device_id_type=MESH)
            for a in range(len(refs) // 2) for k, to in enumerate(targets)]


def _gather_second(refs, send_sems, recv_sems):
    x, y, c, chips = _place()
    copies = []
    for a, land in enumerate(refs):
        for j, (px, py) in enumerate(chips):
            block = land.at[4 * px + 2 * py + c]
            copies.append(pltpu.make_async_remote_copy(src_ref=block, dst_ref=block, send_sem=send_sems.at[3 * a + j],
                                                       recv_sem=recv_sems.at[3 * a + j], device_id=(x, y, 1 - c),
                                                       device_id_type=MESH))
    return copies


def _reduce_first(refs, send_sems, recv_sems):
    x, y, c, _ = _place()
    return [pltpu.make_async_remote_copy(src_ref=refs[2 * a].at[j, 1 - c], dst_ref=refs[2 * a + 1].at[j],
                                         send_sem=send_sems.at[4 * a + j], recv_sem=recv_sems.at[4 * a + j],
                                         device_id=(x, y, 1 - c), device_id_type=MESH)
            for a in range(len(refs) // 2) for j in range(4)]


def _reduce_second(refs, send_sems, recv_sems):
    _, _, c, chips = _place()
    return [pltpu.make_async_remote_copy(src_ref=refs[2 * a].at[2 * px + py], dst_ref=refs[2 * a + 1].at[k],
                                         send_sem=send_sems.at[3 * a + k], recv_sem=recv_sems.at[3 * a + k],
                                         device_id=(px, py, c), device_id_type=MESH)
            for a in range(len(refs) // 2) for k, (px, py) in enumerate(chips)]


def _split_start(name, groups):
    arrays = [a for g in groups for a in g[0]]
    n = len(arrays)

    def body(*refs):
        sems = refs[n:n + 2 * len(groups)]
        at = 0
        for gi, (members, _, build) in enumerate(groups):
            for cp in build(refs[at:at + len(members)], sems[2 * gi], sems[2 * gi + 1]):
                cp.start()
            at += len(members)
        refs[-1][...] = jnp.zeros_like(refs[-1])

    sem_shapes = [pltpu.SemaphoreType.DMA((g[1],)) for g in groups for _ in range(2)]
    outs = pl.pallas_call(
        body, name=name, in_specs=[IN_HBM] * n,
        out_shape=(*sem_shapes, *[pltpu.HBM(a.shape, a.dtype) for a in arrays], jax.ShapeDtypeStruct((8, 128), F32)),
        out_specs=(*[SEM] * len(sem_shapes), *[IN_HBM] * n, pl.BlockSpec(memory_space=pltpu.VMEM)),
        input_output_aliases={i: len(sem_shapes) + i for i in range(n)}, compiler_params=SPLIT_PARAMS,
    )(*[pltpu.with_memory_space_constraint(a, pltpu.HBM) for a in arrays])
    per_group, at = [], len(sem_shapes)
    for gi, (members, _, _) in enumerate(groups):
        per_group.append((outs[2 * gi], outs[2 * gi + 1], list(outs[at:at + len(members)])))
        at += len(members)
    return per_group, outs[-1]


def _split_wait(name, started, build, after):
    send_sems, recv_sems, arrays = started
    n = len(arrays)

    def body(*refs):
        for cp in build(refs[:n], refs[n], refs[n + 1]):
            cp.wait_send()
            cp.wait_recv()

    return pl.pallas_call(
        body, name=name, in_specs=[IN_HBM] * n + [SEM, SEM, ANY],
        out_shape=tuple(pltpu.HBM(a.shape, a.dtype) for a in arrays), out_specs=tuple([IN_HBM] * n),
        input_output_aliases={i: i for i in range(n)}, compiler_params=SPLIT_PARAMS,
    )(*arrays, send_sems, recv_sems, after)


def _gather_landing(shard, me):
    return lax.dynamic_update_slice(lax.empty((N_DEV,) + shard.shape, shard.dtype), shard[None],
                                    (me,) + (0,) * shard.ndim)


def _tile_2d(rows, cols):
    for t in (256, 176, 128):
        if rows % t == 0:
            return t, cols
    return rows, 256


def _pair_sum(part, recv, core, name):
    _, rows, cols = recv.shape
    tr, tc = _tile_2d(rows, cols)

    def body(c_ref, p_ref, r_ref, o_ref):
        del c_ref
        o_ref[...] = (p_ref[...].astype(F32) + r_ref[...].astype(F32)).astype(BF)

    grid_spec = pltpu.PrefetchScalarGridSpec(
        num_scalar_prefetch=1, grid=(4, rows // tr, cols // tc),
        in_specs=[pl.BlockSpec((None, None, tr, tc), lambda j, i, k, c_ref: (j, c_ref[0], i, k)),
                  pl.BlockSpec((None, tr, tc), lambda j, i, k, c_ref: (j, i, k))],
        out_specs=pl.BlockSpec((None, tr, tc), lambda j, i, k, c_ref: (j, i, k)))
    return pl.pallas_call(
        body, name=name, grid_spec=grid_spec, out_shape=jax.ShapeDtypeStruct(recv.shape, BF),
        compiler_params=_params("parallel", "parallel", "parallel"),
    )(core, part, recv)


def _adamw(w, g, m, v):
    m = ADAM_B1 * m + (1.0 - ADAM_B1) * g
    v = ADAM_B2 * v + (1.0 - ADAM_B2) * (g * g)
    delta = -ADAM_LR * ((m / ADAM_C1) / (jnp.sqrt(v / ADAM_C2) + ADAM_EPS) + ADAM_WD * w)
    return delta, m, v


def _chip_sum_adamw(sums, recv, w, m, v, chip, name):
    rows, cols = w.shape
    tr, tc = _tile_2d(rows, cols)

    def body(chip_ref, s_ref, r_ref, w_ref, m_ref, v_ref, g_out, d_out, m_out, v_out):
        del chip_ref
        g = s_ref[...].astype(F32)
        for k in range(3):
            g = g + r_ref[k].astype(F32)
        g_out[...] = g
        d_out[...], m_out[...], v_out[...] = _adamw(w_ref[...], g, m_ref[...], v_ref[...])

    tile = pl.BlockSpec((tr, tc), lambda i, k, chip_ref: (i, k))
    grid_spec = pltpu.PrefetchScalarGridSpec(
        num_scalar_prefetch=1, grid=(rows // tr, cols // tc),
        in_specs=[pl.BlockSpec((None, tr, tc), lambda i, k, chip_ref: (chip_ref[0], i, k)),
                  pl.BlockSpec((3, tr, tc), lambda i, k, chip_ref: (0, i, k)), tile, tile, tile],
        out_specs=[tile] * 4)
    return pl.pallas_call(
        body, name=name, grid_spec=grid_spec, out_shape=[jax.ShapeDtypeStruct((rows, cols), F32)] * 4,
        compiler_params=_params("parallel", "parallel"),
    )(chip, sums, recv, w, m, v)


def _sum8_adamw(parts, w, m, v):
    rows = w.shape[0]

    def body(p_ref, w_ref, m_ref, v_ref, g_out, d_out, m_out, v_out):
        g = p_ref[0]
        for d in range(1, N_DEV):
            g = g + p_ref[d]
        g_out[...] = g
        d_out[...], m_out[...], v_out[...] = _adamw(w_ref[...], g, m_ref[...], v_ref[...])

    full = _const_spec((rows, 128))
    return pl.pallas_call(
        body, name="small_sum_adamw", grid=(1,), in_specs=[_const_spec((N_DEV, rows, 128)), full, full, full],
        out_specs=[full] * 4, out_shape=[jax.ShapeDtypeStruct((rows, 128), F32)] * 4,
        compiler_params=_params("arbitrary"),
    )(parts, w, m, v)


def _plain_adamw(g, w, m, v):
    rows = w.shape[0]

    def body(g_ref, w_ref, m_ref, v_ref, d_out, m_out, v_out):
        d_out[...], m_out[...], v_out[...] = _adamw(w_ref[...], g_ref[...], m_ref[...], v_ref[...])

    full = _const_spec((rows, 128))
    return pl.pallas_call(
        body, name="shard_adamw", grid=(1,), in_specs=[full] * 4, out_specs=[full] * 3,
        out_shape=[jax.ShapeDtypeStruct((rows, 128), F32)] * 3, compiler_params=_params("arbitrary"),
    )(g, w, m, v)


def _pack(arrays, rows):
    flat = jnp.concatenate([a.reshape(-1) for a in arrays])
    return jnp.pad(flat, (0, rows * 128 - flat.shape[0])).reshape(rows, 128)


def _unpack(packed, shapes):
    flat = packed.reshape(-1)
    out, at = [], 0
    for s in shapes:
        size = 1
        for d in s:
            size *= d
        out.append(flat[at:at + size].reshape(s))
        at += size
    return out


MM_TILE = 512
N_MM_TILES = SEQ // MM_TILE
CAT_TILE = 512
N_CAT_TILES = N_CAT // CAT_TILE
SMALL_ROWS = 808
SHARD_ROWS = 32


def kernel(x, g_mix, w_in, b_gate, w_gk_up, b_gk, w_pool_grp, pool_scale, g_gla_head, w_pool_proj, w_gla_proj, w_out, g_ffn, w_up, w_conv, b_conv, w_down, g_final, loss_target, m_g_mix, m_w_in, m_b_gate, m_w_gk_up, m_b_gk, m_w_pool_grp, m_pool_scale, m_g_gla_head, m_w_pool_proj, m_w_gla_proj, m_w_out, m_g_ffn, m_w_up, m_w_conv, m_b_conv, m_w_down, m_g_final, v_g_mix, v_w_in, v_b_gate, v_w_gk_up, v_b_gk, v_w_pool_grp, v_pool_scale, v_g_gla_head, v_w_pool_proj, v_w_gla_proj, v_w_out, v_g_ffn, v_w_up, v_w_conv, v_b_conv, v_w_down, v_g_final):
    xi, yi, ci = lax.axis_index("x"), lax.axis_index("y"), lax.axis_index("c")
    me = 4 * xi + 2 * yi + ci
    core = jnp.reshape(ci, (1,)).astype(jnp.int32)
    chip = jnp.reshape(2 * xi + yi, (1,)).astype(jnp.int32)
    xs, target = x[0], loss_target[0]

    big = dict(w_in=w_in[0].T, w_pool_proj=w_pool_proj[0], w_gla_proj=w_gla_proj[0], w_out=w_out[0], w_up=w_up[0].T,
               w_down=w_down[0])
    moments = dict(w_in=(m_w_in[0].T, v_w_in[0].T), w_pool_proj=(m_w_pool_proj[0], v_w_pool_proj[0]),
                   w_gla_proj=(m_w_gla_proj[0], v_w_gla_proj[0]), w_out=(m_w_out[0], v_w_out[0]),
                   w_up=(m_w_up[0].T, v_w_up[0].T), w_down=(m_w_down[0], v_w_down[0]))
    names = list(big)
    shards = {k: big[k].astype(BF) for k in names}
    shards["w_gk_up"], shards["w_conv"] = w_gk_up[0], w_conv[0]
    gather_groups = (("w_in", "w_gk_up"), ("w_pool_proj", "w_gla_proj", "w_out"), ("w_up", "w_down", "w_conv"))
    started, token = _split_start("gather_start", [
        ([t for k in g for t in (shards[k], _gather_landing(shards[k], me))], 4 * len(g), _gather_first)
        for g in gather_groups])

    def gather_pass(gi, after):
        lands = list(_split_wait(f"gather_wait_{gi}", started[gi], _gather_first, after)[1::2])
        passed, _ = _split_start(f"gather_pass_{gi}", [(lands, 3 * len(lands), _gather_second)])
        return passed[0]

    def gather_done(gi, passed, after):
        return dict(zip(gather_groups[gi], _split_wait(f"gather_pass_wait_{gi}", passed, _gather_second, after)))

    tok = lambda i, j, k: (i, 0)
    whole = lambda i, j, k: (0, 0)
    kblk = lambda i, j, k: (k, 0)
    ff_tile = (None, None, MM_TILE, FF_BLK)
    ff_seq = (None, None, SEQ, FF_BLK)

    h = _rms_fwd(xs, g_mix + token[:1, :1], "rms_mix")
    wg = gather_done(0, gather_pass(0, h), h)
    wt_in = wg["w_in"].reshape(IN_TOTAL, D_MODEL)
    wt_cat = jnp.concatenate([wt_in[R_QKV:R_OG], wt_in[R_GATE:], wt_in[R_OG:R_GK], wt_in[R_POOL:R_QKV]], axis=0)
    wt_gk = jnp.pad(wt_in[R_GK:R_GATE], ((0, GK_PAD - GATE_RANK), (0, 0)))
    wgk_pad = jnp.pad(wg["w_gk_up"].transpose(1, 0, 2).reshape(GATE_RANK, GLA_DK), ((0, GK_PAD - GATE_RANK), (0, 0)))
    zcat = _mm(h, wt_cat, out_shape=(SEQ, N_CAT), out_dtype=F32, grid=(N_CAT_TILES, 1, 1),
               blk_a=(SEQ, D_MODEL), blk_b=(CAT_TILE, D_MODEL), blk_o=(SEQ, CAT_TILE),
               map_a=whole, map_b=lambda j, i, k: (j, 0), map_o=lambda j, i, k: (0, j), tb=True, name="mm_in")
    la = _gk_fwd(h, wt_gk, wgk_pad, b_gk)
    passed = gather_pass(1, la)
    o, states = _gla_fwd(zcat, la)
    wg = gather_done(1, passed, o)
    wpp = wg["w_pool_proj"].transpose(1, 0, 2).reshape(POOL_WIDTH, D_MODEL)
    wgp = wg["w_gla_proj"].reshape(D_MODEL, D_MODEL)
    wout = wg["w_out"].reshape(D_MODEL, D_MODEL)
    og = _post_gla_fwd(o, zcat, g_gla_head)
    passed = gather_pass(2, og)
    ps = _pool_fwd(zcat, w_pool_grp[0], pool_scale)
    y_pool = _mm(ps, wpp, out_shape=(SEQ, D_MODEL), out_dtype=F32, grid=(N_MM_TILES, 1, 1),
                 blk_a=(MM_TILE, POOL_WIDTH), blk_b=(POOL_WIDTH, D_MODEL), blk_o=(MM_TILE, D_MODEL),
                 map_a=tok, map_b=whole, map_o=tok, name="mm_pool_proj")
    sq = dict(out_shape=(SEQ, D_MODEL), grid=(N_MM_TILES, 1, 1), blk_a=(MM_TILE, D_MODEL), blk_b=(D_MODEL, D_MODEL),
              blk_o=(MM_TILE, D_MODEL), map_a=tok, map_b=whole, map_o=tok)
    y_gla = _mm(og, wgp, out_dtype=F32, name="mm_gla_proj", **sq)
    mixed = _mix_fwd(zcat, b_gate, y_pool, y_gla)
    x1 = _mm(mixed, wout, out_dtype=F32, res=xs, name="mm_out", **sq)
    h2 = _rms_fwd(x1, g_ffn, "rms_ffn")
    wg = gather_done(2, passed, h2)
    wt_up = wg["w_up"].reshape(2 * D_FF, D_MODEL)
    wdown = wg["w_down"].reshape(D_FF, D_MODEL)
    wconv4 = wg["w_conv"].reshape(2, 4, 3, FF_BLK)
    bconv4 = b_conv.reshape(2, 4, 1, FF_BLK)
    blk4 = lambda b, i, k: (b // 4, b % 4, 0, 0)
    u4 = _mm(h2, wt_up, out_shape=(2, 4, SEQ, FF_BLK), out_dtype=F32, grid=(N_DEV, 1, 1),
             blk_a=(SEQ, D_MODEL), blk_b=(FF_BLK, D_MODEL), blk_o=ff_seq,
             map_a=whole, map_b=lambda b, i, k: (b, 0), map_o=blk4, tb=True, name="mm_up")
    act = _conv_fwd(u4, wconv4, bconv4)
    x2 = _mm_tokens(act, wdown, blk_a=(None, 4, TOK_MM_TILE, FF_BLK), map_a=lambda i: (0, 0, i, 0),
                    pieces=[(b, b * FF_BLK, FF_BLK) for b in range(4)], res=x1, name="mm_down")
    loss_part, dx2, dx2_bf, dg_final = _final_loss(x2, g_final.reshape(1, D_MODEL), target)

    da = _mm(dx2_bf, wdown, out_shape=(1, 4, SEQ, FF_BLK), out_dtype=BF, grid=(4, 1, 1),
             blk_a=(SEQ, D_MODEL), blk_b=(FF_BLK, D_MODEL), blk_o=ff_seq,
             map_a=whole, map_b=lambda b, i, k: (b, 0), map_o=lambda b, i, k: (0, b, 0, 0), tb=True, name="mm_d_act")
    d_wdown = _mm(act, dx2_bf, out_shape=(D_FF, D_MODEL), out_dtype=BF, grid=(4, 1, 1),
                  blk_a=ff_seq, blk_b=(SEQ, D_MODEL), blk_o=(FF_BLK, D_MODEL),
                  map_a=lambda b, i, k: (0, b, 0, 0), map_b=whole, map_o=lambda b, i, k: (b, 0), ta=True,
                  name="mm_d_wdown")
    du4, d_wconv, d_bconv = _conv_bwd(u4, da, wconv4, bconv4)
    dh2 = _mm_tokens(du4, wt_up, blk_a=(2, 4, TOK_MM_TILE, FF_BLK), map_a=lambda i: (0, 0, i, 0),
                     pieces=[((b // 4, b % 4), b * FF_BLK, FF_BLK) for b in range(N_DEV)], name="mm_d_h2")
    d_wt_up = _mm(du4, h2, out_shape=(2 * D_FF, D_MODEL), out_dtype=BF, grid=(N_DEV, 1, 1),
                  blk_a=ff_seq, blk_b=(SEQ, D_MODEL), blk_o=(FF_BLK, D_MODEL),
                  map_a=blk4, map_b=whole, map_o=lambda b, i, k: (b, 0), ta=True, name="mm_d_wup")
    res = {}

    def reduce_start(keys, parts):
        arrays = [t for k in keys for t in (parts[k], lax.empty((4,) + parts[k].shape[2:], BF))]
        st, tkn = _split_start("reduce_start_" + keys[0], [(arrays, 4 * len(keys), _reduce_first)])
        return st[0], tkn

    def reduce_cross(keys, st, after):
        arrays = _split_wait("reduce_wait_" + keys[0], st, _reduce_first, after)
        sums = [_pair_sum(p, r, core, "pair_sum_" + k) for k, p, r in zip(keys, arrays[0::2], arrays[1::2])]
        arrays = [t for s in sums for t in (s, lax.empty((3,) + s.shape[1:], BF))]
        st2, tkn = _split_start("reduce_cross_" + keys[0], [(arrays, 3 * len(keys), _reduce_second)])
        return st2[0], tkn

    def reduce_done(keys, st2, after):
        arrays = _split_wait("reduce_cross_wait_" + keys[0], st2, _reduce_second, after)
        for k, s, r in zip(keys, arrays[0::2], arrays[1::2]):
            outs = _chip_sum_adamw(s, r, big[k], moments[k][0], moments[k][1], chip, "adamw_" + k)
            res[k] = [(t.T if k in ("w_in", "w_up") else t)[None] for t in outs]

    ffn_keys = ("w_down", "w_up")
    ffn_red, tkn = reduce_start(ffn_keys, dict(w_down=d_wdown.reshape(4, 2, D_FF // N_DEV, D_MODEL),
                                               w_up=d_wt_up.reshape(4, 2, FF_BLK, D_MODEL)))
    dx1, dg_ffn = _rms_bwd(dh2, x1, g_ffn + tkn[:1, :1], dx2, "rms_ffn_bwd")

    sq_t = dict(out_shape=(D_MODEL, D_MODEL), grid=(1, 1, N_MM_TILES), blk_a=(MM_TILE, D_MODEL),
                blk_b=(MM_TILE, D_MODEL), blk_o=(D_MODEL, D_MODEL), map_a=kblk, map_b=kblk, map_o=whole, ta=True)
    dmixed = _mm(dx1, wout, out_dtype=F32, tb=True, name="mm_d_mixed", **sq)
    d_wout = _mm(mixed, dx1, out_dtype=BF, name="mm_d_wout", **sq_t)
    dzcat, dy_pool, dy_gla, db_gate = _mix_bwd(dmixed, zcat, b_gate, y_pool, y_gla)
    ffn_red, _ = reduce_cross(ffn_keys, ffn_red, db_gate)
    d_og =_mm(dy_gla, wgp, out_dtype=F32, tb=True, name="mm_d_og", **sq)
    d_wgp = _mm(og, dy_gla, out_dtype=BF, name="mm_d_wgp", **sq_t)
    mix_keys = ("w_out", "w_gla_proj")
    mix_red, tkn = reduce_start(mix_keys, dict(w_out=d_wout.reshape(4, 2, D_MODEL // N_DEV, D_MODEL),
                                               w_gla_proj=d_wgp.reshape(4, 2, D_MODEL // N_DEV, D_MODEL)))
    dzcat, d_o, dg_head = _post_gla_bwd(dzcat, d_og, o, zcat, g_gla_head + tkn[:1, :1])
    dzcat, dla = _gla_bwd(dzcat, zcat, la, d_o, states)
    mix_red, _ = reduce_cross(mix_keys, mix_red, dla)
    dh_gk, d_wt_gk, d_wgk, db_gk = _gk_bwd(dla, h, wt_gk, wgk_pad, b_gk)
    dps = _mm(dy_pool, wpp, out_shape=(SEQ, POOL_WIDTH), out_dtype=F32, grid=(N_MM_TILES, 1, 1),
              blk_a=(MM_TILE, D_MODEL), blk_b=(POOL_WIDTH, D_MODEL), blk_o=(MM_TILE, POOL_WIDTH),
              map_a=tok, map_b=whole, map_o=tok, tb=True, name="mm_d_ps")
    d_wpp = _mm(ps, dy_pool, out_shape=(POOL_WIDTH, D_MODEL), out_dtype=F32, grid=(1, 1, N_MM_TILES),
                blk_a=(MM_TILE, POOL_WIDTH), blk_b=(MM_TILE, D_MODEL), blk_o=(POOL_WIDTH, D_MODEL),
                map_a=kblk, map_b=kblk, map_o=whole, ta=True, name="mm_d_wpp")
    dzcat, d_wgrp, d_scale = _pool_bwd(dzcat, zcat, dps, w_pool_grp[0], pool_scale)
    d_wt_cat = _mm(dzcat, h, out_shape=(N_CAT, D_MODEL), out_dtype=BF, grid=(N_CAT_TILES, 1, 1),
                   blk_a=(SEQ, CAT_TILE), blk_b=(SEQ, D_MODEL), blk_o=(CAT_TILE, D_MODEL),
                   map_a=lambda j, i, k: (0, j), map_b=whole, map_o=lambda j, i, k: (j, 0), ta=True, name="mm_d_wcat")
    d_wt_in = jnp.concatenate([d_wt_cat[C_POOL:], d_wt_cat[C_QKV:C_GATE], d_wt_cat[C_OG:C_POOL],
                               d_wt_gk[:GATE_RANK].astype(BF), d_wt_cat[C_GATE:C_OG]], axis=0)
    in_keys = ("w_in", "w_pool_proj")
    in_red, tkn = reduce_start(in_keys, dict(
        w_in=d_wt_in.reshape(4, 2, IN_SHARD, D_MODEL),
        w_pool_proj=d_wpp.reshape(POOL_WIDTH, N_DEV, D_MODEL // N_DEV).transpose(1, 0, 2).astype(BF)
        .reshape(4, 2, POOL_WIDTH, D_MODEL // N_DEV)))
    dh = _mm_tokens(dzcat, wt_cat, blk_a=(TOK_MM_TILE, N_CAT), map_a=lambda i: (i, 0), pieces=[(None, 0, N_CAT)],
                    res=dh_gk, name="mm_d_h")
    in_red, tkn = reduce_cross(in_keys, in_red, dh)
    grad_x, dg_mix = _rms_bwd(dh, xs, g_mix + tkn[:1, :1], dx1, "rms_mix_bwd")
    reduce_done(ffn_keys, ffn_red, grad_x)
    reduce_done(mix_keys, mix_red, res["w_down"][0])

    small = [("g_mix", dg_mix, g_mix, m_g_mix, v_g_mix), ("b_gate", db_gate, b_gate, m_b_gate, v_b_gate),
             ("w_gk_up", d_wgk[:GATE_RANK], None, None, None), ("b_gk", db_gk, b_gk, m_b_gk, v_b_gk),
             ("w_pool_grp", d_wgrp, w_pool_grp, m_w_pool_grp, v_w_pool_grp),
             ("pool_scale", d_scale, pool_scale, m_pool_scale, v_pool_scale),
             ("g_gla_head", dg_head, g_gla_head, m_g_gla_head, v_g_gla_head), ("g_ffn", dg_ffn, g_ffn, m_g_ffn, v_g_ffn),
             ("w_conv", d_wconv, None, None, None), ("b_conv", d_bconv, b_conv, m_b_conv, v_b_conv),
             ("g_final", dg_final, g_final, m_g_final, v_g_final), ("loss", loss_part, None, None, None)]
    zeros_like_part = lambda t: jnp.zeros(t[1].shape, F32)
    g_all = _all_gather([_pack([t[1] for t in small], SMALL_ROWS)], "gather_small_grads")[0]
    packed = _sum8_adamw(g_all, *[_pack([zeros_like_part(t) if t[i] is None else t[i] for t in small], SMALL_ROWS)
                                  for i in (2, 3, 4)])
    shapes = [t[1].shape if t[2] is None else t[2].shape for t in small]
    unpacked = [_unpack(p, shapes) for p in packed]
    for idx, t in enumerate(small):
        if t[2] is not None:
            res[t[0]] = [unpacked[q][idx] for q in range(4)]
    g_wgk = lax.dynamic_slice(unpacked[0][2], (0, me * (GLA_DK // N_DEV)), (GATE_RANK, GLA_DK // N_DEV))
    g_wconv = lax.dynamic_index_in_dim(unpacked[0][8].reshape(N_DEV, 3, FF_BLK), me, axis=0, keepdims=False)
    shard_shapes = [(1, GATE_RANK, GLA_DK // N_DEV), (1, 3, FF_BLK)]
    shard_out = _plain_adamw(_pack([g_wgk, g_wconv], SHARD_ROWS), _pack([w_gk_up, w_conv], SHARD_ROWS),
                             _pack([m_w_gk_up, m_w_conv], SHARD_ROWS), _pack([v_w_gk_up, v_w_conv], SHARD_ROWS))
    shard_un = [_unpack(p, shard_shapes) for p in shard_out]
    res["w_gk_up"] = [g_wgk[None]] + [s[0] for s in shard_un]
    res["w_conv"] = [g_wconv[None]] + [s[1] for s in shard_un]

    reduce_done(in_keys, in_red, shard_out[0])
    loss = unpacked[0][11][0, 0]
    order =["g_mix", "w_in", "b_gate", "w_gk_up", "b_gk", "w_pool_grp", "pool_scale", "g_gla_head", "w_pool_proj",
             "w_gla_proj", "w_out", "g_ffn", "w_up", "w_conv", "b_conv", "w_down", "g_final"]
    return (loss, grad_x[None], *[res[k][0] for k in order], *[res[k][1] for k in order],
            *[res[k][2] for k in order], *[res[k][3] for k in order])
```

```python
import functools

import jax
import jax.numpy as jnp
from jax import lax
from jax.experimental import pallas as pl
from jax.experimental.pallas import tpu as pltpu

F32 = jnp.float32
BF = jnp.bfloat16
HIGHEST = lax.Precision.HIGHEST
MESH = pl.DeviceIdType.MESH

N_DEV = 8
SEQ = 2048
D_MODEL = 1024
CHUNK = 64
EPS = 1e-6
POOL_WIDTH = 512
POOL_WINDOWS = (2, 4, 8, 16)
POOL_GD = 128
POOL_HALO = 16
HEADS = 4
HK = 128
HV = 256
GLA_DK = 512
GATE_RANK = 16
GATE_NORM = 16.0
D_FF = 2816
FF_BLK = 704
IN_TOTAL = 5648
IN_SHARD = 706
C_QKV, C_GATE, C_OG, C_POOL = 0, 2048, 4096, 5120
N_CAT = 5632
R_POOL, R_QKV, R_OG, R_GK, R_GATE = 0, 512, 2560, 3584, 3600
GK_PAD = 128

ADAM_LR, ADAM_B1, ADAM_B2, ADAM_EPS, ADAM_WD, ADAM_STEP = 0.001, 0.9, 0.999, 1e-08, 0.01, 10
ADAM_C1 = 1.0 - ADAM_B1 ** ADAM_STEP
ADAM_C2 = 1.0 - ADAM_B2 ** ADAM_STEP

VMEM_BYTES_V7X = 64 * 1024 * 1024
VMEM_LIMIT = 48 * 1024 * 1024

TOK_TILE = 256
HALO = 8
GLA_CPS = 4


def _params(*sem):
    return pltpu.CompilerParams(dimension_semantics=sem, vmem_limit_bytes=VMEM_LIMIT)


def _const_spec(shape):
    nd = len(shape)
    return pl.BlockSpec(shape, lambda *_: (0,) * nd)


def _in_hbm(t):
    return pltpu.with_memory_space_constraint(t, pltpu.HBM)


def _dot(a, b, ta=False, tb=False):
    dims = (((0 if ta else 1,), (1 if tb else 0,)), ((), ()))
    return lax.dot_general(a.astype(BF), b.astype(BF), dims, preferred_element_type=F32)


def _dot_exact(a, b):
    return jnp.dot(a, b, precision=HIGHEST, preferred_element_type=F32)


def _sigmoid(x):
    return 0.5 * jnp.tanh(0.5 * x) + 0.5


def _mm(a, b, *, out_shape, out_dtype, grid, blk_a, blk_b, blk_o, map_a, map_b, map_o, ta=False, tb=False,
        res=None, name):
    gk = grid[2]

    def body(*refs):
        if res is None:
            a_ref, b_ref, o_ref = refs[:3]
            r_ref = None
            scr = refs[3:]
        else:
            a_ref, b_ref, r_ref, o_ref = refs[:4]
            scr = refs[4:]
        prod = _dot(a_ref[...], b_ref[...], ta, tb)

        def finish(total):
            if r_ref is not None:
                total = total + r_ref[...]
            o_ref[...] = total.astype(out_dtype)

        if gk == 1:
            finish(prod)
        else:
            acc = scr[0]
            k = pl.program_id(2)

            @pl.when(k == 0)
            def _():
                acc[...] = prod

            @pl.when(k > 0)
            def _():
                acc[...] += prod

            @pl.when(k == gk - 1)
            def _():
                finish(acc[...])

    in_specs = [pl.BlockSpec(blk_a, map_a), pl.BlockSpec(blk_b, map_b)]
    args = [a, b]
    if res is not None:
        in_specs.append(pl.BlockSpec(blk_o, map_o))
        args.append(res)
    return pl.pallas_call(
        body, name=name, grid=grid, in_specs=in_specs, out_specs=pl.BlockSpec(blk_o, map_o),
        out_shape=jax.ShapeDtypeStruct(out_shape, out_dtype),
        scratch_shapes=[] if gk == 1 else [pltpu.VMEM(tuple(d for d in blk_o if d is not None), F32)],
        compiler_params=_params("parallel", "parallel", "arbitrary"),
    )(*[_in_hbm(t) for t in args])


TOK_MM_TILE = 256


def _mm_tokens(a, w, *, blk_a, map_a, pieces, res=None, name):
    def body(*refs):
        a_ref, w_ref = refs[:2]
        o_ref = refs[-1]
        total = None
        for idx, row, n in pieces:
            av = a_ref[...] if idx is None else a_ref[idx]
            prod = _dot(av, w_ref[row:row + n, :])
            total = prod if total is None else total + prod
        if res is not None:
            total = total + refs[2][...]
        o_ref[...] = total

    tile = pl.BlockSpec((TOK_MM_TILE, D_MODEL), lambda i: (i, 0))
    in_specs = [pl.BlockSpec(blk_a, map_a), _const_spec(w.shape)]
    args = [a, w]
    if res is not None:
        in_specs.append(tile)
        args.append(res)
    return pl.pallas_call(
        body, name=name, grid=(SEQ // TOK_MM_TILE,), in_specs=in_specs, out_specs=tile,
        out_shape=jax.ShapeDtypeStruct((SEQ, D_MODEL), F32), compiler_params=_params("parallel"),
    )(*[_in_hbm(t) for t in args])


def _rms_fwd(x, g, name):
    def body(x_ref, g_ref, o_ref):
        xv = x_ref[...]
        r = lax.rsqrt(jnp.mean(xv * xv, axis=-1, keepdims=True) + EPS)
        o_ref[...] = (xv * r * g_ref[...]).astype(BF)

    tile = pl.BlockSpec((TOK_TILE, D_MODEL), lambda i: (i, 0))
    return pl.pallas_call(
        body, name=name, grid=(SEQ // TOK_TILE,), in_specs=[tile, _const_spec((1, D_MODEL))], out_specs=tile,
        out_shape=jax.ShapeDtypeStruct((SEQ, D_MODEL), BF), compiler_params=_params("parallel"),
    )(*map(_in_hbm, (x, g)))


def _rms_bwd(dy, x, g, dres, name):
    def body(dy_ref, x_ref, g_ref, dres_ref, dx_ref, dg_ref):
        xv = x_ref[...]
        r = lax.rsqrt(jnp.mean(xv * xv, axis=-1, keepdims=True) + EPS)
        xn = xv * r
        dyv = dy_ref[...]
        dxn = dyv * g_ref[...]
        dx_ref[...] = dres_ref[...] + r * (dxn - xn * jnp.mean(dxn * xn, axis=-1, keepdims=True))
        part = jnp.sum(dyv * xn, axis=0, keepdims=True)

        @pl.when(pl.program_id(0) == 0)
        def _():
            dg_ref[...] = part

        @pl.when(pl.program_id(0) > 0)
        def _():
            dg_ref[...] += part

    tile = pl.BlockSpec((TOK_TILE, D_MODEL), lambda i: (i, 0))
    vec = _const_spec((1, D_MODEL))
    return pl.pallas_call(
        body, name=name, grid=(SEQ // TOK_TILE,), in_specs=[tile, tile, vec, tile], out_specs=[tile, vec],
        out_shape=[jax.ShapeDtypeStruct((SEQ, D_MODEL), F32), jax.ShapeDtypeStruct((1, D_MODEL), F32)],
        compiler_params=_params("arbitrary"),
    )(*map(_in_hbm, (dy, x, g, dres)))


def _final_loss(x2, g, target):
    def body(x_ref, g_ref, t_ref, loss_ref, dx_ref, dxb_ref, dg_ref):
        xv = x_ref[...]
        r = lax.rsqrt(jnp.mean(xv * xv, axis=-1, keepdims=True) + EPS)
        xn = xv * r
        gv = g_ref[...]
        err = xn * gv - t_ref[...]
        lpart = jnp.full((1, 128), 0.5 * jnp.sum(jnp.mean(err * err, axis=-1, keepdims=True)), F32)
        dyv = err * (1.0 / D_MODEL)
        dxn = dyv * gv
        dxv = r * (dxn - xn * jnp.mean(dxn * xn, axis=-1, keepdims=True))
        dx_ref[...] = dxv
        dxb_ref[...] = dxv.astype(BF)
        gpart = jnp.sum(dyv * xn, axis=0, keepdims=True)

        @pl.when(pl.program_id(0) == 0)
        def _():
            loss_ref[...] = lpart
            dg_ref[...] = gpart

        @pl.when(pl.program_id(0) > 0)
        def _():
            loss_ref[...] += lpart
            dg_ref[...] += gpart

    tile = pl.BlockSpec((TOK_TILE, D_MODEL), lambda i: (i, 0))
    vec = _const_spec((1, D_MODEL))
    return pl.pallas_call(
        body, name="final_loss", grid=(SEQ // TOK_TILE,), in_specs=[tile, vec, tile],
        out_specs=[_const_spec((1, 128)), tile, tile, vec],
        out_shape=[jax.ShapeDtypeStruct((1, 128), F32), jax.ShapeDtypeStruct((SEQ, D_MODEL), F32),
                   jax.ShapeDtypeStruct((SEQ, D_MODEL), BF), jax.ShapeDtypeStruct((1, D_MODEL), F32)],
        compiler_params=_params("arbitrary"),
    )(*map(_in_hbm, (x2, g, target)))


def _pool_counts(w):
    pos = lax.broadcasted_iota(jnp.int32, (SEQ, 1), 0).astype(F32)
    return jnp.minimum(pos + 1.0, float(w))


def _pool_window(u, w, ext):
    ext[pl.ds(POOL_HALO, SEQ), :] = u
    win = u
    for j in range(1, w):
        win = win + ext[pl.ds(POOL_HALO - j, SEQ), :]
    return win / _pool_counts(w) - u


def _pool_fwd(zcat, w_grp, scale):
    def body(z_ref, w_ref, s_ref, o_ref, ext):
        ext[pl.ds(0, POOL_HALO), :] = jnp.zeros((POOL_HALO, POOL_GD), F32)
        for g, w in enumerate(POOL_WINDOWS):
            cols = slice(g * POOL_GD, (g + 1) * POOL_GD)
            p = _pool_window(z_ref[:, cols], w, ext)
            o_ref[:, cols] = (_dot(p, w_ref[g]) * s_ref[:, cols]).astype(BF)

    return pl.pallas_call(
        body, name="pool_fwd", grid=(1,),
        in_specs=[pl.BlockSpec((SEQ, POOL_WIDTH), lambda i: (0, C_POOL // POOL_WIDTH)),
                  _const_spec((4, POOL_GD, POOL_GD)), _const_spec((1, POOL_WIDTH))],
        out_specs=_const_spec((SEQ, POOL_WIDTH)), out_shape=jax.ShapeDtypeStruct((SEQ, POOL_WIDTH), BF),
        scratch_shapes=[pltpu.VMEM((POOL_HALO + SEQ, POOL_GD), F32)], compiler_params=_params("arbitrary"),
    )(*map(_in_hbm, (zcat, w_grp, scale)))


def _pool_bwd(dzcat, zcat, dps, w_grp, scale):
    def body(dz_in, z_ref, dps_ref, w_ref, s_ref, dz_ref, dw_ref, dsc_ref, ext, ext2):
        del dz_in
        ext[pl.ds(0, POOL_HALO), :] = jnp.zeros((POOL_HALO, POOL_GD), F32)
        ext2[pl.ds(SEQ, POOL_HALO), :] = jnp.zeros((POOL_HALO, POOL_GD), F32)
        for g, w in enumerate(POOL_WINDOWS):
            cols = slice(g * POOL_GD, (g + 1) * POOL_GD)
            p = _pool_window(z_ref[:, cols], w, ext)
            wg = w_ref[g]
            pg = _dot(p, wg)
            dpsv = dps_ref[:, cols]
            dsc_ref[:, cols] = jnp.sum(dpsv * pg, axis=0, keepdims=True)
            dpg = dpsv * s_ref[:, cols]
            dw_ref[g] = _dot(p, dpg, ta=True)
            dp = _dot(dpg, wg, tb=True)
            dpc = dp / _pool_counts(w)
            ext2[pl.ds(0, SEQ), :] = dpc
            du = dpc
            for j in range(1, w):
                du = du + ext2[pl.ds(j, SEQ), :]
            dz_ref[:, cols] = (du - dp).astype(BF)

    return pl.pallas_call(
        body, name="pool_bwd", grid=(1,),
        in_specs=[pl.BlockSpec(memory_space=pl.ANY),
                  pl.BlockSpec((SEQ, POOL_WIDTH), lambda i: (0, C_POOL // POOL_WIDTH)),
                  _const_spec((SEQ, POOL_WIDTH)), _const_spec((4, POOL_GD, POOL_GD)), _const_spec((1, POOL_WIDTH))],
        out_specs=[pl.BlockSpec((SEQ, POOL_WIDTH), lambda i: (0, C_POOL // POOL_WIDTH)),
                   _const_spec((4, POOL_GD, POOL_GD)), _const_spec((1, POOL_WIDTH))],
        out_shape=[jax.ShapeDtypeStruct((SEQ, N_CAT), BF), jax.ShapeDtypeStruct((4, POOL_GD, POOL_GD), F32),
                   jax.ShapeDtypeStruct((1, POOL_WIDTH), F32)],
        scratch_shapes=[pltpu.VMEM((POOL_HALO + SEQ, POOL_GD), F32), pltpu.VMEM((SEQ + POOL_HALO, POOL_GD), F32)],
        input_output_aliases={0: 0}, compiler_params=_params("arbitrary"),
    )(*map(_in_hbm, (dzcat, zcat, dps, w_grp, scale)))


GK_TILE = 512


def _gk_fwd(h, wt_gk, wgk_pad, b_gk):
    def body(h_ref, wt_ref, w_ref, b_ref, la_ref):
        z_gk = _dot(h_ref[...], wt_ref[...], tb=True)
        pre = _dot(z_gk, w_ref[...]) + b_ref[...]
        la_ref[...] = (jnp.minimum(pre, 0.0) - jnp.log(1.0 + jnp.exp(-jnp.abs(pre)))) * (1.0 / GATE_NORM)

    return pl.pallas_call(
        body, name="gk_fwd", grid=(SEQ // GK_TILE,),
        in_specs=[pl.BlockSpec((GK_TILE, D_MODEL), lambda i: (i, 0)), _const_spec((GK_PAD, D_MODEL)),
                  _const_spec((GK_PAD, GLA_DK)), _const_spec((1, GLA_DK))],
        out_specs=pl.BlockSpec((GK_TILE, GLA_DK), lambda i: (i, 0)),
        out_shape=jax.ShapeDtypeStruct((SEQ, GLA_DK), F32), compiler_params=_params("parallel"),
    )(*map(_in_hbm, (h, wt_gk, wgk_pad, b_gk)))


def _gk_bwd(dla, h, wt_gk, wgk_pad, b_gk):
    def body(dla_ref, h_ref, wt_ref, w_ref, b_ref, dh_ref, dwt_ref, dw_ref, db_ref):
        hv = h_ref[...]
        wtv = wt_ref[...]
        wv = w_ref[...]
        z_gk = _dot(hv, wtv, tb=True)
        pre = _dot(z_gk, wv) + b_ref[...]
        dpre = dla_ref[...] * (1.0 / GATE_NORM) * (1.0 - _sigmoid(pre))
        dz_gk = _dot(dpre, wv, tb=True)
        dh_ref[...] = _dot(dz_gk, wtv)
        dwtp = _dot(dz_gk, hv, ta=True)
        dwp = _dot(z_gk, dpre, ta=True)[:GATE_RANK]
        dbp = jnp.sum(dpre, axis=0, keepdims=True)

        @pl.when(pl.program_id(0) == 0)
        def _():
            dwt_ref[...] = dwtp
            dw_ref[...] = dwp
            db_ref[...] = dbp

        @pl.when(pl.program_id(0) > 0)
        def _():
            dwt_ref[...] += dwtp
            dw_ref[...] += dwp
            db_ref[...] += dbp

    tile = pl.BlockSpec((GK_TILE, D_MODEL), lambda i: (i, 0))
    return pl.pallas_call(
        body, name="gk_bwd", grid=(SEQ // GK_TILE,),
        in_specs=[pl.BlockSpec((GK_TILE, GLA_DK), lambda i: (i, 0)), tile, _const_spec((GK_PAD, D_MODEL)),
                  _const_spec((GK_PAD, GLA_DK)), _const_spec((1, GLA_DK))],
        out_specs=[tile, _const_spec((GK_PAD, D_MODEL)), _const_spec((GATE_RANK, GLA_DK)), _const_spec((1, GLA_DK))],
        out_shape=[jax.ShapeDtypeStruct((SEQ, D_MODEL), F32), jax.ShapeDtypeStruct((GK_PAD, D_MODEL), F32),
                   jax.ShapeDtypeStruct((GATE_RANK, GLA_DK), F32), jax.ShapeDtypeStruct((1, GLA_DK), F32)],
        compiler_params=_params("arbitrary"),
    )(*map(_in_hbm, (dla, h, wt_gk, wgk_pad, b_gk)))


GLA_ROWS = GLA_CPS * CHUNK
GLA_STEPS = SEQ // GLA_ROWS
QKV_W = 2048


def _gla_chunk(qkv_ref, la_ref, rows, h):
    tri = lax.broadcasted_iota(jnp.int32, (CHUNK, CHUNK), 0) >= lax.broadcasted_iota(jnp.int32, (CHUNK, CHUNK), 1)
    q = qkv_ref[rows, h * HK:(h + 1) * HK] * (HK ** -0.5)
    k = qkv_ref[rows, GLA_DK + h * HK:GLA_DK + (h + 1) * HK]
    v = qkv_ref[rows, 2 * GLA_DK + h * HV:2 * GLA_DK + (h + 1) * HV]
    la = la_ref[rows, h * HK:(h + 1) * HK]
    bc = _dot_exact(tri.astype(F32), la)
    e_pos, e_neg = jnp.exp(bc), jnp.exp(-bc)
    dl = jnp.exp(jnp.sum(la, axis=0, keepdims=True))
    q_fw, q_bw, k_fw, k_bw = q * e_pos, q * e_neg, k * e_neg, k * e_pos
    scores = jnp.where(tri, _dot(q_fw, k_fw, tb=True), _dot(q_bw, k_bw, tb=True))
    return tri, v, e_pos, e_neg, dl, q_fw, q_bw, k_fw, k_bw, scores


def _gla_fwd(zcat, la):
    def body(qkv_ref, la_ref, o_ref, st_ref, state):
        @pl.when(pl.program_id(0) == 0)
        def _():
            state[...] = jnp.zeros_like(state)

        for c in range(GLA_CPS):
            rows = slice(c * CHUNK, (c + 1) * CHUNK)
            for h in range(HEADS):
                _, v, _, _, dl, q_fw, _, k_fw, _, scores = _gla_chunk(qkv_ref, la_ref, rows, h)
                st = state[h]
                st_ref[c, h] = st
                o_ref[rows, h * HV:(h + 1) * HV] = _dot(scores, v) + _dot(q_fw, st, tb=True)
                state[h] = st * dl + _dot(v, k_fw * dl, ta=True)

    return pl.pallas_call(
        body, name="gla_fwd", grid=(GLA_STEPS,),
        in_specs=[pl.BlockSpec((GLA_ROWS, QKV_W), lambda i: (i, 0)), pl.BlockSpec((GLA_ROWS, GLA_DK), lambda i: (i, 0))],
        out_specs=[pl.BlockSpec((GLA_ROWS, D_MODEL), lambda i: (i, 0)),
                   pl.BlockSpec((GLA_CPS, HEADS, HV, HK), lambda i: (i, 0, 0, 0))],
        out_shape=[jax.ShapeDtypeStruct((SEQ, D_MODEL), F32),
                   jax.ShapeDtypeStruct((SEQ // CHUNK, HEADS, HV, HK), F32)],
        scratch_shapes=[pltpu.VMEM((HEADS, HV, HK), F32)], compiler_params=_params("arbitrary"),
    )(*map(_in_hbm, (zcat, la)))


def _gla_bwd(dzcat, zcat, la, d_o, states):
    def body(dz_in, qkv_ref, la_ref, do_ref, st_ref, dqkv_ref, dla_ref, dstate):
        del dz_in

        @pl.when(pl.program_id(0) == 0)
        def _():
            dstate[...] = jnp.zeros_like(dstate)

        last_row = lax.broadcasted_iota(jnp.int32, (CHUNK, HK), 0) == CHUNK - 1
        upper = (lax.broadcasted_iota(jnp.int32, (CHUNK, CHUNK), 0)
                 <= lax.broadcasted_iota(jnp.int32, (CHUNK, CHUNK), 1)).astype(F32)
        for c in reversed(range(GLA_CPS)):
            rows = slice(c * CHUNK, (c + 1) * CHUNK)
            for h in range(HEADS):
                tri, v, e_pos, e_neg, dl, q_fw, q_bw, k_fw, k_bw, scores = _gla_chunk(qkv_ref, la_ref, rows, h)
                st = st_ref[c, h]
                dst = dstate[h]
                d_out = do_ref[rows, h * HV:(h + 1) * HV]
                k_dec = k_fw * dl
                dp = _dot(d_out, v, tb=True)
                dp_fw = jnp.where(tri, dp, 0.0)
                dp_bw = jnp.where(tri, 0.0, dp)
                dv = _dot(scores, d_out, ta=True) + _dot(k_dec, dst, tb=True)
                dk_dec = _dot(v, dst)
                dq_fw = _dot(dp_fw, k_fw) + _dot(d_out, st)
                dk_fw = _dot(dp_fw, q_fw, ta=True) + dk_dec * dl
                dq_bw = _dot(dp_bw, k_bw)
                dk_bw = _dot(dp_bw, q_bw, ta=True)
                ddl = jnp.sum(st * dst, axis=0, keepdims=True) + jnp.sum(k_fw * dk_dec, axis=0, keepdims=True)
                dstate[h] = dst * dl + _dot(d_out, q_fw, ta=True)
                dq = (dq_fw * e_pos + dq_bw * e_neg) * (HK ** -0.5)
                dk = dk_fw * e_neg + dk_bw * e_pos
                db = dq_fw * q_fw - dk_fw * k_fw - dq_bw * q_bw + dk_bw * k_bw + jnp.where(last_row, ddl * dl, 0.0)
                dla_ref[rows, h * HK:(h + 1) * HK] = _dot_exact(upper, db)
                dqkv_ref[rows, h * HK:(h + 1) * HK] = dq.astype(BF)
                dqkv_ref[rows, GLA_DK + h * HK:GLA_DK + (h + 1) * HK] = dk.astype(BF)
                dqkv_ref[rows, 2 * GLA_DK + h * HV:2 * GLA_DK + (h + 1) * HV] = dv.astype(BF)

    rev = lambda i: (GLA_STEPS - 1 - i, 0)
    return pl.pallas_call(
        body, name="gla_bwd", grid=(GLA_STEPS,),
        in_specs=[pl.BlockSpec(memory_space=pl.ANY), pl.BlockSpec((GLA_ROWS, QKV_W), rev),
                  pl.BlockSpec((GLA_ROWS, GLA_DK), rev), pl.BlockSpec((GLA_ROWS, D_MODEL), rev),
                  pl.BlockSpec((GLA_CPS, HEADS, HV, HK), lambda i: (GLA_STEPS - 1 - i, 0, 0, 0))],
        out_specs=[pl.BlockSpec((GLA_ROWS, QKV_W), rev), pl.BlockSpec((GLA_ROWS, GLA_DK), rev)],
        out_shape=[jax.ShapeDtypeStruct((SEQ, N_CAT), BF), jax.ShapeDtypeStruct((SEQ, GLA_DK), F32)],
        scratch_shapes=[pltpu.VMEM((HEADS, HV, HK), F32)], input_output_aliases={0: 0},
        compiler_params=_params("arbitrary"),
    )(*map(_in_hbm, (dzcat, zcat, la, d_o, states)))


def _silu_parts(x):
    s = _sigmoid(x)
    return x * s, s * (1.0 + x * (1.0 - s))


def _post_gla_fwd(o, zcat, g_head):
    def body(o_ref, zog_ref, g_ref, out_ref):
        for h in range(HEADS):
            cols = slice(h * HV, (h + 1) * HV)
            ov = o_ref[:, cols]
            r = lax.rsqrt(jnp.mean(ov * ov, axis=-1, keepdims=True) + EPS)
            act, _ = _silu_parts(zog_ref[:, cols])
            out_ref[:, cols] = (ov * r * g_ref[...] * act).astype(BF)

    tile = pl.BlockSpec((TOK_TILE, D_MODEL), lambda i: (i, 0))
    return pl.pallas_call(
        body, name="post_gla_fwd", grid=(SEQ // TOK_TILE,),
        in_specs=[tile, pl.BlockSpec((TOK_TILE, D_MODEL), lambda i: (i, C_OG // D_MODEL)), _const_spec((1, HV))],
        out_specs=tile, out_shape=jax.ShapeDtypeStruct((SEQ, D_MODEL), BF), compiler_params=_params("parallel"),
    )(*map(_in_hbm, (o, zcat, g_head)))


def _post_gla_bwd(dzcat, d_og, o, zcat, g_head):
    def body(dz_in, dog_ref, o_ref, zog_ref, g_ref, dz_ref, do_ref, dg_ref):
        del dz_in
        gpart = jnp.zeros((1, HV), F32)
        gv = g_ref[...]
        for h in range(HEADS):
            cols = slice(h * HV, (h + 1) * HV)
            ov = o_ref[:, cols]
            r = lax.rsqrt(jnp.mean(ov * ov, axis=-1, keepdims=True) + EPS)
            on = ov * r
            act, dact = _silu_parts(zog_ref[:, cols])
            dogv = dog_ref[:, cols]
            dz_ref[:, cols] = (dogv * on * gv * dact).astype(BF)
            d_on_g = dogv * act
            gpart = gpart + jnp.sum(d_on_g * on, axis=0, keepdims=True)
            dxn = d_on_g * gv
            do_ref[:, cols] = r * (dxn - on * jnp.mean(dxn * on, axis=-1, keepdims=True))

        @pl.when(pl.program_id(0) == 0)
        def _():
            dg_ref[...] = gpart

        @pl.when(pl.program_id(0) > 0)
        def _():
            dg_ref[...] += gpart

    tile = pl.BlockSpec((TOK_TILE, D_MODEL), lambda i: (i, 0))
    ogspec = pl.BlockSpec((TOK_TILE, D_MODEL), lambda i: (i, C_OG // D_MODEL))
    return pl.pallas_call(
        body, name="post_gla_bwd", grid=(SEQ // TOK_TILE,),
        in_specs=[pl.BlockSpec(memory_space=pl.ANY), tile, tile, ogspec, _const_spec((1, HV))],
        out_specs=[ogspec, tile, _const_spec((1, HV))],
        out_shape=[jax.ShapeDtypeStruct((SEQ, N_CAT), BF), jax.ShapeDtypeStruct((SEQ, D_MODEL), F32),
                   jax.ShapeDtypeStruct((1, HV), F32)],
        input_output_aliases={0: 0}, compiler_params=_params("arbitrary"),
    )(*map(_in_hbm, (dzcat, d_og, o, zcat, g_head)))


GATE_W = 2 * D_MODEL


def _mix_fwd(zcat, b_gate, y_pool, y_gla):
    def body(zg_ref, b_ref, yp_ref, yg_ref, out_ref):
        g0 = _sigmoid(zg_ref[:, :D_MODEL] + b_ref[:, :D_MODEL])
        g1 = _sigmoid(zg_ref[:, D_MODEL:] + b_ref[:, D_MODEL:])
        out_ref[...] = (g0 * yp_ref[...] + g1 * yg_ref[...]).astype(BF)

    tile = pl.BlockSpec((TOK_TILE, D_MODEL), lambda i: (i, 0))
    return pl.pallas_call(
        body, name="mix_fwd", grid=(SEQ // TOK_TILE,),
        in_specs=[pl.BlockSpec((TOK_TILE, GATE_W), lambda i: (i, C_GATE // GATE_W)), _const_spec((1, GATE_W)), tile, tile],
        out_specs=tile, out_shape=jax.ShapeDtypeStruct((SEQ, D_MODEL), BF), compiler_params=_params("parallel"),
    )(*map(_in_hbm, (zcat, b_gate, y_pool, y_gla)))


def _mix_bwd(dmixed, zcat, b_gate, y_pool, y_gla):
    def body(dm_ref, zg_ref, b_ref, yp_ref, yg_ref, dz_ref, dyp_ref, dyg_ref, db_ref):
        dm = dm_ref[...]
        g0 = _sigmoid(zg_ref[:, :D_MODEL] + b_ref[:, :D_MODEL])
        g1 = _sigmoid(zg_ref[:, D_MODEL:] + b_ref[:, D_MODEL:])
        dyp_ref[...] = (dm * g0).astype(BF)
        dyg_ref[...] = (dm * g1).astype(BF)
        dz0 = dm * yp_ref[...] * g0 * (1.0 - g0)
        dz1 = dm * yg_ref[...] * g1 * (1.0 - g1)
        dz_ref[:, :D_MODEL] = dz0.astype(BF)
        dz_ref[:, D_MODEL:] = dz1.astype(BF)
        b0 = jnp.sum(dz0, axis=0, keepdims=True)
        b1 = jnp.sum(dz1, axis=0, keepdims=True)

        @pl.when(pl.program_id(0) == 0)
        def _():
            db_ref[:, :D_MODEL] = b0
            db_ref[:, D_MODEL:] = b1

        @pl.when(pl.program_id(0) > 0)
        def _():
            db_ref[:, :D_MODEL] += b0
            db_ref[:, D_MODEL:] += b1

    tile = pl.BlockSpec((TOK_TILE, D_MODEL), lambda i: (i, 0))
    gspec = pl.BlockSpec((TOK_TILE, GATE_W), lambda i: (i, C_GATE // GATE_W))
    return pl.pallas_call(
        body, name="mix_bwd", grid=(SEQ // TOK_TILE,),
        in_specs=[tile, gspec, _const_spec((1, GATE_W)), tile, tile],
        out_specs=[gspec, tile, tile, _const_spec((1, GATE_W))],
        out_shape=[jax.ShapeDtypeStruct((SEQ, N_CAT), BF), jax.ShapeDtypeStruct((SEQ, D_MODEL), BF),
                   jax.ShapeDtypeStruct((SEQ, D_MODEL), BF), jax.ShapeDtypeStruct((1, GATE_W), F32)],
        compiler_params=_params("arbitrary"),
    )(*map(_in_hbm, (dmixed, zcat, b_gate, y_pool, y_gla)))


N_TOK_TILES = SEQ // TOK_TILE
HALO_PER_TILE = TOK_TILE // HALO


LANE_TILES = tuple((lo, min(128, FF_BLK - lo)) for lo in range(0, FF_BLK, 128))


def _taps(w_ref, b_ref, half, lanes, rows):
    shape = (rows, lanes.stop - lanes.start)
    return ([jnp.broadcast_to(w_ref[half, j:j + 1, lanes], shape) for j in range(3)],
            jnp.broadcast_to(b_ref[half, :, lanes], shape))


def _conv_strips(u_ref, ub_ref, ua_ref, taps, lanes, width, n_strips):
    first = pl.program_id(1) == 0
    row = lax.broadcasted_iota(jnp.int32, (HALO, width), 0)
    prev = [[pltpu.roll(jnp.where(first, 0.0, ub_ref[half, :, lanes]), k, 0) for k in (1, 2)] for half in range(2)]
    for s in range(n_strips + (ua_ref is not None)):
        u3, conv = [], []
        for half in range(2):
            cur = u_ref[half, s * HALO:(s + 1) * HALO, lanes] if s < n_strips else ua_ref[half, :, lanes]
            rolled = [pltpu.roll(cur, k, 0) for k in (1, 2)]
            frames = [jnp.where(row >= 2, rolled[1], prev[half][1]), jnp.where(row >= 1, rolled[0], prev[half][0]), cur]
            prev[half] = rolled
            w3, bias = taps[half]
            u3.append(frames)
            conv.append(bias + frames[0] * w3[0] + frames[1] * w3[1] + frames[2] * w3[2])
        yield s, u3, conv


def _pair_specs(pairs):
    tile = pl.BlockSpec((pairs, None, TOK_TILE, FF_BLK), lambda b, i: (0, b, i, 0))
    before = pl.BlockSpec((pairs, None, HALO, FF_BLK), lambda b, i: (0, b, jnp.maximum(i * HALO_PER_TILE - 1, 0), 0))
    after = pl.BlockSpec((pairs, None, HALO, FF_BLK),
                         lambda b, i: (0, b, jnp.minimum((i + 1) * HALO_PER_TILE, SEQ // HALO - 1), 0))

    def vec(rows):
        return pl.BlockSpec((2, None, rows, FF_BLK), lambda b, i: (0, b, 0, 0))

    return tile, before, after, vec


N_STRIPS = TOK_TILE // HALO


def _conv_fwd(u, w_conv, b_conv):
    def body(u_ref, ub_ref, w_ref, b_ref, a_ref):
        for lo, width in LANE_TILES:
            lanes = slice(lo, lo + width)
            taps = [_taps(w_ref, b_ref, half, lanes, HALO) for half in range(2)]
            pending = None
            for s, _, (cg, cv) in _conv_strips(u_ref, ub_ref, None, taps, lanes, width, N_STRIPS):
                act = cg * _sigmoid(cg) * cv
                if s % 2 == 0:
                    pending = act
                else:
                    a_ref[0, (s - 1) * HALO:(s + 1) * HALO, lanes] = jnp.concatenate([pending, act], axis=0).astype(BF)

    tile, before, _, vec = _pair_specs(2)
    out_tile, _, _, _ = _pair_specs(1)
    return pl.pallas_call(
        body, name="conv_fwd", grid=(4, N_TOK_TILES), in_specs=[tile, before, vec(3), vec(1)],
        out_specs=out_tile, out_shape=jax.ShapeDtypeStruct((1, 4, SEQ, FF_BLK), BF),
        compiler_params=_params("parallel", "parallel"),
    )(*map(_in_hbm, (u, u, w_conv, b_conv)))


def _conv_bwd(u, da, w_conv, b_conv):
    def body(u_ref, ub_ref, ua_ref, da_ref, daa_ref, w_ref, b_ref, du_ref, dw_ref, db_ref):
        i = pl.program_id(1)

        @pl.when(i == 0)
        def _():
            dw_ref[...] = jnp.zeros_like(dw_ref)
            db_ref[...] = jnp.zeros_like(db_ref)

        for lo, width in LANE_TILES:
            lanes = slice(lo, lo + width)
            row = lax.broadcasted_iota(jnp.int32, (HALO, width), 0)
            taps = [_taps(w_ref, b_ref, half, lanes, HALO) for half in range(2)]
            acc_w = [[jnp.zeros((HALO, width), F32) for _ in range(3)] for _ in range(2)]
            acc_b = [jnp.zeros((HALO, width), F32) for _ in range(2)]
            da_pair, pending = None, [None, None]
            dc_prev, up_prev = [None, None], [None, None]
            for s, u3, (cg, cv) in _conv_strips(u_ref, ub_ref, ua_ref, taps, lanes, width, N_STRIPS):
                act, dact = _silu_parts(cg)
                if s == N_STRIPS:
                    da = jnp.where(i < N_TOK_TILES - 1, daa_ref[0, :, lanes].astype(F32), 0.0)
                elif s % 2 == 0:
                    da_pair = da_ref[0, s * HALO:(s + 2) * HALO, lanes].astype(F32)
                    da = da_pair[:HALO]
                else:
                    da = da_pair[HALO:]
                dc = (da * cv * dact, da * act)
                for half in range(2):
                    up = [pltpu.roll(dc[half], HALO - k, 0) for k in (1, 2)]
                    if s < N_STRIPS:
                        for j in range(3):
                            acc_w[half][j] = acc_w[half][j] + dc[half] * u3[half][j]
                        acc_b[half] = acc_b[half] + dc[half]
                    if s >= 1:
                        w3 = taps[half][0]
                        du = (dc_prev[half] * w3[2] + jnp.where(row < HALO - 1, up_prev[half][0], up[0]) * w3[1]
                              + jnp.where(row < HALO - 2, up_prev[half][1], up[1]) * w3[0])
                        if (s - 1) % 2 == 0:
                            pending[half] = du
                        else:
                            du_ref[half, (s - 2) * HALO:s * HALO, lanes] = jnp.concatenate([pending[half], du],
                                                                                           axis=0).astype(BF)
                    dc_prev[half], up_prev[half] = dc[half], up
            for half in range(2):
                for j in range(3):
                    dw_ref[half, j:j + 1, lanes] += jnp.sum(acc_w[half][j], axis=0, keepdims=True)
                db_ref[half, :, lanes] += jnp.sum(acc_b[half], axis=0, keepdims=True)

    tile, before, after, vec = _pair_specs(2)
    da_tile, _, da_after_spec, _ = _pair_specs(1)
    return pl.pallas_call(
        body, name="conv_bwd", grid=(4, N_TOK_TILES),
        in_specs=[tile, before, after, da_tile, da_after_spec, vec(3), vec(1)],
        out_specs=[tile, vec(3), vec(1)],
        out_shape=[jax.ShapeDtypeStruct((2, 4, SEQ, FF_BLK), BF), jax.ShapeDtypeStruct((2, 4, 3, FF_BLK), F32),
                   jax.ShapeDtypeStruct((2, 4, 1, FF_BLK), F32)],
        compiler_params=_params("parallel", "arbitrary"),
    )(*map(_in_hbm, (u, u, u, da, da, w_conv, b_conv)))


W_IN_SEGMENTS = ((R_POOL, POOL_WIDTH, "cat", C_POOL), (R_QKV, QKV_W, "cat", C_QKV), (R_OG, D_MODEL, "cat", C_OG),
                 (R_GK, GATE_RANK, "gk", 0), (R_GATE, GATE_W, "cat", C_GATE))


def _slab_pieces(d):
    lo, hi = d * IN_SHARD, (d + 1) * IN_SHARD
    pieces = []
    for start, n, dest, at in W_IN_SEGMENTS:
        a, b = max(lo, start), min(hi, start + n)
        if a < b:
            assert (a - lo) % 2 == 0 and (b - a) % 2 == 0 and (at + a - start) % 2 == 0
            pieces.append(((a - lo) // 2, (b - a) // 2, dest, (at + a - start) // 2))
    return pieces


def _unshard_w_in(slabs):
    def body(slab_ref, cat_ref, gk_ref):
        d = pl.program_id(0)
        src = slab_ref.bitcast(jnp.uint32)
        dst = dict(cat=cat_ref.bitcast(jnp.uint32), gk=gk_ref.bitcast(jnp.uint32))

        @pl.when(d == 0)
        def _():
            gk_ref[...] = jnp.zeros_like(gk_ref)

        for dd in range(N_DEV):
            @pl.when(d == dd)
            def _():
                for a, n, dest, at in _slab_pieces(dd):
                    dst[dest][pl.ds(at, n), :] = src[0, pl.ds(a, n), :]

    return pl.pallas_call(
        body, name="unshard_w_in", grid=(N_DEV,),
        in_specs=[pl.BlockSpec((1, IN_SHARD, D_MODEL), lambda d: (d, 0, 0))],
        out_specs=[_const_spec((N_CAT, D_MODEL)), _const_spec((GK_PAD, D_MODEL))],
        out_shape=[jax.ShapeDtypeStruct((N_CAT, D_MODEL), BF), jax.ShapeDtypeStruct((GK_PAD, D_MODEL), BF)],
        compiler_params=_params("arbitrary"),
    )(_in_hbm(slabs))


def _shard_d_w_in(d_cat, d_gk):
    def body(cat_ref, gk_ref, slab_ref):
        d = pl.program_id(0)
        cat = cat_ref.bitcast(jnp.uint32)
        gk = pltpu.bitcast(gk_ref[0:GATE_RANK, :].astype(BF), jnp.uint32)
        dst = slab_ref.bitcast(jnp.uint32)
        for dd in range(N_DEV):
            @pl.when(d == dd)
            def _():
                for a, n, source, at in _slab_pieces(dd):
                    dst[0, pl.ds(a, n), :] = gk[at:at + n] if source == "gk" else cat[pl.ds(at, n), :]

    return pl.pallas_call(
        body, name="shard_d_w_in", grid=(N_DEV,),
        in_specs=[_const_spec((N_CAT, D_MODEL)), _const_spec((GK_PAD, D_MODEL))],
        out_specs=pl.BlockSpec((1, IN_SHARD, D_MODEL), lambda d: (d, 0, 0)),
        out_shape=jax.ShapeDtypeStruct((N_DEV, IN_SHARD, D_MODEL), BF), compiler_params=_params("parallel"),
    )(_in_hbm(d_cat), _in_hbm(d_gk))


ANY = pl.BlockSpec(memory_space=pl.ANY)


def _place():
    x, y, c = lax.axis_index("x"), lax.axis_index("y"), lax.axis_index("c")
    other_chips = [(1 - x, y), (x, 1 - y), (1 - x, 1 - y)]
    return x, y, c, other_chips


def _all_gather(shards, name):
    n = len(shards)

    def body(*refs):
        src, out = refs[:n], refs[n:2 * n]
        send_sems, recv_sems, local_sems = refs[2 * n:]
        x, y, c, chips = _place()
        me, sibling = (x, y, c), (x, y, 1 - c)

        def copy(a, k, block, to, own=False):
            dst = out[a].at[4 * block[0] + 2 * block[1] + block[2]]
            return pltpu.make_async_remote_copy(src_ref=src[a] if own else dst, dst_ref=dst, send_sem=send_sems.at[a, k],
                                                recv_sem=recv_sems.at[a, k], device_id=to, device_id_type=MESH)

        mine = [pltpu.make_async_copy(src[a], out[a].at[4 * x + 2 * y + c], local_sems.at[a]) for a in range(n)]
        first = []
        for a in range(n):
            mine[a].start()
            first.append(copy(a, 0, me, sibling, own=True))
            first += [copy(a, 1 + j, me, (*chip, c), own=True) for j, chip in enumerate(chips)]
        for cp in first:
            cp.start()
        passed = []
        for j, chip in enumerate(chips):
            for a in range(n):
                copy(a, 1 + j, (*chip, c), me).wait_recv()
                passed.append(copy(a, 4 + j, (*chip, c), sibling))
                passed[-1].start()
        for a in range(n):
            copy(a, 0, sibling, me).wait_recv()
            for j, chip in enumerate(chips):
                copy(a, 4 + j, (*chip, 1 - c), me).wait_recv()
        for cp in first + passed:
            cp.wait_send()
        for cp in mine:
            cp.wait()

    return pl.pallas_call(
        body, name=name, in_specs=[ANY] * n, out_specs=[ANY] * n,
        out_shape=[jax.ShapeDtypeStruct((N_DEV,) + s.shape, s.dtype) for s in shards],
        scratch_shapes=[pltpu.SemaphoreType.DMA((n, 7)), pltpu.SemaphoreType.DMA((n, 7)), pltpu.SemaphoreType.DMA((n,))],
    )(*map(_in_hbm, shards))


SEM = pl.BlockSpec(memory_space=pltpu.SEMAPHORE)
IN_HBM = pl.BlockSpec(memory_space=pltpu.HBM)
SPLIT_PARAMS = pltpu.CompilerParams(has_side_effects=pltpu.SideEffectType.DATAFLOW_SIDE_EFFECTING)


def _gather_first(refs, send_sems, recv_sems):
    x, y, c, chips = _place()
    targets = [(x, y, 1 - c)] + [(px, py, c) for px, py in chips]
    return [pltpu.make_async_remote_copy(src_ref=refs[2 * a], dst_ref=refs[2 * a + 1].at[4 * x + 2 * y + c],
                                         send_sem=send_sems.at[4 * a + k], recv_sem=recv_sems.at[4 * a + k],
                                         device_id=to, device_id_type=MESH)
            for a in range(len(refs) // 2) for k, to in enumerate(targets)]


def _gather_second(refs, send_sems, recv_sems):
    x, y, c, chips = _place()
    copies = []
    for a, land in enumerate(refs):
        for j, (px, py) in enumerate(chips):
            block = land.at[4 * px + 2 * py + c]
            copies.append(pltpu.make_async_remote_copy(src_ref=block, dst_ref=block, send_sem=send_sems.at[3 * a + j],
                                                       recv_sem=recv_sems.at[3 * a + j], device_id=(x, y, 1 - c),
                                                       device_id_type=MESH))
    return copies


def _reduce_first(refs, send_sems, recv_sems):
    x, y, c, _ = _place()
    return [pltpu.make_async_remote_copy(src_ref=refs[2 * a].at[j, 1 - c], dst_ref=refs[2 * a + 1].at[j],
                                         send_sem=send_sems.at[4 * a + j], recv_sem=recv_sems.at[4 * a + j],
                                         device_id=(x, y, 1 - c), device_id_type=MESH)
            for a in range(len(refs) // 2) for j in range(4)]


def _reduce_second(refs, send_sems, recv_sems):
    _, _, c, chips = _place()
    return [pltpu.make_async_remote_copy(src_ref=refs[2 * a].at[2 * px + py], dst_ref=refs[2 * a + 1].at[k],
                                         send_sem=send_sems.at[3 * a + k], recv_sem=recv_sems.at[3 * a + k],
                                         device_id=(px, py, c), device_id_type=MESH)
            for a in range(len(refs) // 2) for k, (px, py) in enumerate(chips)]


def _split_start(name, groups):
    arrays = [a for g in groups for a in g[0]]
    n = len(arrays)

    def body(*refs):
        sems = refs[n:n + 2 * len(groups)]
        at = 0
        for gi, (members, _, build) in enumerate(groups):
            for cp in build(refs[at:at + len(members)], sems[2 * gi], sems[2 * gi + 1]):
                cp.start()
            at += len(members)
        refs[-1][...] = jnp.zeros_like(refs[-1])

    sem_shapes = [pltpu.SemaphoreType.DMA((g[1],)) for g in groups for _ in range(2)]
    outs = pl.pallas_call(
        body, name=name, in_specs=[IN_HBM] * n,
        out_shape=(*sem_shapes, *[pltpu.HBM(a.shape, a.dtype) for a in arrays], jax.ShapeDtypeStruct((8, 128), F32)),
        out_specs=(*[SEM] * len(sem_shapes), *[IN_HBM] * n, pl.BlockSpec(memory_space=pltpu.VMEM)),
        input_output_aliases={i: len(sem_shapes) + i for i in range(n)}, compiler_params=SPLIT_PARAMS,
    )(*[pltpu.with_memory_space_constraint(a, pltpu.HBM) for a in arrays])
    per_group, at = [], len(sem_shapes)
    for gi, (members, _, _) in enumerate(groups):
        per_group.append((outs[2 * gi], outs[2 * gi + 1], list(outs[at:at + len(members)])))
        at += len(members)
    return per_group, outs[-1]


def _split_wait(name, started, build, after):
    send_sems, recv_sems, arrays = started
    n = len(arrays)

    def body(*refs):
        for cp in build(refs[:n], refs[n], refs[n + 1]):
            cp.wait_send()
            cp.wait_recv()

    return pl.pallas_call(
        body, name=name, in_specs=[IN_HBM] * n + [SEM, SEM, ANY],
        out_shape=tuple(pltpu.HBM(a.shape, a.dtype) for a in arrays), out_specs=tuple([IN_HBM] * n),
        input_output_aliases={i: i for i in range(n)}, compiler_params=SPLIT_PARAMS,
    )(*arrays, send_sems, recv_sems, after)


def _gather_landing(shard, me):
    return lax.dynamic_update_slice(lax.empty((N_DEV,) + shard.shape, shard.dtype), shard[None],
                                    (me,) + (0,) * shard.ndim)


def _tile_2d(rows, cols):
    for t in (256, 176, 128):
        if rows % t == 0:
            return t, cols
    return rows, 256


def _pair_sum(part, recv, core, name):
    _, rows, cols = recv.shape
    tr, tc = _tile_2d(rows, cols)

    def body(c_ref, p_ref, r_ref, o_ref):
        del c_ref
        o_ref[...] = (p_ref[...].astype(F32) + r_ref[...].astype(F32)).astype(BF)

    grid_spec = pltpu.PrefetchScalarGridSpec(
        num_scalar_prefetch=1, grid=(4, rows // tr, cols // tc),
        in_specs=[pl.BlockSpec((None, None, tr, tc), lambda j, i, k, c_ref: (j, c_ref[0], i, k)),
                  pl.BlockSpec((None, tr, tc), lambda j, i, k, c_ref: (j, i, k))],
        out_specs=pl.BlockSpec((None, tr, tc), lambda j, i, k, c_ref: (j, i, k)))
    return pl.pallas_call(
        body, name=name, grid_spec=grid_spec, out_shape=jax.ShapeDtypeStruct(recv.shape, BF),
        compiler_params=_params("parallel", "parallel", "parallel"),
    )(core, *map(_in_hbm, (part, recv)))


def _adamw(w, g, m, v):
    m = ADAM_B1 * m + (1.0 - ADAM_B1) * g
    v = ADAM_B2 * v + (1.0 - ADAM_B2) * (g * g)
    delta = -ADAM_LR * ((m / ADAM_C1) / (jnp.sqrt(v / ADAM_C2) + ADAM_EPS) + ADAM_WD * w)
    return delta, m, v


def _chip_sum_adamw(sums, recv, w, m, v, chip, name):
    rows, cols = w.shape
    tr, tc = _tile_2d(rows, cols)

    def body(chip_ref, s_ref, r_ref, w_ref, m_ref, v_ref, g_out, d_out, m_out, v_out):
        del chip_ref
        g = s_ref[...].astype(F32)
        for k in range(3):
            g = g + r_ref[k].astype(F32)
        g_out[...] = g
        d_out[...], m_out[...], v_out[...] = _adamw(w_ref[...], g, m_ref[...], v_ref[...])

    tile = pl.BlockSpec((tr, tc), lambda i, k, chip_ref: (i, k))
    grid_spec = pltpu.PrefetchScalarGridSpec(
        num_scalar_prefetch=1, grid=(rows // tr, cols // tc),
        in_specs=[pl.BlockSpec((None, tr, tc), lambda i, k, chip_ref: (chip_ref[0], i, k)),
                  pl.BlockSpec((3, tr, tc), lambda i, k, chip_ref: (0, i, k)), tile, tile, tile],
        out_specs=[tile] * 4)
    return pl.pallas_call(
        body, name=name, grid_spec=grid_spec, out_shape=[jax.ShapeDtypeStruct((rows, cols), F32)] * 4,
        compiler_params=_params("parallel", "parallel"),
    )(chip, *map(_in_hbm, (sums, recv, w, m, v)))


def _small_sum_adamw(me, entries, loss_parts):
    def whole(shape, squeeze=0, pick=False):
        blk = (None,) * squeeze + tuple(shape[squeeze:])
        if pick:
            blk = (shape[0], None) + tuple(shape[2:])
            return pl.BlockSpec(blk, lambda i, me_ref: (0, me_ref[0]) + (0,) * (len(shape) - 2))
        return pl.BlockSpec(blk, lambda i, me_ref: (0,) * len(shape))

    in_specs, out_specs, out_shape, args = [], [], [], []
    for parts, w, m, v, sharded in entries:
        lead = w.ndim - (parts.ndim - (2 if sharded else 1))
        in_specs += [whole(parts.shape, pick=sharded)] + [whole(w.shape, squeeze=lead)] * 3
        out_specs += [whole(w.shape, squeeze=lead)] * 4
        out_shape += [jax.ShapeDtypeStruct(w.shape, F32)] * 4
        args += [parts, w, m, v]
    in_specs.append(whole(loss_parts.shape))
    out_specs.append(whole(loss_parts.shape[1:]))
    out_shape.append(jax.ShapeDtypeStruct(loss_parts.shape[1:], F32))
    n = len(entries)

    def added(p_ref):
        total = p_ref[0]
        for d in range(1, N_DEV):
            total = total + p_ref[d]
        return total

    def body(me_ref, *refs):
        del me_ref
        ins, outs = refs[:4 * n + 1], refs[4 * n + 1:]
        for e in range(n):
            p_ref, w_ref, m_ref, v_ref = ins[4 * e:4 * e + 4]
            g_out, d_out, m_out, v_out = outs[4 * e:4 * e + 4]
            g = added(p_ref)
            g_out[...] = g
            d_out[...], m_out[...], v_out[...] = _adamw(w_ref[...], g, m_ref[...], v_ref[...])
        outs[4 * n][...] = added(ins[4 * n])

    grid_spec = pltpu.PrefetchScalarGridSpec(num_scalar_prefetch=1, grid=(1,), in_specs=in_specs, out_specs=out_specs)
    outs = pl.pallas_call(body, name="small_sum_adamw", grid_spec=grid_spec, out_shape=out_shape,
                          compiler_params=_params("arbitrary"))(me, *map(_in_hbm, args + [loss_parts]))
    return [outs[4 * e:4 * e + 4] for e in range(n)], outs[4 * n]


MM_TILE = 512
N_MM_TILES = SEQ // MM_TILE
CAT_TILE = 512
N_CAT_TILES = N_CAT // CAT_TILE
SMALL_ROWS = 808
SHARD_ROWS = 32


def kernel(x, g_mix, w_in, b_gate, w_gk_up, b_gk, w_pool_grp, pool_scale, g_gla_head, w_pool_proj, w_gla_proj, w_out, g_ffn, w_up, w_conv, b_conv, w_down, g_final, loss_target, m_g_mix, m_w_in, m_b_gate, m_w_gk_up, m_b_gk, m_w_pool_grp, m_pool_scale, m_g_gla_head, m_w_pool_proj, m_w_gla_proj, m_w_out, m_g_ffn, m_w_up, m_w_conv, m_b_conv, m_w_down, m_g_final, v_g_mix, v_w_in, v_b_gate, v_w_gk_up, v_b_gk, v_w_pool_grp, v_pool_scale, v_g_gla_head, v_w_pool_proj, v_w_gla_proj, v_w_out, v_g_ffn, v_w_up, v_w_conv, v_b_conv, v_w_down, v_g_final):
    xi, yi, ci = lax.axis_index("x"), lax.axis_index("y"), lax.axis_index("c")
    me = 4 * xi + 2 * yi + ci
    core = jnp.reshape(ci, (1,)).astype(jnp.int32)
    chip = jnp.reshape(2 * xi + yi, (1,)).astype(jnp.int32)
    xs, target = x[0], loss_target[0]

    big = dict(w_in=w_in[0].T, w_pool_proj=w_pool_proj[0], w_gla_proj=w_gla_proj[0], w_out=w_out[0], w_up=w_up[0].T,
               w_down=w_down[0])
    moments = dict(w_in=(m_w_in[0].T, v_w_in[0].T), w_pool_proj=(m_w_pool_proj[0], v_w_pool_proj[0]),
                   w_gla_proj=(m_w_gla_proj[0], v_w_gla_proj[0]), w_out=(m_w_out[0], v_w_out[0]),
                   w_up=(m_w_up[0].T, v_w_up[0].T), w_down=(m_w_down[0], v_w_down[0]))
    names = list(big)
    shards = {k: big[k].astype(BF) for k in names}
    shards["w_gk_up"], shards["w_conv"] = w_gk_up[0], w_conv[0]
    gather_groups = (("w_in", "w_gk_up"), ("w_pool_proj", "w_gla_proj", "w_out"), ("w_up", "w_down", "w_conv"))
    started, token = _split_start("gather_start", [
        ([t for k in g for t in (shards[k], _gather_landing(shards[k], me))], 4 * len(g), _gather_first)
        for g in gather_groups])

    def gather_pass(gi, after):
        lands = list(_split_wait(f"gather_wait_{gi}", started[gi], _gather_first, after)[1::2])
        passed, _ = _split_start(f"gather_pass_{gi}", [(lands, 3 * len(lands), _gather_second)])
        return passed[0]

    def gather_done(gi, passed, after):
        return dict(zip(gather_groups[gi], _split_wait(f"gather_pass_wait_{gi}", passed, _gather_second, after)))

    tok = lambda i, j, k: (i, 0)
    whole = lambda i, j, k: (0, 0)
    kblk = lambda i, j, k: (k, 0)
    ff_tile = (None, None, MM_TILE, FF_BLK)
    ff_seq = (None, None, SEQ, FF_BLK)

    h = _rms_fwd(xs, g_mix + token[:1, :1], "rms_mix")
    wg = gather_done(0, gather_pass(0, h), h)
    wt_cat, wt_gk = _unshard_w_in(wg["w_in"])
    wgk_pad = jnp.pad(wg["w_gk_up"].transpose(1, 0, 2).reshape(GATE_RANK, GLA_DK), ((0, GK_PAD - GATE_RANK), (0, 0)))
    zcat = _mm(h, wt_cat, out_shape=(SEQ, N_CAT), out_dtype=F32, grid=(N_CAT_TILES, 1, 1),
               blk_a=(SEQ, D_MODEL), blk_b=(CAT_TILE, D_MODEL), blk_o=(SEQ, CAT_TILE),
               map_a=whole, map_b=lambda j, i, k: (j, 0), map_o=lambda j, i, k: (0, j), tb=True, name="mm_in")
    la = _gk_fwd(h, wt_gk, wgk_pad, b_gk)
    passed = gather_pass(1, la)
    o, states = _gla_fwd(zcat, la)
    wg = gather_done(1, passed, o)
    wpp = wg["w_pool_proj"].transpose(1, 0, 2).reshape(POOL_WIDTH, D_MODEL)
    wgp = wg["w_gla_proj"].reshape(D_MODEL, D_MODEL)
    wout = wg["w_out"].reshape(D_MODEL, D_MODEL)
    og = _post_gla_fwd(o, zcat, g_gla_head)
    passed = gather_pass(2, og)
    ps = _pool_fwd(zcat, w_pool_grp[0], pool_scale)
    y_pool = _mm(ps, wpp, out_shape=(SEQ, D_MODEL), out_dtype=F32, grid=(N_MM_TILES, 1, 1),
                 blk_a=(MM_TILE, POOL_WIDTH), blk_b=(POOL_WIDTH, D_MODEL), blk_o=(MM_TILE, D_MODEL),
                 map_a=tok, map_b=whole, map_o=tok, name="mm_pool_proj")
    sq = dict(out_shape=(SEQ, D_MODEL), grid=(N_MM_TILES, 1, 1), blk_a=(MM_TILE, D_MODEL), blk_b=(D_MODEL, D_MODEL),
              blk_o=(MM_TILE, D_MODEL), map_a=tok, map_b=whole, map_o=tok)
    y_gla = _mm(og, wgp, out_dtype=F32, name="mm_gla_proj", **sq)
    mixed = _mix_fwd(zcat, b_gate, y_pool, y_gla)
    x1 = _mm(mixed, wout, out_dtype=F32, res=xs, name="mm_out", **sq)
    h2 = _rms_fwd(x1, g_ffn, "rms_ffn")
    wg = gather_done(2, passed, h2)
    wt_up = wg["w_up"].reshape(2 * D_FF, D_MODEL)
    wdown = wg["w_down"].reshape(D_FF, D_MODEL)
    wconv4 = wg["w_conv"].reshape(2, 4, 3, FF_BLK)
    bconv4 = b_conv.reshape(2, 4, 1, FF_BLK)
    blk4 = lambda b, i, k: (b // 4, b % 4, 0, 0)
    u4 = _mm(h2, wt_up, out_shape=(2, 4, SEQ, FF_BLK), out_dtype=F32, grid=(N_DEV, 1, 1),
             blk_a=(SEQ, D_MODEL), blk_b=(FF_BLK, D_MODEL), blk_o=ff_seq,
             map_a=whole, map_b=lambda b, i, k: (b, 0), map_o=blk4, tb=True, name="mm_up")
    act = _conv_fwd(u4, wconv4, bconv4)
    x2 = _mm_tokens(act, wdown, blk_a=(None, 4, TOK_MM_TILE, FF_BLK), map_a=lambda i: (0, 0, i, 0),
                    pieces=[(b, b * FF_BLK, FF_BLK) for b in range(4)], res=x1, name="mm_down")
    loss_part, dx2, dx2_bf, dg_final = _final_loss(x2, g_final.reshape(1, D_MODEL), target)

    da = _mm(dx2_bf, wdown, out_shape=(1, 4, SEQ, FF_BLK), out_dtype=BF, grid=(4, 1, 1),
             blk_a=(SEQ, D_MODEL), blk_b=(FF_BLK, D_MODEL), blk_o=ff_seq,
             map_a=whole, map_b=lambda b, i, k: (b, 0), map_o=lambda b, i, k: (0, b, 0, 0), tb=True, name="mm_d_act")
    d_wdown = _mm(act, dx2_bf, out_shape=(D_FF, D_MODEL), out_dtype=BF, grid=(4, 1, 1),
                  blk_a=ff_seq, blk_b=(SEQ, D_MODEL), blk_o=(FF_BLK, D_MODEL),
                  map_a=lambda b, i, k: (0, b, 0, 0), map_b=whole, map_o=lambda b, i, k: (b, 0), ta=True,
                  name="mm_d_wdown")
    du4, d_wconv, d_bconv = _conv_bwd(u4, da, wconv4, bconv4)
    dh2 = _mm_tokens(du4, wt_up, blk_a=(2, 4, TOK_MM_TILE, FF_BLK), map_a=lambda i: (0, 0, i, 0),
                     pieces=[((b // 4, b % 4), b * FF_BLK, FF_BLK) for b in range(N_DEV)], name="mm_d_h2")
    d_wt_up = _mm(du4, h2, out_shape=(2 * D_FF, D_MODEL), out_dtype=BF, grid=(N_DEV, 1, 1),
                  blk_a=ff_seq, blk_b=(SEQ, D_MODEL), blk_o=(FF_BLK, D_MODEL),
                  map_a=blk4, map_b=whole, map_o=lambda b, i, k: (b, 0), ta=True, name="mm_d_wup")
    res = {}

    def reduce_start(keys, parts):
        arrays = [t for k in keys for t in (parts[k], lax.empty((4,) + parts[k].shape[2:], BF))]
        st, tkn = _split_start("reduce_start_" + keys[0], [(arrays, 4 * len(keys), _reduce_first)])
        return st[0], tkn

    def reduce_cross(keys, st, after):
        arrays = _split_wait("reduce_wait_" + keys[0], st, _reduce_first, after)
        sums = [_pair_sum(p, r, core, "pair_sum_" + k) for k, p, r in zip(keys, arrays[0::2], arrays[1::2])]
        arrays = [t for s in sums for t in (s, lax.empty((3,) + s.shape[1:], BF))]
        st2, tkn = _split_start("reduce_cross_" + keys[0], [(arrays, 3 * len(keys), _reduce_second)])
        return st2[0], tkn

    def reduce_done(keys, st2, after):
        arrays = _split_wait("reduce_cross_wait_" + keys[0], st2, _reduce_second, after)
        for k, s, r in zip(keys, arrays[0::2], arrays[1::2]):
            outs = _chip_sum_adamw(s, r, big[k], moments[k][0], moments[k][1], chip, "adamw_" + k)
            res[k] = [(t.T if k in ("w_in", "w_up") else t)[None] for t in outs]

    ffn_keys = ("w_down", "w_up")
    ffn_red, tkn = reduce_start(ffn_keys, dict(w_down=d_wdown.reshape(4, 2, D_FF // N_DEV, D_MODEL),
                                               w_up=d_wt_up.reshape(4, 2, FF_BLK, D_MODEL)))
    dx1, dg_ffn = _rms_bwd(dh2, x1, g_ffn + tkn[:1, :1], dx2, "rms_ffn_bwd")

    sq_t = dict(out_shape=(D_MODEL, D_MODEL), grid=(1, 1, N_MM_TILES), blk_a=(MM_TILE, D_MODEL),
                blk_b=(MM_TILE, D_MODEL), blk_o=(D_MODEL, D_MODEL), map_a=kblk, map_b=kblk, map_o=whole, ta=True)
    dmixed = _mm(dx1, wout, out_dtype=F32, tb=True, name="mm_d_mixed", **sq)
    d_wout = _mm(mixed, dx1, out_dtype=BF, name="mm_d_wout", **sq_t)
    dzcat, dy_pool, dy_gla, db_gate = _mix_bwd(dmixed, zcat, b_gate, y_pool, y_gla)
    ffn_red, _ = reduce_cross(ffn_keys, ffn_red, db_gate)
    d_og =_mm(dy_gla, wgp, out_dtype=F32, tb=True, name="mm_d_og", **sq)
    d_wgp = _mm(og, dy_gla, out_dtype=BF, name="mm_d_wgp", **sq_t)
    mix_keys = ("w_out", "w_gla_proj")
    mix_red, tkn = reduce_start(mix_keys, dict(w_out=d_wout.reshape(4, 2, D_MODEL // N_DEV, D_MODEL),
                                               w_gla_proj=d_wgp.reshape(4, 2, D_MODEL // N_DEV, D_MODEL)))
    dzcat, d_o, dg_head = _post_gla_bwd(dzcat, d_og, o, zcat, g_gla_head + tkn[:1, :1])
    dzcat, dla = _gla_bwd(dzcat, zcat, la, d_o, states)
    mix_red, _ = reduce_cross(mix_keys, mix_red, dla)
    dh_gk, d_wt_gk, d_wgk, db_gk = _gk_bwd(dla, h, wt_gk, wgk_pad, b_gk)
    dps = _mm(dy_pool, wpp, out_shape=(SEQ, POOL_WIDTH), out_dtype=F32, grid=(N_MM_TILES, 1, 1),
              blk_a=(MM_TILE, D_MODEL), blk_b=(POOL_WIDTH, D_MODEL), blk_o=(MM_TILE, POOL_WIDTH),
              map_a=tok, map_b=whole, map_o=tok, tb=True, name="mm_d_ps")
    d_wpp = _mm(ps, dy_pool, out_shape=(POOL_WIDTH, D_MODEL), out_dtype=F32, grid=(1, 1, N_MM_TILES),
                blk_a=(MM_TILE, POOL_WIDTH), blk_b=(MM_TILE, D_MODEL), blk_o=(POOL_WIDTH, D_MODEL),
                map_a=kblk, map_b=kblk, map_o=whole, ta=True, name="mm_d_wpp")
    dzcat, d_wgrp, d_scale = _pool_bwd(dzcat, zcat, dps, w_pool_grp[0], pool_scale)
    d_wt_cat = _mm(dzcat, h, out_shape=(N_CAT, D_MODEL), out_dtype=BF, grid=(N_CAT_TILES, 1, 1),
                   blk_a=(SEQ, CAT_TILE), blk_b=(SEQ, D_MODEL), blk_o=(CAT_TILE, D_MODEL),
                   map_a=lambda j, i, k: (0, j), map_b=whole, map_o=lambda j, i, k: (j, 0), ta=True, name="mm_d_wcat")
    in_keys = ("w_in", "w_pool_proj")
    in_red, tkn = reduce_start(in_keys, dict(
        w_in=_shard_d_w_in(d_wt_cat, d_wt_gk).reshape(4, 2, IN_SHARD, D_MODEL),
        w_pool_proj=d_wpp.reshape(POOL_WIDTH, N_DEV, D_MODEL // N_DEV).transpose(1, 0, 2).astype(BF)
        .reshape(4, 2, POOL_WIDTH, D_MODEL // N_DEV)))
    dh = _mm_tokens(dzcat, wt_cat, blk_a=(TOK_MM_TILE, N_CAT), map_a=lambda i: (i, 0), pieces=[(None, 0, N_CAT)],
                    res=dh_gk, name="mm_d_h")
    in_red, tkn = reduce_cross(in_keys, in_red, dh)
    grad_x, dg_mix = _rms_bwd(dh, xs, g_mix + tkn[:1, :1], dx1, "rms_mix_bwd")
    reduce_done(ffn_keys, ffn_red, grad_x)
    reduce_done(mix_keys, mix_red, res["w_down"][0])

    row = lambda t: t.reshape(1, D_MODEL)
    conv_vec = lambda t: t.reshape(2, 4, 1, FF_BLK)
    small = [("g_mix", dg_mix, g_mix, m_g_mix, v_g_mix, False), ("b_gate", db_gate, b_gate, m_b_gate, v_b_gate, False),
             ("w_gk_up", d_wgk.reshape(GATE_RANK, N_DEV, GLA_DK // N_DEV).transpose(1, 0, 2), w_gk_up, m_w_gk_up,
              v_w_gk_up, True),
             ("b_gk", db_gk, b_gk, m_b_gk, v_b_gk, False),
             ("w_pool_grp", d_wgrp, w_pool_grp, m_w_pool_grp, v_w_pool_grp, False),
             ("pool_scale", d_scale, pool_scale, m_pool_scale, v_pool_scale, False),
             ("g_gla_head", dg_head, g_gla_head, m_g_gla_head, v_g_gla_head, False),
             ("g_ffn", dg_ffn, g_ffn, m_g_ffn, v_g_ffn, False),
             ("w_conv", d_wconv.reshape(N_DEV, 3, FF_BLK), w_conv, m_w_conv, v_w_conv, True),
             ("b_conv", d_bconv, conv_vec(b_conv), conv_vec(m_b_conv), conv_vec(v_b_conv), False),
             ("g_final", dg_final, row(g_final), row(m_g_final), row(v_g_final), False)]
    gathered = _all_gather([t[1] for t in small] + [loss_part], "gather_small_grads")
    small_out, loss_sum = _small_sum_adamw(jnp.reshape(me, (1,)).astype(jnp.int32),
                                           [(p,) + t[2:] for p, t in zip(gathered, small)], gathered[-1])
    for t, outs in zip(small, small_out):
        res[t[0]] = list(outs)
    res["b_conv"] = [t.reshape(b_conv.shape) for t in res["b_conv"]]
    res["g_final"] = [t.reshape(g_final.shape) for t in res["g_final"]]

    reduce_done(in_keys, in_red, loss_sum)
    loss = loss_sum[0, 0]
    order =["g_mix", "w_in", "b_gate", "w_gk_up", "b_gk", "w_pool_grp", "pool_scale", "g_gla_head", "w_pool_proj",
             "w_gla_proj", "w_out", "g_ffn", "w_up", "w_conv", "b_conv", "w_down", "g_final"]
    return (loss, grad_x[None], *[res[k][0] for k in order], *[res[k][1] for k in order],
            *[res[k][2] for k in order], *[res[k][3] for k in order])
```

```python
import functools

import jax
import jax.numpy as jnp
from jax import lax
from jax.experimental import pallas as pl
from jax.experimental.pallas import tpu as pltpu

F32 = jnp.float32
BF = jnp.bfloat16
HIGHEST = lax.Precision.HIGHEST
MESH = pl.DeviceIdType.MESH

N_DEV = 8
SEQ = 2048
D_MODEL = 1024
CHUNK = 64
EPS = 1e-6
POOL_WIDTH = 512
POOL_WINDOWS = (2, 4, 8, 16)
POOL_GD = 128
POOL_HALO = 16
HEADS = 4
HK = 128
HV = 256
GLA_DK = 512
GATE_RANK = 16
GATE_NORM = 16.0
D_FF = 2816
FF_BLK = 704
IN_TOTAL = 5648
IN_SHARD = 706
C_QKV, C_GATE, C_OG, C_POOL = 0, 2048, 4096, 5120
N_CAT = 5632
R_POOL, R_QKV, R_OG, R_GK, R_GATE = 0, 512, 2560, 3584, 3600
GK_PAD = 128

ADAM_LR, ADAM_B1, ADAM_B2, ADAM_EPS, ADAM_WD, ADAM_STEP = 0.001, 0.9, 0.999, 1e-08, 0.01, 10
ADAM_C1 = 1.0 - ADAM_B1 ** ADAM_STEP
ADAM_C2 = 1.0 - ADAM_B2 ** ADAM_STEP

VMEM_BYTES_V7X = 64 * 1024 * 1024
VMEM_LIMIT = 48 * 1024 * 1024

TOK_TILE = 256
HALO = 8
GLA_CPS = 4


def _params(*sem):
    return pltpu.CompilerParams(dimension_semantics=sem, vmem_limit_bytes=VMEM_LIMIT)


def _const_spec(shape):
    nd = len(shape)
    return pl.BlockSpec(shape, lambda *_: (0,) * nd)


def _in_hbm(t):
    return pltpu.with_memory_space_constraint(t, pltpu.HBM)


def _dot(a, b, ta=False, tb=False):
    dims = (((0 if ta else 1,), (1 if tb else 0,)), ((), ()))
    return lax.dot_general(a.astype(BF), b.astype(BF), dims, preferred_element_type=F32)


def _dot_exact(a, b):
    return jnp.dot(a, b, precision=HIGHEST, preferred_element_type=F32)


def _sigmoid(x):
    return 0.5 * jnp.tanh(0.5 * x) + 0.5


def _mm(a, b, *, out_shape, out_dtype, grid, blk_a, blk_b, blk_o, map_a, map_b, map_o, ta=False, tb=False,
        res=None, name):
    gk = grid[2]

    def body(*refs):
        if res is None:
            a_ref, b_ref, o_ref = refs[:3]
            r_ref = None
            scr = refs[3:]
        else:
            a_ref, b_ref, r_ref, o_ref = refs[:4]
            scr = refs[4:]
        prod = _dot(a_ref[...], b_ref[...], ta, tb)

        def finish(total):
            if r_ref is not None:
                total = total + r_ref[...]
            o_ref[...] = total.astype(out_dtype)

        if gk == 1:
            finish(prod)
        else:
            acc = scr[0]
            k = pl.program_id(2)

            @pl.when(k == 0)
            def _():
                acc[...] = prod

            @pl.when(k > 0)
            def _():
                acc[...] += prod

            @pl.when(k == gk - 1)
            def _():
                finish(acc[...])

    in_specs = [pl.BlockSpec(blk_a, map_a), pl.BlockSpec(blk_b, map_b)]
    args = [a, b]
    if res is not None:
        in_specs.append(pl.BlockSpec(blk_o, map_o))
        args.append(res)
    return pl.pallas_call(
        body, name=name, grid=grid, in_specs=in_specs, out_specs=pl.BlockSpec(blk_o, map_o),
        out_shape=jax.ShapeDtypeStruct(out_shape, out_dtype),
        scratch_shapes=[] if gk == 1 else [pltpu.VMEM(tuple(d for d in blk_o if d is not None), F32)],
        compiler_params=_params("parallel", "parallel", "arbitrary"),
    )(*[_in_hbm(t) for t in args])


TOK_MM_TILE = 256


def _mm_tokens(a, w, *, blk_a, map_a, pieces, res=None, after=None, name):
    def body(*refs):
        a_ref, w_ref = refs[:2]
        o_ref = refs[-1]
        total = None
        for idx, row, n in pieces:
            av = a_ref[...] if idx is None else a_ref[idx]
            prod = _dot(av, w_ref[row:row + n, :])
            total = prod if total is None else total + prod
        if res is not None:
            total = total + refs[2][...]
        o_ref[...] = total

    tile = pl.BlockSpec((TOK_MM_TILE, D_MODEL), lambda i: (i, 0))
    in_specs = [pl.BlockSpec(blk_a, map_a), _const_spec(w.shape)]
    args = [a, w]
    if res is not None:
        in_specs.append(tile)
        args.append(res)
    if after is not None:
        in_specs.append(pl.BlockSpec(memory_space=pl.ANY))
        args.append(after)
    return pl.pallas_call(
        body, name=name, grid=(SEQ // TOK_MM_TILE,), in_specs=in_specs, out_specs=tile,
        out_shape=jax.ShapeDtypeStruct((SEQ, D_MODEL), F32), compiler_params=_params("parallel"),
    )(*[_in_hbm(t) for t in args])


def _rms_fwd(x, g, name):
    def body(x_ref, g_ref, o_ref):
        xv = x_ref[...]
        r = lax.rsqrt(jnp.mean(xv * xv, axis=-1, keepdims=True) + EPS)
        o_ref[...] = (xv * r * g_ref[...]).astype(BF)

    tile = pl.BlockSpec((TOK_TILE, D_MODEL), lambda i: (i, 0))
    return pl.pallas_call(
        body, name=name, grid=(SEQ // TOK_TILE,), in_specs=[tile, _const_spec((1, D_MODEL))], out_specs=tile,
        out_shape=jax.ShapeDtypeStruct((SEQ, D_MODEL), BF), compiler_params=_params("parallel"),
    )(*map(_in_hbm, (x, g)))


def _rms_bwd(dy, x, g, dres, name):
    def body(dy_ref, x_ref, g_ref, dres_ref, dx_ref, dg_ref):
        xv = x_ref[...]
        r = lax.rsqrt(jnp.mean(xv * xv, axis=-1, keepdims=True) + EPS)
        xn = xv * r
        dyv = dy_ref[...]
        dxn = dyv * g_ref[...]
        dx_ref[...] = dres_ref[...] + r * (dxn - xn * jnp.mean(dxn * xn, axis=-1, keepdims=True))
        part = jnp.sum(dyv * xn, axis=0, keepdims=True)

        @pl.when(pl.program_id(0) == 0)
        def _():
            dg_ref[...] = part

        @pl.when(pl.program_id(0) > 0)
        def _():
            dg_ref[...] += part

    tile = pl.BlockSpec((TOK_TILE, D_MODEL), lambda i: (i, 0))
    vec = _const_spec((1, D_MODEL))
    return pl.pallas_call(
        body, name=name, grid=(SEQ // TOK_TILE,), in_specs=[tile, tile, vec, tile], out_specs=[tile, vec],
        out_shape=[jax.ShapeDtypeStruct((SEQ, D_MODEL), F32), jax.ShapeDtypeStruct((1, D_MODEL), F32)],
        compiler_params=_params("arbitrary"),
    )(*map(_in_hbm, (dy, x, g, dres)))


def _final_loss(x2, g, target):
    def body(x_ref, g_ref, t_ref, loss_ref, dx_ref, dxb_ref, dg_ref):
        xv = x_ref[...]
        r = lax.rsqrt(jnp.mean(xv * xv, axis=-1, keepdims=True) + EPS)
        xn = xv * r
        gv = g_ref[...]
        err = xn * gv - t_ref[...]
        lpart = jnp.full((1, 128), 0.5 * jnp.sum(jnp.mean(err * err, axis=-1, keepdims=True)), F32)
        dyv = err * (1.0 / D_MODEL)
        dxn = dyv * gv
        dxv = r * (dxn - xn * jnp.mean(dxn * xn, axis=-1, keepdims=True))
        dx_ref[...] = dxv
        dxb_ref[...] = dxv.astype(BF)
        gpart = jnp.sum(dyv * xn, axis=0, keepdims=True)

        @pl.when(pl.program_id(0) == 0)
        def _():
            loss_ref[...] = lpart
            dg_ref[...] = gpart

        @pl.when(pl.program_id(0) > 0)
        def _():
            loss_ref[...] += lpart
            dg_ref[...] += gpart

    tile = pl.BlockSpec((TOK_TILE, D_MODEL), lambda i: (i, 0))
    vec = _const_spec((1, D_MODEL))
    return pl.pallas_call(
        body, name="final_loss", grid=(SEQ // TOK_TILE,), in_specs=[tile, vec, tile],
        out_specs=[_const_spec((1, 128)), tile, tile, vec],
        out_shape=[jax.ShapeDtypeStruct((1, 128), F32), jax.ShapeDtypeStruct((SEQ, D_MODEL), F32),
                   jax.ShapeDtypeStruct((SEQ, D_MODEL), BF), jax.ShapeDtypeStruct((1, D_MODEL), F32)],
        compiler_params=_params("arbitrary"),
    )(*map(_in_hbm, (x2, g, target)))


def _pool_counts(w):
    pos = lax.broadcasted_iota(jnp.int32, (SEQ, 1), 0).astype(F32)
    return jnp.minimum(pos + 1.0, float(w))


def _pool_window(u, w, ext):
    ext[pl.ds(POOL_HALO, SEQ), :] = u
    win = u
    for j in range(1, w):
        win = win + ext[pl.ds(POOL_HALO - j, SEQ), :]
    return win / _pool_counts(w) - u


def _pool_fwd(zcat, w_grp, scale):
    def body(z_ref, w_ref, s_ref, o_ref, ext):
        ext[pl.ds(0, POOL_HALO), :] = jnp.zeros((POOL_HALO, POOL_GD), F32)
        for g, w in enumerate(POOL_WINDOWS):
            cols = slice(g * POOL_GD, (g + 1) * POOL_GD)
            p = _pool_window(z_ref[:, cols], w, ext)
            o_ref[:, cols] = (_dot(p, w_ref[g]) * s_ref[:, cols]).astype(BF)

    return pl.pallas_call(
        body, name="pool_fwd", grid=(1,),
        in_specs=[pl.BlockSpec((SEQ, POOL_WIDTH), lambda i: (0, C_POOL // POOL_WIDTH)),
                  _const_spec((4, POOL_GD, POOL_GD)), _const_spec((1, POOL_WIDTH))],
        out_specs=_const_spec((SEQ, POOL_WIDTH)), out_shape=jax.ShapeDtypeStruct((SEQ, POOL_WIDTH), BF),
        scratch_shapes=[pltpu.VMEM((POOL_HALO + SEQ, POOL_GD), F32)], compiler_params=_params("arbitrary"),
    )(*map(_in_hbm, (zcat, w_grp, scale)))


def _pool_bwd(dzcat, zcat, dps, w_grp, scale):
    def body(dz_in, z_ref, dps_ref, w_ref, s_ref, dz_ref, dw_ref, dsc_ref, ext, ext2):
        del dz_in
        ext[pl.ds(0, POOL_HALO), :] = jnp.zeros((POOL_HALO, POOL_GD), F32)
        ext2[pl.ds(SEQ, POOL_HALO), :] = jnp.zeros((POOL_HALO, POOL_GD), F32)
        for g, w in enumerate(POOL_WINDOWS):
            cols = slice(g * POOL_GD, (g + 1) * POOL_GD)
            p = _pool_window(z_ref[:, cols], w, ext)
            wg = w_ref[g]
            pg = _dot(p, wg)
            dpsv = dps_ref[:, cols]
            dsc_ref[:, cols] = jnp.sum(dpsv * pg, axis=0, keepdims=True)
            dpg = dpsv * s_ref[:, cols]
            dw_ref[g] = _dot(p, dpg, ta=True)
            dp = _dot(dpg, wg, tb=True)
            dpc = dp / _pool_counts(w)
            ext2[pl.ds(0, SEQ), :] = dpc
            du = dpc
            for j in range(1, w):
                du = du + ext2[pl.ds(j, SEQ), :]
            dz_ref[:, cols] = (du - dp).astype(BF)

    return pl.pallas_call(
        body, name="pool_bwd", grid=(1,),
        in_specs=[pl.BlockSpec(memory_space=pl.ANY),
                  pl.BlockSpec((SEQ, POOL_WIDTH), lambda i: (0, C_POOL // POOL_WIDTH)),
                  _const_spec((SEQ, POOL_WIDTH)), _const_spec((4, POOL_GD, POOL_GD)), _const_spec((1, POOL_WIDTH))],
        out_specs=[pl.BlockSpec((SEQ, POOL_WIDTH), lambda i: (0, C_POOL // POOL_WIDTH)),
                   _const_spec((4, POOL_GD, POOL_GD)), _const_spec((1, POOL_WIDTH))],
        out_shape=[jax.ShapeDtypeStruct((SEQ, N_CAT), BF), jax.ShapeDtypeStruct((4, POOL_GD, POOL_GD), F32),
                   jax.ShapeDtypeStruct((1, POOL_WIDTH), F32)],
        scratch_shapes=[pltpu.VMEM((POOL_HALO + SEQ, POOL_GD), F32), pltpu.VMEM((SEQ + POOL_HALO, POOL_GD), F32)],
        input_output_aliases={0: 0}, compiler_params=_params("arbitrary"),
    )(*map(_in_hbm, (dzcat, zcat, dps, w_grp, scale)))


GK_TILE = 512


def _gk_fwd(h, wt_gk, wgk_pad, b_gk):
    def body(h_ref, wt_ref, w_ref, b_ref, la_ref):
        z_gk = _dot(h_ref[...], wt_ref[...], tb=True)
        pre = _dot(z_gk, w_ref[...]) + b_ref[...]
        la_ref[...] = (jnp.minimum(pre, 0.0) - jnp.log(1.0 + jnp.exp(-jnp.abs(pre)))) * (1.0 / GATE_NORM)

    return pl.pallas_call(
        body, name="gk_fwd", grid=(SEQ // GK_TILE,),
        in_specs=[pl.BlockSpec((GK_TILE, D_MODEL), lambda i: (i, 0)), _const_spec((GK_PAD, D_MODEL)),
                  _const_spec((GK_PAD, GLA_DK)), _const_spec((1, GLA_DK))],
        out_specs=pl.BlockSpec((GK_TILE, GLA_DK), lambda i: (i, 0)),
        out_shape=jax.ShapeDtypeStruct((SEQ, GLA_DK), F32), compiler_params=_params("parallel"),
    )(*map(_in_hbm, (h, wt_gk, wgk_pad, b_gk)))


def _gk_bwd(dla, h, wt_gk, wgk_pad, b_gk):
    def body(dla_ref, h_ref, wt_ref, w_ref, b_ref, dh_ref, dwt_ref, dw_ref, db_ref):
        hv = h_ref[...]
        wtv = wt_ref[...]
        wv = w_ref[...]
        z_gk = _dot(hv, wtv, tb=True)
        pre = _dot(z_gk, wv) + b_ref[...]
        dpre = dla_ref[...] * (1.0 / GATE_NORM) * (1.0 - _sigmoid(pre))
        dz_gk = _dot(dpre, wv, tb=True)
        dh_ref[...] = _dot(dz_gk, wtv)
        dwtp = _dot(dz_gk, hv, ta=True)
        dwp = _dot(z_gk, dpre, ta=True)[:GATE_RANK]
        dbp = jnp.sum(dpre, axis=0, keepdims=True)

        @pl.when(pl.program_id(0) == 0)
        def _():
            dwt_ref[...] = dwtp
            dw_ref[...] = dwp
            db_ref[...] = dbp

        @pl.when(pl.program_id(0) > 0)
        def _():
            dwt_ref[...] += dwtp
            dw_ref[...] += dwp
            db_ref[...] += dbp

    tile = pl.BlockSpec((GK_TILE, D_MODEL), lambda i: (i, 0))
    return pl.pallas_call(
        body, name="gk_bwd", grid=(SEQ // GK_TILE,),
        in_specs=[pl.BlockSpec((GK_TILE, GLA_DK), lambda i: (i, 0)), tile, _const_spec((GK_PAD, D_MODEL)),
                  _const_spec((GK_PAD, GLA_DK)), _const_spec((1, GLA_DK))],
        out_specs=[tile, _const_spec((GK_PAD, D_MODEL)), _const_spec((GATE_RANK, GLA_DK)), _const_spec((1, GLA_DK))],
        out_shape=[jax.ShapeDtypeStruct((SEQ, D_MODEL), F32), jax.ShapeDtypeStruct((GK_PAD, D_MODEL), F32),
                   jax.ShapeDtypeStruct((GATE_RANK, GLA_DK), F32), jax.ShapeDtypeStruct((1, GLA_DK), F32)],
        compiler_params=_params("arbitrary"),
    )(*map(_in_hbm, (dla, h, wt_gk, wgk_pad, b_gk)))


GLA_ROWS = GLA_CPS * CHUNK
GLA_STEPS = SEQ // GLA_ROWS
QKV_W = 2048


def _gla_chunk(qkv_ref, la_ref, rows, h):
    tri = lax.broadcasted_iota(jnp.int32, (CHUNK, CHUNK), 0) >= lax.broadcasted_iota(jnp.int32, (CHUNK, CHUNK), 1)
    q = qkv_ref[rows, h * HK:(h + 1) * HK] * (HK ** -0.5)
    k = qkv_ref[rows, GLA_DK + h * HK:GLA_DK + (h + 1) * HK]
    v = qkv_ref[rows, 2 * GLA_DK + h * HV:2 * GLA_DK + (h + 1) * HV]
    la = la_ref[rows, h * HK:(h + 1) * HK]
    bc = _dot_exact(tri.astype(F32), la)
    e_pos, e_neg = jnp.exp(bc), jnp.exp(-bc)
    dl = jnp.exp(jnp.sum(la, axis=0, keepdims=True))
    q_fw, q_bw, k_fw, k_bw = q * e_pos, q * e_neg, k * e_neg, k * e_pos
    scores = jnp.where(tri, _dot(q_fw, k_fw, tb=True), _dot(q_bw, k_bw, tb=True))
    return tri, v, e_pos, e_neg, dl, q_fw, q_bw, k_fw, k_bw, scores


def _gla_fwd(zcat, la, after):
    def body(qkv_ref, la_ref, after_ref, o_ref, st_ref, state):
        del after_ref

        @pl.when(pl.program_id(0) == 0)
        def _():
            state[...] = jnp.zeros_like(state)

        for c in range(GLA_CPS):
            rows = slice(c * CHUNK, (c + 1) * CHUNK)
            for h in range(HEADS):
                _, v, _, _, dl, q_fw, _, k_fw, _, scores = _gla_chunk(qkv_ref, la_ref, rows, h)
                st = state[h]
                st_ref[c, h] = st
                o_ref[rows, h * HV:(h + 1) * HV] = _dot(scores, v) + _dot(q_fw, st, tb=True)
                state[h] = st * dl + _dot(v, k_fw * dl, ta=True)

    return pl.pallas_call(
        body, name="gla_fwd", grid=(GLA_STEPS,),
        in_specs=[pl.BlockSpec((GLA_ROWS, QKV_W), lambda i: (i, 0)), pl.BlockSpec((GLA_ROWS, GLA_DK), lambda i: (i, 0)),
                  pl.BlockSpec(memory_space=pl.ANY)],
        out_specs=[pl.BlockSpec((GLA_ROWS, D_MODEL), lambda i: (i, 0)),
                   pl.BlockSpec((GLA_CPS, HEADS, HV, HK), lambda i: (i, 0, 0, 0))],
        out_shape=[jax.ShapeDtypeStruct((SEQ, D_MODEL), F32),
                   jax.ShapeDtypeStruct((SEQ // CHUNK, HEADS, HV, HK), F32)],
        scratch_shapes=[pltpu.VMEM((HEADS, HV, HK), F32)], compiler_params=_params("arbitrary"),
    )(*map(_in_hbm, (zcat, la)), after)


def _gla_bwd(dzcat, zcat, la, d_o, states):
    def body(dz_in, qkv_ref, la_ref, do_ref, st_ref, dqkv_ref, dla_ref, dstate):
        del dz_in

        @pl.when(pl.program_id(0) == 0)
        def _():
            dstate[...] = jnp.zeros_like(dstate)

        last_row = lax.broadcasted_iota(jnp.int32, (CHUNK, HK), 0) == CHUNK - 1
        upper = (lax.broadcasted_iota(jnp.int32, (CHUNK, CHUNK), 0)
                 <= lax.broadcasted_iota(jnp.int32, (CHUNK, CHUNK), 1)).astype(F32)
        for c in reversed(range(GLA_CPS)):
            rows = slice(c * CHUNK, (c + 1) * CHUNK)
            for h in range(HEADS):
                tri, v, e_pos, e_neg, dl, q_fw, q_bw, k_fw, k_bw, scores = _gla_chunk(qkv_ref, la_ref, rows, h)
                st = st_ref[c, h]
                dst = dstate[h]
                d_out = do_ref[rows, h * HV:(h + 1) * HV]
                k_dec = k_fw * dl
                dp = _dot(d_out, v, tb=True)
                dp_fw = jnp.where(tri, dp, 0.0)
                dp_bw = jnp.where(tri, 0.0, dp)
                dv = _dot(scores, d_out, ta=True) + _dot(k_dec, dst, tb=True)
                dk_dec = _dot(v, dst)
                dq_fw = _dot(dp_fw, k_fw) + _dot(d_out, st)
                dk_fw = _dot(dp_fw, q_fw, ta=True) + dk_dec * dl
                dq_bw = _dot(dp_bw, k_bw)
                dk_bw = _dot(dp_bw, q_bw, ta=True)
                ddl = jnp.sum(st * dst, axis=0, keepdims=True) + jnp.sum(k_fw * dk_dec, axis=0, keepdims=True)
                dstate[h] = dst * dl + _dot(d_out, q_fw, ta=True)
                dq = (dq_fw * e_pos + dq_bw * e_neg) * (HK ** -0.5)
                dk = dk_fw * e_neg + dk_bw * e_pos
                db = dq_fw * q_fw - dk_fw * k_fw - dq_bw * q_bw + dk_bw * k_bw + jnp.where(last_row, ddl * dl, 0.0)
                dla_ref[rows, h * HK:(h + 1) * HK] = _dot_exact(upper, db)
                dqkv_ref[rows, h * HK:(h + 1) * HK] = dq.astype(BF)
                dqkv_ref[rows, GLA_DK + h * HK:GLA_DK + (h + 1) * HK] = dk.astype(BF)
                dqkv_ref[rows, 2 * GLA_DK + h * HV:2 * GLA_DK + (h + 1) * HV] = dv.astype(BF)

    rev = lambda i: (GLA_STEPS - 1 - i, 0)
    return pl.pallas_call(
        body, name="gla_bwd", grid=(GLA_STEPS,),
        in_specs=[pl.BlockSpec(memory_space=pl.ANY), pl.BlockSpec((GLA_ROWS, QKV_W), rev),
                  pl.BlockSpec((GLA_ROWS, GLA_DK), rev), pl.BlockSpec((GLA_ROWS, D_MODEL), rev),
                  pl.BlockSpec((GLA_CPS, HEADS, HV, HK), lambda i: (GLA_STEPS - 1 - i, 0, 0, 0))],
        out_specs=[pl.BlockSpec((GLA_ROWS, QKV_W), rev), pl.BlockSpec((GLA_ROWS, GLA_DK), rev)],
        out_shape=[jax.ShapeDtypeStruct((SEQ, N_CAT), BF), jax.ShapeDtypeStruct((SEQ, GLA_DK), F32)],
        scratch_shapes=[pltpu.VMEM((HEADS, HV, HK), F32)], input_output_aliases={0: 0},
        compiler_params=_params("arbitrary"),
    )(*map(_in_hbm, (dzcat, zcat, la, d_o, states)))


def _silu_parts(x):
    s = _sigmoid(x)
    return x * s, s * (1.0 + x * (1.0 - s))


def _post_gla_fwd(o, zcat, g_head):
    def body(o_ref, zog_ref, g_ref, out_ref):
        for h in range(HEADS):
            cols = slice(h * HV, (h + 1) * HV)
            ov = o_ref[:, cols]
            r = lax.rsqrt(jnp.mean(ov * ov, axis=-1, keepdims=True) + EPS)
            act, _ = _silu_parts(zog_ref[:, cols])
            out_ref[:, cols] = (ov * r * g_ref[...] * act).astype(BF)

    tile = pl.BlockSpec((TOK_TILE, D_MODEL), lambda i: (i, 0))
    return pl.pallas_call(
        body, name="post_gla_fwd", grid=(SEQ // TOK_TILE,),
        in_specs=[tile, pl.BlockSpec((TOK_TILE, D_MODEL), lambda i: (i, C_OG // D_MODEL)), _const_spec((1, HV))],
        out_specs=tile, out_shape=jax.ShapeDtypeStruct((SEQ, D_MODEL), BF), compiler_params=_params("parallel"),
    )(*map(_in_hbm, (o, zcat, g_head)))


def _post_gla_bwd(dzcat, d_og, o, zcat, g_head):
    def body(dz_in, dog_ref, o_ref, zog_ref, g_ref, dz_ref, do_ref, dg_ref):
        del dz_in
        gpart = jnp.zeros((1, HV), F32)
        gv = g_ref[...]
        for h in range(HEADS):
            cols = slice(h * HV, (h + 1) * HV)
            ov = o_ref[:, cols]
            r = lax.rsqrt(jnp.mean(ov * ov, axis=-1, keepdims=True) + EPS)
            on = ov * r
            act, dact = _silu_parts(zog_ref[:, cols])
            dogv = dog_ref[:, cols]
            dz_ref[:, cols] = (dogv * on * gv * dact).astype(BF)
            d_on_g = dogv * act
            gpart = gpart + jnp.sum(d_on_g * on, axis=0, keepdims=True)
            dxn = d_on_g * gv
            do_ref[:, cols] = r * (dxn - on * jnp.mean(dxn * on, axis=-1, keepdims=True))

        @pl.when(pl.program_id(0) == 0)
        def _():
            dg_ref[...] = gpart

        @pl.when(pl.program_id(0) > 0)
        def _():
            dg_ref[...] += gpart

    tile = pl.BlockSpec((TOK_TILE, D_MODEL), lambda i: (i, 0))
    ogspec = pl.BlockSpec((TOK_TILE, D_MODEL), lambda i: (i, C_OG // D_MODEL))
    return pl.pallas_call(
        body, name="post_gla_bwd", grid=(SEQ // TOK_TILE,),
        in_specs=[pl.BlockSpec(memory_space=pl.ANY), tile, tile, ogspec, _const_spec((1, HV))],
        out_specs=[ogspec, tile, _const_spec((1, HV))],
        out_shape=[jax.ShapeDtypeStruct((SEQ, N_CAT), BF), jax.ShapeDtypeStruct((SEQ, D_MODEL), F32),
                   jax.ShapeDtypeStruct((1, HV), F32)],
        input_output_aliases={0: 0}, compiler_params=_params("arbitrary"),
    )(*map(_in_hbm, (dzcat, d_og, o, zcat, g_head)))


GATE_W = 2 * D_MODEL


def _mix_fwd(zcat, b_gate, y_pool, y_gla):
    def body(zg_ref, b_ref, yp_ref, yg_ref, out_ref):
        g0 = _sigmoid(zg_ref[:, :D_MODEL] + b_ref[:, :D_MODEL])
        g1 = _sigmoid(zg_ref[:, D_MODEL:] + b_ref[:, D_MODEL:])
        out_ref[...] = (g0 * yp_ref[...] + g1 * yg_ref[...]).astype(BF)

    tile = pl.BlockSpec((TOK_TILE, D_MODEL), lambda i: (i, 0))
    return pl.pallas_call(
        body, name="mix_fwd", grid=(SEQ // TOK_TILE,),
        in_specs=[pl.BlockSpec((TOK_TILE, GATE_W), lambda i: (i, C_GATE // GATE_W)), _const_spec((1, GATE_W)), tile, tile],
        out_specs=tile, out_shape=jax.ShapeDtypeStruct((SEQ, D_MODEL), BF), compiler_params=_params("parallel"),
    )(*map(_in_hbm, (zcat, b_gate, y_pool, y_gla)))


def _mix_bwd(dmixed, zcat, b_gate, y_pool, y_gla):
    def body(dm_ref, zg_ref, b_ref, yp_ref, yg_ref, dz_ref, dyp_ref, dyg_ref, db_ref):
        dm = dm_ref[...]
        g0 = _sigmoid(zg_ref[:, :D_MODEL] + b_ref[:, :D_MODEL])
        g1 = _sigmoid(zg_ref[:, D_MODEL:] + b_ref[:, D_MODEL:])
        dyp_ref[...] = (dm * g0).astype(BF)
        dyg_ref[...] = (dm * g1).astype(BF)
        dz0 = dm * yp_ref[...] * g0 * (1.0 - g0)
        dz1 = dm * yg_ref[...] * g1 * (1.0 - g1)
        dz_ref[:, :D_MODEL] = dz0.astype(BF)
        dz_ref[:, D_MODEL:] = dz1.astype(BF)
        b0 = jnp.sum(dz0, axis=0, keepdims=True)
        b1 = jnp.sum(dz1, axis=0, keepdims=True)

        @pl.when(pl.program_id(0) == 0)
        def _():
            db_ref[:, :D_MODEL] = b0
            db_ref[:, D_MODEL:] = b1

        @pl.when(pl.program_id(0) > 0)
        def _():
            db_ref[:, :D_MODEL] += b0
            db_ref[:, D_MODEL:] += b1

    tile = pl.BlockSpec((TOK_TILE, D_MODEL), lambda i: (i, 0))
    gspec = pl.BlockSpec((TOK_TILE, GATE_W), lambda i: (i, C_GATE // GATE_W))
    return pl.pallas_call(
        body, name="mix_bwd", grid=(SEQ // TOK_TILE,),
        in_specs=[tile, gspec, _const_spec((1, GATE_W)), tile, tile],
        out_specs=[gspec, tile, tile, _const_spec((1, GATE_W))],
        out_shape=[jax.ShapeDtypeStruct((SEQ, N_CAT), BF), jax.ShapeDtypeStruct((SEQ, D_MODEL), BF),
                   jax.ShapeDtypeStruct((SEQ, D_MODEL), BF), jax.ShapeDtypeStruct((1, GATE_W), F32)],
        compiler_params=_params("arbitrary"),
    )(*map(_in_hbm, (dmixed, zcat, b_gate, y_pool, y_gla)))


N_TOK_TILES = SEQ // TOK_TILE
HALO_PER_TILE = TOK_TILE // HALO


LANE_TILES = tuple((lo, min(128, FF_BLK - lo)) for lo in range(0, FF_BLK, 128))


def _taps(w_ref, b_ref, half, lanes, rows):
    shape = (rows, lanes.stop - lanes.start)
    return ([jnp.broadcast_to(w_ref[half, j:j + 1, lanes], shape) for j in range(3)],
            jnp.broadcast_to(b_ref[half, :, lanes], shape))


def _conv_strips(u_ref, ub_ref, ua_ref, taps, lanes, width, n_strips):
    first = pl.program_id(1) == 0
    row = lax.broadcasted_iota(jnp.int32, (HALO, width), 0)
    prev = [[pltpu.roll(jnp.where(first, 0.0, ub_ref[half, :, lanes]), k, 0) for k in (1, 2)] for half in range(2)]
    for s in range(n_strips + (ua_ref is not None)):
        u3, conv = [], []
        for half in range(2):
            cur = u_ref[half, s * HALO:(s + 1) * HALO, lanes] if s < n_strips else ua_ref[half, :, lanes]
            rolled = [pltpu.roll(cur, k, 0) for k in (1, 2)]
            frames = [jnp.where(row >= 2, rolled[1], prev[half][1]), jnp.where(row >= 1, rolled[0], prev[half][0]), cur]
            prev[half] = rolled
            w3, bias = taps[half]
            u3.append(frames)
            conv.append(bias + frames[0] * w3[0] + frames[1] * w3[1] + frames[2] * w3[2])
        yield s, u3, conv


def _pair_specs(pairs):
    tile = pl.BlockSpec((pairs, None, TOK_TILE, FF_BLK), lambda b, i: (0, b, i, 0))
    before = pl.BlockSpec((pairs, None, HALO, FF_BLK), lambda b, i: (0, b, jnp.maximum(i * HALO_PER_TILE - 1, 0), 0))
    after = pl.BlockSpec((pairs, None, HALO, FF_BLK),
                         lambda b, i: (0, b, jnp.minimum((i + 1) * HALO_PER_TILE, SEQ // HALO - 1), 0))

    def vec(rows):
        return pl.BlockSpec((2, None, rows, FF_BLK), lambda b, i: (0, b, 0, 0))

    return tile, before, after, vec


N_STRIPS = TOK_TILE // HALO


def _conv_fwd(u, w_conv, b_conv):
    def body(u_ref, ub_ref, w_ref, b_ref, a_ref):
        for lo, width in LANE_TILES:
            lanes = slice(lo, lo + width)
            taps = [_taps(w_ref, b_ref, half, lanes, HALO) for half in range(2)]
            pending = None
            for s, _, (cg, cv) in _conv_strips(u_ref, ub_ref, None, taps, lanes, width, N_STRIPS):
                act = cg * _sigmoid(cg) * cv
                if s % 2 == 0:
                    pending = act
                else:
                    a_ref[0, (s - 1) * HALO:(s + 1) * HALO, lanes] = jnp.concatenate([pending, act], axis=0).astype(BF)

    tile, before, _, vec = _pair_specs(2)
    out_tile, _, _, _ = _pair_specs(1)
    return pl.pallas_call(
        body, name="conv_fwd", grid=(4, N_TOK_TILES), in_specs=[tile, before, vec(3), vec(1)],
        out_specs=out_tile, out_shape=jax.ShapeDtypeStruct((1, 4, SEQ, FF_BLK), BF),
        compiler_params=_params("parallel", "parallel"),
    )(*map(_in_hbm, (u, u, w_conv, b_conv)))


def _conv_bwd(u, da, w_conv, b_conv):
    def body(u_ref, ub_ref, ua_ref, da_ref, daa_ref, w_ref, b_ref, du_ref, dw_ref, db_ref):
        i = pl.program_id(1)

        @pl.when(i == 0)
        def _():
            dw_ref[...] = jnp.zeros_like(dw_ref)
            db_ref[...] = jnp.zeros_like(db_ref)

        for lo, width in LANE_TILES:
            lanes = slice(lo, lo + width)
            row = lax.broadcasted_iota(jnp.int32, (HALO, width), 0)
            taps = [_taps(w_ref, b_ref, half, lanes, HALO) for half in range(2)]
            acc_w = [[jnp.zeros((HALO, width), F32) for _ in range(3)] for _ in range(2)]
            acc_b = [jnp.zeros((HALO, width), F32) for _ in range(2)]
            da_pair, pending = None, [None, None]
            dc_prev, up_prev = [None, None], [None, None]
            for s, u3, (cg, cv) in _conv_strips(u_ref, ub_ref, ua_ref, taps, lanes, width, N_STRIPS):
                act, dact = _silu_parts(cg)
                if s == N_STRIPS:
                    da = jnp.where(i < N_TOK_TILES - 1, daa_ref[0, :, lanes].astype(F32), 0.0)
                elif s % 2 == 0:
                    da_pair = da_ref[0, s * HALO:(s + 2) * HALO, lanes].astype(F32)
                    da = da_pair[:HALO]
                else:
                    da = da_pair[HALO:]
                dc = (da * cv * dact, da * act)
                for half in range(2):
                    up = [pltpu.roll(dc[half], HALO - k, 0) for k in (1, 2)]
                    if s < N_STRIPS:
                        for j in range(3):
                            acc_w[half][j] = acc_w[half][j] + dc[half] * u3[half][j]
                        acc_b[half] = acc_b[half] + dc[half]
                    if s >= 1:
                        w3 = taps[half][0]
                        du = (dc_prev[half] * w3[2] + jnp.where(row < HALO - 1, up_prev[half][0], up[0]) * w3[1]
                              + jnp.where(row < HALO - 2, up_prev[half][1], up[1]) * w3[0])
                        if (s - 1) % 2 == 0:
                            pending[half] = du
                        else:
                            du_ref[half, (s - 2) * HALO:s * HALO, lanes] = jnp.concatenate([pending[half], du],
                                                                                           axis=0).astype(BF)
                    dc_prev[half], up_prev[half] = dc[half], up
            for half in range(2):
                for j in range(3):
                    dw_ref[half, j:j + 1, lanes] += jnp.sum(acc_w[half][j], axis=0, keepdims=True)
                db_ref[half, :, lanes] += jnp.sum(acc_b[half], axis=0, keepdims=True)

    tile, before, after, vec = _pair_specs(2)
    da_tile, _, da_after_spec, _ = _pair_specs(1)
    return pl.pallas_call(
        body, name="conv_bwd", grid=(4, N_TOK_TILES),
        in_specs=[tile, before, after, da_tile, da_after_spec, vec(3), vec(1)],
        out_specs=[tile, vec(3), vec(1)],
        out_shape=[jax.ShapeDtypeStruct((2, 4, SEQ, FF_BLK), BF), jax.ShapeDtypeStruct((2, 4, 3, FF_BLK), F32),
                   jax.ShapeDtypeStruct((2, 4, 1, FF_BLK), F32)],
        compiler_params=_params("parallel", "arbitrary"),
    )(*map(_in_hbm, (u, u, u, da, da, w_conv, b_conv)))


W_IN_SEGMENTS = ((R_POOL, POOL_WIDTH, "cat", C_POOL), (R_QKV, QKV_W, "cat", C_QKV), (R_OG, D_MODEL, "cat", C_OG),
                 (R_GK, GATE_RANK, "gk", 0), (R_GATE, GATE_W, "cat", C_GATE))


def _slab_pieces(d):
    lo, hi = d * IN_SHARD, (d + 1) * IN_SHARD
    pieces = []
    for start, n, dest, at in W_IN_SEGMENTS:
        a, b = max(lo, start), min(hi, start + n)
        if a < b:
            assert (a - lo) % 2 == 0 and (b - a) % 2 == 0 and (at + a - start) % 2 == 0
            pieces.append(((a - lo) // 2, (b - a) // 2, dest, (at + a - start) // 2))
    return pieces


def _unshard_w_in(slabs):
    def body(slab_ref, cat_ref, gk_ref):
        d = pl.program_id(0)
        src = slab_ref.bitcast(jnp.uint32)
        dst = dict(cat=cat_ref.bitcast(jnp.uint32), gk=gk_ref.bitcast(jnp.uint32))

        @pl.when(d == 0)
        def _():
            gk_ref[...] = jnp.zeros_like(gk_ref)

        for dd in range(N_DEV):
            @pl.when(d == dd)
            def _():
                for a, n, dest, at in _slab_pieces(dd):
                    dst[dest][pl.ds(at, n), :] = src[0, pl.ds(a, n), :]

    return pl.pallas_call(
        body, name="unshard_w_in", grid=(N_DEV,),
        in_specs=[pl.BlockSpec((1, IN_SHARD, D_MODEL), lambda d: (d, 0, 0))],
        out_specs=[_const_spec((N_CAT, D_MODEL)), _const_spec((GK_PAD, D_MODEL))],
        out_shape=[jax.ShapeDtypeStruct((N_CAT, D_MODEL), BF), jax.ShapeDtypeStruct((GK_PAD, D_MODEL), BF)],
        compiler_params=_params("arbitrary"),
    )(_in_hbm(slabs))


def _shard_d_w_in(d_cat, d_gk):
    def body(cat_ref, gk_ref, slab_ref):
        d = pl.program_id(0)
        cat = cat_ref.bitcast(jnp.uint32)
        gk = pltpu.bitcast(gk_ref[0:GATE_RANK, :].astype(BF), jnp.uint32)
        dst = slab_ref.bitcast(jnp.uint32)
        for dd in range(N_DEV):
            @pl.when(d == dd)
            def _():
                for a, n, source, at in _slab_pieces(dd):
                    dst[0, pl.ds(a, n), :] = gk[at:at + n] if source == "gk" else cat[pl.ds(at, n), :]

    return pl.pallas_call(
        body, name="shard_d_w_in", grid=(N_DEV,),
        in_specs=[_const_spec((N_CAT, D_MODEL)), _const_spec((GK_PAD, D_MODEL))],
        out_specs=pl.BlockSpec((1, IN_SHARD, D_MODEL), lambda d: (d, 0, 0)),
        out_shape=jax.ShapeDtypeStruct((N_DEV, IN_SHARD, D_MODEL), BF), compiler_params=_params("parallel"),
    )(_in_hbm(d_cat), _in_hbm(d_gk))


ANY = pl.BlockSpec(memory_space=pl.ANY)


def _place():
    x, y, c = lax.axis_index("x"), lax.axis_index("y"), lax.axis_index("c")
    other_chips = [(1 - x, y), (x, 1 - y), (1 - x, 1 - y)]
    return x, y, c, other_chips


def _all_gather(shards, name):
    n = len(shards)

    def body(*refs):
        src, out = refs[:n], refs[n:2 * n]
        send_sems, recv_sems, local_sems = refs[2 * n:]
        x, y, c, chips = _place()
        me, sibling = (x, y, c), (x, y, 1 - c)

        def copy(a, k, block, to, own=False):
            dst = out[a].at[4 * block[0] + 2 * block[1] + block[2]]
            return pltpu.make_async_remote_copy(src_ref=src[a] if own else dst, dst_ref=dst, send_sem=send_sems.at[a, k],
                                                recv_sem=recv_sems.at[a, k], device_id=to, device_id_type=MESH)

        mine = [pltpu.make_async_copy(src[a], out[a].at[4 * x + 2 * y + c], local_sems.at[a]) for a in range(n)]
        first = []
        for a in range(n):
            mine[a].start()
            first.append(copy(a, 0, me, sibling, own=True))
            first += [copy(a, 1 + j, me, (*chip, c), own=True) for j, chip in enumerate(chips)]
        for cp in first:
            cp.start()
        passed = []
        for j, chip in enumerate(chips):
            for a in range(n):
                copy(a, 1 + j, (*chip, c), me).wait_recv()
                passed.append(copy(a, 4 + j, (*chip, c), sibling))
                passed[-1].start()
        for a in range(n):
            copy(a, 0, sibling, me).wait_recv()
            for j, chip in enumerate(chips):
                copy(a, 4 + j, (*chip, 1 - c), me).wait_recv()
        for cp in first + passed:
            cp.wait_send()
        for cp in mine:
            cp.wait()

    return pl.pallas_call(
        body, name=name, in_specs=[ANY] * n, out_specs=[ANY] * n,
        out_shape=[jax.ShapeDtypeStruct((N_DEV,) + s.shape, s.dtype) for s in shards],
        scratch_shapes=[pltpu.SemaphoreType.DMA((n, 7)), pltpu.SemaphoreType.DMA((n, 7)), pltpu.SemaphoreType.DMA((n,))],
    )(*map(_in_hbm, shards))


SEM = pl.BlockSpec(memory_space=pltpu.SEMAPHORE)
IN_HBM = pl.BlockSpec(memory_space=pltpu.HBM)
SPLIT_PARAMS = pltpu.CompilerParams(has_side_effects=pltpu.SideEffectType.DATAFLOW_SIDE_EFFECTING)


def _gather_first(refs, send_sems, recv_sems):
    x, y, c, chips = _place()
    targets = [(x, y, 1 - c)] + [(px, py, c) for px, py in chips]
    return [pltpu.make_async_remote_copy(src_ref=refs[2 * a], dst_ref=refs[2 * a + 1].at[4 * x + 2 * y + c],
                                         send_sem=send_sems.at[4 * a + k], recv_sem=recv_sems.at[4 * a + k],
                                         device_id=to, device_id_type=MESH)
            for a in range(len(refs) // 2) for k, to in enumerate(targets)]


def _gather_second(refs, send_sems, recv_sems):
    x, y, c, chips = _place()
    copies = []
    for a, land in enumerate(refs):
        for j, (px, py) in enumerate(chips):
            block = land.at[4 * px + 2 * py + c]
            copies.append(pltpu.make_async_remote_copy(src_ref=block, dst_ref=block, send_sem=send_sems.at[3 * a + j],
                                                       recv_sem=recv_sems.at[3 * a + j], device_id=(x, y, 1 - c),
                                                       device_id_type=MESH))
    return copies


def _reduce_first(refs, send_sems, recv_sems):
    x, y, c, _ = _place()
    return [pltpu.make_async_remote_copy(src_ref=refs[2 * a].at[j, 1 - c], dst_ref=refs[2 * a + 1].at[j],
                                         send_sem=send_sems.at[4 * a + j], recv_sem=recv_sems.at[4 * a + j],
                                         device_id=(x, y, 1 - c), device_id_type=MESH)
            for a in range(len(refs) // 2) for j in range(4)]


def _reduce_second(refs, send_sems, recv_sems):
    _, _, c, chips = _place()
    return [pltpu.make_async_remote_copy(src_ref=refs[2 * a].at[2 * px + py], dst_ref=refs[2 * a + 1].at[k],
                                         send_sem=send_sems.at[3 * a + k], recv_sem=recv_sems.at[3 * a + k],
                                         device_id=(px, py, c), device_id_type=MESH)
            for a in range(len(refs) // 2) for k, (px, py) in enumerate(chips)]


def _split_start(name, groups):
    arrays = [a for g in groups for a in g[0]]
    n = len(arrays)

    def body(*refs):
        sems = refs[n:n + 2 * len(groups)]
        at = 0
        for gi, (members, _, build) in enumerate(groups):
            for cp in build(refs[at:at + len(members)], sems[2 * gi], sems[2 * gi + 1]):
                cp.start()
            at += len(members)
        refs[-1][...] = jnp.zeros_like(refs[-1])

    sem_shapes = [pltpu.SemaphoreType.DMA((g[1],)) for g in groups for _ in range(2)]
    outs = pl.pallas_call(
        body, name=name, in_specs=[IN_HBM] * n,
        out_shape=(*sem_shapes, *[pltpu.HBM(a.shape, a.dtype) for a in arrays], jax.ShapeDtypeStruct((8, 128), F32)),
        out_specs=(*[SEM] * len(sem_shapes), *[IN_HBM] * n, pl.BlockSpec(memory_space=pltpu.VMEM)),
        input_output_aliases={i: len(sem_shapes) + i for i in range(n)}, compiler_params=SPLIT_PARAMS,
    )(*[pltpu.with_memory_space_constraint(a, pltpu.HBM) for a in arrays])
    per_group, at = [], len(sem_shapes)
    for gi, (members, _, _) in enumerate(groups):
        per_group.append((outs[2 * gi], outs[2 * gi + 1], list(outs[at:at + len(members)])))
        at += len(members)
    return per_group, outs[-1]


def _split_wait(name, started, build, after):
    send_sems, recv_sems, arrays = started
    n = len(arrays)

    def body(*refs):
        for cp in build(refs[:n], refs[n], refs[n + 1]):
            cp.wait_send()
            cp.wait_recv()

    return pl.pallas_call(
        body, name=name, in_specs=[IN_HBM] * n + [SEM, SEM, ANY],
        out_shape=tuple(pltpu.HBM(a.shape, a.dtype) for a in arrays), out_specs=tuple([IN_HBM] * n),
        input_output_aliases={i: i for i in range(n)}, compiler_params=SPLIT_PARAMS,
    )(*arrays, send_sems, recv_sems, after)


def _gather_landing(shard, me):
    return lax.dynamic_update_slice(lax.empty((N_DEV,) + shard.shape, shard.dtype), shard[None],
                                    (me,) + (0,) * shard.ndim)


def _tile_2d(rows, cols):
    for t in (256, 176, 128):
        if rows % t == 0:
            return t, cols
    return rows, 256


def _pair_sum(part, recv, core, name):
    _, rows, cols = recv.shape
    tr, tc = _tile_2d(rows, cols)

    def body(c_ref, p_ref, r_ref, o_ref):
        del c_ref
        o_ref[...] = (p_ref[...].astype(F32) + r_ref[...].astype(F32)).astype(BF)

    grid_spec = pltpu.PrefetchScalarGridSpec(
        num_scalar_prefetch=1, grid=(4, rows // tr, cols // tc),
        in_specs=[pl.BlockSpec((None, None, tr, tc), lambda j, i, k, c_ref: (j, c_ref[0], i, k)),
                  pl.BlockSpec((None, tr, tc), lambda j, i, k, c_ref: (j, i, k))],
        out_specs=pl.BlockSpec((None, tr, tc), lambda j, i, k, c_ref: (j, i, k)))
    return pl.pallas_call(
        body, name=name, grid_spec=grid_spec, out_shape=jax.ShapeDtypeStruct(recv.shape, BF),
        compiler_params=_params("parallel", "parallel", "parallel"),
    )(core, *map(_in_hbm, (part, recv)))


def _adamw(w, g, m, v):
    m = ADAM_B1 * m + (1.0 - ADAM_B1) * g
    v = ADAM_B2 * v + (1.0 - ADAM_B2) * (g * g)
    delta = -ADAM_LR * ((m / ADAM_C1) / (jnp.sqrt(v / ADAM_C2) + ADAM_EPS) + ADAM_WD * w)
    return delta, m, v


def _chip_sum_adamw(sums, recv, w, m, v, chip, name):
    rows, cols = w.shape
    tr, tc = _tile_2d(rows, cols)

    def body(chip_ref, s_ref, r_ref, w_ref, m_ref, v_ref, g_out, d_out, m_out, v_out):
        del chip_ref
        g = s_ref[...].astype(F32)
        for k in range(3):
            g = g + r_ref[k].astype(F32)
        g_out[...] = g
        d_out[...], m_out[...], v_out[...] = _adamw(w_ref[...], g, m_ref[...], v_ref[...])

    tile = pl.BlockSpec((tr, tc), lambda i, k, chip_ref: (i, k))
    grid_spec = pltpu.PrefetchScalarGridSpec(
        num_scalar_prefetch=1, grid=(rows // tr, cols // tc),
        in_specs=[pl.BlockSpec((None, tr, tc), lambda i, k, chip_ref: (chip_ref[0], i, k)),
                  pl.BlockSpec((3, tr, tc), lambda i, k, chip_ref: (0, i, k)), tile, tile, tile],
        out_specs=[tile] * 4)
    return pl.pallas_call(
        body, name=name, grid_spec=grid_spec, out_shape=[jax.ShapeDtypeStruct((rows, cols), F32)] * 4,
        compiler_params=_params("parallel", "parallel"),
    )(chip, *map(_in_hbm, (sums, recv, w, m, v)))


def _small_sum_adamw(me, entries, loss_parts):
    def whole(shape, squeeze=0, pick=False):
        blk = (None,) * squeeze + tuple(shape[squeeze:])
        if pick:
            blk = (shape[0], None) + tuple(shape[2:])
            return pl.BlockSpec(blk, lambda i, me_ref: (0, me_ref[0]) + (0,) * (len(shape) - 2))
        return pl.BlockSpec(blk, lambda i, me_ref: (0,) * len(shape))

    in_specs, out_specs, out_shape, args = [], [], [], []
    for parts, w, m, v, sharded in entries:
        lead = w.ndim - (parts.ndim - (2 if sharded else 1))
        in_specs += [whole(parts.shape, pick=sharded)] + [whole(w.shape, squeeze=lead)] * 3
        out_specs += [whole(w.shape, squeeze=lead)] * 4
        out_shape += [jax.ShapeDtypeStruct(w.shape, F32)] * 4
        args += [parts, w, m, v]
    in_specs.append(whole(loss_parts.shape))
    out_specs.append(whole(loss_parts.shape[1:]))
    out_shape.append(jax.ShapeDtypeStruct(loss_parts.shape[1:], F32))
    n = len(entries)

    def added(p_ref):
        total = p_ref[0]
        for d in range(1, N_DEV):
            total = total + p_ref[d]
        return total

    def body(me_ref, *refs):
        del me_ref
        ins, outs = refs[:4 * n + 1], refs[4 * n + 1:]
        for e in range(n):
            p_ref, w_ref, m_ref, v_ref = ins[4 * e:4 * e + 4]
            g_out, d_out, m_out, v_out = outs[4 * e:4 * e + 4]
            g = added(p_ref)
            g_out[...] = g
            d_out[...], m_out[...], v_out[...] = _adamw(w_ref[...], g, m_ref[...], v_ref[...])
        outs[4 * n][...] = added(ins[4 * n])

    grid_spec = pltpu.PrefetchScalarGridSpec(num_scalar_prefetch=1, grid=(1,), in_specs=in_specs, out_specs=out_specs)
    outs = pl.pallas_call(body, name="small_sum_adamw", grid_spec=grid_spec, out_shape=out_shape,
                          compiler_params=_params("arbitrary"))(me, *map(_in_hbm, args + [loss_parts]))
    return [outs[4 * e:4 * e + 4] for e in range(n)], outs[4 * n]


MM_TILE = 512
N_MM_TILES = SEQ // MM_TILE
CAT_TILE = 512
N_CAT_TILES = N_CAT // CAT_TILE
SMALL_ROWS = 808
SHARD_ROWS = 32


def kernel(x, g_mix, w_in, b_gate, w_gk_up, b_gk, w_pool_grp, pool_scale, g_gla_head, w_pool_proj, w_gla_proj, w_out, g_ffn, w_up, w_conv, b_conv, w_down, g_final, loss_target, m_g_mix, m_w_in, m_b_gate, m_w_gk_up, m_b_gk, m_w_pool_grp, m_pool_scale, m_g_gla_head, m_w_pool_proj, m_w_gla_proj, m_w_out, m_g_ffn, m_w_up, m_w_conv, m_b_conv, m_w_down, m_g_final, v_g_mix, v_w_in, v_b_gate, v_w_gk_up, v_b_gk, v_w_pool_grp, v_pool_scale, v_g_gla_head, v_w_pool_proj, v_w_gla_proj, v_w_out, v_g_ffn, v_w_up, v_w_conv, v_b_conv, v_w_down, v_g_final):
    xi, yi, ci = lax.axis_index("x"), lax.axis_index("y"), lax.axis_index("c")
    me = 4 * xi + 2 * yi + ci
    core = jnp.reshape(ci, (1,)).astype(jnp.int32)
    chip = jnp.reshape(2 * xi + yi, (1,)).astype(jnp.int32)
    xs, target = x[0], loss_target[0]

    big = dict(w_in=w_in[0].T, w_pool_proj=w_pool_proj[0], w_gla_proj=w_gla_proj[0], w_out=w_out[0], w_up=w_up[0].T,
               w_down=w_down[0])
    moments = dict(w_in=(m_w_in[0].T, v_w_in[0].T), w_pool_proj=(m_w_pool_proj[0], v_w_pool_proj[0]),
                   w_gla_proj=(m_w_gla_proj[0], v_w_gla_proj[0]), w_out=(m_w_out[0], v_w_out[0]),
                   w_up=(m_w_up[0].T, v_w_up[0].T), w_down=(m_w_down[0], v_w_down[0]))
    names = list(big)
    shards = {k: big[k].astype(BF) for k in names}
    shards["w_gk_up"], shards["w_conv"] = w_gk_up[0], w_conv[0]
    gather_groups = (("w_in", "w_gk_up"), ("w_pool_proj", "w_gla_proj", "w_out"), ("w_up", "w_down", "w_conv"))
    started, token = _split_start("gather_start", [
        ([t for k in g for t in (shards[k], _gather_landing(shards[k], me))], 4 * len(g), _gather_first)
        for g in gather_groups])

    def gather_pass(gi, after):
        lands = list(_split_wait(f"gather_wait_{gi}", started[gi], _gather_first, after)[1::2])
        passed, tkn = _split_start(f"gather_pass_{gi}", [(lands, 3 * len(lands), _gather_second)])
        return passed[0], tkn

    def gather_done(gi, passed, after):
        return dict(zip(gather_groups[gi], _split_wait(f"gather_pass_wait_{gi}", passed, _gather_second, after)))

    tok = lambda i, j, k: (i, 0)
    whole = lambda i, j, k: (0, 0)
    kblk = lambda i, j, k: (k, 0)
    ff_tile = (None, None, MM_TILE, FF_BLK)
    ff_seq = (None, None, SEQ, FF_BLK)

    h = _rms_fwd(xs, g_mix + token[:1, :1], "rms_mix")
    wg = gather_done(0, gather_pass(0, h)[0], h)
    wt_cat, wt_gk = _unshard_w_in(wg["w_in"])
    wgk_pad = jnp.pad(wg["w_gk_up"].transpose(1, 0, 2).reshape(GATE_RANK, GLA_DK), ((0, GK_PAD - GATE_RANK), (0, 0)))
    zcat = _mm(h, wt_cat, out_shape=(SEQ, N_CAT), out_dtype=F32, grid=(N_CAT_TILES, 1, 1),
               blk_a=(SEQ, D_MODEL), blk_b=(CAT_TILE, D_MODEL), blk_o=(SEQ, CAT_TILE),
               map_a=whole, map_b=lambda j, i, k: (j, 0), map_o=lambda j, i, k: (0, j), tb=True, name="mm_in")
    la = _gk_fwd(h, wt_gk, wgk_pad, b_gk)
    passed, tkn = gather_pass(1, la)
    o, states = _gla_fwd(zcat, la, tkn)
    wg = gather_done(1, passed, o)
    wpp = wg["w_pool_proj"].transpose(1, 0, 2).reshape(POOL_WIDTH, D_MODEL)
    wgp = wg["w_gla_proj"].reshape(D_MODEL, D_MODEL)
    wout = wg["w_out"].reshape(D_MODEL, D_MODEL)
    og = _post_gla_fwd(o, zcat, g_gla_head)
    ps = _pool_fwd(zcat, w_pool_grp[0], pool_scale)
    y_pool = _mm(ps, wpp, out_shape=(SEQ, D_MODEL), out_dtype=F32, grid=(N_MM_TILES, 1, 1),
                 blk_a=(MM_TILE, POOL_WIDTH), blk_b=(POOL_WIDTH, D_MODEL), blk_o=(MM_TILE, D_MODEL),
                 map_a=tok, map_b=whole, map_o=tok, name="mm_pool_proj")
    sq = dict(out_shape=(SEQ, D_MODEL), grid=(N_MM_TILES, 1, 1), blk_a=(MM_TILE, D_MODEL), blk_b=(D_MODEL, D_MODEL),
              blk_o=(MM_TILE, D_MODEL), map_a=tok, map_b=whole, map_o=tok)
    y_gla = _mm(og, wgp, out_dtype=F32, name="mm_gla_proj", **sq)
    passed, tkn = gather_pass(2, y_gla)
    mixed = _mix_fwd(zcat, b_gate + tkn[:1, :1], y_pool, y_gla)
    x1 = _mm(mixed, wout, out_dtype=F32, res=xs, name="mm_out", **sq)
    h2 = _rms_fwd(x1, g_ffn, "rms_ffn")
    wg = gather_done(2, passed, h2)
    wt_up = wg["w_up"].reshape(2 * D_FF, D_MODEL)
    wdown = wg["w_down"].reshape(D_FF, D_MODEL)
    wconv4 = wg["w_conv"].reshape(2, 4, 3, FF_BLK)
    bconv4 = b_conv.reshape(2, 4, 1, FF_BLK)
    blk4 = lambda b, i, k: (b // 4, b % 4, 0, 0)
    u4 = _mm(h2, wt_up, out_shape=(2, 4, SEQ, FF_BLK), out_dtype=F32, grid=(N_DEV, 1, 1),
             blk_a=(SEQ, D_MODEL), blk_b=(FF_BLK, D_MODEL), blk_o=ff_seq,
             map_a=whole, map_b=lambda b, i, k: (b, 0), map_o=blk4, tb=True, name="mm_up")
    act = _conv_fwd(u4, wconv4, bconv4)
    x2 = _mm_tokens(act, wdown, blk_a=(None, 4, TOK_MM_TILE, FF_BLK), map_a=lambda i: (0, 0, i, 0),
                    pieces=[(b, b * FF_BLK, FF_BLK) for b in range(4)], res=x1, name="mm_down")
    loss_part, dx2, dx2_bf, dg_final = _final_loss(x2, g_final.reshape(1, D_MODEL), target)

    da = _mm(dx2_bf, wdown, out_shape=(1, 4, SEQ, FF_BLK), out_dtype=BF, grid=(4, 1, 1),
             blk_a=(SEQ, D_MODEL), blk_b=(FF_BLK, D_MODEL), blk_o=ff_seq,
             map_a=whole, map_b=lambda b, i, k: (b, 0), map_o=lambda b, i, k: (0, b, 0, 0), tb=True, name="mm_d_act")
    d_wdown = _mm(act, dx2_bf, out_shape=(D_FF, D_MODEL), out_dtype=BF, grid=(4, 1, 1),
                  blk_a=ff_seq, blk_b=(SEQ, D_MODEL), blk_o=(FF_BLK, D_MODEL),
                  map_a=lambda b, i, k: (0, b, 0, 0), map_b=whole, map_o=lambda b, i, k: (b, 0), ta=True,
                  name="mm_d_wdown")
    du4, d_wconv, d_bconv = _conv_bwd(u4, da, wconv4, bconv4)
    dh2 = _mm_tokens(du4, wt_up, blk_a=(2, 4, TOK_MM_TILE, FF_BLK), map_a=lambda i: (0, 0, i, 0),
                     pieces=[((b // 4, b % 4), b * FF_BLK, FF_BLK) for b in range(N_DEV)], name="mm_d_h2")
    d_wt_up = _mm(du4, h2, out_shape=(2 * D_FF, D_MODEL), out_dtype=BF, grid=(N_DEV, 1, 1),
                  blk_a=ff_seq, blk_b=(SEQ, D_MODEL), blk_o=(FF_BLK, D_MODEL),
                  map_a=blk4, map_b=whole, map_o=lambda b, i, k: (b, 0), ta=True, name="mm_d_wup")
    res = {}

    def reduce_start(keys, parts):
        arrays = [t for k in keys for t in (parts[k], lax.empty((4,) + parts[k].shape[2:], BF))]
        st, tkn = _split_start("reduce_start_" + keys[0], [(arrays, 4 * len(keys), _reduce_first)])
        return st[0], tkn

    def reduce_cross(keys, st, after):
        arrays = _split_wait("reduce_wait_" + keys[0], st, _reduce_first, after)
        sums = [_pair_sum(p, r, core, "pair_sum_" + k) for k, p, r in zip(keys, arrays[0::2], arrays[1::2])]
        arrays = [t for s in sums for t in (s, lax.empty((3,) + s.shape[1:], BF))]
        st2, tkn = _split_start("reduce_cross_" + keys[0], [(arrays, 3 * len(keys), _reduce_second)])
        return st2[0], tkn

    def reduce_done(keys, st2, after):
        arrays = _split_wait("reduce_cross_wait_" + keys[0], st2, _reduce_second, after)
        for k, s, r in zip(keys, arrays[0::2], arrays[1::2]):
            outs = _chip_sum_adamw(s, r, big[k], moments[k][0], moments[k][1], chip, "adamw_" + k)
            res[k] = [(t.T if k in ("w_in", "w_up") else t)[None] for t in outs]

    ffn_keys = ("w_down", "w_up")
    ffn_red, tkn = reduce_start(ffn_keys, dict(w_down=d_wdown.reshape(4, 2, D_FF // N_DEV, D_MODEL),
                                               w_up=d_wt_up.reshape(4, 2, FF_BLK, D_MODEL)))
    dx1, dg_ffn = _rms_bwd(dh2, x1, g_ffn + tkn[:1, :1], dx2, "rms_ffn_bwd")

    sq_t = dict(out_shape=(D_MODEL, D_MODEL), grid=(1, 1, N_MM_TILES), blk_a=(MM_TILE, D_MODEL),
                blk_b=(MM_TILE, D_MODEL), blk_o=(D_MODEL, D_MODEL), map_a=kblk, map_b=kblk, map_o=whole, ta=True)
    dmixed = _mm(dx1, wout, out_dtype=F32, tb=True, name="mm_d_mixed", **sq)
    d_wout = _mm(mixed, dx1, out_dtype=BF, name="mm_d_wout", **sq_t)
    dzcat, dy_pool, dy_gla, db_gate = _mix_bwd(dmixed, zcat, b_gate, y_pool, y_gla)
    ffn_red, _ = reduce_cross(ffn_keys, ffn_red, db_gate)
    d_og =_mm(dy_gla, wgp, out_dtype=F32, tb=True, name="mm_d_og", **sq)
    d_wgp = _mm(og, dy_gla, out_dtype=BF, name="mm_d_wgp", **sq_t)
    mix_keys = ("w_out", "w_gla_proj")
    mix_red, tkn = reduce_start(mix_keys, dict(w_out=d_wout.reshape(4, 2, D_MODEL // N_DEV, D_MODEL),
                                               w_gla_proj=d_wgp.reshape(4, 2, D_MODEL // N_DEV, D_MODEL)))
    dzcat, d_o, dg_head = _post_gla_bwd(dzcat, d_og, o, zcat, g_gla_head + tkn[:1, :1])
    dzcat, dla = _gla_bwd(dzcat, zcat, la, d_o, states)
    mix_red, tkn = reduce_cross(mix_keys, mix_red, dla)
    dh_gk, d_wt_gk, d_wgk, db_gk = _gk_bwd(dla, h, wt_gk, wgk_pad, b_gk + tkn[:1, :1])
    dps = _mm(dy_pool, wpp, out_shape=(SEQ, POOL_WIDTH), out_dtype=F32, grid=(N_MM_TILES, 1, 1),
              blk_a=(MM_TILE, D_MODEL), blk_b=(POOL_WIDTH, D_MODEL), blk_o=(MM_TILE, POOL_WIDTH),
              map_a=tok, map_b=whole, map_o=tok, tb=True, name="mm_d_ps")
    d_wpp = _mm(ps, dy_pool, out_shape=(POOL_WIDTH, D_MODEL), out_dtype=F32, grid=(1, 1, N_MM_TILES),
                blk_a=(MM_TILE, POOL_WIDTH), blk_b=(MM_TILE, D_MODEL), blk_o=(POOL_WIDTH, D_MODEL),
                map_a=kblk, map_b=kblk, map_o=whole, ta=True, name="mm_d_wpp")
    dzcat, d_wgrp, d_scale = _pool_bwd(dzcat, zcat, dps, w_pool_grp[0], pool_scale)
    d_wt_cat = _mm(dzcat, h, out_shape=(N_CAT, D_MODEL), out_dtype=BF, grid=(N_CAT_TILES, 1, 1),
                   blk_a=(SEQ, CAT_TILE), blk_b=(SEQ, D_MODEL), blk_o=(CAT_TILE, D_MODEL),
                   map_a=lambda j, i, k: (0, j), map_b=whole, map_o=lambda j, i, k: (j, 0), ta=True, name="mm_d_wcat")
    in_keys = ("w_in", "w_pool_proj")
    in_red, tkn = reduce_start(in_keys, dict(
        w_in=_shard_d_w_in(d_wt_cat, d_wt_gk).reshape(4, 2, IN_SHARD, D_MODEL),
        w_pool_proj=d_wpp.reshape(POOL_WIDTH, N_DEV, D_MODEL // N_DEV).transpose(1, 0, 2).astype(BF)
        .reshape(4, 2, POOL_WIDTH, D_MODEL // N_DEV)))
    dh = _mm_tokens(dzcat, wt_cat, blk_a=(TOK_MM_TILE, N_CAT), map_a=lambda i: (i, 0), pieces=[(None, 0, N_CAT)],
                    res=dh_gk, after=tkn, name="mm_d_h")
    in_red, tkn = reduce_cross(in_keys, in_red, dh)
    grad_x, dg_mix = _rms_bwd(dh, xs, g_mix + tkn[:1, :1], dx1, "rms_mix_bwd")
    reduce_done(ffn_keys, ffn_red, grad_x)
    reduce_done(mix_keys, mix_red, res["w_down"][0])

    row = lambda t: t.reshape(1, D_MODEL)
    conv_vec = lambda t: t.reshape(2, 4, 1, FF_BLK)
    small = [("g_mix", dg_mix, g_mix, m_g_mix, v_g_mix, False), ("b_gate", db_gate, b_gate, m_b_gate, v_b_gate, False),
             ("w_gk_up", d_wgk.reshape(GATE_RANK, N_DEV, GLA_DK // N_DEV).transpose(1, 0, 2), w_gk_up, m_w_gk_up,
              v_w_gk_up, True),
             ("b_gk", db_gk, b_gk, m_b_gk, v_b_gk, False),
             ("w_pool_grp", d_wgrp, w_pool_grp, m_w_pool_grp, v_w_pool_grp, False),
             ("pool_scale", d_scale, pool_scale, m_pool_scale, v_pool_scale, False),
             ("g_gla_head", dg_head, g_gla_head, m_g_gla_head, v_g_gla_head, False),
             ("g_ffn", dg_ffn, g_ffn, m_g_ffn, v_g_ffn, False),
             ("w_conv", d_wconv.reshape(N_DEV, 3, FF_BLK), w_conv, m_w_conv, v_w_conv, True),
             ("b_conv", d_bconv, conv_vec(b_conv), conv_vec(m_b_conv), conv_vec(v_b_conv), False),
             ("g_final", dg_final, row(g_final), row(m_g_final), row(v_g_final), False)]
    gathered = _all_gather([t[1] for t in small] + [loss_part], "gather_small_grads")
    small_out, loss_sum = _small_sum_adamw(jnp.reshape(me, (1,)).astype(jnp.int32),
                                           [(p,) + t[2:] for p, t in zip(gathered, small)], gathered[-1])
    for t, outs in zip(small, small_out):
        res[t[0]] = list(outs)
    res["b_conv"] = [t.reshape(b_conv.shape) for t in res["b_conv"]]
    res["g_final"] = [t.reshape(g_final.shape) for t in res["g_final"]]

    reduce_done(in_keys, in_red, loss_sum)
    loss = loss_sum[0, 0]
    order =["g_mix", "w_in", "b_gate", "w_gk_up", "b_gk", "w_pool_grp", "pool_scale", "g_gla_head", "w_pool_proj",
             "w_gla_proj", "w_out", "g_ffn", "w_up", "w_conv", "b_conv", "w_down", "g_final"]
    return (loss, grad_x[None], *[res[k][0] for k in order], *[res[k][1] for k in order],
            *[res[k][2] for k in order], *[res[k][3] for k in order])
```

```python
import functools

import jax
import jax.numpy as jnp
from jax import lax
from jax.experimental import pallas as pl
from jax.experimental.pallas import tpu as pltpu

F32 = jnp.float32
BF = jnp.bfloat16
HIGHEST = lax.Precision.HIGHEST
MESH = pl.DeviceIdType.MESH

N_DEV = 8
SEQ = 2048
D_MODEL = 1024
CHUNK = 64
EPS = 1e-6
POOL_WIDTH = 512
POOL_WINDOWS = (2, 4, 8, 16)
POOL_GD = 128
POOL_HALO = 16
HEADS = 4
HK = 128
HV = 256
GLA_DK = 512
GATE_RANK = 16
GATE_NORM = 16.0
D_FF = 2816
FF_BLK = 704
IN_TOTAL = 5648
IN_SHARD = 706
C_QKV, C_GATE, C_OG, C_POOL = 0, 2048, 4096, 5120
N_CAT = 5632
R_POOL, R_QKV, R_OG, R_GK, R_GATE = 0, 512, 2560, 3584, 3600
GK_PAD = 128

ADAM_LR, ADAM_B1, ADAM_B2, ADAM_EPS, ADAM_WD, ADAM_STEP = 0.001, 0.9, 0.999, 1e-08, 0.01, 10
ADAM_C1 = 1.0 - ADAM_B1 ** ADAM_STEP
ADAM_C2 = 1.0 - ADAM_B2 ** ADAM_STEP

VMEM_BYTES_V7X = 64 * 1024 * 1024
VMEM_LIMIT = 48 * 1024 * 1024

TOK_TILE = 256
HALO = 8
GLA_CPS = 4


def _params(*sem):
    return pltpu.CompilerParams(dimension_semantics=sem, vmem_limit_bytes=VMEM_LIMIT)


def _const_spec(shape):
    nd = len(shape)
    return pl.BlockSpec(shape, lambda *_: (0,) * nd)


def _in_hbm(t):
    return pltpu.with_memory_space_constraint(t, pltpu.HBM)


def _dot(a, b, ta=False, tb=False):
    dims = (((0 if ta else 1,), (1 if tb else 0,)), ((), ()))
    return lax.dot_general(a.astype(BF), b.astype(BF), dims, preferred_element_type=F32)


def _dot_exact(a, b):
    return jnp.dot(a, b, precision=HIGHEST, preferred_element_type=F32)


def _sigmoid(x):
    return 0.5 * jnp.tanh(0.5 * x) + 0.5


def _mm(a, b, *, out_shape, out_dtype, grid, blk_a, blk_b, blk_o, map_a, map_b, map_o, ta=False, tb=False,
        res=None, name):
    gk = grid[2]

    def body(*refs):
        if res is None:
            a_ref, b_ref, o_ref = refs[:3]
            r_ref = None
            scr = refs[3:]
        else:
            a_ref, b_ref, r_ref, o_ref = refs[:4]
            scr = refs[4:]
        prod = _dot(a_ref[...], b_ref[...], ta, tb)

        def finish(total):
            if r_ref is not None:
                total = total + r_ref[...]
            o_ref[...] = total.astype(out_dtype)

        if gk == 1:
            finish(prod)
        else:
            acc = scr[0]
            k = pl.program_id(2)

            @pl.when(k == 0)
            def _():
                acc[...] = prod

            @pl.when(k > 0)
            def _():
                acc[...] += prod

            @pl.when(k == gk - 1)
            def _():
                finish(acc[...])

    in_specs = [pl.BlockSpec(blk_a, map_a), pl.BlockSpec(blk_b, map_b)]
    args = [a, b]
    if res is not None:
        in_specs.append(pl.BlockSpec(blk_o, map_o))
        args.append(res)
    return pl.pallas_call(
        body, name=name, grid=grid, in_specs=in_specs, out_specs=pl.BlockSpec(blk_o, map_o),
        out_shape=jax.ShapeDtypeStruct(out_shape, out_dtype),
        scratch_shapes=[] if gk == 1 else [pltpu.VMEM(tuple(d for d in blk_o if d is not None), F32)],
        compiler_params=_params("parallel", "parallel", "arbitrary"),
    )(*[_in_hbm(t) for t in args])


TOK_MM_TILE = 256


def _mm_tokens(a, w, *, blk_a, map_a, pieces, res=None, after=None, then=None, name):
    n_in = 2 + (res is not None) + (after is not None) + (0 if then is None else len(then) - 1)

    def accumulate(ref, part):
        @pl.when(pl.program_id(0) == 0)
        def _():
            ref[...] = part

        @pl.when(pl.program_id(0) > 0)
        def _():
            ref[...] += part

    def body(*refs):
        a_ref, w_ref = refs[:2]
        extra, outs = refs[n_in - (0 if then is None else len(then) - 1):n_in], refs[n_in:]
        total = None
        for idx, row, n in pieces:
            av = a_ref[...] if idx is None else a_ref[idx]
            prod = _dot(av, w_ref[row:row + n, :])
            total = prod if total is None else total + prod
        if res is not None:
            total = total + refs[2][...]
        if then is None:
            outs[0][...] = total
        elif then[0] == "rms_bwd":
            dx, part = _rms_bwd_tile(total, extra[0][...], extra[1][...], extra[2][...])
            outs[0][...] = dx
            accumulate(outs[1], part)
        else:
            lpart, dx, part = _loss_tile(total, extra[0][...], extra[1][...])
            outs[1][...] = dx
            outs[2][...] = dx.astype(BF)
            accumulate(outs[0], lpart)
            accumulate(outs[3], part)

    tile = pl.BlockSpec((TOK_MM_TILE, D_MODEL), lambda i: (i, 0))
    vec = _const_spec((1, D_MODEL))
    big = jax.ShapeDtypeStruct((SEQ, D_MODEL), F32)
    small = jax.ShapeDtypeStruct((1, D_MODEL), F32)
    in_specs = [pl.BlockSpec(blk_a, map_a), _const_spec(w.shape)]
    args = [a, w]
    if res is not None:
        in_specs.append(tile)
        args.append(res)
    if after is not None:
        in_specs.append(pl.BlockSpec(memory_space=pl.ANY))
        args.append(after)
    if then is None:
        out_specs, out_shape = tile, big
    elif then[0] == "rms_bwd":
        in_specs += [tile, vec, tile]
        out_specs, out_shape = [tile, vec], [big, small]
    else:
        in_specs += [vec, tile]
        out_specs = [_const_spec((1, 128)), tile, tile, vec]
        out_shape = [jax.ShapeDtypeStruct((1, 128), F32), big, jax.ShapeDtypeStruct((SEQ, D_MODEL), BF), small]
    if then is not None:
        args += list(then[1:])
    return pl.pallas_call(
        body, name=name, grid=(SEQ // TOK_MM_TILE,), in_specs=in_specs, out_specs=out_specs, out_shape=out_shape,
        compiler_params=_params("parallel" if then is None else "arbitrary"),
    )(*[_in_hbm(t) for t in args])


def _rms_fwd(x, g, name):
    def body(x_ref, g_ref, o_ref):
        xv = x_ref[...]
        r = lax.rsqrt(jnp.mean(xv * xv, axis=-1, keepdims=True) + EPS)
        o_ref[...] = (xv * r * g_ref[...]).astype(BF)

    tile = pl.BlockSpec((TOK_TILE, D_MODEL), lambda i: (i, 0))
    return pl.pallas_call(
        body, name=name, grid=(SEQ // TOK_TILE,), in_specs=[tile, _const_spec((1, D_MODEL))], out_specs=tile,
        out_shape=jax.ShapeDtypeStruct((SEQ, D_MODEL), BF), compiler_params=_params("parallel"),
    )(*map(_in_hbm, (x, g)))


def _rms_bwd_tile(dyv, xv, gv, dresv):
    r = lax.rsqrt(jnp.mean(xv * xv, axis=-1, keepdims=True) + EPS)
    xn = xv * r
    dxn = dyv * gv
    return dresv + r * (dxn - xn * jnp.mean(dxn * xn, axis=-1, keepdims=True)), jnp.sum(dyv * xn, axis=0, keepdims=True)


def _rms_bwd(dy, x, g, dres, name):
    def body(dy_ref, x_ref, g_ref, dres_ref, dx_ref, dg_ref):
        dx_ref[...], part = _rms_bwd_tile(dy_ref[...], x_ref[...], g_ref[...], dres_ref[...])

        @pl.when(pl.program_id(0) == 0)
        def _():
            dg_ref[...] = part

        @pl.when(pl.program_id(0) > 0)
        def _():
            dg_ref[...] += part

    tile = pl.BlockSpec((TOK_TILE, D_MODEL), lambda i: (i, 0))
    vec = _const_spec((1, D_MODEL))
    return pl.pallas_call(
        body, name=name, grid=(SEQ // TOK_TILE,), in_specs=[tile, tile, vec, tile], out_specs=[tile, vec],
        out_shape=[jax.ShapeDtypeStruct((SEQ, D_MODEL), F32), jax.ShapeDtypeStruct((1, D_MODEL), F32)],
        compiler_params=_params("arbitrary"),
    )(*map(_in_hbm, (dy, x, g, dres)))


def _loss_tile(xv, gv, tv):
    r = lax.rsqrt(jnp.mean(xv * xv, axis=-1, keepdims=True) + EPS)
    xn = xv * r
    err = xn * gv - tv
    lpart = jnp.full((1, 128), 0.5 * jnp.sum(jnp.mean(err * err, axis=-1, keepdims=True)), F32)
    dyv = err * (1.0 / D_MODEL)
    dxn = dyv * gv
    return lpart, r * (dxn - xn * jnp.mean(dxn * xn, axis=-1, keepdims=True)), jnp.sum(dyv * xn, axis=0, keepdims=True)


def _pool_counts(w):
    pos = lax.broadcasted_iota(jnp.int32, (SEQ, 1), 0).astype(F32)
    return jnp.minimum(pos + 1.0, float(w))


def _pool_window(u, w, ext):
    ext[pl.ds(POOL_HALO, SEQ), :] = u
    win = u
    for j in range(1, w):
        win = win + ext[pl.ds(POOL_HALO - j, SEQ), :]
    return win / _pool_counts(w) - u


def _pool_fwd(zcat, w_grp, scale):
    def body(z_ref, w_ref, s_ref, o_ref, ext):
        ext[pl.ds(0, POOL_HALO), :] = jnp.zeros((POOL_HALO, POOL_GD), F32)
        for g, w in enumerate(POOL_WINDOWS):
            cols = slice(g * POOL_GD, (g + 1) * POOL_GD)
            p = _pool_window(z_ref[:, cols], w, ext)
            o_ref[:, cols] = (_dot(p, w_ref[g]) * s_ref[:, cols]).astype(BF)

    return pl.pallas_call(
        body, name="pool_fwd", grid=(1,),
        in_specs=[pl.BlockSpec((SEQ, POOL_WIDTH), lambda i: (0, C_POOL // POOL_WIDTH)),
                  _const_spec((4, POOL_GD, POOL_GD)), _const_spec((1, POOL_WIDTH))],
        out_specs=_const_spec((SEQ, POOL_WIDTH)), out_shape=jax.ShapeDtypeStruct((SEQ, POOL_WIDTH), BF),
        scratch_shapes=[pltpu.VMEM((POOL_HALO + SEQ, POOL_GD), F32)], compiler_params=_params("arbitrary"),
    )(*map(_in_hbm, (zcat, w_grp, scale)))


def _pool_bwd(dzcat, zcat, dps, w_grp, scale):
    def body(dz_in, z_ref, dps_ref, w_ref, s_ref, dz_ref, dw_ref, dsc_ref, ext, ext2):
        del dz_in
        ext[pl.ds(0, POOL_HALO), :] = jnp.zeros((POOL_HALO, POOL_GD), F32)
        ext2[pl.ds(SEQ, POOL_HALO), :] = jnp.zeros((POOL_HALO, POOL_GD), F32)
        for g, w in enumerate(POOL_WINDOWS):
            cols = slice(g * POOL_GD, (g + 1) * POOL_GD)
            p = _pool_window(z_ref[:, cols], w, ext)
            wg = w_ref[g]
            pg = _dot(p, wg)
            dpsv = dps_ref[:, cols]
            dsc_ref[:, cols] = jnp.sum(dpsv * pg, axis=0, keepdims=True)
            dpg = dpsv * s_ref[:, cols]
            dw_ref[g] = _dot(p, dpg, ta=True)
            dp = _dot(dpg, wg, tb=True)
            dpc = dp / _pool_counts(w)
            ext2[pl.ds(0, SEQ), :] = dpc
            du = dpc
            for j in range(1, w):
                du = du + ext2[pl.ds(j, SEQ), :]
            dz_ref[:, cols] = (du - dp).astype(BF)

    return pl.pallas_call(
        body, name="pool_bwd", grid=(1,),
        in_specs=[pl.BlockSpec(memory_space=pl.ANY),
                  pl.BlockSpec((SEQ, POOL_WIDTH), lambda i: (0, C_POOL // POOL_WIDTH)),
                  _const_spec((SEQ, POOL_WIDTH)), _const_spec((4, POOL_GD, POOL_GD)), _const_spec((1, POOL_WIDTH))],
        out_specs=[pl.BlockSpec((SEQ, POOL_WIDTH), lambda i: (0, C_POOL // POOL_WIDTH)),
                   _const_spec((4, POOL_GD, POOL_GD)), _const_spec((1, POOL_WIDTH))],
        out_shape=[jax.ShapeDtypeStruct((SEQ, N_CAT), BF), jax.ShapeDtypeStruct((4, POOL_GD, POOL_GD), F32),
                   jax.ShapeDtypeStruct((1, POOL_WIDTH), F32)],
        scratch_shapes=[pltpu.VMEM((POOL_HALO + SEQ, POOL_GD), F32), pltpu.VMEM((SEQ + POOL_HALO, POOL_GD), F32)],
        input_output_aliases={0: 0}, compiler_params=_params("arbitrary"),
    )(*map(_in_hbm, (dzcat, zcat, dps, w_grp, scale)))


GK_TILE = 512


def _gk_fwd(h, wt_gk, wgk_pad, b_gk):
    def body(h_ref, wt_ref, w_ref, b_ref, la_ref):
        z_gk = _dot(h_ref[...], wt_ref[...], tb=True)
        pre = _dot(z_gk, w_ref[...]) + b_ref[...]
        la_ref[...] = (jnp.minimum(pre, 0.0) - jnp.log(1.0 + jnp.exp(-jnp.abs(pre)))) * (1.0 / GATE_NORM)

    return pl.pallas_call(
        body, name="gk_fwd", grid=(SEQ // GK_TILE,),
        in_specs=[pl.BlockSpec((GK_TILE, D_MODEL), lambda i: (i, 0)), _const_spec((GK_PAD, D_MODEL)),
                  _const_spec((GK_PAD, GLA_DK)), _const_spec((1, GLA_DK))],
        out_specs=pl.BlockSpec((GK_TILE, GLA_DK), lambda i: (i, 0)),
        out_shape=jax.ShapeDtypeStruct((SEQ, GLA_DK), F32), compiler_params=_params("parallel"),
    )(*map(_in_hbm, (h, wt_gk, wgk_pad, b_gk)))


def _gk_bwd(dla, h, wt_gk, wgk_pad, b_gk):
    def body(dla_ref, h_ref, wt_ref, w_ref, b_ref, dh_ref, dwt_ref, dw_ref, db_ref):
        hv = h_ref[...]
        wtv = wt_ref[...]
        wv = w_ref[...]
        z_gk = _dot(hv, wtv, tb=True)
        pre = _dot(z_gk, wv) + b_ref[...]
        dpre = dla_ref[...] * (1.0 / GATE_NORM) * (1.0 - _sigmoid(pre))
        dz_gk = _dot(dpre, wv, tb=True)
        dh_ref[...] = _dot(dz_gk, wtv)
        dwtp = _dot(dz_gk, hv, ta=True)
        dwp = _dot(z_gk, dpre, ta=True)[:GATE_RANK]
        dbp = jnp.sum(dpre, axis=0, keepdims=True)

        @pl.when(pl.program_id(0) == 0)
        def _():
            dwt_ref[...] = dwtp
            dw_ref[...] = dwp
            db_ref[...] = dbp

        @pl.when(pl.program_id(0) > 0)
        def _():
            dwt_ref[...] += dwtp
            dw_ref[...] += dwp
            db_ref[...] += dbp

    tile = pl.BlockSpec((GK_TILE, D_MODEL), lambda i: (i, 0))
    return pl.pallas_call(
        body, name="gk_bwd", grid=(SEQ // GK_TILE,),
        in_specs=[pl.BlockSpec((GK_TILE, GLA_DK), lambda i: (i, 0)), tile, _const_spec((GK_PAD, D_MODEL)),
                  _const_spec((GK_PAD, GLA_DK)), _const_spec((1, GLA_DK))],
        out_specs=[tile, _const_spec((GK_PAD, D_MODEL)), _const_spec((GATE_RANK, GLA_DK)), _const_spec((1, GLA_DK))],
        out_shape=[jax.ShapeDtypeStruct((SEQ, D_MODEL), F32), jax.ShapeDtypeStruct((GK_PAD, D_MODEL), F32),
                   jax.ShapeDtypeStruct((GATE_RANK, GLA_DK), F32), jax.ShapeDtypeStruct((1, GLA_DK), F32)],
        compiler_params=_params("arbitrary"),
    )(*map(_in_hbm, (dla, h, wt_gk, wgk_pad, b_gk)))


GLA_ROWS = GLA_CPS * CHUNK
GLA_STEPS = SEQ // GLA_ROWS
QKV_W = 2048


def _gla_chunk(qkv_ref, la_ref, rows, h):
    tri = lax.broadcasted_iota(jnp.int32, (CHUNK, CHUNK), 0) >= lax.broadcasted_iota(jnp.int32, (CHUNK, CHUNK), 1)
    q = qkv_ref[rows, h * HK:(h + 1) * HK] * (HK ** -0.5)
    k = qkv_ref[rows, GLA_DK + h * HK:GLA_DK + (h + 1) * HK]
    v = qkv_ref[rows, 2 * GLA_DK + h * HV:2 * GLA_DK + (h + 1) * HV]
    la = la_ref[rows, h * HK:(h + 1) * HK]
    bc = _dot_exact(tri.astype(F32), la)
    e_pos, e_neg = jnp.exp(bc), jnp.exp(-bc)
    dl = jnp.exp(jnp.sum(la, axis=0, keepdims=True))
    q_fw, q_bw, k_fw, k_bw = q * e_pos, q * e_neg, k * e_neg, k * e_pos
    scores = jnp.where(tri, _dot(q_fw, k_fw, tb=True), _dot(q_bw, k_bw, tb=True))
    return tri, v, e_pos, e_neg, dl, q_fw, q_bw, k_fw, k_bw, scores


def _gla_fwd(zcat, la, after):
    def body(qkv_ref, la_ref, after_ref, o_ref, st_ref, state):
        del after_ref

        @pl.when(pl.program_id(0) == 0)
        def _():
            state[...] = jnp.zeros_like(state)

        for c in range(GLA_CPS):
            rows = slice(c * CHUNK, (c + 1) * CHUNK)
            for h in range(HEADS):
                _, v, _, _, dl, q_fw, _, k_fw, _, scores = _gla_chunk(qkv_ref, la_ref, rows, h)
                st = state[h]
                st_ref[c, h] = st
                o_ref[rows, h * HV:(h + 1) * HV] = _dot(scores, v) + _dot(q_fw, st, tb=True)
                state[h] = st * dl + _dot(v, k_fw * dl, ta=True)

    return pl.pallas_call(
        body, name="gla_fwd", grid=(GLA_STEPS,),
        in_specs=[pl.BlockSpec((GLA_ROWS, QKV_W), lambda i: (i, 0)), pl.BlockSpec((GLA_ROWS, GLA_DK), lambda i: (i, 0)),
                  pl.BlockSpec(memory_space=pl.ANY)],
        out_specs=[pl.BlockSpec((GLA_ROWS, D_MODEL), lambda i: (i, 0)),
                   pl.BlockSpec((GLA_CPS, HEADS, HV, HK), lambda i: (i, 0, 0, 0))],
        out_shape=[jax.ShapeDtypeStruct((SEQ, D_MODEL), F32),
                   jax.ShapeDtypeStruct((SEQ // CHUNK, HEADS, HV, HK), F32)],
        scratch_shapes=[pltpu.VMEM((HEADS, HV, HK), F32)], compiler_params=_params("arbitrary"),
    )(*map(_in_hbm, (zcat, la)), after)


def _gla_bwd(dzcat, zcat, la, d_o, states):
    def body(dz_in, qkv_ref, la_ref, do_ref, st_ref, dqkv_ref, dla_ref, dstate):
        del dz_in

        @pl.when(pl.program_id(0) == 0)
        def _():
            dstate[...] = jnp.zeros_like(dstate)

        last_row = lax.broadcasted_iota(jnp.int32, (CHUNK, HK), 0) == CHUNK - 1
        upper = (lax.broadcasted_iota(jnp.int32, (CHUNK, CHUNK), 0)
                 <= lax.broadcasted_iota(jnp.int32, (CHUNK, CHUNK), 1)).astype(F32)
        for c in reversed(range(GLA_CPS)):
            rows = slice(c * CHUNK, (c + 1) * CHUNK)
            for h in range(HEADS):
                tri, v, e_pos, e_neg, dl, q_fw, q_bw, k_fw, k_bw, scores = _gla_chunk(qkv_ref, la_ref, rows, h)
                st = st_ref[c, h]
                dst = dstate[h]
                d_out = do_ref[rows, h * HV:(h + 1) * HV]
                k_dec = k_fw * dl
                dp = _dot(d_out, v, tb=True)
                dp_fw = jnp.where(tri, dp, 0.0)
                dp_bw = jnp.where(tri, 0.0, dp)
                dv = _dot(scores, d_out, ta=True) + _dot(k_dec, dst, tb=True)
                dk_dec = _dot(v, dst)
                dq_fw = _dot(dp_fw, k_fw) + _dot(d_out, st)
                dk_fw = _dot(dp_fw, q_fw, ta=True) + dk_dec * dl
                dq_bw = _dot(dp_bw, k_bw)
                dk_bw = _dot(dp_bw, q_bw, ta=True)
                ddl = jnp.sum(st * dst, axis=0, keepdims=True) + jnp.sum(k_fw * dk_dec, axis=0, keepdims=True)
                dstate[h] = dst * dl + _dot(d_out, q_fw, ta=True)
                dq = (dq_fw * e_pos + dq_bw * e_neg) * (HK ** -0.5)
                dk = dk_fw * e_neg + dk_bw * e_pos
                db = dq_fw * q_fw - dk_fw * k_fw - dq_bw * q_bw + dk_bw * k_bw + jnp.where(last_row, ddl * dl, 0.0)
                dla_ref[rows, h * HK:(h + 1) * HK] = _dot_exact(upper, db)
                dqkv_ref[rows, h * HK:(h + 1) * HK] = dq.astype(BF)
                dqkv_ref[rows, GLA_DK + h * HK:GLA_DK + (h + 1) * HK] = dk.astype(BF)
                dqkv_ref[rows, 2 * GLA_DK + h * HV:2 * GLA_DK + (h + 1) * HV] = dv.astype(BF)

    rev = lambda i: (GLA_STEPS - 1 - i, 0)
    return pl.pallas_call(
        body, name="gla_bwd", grid=(GLA_STEPS,),
        in_specs=[pl.BlockSpec(memory_space=pl.ANY), pl.BlockSpec((GLA_ROWS, QKV_W), rev),
                  pl.BlockSpec((GLA_ROWS, GLA_DK), rev), pl.BlockSpec((GLA_ROWS, D_MODEL), rev),
                  pl.BlockSpec((GLA_CPS, HEADS, HV, HK), lambda i: (GLA_STEPS - 1 - i, 0, 0, 0))],
        out_specs=[pl.BlockSpec((GLA_ROWS, QKV_W), rev), pl.BlockSpec((GLA_ROWS, GLA_DK), rev)],
        out_shape=[jax.ShapeDtypeStruct((SEQ, N_CAT), BF), jax.ShapeDtypeStruct((SEQ, GLA_DK), F32)],
        scratch_shapes=[pltpu.VMEM((HEADS, HV, HK), F32)], input_output_aliases={0: 0},
        compiler_params=_params("arbitrary"),
    )(*map(_in_hbm, (dzcat, zcat, la, d_o, states)))


def _silu_parts(x):
    s = _sigmoid(x)
    return x * s, s * (1.0 + x * (1.0 - s))


def _post_gla_fwd(o, zcat, g_head):
    def body(o_ref, zog_ref, g_ref, out_ref):
        for h in range(HEADS):
            cols = slice(h * HV, (h + 1) * HV)
            ov = o_ref[:, cols]
            r = lax.rsqrt(jnp.mean(ov * ov, axis=-1, keepdims=True) + EPS)
            act, _ = _silu_parts(zog_ref[:, cols])
            out_ref[:, cols] = (ov * r * g_ref[...] * act).astype(BF)

    tile = pl.BlockSpec((TOK_TILE, D_MODEL), lambda i: (i, 0))
    return pl.pallas_call(
        body, name="post_gla_fwd", grid=(SEQ // TOK_TILE,),
        in_specs=[tile, pl.BlockSpec((TOK_TILE, D_MODEL), lambda i: (i, C_OG // D_MODEL)), _const_spec((1, HV))],
        out_specs=tile, out_shape=jax.ShapeDtypeStruct((SEQ, D_MODEL), BF), compiler_params=_params("parallel"),
    )(*map(_in_hbm, (o, zcat, g_head)))


def _post_gla_bwd(dzcat, d_og, o, zcat, g_head):
    def body(dz_in, dog_ref, o_ref, zog_ref, g_ref, dz_ref, do_ref, dg_ref):
        del dz_in
        gpart = jnp.zeros((1, HV), F32)
        gv = g_ref[...]
        for h in range(HEADS):
            cols = slice(h * HV, (h + 1) * HV)
            ov = o_ref[:, cols]
            r = lax.rsqrt(jnp.mean(ov * ov, axis=-1, keepdims=True) + EPS)
            on = ov * r
            act, dact = _silu_parts(zog_ref[:, cols])
            dogv = dog_ref[:, cols]
            dz_ref[:, cols] = (dogv * on * gv * dact).astype(BF)
            d_on_g = dogv * act
            gpart = gpart + jnp.sum(d_on_g * on, axis=0, keepdims=True)
            dxn = d_on_g * gv
            do_ref[:, cols] = r * (dxn - on * jnp.mean(dxn * on, axis=-1, keepdims=True))

        @pl.when(pl.program_id(0) == 0)
        def _():
            dg_ref[...] = gpart

        @pl.when(pl.program_id(0) > 0)
        def _():
            dg_ref[...] += gpart

    tile = pl.BlockSpec((TOK_TILE, D_MODEL), lambda i: (i, 0))
    ogspec = pl.BlockSpec((TOK_TILE, D_MODEL), lambda i: (i, C_OG // D_MODEL))
    return pl.pallas_call(
        body, name="post_gla_bwd", grid=(SEQ // TOK_TILE,),
        in_specs=[pl.BlockSpec(memory_space=pl.ANY), tile, tile, ogspec, _const_spec((1, HV))],
        out_specs=[ogspec, tile, _const_spec((1, HV))],
        out_shape=[jax.ShapeDtypeStruct((SEQ, N_CAT), BF), jax.ShapeDtypeStruct((SEQ, D_MODEL), F32),
                   jax.ShapeDtypeStruct((1, HV), F32)],
        input_output_aliases={0: 0}, compiler_params=_params("arbitrary"),
    )(*map(_in_hbm, (dzcat, d_og, o, zcat, g_head)))


GATE_W = 2 * D_MODEL


def _mix_fwd(zcat, b_gate, y_pool, y_gla):
    def body(zg_ref, b_ref, yp_ref, yg_ref, out_ref):
        g0 = _sigmoid(zg_ref[:, :D_MODEL] + b_ref[:, :D_MODEL])
        g1 = _sigmoid(zg_ref[:, D_MODEL:] + b_ref[:, D_MODEL:])
        out_ref[...] = (g0 * yp_ref[...] + g1 * yg_ref[...]).astype(BF)

    tile = pl.BlockSpec((TOK_TILE, D_MODEL), lambda i: (i, 0))
    return pl.pallas_call(
        body, name="mix_fwd", grid=(SEQ // TOK_TILE,),
        in_specs=[pl.BlockSpec((TOK_TILE, GATE_W), lambda i: (i, C_GATE // GATE_W)), _const_spec((1, GATE_W)), tile, tile],
        out_specs=tile, out_shape=jax.ShapeDtypeStruct((SEQ, D_MODEL), BF), compiler_params=_params("parallel"),
    )(*map(_in_hbm, (zcat, b_gate, y_pool, y_gla)))


def _mix_bwd(dmixed, zcat, b_gate, y_pool, y_gla):
    def body(dm_ref, zg_ref, b_ref, yp_ref, yg_ref, dz_ref, dyp_ref, dyg_ref, db_ref):
        dm = dm_ref[...]
        g0 = _sigmoid(zg_ref[:, :D_MODEL] + b_ref[:, :D_MODEL])
        g1 = _sigmoid(zg_ref[:, D_MODEL:] + b_ref[:, D_MODEL:])
        dyp_ref[...] = (dm * g0).astype(BF)
        dyg_ref[...] = (dm * g1).astype(BF)
        dz0 = dm * yp_ref[...] * g0 * (1.0 - g0)
        dz1 = dm * yg_ref[...] * g1 * (1.0 - g1)
        dz_ref[:, :D_MODEL] = dz0.astype(BF)
        dz_ref[:, D_MODEL:] = dz1.astype(BF)
        b0 = jnp.sum(dz0, axis=0, keepdims=True)
        b1 = jnp.sum(dz1, axis=0, keepdims=True)

        @pl.when(pl.program_id(0) == 0)
        def _():
            db_ref[:, :D_MODEL] = b0
            db_ref[:, D_MODEL:] = b1

        @pl.when(pl.program_id(0) > 0)
        def _():
            db_ref[:, :D_MODEL] += b0
            db_ref[:, D_MODEL:] += b1

    tile = pl.BlockSpec((TOK_TILE, D_MODEL), lambda i: (i, 0))
    gspec = pl.BlockSpec((TOK_TILE, GATE_W), lambda i: (i, C_GATE // GATE_W))
    return pl.pallas_call(
        body, name="mix_bwd", grid=(SEQ // TOK_TILE,),
        in_specs=[tile, gspec, _const_spec((1, GATE_W)), tile, tile],
        out_specs=[gspec, tile, tile, _const_spec((1, GATE_W))],
        out_shape=[jax.ShapeDtypeStruct((SEQ, N_CAT), BF), jax.ShapeDtypeStruct((SEQ, D_MODEL), BF),
                   jax.ShapeDtypeStruct((SEQ, D_MODEL), BF), jax.ShapeDtypeStruct((1, GATE_W), F32)],
        compiler_params=_params("arbitrary"),
    )(*map(_in_hbm, (dmixed, zcat, b_gate, y_pool, y_gla)))


N_TOK_TILES = SEQ // TOK_TILE
HALO_PER_TILE = TOK_TILE // HALO


LANE_TILES = tuple((lo, min(128, FF_BLK - lo)) for lo in range(0, FF_BLK, 128))


def _taps(w_ref, b_ref, half, lanes, rows):
    shape = (rows, lanes.stop - lanes.start)
    return ([jnp.broadcast_to(w_ref[half, j:j + 1, lanes], shape) for j in range(3)],
            jnp.broadcast_to(b_ref[half, :, lanes], shape))


def _conv_strips(u_ref, ub_ref, ua_ref, taps, lanes, width, n_strips):
    first = pl.program_id(1) == 0
    row = lax.broadcasted_iota(jnp.int32, (HALO, width), 0)
    prev = [[pltpu.roll(jnp.where(first, 0.0, ub_ref[half, :, lanes]), k, 0) for k in (1, 2)] for half in range(2)]
    for s in range(n_strips + (ua_ref is not None)):
        u3, conv = [], []
        for half in range(2):
            cur = u_ref[half, s * HALO:(s + 1) * HALO, lanes] if s < n_strips else ua_ref[half, :, lanes]
            rolled = [pltpu.roll(cur, k, 0) for k in (1, 2)]
            frames = [jnp.where(row >= 2, rolled[1], prev[half][1]), jnp.where(row >= 1, rolled[0], prev[half][0]), cur]
            prev[half] = rolled
            w3, bias = taps[half]
            u3.append(frames)
            conv.append(bias + frames[0] * w3[0] + frames[1] * w3[1] + frames[2] * w3[2])
        yield s, u3, conv


def _pair_specs(pairs):
    tile = pl.BlockSpec((pairs, None, TOK_TILE, FF_BLK), lambda b, i: (0, b, i, 0))
    before = pl.BlockSpec((pairs, None, HALO, FF_BLK), lambda b, i: (0, b, jnp.maximum(i * HALO_PER_TILE - 1, 0), 0))
    after = pl.BlockSpec((pairs, None, HALO, FF_BLK),
                         lambda b, i: (0, b, jnp.minimum((i + 1) * HALO_PER_TILE, SEQ // HALO - 1), 0))

    def vec(rows):
        return pl.BlockSpec((2, None, rows, FF_BLK), lambda b, i: (0, b, 0, 0))

    return tile, before, after, vec


N_STRIPS = TOK_TILE // HALO


def _conv_fwd(u, w_conv, b_conv):
    def body(u_ref, ub_ref, w_ref, b_ref, a_ref):
        for lo, width in LANE_TILES:
            lanes = slice(lo, lo + width)
            taps = [_taps(w_ref, b_ref, half, lanes, HALO) for half in range(2)]
            pending = None
            for s, _, (cg, cv) in _conv_strips(u_ref, ub_ref, None, taps, lanes, width, N_STRIPS):
                act = cg * _sigmoid(cg) * cv
                if s % 2 == 0:
                    pending = act
                else:
                    a_ref[0, (s - 1) * HALO:(s + 1) * HALO, lanes] = jnp.concatenate([pending, act], axis=0).astype(BF)

    tile, before, _, vec = _pair_specs(2)
    out_tile, _, _, _ = _pair_specs(1)
    return pl.pallas_call(
        body, name="conv_fwd", grid=(4, N_TOK_TILES), in_specs=[tile, before, vec(3), vec(1)],
        out_specs=out_tile, out_shape=jax.ShapeDtypeStruct((1, 4, SEQ, FF_BLK), BF),
        compiler_params=_params("parallel", "parallel"),
    )(*map(_in_hbm, (u, u, w_conv, b_conv)))


def _conv_bwd(u, da, w_conv, b_conv):
    def body(u_ref, ub_ref, ua_ref, da_ref, daa_ref, w_ref, b_ref, du_ref, dw_ref, db_ref):
        i = pl.program_id(1)

        @pl.when(i == 0)
        def _():
            dw_ref[...] = jnp.zeros_like(dw_ref)
            db_ref[...] = jnp.zeros_like(db_ref)

        for lo, width in LANE_TILES:
            lanes = slice(lo, lo + width)
            row = lax.broadcasted_iota(jnp.int32, (HALO, width), 0)
            taps = [_taps(w_ref, b_ref, half, lanes, HALO) for half in range(2)]
            acc_w = [[jnp.zeros((HALO, width), F32) for _ in range(3)] for _ in range(2)]
            acc_b = [jnp.zeros((HALO, width), F32) for _ in range(2)]
            da_pair, pending = None, [None, None]
            dc_prev, up_prev = [None, None], [None, None]
            for s, u3, (cg, cv) in _conv_strips(u_ref, ub_ref, ua_ref, taps, lanes, width, N_STRIPS):
                act, dact = _silu_parts(cg)
                if s == N_STRIPS:
                    da = jnp.where(i < N_TOK_TILES - 1, daa_ref[0, :, lanes].astype(F32), 0.0)
                elif s % 2 == 0:
                    da_pair = da_ref[0, s * HALO:(s + 2) * HALO, lanes].astype(F32)
                    da = da_pair[:HALO]
                else:
                    da = da_pair[HALO:]
                dc = (da * cv * dact, da * act)
                for half in range(2):
                    up = [pltpu.roll(dc[half], HALO - k, 0) for k in (1, 2)]
                    if s < N_STRIPS:
                        for j in range(3):
                            acc_w[half][j] = acc_w[half][j] + dc[half] * u3[half][j]
                        acc_b[half] = acc_b[half] + dc[half]
                    if s >= 1:
                        w3 = taps[half][0]
                        du = (dc_prev[half] * w3[2] + jnp.where(row < HALO - 1, up_prev[half][0], up[0]) * w3[1]
                              + jnp.where(row < HALO - 2, up_prev[half][1], up[1]) * w3[0])
                        if (s - 1) % 2 == 0:
                            pending[half] = du
                        else:
                            du_ref[half, (s - 2) * HALO:s * HALO, lanes] = jnp.concatenate([pending[half], du],
                                                                                           axis=0).astype(BF)
                    dc_prev[half], up_prev[half] = dc[half], up
            for half in range(2):
                for j in range(3):
                    dw_ref[half, j:j + 1, lanes] += jnp.sum(acc_w[half][j], axis=0, keepdims=True)
                db_ref[half, :, lanes] += jnp.sum(acc_b[half], axis=0, keepdims=True)

    tile, before, after, vec = _pair_specs(2)
    da_tile, _, da_after_spec, _ = _pair_specs(1)
    return pl.pallas_call(
        body, name="conv_bwd", grid=(4, N_TOK_TILES),
        in_specs=[tile, before, after, da_tile, da_after_spec, vec(3), vec(1)],
        out_specs=[tile, vec(3), vec(1)],
        out_shape=[jax.ShapeDtypeStruct((2, 4, SEQ, FF_BLK), BF), jax.ShapeDtypeStruct((2, 4, 3, FF_BLK), F32),
                   jax.ShapeDtypeStruct((2, 4, 1, FF_BLK), F32)],
        compiler_params=_params("parallel", "arbitrary"),
    )(*map(_in_hbm, (u, u, u, da, da, w_conv, b_conv)))


W_IN_SEGMENTS = ((R_POOL, POOL_WIDTH, "cat", C_POOL), (R_QKV, QKV_W, "cat", C_QKV), (R_OG, D_MODEL, "cat", C_OG),
                 (R_GK, GATE_RANK, "gk", 0), (R_GATE, GATE_W, "cat", C_GATE))


def _slab_pieces(d):
    lo, hi = d * IN_SHARD, (d + 1) * IN_SHARD
    pieces = []
    for start, n, dest, at in W_IN_SEGMENTS:
        a, b = max(lo, start), min(hi, start + n)
        if a < b:
            assert (a - lo) % 2 == 0 and (b - a) % 2 == 0 and (at + a - start) % 2 == 0
            pieces.append(((a - lo) // 2, (b - a) // 2, dest, (at + a - start) // 2))
    return pieces


def _unshard_w_in(slabs):
    def body(slab_ref, cat_ref, gk_ref):
        d = pl.program_id(0)
        src = slab_ref.bitcast(jnp.uint32)
        dst = dict(cat=cat_ref.bitcast(jnp.uint32), gk=gk_ref.bitcast(jnp.uint32))

        @pl.when(d == 0)
        def _():
            gk_ref[...] = jnp.zeros_like(gk_ref)

        for dd in range(N_DEV):
            @pl.when(d == dd)
            def _():
                for a, n, dest, at in _slab_pieces(dd):
                    dst[dest][pl.ds(at, n), :] = src[0, pl.ds(a, n), :]

    return pl.pallas_call(
        body, name="unshard_w_in", grid=(N_DEV,),
        in_specs=[pl.BlockSpec((1, IN_SHARD, D_MODEL), lambda d: (d, 0, 0))],
        out_specs=[_const_spec((N_CAT, D_MODEL)), _const_spec((GK_PAD, D_MODEL))],
        out_shape=[jax.ShapeDtypeStruct((N_CAT, D_MODEL), BF), jax.ShapeDtypeStruct((GK_PAD, D_MODEL), BF)],
        compiler_params=_params("arbitrary"),
    )(_in_hbm(slabs))


def _shard_d_w_in(d_cat, d_gk):
    def body(cat_ref, gk_ref, slab_ref):
        d = pl.program_id(0)
        cat = cat_ref.bitcast(jnp.uint32)
        gk = pltpu.bitcast(gk_ref[0:GATE_RANK, :].astype(BF), jnp.uint32)
        dst = slab_ref.bitcast(jnp.uint32)
        for dd in range(N_DEV):
            @pl.when(d == dd)
            def _():
                for a, n, source, at in _slab_pieces(dd):
                    dst[0, pl.ds(a, n), :] = gk[at:at + n] if source == "gk" else cat[pl.ds(at, n), :]

    return pl.pallas_call(
        body, name="shard_d_w_in", grid=(N_DEV,),
        in_specs=[_const_spec((N_CAT, D_MODEL)), _const_spec((GK_PAD, D_MODEL))],
        out_specs=pl.BlockSpec((1, IN_SHARD, D_MODEL), lambda d: (d, 0, 0)),
        out_shape=jax.ShapeDtypeStruct((N_DEV, IN_SHARD, D_MODEL), BF), compiler_params=_params("parallel"),
    )(_in_hbm(d_cat), _in_hbm(d_gk))


ANY = pl.BlockSpec(memory_space=pl.ANY)


def _place():
    x, y, c = lax.axis_index("x"), lax.axis_index("y"), lax.axis_index("c")
    other_chips = [(1 - x, y), (x, 1 - y), (1 - x, 1 - y)]
    return x, y, c, other_chips


def _all_gather(shards, name):
    n = len(shards)

    def body(*refs):
        src, out = refs[:n], refs[n:2 * n]
        send_sems, recv_sems, local_sems = refs[2 * n:]
        x, y, c, chips = _place()
        me, sibling = (x, y, c), (x, y, 1 - c)

        def copy(a, k, block, to, own=False):
            dst = out[a].at[4 * block[0] + 2 * block[1] + block[2]]
            return pltpu.make_async_remote_copy(src_ref=src[a] if own else dst, dst_ref=dst, send_sem=send_sems.at[a, k],
                                                recv_sem=recv_sems.at[a, k], device_id=to, device_id_type=MESH)

        mine = [pltpu.make_async_copy(src[a], out[a].at[4 * x + 2 * y + c], local_sems.at[a]) for a in range(n)]
        first = []
        for a in range(n):
            mine[a].start()
            first.append(copy(a, 0, me, sibling, own=True))
            first += [copy(a, 1 + j, me, (*chip, c), own=True) for j, chip in enumerate(chips)]
        for cp in first:
            cp.start()
        passed = []
        for j, chip in enumerate(chips):
            for a in range(n):
                copy(a, 1 + j, (*chip, c), me).wait_recv()
                passed.append(copy(a, 4 + j, (*chip, c), sibling))
                passed[-1].start()
        for a in range(n):
            copy(a, 0, sibling, me).wait_recv()
            for j, chip in enumerate(chips):
                copy(a, 4 + j, (*chip, 1 - c), me).wait_recv()
        for cp in first + passed:
            cp.wait_send()
        for cp in mine:
            cp.wait()

    return pl.pallas_call(
        body, name=name, in_specs=[ANY] * n, out_specs=[ANY] * n,
        out_shape=[jax.ShapeDtypeStruct((N_DEV,) + s.shape, s.dtype) for s in shards],
        scratch_shapes=[pltpu.SemaphoreType.DMA((n, 7)), pltpu.SemaphoreType.DMA((n, 7)), pltpu.SemaphoreType.DMA((n,))],
    )(*map(_in_hbm, shards))


SEM = pl.BlockSpec(memory_space=pltpu.SEMAPHORE)
IN_HBM = pl.BlockSpec(memory_space=pltpu.HBM)
SPLIT_PARAMS = pltpu.CompilerParams(has_side_effects=pltpu.SideEffectType.DATAFLOW_SIDE_EFFECTING)


def _gather_first(refs, send_sems, recv_sems):
    x, y, c, chips = _place()
    targets = [(x, y, 1 - c)] + [(px, py, c) for px, py in chips]
    return [pltpu.make_async_remote_copy(src_ref=refs[2 * a], dst_ref=refs[2 * a + 1].at[4 * x + 2 * y + c],
                                         send_sem=send_sems.at[4 * a + k], recv_sem=recv_sems.at[4 * a + k],
                                         device_id=to, device_id_type=MESH)
            for a in range(len(refs) // 2) for k, to in enumerate(targets)]


def _gather_second(refs, send_sems, recv_sems):
    x, y, c, chips = _place()
    copies = []
    for a, land in enumerate(refs):
        for j, (px, py) in enumerate(chips):
            block = land.at[4 * px + 2 * py + c]
            copies.append(pltpu.make_async_remote_copy(src_ref=block, dst_ref=block, send_sem=send_sems.at[3 * a + j],
                                                       recv_sem=recv_sems.at[3 * a + j], device_id=(x, y, 1 - c),
                                                       device_id_type=MESH))
    return copies


def _reduce_first(refs, send_sems, recv_sems):
    x, y, c, _ = _place()
    return [pltpu.make_async_remote_copy(src_ref=refs[2 * a].at[j, 1 - c], dst_ref=refs[2 * a + 1].at[j],
                                         send_sem=send_sems.at[4 * a + j], recv_sem=recv_sems.at[4 * a + j],
                                         device_id=(x, y, 1 - c), device_id_type=MESH)
            for a in range(len(refs) // 2) for j in range(4)]


def _reduce_second(refs, send_sems, recv_sems):
    _, _, c, chips = _place()
    return [pltpu.make_async_remote_copy(src_ref=refs[2 * a].at[2 * px + py], dst_ref=refs[2 * a + 1].at[k],
                                         send_sem=send_sems.at[3 * a + k], recv_sem=recv_sems.at[3 * a + k],
                                         device_id=(px, py, c), device_id_type=MESH)
            for a in range(len(refs) // 2) for k, (px, py) in enumerate(chips)]


def _split_start(name, groups):
    arrays = [a for g in groups for a in g[0]]
    n = len(arrays)

    def body(*refs):
        sems = refs[n:n + 2 * len(groups)]
        at = 0
        for gi, (members, _, build) in enumerate(groups):
            for cp in build(refs[at:at + len(members)], sems[2 * gi], sems[2 * gi + 1]):
                cp.start()
            at += len(members)
        refs[-1][...] = jnp.zeros_like(refs[-1])

    sem_shapes = [pltpu.SemaphoreType.DMA((g[1],)) for g in groups for _ in range(2)]
    outs = pl.pallas_call(
        body, name=name, in_specs=[IN_HBM] * n,
        out_shape=(*sem_shapes, *[pltpu.HBM(a.shape, a.dtype) for a in arrays], jax.ShapeDtypeStruct((8, 128), F32)),
        out_specs=(*[SEM] * len(sem_shapes), *[IN_HBM] * n, pl.BlockSpec(memory_space=pltpu.VMEM)),
        input_output_aliases={i: len(sem_shapes) + i for i in range(n)}, compiler_params=SPLIT_PARAMS,
    )(*[pltpu.with_memory_space_constraint(a, pltpu.HBM) for a in arrays])
    per_group, at = [], len(sem_shapes)
    for gi, (members, _, _) in enumerate(groups):
        per_group.append((outs[2 * gi], outs[2 * gi + 1], list(outs[at:at + len(members)])))
        at += len(members)
    return per_group, outs[-1]


def _split_wait(name, started, build, after):
    send_sems, recv_sems, arrays = started
    n = len(arrays)

    def body(*refs):
        for cp in build(refs[:n], refs[n], refs[n + 1]):
            cp.wait_send()
            cp.wait_recv()

    return pl.pallas_call(
        body, name=name, in_specs=[IN_HBM] * n + [SEM, SEM, ANY],
        out_shape=tuple(pltpu.HBM(a.shape, a.dtype) for a in arrays), out_specs=tuple([IN_HBM] * n),
        input_output_aliases={i: i for i in range(n)}, compiler_params=SPLIT_PARAMS,
    )(*arrays, send_sems, recv_sems, after)


def _gather_landing(shard, me):
    return lax.dynamic_update_slice(lax.empty((N_DEV,) + shard.shape, shard.dtype), shard[None],
                                    (me,) + (0,) * shard.ndim)


def _tile_2d(rows, cols):
    for t in (256, 176, 128):
        if rows % t == 0:
            return t, cols
    return rows, 256


def _pair_sum(part, recv, core, name):
    _, rows, cols = recv.shape
    tr, tc = rows, cols

    def body(c_ref, p_ref, r_ref, o_ref):
        del c_ref
        o_ref[...] = (p_ref[...].astype(F32) + r_ref[...].astype(F32)).astype(BF)

    grid_spec = pltpu.PrefetchScalarGridSpec(
        num_scalar_prefetch=1, grid=(4, rows // tr, cols // tc),
        in_specs=[pl.BlockSpec((None, None, tr, tc), lambda j, i, k, c_ref: (j, c_ref[0], i, k)),
                  pl.BlockSpec((None, tr, tc), lambda j, i, k, c_ref: (j, i, k))],
        out_specs=pl.BlockSpec((None, tr, tc), lambda j, i, k, c_ref: (j, i, k)))
    return pl.pallas_call(
        body, name=name, grid_spec=grid_spec, out_shape=jax.ShapeDtypeStruct(recv.shape, BF),
        compiler_params=_params("parallel", "parallel", "parallel"),
    )(core, *map(_in_hbm, (part, recv)))


def _adamw(w, g, m, v):
    m = ADAM_B1 * m + (1.0 - ADAM_B1) * g
    v = ADAM_B2 * v + (1.0 - ADAM_B2) * (g * g)
    delta = -ADAM_LR * ((m / ADAM_C1) / (jnp.sqrt(v / ADAM_C2) + ADAM_EPS) + ADAM_WD * w)
    return delta, m, v


def _chip_sum_adamw(sums, recv, w, m, v, chip, name):
    rows, cols = w.shape
    tr, tc = _tile_2d(rows, cols)

    def body(chip_ref, s_ref, r_ref, w_ref, m_ref, v_ref, g_out, d_out, m_out, v_out):
        del chip_ref
        g = s_ref[...].astype(F32)
        for k in range(3):
            g = g + r_ref[k].astype(F32)
        g_out[...] = g
        d_out[...], m_out[...], v_out[...] = _adamw(w_ref[...], g, m_ref[...], v_ref[...])

    tile = pl.BlockSpec((tr, tc), lambda i, k, chip_ref: (i, k))
    grid_spec = pltpu.PrefetchScalarGridSpec(
        num_scalar_prefetch=1, grid=(rows // tr, cols // tc),
        in_specs=[pl.BlockSpec((None, tr, tc), lambda i, k, chip_ref: (chip_ref[0], i, k)),
                  pl.BlockSpec((3, tr, tc), lambda i, k, chip_ref: (0, i, k)), tile, tile, tile],
        out_specs=[tile] * 4)
    return pl.pallas_call(
        body, name=name, grid_spec=grid_spec, out_shape=[jax.ShapeDtypeStruct((rows, cols), F32)] * 4,
        compiler_params=_params("parallel", "parallel"),
    )(chip, *map(_in_hbm, (sums, recv, w, m, v)))


def _small_sum_adamw(me, entries, loss_parts):
    def whole(shape, squeeze=0, pick=False):
        blk = (None,) * squeeze + tuple(shape[squeeze:])
        if pick:
            blk = (shape[0], None) + tuple(shape[2:])
            return pl.BlockSpec(blk, lambda i, me_ref: (0, me_ref[0]) + (0,) * (len(shape) - 2))
        return pl.BlockSpec(blk, lambda i, me_ref: (0,) * len(shape))

    in_specs, out_specs, out_shape, args = [], [], [], []
    for parts, w, m, v, sharded in entries:
        lead = w.ndim - (parts.ndim - (2 if sharded else 1))
        in_specs += [whole(parts.shape, pick=sharded)] + [whole(w.shape, squeeze=lead)] * 3
        out_specs += [whole(w.shape, squeeze=lead)] * 4
        out_shape += [jax.ShapeDtypeStruct(w.shape, F32)] * 4
        args += [parts, w, m, v]
    in_specs.append(whole(loss_parts.shape))
    out_specs.append(whole(loss_parts.shape[1:]))
    out_shape.append(jax.ShapeDtypeStruct(loss_parts.shape[1:], F32))
    n = len(entries)

    def added(p_ref):
        total = p_ref[0]
        for d in range(1, N_DEV):
            total = total + p_ref[d]
        return total

    def body(me_ref, *refs):
        del me_ref
        ins, outs = refs[:4 * n + 1], refs[4 * n + 1:]
        for e in range(n):
            p_ref, w_ref, m_ref, v_ref = ins[4 * e:4 * e + 4]
            g_out, d_out, m_out, v_out = outs[4 * e:4 * e + 4]
            g = added(p_ref)
            g_out[...] = g
            d_out[...], m_out[...], v_out[...] = _adamw(w_ref[...], g, m_ref[...], v_ref[...])
        outs[4 * n][...] = added(ins[4 * n])

    grid_spec = pltpu.PrefetchScalarGridSpec(num_scalar_prefetch=1, grid=(1,), in_specs=in_specs, out_specs=out_specs)
    outs = pl.pallas_call(body, name="small_sum_adamw", grid_spec=grid_spec, out_shape=out_shape,
                          compiler_params=_params("arbitrary"))(me, *map(_in_hbm, args + [loss_parts]))
    return [outs[4 * e:4 * e + 4] for e in range(n)], outs[4 * n]


MM_TILE = 512
N_MM_TILES = SEQ // MM_TILE
CAT_TILE = 512
N_CAT_TILES = N_CAT // CAT_TILE
SMALL_ROWS = 808
SHARD_ROWS = 32


def kernel(x, g_mix, w_in, b_gate, w_gk_up, b_gk, w_pool_grp, pool_scale, g_gla_head, w_pool_proj, w_gla_proj, w_out, g_ffn, w_up, w_conv, b_conv, w_down, g_final, loss_target, m_g_mix, m_w_in, m_b_gate, m_w_gk_up, m_b_gk, m_w_pool_grp, m_pool_scale, m_g_gla_head, m_w_pool_proj, m_w_gla_proj, m_w_out, m_g_ffn, m_w_up, m_w_conv, m_b_conv, m_w_down, m_g_final, v_g_mix, v_w_in, v_b_gate, v_w_gk_up, v_b_gk, v_w_pool_grp, v_pool_scale, v_g_gla_head, v_w_pool_proj, v_w_gla_proj, v_w_out, v_g_ffn, v_w_up, v_w_conv, v_b_conv, v_w_down, v_g_final):
    xi, yi, ci = lax.axis_index("x"), lax.axis_index("y"), lax.axis_index("c")
    me = 4 * xi + 2 * yi + ci
    core = jnp.reshape(ci, (1,)).astype(jnp.int32)
    chip = jnp.reshape(2 * xi + yi, (1,)).astype(jnp.int32)
    xs, target = x[0], loss_target[0]

    big = dict(w_in=w_in[0].T, w_pool_proj=w_pool_proj[0], w_gla_proj=w_gla_proj[0], w_out=w_out[0], w_up=w_up[0].T,
               w_down=w_down[0])
    moments = dict(w_in=(m_w_in[0].T, v_w_in[0].T), w_pool_proj=(m_w_pool_proj[0], v_w_pool_proj[0]),
                   w_gla_proj=(m_w_gla_proj[0], v_w_gla_proj[0]), w_out=(m_w_out[0], v_w_out[0]),
                   w_up=(m_w_up[0].T, v_w_up[0].T), w_down=(m_w_down[0], v_w_down[0]))
    names = list(big)
    shards = {k: big[k].astype(BF) for k in names}
    shards["w_gk_up"], shards["w_conv"] = w_gk_up[0], w_conv[0]
    gather_groups = (("w_in", "w_gk_up"), ("w_pool_proj", "w_gla_proj", "w_out"), ("w_up", "w_down", "w_conv"))
    started, token = _split_start("gather_start", [
        ([t for k in g for t in (shards[k], _gather_landing(shards[k], me))], 4 * len(g), _gather_first)
        for g in gather_groups])

    def gather_pass(gi, after):
        lands = list(_split_wait(f"gather_wait_{gi}", started[gi], _gather_first, after)[1::2])
        passed, tkn = _split_start(f"gather_pass_{gi}", [(lands, 3 * len(lands), _gather_second)])
        return passed[0], tkn

    def gather_done(gi, passed, after):
        return dict(zip(gather_groups[gi], _split_wait(f"gather_pass_wait_{gi}", passed, _gather_second, after)))

    tok = lambda i, j, k: (i, 0)
    whole = lambda i, j, k: (0, 0)
    kblk = lambda i, j, k: (k, 0)
    ff_tile = (None, None, MM_TILE, FF_BLK)
    ff_seq = (None, None, SEQ, FF_BLK)

    h = _rms_fwd(xs, g_mix + token[:1, :1], "rms_mix")
    wg = gather_done(0, gather_pass(0, h)[0], h)
    wt_cat, wt_gk = _unshard_w_in(wg["w_in"])
    wgk_pad = jnp.pad(wg["w_gk_up"].transpose(1, 0, 2).reshape(GATE_RANK, GLA_DK), ((0, GK_PAD - GATE_RANK), (0, 0)))
    zcat = _mm(h, wt_cat, out_shape=(SEQ, N_CAT), out_dtype=F32, grid=(N_CAT_TILES, 1, 1),
               blk_a=(SEQ, D_MODEL), blk_b=(CAT_TILE, D_MODEL), blk_o=(SEQ, CAT_TILE),
               map_a=whole, map_b=lambda j, i, k: (j, 0), map_o=lambda j, i, k: (0, j), tb=True, name="mm_in")
    la = _gk_fwd(h, wt_gk, wgk_pad, b_gk)
    passed, tkn = gather_pass(1, la)
    o, states = _gla_fwd(zcat, la, tkn)
    wg = gather_done(1, passed, o)
    wpp = wg["w_pool_proj"].transpose(1, 0, 2).reshape(POOL_WIDTH, D_MODEL)
    wgp = wg["w_gla_proj"].reshape(D_MODEL, D_MODEL)
    wout = wg["w_out"].reshape(D_MODEL, D_MODEL)
    og = _post_gla_fwd(o, zcat, g_gla_head)
    ps = _pool_fwd(zcat, w_pool_grp[0], pool_scale)
    y_pool = _mm(ps, wpp, out_shape=(SEQ, D_MODEL), out_dtype=F32, grid=(N_MM_TILES, 1, 1),
                 blk_a=(MM_TILE, POOL_WIDTH), blk_b=(POOL_WIDTH, D_MODEL), blk_o=(MM_TILE, D_MODEL),
                 map_a=tok, map_b=whole, map_o=tok, name="mm_pool_proj")
    sq = dict(out_shape=(SEQ, D_MODEL), grid=(N_MM_TILES, 1, 1), blk_a=(MM_TILE, D_MODEL), blk_b=(D_MODEL, D_MODEL),
              blk_o=(MM_TILE, D_MODEL), map_a=tok, map_b=whole, map_o=tok)
    y_gla = _mm(og, wgp, out_dtype=F32, name="mm_gla_proj", **sq)
    passed, tkn = gather_pass(2, y_gla)
    mixed = _mix_fwd(zcat, b_gate + tkn[:1, :1], y_pool, y_gla)
    x1 = _mm(mixed, wout, out_dtype=F32, res=xs, name="mm_out", **sq)
    h2 = _rms_fwd(x1, g_ffn, "rms_ffn")
    wg = gather_done(2, passed, h2)
    wt_up = wg["w_up"].reshape(2 * D_FF, D_MODEL)
    wdown = wg["w_down"].reshape(D_FF, D_MODEL)
    wconv4 = wg["w_conv"].reshape(2, 4, 3, FF_BLK)
    bconv4 = b_conv.reshape(2, 4, 1, FF_BLK)
    blk4 = lambda b, i, k: (b // 4, b % 4, 0, 0)
    u4 = _mm(h2, wt_up, out_shape=(2, 4, SEQ, FF_BLK), out_dtype=F32, grid=(N_DEV, 1, 1),
             blk_a=(SEQ, D_MODEL), blk_b=(FF_BLK, D_MODEL), blk_o=ff_seq,
             map_a=whole, map_b=lambda b, i, k: (b, 0), map_o=blk4, tb=True, name="mm_up")
    act = _conv_fwd(u4, wconv4, bconv4)
    loss_part, dx2, dx2_bf, dg_final = _mm_tokens(
        act, wdown, blk_a=(None, 4, TOK_MM_TILE, FF_BLK), map_a=lambda i: (0, 0, i, 0),
        pieces=[(b, b * FF_BLK, FF_BLK) for b in range(4)], res=x1, then=("loss", g_final.reshape(1, D_MODEL), target),
        name="mm_down_loss")

    da = _mm(dx2_bf, wdown, out_shape=(1, 4, SEQ, FF_BLK), out_dtype=BF, grid=(4, 1, 1),
             blk_a=(SEQ, D_MODEL), blk_b=(FF_BLK, D_MODEL), blk_o=ff_seq,
             map_a=whole, map_b=lambda b, i, k: (b, 0), map_o=lambda b, i, k: (0, b, 0, 0), tb=True, name="mm_d_act")
    d_wdown = _mm(act, dx2_bf, out_shape=(D_FF, D_MODEL), out_dtype=BF, grid=(4, 1, 1),
                  blk_a=ff_seq, blk_b=(SEQ, D_MODEL), blk_o=(FF_BLK, D_MODEL),
                  map_a=lambda b, i, k: (0, b, 0, 0), map_b=whole, map_o=lambda b, i, k: (b, 0), ta=True,
                  name="mm_d_wdown")
    du4, d_wconv, d_bconv = _conv_bwd(u4, da, wconv4, bconv4)
    d_wt_up = _mm(du4, h2, out_shape=(2 * D_FF, D_MODEL), out_dtype=BF, grid=(N_DEV, 1, 1),
                  blk_a=ff_seq, blk_b=(SEQ, D_MODEL), blk_o=(FF_BLK, D_MODEL),
                  map_a=blk4, map_b=whole, map_o=lambda b, i, k: (b, 0), ta=True, name="mm_d_wup")
    res = {}

    def reduce_start(keys, parts):
        arrays = [t for k in keys for t in (parts[k], lax.empty((4,) + parts[k].shape[2:], BF))]
        st, tkn = _split_start("reduce_start_" + keys[0], [(arrays, 4 * len(keys), _reduce_first)])
        return st[0], tkn

    def reduce_cross(keys, st, after):
        arrays = _split_wait("reduce_wait_" + keys[0], st, _reduce_first, after)
        sums = [_pair_sum(p, r, core, "pair_sum_" + k) for k, p, r in zip(keys, arrays[0::2], arrays[1::2])]
        arrays = [t for s in sums for t in (s, lax.empty((3,) + s.shape[1:], BF))]
        st2, tkn = _split_start("reduce_cross_" + keys[0], [(arrays, 3 * len(keys), _reduce_second)])
        return st2[0], tkn

    def reduce_done(keys, st2, after):
        arrays = _split_wait("reduce_cross_wait_" + keys[0], st2, _reduce_second, after)
        for k, s, r in zip(keys, arrays[0::2], arrays[1::2]):
            outs = _chip_sum_adamw(s, r, big[k], moments[k][0], moments[k][1], chip, "adamw_" + k)
            res[k] = [(t.T if k in ("w_in", "w_up") else t)[None] for t in outs]

    ffn_keys = ("w_down", "w_up")
    ffn_red, tkn = reduce_start(ffn_keys, dict(w_down=d_wdown.reshape(4, 2, D_FF // N_DEV, D_MODEL),
                                               w_up=d_wt_up.reshape(4, 2, FF_BLK, D_MODEL)))
    dx1, dg_ffn = _mm_tokens(
        du4, wt_up, blk_a=(2, 4, TOK_MM_TILE, FF_BLK), map_a=lambda i: (0, 0, i, 0),
        pieces=[((b // 4, b % 4), b * FF_BLK, FF_BLK) for b in range(N_DEV)], after=tkn, then=("rms_bwd", x1, g_ffn, dx2),
        name="mm_d_h2_rms")

    sq_t = dict(out_shape=(D_MODEL, D_MODEL), grid=(1, 1, N_MM_TILES), blk_a=(MM_TILE, D_MODEL),
                blk_b=(MM_TILE, D_MODEL), blk_o=(D_MODEL, D_MODEL), map_a=kblk, map_b=kblk, map_o=whole, ta=True)
    dmixed = _mm(dx1, wout, out_dtype=F32, tb=True, name="mm_d_mixed", **sq)
    d_wout = _mm(mixed, dx1, out_dtype=BF, name="mm_d_wout", **sq_t)
    dzcat, dy_pool, dy_gla, db_gate = _mix_bwd(dmixed, zcat, b_gate, y_pool, y_gla)
    ffn_red, _ = reduce_cross(ffn_keys, ffn_red, db_gate)
    d_og =_mm(dy_gla, wgp, out_dtype=F32, tb=True, name="mm_d_og", **sq)
    d_wgp = _mm(og, dy_gla, out_dtype=BF, name="mm_d_wgp", **sq_t)
    mix_keys = ("w_out", "w_gla_proj")
    mix_red, tkn = reduce_start(mix_keys, dict(w_out=d_wout.reshape(4, 2, D_MODEL // N_DEV, D_MODEL),
                                               w_gla_proj=d_wgp.reshape(4, 2, D_MODEL // N_DEV, D_MODEL)))
    dzcat, d_o, dg_head = _post_gla_bwd(dzcat, d_og, o, zcat, g_gla_head + tkn[:1, :1])
    dzcat, dla = _gla_bwd(dzcat, zcat, la, d_o, states)
    mix_red, tkn = reduce_cross(mix_keys, mix_red, dla)
    dh_gk, d_wt_gk, d_wgk, db_gk = _gk_bwd(dla, h, wt_gk, wgk_pad, b_gk + tkn[:1, :1])
    dps = _mm(dy_pool, wpp, out_shape=(SEQ, POOL_WIDTH), out_dtype=F32, grid=(N_MM_TILES, 1, 1),
              blk_a=(MM_TILE, D_MODEL), blk_b=(POOL_WIDTH, D_MODEL), blk_o=(MM_TILE, POOL_WIDTH),
              map_a=tok, map_b=whole, map_o=tok, tb=True, name="mm_d_ps")
    d_wpp = _mm(ps, dy_pool, out_shape=(POOL_WIDTH, D_MODEL), out_dtype=F32, grid=(1, 1, N_MM_TILES),
                blk_a=(MM_TILE, POOL_WIDTH), blk_b=(MM_TILE, D_MODEL), blk_o=(POOL_WIDTH, D_MODEL),
                map_a=kblk, map_b=kblk, map_o=whole, ta=True, name="mm_d_wpp")
    dzcat, d_wgrp, d_scale = _pool_bwd(dzcat, zcat, dps, w_pool_grp[0], pool_scale)
    d_wt_cat = _mm(dzcat, h, out_shape=(N_CAT, D_MODEL), out_dtype=BF, grid=(N_CAT_TILES, 1, 1),
                   blk_a=(SEQ, CAT_TILE), blk_b=(SEQ, D_MODEL), blk_o=(CAT_TILE, D_MODEL),
                   map_a=lambda j, i, k: (0, j), map_b=whole, map_o=lambda j, i, k: (j, 0), ta=True, name="mm_d_wcat")
    in_keys = ("w_in", "w_pool_proj")
    in_red, tkn = reduce_start(in_keys, dict(
        w_in=_shard_d_w_in(d_wt_cat, d_wt_gk).reshape(4, 2, IN_SHARD, D_MODEL),
        w_pool_proj=d_wpp.reshape(POOL_WIDTH, N_DEV, D_MODEL // N_DEV).transpose(1, 0, 2).astype(BF)
        .reshape(4, 2, POOL_WIDTH, D_MODEL // N_DEV)))
    dh = _mm_tokens(dzcat, wt_cat, blk_a=(TOK_MM_TILE, N_CAT), map_a=lambda i: (i, 0), pieces=[(None, 0, N_CAT)],
                    res=dh_gk, after=tkn, name="mm_d_h")
    in_red, tkn = reduce_cross(in_keys, in_red, dh)
    grad_x, dg_mix = _rms_bwd(dh, xs, g_mix + tkn[:1, :1], dx1, "rms_mix_bwd")
    reduce_done(ffn_keys, ffn_red, grad_x)
    reduce_done(mix_keys, mix_red, res["w_down"][0])

    row = lambda t: t.reshape(1, D_MODEL)
    conv_vec = lambda t: t.reshape(2, 4, 1, FF_BLK)
    small = [("g_mix", dg_mix, g_mix, m_g_mix, v_g_mix, False), ("b_gate", db_gate, b_gate, m_b_gate, v_b_gate, False),
             ("w_gk_up", d_wgk.reshape(GATE_RANK, N_DEV, GLA_DK // N_DEV).transpose(1, 0, 2), w_gk_up, m_w_gk_up,
              v_w_gk_up, True),
             ("b_gk", db_gk, b_gk, m_b_gk, v_b_gk, False),
             ("w_pool_grp", d_wgrp, w_pool_grp, m_w_pool_grp, v_w_pool_grp, False),
             ("pool_scale", d_scale, pool_scale, m_pool_scale, v_pool_scale, False),
             ("g_gla_head", dg_head, g_gla_head, m_g_gla_head, v_g_gla_head, False),
             ("g_ffn", dg_ffn, g_ffn, m_g_ffn, v_g_ffn, False),
             ("w_conv", d_wconv.reshape(N_DEV, 3, FF_BLK), w_conv, m_w_conv, v_w_conv, True),
             ("b_conv", d_bconv, conv_vec(b_conv), conv_vec(m_b_conv), conv_vec(v_b_conv), False),
             ("g_final", dg_final, row(g_final), row(m_g_final), row(v_g_final), False)]
    gathered = _all_gather([t[1] for t in small] + [loss_part], "gather_small_grads")
    small_out, loss_sum = _small_sum_adamw(jnp.reshape(me, (1,)).astype(jnp.int32),
                                           [(p,) + t[2:] for p, t in zip(gathered, small)], gathered[-1])
    for t, outs in zip(small, small_out):
        res[t[0]] = list(outs)
    res["b_conv"] = [t.reshape(b_conv.shape) for t in res["b_conv"]]
    res["g_final"] = [t.reshape(g_final.shape) for t in res["g_final"]]

    reduce_done(in_keys, in_red, loss_sum)
    loss = loss_sum[0, 0]
    order =["g_mix", "w_in", "b_gate", "w_gk_up", "b_gk", "w_pool_grp", "pool_scale", "g_gla_head", "w_pool_proj",
             "w_gla_proj", "w_out", "g_ffn", "w_up", "w_conv", "b_conv", "w_down", "g_final"]
    return (loss, grad_x[None], *[res[k][0] for k in order], *[res[k][1] for k in order],
            *[res[k][2] for k in order], *[res[k][3] for k in order])
```

```python
import functools

import jax
import jax.numpy as jnp
from jax import lax
from jax.experimental import pallas as pl
from jax.experimental.pallas import tpu as pltpu

F32 = jnp.float32
BF = jnp.bfloat16
HIGHEST = lax.Precision.HIGHEST
MESH = pl.DeviceIdType.MESH

N_DEV = 8
SEQ = 2048
D_MODEL = 1024
CHUNK = 64
EPS = 1e-6
POOL_WIDTH = 512
POOL_WINDOWS = (2, 4, 8, 16)
POOL_GD = 128
POOL_HALO = 16
HEADS = 4
HK = 128
HV = 256
GLA_DK = 512
GATE_RANK = 16
GATE_NORM = 16.0
D_FF = 2816
FF_BLK = 704
IN_TOTAL = 5648
IN_SHARD = 706
C_QKV, C_GATE, C_OG, C_POOL = 0, 2048, 4096, 5120
N_CAT = 5632
R_POOL, R_QKV, R_OG, R_GK, R_GATE = 0, 512, 2560, 3584, 3600
GK_PAD = 128

ADAM_LR, ADAM_B1, ADAM_B2, ADAM_EPS, ADAM_WD, ADAM_STEP = 0.001, 0.9, 0.999, 1e-08, 0.01, 10
ADAM_C1 = 1.0 - ADAM_B1 ** ADAM_STEP
ADAM_C2 = 1.0 - ADAM_B2 ** ADAM_STEP

VMEM_BYTES_V7X = 64 * 1024 * 1024
VMEM_LIMIT = 48 * 1024 * 1024

TOK_TILE = 256
HALO = 8
GLA_CPS = 4


def _params(*sem):
    return pltpu.CompilerParams(dimension_semantics=sem, vmem_limit_bytes=VMEM_LIMIT)


def _const_spec(shape):
    nd = len(shape)
    return pl.BlockSpec(shape, lambda *_: (0,) * nd)


def _in_hbm(t):
    return pltpu.with_memory_space_constraint(t, pltpu.HBM)


def _dot(a, b, ta=False, tb=False):
    dims = (((0 if ta else 1,), (1 if tb else 0,)), ((), ()))
    return lax.dot_general(a.astype(BF), b.astype(BF), dims, preferred_element_type=F32)


def _dot_exact(a, b):
    return jnp.dot(a, b, precision=HIGHEST, preferred_element_type=F32)


def _sigmoid(x):
    return 0.5 * jnp.tanh(0.5 * x) + 0.5


def _mm(a, b, *, out_shape, out_dtype, grid, blk_a, blk_b, blk_o, map_a, map_b, map_o, ta=False, tb=False,
        res=None, name):
    gk = grid[2]

    def body(*refs):
        if res is None:
            a_ref, b_ref, o_ref = refs[:3]
            r_ref = None
            scr = refs[3:]
        else:
            a_ref, b_ref, r_ref, o_ref = refs[:4]
            scr = refs[4:]
        prod = _dot(a_ref[...], b_ref[...], ta, tb)

        def finish(total):
            if r_ref is not None:
                total = total + r_ref[...]
            o_ref[...] = total.astype(out_dtype)

        if gk == 1:
            finish(prod)
        else:
            acc = scr[0]
            k = pl.program_id(2)

            @pl.when(k == 0)
            def _():
                acc[...] = prod

            @pl.when(k > 0)
            def _():
                acc[...] += prod

            @pl.when(k == gk - 1)
            def _():
                finish(acc[...])

    in_specs = [pl.BlockSpec(blk_a, map_a), pl.BlockSpec(blk_b, map_b)]
    args = [a, b]
    if res is not None:
        in_specs.append(pl.BlockSpec(blk_o, map_o))
        args.append(res)
    return pl.pallas_call(
        body, name=name, grid=grid, in_specs=in_specs, out_specs=pl.BlockSpec(blk_o, map_o),
        out_shape=jax.ShapeDtypeStruct(out_shape, out_dtype),
        scratch_shapes=[] if gk == 1 else [pltpu.VMEM(tuple(d for d in blk_o if d is not None), F32)],
        compiler_params=_params("parallel", "parallel", "arbitrary"),
    )(*[_in_hbm(t) for t in args])


TOK_MM_TILE = 256


def _mm_tokens(a, w, *, blk_a, map_a, pieces, res=None, after=None, then=None, name):
    n_in = 2 + (res is not None) + (after is not None) + (0 if then is None else len(then) - 1)

    def accumulate(ref, part):
        @pl.when(pl.program_id(0) == 0)
        def _():
            ref[...] = part

        @pl.when(pl.program_id(0) > 0)
        def _():
            ref[...] += part

    def body(*refs):
        a_ref, w_ref = refs[:2]
        extra, outs = refs[n_in - (0 if then is None else len(then) - 1):n_in], refs[n_in:]
        total = None
        for idx, row, n in pieces:
            av = a_ref[...] if idx is None else a_ref[idx]
            prod = _dot(av, w_ref[row:row + n, :])
            total = prod if total is None else total + prod
        if res is not None:
            total = total + refs[2][...]
        if then is None:
            outs[0][...] = total
        elif then[0] == "rms_bwd":
            dx, part = _rms_bwd_tile(total, extra[0][...], extra[1][...], extra[2][...])
            outs[0][...] = dx
            accumulate(outs[1], part)
        else:
            lpart, dx, part = _loss_tile(total, extra[0][...], extra[1][...])
            outs[1][...] = dx
            outs[2][...] = dx.astype(BF)
            accumulate(outs[0], lpart)
            accumulate(outs[3], part)

    tile = pl.BlockSpec((TOK_MM_TILE, D_MODEL), lambda i: (i, 0))
    vec = _const_spec((1, D_MODEL))
    big = jax.ShapeDtypeStruct((SEQ, D_MODEL), F32)
    small = jax.ShapeDtypeStruct((1, D_MODEL), F32)
    in_specs = [pl.BlockSpec(blk_a, map_a), pl.BlockSpec(w.shape, lambda i: (0, 0), pipeline_mode=pl.Buffered(1))]
    args = [a, w]
    if res is not None:
        in_specs.append(tile)
        args.append(res)
    if after is not None:
        in_specs.append(pl.BlockSpec(memory_space=pl.ANY))
        args.append(after)
    if then is None:
        out_specs, out_shape = tile, big
    elif then[0] == "rms_bwd":
        in_specs += [tile, vec, tile]
        out_specs, out_shape = [tile, vec], [big, small]
    else:
        in_specs += [vec, tile]
        out_specs = [_const_spec((1, 128)), tile, tile, vec]
        out_shape = [jax.ShapeDtypeStruct((1, 128), F32), big, jax.ShapeDtypeStruct((SEQ, D_MODEL), BF), small]
    if then is not None:
        args += list(then[1:])
    return pl.pallas_call(
        body, name=name, grid=(SEQ // TOK_MM_TILE,), in_specs=in_specs, out_specs=out_specs, out_shape=out_shape,
        compiler_params=_params("parallel" if then is None else "arbitrary"),
    )(*[_in_hbm(t) for t in args])


def _rms_fwd(x, g, name):
    def body(x_ref, g_ref, o_ref):
        xv = x_ref[...]
        r = lax.rsqrt(jnp.mean(xv * xv, axis=-1, keepdims=True) + EPS)
        o_ref[...] = (xv * r * g_ref[...]).astype(BF)

    tile = pl.BlockSpec((TOK_TILE, D_MODEL), lambda i: (i, 0))
    return pl.pallas_call(
        body, name=name, grid=(SEQ // TOK_TILE,), in_specs=[tile, _const_spec((1, D_MODEL))], out_specs=tile,
        out_shape=jax.ShapeDtypeStruct((SEQ, D_MODEL), BF), compiler_params=_params("parallel"),
    )(*map(_in_hbm, (x, g)))


def _rms_bwd_tile(dyv, xv, gv, dresv):
    r = lax.rsqrt(jnp.mean(xv * xv, axis=-1, keepdims=True) + EPS)
    xn = xv * r
    dxn = dyv * gv
    return dresv + r * (dxn - xn * jnp.mean(dxn * xn, axis=-1, keepdims=True)), jnp.sum(dyv * xn, axis=0, keepdims=True)


def _rms_bwd(dy, x, g, dres, name):
    def body(dy_ref, x_ref, g_ref, dres_ref, dx_ref, dg_ref):
        dx_ref[...], part = _rms_bwd_tile(dy_ref[...], x_ref[...], g_ref[...], dres_ref[...])

        @pl.when(pl.program_id(0) == 0)
        def _():
            dg_ref[...] = part

        @pl.when(pl.program_id(0) > 0)
        def _():
            dg_ref[...] += part

    tile = pl.BlockSpec((TOK_TILE, D_MODEL), lambda i: (i, 0))
    vec = _const_spec((1, D_MODEL))
    return pl.pallas_call(
        body, name=name, grid=(SEQ // TOK_TILE,), in_specs=[tile, tile, vec, tile], out_specs=[tile, vec],
        out_shape=[jax.ShapeDtypeStruct((SEQ, D_MODEL), F32), jax.ShapeDtypeStruct((1, D_MODEL), F32)],
        compiler_params=_params("arbitrary"),
    )(*map(_in_hbm, (dy, x, g, dres)))


def _loss_tile(xv, gv, tv):
    r = lax.rsqrt(jnp.mean(xv * xv, axis=-1, keepdims=True) + EPS)
    xn = xv * r
    err = xn * gv - tv
    lpart = jnp.full((1, 128), 0.5 * jnp.sum(jnp.mean(err * err, axis=-1, keepdims=True)), F32)
    dyv = err * (1.0 / D_MODEL)
    dxn = dyv * gv
    return lpart, r * (dxn - xn * jnp.mean(dxn * xn, axis=-1, keepdims=True)), jnp.sum(dyv * xn, axis=0, keepdims=True)


def _pool_counts(w):
    pos = lax.broadcasted_iota(jnp.int32, (SEQ, 1), 0).astype(F32)
    return jnp.minimum(pos + 1.0, float(w))


def _pool_window(u, w, ext):
    ext[pl.ds(POOL_HALO, SEQ), :] = u
    win = u
    for j in range(1, w):
        win = win + ext[pl.ds(POOL_HALO - j, SEQ), :]
    return win / _pool_counts(w) - u


def _pool_fwd(zcat, w_grp, scale):
    def body(z_ref, w_ref, s_ref, o_ref, ext):
        ext[pl.ds(0, POOL_HALO), :] = jnp.zeros((POOL_HALO, POOL_GD), F32)
        for g, w in enumerate(POOL_WINDOWS):
            cols = slice(g * POOL_GD, (g + 1) * POOL_GD)
            p = _pool_window(z_ref[:, cols], w, ext)
            o_ref[:, cols] = (_dot(p, w_ref[g]) * s_ref[:, cols]).astype(BF)

    return pl.pallas_call(
        body, name="pool_fwd", grid=(1,),
        in_specs=[pl.BlockSpec((SEQ, POOL_WIDTH), lambda i: (0, C_POOL // POOL_WIDTH)),
                  _const_spec((4, POOL_GD, POOL_GD)), _const_spec((1, POOL_WIDTH))],
        out_specs=_const_spec((SEQ, POOL_WIDTH)), out_shape=jax.ShapeDtypeStruct((SEQ, POOL_WIDTH), BF),
        scratch_shapes=[pltpu.VMEM((POOL_HALO + SEQ, POOL_GD), F32)], compiler_params=_params("arbitrary"),
    )(*map(_in_hbm, (zcat, w_grp, scale)))


def _pool_bwd(dzcat, zcat, dps, w_grp, scale):
    def body(dz_in, z_ref, dps_ref, w_ref, s_ref, dz_ref, dw_ref, dsc_ref, ext, ext2):
        del dz_in
        ext[pl.ds(0, POOL_HALO), :] = jnp.zeros((POOL_HALO, POOL_GD), F32)
        ext2[pl.ds(SEQ, POOL_HALO), :] = jnp.zeros((POOL_HALO, POOL_GD), F32)
        for g, w in enumerate(POOL_WINDOWS):
            cols = slice(g * POOL_GD, (g + 1) * POOL_GD)
            p = _pool_window(z_ref[:, cols], w, ext)
            wg = w_ref[g]
            pg = _dot(p, wg)
            dpsv = dps_ref[:, cols]
            dsc_ref[:, cols] = jnp.sum(dpsv * pg, axis=0, keepdims=True)
            dpg = dpsv * s_ref[:, cols]
            dw_ref[g] = _dot(p, dpg, ta=True)
            dp = _dot(dpg, wg, tb=True)
            dpc = dp / _pool_counts(w)
            ext2[pl.ds(0, SEQ), :] = dpc
            du = dpc
            for j in range(1, w):
                du = du + ext2[pl.ds(j, SEQ), :]
            dz_ref[:, cols] = (du - dp).astype(BF)

    return pl.pallas_call(
        body, name="pool_bwd", grid=(1,),
        in_specs=[pl.BlockSpec(memory_space=pl.ANY),
                  pl.BlockSpec((SEQ, POOL_WIDTH), lambda i: (0, C_POOL // POOL_WIDTH)),
                  _const_spec((SEQ, POOL_WIDTH)), _const_spec((4, POOL_GD, POOL_GD)), _const_spec((1, POOL_WIDTH))],
        out_specs=[pl.BlockSpec((SEQ, POOL_WIDTH), lambda i: (0, C_POOL // POOL_WIDTH)),
                   _const_spec((4, POOL_GD, POOL_GD)), _const_spec((1, POOL_WIDTH))],
        out_shape=[jax.ShapeDtypeStruct((SEQ, N_CAT), BF), jax.ShapeDtypeStruct((4, POOL_GD, POOL_GD), F32),
                   jax.ShapeDtypeStruct((1, POOL_WIDTH), F32)],
        scratch_shapes=[pltpu.VMEM((POOL_HALO + SEQ, POOL_GD), F32), pltpu.VMEM((SEQ + POOL_HALO, POOL_GD), F32)],
        input_output_aliases={0: 0}, compiler_params=_params("arbitrary"),
    )(*map(_in_hbm, (dzcat, zcat, dps, w_grp, scale)))


GK_TILE = 512


def _gk_fwd(h, wt_gk, wgk_pad, b_gk):
    def body(h_ref, wt_ref, w_ref, b_ref, la_ref):
        z_gk = _dot(h_ref[...], wt_ref[...], tb=True)
        pre = _dot(z_gk, w_ref[...]) + b_ref[...]
        la_ref[...] = (jnp.minimum(pre, 0.0) - jnp.log(1.0 + jnp.exp(-jnp.abs(pre)))) * (1.0 / GATE_NORM)

    return pl.pallas_call(
        body, name="gk_fwd", grid=(SEQ // GK_TILE,),
        in_specs=[pl.BlockSpec((GK_TILE, D_MODEL), lambda i: (i, 0)), _const_spec((GK_PAD, D_MODEL)),
                  _const_spec((GK_PAD, GLA_DK)), _const_spec((1, GLA_DK))],
        out_specs=pl.BlockSpec((GK_TILE, GLA_DK), lambda i: (i, 0)),
        out_shape=jax.ShapeDtypeStruct((SEQ, GLA_DK), F32), compiler_params=_params("parallel"),
    )(*map(_in_hbm, (h, wt_gk, wgk_pad, b_gk)))


def _gk_bwd(dla, h, wt_gk, wgk_pad, b_gk):
    def body(dla_ref, h_ref, wt_ref, w_ref, b_ref, dh_ref, dwt_ref, dw_ref, db_ref):
        hv = h_ref[...]
        wtv = wt_ref[...]
        wv = w_ref[...]
        z_gk = _dot(hv, wtv, tb=True)
        pre = _dot(z_gk, wv) + b_ref[...]
        dpre = dla_ref[...] * (1.0 / GATE_NORM) * (1.0 - _sigmoid(pre))
        dz_gk = _dot(dpre, wv, tb=True)
        dh_ref[...] = _dot(dz_gk, wtv)
        dwtp = _dot(dz_gk, hv, ta=True)
        dwp = _dot(z_gk, dpre, ta=True)[:GATE_RANK]
        dbp = jnp.sum(dpre, axis=0, keepdims=True)

        @pl.when(pl.program_id(0) == 0)
        def _():
            dwt_ref[...] = dwtp
            dw_ref[...] = dwp
            db_ref[...] = dbp

        @pl.when(pl.program_id(0) > 0)
        def _():
            dwt_ref[...] += dwtp
            dw_ref[...] += dwp
            db_ref[...] += dbp

    tile = pl.BlockSpec((GK_TILE, D_MODEL), lambda i: (i, 0))
    return pl.pallas_call(
        body, name="gk_bwd", grid=(SEQ // GK_TILE,),
        in_specs=[pl.BlockSpec((GK_TILE, GLA_DK), lambda i: (i, 0)), tile, _const_spec((GK_PAD, D_MODEL)),
                  _const_spec((GK_PAD, GLA_DK)), _const_spec((1, GLA_DK))],
        out_specs=[tile, _const_spec((GK_PAD, D_MODEL)), _const_spec((GATE_RANK, GLA_DK)), _const_spec((1, GLA_DK))],
        out_shape=[jax.ShapeDtypeStruct((SEQ, D_MODEL), F32), jax.ShapeDtypeStruct((GK_PAD, D_MODEL), F32),
                   jax.ShapeDtypeStruct((GATE_RANK, GLA_DK), F32), jax.ShapeDtypeStruct((1, GLA_DK), F32)],
        compiler_params=_params("arbitrary"),
    )(*map(_in_hbm, (dla, h, wt_gk, wgk_pad, b_gk)))


GLA_ROWS = GLA_CPS * CHUNK
GLA_STEPS = SEQ // GLA_ROWS
QKV_W = 2048


def _gla_chunk(qkv_ref, la_ref, rows, h):
    tri = lax.broadcasted_iota(jnp.int32, (CHUNK, CHUNK), 0) >= lax.broadcasted_iota(jnp.int32, (CHUNK, CHUNK), 1)
    q = qkv_ref[rows, h * HK:(h + 1) * HK] * (HK ** -0.5)
    k = qkv_ref[rows, GLA_DK + h * HK:GLA_DK + (h + 1) * HK]
    v = qkv_ref[rows, 2 * GLA_DK + h * HV:2 * GLA_DK + (h + 1) * HV]
    la = la_ref[rows, h * HK:(h + 1) * HK]
    bc = _dot_exact(tri.astype(F32), la)
    e_pos, e_neg = jnp.exp(bc), jnp.exp(-bc)
    dl = jnp.exp(jnp.sum(la, axis=0, keepdims=True))
    q_fw, q_bw, k_fw, k_bw = q * e_pos, q * e_neg, k * e_neg, k * e_pos
    scores = jnp.where(tri, _dot(q_fw, k_fw, tb=True), _dot(q_bw, k_bw, tb=True))
    return tri, v, e_pos, e_neg, dl, q_fw, q_bw, k_fw, k_bw, scores


def _gla_fwd(zcat, la, after):
    def body(qkv_ref, la_ref, after_ref, o_ref, st_ref, state):
        del after_ref

        @pl.when(pl.program_id(0) == 0)
        def _():
            state[...] = jnp.zeros_like(state)

        for c in range(GLA_CPS):
            rows = slice(c * CHUNK, (c + 1) * CHUNK)
            for h in range(HEADS):
                _, v, _, _, dl, q_fw, _, k_fw, _, scores = _gla_chunk(qkv_ref, la_ref, rows, h)
                st = state[h]
                st_ref[c, h] = st
                o_ref[rows, h * HV:(h + 1) * HV] = _dot(scores, v) + _dot(q_fw, st, tb=True)
                state[h] = st * dl + _dot(v, k_fw * dl, ta=True)

    return pl.pallas_call(
        body, name="gla_fwd", grid=(GLA_STEPS,),
        in_specs=[pl.BlockSpec((GLA_ROWS, QKV_W), lambda i: (i, 0)), pl.BlockSpec((GLA_ROWS, GLA_DK), lambda i: (i, 0)),
                  pl.BlockSpec(memory_space=pl.ANY)],
        out_specs=[pl.BlockSpec((GLA_ROWS, D_MODEL), lambda i: (i, 0)),
                   pl.BlockSpec((GLA_CPS, HEADS, HV, HK), lambda i: (i, 0, 0, 0))],
        out_shape=[jax.ShapeDtypeStruct((SEQ, D_MODEL), F32),
                   jax.ShapeDtypeStruct((SEQ // CHUNK, HEADS, HV, HK), F32)],
        scratch_shapes=[pltpu.VMEM((HEADS, HV, HK), F32)], compiler_params=_params("arbitrary"),
    )(*map(_in_hbm, (zcat, la)), after)


def _gla_bwd(dzcat, zcat, la, d_o, states):
    def body(dz_in, qkv_ref, la_ref, do_ref, st_ref, dqkv_ref, dla_ref, dstate):
        del dz_in

        @pl.when(pl.program_id(0) == 0)
        def _():
            dstate[...] = jnp.zeros_like(dstate)

        last_row = lax.broadcasted_iota(jnp.int32, (CHUNK, HK), 0) == CHUNK - 1
        upper = (lax.broadcasted_iota(jnp.int32, (CHUNK, CHUNK), 0)
                 <= lax.broadcasted_iota(jnp.int32, (CHUNK, CHUNK), 1)).astype(F32)
        for c in reversed(range(GLA_CPS)):
            rows = slice(c * CHUNK, (c + 1) * CHUNK)
            for h in range(HEADS):
                tri, v, e_pos, e_neg, dl, q_fw, q_bw, k_fw, k_bw, scores = _gla_chunk(qkv_ref, la_ref, rows, h)
                st = st_ref[c, h]
                dst = dstate[h]
                d_out = do_ref[rows, h * HV:(h + 1) * HV]
                k_dec = k_fw * dl
                dp = _dot(d_out, v, tb=True)
                dp_fw = jnp.where(tri, dp, 0.0)
                dp_bw = jnp.where(tri, 0.0, dp)
                dv = _dot(scores, d_out, ta=True) + _dot(k_dec, dst, tb=True)
                dk_dec = _dot(v, dst)
                dq_fw = _dot(dp_fw, k_fw) + _dot(d_out, st)
                dk_fw = _dot(dp_fw, q_fw, ta=True) + dk_dec * dl
                dq_bw = _dot(dp_bw, k_bw)
                dk_bw = _dot(dp_bw, q_bw, ta=True)
                ddl = jnp.sum(st * dst, axis=0, keepdims=True) + jnp.sum(k_fw * dk_dec, axis=0, keepdims=True)
                dstate[h] = dst * dl + _dot(d_out, q_fw, ta=True)
                dq = (dq_fw * e_pos + dq_bw * e_neg) * (HK ** -0.5)
                dk = dk_fw * e_neg + dk_bw * e_pos
                db = dq_fw * q_fw - dk_fw * k_fw - dq_bw * q_bw + dk_bw * k_bw + jnp.where(last_row, ddl * dl, 0.0)
                dla_ref[rows, h * HK:(h + 1) * HK] = _dot_exact(upper, db)
                dqkv_ref[rows, h * HK:(h + 1) * HK] = dq.astype(BF)
                dqkv_ref[rows, GLA_DK + h * HK:GLA_DK + (h + 1) * HK] = dk.astype(BF)
                dqkv_ref[rows, 2 * GLA_DK + h * HV:2 * GLA_DK + (h + 1) * HV] = dv.astype(BF)

    rev = lambda i: (GLA_STEPS - 1 - i, 0)
    return pl.pallas_call(
        body, name="gla_bwd", grid=(GLA_STEPS,),
        in_specs=[pl.BlockSpec(memory_space=pl.ANY), pl.BlockSpec((GLA_ROWS, QKV_W), rev),
                  pl.BlockSpec((GLA_ROWS, GLA_DK), rev), pl.BlockSpec((GLA_ROWS, D_MODEL), rev),
                  pl.BlockSpec((GLA_CPS, HEADS, HV, HK), lambda i: (GLA_STEPS - 1 - i, 0, 0, 0))],
        out_specs=[pl.BlockSpec((GLA_ROWS, QKV_W), rev), pl.BlockSpec((GLA_ROWS, GLA_DK), rev)],
        out_shape=[jax.ShapeDtypeStruct((SEQ, N_CAT), BF), jax.ShapeDtypeStruct((SEQ, GLA_DK), F32)],
        scratch_shapes=[pltpu.VMEM((HEADS, HV, HK), F32)], input_output_aliases={0: 0},
        compiler_params=_params("arbitrary"),
    )(*map(_in_hbm, (dzcat, zcat, la, d_o, states)))


def _silu_parts(x):
    s = _sigmoid(x)
    return x * s, s * (1.0 + x * (1.0 - s))


def _post_gla_fwd(o, zcat, g_head):
    def body(o_ref, zog_ref, g_ref, out_ref):
        for h in range(HEADS):
            cols = slice(h * HV, (h + 1) * HV)
            ov = o_ref[:, cols]
            r = lax.rsqrt(jnp.mean(ov * ov, axis=-1, keepdims=True) + EPS)
            act, _ = _silu_parts(zog_ref[:, cols])
            out_ref[:, cols] = (ov * r * g_ref[...] * act).astype(BF)

    tile = pl.BlockSpec((TOK_TILE, D_MODEL), lambda i: (i, 0))
    return pl.pallas_call(
        body, name="post_gla_fwd", grid=(SEQ // TOK_TILE,),
        in_specs=[tile, pl.BlockSpec((TOK_TILE, D_MODEL), lambda i: (i, C_OG // D_MODEL)), _const_spec((1, HV))],
        out_specs=tile, out_shape=jax.ShapeDtypeStruct((SEQ, D_MODEL), BF), compiler_params=_params("parallel"),
    )(*map(_in_hbm, (o, zcat, g_head)))


def _post_gla_bwd(dzcat, dy_gla, w_gla_proj, o, zcat, g_head):
    def body(dz_in, dyg_ref, w_ref, o_ref, zog_ref, g_ref, dz_ref, do_ref, dg_ref):
        del dz_in
        dog = _dot(dyg_ref[...], w_ref[...], tb=True)
        gpart = jnp.zeros((1, HV), F32)
        gv = g_ref[...]
        for h in range(HEADS):
            cols = slice(h * HV, (h + 1) * HV)
            ov = o_ref[:, cols]
            r = lax.rsqrt(jnp.mean(ov * ov, axis=-1, keepdims=True) + EPS)
            on = ov * r
            act, dact = _silu_parts(zog_ref[:, cols])
            dogv = dog[:, cols]
            dz_ref[:, cols] = (dogv * on * gv * dact).astype(BF)
            d_on_g = dogv * act
            gpart = gpart + jnp.sum(d_on_g * on, axis=0, keepdims=True)
            dxn = d_on_g * gv
            do_ref[:, cols] = r * (dxn - on * jnp.mean(dxn * on, axis=-1, keepdims=True))

        @pl.when(pl.program_id(0) == 0)
        def _():
            dg_ref[...] = gpart

        @pl.when(pl.program_id(0) > 0)
        def _():
            dg_ref[...] += gpart

    tile = pl.BlockSpec((TOK_TILE, D_MODEL), lambda i: (i, 0))
    ogspec = pl.BlockSpec((TOK_TILE, D_MODEL), lambda i: (i, C_OG // D_MODEL))
    return pl.pallas_call(
        body, name="post_gla_bwd", grid=(SEQ // TOK_TILE,),
        in_specs=[pl.BlockSpec(memory_space=pl.ANY), tile, _const_spec((D_MODEL, D_MODEL)), tile, ogspec,
                  _const_spec((1, HV))],
        out_specs=[ogspec, tile, _const_spec((1, HV))],
        out_shape=[jax.ShapeDtypeStruct((SEQ, N_CAT), BF), jax.ShapeDtypeStruct((SEQ, D_MODEL), F32),
                   jax.ShapeDtypeStruct((1, HV), F32)],
        input_output_aliases={0: 0}, compiler_params=_params("arbitrary"),
    )(*map(_in_hbm, (dzcat, dy_gla, w_gla_proj, o, zcat, g_head)))


GATE_W = 2 * D_MODEL


def _mix_out_fwd(ps, og, zcat, x, w_pool_proj, w_gla_proj, w_out, b_gate, g_ffn, after):
    def body(ps_ref, og_ref, zg_ref, x_ref, wpp_ref, wgp_ref, wout_ref, b_ref, g_ref, after_ref,
             yp_ref, yg_ref, mixed_ref, x1_ref, h2_ref):
        del after_ref
        y_pool = _dot(ps_ref[...], wpp_ref[...])
        y_gla = _dot(og_ref[...], wgp_ref[...])
        yp_ref[...] = y_pool
        yg_ref[...] = y_gla
        g0 = _sigmoid(zg_ref[:, :D_MODEL] + b_ref[:, :D_MODEL])
        g1 = _sigmoid(zg_ref[:, D_MODEL:] + b_ref[:, D_MODEL:])
        mixed = (g0 * y_pool + g1 * y_gla).astype(BF)
        mixed_ref[...] = mixed
        x1 = x_ref[...] + _dot(mixed, wout_ref[...])
        x1_ref[...] = x1
        r = lax.rsqrt(jnp.mean(x1 * x1, axis=-1, keepdims=True) + EPS)
        h2_ref[...] = (x1 * r * g_ref[...]).astype(BF)

    tile = pl.BlockSpec((TOK_TILE, D_MODEL), lambda i: (i, 0))
    resident = lambda shape: pl.BlockSpec(shape, lambda i: (0, 0), pipeline_mode=pl.Buffered(1))
    f32, bf16 = jax.ShapeDtypeStruct((SEQ, D_MODEL), F32), jax.ShapeDtypeStruct((SEQ, D_MODEL), BF)
    return pl.pallas_call(
        body, name="mix_out_fwd", grid=(SEQ // TOK_TILE,),
        in_specs=[pl.BlockSpec((TOK_TILE, POOL_WIDTH), lambda i: (i, 0)), tile,
                  pl.BlockSpec((TOK_TILE, GATE_W), lambda i: (i, C_GATE // GATE_W)), tile,
                  resident((POOL_WIDTH, D_MODEL)), resident((D_MODEL, D_MODEL)), resident((D_MODEL, D_MODEL)),
                  _const_spec((1, GATE_W)), _const_spec((1, D_MODEL)), pl.BlockSpec(memory_space=pl.ANY)],
        out_specs=[tile] * 5, out_shape=[f32, f32, bf16, f32, bf16], compiler_params=_params("parallel"),
    )(*map(_in_hbm, (ps, og, zcat, x, w_pool_proj, w_gla_proj, w_out, b_gate, g_ffn)), after)


def _mix_bwd(dx1, w_out, zcat, b_gate, y_pool, y_gla):
    def body(dx_ref, w_ref, zg_ref, b_ref, yp_ref, yg_ref, dz_ref, dyp_ref, dyg_ref, db_ref):
        dm = _dot(dx_ref[...], w_ref[...], tb=True)
        g0 = _sigmoid(zg_ref[:, :D_MODEL] + b_ref[:, :D_MODEL])
        g1 = _sigmoid(zg_ref[:, D_MODEL:] + b_ref[:, D_MODEL:])
        dyp_ref[...] = (dm * g0).astype(BF)
        dyg_ref[...] = (dm * g1).astype(BF)
        dz0 = dm * yp_ref[...] * g0 * (1.0 - g0)
        dz1 = dm * yg_ref[...] * g1 * (1.0 - g1)
        dz_ref[:, :D_MODEL] = dz0.astype(BF)
        dz_ref[:, D_MODEL:] = dz1.astype(BF)
        b0 = jnp.sum(dz0, axis=0, keepdims=True)
        b1 = jnp.sum(dz1, axis=0, keepdims=True)

        @pl.when(pl.program_id(0) == 0)
        def _():
            db_ref[:, :D_MODEL] = b0
            db_ref[:, D_MODEL:] = b1

        @pl.when(pl.program_id(0) > 0)
        def _():
            db_ref[:, :D_MODEL] += b0
            db_ref[:, D_MODEL:] += b1

    tile = pl.BlockSpec((TOK_TILE, D_MODEL), lambda i: (i, 0))
    gspec = pl.BlockSpec((TOK_TILE, GATE_W), lambda i: (i, C_GATE // GATE_W))
    return pl.pallas_call(
        body, name="mix_bwd", grid=(SEQ // TOK_TILE,),
        in_specs=[tile, _const_spec((D_MODEL, D_MODEL)), gspec, _const_spec((1, GATE_W)), tile, tile],
        out_specs=[gspec, tile, tile, _const_spec((1, GATE_W))],
        out_shape=[jax.ShapeDtypeStruct((SEQ, N_CAT), BF), jax.ShapeDtypeStruct((SEQ, D_MODEL), BF),
                   jax.ShapeDtypeStruct((SEQ, D_MODEL), BF), jax.ShapeDtypeStruct((1, GATE_W), F32)],
        compiler_params=_params("arbitrary"),
    )(*map(_in_hbm, (dx1, w_out, zcat, b_gate, y_pool, y_gla)))


N_TOK_TILES = SEQ // TOK_TILE
HALO_PER_TILE = TOK_TILE // HALO


LANE_TILES = tuple((lo, min(128, FF_BLK - lo)) for lo in range(0, FF_BLK, 128))


def _taps(w_ref, b_ref, half, lanes, rows):
    shape = (rows, lanes.stop - lanes.start)
    return ([jnp.broadcast_to(w_ref[half, j:j + 1, lanes], shape) for j in range(3)],
            jnp.broadcast_to(b_ref[half, :, lanes], shape))


def _conv_strips(u_ref, ub_ref, ua_ref, taps, lanes, width, n_strips):
    first = pl.program_id(1) == 0
    row = lax.broadcasted_iota(jnp.int32, (HALO, width), 0)
    prev = [[pltpu.roll(jnp.where(first, 0.0, ub_ref[half, :, lanes]), k, 0) for k in (1, 2)] for half in range(2)]
    for s in range(n_strips + (ua_ref is not None)):
        u3, conv = [], []
        for half in range(2):
            cur = u_ref[half, s * HALO:(s + 1) * HALO, lanes] if s < n_strips else ua_ref[half, :, lanes]
            rolled = [pltpu.roll(cur, k, 0) for k in (1, 2)]
            frames = [jnp.where(row >= 2, rolled[1], prev[half][1]), jnp.where(row >= 1, rolled[0], prev[half][0]), cur]
            prev[half] = rolled
            w3, bias = taps[half]
            u3.append(frames)
            conv.append(bias + frames[0] * w3[0] + frames[1] * w3[1] + frames[2] * w3[2])
        yield s, u3, conv


def _pair_specs(pairs):
    tile = pl.BlockSpec((pairs, None, TOK_TILE, FF_BLK), lambda b, i: (0, b, i, 0))
    before = pl.BlockSpec((pairs, None, HALO, FF_BLK), lambda b, i: (0, b, jnp.maximum(i * HALO_PER_TILE - 1, 0), 0))
    after = pl.BlockSpec((pairs, None, HALO, FF_BLK),
                         lambda b, i: (0, b, jnp.minimum((i + 1) * HALO_PER_TILE, SEQ // HALO - 1), 0))

    def vec(rows):
        return pl.BlockSpec((2, None, rows, FF_BLK), lambda b, i: (0, b, 0, 0))

    return tile, before, after, vec


N_STRIPS = TOK_TILE // HALO


def _conv_fwd(u, w_conv, b_conv):
    def body(u_ref, ub_ref, w_ref, b_ref, a_ref):
        for lo, width in LANE_TILES:
            lanes = slice(lo, lo + width)
            taps = [_taps(w_ref, b_ref, half, lanes, HALO) for half in range(2)]
            pending = None
            for s, _, (cg, cv) in _conv_strips(u_ref, ub_ref, None, taps, lanes, width, N_STRIPS):
                act = cg * _sigmoid(cg) * cv
                if s % 2 == 0:
                    pending = act
                else:
                    a_ref[0, (s - 1) * HALO:(s + 1) * HALO, lanes] = jnp.concatenate([pending, act], axis=0).astype(BF)

    tile, before, _, vec = _pair_specs(2)
    out_tile, _, _, _ = _pair_specs(1)
    return pl.pallas_call(
        body, name="conv_fwd", grid=(4, N_TOK_TILES), in_specs=[tile, before, vec(3), vec(1)],
        out_specs=out_tile, out_shape=jax.ShapeDtypeStruct((1, 4, SEQ, FF_BLK), BF),
        compiler_params=_params("parallel", "parallel"),
    )(*map(_in_hbm, (u, u, w_conv, b_conv)))


def _conv_bwd(u, da, w_conv, b_conv):
    def body(u_ref, ub_ref, ua_ref, da_ref, daa_ref, w_ref, b_ref, du_ref, dw_ref, db_ref):
        i = pl.program_id(1)

        @pl.when(i == 0)
        def _():
            dw_ref[...] = jnp.zeros_like(dw_ref)
            db_ref[...] = jnp.zeros_like(db_ref)

        for lo, width in LANE_TILES:
            lanes = slice(lo, lo + width)
            row = lax.broadcasted_iota(jnp.int32, (HALO, width), 0)
            taps = [_taps(w_ref, b_ref, half, lanes, HALO) for half in range(2)]
            acc_w = [[jnp.zeros((HALO, width), F32) for _ in range(3)] for _ in range(2)]
            acc_b = [jnp.zeros((HALO, width), F32) for _ in range(2)]
            da_pair, pending = None, [None, None]
            dc_prev, up_prev = [None, None], [None, None]
            for s, u3, (cg, cv) in _conv_strips(u_ref, ub_ref, ua_ref, taps, lanes, width, N_STRIPS):
                act, dact = _silu_parts(cg)
                if s == N_STRIPS:
                    da = jnp.where(i < N_TOK_TILES - 1, daa_ref[0, :, lanes].astype(F32), 0.0)
                elif s % 2 == 0:
                    da_pair = da_ref[0, s * HALO:(s + 2) * HALO, lanes].astype(F32)
                    da = da_pair[:HALO]
                else:
                    da = da_pair[HALO:]
                dc = (da * cv * dact, da * act)
                for half in range(2):
                    up = [pltpu.roll(dc[half], HALO - k, 0) for k in (1, 2)]
                    if s < N_STRIPS:
                        for j in range(3):
                            acc_w[half][j] = acc_w[half][j] + dc[half] * u3[half][j]
                        acc_b[half] = acc_b[half] + dc[half]
                    if s >= 1:
                        w3 = taps[half][0]
                        du = (dc_prev[half] * w3[2] + jnp.where(row < HALO - 1, up_prev[half][0], up[0]) * w3[1]
                              + jnp.where(row < HALO - 2, up_prev[half][1], up[1]) * w3[0])
                        if (s - 1) % 2 == 0:
                            pending[half] = du
                        else:
                            du_ref[half, (s - 2) * HALO:s * HALO, lanes] = jnp.concatenate([pending[half], du],
                                                                                           axis=0).astype(BF)
                    dc_prev[half], up_prev[half] = dc[half], up
            for half in range(2):
                for j in range(3):
                    dw_ref[half, j:j + 1, lanes] += jnp.sum(acc_w[half][j], axis=0, keepdims=True)
                db_ref[half, :, lanes] += jnp.sum(acc_b[half], axis=0, keepdims=True)

    tile, before, after, vec = _pair_specs(2)
    da_tile, _, da_after_spec, _ = _pair_specs(1)
    return pl.pallas_call(
        body, name="conv_bwd", grid=(4, N_TOK_TILES),
        in_specs=[tile, before, after, da_tile, da_after_spec, vec(3), vec(1)],
        out_specs=[tile, vec(3), vec(1)],
        out_shape=[jax.ShapeDtypeStruct((2, 4, SEQ, FF_BLK), BF), jax.ShapeDtypeStruct((2, 4, 3, FF_BLK), F32),
                   jax.ShapeDtypeStruct((2, 4, 1, FF_BLK), F32)],
        compiler_params=_params("parallel", "arbitrary"),
    )(*map(_in_hbm, (u, u, u, da, da, w_conv, b_conv)))


W_IN_SEGMENTS = ((R_POOL, POOL_WIDTH, "cat", C_POOL), (R_QKV, QKV_W, "cat", C_QKV), (R_OG, D_MODEL, "cat", C_OG),
                 (R_GK, GATE_RANK, "gk", 0), (R_GATE, GATE_W, "cat", C_GATE))


def _slab_pieces(d):
    lo, hi = d * IN_SHARD, (d + 1) * IN_SHARD
    pieces = []
    for start, n, dest, at in W_IN_SEGMENTS:
        a, b = max(lo, start), min(hi, start + n)
        if a < b:
            assert (a - lo) % 2 == 0 and (b - a) % 2 == 0 and (at + a - start) % 2 == 0
            pieces.append(((a - lo) // 2, (b - a) // 2, dest, (at + a - start) // 2))
    return pieces


def _unshard_w_in(slabs):
    def body(slab_ref, cat_ref, gk_ref):
        d = pl.program_id(0)
        src = slab_ref.bitcast(jnp.uint32)
        dst = dict(cat=cat_ref.bitcast(jnp.uint32), gk=gk_ref.bitcast(jnp.uint32))

        @pl.when(d == 0)
        def _():
            gk_ref[...] = jnp.zeros_like(gk_ref)

        for dd in range(N_DEV):
            @pl.when(d == dd)
            def _():
                for a, n, dest, at in _slab_pieces(dd):
                    dst[dest][pl.ds(at, n), :] = src[0, pl.ds(a, n), :]

    return pl.pallas_call(
        body, name="unshard_w_in", grid=(N_DEV,),
        in_specs=[pl.BlockSpec((1, IN_SHARD, D_MODEL), lambda d: (d, 0, 0))],
        out_specs=[_const_spec((N_CAT, D_MODEL)), _const_spec((GK_PAD, D_MODEL))],
        out_shape=[jax.ShapeDtypeStruct((N_CAT, D_MODEL), BF), jax.ShapeDtypeStruct((GK_PAD, D_MODEL), BF)],
        compiler_params=_params("arbitrary"),
    )(_in_hbm(slabs))


def _shard_d_w_in(d_cat, d_gk):
    def body(cat_ref, gk_ref, slab_ref):
        d = pl.program_id(0)
        cat = cat_ref.bitcast(jnp.uint32)
        gk = pltpu.bitcast(gk_ref[0:GATE_RANK, :].astype(BF), jnp.uint32)
        dst = slab_ref.bitcast(jnp.uint32)
        for dd in range(N_DEV):
            @pl.when(d == dd)
            def _():
                for a, n, source, at in _slab_pieces(dd):
                    dst[0, pl.ds(a, n), :] = gk[at:at + n] if source == "gk" else cat[pl.ds(at, n), :]

    return pl.pallas_call(
        body, name="shard_d_w_in", grid=(N_DEV,),
        in_specs=[_const_spec((N_CAT, D_MODEL)), _const_spec((GK_PAD, D_MODEL))],
        out_specs=pl.BlockSpec((1, IN_SHARD, D_MODEL), lambda d: (d, 0, 0)),
        out_shape=jax.ShapeDtypeStruct((N_DEV, IN_SHARD, D_MODEL), BF), compiler_params=_params("parallel"),
    )(_in_hbm(d_cat), _in_hbm(d_gk))


ANY = pl.BlockSpec(memory_space=pl.ANY)


def _place():
    x, y, c = lax.axis_index("x"), lax.axis_index("y"), lax.axis_index("c")
    other_chips = [(1 - x, y), (x, 1 - y), (1 - x, 1 - y)]
    return x, y, c, other_chips


def _all_gather(shards, name):
    n = len(shards)

    def body(*refs):
        src, out = refs[:n], refs[n:2 * n]
        send_sems, recv_sems, local_sems = refs[2 * n:]
        x, y, c, chips = _place()
        me, sibling = (x, y, c), (x, y, 1 - c)

        def copy(a, k, block, to, own=False):
            dst = out[a].at[4 * block[0] + 2 * block[1] + block[2]]
            return pltpu.make_async_remote_copy(src_ref=src[a] if own else dst, dst_ref=dst, send_sem=send_sems.at[a, k],
                                                recv_sem=recv_sems.at[a, k], device_id=to, device_id_type=MESH)

        mine = [pltpu.make_async_copy(src[a], out[a].at[4 * x + 2 * y + c], local_sems.at[a]) for a in range(n)]
        first = []
        for a in range(n):
            mine[a].start()
            first.append(copy(a, 0, me, sibling, own=True))
            first += [copy(a, 1 + j, me, (*chip, c), own=True) for j, chip in enumerate(chips)]
        for cp in first:
            cp.start()
        passed = []
        for j, chip in enumerate(chips):
            for a in range(n):
                copy(a, 1 + j, (*chip, c), me).wait_recv()
                passed.append(copy(a, 4 + j, (*chip, c), sibling))
                passed[-1].start()
        for a in range(n):
            copy(a, 0, sibling, me).wait_recv()
            for j, chip in enumerate(chips):
                copy(a, 4 + j, (*chip, 1 - c), me).wait_recv()
        for cp in first + passed:
            cp.wait_send()
        for cp in mine:
            cp.wait()

    return pl.pallas_call(
        body, name=name, in_specs=[ANY] * n, out_specs=[ANY] * n,
        out_shape=[jax.ShapeDtypeStruct((N_DEV,) + s.shape, s.dtype) for s in shards],
        scratch_shapes=[pltpu.SemaphoreType.DMA((n, 7)), pltpu.SemaphoreType.DMA((n, 7)), pltpu.SemaphoreType.DMA((n,))],
    )(*map(_in_hbm, shards))


SEM = pl.BlockSpec(memory_space=pltpu.SEMAPHORE)
IN_HBM = pl.BlockSpec(memory_space=pltpu.HBM)
SPLIT_PARAMS = pltpu.CompilerParams(has_side_effects=pltpu.SideEffectType.DATAFLOW_SIDE_EFFECTING)


def _gather_first(refs, send_sems, recv_sems):
    x, y, c, chips = _place()
    targets = [(x, y, 1 - c)] + [(px, py, c) for px, py in chips]
    return [pltpu.make_async_remote_copy(src_ref=refs[2 * a], dst_ref=refs[2 * a + 1].at[4 * x + 2 * y + c],
                                         send_sem=send_sems.at[4 * a + k], recv_sem=recv_sems.at[4 * a + k],
                                         device_id=to, device_id_type=MESH)
            for a in range(len(refs) // 2) for k, to in enumerate(targets)]


def _gather_second(refs, send_sems, recv_sems):
    x, y, c, chips = _place()
    copies = []
    for a, land in enumerate(refs):
        for j, (px, py) in enumerate(chips):
            block = land.at[4 * px + 2 * py + c]
            copies.append(pltpu.make_async_remote_copy(src_ref=block, dst_ref=block, send_sem=send_sems.at[3 * a + j],
                                                       recv_sem=recv_sems.at[3 * a + j], device_id=(x, y, 1 - c),
                                                       device_id_type=MESH))
    return copies


def _reduce_first(refs, send_sems, recv_sems):
    x, y, c, _ = _place()
    return [pltpu.make_async_remote_copy(src_ref=refs[2 * a].at[j, 1 - c], dst_ref=refs[2 * a + 1].at[j],
                                         send_sem=send_sems.at[4 * a + j], recv_sem=recv_sems.at[4 * a + j],
                                         device_id=(x, y, 1 - c), device_id_type=MESH)
            for a in range(len(refs) // 2) for j in range(4)]


def _reduce_second(refs, send_sems, recv_sems):
    _, _, c, chips = _place()
    return [pltpu.make_async_remote_copy(src_ref=refs[2 * a].at[2 * px + py], dst_ref=refs[2 * a + 1].at[k],
                                         send_sem=send_sems.at[3 * a + k], recv_sem=recv_sems.at[3 * a + k],
                                         device_id=(px, py, c), device_id_type=MESH)
            for a in range(len(refs) // 2) for k, (px, py) in enumerate(chips)]


def _split_start(name, groups):
    arrays = [a for g in groups for a in g[0]]
    n = len(arrays)

    def body(*refs):
        sems = refs[n:n + 2 * len(groups)]
        at = 0
        for gi, (members, _, build) in enumerate(groups):
            for cp in build(refs[at:at + len(members)], sems[2 * gi], sems[2 * gi + 1]):
                cp.start()
            at += len(members)
        refs[-1][...] = jnp.zeros_like(refs[-1])

    sem_shapes = [pltpu.SemaphoreType.DMA((g[1],)) for g in groups for _ in range(2)]
    outs = pl.pallas_call(
        body, name=name, in_specs=[IN_HBM] * n,
        out_shape=(*sem_shapes, *[pltpu.HBM(a.shape, a.dtype) for a in arrays], jax.ShapeDtypeStruct((8, 128), F32)),
        out_specs=(*[SEM] * len(sem_shapes), *[IN_HBM] * n, pl.BlockSpec(memory_space=pltpu.VMEM)),
        input_output_aliases={i: len(sem_shapes) + i for i in range(n)}, compiler_params=SPLIT_PARAMS,
    )(*[pltpu.with_memory_space_constraint(a, pltpu.HBM) for a in arrays])
    per_group, at = [], len(sem_shapes)
    for gi, (members, _, _) in enumerate(groups):
        per_group.append((outs[2 * gi], outs[2 * gi + 1], list(outs[at:at + len(members)])))
        at += len(members)
    return per_group, outs[-1]


def _split_wait(name, started, build, after):
    send_sems, recv_sems, arrays = started
    n = len(arrays)

    def body(*refs):
        for cp in build(refs[:n], refs[n], refs[n + 1]):
            cp.wait_send()
            cp.wait_recv()

    return pl.pallas_call(
        body, name=name, in_specs=[IN_HBM] * n + [SEM, SEM, ANY],
        out_shape=tuple(pltpu.HBM(a.shape, a.dtype) for a in arrays), out_specs=tuple([IN_HBM] * n),
        input_output_aliases={i: i for i in range(n)}, compiler_params=SPLIT_PARAMS,
    )(*arrays, send_sems, recv_sems, after)


def _gather_landing(shard, me):
    return lax.dynamic_update_slice(lax.empty((N_DEV,) + shard.shape, shard.dtype), shard[None],
                                    (me,) + (0,) * shard.ndim)


def _tile_2d(rows, cols):
    for t in (256, 176, 128):
        if rows % t == 0:
            return t, cols
    return rows, 256


def _pair_sum(part, recv, core, name):
    _, rows, cols = recv.shape
    tr, tc = rows, cols

    def body(c_ref, p_ref, r_ref, o_ref):
        del c_ref
        o_ref[...] = (p_ref[...].astype(F32) + r_ref[...].astype(F32)).astype(BF)

    grid_spec = pltpu.PrefetchScalarGridSpec(
        num_scalar_prefetch=1, grid=(4, rows // tr, cols // tc),
        in_specs=[pl.BlockSpec((None, None, tr, tc), lambda j, i, k, c_ref: (j, c_ref[0], i, k)),
                  pl.BlockSpec((None, tr, tc), lambda j, i, k, c_ref: (j, i, k))],
        out_specs=pl.BlockSpec((None, tr, tc), lambda j, i, k, c_ref: (j, i, k)))
    return pl.pallas_call(
        body, name=name, grid_spec=grid_spec, out_shape=jax.ShapeDtypeStruct(recv.shape, BF),
        compiler_params=_params("parallel", "parallel", "parallel"),
    )(core, *map(_in_hbm, (part, recv)))


def _adamw(w, g, m, v):
    m = ADAM_B1 * m + (1.0 - ADAM_B1) * g
    v = ADAM_B2 * v + (1.0 - ADAM_B2) * (g * g)
    delta = -ADAM_LR * ((m / ADAM_C1) / (jnp.sqrt(v / ADAM_C2) + ADAM_EPS) + ADAM_WD * w)
    return delta, m, v


def _chip_sum_adamw(sums, recv, w, m, v, chip, name):
    rows, cols = w.shape
    tr, tc = _tile_2d(rows, cols)

    def body(chip_ref, s_ref, r_ref, w_ref, m_ref, v_ref, g_out, d_out, m_out, v_out):
        del chip_ref
        g = s_ref[...].astype(F32)
        for k in range(3):
            g = g + r_ref[k].astype(F32)
        g_out[...] = g
        d_out[...], m_out[...], v_out[...] = _adamw(w_ref[...], g, m_ref[...], v_ref[...])

    tile = pl.BlockSpec((tr, tc), lambda i, k, chip_ref: (i, k))
    grid_spec = pltpu.PrefetchScalarGridSpec(
        num_scalar_prefetch=1, grid=(rows // tr, cols // tc),
        in_specs=[pl.BlockSpec((None, tr, tc), lambda i, k, chip_ref: (chip_ref[0], i, k)),
                  pl.BlockSpec((3, tr, tc), lambda i, k, chip_ref: (0, i, k)), tile, tile, tile],
        out_specs=[tile] * 4)
    return pl.pallas_call(
        body, name=name, grid_spec=grid_spec, out_shape=[jax.ShapeDtypeStruct((rows, cols), F32)] * 4,
        compiler_params=_params("parallel", "parallel"),
    )(chip, *map(_in_hbm, (sums, recv, w, m, v)))


def _small_sum_adamw(me, entries, loss_parts):
    def whole(shape, squeeze=0, pick=False):
        blk = (None,) * squeeze + tuple(shape[squeeze:])
        if pick:
            blk = (shape[0], None) + tuple(shape[2:])
            return pl.BlockSpec(blk, lambda i, me_ref: (0, me_ref[0]) + (0,) * (len(shape) - 2))
        return pl.BlockSpec(blk, lambda i, me_ref: (0,) * len(shape))

    in_specs, out_specs, out_shape, args = [], [], [], []
    for parts, w, m, v, sharded in entries:
        lead = w.ndim - (parts.ndim - (2 if sharded else 1))
        in_specs += [whole(parts.shape, pick=sharded)] + [whole(w.shape, squeeze=lead)] * 3
        out_specs += [whole(w.shape, squeeze=lead)] * 4
        out_shape += [jax.ShapeDtypeStruct(w.shape, F32)] * 4
        args += [parts, w, m, v]
    in_specs.append(whole(loss_parts.shape))
    out_specs.append(whole(loss_parts.shape[1:]))
    out_shape.append(jax.ShapeDtypeStruct(loss_parts.shape[1:], F32))
    n = len(entries)

    def added(p_ref):
        total = p_ref[0]
        for d in range(1, N_DEV):
            total = total + p_ref[d]
        return total

    def body(me_ref, *refs):
        del me_ref
        ins, outs = refs[:4 * n + 1], refs[4 * n + 1:]
        for e in range(n):
            p_ref, w_ref, m_ref, v_ref = ins[4 * e:4 * e + 4]
            g_out, d_out, m_out, v_out = outs[4 * e:4 * e + 4]
            g = added(p_ref)
            g_out[...] = g
            d_out[...], m_out[...], v_out[...] = _adamw(w_ref[...], g, m_ref[...], v_ref[...])
        outs[4 * n][...] = added(ins[4 * n])

    grid_spec = pltpu.PrefetchScalarGridSpec(num_scalar_prefetch=1, grid=(1,), in_specs=in_specs, out_specs=out_specs)
    outs = pl.pallas_call(body, name="small_sum_adamw", grid_spec=grid_spec, out_shape=out_shape,
                          compiler_params=_params("arbitrary"))(me, *map(_in_hbm, args + [loss_parts]))
    return [outs[4 * e:4 * e + 4] for e in range(n)], outs[4 * n]


MM_TILE = 512
N_MM_TILES = SEQ // MM_TILE
CAT_TILE = 512
N_CAT_TILES = N_CAT // CAT_TILE
SMALL_ROWS = 808
SHARD_ROWS = 32


def kernel(x, g_mix, w_in, b_gate, w_gk_up, b_gk, w_pool_grp, pool_scale, g_gla_head, w_pool_proj, w_gla_proj, w_out, g_ffn, w_up, w_conv, b_conv, w_down, g_final, loss_target, m_g_mix, m_w_in, m_b_gate, m_w_gk_up, m_b_gk, m_w_pool_grp, m_pool_scale, m_g_gla_head, m_w_pool_proj, m_w_gla_proj, m_w_out, m_g_ffn, m_w_up, m_w_conv, m_b_conv, m_w_down, m_g_final, v_g_mix, v_w_in, v_b_gate, v_w_gk_up, v_b_gk, v_w_pool_grp, v_pool_scale, v_g_gla_head, v_w_pool_proj, v_w_gla_proj, v_w_out, v_g_ffn, v_w_up, v_w_conv, v_b_conv, v_w_down, v_g_final):
    xi, yi, ci = lax.axis_index("x"), lax.axis_index("y"), lax.axis_index("c")
    me = 4 * xi + 2 * yi + ci
    core = jnp.reshape(ci, (1,)).astype(jnp.int32)
    chip = jnp.reshape(2 * xi + yi, (1,)).astype(jnp.int32)
    xs, target = x[0], loss_target[0]

    big = dict(w_in=w_in[0].T, w_pool_proj=w_pool_proj[0], w_gla_proj=w_gla_proj[0], w_out=w_out[0], w_up=w_up[0].T,
               w_down=w_down[0])
    moments = dict(w_in=(m_w_in[0].T, v_w_in[0].T), w_pool_proj=(m_w_pool_proj[0], v_w_pool_proj[0]),
                   w_gla_proj=(m_w_gla_proj[0], v_w_gla_proj[0]), w_out=(m_w_out[0], v_w_out[0]),
                   w_up=(m_w_up[0].T, v_w_up[0].T), w_down=(m_w_down[0], v_w_down[0]))
    names = list(big)
    shards = {k: big[k].astype(BF) for k in names}
    shards["w_gk_up"], shards["w_conv"] = w_gk_up[0], w_conv[0]
    gather_groups = (("w_in", "w_gk_up"), ("w_pool_proj", "w_gla_proj", "w_out"), ("w_up", "w_down", "w_conv"))
    started, token = _split_start("gather_start", [
        ([t for k in g for t in (shards[k], _gather_landing(shards[k], me))], 4 * len(g), _gather_first)
        for g in gather_groups])

    def gather_pass(gi, after):
        lands = list(_split_wait(f"gather_wait_{gi}", started[gi], _gather_first, after)[1::2])
        passed, tkn = _split_start(f"gather_pass_{gi}", [(lands, 3 * len(lands), _gather_second)])
        return passed[0], tkn

    def gather_done(gi, passed, after):
        return dict(zip(gather_groups[gi], _split_wait(f"gather_pass_wait_{gi}", passed, _gather_second, after)))

    tok = lambda i, j, k: (i, 0)
    whole = lambda i, j, k: (0, 0)
    kblk = lambda i, j, k: (k, 0)
    ff_tile = (None, None, MM_TILE, FF_BLK)
    ff_seq = (None, None, SEQ, FF_BLK)

    h = _rms_fwd(xs, g_mix + token[:1, :1], "rms_mix")
    wg = gather_done(0, gather_pass(0, h)[0], h)
    wt_cat, wt_gk = _unshard_w_in(wg["w_in"])
    wgk_pad = jnp.pad(wg["w_gk_up"].transpose(1, 0, 2).reshape(GATE_RANK, GLA_DK), ((0, GK_PAD - GATE_RANK), (0, 0)))
    zcat = _mm(h, wt_cat, out_shape=(SEQ, N_CAT), out_dtype=F32, grid=(N_CAT_TILES, 1, 1),
               blk_a=(SEQ, D_MODEL), blk_b=(CAT_TILE, D_MODEL), blk_o=(SEQ, CAT_TILE),
               map_a=whole, map_b=lambda j, i, k: (j, 0), map_o=lambda j, i, k: (0, j), tb=True, name="mm_in")
    la = _gk_fwd(h, wt_gk, wgk_pad, b_gk)
    passed, tkn = gather_pass(1, la)
    o, states = _gla_fwd(zcat, la, tkn)
    wg = gather_done(1, passed, o)
    wpp = wg["w_pool_proj"].transpose(1, 0, 2).reshape(POOL_WIDTH, D_MODEL)
    wgp = wg["w_gla_proj"].reshape(D_MODEL, D_MODEL)
    wout = wg["w_out"].reshape(D_MODEL, D_MODEL)
    og = _post_gla_fwd(o, zcat, g_gla_head)
    ps = _pool_fwd(zcat, w_pool_grp[0], pool_scale)
    passed, tkn = gather_pass(2, ps)
    y_pool, y_gla, mixed, x1, h2 = _mix_out_fwd(ps, og, zcat, xs, wpp, wgp, wout, b_gate, g_ffn, tkn)
    wg = gather_done(2, passed, h2)
    wt_up = wg["w_up"].reshape(2 * D_FF, D_MODEL)
    wdown = wg["w_down"].reshape(D_FF, D_MODEL)
    wconv4 = wg["w_conv"].reshape(2, 4, 3, FF_BLK)
    bconv4 = b_conv.reshape(2, 4, 1, FF_BLK)
    blk4 = lambda b, i, k: (b // 4, b % 4, 0, 0)
    u4 = _mm(h2, wt_up, out_shape=(2, 4, SEQ, FF_BLK), out_dtype=F32, grid=(N_DEV, 1, 1),
             blk_a=(SEQ, D_MODEL), blk_b=(FF_BLK, D_MODEL), blk_o=ff_seq,
             map_a=whole, map_b=lambda b, i, k: (b, 0), map_o=blk4, tb=True, name="mm_up")
    act = _conv_fwd(u4, wconv4, bconv4)
    loss_part, dx2, dx2_bf, dg_final = _mm_tokens(
        act, wdown, blk_a=(None, 4, TOK_MM_TILE, FF_BLK), map_a=lambda i: (0, 0, i, 0),
        pieces=[(b, b * FF_BLK, FF_BLK) for b in range(4)], res=x1, then=("loss", g_final.reshape(1, D_MODEL), target),
        name="mm_down_loss")

    da = _mm(dx2_bf, wdown, out_shape=(1, 4, SEQ, FF_BLK), out_dtype=BF, grid=(4, 1, 1),
             blk_a=(SEQ, D_MODEL), blk_b=(FF_BLK, D_MODEL), blk_o=ff_seq,
             map_a=whole, map_b=lambda b, i, k: (b, 0), map_o=lambda b, i, k: (0, b, 0, 0), tb=True, name="mm_d_act")
    d_wdown = _mm(act, dx2_bf, out_shape=(D_FF, D_MODEL), out_dtype=BF, grid=(4, 1, 1),
                  blk_a=ff_seq, blk_b=(SEQ, D_MODEL), blk_o=(FF_BLK, D_MODEL),
                  map_a=lambda b, i, k: (0, b, 0, 0), map_b=whole, map_o=lambda b, i, k: (b, 0), ta=True,
                  name="mm_d_wdown")
    du4, d_wconv, d_bconv = _conv_bwd(u4, da, wconv4, bconv4)
    d_wt_up = _mm(du4, h2, out_shape=(2 * D_FF, D_MODEL), out_dtype=BF, grid=(N_DEV, 1, 1),
                  blk_a=ff_seq, blk_b=(SEQ, D_MODEL), blk_o=(FF_BLK, D_MODEL),
                  map_a=blk4, map_b=whole, map_o=lambda b, i, k: (b, 0), ta=True, name="mm_d_wup")
    res = {}

    def reduce_start(keys, parts):
        arrays = [t for k in keys for t in (parts[k], lax.empty((4,) + parts[k].shape[2:], BF))]
        st, tkn = _split_start("reduce_start_" + keys[0], [(arrays, 4 * len(keys), _reduce_first)])
        return st[0], tkn

    def reduce_cross(keys, st, after):
        arrays = _split_wait("reduce_wait_" + keys[0], st, _reduce_first, after)
        sums = [_pair_sum(p, r, core, "pair_sum_" + k) for k, p, r in zip(keys, arrays[0::2], arrays[1::2])]
        arrays = [t for s in sums for t in (s, lax.empty((3,) + s.shape[1:], BF))]
        st2, tkn = _split_start("reduce_cross_" + keys[0], [(arrays, 3 * len(keys), _reduce_second)])
        return st2[0], tkn

    def reduce_done(keys, st2, after):
        arrays = _split_wait("reduce_cross_wait_" + keys[0], st2, _reduce_second, after)
        for k, s, r in zip(keys, arrays[0::2], arrays[1::2]):
            outs = _chip_sum_adamw(s, r, big[k], moments[k][0], moments[k][1], chip, "adamw_" + k)
            res[k] = [(t.T if k in ("w_in", "w_up") else t)[None] for t in outs]

    ffn_keys = ("w_down", "w_up")
    ffn_red, tkn = reduce_start(ffn_keys, dict(w_down=d_wdown.reshape(4, 2, D_FF // N_DEV, D_MODEL),
                                               w_up=d_wt_up.reshape(4, 2, FF_BLK, D_MODEL)))
    dx1, dg_ffn = _mm_tokens(
        du4, wt_up, blk_a=(2, 4, TOK_MM_TILE, FF_BLK), map_a=lambda i: (0, 0, i, 0),
        pieces=[((b // 4, b % 4), b * FF_BLK, FF_BLK) for b in range(N_DEV)], after=tkn, then=("rms_bwd", x1, g_ffn, dx2),
        name="mm_d_h2_rms")

    sq_t = dict(out_shape=(D_MODEL, D_MODEL), grid=(1, 1, N_MM_TILES), blk_a=(MM_TILE, D_MODEL),
                blk_b=(MM_TILE, D_MODEL), blk_o=(D_MODEL, D_MODEL), map_a=kblk, map_b=kblk, map_o=whole, ta=True)
    d_wout = _mm(mixed, dx1, out_dtype=BF, name="mm_d_wout", **sq_t)
    dzcat, dy_pool, dy_gla, db_gate = _mix_bwd(dx1, wout, zcat, b_gate, y_pool, y_gla)
    ffn_red, _ = reduce_cross(ffn_keys, ffn_red, db_gate)
    d_wgp = _mm(og, dy_gla, out_dtype=BF, name="mm_d_wgp", **sq_t)
    mix_keys = ("w_out", "w_gla_proj")
    mix_red, tkn = reduce_start(mix_keys, dict(w_out=d_wout.reshape(4, 2, D_MODEL // N_DEV, D_MODEL),
                                               w_gla_proj=d_wgp.reshape(4, 2, D_MODEL // N_DEV, D_MODEL)))
    dzcat, d_o, dg_head = _post_gla_bwd(dzcat, dy_gla, wgp, o, zcat, g_gla_head + tkn[:1, :1])
    dzcat, dla = _gla_bwd(dzcat, zcat, la, d_o, states)
    mix_red, tkn = reduce_cross(mix_keys, mix_red, dla)
    dh_gk, d_wt_gk, d_wgk, db_gk = _gk_bwd(dla, h, wt_gk, wgk_pad, b_gk + tkn[:1, :1])
    dps = _mm(dy_pool, wpp, out_shape=(SEQ, POOL_WIDTH), out_dtype=F32, grid=(N_MM_TILES, 1, 1),
              blk_a=(MM_TILE, D_MODEL), blk_b=(POOL_WIDTH, D_MODEL), blk_o=(MM_TILE, POOL_WIDTH),
              map_a=tok, map_b=whole, map_o=tok, tb=True, name="mm_d_ps")
    d_wpp = _mm(ps, dy_pool, out_shape=(POOL_WIDTH, D_MODEL), out_dtype=F32, grid=(1, 1, N_MM_TILES),
                blk_a=(MM_TILE, POOL_WIDTH), blk_b=(MM_TILE, D_MODEL), blk_o=(POOL_WIDTH, D_MODEL),
                map_a=kblk, map_b=kblk, map_o=whole, ta=True, name="mm_d_wpp")
    dzcat, d_wgrp, d_scale = _pool_bwd(dzcat, zcat, dps, w_pool_grp[0], pool_scale)
    d_wt_cat = _mm(dzcat, h, out_shape=(N_CAT, D_MODEL), out_dtype=BF, grid=(N_CAT_TILES, 1, 1),
                   blk_a=(SEQ, CAT_TILE), blk_b=(SEQ, D_MODEL), blk_o=(CAT_TILE, D_MODEL),
                   map_a=lambda j, i, k: (0, j), map_b=whole, map_o=lambda j, i, k: (j, 0), ta=True, name="mm_d_wcat")
    in_keys = ("w_in", "w_pool_proj")
    in_red, tkn = reduce_start(in_keys, dict(
        w_in=_shard_d_w_in(d_wt_cat, d_wt_gk).reshape(4, 2, IN_SHARD, D_MODEL),
        w_pool_proj=d_wpp.reshape(POOL_WIDTH, N_DEV, D_MODEL // N_DEV).transpose(1, 0, 2).astype(BF)
        .reshape(4, 2, POOL_WIDTH, D_MODEL // N_DEV)))
    dh = _mm_tokens(dzcat, wt_cat, blk_a=(TOK_MM_TILE, N_CAT), map_a=lambda i: (i, 0), pieces=[(None, 0, N_CAT)],
                    res=dh_gk, after=tkn, name="mm_d_h")
    in_red, tkn = reduce_cross(in_keys, in_red, dh)
    grad_x, dg_mix = _rms_bwd(dh, xs, g_mix + tkn[:1, :1], dx1, "rms_mix_bwd")
    reduce_done(ffn_keys, ffn_red, grad_x)
    reduce_done(mix_keys, mix_red, res["w_down"][0])

    row = lambda t: t.reshape(1, D_MODEL)
    conv_vec = lambda t: t.reshape(2, 4, 1, FF_BLK)
    small = [("g_mix", dg_mix, g_mix, m_g_mix, v_g_mix, False), ("b_gate", db_gate, b_gate, m_b_gate, v_b_gate, False),
             ("w_gk_up", d_wgk.reshape(GATE_RANK, N_DEV, GLA_DK // N_DEV).transpose(1, 0, 2), w_gk_up, m_w_gk_up,
              v_w_gk_up, True),
             ("b_gk", db_gk, b_gk, m_b_gk, v_b_gk, False),
             ("w_pool_grp", d_wgrp, w_pool_grp, m_w_pool_grp, v_w_pool_grp, False),
             ("pool_scale", d_scale, pool_scale, m_pool_scale, v_pool_scale, False),
             ("g_gla_head", dg_head, g_gla_head, m_g_gla_head, v_g_gla_head, False),
             ("g_ffn", dg_ffn, g_ffn, m_g_ffn, v_g_ffn, False),
             ("w_conv", d_wconv.reshape(N_DEV, 3, FF_BLK), w_conv, m_w_conv, v_w_conv, True),
             ("b_conv", d_bconv, conv_vec(b_conv), conv_vec(m_b_conv), conv_vec(v_b_conv), False),
             ("g_final", dg_final, row(g_final), row(m_g_final), row(v_g_final), False)]
    gathered = _all_gather([t[1] for t in small] + [loss_part], "gather_small_grads")
    small_out, loss_sum = _small_sum_adamw(jnp.reshape(me, (1,)).astype(jnp.int32),
                                           [(p,) + t[2:] for p, t in zip(gathered, small)], gathered[-1])
    for t, outs in zip(small, small_out):
        res[t[0]] = list(outs)
    res["b_conv"] = [t.reshape(b_conv.shape) for t in res["b_conv"]]
    res["g_final"] = [t.reshape(g_final.shape) for t in res["g_final"]]

    reduce_done(in_keys, in_red, loss_sum)
    loss = loss_sum[0, 0]
    order =["g_mix", "w_in", "b_gate", "w_gk_up", "b_gk", "w_pool_grp", "pool_scale", "g_gla_head", "w_pool_proj",
             "w_gla_proj", "w_out", "g_ffn", "w_up", "w_conv", "b_conv", "w_down", "g_final"]
    return (loss, grad_x[None], *[res[k][0] for k in order], *[res[k][1] for k in order],
            *[res[k][2] for k in order], *[res[k][3] for k in order])
```

```python
import functools

import jax
import jax.numpy as jnp
from jax import lax
from jax.experimental import pallas as pl
from jax.experimental.pallas import tpu as pltpu

F32 = jnp.float32
BF = jnp.bfloat16
HIGHEST = lax.Precision.HIGHEST
MESH = pl.DeviceIdType.MESH

N_DEV = 8
SEQ = 2048
D_MODEL = 1024
CHUNK = 64
EPS = 1e-6
POOL_WIDTH = 512
POOL_WINDOWS = (2, 4, 8, 16)
POOL_GD = 128
POOL_HALO = 16
HEADS = 4
HK = 128
HV = 256
GLA_DK = 512
GATE_RANK = 16
GATE_NORM = 16.0
D_FF = 2816
FF_BLK = 704
IN_TOTAL = 5648
IN_SHARD = 706
C_QKV, C_GATE, C_OG, C_POOL = 0, 2048, 4096, 5120
N_CAT = 5632
R_POOL, R_QKV, R_OG, R_GK, R_GATE = 0, 512, 2560, 3584, 3600
GK_PAD = 128

ADAM_LR, ADAM_B1, ADAM_B2, ADAM_EPS, ADAM_WD, ADAM_STEP = 0.001, 0.9, 0.999, 1e-08, 0.01, 10
ADAM_C1 = 1.0 - ADAM_B1 ** ADAM_STEP
ADAM_C2 = 1.0 - ADAM_B2 ** ADAM_STEP

VMEM_BYTES_V7X = 64 * 1024 * 1024
VMEM_LIMIT = 48 * 1024 * 1024

TOK_TILE = 256
HALO = 8
GLA_CPS = 4


def _params(*sem):
    return pltpu.CompilerParams(dimension_semantics=sem, vmem_limit_bytes=VMEM_LIMIT)


def _const_spec(shape):
    nd = len(shape)
    return pl.BlockSpec(shape, lambda *_: (0,) * nd)


def _in_hbm(t):
    return pltpu.with_memory_space_constraint(t, pltpu.HBM)


def _dot(a, b, ta=False, tb=False):
    dims = (((0 if ta else 1,), (1 if tb else 0,)), ((), ()))
    return lax.dot_general(a.astype(BF), b.astype(BF), dims, preferred_element_type=F32)


def _dot_exact(a, b):
    return jnp.dot(a, b, precision=HIGHEST, preferred_element_type=F32)


def _sigmoid(x):
    return 0.5 * jnp.tanh(0.5 * x) + 0.5


def _mm(a, b, *, out_shape, out_dtype, grid, blk_a, blk_b, blk_o, map_a, map_b, map_o, ta=False, tb=False,
        res=None, name):
    gk = grid[2]

    def body(*refs):
        if res is None:
            a_ref, b_ref, o_ref = refs[:3]
            r_ref = None
            scr = refs[3:]
        else:
            a_ref, b_ref, r_ref, o_ref = refs[:4]
            scr = refs[4:]
        prod = _dot(a_ref[...], b_ref[...], ta, tb)

        def finish(total):
            if r_ref is not None:
                total = total + r_ref[...]
            o_ref[...] = total.astype(out_dtype)

        if gk == 1:
            finish(prod)
        else:
            acc = scr[0]
            k = pl.program_id(2)

            @pl.when(k == 0)
            def _():
                acc[...] = prod

            @pl.when(k > 0)
            def _():
                acc[...] += prod

            @pl.when(k == gk - 1)
            def _():
                finish(acc[...])

    in_specs = [pl.BlockSpec(blk_a, map_a), pl.BlockSpec(blk_b, map_b)]
    args = [a, b]
    if res is not None:
        in_specs.append(pl.BlockSpec(blk_o, map_o))
        args.append(res)
    return pl.pallas_call(
        body, name=name, grid=grid, in_specs=in_specs, out_specs=pl.BlockSpec(blk_o, map_o),
        out_shape=jax.ShapeDtypeStruct(out_shape, out_dtype),
        scratch_shapes=[] if gk == 1 else [pltpu.VMEM(tuple(d for d in blk_o if d is not None), F32)],
        compiler_params=_params("parallel", "parallel", "arbitrary"),
    )(*[_in_hbm(t) for t in args])


TOK_MM_TILE = 256


def _mm_tokens(a, w, *, blk_a, map_a, pieces, res=None, after=None, then=None, name):
    n_in = 2 + (res is not None) + (after is not None) + (0 if then is None else len(then) - 1)

    def accumulate(ref, part):
        @pl.when(pl.program_id(0) == 0)
        def _():
            ref[...] = part

        @pl.when(pl.program_id(0) > 0)
        def _():
            ref[...] += part

    def body(*refs):
        a_ref, w_ref = refs[:2]
        extra, outs = refs[n_in - (0 if then is None else len(then) - 1):n_in], refs[n_in:]
        total = None
        for idx, row, n in pieces:
            av = a_ref[...] if idx is None else a_ref[idx]
            prod = _dot(av, w_ref[row:row + n, :])
            total = prod if total is None else total + prod
        if res is not None:
            total = total + refs[2][...]
        if then is None:
            outs[0][...] = total
        elif then[0] == "rms_bwd":
            dx, part = _rms_bwd_tile(total, extra[0][...], extra[1][...], extra[2][...])
            outs[0][...] = dx
            accumulate(outs[1], part)
        else:
            lpart, dx, part = _loss_tile(total, extra[0][...], extra[1][...])
            outs[1][...] = dx
            outs[2][...] = dx.astype(BF)
            accumulate(outs[0], lpart)
            accumulate(outs[3], part)

    tile = pl.BlockSpec((TOK_MM_TILE, D_MODEL), lambda i: (i, 0))
    vec = _const_spec((1, D_MODEL))
    big = jax.ShapeDtypeStruct((SEQ, D_MODEL), F32)
    small = jax.ShapeDtypeStruct((1, D_MODEL), F32)
    in_specs = [pl.BlockSpec(blk_a, map_a), pl.BlockSpec(w.shape, lambda i: (0, 0), pipeline_mode=pl.Buffered(1))]
    args = [a, w]
    if res is not None:
        in_specs.append(tile)
        args.append(res)
    if after is not None:
        in_specs.append(pl.BlockSpec(memory_space=pl.ANY))
        args.append(after)
    if then is None:
        out_specs, out_shape = tile, big
    elif then[0] == "rms_bwd":
        in_specs += [tile, vec, tile]
        out_specs, out_shape = [tile, vec], [big, small]
    else:
        in_specs += [vec, tile]
        out_specs = [_const_spec((1, 128)), tile, tile, vec]
        out_shape = [jax.ShapeDtypeStruct((1, 128), F32), big, jax.ShapeDtypeStruct((SEQ, D_MODEL), BF), small]
    if then is not None:
        args += list(then[1:])
    return pl.pallas_call(
        body, name=name, grid=(SEQ // TOK_MM_TILE,), in_specs=in_specs, out_specs=out_specs, out_shape=out_shape,
        compiler_params=_params("parallel" if then is None else "arbitrary"),
    )(*[_in_hbm(t) for t in args])


def _rms_fwd(x, g, name):
    def body(x_ref, g_ref, o_ref):
        xv = x_ref[...]
        r = lax.rsqrt(jnp.mean(xv * xv, axis=-1, keepdims=True) + EPS)
        o_ref[...] = (xv * r * g_ref[...]).astype(BF)

    tile = pl.BlockSpec((TOK_TILE, D_MODEL), lambda i: (i, 0))
    return pl.pallas_call(
        body, name=name, grid=(SEQ // TOK_TILE,), in_specs=[tile, _const_spec((1, D_MODEL))], out_specs=tile,
        out_shape=jax.ShapeDtypeStruct((SEQ, D_MODEL), BF), compiler_params=_params("parallel"),
    )(*map(_in_hbm, (x, g)))


def _rms_bwd_tile(dyv, xv, gv, dresv):
    r = lax.rsqrt(jnp.mean(xv * xv, axis=-1, keepdims=True) + EPS)
    xn = xv * r
    dxn = dyv * gv
    return dresv + r * (dxn - xn * jnp.mean(dxn * xn, axis=-1, keepdims=True)), jnp.sum(dyv * xn, axis=0, keepdims=True)


def _loss_tile(xv, gv, tv):
    r = lax.rsqrt(jnp.mean(xv * xv, axis=-1, keepdims=True) + EPS)
    xn = xv * r
    err = xn * gv - tv
    lpart = jnp.full((1, 128), 0.5 * jnp.sum(jnp.mean(err * err, axis=-1, keepdims=True)), F32)
    dyv = err * (1.0 / D_MODEL)
    dxn = dyv * gv
    return lpart, r * (dxn - xn * jnp.mean(dxn * xn, axis=-1, keepdims=True)), jnp.sum(dyv * xn, axis=0, keepdims=True)


def _pool_counts(w):
    pos = lax.broadcasted_iota(jnp.int32, (SEQ, 1), 0).astype(F32)
    return jnp.minimum(pos + 1.0, float(w))


def _pool_window(u, w, ext):
    ext[pl.ds(POOL_HALO, SEQ), :] = u
    win = u
    for j in range(1, w):
        win = win + ext[pl.ds(POOL_HALO - j, SEQ), :]
    return win / _pool_counts(w) - u


def _pool_fwd(zcat, w_grp, scale):
    def body(z_ref, w_ref, s_ref, o_ref, ext):
        ext[pl.ds(0, POOL_HALO), :] = jnp.zeros((POOL_HALO, POOL_GD), F32)
        for g, w in enumerate(POOL_WINDOWS):
            cols = slice(g * POOL_GD, (g + 1) * POOL_GD)
            p = _pool_window(z_ref[:, cols], w, ext)
            o_ref[:, cols] = (_dot(p, w_ref[g]) * s_ref[:, cols]).astype(BF)

    return pl.pallas_call(
        body, name="pool_fwd", grid=(1,),
        in_specs=[pl.BlockSpec((SEQ, POOL_WIDTH), lambda i: (0, C_POOL // POOL_WIDTH)),
                  _const_spec((4, POOL_GD, POOL_GD)), _const_spec((1, POOL_WIDTH))],
        out_specs=_const_spec((SEQ, POOL_WIDTH)), out_shape=jax.ShapeDtypeStruct((SEQ, POOL_WIDTH), BF),
        scratch_shapes=[pltpu.VMEM((POOL_HALO + SEQ, POOL_GD), F32)], compiler_params=_params("arbitrary"),
    )(*map(_in_hbm, (zcat, w_grp, scale)))


def _pool_bwd(dzcat, zcat, dps, w_grp, scale):
    def body(dz_in, z_ref, dps_ref, w_ref, s_ref, dz_ref, dw_ref, dsc_ref, ext, ext2):
        del dz_in
        ext[pl.ds(0, POOL_HALO), :] = jnp.zeros((POOL_HALO, POOL_GD), F32)
        ext2[pl.ds(SEQ, POOL_HALO), :] = jnp.zeros((POOL_HALO, POOL_GD), F32)
        for g, w in enumerate(POOL_WINDOWS):
            cols = slice(g * POOL_GD, (g + 1) * POOL_GD)
            p = _pool_window(z_ref[:, cols], w, ext)
            wg = w_ref[g]
            pg = _dot(p, wg)
            dpsv = dps_ref[:, cols]
            dsc_ref[:, cols] = jnp.sum(dpsv * pg, axis=0, keepdims=True)
            dpg = dpsv * s_ref[:, cols]
            dw_ref[g] = _dot(p, dpg, ta=True)
            dp = _dot(dpg, wg, tb=True)
            dpc = dp / _pool_counts(w)
            ext2[pl.ds(0, SEQ), :] = dpc
            du = dpc
            for j in range(1, w):
                du = du + ext2[pl.ds(j, SEQ), :]
            dz_ref[:, cols] = (du - dp).astype(BF)

    return pl.pallas_call(
        body, name="pool_bwd", grid=(1,),
        in_specs=[pl.BlockSpec(memory_space=pl.ANY),
                  pl.BlockSpec((SEQ, POOL_WIDTH), lambda i: (0, C_POOL // POOL_WIDTH)),
                  _const_spec((SEQ, POOL_WIDTH)), _const_spec((4, POOL_GD, POOL_GD)), _const_spec((1, POOL_WIDTH))],
        out_specs=[pl.BlockSpec((SEQ, POOL_WIDTH), lambda i: (0, C_POOL // POOL_WIDTH)),
                   _const_spec((4, POOL_GD, POOL_GD)), _const_spec((1, POOL_WIDTH))],
        out_shape=[jax.ShapeDtypeStruct((SEQ, N_CAT), BF), jax.ShapeDtypeStruct((4, POOL_GD, POOL_GD), F32),
                   jax.ShapeDtypeStruct((1, POOL_WIDTH), F32)],
        scratch_shapes=[pltpu.VMEM((POOL_HALO + SEQ, POOL_GD), F32), pltpu.VMEM((SEQ + POOL_HALO, POOL_GD), F32)],
        input_output_aliases={0: 0}, compiler_params=_params("arbitrary"),
    )(*map(_in_hbm, (dzcat, zcat, dps, w_grp, scale)))


GK_TILE = 512


def _gk_fwd(h, wt_gk, wgk_pad, b_gk):
    def body(h_ref, wt_ref, w_ref, b_ref, la_ref):
        z_gk = _dot(h_ref[...], wt_ref[...], tb=True)
        pre = _dot(z_gk, w_ref[...]) + b_ref[...]
        la_ref[...] = (jnp.minimum(pre, 0.0) - jnp.log(1.0 + jnp.exp(-jnp.abs(pre)))) * (1.0 / GATE_NORM)

    return pl.pallas_call(
        body, name="gk_fwd", grid=(SEQ // GK_TILE,),
        in_specs=[pl.BlockSpec((GK_TILE, D_MODEL), lambda i: (i, 0)), _const_spec((GK_PAD, D_MODEL)),
                  _const_spec((GK_PAD, GLA_DK)), _const_spec((1, GLA_DK))],
        out_specs=pl.BlockSpec((GK_TILE, GLA_DK), lambda i: (i, 0)),
        out_shape=jax.ShapeDtypeStruct((SEQ, GLA_DK), F32), compiler_params=_params("parallel"),
    )(*map(_in_hbm, (h, wt_gk, wgk_pad, b_gk)))


def _gk_bwd(dla, h, wt_gk, wgk_pad, b_gk):
    def body(dla_ref, h_ref, wt_ref, w_ref, b_ref, dh_ref, dwt_ref, dw_ref, db_ref):
        hv = h_ref[...]
        wtv = wt_ref[...]
        wv = w_ref[...]
        z_gk = _dot(hv, wtv, tb=True)
        pre = _dot(z_gk, wv) + b_ref[...]
        dpre = dla_ref[...] * (1.0 / GATE_NORM) * (1.0 - _sigmoid(pre))
        dz_gk = _dot(dpre, wv, tb=True)
        dh_ref[...] = _dot(dz_gk, wtv)
        dwtp = _dot(dz_gk, hv, ta=True)
        dwp = _dot(z_gk, dpre, ta=True)[:GATE_RANK]
        dbp = jnp.sum(dpre, axis=0, keepdims=True)

        @pl.when(pl.program_id(0) == 0)
        def _():
            dwt_ref[...] = dwtp
            dw_ref[...] = dwp
            db_ref[...] = dbp

        @pl.when(pl.program_id(0) > 0)
        def _():
            dwt_ref[...] += dwtp
            dw_ref[...] += dwp
            db_ref[...] += dbp

    tile = pl.BlockSpec((GK_TILE, D_MODEL), lambda i: (i, 0))
    return pl.pallas_call(
        body, name="gk_bwd", grid=(SEQ // GK_TILE,),
        in_specs=[pl.BlockSpec((GK_TILE, GLA_DK), lambda i: (i, 0)), tile, _const_spec((GK_PAD, D_MODEL)),
                  _const_spec((GK_PAD, GLA_DK)), _const_spec((1, GLA_DK))],
        out_specs=[tile, _const_spec((GK_PAD, D_MODEL)), _const_spec((GATE_RANK, GLA_DK)), _const_spec((1, GLA_DK))],
        out_shape=[jax.ShapeDtypeStruct((SEQ, D_MODEL), F32), jax.ShapeDtypeStruct((GK_PAD, D_MODEL), F32),
                   jax.ShapeDtypeStruct((GATE_RANK, GLA_DK), F32), jax.ShapeDtypeStruct((1, GLA_DK), F32)],
        compiler_params=_params("arbitrary"),
    )(*map(_in_hbm, (dla, h, wt_gk, wgk_pad, b_gk)))


GLA_ROWS = GLA_CPS * CHUNK
GLA_STEPS = SEQ // GLA_ROWS
QKV_W = 2048


def _gla_chunk(qkv_ref, la_ref, rows, h):
    tri = lax.broadcasted_iota(jnp.int32, (CHUNK, CHUNK), 0) >= lax.broadcasted_iota(jnp.int32, (CHUNK, CHUNK), 1)
    q = qkv_ref[rows, h * HK:(h + 1) * HK] * (HK ** -0.5)
    k = qkv_ref[rows, GLA_DK + h * HK:GLA_DK + (h + 1) * HK]
    v = qkv_ref[rows, 2 * GLA_DK + h * HV:2 * GLA_DK + (h + 1) * HV]
    la = la_ref[rows, h * HK:(h + 1) * HK]
    bc = _dot_exact(tri.astype(F32), la)
    e_pos, e_neg = jnp.exp(bc), jnp.exp(-bc)
    dl = jnp.exp(jnp.sum(la, axis=0, keepdims=True))
    q_fw, q_bw, k_fw, k_bw = q * e_pos, q * e_neg, k * e_neg, k * e_pos
    scores = jnp.where(tri, _dot(q_fw, k_fw, tb=True), _dot(q_bw, k_bw, tb=True))
    return tri, v, e_pos, e_neg, dl, q_fw, q_bw, k_fw, k_bw, scores


def _gla_fwd(zcat, la, after):
    def body(qkv_ref, la_ref, after_ref, o_ref, st_ref, state):
        del after_ref

        @pl.when(pl.program_id(0) == 0)
        def _():
            state[...] = jnp.zeros_like(state)

        for c in range(GLA_CPS):
            rows = slice(c * CHUNK, (c + 1) * CHUNK)
            for h in range(HEADS):
                _, v, _, _, dl, q_fw, _, k_fw, _, scores = _gla_chunk(qkv_ref, la_ref, rows, h)
                st = state[h]
                st_ref[c, h] = st
                o_ref[rows, h * HV:(h + 1) * HV] = _dot(scores, v) + _dot(q_fw, st, tb=True)
                state[h] = st * dl + _dot(v, k_fw * dl, ta=True)

    return pl.pallas_call(
        body, name="gla_fwd", grid=(GLA_STEPS,),
        in_specs=[pl.BlockSpec((GLA_ROWS, QKV_W), lambda i: (i, 0)), pl.BlockSpec((GLA_ROWS, GLA_DK), lambda i: (i, 0)),
                  pl.BlockSpec(memory_space=pl.ANY)],
        out_specs=[pl.BlockSpec((GLA_ROWS, D_MODEL), lambda i: (i, 0)),
                   pl.BlockSpec((GLA_CPS, HEADS, HV, HK), lambda i: (i, 0, 0, 0))],
        out_shape=[jax.ShapeDtypeStruct((SEQ, D_MODEL), F32),
                   jax.ShapeDtypeStruct((SEQ // CHUNK, HEADS, HV, HK), F32)],
        scratch_shapes=[pltpu.VMEM((HEADS, HV, HK), F32)], compiler_params=_params("arbitrary"),
    )(*map(_in_hbm, (zcat, la)), after)


def _gla_bwd(dzcat, zcat, la, d_o, states):
    def body(dz_in, qkv_ref, la_ref, do_ref, st_ref, dqkv_ref, dla_ref, dstate):
        del dz_in

        @pl.when(pl.program_id(0) == 0)
        def _():
            dstate[...] = jnp.zeros_like(dstate)

        last_row = lax.broadcasted_iota(jnp.int32, (CHUNK, HK), 0) == CHUNK - 1
        upper = (lax.broadcasted_iota(jnp.int32, (CHUNK, CHUNK), 0)
                 <= lax.broadcasted_iota(jnp.int32, (CHUNK, CHUNK), 1)).astype(F32)
        for c in reversed(range(GLA_CPS)):
            rows = slice(c * CHUNK, (c + 1) * CHUNK)
            for h in range(HEADS):
                tri, v, e_pos, e_neg, dl, q_fw, q_bw, k_fw, k_bw, scores = _gla_chunk(qkv_ref, la_ref, rows, h)
                st = st_ref[c, h]
                dst = dstate[h]
                d_out = do_ref[rows, h * HV:(h + 1) * HV]
                k_dec = k_fw * dl
                dp = _dot(d_out, v, tb=True)
                dp_fw = jnp.where(tri, dp, 0.0)
                dp_bw = jnp.where(tri, 0.0, dp)
                dv = _dot(scores, d_out, ta=True) + _dot(k_dec, dst, tb=True)
                dk_dec = _dot(v, dst)
                dq_fw = _dot(dp_fw, k_fw) + _dot(d_out, st)
                dk_fw = _dot(dp_fw, q_fw, ta=True) + dk_dec * dl
                dq_bw = _dot(dp_bw, k_bw)
                dk_bw = _dot(dp_bw, q_bw, ta=True)
                ddl = jnp.sum(st * dst, axis=0, keepdims=True) + jnp.sum(k_fw * dk_dec, axis=0, keepdims=True)
                dstate[h] = dst * dl + _dot(d_out, q_fw, ta=True)
                dq = (dq_fw * e_pos + dq_bw * e_neg) * (HK ** -0.5)
                dk = dk_fw * e_neg + dk_bw * e_pos
                db = dq_fw * q_fw - dk_fw * k_fw - dq_bw * q_bw + dk_bw * k_bw + jnp.where(last_row, ddl * dl, 0.0)
                dla_ref[rows, h * HK:(h + 1) * HK] = _dot_exact(upper, db)
                dqkv_ref[rows, h * HK:(h + 1) * HK] = dq.astype(BF)
                dqkv_ref[rows, GLA_DK + h * HK:GLA_DK + (h + 1) * HK] = dk.astype(BF)
                dqkv_ref[rows, 2 * GLA_DK + h * HV:2 * GLA_DK + (h + 1) * HV] = dv.astype(BF)

    rev = lambda i: (GLA_STEPS - 1 - i, 0)
    return pl.pallas_call(
        body, name="gla_bwd", grid=(GLA_STEPS,),
        in_specs=[pl.BlockSpec(memory_space=pl.ANY), pl.BlockSpec((GLA_ROWS, QKV_W), rev),
                  pl.BlockSpec((GLA_ROWS, GLA_DK), rev), pl.BlockSpec((GLA_ROWS, D_MODEL), rev),
                  pl.BlockSpec((GLA_CPS, HEADS, HV, HK), lambda i: (GLA_STEPS - 1 - i, 0, 0, 0))],
        out_specs=[pl.BlockSpec((GLA_ROWS, QKV_W), rev), pl.BlockSpec((GLA_ROWS, GLA_DK), rev)],
        out_shape=[jax.ShapeDtypeStruct((SEQ, N_CAT), BF), jax.ShapeDtypeStruct((SEQ, GLA_DK), F32)],
        scratch_shapes=[pltpu.VMEM((HEADS, HV, HK), F32)], input_output_aliases={0: 0},
        compiler_params=_params("arbitrary"),
    )(*map(_in_hbm, (dzcat, zcat, la, d_o, states)))


def _silu_parts(x):
    s = _sigmoid(x)
    return x * s, s * (1.0 + x * (1.0 - s))


def _post_gla_fwd(o, zcat, g_head):
    def body(o_ref, zog_ref, g_ref, out_ref):
        for h in range(HEADS):
            cols = slice(h * HV, (h + 1) * HV)
            ov = o_ref[:, cols]
            r = lax.rsqrt(jnp.mean(ov * ov, axis=-1, keepdims=True) + EPS)
            act, _ = _silu_parts(zog_ref[:, cols])
            out_ref[:, cols] = (ov * r * g_ref[...] * act).astype(BF)

    tile = pl.BlockSpec((TOK_TILE, D_MODEL), lambda i: (i, 0))
    return pl.pallas_call(
        body, name="post_gla_fwd", grid=(SEQ // TOK_TILE,),
        in_specs=[tile, pl.BlockSpec((TOK_TILE, D_MODEL), lambda i: (i, C_OG // D_MODEL)), _const_spec((1, HV))],
        out_specs=tile, out_shape=jax.ShapeDtypeStruct((SEQ, D_MODEL), BF), compiler_params=_params("parallel"),
    )(*map(_in_hbm, (o, zcat, g_head)))


def _post_gla_bwd(dzcat, dy_gla, w_gla_proj, o, zcat, g_head):
    def body(dz_in, dyg_ref, w_ref, o_ref, zog_ref, g_ref, dz_ref, do_ref, dg_ref):
        del dz_in
        dog = _dot(dyg_ref[...], w_ref[...], tb=True)
        gpart = jnp.zeros((1, HV), F32)
        gv = g_ref[...]
        for h in range(HEADS):
            cols = slice(h * HV, (h + 1) * HV)
            ov = o_ref[:, cols]
            r = lax.rsqrt(jnp.mean(ov * ov, axis=-1, keepdims=True) + EPS)
            on = ov * r
            act, dact = _silu_parts(zog_ref[:, cols])
            dogv = dog[:, cols]
            dz_ref[:, cols] = (dogv * on * gv * dact).astype(BF)
            d_on_g = dogv * act
            gpart = gpart + jnp.sum(d_on_g * on, axis=0, keepdims=True)
            dxn = d_on_g * gv
            do_ref[:, cols] = r * (dxn - on * jnp.mean(dxn * on, axis=-1, keepdims=True))

        @pl.when(pl.program_id(0) == 0)
        def _():
            dg_ref[...] = gpart

        @pl.when(pl.program_id(0) > 0)
        def _():
            dg_ref[...] += gpart

    tile = pl.BlockSpec((TOK_TILE, D_MODEL), lambda i: (i, 0))
    ogspec = pl.BlockSpec((TOK_TILE, D_MODEL), lambda i: (i, C_OG // D_MODEL))
    return pl.pallas_call(
        body, name="post_gla_bwd", grid=(SEQ // TOK_TILE,),
        in_specs=[pl.BlockSpec(memory_space=pl.ANY), tile, _const_spec((D_MODEL, D_MODEL)), tile, ogspec,
                  _const_spec((1, HV))],
        out_specs=[ogspec, tile, _const_spec((1, HV))],
        out_shape=[jax.ShapeDtypeStruct((SEQ, N_CAT), BF), jax.ShapeDtypeStruct((SEQ, D_MODEL), F32),
                   jax.ShapeDtypeStruct((1, HV), F32)],
        input_output_aliases={0: 0}, compiler_params=_params("arbitrary"),
    )(*map(_in_hbm, (dzcat, dy_gla, w_gla_proj, o, zcat, g_head)))


GATE_W = 2 * D_MODEL


def _mix_out_fwd(ps, og, zcat, x, w_pool_proj, w_gla_proj, w_out, b_gate, g_ffn, after):
    def body(ps_ref, og_ref, zg_ref, x_ref, wpp_ref, wgp_ref, wout_ref, b_ref, g_ref, after_ref,
             yp_ref, yg_ref, mixed_ref, x1_ref, h2_ref):
        del after_ref
        y_pool = _dot(ps_ref[...], wpp_ref[...])
        y_gla = _dot(og_ref[...], wgp_ref[...])
        yp_ref[...] = y_pool
        yg_ref[...] = y_gla
        g0 = _sigmoid(zg_ref[:, :D_MODEL] + b_ref[:, :D_MODEL])
        g1 = _sigmoid(zg_ref[:, D_MODEL:] + b_ref[:, D_MODEL:])
        mixed = (g0 * y_pool + g1 * y_gla).astype(BF)
        mixed_ref[...] = mixed
        x1 = x_ref[...] + _dot(mixed, wout_ref[...])
        x1_ref[...] = x1
        r = lax.rsqrt(jnp.mean(x1 * x1, axis=-1, keepdims=True) + EPS)
        h2_ref[...] = (x1 * r * g_ref[...]).astype(BF)

    tile = pl.BlockSpec((TOK_TILE, D_MODEL), lambda i: (i, 0))
    resident = lambda shape: pl.BlockSpec(shape, lambda i: (0, 0), pipeline_mode=pl.Buffered(1))
    f32, bf16 = jax.ShapeDtypeStruct((SEQ, D_MODEL), F32), jax.ShapeDtypeStruct((SEQ, D_MODEL), BF)
    return pl.pallas_call(
        body, name="mix_out_fwd", grid=(SEQ // TOK_TILE,),
        in_specs=[pl.BlockSpec((TOK_TILE, POOL_WIDTH), lambda i: (i, 0)), tile,
                  pl.BlockSpec((TOK_TILE, GATE_W), lambda i: (i, C_GATE // GATE_W)), tile,
                  resident((POOL_WIDTH, D_MODEL)), resident((D_MODEL, D_MODEL)), resident((D_MODEL, D_MODEL)),
                  _const_spec((1, GATE_W)), _const_spec((1, D_MODEL)), pl.BlockSpec(memory_space=pl.ANY)],
        out_specs=[tile] * 5, out_shape=[f32, f32, bf16, f32, bf16], compiler_params=_params("parallel"),
    )(*map(_in_hbm, (ps, og, zcat, x, w_pool_proj, w_gla_proj, w_out, b_gate, g_ffn)), after)


def _mix_bwd(dx1, w_out, zcat, b_gate, y_pool, y_gla):
    def body(dx_ref, w_ref, zg_ref, b_ref, yp_ref, yg_ref, dz_ref, dyp_ref, dyg_ref, db_ref):
        dm = _dot(dx_ref[...], w_ref[...], tb=True)
        g0 = _sigmoid(zg_ref[:, :D_MODEL] + b_ref[:, :D_MODEL])
        g1 = _sigmoid(zg_ref[:, D_MODEL:] + b_ref[:, D_MODEL:])
        dyp_ref[...] = (dm * g0).astype(BF)
        dyg_ref[...] = (dm * g1).astype(BF)
        dz0 = dm * yp_ref[...] * g0 * (1.0 - g0)
        dz1 = dm * yg_ref[...] * g1 * (1.0 - g1)
        dz_ref[:, :D_MODEL] = dz0.astype(BF)
        dz_ref[:, D_MODEL:] = dz1.astype(BF)
        b0 = jnp.sum(dz0, axis=0, keepdims=True)
        b1 = jnp.sum(dz1, axis=0, keepdims=True)

        @pl.when(pl.program_id(0) == 0)
        def _():
            db_ref[:, :D_MODEL] = b0
            db_ref[:, D_MODEL:] = b1

        @pl.when(pl.program_id(0) > 0)
        def _():
            db_ref[:, :D_MODEL] += b0
            db_ref[:, D_MODEL:] += b1

    tile = pl.BlockSpec((TOK_TILE, D_MODEL), lambda i: (i, 0))
    gspec = pl.BlockSpec((TOK_TILE, GATE_W), lambda i: (i, C_GATE // GATE_W))
    return pl.pallas_call(
        body, name="mix_bwd", grid=(SEQ // TOK_TILE,),
        in_specs=[tile, _const_spec((D_MODEL, D_MODEL)), gspec, _const_spec((1, GATE_W)), tile, tile],
        out_specs=[gspec, tile, tile, _const_spec((1, GATE_W))],
        out_shape=[jax.ShapeDtypeStruct((SEQ, N_CAT), BF), jax.ShapeDtypeStruct((SEQ, D_MODEL), BF),
                   jax.ShapeDtypeStruct((SEQ, D_MODEL), BF), jax.ShapeDtypeStruct((1, GATE_W), F32)],
        compiler_params=_params("arbitrary"),
    )(*map(_in_hbm, (dx1, w_out, zcat, b_gate, y_pool, y_gla)))


N_TOK_TILES = SEQ // TOK_TILE
HALO_PER_TILE = TOK_TILE // HALO


LANE_TILES = tuple((lo, min(128, FF_BLK - lo)) for lo in range(0, FF_BLK, 128))


def _taps(w_ref, b_ref, half, lanes, rows):
    shape = (rows, lanes.stop - lanes.start)
    return ([jnp.broadcast_to(w_ref[half, j:j + 1, lanes], shape) for j in range(3)],
            jnp.broadcast_to(b_ref[half, :, lanes], shape))


def _conv_strips(u_ref, ub_ref, ua_ref, taps, lanes, width, n_strips):
    first = pl.program_id(1) == 0
    row = lax.broadcasted_iota(jnp.int32, (HALO, width), 0)
    prev = [[pltpu.roll(jnp.where(first, 0.0, ub_ref[half, :, lanes]), k, 0) for k in (1, 2)] for half in range(2)]
    for s in range(n_strips + (ua_ref is not None)):
        u3, conv = [], []
        for half in range(2):
            cur = u_ref[half, s * HALO:(s + 1) * HALO, lanes] if s < n_strips else ua_ref[half, :, lanes]
            rolled = [pltpu.roll(cur, k, 0) for k in (1, 2)]
            frames = [jnp.where(row >= 2, rolled[1], prev[half][1]), jnp.where(row >= 1, rolled[0], prev[half][0]), cur]
            prev[half] = rolled
            w3, bias = taps[half]
            u3.append(frames)
            conv.append(bias + frames[0] * w3[0] + frames[1] * w3[1] + frames[2] * w3[2])
        yield s, u3, conv


def _pair_specs(pairs):
    tile = pl.BlockSpec((pairs, None, TOK_TILE, FF_BLK), lambda b, i: (0, b, i, 0))
    before = pl.BlockSpec((pairs, None, HALO, FF_BLK), lambda b, i: (0, b, jnp.maximum(i * HALO_PER_TILE - 1, 0), 0))
    after = pl.BlockSpec((pairs, None, HALO, FF_BLK),
                         lambda b, i: (0, b, jnp.minimum((i + 1) * HALO_PER_TILE, SEQ // HALO - 1), 0))

    def vec(rows):
        return pl.BlockSpec((2, None, rows, FF_BLK), lambda b, i: (0, b, 0, 0))

    return tile, before, after, vec


N_STRIPS = TOK_TILE // HALO


def _conv_fwd(u, w_conv, b_conv):
    def body(u_ref, ub_ref, w_ref, b_ref, a_ref):
        for lo, width in LANE_TILES:
            lanes = slice(lo, lo + width)
            taps = [_taps(w_ref, b_ref, half, lanes, HALO) for half in range(2)]
            pending = None
            for s, _, (cg, cv) in _conv_strips(u_ref, ub_ref, None, taps, lanes, width, N_STRIPS):
                act = cg * _sigmoid(cg) * cv
                if s % 2 == 0:
                    pending = act
                else:
                    a_ref[0, (s - 1) * HALO:(s + 1) * HALO, lanes] = jnp.concatenate([pending, act], axis=0).astype(BF)

    tile, before, _, vec = _pair_specs(2)
    out_tile, _, _, _ = _pair_specs(1)
    return pl.pallas_call(
        body, name="conv_fwd", grid=(4, N_TOK_TILES), in_specs=[tile, before, vec(3), vec(1)],
        out_specs=out_tile, out_shape=jax.ShapeDtypeStruct((1, 4, SEQ, FF_BLK), BF),
        compiler_params=_params("parallel", "parallel"),
    )(*map(_in_hbm, (u, u, w_conv, b_conv)))


def _conv_bwd(u, da, w_conv, b_conv):
    def body(u_ref, ub_ref, ua_ref, da_ref, daa_ref, w_ref, b_ref, du_ref, dw_ref, db_ref):
        i = pl.program_id(1)

        @pl.when(i == 0)
        def _():
            dw_ref[...] = jnp.zeros_like(dw_ref)
            db_ref[...] = jnp.zeros_like(db_ref)

        for lo, width in LANE_TILES:
            lanes = slice(lo, lo + width)
            row = lax.broadcasted_iota(jnp.int32, (HALO, width), 0)
            taps = [_taps(w_ref, b_ref, half, lanes, HALO) for half in range(2)]
            acc_w = [[jnp.zeros((HALO, width), F32) for _ in range(3)] for _ in range(2)]
            acc_b = [jnp.zeros((HALO, width), F32) for _ in range(2)]
            da_pair, pending = None, [None, None]
            dc_prev, up_prev = [None, None], [None, None]
            for s, u3, (cg, cv) in _conv_strips(u_ref, ub_ref, ua_ref, taps, lanes, width, N_STRIPS):
                act, dact = _silu_parts(cg)
                if s == N_STRIPS:
                    da = jnp.where(i < N_TOK_TILES - 1, daa_ref[0, :, lanes].astype(F32), 0.0)
                elif s % 2 == 0:
                    da_pair = da_ref[0, s * HALO:(s + 2) * HALO, lanes].astype(F32)
                    da = da_pair[:HALO]
                else:
                    da = da_pair[HALO:]
                dc = (da * cv * dact, da * act)
                for half in range(2):
                    up = [pltpu.roll(dc[half], HALO - k, 0) for k in (1, 2)]
                    if s < N_STRIPS:
                        for j in range(3):
                            acc_w[half][j] = acc_w[half][j] + dc[half] * u3[half][j]
                        acc_b[half] = acc_b[half] + dc[half]
                    if s >= 1:
                        w3 = taps[half][0]
                        du = (dc_prev[half] * w3[2] + jnp.where(row < HALO - 1, up_prev[half][0], up[0]) * w3[1]
                              + jnp.where(row < HALO - 2, up_prev[half][1], up[1]) * w3[0])
                        if (s - 1) % 2 == 0:
                            pending[half] = du
                        else:
                            du_ref[half, (s - 2) * HALO:s * HALO, lanes] = jnp.concatenate([pending[half], du],
                                                                                           axis=0).astype(BF)
                    dc_prev[half], up_prev[half] = dc[half], up
            for half in range(2):
                for j in range(3):
                    dw_ref[half, j:j + 1, lanes] += jnp.sum(acc_w[half][j], axis=0, keepdims=True)
                db_ref[half, :, lanes] += jnp.sum(acc_b[half], axis=0, keepdims=True)

    tile, before, after, vec = _pair_specs(2)
    da_tile, _, da_after_spec, _ = _pair_specs(1)
    return pl.pallas_call(
        body, name="conv_bwd", grid=(4, N_TOK_TILES),
        in_specs=[tile, before, after, da_tile, da_after_spec, vec(3), vec(1)],
        out_specs=[tile, vec(3), vec(1)],
        out_shape=[jax.ShapeDtypeStruct((2, 4, SEQ, FF_BLK), BF), jax.ShapeDtypeStruct((2, 4, 3, FF_BLK), F32),
                   jax.ShapeDtypeStruct((2, 4, 1, FF_BLK), F32)],
        compiler_params=_params("parallel", "arbitrary"),
    )(*map(_in_hbm, (u, u, u, da, da, w_conv, b_conv)))


W_IN_SEGMENTS = ((R_POOL, POOL_WIDTH, "cat", C_POOL), (R_QKV, QKV_W, "cat", C_QKV), (R_OG, D_MODEL, "cat", C_OG),
                 (R_GK, GATE_RANK, "gk", 0), (R_GATE, GATE_W, "cat", C_GATE))


def _slab_pieces(d):
    lo, hi = d * IN_SHARD, (d + 1) * IN_SHARD
    pieces = []
    for start, n, dest, at in W_IN_SEGMENTS:
        a, b = max(lo, start), min(hi, start + n)
        if a < b:
            assert (a - lo) % 2 == 0 and (b - a) % 2 == 0 and (at + a - start) % 2 == 0
            pieces.append(((a - lo) // 2, (b - a) // 2, dest, (at + a - start) // 2))
    return pieces


def _unshard_w_in(slabs):
    def body(slab_ref, cat_ref, gk_ref):
        d = pl.program_id(0)
        src = slab_ref.bitcast(jnp.uint32)
        dst = dict(cat=cat_ref.bitcast(jnp.uint32), gk=gk_ref.bitcast(jnp.uint32))

        @pl.when(d == 0)
        def _():
            gk_ref[...] = jnp.zeros_like(gk_ref)

        for dd in range(N_DEV):
            @pl.when(d == dd)
            def _():
                for a, n, dest, at in _slab_pieces(dd):
                    dst[dest][pl.ds(at, n), :] = src[0, pl.ds(a, n), :]

    return pl.pallas_call(
        body, name="unshard_w_in", grid=(N_DEV,),
        in_specs=[pl.BlockSpec((1, IN_SHARD, D_MODEL), lambda d: (d, 0, 0))],
        out_specs=[_const_spec((N_CAT, D_MODEL)), _const_spec((GK_PAD, D_MODEL))],
        out_shape=[jax.ShapeDtypeStruct((N_CAT, D_MODEL), BF), jax.ShapeDtypeStruct((GK_PAD, D_MODEL), BF)],
        compiler_params=_params("arbitrary"),
    )(_in_hbm(slabs))


def _shard_d_w_in(d_cat, d_gk):
    def body(cat_ref, gk_ref, slab_ref):
        d = pl.program_id(0)
        cat = cat_ref.bitcast(jnp.uint32)
        gk = pltpu.bitcast(gk_ref[0:GATE_RANK, :].astype(BF), jnp.uint32)
        dst = slab_ref.bitcast(jnp.uint32)
        for dd in range(N_DEV):
            @pl.when(d == dd)
            def _():
                for a, n, source, at in _slab_pieces(dd):
                    dst[0, pl.ds(a, n), :] = gk[at:at + n] if source == "gk" else cat[pl.ds(at, n), :]

    return pl.pallas_call(
        body, name="shard_d_w_in", grid=(N_DEV,),
        in_specs=[_const_spec((N_CAT, D_MODEL)), _const_spec((GK_PAD, D_MODEL))],
        out_specs=pl.BlockSpec((1, IN_SHARD, D_MODEL), lambda d: (d, 0, 0)),
        out_shape=jax.ShapeDtypeStruct((N_DEV, IN_SHARD, D_MODEL), BF), compiler_params=_params("parallel"),
    )(_in_hbm(d_cat), _in_hbm(d_gk))


ANY = pl.BlockSpec(memory_space=pl.ANY)


def _place():
    x, y, c = lax.axis_index("x"), lax.axis_index("y"), lax.axis_index("c")
    other_chips = [(1 - x, y), (x, 1 - y), (1 - x, 1 - y)]
    return x, y, c, other_chips


def _all_gather(shards, name):
    n = len(shards)

    def body(*refs):
        src, out = refs[:n], refs[n:2 * n]
        send_sems, recv_sems, local_sems = refs[2 * n:]
        x, y, c, chips = _place()
        me, sibling = (x, y, c), (x, y, 1 - c)

        def copy(a, k, block, to, own=False):
            dst = out[a].at[4 * block[0] + 2 * block[1] + block[2]]
            return pltpu.make_async_remote_copy(src_ref=src[a] if own else dst, dst_ref=dst, send_sem=send_sems.at[a, k],
                                                recv_sem=recv_sems.at[a, k], device_id=to, device_id_type=MESH)

        mine = [pltpu.make_async_copy(src[a], out[a].at[4 * x + 2 * y + c], local_sems.at[a]) for a in range(n)]
        first = []
        for a in range(n):
            mine[a].start()
            first.append(copy(a, 0, me, sibling, own=True))
            first += [copy(a, 1 + j, me, (*chip, c), own=True) for j, chip in enumerate(chips)]
        for cp in first:
            cp.start()
        passed = []
        for j, chip in enumerate(chips):
            for a in range(n):
                copy(a, 1 + j, (*chip, c), me).wait_recv()
                passed.append(copy(a, 4 + j, (*chip, c), sibling))
                passed[-1].start()
        for a in range(n):
            copy(a, 0, sibling, me).wait_recv()
            for j, chip in enumerate(chips):
                copy(a, 4 + j, (*chip, 1 - c), me).wait_recv()
        for cp in first + passed:
            cp.wait_send()
        for cp in mine:
            cp.wait()

    return pl.pallas_call(
        body, name=name, in_specs=[ANY] * n, out_specs=[ANY] * n,
        out_shape=[jax.ShapeDtypeStruct((N_DEV,) + s.shape, s.dtype) for s in shards],
        scratch_shapes=[pltpu.SemaphoreType.DMA((n, 7)), pltpu.SemaphoreType.DMA((n, 7)), pltpu.SemaphoreType.DMA((n,))],
    )(*map(_in_hbm, shards))


SEM = pl.BlockSpec(memory_space=pltpu.SEMAPHORE)
IN_HBM = pl.BlockSpec(memory_space=pltpu.HBM)
SPLIT_PARAMS = pltpu.CompilerParams(has_side_effects=pltpu.SideEffectType.DATAFLOW_SIDE_EFFECTING)


def _gather_first(refs, send_sems, recv_sems):
    x, y, c, chips = _place()
    targets = [(x, y, 1 - c)] + [(px, py, c) for px, py in chips]
    return [pltpu.make_async_remote_copy(src_ref=refs[2 * a], dst_ref=refs[2 * a + 1].at[4 * x + 2 * y + c],
                                         send_sem=send_sems.at[4 * a + k], recv_sem=recv_sems.at[4 * a + k],
                                         device_id=to, device_id_type=MESH)
            for a in range(len(refs) // 2) for k, to in enumerate(targets)]


def _gather_second(refs, send_sems, recv_sems):
    x, y, c, chips = _place()
    copies = []
    for a, land in enumerate(refs):
        for j, (px, py) in enumerate(chips):
            block = land.at[4 * px + 2 * py + c]
            copies.append(pltpu.make_async_remote_copy(src_ref=block, dst_ref=block, send_sem=send_sems.at[3 * a + j],
                                                       recv_sem=recv_sems.at[3 * a + j], device_id=(x, y, 1 - c),
                                                       device_id_type=MESH))
    return copies


def _reduce_first(refs, send_sems, recv_sems):
    x, y, c, _ = _place()
    return [pltpu.make_async_remote_copy(src_ref=refs[2 * a].at[j, 1 - c], dst_ref=refs[2 * a + 1].at[j],
                                         send_sem=send_sems.at[4 * a + j], recv_sem=recv_sems.at[4 * a + j],
                                         device_id=(x, y, 1 - c), device_id_type=MESH)
            for a in range(len(refs) // 2) for j in range(4)]


def _reduce_second(refs, send_sems, recv_sems):
    _, _, c, chips = _place()
    return [pltpu.make_async_remote_copy(src_ref=refs[2 * a].at[2 * px + py], dst_ref=refs[2 * a + 1].at[k],
                                         send_sem=send_sems.at[3 * a + k], recv_sem=recv_sems.at[3 * a + k],
                                         device_id=(px, py, c), device_id_type=MESH)
            for a in range(len(refs) // 2) for k, (px, py) in enumerate(chips)]


def _split_start(name, groups):
    arrays = [a for g in groups for a in g[0]]
    n = len(arrays)

    def body(*refs):
        sems = refs[n:n + 2 * len(groups)]
        at = 0
        for gi, (members, _, build) in enumerate(groups):
            for cp in build(refs[at:at + len(members)], sems[2 * gi], sems[2 * gi + 1]):
                cp.start()
            at += len(members)
        refs[-1][...] = jnp.zeros_like(refs[-1])

    sem_shapes = [pltpu.SemaphoreType.DMA((g[1],)) for g in groups for _ in range(2)]
    outs = pl.pallas_call(
        body, name=name, in_specs=[IN_HBM] * n,
        out_shape=(*sem_shapes, *[pltpu.HBM(a.shape, a.dtype) for a in arrays], jax.ShapeDtypeStruct((8, 128), F32)),
        out_specs=(*[SEM] * len(sem_shapes), *[IN_HBM] * n, pl.BlockSpec(memory_space=pltpu.VMEM)),
        input_output_aliases={i: len(sem_shapes) + i for i in range(n)}, compiler_params=SPLIT_PARAMS,
    )(*[pltpu.with_memory_space_constraint(a, pltpu.HBM) for a in arrays])
    per_group, at = [], len(sem_shapes)
    for gi, (members, _, _) in enumerate(groups):
        per_group.append((outs[2 * gi], outs[2 * gi + 1], list(outs[at:at + len(members)])))
        at += len(members)
    return per_group, outs[-1]


def _split_wait(name, started, build, after):
    send_sems, recv_sems, arrays = started
    n = len(arrays)
    after = after if isinstance(after, (tuple, list)) else (after,)

    def body(*refs):
        for cp in build(refs[:n], refs[n], refs[n + 1]):
            cp.wait_send()
            cp.wait_recv()

    return pl.pallas_call(
        body, name=name, in_specs=[IN_HBM] * n + [SEM, SEM] + [ANY] * len(after),
        out_shape=tuple(pltpu.HBM(a.shape, a.dtype) for a in arrays), out_specs=tuple([IN_HBM] * n),
        input_output_aliases={i: i for i in range(n)}, compiler_params=SPLIT_PARAMS,
    )(*arrays, send_sems, recv_sems, *after)


def _gather_landing(shard, me):
    return lax.dynamic_update_slice(lax.empty((N_DEV,) + shard.shape, shard.dtype), shard[None],
                                    (me,) + (0,) * shard.ndim)


def _tile_2d(rows, cols):
    for t in (256, 176, 128):
        if rows % t == 0:
            return t, cols
    return rows, 256


def _pair_sum(part, recv, core, name):
    _, rows, cols = recv.shape
    tr, tc = rows, cols

    def body(c_ref, p_ref, r_ref, o_ref):
        del c_ref
        o_ref[...] = (p_ref[...].astype(F32) + r_ref[...].astype(F32)).astype(BF)

    grid_spec = pltpu.PrefetchScalarGridSpec(
        num_scalar_prefetch=1, grid=(4, rows // tr, cols // tc),
        in_specs=[pl.BlockSpec((None, None, tr, tc), lambda j, i, k, c_ref: (j, c_ref[0], i, k)),
                  pl.BlockSpec((None, tr, tc), lambda j, i, k, c_ref: (j, i, k))],
        out_specs=pl.BlockSpec((None, tr, tc), lambda j, i, k, c_ref: (j, i, k)))
    return pl.pallas_call(
        body, name=name, grid_spec=grid_spec, out_shape=jax.ShapeDtypeStruct(recv.shape, BF),
        compiler_params=_params("parallel", "parallel", "parallel"),
    )(core, *map(_in_hbm, (part, recv)))


def _adamw(w, g, m, v):
    m = ADAM_B1 * m + (1.0 - ADAM_B1) * g
    v = ADAM_B2 * v + (1.0 - ADAM_B2) * (g * g)
    delta = -ADAM_LR * ((m / ADAM_C1) / (jnp.sqrt(v / ADAM_C2) + ADAM_EPS) + ADAM_WD * w)
    return delta, m, v


def _chip_sum_adamw(sums, recv, w, m, v, chip, name):
    rows, cols = w.shape
    tr, tc = _tile_2d(rows, cols)

    def body(chip_ref, s_ref, r_ref, w_ref, m_ref, v_ref, g_out, d_out, m_out, v_out):
        del chip_ref
        g = s_ref[...].astype(F32)
        for k in range(3):
            g = g + r_ref[k].astype(F32)
        g_out[...] = g
        d_out[...], m_out[...], v_out[...] = _adamw(w_ref[...], g, m_ref[...], v_ref[...])

    tile = pl.BlockSpec((tr, tc), lambda i, k, chip_ref: (i, k))
    grid_spec = pltpu.PrefetchScalarGridSpec(
        num_scalar_prefetch=1, grid=(rows // tr, cols // tc),
        in_specs=[pl.BlockSpec((None, tr, tc), lambda i, k, chip_ref: (chip_ref[0], i, k)),
                  pl.BlockSpec((3, tr, tc), lambda i, k, chip_ref: (0, i, k)), tile, tile, tile],
        out_specs=[tile] * 4)
    return pl.pallas_call(
        body, name=name, grid_spec=grid_spec, out_shape=[jax.ShapeDtypeStruct((rows, cols), F32)] * 4,
        compiler_params=_params("parallel", "parallel"),
    )(chip, *map(_in_hbm, (sums, recv, w, m, v)))


def _small_sum_adamw(me, entries, loss_parts):
    def whole(shape, squeeze=0, pick=False):
        blk = (None,) * squeeze + tuple(shape[squeeze:])
        if pick:
            blk = (shape[0], None) + tuple(shape[2:])
            return pl.BlockSpec(blk, lambda i, me_ref: (0, me_ref[0]) + (0,) * (len(shape) - 2))
        return pl.BlockSpec(blk, lambda i, me_ref: (0,) * len(shape))

    in_specs, out_specs, out_shape, args = [], [], [], []
    for parts, w, m, v, sharded in entries:
        lead = w.ndim - (parts.ndim - (2 if sharded else 1))
        in_specs += [whole(parts.shape, pick=sharded)] + [whole(w.shape, squeeze=lead)] * 3
        out_specs += [whole(w.shape, squeeze=lead)] * 4
        out_shape += [jax.ShapeDtypeStruct(w.shape, F32)] * 4
        args += [parts, w, m, v]
    in_specs.append(whole(loss_parts.shape))
    out_specs.append(whole(loss_parts.shape[1:]))
    out_shape.append(jax.ShapeDtypeStruct(loss_parts.shape[1:], F32))
    n = len(entries)

    def added(p_ref):
        total = p_ref[0]
        for d in range(1, N_DEV):
            total = total + p_ref[d]
        return total

    def body(me_ref, *refs):
        del me_ref
        ins, outs = refs[:4 * n + 1], refs[4 * n + 1:]
        for e in range(n):
            p_ref, w_ref, m_ref, v_ref = ins[4 * e:4 * e + 4]
            g_out, d_out, m_out, v_out = outs[4 * e:4 * e + 4]
            g = added(p_ref)
            g_out[...] = g
            d_out[...], m_out[...], v_out[...] = _adamw(w_ref[...], g, m_ref[...], v_ref[...])
        outs[4 * n][...] = added(ins[4 * n])

    grid_spec = pltpu.PrefetchScalarGridSpec(num_scalar_prefetch=1, grid=(1,), in_specs=in_specs, out_specs=out_specs)
    outs = pl.pallas_call(body, name="small_sum_adamw", grid_spec=grid_spec, out_shape=out_shape,
                          compiler_params=_params("arbitrary"))(me, *map(_in_hbm, args + [loss_parts]))
    return [outs[4 * e:4 * e + 4] for e in range(n)], outs[4 * n]


MM_TILE = 512
N_MM_TILES = SEQ // MM_TILE
CAT_TILE = 512
N_CAT_TILES = N_CAT // CAT_TILE
SMALL_ROWS = 808
SHARD_ROWS = 32


def kernel(x, g_mix, w_in, b_gate, w_gk_up, b_gk, w_pool_grp, pool_scale, g_gla_head, w_pool_proj, w_gla_proj, w_out, g_ffn, w_up, w_conv, b_conv, w_down, g_final, loss_target, m_g_mix, m_w_in, m_b_gate, m_w_gk_up, m_b_gk, m_w_pool_grp, m_pool_scale, m_g_gla_head, m_w_pool_proj, m_w_gla_proj, m_w_out, m_g_ffn, m_w_up, m_w_conv, m_b_conv, m_w_down, m_g_final, v_g_mix, v_w_in, v_b_gate, v_w_gk_up, v_b_gk, v_w_pool_grp, v_pool_scale, v_g_gla_head, v_w_pool_proj, v_w_gla_proj, v_w_out, v_g_ffn, v_w_up, v_w_conv, v_b_conv, v_w_down, v_g_final):
    xi, yi, ci = lax.axis_index("x"), lax.axis_index("y"), lax.axis_index("c")
    me = 4 * xi + 2 * yi + ci
    core = jnp.reshape(ci, (1,)).astype(jnp.int32)
    chip = jnp.reshape(2 * xi + yi, (1,)).astype(jnp.int32)
    xs, target = x[0], loss_target[0]

    big = dict(w_in=w_in[0].T, w_pool_proj=w_pool_proj[0], w_gla_proj=w_gla_proj[0], w_out=w_out[0], w_up=w_up[0].T,
               w_down=w_down[0])
    moments = dict(w_in=(m_w_in[0].T, v_w_in[0].T), w_pool_proj=(m_w_pool_proj[0], v_w_pool_proj[0]),
                   w_gla_proj=(m_w_gla_proj[0], v_w_gla_proj[0]), w_out=(m_w_out[0], v_w_out[0]),
                   w_up=(m_w_up[0].T, v_w_up[0].T), w_down=(m_w_down[0], v_w_down[0]))
    names = list(big)
    shards = {k: big[k].astype(BF) for k in names}
    shards["w_gk_up"], shards["w_conv"] = w_gk_up[0], w_conv[0]
    gather_groups = (("w_in", "w_gk_up"), ("w_pool_proj", "w_gla_proj", "w_out"), ("w_up", "w_down", "w_conv"))
    started, token = _split_start("gather_start", [
        ([t for k in g for t in (shards[k], _gather_landing(shards[k], me))], 4 * len(g), _gather_first)
        for g in gather_groups])

    def gather_pass(gi, after):
        lands = list(_split_wait(f"gather_wait_{gi}", started[gi], _gather_first, after)[1::2])
        passed, tkn = _split_start(f"gather_pass_{gi}", [(lands, 3 * len(lands), _gather_second)])
        return passed[0], tkn

    def gather_done(gi, passed, after):
        return dict(zip(gather_groups[gi], _split_wait(f"gather_pass_wait_{gi}", passed, _gather_second, after)))

    tok = lambda i, j, k: (i, 0)
    whole = lambda i, j, k: (0, 0)
    kblk = lambda i, j, k: (k, 0)
    ff_tile = (None, None, MM_TILE, FF_BLK)
    ff_seq = (None, None, SEQ, FF_BLK)

    h = _rms_fwd(xs, g_mix + token[:1, :1], "rms_mix")
    wg = gather_done(0, gather_pass(0, h)[0], h)
    wt_cat, wt_gk = _unshard_w_in(wg["w_in"])
    wgk_pad = jnp.pad(wg["w_gk_up"].transpose(1, 0, 2).reshape(GATE_RANK, GLA_DK), ((0, GK_PAD - GATE_RANK), (0, 0)))
    zcat = _mm(h, wt_cat, out_shape=(SEQ, N_CAT), out_dtype=F32, grid=(N_CAT_TILES, 1, 1),
               blk_a=(SEQ, D_MODEL), blk_b=(CAT_TILE, D_MODEL), blk_o=(SEQ, CAT_TILE),
               map_a=whole, map_b=lambda j, i, k: (j, 0), map_o=lambda j, i, k: (0, j), tb=True, name="mm_in")
    la = _gk_fwd(h, wt_gk, wgk_pad, b_gk)
    passed, tkn = gather_pass(1, la)
    o, states = _gla_fwd(zcat, la, tkn)
    wg = gather_done(1, passed, o)
    wpp = wg["w_pool_proj"].transpose(1, 0, 2).reshape(POOL_WIDTH, D_MODEL)
    wgp = wg["w_gla_proj"].reshape(D_MODEL, D_MODEL)
    wout = wg["w_out"].reshape(D_MODEL, D_MODEL)
    og = _post_gla_fwd(o, zcat, g_gla_head)
    ps = _pool_fwd(zcat, w_pool_grp[0], pool_scale)
    passed, tkn = gather_pass(2, (og, ps))
    y_pool, y_gla, mixed, x1, h2 = _mix_out_fwd(ps, og, zcat, xs, wpp, wgp, wout, b_gate, g_ffn, tkn)
    wg = gather_done(2, passed, h2)
    wt_up = wg["w_up"].reshape(2 * D_FF, D_MODEL)
    wdown = wg["w_down"].reshape(D_FF, D_MODEL)
    wconv4 = wg["w_conv"].reshape(2, 4, 3, FF_BLK)
    bconv4 = b_conv.reshape(2, 4, 1, FF_BLK)
    blk4 = lambda b, i, k: (b // 4, b % 4, 0, 0)
    u4 = _mm(h2, wt_up, out_shape=(2, 4, SEQ, FF_BLK), out_dtype=F32, grid=(N_DEV, 1, 1),
             blk_a=(SEQ, D_MODEL), blk_b=(FF_BLK, D_MODEL), blk_o=ff_seq,
             map_a=whole, map_b=lambda b, i, k: (b, 0), map_o=blk4, tb=True, name="mm_up")
    act = _conv_fwd(u4, wconv4, bconv4)
    loss_part, dx2, dx2_bf, dg_final = _mm_tokens(
        act, wdown, blk_a=(None, 4, TOK_MM_TILE, FF_BLK), map_a=lambda i: (0, 0, i, 0),
        pieces=[(b, b * FF_BLK, FF_BLK) for b in range(4)], res=x1, then=("loss", g_final.reshape(1, D_MODEL), target),
        name="mm_down_loss")

    da = _mm(dx2_bf, wdown, out_shape=(1, 4, SEQ, FF_BLK), out_dtype=BF, grid=(4, 1, 1),
             blk_a=(SEQ, D_MODEL), blk_b=(FF_BLK, D_MODEL), blk_o=ff_seq,
             map_a=whole, map_b=lambda b, i, k: (b, 0), map_o=lambda b, i, k: (0, b, 0, 0), tb=True, name="mm_d_act")
    d_wdown = _mm(act, dx2_bf, out_shape=(D_FF, D_MODEL), out_dtype=BF, grid=(4, 1, 1),
                  blk_a=ff_seq, blk_b=(SEQ, D_MODEL), blk_o=(FF_BLK, D_MODEL),
                  map_a=lambda b, i, k: (0, b, 0, 0), map_b=whole, map_o=lambda b, i, k: (b, 0), ta=True,
                  name="mm_d_wdown")
    du4, d_wconv, d_bconv = _conv_bwd(u4, da, wconv4, bconv4)
    d_wt_up = _mm(du4, h2, out_shape=(2 * D_FF, D_MODEL), out_dtype=BF, grid=(N_DEV, 1, 1),
                  blk_a=ff_seq, blk_b=(SEQ, D_MODEL), blk_o=(FF_BLK, D_MODEL),
                  map_a=blk4, map_b=whole, map_o=lambda b, i, k: (b, 0), ta=True, name="mm_d_wup")
    res = {}

    def reduce_start(keys, parts):
        arrays = [t for k in keys for t in (parts[k], lax.empty((4,) + parts[k].shape[2:], BF))]
        st, tkn = _split_start("reduce_start_" + keys[0], [(arrays, 4 * len(keys), _reduce_first)])
        return st[0], tkn

    def reduce_cross(keys, st, after):
        arrays = _split_wait("reduce_wait_" + keys[0], st, _reduce_first, after)
        sums = [_pair_sum(p, r, core, "pair_sum_" + k) for k, p, r in zip(keys, arrays[0::2], arrays[1::2])]
        arrays = [t for s in sums for t in (s, lax.empty((3,) + s.shape[1:], BF))]
        st2, tkn = _split_start("reduce_cross_" + keys[0], [(arrays, 3 * len(keys), _reduce_second)])
        return st2[0], tkn

    def reduce_done(keys, st2, after):
        arrays = _split_wait("reduce_cross_wait_" + keys[0], st2, _reduce_second, after)
        for k, s, r in zip(keys, arrays[0::2], arrays[1::2]):
            outs = _chip_sum_adamw(s, r, big[k], moments[k][0], moments[k][1], chip, "adamw_" + k)
            res[k] = [(t.T if k in ("w_in", "w_up") else t)[None] for t in outs]

    ffn_keys = ("w_down", "w_up")
    ffn_red, tkn = reduce_start(ffn_keys, dict(w_down=d_wdown.reshape(4, 2, D_FF // N_DEV, D_MODEL),
                                               w_up=d_wt_up.reshape(4, 2, FF_BLK, D_MODEL)))
    dx1, dg_ffn = _mm_tokens(
        du4, wt_up, blk_a=(2, 4, TOK_MM_TILE, FF_BLK), map_a=lambda i: (0, 0, i, 0),
        pieces=[((b // 4, b % 4), b * FF_BLK, FF_BLK) for b in range(N_DEV)], after=tkn, then=("rms_bwd", x1, g_ffn, dx2),
        name="mm_d_h2_rms")

    sq_t = dict(out_shape=(D_MODEL, D_MODEL), grid=(1, 1, N_MM_TILES), blk_a=(MM_TILE, D_MODEL),
                blk_b=(MM_TILE, D_MODEL), blk_o=(D_MODEL, D_MODEL), map_a=kblk, map_b=kblk, map_o=whole, ta=True)
    d_wout = _mm(mixed, dx1, out_dtype=BF, name="mm_d_wout", **sq_t)
    dzcat, dy_pool, dy_gla, db_gate = _mix_bwd(dx1, wout, zcat, b_gate, y_pool, y_gla)
    ffn_red, _ = reduce_cross(ffn_keys, ffn_red, db_gate)
    d_wgp = _mm(og, dy_gla, out_dtype=BF, name="mm_d_wgp", **sq_t)
    mix_keys = ("w_out", "w_gla_proj")
    mix_red, tkn = reduce_start(mix_keys, dict(w_out=d_wout.reshape(4, 2, D_MODEL // N_DEV, D_MODEL),
                                               w_gla_proj=d_wgp.reshape(4, 2, D_MODEL // N_DEV, D_MODEL)))
    dzcat, d_o, dg_head = _post_gla_bwd(dzcat, dy_gla, wgp, o, zcat, g_gla_head + tkn[:1, :1])
    dzcat, dla = _gla_bwd(dzcat, zcat, la, d_o, states)
    mix_red, tkn = reduce_cross(mix_keys, mix_red, dla)
    dh_gk, d_wt_gk, d_wgk, db_gk = _gk_bwd(dla, h, wt_gk, wgk_pad, b_gk + tkn[:1, :1])
    dps = _mm(dy_pool, wpp, out_shape=(SEQ, POOL_WIDTH), out_dtype=F32, grid=(N_MM_TILES, 1, 1),
              blk_a=(MM_TILE, D_MODEL), blk_b=(POOL_WIDTH, D_MODEL), blk_o=(MM_TILE, POOL_WIDTH),
              map_a=tok, map_b=whole, map_o=tok, tb=True, name="mm_d_ps")
    d_wpp = _mm(ps, dy_pool, out_shape=(POOL_WIDTH, D_MODEL), out_dtype=F32, grid=(1, 1, N_MM_TILES),
                blk_a=(MM_TILE, POOL_WIDTH), blk_b=(MM_TILE, D_MODEL), blk_o=(POOL_WIDTH, D_MODEL),
                map_a=kblk, map_b=kblk, map_o=whole, ta=True, name="mm_d_wpp")
    dzcat, d_wgrp, d_scale = _pool_bwd(dzcat, zcat, dps, w_pool_grp[0], pool_scale)
    d_wt_cat = _mm(dzcat, h, out_shape=(N_CAT, D_MODEL), out_dtype=BF, grid=(N_CAT_TILES, 1, 1),
                   blk_a=(SEQ, CAT_TILE), blk_b=(SEQ, D_MODEL), blk_o=(CAT_TILE, D_MODEL),
                   map_a=lambda j, i, k: (0, j), map_b=whole, map_o=lambda j, i, k: (j, 0), ta=True, name="mm_d_wcat")
    in_keys = ("w_in", "w_pool_proj")
    in_red, tkn = reduce_start(in_keys, dict(
        w_in=_shard_d_w_in(d_wt_cat, d_wt_gk).reshape(4, 2, IN_SHARD, D_MODEL),
        w_pool_proj=d_wpp.reshape(POOL_WIDTH, N_DEV, D_MODEL // N_DEV).transpose(1, 0, 2).astype(BF)
        .reshape(4, 2, POOL_WIDTH, D_MODEL // N_DEV)))
    in_red, tkn = reduce_cross(in_keys, in_red, tkn)
    grad_x, dg_mix = _mm_tokens(dzcat, wt_cat, blk_a=(TOK_MM_TILE, N_CAT), map_a=lambda i: (i, 0),
                                pieces=[(None, 0, N_CAT)], res=dh_gk, after=tkn, then=("rms_bwd", xs, g_mix, dx1),
                                name="mm_d_h_rms")
    reduce_done(ffn_keys, ffn_red, grad_x)
    reduce_done(mix_keys, mix_red, res["w_down"][0])

    row = lambda t: t.reshape(1, D_MODEL)
    conv_vec = lambda t: t.reshape(2, 4, 1, FF_BLK)
    small = [("g_mix", dg_mix, g_mix, m_g_mix, v_g_mix, False), ("b_gate", db_gate, b_gate, m_b_gate, v_b_gate, False),
             ("w_gk_up", d_wgk.reshape(GATE_RANK, N_DEV, GLA_DK // N_DEV).transpose(1, 0, 2), w_gk_up, m_w_gk_up,
              v_w_gk_up, True),
             ("b_gk", db_gk, b_gk, m_b_gk, v_b_gk, False),
             ("w_pool_grp", d_wgrp, w_pool_grp, m_w_pool_grp, v_w_pool_grp, False),
             ("pool_scale", d_scale, pool_scale, m_pool_scale, v_pool_scale, False),
             ("g_gla_head", dg_head, g_gla_head, m_g_gla_head, v_g_gla_head, False),
             ("g_ffn", dg_ffn, g_ffn, m_g_ffn, v_g_ffn, False),
             ("w_conv", d_wconv.reshape(N_DEV, 3, FF_BLK), w_conv, m_w_conv, v_w_conv, True),
             ("b_conv", d_bconv, conv_vec(b_conv), conv_vec(m_b_conv), conv_vec(v_b_conv), False),
             ("g_final", dg_final, row(g_final), row(m_g_final), row(v_g_final), False)]
    gathered = _all_gather([t[1] for t in small] + [loss_part], "gather_small_grads")
    small_out, loss_sum = _small_sum_adamw(jnp.reshape(me, (1,)).astype(jnp.int32),
                                           [(p,) + t[2:] for p, t in zip(gathered, small)], gathered[-1])
    for t, outs in zip(small, small_out):
        res[t[0]] = list(outs)
    res["b_conv"] = [t.reshape(b_conv.shape) for t in res["b_conv"]]
    res["g_final"] = [t.reshape(g_final.shape) for t in res["g_final"]]

    reduce_done(in_keys, in_red, loss_sum)
    loss = loss_sum[0, 0]
    order =["g_mix", "w_in", "b_gate", "w_gk_up", "b_gk", "w_pool_grp", "pool_scale", "g_gla_head", "w_pool_proj",
             "w_gla_proj", "w_out", "g_ffn", "w_up", "w_conv", "b_conv", "w_down", "g_final"]
    return (loss, grad_x[None], *[res[k][0] for k in order], *[res[k][1] for k in order],
            *[res[k][2] for k in order], *[res[k][3] for k in order])
```

```python
import jax
import jax.numpy as jnp
from jax import lax
from jax.experimental import pallas as pl
from jax.experimental.pallas import tpu as pltpu

F32 = jnp.float32
BF = jnp.bfloat16
HIGHEST = lax.Precision.HIGHEST
MESH = pl.DeviceIdType.MESH

N_DEV = 8
SEQ = 2048
D_MODEL = 1024
CHUNK = 64
EPS = 1e-6
POOL_WIDTH = 512
POOL_WINDOWS = (2, 4, 8, 16)
POOL_GD = 128
POOL_HALO = 16
HEADS = 4
HK = 128
HV = 256
GLA_DK = 512
GATE_RANK = 16
GATE_NORM = 16.0
D_FF = 2816
FF_BLK = 704
IN_SHARD = 706
C_QKV, C_GATE, C_OG, C_POOL = 0, 2048, 4096, 5120
N_CAT = 5632
R_POOL, R_QKV, R_OG, R_GK, R_GATE = 0, 512, 2560, 3584, 3600
GK_PAD = 128

ADAM_LR, ADAM_B1, ADAM_B2, ADAM_EPS, ADAM_WD, ADAM_STEP = 0.001, 0.9, 0.999, 1e-08, 0.01, 10
ADAM_C1 = 1.0 - ADAM_B1 ** ADAM_STEP
ADAM_C2 = 1.0 - ADAM_B2 ** ADAM_STEP

VMEM_BYTES_V7X = 64 * 1024 * 1024
VMEM_LIMIT = VMEM_BYTES_V7X * 3 // 4

TOK_TILE = 256
HALO = 8
GLA_CPS = 4


def _params(*sem):
    return pltpu.CompilerParams(dimension_semantics=sem, vmem_limit_bytes=VMEM_LIMIT)


def _const_spec(shape):
    nd = len(shape)
    return pl.BlockSpec(shape, lambda *_: (0,) * nd)


def _in_hbm(t):
    return pltpu.with_memory_space_constraint(t, pltpu.HBM)


def _dot(a, b, ta=False, tb=False):
    dims = (((0 if ta else 1,), (1 if tb else 0,)), ((), ()))
    return lax.dot_general(a.astype(BF), b.astype(BF), dims, preferred_element_type=F32)


def _dot_exact(a, b):
    return jnp.dot(a, b, precision=HIGHEST, preferred_element_type=F32)


def _sigmoid(x):
    return 0.5 * jnp.tanh(0.5 * x) + 0.5


def _mm(a, b, *, out_shape, out_dtype, grid, blk_a, blk_b, blk_o, map_a, map_b, map_o, ta=False, tb=False,
        res=None, name):
    gk = grid[2]

    def body(*refs):
        if res is None:
            a_ref, b_ref, o_ref = refs[:3]
            r_ref = None
            scr = refs[3:]
        else:
            a_ref, b_ref, r_ref, o_ref = refs[:4]
            scr = refs[4:]
        prod = _dot(a_ref[...], b_ref[...], ta, tb)

        def finish(total):
            if r_ref is not None:
                total = total + r_ref[...]
            o_ref[...] = total.astype(out_dtype)

        if gk == 1:
            finish(prod)
        else:
            acc = scr[0]
            k = pl.program_id(2)

            @pl.when(k == 0)
            def _():
                acc[...] = prod

            @pl.when(k > 0)
            def _():
                acc[...] += prod

            @pl.when(k == gk - 1)
            def _():
                finish(acc[...])

    in_specs = [pl.BlockSpec(blk_a, map_a), pl.BlockSpec(blk_b, map_b)]
    args = [a, b]
    if res is not None:
        in_specs.append(pl.BlockSpec(blk_o, map_o))
        args.append(res)
    return pl.pallas_call(
        body, name=name, grid=grid, in_specs=in_specs, out_specs=pl.BlockSpec(blk_o, map_o),
        out_shape=jax.ShapeDtypeStruct(out_shape, out_dtype),
        scratch_shapes=[] if gk == 1 else [pltpu.VMEM(tuple(d for d in blk_o if d is not None), F32)],
        compiler_params=_params("parallel", "parallel", "arbitrary"),
    )(*[_in_hbm(t) for t in args])


TOK_MM_TILE = 256


def _mm_tokens(a, w, *, blk_a, map_a, pieces, res=None, after=None, then=None, name):
    n_in = 2 + (res is not None) + (after is not None) + (0 if then is None else len(then) - 1)

    def accumulate(ref, part):
        @pl.when(pl.program_id(0) == 0)
        def _():
            ref[...] = part

        @pl.when(pl.program_id(0) > 0)
        def _():
            ref[...] += part

    def body(*refs):
        a_ref, w_ref = refs[:2]
        extra, outs = refs[n_in - (0 if then is None else len(then) - 1):n_in], refs[n_in:]
        total = None
        for idx, row, n in pieces:
            av = a_ref[...] if idx is None else a_ref[idx]
            prod = _dot(av, w_ref[row:row + n, :])
            total = prod if total is None else total + prod
        if res is not None:
            total = total + refs[2][...]
        if then is None:
            outs[0][...] = total
        elif then[0] == "rms_bwd":
            dx, part = _rms_bwd_tile(total, extra[0][...], extra[1][...], extra[2][...])
            outs[0][...] = dx
            accumulate(outs[1], part)
        else:
            lpart, dx, part = _loss_tile(total, extra[0][...], extra[1][...])
            outs[1][...] = dx
            outs[2][...] = dx.astype(BF)
            accumulate(outs[0], lpart)
            accumulate(outs[3], part)

    tile = pl.BlockSpec((TOK_MM_TILE, D_MODEL), lambda i: (i, 0))
    vec = _const_spec((1, D_MODEL))
    big = jax.ShapeDtypeStruct((SEQ, D_MODEL), F32)
    small = jax.ShapeDtypeStruct((1, D_MODEL), F32)
    in_specs = [pl.BlockSpec(blk_a, map_a), pl.BlockSpec(w.shape, lambda i: (0, 0), pipeline_mode=pl.Buffered(1))]
    args = [a, w]
    if res is not None:
        in_specs.append(tile)
        args.append(res)
    if after is not None:
        in_specs.append(pl.BlockSpec(memory_space=pl.ANY))
        args.append(after)
    if then is None:
        out_specs, out_shape = tile, big
    elif then[0] == "rms_bwd":
        in_specs += [tile, vec, tile]
        out_specs, out_shape = [tile, vec], [big, small]
    else:
        in_specs += [vec, tile]
        out_specs = [_const_spec((1, 128)), tile, tile, vec]
        out_shape = [jax.ShapeDtypeStruct((1, 128), F32), big, jax.ShapeDtypeStruct((SEQ, D_MODEL), BF), small]
    if then is not None:
        args += list(then[1:])
    return pl.pallas_call(
        body, name=name, grid=(SEQ // TOK_MM_TILE,), in_specs=in_specs, out_specs=out_specs, out_shape=out_shape,
        compiler_params=_params("parallel" if then is None else "arbitrary"),
    )(*[_in_hbm(t) for t in args])


def _rms_fwd(x, g, name):
    def body(x_ref, g_ref, o_ref):
        xv = x_ref[...]
        r = lax.rsqrt(jnp.mean(xv * xv, axis=-1, keepdims=True) + EPS)
        o_ref[...] = (xv * r * g_ref[...]).astype(BF)

    tile = pl.BlockSpec((TOK_TILE, D_MODEL), lambda i: (i, 0))
    return pl.pallas_call(
        body, name=name, grid=(SEQ // TOK_TILE,), in_specs=[tile, _const_spec((1, D_MODEL))], out_specs=tile,
        out_shape=jax.ShapeDtypeStruct((SEQ, D_MODEL), BF), compiler_params=_params("parallel"),
    )(*map(_in_hbm, (x, g)))


def _rms_bwd_tile(dyv, xv, gv, dresv):
    r = lax.rsqrt(jnp.mean(xv * xv, axis=-1, keepdims=True) + EPS)
    xn = xv * r
    dxn = dyv * gv
    return dresv + r * (dxn - xn * jnp.mean(dxn * xn, axis=-1, keepdims=True)), jnp.sum(dyv * xn, axis=0, keepdims=True)


def _loss_tile(xv, gv, tv):
    r = lax.rsqrt(jnp.mean(xv * xv, axis=-1, keepdims=True) + EPS)
    xn = xv * r
    err = xn * gv - tv
    lpart = jnp.full((1, 128), 0.5 * jnp.sum(jnp.mean(err * err, axis=-1, keepdims=True)), F32)
    dyv = err * (1.0 / D_MODEL)
    dxn = dyv * gv
    return lpart, r * (dxn - xn * jnp.mean(dxn * xn, axis=-1, keepdims=True)), jnp.sum(dyv * xn, axis=0, keepdims=True)


def _pool_counts(w):
    pos = lax.broadcasted_iota(jnp.int32, (SEQ, 1), 0).astype(F32)
    return jnp.minimum(pos + 1.0, float(w))


def _pool_window(u, w, ext):
    ext[pl.ds(POOL_HALO, SEQ), :] = u
    win = u
    for j in range(1, w):
        win = win + ext[pl.ds(POOL_HALO - j, SEQ), :]
    return win / _pool_counts(w) - u


def _pool_fwd(zcat, w_grp, scale):
    def body(z_ref, w_ref, s_ref, o_ref, ext):
        ext[pl.ds(0, POOL_HALO), :] = jnp.zeros((POOL_HALO, POOL_GD), F32)
        for g, w in enumerate(POOL_WINDOWS):
            cols = slice(g * POOL_GD, (g + 1) * POOL_GD)
            p = _pool_window(z_ref[:, cols], w, ext)
            o_ref[:, cols] = (_dot(p, w_ref[g]) * s_ref[:, cols]).astype(BF)

    return pl.pallas_call(
        body, name="pool_fwd", grid=(1,),
        in_specs=[pl.BlockSpec((SEQ, POOL_WIDTH), lambda i: (0, C_POOL // POOL_WIDTH)),
                  _const_spec((4, POOL_GD, POOL_GD)), _const_spec((1, POOL_WIDTH))],
        out_specs=_const_spec((SEQ, POOL_WIDTH)), out_shape=jax.ShapeDtypeStruct((SEQ, POOL_WIDTH), BF),
        scratch_shapes=[pltpu.VMEM((POOL_HALO + SEQ, POOL_GD), F32)], compiler_params=_params("arbitrary"),
    )(*map(_in_hbm, (zcat, w_grp, scale)))


def _pool_bwd(dzcat, zcat, dps, w_grp, scale):
    def body(dz_in, z_ref, dps_ref, w_ref, s_ref, dz_ref, dw_ref, dsc_ref, ext, ext2):
        del dz_in
        ext[pl.ds(0, POOL_HALO), :] = jnp.zeros((POOL_HALO, POOL_GD), F32)
        ext2[pl.ds(SEQ, POOL_HALO), :] = jnp.zeros((POOL_HALO, POOL_GD), F32)
        for g, w in enumerate(POOL_WINDOWS):
            cols = slice(g * POOL_GD, (g + 1) * POOL_GD)
            p = _pool_window(z_ref[:, cols], w, ext)
            wg = w_ref[g]
            pg = _dot(p, wg)
            dpsv = dps_ref[:, cols]
            dsc_ref[:, cols] = jnp.sum(dpsv * pg, axis=0, keepdims=True)
            dpg = dpsv * s_ref[:, cols]
            dw_ref[g] = _dot(p, dpg, ta=True)
            dp = _dot(dpg, wg, tb=True)
            dpc = dp / _pool_counts(w)
            ext2[pl.ds(0, SEQ), :] = dpc
            du = dpc
            for j in range(1, w):
                du = du + ext2[pl.ds(j, SEQ), :]
            dz_ref[:, cols] = (du - dp).astype(BF)

    return pl.pallas_call(
        body, name="pool_bwd", grid=(1,),
        in_specs=[pl.BlockSpec(memory_space=pl.ANY),
                  pl.BlockSpec((SEQ, POOL_WIDTH), lambda i: (0, C_POOL // POOL_WIDTH)),
                  _const_spec((SEQ, POOL_WIDTH)), _const_spec((4, POOL_GD, POOL_GD)), _const_spec((1, POOL_WIDTH))],
        out_specs=[pl.BlockSpec((SEQ, POOL_WIDTH), lambda i: (0, C_POOL // POOL_WIDTH)),
                   _const_spec((4, POOL_GD, POOL_GD)), _const_spec((1, POOL_WIDTH))],
        out_shape=[jax.ShapeDtypeStruct((SEQ, N_CAT), BF), jax.ShapeDtypeStruct((4, POOL_GD, POOL_GD), F32),
                   jax.ShapeDtypeStruct((1, POOL_WIDTH), F32)],
        scratch_shapes=[pltpu.VMEM((POOL_HALO + SEQ, POOL_GD), F32), pltpu.VMEM((SEQ + POOL_HALO, POOL_GD), F32)],
        input_output_aliases={0: 0}, compiler_params=_params("arbitrary"),
    )(*map(_in_hbm, (dzcat, zcat, dps, w_grp, scale)))


GK_TILE = 512


def _gk_fwd(h, wt_gk, wgk_pad, b_gk):
    def body(h_ref, wt_ref, w_ref, b_ref, la_ref):
        z_gk = _dot(h_ref[...], wt_ref[...], tb=True)
        pre = _dot(z_gk, w_ref[...]) + b_ref[...]
        la_ref[...] = (jnp.minimum(pre, 0.0) - jnp.log(1.0 + jnp.exp(-jnp.abs(pre)))) * (1.0 / GATE_NORM)

    return pl.pallas_call(
        body, name="gk_fwd", grid=(SEQ // GK_TILE,),
        in_specs=[pl.BlockSpec((GK_TILE, D_MODEL), lambda i: (i, 0)), _const_spec((GK_PAD, D_MODEL)),
                  _const_spec((GK_PAD, GLA_DK)), _const_spec((1, GLA_DK))],
        out_specs=pl.BlockSpec((GK_TILE, GLA_DK), lambda i: (i, 0)),
        out_shape=jax.ShapeDtypeStruct((SEQ, GLA_DK), F32), compiler_params=_params("parallel"),
    )(*map(_in_hbm, (h, wt_gk, wgk_pad, b_gk)))


def _gk_bwd(dla, h, wt_gk, wgk_pad, b_gk):
    def body(dla_ref, h_ref, wt_ref, w_ref, b_ref, dh_ref, dwt_ref, dw_ref, db_ref):
        hv = h_ref[...]
        wtv = wt_ref[...]
        wv = w_ref[...]
        z_gk = _dot(hv, wtv, tb=True)
        pre = _dot(z_gk, wv) + b_ref[...]
        dpre = dla_ref[...] * (1.0 / GATE_NORM) * (1.0 - _sigmoid(pre))
        dz_gk = _dot(dpre, wv, tb=True)
        dh_ref[...] = _dot(dz_gk, wtv)
        dwtp = _dot(dz_gk, hv, ta=True)
        dwp = _dot(z_gk, dpre, ta=True)[:GATE_RANK]
        dbp = jnp.sum(dpre, axis=0, keepdims=True)

        @pl.when(pl.program_id(0) == 0)
        def _():
            dwt_ref[...] = dwtp
            dw_ref[...] = dwp
            db_ref[...] = dbp

        @pl.when(pl.program_id(0) > 0)
        def _():
            dwt_ref[...] += dwtp
            dw_ref[...] += dwp
            db_ref[...] += dbp

    tile = pl.BlockSpec((GK_TILE, D_MODEL), lambda i: (i, 0))
    return pl.pallas_call(
        body, name="gk_bwd", grid=(SEQ // GK_TILE,),
        in_specs=[pl.BlockSpec((GK_TILE, GLA_DK), lambda i: (i, 0)), tile, _const_spec((GK_PAD, D_MODEL)),
                  _const_spec((GK_PAD, GLA_DK)), _const_spec((1, GLA_DK))],
        out_specs=[tile, _const_spec((GK_PAD, D_MODEL)), _const_spec((GATE_RANK, GLA_DK)), _const_spec((1, GLA_DK))],
        out_shape=[jax.ShapeDtypeStruct((SEQ, D_MODEL), F32), jax.ShapeDtypeStruct((GK_PAD, D_MODEL), F32),
                   jax.ShapeDtypeStruct((GATE_RANK, GLA_DK), F32), jax.ShapeDtypeStruct((1, GLA_DK), F32)],
        compiler_params=_params("arbitrary"),
    )(*map(_in_hbm, (dla, h, wt_gk, wgk_pad, b_gk)))


GLA_ROWS = GLA_CPS * CHUNK
GLA_STEPS = SEQ // GLA_ROWS
QKV_W = 2048


def _tri():
    return lax.broadcasted_iota(jnp.int32, (CHUNK, CHUNK), 0) >= lax.broadcasted_iota(jnp.int32, (CHUNK, CHUNK), 1)


def _chunk_cumsum(la_ref, rows):
    return _dot_exact(_tri().astype(F32), la_ref[rows, :])


def _gla_chunk(qkv_ref, la_ref, rows, h, bc_all):
    tri = _tri()
    q = qkv_ref[rows, h * HK:(h + 1) * HK] * (HK ** -0.5)
    k = qkv_ref[rows, GLA_DK + h * HK:GLA_DK + (h + 1) * HK]
    v = qkv_ref[rows, 2 * GLA_DK + h * HV:2 * GLA_DK + (h + 1) * HV].astype(BF)
    la = la_ref[rows, h * HK:(h + 1) * HK]
    bc = bc_all[:, h * HK:(h + 1) * HK]
    e_pos, e_neg = jnp.exp(bc), jnp.exp(-bc)
    dl = jnp.exp(jnp.sum(la, axis=0, keepdims=True))
    q_fw, q_bw, k_fw, k_bw = q * e_pos, q * e_neg, k * e_neg, k * e_pos
    scores = jnp.where(tri, _dot(q_fw, k_fw, tb=True), _dot(q_bw, k_bw, tb=True))
    return tri, v, e_pos, e_neg, dl, q_fw, q_bw, k_fw, k_bw, scores


def _gla_fwd(zcat, la, after):
    def body(qkv_ref, la_ref, after_ref, o_ref, st_ref, state):
        del after_ref

        @pl.when(pl.program_id(0) == 0)
        def _():
            state[...] = jnp.zeros_like(state)

        for c in range(GLA_CPS):
            rows = slice(c * CHUNK, (c + 1) * CHUNK)
            bc_all = _chunk_cumsum(la_ref, rows)
            for h in range(HEADS):
                _, v, _, _, dl, q_fw, _, k_fw, _, scores = _gla_chunk(qkv_ref, la_ref, rows, h, bc_all)
                st = state[h]
                st_ref[c, h] = st
                o_ref[rows, h * HV:(h + 1) * HV] = _dot(scores, v) + _dot(q_fw, st, tb=True)
                state[h] = st * dl + _dot(v, k_fw * dl, ta=True)

    return pl.pallas_call(
        body, name="gla_fwd", grid=(GLA_STEPS,),
        in_specs=[pl.BlockSpec((GLA_ROWS, QKV_W), lambda i: (i, 0)), pl.BlockSpec((GLA_ROWS, GLA_DK), lambda i: (i, 0)),
                  pl.BlockSpec(memory_space=pl.ANY)],
        out_specs=[pl.BlockSpec((GLA_ROWS, D_MODEL), lambda i: (i, 0)),
                   pl.BlockSpec((GLA_CPS, HEADS, HV, HK), lambda i: (i, 0, 0, 0))],
        out_shape=[jax.ShapeDtypeStruct((SEQ, D_MODEL), F32),
                   jax.ShapeDtypeStruct((SEQ // CHUNK, HEADS, HV, HK), F32)],
        scratch_shapes=[pltpu.VMEM((HEADS, HV, HK), F32)], compiler_params=_params("arbitrary"),
    )(*map(_in_hbm, (zcat, la)), after)


def _gla_bwd(dzcat, zcat, la, d_o, states):
    def body(dz_in, qkv_ref, la_ref, do_ref, st_ref, dqkv_ref, dla_ref, dstate):
        del dz_in

        @pl.when(pl.program_id(0) == 0)
        def _():
            dstate[...] = jnp.zeros_like(dstate)

        last_row = lax.broadcasted_iota(jnp.int32, (CHUNK, HK), 0) == CHUNK - 1
        upper = (lax.broadcasted_iota(jnp.int32, (CHUNK, CHUNK), 0)
                 <= lax.broadcasted_iota(jnp.int32, (CHUNK, CHUNK), 1)).astype(F32)
        for c in reversed(range(GLA_CPS)):
            rows = slice(c * CHUNK, (c + 1) * CHUNK)
            bc_all = _chunk_cumsum(la_ref, rows)
            dbs = []
            for h in range(HEADS):
                tri, v, e_pos, e_neg, dl, q_fw, q_bw, k_fw, k_bw, scores = _gla_chunk(qkv_ref, la_ref, rows, h, bc_all)
                st = st_ref[c, h]
                dst = dstate[h]
                d_out = do_ref[rows, h * HV:(h + 1) * HV].astype(BF)
                k_dec = k_fw * dl
                dp = _dot(d_out, v, tb=True)
                dp_fw = jnp.where(tri, dp, 0.0)
                dp_bw = jnp.where(tri, 0.0, dp)
                dv = _dot(scores, d_out, ta=True) + _dot(k_dec, dst, tb=True)
                dk_dec = _dot(v, dst)
                dq_fw = _dot(dp_fw, k_fw) + _dot(d_out, st)
                dk_fw = _dot(dp_fw, q_fw, ta=True) + dk_dec * dl
                dq_bw = _dot(dp_bw, k_bw)
                dk_bw = _dot(dp_bw, q_bw, ta=True)
                ddl = jnp.sum(st * dst, axis=0, keepdims=True) + jnp.sum(k_fw * dk_dec, axis=0, keepdims=True)
                dstate[h] = dst * dl + _dot(d_out, q_fw, ta=True)
                dq = (dq_fw * e_pos + dq_bw * e_neg) * (HK ** -0.5)
                dk = dk_fw * e_neg + dk_bw * e_pos
                dbs.append(dq_fw * q_fw - dk_fw * k_fw - dq_bw * q_bw + dk_bw * k_bw + jnp.where(last_row, ddl * dl, 0.0))
                dqkv_ref[rows, h * HK:(h + 1) * HK] = dq.astype(BF)
                dqkv_ref[rows, GLA_DK + h * HK:GLA_DK + (h + 1) * HK] = dk.astype(BF)
                dqkv_ref[rows, 2 * GLA_DK + h * HV:2 * GLA_DK + (h + 1) * HV] = dv.astype(BF)
            dla_ref[rows, :] = _dot_exact(upper, jnp.concatenate(dbs, axis=1))

    rev = lambda i: (GLA_STEPS - 1 - i, 0)
    return pl.pallas_call(
        body, name="gla_bwd", grid=(GLA_STEPS,),
        in_specs=[pl.BlockSpec(memory_space=pl.ANY), pl.BlockSpec((GLA_ROWS, QKV_W), rev),
                  pl.BlockSpec((GLA_ROWS, GLA_DK), rev), pl.BlockSpec((GLA_ROWS, D_MODEL), rev),
                  pl.BlockSpec((GLA_CPS, HEADS, HV, HK), lambda i: (GLA_STEPS - 1 - i, 0, 0, 0))],
        out_specs=[pl.BlockSpec((GLA_ROWS, QKV_W), rev), pl.BlockSpec((GLA_ROWS, GLA_DK), rev)],
        out_shape=[jax.ShapeDtypeStruct((SEQ, N_CAT), BF), jax.ShapeDtypeStruct((SEQ, GLA_DK), F32)],
        scratch_shapes=[pltpu.VMEM((HEADS, HV, HK), F32)], input_output_aliases={0: 0},
        compiler_params=_params("arbitrary"),
    )(*map(_in_hbm, (dzcat, zcat, la, d_o, states)))


def _silu_parts(x):
    s = _sigmoid(x)
    return x * s, s * (1.0 + x * (1.0 - s))


def _post_gla_fwd(o, zcat, g_head):
    def body(o_ref, zog_ref, g_ref, out_ref):
        for h in range(HEADS):
            cols = slice(h * HV, (h + 1) * HV)
            ov = o_ref[:, cols]
            r = lax.rsqrt(jnp.mean(ov * ov, axis=-1, keepdims=True) + EPS)
            act, _ = _silu_parts(zog_ref[:, cols])
            out_ref[:, cols] = (ov * r * g_ref[...] * act).astype(BF)

    tile = pl.BlockSpec((TOK_TILE, D_MODEL), lambda i: (i, 0))
    return pl.pallas_call(
        body, name="post_gla_fwd", grid=(SEQ // TOK_TILE,),
        in_specs=[tile, pl.BlockSpec((TOK_TILE, D_MODEL), lambda i: (i, C_OG // D_MODEL)), _const_spec((1, HV))],
        out_specs=tile, out_shape=jax.ShapeDtypeStruct((SEQ, D_MODEL), BF), compiler_params=_params("parallel"),
    )(*map(_in_hbm, (o, zcat, g_head)))


def _post_gla_bwd(dzcat, dy_gla, w_gla_proj, o, zcat, g_head):
    def body(dz_in, dyg_ref, w_ref, o_ref, zog_ref, g_ref, dz_ref, do_ref, dg_ref):
        del dz_in
        dog = _dot(dyg_ref[...], w_ref[...], tb=True)
        gpart = jnp.zeros((1, HV), F32)
        gv = g_ref[...]
        for h in range(HEADS):
            cols = slice(h * HV, (h + 1) * HV)
            ov = o_ref[:, cols]
            r = lax.rsqrt(jnp.mean(ov * ov, axis=-1, keepdims=True) + EPS)
            on = ov * r
            act, dact = _silu_parts(zog_ref[:, cols])
            dogv = dog[:, cols]
            dz_ref[:, cols] = (dogv * on * gv * dact).astype(BF)
            d_on_g = dogv * act
            gpart = gpart + jnp.sum(d_on_g * on, axis=0, keepdims=True)
            dxn = d_on_g * gv
            do_ref[:, cols] = r * (dxn - on * jnp.mean(dxn * on, axis=-1, keepdims=True))

        @pl.when(pl.program_id(0) == 0)
        def _():
            dg_ref[...] = gpart

        @pl.when(pl.program_id(0) > 0)
        def _():
            dg_ref[...] += gpart

    tile = pl.BlockSpec((TOK_TILE, D_MODEL), lambda i: (i, 0))
    ogspec = pl.BlockSpec((TOK_TILE, D_MODEL), lambda i: (i, C_OG // D_MODEL))
    return pl.pallas_call(
        body, name="post_gla_bwd", grid=(SEQ // TOK_TILE,),
        in_specs=[pl.BlockSpec(memory_space=pl.ANY), tile, _const_spec((D_MODEL, D_MODEL)), tile, ogspec,
                  _const_spec((1, HV))],
        out_specs=[ogspec, tile, _const_spec((1, HV))],
        out_shape=[jax.ShapeDtypeStruct((SEQ, N_CAT), BF), jax.ShapeDtypeStruct((SEQ, D_MODEL), F32),
                   jax.ShapeDtypeStruct((1, HV), F32)],
        input_output_aliases={0: 0}, compiler_params=_params("arbitrary"),
    )(*map(_in_hbm, (dzcat, dy_gla, w_gla_proj, o, zcat, g_head)))


GATE_W = 2 * D_MODEL


def _mix_out_fwd(ps, og, zcat, x, w_pool_proj, w_gla_proj, w_out, b_gate, g_ffn, after):
    def body(ps_ref, og_ref, zg_ref, x_ref, wpp_ref, wgp_ref, wout_ref, b_ref, g_ref, after_ref,
             yp_ref, yg_ref, mixed_ref, x1_ref, h2_ref):
        del after_ref
        y_pool = _dot(ps_ref[...], wpp_ref[...])
        y_gla = _dot(og_ref[...], wgp_ref[...])
        yp_ref[...] = y_pool
        yg_ref[...] = y_gla
        g0 = _sigmoid(zg_ref[:, :D_MODEL] + b_ref[:, :D_MODEL])
        g1 = _sigmoid(zg_ref[:, D_MODEL:] + b_ref[:, D_MODEL:])
        mixed = (g0 * y_pool + g1 * y_gla).astype(BF)
        mixed_ref[...] = mixed
        x1 = x_ref[...] + _dot(mixed, wout_ref[...])
        x1_ref[...] = x1
        r = lax.rsqrt(jnp.mean(x1 * x1, axis=-1, keepdims=True) + EPS)
        h2_ref[...] = (x1 * r * g_ref[...]).astype(BF)

    tile = pl.BlockSpec((TOK_TILE, D_MODEL), lambda i: (i, 0))
    resident = lambda shape: pl.BlockSpec(shape, lambda i: (0, 0), pipeline_mode=pl.Buffered(1))
    f32, bf16 = jax.ShapeDtypeStruct((SEQ, D_MODEL), F32), jax.ShapeDtypeStruct((SEQ, D_MODEL), BF)
    return pl.pallas_call(
        body, name="mix_out_fwd", grid=(SEQ // TOK_TILE,),
        in_specs=[pl.BlockSpec((TOK_TILE, POOL_WIDTH), lambda i: (i, 0)), tile,
                  pl.BlockSpec((TOK_TILE, GATE_W), lambda i: (i, C_GATE // GATE_W)), tile,
                  resident((POOL_WIDTH, D_MODEL)), resident((D_MODEL, D_MODEL)), resident((D_MODEL, D_MODEL)),
                  _const_spec((1, GATE_W)), _const_spec((1, D_MODEL)), pl.BlockSpec(memory_space=pl.ANY)],
        out_specs=[tile] * 5, out_shape=[f32, f32, bf16, f32, bf16], compiler_params=_params("parallel"),
    )(*map(_in_hbm, (ps, og, zcat, x, w_pool_proj, w_gla_proj, w_out, b_gate, g_ffn)), after)


def _mix_bwd(dx1, w_out, zcat, b_gate, y_pool, y_gla):
    def body(dx_ref, w_ref, zg_ref, b_ref, yp_ref, yg_ref, dz_ref, dyp_ref, dyg_ref, db_ref):
        dm = _dot(dx_ref[...], w_ref[...], tb=True)
        g0 = _sigmoid(zg_ref[:, :D_MODEL] + b_ref[:, :D_MODEL])
        g1 = _sigmoid(zg_ref[:, D_MODEL:] + b_ref[:, D_MODEL:])
        dyp_ref[...] = (dm * g0).astype(BF)
        dyg_ref[...] = (dm * g1).astype(BF)
        dz0 = dm * yp_ref[...] * g0 * (1.0 - g0)
        dz1 = dm * yg_ref[...] * g1 * (1.0 - g1)
        dz_ref[:, :D_MODEL] = dz0.astype(BF)
        dz_ref[:, D_MODEL:] = dz1.astype(BF)
        b0 = jnp.sum(dz0, axis=0, keepdims=True)
        b1 = jnp.sum(dz1, axis=0, keepdims=True)

        @pl.when(pl.program_id(0) == 0)
        def _():
            db_ref[:, :D_MODEL] = b0
            db_ref[:, D_MODEL:] = b1

        @pl.when(pl.program_id(0) > 0)
        def _():
            db_ref[:, :D_MODEL] += b0
            db_ref[:, D_MODEL:] += b1

    tile = pl.BlockSpec((TOK_TILE, D_MODEL), lambda i: (i, 0))
    gspec = pl.BlockSpec((TOK_TILE, GATE_W), lambda i: (i, C_GATE // GATE_W))
    return pl.pallas_call(
        body, name="mix_bwd", grid=(SEQ // TOK_TILE,),
        in_specs=[tile, _const_spec((D_MODEL, D_MODEL)), gspec, _const_spec((1, GATE_W)), tile, tile],
        out_specs=[gspec, tile, tile, _const_spec((1, GATE_W))],
        out_shape=[jax.ShapeDtypeStruct((SEQ, N_CAT), BF), jax.ShapeDtypeStruct((SEQ, D_MODEL), BF),
                   jax.ShapeDtypeStruct((SEQ, D_MODEL), BF), jax.ShapeDtypeStruct((1, GATE_W), F32)],
        compiler_params=_params("arbitrary"),
    )(*map(_in_hbm, (dx1, w_out, zcat, b_gate, y_pool, y_gla)))


N_TOK_TILES = SEQ // TOK_TILE
HALO_PER_TILE = TOK_TILE // HALO


LANE_TILES = tuple((lo, min(128, FF_BLK - lo)) for lo in range(0, FF_BLK, 128))


def _taps(w_ref, b_ref, half, lanes, rows):
    shape = (rows, lanes.stop - lanes.start)
    return ([jnp.broadcast_to(w_ref[half, j:j + 1, lanes], shape) for j in range(3)],
            jnp.broadcast_to(b_ref[half, :, lanes], shape))


def _conv_strips(u_ref, ub_ref, ua_ref, taps, lanes, width, n_strips):
    first = pl.program_id(1) == 0
    row = lax.broadcasted_iota(jnp.int32, (HALO, width), 0)
    prev = [[pltpu.roll(jnp.where(first, 0.0, ub_ref[half, :, lanes]), k, 0) for k in (1, 2)] for half in range(2)]
    for s in range(n_strips + (ua_ref is not None)):
        u3, conv = [], []
        for half in range(2):
            cur = u_ref[half, s * HALO:(s + 1) * HALO, lanes] if s < n_strips else ua_ref[half, :, lanes]
            rolled = [pltpu.roll(cur, k, 0) for k in (1, 2)]
            frames = [jnp.where(row >= 2, rolled[1], prev[half][1]), jnp.where(row >= 1, rolled[0], prev[half][0]), cur]
            prev[half] = rolled
            w3, bias = taps[half]
            u3.append(frames)
            conv.append(bias + frames[0] * w3[0] + frames[1] * w3[1] + frames[2] * w3[2])
        yield s, u3, conv


def _pair_specs(pairs):
    tile = pl.BlockSpec((pairs, None, TOK_TILE, FF_BLK), lambda b, i: (0, b, i, 0))
    before = pl.BlockSpec((pairs, None, HALO, FF_BLK), lambda b, i: (0, b, jnp.maximum(i * HALO_PER_TILE - 1, 0), 0))
    after = pl.BlockSpec((pairs, None, HALO, FF_BLK),
                         lambda b, i: (0, b, jnp.minimum((i + 1) * HALO_PER_TILE, SEQ // HALO - 1), 0))

    def vec(rows):
        return pl.BlockSpec((2, None, rows, FF_BLK), lambda b, i: (0, b, 0, 0))

    return tile, before, after, vec


N_STRIPS = TOK_TILE // HALO


def _conv_fwd(u, w_conv, b_conv):
    def body(u_ref, ub_ref, w_ref, b_ref, a_ref):
        for lo, width in LANE_TILES:
            lanes = slice(lo, lo + width)
            taps = [_taps(w_ref, b_ref, half, lanes, HALO) for half in range(2)]
            pending = None
            for s, _, (cg, cv) in _conv_strips(u_ref, ub_ref, None, taps, lanes, width, N_STRIPS):
                act = cg * _sigmoid(cg) * cv
                if s % 2 == 0:
                    pending = act
                else:
                    a_ref[0, (s - 1) * HALO:(s + 1) * HALO, lanes] = jnp.concatenate([pending, act], axis=0).astype(BF)

    tile, before, _, vec = _pair_specs(2)
    out_tile, _, _, _ = _pair_specs(1)
    return pl.pallas_call(
        body, name="conv_fwd", grid=(4, N_TOK_TILES), in_specs=[tile, before, vec(3), vec(1)],
        out_specs=out_tile, out_shape=jax.ShapeDtypeStruct((1, 4, SEQ, FF_BLK), BF),
        compiler_params=_params("parallel", "parallel"),
    )(*map(_in_hbm, (u, u, w_conv, b_conv)))


def _conv_bwd(u, da, w_conv, b_conv):
    def body(u_ref, ub_ref, ua_ref, da_ref, daa_ref, w_ref, b_ref, du_ref, dw_ref, db_ref):
        i = pl.program_id(1)

        @pl.when(i == 0)
        def _():
            dw_ref[...] = jnp.zeros_like(dw_ref)
            db_ref[...] = jnp.zeros_like(db_ref)

        for lo, width in LANE_TILES:
            lanes = slice(lo, lo + width)
            row = lax.broadcasted_iota(jnp.int32, (HALO, width), 0)
            taps = [_taps(w_ref, b_ref, half, lanes, HALO) for half in range(2)]
            acc_w = [[jnp.zeros((HALO, width), F32) for _ in range(3)] for _ in range(2)]
            acc_b = [jnp.zeros((HALO, width), F32) for _ in range(2)]
            da_pair, pending = None, [None, None]
            dc_prev, up_prev = [None, None], [None, None]
            for s, u3, (cg, cv) in _conv_strips(u_ref, ub_ref, ua_ref, taps, lanes, width, N_STRIPS):
                act, dact = _silu_parts(cg)
                if s == N_STRIPS:
                    da = jnp.where(i < N_TOK_TILES - 1, daa_ref[0, :, lanes].astype(F32), 0.0)
                elif s % 2 == 0:
                    da_pair = da_ref[0, s * HALO:(s + 2) * HALO, lanes].astype(F32)
                    da = da_pair[:HALO]
                else:
                    da = da_pair[HALO:]
                dc = (da * cv * dact, da * act)
                for half in range(2):
                    up = [pltpu.roll(dc[half], HALO - k, 0) for k in (1, 2)]
                    if s < N_STRIPS:
                        for j in range(3):
                            acc_w[half][j] = acc_w[half][j] + dc[half] * u3[half][j]
                        acc_b[half] = acc_b[half] + dc[half]
                    if s >= 1:
                        w3 = taps[half][0]
                        du = (dc_prev[half] * w3[2] + jnp.where(row < HALO - 1, up_prev[half][0], up[0]) * w3[1]
                              + jnp.where(row < HALO - 2, up_prev[half][1], up[1]) * w3[0])
                        if (s - 1) % 2 == 0:
                            pending[half] = du
                        else:
                            du_ref[half, (s - 2) * HALO:s * HALO, lanes] = jnp.concatenate([pending[half], du],
                                                                                           axis=0).astype(BF)
                    dc_prev[half], up_prev[half] = dc[half], up
            for half in range(2):
                for j in range(3):
                    dw_ref[half, j:j + 1, lanes] += jnp.sum(acc_w[half][j], axis=0, keepdims=True)
                db_ref[half, :, lanes] += jnp.sum(acc_b[half], axis=0, keepdims=True)

    tile, before, after, vec = _pair_specs(2)
    da_tile, _, da_after_spec, _ = _pair_specs(1)
    return pl.pallas_call(
        body, name="conv_bwd", grid=(4, N_TOK_TILES),
        in_specs=[tile, before, after, da_tile, da_after_spec, vec(3), vec(1)],
        out_specs=[tile, vec(3), vec(1)],
        out_shape=[jax.ShapeDtypeStruct((2, 4, SEQ, FF_BLK), BF), jax.ShapeDtypeStruct((2, 4, 3, FF_BLK), F32),
                   jax.ShapeDtypeStruct((2, 4, 1, FF_BLK), F32)],
        compiler_params=_params("parallel", "arbitrary"),
    )(*map(_in_hbm, (u, u, u, da, da, w_conv, b_conv)))


W_IN_SEGMENTS = ((R_POOL, POOL_WIDTH, "cat", C_POOL), (R_QKV, QKV_W, "cat", C_QKV), (R_OG, D_MODEL, "cat", C_OG),
                 (R_GK, GATE_RANK, "gk", 0), (R_GATE, GATE_W, "cat", C_GATE))


def _slab_pieces(d):
    lo, hi = d * IN_SHARD, (d + 1) * IN_SHARD
    pieces = []
    for start, n, dest, at in W_IN_SEGMENTS:
        a, b = max(lo, start), min(hi, start + n)
        if a < b:
            assert (a - lo) % 2 == 0 and (b - a) % 2 == 0 and (at + a - start) % 2 == 0
            pieces.append(((a - lo) // 2, (b - a) // 2, dest, (at + a - start) // 2))
    return pieces


def _unshard_w_in(slabs):
    def body(slab_ref, cat_ref, gk_ref):
        d = pl.program_id(0)
        src = slab_ref.bitcast(jnp.uint32)
        dst = dict(cat=cat_ref.bitcast(jnp.uint32), gk=gk_ref.bitcast(jnp.uint32))

        @pl.when(d == 0)
        def _():
            gk_ref[...] = jnp.zeros_like(gk_ref)

        for dd in range(N_DEV):
            @pl.when(d == dd)
            def _():
                for a, n, dest, at in _slab_pieces(dd):
                    dst[dest][pl.ds(at, n), :] = src[0, pl.ds(a, n), :]

    return pl.pallas_call(
        body, name="unshard_w_in", grid=(N_DEV,),
        in_specs=[pl.BlockSpec((1, IN_SHARD, D_MODEL), lambda d: (d, 0, 0))],
        out_specs=[_const_spec((N_CAT, D_MODEL)), _const_spec((GK_PAD, D_MODEL))],
        out_shape=[jax.ShapeDtypeStruct((N_CAT, D_MODEL), BF), jax.ShapeDtypeStruct((GK_PAD, D_MODEL), BF)],
        compiler_params=_params("arbitrary"),
    )(_in_hbm(slabs))


def _shard_d_w_in(d_cat, d_gk):
    def body(cat_ref, gk_ref, slab_ref):
        d = pl.program_id(0)
        cat = cat_ref.bitcast(jnp.uint32)
        gk = pltpu.bitcast(gk_ref[0:GATE_RANK, :].astype(BF), jnp.uint32)
        dst = slab_ref.bitcast(jnp.uint32)
        for dd in range(N_DEV):
            @pl.when(d == dd)
            def _():
                for a, n, source, at in _slab_pieces(dd):
                    dst[0, pl.ds(a, n), :] = gk[at:at + n] if source == "gk" else cat[pl.ds(at, n), :]

    return pl.pallas_call(
        body, name="shard_d_w_in", grid=(N_DEV,),
        in_specs=[_const_spec((N_CAT, D_MODEL)), _const_spec((GK_PAD, D_MODEL))],
        out_specs=pl.BlockSpec((1, IN_SHARD, D_MODEL), lambda d: (d, 0, 0)),
        out_shape=jax.ShapeDtypeStruct((N_DEV, IN_SHARD, D_MODEL), BF), compiler_params=_params("parallel"),
    )(_in_hbm(d_cat), _in_hbm(d_gk))


ANY = pl.BlockSpec(memory_space=pl.ANY)


def _place():
    x, y, c = lax.axis_index("x"), lax.axis_index("y"), lax.axis_index("c")
    other_chips = [(1 - x, y), (x, 1 - y), (1 - x, 1 - y)]
    return x, y, c, other_chips


def _all_gather(shards, name):
    n = len(shards)

    def body(*refs):
        src, out = refs[:n], refs[n:2 * n]
        send_sems, recv_sems, local_sems = refs[2 * n:]
        x, y, c, chips = _place()
        me, sibling = (x, y, c), (x, y, 1 - c)

        def copy(a, k, block, to, own=False):
            dst = out[a].at[4 * block[0] + 2 * block[1] + block[2]]
            return pltpu.make_async_remote_copy(src_ref=src[a] if own else dst, dst_ref=dst, send_sem=send_sems.at[a, k],
                                                recv_sem=recv_sems.at[a, k], device_id=to, device_id_type=MESH)

        mine = [pltpu.make_async_copy(src[a], out[a].at[4 * x + 2 * y + c], local_sems.at[a]) for a in range(n)]
        first = []
        for a in range(n):
            mine[a].start()
            first.append(copy(a, 0, me, sibling, own=True))
            first += [copy(a, 1 + j, me, (*chip, c), own=True) for j, chip in enumerate(chips)]
        for cp in first:
            cp.start()
        passed = []
        for j, chip in enumerate(chips):
            for a in range(n):
                copy(a, 1 + j, (*chip, c), me).wait_recv()
                passed.append(copy(a, 4 + j, (*chip, c), sibling))
                passed[-1].start()
        for a in range(n):
            copy(a, 0, sibling, me).wait_recv()
            for j, chip in enumerate(chips):
                copy(a, 4 + j, (*chip, 1 - c), me).wait_recv()
        for cp in first + passed:
            cp.wait_send()
        for cp in mine:
            cp.wait()

    return pl.pallas_call(
        body, name=name, in_specs=[ANY] * n, out_specs=[ANY] * n,
        out_shape=[jax.ShapeDtypeStruct((N_DEV,) + s.shape, s.dtype) for s in shards],
        scratch_shapes=[pltpu.SemaphoreType.DMA((n, 7)), pltpu.SemaphoreType.DMA((n, 7)), pltpu.SemaphoreType.DMA((n,))],
    )(*map(_in_hbm, shards))


SEM = pl.BlockSpec(memory_space=pltpu.SEMAPHORE)
IN_HBM = pl.BlockSpec(memory_space=pltpu.HBM)
SPLIT_PARAMS = pltpu.CompilerParams(has_side_effects=pltpu.SideEffectType.DATAFLOW_SIDE_EFFECTING)


def _gather_first(refs, send_sems, recv_sems):
    x, y, c, chips = _place()
    targets = [(x, y, 1 - c)] + [(px, py, c) for px, py in chips]
    return [pltpu.make_async_remote_copy(src_ref=refs[2 * a], dst_ref=refs[2 * a + 1].at[4 * x + 2 * y + c],
                                         send_sem=send_sems.at[4 * a + k], recv_sem=recv_sems.at[4 * a + k],
                                         device_id=to, device_id_type=MESH)
            for a in range(len(refs) // 2) for k, to in enumerate(targets)]


def _gather_second(refs, send_sems, recv_sems):
    x, y, c, chips = _place()
    copies = []
    for a, land in enumerate(refs):
        for j, (px, py) in enumerate(chips):
            block = land.at[4 * px + 2 * py + c]
            copies.append(pltpu.make_async_remote_copy(src_ref=block, dst_ref=block, send_sem=send_sems.at[3 * a + j],
                                                       recv_sem=recv_sems.at[3 * a + j], device_id=(x, y, 1 - c),
                                                       device_id_type=MESH))
    return copies


def _reduce_first(refs, send_sems, recv_sems):
    x, y, c, _ = _place()
    return [pltpu.make_async_remote_copy(src_ref=refs[2 * a].at[j, 1 - c], dst_ref=refs[2 * a + 1].at[j],
                                         send_sem=send_sems.at[4 * a + j], recv_sem=recv_sems.at[4 * a + j],
                                         device_id=(x, y, 1 - c), device_id_type=MESH)
            for a in range(len(refs) // 2) for j in range(4)]


def _reduce_second(refs, send_sems, recv_sems):
    _, _, c, chips = _place()
    return [pltpu.make_async_remote_copy(src_ref=refs[2 * a].at[2 * px + py], dst_ref=refs[2 * a + 1].at[k],
                                         send_sem=send_sems.at[3 * a + k], recv_sem=recv_sems.at[3 * a + k],
                                         device_id=(px, py, c), device_id_type=MESH)
            for a in range(len(refs) // 2) for k, (px, py) in enumerate(chips)]


def _split_start(name, groups):
    arrays = [a for g in groups for a in g[0]]
    n = len(arrays)

    def body(*refs):
        sems = refs[n:n + 2 * len(groups)]
        at = 0
        for gi, (members, _, build) in enumerate(groups):
            for cp in build(refs[at:at + len(members)], sems[2 * gi], sems[2 * gi + 1]):
                cp.start()
            at += len(members)
        refs[-1][...] = jnp.zeros_like(refs[-1])

    sem_shapes = [pltpu.SemaphoreType.DMA((g[1],)) for g in groups for _ in range(2)]
    outs = pl.pallas_call(
        body, name=name, in_specs=[IN_HBM] * n,
        out_shape=(*sem_shapes, *[pltpu.HBM(a.shape, a.dtype) for a in arrays], jax.ShapeDtypeStruct((8, 128), F32)),
        out_specs=(*[SEM] * len(sem_shapes), *[IN_HBM] * n, pl.BlockSpec(memory_space=pltpu.VMEM)),
        input_output_aliases={i: len(sem_shapes) + i for i in range(n)}, compiler_params=SPLIT_PARAMS,
    )(*[pltpu.with_memory_space_constraint(a, pltpu.HBM) for a in arrays])
    per_group, at = [], len(sem_shapes)
    for gi, (members, _, _) in enumerate(groups):
        per_group.append((outs[2 * gi], outs[2 * gi + 1], list(outs[at:at + len(members)])))
        at += len(members)
    return per_group, outs[-1]


def _split_wait(name, started, build, after):
    send_sems, recv_sems, arrays = started
    n = len(arrays)
    after = after if isinstance(after, (tuple, list)) else (after,)

    def body(*refs):
        for cp in build(refs[:n], refs[n], refs[n + 1]):
            cp.wait_send()
            cp.wait_recv()

    return pl.pallas_call(
        body, name=name, in_specs=[IN_HBM] * n + [SEM, SEM] + [ANY] * len(after),
        out_shape=tuple(pltpu.HBM(a.shape, a.dtype) for a in arrays), out_specs=tuple([IN_HBM] * n),
        input_output_aliases={i: i for i in range(n)}, compiler_params=SPLIT_PARAMS,
    )(*arrays, send_sems, recv_sems, *after)


def _gather_landing(shard, me):
    return lax.dynamic_update_slice(lax.empty((N_DEV,) + shard.shape, shard.dtype), shard[None],
                                    (me,) + (0,) * shard.ndim)


def _tile_2d(rows, cols):
    for t in (256, 176, 128):
        if rows % t == 0:
            return t, cols
    return rows, 256


def _pair_sum(part, recv, core, name):
    _, rows, cols = recv.shape
    tr, tc = rows, cols

    def body(c_ref, p_ref, r_ref, o_ref):
        del c_ref
        o_ref[...] = (p_ref[...].astype(F32) + r_ref[...].astype(F32)).astype(BF)

    grid_spec = pltpu.PrefetchScalarGridSpec(
        num_scalar_prefetch=1, grid=(4, rows // tr, cols // tc),
        in_specs=[pl.BlockSpec((None, None, tr, tc), lambda j, i, k, c_ref: (j, c_ref[0], i, k)),
                  pl.BlockSpec((None, tr, tc), lambda j, i, k, c_ref: (j, i, k))],
        out_specs=pl.BlockSpec((None, tr, tc), lambda j, i, k, c_ref: (j, i, k)))
    return pl.pallas_call(
        body, name=name, grid_spec=grid_spec, out_shape=jax.ShapeDtypeStruct(recv.shape, BF),
        compiler_params=_params("parallel", "parallel", "parallel"),
    )(core, *map(_in_hbm, (part, recv)))


def _adamw(w, g, m, v):
    m = ADAM_B1 * m + (1.0 - ADAM_B1) * g
    v = ADAM_B2 * v + (1.0 - ADAM_B2) * (g * g)
    delta = -ADAM_LR * ((m / ADAM_C1) / (jnp.sqrt(v / ADAM_C2) + ADAM_EPS) + ADAM_WD * w)
    return delta, m, v


def _chip_sum_adamw(sums, recv, w, m, v, chip, name):
    rows, cols = w.shape
    tr, tc = _tile_2d(rows, cols)

    def body(chip_ref, s_ref, r_ref, w_ref, m_ref, v_ref, g_out, d_out, m_out, v_out):
        del chip_ref
        g = s_ref[...].astype(F32)
        for k in range(3):
            g = g + r_ref[k].astype(F32)
        g_out[...] = g
        d_out[...], m_out[...], v_out[...] = _adamw(w_ref[...], g, m_ref[...], v_ref[...])

    tile = pl.BlockSpec((tr, tc), lambda i, k, chip_ref: (i, k))
    grid_spec = pltpu.PrefetchScalarGridSpec(
        num_scalar_prefetch=1, grid=(rows // tr, cols // tc),
        in_specs=[pl.BlockSpec((None, tr, tc), lambda i, k, chip_ref: (chip_ref[0], i, k)),
                  pl.BlockSpec((3, tr, tc), lambda i, k, chip_ref: (0, i, k)), tile, tile, tile],
        out_specs=[tile] * 4)
    return pl.pallas_call(
        body, name=name, grid_spec=grid_spec, out_shape=[jax.ShapeDtypeStruct((rows, cols), F32)] * 4,
        compiler_params=_params("parallel", "parallel"),
    )(chip, *map(_in_hbm, (sums, recv, w, m, v)))


def _small_sum_adamw(me, entries, loss_parts):
    def whole(shape, squeeze=0, pick=False):
        blk = (None,) * squeeze + tuple(shape[squeeze:])
        if pick:
            blk = (shape[0], None) + tuple(shape[2:])
            return pl.BlockSpec(blk, lambda i, me_ref: (0, me_ref[0]) + (0,) * (len(shape) - 2))
        return pl.BlockSpec(blk, lambda i, me_ref: (0,) * len(shape))

    in_specs, out_specs, out_shape, args = [], [], [], []
    for parts, w, m, v, sharded in entries:
        lead = w.ndim - (parts.ndim - (2 if sharded else 1))
        in_specs += [whole(parts.shape, pick=sharded)] + [whole(w.shape, squeeze=lead)] * 3
        out_specs += [whole(w.shape, squeeze=lead)] * 4
        out_shape += [jax.ShapeDtypeStruct(w.shape, F32)] * 4
        args += [parts, w, m, v]
    in_specs.append(whole(loss_parts.shape))
    out_specs.append(whole(loss_parts.shape[1:]))
    out_shape.append(jax.ShapeDtypeStruct(loss_parts.shape[1:], F32))
    n = len(entries)

    def added(p_ref):
        total = p_ref[0]
        for d in range(1, N_DEV):
            total = total + p_ref[d]
        return total

    def body(me_ref, *refs):
        del me_ref
        ins, outs = refs[:4 * n + 1], refs[4 * n + 1:]
        for e in range(n):
            p_ref, w_ref, m_ref, v_ref = ins[4 * e:4 * e + 4]
            g_out, d_out, m_out, v_out = outs[4 * e:4 * e + 4]
            g = added(p_ref)
            g_out[...] = g
            d_out[...], m_out[...], v_out[...] = _adamw(w_ref[...], g, m_ref[...], v_ref[...])
        outs[4 * n][...] = added(ins[4 * n])

    grid_spec = pltpu.PrefetchScalarGridSpec(num_scalar_prefetch=1, grid=(1,), in_specs=in_specs, out_specs=out_specs)
    outs = pl.pallas_call(body, name="small_sum_adamw", grid_spec=grid_spec, out_shape=out_shape,
                          compiler_params=_params("arbitrary"))(me, *map(_in_hbm, args + [loss_parts]))
    return [outs[4 * e:4 * e + 4] for e in range(n)], outs[4 * n]


MM_TILE = 512
N_MM_TILES = SEQ // MM_TILE
CAT_TILE = 512
N_CAT_TILES = N_CAT // CAT_TILE


def kernel(x, g_mix, w_in, b_gate, w_gk_up, b_gk, w_pool_grp, pool_scale, g_gla_head, w_pool_proj, w_gla_proj, w_out, g_ffn, w_up, w_conv, b_conv, w_down, g_final, loss_target, m_g_mix, m_w_in, m_b_gate, m_w_gk_up, m_b_gk, m_w_pool_grp, m_pool_scale, m_g_gla_head, m_w_pool_proj, m_w_gla_proj, m_w_out, m_g_ffn, m_w_up, m_w_conv, m_b_conv, m_w_down, m_g_final, v_g_mix, v_w_in, v_b_gate, v_w_gk_up, v_b_gk, v_w_pool_grp, v_pool_scale, v_g_gla_head, v_w_pool_proj, v_w_gla_proj, v_w_out, v_g_ffn, v_w_up, v_w_conv, v_b_conv, v_w_down, v_g_final):
    xi, yi, ci = lax.axis_index("x"), lax.axis_index("y"), lax.axis_index("c")
    me = 4 * xi + 2 * yi + ci
    core = jnp.reshape(ci, (1,)).astype(jnp.int32)
    chip = jnp.reshape(2 * xi + yi, (1,)).astype(jnp.int32)
    xs, target = x[0], loss_target[0]

    big = dict(w_in=w_in[0].T, w_pool_proj=w_pool_proj[0], w_gla_proj=w_gla_proj[0], w_out=w_out[0], w_up=w_up[0].T,
               w_down=w_down[0])
    moments = dict(w_in=(m_w_in[0].T, v_w_in[0].T), w_pool_proj=(m_w_pool_proj[0], v_w_pool_proj[0]),
                   w_gla_proj=(m_w_gla_proj[0], v_w_gla_proj[0]), w_out=(m_w_out[0], v_w_out[0]),
                   w_up=(m_w_up[0].T, v_w_up[0].T), w_down=(m_w_down[0], v_w_down[0]))
    names = list(big)
    shards = {k: big[k].astype(BF) for k in names}
    shards["w_gk_up"], shards["w_conv"] = w_gk_up[0], w_conv[0]
    gather_groups = (("w_in", "w_gk_up"), ("w_pool_proj", "w_gla_proj", "w_out"), ("w_up", "w_down", "w_conv"))
    started, token = _split_start("gather_start", [
        ([t for k in g for t in (shards[k], _gather_landing(shards[k], me))], 4 * len(g), _gather_first)
        for g in gather_groups])

    def gather_pass(gi, after):
        lands = list(_split_wait(f"gather_wait_{gi}", started[gi], _gather_first, after)[1::2])
        passed, tkn = _split_start(f"gather_pass_{gi}", [(lands, 3 * len(lands), _gather_second)])
        return passed[0], tkn

    def gather_done(gi, passed, after):
        return dict(zip(gather_groups[gi], _split_wait(f"gather_pass_wait_{gi}", passed, _gather_second, after)))

    tok = lambda i, j, k: (i, 0)
    whole = lambda i, j, k: (0, 0)
    kblk = lambda i, j, k: (k, 0)
    ff_seq = (None, None, SEQ, FF_BLK)

    h = _rms_fwd(xs, g_mix + token[:1, :1], "rms_mix")
    wg = gather_done(0, gather_pass(0, h)[0], h)
    wt_cat, wt_gk = _unshard_w_in(wg["w_in"])
    wgk_pad = jnp.pad(wg["w_gk_up"].transpose(1, 0, 2).reshape(GATE_RANK, GLA_DK), ((0, GK_PAD - GATE_RANK), (0, 0)))
    zcat = _mm(h, wt_cat, out_shape=(SEQ, N_CAT), out_dtype=F32, grid=(N_CAT_TILES, 1, 1),
               blk_a=(SEQ, D_MODEL), blk_b=(CAT_TILE, D_MODEL), blk_o=(SEQ, CAT_TILE),
               map_a=whole, map_b=lambda j, i, k: (j, 0), map_o=lambda j, i, k: (0, j), tb=True, name="mm_in")
    la = _gk_fwd(h, wt_gk, wgk_pad, b_gk)
    passed, tkn = gather_pass(1, la)
    o, states = _gla_fwd(zcat, la, tkn)
    wg = gather_done(1, passed, o)
    wpp = wg["w_pool_proj"].transpose(1, 0, 2).reshape(POOL_WIDTH, D_MODEL)
    wgp = wg["w_gla_proj"].reshape(D_MODEL, D_MODEL)
    wout = wg["w_out"].reshape(D_MODEL, D_MODEL)
    og = _post_gla_fwd(o, zcat, g_gla_head)
    ps = _pool_fwd(zcat, w_pool_grp[0], pool_scale)
    passed, tkn = gather_pass(2, (og, ps))
    y_pool, y_gla, mixed, x1, h2 = _mix_out_fwd(ps, og, zcat, xs, wpp, wgp, wout, b_gate, g_ffn, tkn)
    wg = gather_done(2, passed, h2)
    wt_up = wg["w_up"].reshape(2 * D_FF, D_MODEL)
    wdown = wg["w_down"].reshape(D_FF, D_MODEL)
    wconv4 = wg["w_conv"].reshape(2, 4, 3, FF_BLK)
    bconv4 = b_conv.reshape(2, 4, 1, FF_BLK)
    blk4 = lambda b, i, k: (b // 4, b % 4, 0, 0)
    u4 = _mm(h2, wt_up, out_shape=(2, 4, SEQ, FF_BLK), out_dtype=F32, grid=(N_DEV, 1, 1),
             blk_a=(SEQ, D_MODEL), blk_b=(FF_BLK, D_MODEL), blk_o=ff_seq,
             map_a=whole, map_b=lambda b, i, k: (b, 0), map_o=blk4, tb=True, name="mm_up")
    act = _conv_fwd(u4, wconv4, bconv4)
    loss_part, dx2, dx2_bf, dg_final = _mm_tokens(
        act, wdown, blk_a=(None, 4, TOK_MM_TILE, FF_BLK), map_a=lambda i: (0, 0, i, 0),
        pieces=[(b, b * FF_BLK, FF_BLK) for b in range(4)], res=x1, then=("loss", g_final.reshape(1, D_MODEL), target),
        name="mm_down_loss")

    da = _mm(dx2_bf, wdown, out_shape=(1, 4, SEQ, FF_BLK), out_dtype=BF, grid=(4, 1, 1),
             blk_a=(SEQ, D_MODEL), blk_b=(FF_BLK, D_MODEL), blk_o=ff_seq,
             map_a=whole, map_b=lambda b, i, k: (b, 0), map_o=lambda b, i, k: (0, b, 0, 0), tb=True, name="mm_d_act")
    d_wdown = _mm(act, dx2_bf, out_shape=(D_FF, D_MODEL), out_dtype=BF, grid=(4, 1, 1),
                  blk_a=ff_seq, blk_b=(SEQ, D_MODEL), blk_o=(FF_BLK, D_MODEL),
                  map_a=lambda b, i, k: (0, b, 0, 0), map_b=whole, map_o=lambda b, i, k: (b, 0), ta=True,
                  name="mm_d_wdown")
    du4, d_wconv, d_bconv = _conv_bwd(u4, da, wconv4, bconv4)
    d_wt_up = _mm(du4, h2, out_shape=(2 * D_FF, D_MODEL), out_dtype=BF, grid=(N_DEV, 1, 1),
                  blk_a=ff_seq, blk_b=(SEQ, D_MODEL), blk_o=(FF_BLK, D_MODEL),
                  map_a=blk4, map_b=whole, map_o=lambda b, i, k: (b, 0), ta=True, name="mm_d_wup")
    res = {}

    def reduce_start(keys, parts):
        arrays = [t for k in keys for t in (parts[k], lax.empty((4,) + parts[k].shape[2:], BF))]
        st, tkn = _split_start("reduce_start_" + keys[0], [(arrays, 4 * len(keys), _reduce_first)])
        return st[0], tkn

    def reduce_cross(keys, st, after):
        arrays = _split_wait("reduce_wait_" + keys[0], st, _reduce_first, after)
        sums = [_pair_sum(p, r, core, "pair_sum_" + k) for k, p, r in zip(keys, arrays[0::2], arrays[1::2])]
        arrays = [t for s in sums for t in (s, lax.empty((3,) + s.shape[1:], BF))]
        st2, tkn = _split_start("reduce_cross_" + keys[0], [(arrays, 3 * len(keys), _reduce_second)])
        return st2[0], tkn

    def reduce_done(keys, st2, after):
        arrays = _split_wait("reduce_cross_wait_" + keys[0], st2, _reduce_second, after)
        for k, s, r in zip(keys, arrays[0::2], arrays[1::2]):
            outs = _chip_sum_adamw(s, r, big[k], moments[k][0], moments[k][1], chip, "adamw_" + k)
            res[k] = [(t.T if k in ("w_in", "w_up") else t)[None] for t in outs]

    ffn_keys = ("w_down", "w_up")
    ffn_red, tkn = reduce_start(ffn_keys, dict(w_down=d_wdown.reshape(4, 2, D_FF // N_DEV, D_MODEL),
                                               w_up=d_wt_up.reshape(4, 2, FF_BLK, D_MODEL)))
    dx1, dg_ffn = _mm_tokens(
        du4, wt_up, blk_a=(2, 4, TOK_MM_TILE, FF_BLK), map_a=lambda i: (0, 0, i, 0),
        pieces=[((b // 4, b % 4), b * FF_BLK, FF_BLK) for b in range(N_DEV)], after=tkn, then=("rms_bwd", x1, g_ffn, dx2),
        name="mm_d_h2_rms")

    sq_t = dict(out_shape=(D_MODEL, D_MODEL), grid=(1, 1, N_MM_TILES), blk_a=(MM_TILE, D_MODEL),
                blk_b=(MM_TILE, D_MODEL), blk_o=(D_MODEL, D_MODEL), map_a=kblk, map_b=kblk, map_o=whole, ta=True)
    d_wout = _mm(mixed, dx1, out_dtype=BF, name="mm_d_wout", **sq_t)
    dzcat, dy_pool, dy_gla, db_gate = _mix_bwd(dx1, wout, zcat, b_gate, y_pool, y_gla)
    ffn_red, _ = reduce_cross(ffn_keys, ffn_red, db_gate)
    d_wgp = _mm(og, dy_gla, out_dtype=BF, name="mm_d_wgp", **sq_t)
    mix_keys = ("w_out", "w_gla_proj")
    mix_red, tkn = reduce_start(mix_keys, dict(w_out=d_wout.reshape(4, 2, D_MODEL // N_DEV, D_MODEL),
                                               w_gla_proj=d_wgp.reshape(4, 2, D_MODEL // N_DEV, D_MODEL)))
    dzcat, d_o, dg_head = _post_gla_bwd(dzcat, dy_gla, wgp, o, zcat, g_gla_head + tkn[:1, :1])
    dzcat, dla = _gla_bwd(dzcat, zcat, la, d_o, states)
    mix_red, tkn = reduce_cross(mix_keys, mix_red, dla)
    dh_gk, d_wt_gk, d_wgk, db_gk = _gk_bwd(dla, h, wt_gk, wgk_pad, b_gk + tkn[:1, :1])
    dps = _mm(dy_pool, wpp, out_shape=(SEQ, POOL_WIDTH), out_dtype=F32, grid=(N_MM_TILES, 1, 1),
              blk_a=(MM_TILE, D_MODEL), blk_b=(POOL_WIDTH, D_MODEL), blk_o=(MM_TILE, POOL_WIDTH),
              map_a=tok, map_b=whole, map_o=tok, tb=True, name="mm_d_ps")
    d_wpp = _mm(ps, dy_pool, out_shape=(POOL_WIDTH, D_MODEL), out_dtype=F32, grid=(1, 1, N_MM_TILES),
                blk_a=(MM_TILE, POOL_WIDTH), blk_b=(MM_TILE, D_MODEL), blk_o=(POOL_WIDTH, D_MODEL),
                map_a=kblk, map_b=kblk, map_o=whole, ta=True, name="mm_d_wpp")
    dzcat, d_wgrp, d_scale = _pool_bwd(dzcat, zcat, dps, w_pool_grp[0], pool_scale)
    d_wt_cat = _mm(dzcat, h, out_shape=(N_CAT, D_MODEL), out_dtype=BF, grid=(N_CAT_TILES, 1, 1),
                   blk_a=(SEQ, CAT_TILE), blk_b=(SEQ, D_MODEL), blk_o=(CAT_TILE, D_MODEL),
                   map_a=lambda j, i, k: (0, j), map_b=whole, map_o=lambda j, i, k: (j, 0), ta=True, name="mm_d_wcat")
    in_keys = ("w_in", "w_pool_proj")
    in_red, tkn = reduce_start(in_keys, dict(
        w_in=_shard_d_w_in(d_wt_cat, d_wt_gk).reshape(4, 2, IN_SHARD, D_MODEL),
        w_pool_proj=d_wpp.reshape(POOL_WIDTH, N_DEV, D_MODEL // N_DEV).transpose(1, 0, 2).astype(BF)
        .reshape(4, 2, POOL_WIDTH, D_MODEL // N_DEV)))
    in_red, tkn = reduce_cross(in_keys, in_red, tkn)
    grad_x, dg_mix = _mm_tokens(dzcat, wt_cat, blk_a=(TOK_MM_TILE, N_CAT), map_a=lambda i: (i, 0),
                                pieces=[(None, 0, N_CAT)], res=dh_gk, after=tkn, then=("rms_bwd", xs, g_mix, dx1),
                                name="mm_d_h_rms")
    reduce_done(ffn_keys, ffn_red, grad_x)
    reduce_done(mix_keys, mix_red, res["w_down"][0])

    row = lambda t: t.reshape(1, D_MODEL)
    conv_vec = lambda t: t.reshape(2, 4, 1, FF_BLK)
    small = [("g_mix", dg_mix, g_mix, m_g_mix, v_g_mix, False), ("b_gate", db_gate, b_gate, m_b_gate, v_b_gate, False),
             ("w_gk_up", d_wgk.reshape(GATE_RANK, N_DEV, GLA_DK // N_DEV).transpose(1, 0, 2), w_gk_up, m_w_gk_up,
              v_w_gk_up, True),
             ("b_gk", db_gk, b_gk, m_b_gk, v_b_gk, False),
             ("w_pool_grp", d_wgrp, w_pool_grp, m_w_pool_grp, v_w_pool_grp, False),
             ("pool_scale", d_scale, pool_scale, m_pool_scale, v_pool_scale, False),
             ("g_gla_head", dg_head, g_gla_head, m_g_gla_head, v_g_gla_head, False),
             ("g_ffn", dg_ffn, g_ffn, m_g_ffn, v_g_ffn, False),
             ("w_conv", d_wconv.reshape(N_DEV, 3, FF_BLK), w_conv, m_w_conv, v_w_conv, True),
             ("b_conv", d_bconv, conv_vec(b_conv), conv_vec(m_b_conv), conv_vec(v_b_conv), False),
             ("g_final", dg_final, row(g_final), row(m_g_final), row(v_g_final), False)]
    gathered = _all_gather([t[1] for t in small] + [loss_part], "gather_small_grads")
    small_out, loss_sum = _small_sum_adamw(jnp.reshape(me, (1,)).astype(jnp.int32),
                                           [(p,) + t[2:] for p, t in zip(gathered, small)], gathered[-1])
    for t, outs in zip(small, small_out):
        res[t[0]] = list(outs)
    res["b_conv"] = [t.reshape(b_conv.shape) for t in res["b_conv"]]
    res["g_final"] = [t.reshape(g_final.shape) for t in res["g_final"]]

    reduce_done(in_keys, in_red, loss_sum)
    loss = loss_sum[0, 0]
    order =["g_mix", "w_in", "b_gate", "w_gk_up", "b_gk", "w_pool_grp", "pool_scale", "g_gla_head", "w_pool_proj",
             "w_gla_proj", "w_out", "g_ffn", "w_up", "w_conv", "b_conv", "w_down", "g_final"]
    return (loss, grad_x[None], *[res[k][0] for k in order], *[res[k][1] for k in order],
            *[res[k][2] for k in order], *[res[k][3] for k in order])
```

```python
import jax
import jax.numpy as jnp
from jax import lax
from jax.experimental import pallas as pl
from jax.experimental.pallas import tpu as pltpu

F32 = jnp.float32
BF = jnp.bfloat16
HIGHEST = lax.Precision.HIGHEST
MESH = pl.DeviceIdType.MESH

N_DEV = 8
SEQ = 2048
D_MODEL = 1024
CHUNK = 64
EPS = 1e-6
POOL_WIDTH = 512
POOL_WINDOWS = (2, 4, 8, 16)
POOL_GD = 128
POOL_HALO = 16
HEADS = 4
HK = 128
HV = 256
GLA_DK = 512
GATE_RANK = 16
GATE_NORM = 16.0
D_FF = 2816
FF_BLK = 704
IN_SHARD = 706
C_QKV, C_GATE, C_OG, C_POOL = 0, 2048, 4096, 5120
N_CAT = 5632
R_POOL, R_QKV, R_OG, R_GK, R_GATE = 0, 512, 2560, 3584, 3600
GK_PAD = 128

ADAM_LR, ADAM_B1, ADAM_B2, ADAM_EPS, ADAM_WD, ADAM_STEP = 0.001, 0.9, 0.999, 1e-08, 0.01, 10
ADAM_C1 = 1.0 - ADAM_B1 ** ADAM_STEP
ADAM_C2 = 1.0 - ADAM_B2 ** ADAM_STEP

VMEM_BYTES_V7X = 64 * 1024 * 1024
VMEM_LIMIT = VMEM_BYTES_V7X * 3 // 4

TOK_TILE = 256
HALO = 8
GLA_CPS = 4


def _params(*sem):
    return pltpu.CompilerParams(dimension_semantics=sem, vmem_limit_bytes=VMEM_LIMIT)


def _const_spec(shape):
    nd = len(shape)
    return pl.BlockSpec(shape, lambda *_: (0,) * nd)


def _in_hbm(t):
    return pltpu.with_memory_space_constraint(t, pltpu.HBM)


def _dot(a, b, ta=False, tb=False):
    dims = (((0 if ta else 1,), (1 if tb else 0,)), ((), ()))
    return lax.dot_general(a.astype(BF), b.astype(BF), dims, preferred_element_type=F32)


def _dot_exact(a, b):
    return jnp.dot(a, b, precision=HIGHEST, preferred_element_type=F32)


def _sigmoid(x):
    return 0.5 * jnp.tanh(0.5 * x) + 0.5


def _mm(a, b, *, out_shape, out_dtype, grid, blk_a, blk_b, blk_o, map_a, map_b, map_o, ta=False, tb=False,
        after=None, name):
    gk = grid[2]
    n_in = 2 + (after is not None)

    def body(*refs):
        a_ref, b_ref, o_ref = refs[0], refs[1], refs[n_in]
        prod = _dot(a_ref[...], b_ref[...], ta, tb)
        if gk == 1:
            o_ref[...] = prod.astype(out_dtype)
        else:
            acc = refs[n_in + 1]
            k = pl.program_id(2)

            @pl.when(k == 0)
            def _():
                acc[...] = prod

            @pl.when(k > 0)
            def _():
                acc[...] += prod

            @pl.when(k == gk - 1)
            def _():
                o_ref[...] = acc[...].astype(out_dtype)

    in_specs = [pl.BlockSpec(blk_a, map_a), pl.BlockSpec(blk_b, map_b)]
    args = [_in_hbm(a), _in_hbm(b)]
    if after is not None:
        in_specs.append(pl.BlockSpec(memory_space=pl.ANY))
        args.append(after)
    return pl.pallas_call(
        body, name=name, grid=grid, in_specs=in_specs, out_specs=pl.BlockSpec(blk_o, map_o),
        out_shape=jax.ShapeDtypeStruct(out_shape, out_dtype),
        scratch_shapes=[] if gk == 1 else [pltpu.VMEM(tuple(d for d in blk_o if d is not None), F32)],
        compiler_params=_params("parallel", "parallel", "arbitrary"),
    )(*args)


TOK_MM_TILE = 256


def _mm_tokens(a, w, *, blk_a, map_a, pieces, res=None, after=None, then=None, name):
    n_in = 2 + (res is not None) + (after is not None) + (0 if then is None else len(then) - 1)

    def accumulate(ref, part):
        @pl.when(pl.program_id(0) == 0)
        def _():
            ref[...] = part

        @pl.when(pl.program_id(0) > 0)
        def _():
            ref[...] += part

    def body(*refs):
        a_ref, w_ref = refs[:2]
        extra, outs = refs[n_in - (0 if then is None else len(then) - 1):n_in], refs[n_in:]
        total = None
        for idx, row, n in pieces:
            av = a_ref[...] if idx is None else a_ref[idx]
            prod = _dot(av, w_ref[row:row + n, :])
            total = prod if total is None else total + prod
        if res is not None:
            total = total + refs[2][...]
        if then is None:
            outs[0][...] = total
        elif then[0] == "rms_bwd":
            dx, part = _rms_bwd_tile(total, extra[0][...], extra[1][...], extra[2][...])
            outs[0][...] = dx
            accumulate(outs[1], part)
        else:
            lpart, dx, part = _loss_tile(total, extra[0][...], extra[1][...])
            outs[1][...] = dx
            outs[2][...] = dx.astype(BF)
            accumulate(outs[0], lpart)
            accumulate(outs[3], part)

    tile = pl.BlockSpec((TOK_MM_TILE, D_MODEL), lambda i: (i, 0))
    vec = _const_spec((1, D_MODEL))
    big = jax.ShapeDtypeStruct((SEQ, D_MODEL), F32)
    small = jax.ShapeDtypeStruct((1, D_MODEL), F32)
    in_specs = [pl.BlockSpec(blk_a, map_a), pl.BlockSpec(w.shape, lambda i: (0, 0), pipeline_mode=pl.Buffered(1))]
    args = [a, w]
    if res is not None:
        in_specs.append(tile)
        args.append(res)
    if after is not None:
        in_specs.append(pl.BlockSpec(memory_space=pl.ANY))
        args.append(after)
    if then is None:
        out_specs, out_shape = tile, big
    elif then[0] == "rms_bwd":
        in_specs += [tile, vec, tile]
        out_specs, out_shape = [tile, vec], [big, small]
    else:
        in_specs += [vec, tile]
        out_specs = [_const_spec((1, 128)), tile, tile, vec]
        out_shape = [jax.ShapeDtypeStruct((1, 128), F32), big, jax.ShapeDtypeStruct((SEQ, D_MODEL), BF), small]
    if then is not None:
        args += list(then[1:])
    return pl.pallas_call(
        body, name=name, grid=(SEQ // TOK_MM_TILE,), in_specs=in_specs, out_specs=out_specs, out_shape=out_shape,
        compiler_params=_params("parallel" if then is None else "arbitrary"),
    )(*[_in_hbm(t) for t in args])


def _rms_fwd(x, g, name):
    def body(x_ref, g_ref, o_ref):
        xv = x_ref[...]
        r = lax.rsqrt(jnp.mean(xv * xv, axis=-1, keepdims=True) + EPS)
        o_ref[...] = (xv * r * g_ref[...]).astype(BF)

    tile = pl.BlockSpec((TOK_TILE, D_MODEL), lambda i: (i, 0))
    return pl.pallas_call(
        body, name=name, grid=(SEQ // TOK_TILE,), in_specs=[tile, _const_spec((1, D_MODEL))], out_specs=tile,
        out_shape=jax.ShapeDtypeStruct((SEQ, D_MODEL), BF), compiler_params=_params("parallel"),
    )(*map(_in_hbm, (x, g)))


def _rms_bwd_tile(dyv, xv, gv, dresv):
    r = lax.rsqrt(jnp.mean(xv * xv, axis=-1, keepdims=True) + EPS)
    xn = xv * r
    dxn = dyv * gv
    return dresv + r * (dxn - xn * jnp.mean(dxn * xn, axis=-1, keepdims=True)), jnp.sum(dyv * xn, axis=0, keepdims=True)


def _loss_tile(xv, gv, tv):
    r = lax.rsqrt(jnp.mean(xv * xv, axis=-1, keepdims=True) + EPS)
    xn = xv * r
    err = xn * gv - tv
    lpart = jnp.full((1, 128), 0.5 * jnp.sum(jnp.mean(err * err, axis=-1, keepdims=True)), F32)
    dyv = err * (1.0 / D_MODEL)
    dxn = dyv * gv
    return lpart, r * (dxn - xn * jnp.mean(dxn * xn, axis=-1, keepdims=True)), jnp.sum(dyv * xn, axis=0, keepdims=True)


def _pool_counts(w):
    pos = lax.broadcasted_iota(jnp.int32, (SEQ, 1), 0).astype(F32)
    return jnp.minimum(pos + 1.0, float(w))


def _pool_window(u, w, ext):
    ext[pl.ds(POOL_HALO, SEQ), :] = u
    win = u
    for j in range(1, w):
        win = win + ext[pl.ds(POOL_HALO - j, SEQ), :]
    return win / _pool_counts(w) - u


def _pool_fwd(zcat, w_grp, scale):
    def body(z_ref, w_ref, s_ref, o_ref, ext):
        ext[pl.ds(0, POOL_HALO), :] = jnp.zeros((POOL_HALO, POOL_GD), F32)
        for g, w in enumerate(POOL_WINDOWS):
            cols = slice(g * POOL_GD, (g + 1) * POOL_GD)
            p = _pool_window(z_ref[:, cols], w, ext)
            o_ref[:, cols] = (_dot(p, w_ref[g]) * s_ref[:, cols]).astype(BF)

    return pl.pallas_call(
        body, name="pool_fwd", grid=(1,),
        in_specs=[pl.BlockSpec((SEQ, POOL_WIDTH), lambda i: (0, C_POOL // POOL_WIDTH)),
                  _const_spec((4, POOL_GD, POOL_GD)), _const_spec((1, POOL_WIDTH))],
        out_specs=_const_spec((SEQ, POOL_WIDTH)), out_shape=jax.ShapeDtypeStruct((SEQ, POOL_WIDTH), BF),
        scratch_shapes=[pltpu.VMEM((POOL_HALO + SEQ, POOL_GD), F32)], compiler_params=_params("arbitrary"),
    )(*map(_in_hbm, (zcat, w_grp, scale)))


def _pool_bwd(dzcat, zcat, dps, w_grp, scale):
    def body(dz_in, z_ref, dps_ref, w_ref, s_ref, dz_ref, dw_ref, dsc_ref, ext, ext2):
        del dz_in
        ext[pl.ds(0, POOL_HALO), :] = jnp.zeros((POOL_HALO, POOL_GD), F32)
        ext2[pl.ds(SEQ, POOL_HALO), :] = jnp.zeros((POOL_HALO, POOL_GD), F32)
        for g, w in enumerate(POOL_WINDOWS):
            cols = slice(g * POOL_GD, (g + 1) * POOL_GD)
            p = _pool_window(z_ref[:, cols], w, ext)
            wg = w_ref[g]
            pg = _dot(p, wg)
            dpsv = dps_ref[:, cols]
            dsc_ref[:, cols] = jnp.sum(dpsv * pg, axis=0, keepdims=True)
            dpg = dpsv * s_ref[:, cols]
            dw_ref[g] = _dot(p, dpg, ta=True)
            dp = _dot(dpg, wg, tb=True)
            dpc = dp / _pool_counts(w)
            ext2[pl.ds(0, SEQ), :] = dpc
            du = dpc
            for j in range(1, w):
                du = du + ext2[pl.ds(j, SEQ), :]
            dz_ref[:, cols] = (du - dp).astype(BF)

    return pl.pallas_call(
        body, name="pool_bwd", grid=(1,),
        in_specs=[pl.BlockSpec(memory_space=pl.ANY),
                  pl.BlockSpec((SEQ, POOL_WIDTH), lambda i: (0, C_POOL // POOL_WIDTH)),
                  _const_spec((SEQ, POOL_WIDTH)), _const_spec((4, POOL_GD, POOL_GD)), _const_spec((1, POOL_WIDTH))],
        out_specs=[pl.BlockSpec((SEQ, POOL_WIDTH), lambda i: (0, C_POOL // POOL_WIDTH)),
                   _const_spec((4, POOL_GD, POOL_GD)), _const_spec((1, POOL_WIDTH))],
        out_shape=[jax.ShapeDtypeStruct((SEQ, N_CAT), BF), jax.ShapeDtypeStruct((4, POOL_GD, POOL_GD), F32),
                   jax.ShapeDtypeStruct((1, POOL_WIDTH), F32)],
        scratch_shapes=[pltpu.VMEM((POOL_HALO + SEQ, POOL_GD), F32), pltpu.VMEM((SEQ + POOL_HALO, POOL_GD), F32)],
        input_output_aliases={0: 0}, compiler_params=_params("arbitrary"),
    )(*map(_in_hbm, (dzcat, zcat, dps, w_grp, scale)))


GK_TILE = 512


def _gk_fwd(h, wt_gk, wgk_pad, b_gk):
    def body(h_ref, wt_ref, w_ref, b_ref, la_ref):
        z_gk = _dot(h_ref[...], wt_ref[...], tb=True)
        pre = _dot(z_gk, w_ref[...]) + b_ref[...]
        la_ref[...] = (jnp.minimum(pre, 0.0) - jnp.log(1.0 + jnp.exp(-jnp.abs(pre)))) * (1.0 / GATE_NORM)

    return pl.pallas_call(
        body, name="gk_fwd", grid=(SEQ // GK_TILE,),
        in_specs=[pl.BlockSpec((GK_TILE, D_MODEL), lambda i: (i, 0)), _const_spec((GK_PAD, D_MODEL)),
                  _const_spec((GK_PAD, GLA_DK)), _const_spec((1, GLA_DK))],
        out_specs=pl.BlockSpec((GK_TILE, GLA_DK), lambda i: (i, 0)),
        out_shape=jax.ShapeDtypeStruct((SEQ, GLA_DK), F32), compiler_params=_params("parallel"),
    )(*map(_in_hbm, (h, wt_gk, wgk_pad, b_gk)))


def _gk_bwd(dla, h, wt_gk, wgk_pad, b_gk):
    def body(dla_ref, h_ref, wt_ref, w_ref, b_ref, dh_ref, dwt_ref, dw_ref, db_ref):
        hv = h_ref[...]
        wtv = wt_ref[...]
        wv = w_ref[...]
        z_gk = _dot(hv, wtv, tb=True)
        pre = _dot(z_gk, wv) + b_ref[...]
        dpre = dla_ref[...] * (1.0 / GATE_NORM) * (1.0 - _sigmoid(pre))
        dz_gk = _dot(dpre, wv, tb=True)
        dh_ref[...] = _dot(dz_gk, wtv)
        dwtp = _dot(dz_gk, hv, ta=True)
        dwp = _dot(z_gk, dpre, ta=True)[:GATE_RANK]
        dbp = jnp.sum(dpre, axis=0, keepdims=True)

        @pl.when(pl.program_id(0) == 0)
        def _():
            dwt_ref[...] = dwtp
            dw_ref[...] = dwp
            db_ref[...] = dbp

        @pl.when(pl.program_id(0) > 0)
        def _():
            dwt_ref[...] += dwtp
            dw_ref[...] += dwp
            db_ref[...] += dbp

    tile = pl.BlockSpec((GK_TILE, D_MODEL), lambda i: (i, 0))
    return pl.pallas_call(
        body, name="gk_bwd", grid=(SEQ // GK_TILE,),
        in_specs=[pl.BlockSpec((GK_TILE, GLA_DK), lambda i: (i, 0)), tile, _const_spec((GK_PAD, D_MODEL)),
                  _const_spec((GK_PAD, GLA_DK)), _const_spec((1, GLA_DK))],
        out_specs=[tile, _const_spec((GK_PAD, D_MODEL)), _const_spec((GATE_RANK, GLA_DK)), _const_spec((1, GLA_DK))],
        out_shape=[jax.ShapeDtypeStruct((SEQ, D_MODEL), F32), jax.ShapeDtypeStruct((GK_PAD, D_MODEL), F32),
                   jax.ShapeDtypeStruct((GATE_RANK, GLA_DK), F32), jax.ShapeDtypeStruct((1, GLA_DK), F32)],
        compiler_params=_params("arbitrary"),
    )(*map(_in_hbm, (dla, h, wt_gk, wgk_pad, b_gk)))


GLA_ROWS = GLA_CPS * CHUNK
GLA_STEPS = SEQ // GLA_ROWS
QKV_W = 2048


def _tri():
    return lax.broadcasted_iota(jnp.int32, (CHUNK, CHUNK), 0) >= lax.broadcasted_iota(jnp.int32, (CHUNK, CHUNK), 1)


def _chunk_cumsum(la_ref, rows):
    return _dot_exact(_tri().astype(F32), la_ref[rows, :])


def _gla_chunk(qkv_ref, la_ref, rows, h, bc_all):
    tri = _tri()
    q = qkv_ref[rows, h * HK:(h + 1) * HK] * (HK ** -0.5)
    k = qkv_ref[rows, GLA_DK + h * HK:GLA_DK + (h + 1) * HK]
    v = qkv_ref[rows, 2 * GLA_DK + h * HV:2 * GLA_DK + (h + 1) * HV].astype(BF)
    la = la_ref[rows, h * HK:(h + 1) * HK]
    bc = bc_all[:, h * HK:(h + 1) * HK]
    e_pos, e_neg = jnp.exp(bc), jnp.exp(-bc)
    dl = jnp.exp(jnp.sum(la, axis=0, keepdims=True))
    q_fw, q_bw, k_fw, k_bw = q * e_pos, q * e_neg, k * e_neg, k * e_pos
    scores = jnp.where(tri, _dot(q_fw, k_fw, tb=True), _dot(q_bw, k_bw, tb=True))
    return tri, v, e_pos, e_neg, dl, q_fw, q_bw, k_fw, k_bw, scores


def _gla_fwd(zcat, la, after):
    def body(qkv_ref, la_ref, after_ref, o_ref, st_ref, state):
        del after_ref

        @pl.when(pl.program_id(0) == 0)
        def _():
            state[...] = jnp.zeros_like(state)

        for c in range(GLA_CPS):
            rows = slice(c * CHUNK, (c + 1) * CHUNK)
            bc_all = _chunk_cumsum(la_ref, rows)
            for h in range(HEADS):
                _, v, _, _, dl, q_fw, _, k_fw, _, scores = _gla_chunk(qkv_ref, la_ref, rows, h, bc_all)
                st = state[h]
                st_ref[c, h] = st
                o_ref[rows, h * HV:(h + 1) * HV] = _dot(scores, v) + _dot(q_fw, st, tb=True)
                state[h] = st * dl + _dot(v, k_fw * dl, ta=True)

    return pl.pallas_call(
        body, name="gla_fwd", grid=(GLA_STEPS,),
        in_specs=[pl.BlockSpec((GLA_ROWS, QKV_W), lambda i: (i, 0)), pl.BlockSpec((GLA_ROWS, GLA_DK), lambda i: (i, 0)),
                  pl.BlockSpec(memory_space=pl.ANY)],
        out_specs=[pl.BlockSpec((GLA_ROWS, D_MODEL), lambda i: (i, 0)),
                   pl.BlockSpec((GLA_CPS, HEADS, HV, HK), lambda i: (i, 0, 0, 0))],
        out_shape=[jax.ShapeDtypeStruct((SEQ, D_MODEL), F32),
                   jax.ShapeDtypeStruct((SEQ // CHUNK, HEADS, HV, HK), F32)],
        scratch_shapes=[pltpu.VMEM((HEADS, HV, HK), F32)], compiler_params=_params("arbitrary"),
    )(*map(_in_hbm, (zcat, la)), after)


def _gla_bwd(dzcat, zcat, la, d_o, states):
    def body(dz_in, qkv_ref, la_ref, do_ref, st_ref, dqkv_ref, dla_ref, dstate):
        del dz_in

        @pl.when(pl.program_id(0) == 0)
        def _():
            dstate[...] = jnp.zeros_like(dstate)

        last_row = lax.broadcasted_iota(jnp.int32, (CHUNK, HK), 0) == CHUNK - 1
        upper = (lax.broadcasted_iota(jnp.int32, (CHUNK, CHUNK), 0)
                 <= lax.broadcasted_iota(jnp.int32, (CHUNK, CHUNK), 1)).astype(F32)
        for c in reversed(range(GLA_CPS)):
            rows = slice(c * CHUNK, (c + 1) * CHUNK)
            bc_all = _chunk_cumsum(la_ref, rows)
            dbs = []
            for h in range(HEADS):
                tri, v, e_pos, e_neg, dl, q_fw, q_bw, k_fw, k_bw, scores = _gla_chunk(qkv_ref, la_ref, rows, h, bc_all)
                st = st_ref[c, h]
                dst = dstate[h]
                d_out = do_ref[rows, h * HV:(h + 1) * HV].astype(BF)
                k_dec = k_fw * dl
                dp = _dot(d_out, v, tb=True)
                dp_fw = jnp.where(tri, dp, 0.0)
                dp_bw = jnp.where(tri, 0.0, dp)
                dv = _dot(scores, d_out, ta=True) + _dot(k_dec, dst, tb=True)
                dk_dec = _dot(v, dst)
                dq_fw = _dot(dp_fw, k_fw) + _dot(d_out, st)
                dk_fw = _dot(dp_fw, q_fw, ta=True) + dk_dec * dl
                dq_bw = _dot(dp_bw, k_bw)
                dk_bw = _dot(dp_bw, q_bw, ta=True)
                ddl = jnp.sum(st * dst, axis=0, keepdims=True) + jnp.sum(k_fw * dk_dec, axis=0, keepdims=True)
                dstate[h] = dst * dl + _dot(d_out, q_fw, ta=True)
                dq = (dq_fw * e_pos + dq_bw * e_neg) * (HK ** -0.5)
                dk = dk_fw * e_neg + dk_bw * e_pos
                dbs.append(dq_fw * q_fw - dk_fw * k_fw - dq_bw * q_bw + dk_bw * k_bw + jnp.where(last_row, ddl * dl, 0.0))
                dqkv_ref[rows, h * HK:(h + 1) * HK] = dq.astype(BF)
                dqkv_ref[rows, GLA_DK + h * HK:GLA_DK + (h + 1) * HK] = dk.astype(BF)
                dqkv_ref[rows, 2 * GLA_DK + h * HV:2 * GLA_DK + (h + 1) * HV] = dv.astype(BF)
            dla_ref[rows, :] = _dot_exact(upper, jnp.concatenate(dbs, axis=1))

    rev = lambda i: (GLA_STEPS - 1 - i, 0)
    return pl.pallas_call(
        body, name="gla_bwd", grid=(GLA_STEPS,),
        in_specs=[pl.BlockSpec(memory_space=pl.ANY), pl.BlockSpec((GLA_ROWS, QKV_W), rev),
                  pl.BlockSpec((GLA_ROWS, GLA_DK), rev), pl.BlockSpec((GLA_ROWS, D_MODEL), rev),
                  pl.BlockSpec((GLA_CPS, HEADS, HV, HK), lambda i: (GLA_STEPS - 1 - i, 0, 0, 0))],
        out_specs=[pl.BlockSpec((GLA_ROWS, QKV_W), rev), pl.BlockSpec((GLA_ROWS, GLA_DK), rev)],
        out_shape=[jax.ShapeDtypeStruct((SEQ, N_CAT), BF), jax.ShapeDtypeStruct((SEQ, GLA_DK), F32)],
        scratch_shapes=[pltpu.VMEM((HEADS, HV, HK), F32)], input_output_aliases={0: 0},
        compiler_params=_params("arbitrary"),
    )(*map(_in_hbm, (dzcat, zcat, la, d_o, states)))


def _silu_parts(x):
    s = _sigmoid(x)
    return x * s, s * (1.0 + x * (1.0 - s))


def _post_gla_fwd(o, zcat, g_head):
    def body(o_ref, zog_ref, g_ref, out_ref):
        for h in range(HEADS):
            cols = slice(h * HV, (h + 1) * HV)
            ov = o_ref[:, cols]
            r = lax.rsqrt(jnp.mean(ov * ov, axis=-1, keepdims=True) + EPS)
            act, _ = _silu_parts(zog_ref[:, cols])
            out_ref[:, cols] = (ov * r * g_ref[...] * act).astype(BF)

    tile = pl.BlockSpec((TOK_TILE, D_MODEL), lambda i: (i, 0))
    return pl.pallas_call(
        body, name="post_gla_fwd", grid=(SEQ // TOK_TILE,),
        in_specs=[tile, pl.BlockSpec((TOK_TILE, D_MODEL), lambda i: (i, C_OG // D_MODEL)), _const_spec((1, HV))],
        out_specs=tile, out_shape=jax.ShapeDtypeStruct((SEQ, D_MODEL), BF), compiler_params=_params("parallel"),
    )(*map(_in_hbm, (o, zcat, g_head)))


def _post_gla_bwd(dzcat, dy_gla, w_gla_proj, o, zcat, g_head):
    def body(dz_in, dyg_ref, w_ref, o_ref, zog_ref, g_ref, dz_ref, do_ref, dg_ref):
        del dz_in
        dog = _dot(dyg_ref[...], w_ref[...], tb=True)
        gpart = jnp.zeros((1, HV), F32)
        gv = g_ref[...]
        for h in range(HEADS):
            cols = slice(h * HV, (h + 1) * HV)
            ov = o_ref[:, cols]
            r = lax.rsqrt(jnp.mean(ov * ov, axis=-1, keepdims=True) + EPS)
            on = ov * r
            act, dact = _silu_parts(zog_ref[:, cols])
            dogv = dog[:, cols]
            dz_ref[:, cols] = (dogv * on * gv * dact).astype(BF)
            d_on_g = dogv * act
            gpart = gpart + jnp.sum(d_on_g * on, axis=0, keepdims=True)
            dxn = d_on_g * gv
            do_ref[:, cols] = r * (dxn - on * jnp.mean(dxn * on, axis=-1, keepdims=True))

        @pl.when(pl.program_id(0) == 0)
        def _():
            dg_ref[...] = gpart

        @pl.when(pl.program_id(0) > 0)
        def _():
            dg_ref[...] += gpart

    tile = pl.BlockSpec((TOK_TILE, D_MODEL), lambda i: (i, 0))
    ogspec = pl.BlockSpec((TOK_TILE, D_MODEL), lambda i: (i, C_OG // D_MODEL))
    return pl.pallas_call(
        body, name="post_gla_bwd", grid=(SEQ // TOK_TILE,),
        in_specs=[pl.BlockSpec(memory_space=pl.ANY), tile, _const_spec((D_MODEL, D_MODEL)), tile, ogspec,
                  _const_spec((1, HV))],
        out_specs=[ogspec, tile, _const_spec((1, HV))],
        out_shape=[jax.ShapeDtypeStruct((SEQ, N_CAT), BF), jax.ShapeDtypeStruct((SEQ, D_MODEL), F32),
                   jax.ShapeDtypeStruct((1, HV), F32)],
        input_output_aliases={0: 0}, compiler_params=_params("arbitrary"),
    )(*map(_in_hbm, (dzcat, dy_gla, w_gla_proj, o, zcat, g_head)))


GATE_W = 2 * D_MODEL


def _mix_out_fwd(ps, og, zcat, x, w_pool_proj, w_gla_proj, w_out, b_gate, g_ffn, after):
    def body(ps_ref, og_ref, zg_ref, x_ref, wpp_ref, wgp_ref, wout_ref, b_ref, g_ref, after_ref,
             yp_ref, yg_ref, mixed_ref, x1_ref, h2_ref):
        del after_ref
        y_pool = _dot(ps_ref[...], wpp_ref[...])
        y_gla = _dot(og_ref[...], wgp_ref[...])
        yp_ref[...] = y_pool
        yg_ref[...] = y_gla
        g0 = _sigmoid(zg_ref[:, :D_MODEL] + b_ref[:, :D_MODEL])
        g1 = _sigmoid(zg_ref[:, D_MODEL:] + b_ref[:, D_MODEL:])
        mixed = (g0 * y_pool + g1 * y_gla).astype(BF)
        mixed_ref[...] = mixed
        x1 = x_ref[...] + _dot(mixed, wout_ref[...])
        x1_ref[...] = x1
        r = lax.rsqrt(jnp.mean(x1 * x1, axis=-1, keepdims=True) + EPS)
        h2_ref[...] = (x1 * r * g_ref[...]).astype(BF)

    tile = pl.BlockSpec((TOK_TILE, D_MODEL), lambda i: (i, 0))
    resident = lambda shape: pl.BlockSpec(shape, lambda i: (0, 0), pipeline_mode=pl.Buffered(1))
    f32, bf16 = jax.ShapeDtypeStruct((SEQ, D_MODEL), F32), jax.ShapeDtypeStruct((SEQ, D_MODEL), BF)
    return pl.pallas_call(
        body, name="mix_out_fwd", grid=(SEQ // TOK_TILE,),
        in_specs=[pl.BlockSpec((TOK_TILE, POOL_WIDTH), lambda i: (i, 0)), tile,
                  pl.BlockSpec((TOK_TILE, GATE_W), lambda i: (i, C_GATE // GATE_W)), tile,
                  resident((POOL_WIDTH, D_MODEL)), resident((D_MODEL, D_MODEL)), resident((D_MODEL, D_MODEL)),
                  _const_spec((1, GATE_W)), _const_spec((1, D_MODEL)), pl.BlockSpec(memory_space=pl.ANY)],
        out_specs=[tile] * 5, out_shape=[f32, f32, bf16, f32, bf16], compiler_params=_params("parallel"),
    )(*map(_in_hbm, (ps, og, zcat, x, w_pool_proj, w_gla_proj, w_out, b_gate, g_ffn)), after)


def _mix_bwd(dx1, w_out, zcat, b_gate, y_pool, y_gla):
    def body(dx_ref, w_ref, zg_ref, b_ref, yp_ref, yg_ref, dz_ref, dyp_ref, dyg_ref, db_ref):
        dm = _dot(dx_ref[...], w_ref[...], tb=True)
        g0 = _sigmoid(zg_ref[:, :D_MODEL] + b_ref[:, :D_MODEL])
        g1 = _sigmoid(zg_ref[:, D_MODEL:] + b_ref[:, D_MODEL:])
        dyp_ref[...] = (dm * g0).astype(BF)
        dyg_ref[...] = (dm * g1).astype(BF)
        dz0 = dm * yp_ref[...] * g0 * (1.0 - g0)
        dz1 = dm * yg_ref[...] * g1 * (1.0 - g1)
        dz_ref[:, :D_MODEL] = dz0.astype(BF)
        dz_ref[:, D_MODEL:] = dz1.astype(BF)
        b0 = jnp.sum(dz0, axis=0, keepdims=True)
        b1 = jnp.sum(dz1, axis=0, keepdims=True)

        @pl.when(pl.program_id(0) == 0)
        def _():
            db_ref[:, :D_MODEL] = b0
            db_ref[:, D_MODEL:] = b1

        @pl.when(pl.program_id(0) > 0)
        def _():
            db_ref[:, :D_MODEL] += b0
            db_ref[:, D_MODEL:] += b1

    tile = pl.BlockSpec((TOK_TILE, D_MODEL), lambda i: (i, 0))
    gspec = pl.BlockSpec((TOK_TILE, GATE_W), lambda i: (i, C_GATE // GATE_W))
    return pl.pallas_call(
        body, name="mix_bwd", grid=(SEQ // TOK_TILE,),
        in_specs=[tile, _const_spec((D_MODEL, D_MODEL)), gspec, _const_spec((1, GATE_W)), tile, tile],
        out_specs=[gspec, tile, tile, _const_spec((1, GATE_W))],
        out_shape=[jax.ShapeDtypeStruct((SEQ, N_CAT), BF), jax.ShapeDtypeStruct((SEQ, D_MODEL), BF),
                   jax.ShapeDtypeStruct((SEQ, D_MODEL), BF), jax.ShapeDtypeStruct((1, GATE_W), F32)],
        compiler_params=_params("arbitrary"),
    )(*map(_in_hbm, (dx1, w_out, zcat, b_gate, y_pool, y_gla)))


N_TOK_TILES = SEQ // TOK_TILE
HALO_PER_TILE = TOK_TILE // HALO


LANE_TILES = tuple((lo, min(128, FF_BLK - lo)) for lo in range(0, FF_BLK, 128))


def _taps(w_ref, b_ref, half, lanes, rows):
    shape = (rows, lanes.stop - lanes.start)
    return ([jnp.broadcast_to(w_ref[half, j:j + 1, lanes], shape) for j in range(3)],
            jnp.broadcast_to(b_ref[half, :, lanes], shape))


def _conv_strips(u_ref, ub_ref, ua_ref, taps, lanes, width, n_strips):
    first = pl.program_id(1) == 0
    row = lax.broadcasted_iota(jnp.int32, (HALO, width), 0)
    prev = [[pltpu.roll(jnp.where(first, 0.0, ub_ref[half, :, lanes]), k, 0) for k in (1, 2)] for half in range(2)]
    for s in range(n_strips + (ua_ref is not None)):
        u3, conv = [], []
        for half in range(2):
            cur = u_ref[half, s * HALO:(s + 1) * HALO, lanes] if s < n_strips else ua_ref[half, :, lanes]
            rolled = [pltpu.roll(cur, k, 0) for k in (1, 2)]
            frames = [jnp.where(row >= 2, rolled[1], prev[half][1]), jnp.where(row >= 1, rolled[0], prev[half][0]), cur]
            prev[half] = rolled
            w3, bias = taps[half]
            u3.append(frames)
            conv.append(bias + frames[0] * w3[0] + frames[1] * w3[1] + frames[2] * w3[2])
        yield s, u3, conv


def _pair_specs(pairs):
    tile = pl.BlockSpec((pairs, None, TOK_TILE, FF_BLK), lambda b, i: (0, b, i, 0))
    before = pl.BlockSpec((pairs, None, HALO, FF_BLK), lambda b, i: (0, b, jnp.maximum(i * HALO_PER_TILE - 1, 0), 0))
    after = pl.BlockSpec((pairs, None, HALO, FF_BLK),
                         lambda b, i: (0, b, jnp.minimum((i + 1) * HALO_PER_TILE, SEQ // HALO - 1), 0))

    def vec(rows):
        return pl.BlockSpec((2, None, rows, FF_BLK), lambda b, i: (0, b, 0, 0))

    return tile, before, after, vec


N_STRIPS = TOK_TILE // HALO


def _conv_fwd(u, w_conv, b_conv):
    def body(u_ref, ub_ref, w_ref, b_ref, a_ref):
        for lo, width in LANE_TILES:
            lanes = slice(lo, lo + width)
            taps = [_taps(w_ref, b_ref, half, lanes, HALO) for half in range(2)]
            pending = None
            for s, _, (cg, cv) in _conv_strips(u_ref, ub_ref, None, taps, lanes, width, N_STRIPS):
                act = cg * _sigmoid(cg) * cv
                if s % 2 == 0:
                    pending = act
                else:
                    a_ref[0, (s - 1) * HALO:(s + 1) * HALO, lanes] = jnp.concatenate([pending, act], axis=0).astype(BF)

    tile, before, _, vec = _pair_specs(2)
    out_tile, _, _, _ = _pair_specs(1)
    return pl.pallas_call(
        body, name="conv_fwd", grid=(4, N_TOK_TILES), in_specs=[tile, before, vec(3), vec(1)],
        out_specs=out_tile, out_shape=jax.ShapeDtypeStruct((1, 4, SEQ, FF_BLK), BF),
        compiler_params=_params("parallel", "parallel"),
    )(*map(_in_hbm, (u, u, w_conv, b_conv)))


def _conv_bwd(u, da, w_conv, b_conv):
    def body(u_ref, ub_ref, ua_ref, da_ref, daa_ref, w_ref, b_ref, du_ref, dw_ref, db_ref):
        i = pl.program_id(1)

        @pl.when(i == 0)
        def _():
            dw_ref[...] = jnp.zeros_like(dw_ref)
            db_ref[...] = jnp.zeros_like(db_ref)

        for lo, width in LANE_TILES:
            lanes = slice(lo, lo + width)
            row = lax.broadcasted_iota(jnp.int32, (HALO, width), 0)
            taps = [_taps(w_ref, b_ref, half, lanes, HALO) for half in range(2)]
            acc_w = [[jnp.zeros((HALO, width), F32) for _ in range(3)] for _ in range(2)]
            acc_b = [jnp.zeros((HALO, width), F32) for _ in range(2)]
            da_pair, pending = None, [None, None]
            dc_prev, up_prev = [None, None], [None, None]
            for s, u3, (cg, cv) in _conv_strips(u_ref, ub_ref, ua_ref, taps, lanes, width, N_STRIPS):
                act, dact = _silu_parts(cg)
                if s == N_STRIPS:
                    da = jnp.where(i < N_TOK_TILES - 1, daa_ref[0, :, lanes].astype(F32), 0.0)
                elif s % 2 == 0:
                    da_pair = da_ref[0, s * HALO:(s + 2) * HALO, lanes].astype(F32)
                    da = da_pair[:HALO]
                else:
                    da = da_pair[HALO:]
                dc = (da * cv * dact, da * act)
                for half in range(2):
                    up = [pltpu.roll(dc[half], HALO - k, 0) for k in (1, 2)]
                    if s < N_STRIPS:
                        for j in range(3):
                            acc_w[half][j] = acc_w[half][j] + dc[half] * u3[half][j]
                        acc_b[half] = acc_b[half] + dc[half]
                    if s >= 1:
                        w3 = taps[half][0]
                        du = (dc_prev[half] * w3[2] + jnp.where(row < HALO - 1, up_prev[half][0], up[0]) * w3[1]
                              + jnp.where(row < HALO - 2, up_prev[half][1], up[1]) * w3[0])
                        if (s - 1) % 2 == 0:
                            pending[half] = du
                        else:
                            du_ref[half, (s - 2) * HALO:s * HALO, lanes] = jnp.concatenate([pending[half], du],
                                                                                           axis=0).astype(BF)
                    dc_prev[half], up_prev[half] = dc[half], up
            for half in range(2):
                for j in range(3):
                    dw_ref[half, j:j + 1, lanes] += jnp.sum(acc_w[half][j], axis=0, keepdims=True)
                db_ref[half, :, lanes] += jnp.sum(acc_b[half], axis=0, keepdims=True)

    tile, before, after, vec = _pair_specs(2)
    da_tile, _, da_after_spec, _ = _pair_specs(1)
    return pl.pallas_call(
        body, name="conv_bwd", grid=(4, N_TOK_TILES),
        in_specs=[tile, before, after, da_tile, da_after_spec, vec(3), vec(1)],
        out_specs=[tile, vec(3), vec(1)],
        out_shape=[jax.ShapeDtypeStruct((2, 4, SEQ, FF_BLK), BF), jax.ShapeDtypeStruct((2, 4, 3, FF_BLK), F32),
                   jax.ShapeDtypeStruct((2, 4, 1, FF_BLK), F32)],
        compiler_params=_params("parallel", "arbitrary"),
    )(*map(_in_hbm, (u, u, u, da, da, w_conv, b_conv)))


W_IN_SEGMENTS = ((R_POOL, POOL_WIDTH, "cat", C_POOL), (R_QKV, QKV_W, "cat", C_QKV), (R_OG, D_MODEL, "cat", C_OG),
                 (R_GK, GATE_RANK, "gk", 0), (R_GATE, GATE_W, "cat", C_GATE))


def _slab_pieces(d):
    lo, hi = d * IN_SHARD, (d + 1) * IN_SHARD
    pieces = []
    for start, n, dest, at in W_IN_SEGMENTS:
        a, b = max(lo, start), min(hi, start + n)
        if a < b:
            assert (a - lo) % 2 == 0 and (b - a) % 2 == 0 and (at + a - start) % 2 == 0
            pieces.append(((a - lo) // 2, (b - a) // 2, dest, (at + a - start) // 2))
    return pieces


def _unshard_w_in(slabs):
    def body(slab_ref, cat_ref, gk_ref):
        d = pl.program_id(0)
        src = slab_ref.bitcast(jnp.uint32)
        dst = dict(cat=cat_ref.bitcast(jnp.uint32), gk=gk_ref.bitcast(jnp.uint32))

        @pl.when(d == 0)
        def _():
            gk_ref[...] = jnp.zeros_like(gk_ref)

        for dd in range(N_DEV):
            @pl.when(d == dd)
            def _():
                for a, n, dest, at in _slab_pieces(dd):
                    dst[dest][pl.ds(at, n), :] = src[0, pl.ds(a, n), :]

    return pl.pallas_call(
        body, name="unshard_w_in", grid=(N_DEV,),
        in_specs=[pl.BlockSpec((1, IN_SHARD, D_MODEL), lambda d: (d, 0, 0))],
        out_specs=[_const_spec((N_CAT, D_MODEL)), _const_spec((GK_PAD, D_MODEL))],
        out_shape=[jax.ShapeDtypeStruct((N_CAT, D_MODEL), BF), jax.ShapeDtypeStruct((GK_PAD, D_MODEL), BF)],
        compiler_params=_params("arbitrary"),
    )(_in_hbm(slabs))


def _shard_d_w_in(d_cat, d_gk):
    def body(cat_ref, gk_ref, slab_ref):
        d = pl.program_id(0)
        cat = cat_ref.bitcast(jnp.uint32)
        gk = pltpu.bitcast(gk_ref[0:GATE_RANK, :].astype(BF), jnp.uint32)
        dst = slab_ref.bitcast(jnp.uint32)
        for dd in range(N_DEV):
            @pl.when(d == dd)
            def _():
                for a, n, source, at in _slab_pieces(dd):
                    dst[0, pl.ds(a, n), :] = gk[at:at + n] if source == "gk" else cat[pl.ds(at, n), :]

    return pl.pallas_call(
        body, name="shard_d_w_in", grid=(N_DEV,),
        in_specs=[_const_spec((N_CAT, D_MODEL)), _const_spec((GK_PAD, D_MODEL))],
        out_specs=pl.BlockSpec((1, IN_SHARD, D_MODEL), lambda d: (d, 0, 0)),
        out_shape=jax.ShapeDtypeStruct((N_DEV, IN_SHARD, D_MODEL), BF), compiler_params=_params("parallel"),
    )(_in_hbm(d_cat), _in_hbm(d_gk))


ANY = pl.BlockSpec(memory_space=pl.ANY)


def _place():
    x, y, c = lax.axis_index("x"), lax.axis_index("y"), lax.axis_index("c")
    other_chips = [(1 - x, y), (x, 1 - y), (1 - x, 1 - y)]
    return x, y, c, other_chips


SEM = pl.BlockSpec(memory_space=pltpu.SEMAPHORE)
IN_HBM = pl.BlockSpec(memory_space=pltpu.HBM)
SPLIT_PARAMS = pltpu.CompilerParams(has_side_effects=pltpu.SideEffectType.DATAFLOW_SIDE_EFFECTING)


def _gather_first(refs, send_sems, recv_sems):
    x, y, c, chips = _place()
    targets = [(x, y, 1 - c)] + [(px, py, c) for px, py in chips]
    return [pltpu.make_async_remote_copy(src_ref=refs[2 * a], dst_ref=refs[2 * a + 1].at[4 * x + 2 * y + c],
                                         send_sem=send_sems.at[4 * a + k], recv_sem=recv_sems.at[4 * a + k],
                                         device_id=to, device_id_type=MESH)
            for a in range(len(refs) // 2) for k, to in enumerate(targets)]


def _gather_direct(refs, send_sems, recv_sems):
    x, y, c, _ = _place()
    flips = [(dx, dy, dc) for dx in (0, 1) for dy in (0, 1) for dc in (0, 1) if dx + dy + dc]
    targets = [(1 - x if dx else x, 1 - y if dy else y, 1 - c if dc else c) for dx, dy, dc in flips]
    return [pltpu.make_async_remote_copy(src_ref=refs[2 * a], dst_ref=refs[2 * a + 1].at[4 * x + 2 * y + c],
                                         send_sem=send_sems.at[7 * a + k], recv_sem=recv_sems.at[7 * a + k],
                                         device_id=to, device_id_type=MESH)
            for a in range(len(refs) // 2) for k, to in enumerate(targets)]


def _gather_second(refs, send_sems, recv_sems):
    x, y, c, chips = _place()
    copies = []
    for a, land in enumerate(refs):
        for j, (px, py) in enumerate(chips):
            block = land.at[4 * px + 2 * py + c]
            copies.append(pltpu.make_async_remote_copy(src_ref=block, dst_ref=block, send_sem=send_sems.at[3 * a + j],
                                                       recv_sem=recv_sems.at[3 * a + j], device_id=(x, y, 1 - c),
                                                       device_id_type=MESH))
    return copies


def _reduce_first(refs, send_sems, recv_sems):
    x, y, c, _ = _place()
    return [pltpu.make_async_remote_copy(src_ref=refs[2 * a].at[j, 1 - c], dst_ref=refs[2 * a + 1].at[j],
                                         send_sem=send_sems.at[4 * a + j], recv_sem=recv_sems.at[4 * a + j],
                                         device_id=(x, y, 1 - c), device_id_type=MESH)
            for a in range(len(refs) // 2) for j in range(4)]


def _reduce_second(refs, send_sems, recv_sems):
    _, _, c, chips = _place()
    return [pltpu.make_async_remote_copy(src_ref=refs[2 * a].at[2 * px + py], dst_ref=refs[2 * a + 1].at[k],
                                         send_sem=send_sems.at[3 * a + k], recv_sem=recv_sems.at[3 * a + k],
                                         device_id=(px, py, c), device_id_type=MESH)
            for a in range(len(refs) // 2) for k, (px, py) in enumerate(chips)]


def _split_start(name, groups):
    arrays = [a for g in groups for a in g[0]]
    n = len(arrays)

    def body(*refs):
        sems = refs[n:n + 2 * len(groups)]
        at = 0
        for gi, (members, _, build) in enumerate(groups):
            for cp in build(refs[at:at + len(members)], sems[2 * gi], sems[2 * gi + 1]):
                cp.start()
            at += len(members)
        refs[-1][...] = jnp.zeros_like(refs[-1])

    sem_shapes = [pltpu.SemaphoreType.DMA((g[1],)) for g in groups for _ in range(2)]
    outs = pl.pallas_call(
        body, name=name, in_specs=[IN_HBM] * n,
        out_shape=(*sem_shapes, *[pltpu.HBM(a.shape, a.dtype) for a in arrays], jax.ShapeDtypeStruct((8, 128), F32)),
        out_specs=(*[SEM] * len(sem_shapes), *[IN_HBM] * n, pl.BlockSpec(memory_space=pltpu.VMEM)),
        input_output_aliases={i: len(sem_shapes) + i for i in range(n)}, compiler_params=SPLIT_PARAMS,
    )(*[pltpu.with_memory_space_constraint(a, pltpu.HBM) for a in arrays])
    per_group, at = [], len(sem_shapes)
    for gi, (members, _, _) in enumerate(groups):
        per_group.append((outs[2 * gi], outs[2 * gi + 1], list(outs[at:at + len(members)])))
        at += len(members)
    return per_group, outs[-1]


def _split_wait(name, started, build, after):
    send_sems, recv_sems, arrays = started
    n = len(arrays)
    after = after if isinstance(after, (tuple, list)) else (after,)

    def body(*refs):
        for cp in build(refs[:n], refs[n], refs[n + 1]):
            cp.wait_send()
            cp.wait_recv()

    return pl.pallas_call(
        body, name=name, in_specs=[IN_HBM] * n + [SEM, SEM] + [ANY] * len(after),
        out_shape=tuple(pltpu.HBM(a.shape, a.dtype) for a in arrays), out_specs=tuple([IN_HBM] * n),
        input_output_aliases={i: i for i in range(n)}, compiler_params=SPLIT_PARAMS,
    )(*arrays, send_sems, recv_sems, *after)


def _gather_landing(shard, me):
    return lax.dynamic_update_slice(lax.empty((N_DEV,) + shard.shape, shard.dtype), shard[None],
                                    (me,) + (0,) * shard.ndim)


def _tile_2d(rows, cols):
    for t in (256, 176, 128):
        if rows % t == 0:
            return t, cols
    return rows, 256


def _pair_sum(part, recv, core, name):
    _, rows, cols = recv.shape
    tr, tc = rows, cols

    def body(c_ref, p_ref, r_ref, o_ref):
        del c_ref
        o_ref[...] = (p_ref[...].astype(F32) + r_ref[...].astype(F32)).astype(BF)

    grid_spec = pltpu.PrefetchScalarGridSpec(
        num_scalar_prefetch=1, grid=(4, rows // tr, cols // tc),
        in_specs=[pl.BlockSpec((None, None, tr, tc), lambda j, i, k, c_ref: (j, c_ref[0], i, k)),
                  pl.BlockSpec((None, tr, tc), lambda j, i, k, c_ref: (j, i, k))],
        out_specs=pl.BlockSpec((None, tr, tc), lambda j, i, k, c_ref: (j, i, k)))
    return pl.pallas_call(
        body, name=name, grid_spec=grid_spec, out_shape=jax.ShapeDtypeStruct(recv.shape, BF),
        compiler_params=_params("parallel", "parallel", "parallel"),
    )(core, *map(_in_hbm, (part, recv)))


def _adamw(w, g, m, v):
    m = ADAM_B1 * m + (1.0 - ADAM_B1) * g
    v = ADAM_B2 * v + (1.0 - ADAM_B2) * (g * g)
    delta = -ADAM_LR * ((m / ADAM_C1) / (jnp.sqrt(v / ADAM_C2) + ADAM_EPS) + ADAM_WD * w)
    return delta, m, v


def _chip_sum_adamw(sums, recv, w, m, v, chip, name):
    rows, cols = w.shape
    tr, tc = _tile_2d(rows, cols)

    def body(chip_ref, s_ref, r_ref, w_ref, m_ref, v_ref, g_out, d_out, m_out, v_out):
        del chip_ref
        g = s_ref[...].astype(F32)
        for k in range(3):
            g = g + r_ref[k].astype(F32)
        g_out[...] = g
        d_out[...], m_out[...], v_out[...] = _adamw(w_ref[...], g, m_ref[...], v_ref[...])

    tile = pl.BlockSpec((tr, tc), lambda i, k, chip_ref: (i, k))
    grid_spec = pltpu.PrefetchScalarGridSpec(
        num_scalar_prefetch=1, grid=(rows // tr, cols // tc),
        in_specs=[pl.BlockSpec((None, tr, tc), lambda i, k, chip_ref: (chip_ref[0], i, k)),
                  pl.BlockSpec((3, tr, tc), lambda i, k, chip_ref: (0, i, k)), tile, tile, tile],
        out_specs=[tile] * 4)
    return pl.pallas_call(
        body, name=name, grid_spec=grid_spec, out_shape=[jax.ShapeDtypeStruct((rows, cols), F32)] * 4,
        compiler_params=_params("parallel", "parallel"),
    )(chip, *map(_in_hbm, (sums, recv, w, m, v)))


def _small_sum_adamw(me, entries, loss_parts):
    def whole(shape, squeeze=0, pick=False):
        blk = (None,) * squeeze + tuple(shape[squeeze:])
        if pick:
            blk = (shape[0], None) + tuple(shape[2:])
            return pl.BlockSpec(blk, lambda i, me_ref: (0, me_ref[0]) + (0,) * (len(shape) - 2))
        return pl.BlockSpec(blk, lambda i, me_ref: (0,) * len(shape))

    in_specs, out_specs, out_shape, args = [], [], [], []
    for parts, w, m, v, sharded in entries:
        lead = w.ndim - (parts.ndim - (2 if sharded else 1))
        in_specs += [whole(parts.shape, pick=sharded)] + [whole(w.shape, squeeze=lead)] * 3
        out_specs += [whole(w.shape, squeeze=lead)] * 4
        out_shape += [jax.ShapeDtypeStruct(w.shape, F32)] * 4
        args += [parts, w, m, v]
    in_specs.append(whole(loss_parts.shape))
    out_specs.append(whole(loss_parts.shape[1:]))
    out_shape.append(jax.ShapeDtypeStruct(loss_parts.shape[1:], F32))
    n = len(entries)

    def added(p_ref):
        total = p_ref[0]
        for d in range(1, N_DEV):
            total = total + p_ref[d]
        return total

    def body(me_ref, *refs):
        del me_ref
        ins, outs = refs[:4 * n + 1], refs[4 * n + 1:]
        for e in range(n):
            p_ref, w_ref, m_ref, v_ref = ins[4 * e:4 * e + 4]
            g_out, d_out, m_out, v_out = outs[4 * e:4 * e + 4]
            g = added(p_ref)
            g_out[...] = g
            d_out[...], m_out[...], v_out[...] = _adamw(w_ref[...], g, m_ref[...], v_ref[...])
        outs[4 * n][...] = added(ins[4 * n])

    grid_spec = pltpu.PrefetchScalarGridSpec(num_scalar_prefetch=1, grid=(1,), in_specs=in_specs, out_specs=out_specs)
    outs = pl.pallas_call(body, name="small_sum_adamw", grid_spec=grid_spec, out_shape=out_shape,
                          compiler_params=_params("arbitrary"))(me, *map(_in_hbm, args + [loss_parts]))
    return [outs[4 * e:4 * e + 4] for e in range(n)], outs[4 * n]


MM_TILE = 512
N_MM_TILES = SEQ // MM_TILE
CAT_TILE = 512
N_CAT_TILES = N_CAT // CAT_TILE


def kernel(x, g_mix, w_in, b_gate, w_gk_up, b_gk, w_pool_grp, pool_scale, g_gla_head, w_pool_proj, w_gla_proj, w_out, g_ffn, w_up, w_conv, b_conv, w_down, g_final, loss_target, m_g_mix, m_w_in, m_b_gate, m_w_gk_up, m_b_gk, m_w_pool_grp, m_pool_scale, m_g_gla_head, m_w_pool_proj, m_w_gla_proj, m_w_out, m_g_ffn, m_w_up, m_w_conv, m_b_conv, m_w_down, m_g_final, v_g_mix, v_w_in, v_b_gate, v_w_gk_up, v_b_gk, v_w_pool_grp, v_pool_scale, v_g_gla_head, v_w_pool_proj, v_w_gla_proj, v_w_out, v_g_ffn, v_w_up, v_w_conv, v_b_conv, v_w_down, v_g_final):
    xi, yi, ci = lax.axis_index("x"), lax.axis_index("y"), lax.axis_index("c")
    me = 4 * xi + 2 * yi + ci
    core = jnp.reshape(ci, (1,)).astype(jnp.int32)
    chip = jnp.reshape(2 * xi + yi, (1,)).astype(jnp.int32)
    xs, target = x[0], loss_target[0]

    big = dict(w_in=w_in[0].T, w_pool_proj=w_pool_proj[0], w_gla_proj=w_gla_proj[0], w_out=w_out[0], w_up=w_up[0].T,
               w_down=w_down[0])
    moments = dict(w_in=(m_w_in[0].T, v_w_in[0].T), w_pool_proj=(m_w_pool_proj[0], v_w_pool_proj[0]),
                   w_gla_proj=(m_w_gla_proj[0], v_w_gla_proj[0]), w_out=(m_w_out[0], v_w_out[0]),
                   w_up=(m_w_up[0].T, v_w_up[0].T), w_down=(m_w_down[0], v_w_down[0]))
    names = list(big)
    shards = {k: big[k].astype(BF) for k in names}
    shards["w_gk_up"], shards["w_conv"] = w_gk_up[0], w_conv[0]
    gather_groups = (("w_in", "w_gk_up"), ("w_pool_proj", "w_gla_proj", "w_out"), ("w_up", "w_down", "w_conv"))
    started, token = _split_start("gather_start", [
        ([t for k in g for t in (shards[k], _gather_landing(shards[k], me))], 4 * len(g), _gather_first)
        for g in gather_groups])

    def gather_pass(gi, after):
        lands = list(_split_wait(f"gather_wait_{gi}", started[gi], _gather_first, after)[1::2])
        passed, tkn = _split_start(f"gather_pass_{gi}", [(lands, 3 * len(lands), _gather_second)])
        return passed[0], tkn

    def gather_done(gi, passed, after):
        return dict(zip(gather_groups[gi], _split_wait(f"gather_pass_wait_{gi}", passed, _gather_second, after)))

    tok = lambda i, j, k: (i, 0)
    whole = lambda i, j, k: (0, 0)
    kblk = lambda i, j, k: (k, 0)
    ff_seq = (None, None, SEQ, FF_BLK)

    h = _rms_fwd(xs, g_mix + token[:1, :1], "rms_mix")
    wg = gather_done(0, gather_pass(0, h)[0], h)
    wt_cat, wt_gk = _unshard_w_in(wg["w_in"])
    wgk_pad = jnp.pad(wg["w_gk_up"].transpose(1, 0, 2).reshape(GATE_RANK, GLA_DK), ((0, GK_PAD - GATE_RANK), (0, 0)))
    zcat = _mm(h, wt_cat, out_shape=(SEQ, N_CAT), out_dtype=F32, grid=(N_CAT_TILES, 1, 1),
               blk_a=(SEQ, D_MODEL), blk_b=(CAT_TILE, D_MODEL), blk_o=(SEQ, CAT_TILE),
               map_a=whole, map_b=lambda j, i, k: (j, 0), map_o=lambda j, i, k: (0, j), tb=True, name="mm_in")
    la = _gk_fwd(h, wt_gk, wgk_pad, b_gk)
    passed, tkn = gather_pass(1, la)
    o, states = _gla_fwd(zcat, la, tkn)
    wg = gather_done(1, passed, o)
    wpp = wg["w_pool_proj"].transpose(1, 0, 2).reshape(POOL_WIDTH, D_MODEL)
    wgp = wg["w_gla_proj"].reshape(D_MODEL, D_MODEL)
    wout = wg["w_out"].reshape(D_MODEL, D_MODEL)
    og = _post_gla_fwd(o, zcat, g_gla_head)
    ps = _pool_fwd(zcat, w_pool_grp[0], pool_scale)
    passed, tkn = gather_pass(2, (og, ps))
    y_pool, y_gla, mixed, x1, h2 = _mix_out_fwd(ps, og, zcat, xs, wpp, wgp, wout, b_gate, g_ffn, tkn)
    wg = gather_done(2, passed, h2)
    wt_up = wg["w_up"].reshape(2 * D_FF, D_MODEL)
    wdown = wg["w_down"].reshape(D_FF, D_MODEL)
    wconv4 = wg["w_conv"].reshape(2, 4, 3, FF_BLK)
    bconv4 = b_conv.reshape(2, 4, 1, FF_BLK)
    blk4 = lambda b, i, k: (b // 4, b % 4, 0, 0)
    u4 = _mm(h2, wt_up, out_shape=(2, 4, SEQ, FF_BLK), out_dtype=F32, grid=(N_DEV, 1, 1),
             blk_a=(SEQ, D_MODEL), blk_b=(FF_BLK, D_MODEL), blk_o=ff_seq,
             map_a=whole, map_b=lambda b, i, k: (b, 0), map_o=blk4, tb=True, name="mm_up")
    act = _conv_fwd(u4, wconv4, bconv4)
    loss_part, dx2, dx2_bf, dg_final = _mm_tokens(
        act, wdown, blk_a=(None, 4, TOK_MM_TILE, FF_BLK), map_a=lambda i: (0, 0, i, 0),
        pieces=[(b, b * FF_BLK, FF_BLK) for b in range(4)], res=x1, then=("loss", g_final.reshape(1, D_MODEL), target),
        name="mm_down_loss")

    da = _mm(dx2_bf, wdown, out_shape=(1, 4, SEQ, FF_BLK), out_dtype=BF, grid=(4, 1, 1),
             blk_a=(SEQ, D_MODEL), blk_b=(FF_BLK, D_MODEL), blk_o=ff_seq,
             map_a=whole, map_b=lambda b, i, k: (b, 0), map_o=lambda b, i, k: (0, b, 0, 0), tb=True, name="mm_d_act")
    d_wdown = _mm(act, dx2_bf, out_shape=(D_FF, D_MODEL), out_dtype=BF, grid=(4, 1, 1),
                  blk_a=ff_seq, blk_b=(SEQ, D_MODEL), blk_o=(FF_BLK, D_MODEL),
                  map_a=lambda b, i, k: (0, b, 0, 0), map_b=whole, map_o=lambda b, i, k: (b, 0), ta=True,
                  name="mm_d_wdown")
    du4, d_wconv, d_bconv = _conv_bwd(u4, da, wconv4, bconv4)
    d_wt_up = _mm(du4, h2, out_shape=(2 * D_FF, D_MODEL), out_dtype=BF, grid=(N_DEV, 1, 1),
                  blk_a=ff_seq, blk_b=(SEQ, D_MODEL), blk_o=(FF_BLK, D_MODEL),
                  map_a=blk4, map_b=whole, map_o=lambda b, i, k: (b, 0), ta=True, name="mm_d_wup")
    res = {}

    def reduce_start(keys, parts):
        arrays = [t for k in keys for t in (parts[k], lax.empty((4,) + parts[k].shape[2:], BF))]
        st, tkn = _split_start("reduce_start_" + keys[0], [(arrays, 4 * len(keys), _reduce_first)])
        return st[0], tkn

    def reduce_cross(keys, st, after):
        arrays = _split_wait("reduce_wait_" + keys[0], st, _reduce_first, after)
        sums = [_pair_sum(p, r, core, "pair_sum_" + k) for k, p, r in zip(keys, arrays[0::2], arrays[1::2])]
        arrays = [t for s in sums for t in (s, lax.empty((3,) + s.shape[1:], BF))]
        st2, tkn = _split_start("reduce_cross_" + keys[0], [(arrays, 3 * len(keys), _reduce_second)])
        return st2[0], tkn

    def reduce_done(keys, st2, after):
        arrays = _split_wait("reduce_cross_wait_" + keys[0], st2, _reduce_second, after)
        for k, s, r in zip(keys, arrays[0::2], arrays[1::2]):
            outs = _chip_sum_adamw(s, r, big[k], moments[k][0], moments[k][1], chip, "adamw_" + k)
            res[k] = [(t.T if k in ("w_in", "w_up") else t)[None] for t in outs]

    ffn_keys = ("w_down", "w_up")
    ffn_red, tkn = reduce_start(ffn_keys, dict(w_down=d_wdown.reshape(4, 2, D_FF // N_DEV, D_MODEL),
                                               w_up=d_wt_up.reshape(4, 2, FF_BLK, D_MODEL)))
    dx1, dg_ffn = _mm_tokens(
        du4, wt_up, blk_a=(2, 4, TOK_MM_TILE, FF_BLK), map_a=lambda i: (0, 0, i, 0),
        pieces=[((b // 4, b % 4), b * FF_BLK, FF_BLK) for b in range(N_DEV)], after=tkn, then=("rms_bwd", x1, g_ffn, dx2),
        name="mm_d_h2_rms")

    sq_t = dict(out_shape=(D_MODEL, D_MODEL), grid=(1, 1, N_MM_TILES), blk_a=(MM_TILE, D_MODEL),
                blk_b=(MM_TILE, D_MODEL), blk_o=(D_MODEL, D_MODEL), map_a=kblk, map_b=kblk, map_o=whole, ta=True)
    d_wout = _mm(mixed, dx1, out_dtype=BF, name="mm_d_wout", **sq_t)
    dzcat, dy_pool, dy_gla, db_gate = _mix_bwd(dx1, wout, zcat, b_gate, y_pool, y_gla)
    ffn_red, _ = reduce_cross(ffn_keys, ffn_red, db_gate)
    d_wgp = _mm(og, dy_gla, out_dtype=BF, name="mm_d_wgp", **sq_t)
    mix_keys = ("w_out", "w_gla_proj")
    mix_red, tkn = reduce_start(mix_keys, dict(w_out=d_wout.reshape(4, 2, D_MODEL // N_DEV, D_MODEL),
                                               w_gla_proj=d_wgp.reshape(4, 2, D_MODEL // N_DEV, D_MODEL)))
    dzcat, d_o, dg_head = _post_gla_bwd(dzcat, dy_gla, wgp, o, zcat, g_gla_head + tkn[:1, :1])
    dzcat, dla = _gla_bwd(dzcat, zcat, la, d_o, states)
    mix_red, tkn = reduce_cross(mix_keys, mix_red, dla)
    dh_gk, d_wt_gk, d_wgk, db_gk = _gk_bwd(dla, h, wt_gk, wgk_pad, b_gk + tkn[:1, :1])
    dps = _mm(dy_pool, wpp, out_shape=(SEQ, POOL_WIDTH), out_dtype=F32, grid=(N_MM_TILES, 1, 1),
              blk_a=(MM_TILE, D_MODEL), blk_b=(POOL_WIDTH, D_MODEL), blk_o=(MM_TILE, POOL_WIDTH),
              map_a=tok, map_b=whole, map_o=tok, tb=True, name="mm_d_ps")
    d_wpp = _mm(ps, dy_pool, out_shape=(POOL_WIDTH, D_MODEL), out_dtype=F32, grid=(1, 1, N_MM_TILES),
                blk_a=(MM_TILE, POOL_WIDTH), blk_b=(MM_TILE, D_MODEL), blk_o=(POOL_WIDTH, D_MODEL),
                map_a=kblk, map_b=kblk, map_o=whole, ta=True, name="mm_d_wpp")
    dzcat, d_wgrp, d_scale = _pool_bwd(dzcat, zcat, dps, w_pool_grp[0], pool_scale)
    row = lambda t: t.reshape(1, D_MODEL)
    conv_vec = lambda t: t.reshape(2, 4, 1, FF_BLK)
    small = [("b_gate", db_gate, b_gate, m_b_gate, v_b_gate, False),
             ("w_gk_up", d_wgk.reshape(GATE_RANK, N_DEV, GLA_DK // N_DEV).transpose(1, 0, 2), w_gk_up, m_w_gk_up,
              v_w_gk_up, True),
             ("b_gk", db_gk, b_gk, m_b_gk, v_b_gk, False),
             ("w_pool_grp", d_wgrp, w_pool_grp, m_w_pool_grp, v_w_pool_grp, False),
             ("pool_scale", d_scale, pool_scale, m_pool_scale, v_pool_scale, False),
             ("g_gla_head", dg_head, g_gla_head, m_g_gla_head, v_g_gla_head, False),
             ("g_ffn", dg_ffn, g_ffn, m_g_ffn, v_g_ffn, False),
             ("w_conv", d_wconv.reshape(N_DEV, 3, FF_BLK), w_conv, m_w_conv, v_w_conv, True),
             ("b_conv", d_bconv, conv_vec(b_conv), conv_vec(m_b_conv), conv_vec(v_b_conv), False),
             ("g_final", dg_final, row(g_final), row(m_g_final), row(v_g_final), False)]

    def small_start(parts, name):
        arrays = [t for p in parts for t in (p, _gather_landing(p, me))]
        st, tkn = _split_start(name, [(arrays, 7 * len(parts), _gather_direct)])
        return st[0], tkn

    small_sent, tkn = small_start([t[1] for t in small] + [loss_part], "small_start")
    d_wt_cat = _mm(dzcat, h, out_shape=(N_CAT, D_MODEL), out_dtype=BF, grid=(N_CAT_TILES, 1, 1),
                   blk_a=(SEQ, CAT_TILE), blk_b=(SEQ, D_MODEL), blk_o=(CAT_TILE, D_MODEL),
                   map_a=lambda j, i, k: (0, j), map_b=whole, map_o=lambda j, i, k: (j, 0), ta=True, after=tkn,
                   name="mm_d_wcat")
    in_keys = ("w_in", "w_pool_proj")
    in_red, tkn = reduce_start(in_keys, dict(
        w_in=_shard_d_w_in(d_wt_cat, d_wt_gk).reshape(4, 2, IN_SHARD, D_MODEL),
        w_pool_proj=d_wpp.reshape(POOL_WIDTH, N_DEV, D_MODEL // N_DEV).transpose(1, 0, 2).astype(BF)
        .reshape(4, 2, POOL_WIDTH, D_MODEL // N_DEV)))
    in_red, tkn = reduce_cross(in_keys, in_red, tkn)
    grad_x, dg_mix = _mm_tokens(dzcat, wt_cat, blk_a=(TOK_MM_TILE, N_CAT), map_a=lambda i: (i, 0),
                                pieces=[(None, 0, N_CAT)], res=dh_gk, after=tkn, then=("rms_bwd", xs, g_mix, dx1),
                                name="mm_d_h_rms")
    g_mix_sent, _ = small_start([dg_mix], "g_mix_start")
    reduce_done(ffn_keys, ffn_red, grad_x)
    reduce_done(mix_keys, mix_red, res["w_down"][0])
    gathered = _split_wait("small_wait", small_sent, _gather_direct, res["w_out"][0])[1::2]
    small.append(("g_mix", dg_mix, g_mix, m_g_mix, v_g_mix, False))
    gathered = list(gathered[:-1]) + [_split_wait("g_mix_wait", g_mix_sent, _gather_direct, gathered[0])[1], gathered[-1]]
    small_out, loss_sum = _small_sum_adamw(jnp.reshape(me, (1,)).astype(jnp.int32),
                                           [(p,) + t[2:] for p, t in zip(gathered, small)], gathered[-1])
    for t, outs in zip(small, small_out):
        res[t[0]] = list(outs)
    res["b_conv"] = [t.reshape(b_conv.shape) for t in res["b_conv"]]
    res["g_final"] = [t.reshape(g_final.shape) for t in res["g_final"]]

    reduce_done(in_keys, in_red, loss_sum)
    loss = loss_sum[0, 0]
    order =["g_mix", "w_in", "b_gate", "w_gk_up", "b_gk", "w_pool_grp", "pool_scale", "g_gla_head", "w_pool_proj",
             "w_gla_proj", "w_out", "g_ffn", "w_up", "w_conv", "b_conv", "w_down", "g_final"]
    return (loss, grad_x[None], *[res[k][0] for k in order], *[res[k][1] for k in order],
            *[res[k][2] for k in order], *[res[k][3] for k in order])
```

```python
import jax
import jax.numpy as jnp
from jax import lax
from jax.experimental import pallas as pl
from jax.experimental.pallas import tpu as pltpu

F32 = jnp.float32
BF = jnp.bfloat16
HIGHEST = lax.Precision.HIGHEST
MESH = pl.DeviceIdType.MESH

N_DEV = 8
SEQ = 2048
D_MODEL = 1024
CHUNK = 64
EPS = 1e-6
POOL_WIDTH = 512
POOL_WINDOWS = (2, 4, 8, 16)
POOL_GD = 128
POOL_HALO = 16
HEADS = 4
HK = 128
HV = 256
GLA_DK = 512
GATE_RANK = 16
GATE_NORM = 16.0
D_FF = 2816
FF_BLK = 704
IN_SHARD = 706
C_QKV, C_GATE, C_OG, C_POOL = 0, 2048, 4096, 5120
N_CAT = 5632
R_POOL, R_QKV, R_OG, R_GK, R_GATE = 0, 512, 2560, 3584, 3600
GK_PAD = 128

ADAM_LR, ADAM_B1, ADAM_B2, ADAM_EPS, ADAM_WD, ADAM_STEP = 0.001, 0.9, 0.999, 1e-08, 0.01, 10
ADAM_C1 = 1.0 - ADAM_B1 ** ADAM_STEP
ADAM_C2 = 1.0 - ADAM_B2 ** ADAM_STEP

VMEM_BYTES_V7X = 64 * 1024 * 1024
VMEM_LIMIT = VMEM_BYTES_V7X * 3 // 4

TOK_TILE = 256
HALO = 8
GLA_CPS = 4


def _params(*sem):
    return pltpu.CompilerParams(dimension_semantics=sem, vmem_limit_bytes=VMEM_LIMIT)


def _const_spec(shape):
    nd = len(shape)
    return pl.BlockSpec(shape, lambda *_: (0,) * nd)


def _in_hbm(t):
    return pltpu.with_memory_space_constraint(t, pltpu.HBM)


def _dot(a, b, ta=False, tb=False):
    dims = (((0 if ta else 1,), (1 if tb else 0,)), ((), ()))
    return lax.dot_general(a.astype(BF), b.astype(BF), dims, preferred_element_type=F32)


def _dot_exact(a, b):
    return jnp.dot(a, b, precision=HIGHEST, preferred_element_type=F32)


def _sigmoid(x):
    return 0.5 * jnp.tanh(0.5 * x) + 0.5


def _mm(a, b, *, out_shape, out_dtype, grid, blk_a, blk_b, blk_o, map_a, map_b, map_o, ta=False, tb=False,
        after=None, name):
    gk = grid[2]
    n_in = 2 + (after is not None)

    def body(*refs):
        a_ref, b_ref, o_ref = refs[0], refs[1], refs[n_in]
        prod = _dot(a_ref[...], b_ref[...], ta, tb)
        if gk == 1:
            o_ref[...] = prod.astype(out_dtype)
        else:
            acc = refs[n_in + 1]
            k = pl.program_id(2)

            @pl.when(k == 0)
            def _():
                acc[...] = prod

            @pl.when(k > 0)
            def _():
                acc[...] += prod

            @pl.when(k == gk - 1)
            def _():
                o_ref[...] = acc[...].astype(out_dtype)

    in_specs = [pl.BlockSpec(blk_a, map_a), pl.BlockSpec(blk_b, map_b)]
    args = [_in_hbm(a), _in_hbm(b)]
    if after is not None:
        in_specs.append(pl.BlockSpec(memory_space=pl.ANY))
        args.append(after)
    return pl.pallas_call(
        body, name=name, grid=grid, in_specs=in_specs, out_specs=pl.BlockSpec(blk_o, map_o),
        out_shape=jax.ShapeDtypeStruct(out_shape, out_dtype),
        scratch_shapes=[] if gk == 1 else [pltpu.VMEM(tuple(d for d in blk_o if d is not None), F32)],
        compiler_params=_params("parallel", "parallel", "arbitrary"),
    )(*args)


TOK_MM_TILE = 256


def _mm_tokens(a, w, *, blk_a, map_a, pieces, res=None, after=None, then=None, name):
    n_in = 2 + (res is not None) + (after is not None) + (0 if then is None else len(then) - 1)

    def accumulate(ref, part):
        @pl.when(pl.program_id(0) == 0)
        def _():
            ref[...] = part

        @pl.when(pl.program_id(0) > 0)
        def _():
            ref[...] += part

    def body(*refs):
        a_ref, w_ref = refs[:2]
        extra, outs = refs[n_in - (0 if then is None else len(then) - 1):n_in], refs[n_in:]
        total = None
        for idx, row, n in pieces:
            av = a_ref[...] if idx is None else a_ref[idx]
            prod = _dot(av, w_ref[row:row + n, :])
            total = prod if total is None else total + prod
        if res is not None:
            total = total + refs[2][...]
        if then is None:
            outs[0][...] = total
        elif then[0] == "rms_bwd":
            dx, part = _rms_bwd_tile(total, extra[0][...], extra[1][...], extra[2][...])
            outs[0][...] = dx
            accumulate(outs[1], part)
        else:
            lpart, dx, part = _loss_tile(total, extra[0][...], extra[1][...])
            outs[1][...] = dx
            outs[2][...] = dx.astype(BF)
            accumulate(outs[0], lpart)
            accumulate(outs[3], part)

    tile = pl.BlockSpec((TOK_MM_TILE, D_MODEL), lambda i: (i, 0))
    vec = _const_spec((1, D_MODEL))
    big = jax.ShapeDtypeStruct((SEQ, D_MODEL), F32)
    small = jax.ShapeDtypeStruct((1, D_MODEL), F32)
    in_specs = [pl.BlockSpec(blk_a, map_a), pl.BlockSpec(w.shape, lambda i: (0, 0), pipeline_mode=pl.Buffered(1))]
    args = [a, w]
    if res is not None:
        in_specs.append(tile)
        args.append(res)
    if after is not None:
        in_specs.append(pl.BlockSpec(memory_space=pl.ANY))
        args.append(after)
    if then is None:
        out_specs, out_shape = tile, big
    elif then[0] == "rms_bwd":
        in_specs += [tile, vec, tile]
        out_specs, out_shape = [tile, vec], [big, small]
    else:
        in_specs += [vec, tile]
        out_specs = [_const_spec((1, 128)), tile, tile, vec]
        out_shape = [jax.ShapeDtypeStruct((1, 128), F32), big, jax.ShapeDtypeStruct((SEQ, D_MODEL), BF), small]
    if then is not None:
        args += list(then[1:])
    return pl.pallas_call(
        body, name=name, grid=(SEQ // TOK_MM_TILE,), in_specs=in_specs, out_specs=out_specs, out_shape=out_shape,
        compiler_params=_params("parallel" if then is None else "arbitrary"),
    )(*[_in_hbm(t) for t in args])


def _rms_fwd(x, g, name):
    def body(x_ref, g_ref, o_ref):
        xv = x_ref[...]
        r = lax.rsqrt(jnp.mean(xv * xv, axis=-1, keepdims=True) + EPS)
        o_ref[...] = (xv * r * g_ref[...]).astype(BF)

    tile = pl.BlockSpec((TOK_TILE, D_MODEL), lambda i: (i, 0))
    return pl.pallas_call(
        body, name=name, grid=(SEQ // TOK_TILE,), in_specs=[tile, _const_spec((1, D_MODEL))], out_specs=tile,
        out_shape=jax.ShapeDtypeStruct((SEQ, D_MODEL), BF), compiler_params=_params("parallel"),
    )(*map(_in_hbm, (x, g)))


def _rms_bwd_tile(dyv, xv, gv, dresv):
    r = lax.rsqrt(jnp.mean(xv * xv, axis=-1, keepdims=True) + EPS)
    xn = xv * r
    dxn = dyv * gv
    return dresv + r * (dxn - xn * jnp.mean(dxn * xn, axis=-1, keepdims=True)), jnp.sum(dyv * xn, axis=0, keepdims=True)


def _loss_tile(xv, gv, tv):
    r = lax.rsqrt(jnp.mean(xv * xv, axis=-1, keepdims=True) + EPS)
    xn = xv * r
    err = xn * gv - tv
    lpart = jnp.full((1, 128), 0.5 * jnp.sum(jnp.mean(err * err, axis=-1, keepdims=True)), F32)
    dyv = err * (1.0 / D_MODEL)
    dxn = dyv * gv
    return lpart, r * (dxn - xn * jnp.mean(dxn * xn, axis=-1, keepdims=True)), jnp.sum(dyv * xn, axis=0, keepdims=True)


def _pool_counts(w):
    pos = lax.broadcasted_iota(jnp.int32, (SEQ, 1), 0).astype(F32)
    return jnp.minimum(pos + 1.0, float(w))


def _pool_window(u, w, ext):
    ext[pl.ds(POOL_HALO, SEQ), :] = u
    win = u
    for j in range(1, w):
        win = win + ext[pl.ds(POOL_HALO - j, SEQ), :]
    return win / _pool_counts(w) - u


def _pool_fwd(zcat, w_grp, scale):
    def body(z_ref, w_ref, s_ref, o_ref, ext):
        ext[pl.ds(0, POOL_HALO), :] = jnp.zeros((POOL_HALO, POOL_GD), F32)
        for g, w in enumerate(POOL_WINDOWS):
            cols = slice(g * POOL_GD, (g + 1) * POOL_GD)
            p = _pool_window(z_ref[:, cols], w, ext)
            o_ref[:, cols] = (_dot(p, w_ref[g]) * s_ref[:, cols]).astype(BF)

    return pl.pallas_call(
        body, name="pool_fwd", grid=(1,),
        in_specs=[pl.BlockSpec((SEQ, POOL_WIDTH), lambda i: (0, C_POOL // POOL_WIDTH)),
                  _const_spec((4, POOL_GD, POOL_GD)), _const_spec((1, POOL_WIDTH))],
        out_specs=_const_spec((SEQ, POOL_WIDTH)), out_shape=jax.ShapeDtypeStruct((SEQ, POOL_WIDTH), BF),
        scratch_shapes=[pltpu.VMEM((POOL_HALO + SEQ, POOL_GD), F32)], compiler_params=_params("arbitrary"),
    )(*map(_in_hbm, (zcat, w_grp, scale)))


def _pool_bwd(dzcat, zcat, dps, w_grp, scale):
    def body(dz_in, z_ref, dps_ref, w_ref, s_ref, dz_ref, dw_ref, dsc_ref, ext, ext2):
        del dz_in
        ext[pl.ds(0, POOL_HALO), :] = jnp.zeros((POOL_HALO, POOL_GD), F32)
        ext2[pl.ds(SEQ, POOL_HALO), :] = jnp.zeros((POOL_HALO, POOL_GD), F32)
        for g, w in enumerate(POOL_WINDOWS):
            cols = slice(g * POOL_GD, (g + 1) * POOL_GD)
            p = _pool_window(z_ref[:, cols], w, ext)
            wg = w_ref[g]
            pg = _dot(p, wg)
            dpsv = dps_ref[:, cols]
            dsc_ref[:, cols] = jnp.sum(dpsv * pg, axis=0, keepdims=True)
            dpg = dpsv * s_ref[:, cols]
            dw_ref[g] = _dot(p, dpg, ta=True)
            dp = _dot(dpg, wg, tb=True)
            dpc = dp / _pool_counts(w)
            ext2[pl.ds(0, SEQ), :] = dpc
            du = dpc
            for j in range(1, w):
                du = du + ext2[pl.ds(j, SEQ), :]
            dz_ref[:, cols] = (du - dp).astype(BF)

    return pl.pallas_call(
        body, name="pool_bwd", grid=(1,),
        in_specs=[pl.BlockSpec(memory_space=pl.ANY),
                  pl.BlockSpec((SEQ, POOL_WIDTH), lambda i: (0, C_POOL // POOL_WIDTH)),
                  _const_spec((SEQ, POOL_WIDTH)), _const_spec((4, POOL_GD, POOL_GD)), _const_spec((1, POOL_WIDTH))],
        out_specs=[pl.BlockSpec((SEQ, POOL_WIDTH), lambda i: (0, C_POOL // POOL_WIDTH)),
                   _const_spec((4, POOL_GD, POOL_GD)), _const_spec((1, POOL_WIDTH))],
        out_shape=[jax.ShapeDtypeStruct((SEQ, N_CAT), BF), jax.ShapeDtypeStruct((4, POOL_GD, POOL_GD), F32),
                   jax.ShapeDtypeStruct((1, POOL_WIDTH), F32)],
        scratch_shapes=[pltpu.VMEM((POOL_HALO + SEQ, POOL_GD), F32), pltpu.VMEM((SEQ + POOL_HALO, POOL_GD), F32)],
        input_output_aliases={0: 0}, compiler_params=_params("arbitrary"),
    )(*map(_in_hbm, (dzcat, zcat, dps, w_grp, scale)))


GK_TILE = 512


def _gk_fwd(h, wt_gk, wgk_pad, b_gk):
    def body(h_ref, wt_ref, w_ref, b_ref, la_ref):
        z_gk = _dot(h_ref[...], wt_ref[...], tb=True)
        pre = _dot(z_gk, w_ref[...]) + b_ref[...]
        la_ref[...] = (jnp.minimum(pre, 0.0) - jnp.log(1.0 + jnp.exp(-jnp.abs(pre)))) * (1.0 / GATE_NORM)

    return pl.pallas_call(
        body, name="gk_fwd", grid=(SEQ // GK_TILE,),
        in_specs=[pl.BlockSpec((GK_TILE, D_MODEL), lambda i: (i, 0)), _const_spec((GK_PAD, D_MODEL)),
                  _const_spec((GK_PAD, GLA_DK)), _const_spec((1, GLA_DK))],
        out_specs=pl.BlockSpec((GK_TILE, GLA_DK), lambda i: (i, 0)),
        out_shape=jax.ShapeDtypeStruct((SEQ, GLA_DK), F32), compiler_params=_params("parallel"),
    )(*map(_in_hbm, (h, wt_gk, wgk_pad, b_gk)))


def _gk_bwd(dla, h, wt_gk, wgk_pad, b_gk):
    def body(dla_ref, h_ref, wt_ref, w_ref, b_ref, dh_ref, dwt_ref, dw_ref, db_ref):
        hv = h_ref[...]
        wtv = wt_ref[...]
        wv = w_ref[...]
        z_gk = _dot(hv, wtv, tb=True)
        pre = _dot(z_gk, wv) + b_ref[...]
        dpre = dla_ref[...] * (1.0 / GATE_NORM) * (1.0 - _sigmoid(pre))
        dz_gk = _dot(dpre, wv, tb=True)
        dh_ref[...] = _dot(dz_gk, wtv)
        dwtp = _dot(dz_gk, hv, ta=True)
        dwp = _dot(z_gk, dpre, ta=True)[:GATE_RANK]
        dbp = jnp.sum(dpre, axis=0, keepdims=True)

        @pl.when(pl.program_id(0) == 0)
        def _():
            dwt_ref[...] = dwtp
            dw_ref[...] = dwp
            db_ref[...] = dbp

        @pl.when(pl.program_id(0) > 0)
        def _():
            dwt_ref[...] += dwtp
            dw_ref[...] += dwp
            db_ref[...] += dbp

    tile = pl.BlockSpec((GK_TILE, D_MODEL), lambda i: (i, 0))
    return pl.pallas_call(
        body, name="gk_bwd", grid=(SEQ // GK_TILE,),
        in_specs=[pl.BlockSpec((GK_TILE, GLA_DK), lambda i: (i, 0)), tile, _const_spec((GK_PAD, D_MODEL)),
                  _const_spec((GK_PAD, GLA_DK)), _const_spec((1, GLA_DK))],
        out_specs=[tile, _const_spec((GK_PAD, D_MODEL)), _const_spec((GATE_RANK, GLA_DK)), _const_spec((1, GLA_DK))],
        out_shape=[jax.ShapeDtypeStruct((SEQ, D_MODEL), F32), jax.ShapeDtypeStruct((GK_PAD, D_MODEL), F32),
                   jax.ShapeDtypeStruct((GATE_RANK, GLA_DK), F32), jax.ShapeDtypeStruct((1, GLA_DK), F32)],
        compiler_params=_params("arbitrary"),
    )(*map(_in_hbm, (dla, h, wt_gk, wgk_pad, b_gk)))


GLA_ROWS = GLA_CPS * CHUNK
GLA_STEPS = SEQ // GLA_ROWS
QKV_W = 2048


def _tri():
    return lax.broadcasted_iota(jnp.int32, (CHUNK, CHUNK), 0) >= lax.broadcasted_iota(jnp.int32, (CHUNK, CHUNK), 1)


def _chunk_cumsum(la_ref, rows):
    return _dot_exact(_tri().astype(F32), la_ref[rows, :])


def _gla_chunk(qkv_ref, la_ref, rows, h, bc_all):
    tri = _tri()
    q = qkv_ref[rows, h * HK:(h + 1) * HK] * (HK ** -0.5)
    k = qkv_ref[rows, GLA_DK + h * HK:GLA_DK + (h + 1) * HK]
    v = qkv_ref[rows, 2 * GLA_DK + h * HV:2 * GLA_DK + (h + 1) * HV].astype(BF)
    la = la_ref[rows, h * HK:(h + 1) * HK]
    bc = bc_all[:, h * HK:(h + 1) * HK]
    e_pos, e_neg = jnp.exp(bc), jnp.exp(-bc)
    dl = jnp.exp(jnp.sum(la, axis=0, keepdims=True))
    q_fw, q_bw, k_fw, k_bw = q * e_pos, q * e_neg, k * e_neg, k * e_pos
    scores = jnp.where(tri, _dot(q_fw, k_fw, tb=True), _dot(q_bw, k_bw, tb=True))
    return tri, v, e_pos, e_neg, dl, q_fw, q_bw, k_fw, k_bw, scores


def _gla_fwd(zcat, la, after):
    def body(qkv_ref, la_ref, after_ref, o_ref, st_ref, state):
        del after_ref

        @pl.when(pl.program_id(0) == 0)
        def _():
            state[...] = jnp.zeros_like(state)

        for c in range(GLA_CPS):
            rows = slice(c * CHUNK, (c + 1) * CHUNK)
            bc_all = _chunk_cumsum(la_ref, rows)
            for h in range(HEADS):
                _, v, _, _, dl, q_fw, _, k_fw, _, scores = _gla_chunk(qkv_ref, la_ref, rows, h, bc_all)
                st = state[h]
                st_ref[c, h] = st
                o_ref[rows, h * HV:(h + 1) * HV] = _dot(scores, v) + _dot(q_fw, st, tb=True)
                state[h] = st * dl + _dot(v, k_fw * dl, ta=True)

    return pl.pallas_call(
        body, name="gla_fwd", grid=(GLA_STEPS,),
        in_specs=[pl.BlockSpec((GLA_ROWS, QKV_W), lambda i: (i, 0)), pl.BlockSpec((GLA_ROWS, GLA_DK), lambda i: (i, 0)),
                  pl.BlockSpec(memory_space=pl.ANY)],
        out_specs=[pl.BlockSpec((GLA_ROWS, D_MODEL), lambda i: (i, 0)),
                   pl.BlockSpec((GLA_CPS, HEADS, HV, HK), lambda i: (i, 0, 0, 0))],
        out_shape=[jax.ShapeDtypeStruct((SEQ, D_MODEL), F32),
                   jax.ShapeDtypeStruct((SEQ // CHUNK, HEADS, HV, HK), F32)],
        scratch_shapes=[pltpu.VMEM((HEADS, HV, HK), F32)], compiler_params=_params("arbitrary"),
    )(*map(_in_hbm, (zcat, la)), after)


def _gla_bwd(dzcat, zcat, la, d_o, states):
    def body(dz_in, qkv_ref, la_ref, do_ref, st_ref, dqkv_ref, dla_ref, dstate):
        del dz_in

        @pl.when(pl.program_id(0) == 0)
        def _():
            dstate[...] = jnp.zeros_like(dstate)

        last_row = lax.broadcasted_iota(jnp.int32, (CHUNK, HK), 0) == CHUNK - 1
        upper = (lax.broadcasted_iota(jnp.int32, (CHUNK, CHUNK), 0)
                 <= lax.broadcasted_iota(jnp.int32, (CHUNK, CHUNK), 1)).astype(F32)
        for c in reversed(range(GLA_CPS)):
            rows = slice(c * CHUNK, (c + 1) * CHUNK)
            bc_all = _chunk_cumsum(la_ref, rows)
            dbs = []
            for h in range(HEADS):
                tri, v, e_pos, e_neg, dl, q_fw, q_bw, k_fw, k_bw, scores = _gla_chunk(qkv_ref, la_ref, rows, h, bc_all)
                st = st_ref[c, h]
                dst = dstate[h]
                d_out = do_ref[rows, h * HV:(h + 1) * HV].astype(BF)
                k_dec = k_fw * dl
                dp = _dot(d_out, v, tb=True)
                dp_fw = jnp.where(tri, dp, 0.0)
                dp_bw = jnp.where(tri, 0.0, dp)
                dv = _dot(scores, d_out, ta=True) + _dot(k_dec, dst, tb=True)
                dk_dec = _dot(v, dst)
                dq_fw = _dot(dp_fw, k_fw) + _dot(d_out, st)
                dk_fw = _dot(dp_fw, q_fw, ta=True) + dk_dec * dl
                dq_bw = _dot(dp_bw, k_bw)
                dk_bw = _dot(dp_bw, q_bw, ta=True)
                ddl = jnp.sum(st * dst, axis=0, keepdims=True) + jnp.sum(k_fw * dk_dec, axis=0, keepdims=True)
                dstate[h] = dst * dl + _dot(d_out, q_fw, ta=True)
                dq = (dq_fw * e_pos + dq_bw * e_neg) * (HK ** -0.5)
                dk = dk_fw * e_neg + dk_bw * e_pos
                dbs.append(dq_fw * q_fw - dk_fw * k_fw - dq_bw * q_bw + dk_bw * k_bw + jnp.where(last_row, ddl * dl, 0.0))
                dqkv_ref[rows, h * HK:(h + 1) * HK] = dq.astype(BF)
                dqkv_ref[rows, GLA_DK + h * HK:GLA_DK + (h + 1) * HK] = dk.astype(BF)
                dqkv_ref[rows, 2 * GLA_DK + h * HV:2 * GLA_DK + (h + 1) * HV] = dv.astype(BF)
            dla_ref[rows, :] = _dot_exact(upper, jnp.concatenate(dbs, axis=1))

    rev = lambda i: (GLA_STEPS - 1 - i, 0)
    return pl.pallas_call(
        body, name="gla_bwd", grid=(GLA_STEPS,),
        in_specs=[pl.BlockSpec(memory_space=pl.ANY), pl.BlockSpec((GLA_ROWS, QKV_W), rev),
                  pl.BlockSpec((GLA_ROWS, GLA_DK), rev), pl.BlockSpec((GLA_ROWS, D_MODEL), rev),
                  pl.BlockSpec((GLA_CPS, HEADS, HV, HK), lambda i: (GLA_STEPS - 1 - i, 0, 0, 0))],
        out_specs=[pl.BlockSpec((GLA_ROWS, QKV_W), rev), pl.BlockSpec((GLA_ROWS, GLA_DK), rev)],
        out_shape=[jax.ShapeDtypeStruct((SEQ, N_CAT), BF), jax.ShapeDtypeStruct((SEQ, GLA_DK), F32)],
        scratch_shapes=[pltpu.VMEM((HEADS, HV, HK), F32)], input_output_aliases={0: 0},
        compiler_params=_params("arbitrary"),
    )(*map(_in_hbm, (dzcat, zcat, la, d_o, states)))


def _silu_parts(x):
    s = _sigmoid(x)
    return x * s, s * (1.0 + x * (1.0 - s))


def _post_gla_fwd(o, zcat, g_head):
    def body(o_ref, zog_ref, g_ref, out_ref):
        for h in range(HEADS):
            cols = slice(h * HV, (h + 1) * HV)
            ov = o_ref[:, cols]
            r = lax.rsqrt(jnp.mean(ov * ov, axis=-1, keepdims=True) + EPS)
            act, _ = _silu_parts(zog_ref[:, cols])
            out_ref[:, cols] = (ov * r * g_ref[...] * act).astype(BF)

    tile = pl.BlockSpec((TOK_TILE, D_MODEL), lambda i: (i, 0))
    return pl.pallas_call(
        body, name="post_gla_fwd", grid=(SEQ // TOK_TILE,),
        in_specs=[tile, pl.BlockSpec((TOK_TILE, D_MODEL), lambda i: (i, C_OG // D_MODEL)), _const_spec((1, HV))],
        out_specs=tile, out_shape=jax.ShapeDtypeStruct((SEQ, D_MODEL), BF), compiler_params=_params("parallel"),
    )(*map(_in_hbm, (o, zcat, g_head)))


def _post_gla_bwd(dzcat, dy_gla, w_gla_proj, o, zcat, g_head):
    def body(dz_in, dyg_ref, w_ref, o_ref, zog_ref, g_ref, dz_ref, do_ref, dg_ref):
        del dz_in
        dog = _dot(dyg_ref[...], w_ref[...], tb=True)
        gpart = jnp.zeros((1, HV), F32)
        gv = g_ref[...]
        for h in range(HEADS):
            cols = slice(h * HV, (h + 1) * HV)
            ov = o_ref[:, cols]
            r = lax.rsqrt(jnp.mean(ov * ov, axis=-1, keepdims=True) + EPS)
            on = ov * r
            act, dact = _silu_parts(zog_ref[:, cols])
            dogv = dog[:, cols]
            dz_ref[:, cols] = (dogv * on * gv * dact).astype(BF)
            d_on_g = dogv * act
            gpart = gpart + jnp.sum(d_on_g * on, axis=0, keepdims=True)
            dxn = d_on_g * gv
            do_ref[:, cols] = r * (dxn - on * jnp.mean(dxn * on, axis=-1, keepdims=True))

        @pl.when(pl.program_id(0) == 0)
        def _():
            dg_ref[...] = gpart

        @pl.when(pl.program_id(0) > 0)
        def _():
            dg_ref[...] += gpart

    tile = pl.BlockSpec((TOK_TILE, D_MODEL), lambda i: (i, 0))
    ogspec = pl.BlockSpec((TOK_TILE, D_MODEL), lambda i: (i, C_OG // D_MODEL))
    return pl.pallas_call(
        body, name="post_gla_bwd", grid=(SEQ // TOK_TILE,),
        in_specs=[pl.BlockSpec(memory_space=pl.ANY), tile, _const_spec((D_MODEL, D_MODEL)), tile, ogspec,
                  _const_spec((1, HV))],
        out_specs=[ogspec, tile, _const_spec((1, HV))],
        out_shape=[jax.ShapeDtypeStruct((SEQ, N_CAT), BF), jax.ShapeDtypeStruct((SEQ, D_MODEL), F32),
                   jax.ShapeDtypeStruct((1, HV), F32)],
        input_output_aliases={0: 0}, compiler_params=_params("arbitrary"),
    )(*map(_in_hbm, (dzcat, dy_gla, w_gla_proj, o, zcat, g_head)))


GATE_W = 2 * D_MODEL


def _mix_out_fwd(ps, og, zcat, x, w_pool_proj, w_gla_proj, w_out, b_gate, g_ffn, after):
    def body(ps_ref, og_ref, zg_ref, x_ref, wpp_ref, wgp_ref, wout_ref, b_ref, g_ref, after_ref,
             yp_ref, yg_ref, mixed_ref, x1_ref, h2_ref):
        del after_ref
        y_pool = _dot(ps_ref[...], wpp_ref[...])
        y_gla = _dot(og_ref[...], wgp_ref[...])
        yp_ref[...] = y_pool
        yg_ref[...] = y_gla
        g0 = _sigmoid(zg_ref[:, :D_MODEL] + b_ref[:, :D_MODEL])
        g1 = _sigmoid(zg_ref[:, D_MODEL:] + b_ref[:, D_MODEL:])
        mixed = (g0 * y_pool + g1 * y_gla).astype(BF)
        mixed_ref[...] = mixed
        x1 = x_ref[...] + _dot(mixed, wout_ref[...])
        x1_ref[...] = x1
        r = lax.rsqrt(jnp.mean(x1 * x1, axis=-1, keepdims=True) + EPS)
        h2_ref[...] = (x1 * r * g_ref[...]).astype(BF)

    tile = pl.BlockSpec((TOK_TILE, D_MODEL), lambda i: (i, 0))
    resident = lambda shape: pl.BlockSpec(shape, lambda i: (0, 0), pipeline_mode=pl.Buffered(1))
    f32, bf16 = jax.ShapeDtypeStruct((SEQ, D_MODEL), F32), jax.ShapeDtypeStruct((SEQ, D_MODEL), BF)
    return pl.pallas_call(
        body, name="mix_out_fwd", grid=(SEQ // TOK_TILE,),
        in_specs=[pl.BlockSpec((TOK_TILE, POOL_WIDTH), lambda i: (i, 0)), tile,
                  pl.BlockSpec((TOK_TILE, GATE_W), lambda i: (i, C_GATE // GATE_W)), tile,
                  resident((POOL_WIDTH, D_MODEL)), resident((D_MODEL, D_MODEL)), resident((D_MODEL, D_MODEL)),
                  _const_spec((1, GATE_W)), _const_spec((1, D_MODEL)), pl.BlockSpec(memory_space=pl.ANY)],
        out_specs=[tile] * 5, out_shape=[f32, f32, bf16, f32, bf16], compiler_params=_params("parallel"),
    )(*map(_in_hbm, (ps, og, zcat, x, w_pool_proj, w_gla_proj, w_out, b_gate, g_ffn)), after)


def _mix_bwd(dx1, w_out, zcat, b_gate, y_pool, y_gla):
    def body(dx_ref, w_ref, zg_ref, b_ref, yp_ref, yg_ref, dz_ref, dyp_ref, dyg_ref, db_ref):
        dm = _dot(dx_ref[...], w_ref[...], tb=True)
        g0 = _sigmoid(zg_ref[:, :D_MODEL] + b_ref[:, :D_MODEL])
        g1 = _sigmoid(zg_ref[:, D_MODEL:] + b_ref[:, D_MODEL:])
        dyp_ref[...] = (dm * g0).astype(BF)
        dyg_ref[...] = (dm * g1).astype(BF)
        dz0 = dm * yp_ref[...] * g0 * (1.0 - g0)
        dz1 = dm * yg_ref[...] * g1 * (1.0 - g1)
        dz_ref[:, :D_MODEL] = dz0.astype(BF)
        dz_ref[:, D_MODEL:] = dz1.astype(BF)
        b0 = jnp.sum(dz0, axis=0, keepdims=True)
        b1 = jnp.sum(dz1, axis=0, keepdims=True)

        @pl.when(pl.program_id(0) == 0)
        def _():
            db_ref[:, :D_MODEL] = b0
            db_ref[:, D_MODEL:] = b1

        @pl.when(pl.program_id(0) > 0)
        def _():
            db_ref[:, :D_MODEL] += b0
            db_ref[:, D_MODEL:] += b1

    tile = pl.BlockSpec((TOK_TILE, D_MODEL), lambda i: (i, 0))
    gspec = pl.BlockSpec((TOK_TILE, GATE_W), lambda i: (i, C_GATE // GATE_W))
    return pl.pallas_call(
        body, name="mix_bwd", grid=(SEQ // TOK_TILE,),
        in_specs=[tile, _const_spec((D_MODEL, D_MODEL)), gspec, _const_spec((1, GATE_W)), tile, tile],
        out_specs=[gspec, tile, tile, _const_spec((1, GATE_W))],
        out_shape=[jax.ShapeDtypeStruct((SEQ, N_CAT), BF), jax.ShapeDtypeStruct((SEQ, D_MODEL), BF),
                   jax.ShapeDtypeStruct((SEQ, D_MODEL), BF), jax.ShapeDtypeStruct((1, GATE_W), F32)],
        compiler_params=_params("arbitrary"),
    )(*map(_in_hbm, (dx1, w_out, zcat, b_gate, y_pool, y_gla)))


N_TOK_TILES = SEQ // TOK_TILE
HALO_PER_TILE = TOK_TILE // HALO


LANE_TILES = tuple((lo, min(128, FF_BLK - lo)) for lo in range(0, FF_BLK, 128))


def _taps(w_ref, b_ref, half, lanes, rows):
    shape = (rows, lanes.stop - lanes.start)
    return ([jnp.broadcast_to(w_ref[half, j:j + 1, lanes], shape) for j in range(3)],
            jnp.broadcast_to(b_ref[half, :, lanes], shape))


def _conv_strips(u_ref, ub_ref, ua_ref, taps, lanes, width, n_strips):
    first = pl.program_id(1) == 0
    row = lax.broadcasted_iota(jnp.int32, (HALO, width), 0)
    prev = [[pltpu.roll(jnp.where(first, 0.0, ub_ref[half, :, lanes]), k, 0) for k in (1, 2)] for half in range(2)]
    for s in range(n_strips + (ua_ref is not None)):
        u3, conv = [], []
        for half in range(2):
            cur = u_ref[half, s * HALO:(s + 1) * HALO, lanes] if s < n_strips else ua_ref[half, :, lanes]
            rolled = [pltpu.roll(cur, k, 0) for k in (1, 2)]
            frames = [jnp.where(row >= 2, rolled[1], prev[half][1]), jnp.where(row >= 1, rolled[0], prev[half][0]), cur]
            prev[half] = rolled
            w3, bias = taps[half]
            u3.append(frames)
            conv.append(bias + frames[0] * w3[0] + frames[1] * w3[1] + frames[2] * w3[2])
        yield s, u3, conv


def _pair_specs(pairs):
    tile = pl.BlockSpec((pairs, None, TOK_TILE, FF_BLK), lambda b, i: (0, b, i, 0))
    before = pl.BlockSpec((pairs, None, HALO, FF_BLK), lambda b, i: (0, b, jnp.maximum(i * HALO_PER_TILE - 1, 0), 0))
    after = pl.BlockSpec((pairs, None, HALO, FF_BLK),
                         lambda b, i: (0, b, jnp.minimum((i + 1) * HALO_PER_TILE, SEQ // HALO - 1), 0))

    def vec(rows):
        return pl.BlockSpec((2, None, rows, FF_BLK), lambda b, i: (0, b, 0, 0))

    return tile, before, after, vec


N_STRIPS = TOK_TILE // HALO


def _conv_fwd(u, w_conv, b_conv):
    def body(u_ref, ub_ref, w_ref, b_ref, a_ref):
        for lo, width in LANE_TILES:
            lanes = slice(lo, lo + width)
            taps = [_taps(w_ref, b_ref, half, lanes, HALO) for half in range(2)]
            pending = None
            for s, _, (cg, cv) in _conv_strips(u_ref, ub_ref, None, taps, lanes, width, N_STRIPS):
                act = cg * _sigmoid(cg) * cv
                if s % 2 == 0:
                    pending = act
                else:
                    a_ref[0, (s - 1) * HALO:(s + 1) * HALO, lanes] = jnp.concatenate([pending, act], axis=0).astype(BF)

    tile, before, _, vec = _pair_specs(2)
    out_tile, _, _, _ = _pair_specs(1)
    return pl.pallas_call(
        body, name="conv_fwd", grid=(4, N_TOK_TILES), in_specs=[tile, before, vec(3), vec(1)],
        out_specs=out_tile, out_shape=jax.ShapeDtypeStruct((1, 4, SEQ, FF_BLK), BF),
        compiler_params=_params("parallel", "parallel"),
    )(*map(_in_hbm, (u, u, w_conv, b_conv)))


def _conv_bwd(u, da, w_conv, b_conv):
    def body(u_ref, ub_ref, ua_ref, da_ref, daa_ref, w_ref, b_ref, du_ref, dw_ref, db_ref):
        i = pl.program_id(1)

        @pl.when(i == 0)
        def _():
            dw_ref[...] = jnp.zeros_like(dw_ref)
            db_ref[...] = jnp.zeros_like(db_ref)

        for lo, width in LANE_TILES:
            lanes = slice(lo, lo + width)
            row = lax.broadcasted_iota(jnp.int32, (HALO, width), 0)
            taps = [_taps(w_ref, b_ref, half, lanes, HALO) for half in range(2)]
            acc_w = [[jnp.zeros((HALO, width), F32) for _ in range(3)] for _ in range(2)]
            acc_b = [jnp.zeros((HALO, width), F32) for _ in range(2)]
            da_pair, pending = None, [None, None]
            dc_prev, up_prev = [None, None], [None, None]
            for s, u3, (cg, cv) in _conv_strips(u_ref, ub_ref, ua_ref, taps, lanes, width, N_STRIPS):
                act, dact = _silu_parts(cg)
                if s == N_STRIPS:
                    da = jnp.where(i < N_TOK_TILES - 1, daa_ref[0, :, lanes].astype(F32), 0.0)
                elif s % 2 == 0:
                    da_pair = da_ref[0, s * HALO:(s + 2) * HALO, lanes].astype(F32)
                    da = da_pair[:HALO]
                else:
                    da = da_pair[HALO:]
                dc = (da * cv * dact, da * act)
                for half in range(2):
                    up = [pltpu.roll(dc[half], HALO - k, 0) for k in (1, 2)]
                    if s < N_STRIPS:
                        for j in range(3):
                            acc_w[half][j] = acc_w[half][j] + dc[half] * u3[half][j]
                        acc_b[half] = acc_b[half] + dc[half]
                    if s >= 1:
                        w3 = taps[half][0]
                        du = (dc_prev[half] * w3[2] + jnp.where(row < HALO - 1, up_prev[half][0], up[0]) * w3[1]
                              + jnp.where(row < HALO - 2, up_prev[half][1], up[1]) * w3[0])
                        if (s - 1) % 2 == 0:
                            pending[half] = du
                        else:
                            du_ref[half, (s - 2) * HALO:s * HALO, lanes] = jnp.concatenate([pending[half], du],
                                                                                           axis=0).astype(BF)
                    dc_prev[half], up_prev[half] = dc[half], up
            for half in range(2):
                for j in range(3):
                    dw_ref[half, j:j + 1, lanes] += jnp.sum(acc_w[half][j], axis=0, keepdims=True)
                db_ref[half, :, lanes] += jnp.sum(acc_b[half], axis=0, keepdims=True)

    tile, before, after, vec = _pair_specs(2)
    da_tile, _, da_after_spec, _ = _pair_specs(1)
    return pl.pallas_call(
        body, name="conv_bwd", grid=(4, N_TOK_TILES),
        in_specs=[tile, before, after, da_tile, da_after_spec, vec(3), vec(1)],
        out_specs=[tile, vec(3), vec(1)],
        out_shape=[jax.ShapeDtypeStruct((2, 4, SEQ, FF_BLK), BF), jax.ShapeDtypeStruct((2, 4, 3, FF_BLK), F32),
                   jax.ShapeDtypeStruct((2, 4, 1, FF_BLK), F32)],
        compiler_params=_params("parallel", "arbitrary"),
    )(*map(_in_hbm, (u, u, u, da, da, w_conv, b_conv)))


W_IN_SEGMENTS = ((R_POOL, POOL_WIDTH, "cat", C_POOL), (R_QKV, QKV_W, "cat", C_QKV), (R_OG, D_MODEL, "cat", C_OG),
                 (R_GK, GATE_RANK, "gk", 0), (R_GATE, GATE_W, "cat", C_GATE))


def _slab_pieces(d):
    lo, hi = d * IN_SHARD, (d + 1) * IN_SHARD
    pieces = []
    for start, n, dest, at in W_IN_SEGMENTS:
        a, b = max(lo, start), min(hi, start + n)
        if a < b:
            assert (a - lo) % 2 == 0 and (b - a) % 2 == 0 and (at + a - start) % 2 == 0
            pieces.append(((a - lo) // 2, (b - a) // 2, dest, (at + a - start) // 2))
    return pieces


def _unshard_w_in(slabs):
    def body(slab_ref, cat_ref, gk_ref):
        d = pl.program_id(0)
        src = slab_ref.bitcast(jnp.uint32)
        dst = dict(cat=cat_ref.bitcast(jnp.uint32), gk=gk_ref.bitcast(jnp.uint32))

        @pl.when(d == 0)
        def _():
            gk_ref[...] = jnp.zeros_like(gk_ref)

        for dd in range(N_DEV):
            @pl.when(d == dd)
            def _():
                for a, n, dest, at in _slab_pieces(dd):
                    dst[dest][pl.ds(at, n), :] = src[0, pl.ds(a, n), :]

    return pl.pallas_call(
        body, name="unshard_w_in", grid=(N_DEV,),
        in_specs=[pl.BlockSpec((1, IN_SHARD, D_MODEL), lambda d: (d, 0, 0))],
        out_specs=[_const_spec((N_CAT, D_MODEL)), _const_spec((GK_PAD, D_MODEL))],
        out_shape=[jax.ShapeDtypeStruct((N_CAT, D_MODEL), BF), jax.ShapeDtypeStruct((GK_PAD, D_MODEL), BF)],
        compiler_params=_params("arbitrary"),
    )(_in_hbm(slabs))


def _shard_d_w_in(d_cat, d_gk):
    def body(cat_ref, gk_ref, slab_ref):
        d = pl.program_id(0)
        cat = cat_ref.bitcast(jnp.uint32)
        gk = pltpu.bitcast(gk_ref[0:GATE_RANK, :].astype(BF), jnp.uint32)
        dst = slab_ref.bitcast(jnp.uint32)
        for dd in range(N_DEV):
            @pl.when(d == dd)
            def _():
                for a, n, source, at in _slab_pieces(dd):
                    dst[0, pl.ds(a, n), :] = gk[at:at + n] if source == "gk" else cat[pl.ds(at, n), :]

    return pl.pallas_call(
        body, name="shard_d_w_in", grid=(N_DEV,),
        in_specs=[_const_spec((N_CAT, D_MODEL)), _const_spec((GK_PAD, D_MODEL))],
        out_specs=pl.BlockSpec((1, IN_SHARD, D_MODEL), lambda d: (d, 0, 0)),
        out_shape=jax.ShapeDtypeStruct((N_DEV, IN_SHARD, D_MODEL), BF), compiler_params=_params("parallel"),
    )(_in_hbm(d_cat), _in_hbm(d_gk))


ANY = pl.BlockSpec(memory_space=pl.ANY)


def _place():
    x, y, c = lax.axis_index("x"), lax.axis_index("y"), lax.axis_index("c")
    other_chips = [(1 - x, y), (x, 1 - y), (1 - x, 1 - y)]
    return x, y, c, other_chips


SEM = pl.BlockSpec(memory_space=pltpu.SEMAPHORE)
IN_HBM = pl.BlockSpec(memory_space=pltpu.HBM)
SPLIT_PARAMS = pltpu.CompilerParams(has_side_effects=pltpu.SideEffectType.DATAFLOW_SIDE_EFFECTING)


def _gather_first(refs, send_sems, recv_sems):
    x, y, c, chips = _place()
    targets = [(x, y, 1 - c)] + [(px, py, c) for px, py in chips]
    return [pltpu.make_async_remote_copy(src_ref=refs[2 * a], dst_ref=refs[2 * a + 1].at[4 * x + 2 * y + c],
                                         send_sem=send_sems.at[4 * a + k], recv_sem=recv_sems.at[4 * a + k],
                                         device_id=to, device_id_type=MESH)
            for a in range(len(refs) // 2) for k, to in enumerate(targets)]


def _gather_direct(refs, send_sems, recv_sems):
    x, y, c, _ = _place()
    flips = [(dx, dy, dc) for dx in (0, 1) for dy in (0, 1) for dc in (0, 1) if dx + dy + dc]
    targets = [(1 - x if dx else x, 1 - y if dy else y, 1 - c if dc else c) for dx, dy, dc in flips]
    return [pltpu.make_async_remote_copy(src_ref=refs[2 * a], dst_ref=refs[2 * a + 1].at[4 * x + 2 * y + c],
                                         send_sem=send_sems.at[7 * a + k], recv_sem=recv_sems.at[7 * a + k],
                                         device_id=to, device_id_type=MESH)
            for a in range(len(refs) // 2) for k, to in enumerate(targets)]


def _gather_second(refs, send_sems, recv_sems):
    x, y, c, chips = _place()
    copies = []
    for a, land in enumerate(refs):
        for j, (px, py) in enumerate(chips):
            block = land.at[4 * px + 2 * py + c]
            copies.append(pltpu.make_async_remote_copy(src_ref=block, dst_ref=block, send_sem=send_sems.at[3 * a + j],
                                                       recv_sem=recv_sems.at[3 * a + j], device_id=(x, y, 1 - c),
                                                       device_id_type=MESH))
    return copies


def _reduce_first(refs, send_sems, recv_sems):
    x, y, c, _ = _place()
    return [pltpu.make_async_remote_copy(src_ref=refs[2 * a].at[j, 1 - c], dst_ref=refs[2 * a + 1].at[j],
                                         send_sem=send_sems.at[4 * a + j], recv_sem=recv_sems.at[4 * a + j],
                                         device_id=(x, y, 1 - c), device_id_type=MESH)
            for a in range(len(refs) // 2) for j in range(4)]


def _reduce_second(refs, send_sems, recv_sems):
    _, _, c, chips = _place()
    return [pltpu.make_async_remote_copy(src_ref=refs[2 * a].at[2 * px + py], dst_ref=refs[2 * a + 1].at[k],
                                         send_sem=send_sems.at[3 * a + k], recv_sem=recv_sems.at[3 * a + k],
                                         device_id=(px, py, c), device_id_type=MESH)
            for a in range(len(refs) // 2) for k, (px, py) in enumerate(chips)]


def _split_start(name, groups):
    arrays = [a for g in groups for a in g[0]]
    n = len(arrays)

    def body(*refs):
        sems = refs[n:n + 2 * len(groups)]
        at = 0
        for gi, (members, _, build) in enumerate(groups):
            for cp in build(refs[at:at + len(members)], sems[2 * gi], sems[2 * gi + 1]):
                cp.start()
            at += len(members)
        refs[-1][...] = jnp.zeros_like(refs[-1])

    sem_shapes = [pltpu.SemaphoreType.DMA((g[1],)) for g in groups for _ in range(2)]
    outs = pl.pallas_call(
        body, name=name, in_specs=[IN_HBM] * n,
        out_shape=(*sem_shapes, *[pltpu.HBM(a.shape, a.dtype) for a in arrays], jax.ShapeDtypeStruct((8, 128), F32)),
        out_specs=(*[SEM] * len(sem_shapes), *[IN_HBM] * n, pl.BlockSpec(memory_space=pltpu.VMEM)),
        input_output_aliases={i: len(sem_shapes) + i for i in range(n)}, compiler_params=SPLIT_PARAMS,
    )(*[pltpu.with_memory_space_constraint(a, pltpu.HBM) for a in arrays])
    per_group, at = [], len(sem_shapes)
    for gi, (members, _, _) in enumerate(groups):
        per_group.append((outs[2 * gi], outs[2 * gi + 1], list(outs[at:at + len(members)])))
        at += len(members)
    return per_group, outs[-1]


def _split_wait(name, started, build, after):
    send_sems, recv_sems, arrays = started
    n = len(arrays)
    after = after if isinstance(after, (tuple, list)) else (after,)

    def body(*refs):
        for cp in build(refs[:n], refs[n], refs[n + 1]):
            cp.wait_send()
            cp.wait_recv()

    return pl.pallas_call(
        body, name=name, in_specs=[IN_HBM] * n + [SEM, SEM] + [ANY] * len(after),
        out_shape=tuple(pltpu.HBM(a.shape, a.dtype) for a in arrays), out_specs=tuple([IN_HBM] * n),
        input_output_aliases={i: i for i in range(n)}, compiler_params=SPLIT_PARAMS,
    )(*arrays, send_sems, recv_sems, *after)


def _gather_landing(shard, me):
    return lax.dynamic_update_slice(lax.empty((N_DEV,) + shard.shape, shard.dtype), shard[None],
                                    (me,) + (0,) * shard.ndim)


def _tile_2d(rows, cols):
    for t in (256, 176, 128):
        if rows % t == 0:
            return t, cols
    return rows, 256


def _pair_sum(part, recv, core, name):
    _, rows, cols = recv.shape
    tr, tc = rows, cols

    def body(c_ref, p_ref, r_ref, o_ref):
        del c_ref
        o_ref[...] = (p_ref[...].astype(F32) + r_ref[...].astype(F32)).astype(BF)

    grid_spec = pltpu.PrefetchScalarGridSpec(
        num_scalar_prefetch=1, grid=(4, rows // tr, cols // tc),
        in_specs=[pl.BlockSpec((None, None, tr, tc), lambda j, i, k, c_ref: (j, c_ref[0], i, k)),
                  pl.BlockSpec((None, tr, tc), lambda j, i, k, c_ref: (j, i, k))],
        out_specs=pl.BlockSpec((None, tr, tc), lambda j, i, k, c_ref: (j, i, k)))
    return pl.pallas_call(
        body, name=name, grid_spec=grid_spec, out_shape=jax.ShapeDtypeStruct(recv.shape, BF),
        compiler_params=_params("parallel", "parallel", "parallel"),
    )(core, *map(_in_hbm, (part, recv)))


def _adamw(w, g, m, v):
    m = ADAM_B1 * m + (1.0 - ADAM_B1) * g
    v = ADAM_B2 * v + (1.0 - ADAM_B2) * (g * g)
    delta = -ADAM_LR * ((m / ADAM_C1) / (jnp.sqrt(v / ADAM_C2) + ADAM_EPS) + ADAM_WD * w)
    return delta, m, v


def _chip_sum_adamw(sums, recv, w, m, v, chip, name):
    rows, cols = w.shape
    tr, tc = _tile_2d(rows, cols)

    def body(chip_ref, s_ref, r_ref, w_ref, m_ref, v_ref, g_out, d_out, m_out, v_out):
        del chip_ref
        g = s_ref[...].astype(F32)
        for k in range(3):
            g = g + r_ref[k].astype(F32)
        g_out[...] = g
        d_out[...], m_out[...], v_out[...] = _adamw(w_ref[...], g, m_ref[...], v_ref[...])

    tile = pl.BlockSpec((tr, tc), lambda i, k, chip_ref: (i, k))
    grid_spec = pltpu.PrefetchScalarGridSpec(
        num_scalar_prefetch=1, grid=(rows // tr, cols // tc),
        in_specs=[pl.BlockSpec((None, tr, tc), lambda i, k, chip_ref: (chip_ref[0], i, k)),
                  pl.BlockSpec((3, tr, tc), lambda i, k, chip_ref: (0, i, k)), tile, tile, tile],
        out_specs=[tile] * 4)
    return pl.pallas_call(
        body, name=name, grid_spec=grid_spec, out_shape=[jax.ShapeDtypeStruct((rows, cols), F32)] * 4,
        compiler_params=_params("parallel", "parallel"),
    )(chip, *map(_in_hbm, (sums, recv, w, m, v)))


def _small_sum_adamw(me, entries, loss_parts):
    def whole(shape, squeeze=0, pick=False):
        blk = (None,) * squeeze + tuple(shape[squeeze:])
        if pick:
            blk = (shape[0], None) + tuple(shape[2:])
            return pl.BlockSpec(blk, lambda i, me_ref: (0, me_ref[0]) + (0,) * (len(shape) - 2))
        return pl.BlockSpec(blk, lambda i, me_ref: (0,) * len(shape))

    in_specs, out_specs, out_shape, args = [], [], [], []
    for parts, w, m, v, sharded in entries:
        lead = w.ndim - (parts.ndim - (2 if sharded else 1))
        in_specs += [whole(parts.shape, pick=sharded)] + [whole(w.shape, squeeze=lead)] * 3
        out_specs += [whole(w.shape, squeeze=lead)] * 4
        out_shape += [jax.ShapeDtypeStruct(w.shape, F32)] * 4
        args += [parts, w, m, v]
    in_specs.append(whole(loss_parts.shape))
    out_specs.append(whole(loss_parts.shape[1:]))
    out_shape.append(jax.ShapeDtypeStruct(loss_parts.shape[1:], F32))
    n = len(entries)

    def added(p_ref):
        total = p_ref[0]
        for d in range(1, N_DEV):
            total = total + p_ref[d]
        return total

    def body(me_ref, *refs):
        del me_ref
        ins, outs = refs[:4 * n + 1], refs[4 * n + 1:]
        for e in range(n):
            p_ref, w_ref, m_ref, v_ref = ins[4 * e:4 * e + 4]
            g_out, d_out, m_out, v_out = outs[4 * e:4 * e + 4]
            g = added(p_ref)
            g_out[...] = g
            d_out[...], m_out[...], v_out[...] = _adamw(w_ref[...], g, m_ref[...], v_ref[...])
        outs[4 * n][...] = added(ins[4 * n])

    grid_spec = pltpu.PrefetchScalarGridSpec(num_scalar_prefetch=1, grid=(1,), in_specs=in_specs, out_specs=out_specs)
    outs = pl.pallas_call(body, name="small_sum_adamw", grid_spec=grid_spec, out_shape=out_shape,
                          compiler_params=_params("arbitrary"))(me, *map(_in_hbm, args + [loss_parts]))
    return [outs[4 * e:4 * e + 4] for e in range(n)], outs[4 * n]


MM_TILE = 512
N_MM_TILES = SEQ // MM_TILE
CAT_TILE = 512
N_CAT_TILES = N_CAT // CAT_TILE


def kernel(x, g_mix, w_in, b_gate, w_gk_up, b_gk, w_pool_grp, pool_scale, g_gla_head, w_pool_proj, w_gla_proj, w_out, g_ffn, w_up, w_conv, b_conv, w_down, g_final, loss_target, m_g_mix, m_w_in, m_b_gate, m_w_gk_up, m_b_gk, m_w_pool_grp, m_pool_scale, m_g_gla_head, m_w_pool_proj, m_w_gla_proj, m_w_out, m_g_ffn, m_w_up, m_w_conv, m_b_conv, m_w_down, m_g_final, v_g_mix, v_w_in, v_b_gate, v_w_gk_up, v_b_gk, v_w_pool_grp, v_pool_scale, v_g_gla_head, v_w_pool_proj, v_w_gla_proj, v_w_out, v_g_ffn, v_w_up, v_w_conv, v_b_conv, v_w_down, v_g_final):
    xi, yi, ci = lax.axis_index("x"), lax.axis_index("y"), lax.axis_index("c")
    me = 4 * xi + 2 * yi + ci
    core = jnp.reshape(ci, (1,)).astype(jnp.int32)
    chip = jnp.reshape(2 * xi + yi, (1,)).astype(jnp.int32)
    xs, target = x[0], loss_target[0]

    big = dict(w_in=w_in[0].T, w_pool_proj=w_pool_proj[0], w_gla_proj=w_gla_proj[0], w_out=w_out[0], w_up=w_up[0].T,
               w_down=w_down[0])
    moments = dict(w_in=(m_w_in[0].T, v_w_in[0].T), w_pool_proj=(m_w_pool_proj[0], v_w_pool_proj[0]),
                   w_gla_proj=(m_w_gla_proj[0], v_w_gla_proj[0]), w_out=(m_w_out[0], v_w_out[0]),
                   w_up=(m_w_up[0].T, v_w_up[0].T), w_down=(m_w_down[0], v_w_down[0]))
    names = list(big)
    shards = {k: big[k].astype(BF) for k in names}
    shards["w_gk_up"], shards["w_conv"] = w_gk_up[0], w_conv[0]
    gather_groups = (("w_in", "w_gk_up"), ("w_pool_proj", "w_gla_proj", "w_out"), ("w_up", "w_down", "w_conv"))
    started, token = _split_start("gather_start", [
        ([t for k in g for t in (shards[k], _gather_landing(shards[k], me))], 4 * len(g), _gather_first)
        for g in gather_groups])

    def gather_pass(gi, after):
        lands = list(_split_wait(f"gather_wait_{gi}", started[gi], _gather_first, after)[1::2])
        passed, tkn = _split_start(f"gather_pass_{gi}", [(lands, 3 * len(lands), _gather_second)])
        return passed[0], tkn

    def gather_done(gi, passed, after):
        return dict(zip(gather_groups[gi], _split_wait(f"gather_pass_wait_{gi}", passed, _gather_second, after)))

    tok = lambda i, j, k: (i, 0)
    whole = lambda i, j, k: (0, 0)
    kblk = lambda i, j, k: (k, 0)
    ff_seq = (None, None, SEQ, FF_BLK)

    h = _rms_fwd(xs, g_mix + token[:1, :1], "rms_mix")
    wg = gather_done(0, gather_pass(0, h)[0], h)
    wt_cat, wt_gk = _unshard_w_in(wg["w_in"])
    wgk_pad = jnp.pad(wg["w_gk_up"].transpose(1, 0, 2).reshape(GATE_RANK, GLA_DK), ((0, GK_PAD - GATE_RANK), (0, 0)))
    zcat = _mm(h, wt_cat, out_shape=(SEQ, N_CAT), out_dtype=F32, grid=(N_CAT_TILES, 1, 1),
               blk_a=(SEQ, D_MODEL), blk_b=(CAT_TILE, D_MODEL), blk_o=(SEQ, CAT_TILE),
               map_a=whole, map_b=lambda j, i, k: (j, 0), map_o=lambda j, i, k: (0, j), tb=True, name="mm_in")
    la = _gk_fwd(h, wt_gk, wgk_pad, b_gk)
    passed, tkn = gather_pass(1, la)
    o, states = _gla_fwd(zcat, la, tkn)
    wg = gather_done(1, passed, o)
    wpp = wg["w_pool_proj"].transpose(1, 0, 2).reshape(POOL_WIDTH, D_MODEL)
    wgp = wg["w_gla_proj"].reshape(D_MODEL, D_MODEL)
    wout = wg["w_out"].reshape(D_MODEL, D_MODEL)
    og = _post_gla_fwd(o, zcat, g_gla_head)
    ps = _pool_fwd(zcat, w_pool_grp[0], pool_scale)
    passed, tkn = gather_pass(2, (og, ps))
    y_pool, y_gla, mixed, x1, h2 = _mix_out_fwd(ps, og, zcat, xs, wpp, wgp, wout, b_gate, g_ffn, tkn)
    wg = gather_done(2, passed, h2)
    wt_up = wg["w_up"].reshape(2 * D_FF, D_MODEL)
    wdown = wg["w_down"].reshape(D_FF, D_MODEL)
    wconv4 = wg["w_conv"].reshape(2, 4, 3, FF_BLK)
    bconv4 = b_conv.reshape(2, 4, 1, FF_BLK)
    blk4 = lambda b, i, k: (b // 4, b % 4, 0, 0)
    u4 = _mm(h2, wt_up, out_shape=(2, 4, SEQ, FF_BLK), out_dtype=F32, grid=(N_DEV, 1, 1),
             blk_a=(SEQ, D_MODEL), blk_b=(FF_BLK, D_MODEL), blk_o=ff_seq,
             map_a=whole, map_b=lambda b, i, k: (b, 0), map_o=blk4, tb=True, name="mm_up")
    act = _conv_fwd(u4, wconv4, bconv4)
    loss_part, dx2, dx2_bf, dg_final = _mm_tokens(
        act, wdown, blk_a=(None, 4, TOK_MM_TILE, FF_BLK), map_a=lambda i: (0, 0, i, 0),
        pieces=[(b, b * FF_BLK, FF_BLK) for b in range(4)], res=x1, then=("loss", g_final.reshape(1, D_MODEL), target),
        name="mm_down_loss")

    da = _mm(dx2_bf, wdown, out_shape=(1, 4, SEQ, FF_BLK), out_dtype=BF, grid=(4, 1, 1),
             blk_a=(SEQ, D_MODEL), blk_b=(FF_BLK, D_MODEL), blk_o=ff_seq,
             map_a=whole, map_b=lambda b, i, k: (b, 0), map_o=lambda b, i, k: (0, b, 0, 0), tb=True, name="mm_d_act")
    d_wdown = _mm(act, dx2_bf, out_shape=(D_FF, D_MODEL), out_dtype=BF, grid=(4, 1, 1),
                  blk_a=ff_seq, blk_b=(SEQ, D_MODEL), blk_o=(FF_BLK, D_MODEL),
                  map_a=lambda b, i, k: (0, b, 0, 0), map_b=whole, map_o=lambda b, i, k: (b, 0), ta=True,
                  name="mm_d_wdown")
    du4, d_wconv, d_bconv = _conv_bwd(u4, da, wconv4, bconv4)
    d_wt_up = _mm(du4, h2, out_shape=(2 * D_FF, D_MODEL), out_dtype=BF, grid=(N_DEV, 1, 1),
                  blk_a=ff_seq, blk_b=(SEQ, D_MODEL), blk_o=(FF_BLK, D_MODEL),
                  map_a=blk4, map_b=whole, map_o=lambda b, i, k: (b, 0), ta=True, name="mm_d_wup")
    res = {}

    def reduce_start(keys, parts):
        arrays = [t for k in keys for t in (parts[k], lax.empty((4,) + parts[k].shape[2:], BF))]
        st, tkn = _split_start("reduce_start_" + keys[0], [(arrays, 4 * len(keys), _reduce_first)])
        return st[0], tkn

    def reduce_cross(keys, st, after):
        arrays = _split_wait("reduce_wait_" + keys[0], st, _reduce_first, after)
        sums = [_pair_sum(p, r, core, "pair_sum_" + k) for k, p, r in zip(keys, arrays[0::2], arrays[1::2])]
        arrays = [t for s in sums for t in (s, lax.empty((3,) + s.shape[1:], BF))]
        st2, tkn = _split_start("reduce_cross_" + keys[0], [(arrays, 3 * len(keys), _reduce_second)])
        return st2[0], tkn

    def reduce_done(keys, st2, after):
        arrays = _split_wait("reduce_cross_wait_" + keys[0], st2, _reduce_second, after)
        for k, s, r in zip(keys, arrays[0::2], arrays[1::2]):
            outs = _chip_sum_adamw(s, r, big[k], moments[k][0], moments[k][1], chip, "adamw_" + k)
            res[k] = [(t.T if k in ("w_in", "w_up") else t)[None] for t in outs]

    ffn_keys = ("w_down", "w_up")
    ffn_red, tkn = reduce_start(ffn_keys, dict(w_down=d_wdown.reshape(4, 2, D_FF // N_DEV, D_MODEL),
                                               w_up=d_wt_up.reshape(4, 2, FF_BLK, D_MODEL)))
    dx1, dg_ffn = _mm_tokens(
        du4, wt_up, blk_a=(2, 4, TOK_MM_TILE, FF_BLK), map_a=lambda i: (0, 0, i, 0),
        pieces=[((b // 4, b % 4), b * FF_BLK, FF_BLK) for b in range(N_DEV)], after=tkn, then=("rms_bwd", x1, g_ffn, dx2),
        name="mm_d_h2_rms")

    sq_t = dict(out_shape=(D_MODEL, D_MODEL), grid=(1, 1, N_MM_TILES), blk_a=(MM_TILE, D_MODEL),
                blk_b=(MM_TILE, D_MODEL), blk_o=(D_MODEL, D_MODEL), map_a=kblk, map_b=kblk, map_o=whole, ta=True)
    d_wout = _mm(mixed, dx1, out_dtype=BF, name="mm_d_wout", **sq_t)
    dzcat, dy_pool, dy_gla, db_gate = _mix_bwd(dx1, wout, zcat, b_gate, y_pool, y_gla)
    ffn_red, tkn = reduce_cross(ffn_keys, ffn_red, db_gate)
    d_wgp = _mm(og, dy_gla, out_dtype=BF, after=tkn, name="mm_d_wgp", **sq_t)
    mix_keys = ("w_out", "w_gla_proj")
    mix_red, tkn = reduce_start(mix_keys, dict(w_out=d_wout.reshape(4, 2, D_MODEL // N_DEV, D_MODEL),
                                               w_gla_proj=d_wgp.reshape(4, 2, D_MODEL // N_DEV, D_MODEL)))
    dzcat, d_o, dg_head = _post_gla_bwd(dzcat, dy_gla, wgp, o, zcat, g_gla_head + tkn[:1, :1])
    dzcat, dla = _gla_bwd(dzcat, zcat, la, d_o, states)
    mix_red, tkn = reduce_cross(mix_keys, mix_red, dla)
    dh_gk, d_wt_gk, d_wgk, db_gk = _gk_bwd(dla, h, wt_gk, wgk_pad, b_gk + tkn[:1, :1])
    dps = _mm(dy_pool, wpp, out_shape=(SEQ, POOL_WIDTH), out_dtype=F32, grid=(N_MM_TILES, 1, 1),
              blk_a=(MM_TILE, D_MODEL), blk_b=(POOL_WIDTH, D_MODEL), blk_o=(MM_TILE, POOL_WIDTH),
              map_a=tok, map_b=whole, map_o=tok, tb=True, name="mm_d_ps")
    d_wpp = _mm(ps, dy_pool, out_shape=(POOL_WIDTH, D_MODEL), out_dtype=F32, grid=(1, 1, N_MM_TILES),
                blk_a=(MM_TILE, POOL_WIDTH), blk_b=(MM_TILE, D_MODEL), blk_o=(POOL_WIDTH, D_MODEL),
                map_a=kblk, map_b=kblk, map_o=whole, ta=True, name="mm_d_wpp")
    dzcat, d_wgrp, d_scale = _pool_bwd(dzcat, zcat, dps, w_pool_grp[0], pool_scale)
    row = lambda t: t.reshape(1, D_MODEL)
    conv_vec = lambda t: t.reshape(2, 4, 1, FF_BLK)
    small = [("b_gate", db_gate, b_gate, m_b_gate, v_b_gate, False),
             ("w_gk_up", d_wgk.reshape(GATE_RANK, N_DEV, GLA_DK // N_DEV).transpose(1, 0, 2), w_gk_up, m_w_gk_up,
              v_w_gk_up, True),
             ("b_gk", db_gk, b_gk, m_b_gk, v_b_gk, False),
             ("w_pool_grp", d_wgrp, w_pool_grp, m_w_pool_grp, v_w_pool_grp, False),
             ("pool_scale", d_scale, pool_scale, m_pool_scale, v_pool_scale, False),
             ("g_gla_head", dg_head, g_gla_head, m_g_gla_head, v_g_gla_head, False),
             ("g_ffn", dg_ffn, g_ffn, m_g_ffn, v_g_ffn, False),
             ("w_conv", d_wconv.reshape(N_DEV, 3, FF_BLK), w_conv, m_w_conv, v_w_conv, True),
             ("b_conv", d_bconv, conv_vec(b_conv), conv_vec(m_b_conv), conv_vec(v_b_conv), False),
             ("g_final", dg_final, row(g_final), row(m_g_final), row(v_g_final), False)]

    def small_start(parts, name):
        arrays = [t for p in parts for t in (p, _gather_landing(p, me))]
        st, tkn = _split_start(name, [(arrays, 7 * len(parts), _gather_direct)])
        return st[0], tkn

    small_sent, tkn = small_start([t[1] for t in small] + [loss_part], "small_start")
    d_wt_cat = _mm(dzcat, h, out_shape=(N_CAT, D_MODEL), out_dtype=BF, grid=(N_CAT_TILES, 1, 1),
                   blk_a=(SEQ, CAT_TILE), blk_b=(SEQ, D_MODEL), blk_o=(CAT_TILE, D_MODEL),
                   map_a=lambda j, i, k: (0, j), map_b=whole, map_o=lambda j, i, k: (j, 0), ta=True, after=tkn,
                   name="mm_d_wcat")
    in_keys = ("w_in", "w_pool_proj")
    in_red, tkn = reduce_start(in_keys, dict(
        w_in=_shard_d_w_in(d_wt_cat, d_wt_gk).reshape(4, 2, IN_SHARD, D_MODEL),
        w_pool_proj=d_wpp.reshape(POOL_WIDTH, N_DEV, D_MODEL // N_DEV).transpose(1, 0, 2).astype(BF)
        .reshape(4, 2, POOL_WIDTH, D_MODEL // N_DEV)))
    in_red, tkn = reduce_cross(in_keys, in_red, tkn)
    grad_x, dg_mix = _mm_tokens(dzcat, wt_cat, blk_a=(TOK_MM_TILE, N_CAT), map_a=lambda i: (i, 0),
                                pieces=[(None, 0, N_CAT)], res=dh_gk, after=tkn, then=("rms_bwd", xs, g_mix, dx1),
                                name="mm_d_h_rms")
    g_mix_sent, _ = small_start([dg_mix], "g_mix_start")
    reduce_done(ffn_keys, ffn_red, grad_x)
    reduce_done(mix_keys, mix_red, res["w_down"][0])
    gathered = _split_wait("small_wait", small_sent, _gather_direct, res["w_out"][0])[1::2]
    small.append(("g_mix", dg_mix, g_mix, m_g_mix, v_g_mix, False))
    gathered = list(gathered[:-1]) + [_split_wait("g_mix_wait", g_mix_sent, _gather_direct, gathered[0])[1], gathered[-1]]
    small_out, loss_sum = _small_sum_adamw(jnp.reshape(me, (1,)).astype(jnp.int32),
                                           [(p,) + t[2:] for p, t in zip(gathered, small)], gathered[-1])
    for t, outs in zip(small, small_out):
        res[t[0]] = list(outs)
    res["b_conv"] = [t.reshape(b_conv.shape) for t in res["b_conv"]]
    res["g_final"] = [t.reshape(g_final.shape) for t in res["g_final"]]

    reduce_done(in_keys, in_red, loss_sum)
    loss = loss_sum[0, 0]
    order =["g_mix", "w_in", "b_gate", "w_gk_up", "b_gk", "w_pool_grp", "pool_scale", "g_gla_head", "w_pool_proj",
             "w_gla_proj", "w_out", "g_ffn", "w_up", "w_conv", "b_conv", "w_down", "g_final"]
    return (loss, grad_x[None], *[res[k][0] for k in order], *[res[k][1] for k in order],
            *[res[k][2] for k in order], *[res[k][3] for k in order])
```

```python
import jax
import jax.numpy as jnp
from jax import lax
from jax.experimental import pallas as pl
from jax.experimental.pallas import tpu as pltpu

F32 = jnp.float32
BF = jnp.bfloat16
HIGHEST = lax.Precision.HIGHEST
MESH = pl.DeviceIdType.MESH

N_DEV = 8
SEQ = 2048
D_MODEL = 1024
CHUNK = 64
EPS = 1e-6
POOL_WIDTH = 512
POOL_WINDOWS = (2, 4, 8, 16)
POOL_GD = 128
POOL_HALO = 16
HEADS = 4
HK = 128
HV = 256
GLA_DK = 512
GATE_RANK = 16
GATE_NORM = 16.0
D_FF = 2816
FF_BLK = 704
IN_SHARD = 706
C_QKV, C_GATE, C_OG, C_POOL = 0, 2048, 4096, 5120
N_CAT = 5632
R_POOL, R_QKV, R_OG, R_GK, R_GATE = 0, 512, 2560, 3584, 3600
GK_PAD = 128

ADAM_LR, ADAM_B1, ADAM_B2, ADAM_EPS, ADAM_WD, ADAM_STEP = 0.001, 0.9, 0.999, 1e-08, 0.01, 10
ADAM_C1 = 1.0 - ADAM_B1 ** ADAM_STEP
ADAM_C2 = 1.0 - ADAM_B2 ** ADAM_STEP

VMEM_BYTES_V7X = 64 * 1024 * 1024
VMEM_LIMIT = VMEM_BYTES_V7X * 3 // 4

TOK_TILE = 256
HALO = 8
GLA_CPS = 4


def _params(*sem):
    return pltpu.CompilerParams(dimension_semantics=sem, vmem_limit_bytes=VMEM_LIMIT)


def _const_spec(shape):
    nd = len(shape)
    return pl.BlockSpec(shape, lambda *_: (0,) * nd)


def _in_hbm(t):
    return pltpu.with_memory_space_constraint(t, pltpu.HBM)


def _dot(a, b, ta=False, tb=False):
    dims = (((0 if ta else 1,), (1 if tb else 0,)), ((), ()))
    return lax.dot_general(a.astype(BF), b.astype(BF), dims, preferred_element_type=F32)


def _dot_exact(a, b):
    return jnp.dot(a, b, precision=HIGHEST, preferred_element_type=F32)


def _sigmoid(x):
    return 0.5 * jnp.tanh(0.5 * x) + 0.5


def _mm(a, b, *, out_shape, out_dtype, grid, blk_a, blk_b, blk_o, map_a, map_b, map_o, ta=False, tb=False,
        after=None, name):
    gk = grid[2]
    n_in = 2 + (after is not None)

    def body(*refs):
        a_ref, b_ref, o_ref = refs[0], refs[1], refs[n_in]
        prod = _dot(a_ref[...], b_ref[...], ta, tb)
        if gk == 1:
            o_ref[...] = prod.astype(out_dtype)
        else:
            acc = refs[n_in + 1]
            k = pl.program_id(2)

            @pl.when(k == 0)
            def _():
                acc[...] = prod

            @pl.when(k > 0)
            def _():
                acc[...] += prod

            @pl.when(k == gk - 1)
            def _():
                o_ref[...] = acc[...].astype(out_dtype)

    in_specs = [pl.BlockSpec(blk_a, map_a), pl.BlockSpec(blk_b, map_b)]
    args = [_in_hbm(a), _in_hbm(b)]
    if after is not None:
        in_specs.append(pl.BlockSpec(memory_space=pl.ANY))
        args.append(after)
    return pl.pallas_call(
        body, name=name, grid=grid, in_specs=in_specs, out_specs=pl.BlockSpec(blk_o, map_o),
        out_shape=jax.ShapeDtypeStruct(out_shape, out_dtype),
        scratch_shapes=[] if gk == 1 else [pltpu.VMEM(tuple(d for d in blk_o if d is not None), F32)],
        compiler_params=_params("parallel", "parallel", "arbitrary"),
    )(*args)


TOK_MM_TILE = 256


def _mm_tokens(a, w, *, blk_a, map_a, pieces, res=None, after=None, then=None, name):
    n_in = 2 + (res is not None) + (after is not None) + (0 if then is None else len(then) - 1)

    def accumulate(ref, part):
        @pl.when(pl.program_id(0) == 0)
        def _():
            ref[...] = part

        @pl.when(pl.program_id(0) > 0)
        def _():
            ref[...] += part

    def body(*refs):
        a_ref, w_ref = refs[:2]
        extra, outs = refs[n_in - (0 if then is None else len(then) - 1):n_in], refs[n_in:]
        total = None
        for idx, row, n in pieces:
            av = a_ref[...] if idx is None else a_ref[idx]
            prod = _dot(av, w_ref[row:row + n, :])
            total = prod if total is None else total + prod
        if res is not None:
            total = total + refs[2][...]
        if then is None:
            outs[0][...] = total
        elif then[0] == "rms_bwd":
            dx, part = _rms_bwd_tile(total, extra[0][...], extra[1][...], extra[2][...])
            outs[0][...] = dx
            accumulate(outs[1], part)
        else:
            lpart, dx, part = _loss_tile(total, extra[0][...], extra[1][...])
            outs[1][...] = dx
            outs[2][...] = dx.astype(BF)
            accumulate(outs[0], lpart)
            accumulate(outs[3], part)

    tile = pl.BlockSpec((TOK_MM_TILE, D_MODEL), lambda i: (i, 0))
    vec = _const_spec((1, D_MODEL))
    big = jax.ShapeDtypeStruct((SEQ, D_MODEL), F32)
    small = jax.ShapeDtypeStruct((1, D_MODEL), F32)
    in_specs = [pl.BlockSpec(blk_a, map_a), pl.BlockSpec(w.shape, lambda i: (0, 0), pipeline_mode=pl.Buffered(1))]
    args = [a, w]
    if res is not None:
        in_specs.append(tile)
        args.append(res)
    if after is not None:
        in_specs.append(pl.BlockSpec(memory_space=pl.ANY))
        args.append(after)
    if then is None:
        out_specs, out_shape = tile, big
    elif then[0] == "rms_bwd":
        in_specs += [tile, vec, tile]
        out_specs, out_shape = [tile, vec], [big, small]
    else:
        in_specs += [vec, tile]
        out_specs = [_const_spec((1, 128)), tile, tile, vec]
        out_shape = [jax.ShapeDtypeStruct((1, 128), F32), big, jax.ShapeDtypeStruct((SEQ, D_MODEL), BF), small]
    if then is not None:
        args += list(then[1:])
    return pl.pallas_call(
        body, name=name, grid=(SEQ // TOK_MM_TILE,), in_specs=in_specs, out_specs=out_specs, out_shape=out_shape,
        compiler_params=_params("parallel" if then is None else "arbitrary"),
    )(*[_in_hbm(t) for t in args])


def _rms_fwd(x, g, name):
    def body(x_ref, g_ref, o_ref):
        xv = x_ref[...]
        r = lax.rsqrt(jnp.mean(xv * xv, axis=-1, keepdims=True) + EPS)
        o_ref[...] = (xv * r * g_ref[...]).astype(BF)

    tile = pl.BlockSpec((TOK_TILE, D_MODEL), lambda i: (i, 0))
    return pl.pallas_call(
        body, name=name, grid=(SEQ // TOK_TILE,), in_specs=[tile, _const_spec((1, D_MODEL))], out_specs=tile,
        out_shape=jax.ShapeDtypeStruct((SEQ, D_MODEL), BF), compiler_params=_params("parallel"),
    )(*map(_in_hbm, (x, g)))


def _rms_bwd_tile(dyv, xv, gv, dresv):
    r = lax.rsqrt(jnp.mean(xv * xv, axis=-1, keepdims=True) + EPS)
    xn = xv * r
    dxn = dyv * gv
    return dresv + r * (dxn - xn * jnp.mean(dxn * xn, axis=-1, keepdims=True)), jnp.sum(dyv * xn, axis=0, keepdims=True)


def _loss_tile(xv, gv, tv):
    r = lax.rsqrt(jnp.mean(xv * xv, axis=-1, keepdims=True) + EPS)
    xn = xv * r
    err = xn * gv - tv
    lpart = jnp.full((1, 128), 0.5 * jnp.sum(jnp.mean(err * err, axis=-1, keepdims=True)), F32)
    dyv = err * (1.0 / D_MODEL)
    dxn = dyv * gv
    return lpart, r * (dxn - xn * jnp.mean(dxn * xn, axis=-1, keepdims=True)), jnp.sum(dyv * xn, axis=0, keepdims=True)


def _pool_counts(w):
    pos = lax.broadcasted_iota(jnp.int32, (SEQ, 1), 0).astype(F32)
    return jnp.minimum(pos + 1.0, float(w))


def _pool_window(u, w, ext):
    ext[pl.ds(POOL_HALO, SEQ), :] = u
    win = u
    for j in range(1, w):
        win = win + ext[pl.ds(POOL_HALO - j, SEQ), :]
    return win / _pool_counts(w) - u


def _pool_fwd(zcat, w_grp, scale):
    def body(z_ref, w_ref, s_ref, o_ref, ext):
        ext[pl.ds(0, POOL_HALO), :] = jnp.zeros((POOL_HALO, POOL_GD), F32)
        for g, w in enumerate(POOL_WINDOWS):
            cols = slice(g * POOL_GD, (g + 1) * POOL_GD)
            p = _pool_window(z_ref[:, cols], w, ext)
            o_ref[:, cols] = (_dot(p, w_ref[g]) * s_ref[:, cols]).astype(BF)

    return pl.pallas_call(
        body, name="pool_fwd", grid=(1,),
        in_specs=[pl.BlockSpec((SEQ, POOL_WIDTH), lambda i: (0, C_POOL // POOL_WIDTH)),
                  _const_spec((4, POOL_GD, POOL_GD)), _const_spec((1, POOL_WIDTH))],
        out_specs=_const_spec((SEQ, POOL_WIDTH)), out_shape=jax.ShapeDtypeStruct((SEQ, POOL_WIDTH), BF),
        scratch_shapes=[pltpu.VMEM((POOL_HALO + SEQ, POOL_GD), F32)], compiler_params=_params("arbitrary"),
    )(*map(_in_hbm, (zcat, w_grp, scale)))


def _pool_bwd(dzcat, zcat, dps, w_grp, scale):
    def body(dz_in, z_ref, dps_ref, w_ref, s_ref, dz_ref, dw_ref, dsc_ref, ext, ext2):
        del dz_in
        ext[pl.ds(0, POOL_HALO), :] = jnp.zeros((POOL_HALO, POOL_GD), F32)
        ext2[pl.ds(SEQ, POOL_HALO), :] = jnp.zeros((POOL_HALO, POOL_GD), F32)
        for g, w in enumerate(POOL_WINDOWS):
            cols = slice(g * POOL_GD, (g + 1) * POOL_GD)
            p = _pool_window(z_ref[:, cols], w, ext)
            wg = w_ref[g]
            pg = _dot(p, wg)
            dpsv = dps_ref[:, cols]
            dsc_ref[:, cols] = jnp.sum(dpsv * pg, axis=0, keepdims=True)
            dpg = dpsv * s_ref[:, cols]
            dw_ref[g] = _dot(p, dpg, ta=True)
            dp = _dot(dpg, wg, tb=True)
            dpc = dp / _pool_counts(w)
            ext2[pl.ds(0, SEQ), :] = dpc
            du = dpc
            for j in range(1, w):
                du = du + ext2[pl.ds(j, SEQ), :]
            dz_ref[:, cols] = (du - dp).astype(BF)

    return pl.pallas_call(
        body, name="pool_bwd", grid=(1,),
        in_specs=[pl.BlockSpec(memory_space=pl.ANY),
                  pl.BlockSpec((SEQ, POOL_WIDTH), lambda i: (0, C_POOL // POOL_WIDTH)),
                  _const_spec((SEQ, POOL_WIDTH)), _const_spec((4, POOL_GD, POOL_GD)), _const_spec((1, POOL_WIDTH))],
        out_specs=[pl.BlockSpec((SEQ, POOL_WIDTH), lambda i: (0, C_POOL // POOL_WIDTH)),
                   _const_spec((4, POOL_GD, POOL_GD)), _const_spec((1, POOL_WIDTH))],
        out_shape=[jax.ShapeDtypeStruct((SEQ, N_CAT), BF), jax.ShapeDtypeStruct((4, POOL_GD, POOL_GD), F32),
                   jax.ShapeDtypeStruct((1, POOL_WIDTH), F32)],
        scratch_shapes=[pltpu.VMEM((POOL_HALO + SEQ, POOL_GD), F32), pltpu.VMEM((SEQ + POOL_HALO, POOL_GD), F32)],
        input_output_aliases={0: 0}, compiler_params=_params("arbitrary"),
    )(*map(_in_hbm, (dzcat, zcat, dps, w_grp, scale)))


GK_TILE = 512


def _gk_fwd(h, wt_gk, wgk_pad, b_gk):
    def body(h_ref, wt_ref, w_ref, b_ref, la_ref):
        z_gk = _dot(h_ref[...], wt_ref[...], tb=True)
        pre = _dot(z_gk, w_ref[...]) + b_ref[...]
        la_ref[...] = (jnp.minimum(pre, 0.0) - jnp.log(1.0 + jnp.exp(-jnp.abs(pre)))) * (1.0 / GATE_NORM)

    return pl.pallas_call(
        body, name="gk_fwd", grid=(SEQ // GK_TILE,),
        in_specs=[pl.BlockSpec((GK_TILE, D_MODEL), lambda i: (i, 0)), _const_spec((GK_PAD, D_MODEL)),
                  _const_spec((GK_PAD, GLA_DK)), _const_spec((1, GLA_DK))],
        out_specs=pl.BlockSpec((GK_TILE, GLA_DK), lambda i: (i, 0)),
        out_shape=jax.ShapeDtypeStruct((SEQ, GLA_DK), F32), compiler_params=_params("parallel"),
    )(*map(_in_hbm, (h, wt_gk, wgk_pad, b_gk)))


def _gk_bwd(dla, h, wt_gk, wgk_pad, b_gk):
    def body(dla_ref, h_ref, wt_ref, w_ref, b_ref, dh_ref, dwt_ref, dw_ref, db_ref):
        hv = h_ref[...]
        wtv = wt_ref[...]
        wv = w_ref[...]
        z_gk = _dot(hv, wtv, tb=True)
        pre = _dot(z_gk, wv) + b_ref[...]
        dpre = dla_ref[...] * (1.0 / GATE_NORM) * (1.0 - _sigmoid(pre))
        dz_gk = _dot(dpre, wv, tb=True)
        dh_ref[...] = _dot(dz_gk, wtv)
        dwtp = _dot(dz_gk, hv, ta=True)
        dwp = _dot(z_gk, dpre, ta=True)[:GATE_RANK]
        dbp = jnp.sum(dpre, axis=0, keepdims=True)

        @pl.when(pl.program_id(0) == 0)
        def _():
            dwt_ref[...] = dwtp
            dw_ref[...] = dwp
            db_ref[...] = dbp

        @pl.when(pl.program_id(0) > 0)
        def _():
            dwt_ref[...] += dwtp
            dw_ref[...] += dwp
            db_ref[...] += dbp

    tile = pl.BlockSpec((GK_TILE, D_MODEL), lambda i: (i, 0))
    return pl.pallas_call(
        body, name="gk_bwd", grid=(SEQ // GK_TILE,),
        in_specs=[pl.BlockSpec((GK_TILE, GLA_DK), lambda i: (i, 0)), tile, _const_spec((GK_PAD, D_MODEL)),
                  _const_spec((GK_PAD, GLA_DK)), _const_spec((1, GLA_DK))],
        out_specs=[tile, _const_spec((GK_PAD, D_MODEL)), _const_spec((GATE_RANK, GLA_DK)), _const_spec((1, GLA_DK))],
        out_shape=[jax.ShapeDtypeStruct((SEQ, D_MODEL), F32), jax.ShapeDtypeStruct((GK_PAD, D_MODEL), F32),
                   jax.ShapeDtypeStruct((GATE_RANK, GLA_DK), F32), jax.ShapeDtypeStruct((1, GLA_DK), F32)],
        compiler_params=_params("arbitrary"),
    )(*map(_in_hbm, (dla, h, wt_gk, wgk_pad, b_gk)))


GLA_ROWS = GLA_CPS * CHUNK
GLA_STEPS = SEQ // GLA_ROWS
QKV_W = 2048


def _tri():
    return lax.broadcasted_iota(jnp.int32, (CHUNK, CHUNK), 0) >= lax.broadcasted_iota(jnp.int32, (CHUNK, CHUNK), 1)


def _chunk_cumsum(la_ref, rows):
    return _dot_exact(_tri().astype(F32), la_ref[rows, :])


def _gla_chunk(qkv_ref, la_ref, rows, h, bc_all):
    tri = _tri()
    q = qkv_ref[rows, h * HK:(h + 1) * HK] * (HK ** -0.5)
    k = qkv_ref[rows, GLA_DK + h * HK:GLA_DK + (h + 1) * HK]
    v = qkv_ref[rows, 2 * GLA_DK + h * HV:2 * GLA_DK + (h + 1) * HV].astype(BF)
    la = la_ref[rows, h * HK:(h + 1) * HK]
    bc = bc_all[:, h * HK:(h + 1) * HK]
    e_pos, e_neg = jnp.exp(bc), jnp.exp(-bc)
    dl = jnp.exp(jnp.sum(la, axis=0, keepdims=True))
    q_fw, q_bw, k_fw, k_bw = q * e_pos, q * e_neg, k * e_neg, k * e_pos
    scores = jnp.where(tri, _dot(q_fw, k_fw, tb=True), _dot(q_bw, k_bw, tb=True))
    return tri, v, e_pos, e_neg, dl, q_fw, q_bw, k_fw, k_bw, scores


def _gla_fwd(zcat, la, after):
    def body(qkv_ref, la_ref, after_ref, o_ref, st_ref, state):
        del after_ref

        @pl.when(pl.program_id(0) == 0)
        def _():
            state[...] = jnp.zeros_like(state)

        for c in range(GLA_CPS):
            rows = slice(c * CHUNK, (c + 1) * CHUNK)
            bc_all = _chunk_cumsum(la_ref, rows)
            for h in range(HEADS):
                _, v, _, _, dl, q_fw, _, k_fw, _, scores = _gla_chunk(qkv_ref, la_ref, rows, h, bc_all)
                st = state[h]
                st_ref[c, h] = st
                o_ref[rows, h * HV:(h + 1) * HV] = _dot(scores, v) + _dot(q_fw, st, tb=True)
                state[h] = st * dl + _dot(v, k_fw * dl, ta=True)

    return pl.pallas_call(
        body, name="gla_fwd", grid=(GLA_STEPS,),
        in_specs=[pl.BlockSpec((GLA_ROWS, QKV_W), lambda i: (i, 0)), pl.BlockSpec((GLA_ROWS, GLA_DK), lambda i: (i, 0)),
                  pl.BlockSpec(memory_space=pl.ANY)],
        out_specs=[pl.BlockSpec((GLA_ROWS, D_MODEL), lambda i: (i, 0)),
                   pl.BlockSpec((GLA_CPS, HEADS, HV, HK), lambda i: (i, 0, 0, 0))],
        out_shape=[jax.ShapeDtypeStruct((SEQ, D_MODEL), F32),
                   jax.ShapeDtypeStruct((SEQ // CHUNK, HEADS, HV, HK), F32)],
        scratch_shapes=[pltpu.VMEM((HEADS, HV, HK), F32)], compiler_params=_params("arbitrary"),
    )(*map(_in_hbm, (zcat, la)), after)


def _gla_bwd(dzcat, zcat, la, d_o, states):
    def body(dz_in, qkv_ref, la_ref, do_ref, st_ref, dqkv_ref, dla_ref, dstate):
        del dz_in

        @pl.when(pl.program_id(0) == 0)
        def _():
            dstate[...] = jnp.zeros_like(dstate)

        last_row = lax.broadcasted_iota(jnp.int32, (CHUNK, HK), 0) == CHUNK - 1
        upper = (lax.broadcasted_iota(jnp.int32, (CHUNK, CHUNK), 0)
                 <= lax.broadcasted_iota(jnp.int32, (CHUNK, CHUNK), 1)).astype(F32)
        for c in reversed(range(GLA_CPS)):
            rows = slice(c * CHUNK, (c + 1) * CHUNK)
            bc_all = _chunk_cumsum(la_ref, rows)
            dbs = []
            for h in range(HEADS):
                tri, v, e_pos, e_neg, dl, q_fw, q_bw, k_fw, k_bw, scores = _gla_chunk(qkv_ref, la_ref, rows, h, bc_all)
                st = st_ref[c, h]
                dst = dstate[h]
                d_out = do_ref[rows, h * HV:(h + 1) * HV].astype(BF)
                k_dec = k_fw * dl
                dp = _dot(d_out, v, tb=True)
                dp_fw = jnp.where(tri, dp, 0.0)
                dp_bw = jnp.where(tri, 0.0, dp)
                dv = _dot(scores, d_out, ta=True) + _dot(k_dec, dst, tb=True)
                dk_dec = _dot(v, dst)
                dq_fw = _dot(dp_fw, k_fw) + _dot(d_out, st)
                dk_fw = _dot(dp_fw, q_fw, ta=True) + dk_dec * dl
                dq_bw = _dot(dp_bw, k_bw)
                dk_bw = _dot(dp_bw, q_bw, ta=True)
                ddl = jnp.sum(st * dst, axis=0, keepdims=True) + jnp.sum(k_fw * dk_dec, axis=0, keepdims=True)
                dstate[h] = dst * dl + _dot(d_out, q_fw, ta=True)
                dq = (dq_fw * e_pos + dq_bw * e_neg) * (HK ** -0.5)
                dk = dk_fw * e_neg + dk_bw * e_pos
                dbs.append(dq_fw * q_fw - dk_fw * k_fw - dq_bw * q_bw + dk_bw * k_bw + jnp.where(last_row, ddl * dl, 0.0))
                dqkv_ref[rows, h * HK:(h + 1) * HK] = dq.astype(BF)
                dqkv_ref[rows, GLA_DK + h * HK:GLA_DK + (h + 1) * HK] = dk.astype(BF)
                dqkv_ref[rows, 2 * GLA_DK + h * HV:2 * GLA_DK + (h + 1) * HV] = dv.astype(BF)
            dla_ref[rows, :] = _dot_exact(upper, jnp.concatenate(dbs, axis=1))

    rev = lambda i: (GLA_STEPS - 1 - i, 0)
    return pl.pallas_call(
        body, name="gla_bwd", grid=(GLA_STEPS,),
        in_specs=[pl.BlockSpec(memory_space=pl.ANY), pl.BlockSpec((GLA_ROWS, QKV_W), rev),
                  pl.BlockSpec((GLA_ROWS, GLA_DK), rev), pl.BlockSpec((GLA_ROWS, D_MODEL), rev),
                  pl.BlockSpec((GLA_CPS, HEADS, HV, HK), lambda i: (GLA_STEPS - 1 - i, 0, 0, 0))],
        out_specs=[pl.BlockSpec((GLA_ROWS, QKV_W), rev), pl.BlockSpec((GLA_ROWS, GLA_DK), rev)],
        out_shape=[jax.ShapeDtypeStruct((SEQ, N_CAT), BF), jax.ShapeDtypeStruct((SEQ, GLA_DK), F32)],
        scratch_shapes=[pltpu.VMEM((HEADS, HV, HK), F32)], input_output_aliases={0: 0},
        compiler_params=_params("arbitrary"),
    )(*map(_in_hbm, (dzcat, zcat, la, d_o, states)))


def _silu_parts(x):
    s = _sigmoid(x)
    return x * s, s * (1.0 + x * (1.0 - s))


def _post_gla_fwd(o, zcat, g_head):
    def body(o_ref, zog_ref, g_ref, out_ref):
        for h in range(HEADS):
            cols = slice(h * HV, (h + 1) * HV)
            ov = o_ref[:, cols]
            r = lax.rsqrt(jnp.mean(ov * ov, axis=-1, keepdims=True) + EPS)
            act, _ = _silu_parts(zog_ref[:, cols])
            out_ref[:, cols] = (ov * r * g_ref[...] * act).astype(BF)

    tile = pl.BlockSpec((TOK_TILE, D_MODEL), lambda i: (i, 0))
    return pl.pallas_call(
        body, name="post_gla_fwd", grid=(SEQ // TOK_TILE,),
        in_specs=[tile, pl.BlockSpec((TOK_TILE, D_MODEL), lambda i: (i, C_OG // D_MODEL)), _const_spec((1, HV))],
        out_specs=tile, out_shape=jax.ShapeDtypeStruct((SEQ, D_MODEL), BF), compiler_params=_params("parallel"),
    )(*map(_in_hbm, (o, zcat, g_head)))


def _post_gla_bwd(dzcat, dy_gla, w_gla_proj, o, zcat, g_head):
    def body(dz_in, dyg_ref, w_ref, o_ref, zog_ref, g_ref, dz_ref, do_ref, dg_ref):
        del dz_in
        dog = _dot(dyg_ref[...], w_ref[...], tb=True)
        gpart = jnp.zeros((1, HV), F32)
        gv = g_ref[...]
        for h in range(HEADS):
            cols = slice(h * HV, (h + 1) * HV)
            ov = o_ref[:, cols]
            r = lax.rsqrt(jnp.mean(ov * ov, axis=-1, keepdims=True) + EPS)
            on = ov * r
            act, dact = _silu_parts(zog_ref[:, cols])
            dogv = dog[:, cols]
            dz_ref[:, cols] = (dogv * on * gv * dact).astype(BF)
            d_on_g = dogv * act
            gpart = gpart + jnp.sum(d_on_g * on, axis=0, keepdims=True)
            dxn = d_on_g * gv
            do_ref[:, cols] = r * (dxn - on * jnp.mean(dxn * on, axis=-1, keepdims=True))

        @pl.when(pl.program_id(0) == 0)
        def _():
            dg_ref[...] = gpart

        @pl.when(pl.program_id(0) > 0)
        def _():
            dg_ref[...] += gpart

    tile = pl.BlockSpec((TOK_TILE, D_MODEL), lambda i: (i, 0))
    ogspec = pl.BlockSpec((TOK_TILE, D_MODEL), lambda i: (i, C_OG // D_MODEL))
    return pl.pallas_call(
        body, name="post_gla_bwd", grid=(SEQ // TOK_TILE,),
        in_specs=[pl.BlockSpec(memory_space=pl.ANY), tile, _const_spec((D_MODEL, D_MODEL)), tile, ogspec,
                  _const_spec((1, HV))],
        out_specs=[ogspec, tile, _const_spec((1, HV))],
        out_shape=[jax.ShapeDtypeStruct((SEQ, N_CAT), BF), jax.ShapeDtypeStruct((SEQ, D_MODEL), F32),
                   jax.ShapeDtypeStruct((1, HV), F32)],
        input_output_aliases={0: 0}, compiler_params=_params("arbitrary"),
    )(*map(_in_hbm, (dzcat, dy_gla, w_gla_proj, o, zcat, g_head)))


GATE_W = 2 * D_MODEL


def _mix_out_fwd(ps, og, zcat, x, w_pool_proj, w_gla_proj, w_out, b_gate, g_ffn, after):
    def body(ps_ref, og_ref, zg_ref, x_ref, wpp_ref, wgp_ref, wout_ref, b_ref, g_ref, after_ref,
             yp_ref, yg_ref, mixed_ref, x1_ref, h2_ref):
        del after_ref
        y_pool = _dot(ps_ref[...], wpp_ref[...])
        y_gla = _dot(og_ref[...], wgp_ref[...])
        yp_ref[...] = y_pool
        yg_ref[...] = y_gla
        g0 = _sigmoid(zg_ref[:, :D_MODEL] + b_ref[:, :D_MODEL])
        g1 = _sigmoid(zg_ref[:, D_MODEL:] + b_ref[:, D_MODEL:])
        mixed = (g0 * y_pool + g1 * y_gla).astype(BF)
        mixed_ref[...] = mixed
        x1 = x_ref[...] + _dot(mixed, wout_ref[...])
        x1_ref[...] = x1
        r = lax.rsqrt(jnp.mean(x1 * x1, axis=-1, keepdims=True) + EPS)
        h2_ref[...] = (x1 * r * g_ref[...]).astype(BF)

    tile = pl.BlockSpec((TOK_TILE, D_MODEL), lambda i: (i, 0))
    resident = lambda shape: pl.BlockSpec(shape, lambda i: (0, 0), pipeline_mode=pl.Buffered(1))
    f32, bf16 = jax.ShapeDtypeStruct((SEQ, D_MODEL), F32), jax.ShapeDtypeStruct((SEQ, D_MODEL), BF)
    return pl.pallas_call(
        body, name="mix_out_fwd", grid=(SEQ // TOK_TILE,),
        in_specs=[pl.BlockSpec((TOK_TILE, POOL_WIDTH), lambda i: (i, 0)), tile,
                  pl.BlockSpec((TOK_TILE, GATE_W), lambda i: (i, C_GATE // GATE_W)), tile,
                  resident((POOL_WIDTH, D_MODEL)), resident((D_MODEL, D_MODEL)), resident((D_MODEL, D_MODEL)),
                  _const_spec((1, GATE_W)), _const_spec((1, D_MODEL)), pl.BlockSpec(memory_space=pl.ANY)],
        out_specs=[tile] * 5, out_shape=[f32, f32, bf16, f32, bf16], compiler_params=_params("parallel"),
    )(*map(_in_hbm, (ps, og, zcat, x, w_pool_proj, w_gla_proj, w_out, b_gate, g_ffn)), after)


def _mix_bwd(dx1, w_out, zcat, b_gate, y_pool, y_gla):
    def body(dx_ref, w_ref, zg_ref, b_ref, yp_ref, yg_ref, dz_ref, dyp_ref, dyg_ref, db_ref):
        dm = _dot(dx_ref[...], w_ref[...], tb=True)
        g0 = _sigmoid(zg_ref[:, :D_MODEL] + b_ref[:, :D_MODEL])
        g1 = _sigmoid(zg_ref[:, D_MODEL:] + b_ref[:, D_MODEL:])
        dyp_ref[...] = (dm * g0).astype(BF)
        dyg_ref[...] = (dm * g1).astype(BF)
        dz0 = dm * yp_ref[...] * g0 * (1.0 - g0)
        dz1 = dm * yg_ref[...] * g1 * (1.0 - g1)
        dz_ref[:, :D_MODEL] = dz0.astype(BF)
        dz_ref[:, D_MODEL:] = dz1.astype(BF)
        b0 = jnp.sum(dz0, axis=0, keepdims=True)
        b1 = jnp.sum(dz1, axis=0, keepdims=True)

        @pl.when(pl.program_id(0) == 0)
        def _():
            db_ref[:, :D_MODEL] = b0
            db_ref[:, D_MODEL:] = b1

        @pl.when(pl.program_id(0) > 0)
        def _():
            db_ref[:, :D_MODEL] += b0
            db_ref[:, D_MODEL:] += b1

    tile = pl.BlockSpec((TOK_TILE, D_MODEL), lambda i: (i, 0))
    gspec = pl.BlockSpec((TOK_TILE, GATE_W), lambda i: (i, C_GATE // GATE_W))
    return pl.pallas_call(
        body, name="mix_bwd", grid=(SEQ // TOK_TILE,),
        in_specs=[tile, _const_spec((D_MODEL, D_MODEL)), gspec, _const_spec((1, GATE_W)), tile, tile],
        out_specs=[gspec, tile, tile, _const_spec((1, GATE_W))],
        out_shape=[jax.ShapeDtypeStruct((SEQ, N_CAT), BF), jax.ShapeDtypeStruct((SEQ, D_MODEL), BF),
                   jax.ShapeDtypeStruct((SEQ, D_MODEL), BF), jax.ShapeDtypeStruct((1, GATE_W), F32)],
        compiler_params=_params("arbitrary"),
    )(*map(_in_hbm, (dx1, w_out, zcat, b_gate, y_pool, y_gla)))


N_TOK_TILES = SEQ // TOK_TILE
HALO_PER_TILE = TOK_TILE // HALO


LANE_TILES = tuple((lo, min(128, FF_BLK - lo)) for lo in range(0, FF_BLK, 128))


def _taps(w_ref, b_ref, half, lanes, rows):
    shape = (rows, lanes.stop - lanes.start)
    return ([jnp.broadcast_to(w_ref[half, j:j + 1, lanes], shape) for j in range(3)],
            jnp.broadcast_to(b_ref[half, :, lanes], shape))


def _conv_strips(u_ref, ub_ref, ua_ref, taps, lanes, width, n_strips, first):
    row = lax.broadcasted_iota(jnp.int32, (HALO, width), 0)
    prev = [[pltpu.roll(jnp.where(first, 0.0, ub_ref[half, :, lanes]), k, 0) for k in (1, 2)] for half in range(2)]
    for s in range(n_strips + (ua_ref is not None)):
        u3, conv = [], []
        for half in range(2):
            cur = u_ref[half, s * HALO:(s + 1) * HALO, lanes] if s < n_strips else ua_ref[half, :, lanes]
            rolled = [pltpu.roll(cur, k, 0) for k in (1, 2)]
            frames = [jnp.where(row >= 2, rolled[1], prev[half][1]), jnp.where(row >= 1, rolled[0], prev[half][0]), cur]
            prev[half] = rolled
            w3, bias = taps[half]
            u3.append(frames)
            conv.append(bias + frames[0] * w3[0] + frames[1] * w3[1] + frames[2] * w3[2])
        yield s, u3, conv


def _pair_specs(pairs):
    tile = pl.BlockSpec((pairs, None, TOK_TILE, FF_BLK), lambda b, i: (0, b, i, 0))
    before = pl.BlockSpec((pairs, None, HALO, FF_BLK), lambda b, i: (0, b, jnp.maximum(i * HALO_PER_TILE - 1, 0), 0))
    after = pl.BlockSpec((pairs, None, HALO, FF_BLK),
                         lambda b, i: (0, b, jnp.minimum((i + 1) * HALO_PER_TILE, SEQ // HALO - 1), 0))

    def vec(rows):
        return pl.BlockSpec((2, None, rows, FF_BLK), lambda b, i: (0, b, 0, 0))

    return tile, before, after, vec


N_STRIPS = TOK_TILE // HALO


def _up_conv_fwd(h2, wt_up, w_conv, b_conv):
    steps = N_TOK_TILES // 2

    def body(h_ref, h_next, wg_ref, wv_ref, w_ref, b_ref, u_ref, a_ref, buf_a, buf_b, carry):
        j = pl.program_id(1)

        def project(hv, buf):
            buf[0] = _dot(hv, wg_ref[...], tb=True)
            buf[1] = _dot(hv, wv_ref[...], tb=True)

        def conv(buf, row0):
            u_ref[:, row0:row0 + TOK_TILE, :] = buf[...]
            for lo, width in LANE_TILES:
                lanes = slice(lo, lo + width)
                taps = [_taps(w_ref, b_ref, half, lanes, HALO) for half in range(2)]
                pending = None
                for s, _, (cg, cv) in _conv_strips(buf, carry, None, taps, lanes, width, N_STRIPS, False):
                    act = cg * _sigmoid(cg) * cv
                    if s % 2 == 0:
                        pending = act
                    else:
                        a_ref[0, row0 + (s - 1) * HALO:row0 + (s + 1) * HALO, lanes] = (
                            jnp.concatenate([pending, act], axis=0).astype(BF))
            carry[...] = buf[:, TOK_TILE - HALO:, :]

        @pl.when(j == 0)
        def _():
            project(h_ref[0:TOK_TILE, :], buf_a)
            carry[...] = jnp.zeros_like(carry)

        project(h_ref[TOK_TILE:, :], buf_b)
        conv(buf_a, 0)
        project(h_next[...], buf_a)
        conv(buf_b, TOK_TILE)

    w_blk = lambda half: pl.BlockSpec((FF_BLK, D_MODEL), lambda b, j: (b + 4 * half, 0))
    vec = lambda rows: pl.BlockSpec((2, None, rows, FF_BLK), lambda b, j: (0, b, 0, 0))
    u_buf = pltpu.VMEM((2, TOK_TILE, FF_BLK), F32)
    return pl.pallas_call(
        body, name="up_conv_fwd", grid=(4, steps),
        in_specs=[pl.BlockSpec((2 * TOK_TILE, D_MODEL), lambda b, j: (j, 0)),
                  pl.BlockSpec((TOK_TILE, D_MODEL), lambda b, j: (jnp.minimum(2 * j + 2, N_TOK_TILES - 1), 0)),
                  w_blk(0), w_blk(1), vec(3), vec(1)],
        out_specs=[pl.BlockSpec((2, None, 2 * TOK_TILE, FF_BLK), lambda b, j: (0, b, j, 0)),
                   pl.BlockSpec((1, None, 2 * TOK_TILE, FF_BLK), lambda b, j: (0, b, j, 0))],
        out_shape=[jax.ShapeDtypeStruct((2, 4, SEQ, FF_BLK), F32), jax.ShapeDtypeStruct((1, 4, SEQ, FF_BLK), BF)],
        scratch_shapes=[u_buf, u_buf, pltpu.VMEM((2, HALO, FF_BLK), F32)],
        compiler_params=_params("parallel", "arbitrary"),
    )(*map(_in_hbm, (h2, h2, wt_up, wt_up, w_conv, b_conv)))


def _conv_bwd(u, da, w_conv, b_conv):
    def body(u_ref, ub_ref, ua_ref, da_ref, daa_ref, w_ref, b_ref, du_ref, dw_ref, db_ref):
        i = pl.program_id(1)

        @pl.when(i == 0)
        def _():
            dw_ref[...] = jnp.zeros_like(dw_ref)
            db_ref[...] = jnp.zeros_like(db_ref)

        for lo, width in LANE_TILES:
            lanes = slice(lo, lo + width)
            row = lax.broadcasted_iota(jnp.int32, (HALO, width), 0)
            taps = [_taps(w_ref, b_ref, half, lanes, HALO) for half in range(2)]
            acc_w = [[jnp.zeros((HALO, width), F32) for _ in range(3)] for _ in range(2)]
            acc_b = [jnp.zeros((HALO, width), F32) for _ in range(2)]
            da_pair, pending = None, [None, None]
            dc_prev, up_prev = [None, None], [None, None]
            for s, u3, (cg, cv) in _conv_strips(u_ref, ub_ref, ua_ref, taps, lanes, width, N_STRIPS, i == 0):
                act, dact = _silu_parts(cg)
                if s == N_STRIPS:
                    da = jnp.where(i < N_TOK_TILES - 1, daa_ref[0, :, lanes].astype(F32), 0.0)
                elif s % 2 == 0:
                    da_pair = da_ref[0, s * HALO:(s + 2) * HALO, lanes].astype(F32)
                    da = da_pair[:HALO]
                else:
                    da = da_pair[HALO:]
                dc = (da * cv * dact, da * act)
                for half in range(2):
                    up = [pltpu.roll(dc[half], HALO - k, 0) for k in (1, 2)]
                    if s < N_STRIPS:
                        for j in range(3):
                            acc_w[half][j] = acc_w[half][j] + dc[half] * u3[half][j]
                        acc_b[half] = acc_b[half] + dc[half]
                    if s >= 1:
                        w3 = taps[half][0]
                        du = (dc_prev[half] * w3[2] + jnp.where(row < HALO - 1, up_prev[half][0], up[0]) * w3[1]
                              + jnp.where(row < HALO - 2, up_prev[half][1], up[1]) * w3[0])
                        if (s - 1) % 2 == 0:
                            pending[half] = du
                        else:
                            du_ref[half, (s - 2) * HALO:s * HALO, lanes] = jnp.concatenate([pending[half], du],
                                                                                           axis=0).astype(BF)
                    dc_prev[half], up_prev[half] = dc[half], up
            for half in range(2):
                for j in range(3):
                    dw_ref[half, j:j + 1, lanes] += jnp.sum(acc_w[half][j], axis=0, keepdims=True)
                db_ref[half, :, lanes] += jnp.sum(acc_b[half], axis=0, keepdims=True)

    tile, before, after, vec = _pair_specs(2)
    da_tile, _, da_after_spec, _ = _pair_specs(1)
    return pl.pallas_call(
        body, name="conv_bwd", grid=(4, N_TOK_TILES),
        in_specs=[tile, before, after, da_tile, da_after_spec, vec(3), vec(1)],
        out_specs=[tile, vec(3), vec(1)],
        out_shape=[jax.ShapeDtypeStruct((2, 4, SEQ, FF_BLK), BF), jax.ShapeDtypeStruct((2, 4, 3, FF_BLK), F32),
                   jax.ShapeDtypeStruct((2, 4, 1, FF_BLK), F32)],
        compiler_params=_params("parallel", "arbitrary"),
    )(*map(_in_hbm, (u, u, u, da, da, w_conv, b_conv)))


W_IN_SEGMENTS = ((R_POOL, POOL_WIDTH, "cat", C_POOL), (R_QKV, QKV_W, "cat", C_QKV), (R_OG, D_MODEL, "cat", C_OG),
                 (R_GK, GATE_RANK, "gk", 0), (R_GATE, GATE_W, "cat", C_GATE))


def _slab_pieces(d):
    lo, hi = d * IN_SHARD, (d + 1) * IN_SHARD
    pieces = []
    for start, n, dest, at in W_IN_SEGMENTS:
        a, b = max(lo, start), min(hi, start + n)
        if a < b:
            assert (a - lo) % 2 == 0 and (b - a) % 2 == 0 and (at + a - start) % 2 == 0
            pieces.append(((a - lo) // 2, (b - a) // 2, dest, (at + a - start) // 2))
    return pieces


def _unshard_w_in(slabs):
    def body(slab_ref, cat_ref, gk_ref):
        d = pl.program_id(0)
        src = slab_ref.bitcast(jnp.uint32)
        dst = dict(cat=cat_ref.bitcast(jnp.uint32), gk=gk_ref.bitcast(jnp.uint32))

        @pl.when(d == 0)
        def _():
            gk_ref[...] = jnp.zeros_like(gk_ref)

        for dd in range(N_DEV):
            @pl.when(d == dd)
            def _():
                for a, n, dest, at in _slab_pieces(dd):
                    dst[dest][pl.ds(at, n), :] = src[0, pl.ds(a, n), :]

    return pl.pallas_call(
        body, name="unshard_w_in", grid=(N_DEV,),
        in_specs=[pl.BlockSpec((1, IN_SHARD, D_MODEL), lambda d: (d, 0, 0))],
        out_specs=[_const_spec((N_CAT, D_MODEL)), _const_spec((GK_PAD, D_MODEL))],
        out_shape=[jax.ShapeDtypeStruct((N_CAT, D_MODEL), BF), jax.ShapeDtypeStruct((GK_PAD, D_MODEL), BF)],
        compiler_params=_params("arbitrary"),
    )(_in_hbm(slabs))


def _shard_d_w_in(d_cat, d_gk):
    def body(cat_ref, gk_ref, slab_ref):
        d = pl.program_id(0)
        cat = cat_ref.bitcast(jnp.uint32)
        gk = pltpu.bitcast(gk_ref[0:GATE_RANK, :].astype(BF), jnp.uint32)
        dst = slab_ref.bitcast(jnp.uint32)
        for dd in range(N_DEV):
            @pl.when(d == dd)
            def _():
                for a, n, source, at in _slab_pieces(dd):
                    dst[0, pl.ds(a, n), :] = gk[at:at + n] if source == "gk" else cat[pl.ds(at, n), :]

    return pl.pallas_call(
        body, name="shard_d_w_in", grid=(N_DEV,),
        in_specs=[_const_spec((N_CAT, D_MODEL)), _const_spec((GK_PAD, D_MODEL))],
        out_specs=pl.BlockSpec((1, IN_SHARD, D_MODEL), lambda d: (d, 0, 0)),
        out_shape=jax.ShapeDtypeStruct((N_DEV, IN_SHARD, D_MODEL), BF), compiler_params=_params("parallel"),
    )(_in_hbm(d_cat), _in_hbm(d_gk))


ANY = pl.BlockSpec(memory_space=pl.ANY)


def _place():
    x, y, c = lax.axis_index("x"), lax.axis_index("y"), lax.axis_index("c")
    other_chips = [(1 - x, y), (x, 1 - y), (1 - x, 1 - y)]
    return x, y, c, other_chips


SEM = pl.BlockSpec(memory_space=pltpu.SEMAPHORE)
IN_HBM = pl.BlockSpec(memory_space=pltpu.HBM)
SPLIT_PARAMS = pltpu.CompilerParams(has_side_effects=pltpu.SideEffectType.DATAFLOW_SIDE_EFFECTING)


def _gather_first(refs, send_sems, recv_sems):
    x, y, c, chips = _place()
    targets = [(x, y, 1 - c)] + [(px, py, c) for px, py in chips]
    return [pltpu.make_async_remote_copy(src_ref=refs[2 * a], dst_ref=refs[2 * a + 1].at[4 * x + 2 * y + c],
                                         send_sem=send_sems.at[4 * a + k], recv_sem=recv_sems.at[4 * a + k],
                                         device_id=to, device_id_type=MESH)
            for a in range(len(refs) // 2) for k, to in enumerate(targets)]


def _gather_direct(refs, send_sems, recv_sems):
    x, y, c, _ = _place()
    flips = [(dx, dy, dc) for dx in (0, 1) for dy in (0, 1) for dc in (0, 1) if dx + dy + dc]
    targets = [(1 - x if dx else x, 1 - y if dy else y, 1 - c if dc else c) for dx, dy, dc in flips]
    return [pltpu.make_async_remote_copy(src_ref=refs[2 * a], dst_ref=refs[2 * a + 1].at[4 * x + 2 * y + c],
                                         send_sem=send_sems.at[7 * a + k], recv_sem=recv_sems.at[7 * a + k],
                                         device_id=to, device_id_type=MESH)
            for a in range(len(refs) // 2) for k, to in enumerate(targets)]


def _gather_second(refs, send_sems, recv_sems):
    x, y, c, chips = _place()
    copies = []
    for a, land in enumerate(refs):
        for j, (px, py) in enumerate(chips):
            block = land.at[4 * px + 2 * py + c]
            copies.append(pltpu.make_async_remote_copy(src_ref=block, dst_ref=block, send_sem=send_sems.at[3 * a + j],
                                                       recv_sem=recv_sems.at[3 * a + j], device_id=(x, y, 1 - c),
                                                       device_id_type=MESH))
    return copies


def _reduce_first(refs, send_sems, recv_sems):
    x, y, c, _ = _place()
    return [pltpu.make_async_remote_copy(src_ref=refs[2 * a].at[j, 1 - c], dst_ref=refs[2 * a + 1].at[j],
                                         send_sem=send_sems.at[4 * a + j], recv_sem=recv_sems.at[4 * a + j],
                                         device_id=(x, y, 1 - c), device_id_type=MESH)
            for a in range(len(refs) // 2) for j in range(4)]


def _reduce_second(refs, send_sems, recv_sems):
    _, _, c, chips = _place()
    return [pltpu.make_async_remote_copy(src_ref=refs[2 * a].at[2 * px + py], dst_ref=refs[2 * a + 1].at[k],
                                         send_sem=send_sems.at[3 * a + k], recv_sem=recv_sems.at[3 * a + k],
                                         device_id=(px, py, c), device_id_type=MESH)
            for a in range(len(refs) // 2) for k, (px, py) in enumerate(chips)]


def _split_start(name, groups):
    arrays = [a for g in groups for a in g[0]]
    n = len(arrays)

    def body(*refs):
        sems = refs[n:n + 2 * len(groups)]
        at = 0
        for gi, (members, _, build) in enumerate(groups):
            for cp in build(refs[at:at + len(members)], sems[2 * gi], sems[2 * gi + 1]):
                cp.start()
            at += len(members)
        refs[-1][...] = jnp.zeros_like(refs[-1])

    sem_shapes = [pltpu.SemaphoreType.DMA((g[1],)) for g in groups for _ in range(2)]
    outs = pl.pallas_call(
        body, name=name, in_specs=[IN_HBM] * n,
        out_shape=(*sem_shapes, *[pltpu.HBM(a.shape, a.dtype) for a in arrays], jax.ShapeDtypeStruct((8, 128), F32)),
        out_specs=(*[SEM] * len(sem_shapes), *[IN_HBM] * n, pl.BlockSpec(memory_space=pltpu.VMEM)),
        input_output_aliases={i: len(sem_shapes) + i for i in range(n)}, compiler_params=SPLIT_PARAMS,
    )(*[pltpu.with_memory_space_constraint(a, pltpu.HBM) for a in arrays])
    per_group, at = [], len(sem_shapes)
    for gi, (members, _, _) in enumerate(groups):
        per_group.append((outs[2 * gi], outs[2 * gi + 1], list(outs[at:at + len(members)])))
        at += len(members)
    return per_group, outs[-1]


def _split_wait(name, started, build, after):
    send_sems, recv_sems, arrays = started
    n = len(arrays)
    after = after if isinstance(after, (tuple, list)) else (after,)

    def body(*refs):
        for cp in build(refs[:n], refs[n], refs[n + 1]):
            cp.wait_send()
            cp.wait_recv()

    return pl.pallas_call(
        body, name=name, in_specs=[IN_HBM] * n + [SEM, SEM] + [ANY] * len(after),
        out_shape=tuple(pltpu.HBM(a.shape, a.dtype) for a in arrays), out_specs=tuple([IN_HBM] * n),
        input_output_aliases={i: i for i in range(n)}, compiler_params=SPLIT_PARAMS,
    )(*arrays, send_sems, recv_sems, *after)


def _gather_landing(shard, me):
    return lax.dynamic_update_slice(lax.empty((N_DEV,) + shard.shape, shard.dtype), shard[None],
                                    (me,) + (0,) * shard.ndim)


def _tile_2d(rows, cols):
    for t in (256, 176, 128):
        if rows % t == 0:
            return t, cols
    return rows, 256


def _pair_sum(part, recv, core, name):
    _, rows, cols = recv.shape
    tr, tc = rows, cols

    def body(c_ref, p_ref, r_ref, o_ref):
        del c_ref
        o_ref[...] = (p_ref[...].astype(F32) + r_ref[...].astype(F32)).astype(BF)

    grid_spec = pltpu.PrefetchScalarGridSpec(
        num_scalar_prefetch=1, grid=(4, rows // tr, cols // tc),
        in_specs=[pl.BlockSpec((None, None, tr, tc), lambda j, i, k, c_ref: (j, c_ref[0], i, k)),
                  pl.BlockSpec((None, tr, tc), lambda j, i, k, c_ref: (j, i, k))],
        out_specs=pl.BlockSpec((None, tr, tc), lambda j, i, k, c_ref: (j, i, k)))
    return pl.pallas_call(
        body, name=name, grid_spec=grid_spec, out_shape=jax.ShapeDtypeStruct(recv.shape, BF),
        compiler_params=_params("parallel", "parallel", "parallel"),
    )(core, *map(_in_hbm, (part, recv)))


def _adamw(w, g, m, v):
    m = ADAM_B1 * m + (1.0 - ADAM_B1) * g
    v = ADAM_B2 * v + (1.0 - ADAM_B2) * (g * g)
    delta = -ADAM_LR * ((m / ADAM_C1) / (jnp.sqrt(v / ADAM_C2) + ADAM_EPS) + ADAM_WD * w)
    return delta, m, v


def _chip_sum_adamw(sums, recv, w, m, v, chip, name):
    rows, cols = w.shape
    tr, tc = _tile_2d(rows, cols)

    def body(chip_ref, s_ref, r_ref, w_ref, m_ref, v_ref, g_out, d_out, m_out, v_out):
        del chip_ref
        g = s_ref[...].astype(F32)
        for k in range(3):
            g = g + r_ref[k].astype(F32)
        g_out[...] = g
        d_out[...], m_out[...], v_out[...] = _adamw(w_ref[...], g, m_ref[...], v_ref[...])

    tile = pl.BlockSpec((tr, tc), lambda i, k, chip_ref: (i, k))
    grid_spec = pltpu.PrefetchScalarGridSpec(
        num_scalar_prefetch=1, grid=(rows // tr, cols // tc),
        in_specs=[pl.BlockSpec((None, tr, tc), lambda i, k, chip_ref: (chip_ref[0], i, k)),
                  pl.BlockSpec((3, tr, tc), lambda i, k, chip_ref: (0, i, k)), tile, tile, tile],
        out_specs=[tile] * 4)
    return pl.pallas_call(
        body, name=name, grid_spec=grid_spec, out_shape=[jax.ShapeDtypeStruct((rows, cols), F32)] * 4,
        compiler_params=_params("parallel", "parallel"),
    )(chip, *map(_in_hbm, (sums, recv, w, m, v)))


def _small_sum_adamw(me, entries, loss_parts):
    def whole(shape, squeeze=0, pick=False):
        blk = (None,) * squeeze + tuple(shape[squeeze:])
        if pick:
            blk = (shape[0], None) + tuple(shape[2:])
            return pl.BlockSpec(blk, lambda i, me_ref: (0, me_ref[0]) + (0,) * (len(shape) - 2))
        return pl.BlockSpec(blk, lambda i, me_ref: (0,) * len(shape))

    in_specs, out_specs, out_shape, args = [], [], [], []
    for parts, w, m, v, sharded in entries:
        lead = w.ndim - (parts.ndim - (2 if sharded else 1))
        in_specs += [whole(parts.shape, pick=sharded)] + [whole(w.shape, squeeze=lead)] * 3
        out_specs += [whole(w.shape, squeeze=lead)] * 4
        out_shape += [jax.ShapeDtypeStruct(w.shape, F32)] * 4
        args += [parts, w, m, v]
    in_specs.append(whole(loss_parts.shape))
    out_specs.append(whole(loss_parts.shape[1:]))
    out_shape.append(jax.ShapeDtypeStruct(loss_parts.shape[1:], F32))
    n = len(entries)

    def added(p_ref):
        total = p_ref[0]
        for d in range(1, N_DEV):
            total = total + p_ref[d]
        return total

    def body(me_ref, *refs):
        del me_ref
        ins, outs = refs[:4 * n + 1], refs[4 * n + 1:]
        for e in range(n):
            p_ref, w_ref, m_ref, v_ref = ins[4 * e:4 * e + 4]
            g_out, d_out, m_out, v_out = outs[4 * e:4 * e + 4]
            g = added(p_ref)
            g_out[...] = g
            d_out[...], m_out[...], v_out[...] = _adamw(w_ref[...], g, m_ref[...], v_ref[...])
        outs[4 * n][...] = added(ins[4 * n])

    grid_spec = pltpu.PrefetchScalarGridSpec(num_scalar_prefetch=1, grid=(1,), in_specs=in_specs, out_specs=out_specs)
    outs = pl.pallas_call(body, name="small_sum_adamw", grid_spec=grid_spec, out_shape=out_shape,
                          compiler_params=_params("arbitrary"))(me, *map(_in_hbm, args + [loss_parts]))
    return [outs[4 * e:4 * e + 4] for e in range(n)], outs[4 * n]


MM_TILE = 512
N_MM_TILES = SEQ // MM_TILE
CAT_TILE = 512
N_CAT_TILES = N_CAT // CAT_TILE


def kernel(x, g_mix, w_in, b_gate, w_gk_up, b_gk, w_pool_grp, pool_scale, g_gla_head, w_pool_proj, w_gla_proj, w_out, g_ffn, w_up, w_conv, b_conv, w_down, g_final, loss_target, m_g_mix, m_w_in, m_b_gate, m_w_gk_up, m_b_gk, m_w_pool_grp, m_pool_scale, m_g_gla_head, m_w_pool_proj, m_w_gla_proj, m_w_out, m_g_ffn, m_w_up, m_w_conv, m_b_conv, m_w_down, m_g_final, v_g_mix, v_w_in, v_b_gate, v_w_gk_up, v_b_gk, v_w_pool_grp, v_pool_scale, v_g_gla_head, v_w_pool_proj, v_w_gla_proj, v_w_out, v_g_ffn, v_w_up, v_w_conv, v_b_conv, v_w_down, v_g_final):
    xi, yi, ci = lax.axis_index("x"), lax.axis_index("y"), lax.axis_index("c")
    me = 4 * xi + 2 * yi + ci
    core = jnp.reshape(ci, (1,)).astype(jnp.int32)
    chip = jnp.reshape(2 * xi + yi, (1,)).astype(jnp.int32)
    xs, target = x[0], loss_target[0]

    big = dict(w_in=w_in[0].T, w_pool_proj=w_pool_proj[0], w_gla_proj=w_gla_proj[0], w_out=w_out[0], w_up=w_up[0].T,
               w_down=w_down[0])
    moments = dict(w_in=(m_w_in[0].T, v_w_in[0].T), w_pool_proj=(m_w_pool_proj[0], v_w_pool_proj[0]),
                   w_gla_proj=(m_w_gla_proj[0], v_w_gla_proj[0]), w_out=(m_w_out[0], v_w_out[0]),
                   w_up=(m_w_up[0].T, v_w_up[0].T), w_down=(m_w_down[0], v_w_down[0]))
    names = list(big)
    shards = {k: big[k].astype(BF) for k in names}
    shards["w_gk_up"], shards["w_conv"] = w_gk_up[0], w_conv[0]
    gather_groups = (("w_in", "w_gk_up"), ("w_pool_proj", "w_gla_proj", "w_out"), ("w_up", "w_down", "w_conv"))
    started, token = _split_start("gather_start", [
        ([t for k in g for t in (shards[k], _gather_landing(shards[k], me))], 4 * len(g), _gather_first)
        for g in gather_groups])

    def gather_pass(gi, after):
        lands = list(_split_wait(f"gather_wait_{gi}", started[gi], _gather_first, after)[1::2])
        passed, tkn = _split_start(f"gather_pass_{gi}", [(lands, 3 * len(lands), _gather_second)])
        return passed[0], tkn

    def gather_done(gi, passed, after):
        return dict(zip(gather_groups[gi], _split_wait(f"gather_pass_wait_{gi}", passed, _gather_second, after)))

    tok = lambda i, j, k: (i, 0)
    whole = lambda i, j, k: (0, 0)
    kblk = lambda i, j, k: (k, 0)
    ff_seq = (None, None, SEQ, FF_BLK)

    h = _rms_fwd(xs, g_mix + token[:1, :1], "rms_mix")
    wg = gather_done(0, gather_pass(0, h)[0], h)
    wt_cat, wt_gk = _unshard_w_in(wg["w_in"])
    wgk_pad = jnp.pad(wg["w_gk_up"].transpose(1, 0, 2).reshape(GATE_RANK, GLA_DK), ((0, GK_PAD - GATE_RANK), (0, 0)))
    zcat = _mm(h, wt_cat, out_shape=(SEQ, N_CAT), out_dtype=F32, grid=(N_CAT_TILES, 1, 1),
               blk_a=(SEQ, D_MODEL), blk_b=(CAT_TILE, D_MODEL), blk_o=(SEQ, CAT_TILE),
               map_a=whole, map_b=lambda j, i, k: (j, 0), map_o=lambda j, i, k: (0, j), tb=True, name="mm_in")
    la = _gk_fwd(h, wt_gk, wgk_pad, b_gk)
    passed, tkn = gather_pass(1, la)
    o, states = _gla_fwd(zcat, la, tkn)
    wg = gather_done(1, passed, o)
    wpp = wg["w_pool_proj"].transpose(1, 0, 2).reshape(POOL_WIDTH, D_MODEL)
    wgp = wg["w_gla_proj"].reshape(D_MODEL, D_MODEL)
    wout = wg["w_out"].reshape(D_MODEL, D_MODEL)
    og = _post_gla_fwd(o, zcat, g_gla_head)
    ps = _pool_fwd(zcat, w_pool_grp[0], pool_scale)
    passed, tkn = gather_pass(2, (og, ps))
    y_pool, y_gla, mixed, x1, h2 = _mix_out_fwd(ps, og, zcat, xs, wpp, wgp, wout, b_gate, g_ffn, tkn)
    wg = gather_done(2, passed, h2)
    wt_up = wg["w_up"].reshape(2 * D_FF, D_MODEL)
    wdown = wg["w_down"].reshape(D_FF, D_MODEL)
    wconv4 = wg["w_conv"].reshape(2, 4, 3, FF_BLK)
    bconv4 = b_conv.reshape(2, 4, 1, FF_BLK)
    blk4 = lambda b, i, k: (b // 4, b % 4, 0, 0)
    u4, act = _up_conv_fwd(h2, wt_up, wconv4, bconv4)
    loss_part, dx2, dx2_bf, dg_final = _mm_tokens(
        act, wdown, blk_a=(None, 4, TOK_MM_TILE, FF_BLK), map_a=lambda i: (0, 0, i, 0),
        pieces=[(b, b * FF_BLK, FF_BLK) for b in range(4)], res=x1, then=("loss", g_final.reshape(1, D_MODEL), target),
        name="mm_down_loss")

    da = _mm(dx2_bf, wdown, out_shape=(1, 4, SEQ, FF_BLK), out_dtype=BF, grid=(4, 1, 1),
             blk_a=(SEQ, D_MODEL), blk_b=(FF_BLK, D_MODEL), blk_o=ff_seq,
             map_a=whole, map_b=lambda b, i, k: (b, 0), map_o=lambda b, i, k: (0, b, 0, 0), tb=True, name="mm_d_act")
    d_wdown = _mm(act, dx2_bf, out_shape=(D_FF, D_MODEL), out_dtype=BF, grid=(4, 1, 1),
                  blk_a=ff_seq, blk_b=(SEQ, D_MODEL), blk_o=(FF_BLK, D_MODEL),
                  map_a=lambda b, i, k: (0, b, 0, 0), map_b=whole, map_o=lambda b, i, k: (b, 0), ta=True,
                  name="mm_d_wdown")
    du4, d_wconv, d_bconv = _conv_bwd(u4, da, wconv4, bconv4)
    d_wt_up = _mm(du4, h2, out_shape=(2 * D_FF, D_MODEL), out_dtype=BF, grid=(N_DEV, 1, 1),
                  blk_a=ff_seq, blk_b=(SEQ, D_MODEL), blk_o=(FF_BLK, D_MODEL),
                  map_a=blk4, map_b=whole, map_o=lambda b, i, k: (b, 0), ta=True, name="mm_d_wup")
    res = {}

    def reduce_start(keys, parts):
        arrays = [t for k in keys for t in (parts[k], lax.empty((4,) + parts[k].shape[2:], BF))]
        st, tkn = _split_start("reduce_start_" + keys[0], [(arrays, 4 * len(keys), _reduce_first)])
        return st[0], tkn

    def reduce_cross(keys, st, after):
        arrays = _split_wait("reduce_wait_" + keys[0], st, _reduce_first, after)
        sums = [_pair_sum(p, r, core, "pair_sum_" + k) for k, p, r in zip(keys, arrays[0::2], arrays[1::2])]
        arrays = [t for s in sums for t in (s, lax.empty((3,) + s.shape[1:], BF))]
        st2, tkn = _split_start("reduce_cross_" + keys[0], [(arrays, 3 * len(keys), _reduce_second)])
        return st2[0], tkn

    def reduce_done(keys, st2, after):
        arrays = _split_wait("reduce_cross_wait_" + keys[0], st2, _reduce_second, after)
        for k, s, r in zip(keys, arrays[0::2], arrays[1::2]):
            outs = _chip_sum_adamw(s, r, big[k], moments[k][0], moments[k][1], chip, "adamw_" + k)
            res[k] = [(t.T if k in ("w_in", "w_up") else t)[None] for t in outs]

    ffn_keys = ("w_down", "w_up")
    ffn_red, tkn = reduce_start(ffn_keys, dict(w_down=d_wdown.reshape(4, 2, D_FF // N_DEV, D_MODEL),
                                               w_up=d_wt_up.reshape(4, 2, FF_BLK, D_MODEL)))
    dx1, dg_ffn = _mm_tokens(
        du4, wt_up, blk_a=(2, 4, TOK_MM_TILE, FF_BLK), map_a=lambda i: (0, 0, i, 0),
        pieces=[((b // 4, b % 4), b * FF_BLK, FF_BLK) for b in range(N_DEV)], after=tkn, then=("rms_bwd", x1, g_ffn, dx2),
        name="mm_d_h2_rms")

    sq_t = dict(out_shape=(D_MODEL, D_MODEL), grid=(1, 1, N_MM_TILES), blk_a=(MM_TILE, D_MODEL),
                blk_b=(MM_TILE, D_MODEL), blk_o=(D_MODEL, D_MODEL), map_a=kblk, map_b=kblk, map_o=whole, ta=True)
    d_wout = _mm(mixed, dx1, out_dtype=BF, name="mm_d_wout", **sq_t)
    dzcat, dy_pool, dy_gla, db_gate = _mix_bwd(dx1, wout, zcat, b_gate, y_pool, y_gla)
    ffn_red, tkn = reduce_cross(ffn_keys, ffn_red, db_gate)
    d_wgp = _mm(og, dy_gla, out_dtype=BF, after=tkn, name="mm_d_wgp", **sq_t)
    mix_keys = ("w_out", "w_gla_proj")
    mix_red, tkn = reduce_start(mix_keys, dict(w_out=d_wout.reshape(4, 2, D_MODEL // N_DEV, D_MODEL),
                                               w_gla_proj=d_wgp.reshape(4, 2, D_MODEL // N_DEV, D_MODEL)))
    dzcat, d_o, dg_head = _post_gla_bwd(dzcat, dy_gla, wgp, o, zcat, g_gla_head + tkn[:1, :1])
    dzcat, dla = _gla_bwd(dzcat, zcat, la, d_o, states)
    mix_red, tkn = reduce_cross(mix_keys, mix_red, dla)
    dh_gk, d_wt_gk, d_wgk, db_gk = _gk_bwd(dla, h, wt_gk, wgk_pad, b_gk + tkn[:1, :1])
    dps = _mm(dy_pool, wpp, out_shape=(SEQ, POOL_WIDTH), out_dtype=F32, grid=(N_MM_TILES, 1, 1),
              blk_a=(MM_TILE, D_MODEL), blk_b=(POOL_WIDTH, D_MODEL), blk_o=(MM_TILE, POOL_WIDTH),
              map_a=tok, map_b=whole, map_o=tok, tb=True, name="mm_d_ps")
    d_wpp = _mm(ps, dy_pool, out_shape=(POOL_WIDTH, D_MODEL), out_dtype=F32, grid=(1, 1, N_MM_TILES),
                blk_a=(MM_TILE, POOL_WIDTH), blk_b=(MM_TILE, D_MODEL), blk_o=(POOL_WIDTH, D_MODEL),
                map_a=kblk, map_b=kblk, map_o=whole, ta=True, name="mm_d_wpp")
    dzcat, d_wgrp, d_scale = _pool_bwd(dzcat, zcat, dps, w_pool_grp[0], pool_scale)
    row = lambda t: t.reshape(1, D_MODEL)
    conv_vec = lambda t: t.reshape(2, 4, 1, FF_BLK)
    small = [("b_gate", db_gate, b_gate, m_b_gate, v_b_gate, False),
             ("w_gk_up", d_wgk.reshape(GATE_RANK, N_DEV, GLA_DK // N_DEV).transpose(1, 0, 2), w_gk_up, m_w_gk_up,
              v_w_gk_up, True),
             ("b_gk", db_gk, b_gk, m_b_gk, v_b_gk, False),
             ("w_pool_grp", d_wgrp, w_pool_grp, m_w_pool_grp, v_w_pool_grp, False),
             ("pool_scale", d_scale, pool_scale, m_pool_scale, v_pool_scale, False),
             ("g_gla_head", dg_head, g_gla_head, m_g_gla_head, v_g_gla_head, False),
             ("g_ffn", dg_ffn, g_ffn, m_g_ffn, v_g_ffn, False),
             ("w_conv", d_wconv.reshape(N_DEV, 3, FF_BLK), w_conv, m_w_conv, v_w_conv, True),
             ("b_conv", d_bconv, conv_vec(b_conv), conv_vec(m_b_conv), conv_vec(v_b_conv), False),
             ("g_final", dg_final, row(g_final), row(m_g_final), row(v_g_final), False)]

    def small_start(parts, name):
        arrays = [t for p in parts for t in (p, _gather_landing(p, me))]
        st, tkn = _split_start(name, [(arrays, 7 * len(parts), _gather_direct)])
        return st[0], tkn

    small_sent, tkn = small_start([t[1] for t in small] + [loss_part], "small_start")
    d_wt_cat = _mm(dzcat, h, out_shape=(N_CAT, D_MODEL), out_dtype=BF, grid=(N_CAT_TILES, 1, 1),
                   blk_a=(SEQ, CAT_TILE), blk_b=(SEQ, D_MODEL), blk_o=(CAT_TILE, D_MODEL),
                   map_a=lambda j, i, k: (0, j), map_b=whole, map_o=lambda j, i, k: (j, 0), ta=True, after=tkn,
                   name="mm_d_wcat")
    in_keys = ("w_in", "w_pool_proj")
    in_red, tkn = reduce_start(in_keys, dict(
        w_in=_shard_d_w_in(d_wt_cat, d_wt_gk).reshape(4, 2, IN_SHARD, D_MODEL),
        w_pool_proj=d_wpp.reshape(POOL_WIDTH, N_DEV, D_MODEL // N_DEV).transpose(1, 0, 2).astype(BF)
        .reshape(4, 2, POOL_WIDTH, D_MODEL // N_DEV)))
    in_red, tkn = reduce_cross(in_keys, in_red, tkn)
    grad_x, dg_mix = _mm_tokens(dzcat, wt_cat, blk_a=(TOK_MM_TILE, N_CAT), map_a=lambda i: (i, 0),
                                pieces=[(None, 0, N_CAT)], res=dh_gk, after=tkn, then=("rms_bwd", xs, g_mix, dx1),
                                name="mm_d_h_rms")
    g_mix_sent, _ = small_start([dg_mix], "g_mix_start")
    reduce_done(ffn_keys, ffn_red, grad_x)
    reduce_done(mix_keys, mix_red, res["w_down"][0])
    gathered = _split_wait("small_wait", small_sent, _gather_direct, res["w_out"][0])[1::2]
    small.append(("g_mix", dg_mix, g_mix, m_g_mix, v_g_mix, False))
    gathered = list(gathered[:-1]) + [_split_wait("g_mix_wait", g_mix_sent, _gather_direct, gathered[0])[1], gathered[-1]]
    small_out, loss_sum = _small_sum_adamw(jnp.reshape(me, (1,)).astype(jnp.int32),
                                           [(p,) + t[2:] for p, t in zip(gathered, small)], gathered[-1])
    for t, outs in zip(small, small_out):
        res[t[0]] = list(outs)
    res["b_conv"] = [t.reshape(b_conv.shape) for t in res["b_conv"]]
    res["g_final"] = [t.reshape(g_final.shape) for t in res["g_final"]]

    reduce_done(in_keys, in_red, loss_sum)
    loss = loss_sum[0, 0]
    order =["g_mix", "w_in", "b_gate", "w_gk_up", "b_gk", "w_pool_grp", "pool_scale", "g_gla_head", "w_pool_proj",
             "w_gla_proj", "w_out", "g_ffn", "w_up", "w_conv", "b_conv", "w_down", "g_final"]
    return (loss, grad_x[None], *[res[k][0] for k in order], *[res[k][1] for k in order],
            *[res[k][2] for k in order], *[res[k][3] for k in order])
```

```python
import jax
import jax.numpy as jnp
from jax import lax
from jax.experimental import pallas as pl
from jax.experimental.pallas import tpu as pltpu

F32 = jnp.float32
BF = jnp.bfloat16
HIGHEST = lax.Precision.HIGHEST
MESH = pl.DeviceIdType.MESH

N_DEV = 8
SEQ = 2048
D_MODEL = 1024
CHUNK = 64
EPS = 1e-6
POOL_WIDTH = 512
POOL_WINDOWS = (2, 4, 8, 16)
POOL_GD = 128
POOL_HALO = 16
HEADS = 4
HK = 128
HV = 256
GLA_DK = 512
GATE_RANK = 16
GATE_NORM = 16.0
D_FF = 2816
FF_BLK = 704
IN_SHARD = 706
C_QKV, C_GATE, C_OG, C_POOL = 0, 2048, 4096, 5120
N_CAT = 5632
R_POOL, R_QKV, R_OG, R_GK, R_GATE = 0, 512, 2560, 3584, 3600
GK_PAD = 128

ADAM_LR, ADAM_B1, ADAM_B2, ADAM_EPS, ADAM_WD, ADAM_STEP = 0.001, 0.9, 0.999, 1e-08, 0.01, 10
ADAM_C1 = 1.0 - ADAM_B1 ** ADAM_STEP
ADAM_C2 = 1.0 - ADAM_B2 ** ADAM_STEP

VMEM_BYTES_V7X = 64 * 1024 * 1024
VMEM_LIMIT = VMEM_BYTES_V7X * 3 // 4

TOK_TILE = 256
HALO = 8
GLA_CPS = 4


def _params(*sem):
    return pltpu.CompilerParams(dimension_semantics=sem, vmem_limit_bytes=VMEM_LIMIT)


def _const_spec(shape):
    nd = len(shape)
    return pl.BlockSpec(shape, lambda *_: (0,) * nd)


def _in_hbm(t):
    return pltpu.with_memory_space_constraint(t, pltpu.HBM)


def _dot(a, b, ta=False, tb=False):
    dims = (((0 if ta else 1,), (1 if tb else 0,)), ((), ()))
    return lax.dot_general(a.astype(BF), b.astype(BF), dims, preferred_element_type=F32)


def _dot_exact(a, b):
    return jnp.dot(a, b, precision=HIGHEST, preferred_element_type=F32)


def _sigmoid(x):
    return 0.5 * jnp.tanh(0.5 * x) + 0.5


def _mm(a, b, *, out_shape, out_dtype, grid, blk_a, blk_b, blk_o, map_a, map_b, map_o, ta=False, tb=False,
        after=None, name):
    gk = grid[2]
    n_in = 2 + (after is not None)

    def body(*refs):
        a_ref, b_ref, o_ref = refs[0], refs[1], refs[n_in]
        prod = _dot(a_ref[...], b_ref[...], ta, tb)
        if gk == 1:
            o_ref[...] = prod.astype(out_dtype)
        else:
            acc = refs[n_in + 1]
            k = pl.program_id(2)

            @pl.when(k == 0)
            def _():
                acc[...] = prod

            @pl.when(k > 0)
            def _():
                acc[...] += prod

            @pl.when(k == gk - 1)
            def _():
                o_ref[...] = acc[...].astype(out_dtype)

    in_specs = [pl.BlockSpec(blk_a, map_a), pl.BlockSpec(blk_b, map_b)]
    args = [_in_hbm(a), _in_hbm(b)]
    if after is not None:
        in_specs.append(pl.BlockSpec(memory_space=pl.ANY))
        args.append(after)
    return pl.pallas_call(
        body, name=name, grid=grid, in_specs=in_specs, out_specs=pl.BlockSpec(blk_o, map_o),
        out_shape=jax.ShapeDtypeStruct(out_shape, out_dtype),
        scratch_shapes=[] if gk == 1 else [pltpu.VMEM(tuple(d for d in blk_o if d is not None), F32)],
        compiler_params=_params("parallel", "parallel", "arbitrary"),
    )(*args)


TOK_MM_TILE = 256


def _mm_tokens(a, w, *, blk_a, map_a, pieces, res=None, after=None, then=None, name):
    n_in = 2 + (res is not None) + (after is not None) + (0 if then is None else len(then) - 1)

    def accumulate(ref, part):
        @pl.when(pl.program_id(0) == 0)
        def _():
            ref[...] = part

        @pl.when(pl.program_id(0) > 0)
        def _():
            ref[...] += part

    def body(*refs):
        a_ref, w_ref = refs[:2]
        extra, outs = refs[n_in - (0 if then is None else len(then) - 1):n_in], refs[n_in:]
        total = None
        for idx, row, n in pieces:
            av = a_ref[...] if idx is None else a_ref[idx]
            prod = _dot(av, w_ref[row:row + n, :])
            total = prod if total is None else total + prod
        if res is not None:
            total = total + refs[2][...]
        if then is None:
            outs[0][...] = total
        elif then[0] == "rms_bwd":
            dx, part = _rms_bwd_tile(total, extra[0][...], extra[1][...], extra[2][...])
            outs[0][...] = dx
            accumulate(outs[1], part)
        else:
            lpart, dx, part = _loss_tile(total, extra[0][...], extra[1][...])
            outs[1][...] = dx
            outs[2][...] = dx.astype(BF)
            accumulate(outs[0], lpart)
            accumulate(outs[3], part)

    tile = pl.BlockSpec((TOK_MM_TILE, D_MODEL), lambda i: (i, 0))
    vec = _const_spec((1, D_MODEL))
    big = jax.ShapeDtypeStruct((SEQ, D_MODEL), F32)
    small = jax.ShapeDtypeStruct((1, D_MODEL), F32)
    in_specs = [pl.BlockSpec(blk_a, map_a), pl.BlockSpec(w.shape, lambda i: (0, 0), pipeline_mode=pl.Buffered(1))]
    args = [a, w]
    if res is not None:
        in_specs.append(tile)
        args.append(res)
    if after is not None:
        in_specs.append(pl.BlockSpec(memory_space=pl.ANY))
        args.append(after)
    if then is None:
        out_specs, out_shape = tile, big
    elif then[0] == "rms_bwd":
        in_specs += [tile, vec, tile]
        out_specs, out_shape = [tile, vec], [big, small]
    else:
        in_specs += [vec, tile]
        out_specs = [_const_spec((1, 128)), tile, tile, vec]
        out_shape = [jax.ShapeDtypeStruct((1, 128), F32), big, jax.ShapeDtypeStruct((SEQ, D_MODEL), BF), small]
    if then is not None:
        args += list(then[1:])
    return pl.pallas_call(
        body, name=name, grid=(SEQ // TOK_MM_TILE,), in_specs=in_specs, out_specs=out_specs, out_shape=out_shape,
        compiler_params=_params("parallel" if then is None else "arbitrary"),
    )(*[_in_hbm(t) for t in args])


def _rms_fwd(x, g, name):
    def body(x_ref, g_ref, o_ref):
        xv = x_ref[...]
        r = lax.rsqrt(jnp.mean(xv * xv, axis=-1, keepdims=True) + EPS)
        o_ref[...] = (xv * r * g_ref[...]).astype(BF)

    tile = pl.BlockSpec((TOK_TILE, D_MODEL), lambda i: (i, 0))
    return pl.pallas_call(
        body, name=name, grid=(SEQ // TOK_TILE,), in_specs=[tile, _const_spec((1, D_MODEL))], out_specs=tile,
        out_shape=jax.ShapeDtypeStruct((SEQ, D_MODEL), BF), compiler_params=_params("parallel"),
    )(*map(_in_hbm, (x, g)))


def _rms_bwd_tile(dyv, xv, gv, dresv):
    r = lax.rsqrt(jnp.mean(xv * xv, axis=-1, keepdims=True) + EPS)
    xn = xv * r
    dxn = dyv * gv
    return dresv + r * (dxn - xn * jnp.mean(dxn * xn, axis=-1, keepdims=True)), jnp.sum(dyv * xn, axis=0, keepdims=True)


def _loss_tile(xv, gv, tv):
    r = lax.rsqrt(jnp.mean(xv * xv, axis=-1, keepdims=True) + EPS)
    xn = xv * r
    err = xn * gv - tv
    lpart = jnp.full((1, 128), 0.5 * jnp.sum(jnp.mean(err * err, axis=-1, keepdims=True)), F32)
    dyv = err * (1.0 / D_MODEL)
    dxn = dyv * gv
    return lpart, r * (dxn - xn * jnp.mean(dxn * xn, axis=-1, keepdims=True)), jnp.sum(dyv * xn, axis=0, keepdims=True)


def _pool_counts(w):
    pos = lax.broadcasted_iota(jnp.int32, (SEQ, 1), 0).astype(F32)
    return jnp.minimum(pos + 1.0, float(w))


def _pool_window(u, w, ext):
    ext[pl.ds(POOL_HALO, SEQ), :] = u
    win = u
    for j in range(1, w):
        win = win + ext[pl.ds(POOL_HALO - j, SEQ), :]
    return win / _pool_counts(w) - u


def _pool_fwd(zcat, w_grp, scale):
    def body(z_ref, w_ref, s_ref, o_ref, ext):
        ext[pl.ds(0, POOL_HALO), :] = jnp.zeros((POOL_HALO, POOL_GD), F32)
        for g, w in enumerate(POOL_WINDOWS):
            cols = slice(g * POOL_GD, (g + 1) * POOL_GD)
            p = _pool_window(z_ref[:, cols], w, ext)
            o_ref[:, cols] = (_dot(p, w_ref[g]) * s_ref[:, cols]).astype(BF)

    return pl.pallas_call(
        body, name="pool_fwd", grid=(1,),
        in_specs=[pl.BlockSpec((SEQ, POOL_WIDTH), lambda i: (0, C_POOL // POOL_WIDTH)),
                  _const_spec((4, POOL_GD, POOL_GD)), _const_spec((1, POOL_WIDTH))],
        out_specs=_const_spec((SEQ, POOL_WIDTH)), out_shape=jax.ShapeDtypeStruct((SEQ, POOL_WIDTH), BF),
        scratch_shapes=[pltpu.VMEM((POOL_HALO + SEQ, POOL_GD), F32)], compiler_params=_params("arbitrary"),
    )(*map(_in_hbm, (zcat, w_grp, scale)))


def _pool_bwd(dzcat, zcat, dps, w_grp, scale):
    def body(dz_in, z_ref, dps_ref, w_ref, s_ref, dz_ref, dw_ref, dsc_ref, ext, ext2):
        del dz_in
        ext[pl.ds(0, POOL_HALO), :] = jnp.zeros((POOL_HALO, POOL_GD), F32)
        ext2[pl.ds(SEQ, POOL_HALO), :] = jnp.zeros((POOL_HALO, POOL_GD), F32)
        for g, w in enumerate(POOL_WINDOWS):
            cols = slice(g * POOL_GD, (g + 1) * POOL_GD)
            p = _pool_window(z_ref[:, cols], w, ext)
            wg = w_ref[g]
            pg = _dot(p, wg)
            dpsv = dps_ref[:, cols]
            dsc_ref[:, cols] = jnp.sum(dpsv * pg, axis=0, keepdims=True)
            dpg = dpsv * s_ref[:, cols]
            dw_ref[g] = _dot(p, dpg, ta=True)
            dp = _dot(dpg, wg, tb=True)
            dpc = dp / _pool_counts(w)
            ext2[pl.ds(0, SEQ), :] = dpc
            du = dpc
            for j in range(1, w):
                du = du + ext2[pl.ds(j, SEQ), :]
            dz_ref[:, cols] = (du - dp).astype(BF)

    return pl.pallas_call(
        body, name="pool_bwd", grid=(1,),
        in_specs=[pl.BlockSpec(memory_space=pl.ANY),
                  pl.BlockSpec((SEQ, POOL_WIDTH), lambda i: (0, C_POOL // POOL_WIDTH)),
                  _const_spec((SEQ, POOL_WIDTH)), _const_spec((4, POOL_GD, POOL_GD)), _const_spec((1, POOL_WIDTH))],
        out_specs=[pl.BlockSpec((SEQ, POOL_WIDTH), lambda i: (0, C_POOL // POOL_WIDTH)),
                   _const_spec((4, POOL_GD, POOL_GD)), _const_spec((1, POOL_WIDTH))],
        out_shape=[jax.ShapeDtypeStruct((SEQ, N_CAT), BF), jax.ShapeDtypeStruct((4, POOL_GD, POOL_GD), F32),
                   jax.ShapeDtypeStruct((1, POOL_WIDTH), F32)],
        scratch_shapes=[pltpu.VMEM((POOL_HALO + SEQ, POOL_GD), F32), pltpu.VMEM((SEQ + POOL_HALO, POOL_GD), F32)],
        input_output_aliases={0: 0}, compiler_params=_params("arbitrary"),
    )(*map(_in_hbm, (dzcat, zcat, dps, w_grp, scale)))


GK_TILE = 512


def _gk_fwd(h, wt_gk, wgk_pad, b_gk):
    def body(h_ref, wt_ref, w_ref, b_ref, la_ref):
        z_gk = _dot(h_ref[...], wt_ref[...], tb=True)
        pre = _dot(z_gk, w_ref[...]) + b_ref[...]
        la_ref[...] = (jnp.minimum(pre, 0.0) - jnp.log(1.0 + jnp.exp(-jnp.abs(pre)))) * (1.0 / GATE_NORM)

    return pl.pallas_call(
        body, name="gk_fwd", grid=(SEQ // GK_TILE,),
        in_specs=[pl.BlockSpec((GK_TILE, D_MODEL), lambda i: (i, 0)), _const_spec((GK_PAD, D_MODEL)),
                  _const_spec((GK_PAD, GLA_DK)), _const_spec((1, GLA_DK))],
        out_specs=pl.BlockSpec((GK_TILE, GLA_DK), lambda i: (i, 0)),
        out_shape=jax.ShapeDtypeStruct((SEQ, GLA_DK), F32), compiler_params=_params("parallel"),
    )(*map(_in_hbm, (h, wt_gk, wgk_pad, b_gk)))


def _gk_bwd(dla, h, wt_gk, wgk_pad, b_gk):
    def body(dla_ref, h_ref, wt_ref, w_ref, b_ref, dh_ref, dwt_ref, dw_ref, db_ref):
        hv = h_ref[...]
        wtv = wt_ref[...]
        wv = w_ref[...]
        z_gk = _dot(hv, wtv, tb=True)
        pre = _dot(z_gk, wv) + b_ref[...]
        dpre = dla_ref[...] * (1.0 / GATE_NORM) * (1.0 - _sigmoid(pre))
        dz_gk = _dot(dpre, wv, tb=True)
        dh_ref[...] = _dot(dz_gk, wtv)
        dwtp = _dot(dz_gk, hv, ta=True)
        dwp = _dot(z_gk, dpre, ta=True)[:GATE_RANK]
        dbp = jnp.sum(dpre, axis=0, keepdims=True)

        @pl.when(pl.program_id(0) == 0)
        def _():
            dwt_ref[...] = dwtp
            dw_ref[...] = dwp
            db_ref[...] = dbp

        @pl.when(pl.program_id(0) > 0)
        def _():
            dwt_ref[...] += dwtp
            dw_ref[...] += dwp
            db_ref[...] += dbp

    tile = pl.BlockSpec((GK_TILE, D_MODEL), lambda i: (i, 0))
    return pl.pallas_call(
        body, name="gk_bwd", grid=(SEQ // GK_TILE,),
        in_specs=[pl.BlockSpec((GK_TILE, GLA_DK), lambda i: (i, 0)), tile, _const_spec((GK_PAD, D_MODEL)),
                  _const_spec((GK_PAD, GLA_DK)), _const_spec((1, GLA_DK))],
        out_specs=[tile, _const_spec((GK_PAD, D_MODEL)), _const_spec((GATE_RANK, GLA_DK)), _const_spec((1, GLA_DK))],
        out_shape=[jax.ShapeDtypeStruct((SEQ, D_MODEL), F32), jax.ShapeDtypeStruct((GK_PAD, D_MODEL), F32),
                   jax.ShapeDtypeStruct((GATE_RANK, GLA_DK), F32), jax.ShapeDtypeStruct((1, GLA_DK), F32)],
        compiler_params=_params("arbitrary"),
    )(*map(_in_hbm, (dla, h, wt_gk, wgk_pad, b_gk)))


GLA_ROWS = GLA_CPS * CHUNK
GLA_STEPS = SEQ // GLA_ROWS
QKV_W = 2048


def _tri():
    return lax.broadcasted_iota(jnp.int32, (CHUNK, CHUNK), 0) >= lax.broadcasted_iota(jnp.int32, (CHUNK, CHUNK), 1)


def _chunk_cumsum(la_ref, rows):
    return _dot_exact(_tri().astype(F32), la_ref[rows, :])


def _gla_chunk(qkv_ref, la_ref, rows, h, bc_all):
    tri = _tri()
    q = qkv_ref[rows, h * HK:(h + 1) * HK] * (HK ** -0.5)
    k = qkv_ref[rows, GLA_DK + h * HK:GLA_DK + (h + 1) * HK]
    v = qkv_ref[rows, 2 * GLA_DK + h * HV:2 * GLA_DK + (h + 1) * HV].astype(BF)
    la = la_ref[rows, h * HK:(h + 1) * HK]
    bc = bc_all[:, h * HK:(h + 1) * HK]
    e_pos, e_neg = jnp.exp(bc), jnp.exp(-bc)
    dl = jnp.exp(jnp.sum(la, axis=0, keepdims=True))
    q_fw, q_bw, k_fw, k_bw = q * e_pos, q * e_neg, k * e_neg, k * e_pos
    scores = jnp.where(tri, _dot(q_fw, k_fw, tb=True), _dot(q_bw, k_bw, tb=True))
    return tri, v, e_pos, e_neg, dl, q_fw, q_bw, k_fw, k_bw, scores


def _gla_fwd(zcat, la, after):
    def body(qkv_ref, la_ref, after_ref, o_ref, st_ref, state):
        del after_ref

        @pl.when(pl.program_id(0) == 0)
        def _():
            state[...] = jnp.zeros_like(state)

        for c in range(GLA_CPS):
            rows = slice(c * CHUNK, (c + 1) * CHUNK)
            bc_all = _chunk_cumsum(la_ref, rows)
            for h in range(HEADS):
                _, v, _, _, dl, q_fw, _, k_fw, _, scores = _gla_chunk(qkv_ref, la_ref, rows, h, bc_all)
                st = state[h]
                st_ref[c, h] = st
                o_ref[rows, h * HV:(h + 1) * HV] = _dot(scores, v) + _dot(q_fw, st, tb=True)
                state[h] = st * dl + _dot(v, k_fw * dl, ta=True)

    return pl.pallas_call(
        body, name="gla_fwd", grid=(GLA_STEPS,),
        in_specs=[pl.BlockSpec((GLA_ROWS, QKV_W), lambda i: (i, 0)), pl.BlockSpec((GLA_ROWS, GLA_DK), lambda i: (i, 0)),
                  pl.BlockSpec(memory_space=pl.ANY)],
        out_specs=[pl.BlockSpec((GLA_ROWS, D_MODEL), lambda i: (i, 0)),
                   pl.BlockSpec((GLA_CPS, HEADS, HV, HK), lambda i: (i, 0, 0, 0))],
        out_shape=[jax.ShapeDtypeStruct((SEQ, D_MODEL), F32),
                   jax.ShapeDtypeStruct((SEQ // CHUNK, HEADS, HV, HK), F32)],
        scratch_shapes=[pltpu.VMEM((HEADS, HV, HK), F32)], compiler_params=_params("arbitrary"),
    )(*map(_in_hbm, (zcat, la)), after)


def _gla_bwd(dzcat, zcat, la, d_o, states):
    def body(dz_in, qkv_ref, la_ref, do_ref, st_ref, dqkv_ref, dla_ref, dstate):
        del dz_in

        @pl.when(pl.program_id(0) == 0)
        def _():
            dstate[...] = jnp.zeros_like(dstate)

        last_row = lax.broadcasted_iota(jnp.int32, (CHUNK, HK), 0) == CHUNK - 1
        upper = (lax.broadcasted_iota(jnp.int32, (CHUNK, CHUNK), 0)
                 <= lax.broadcasted_iota(jnp.int32, (CHUNK, CHUNK), 1)).astype(F32)
        for c in reversed(range(GLA_CPS)):
            rows = slice(c * CHUNK, (c + 1) * CHUNK)
            bc_all = _chunk_cumsum(la_ref, rows)
            dbs = []
            for h in range(HEADS):
                tri, v, e_pos, e_neg, dl, q_fw, q_bw, k_fw, k_bw, scores = _gla_chunk(qkv_ref, la_ref, rows, h, bc_all)
                st = st_ref[c, h]
                dst = dstate[h]
                d_out = do_ref[rows, h * HV:(h + 1) * HV].astype(BF)
                k_dec = k_fw * dl
                dp = _dot(d_out, v, tb=True)
                dp_fw = jnp.where(tri, dp, 0.0)
                dp_bw = jnp.where(tri, 0.0, dp)
                dv = _dot(scores, d_out, ta=True) + _dot(k_dec, dst, tb=True)
                dk_dec = _dot(v, dst)
                dq_fw = _dot(dp_fw, k_fw) + _dot(d_out, st)
                dk_fw = _dot(dp_fw, q_fw, ta=True) + dk_dec * dl
                dq_bw = _dot(dp_bw, k_bw)
                dk_bw = _dot(dp_bw, q_bw, ta=True)
                ddl = jnp.sum(st * dst, axis=0, keepdims=True) + jnp.sum(k_fw * dk_dec, axis=0, keepdims=True)
                dstate[h] = dst * dl + _dot(d_out, q_fw, ta=True)
                dq = (dq_fw * e_pos + dq_bw * e_neg) * (HK ** -0.5)
                dk = dk_fw * e_neg + dk_bw * e_pos
                dbs.append(dq_fw * q_fw - dk_fw * k_fw - dq_bw * q_bw + dk_bw * k_bw + jnp.where(last_row, ddl * dl, 0.0))
                dqkv_ref[rows, h * HK:(h + 1) * HK] = dq.astype(BF)
                dqkv_ref[rows, GLA_DK + h * HK:GLA_DK + (h + 1) * HK] = dk.astype(BF)
                dqkv_ref[rows, 2 * GLA_DK + h * HV:2 * GLA_DK + (h + 1) * HV] = dv.astype(BF)
            dla_ref[rows, :] = _dot_exact(upper, jnp.concatenate(dbs, axis=1))

    rev = lambda i: (GLA_STEPS - 1 - i, 0)
    return pl.pallas_call(
        body, name="gla_bwd", grid=(GLA_STEPS,),
        in_specs=[pl.BlockSpec(memory_space=pl.ANY), pl.BlockSpec((GLA_ROWS, QKV_W), rev),
                  pl.BlockSpec((GLA_ROWS, GLA_DK), rev), pl.BlockSpec((GLA_ROWS, D_MODEL), rev),
                  pl.BlockSpec((GLA_CPS, HEADS, HV, HK), lambda i: (GLA_STEPS - 1 - i, 0, 0, 0))],
        out_specs=[pl.BlockSpec((GLA_ROWS, QKV_W), rev), pl.BlockSpec((GLA_ROWS, GLA_DK), rev)],
        out_shape=[jax.ShapeDtypeStruct((SEQ, N_CAT), BF), jax.ShapeDtypeStruct((SEQ, GLA_DK), F32)],
        scratch_shapes=[pltpu.VMEM((HEADS, HV, HK), F32)], input_output_aliases={0: 0},
        compiler_params=_params("arbitrary"),
    )(*map(_in_hbm, (dzcat, zcat, la, d_o, states)))


def _silu_parts(x):
    s = _sigmoid(x)
    return x * s, s * (1.0 + x * (1.0 - s))


def _post_gla_fwd(o, zcat, g_head):
    def body(o_ref, zog_ref, g_ref, out_ref):
        for h in range(HEADS):
            cols = slice(h * HV, (h + 1) * HV)
            ov = o_ref[:, cols]
            r = lax.rsqrt(jnp.mean(ov * ov, axis=-1, keepdims=True) + EPS)
            act, _ = _silu_parts(zog_ref[:, cols])
            out_ref[:, cols] = (ov * r * g_ref[...] * act).astype(BF)

    tile = pl.BlockSpec((TOK_TILE, D_MODEL), lambda i: (i, 0))
    return pl.pallas_call(
        body, name="post_gla_fwd", grid=(SEQ // TOK_TILE,),
        in_specs=[tile, pl.BlockSpec((TOK_TILE, D_MODEL), lambda i: (i, C_OG // D_MODEL)), _const_spec((1, HV))],
        out_specs=tile, out_shape=jax.ShapeDtypeStruct((SEQ, D_MODEL), BF), compiler_params=_params("parallel"),
    )(*map(_in_hbm, (o, zcat, g_head)))


def _post_gla_bwd(dzcat, dy_gla, w_gla_proj, o, zcat, g_head):
    def body(dz_in, dyg_ref, w_ref, o_ref, zog_ref, g_ref, dz_ref, do_ref, dg_ref):
        del dz_in
        dog = _dot(dyg_ref[...], w_ref[...], tb=True)
        gpart = jnp.zeros((1, HV), F32)
        gv = g_ref[...]
        for h in range(HEADS):
            cols = slice(h * HV, (h + 1) * HV)
            ov = o_ref[:, cols]
            r = lax.rsqrt(jnp.mean(ov * ov, axis=-1, keepdims=True) + EPS)
            on = ov * r
            act, dact = _silu_parts(zog_ref[:, cols])
            dogv = dog[:, cols]
            dz_ref[:, cols] = (dogv * on * gv * dact).astype(BF)
            d_on_g = dogv * act
            gpart = gpart + jnp.sum(d_on_g * on, axis=0, keepdims=True)
            dxn = d_on_g * gv
            do_ref[:, cols] = r * (dxn - on * jnp.mean(dxn * on, axis=-1, keepdims=True))

        @pl.when(pl.program_id(0) == 0)
        def _():
            dg_ref[...] = gpart

        @pl.when(pl.program_id(0) > 0)
        def _():
            dg_ref[...] += gpart

    tile = pl.BlockSpec((TOK_TILE, D_MODEL), lambda i: (i, 0))
    ogspec = pl.BlockSpec((TOK_TILE, D_MODEL), lambda i: (i, C_OG // D_MODEL))
    return pl.pallas_call(
        body, name="post_gla_bwd", grid=(SEQ // TOK_TILE,),
        in_specs=[pl.BlockSpec(memory_space=pl.ANY), tile, _const_spec((D_MODEL, D_MODEL)), tile, ogspec,
                  _const_spec((1, HV))],
        out_specs=[ogspec, tile, _const_spec((1, HV))],
        out_shape=[jax.ShapeDtypeStruct((SEQ, N_CAT), BF), jax.ShapeDtypeStruct((SEQ, D_MODEL), F32),
                   jax.ShapeDtypeStruct((1, HV), F32)],
        input_output_aliases={0: 0}, compiler_params=_params("arbitrary"),
    )(*map(_in_hbm, (dzcat, dy_gla, w_gla_proj, o, zcat, g_head)))


GATE_W = 2 * D_MODEL


def _mix_out_fwd(ps, og, zcat, x, w_pool_proj, w_gla_proj, w_out, b_gate, g_ffn, after):
    def body(ps_ref, og_ref, zg_ref, x_ref, wpp_ref, wgp_ref, wout_ref, b_ref, g_ref, after_ref,
             yp_ref, yg_ref, mixed_ref, x1_ref, h2_ref):
        del after_ref
        y_pool = _dot(ps_ref[...], wpp_ref[...])
        y_gla = _dot(og_ref[...], wgp_ref[...])
        yp_ref[...] = y_pool
        yg_ref[...] = y_gla
        g0 = _sigmoid(zg_ref[:, :D_MODEL] + b_ref[:, :D_MODEL])
        g1 = _sigmoid(zg_ref[:, D_MODEL:] + b_ref[:, D_MODEL:])
        mixed = (g0 * y_pool + g1 * y_gla).astype(BF)
        mixed_ref[...] = mixed
        x1 = x_ref[...] + _dot(mixed, wout_ref[...])
        x1_ref[...] = x1
        r = lax.rsqrt(jnp.mean(x1 * x1, axis=-1, keepdims=True) + EPS)
        h2_ref[...] = (x1 * r * g_ref[...]).astype(BF)

    tile = pl.BlockSpec((TOK_TILE, D_MODEL), lambda i: (i, 0))
    resident = lambda shape: pl.BlockSpec(shape, lambda i: (0, 0), pipeline_mode=pl.Buffered(1))
    f32, bf16 = jax.ShapeDtypeStruct((SEQ, D_MODEL), F32), jax.ShapeDtypeStruct((SEQ, D_MODEL), BF)
    return pl.pallas_call(
        body, name="mix_out_fwd", grid=(SEQ // TOK_TILE,),
        in_specs=[pl.BlockSpec((TOK_TILE, POOL_WIDTH), lambda i: (i, 0)), tile,
                  pl.BlockSpec((TOK_TILE, GATE_W), lambda i: (i, C_GATE // GATE_W)), tile,
                  resident((POOL_WIDTH, D_MODEL)), resident((D_MODEL, D_MODEL)), resident((D_MODEL, D_MODEL)),
                  _const_spec((1, GATE_W)), _const_spec((1, D_MODEL)), pl.BlockSpec(memory_space=pl.ANY)],
        out_specs=[tile] * 5, out_shape=[f32, f32, bf16, f32, bf16], compiler_params=_params("parallel"),
    )(*map(_in_hbm, (ps, og, zcat, x, w_pool_proj, w_gla_proj, w_out, b_gate, g_ffn)), after)


def _mix_bwd(dx1, w_out, zcat, b_gate, y_pool, y_gla):
    def body(dx_ref, w_ref, zg_ref, b_ref, yp_ref, yg_ref, dz_ref, dyp_ref, dyg_ref, db_ref):
        dm = _dot(dx_ref[...], w_ref[...], tb=True)
        g0 = _sigmoid(zg_ref[:, :D_MODEL] + b_ref[:, :D_MODEL])
        g1 = _sigmoid(zg_ref[:, D_MODEL:] + b_ref[:, D_MODEL:])
        dyp_ref[...] = (dm * g0).astype(BF)
        dyg_ref[...] = (dm * g1).astype(BF)
        dz0 = dm * yp_ref[...] * g0 * (1.0 - g0)
        dz1 = dm * yg_ref[...] * g1 * (1.0 - g1)
        dz_ref[:, :D_MODEL] = dz0.astype(BF)
        dz_ref[:, D_MODEL:] = dz1.astype(BF)
        b0 = jnp.sum(dz0, axis=0, keepdims=True)
        b1 = jnp.sum(dz1, axis=0, keepdims=True)

        @pl.when(pl.program_id(0) == 0)
        def _():
            db_ref[:, :D_MODEL] = b0
            db_ref[:, D_MODEL:] = b1

        @pl.when(pl.program_id(0) > 0)
        def _():
            db_ref[:, :D_MODEL] += b0
            db_ref[:, D_MODEL:] += b1

    tile = pl.BlockSpec((TOK_TILE, D_MODEL), lambda i: (i, 0))
    gspec = pl.BlockSpec((TOK_TILE, GATE_W), lambda i: (i, C_GATE // GATE_W))
    return pl.pallas_call(
        body, name="mix_bwd", grid=(SEQ // TOK_TILE,),
        in_specs=[tile, _const_spec((D_MODEL, D_MODEL)), gspec, _const_spec((1, GATE_W)), tile, tile],
        out_specs=[gspec, tile, tile, _const_spec((1, GATE_W))],
        out_shape=[jax.ShapeDtypeStruct((SEQ, N_CAT), BF), jax.ShapeDtypeStruct((SEQ, D_MODEL), BF),
                   jax.ShapeDtypeStruct((SEQ, D_MODEL), BF), jax.ShapeDtypeStruct((1, GATE_W), F32)],
        compiler_params=_params("arbitrary"),
    )(*map(_in_hbm, (dx1, w_out, zcat, b_gate, y_pool, y_gla)))


N_TOK_TILES = SEQ // TOK_TILE


LANE_TILES = tuple((lo, min(128, FF_BLK - lo)) for lo in range(0, FF_BLK, 128))


def _taps(w_ref, b_ref, half, lanes, rows):
    shape = (rows, lanes.stop - lanes.start)
    return ([jnp.broadcast_to(w_ref[half, j:j + 1, lanes], shape) for j in range(3)],
            jnp.broadcast_to(b_ref[half, :, lanes], shape))


def _conv_strips(u_ref, ub_ref, ua_ref, taps, lanes, width, n_strips, first):
    row = lax.broadcasted_iota(jnp.int32, (HALO, width), 0)
    prev = [[pltpu.roll(jnp.where(first, 0.0, ub_ref[half, :, lanes]), k, 0) for k in (1, 2)] for half in range(2)]
    for s in range(n_strips + (ua_ref is not None)):
        u3, conv = [], []
        for half in range(2):
            cur = u_ref[half, s * HALO:(s + 1) * HALO, lanes] if s < n_strips else ua_ref[half, :, lanes]
            rolled = [pltpu.roll(cur, k, 0) for k in (1, 2)]
            frames = [jnp.where(row >= 2, rolled[1], prev[half][1]), jnp.where(row >= 1, rolled[0], prev[half][0]), cur]
            prev[half] = rolled
            w3, bias = taps[half]
            u3.append(frames)
            conv.append(bias + frames[0] * w3[0] + frames[1] * w3[1] + frames[2] * w3[2])
        yield s, u3, conv


N_STRIPS = TOK_TILE // HALO


def _up_conv_fwd(h2, wt_up, w_conv, b_conv):
    steps = N_TOK_TILES // 2

    def body(h_ref, h_next, wg_ref, wv_ref, w_ref, b_ref, u_ref, a_ref, buf_a, buf_b, carry):
        j = pl.program_id(1)

        def project(hv, buf):
            buf[0] = _dot(hv, wg_ref[...], tb=True)
            buf[1] = _dot(hv, wv_ref[...], tb=True)

        def conv(buf, row0):
            u_ref[:, row0:row0 + TOK_TILE, :] = buf[...]
            for lo, width in LANE_TILES:
                lanes = slice(lo, lo + width)
                taps = [_taps(w_ref, b_ref, half, lanes, HALO) for half in range(2)]
                pending = None
                for s, _, (cg, cv) in _conv_strips(buf, carry, None, taps, lanes, width, N_STRIPS, False):
                    act = cg * _sigmoid(cg) * cv
                    if s % 2 == 0:
                        pending = act
                    else:
                        a_ref[0, row0 + (s - 1) * HALO:row0 + (s + 1) * HALO, lanes] = (
                            jnp.concatenate([pending, act], axis=0).astype(BF))
            carry[...] = buf[:, TOK_TILE - HALO:, :]

        @pl.when(j == 0)
        def _():
            project(h_ref[0:TOK_TILE, :], buf_a)
            carry[...] = jnp.zeros_like(carry)

        project(h_ref[TOK_TILE:, :], buf_b)
        conv(buf_a, 0)
        project(h_next[...], buf_a)
        conv(buf_b, TOK_TILE)

    w_blk = lambda half: pl.BlockSpec((FF_BLK, D_MODEL), lambda b, j: (b + 4 * half, 0))
    vec = lambda rows: pl.BlockSpec((2, None, rows, FF_BLK), lambda b, j: (0, b, 0, 0))
    u_buf = pltpu.VMEM((2, TOK_TILE, FF_BLK), F32)
    return pl.pallas_call(
        body, name="up_conv_fwd", grid=(4, steps),
        in_specs=[pl.BlockSpec((2 * TOK_TILE, D_MODEL), lambda b, j: (j, 0)),
                  pl.BlockSpec((TOK_TILE, D_MODEL), lambda b, j: (jnp.minimum(2 * j + 2, N_TOK_TILES - 1), 0)),
                  w_blk(0), w_blk(1), vec(3), vec(1)],
        out_specs=[pl.BlockSpec((2, None, 2 * TOK_TILE, FF_BLK), lambda b, j: (0, b, j, 0)),
                   pl.BlockSpec((1, None, 2 * TOK_TILE, FF_BLK), lambda b, j: (0, b, j, 0))],
        out_shape=[jax.ShapeDtypeStruct((2, 4, SEQ, FF_BLK), F32), jax.ShapeDtypeStruct((1, 4, SEQ, FF_BLK), BF)],
        scratch_shapes=[u_buf, u_buf, pltpu.VMEM((2, HALO, FF_BLK), F32)],
        compiler_params=_params("parallel", "arbitrary"),
    )(*map(_in_hbm, (h2, h2, wt_up, wt_up, w_conv, b_conv)))


DA_PAD = TOK_TILE + 2 * HALO


def _down_conv_bwd(u, dx2_pad, w_down, w_conv, b_conv):
    steps = N_TOK_TILES // 2

    def body(u_ref, ub_ref, ua_ref, dx_ref, wd_ref, w_ref, b_ref, du_ref, dw_ref, db_ref, da_a, da_b, carry):
        j = pl.program_id(1)

        def project(tile, buf):
            start = pl.multiple_of(tile * TOK_TILE + HALO, HALO)
            buf[...] = _dot(dx_ref[pl.ds(start, TOK_TILE), :], wd_ref[...], tb=True)

        def backward(row0, before, after, buf, first):
            tile = u_ref.at[:, pl.ds(row0, TOK_TILE), :]
            for lo, width in LANE_TILES:
                lanes = slice(lo, lo + width)
                row = lax.broadcasted_iota(jnp.int32, (HALO, width), 0)
                taps = [_taps(w_ref, b_ref, half, lanes, HALO) for half in range(2)]
                acc_w = [[jnp.zeros((HALO, width), F32) for _ in range(3)] for _ in range(2)]
                acc_b = [jnp.zeros((HALO, width), F32) for _ in range(2)]
                pending = [None, None]
                dc_prev, up_prev = [None, None], [None, None]
                for s, u3, (cg, cv) in _conv_strips(tile, before, after, taps, lanes, width, N_STRIPS, first):
                    act, dact = _silu_parts(cg)
                    da = carry[:, lanes] if s == 0 else buf[pl.ds((s - 1) * HALO, HALO), lanes]
                    dc = (da * cv * dact, da * act)
                    for half in range(2):
                        up = [pltpu.roll(dc[half], HALO - k, 0) for k in (1, 2)]
                        if s < N_STRIPS:
                            for t in range(3):
                                acc_w[half][t] = acc_w[half][t] + dc[half] * u3[half][t]
                            acc_b[half] = acc_b[half] + dc[half]
                        if s >= 1:
                            w3 = taps[half][0]
                            du = (dc_prev[half] * w3[2] + jnp.where(row < HALO - 1, up_prev[half][0], up[0]) * w3[1]
                                  + jnp.where(row < HALO - 2, up_prev[half][1], up[1]) * w3[0])
                            if (s - 1) % 2 == 0:
                                pending[half] = du
                            else:
                                du_ref[half, row0 + (s - 2) * HALO:row0 + s * HALO, lanes] = (
                                    jnp.concatenate([pending[half], du], axis=0).astype(BF))
                        dc_prev[half], up_prev[half] = dc[half], up
                for half in range(2):
                    for t in range(3):
                        dw_ref[half, t:t + 1, lanes] += jnp.sum(acc_w[half][t], axis=0, keepdims=True)
                    db_ref[half, :, lanes] += jnp.sum(acc_b[half], axis=0, keepdims=True)
            carry[...] = buf[TOK_TILE - HALO:, :]

        @pl.when(j == 0)
        def _():
            dw_ref[...] = jnp.zeros_like(dw_ref)
            db_ref[...] = jnp.zeros_like(db_ref)
            project(0, da_a)
            carry[...] = _dot(dx_ref[0:HALO, :], wd_ref[...], tb=True)

        project(2 * j + 1, da_b)
        backward(0, ub_ref, u_ref.at[:, pl.ds(TOK_TILE, HALO), :], da_a, j == 0)
        project(2 * j + 2, da_a)
        backward(TOK_TILE, u_ref.at[:, pl.ds(TOK_TILE - HALO, HALO), :], ua_ref, da_b, False)

    halo_blocks = 2 * TOK_TILE // HALO
    tile = pl.BlockSpec((2, None, 2 * TOK_TILE, FF_BLK), lambda b, j: (0, b, j, 0))
    before = pl.BlockSpec((2, None, HALO, FF_BLK), lambda b, j: (0, b, jnp.maximum(j * halo_blocks - 1, 0), 0))
    after = pl.BlockSpec((2, None, HALO, FF_BLK),
                         lambda b, j: (0, b, jnp.minimum((j + 1) * halo_blocks, SEQ // HALO - 1), 0))
    vec = lambda rows: pl.BlockSpec((2, None, rows, FF_BLK), lambda b, j: (0, b, 0, 0))
    da_buf = pltpu.VMEM((TOK_TILE, FF_BLK), F32)
    return pl.pallas_call(
        body, name="down_conv_bwd", grid=(4, steps),
        in_specs=[tile, before, after,
                  pl.BlockSpec(dx2_pad.shape, lambda b, j: (0, 0), pipeline_mode=pl.Buffered(1)),
                  pl.BlockSpec((FF_BLK, D_MODEL), lambda b, j: (b, 0)), vec(3), vec(1)],
        out_specs=[tile, vec(3), vec(1)],
        out_shape=[jax.ShapeDtypeStruct((2, 4, SEQ, FF_BLK), BF), jax.ShapeDtypeStruct((2, 4, 3, FF_BLK), F32),
                   jax.ShapeDtypeStruct((2, 4, 1, FF_BLK), F32)],
        scratch_shapes=[da_buf, da_buf, pltpu.VMEM((HALO, FF_BLK), F32)],
        compiler_params=_params("parallel", "arbitrary"),
    )(*map(_in_hbm, (u, u, u, dx2_pad, w_down, w_conv, b_conv)))


W_IN_SEGMENTS = ((R_POOL, POOL_WIDTH, "cat", C_POOL), (R_QKV, QKV_W, "cat", C_QKV), (R_OG, D_MODEL, "cat", C_OG),
                 (R_GK, GATE_RANK, "gk", 0), (R_GATE, GATE_W, "cat", C_GATE))


def _slab_pieces(d):
    lo, hi = d * IN_SHARD, (d + 1) * IN_SHARD
    pieces = []
    for start, n, dest, at in W_IN_SEGMENTS:
        a, b = max(lo, start), min(hi, start + n)
        if a < b:
            assert (a - lo) % 2 == 0 and (b - a) % 2 == 0 and (at + a - start) % 2 == 0
            pieces.append(((a - lo) // 2, (b - a) // 2, dest, (at + a - start) // 2))
    return pieces


def _unshard_w_in(slabs):
    def body(slab_ref, cat_ref, gk_ref):
        d = pl.program_id(0)
        src = slab_ref.bitcast(jnp.uint32)
        dst = dict(cat=cat_ref.bitcast(jnp.uint32), gk=gk_ref.bitcast(jnp.uint32))

        @pl.when(d == 0)
        def _():
            gk_ref[...] = jnp.zeros_like(gk_ref)

        for dd in range(N_DEV):
            @pl.when(d == dd)
            def _():
                for a, n, dest, at in _slab_pieces(dd):
                    dst[dest][pl.ds(at, n), :] = src[0, pl.ds(a, n), :]

    return pl.pallas_call(
        body, name="unshard_w_in", grid=(N_DEV,),
        in_specs=[pl.BlockSpec((1, IN_SHARD, D_MODEL), lambda d: (d, 0, 0))],
        out_specs=[_const_spec((N_CAT, D_MODEL)), _const_spec((GK_PAD, D_MODEL))],
        out_shape=[jax.ShapeDtypeStruct((N_CAT, D_MODEL), BF), jax.ShapeDtypeStruct((GK_PAD, D_MODEL), BF)],
        compiler_params=_params("arbitrary"),
    )(_in_hbm(slabs))


def _shard_d_w_in(d_cat, d_gk):
    def body(cat_ref, gk_ref, slab_ref):
        d = pl.program_id(0)
        cat = cat_ref.bitcast(jnp.uint32)
        gk = pltpu.bitcast(gk_ref[0:GATE_RANK, :].astype(BF), jnp.uint32)
        dst = slab_ref.bitcast(jnp.uint32)
        for dd in range(N_DEV):
            @pl.when(d == dd)
            def _():
                for a, n, source, at in _slab_pieces(dd):
                    dst[0, pl.ds(a, n), :] = gk[at:at + n] if source == "gk" else cat[pl.ds(at, n), :]

    return pl.pallas_call(
        body, name="shard_d_w_in", grid=(N_DEV,),
        in_specs=[_const_spec((N_CAT, D_MODEL)), _const_spec((GK_PAD, D_MODEL))],
        out_specs=pl.BlockSpec((1, IN_SHARD, D_MODEL), lambda d: (d, 0, 0)),
        out_shape=jax.ShapeDtypeStruct((N_DEV, IN_SHARD, D_MODEL), BF), compiler_params=_params("parallel"),
    )(_in_hbm(d_cat), _in_hbm(d_gk))


ANY = pl.BlockSpec(memory_space=pl.ANY)


def _place():
    x, y, c = lax.axis_index("x"), lax.axis_index("y"), lax.axis_index("c")
    other_chips = [(1 - x, y), (x, 1 - y), (1 - x, 1 - y)]
    return x, y, c, other_chips


SEM = pl.BlockSpec(memory_space=pltpu.SEMAPHORE)
IN_HBM = pl.BlockSpec(memory_space=pltpu.HBM)
SPLIT_PARAMS = pltpu.CompilerParams(has_side_effects=pltpu.SideEffectType.DATAFLOW_SIDE_EFFECTING)


def _gather_first(refs, send_sems, recv_sems):
    x, y, c, chips = _place()
    targets = [(x, y, 1 - c)] + [(px, py, c) for px, py in chips]
    return [pltpu.make_async_remote_copy(src_ref=refs[2 * a], dst_ref=refs[2 * a + 1].at[4 * x + 2 * y + c],
                                         send_sem=send_sems.at[4 * a + k], recv_sem=recv_sems.at[4 * a + k],
                                         device_id=to, device_id_type=MESH)
            for a in range(len(refs) // 2) for k, to in enumerate(targets)]


def _gather_direct(refs, send_sems, recv_sems):
    x, y, c, _ = _place()
    flips = [(dx, dy, dc) for dx in (0, 1) for dy in (0, 1) for dc in (0, 1) if dx + dy + dc]
    targets = [(1 - x if dx else x, 1 - y if dy else y, 1 - c if dc else c) for dx, dy, dc in flips]
    return [pltpu.make_async_remote_copy(src_ref=refs[2 * a], dst_ref=refs[2 * a + 1].at[4 * x + 2 * y + c],
                                         send_sem=send_sems.at[7 * a + k], recv_sem=recv_sems.at[7 * a + k],
                                         device_id=to, device_id_type=MESH)
            for a in range(len(refs) // 2) for k, to in enumerate(targets)]


def _gather_second(refs, send_sems, recv_sems):
    x, y, c, chips = _place()
    copies = []
    for a, land in enumerate(refs):
        for j, (px, py) in enumerate(chips):
            block = land.at[4 * px + 2 * py + c]
            copies.append(pltpu.make_async_remote_copy(src_ref=block, dst_ref=block, send_sem=send_sems.at[3 * a + j],
                                                       recv_sem=recv_sems.at[3 * a + j], device_id=(x, y, 1 - c),
                                                       device_id_type=MESH))
    return copies


def _reduce_first(refs, send_sems, recv_sems):
    x, y, c, _ = _place()
    return [pltpu.make_async_remote_copy(src_ref=refs[2 * a].at[j, 1 - c], dst_ref=refs[2 * a + 1].at[j],
                                         send_sem=send_sems.at[4 * a + j], recv_sem=recv_sems.at[4 * a + j],
                                         device_id=(x, y, 1 - c), device_id_type=MESH)
            for a in range(len(refs) // 2) for j in range(4)]


def _reduce_second(refs, send_sems, recv_sems):
    _, _, c, chips = _place()
    return [pltpu.make_async_remote_copy(src_ref=refs[2 * a].at[2 * px + py], dst_ref=refs[2 * a + 1].at[k],
                                         send_sem=send_sems.at[3 * a + k], recv_sem=recv_sems.at[3 * a + k],
                                         device_id=(px, py, c), device_id_type=MESH)
            for a in range(len(refs) // 2) for k, (px, py) in enumerate(chips)]


def _split_start(name, groups):
    arrays = [a for g in groups for a in g[0]]
    n = len(arrays)

    def body(*refs):
        sems = refs[n:n + 2 * len(groups)]
        at = 0
        for gi, (members, _, build) in enumerate(groups):
            for cp in build(refs[at:at + len(members)], sems[2 * gi], sems[2 * gi + 1]):
                cp.start()
            at += len(members)
        refs[-1][...] = jnp.zeros_like(refs[-1])

    sem_shapes = [pltpu.SemaphoreType.DMA((g[1],)) for g in groups for _ in range(2)]
    outs = pl.pallas_call(
        body, name=name, in_specs=[IN_HBM] * n,
        out_shape=(*sem_shapes, *[pltpu.HBM(a.shape, a.dtype) for a in arrays], jax.ShapeDtypeStruct((8, 128), F32)),
        out_specs=(*[SEM] * len(sem_shapes), *[IN_HBM] * n, pl.BlockSpec(memory_space=pltpu.VMEM)),
        input_output_aliases={i: len(sem_shapes) + i for i in range(n)}, compiler_params=SPLIT_PARAMS,
    )(*[pltpu.with_memory_space_constraint(a, pltpu.HBM) for a in arrays])
    per_group, at = [], len(sem_shapes)
    for gi, (members, _, _) in enumerate(groups):
        per_group.append((outs[2 * gi], outs[2 * gi + 1], list(outs[at:at + len(members)])))
        at += len(members)
    return per_group, outs[-1]


def _split_wait(name, started, build, after):
    send_sems, recv_sems, arrays = started
    n = len(arrays)
    after = after if isinstance(after, (tuple, list)) else (after,)

    def body(*refs):
        for cp in build(refs[:n], refs[n], refs[n + 1]):
            cp.wait_send()
            cp.wait_recv()

    return pl.pallas_call(
        body, name=name, in_specs=[IN_HBM] * n + [SEM, SEM] + [ANY] * len(after),
        out_shape=tuple(pltpu.HBM(a.shape, a.dtype) for a in arrays), out_specs=tuple([IN_HBM] * n),
        input_output_aliases={i: i for i in range(n)}, compiler_params=SPLIT_PARAMS,
    )(*arrays, send_sems, recv_sems, *after)


def _gather_landing(shard, me):
    return lax.dynamic_update_slice(lax.empty((N_DEV,) + shard.shape, shard.dtype), shard[None],
                                    (me,) + (0,) * shard.ndim)


def _tile_2d(rows, cols):
    for t in (256, 176, 128):
        if rows % t == 0:
            return t, cols
    return rows, 256


def _pair_sum(part, recv, core, name):
    _, rows, cols = recv.shape
    tr, tc = rows, cols

    def body(c_ref, p_ref, r_ref, o_ref):
        del c_ref
        o_ref[...] = (p_ref[...].astype(F32) + r_ref[...].astype(F32)).astype(BF)

    grid_spec = pltpu.PrefetchScalarGridSpec(
        num_scalar_prefetch=1, grid=(4, rows // tr, cols // tc),
        in_specs=[pl.BlockSpec((None, None, tr, tc), lambda j, i, k, c_ref: (j, c_ref[0], i, k)),
                  pl.BlockSpec((None, tr, tc), lambda j, i, k, c_ref: (j, i, k))],
        out_specs=pl.BlockSpec((None, tr, tc), lambda j, i, k, c_ref: (j, i, k)))
    return pl.pallas_call(
        body, name=name, grid_spec=grid_spec, out_shape=jax.ShapeDtypeStruct(recv.shape, BF),
        compiler_params=_params("parallel", "parallel", "parallel"),
    )(core, *map(_in_hbm, (part, recv)))


def _adamw(w, g, m, v):
    m = ADAM_B1 * m + (1.0 - ADAM_B1) * g
    v = ADAM_B2 * v + (1.0 - ADAM_B2) * (g * g)
    delta = -ADAM_LR * ((m / ADAM_C1) / (jnp.sqrt(v / ADAM_C2) + ADAM_EPS) + ADAM_WD * w)
    return delta, m, v


def _chip_sum_adamw(sums, recv, w, m, v, chip, name):
    rows, cols = w.shape
    tr, tc = _tile_2d(rows, cols)

    def body(chip_ref, s_ref, r_ref, w_ref, m_ref, v_ref, g_out, d_out, m_out, v_out):
        del chip_ref
        g = s_ref[...].astype(F32)
        for k in range(3):
            g = g + r_ref[k].astype(F32)
        g_out[...] = g
        d_out[...], m_out[...], v_out[...] = _adamw(w_ref[...], g, m_ref[...], v_ref[...])

    tile = pl.BlockSpec((tr, tc), lambda i, k, chip_ref: (i, k))
    grid_spec = pltpu.PrefetchScalarGridSpec(
        num_scalar_prefetch=1, grid=(rows // tr, cols // tc),
        in_specs=[pl.BlockSpec((None, tr, tc), lambda i, k, chip_ref: (chip_ref[0], i, k)),
                  pl.BlockSpec((3, tr, tc), lambda i, k, chip_ref: (0, i, k)), tile, tile, tile],
        out_specs=[tile] * 4)
    return pl.pallas_call(
        body, name=name, grid_spec=grid_spec, out_shape=[jax.ShapeDtypeStruct((rows, cols), F32)] * 4,
        compiler_params=_params("parallel", "parallel"),
    )(chip, *map(_in_hbm, (sums, recv, w, m, v)))


def _small_sum_adamw(me, entries, loss_parts):
    def whole(shape, squeeze=0, pick=False):
        blk = (None,) * squeeze + tuple(shape[squeeze:])
        if pick:
            blk = (shape[0], None) + tuple(shape[2:])
            return pl.BlockSpec(blk, lambda i, me_ref: (0, me_ref[0]) + (0,) * (len(shape) - 2))
        return pl.BlockSpec(blk, lambda i, me_ref: (0,) * len(shape))

    in_specs, out_specs, out_shape, args = [], [], [], []
    for parts, w, m, v, sharded in entries:
        lead = w.ndim - (parts.ndim - (2 if sharded else 1))
        in_specs += [whole(parts.shape, pick=sharded)] + [whole(w.shape, squeeze=lead)] * 3
        out_specs += [whole(w.shape, squeeze=lead)] * 4
        out_shape += [jax.ShapeDtypeStruct(w.shape, F32)] * 4
        args += [parts, w, m, v]
    in_specs.append(whole(loss_parts.shape))
    out_specs.append(whole(loss_parts.shape[1:]))
    out_shape.append(jax.ShapeDtypeStruct(loss_parts.shape[1:], F32))
    n = len(entries)

    def added(p_ref):
        total = p_ref[0]
        for d in range(1, N_DEV):
            total = total + p_ref[d]
        return total

    def body(me_ref, *refs):
        del me_ref
        ins, outs = refs[:4 * n + 1], refs[4 * n + 1:]
        for e in range(n):
            p_ref, w_ref, m_ref, v_ref = ins[4 * e:4 * e + 4]
            g_out, d_out, m_out, v_out = outs[4 * e:4 * e + 4]
            g = added(p_ref)
            g_out[...] = g
            d_out[...], m_out[...], v_out[...] = _adamw(w_ref[...], g, m_ref[...], v_ref[...])
        outs[4 * n][...] = added(ins[4 * n])

    grid_spec = pltpu.PrefetchScalarGridSpec(num_scalar_prefetch=1, grid=(1,), in_specs=in_specs, out_specs=out_specs)
    outs = pl.pallas_call(body, name="small_sum_adamw", grid_spec=grid_spec, out_shape=out_shape,
                          compiler_params=_params("arbitrary"))(me, *map(_in_hbm, args + [loss_parts]))
    return [outs[4 * e:4 * e + 4] for e in range(n)], outs[4 * n]


MM_TILE = 512
N_MM_TILES = SEQ // MM_TILE
CAT_TILE = 512
N_CAT_TILES = N_CAT // CAT_TILE


def kernel(x, g_mix, w_in, b_gate, w_gk_up, b_gk, w_pool_grp, pool_scale, g_gla_head, w_pool_proj, w_gla_proj, w_out, g_ffn, w_up, w_conv, b_conv, w_down, g_final, loss_target, m_g_mix, m_w_in, m_b_gate, m_w_gk_up, m_b_gk, m_w_pool_grp, m_pool_scale, m_g_gla_head, m_w_pool_proj, m_w_gla_proj, m_w_out, m_g_ffn, m_w_up, m_w_conv, m_b_conv, m_w_down, m_g_final, v_g_mix, v_w_in, v_b_gate, v_w_gk_up, v_b_gk, v_w_pool_grp, v_pool_scale, v_g_gla_head, v_w_pool_proj, v_w_gla_proj, v_w_out, v_g_ffn, v_w_up, v_w_conv, v_b_conv, v_w_down, v_g_final):
    xi, yi, ci = lax.axis_index("x"), lax.axis_index("y"), lax.axis_index("c")
    me = 4 * xi + 2 * yi + ci
    core = jnp.reshape(ci, (1,)).astype(jnp.int32)
    chip = jnp.reshape(2 * xi + yi, (1,)).astype(jnp.int32)
    xs, target = x[0], loss_target[0]

    big = dict(w_in=w_in[0].T, w_pool_proj=w_pool_proj[0], w_gla_proj=w_gla_proj[0], w_out=w_out[0], w_up=w_up[0].T,
               w_down=w_down[0])
    moments = dict(w_in=(m_w_in[0].T, v_w_in[0].T), w_pool_proj=(m_w_pool_proj[0], v_w_pool_proj[0]),
                   w_gla_proj=(m_w_gla_proj[0], v_w_gla_proj[0]), w_out=(m_w_out[0], v_w_out[0]),
                   w_up=(m_w_up[0].T, v_w_up[0].T), w_down=(m_w_down[0], v_w_down[0]))
    names = list(big)
    shards = {k: big[k].astype(BF) for k in names}
    shards["w_gk_up"], shards["w_conv"] = w_gk_up[0], w_conv[0]
    gather_groups = (("w_in", "w_gk_up"), ("w_pool_proj", "w_gla_proj", "w_out"), ("w_up", "w_down", "w_conv"))
    started, token = _split_start("gather_start", [
        ([t for k in g for t in (shards[k], _gather_landing(shards[k], me))], 4 * len(g), _gather_first)
        for g in gather_groups])

    def gather_pass(gi, after):
        lands = list(_split_wait(f"gather_wait_{gi}", started[gi], _gather_first, after)[1::2])
        passed, tkn = _split_start(f"gather_pass_{gi}", [(lands, 3 * len(lands), _gather_second)])
        return passed[0], tkn

    def gather_done(gi, passed, after):
        return dict(zip(gather_groups[gi], _split_wait(f"gather_pass_wait_{gi}", passed, _gather_second, after)))

    tok = lambda i, j, k: (i, 0)
    whole = lambda i, j, k: (0, 0)
    kblk = lambda i, j, k: (k, 0)
    ff_seq = (None, None, SEQ, FF_BLK)

    h = _rms_fwd(xs, g_mix + token[:1, :1], "rms_mix")
    wg = gather_done(0, gather_pass(0, h)[0], h)
    wt_cat, wt_gk = _unshard_w_in(wg["w_in"])
    wgk_pad = jnp.pad(wg["w_gk_up"].transpose(1, 0, 2).reshape(GATE_RANK, GLA_DK), ((0, GK_PAD - GATE_RANK), (0, 0)))
    zcat = _mm(h, wt_cat, out_shape=(SEQ, N_CAT), out_dtype=F32, grid=(N_CAT_TILES, 1, 1),
               blk_a=(SEQ, D_MODEL), blk_b=(CAT_TILE, D_MODEL), blk_o=(SEQ, CAT_TILE),
               map_a=whole, map_b=lambda j, i, k: (j, 0), map_o=lambda j, i, k: (0, j), tb=True, name="mm_in")
    la = _gk_fwd(h, wt_gk, wgk_pad, b_gk)
    passed, tkn = gather_pass(1, la)
    o, states = _gla_fwd(zcat, la, tkn)
    wg = gather_done(1, passed, o)
    wpp = wg["w_pool_proj"].transpose(1, 0, 2).reshape(POOL_WIDTH, D_MODEL)
    wgp = wg["w_gla_proj"].reshape(D_MODEL, D_MODEL)
    wout = wg["w_out"].reshape(D_MODEL, D_MODEL)
    og = _post_gla_fwd(o, zcat, g_gla_head)
    ps = _pool_fwd(zcat, w_pool_grp[0], pool_scale)
    passed, tkn = gather_pass(2, (og, ps))
    y_pool, y_gla, mixed, x1, h2 = _mix_out_fwd(ps, og, zcat, xs, wpp, wgp, wout, b_gate, g_ffn, tkn)
    wg = gather_done(2, passed, h2)
    wt_up = wg["w_up"].reshape(2 * D_FF, D_MODEL)
    wdown = wg["w_down"].reshape(D_FF, D_MODEL)
    wconv4 = wg["w_conv"].reshape(2, 4, 3, FF_BLK)
    bconv4 = b_conv.reshape(2, 4, 1, FF_BLK)
    blk4 = lambda b, i, k: (b // 4, b % 4, 0, 0)
    u4, act = _up_conv_fwd(h2, wt_up, wconv4, bconv4)
    loss_part, dx2, dx2_bf, dg_final = _mm_tokens(
        act, wdown, blk_a=(None, 4, TOK_MM_TILE, FF_BLK), map_a=lambda i: (0, 0, i, 0),
        pieces=[(b, b * FF_BLK, FF_BLK) for b in range(4)], res=x1, then=("loss", g_final.reshape(1, D_MODEL), target),
        name="mm_down_loss")

    d_wdown = _mm(act, dx2_bf, out_shape=(D_FF, D_MODEL), out_dtype=BF, grid=(4, 1, 1),
                  blk_a=ff_seq, blk_b=(SEQ, D_MODEL), blk_o=(FF_BLK, D_MODEL),
                  map_a=lambda b, i, k: (0, b, 0, 0), map_b=whole, map_o=lambda b, i, k: (b, 0), ta=True,
                  name="mm_d_wdown")
    du4, d_wconv, d_bconv = _down_conv_bwd(u4, jnp.pad(dx2, ((0, DA_PAD), (0, 0))), wdown, wconv4, bconv4)
    d_wt_up = _mm(du4, h2, out_shape=(2 * D_FF, D_MODEL), out_dtype=BF, grid=(N_DEV, 1, 1),
                  blk_a=ff_seq, blk_b=(SEQ, D_MODEL), blk_o=(FF_BLK, D_MODEL),
                  map_a=blk4, map_b=whole, map_o=lambda b, i, k: (b, 0), ta=True, name="mm_d_wup")
    res = {}

    def reduce_start(keys, parts):
        arrays = [t for k in keys for t in (parts[k], lax.empty((4,) + parts[k].shape[2:], BF))]
        st, tkn = _split_start("reduce_start_" + keys[0], [(arrays, 4 * len(keys), _reduce_first)])
        return st[0], tkn

    def reduce_cross(keys, st, after):
        arrays = _split_wait("reduce_wait_" + keys[0], st, _reduce_first, after)
        sums = [_pair_sum(p, r, core, "pair_sum_" + k) for k, p, r in zip(keys, arrays[0::2], arrays[1::2])]
        arrays = [t for s in sums for t in (s, lax.empty((3,) + s.shape[1:], BF))]
        st2, tkn = _split_start("reduce_cross_" + keys[0], [(arrays, 3 * len(keys), _reduce_second)])
        return st2[0], tkn

    def reduce_done(keys, st2, after):
        arrays = _split_wait("reduce_cross_wait_" + keys[0], st2, _reduce_second, after)
        for k, s, r in zip(keys, arrays[0::2], arrays[1::2]):
            outs = _chip_sum_adamw(s, r, big[k], moments[k][0], moments[k][1], chip, "adamw_" + k)
            res[k] = [(t.T if k in ("w_in", "w_up") else t)[None] for t in outs]

    ffn_keys = ("w_down", "w_up")
    ffn_red, tkn = reduce_start(ffn_keys, dict(w_down=d_wdown.reshape(4, 2, D_FF // N_DEV, D_MODEL),
                                               w_up=d_wt_up.reshape(4, 2, FF_BLK, D_MODEL)))
    dx1, dg_ffn = _mm_tokens(
        du4, wt_up, blk_a=(2, 4, TOK_MM_TILE, FF_BLK), map_a=lambda i: (0, 0, i, 0),
        pieces=[((b // 4, b % 4), b * FF_BLK, FF_BLK) for b in range(N_DEV)], after=tkn, then=("rms_bwd", x1, g_ffn, dx2),
        name="mm_d_h2_rms")

    sq_t = dict(out_shape=(D_MODEL, D_MODEL), grid=(1, 1, N_MM_TILES), blk_a=(MM_TILE, D_MODEL),
                blk_b=(MM_TILE, D_MODEL), blk_o=(D_MODEL, D_MODEL), map_a=kblk, map_b=kblk, map_o=whole, ta=True)
    d_wout = _mm(mixed, dx1, out_dtype=BF, name="mm_d_wout", **sq_t)
    dzcat, dy_pool, dy_gla, db_gate = _mix_bwd(dx1, wout, zcat, b_gate, y_pool, y_gla)
    ffn_red, tkn = reduce_cross(ffn_keys, ffn_red, db_gate)
    d_wgp = _mm(og, dy_gla, out_dtype=BF, after=tkn, name="mm_d_wgp", **sq_t)
    mix_keys = ("w_out", "w_gla_proj")
    mix_red, tkn = reduce_start(mix_keys, dict(w_out=d_wout.reshape(4, 2, D_MODEL // N_DEV, D_MODEL),
                                               w_gla_proj=d_wgp.reshape(4, 2, D_MODEL // N_DEV, D_MODEL)))
    dzcat, d_o, dg_head = _post_gla_bwd(dzcat, dy_gla, wgp, o, zcat, g_gla_head + tkn[:1, :1])
    dzcat, dla = _gla_bwd(dzcat, zcat, la, d_o, states)
    mix_red, tkn = reduce_cross(mix_keys, mix_red, dla)
    dh_gk, d_wt_gk, d_wgk, db_gk = _gk_bwd(dla, h, wt_gk, wgk_pad, b_gk + tkn[:1, :1])
    dps = _mm(dy_pool, wpp, out_shape=(SEQ, POOL_WIDTH), out_dtype=F32, grid=(N_MM_TILES, 1, 1),
              blk_a=(MM_TILE, D_MODEL), blk_b=(POOL_WIDTH, D_MODEL), blk_o=(MM_TILE, POOL_WIDTH),
              map_a=tok, map_b=whole, map_o=tok, tb=True, name="mm_d_ps")
    d_wpp = _mm(ps, dy_pool, out_shape=(POOL_WIDTH, D_MODEL), out_dtype=F32, grid=(1, 1, N_MM_TILES),
                blk_a=(MM_TILE, POOL_WIDTH), blk_b=(MM_TILE, D_MODEL), blk_o=(POOL_WIDTH, D_MODEL),
                map_a=kblk, map_b=kblk, map_o=whole, ta=True, name="mm_d_wpp")
    dzcat, d_wgrp, d_scale = _pool_bwd(dzcat, zcat, dps, w_pool_grp[0], pool_scale)
    row = lambda t: t.reshape(1, D_MODEL)
    conv_vec = lambda t: t.reshape(2, 4, 1, FF_BLK)
    small = [("b_gate", db_gate, b_gate, m_b_gate, v_b_gate, False),
             ("w_gk_up", d_wgk.reshape(GATE_RANK, N_DEV, GLA_DK // N_DEV).transpose(1, 0, 2), w_gk_up, m_w_gk_up,
              v_w_gk_up, True),
             ("b_gk", db_gk, b_gk, m_b_gk, v_b_gk, False),
             ("w_pool_grp", d_wgrp, w_pool_grp, m_w_pool_grp, v_w_pool_grp, False),
             ("pool_scale", d_scale, pool_scale, m_pool_scale, v_pool_scale, False),
             ("g_gla_head", dg_head, g_gla_head, m_g_gla_head, v_g_gla_head, False),
             ("g_ffn", dg_ffn, g_ffn, m_g_ffn, v_g_ffn, False),
             ("w_conv", d_wconv.reshape(N_DEV, 3, FF_BLK), w_conv, m_w_conv, v_w_conv, True),
             ("b_conv", d_bconv, conv_vec(b_conv), conv_vec(m_b_conv), conv_vec(v_b_conv), False),
             ("g_final", dg_final, row(g_final), row(m_g_final), row(v_g_final), False)]

    def small_start(parts, name):
        arrays = [t for p in parts for t in (p, _gather_landing(p, me))]
        st, tkn = _split_start(name, [(arrays, 7 * len(parts), _gather_direct)])
        return st[0], tkn

    small_sent, tkn = small_start([t[1] for t in small] + [loss_part], "small_start")
    d_wt_cat = _mm(dzcat, h, out_shape=(N_CAT, D_MODEL), out_dtype=BF, grid=(N_CAT_TILES, 1, 1),
                   blk_a=(SEQ, CAT_TILE), blk_b=(SEQ, D_MODEL), blk_o=(CAT_TILE, D_MODEL),
                   map_a=lambda j, i, k: (0, j), map_b=whole, map_o=lambda j, i, k: (j, 0), ta=True, after=tkn,
                   name="mm_d_wcat")
    in_keys = ("w_in", "w_pool_proj")
    in_red, tkn = reduce_start(in_keys, dict(
        w_in=_shard_d_w_in(d_wt_cat, d_wt_gk).reshape(4, 2, IN_SHARD, D_MODEL),
        w_pool_proj=d_wpp.reshape(POOL_WIDTH, N_DEV, D_MODEL // N_DEV).transpose(1, 0, 2).astype(BF)
        .reshape(4, 2, POOL_WIDTH, D_MODEL // N_DEV)))
    in_red, tkn = reduce_cross(in_keys, in_red, tkn)
    grad_x, dg_mix = _mm_tokens(dzcat, wt_cat, blk_a=(TOK_MM_TILE, N_CAT), map_a=lambda i: (i, 0),
                                pieces=[(None, 0, N_CAT)], res=dh_gk, after=tkn, then=("rms_bwd", xs, g_mix, dx1),
                                name="mm_d_h_rms")
    g_mix_sent, _ = small_start([dg_mix], "g_mix_start")
    reduce_done(ffn_keys, ffn_red, grad_x)
    reduce_done(mix_keys, mix_red, res["w_down"][0])
    gathered = _split_wait("small_wait", small_sent, _gather_direct, res["w_out"][0])[1::2]
    small.append(("g_mix", dg_mix, g_mix, m_g_mix, v_g_mix, False))
    gathered = list(gathered[:-1]) + [_split_wait("g_mix_wait", g_mix_sent, _gather_direct, gathered[0])[1], gathered[-1]]
    small_out, loss_sum = _small_sum_adamw(jnp.reshape(me, (1,)).astype(jnp.int32),
                                           [(p,) + t[2:] for p, t in zip(gathered, small)], gathered[-1])
    for t, outs in zip(small, small_out):
        res[t[0]] = list(outs)
    res["b_conv"] = [t.reshape(b_conv.shape) for t in res["b_conv"]]
    res["g_final"] = [t.reshape(g_final.shape) for t in res["g_final"]]

    reduce_done(in_keys, in_red, loss_sum)
    loss = loss_sum[0, 0]
    order =["g_mix", "w_in", "b_gate", "w_gk_up", "b_gk", "w_pool_grp", "pool_scale", "g_gla_head", "w_pool_proj",
             "w_gla_proj", "w_out", "g_ffn", "w_up", "w_conv", "b_conv", "w_down", "g_final"]
    return (loss, grad_x[None], *[res[k][0] for k in order], *[res[k][1] for k in order],
            *[res[k][2] for k in order], *[res[k][3] for k in order])
```

```python
import jax
import jax.numpy as jnp
from jax import lax
from jax.experimental import pallas as pl
from jax.experimental.pallas import tpu as pltpu

F32 = jnp.float32
BF = jnp.bfloat16
HIGHEST = lax.Precision.HIGHEST
MESH = pl.DeviceIdType.MESH

N_DEV = 8
SEQ = 2048
D_MODEL = 1024
CHUNK = 64
EPS = 1e-6
POOL_WIDTH = 512
POOL_WINDOWS = (2, 4, 8, 16)
POOL_GD = 128
POOL_HALO = 16
HEADS = 4
HK = 128
HV = 256
GLA_DK = 512
GATE_RANK = 16
GATE_NORM = 16.0
D_FF = 2816
FF_BLK = 704
IN_SHARD = 706
C_QKV, C_GATE, C_OG, C_POOL = 0, 2048, 4096, 5120
N_CAT = 5632
R_POOL, R_QKV, R_OG, R_GK, R_GATE = 0, 512, 2560, 3584, 3600
GK_PAD = 128

ADAM_LR, ADAM_B1, ADAM_B2, ADAM_EPS, ADAM_WD, ADAM_STEP = 0.001, 0.9, 0.999, 1e-08, 0.01, 10
ADAM_C1 = 1.0 - ADAM_B1 ** ADAM_STEP
ADAM_C2 = 1.0 - ADAM_B2 ** ADAM_STEP

VMEM_BYTES_V7X = 64 * 1024 * 1024
VMEM_LIMIT = VMEM_BYTES_V7X * 3 // 4

TOK_TILE = 256
HALO = 8
GLA_CPS = 4


def _params(*sem):
    return pltpu.CompilerParams(dimension_semantics=sem, vmem_limit_bytes=VMEM_LIMIT)


def _const_spec(shape):
    nd = len(shape)
    return pl.BlockSpec(shape, lambda *_: (0,) * nd)


def _in_hbm(t):
    return pltpu.with_memory_space_constraint(t, pltpu.HBM)


def _dot(a, b, ta=False, tb=False):
    dims = (((0 if ta else 1,), (1 if tb else 0,)), ((), ()))
    return lax.dot_general(a.astype(BF), b.astype(BF), dims, preferred_element_type=F32)


def _dot_exact(a, b):
    return jnp.dot(a, b, precision=HIGHEST, preferred_element_type=F32)


def _sigmoid(x):
    return 0.5 * jnp.tanh(0.5 * x) + 0.5


def _mm(a, b, *, out_shape, out_dtype, grid, blk_a, blk_b, blk_o, map_a, map_b, map_o, ta=False, tb=False,
        after=None, name):
    gk = grid[2]
    n_in = 2 + (after is not None)

    def body(*refs):
        a_ref, b_ref, o_ref = refs[0], refs[1], refs[n_in]
        prod = _dot(a_ref[...], b_ref[...], ta, tb)
        if gk == 1:
            o_ref[...] = prod.astype(out_dtype)
        else:
            acc = refs[n_in + 1]
            k = pl.program_id(2)

            @pl.when(k == 0)
            def _():
                acc[...] = prod

            @pl.when(k > 0)
            def _():
                acc[...] += prod

            @pl.when(k == gk - 1)
            def _():
                o_ref[...] = acc[...].astype(out_dtype)

    in_specs = [pl.BlockSpec(blk_a, map_a), pl.BlockSpec(blk_b, map_b)]
    args = [_in_hbm(a), _in_hbm(b)]
    if after is not None:
        in_specs.append(pl.BlockSpec(memory_space=pl.ANY))
        args.append(after)
    return pl.pallas_call(
        body, name=name, grid=grid, in_specs=in_specs, out_specs=pl.BlockSpec(blk_o, map_o),
        out_shape=jax.ShapeDtypeStruct(out_shape, out_dtype),
        scratch_shapes=[] if gk == 1 else [pltpu.VMEM(tuple(d for d in blk_o if d is not None), F32)],
        compiler_params=_params("parallel", "parallel", "arbitrary"),
    )(*args)


TOK_MM_TILE = 256


def _mm_tokens(a, w, *, blk_a, map_a, pieces, res=None, after=None, then=None, name):
    n_in = 2 + (res is not None) + (after is not None) + (0 if then is None else len(then) - 1)

    def accumulate(ref, part):
        @pl.when(pl.program_id(0) == 0)
        def _():
            ref[...] = part

        @pl.when(pl.program_id(0) > 0)
        def _():
            ref[...] += part

    def body(*refs):
        a_ref, w_ref = refs[:2]
        extra, outs = refs[n_in - (0 if then is None else len(then) - 1):n_in], refs[n_in:]
        total = None
        for idx, row, n in pieces:
            av = a_ref[...] if idx is None else a_ref[idx]
            prod = _dot(av, w_ref[row:row + n, :])
            total = prod if total is None else total + prod
        if res is not None:
            total = total + refs[2][...]
        if then is None:
            outs[0][...] = total
        elif then[0] == "rms_bwd":
            dx, part = _rms_bwd_tile(total, extra[0][...], extra[1][...], extra[2][...])
            outs[0][...] = dx
            accumulate(outs[1], part)
        else:
            lpart, dx, part = _loss_tile(total, extra[0][...], extra[1][...])
            outs[1][...] = dx
            outs[2][...] = dx.astype(BF)
            accumulate(outs[0], lpart)
            accumulate(outs[3], part)

    tile = pl.BlockSpec((TOK_MM_TILE, D_MODEL), lambda i: (i, 0))
    vec = _const_spec((1, D_MODEL))
    big = jax.ShapeDtypeStruct((SEQ, D_MODEL), F32)
    small = jax.ShapeDtypeStruct((1, D_MODEL), F32)
    in_specs = [pl.BlockSpec(blk_a, map_a), pl.BlockSpec(w.shape, lambda i: (0, 0), pipeline_mode=pl.Buffered(1))]
    args = [a, w]
    if res is not None:
        in_specs.append(tile)
        args.append(res)
    if after is not None:
        in_specs.append(pl.BlockSpec(memory_space=pl.ANY))
        args.append(after)
    if then is None:
        out_specs, out_shape = tile, big
    elif then[0] == "rms_bwd":
        in_specs += [tile, vec, tile]
        out_specs, out_shape = [tile, vec], [big, small]
    else:
        in_specs += [vec, tile]
        out_specs = [_const_spec((1, 128)), tile, tile, vec]
        out_shape = [jax.ShapeDtypeStruct((1, 128), F32), big, jax.ShapeDtypeStruct((SEQ, D_MODEL), BF), small]
    if then is not None:
        args += list(then[1:])
    return pl.pallas_call(
        body, name=name, grid=(SEQ // TOK_MM_TILE,), in_specs=in_specs, out_specs=out_specs, out_shape=out_shape,
        compiler_params=_params("parallel" if then is None else "arbitrary"),
    )(*[_in_hbm(t) for t in args])


def _rms_fwd(x, g, name):
    def body(x_ref, g_ref, o_ref):
        xv = x_ref[...]
        r = lax.rsqrt(jnp.mean(xv * xv, axis=-1, keepdims=True) + EPS)
        o_ref[...] = (xv * r * g_ref[...]).astype(BF)

    tile = pl.BlockSpec((TOK_TILE, D_MODEL), lambda i: (i, 0))
    return pl.pallas_call(
        body, name=name, grid=(SEQ // TOK_TILE,), in_specs=[tile, _const_spec((1, D_MODEL))], out_specs=tile,
        out_shape=jax.ShapeDtypeStruct((SEQ, D_MODEL), BF), compiler_params=_params("parallel"),
    )(*map(_in_hbm, (x, g)))


def _rms_bwd_tile(dyv, xv, gv, dresv):
    r = lax.rsqrt(jnp.mean(xv * xv, axis=-1, keepdims=True) + EPS)
    xn = xv * r
    dxn = dyv * gv
    return dresv + r * (dxn - xn * jnp.mean(dxn * xn, axis=-1, keepdims=True)), jnp.sum(dyv * xn, axis=0, keepdims=True)


def _loss_tile(xv, gv, tv):
    r = lax.rsqrt(jnp.mean(xv * xv, axis=-1, keepdims=True) + EPS)
    xn = xv * r
    err = xn * gv - tv
    lpart = jnp.full((1, 128), 0.5 * jnp.sum(jnp.mean(err * err, axis=-1, keepdims=True)), F32)
    dyv = err * (1.0 / D_MODEL)
    dxn = dyv * gv
    return lpart, r * (dxn - xn * jnp.mean(dxn * xn, axis=-1, keepdims=True)), jnp.sum(dyv * xn, axis=0, keepdims=True)


def _pool_counts(w):
    pos = lax.broadcasted_iota(jnp.int32, (SEQ, 1), 0).astype(F32)
    return jnp.minimum(pos + 1.0, float(w))


def _pool_window(u, w, ext):
    ext[pl.ds(POOL_HALO, SEQ), :] = u
    win = u
    for j in range(1, w):
        win = win + ext[pl.ds(POOL_HALO - j, SEQ), :]
    return win / _pool_counts(w) - u


def _pool_fwd(zcat, w_grp, scale):
    def body(z_ref, w_ref, s_ref, o_ref, ext):
        ext[pl.ds(0, POOL_HALO), :] = jnp.zeros((POOL_HALO, POOL_GD), F32)
        for g, w in enumerate(POOL_WINDOWS):
            cols = slice(g * POOL_GD, (g + 1) * POOL_GD)
            p = _pool_window(z_ref[:, cols], w, ext)
            o_ref[:, cols] = (_dot(p, w_ref[g]) * s_ref[:, cols]).astype(BF)

    return pl.pallas_call(
        body, name="pool_fwd", grid=(1,),
        in_specs=[pl.BlockSpec((SEQ, POOL_WIDTH), lambda i: (0, C_POOL // POOL_WIDTH)),
                  _const_spec((4, POOL_GD, POOL_GD)), _const_spec((1, POOL_WIDTH))],
        out_specs=_const_spec((SEQ, POOL_WIDTH)), out_shape=jax.ShapeDtypeStruct((SEQ, POOL_WIDTH), BF),
        scratch_shapes=[pltpu.VMEM((POOL_HALO + SEQ, POOL_GD), F32)], compiler_params=_params("arbitrary"),
    )(*map(_in_hbm, (zcat, w_grp, scale)))


def _pool_bwd(dzcat, zcat, dps, w_grp, scale):
    def body(dz_in, z_ref, dps_ref, w_ref, s_ref, dz_ref, dw_ref, dsc_ref, ext, ext2):
        del dz_in
        ext[pl.ds(0, POOL_HALO), :] = jnp.zeros((POOL_HALO, POOL_GD), F32)
        ext2[pl.ds(SEQ, POOL_HALO), :] = jnp.zeros((POOL_HALO, POOL_GD), F32)
        for g, w in enumerate(POOL_WINDOWS):
            cols = slice(g * POOL_GD, (g + 1) * POOL_GD)
            p = _pool_window(z_ref[:, cols], w, ext)
            wg = w_ref[g]
            pg = _dot(p, wg)
            dpsv = dps_ref[:, cols]
            dsc_ref[:, cols] = jnp.sum(dpsv * pg, axis=0, keepdims=True)
            dpg = dpsv * s_ref[:, cols]
            dw_ref[g] = _dot(p, dpg, ta=True)
            dp = _dot(dpg, wg, tb=True)
            dpc = dp / _pool_counts(w)
            ext2[pl.ds(0, SEQ), :] = dpc
            du = dpc
            for j in range(1, w):
                du = du + ext2[pl.ds(j, SEQ), :]
            dz_ref[:, cols] = (du - dp).astype(BF)

    return pl.pallas_call(
        body, name="pool_bwd", grid=(1,),
        in_specs=[pl.BlockSpec(memory_space=pl.ANY),
                  pl.BlockSpec((SEQ, POOL_WIDTH), lambda i: (0, C_POOL // POOL_WIDTH)),
                  _const_spec((SEQ, POOL_WIDTH)), _const_spec((4, POOL_GD, POOL_GD)), _const_spec((1, POOL_WIDTH))],
        out_specs=[pl.BlockSpec((SEQ, POOL_WIDTH), lambda i: (0, C_POOL // POOL_WIDTH)),
                   _const_spec((4, POOL_GD, POOL_GD)), _const_spec((1, POOL_WIDTH))],
        out_shape=[jax.ShapeDtypeStruct((SEQ, N_CAT), BF), jax.ShapeDtypeStruct((4, POOL_GD, POOL_GD), F32),
                   jax.ShapeDtypeStruct((1, POOL_WIDTH), F32)],
        scratch_shapes=[pltpu.VMEM((POOL_HALO + SEQ, POOL_GD), F32), pltpu.VMEM((SEQ + POOL_HALO, POOL_GD), F32)],
        input_output_aliases={0: 0}, compiler_params=_params("arbitrary"),
    )(*map(_in_hbm, (dzcat, zcat, dps, w_grp, scale)))


GK_TILE = 512


def _gk_fwd(h, wt_gk, wgk_pad, b_gk):
    def body(h_ref, wt_ref, w_ref, b_ref, la_ref):
        z_gk = _dot(h_ref[...], wt_ref[...], tb=True)
        pre = _dot(z_gk, w_ref[...]) + b_ref[...]
        la_ref[...] = (jnp.minimum(pre, 0.0) - jnp.log(1.0 + jnp.exp(-jnp.abs(pre)))) * (1.0 / GATE_NORM)

    return pl.pallas_call(
        body, name="gk_fwd", grid=(SEQ // GK_TILE,),
        in_specs=[pl.BlockSpec((GK_TILE, D_MODEL), lambda i: (i, 0)), _const_spec((GK_PAD, D_MODEL)),
                  _const_spec((GK_PAD, GLA_DK)), _const_spec((1, GLA_DK))],
        out_specs=pl.BlockSpec((GK_TILE, GLA_DK), lambda i: (i, 0)),
        out_shape=jax.ShapeDtypeStruct((SEQ, GLA_DK), F32), compiler_params=_params("parallel"),
    )(*map(_in_hbm, (h, wt_gk, wgk_pad, b_gk)))


def _gk_bwd(dla, h, wt_gk, wgk_pad, b_gk):
    def body(dla_ref, h_ref, wt_ref, w_ref, b_ref, dh_ref, dwt_ref, dw_ref, db_ref):
        hv = h_ref[...]
        wtv = wt_ref[...]
        wv = w_ref[...]
        z_gk = _dot(hv, wtv, tb=True)
        pre = _dot(z_gk, wv) + b_ref[...]
        dpre = dla_ref[...] * (1.0 / GATE_NORM) * (1.0 - _sigmoid(pre))
        dz_gk = _dot(dpre, wv, tb=True)
        dh_ref[...] = _dot(dz_gk, wtv)
        dwtp = _dot(dz_gk, hv, ta=True)
        dwp = _dot(z_gk, dpre, ta=True)[:GATE_RANK]
        dbp = jnp.sum(dpre, axis=0, keepdims=True)

        @pl.when(pl.program_id(0) == 0)
        def _():
            dwt_ref[...] = dwtp
            dw_ref[...] = dwp
            db_ref[...] = dbp

        @pl.when(pl.program_id(0) > 0)
        def _():
            dwt_ref[...] += dwtp
            dw_ref[...] += dwp
            db_ref[...] += dbp

    tile = pl.BlockSpec((GK_TILE, D_MODEL), lambda i: (i, 0))
    return pl.pallas_call(
        body, name="gk_bwd", grid=(SEQ // GK_TILE,),
        in_specs=[pl.BlockSpec((GK_TILE, GLA_DK), lambda i: (i, 0)), tile, _const_spec((GK_PAD, D_MODEL)),
                  _const_spec((GK_PAD, GLA_DK)), _const_spec((1, GLA_DK))],
        out_specs=[tile, _const_spec((GK_PAD, D_MODEL)), _const_spec((GATE_RANK, GLA_DK)), _const_spec((1, GLA_DK))],
        out_shape=[jax.ShapeDtypeStruct((SEQ, D_MODEL), F32), jax.ShapeDtypeStruct((GK_PAD, D_MODEL), F32),
                   jax.ShapeDtypeStruct((GATE_RANK, GLA_DK), F32), jax.ShapeDtypeStruct((1, GLA_DK), F32)],
        compiler_params=_params("arbitrary"),
    )(*map(_in_hbm, (dla, h, wt_gk, wgk_pad, b_gk)))


GLA_ROWS = GLA_CPS * CHUNK
GLA_STEPS = SEQ // GLA_ROWS
QKV_W = 2048


def _tri():
    return lax.broadcasted_iota(jnp.int32, (CHUNK, CHUNK), 0) >= lax.broadcasted_iota(jnp.int32, (CHUNK, CHUNK), 1)


def _chunk_cumsum(la_ref, rows):
    return _dot_exact(_tri().astype(F32), la_ref[rows, :])


def _gla_chunk(qkv_ref, la_ref, rows, h, bc_all):
    tri = _tri()
    q = qkv_ref[rows, h * HK:(h + 1) * HK] * (HK ** -0.5)
    k = qkv_ref[rows, GLA_DK + h * HK:GLA_DK + (h + 1) * HK]
    v = qkv_ref[rows, 2 * GLA_DK + h * HV:2 * GLA_DK + (h + 1) * HV].astype(BF)
    la = la_ref[rows, h * HK:(h + 1) * HK]
    bc = bc_all[:, h * HK:(h + 1) * HK]
    e_pos, e_neg = jnp.exp(bc), jnp.exp(-bc)
    dl = jnp.exp(jnp.sum(la, axis=0, keepdims=True))
    q_fw, q_bw, k_fw, k_bw = q * e_pos, q * e_neg, k * e_neg, k * e_pos
    scores = jnp.where(tri, _dot(q_fw, k_fw, tb=True), _dot(q_bw, k_bw, tb=True))
    return tri, v, e_pos, e_neg, dl, q_fw, q_bw, k_fw, k_bw, scores


def _gla_fwd(zcat, la, after):
    def body(qkv_ref, la_ref, after_ref, o_ref, st_ref, state):
        del after_ref

        @pl.when(pl.program_id(0) == 0)
        def _():
            state[...] = jnp.zeros_like(state)

        for c in range(GLA_CPS):
            rows = slice(c * CHUNK, (c + 1) * CHUNK)
            bc_all = _chunk_cumsum(la_ref, rows)
            for h in range(HEADS):
                _, v, _, _, dl, q_fw, _, k_fw, _, scores = _gla_chunk(qkv_ref, la_ref, rows, h, bc_all)
                st = state[h]
                st_ref[c, h] = st
                o_ref[rows, h * HV:(h + 1) * HV] = _dot(scores, v) + _dot(q_fw, st, tb=True)
                state[h] = st * dl + _dot(v, k_fw * dl, ta=True)

    return pl.pallas_call(
        body, name="gla_fwd", grid=(GLA_STEPS,),
        in_specs=[pl.BlockSpec((GLA_ROWS, QKV_W), lambda i: (i, 0)), pl.BlockSpec((GLA_ROWS, GLA_DK), lambda i: (i, 0)),
                  pl.BlockSpec(memory_space=pl.ANY)],
        out_specs=[pl.BlockSpec((GLA_ROWS, D_MODEL), lambda i: (i, 0)),
                   pl.BlockSpec((GLA_CPS, HEADS, HV, HK), lambda i: (i, 0, 0, 0))],
        out_shape=[jax.ShapeDtypeStruct((SEQ, D_MODEL), F32),
                   jax.ShapeDtypeStruct((SEQ // CHUNK, HEADS, HV, HK), F32)],
        scratch_shapes=[pltpu.VMEM((HEADS, HV, HK), F32)], compiler_params=_params("arbitrary"),
    )(*map(_in_hbm, (zcat, la)), after)


def _gla_bwd(dzcat, zcat, la, d_o, states):
    def body(dz_in, qkv_ref, la_ref, do_ref, st_ref, dqkv_ref, dla_ref, dstate):
        del dz_in

        @pl.when(pl.program_id(0) == 0)
        def _():
            dstate[...] = jnp.zeros_like(dstate)

        last_row = lax.broadcasted_iota(jnp.int32, (CHUNK, HK), 0) == CHUNK - 1
        upper = (lax.broadcasted_iota(jnp.int32, (CHUNK, CHUNK), 0)
                 <= lax.broadcasted_iota(jnp.int32, (CHUNK, CHUNK), 1)).astype(F32)
        for c in reversed(range(GLA_CPS)):
            rows = slice(c * CHUNK, (c + 1) * CHUNK)
            bc_all = _chunk_cumsum(la_ref, rows)
            dbs = []
            for h in range(HEADS):
                tri, v, e_pos, e_neg, dl, q_fw, q_bw, k_fw, k_bw, scores = _gla_chunk(qkv_ref, la_ref, rows, h, bc_all)
                st = st_ref[c, h]
                dst = dstate[h]
                d_out = do_ref[rows, h * HV:(h + 1) * HV].astype(BF)
                k_dec = k_fw * dl
                dp = _dot(d_out, v, tb=True)
                dp_fw = jnp.where(tri, dp, 0.0)
                dp_bw = jnp.where(tri, 0.0, dp)
                dv = _dot(scores, d_out, ta=True) + _dot(k_dec, dst, tb=True)
                dk_dec = _dot(v, dst)
                dq_fw = _dot(dp_fw, k_fw) + _dot(d_out, st)
                dk_fw = _dot(dp_fw, q_fw, ta=True) + dk_dec * dl
                dq_bw = _dot(dp_bw, k_bw)
                dk_bw = _dot(dp_bw, q_bw, ta=True)
                ddl = jnp.sum(st * dst, axis=0, keepdims=True) + jnp.sum(k_fw * dk_dec, axis=0, keepdims=True)
                dstate[h] = dst * dl + _dot(d_out, q_fw, ta=True)
                dq = (dq_fw * e_pos + dq_bw * e_neg) * (HK ** -0.5)
                dk = dk_fw * e_neg + dk_bw * e_pos
                dbs.append(dq_fw * q_fw - dk_fw * k_fw - dq_bw * q_bw + dk_bw * k_bw + jnp.where(last_row, ddl * dl, 0.0))
                dqkv_ref[rows, h * HK:(h + 1) * HK] = dq.astype(BF)
                dqkv_ref[rows, GLA_DK + h * HK:GLA_DK + (h + 1) * HK] = dk.astype(BF)
                dqkv_ref[rows, 2 * GLA_DK + h * HV:2 * GLA_DK + (h + 1) * HV] = dv.astype(BF)
            dla_ref[rows, :] = _dot_exact(upper, jnp.concatenate(dbs, axis=1))

    rev = lambda i: (GLA_STEPS - 1 - i, 0)
    return pl.pallas_call(
        body, name="gla_bwd", grid=(GLA_STEPS,),
        in_specs=[pl.BlockSpec(memory_space=pl.ANY), pl.BlockSpec((GLA_ROWS, QKV_W), rev),
                  pl.BlockSpec((GLA_ROWS, GLA_DK), rev), pl.BlockSpec((GLA_ROWS, D_MODEL), rev),
                  pl.BlockSpec((GLA_CPS, HEADS, HV, HK), lambda i: (GLA_STEPS - 1 - i, 0, 0, 0))],
        out_specs=[pl.BlockSpec((GLA_ROWS, QKV_W), rev), pl.BlockSpec((GLA_ROWS, GLA_DK), rev)],
        out_shape=[jax.ShapeDtypeStruct((SEQ, N_CAT), BF), jax.ShapeDtypeStruct((SEQ, GLA_DK), F32)],
        scratch_shapes=[pltpu.VMEM((HEADS, HV, HK), F32)], input_output_aliases={0: 0},
        compiler_params=_params("arbitrary"),
    )(*map(_in_hbm, (dzcat, zcat, la, d_o, states)))


def _silu_parts(x):
    s = _sigmoid(x)
    return x * s, s * (1.0 + x * (1.0 - s))


def _post_gla_fwd(o, zcat, g_head):
    def body(o_ref, zog_ref, g_ref, out_ref):
        for h in range(HEADS):
            cols = slice(h * HV, (h + 1) * HV)
            ov = o_ref[:, cols]
            r = lax.rsqrt(jnp.mean(ov * ov, axis=-1, keepdims=True) + EPS)
            act, _ = _silu_parts(zog_ref[:, cols])
            out_ref[:, cols] = (ov * r * g_ref[...] * act).astype(BF)

    tile = pl.BlockSpec((TOK_TILE, D_MODEL), lambda i: (i, 0))
    return pl.pallas_call(
        body, name="post_gla_fwd", grid=(SEQ // TOK_TILE,),
        in_specs=[tile, pl.BlockSpec((TOK_TILE, D_MODEL), lambda i: (i, C_OG // D_MODEL)), _const_spec((1, HV))],
        out_specs=tile, out_shape=jax.ShapeDtypeStruct((SEQ, D_MODEL), BF), compiler_params=_params("parallel"),
    )(*map(_in_hbm, (o, zcat, g_head)))


def _post_gla_bwd(dzcat, dy_gla, w_gla_proj, o, zcat, g_head):
    def body(dz_in, dyg_ref, w_ref, o_ref, zog_ref, g_ref, dz_ref, do_ref, dg_ref):
        del dz_in
        dog = _dot(dyg_ref[...], w_ref[...], tb=True)
        gpart = jnp.zeros((1, HV), F32)
        gv = g_ref[...]
        for h in range(HEADS):
            cols = slice(h * HV, (h + 1) * HV)
            ov = o_ref[:, cols]
            r = lax.rsqrt(jnp.mean(ov * ov, axis=-1, keepdims=True) + EPS)
            on = ov * r
            act, dact = _silu_parts(zog_ref[:, cols])
            dogv = dog[:, cols]
            dz_ref[:, cols] = (dogv * on * gv * dact).astype(BF)
            d_on_g = dogv * act
            gpart = gpart + jnp.sum(d_on_g * on, axis=0, keepdims=True)
            dxn = d_on_g * gv
            do_ref[:, cols] = r * (dxn - on * jnp.mean(dxn * on, axis=-1, keepdims=True))

        @pl.when(pl.program_id(0) == 0)
        def _():
            dg_ref[...] = gpart

        @pl.when(pl.program_id(0) > 0)
        def _():
            dg_ref[...] += gpart

    tile = pl.BlockSpec((TOK_TILE, D_MODEL), lambda i: (i, 0))
    ogspec = pl.BlockSpec((TOK_TILE, D_MODEL), lambda i: (i, C_OG // D_MODEL))
    return pl.pallas_call(
        body, name="post_gla_bwd", grid=(SEQ // TOK_TILE,),
        in_specs=[pl.BlockSpec(memory_space=pl.ANY), tile, _const_spec((D_MODEL, D_MODEL)), tile, ogspec,
                  _const_spec((1, HV))],
        out_specs=[ogspec, tile, _const_spec((1, HV))],
        out_shape=[jax.ShapeDtypeStruct((SEQ, N_CAT), BF), jax.ShapeDtypeStruct((SEQ, D_MODEL), F32),
                   jax.ShapeDtypeStruct((1, HV), F32)],
        input_output_aliases={0: 0}, compiler_params=_params("arbitrary"),
    )(*map(_in_hbm, (dzcat, dy_gla, w_gla_proj, o, zcat, g_head)))


GATE_W = 2 * D_MODEL


def _mix_out_fwd(ps, og, zcat, x, w_pool_proj, w_gla_proj, w_out, b_gate, g_ffn, after):
    def body(ps_ref, og_ref, zg_ref, x_ref, wpp_ref, wgp_ref, wout_ref, b_ref, g_ref, after_ref,
             yp_ref, yg_ref, mixed_ref, x1_ref, h2_ref):
        del after_ref
        y_pool = _dot(ps_ref[...], wpp_ref[...])
        y_gla = _dot(og_ref[...], wgp_ref[...])
        yp_ref[...] = y_pool
        yg_ref[...] = y_gla
        g0 = _sigmoid(zg_ref[:, :D_MODEL] + b_ref[:, :D_MODEL])
        g1 = _sigmoid(zg_ref[:, D_MODEL:] + b_ref[:, D_MODEL:])
        mixed = (g0 * y_pool + g1 * y_gla).astype(BF)
        mixed_ref[...] = mixed
        x1 = x_ref[...] + _dot(mixed, wout_ref[...])
        x1_ref[...] = x1
        r = lax.rsqrt(jnp.mean(x1 * x1, axis=-1, keepdims=True) + EPS)
        h2_ref[...] = (x1 * r * g_ref[...]).astype(BF)

    tile = pl.BlockSpec((TOK_TILE, D_MODEL), lambda i: (i, 0))
    resident = lambda shape: pl.BlockSpec(shape, lambda i: (0, 0), pipeline_mode=pl.Buffered(1))
    f32, bf16 = jax.ShapeDtypeStruct((SEQ, D_MODEL), F32), jax.ShapeDtypeStruct((SEQ, D_MODEL), BF)
    return pl.pallas_call(
        body, name="mix_out_fwd", grid=(SEQ // TOK_TILE,),
        in_specs=[pl.BlockSpec((TOK_TILE, POOL_WIDTH), lambda i: (i, 0)), tile,
                  pl.BlockSpec((TOK_TILE, GATE_W), lambda i: (i, C_GATE // GATE_W)), tile,
                  resident((POOL_WIDTH, D_MODEL)), resident((D_MODEL, D_MODEL)), resident((D_MODEL, D_MODEL)),
                  _const_spec((1, GATE_W)), _const_spec((1, D_MODEL)), pl.BlockSpec(memory_space=pl.ANY)],
        out_specs=[tile] * 5, out_shape=[f32, f32, bf16, f32, bf16], compiler_params=_params("parallel"),
    )(*map(_in_hbm, (ps, og, zcat, x, w_pool_proj, w_gla_proj, w_out, b_gate, g_ffn)), after)


def _mix_bwd(dx1, w_out, zcat, b_gate, y_pool, y_gla):
    def body(dx_ref, w_ref, zg_ref, b_ref, yp_ref, yg_ref, dz_ref, dyp_ref, dyg_ref, db_ref):
        dm = _dot(dx_ref[...], w_ref[...], tb=True)
        g0 = _sigmoid(zg_ref[:, :D_MODEL] + b_ref[:, :D_MODEL])
        g1 = _sigmoid(zg_ref[:, D_MODEL:] + b_ref[:, D_MODEL:])
        dyp_ref[...] = (dm * g0).astype(BF)
        dyg_ref[...] = (dm * g1).astype(BF)
        dz0 = dm * yp_ref[...] * g0 * (1.0 - g0)
        dz1 = dm * yg_ref[...] * g1 * (1.0 - g1)
        dz_ref[:, :D_MODEL] = dz0.astype(BF)
        dz_ref[:, D_MODEL:] = dz1.astype(BF)
        b0 = jnp.sum(dz0, axis=0, keepdims=True)
        b1 = jnp.sum(dz1, axis=0, keepdims=True)

        @pl.when(pl.program_id(0) == 0)
        def _():
            db_ref[:, :D_MODEL] = b0
            db_ref[:, D_MODEL:] = b1

        @pl.when(pl.program_id(0) > 0)
        def _():
            db_ref[:, :D_MODEL] += b0
            db_ref[:, D_MODEL:] += b1

    tile = pl.BlockSpec((TOK_TILE, D_MODEL), lambda i: (i, 0))
    gspec = pl.BlockSpec((TOK_TILE, GATE_W), lambda i: (i, C_GATE // GATE_W))
    return pl.pallas_call(
        body, name="mix_bwd", grid=(SEQ // TOK_TILE,),
        in_specs=[tile, _const_spec((D_MODEL, D_MODEL)), gspec, _const_spec((1, GATE_W)), tile, tile],
        out_specs=[gspec, tile, tile, _const_spec((1, GATE_W))],
        out_shape=[jax.ShapeDtypeStruct((SEQ, N_CAT), BF), jax.ShapeDtypeStruct((SEQ, D_MODEL), BF),
                   jax.ShapeDtypeStruct((SEQ, D_MODEL), BF), jax.ShapeDtypeStruct((1, GATE_W), F32)],
        compiler_params=_params("arbitrary"),
    )(*map(_in_hbm, (dx1, w_out, zcat, b_gate, y_pool, y_gla)))


N_TOK_TILES = SEQ // TOK_TILE
HALO_PER_TILE = TOK_TILE // HALO


LANE_TILES = tuple((lo, min(128, FF_BLK - lo)) for lo in range(0, FF_BLK, 128))


def _taps(w_ref, b_ref, half, lanes, rows):
    shape = (rows, lanes.stop - lanes.start)
    return ([jnp.broadcast_to(w_ref[half, j:j + 1, lanes], shape) for j in range(3)],
            jnp.broadcast_to(b_ref[half, :, lanes], shape))


def _conv_strips(u_ref, ub_ref, ua_ref, taps, lanes, width, n_strips, first):
    row = lax.broadcasted_iota(jnp.int32, (HALO, width), 0)
    prev = [[pltpu.roll(jnp.where(first, 0.0, ub_ref[half, :, lanes]), k, 0) for k in (1, 2)] for half in range(2)]
    for s in range(n_strips + (ua_ref is not None)):
        u3, conv = [], []
        for half in range(2):
            cur = u_ref[half, s * HALO:(s + 1) * HALO, lanes] if s < n_strips else ua_ref[half, :, lanes]
            rolled = [pltpu.roll(cur, k, 0) for k in (1, 2)]
            frames = [jnp.where(row >= 2, rolled[1], prev[half][1]), jnp.where(row >= 1, rolled[0], prev[half][0]), cur]
            prev[half] = rolled
            w3, bias = taps[half]
            u3.append(frames)
            conv.append(bias + frames[0] * w3[0] + frames[1] * w3[1] + frames[2] * w3[2])
        yield s, u3, conv


def _pair_specs(pairs):
    tile = pl.BlockSpec((pairs, None, TOK_TILE, FF_BLK), lambda b, i: (0, b, i, 0))
    before = pl.BlockSpec((pairs, None, HALO, FF_BLK), lambda b, i: (0, b, jnp.maximum(i * HALO_PER_TILE - 1, 0), 0))
    after = pl.BlockSpec((pairs, None, HALO, FF_BLK),
                         lambda b, i: (0, b, jnp.minimum((i + 1) * HALO_PER_TILE, SEQ // HALO - 1), 0))

    def vec(rows):
        return pl.BlockSpec((2, None, rows, FF_BLK), lambda b, i: (0, b, 0, 0))

    return tile, before, after, vec


N_STRIPS = TOK_TILE // HALO


def _up_conv_fwd(h2, wt_up, w_conv, b_conv):
    steps = N_TOK_TILES // 2

    def body(h_ref, h_next, wg_ref, wv_ref, w_ref, b_ref, u_ref, a_ref, buf_a, buf_b, carry):
        j = pl.program_id(1)

        def project(hv, buf):
            buf[0] = _dot(hv, wg_ref[...], tb=True)
            buf[1] = _dot(hv, wv_ref[...], tb=True)

        def conv(buf, row0):
            u_ref[:, row0:row0 + TOK_TILE, :] = buf[...]
            for lo, width in LANE_TILES:
                lanes = slice(lo, lo + width)
                taps = [_taps(w_ref, b_ref, half, lanes, HALO) for half in range(2)]
                pending = None
                for s, _, (cg, cv) in _conv_strips(buf, carry, None, taps, lanes, width, N_STRIPS, False):
                    act = cg * _sigmoid(cg) * cv
                    if s % 2 == 0:
                        pending = act
                    else:
                        a_ref[0, row0 + (s - 1) * HALO:row0 + (s + 1) * HALO, lanes] = (
                            jnp.concatenate([pending, act], axis=0).astype(BF))
            carry[...] = buf[:, TOK_TILE - HALO:, :]

        @pl.when(j == 0)
        def _():
            project(h_ref[0:TOK_TILE, :], buf_a)
            carry[...] = jnp.zeros_like(carry)

        project(h_ref[TOK_TILE:, :], buf_b)
        conv(buf_a, 0)
        project(h_next[...], buf_a)
        conv(buf_b, TOK_TILE)

    w_blk = lambda half: pl.BlockSpec((FF_BLK, D_MODEL), lambda b, j: (b + 4 * half, 0))
    vec = lambda rows: pl.BlockSpec((2, None, rows, FF_BLK), lambda b, j: (0, b, 0, 0))
    u_buf = pltpu.VMEM((2, TOK_TILE, FF_BLK), F32)
    return pl.pallas_call(
        body, name="up_conv_fwd", grid=(4, steps),
        in_specs=[pl.BlockSpec((2 * TOK_TILE, D_MODEL), lambda b, j: (j, 0)),
                  pl.BlockSpec((TOK_TILE, D_MODEL), lambda b, j: (jnp.minimum(2 * j + 2, N_TOK_TILES - 1), 0)),
                  w_blk(0), w_blk(1), vec(3), vec(1)],
        out_specs=[pl.BlockSpec((2, None, 2 * TOK_TILE, FF_BLK), lambda b, j: (0, b, j, 0)),
                   pl.BlockSpec((1, None, 2 * TOK_TILE, FF_BLK), lambda b, j: (0, b, j, 0))],
        out_shape=[jax.ShapeDtypeStruct((2, 4, SEQ, FF_BLK), F32), jax.ShapeDtypeStruct((1, 4, SEQ, FF_BLK), BF)],
        scratch_shapes=[u_buf, u_buf, pltpu.VMEM((2, HALO, FF_BLK), F32)],
        compiler_params=_params("parallel", "arbitrary"),
    )(*map(_in_hbm, (h2, h2, wt_up, wt_up, w_conv, b_conv)))


def _conv_bwd(u, da, w_conv, b_conv):
    def body(u_ref, ub_ref, ua_ref, da_ref, daa_ref, w_ref, b_ref, du_ref, dw_ref, db_ref):
        i = pl.program_id(1)

        @pl.when(i == 0)
        def _():
            dw_ref[...] = jnp.zeros_like(dw_ref)
            db_ref[...] = jnp.zeros_like(db_ref)

        for lo, width in LANE_TILES:
            lanes = slice(lo, lo + width)
            row = lax.broadcasted_iota(jnp.int32, (HALO, width), 0)
            taps = [_taps(w_ref, b_ref, half, lanes, HALO) for half in range(2)]
            acc_w = [[jnp.zeros((HALO, width), F32) for _ in range(3)] for _ in range(2)]
            acc_b = [jnp.zeros((HALO, width), F32) for _ in range(2)]
            da_pair, pending = None, [None, None]
            dc_prev, up_prev = [None, None], [None, None]
            for s, u3, (cg, cv) in _conv_strips(u_ref, ub_ref, ua_ref, taps, lanes, width, N_STRIPS, i == 0):
                act, dact = _silu_parts(cg)
                if s == N_STRIPS:
                    da = jnp.where(i < N_TOK_TILES - 1, daa_ref[0, :, lanes].astype(F32), 0.0)
                elif s % 2 == 0:
                    da_pair = da_ref[0, s * HALO:(s + 2) * HALO, lanes].astype(F32)
                    da = da_pair[:HALO]
                else:
                    da = da_pair[HALO:]
                dc = (da * cv * dact, da * act)
                for half in range(2):
                    up = [pltpu.roll(dc[half], HALO - k, 0) for k in (1, 2)]
                    if s < N_STRIPS:
                        for j in range(3):
                            acc_w[half][j] = acc_w[half][j] + dc[half] * u3[half][j]
                        acc_b[half] = acc_b[half] + dc[half]
                    if s >= 1:
                        w3 = taps[half][0]
                        du = (dc_prev[half] * w3[2] + jnp.where(row < HALO - 1, up_prev[half][0], up[0]) * w3[1]
                              + jnp.where(row < HALO - 2, up_prev[half][1], up[1]) * w3[0])
                        if (s - 1) % 2 == 0:
                            pending[half] = du
                        else:
                            du_ref[half, (s - 2) * HALO:s * HALO, lanes] = jnp.concatenate([pending[half], du],
                                                                                           axis=0).astype(BF)
                    dc_prev[half], up_prev[half] = dc[half], up
            for half in range(2):
                for j in range(3):
                    dw_ref[half, j:j + 1, lanes] += jnp.sum(acc_w[half][j], axis=0, keepdims=True)
                db_ref[half, :, lanes] += jnp.sum(acc_b[half], axis=0, keepdims=True)

    tile, before, after, vec = _pair_specs(2)
    da_tile, _, da_after_spec, _ = _pair_specs(1)
    return pl.pallas_call(
        body, name="conv_bwd", grid=(4, N_TOK_TILES),
        in_specs=[tile, before, after, da_tile, da_after_spec, vec(3), vec(1)],
        out_specs=[tile, vec(3), vec(1)],
        out_shape=[jax.ShapeDtypeStruct((2, 4, SEQ, FF_BLK), BF), jax.ShapeDtypeStruct((2, 4, 3, FF_BLK), F32),
                   jax.ShapeDtypeStruct((2, 4, 1, FF_BLK), F32)],
        compiler_params=_params("parallel", "arbitrary"),
    )(*map(_in_hbm, (u, u, u, da, da, w_conv, b_conv)))


W_IN_SEGMENTS = ((R_POOL, POOL_WIDTH, "cat", C_POOL), (R_QKV, QKV_W, "cat", C_QKV), (R_OG, D_MODEL, "cat", C_OG),
                 (R_GK, GATE_RANK, "gk", 0), (R_GATE, GATE_W, "cat", C_GATE))


def _slab_pieces(d):
    lo, hi = d * IN_SHARD, (d + 1) * IN_SHARD
    pieces = []
    for start, n, dest, at in W_IN_SEGMENTS:
        a, b = max(lo, start), min(hi, start + n)
        if a < b:
            assert (a - lo) % 2 == 0 and (b - a) % 2 == 0 and (at + a - start) % 2 == 0
            pieces.append(((a - lo) // 2, (b - a) // 2, dest, (at + a - start) // 2))
    return pieces


def _unshard_w_in(slabs):
    def body(slab_ref, cat_ref, gk_ref):
        d = pl.program_id(0)
        src = slab_ref.bitcast(jnp.uint32)
        dst = dict(cat=cat_ref.bitcast(jnp.uint32), gk=gk_ref.bitcast(jnp.uint32))

        @pl.when(d == 0)
        def _():
            gk_ref[...] = jnp.zeros_like(gk_ref)

        for dd in range(N_DEV):
            @pl.when(d == dd)
            def _():
                for a, n, dest, at in _slab_pieces(dd):
                    dst[dest][pl.ds(at, n), :] = src[0, pl.ds(a, n), :]

    return pl.pallas_call(
        body, name="unshard_w_in", grid=(N_DEV,),
        in_specs=[pl.BlockSpec((1, IN_SHARD, D_MODEL), lambda d: (d, 0, 0))],
        out_specs=[_const_spec((N_CAT, D_MODEL)), _const_spec((GK_PAD, D_MODEL))],
        out_shape=[jax.ShapeDtypeStruct((N_CAT, D_MODEL), BF), jax.ShapeDtypeStruct((GK_PAD, D_MODEL), BF)],
        compiler_params=_params("arbitrary"),
    )(_in_hbm(slabs))


def _shard_d_w_in(d_cat, d_gk):
    def body(cat_ref, gk_ref, slab_ref):
        d = pl.program_id(0)
        cat = cat_ref.bitcast(jnp.uint32)
        gk = pltpu.bitcast(gk_ref[0:GATE_RANK, :].astype(BF), jnp.uint32)
        dst = slab_ref.bitcast(jnp.uint32)
        for dd in range(N_DEV):
            @pl.when(d == dd)
            def _():
                for a, n, source, at in _slab_pieces(dd):
                    dst[0, pl.ds(a, n), :] = gk[at:at + n] if source == "gk" else cat[pl.ds(at, n), :]

    return pl.pallas_call(
        body, name="shard_d_w_in", grid=(N_DEV,),
        in_specs=[_const_spec((N_CAT, D_MODEL)), _const_spec((GK_PAD, D_MODEL))],
        out_specs=pl.BlockSpec((1, IN_SHARD, D_MODEL), lambda d: (d, 0, 0)),
        out_shape=jax.ShapeDtypeStruct((N_DEV, IN_SHARD, D_MODEL), BF), compiler_params=_params("parallel"),
    )(_in_hbm(d_cat), _in_hbm(d_gk))


ANY = pl.BlockSpec(memory_space=pl.ANY)


def _place():
    x, y, c = lax.axis_index("x"), lax.axis_index("y"), lax.axis_index("c")
    other_chips = [(1 - x, y), (x, 1 - y), (1 - x, 1 - y)]
    return x, y, c, other_chips


SEM = pl.BlockSpec(memory_space=pltpu.SEMAPHORE)
IN_HBM = pl.BlockSpec(memory_space=pltpu.HBM)
SPLIT_PARAMS = pltpu.CompilerParams(has_side_effects=pltpu.SideEffectType.DATAFLOW_SIDE_EFFECTING)


def _gather_first(refs, send_sems, recv_sems):
    x, y, c, chips = _place()
    targets = [(x, y, 1 - c)] + [(px, py, c) for px, py in chips]
    return [pltpu.make_async_remote_copy(src_ref=refs[2 * a], dst_ref=refs[2 * a + 1].at[4 * x + 2 * y + c],
                                         send_sem=send_sems.at[4 * a + k], recv_sem=recv_sems.at[4 * a + k],
                                         device_id=to, device_id_type=MESH)
            for a in range(len(refs) // 2) for k, to in enumerate(targets)]


def _gather_direct(refs, send_sems, recv_sems):
    x, y, c, _ = _place()
    flips = [(dx, dy, dc) for dx in (0, 1) for dy in (0, 1) for dc in (0, 1) if dx + dy + dc]
    targets = [(1 - x if dx else x, 1 - y if dy else y, 1 - c if dc else c) for dx, dy, dc in flips]
    return [pltpu.make_async_remote_copy(src_ref=refs[2 * a], dst_ref=refs[2 * a + 1].at[4 * x + 2 * y + c],
                                         send_sem=send_sems.at[7 * a + k], recv_sem=recv_sems.at[7 * a + k],
                                         device_id=to, device_id_type=MESH)
            for a in range(len(refs) // 2) for k, to in enumerate(targets)]


def _gather_second(refs, send_sems, recv_sems):
    x, y, c, chips = _place()
    copies = []
    for a, land in enumerate(refs):
        for j, (px, py) in enumerate(chips):
            block = land.at[4 * px + 2 * py + c]
            copies.append(pltpu.make_async_remote_copy(src_ref=block, dst_ref=block, send_sem=send_sems.at[3 * a + j],
                                                       recv_sem=recv_sems.at[3 * a + j], device_id=(x, y, 1 - c),
                                                       device_id_type=MESH))
    return copies


def _reduce_first(refs, send_sems, recv_sems):
    x, y, c, _ = _place()
    return [pltpu.make_async_remote_copy(src_ref=refs[2 * a].at[j, 1 - c], dst_ref=refs[2 * a + 1].at[j],
                                         send_sem=send_sems.at[4 * a + j], recv_sem=recv_sems.at[4 * a + j],
                                         device_id=(x, y, 1 - c), device_id_type=MESH)
            for a in range(len(refs) // 2) for j in range(4)]


def _reduce_second(refs, send_sems, recv_sems):
    _, _, c, chips = _place()
    return [pltpu.make_async_remote_copy(src_ref=refs[2 * a].at[2 * px + py], dst_ref=refs[2 * a + 1].at[k],
                                         send_sem=send_sems.at[3 * a + k], recv_sem=recv_sems.at[3 * a + k],
                                         device_id=(px, py, c), device_id_type=MESH)
            for a in range(len(refs) // 2) for k, (px, py) in enumerate(chips)]


def _split_start(name, groups):
    arrays = [a for g in groups for a in g[0]]
    n = len(arrays)

    def body(*refs):
        sems = refs[n:n + 2 * len(groups)]
        at = 0
        for gi, (members, _, build) in enumerate(groups):
            for cp in build(refs[at:at + len(members)], sems[2 * gi], sems[2 * gi + 1]):
                cp.start()
            at += len(members)
        refs[-1][...] = jnp.zeros_like(refs[-1])

    sem_shapes = [pltpu.SemaphoreType.DMA((g[1],)) for g in groups for _ in range(2)]
    outs = pl.pallas_call(
        body, name=name, in_specs=[IN_HBM] * n,
        out_shape=(*sem_shapes, *[pltpu.HBM(a.shape, a.dtype) for a in arrays], jax.ShapeDtypeStruct((8, 128), F32)),
        out_specs=(*[SEM] * len(sem_shapes), *[IN_HBM] * n, pl.BlockSpec(memory_space=pltpu.VMEM)),
        input_output_aliases={i: len(sem_shapes) + i for i in range(n)}, compiler_params=SPLIT_PARAMS,
    )(*[pltpu.with_memory_space_constraint(a, pltpu.HBM) for a in arrays])
    per_group, at = [], len(sem_shapes)
    for gi, (members, _, _) in enumerate(groups):
        per_group.append((outs[2 * gi], outs[2 * gi + 1], list(outs[at:at + len(members)])))
        at += len(members)
    return per_group, outs[-1]


def _split_wait(name, started, build, after):
    send_sems, recv_sems, arrays = started
    n = len(arrays)
    after = after if isinstance(after, (tuple, list)) else (after,)

    def body(*refs):
        for cp in build(refs[:n], refs[n], refs[n + 1]):
            cp.wait_send()
            cp.wait_recv()

    return pl.pallas_call(
        body, name=name, in_specs=[IN_HBM] * n + [SEM, SEM] + [ANY] * len(after),
        out_shape=tuple(pltpu.HBM(a.shape, a.dtype) for a in arrays), out_specs=tuple([IN_HBM] * n),
        input_output_aliases={i: i for i in range(n)}, compiler_params=SPLIT_PARAMS,
    )(*arrays, send_sems, recv_sems, *after)


def _gather_landing(shard, me):
    return lax.dynamic_update_slice(lax.empty((N_DEV,) + shard.shape, shard.dtype), shard[None],
                                    (me,) + (0,) * shard.ndim)


def _tile_2d(rows, cols):
    for t in (256, 176, 128):
        if rows % t == 0:
            return t, cols
    return rows, 256


def _pair_sum(part, recv, core, name):
    _, rows, cols = recv.shape
    tr, tc = rows, cols

    def body(c_ref, p_ref, r_ref, o_ref):
        del c_ref
        o_ref[...] = (p_ref[...].astype(F32) + r_ref[...].astype(F32)).astype(BF)

    grid_spec = pltpu.PrefetchScalarGridSpec(
        num_scalar_prefetch=1, grid=(4, rows // tr, cols // tc),
        in_specs=[pl.BlockSpec((None, None, tr, tc), lambda j, i, k, c_ref: (j, c_ref[0], i, k)),
                  pl.BlockSpec((None, tr, tc), lambda j, i, k, c_ref: (j, i, k))],
        out_specs=pl.BlockSpec((None, tr, tc), lambda j, i, k, c_ref: (j, i, k)))
    return pl.pallas_call(
        body, name=name, grid_spec=grid_spec, out_shape=jax.ShapeDtypeStruct(recv.shape, BF),
        compiler_params=_params("parallel", "parallel", "parallel"),
    )(core, *map(_in_hbm, (part, recv)))


def _adamw(w, g, m, v):
    m = ADAM_B1 * m + (1.0 - ADAM_B1) * g
    v = ADAM_B2 * v + (1.0 - ADAM_B2) * (g * g)
    delta = -ADAM_LR * ((m / ADAM_C1) / (jnp.sqrt(v / ADAM_C2) + ADAM_EPS) + ADAM_WD * w)
    return delta, m, v


def _chip_sum_adamw(sums, recv, w, m, v, chip, name):
    rows, cols = w.shape
    tr, tc = _tile_2d(rows, cols)

    def body(chip_ref, s_ref, r_ref, w_ref, m_ref, v_ref, g_out, d_out, m_out, v_out):
        del chip_ref
        g = s_ref[...].astype(F32)
        for k in range(3):
            g = g + r_ref[k].astype(F32)
        g_out[...] = g
        d_out[...], m_out[...], v_out[...] = _adamw(w_ref[...], g, m_ref[...], v_ref[...])

    tile = pl.BlockSpec((tr, tc), lambda i, k, chip_ref: (i, k))
    grid_spec = pltpu.PrefetchScalarGridSpec(
        num_scalar_prefetch=1, grid=(rows // tr, cols // tc),
        in_specs=[pl.BlockSpec((None, tr, tc), lambda i, k, chip_ref: (chip_ref[0], i, k)),
                  pl.BlockSpec((3, tr, tc), lambda i, k, chip_ref: (0, i, k)), tile, tile, tile],
        out_specs=[tile] * 4)
    return pl.pallas_call(
        body, name=name, grid_spec=grid_spec, out_shape=[jax.ShapeDtypeStruct((rows, cols), F32)] * 4,
        compiler_params=_params("parallel", "parallel"),
    )(chip, *map(_in_hbm, (sums, recv, w, m, v)))


def _small_sum_adamw(me, entries, loss_parts):
    def whole(shape, squeeze=0, pick=False):
        blk = (None,) * squeeze + tuple(shape[squeeze:])
        if pick:
            blk = (shape[0], None) + tuple(shape[2:])
            return pl.BlockSpec(blk, lambda i, me_ref: (0, me_ref[0]) + (0,) * (len(shape) - 2))
        return pl.BlockSpec(blk, lambda i, me_ref: (0,) * len(shape))

    in_specs, out_specs, out_shape, args = [], [], [], []
    for parts, w, m, v, sharded in entries:
        lead = w.ndim - (parts.ndim - (2 if sharded else 1))
        in_specs += [whole(parts.shape, pick=sharded)] + [whole(w.shape, squeeze=lead)] * 3
        out_specs += [whole(w.shape, squeeze=lead)] * 4
        out_shape += [jax.ShapeDtypeStruct(w.shape, F32)] * 4
        args += [parts, w, m, v]
    in_specs.append(whole(loss_parts.shape))
    out_specs.append(whole(loss_parts.shape[1:]))
    out_shape.append(jax.ShapeDtypeStruct(loss_parts.shape[1:], F32))
    n = len(entries)

    def added(p_ref):
        total = p_ref[0]
        for d in range(1, N_DEV):
            total = total + p_ref[d]
        return total

    def body(me_ref, *refs):
        del me_ref
        ins, outs = refs[:4 * n + 1], refs[4 * n + 1:]
        for e in range(n):
            p_ref, w_ref, m_ref, v_ref = ins[4 * e:4 * e + 4]
            g_out, d_out, m_out, v_out = outs[4 * e:4 * e + 4]
            g = added(p_ref)
            g_out[...] = g
            d_out[...], m_out[...], v_out[...] = _adamw(w_ref[...], g, m_ref[...], v_ref[...])
        outs[4 * n][...] = added(ins[4 * n])

    grid_spec = pltpu.PrefetchScalarGridSpec(num_scalar_prefetch=1, grid=(1,), in_specs=in_specs, out_specs=out_specs)
    outs = pl.pallas_call(body, name="small_sum_adamw", grid_spec=grid_spec, out_shape=out_shape,
                          compiler_params=_params("arbitrary"))(me, *map(_in_hbm, args + [loss_parts]))
    return [outs[4 * e:4 * e + 4] for e in range(n)], outs[4 * n]


MM_TILE = 512
N_MM_TILES = SEQ // MM_TILE
CAT_TILE = 512
N_CAT_TILES = N_CAT // CAT_TILE


def kernel(x, g_mix, w_in, b_gate, w_gk_up, b_gk, w_pool_grp, pool_scale, g_gla_head, w_pool_proj, w_gla_proj, w_out, g_ffn, w_up, w_conv, b_conv, w_down, g_final, loss_target, m_g_mix, m_w_in, m_b_gate, m_w_gk_up, m_b_gk, m_w_pool_grp, m_pool_scale, m_g_gla_head, m_w_pool_proj, m_w_gla_proj, m_w_out, m_g_ffn, m_w_up, m_w_conv, m_b_conv, m_w_down, m_g_final, v_g_mix, v_w_in, v_b_gate, v_w_gk_up, v_b_gk, v_w_pool_grp, v_pool_scale, v_g_gla_head, v_w_pool_proj, v_w_gla_proj, v_w_out, v_g_ffn, v_w_up, v_w_conv, v_b_conv, v_w_down, v_g_final):
    xi, yi, ci = lax.axis_index("x"), lax.axis_index("y"), lax.axis_index("c")
    me = 4 * xi + 2 * yi + ci
    core = jnp.reshape(ci, (1,)).astype(jnp.int32)
    chip = jnp.reshape(2 * xi + yi, (1,)).astype(jnp.int32)
    xs, target = x[0], loss_target[0]

    big = dict(w_in=w_in[0].T, w_pool_proj=w_pool_proj[0], w_gla_proj=w_gla_proj[0], w_out=w_out[0], w_up=w_up[0].T,
               w_down=w_down[0])
    moments = dict(w_in=(m_w_in[0].T, v_w_in[0].T), w_pool_proj=(m_w_pool_proj[0], v_w_pool_proj[0]),
                   w_gla_proj=(m_w_gla_proj[0], v_w_gla_proj[0]), w_out=(m_w_out[0], v_w_out[0]),
                   w_up=(m_w_up[0].T, v_w_up[0].T), w_down=(m_w_down[0], v_w_down[0]))
    names = list(big)
    shards = {k: big[k].astype(BF) for k in names}
    shards["w_gk_up"], shards["w_conv"] = w_gk_up[0], w_conv[0]
    gather_groups = (("w_in", "w_gk_up"), ("w_pool_proj", "w_gla_proj", "w_out"), ("w_up", "w_down", "w_conv"))
    started, token = _split_start("gather_start", [
        ([t for k in g for t in (shards[k], _gather_landing(shards[k], me))], 4 * len(g), _gather_first)
        for g in gather_groups])

    def gather_pass(gi, after):
        lands = list(_split_wait(f"gather_wait_{gi}", started[gi], _gather_first, after)[1::2])
        passed, tkn = _split_start(f"gather_pass_{gi}", [(lands, 3 * len(lands), _gather_second)])
        return passed[0], tkn

    def gather_done(gi, passed, after):
        return dict(zip(gather_groups[gi], _split_wait(f"gather_pass_wait_{gi}", passed, _gather_second, after)))

    tok = lambda i, j, k: (i, 0)
    whole = lambda i, j, k: (0, 0)
    kblk = lambda i, j, k: (k, 0)
    ff_seq = (None, None, SEQ, FF_BLK)

    h = _rms_fwd(xs, g_mix + token[:1, :1], "rms_mix")
    wg = gather_done(0, gather_pass(0, h)[0], h)
    wt_cat, wt_gk = _unshard_w_in(wg["w_in"])
    wgk_pad = jnp.pad(wg["w_gk_up"].transpose(1, 0, 2).reshape(GATE_RANK, GLA_DK), ((0, GK_PAD - GATE_RANK), (0, 0)))
    zcat = _mm(h, wt_cat, out_shape=(SEQ, N_CAT), out_dtype=F32, grid=(N_CAT_TILES, 1, 1),
               blk_a=(SEQ, D_MODEL), blk_b=(CAT_TILE, D_MODEL), blk_o=(SEQ, CAT_TILE),
               map_a=whole, map_b=lambda j, i, k: (j, 0), map_o=lambda j, i, k: (0, j), tb=True, name="mm_in")
    la = _gk_fwd(h, wt_gk, wgk_pad, b_gk)
    passed, tkn = gather_pass(1, la)
    o, states = _gla_fwd(zcat, la, tkn)
    wg = gather_done(1, passed, o)
    wpp = wg["w_pool_proj"].transpose(1, 0, 2).reshape(POOL_WIDTH, D_MODEL)
    wgp = wg["w_gla_proj"].reshape(D_MODEL, D_MODEL)
    wout = wg["w_out"].reshape(D_MODEL, D_MODEL)
    og = _post_gla_fwd(o, zcat, g_gla_head)
    ps = _pool_fwd(zcat, w_pool_grp[0], pool_scale)
    passed, tkn = gather_pass(2, (og, ps))
    y_pool, y_gla, mixed, x1, h2 = _mix_out_fwd(ps, og, zcat, xs, wpp, wgp, wout, b_gate, g_ffn, tkn)
    wg = gather_done(2, passed, h2)
    wt_up = wg["w_up"].reshape(2 * D_FF, D_MODEL)
    wdown = wg["w_down"].reshape(D_FF, D_MODEL)
    wconv4 = wg["w_conv"].reshape(2, 4, 3, FF_BLK)
    bconv4 = b_conv.reshape(2, 4, 1, FF_BLK)
    blk4 = lambda b, i, k: (b // 4, b % 4, 0, 0)
    u4, act = _up_conv_fwd(h2, wt_up, wconv4, bconv4)
    loss_part, dx2, dx2_bf, dg_final = _mm_tokens(
        act, wdown, blk_a=(None, 4, TOK_MM_TILE, FF_BLK), map_a=lambda i: (0, 0, i, 0),
        pieces=[(b, b * FF_BLK, FF_BLK) for b in range(4)], res=x1, then=("loss", g_final.reshape(1, D_MODEL), target),
        name="mm_down_loss")

    da = _mm(dx2_bf, wdown, out_shape=(1, 4, SEQ, FF_BLK), out_dtype=BF, grid=(4, 1, 1),
             blk_a=(SEQ, D_MODEL), blk_b=(FF_BLK, D_MODEL), blk_o=ff_seq,
             map_a=whole, map_b=lambda b, i, k: (b, 0), map_o=lambda b, i, k: (0, b, 0, 0), tb=True, name="mm_d_act")
    d_wdown = _mm(act, dx2_bf, out_shape=(D_FF, D_MODEL), out_dtype=BF, grid=(4, 1, 1),
                  blk_a=ff_seq, blk_b=(SEQ, D_MODEL), blk_o=(FF_BLK, D_MODEL),
                  map_a=lambda b, i, k: (0, b, 0, 0), map_b=whole, map_o=lambda b, i, k: (b, 0), ta=True,
                  name="mm_d_wdown")
    du4, d_wconv, d_bconv = _conv_bwd(u4, da, wconv4, bconv4)
    d_wt_up = _mm(du4, h2, out_shape=(2 * D_FF, D_MODEL), out_dtype=BF, grid=(N_DEV, 1, 1),
                  blk_a=ff_seq, blk_b=(SEQ, D_MODEL), blk_o=(FF_BLK, D_MODEL),
                  map_a=blk4, map_b=whole, map_o=lambda b, i, k: (b, 0), ta=True, name="mm_d_wup")
    res = {}

    def reduce_start(keys, parts):
        arrays = [t for k in keys for t in (parts[k], lax.empty((4,) + parts[k].shape[2:], BF))]
        st, tkn = _split_start("reduce_start_" + keys[0], [(arrays, 4 * len(keys), _reduce_first)])
        return st[0], tkn

    def reduce_cross(keys, st, after):
        arrays = _split_wait("reduce_wait_" + keys[0], st, _reduce_first, after)
        sums = [_pair_sum(p, r, core, "pair_sum_" + k) for k, p, r in zip(keys, arrays[0::2], arrays[1::2])]
        arrays = [t for s in sums for t in (s, lax.empty((3,) + s.shape[1:], BF))]
        st2, tkn = _split_start("reduce_cross_" + keys[0], [(arrays, 3 * len(keys), _reduce_second)])
        return st2[0], tkn

    def reduce_done(keys, st2, after):
        arrays = _split_wait("reduce_cross_wait_" + keys[0], st2, _reduce_second, after)
        for k, s, r in zip(keys, arrays[0::2], arrays[1::2]):
            outs = _chip_sum_adamw(s, r, big[k], moments[k][0], moments[k][1], chip, "adamw_" + k)
            res[k] = [(t.T if k in ("w_in", "w_up") else t)[None] for t in outs]

    ffn_keys = ("w_down", "w_up")
    ffn_red, tkn = reduce_start(ffn_keys, dict(w_down=d_wdown.reshape(4, 2, D_FF // N_DEV, D_MODEL),
                                               w_up=d_wt_up.reshape(4, 2, FF_BLK, D_MODEL)))
    dx1, dg_ffn = _mm_tokens(
        du4, wt_up, blk_a=(2, 4, TOK_MM_TILE, FF_BLK), map_a=lambda i: (0, 0, i, 0),
        pieces=[((b // 4, b % 4), b * FF_BLK, FF_BLK) for b in range(N_DEV)], after=tkn, then=("rms_bwd", x1, g_ffn, dx2),
        name="mm_d_h2_rms")

    sq_t = dict(out_shape=(D_MODEL, D_MODEL), grid=(1, 1, N_MM_TILES), blk_a=(MM_TILE, D_MODEL),
                blk_b=(MM_TILE, D_MODEL), blk_o=(D_MODEL, D_MODEL), map_a=kblk, map_b=kblk, map_o=whole, ta=True)
    d_wout = _mm(mixed, dx1, out_dtype=BF, name="mm_d_wout", **sq_t)
    dzcat, dy_pool, dy_gla, db_gate = _mix_bwd(dx1, wout, zcat, b_gate, y_pool, y_gla)
    ffn_red, tkn = reduce_cross(ffn_keys, ffn_red, db_gate)
    d_wgp = _mm(og, dy_gla, out_dtype=BF, after=tkn, name="mm_d_wgp", **sq_t)
    mix_keys = ("w_out", "w_gla_proj")
    mix_red, tkn = reduce_start(mix_keys, dict(w_out=d_wout.reshape(4, 2, D_MODEL // N_DEV, D_MODEL),
                                               w_gla_proj=d_wgp.reshape(4, 2, D_MODEL // N_DEV, D_MODEL)))
    dzcat, d_o, dg_head = _post_gla_bwd(dzcat, dy_gla, wgp, o, zcat, g_gla_head + tkn[:1, :1])
    dzcat, dla = _gla_bwd(dzcat, zcat, la, d_o, states)
    mix_red, tkn = reduce_cross(mix_keys, mix_red, dla)
    dh_gk, d_wt_gk, d_wgk, db_gk = _gk_bwd(dla, h, wt_gk, wgk_pad, b_gk + tkn[:1, :1])
    dps = _mm(dy_pool, wpp, out_shape=(SEQ, POOL_WIDTH), out_dtype=F32, grid=(N_MM_TILES, 1, 1),
              blk_a=(MM_TILE, D_MODEL), blk_b=(POOL_WIDTH, D_MODEL), blk_o=(MM_TILE, POOL_WIDTH),
              map_a=tok, map_b=whole, map_o=tok, tb=True, name="mm_d_ps")
    d_wpp = _mm(ps, dy_pool, out_shape=(POOL_WIDTH, D_MODEL), out_dtype=F32, grid=(1, 1, N_MM_TILES),
                blk_a=(MM_TILE, POOL_WIDTH), blk_b=(MM_TILE, D_MODEL), blk_o=(POOL_WIDTH, D_MODEL),
                map_a=kblk, map_b=kblk, map_o=whole, ta=True, name="mm_d_wpp")
    dzcat, d_wgrp, d_scale = _pool_bwd(dzcat, zcat, dps, w_pool_grp[0], pool_scale)
    row = lambda t: t.reshape(1, D_MODEL)
    conv_vec = lambda t: t.reshape(2, 4, 1, FF_BLK)
    small = [("b_gate", db_gate, b_gate, m_b_gate, v_b_gate, False),
             ("w_gk_up", d_wgk.reshape(GATE_RANK, N_DEV, GLA_DK // N_DEV).transpose(1, 0, 2), w_gk_up, m_w_gk_up,
              v_w_gk_up, True),
             ("b_gk", db_gk, b_gk, m_b_gk, v_b_gk, False),
             ("w_pool_grp", d_wgrp, w_pool_grp, m_w_pool_grp, v_w_pool_grp, False),
             ("pool_scale", d_scale, pool_scale, m_pool_scale, v_pool_scale, False),
             ("g_gla_head", dg_head, g_gla_head, m_g_gla_head, v_g_gla_head, False),
             ("g_ffn", dg_ffn, g_ffn, m_g_ffn, v_g_ffn, False),
             ("w_conv", d_wconv.reshape(N_DEV, 3, FF_BLK), w_conv, m_w_conv, v_w_conv, True),
             ("b_conv", d_bconv, conv_vec(b_conv), conv_vec(m_b_conv), conv_vec(v_b_conv), False),
             ("g_final", dg_final, row(g_final), row(m_g_final), row(v_g_final), False)]

    def small_start(parts, name):
        arrays = [t for p in parts for t in (p, _gather_landing(p, me))]
        st, tkn = _split_start(name, [(arrays, 7 * len(parts), _gather_direct)])
        return st[0], tkn

    small_sent, tkn = small_start([t[1] for t in small] + [loss_part], "small_start")
    d_wt_cat = _mm(dzcat, h, out_shape=(N_CAT, D_MODEL), out_dtype=BF, grid=(N_CAT_TILES, 1, 1),
                   blk_a=(SEQ, CAT_TILE), blk_b=(SEQ, D_MODEL), blk_o=(CAT_TILE, D_MODEL),
                   map_a=lambda j, i, k: (0, j), map_b=whole, map_o=lambda j, i, k: (j, 0), ta=True, after=tkn,
                   name="mm_d_wcat")
    in_keys = ("w_in", "w_pool_proj")
    in_red, tkn = reduce_start(in_keys, dict(
        w_in=_shard_d_w_in(d_wt_cat, d_wt_gk).reshape(4, 2, IN_SHARD, D_MODEL),
        w_pool_proj=d_wpp.reshape(POOL_WIDTH, N_DEV, D_MODEL // N_DEV).transpose(1, 0, 2).astype(BF)
        .reshape(4, 2, POOL_WIDTH, D_MODEL // N_DEV)))
    reduce_done(ffn_keys, ffn_red, tkn)
    reduce_done(mix_keys, mix_red, res["w_down"][0])
    in_red, tkn = reduce_cross(in_keys, in_red, res["w_out"][0])
    grad_x, dg_mix = _mm_tokens(dzcat, wt_cat, blk_a=(TOK_MM_TILE, N_CAT), map_a=lambda i: (i, 0),
                                pieces=[(None, 0, N_CAT)], res=dh_gk, after=tkn, then=("rms_bwd", xs, g_mix, dx1),
                                name="mm_d_h_rms")
    g_mix_sent, tkn = small_start([dg_mix], "g_mix_start")
    gathered = _split_wait("small_wait", small_sent, _gather_direct, (grad_x, tkn))[1::2]
    small.append(("g_mix", dg_mix, g_mix, m_g_mix, v_g_mix, False))
    gathered = list(gathered[:-1]) + [_split_wait("g_mix_wait", g_mix_sent, _gather_direct, gathered[0])[1], gathered[-1]]
    small_out, loss_sum = _small_sum_adamw(jnp.reshape(me, (1,)).astype(jnp.int32),
                                           [(p,) + t[2:] for p, t in zip(gathered, small)], gathered[-1])
    for t, outs in zip(small, small_out):
        res[t[0]] = list(outs)
    res["b_conv"] = [t.reshape(b_conv.shape) for t in res["b_conv"]]
    res["g_final"] = [t.reshape(g_final.shape) for t in res["g_final"]]

    reduce_done(in_keys, in_red, loss_sum)
    loss = loss_sum[0, 0]
    order =["g_mix", "w_in", "b_gate", "w_gk_up", "b_gk", "w_pool_grp", "pool_scale", "g_gla_head", "w_pool_proj",
             "w_gla_proj", "w_out", "g_ffn", "w_up", "w_conv", "b_conv", "w_down", "g_final"]
    return (loss, grad_x[None], *[res[k][0] for k in order], *[res[k][1] for k in order],
            *[res[k][2] for k in order], *[res[k][3] for k in order])
```

```python
import jax
import jax.numpy as jnp
from jax import lax
from jax.experimental import pallas as pl
from jax.experimental.pallas import tpu as pltpu

F32 = jnp.float32
BF = jnp.bfloat16
HIGHEST = lax.Precision.HIGHEST
MESH = pl.DeviceIdType.MESH

N_DEV = 8
SEQ = 2048
D_MODEL = 1024
CHUNK = 64
EPS = 1e-6
POOL_WIDTH = 512
POOL_WINDOWS = (2, 4, 8, 16)
POOL_GD = 128
POOL_HALO = 16
HEADS = 4
HK = 128
HV = 256
GLA_DK = 512
GATE_RANK = 16
GATE_NORM = 16.0
D_FF = 2816
FF_BLK = 704
IN_SHARD = 706
C_QKV, C_GATE, C_OG, C_POOL = 0, 2048, 4096, 5120
N_CAT = 5632
R_POOL, R_QKV, R_OG, R_GK, R_GATE = 0, 512, 2560, 3584, 3600
GK_PAD = 128

ADAM_LR, ADAM_B1, ADAM_B2, ADAM_EPS, ADAM_WD, ADAM_STEP = 0.001, 0.9, 0.999, 1e-08, 0.01, 10
ADAM_C1 = 1.0 - ADAM_B1 ** ADAM_STEP
ADAM_C2 = 1.0 - ADAM_B2 ** ADAM_STEP

VMEM_BYTES_V7X = 64 * 1024 * 1024
VMEM_LIMIT = VMEM_BYTES_V7X * 3 // 4

TOK_TILE = 256
HALO = 8
GLA_CPS = 4


def _params(*sem):
    return pltpu.CompilerParams(dimension_semantics=sem, vmem_limit_bytes=VMEM_LIMIT)


def _const_spec(shape):
    nd = len(shape)
    return pl.BlockSpec(shape, lambda *_: (0,) * nd)


def _in_hbm(t):
    return pltpu.with_memory_space_constraint(t, pltpu.HBM)


def _dot(a, b, ta=False, tb=False):
    dims = (((0 if ta else 1,), (1 if tb else 0,)), ((), ()))
    return lax.dot_general(a.astype(BF), b.astype(BF), dims, preferred_element_type=F32)


def _dot_exact(a, b):
    return jnp.dot(a, b, precision=HIGHEST, preferred_element_type=F32)


def _sigmoid(x):
    return 0.5 * jnp.tanh(0.5 * x) + 0.5


def _mm(a, b, *, out_shape, out_dtype, grid, blk_a, blk_b, blk_o, map_a, map_b, map_o, ta=False, tb=False,
        after=None, name):
    gk = grid[2]
    n_in = 2 + (after is not None)

    def body(*refs):
        a_ref, b_ref, o_ref = refs[0], refs[1], refs[n_in]
        prod = _dot(a_ref[...], b_ref[...], ta, tb)
        if gk == 1:
            o_ref[...] = prod.astype(out_dtype)
        else:
            acc = refs[n_in + 1]
            k = pl.program_id(2)

            @pl.when(k == 0)
            def _():
                acc[...] = prod

            @pl.when(k > 0)
            def _():
                acc[...] += prod

            @pl.when(k == gk - 1)
            def _():
                o_ref[...] = acc[...].astype(out_dtype)

    in_specs = [pl.BlockSpec(blk_a, map_a), pl.BlockSpec(blk_b, map_b)]
    args = [_in_hbm(a), _in_hbm(b)]
    if after is not None:
        in_specs.append(pl.BlockSpec(memory_space=pl.ANY))
        args.append(after)
    return pl.pallas_call(
        body, name=name, grid=grid, in_specs=in_specs, out_specs=pl.BlockSpec(blk_o, map_o),
        out_shape=jax.ShapeDtypeStruct(out_shape, out_dtype),
        scratch_shapes=[] if gk == 1 else [pltpu.VMEM(tuple(d for d in blk_o if d is not None), F32)],
        compiler_params=_params("parallel", "parallel", "arbitrary"),
    )(*args)


TOK_MM_TILE = 256


def _mm_tokens(a, w, *, blk_a, map_a, pieces, res=None, after=None, then=None, name):
    n_in = 2 + (res is not None) + (after is not None) + (0 if then is None else len(then) - 1)

    def accumulate(ref, part):
        @pl.when(pl.program_id(0) == 0)
        def _():
            ref[...] = part

        @pl.when(pl.program_id(0) > 0)
        def _():
            ref[...] += part

    def body(*refs):
        a_ref, w_ref = refs[:2]
        extra, outs = refs[n_in - (0 if then is None else len(then) - 1):n_in], refs[n_in:]
        total = None
        for idx, row, n in pieces:
            av = a_ref[...] if idx is None else a_ref[idx]
            prod = _dot(av, w_ref[row:row + n, :])
            total = prod if total is None else total + prod
        if res is not None:
            total = total + refs[2][...]
        if then is None:
            outs[0][...] = total
        elif then[0] == "rms_bwd":
            dx, part = _rms_bwd_tile(total, extra[0][...], extra[1][...], extra[2][...])
            outs[0][...] = dx
            accumulate(outs[1], part)
        else:
            lpart, dx, part = _loss_tile(total, extra[0][...], extra[1][...])
            outs[1][...] = dx
            outs[2][...] = dx.astype(BF)
            accumulate(outs[0], lpart)
            accumulate(outs[3], part)

    tile = pl.BlockSpec((TOK_MM_TILE, D_MODEL), lambda i: (i, 0))
    vec = _const_spec((1, D_MODEL))
    big = jax.ShapeDtypeStruct((SEQ, D_MODEL), F32)
    small = jax.ShapeDtypeStruct((1, D_MODEL), F32)
    in_specs = [pl.BlockSpec(blk_a, map_a), pl.BlockSpec(w.shape, lambda i: (0, 0), pipeline_mode=pl.Buffered(1))]
    args = [a, w]
    if res is not None:
        in_specs.append(tile)
        args.append(res)
    if after is not None:
        in_specs.append(pl.BlockSpec(memory_space=pl.ANY))
        args.append(after)
    if then is None:
        out_specs, out_shape = tile, big
    elif then[0] == "rms_bwd":
        in_specs += [tile, vec, tile]
        out_specs, out_shape = [tile, vec], [big, small]
    else:
        in_specs += [vec, tile]
        out_specs = [_const_spec((1, 128)), tile, tile, vec]
        out_shape = [jax.ShapeDtypeStruct((1, 128), F32), big, jax.ShapeDtypeStruct((SEQ, D_MODEL), BF), small]
    if then is not None:
        args += list(then[1:])
    return pl.pallas_call(
        body, name=name, grid=(SEQ // TOK_MM_TILE,), in_specs=in_specs, out_specs=out_specs, out_shape=out_shape,
        compiler_params=_params("parallel" if then is None else "arbitrary"),
    )(*[_in_hbm(t) for t in args])


def _rms_fwd(x, g, name):
    def body(x_ref, g_ref, o_ref):
        xv = x_ref[...]
        r = lax.rsqrt(jnp.mean(xv * xv, axis=-1, keepdims=True) + EPS)
        o_ref[...] = (xv * r * g_ref[...]).astype(BF)

    tile = pl.BlockSpec((TOK_TILE, D_MODEL), lambda i: (i, 0))
    return pl.pallas_call(
        body, name=name, grid=(SEQ // TOK_TILE,), in_specs=[tile, _const_spec((1, D_MODEL))], out_specs=tile,
        out_shape=jax.ShapeDtypeStruct((SEQ, D_MODEL), BF), compiler_params=_params("parallel"),
    )(*map(_in_hbm, (x, g)))


def _rms_bwd_tile(dyv, xv, gv, dresv):
    r = lax.rsqrt(jnp.mean(xv * xv, axis=-1, keepdims=True) + EPS)
    xn = xv * r
    dxn = dyv * gv
    return dresv + r * (dxn - xn * jnp.mean(dxn * xn, axis=-1, keepdims=True)), jnp.sum(dyv * xn, axis=0, keepdims=True)


def _loss_tile(xv, gv, tv):
    r = lax.rsqrt(jnp.mean(xv * xv, axis=-1, keepdims=True) + EPS)
    xn = xv * r
    err = xn * gv - tv
    lpart = jnp.full((1, 128), 0.5 * jnp.sum(jnp.mean(err * err, axis=-1, keepdims=True)), F32)
    dyv = err * (1.0 / D_MODEL)
    dxn = dyv * gv
    return lpart, r * (dxn - xn * jnp.mean(dxn * xn, axis=-1, keepdims=True)), jnp.sum(dyv * xn, axis=0, keepdims=True)


def _pool_counts(w):
    pos = lax.broadcasted_iota(jnp.int32, (SEQ, 1), 0).astype(F32)
    return jnp.minimum(pos + 1.0, float(w))


def _pool_window(u, w, ext):
    ext[pl.ds(POOL_HALO, SEQ), :] = u
    win = u
    for j in range(1, w):
        win = win + ext[pl.ds(POOL_HALO - j, SEQ), :]
    return win / _pool_counts(w) - u


def _pool_fwd(zcat, w_grp, scale):
    def body(z_ref, w_ref, s_ref, o_ref, ext):
        ext[pl.ds(0, POOL_HALO), :] = jnp.zeros((POOL_HALO, POOL_GD), F32)
        for g, w in enumerate(POOL_WINDOWS):
            cols = slice(g * POOL_GD, (g + 1) * POOL_GD)
            p = _pool_window(z_ref[:, cols], w, ext)
            o_ref[:, cols] = (_dot(p, w_ref[g]) * s_ref[:, cols]).astype(BF)

    return pl.pallas_call(
        body, name="pool_fwd", grid=(1,),
        in_specs=[pl.BlockSpec((SEQ, POOL_WIDTH), lambda i: (0, C_POOL // POOL_WIDTH)),
                  _const_spec((4, POOL_GD, POOL_GD)), _const_spec((1, POOL_WIDTH))],
        out_specs=_const_spec((SEQ, POOL_WIDTH)), out_shape=jax.ShapeDtypeStruct((SEQ, POOL_WIDTH), BF),
        scratch_shapes=[pltpu.VMEM((POOL_HALO + SEQ, POOL_GD), F32)], compiler_params=_params("arbitrary"),
    )(*map(_in_hbm, (zcat, w_grp, scale)))


def _pool_bwd(dzcat, zcat, dps, w_grp, scale):
    def body(dz_in, z_ref, dps_ref, w_ref, s_ref, dz_ref, dw_ref, dsc_ref, ext, ext2):
        del dz_in
        ext[pl.ds(0, POOL_HALO), :] = jnp.zeros((POOL_HALO, POOL_GD), F32)
        ext2[pl.ds(SEQ, POOL_HALO), :] = jnp.zeros((POOL_HALO, POOL_GD), F32)
        for g, w in enumerate(POOL_WINDOWS):
            cols = slice(g * POOL_GD, (g + 1) * POOL_GD)
            p = _pool_window(z_ref[:, cols], w, ext)
            wg = w_ref[g]
            pg = _dot(p, wg)
            dpsv = dps_ref[:, cols]
            dsc_ref[:, cols] = jnp.sum(dpsv * pg, axis=0, keepdims=True)
            dpg = dpsv * s_ref[:, cols]
            dw_ref[g] = _dot(p, dpg, ta=True)
            dp = _dot(dpg, wg, tb=True)
            dpc = dp / _pool_counts(w)
            ext2[pl.ds(0, SEQ), :] = dpc
            du = dpc
            for j in range(1, w):
                du = du + ext2[pl.ds(j, SEQ), :]
            dz_ref[:, cols] = (du - dp).astype(BF)

    return pl.pallas_call(
        body, name="pool_bwd", grid=(1,),
        in_specs=[pl.BlockSpec(memory_space=pl.ANY),
                  pl.BlockSpec((SEQ, POOL_WIDTH), lambda i: (0, C_POOL // POOL_WIDTH)),
                  _const_spec((SEQ, POOL_WIDTH)), _const_spec((4, POOL_GD, POOL_GD)), _const_spec((1, POOL_WIDTH))],
        out_specs=[pl.BlockSpec((SEQ, POOL_WIDTH), lambda i: (0, C_POOL // POOL_WIDTH)),
                   _const_spec((4, POOL_GD, POOL_GD)), _const_spec((1, POOL_WIDTH))],
        out_shape=[jax.ShapeDtypeStruct((SEQ, N_CAT), BF), jax.ShapeDtypeStruct((4, POOL_GD, POOL_GD), F32),
                   jax.ShapeDtypeStruct((1, POOL_WIDTH), F32)],
        scratch_shapes=[pltpu.VMEM((POOL_HALO + SEQ, POOL_GD), F32), pltpu.VMEM((SEQ + POOL_HALO, POOL_GD), F32)],
        input_output_aliases={0: 0}, compiler_params=_params("arbitrary"),
    )(*map(_in_hbm, (dzcat, zcat, dps, w_grp, scale)))


GK_TILE = 512


def _gk_fwd(h, wt_gk, wgk_pad, b_gk):
    def body(h_ref, wt_ref, w_ref, b_ref, la_ref):
        z_gk = _dot(h_ref[...], wt_ref[...], tb=True)
        pre = _dot(z_gk, w_ref[...]) + b_ref[...]
        la_ref[...] = (jnp.minimum(pre, 0.0) - jnp.log(1.0 + jnp.exp(-jnp.abs(pre)))) * (1.0 / GATE_NORM)

    return pl.pallas_call(
        body, name="gk_fwd", grid=(SEQ // GK_TILE,),
        in_specs=[pl.BlockSpec((GK_TILE, D_MODEL), lambda i: (i, 0)), _const_spec((GK_PAD, D_MODEL)),
                  _const_spec((GK_PAD, GLA_DK)), _const_spec((1, GLA_DK))],
        out_specs=pl.BlockSpec((GK_TILE, GLA_DK), lambda i: (i, 0)),
        out_shape=jax.ShapeDtypeStruct((SEQ, GLA_DK), F32), compiler_params=_params("parallel"),
    )(*map(_in_hbm, (h, wt_gk, wgk_pad, b_gk)))


def _gk_bwd(dla, h, wt_gk, wgk_pad, b_gk):
    def body(dla_ref, h_ref, wt_ref, w_ref, b_ref, dh_ref, dwt_ref, dw_ref, db_ref):
        hv = h_ref[...]
        wtv = wt_ref[...]
        wv = w_ref[...]
        z_gk = _dot(hv, wtv, tb=True)
        pre = _dot(z_gk, wv) + b_ref[...]
        dpre = dla_ref[...] * (1.0 / GATE_NORM) * (1.0 - _sigmoid(pre))
        dz_gk = _dot(dpre, wv, tb=True)
        dh_ref[...] = _dot(dz_gk, wtv)
        dwtp = _dot(dz_gk, hv, ta=True)
        dwp = _dot(z_gk, dpre, ta=True)[:GATE_RANK]
        dbp = jnp.sum(dpre, axis=0, keepdims=True)

        @pl.when(pl.program_id(0) == 0)
        def _():
            dwt_ref[...] = dwtp
            dw_ref[...] = dwp
            db_ref[...] = dbp

        @pl.when(pl.program_id(0) > 0)
        def _():
            dwt_ref[...] += dwtp
            dw_ref[...] += dwp
            db_ref[...] += dbp

    tile = pl.BlockSpec((GK_TILE, D_MODEL), lambda i: (i, 0))
    return pl.pallas_call(
        body, name="gk_bwd", grid=(SEQ // GK_TILE,),
        in_specs=[pl.BlockSpec((GK_TILE, GLA_DK), lambda i: (i, 0)), tile, _const_spec((GK_PAD, D_MODEL)),
                  _const_spec((GK_PAD, GLA_DK)), _const_spec((1, GLA_DK))],
        out_specs=[tile, _const_spec((GK_PAD, D_MODEL)), _const_spec((GATE_RANK, GLA_DK)), _const_spec((1, GLA_DK))],
        out_shape=[jax.ShapeDtypeStruct((SEQ, D_MODEL), F32), jax.ShapeDtypeStruct((GK_PAD, D_MODEL), F32),
                   jax.ShapeDtypeStruct((GATE_RANK, GLA_DK), F32), jax.ShapeDtypeStruct((1, GLA_DK), F32)],
        compiler_params=_params("arbitrary"),
    )(*map(_in_hbm, (dla, h, wt_gk, wgk_pad, b_gk)))


GLA_ROWS = GLA_CPS * CHUNK
GLA_STEPS = SEQ // GLA_ROWS
QKV_W = 2048


def _tri():
    return lax.broadcasted_iota(jnp.int32, (CHUNK, CHUNK), 0) >= lax.broadcasted_iota(jnp.int32, (CHUNK, CHUNK), 1)


def _chunk_cumsum(la_ref, rows):
    return _dot_exact(_tri().astype(F32), la_ref[rows, :])


def _gla_chunk(qkv_ref, la_ref, rows, h, bc_all):
    tri = _tri()
    q = qkv_ref[rows, h * HK:(h + 1) * HK] * (HK ** -0.5)
    k = qkv_ref[rows, GLA_DK + h * HK:GLA_DK + (h + 1) * HK]
    v = qkv_ref[rows, 2 * GLA_DK + h * HV:2 * GLA_DK + (h + 1) * HV].astype(BF)
    la = la_ref[rows, h * HK:(h + 1) * HK]
    bc = bc_all[:, h * HK:(h + 1) * HK]
    e_pos, e_neg = jnp.exp(bc), jnp.exp(-bc)
    dl = jnp.exp(jnp.sum(la, axis=0, keepdims=True))
    q_fw, q_bw, k_fw, k_bw = q * e_pos, q * e_neg, k * e_neg, k * e_pos
    scores = jnp.where(tri, _dot(q_fw, k_fw, tb=True), _dot(q_bw, k_bw, tb=True))
    return tri, v, e_pos, e_neg, dl, q_fw, q_bw, k_fw, k_bw, scores


def _gla_fwd(zcat, la, after):
    def body(qkv_ref, la_ref, after_ref, o_ref, st_ref, state):
        del after_ref

        @pl.when(pl.program_id(0) == 0)
        def _():
            state[...] = jnp.zeros_like(state)

        for c in range(GLA_CPS):
            rows = slice(c * CHUNK, (c + 1) * CHUNK)
            bc_all = _chunk_cumsum(la_ref, rows)
            for h in range(HEADS):
                _, v, _, _, dl, q_fw, _, k_fw, _, scores = _gla_chunk(qkv_ref, la_ref, rows, h, bc_all)
                st = state[h]
                st_ref[c, h] = st
                o_ref[rows, h * HV:(h + 1) * HV] = _dot(scores, v) + _dot(q_fw, st, tb=True)
                state[h] = st * dl + _dot(v, k_fw * dl, ta=True)

    return pl.pallas_call(
        body, name="gla_fwd", grid=(GLA_STEPS,),
        in_specs=[pl.BlockSpec((GLA_ROWS, QKV_W), lambda i: (i, 0)), pl.BlockSpec((GLA_ROWS, GLA_DK), lambda i: (i, 0)),
                  pl.BlockSpec(memory_space=pl.ANY)],
        out_specs=[pl.BlockSpec((GLA_ROWS, D_MODEL), lambda i: (i, 0)),
                   pl.BlockSpec((GLA_CPS, HEADS, HV, HK), lambda i: (i, 0, 0, 0))],
        out_shape=[jax.ShapeDtypeStruct((SEQ, D_MODEL), F32),
                   jax.ShapeDtypeStruct((SEQ // CHUNK, HEADS, HV, HK), F32)],
        scratch_shapes=[pltpu.VMEM((HEADS, HV, HK), F32)], compiler_params=_params("arbitrary"),
    )(*map(_in_hbm, (zcat, la)), after)


def _gla_bwd(dzcat, zcat, la, d_o, states):
    def body(dz_in, qkv_ref, la_ref, do_ref, st_ref, dqkv_ref, dla_ref, dstate):
        del dz_in

        @pl.when(pl.program_id(0) == 0)
        def _():
            dstate[...] = jnp.zeros_like(dstate)

        last_row = lax.broadcasted_iota(jnp.int32, (CHUNK, HK), 0) == CHUNK - 1
        upper = (lax.broadcasted_iota(jnp.int32, (CHUNK, CHUNK), 0)
                 <= lax.broadcasted_iota(jnp.int32, (CHUNK, CHUNK), 1)).astype(F32)
        for c in reversed(range(GLA_CPS)):
            rows = slice(c * CHUNK, (c + 1) * CHUNK)
            bc_all = _chunk_cumsum(la_ref, rows)
            dbs = []
            for h in range(HEADS):
                tri, v, e_pos, e_neg, dl, q_fw, q_bw, k_fw, k_bw, scores = _gla_chunk(qkv_ref, la_ref, rows, h, bc_all)
                st = st_ref[c, h]
                dst = dstate[h]
                d_out = do_ref[rows, h * HV:(h + 1) * HV].astype(BF)
                k_dec = k_fw * dl
                dp = _dot(d_out, v, tb=True)
                dp_fw = jnp.where(tri, dp, 0.0)
                dp_bw = jnp.where(tri, 0.0, dp)
                dv = _dot(scores, d_out, ta=True) + _dot(k_dec, dst, tb=True)
                dk_dec = _dot(v, dst)
                dq_fw = _dot(dp_fw, k_fw) + _dot(d_out, st)
                dk_fw = _dot(dp_fw, q_fw, ta=True) + dk_dec * dl
                dq_bw = _dot(dp_bw, k_bw)
                dk_bw = _dot(dp_bw, q_bw, ta=True)
                ddl = jnp.sum(st * dst, axis=0, keepdims=True) + jnp.sum(k_fw * dk_dec, axis=0, keepdims=True)
                dstate[h] = dst * dl + _dot(d_out, q_fw, ta=True)
                dq = (dq_fw * e_pos + dq_bw * e_neg) * (HK ** -0.5)
                dk = dk_fw * e_neg + dk_bw * e_pos
                dbs.append(dq_fw * q_fw - dk_fw * k_fw - dq_bw * q_bw + dk_bw * k_bw + jnp.where(last_row, ddl * dl, 0.0))
                dqkv_ref[rows, h * HK:(h + 1) * HK] = dq.astype(BF)
                dqkv_ref[rows, GLA_DK + h * HK:GLA_DK + (h + 1) * HK] = dk.astype(BF)
                dqkv_ref[rows, 2 * GLA_DK + h * HV:2 * GLA_DK + (h + 1) * HV] = dv.astype(BF)
            dla_ref[rows, :] = _dot_exact(upper, jnp.concatenate(dbs, axis=1))

    rev = lambda i: (GLA_STEPS - 1 - i, 0)
    return pl.pallas_call(
        body, name="gla_bwd", grid=(GLA_STEPS,),
        in_specs=[pl.BlockSpec(memory_space=pl.ANY), pl.BlockSpec((GLA_ROWS, QKV_W), rev),
                  pl.BlockSpec((GLA_ROWS, GLA_DK), rev), pl.BlockSpec((GLA_ROWS, D_MODEL), rev),
                  pl.BlockSpec((GLA_CPS, HEADS, HV, HK), lambda i: (GLA_STEPS - 1 - i, 0, 0, 0))],
        out_specs=[pl.BlockSpec((GLA_ROWS, QKV_W), rev), pl.BlockSpec((GLA_ROWS, GLA_DK), rev)],
        out_shape=[jax.ShapeDtypeStruct((SEQ, N_CAT), BF), jax.ShapeDtypeStruct((SEQ, GLA_DK), F32)],
        scratch_shapes=[pltpu.VMEM((HEADS, HV, HK), F32)], input_output_aliases={0: 0},
        compiler_params=_params("arbitrary"),
    )(*map(_in_hbm, (dzcat, zcat, la, d_o, states)))


def _silu_parts(x):
    s = _sigmoid(x)
    return x * s, s * (1.0 + x * (1.0 - s))


def _post_gla_fwd(o, zcat, g_head):
    def body(o_ref, zog_ref, g_ref, out_ref):
        for h in range(HEADS):
            cols = slice(h * HV, (h + 1) * HV)
            ov = o_ref[:, cols]
            r = lax.rsqrt(jnp.mean(ov * ov, axis=-1, keepdims=True) + EPS)
            act, _ = _silu_parts(zog_ref[:, cols])
            out_ref[:, cols] = (ov * r * g_ref[...] * act).astype(BF)

    tile = pl.BlockSpec((TOK_TILE, D_MODEL), lambda i: (i, 0))
    return pl.pallas_call(
        body, name="post_gla_fwd", grid=(SEQ // TOK_TILE,),
        in_specs=[tile, pl.BlockSpec((TOK_TILE, D_MODEL), lambda i: (i, C_OG // D_MODEL)), _const_spec((1, HV))],
        out_specs=tile, out_shape=jax.ShapeDtypeStruct((SEQ, D_MODEL), BF), compiler_params=_params("parallel"),
    )(*map(_in_hbm, (o, zcat, g_head)))


def _post_gla_bwd(dzcat, dy_gla, w_gla_proj, o, zcat, g_head):
    def body(dz_in, dyg_ref, w_ref, o_ref, zog_ref, g_ref, dz_ref, do_ref, dg_ref):
        del dz_in
        dog = _dot(dyg_ref[...], w_ref[...], tb=True)
        gpart = jnp.zeros((1, HV), F32)
        gv = g_ref[...]
        for h in range(HEADS):
            cols = slice(h * HV, (h + 1) * HV)
            ov = o_ref[:, cols]
            r = lax.rsqrt(jnp.mean(ov * ov, axis=-1, keepdims=True) + EPS)
            on = ov * r
            act, dact = _silu_parts(zog_ref[:, cols])
            dogv = dog[:, cols]
            dz_ref[:, cols] = (dogv * on * gv * dact).astype(BF)
            d_on_g = dogv * act
            gpart = gpart + jnp.sum(d_on_g * on, axis=0, keepdims=True)
            dxn = d_on_g * gv
            do_ref[:, cols] = r * (dxn - on * jnp.mean(dxn * on, axis=-1, keepdims=True))

        @pl.when(pl.program_id(0) == 0)
        def _():
            dg_ref[...] = gpart

        @pl.when(pl.program_id(0) > 0)
        def _():
            dg_ref[...] += gpart

    tile = pl.BlockSpec((TOK_TILE, D_MODEL), lambda i: (i, 0))
    ogspec = pl.BlockSpec((TOK_TILE, D_MODEL), lambda i: (i, C_OG // D_MODEL))
    return pl.pallas_call(
        body, name="post_gla_bwd", grid=(SEQ // TOK_TILE,),
        in_specs=[pl.BlockSpec(memory_space=pl.ANY), tile, _const_spec((D_MODEL, D_MODEL)), tile, ogspec,
                  _const_spec((1, HV))],
        out_specs=[ogspec, tile, _const_spec((1, HV))],
        out_shape=[jax.ShapeDtypeStruct((SEQ, N_CAT), BF), jax.ShapeDtypeStruct((SEQ, D_MODEL), F32),
                   jax.ShapeDtypeStruct((1, HV), F32)],
        input_output_aliases={0: 0}, compiler_params=_params("arbitrary"),
    )(*map(_in_hbm, (dzcat, dy_gla, w_gla_proj, o, zcat, g_head)))


GATE_W = 2 * D_MODEL


def _mix_out_fwd(ps, og, zcat, x, w_pool_proj, w_gla_proj, w_out, b_gate, g_ffn, after):
    def body(ps_ref, og_ref, zg_ref, x_ref, wpp_ref, wgp_ref, wout_ref, b_ref, g_ref, after_ref,
             yp_ref, yg_ref, mixed_ref, x1_ref, h2_ref):
        del after_ref
        y_pool = _dot(ps_ref[...], wpp_ref[...])
        y_gla = _dot(og_ref[...], wgp_ref[...])
        yp_ref[...] = y_pool
        yg_ref[...] = y_gla
        g0 = _sigmoid(zg_ref[:, :D_MODEL] + b_ref[:, :D_MODEL])
        g1 = _sigmoid(zg_ref[:, D_MODEL:] + b_ref[:, D_MODEL:])
        mixed = (g0 * y_pool + g1 * y_gla).astype(BF)
        mixed_ref[...] = mixed
        x1 = x_ref[...] + _dot(mixed, wout_ref[...])
        x1_ref[...] = x1
        r = lax.rsqrt(jnp.mean(x1 * x1, axis=-1, keepdims=True) + EPS)
        h2_ref[...] = (x1 * r * g_ref[...]).astype(BF)

    tile = pl.BlockSpec((TOK_TILE, D_MODEL), lambda i: (i, 0))
    resident = lambda shape: pl.BlockSpec(shape, lambda i: (0, 0), pipeline_mode=pl.Buffered(1))
    f32, bf16 = jax.ShapeDtypeStruct((SEQ, D_MODEL), F32), jax.ShapeDtypeStruct((SEQ, D_MODEL), BF)
    return pl.pallas_call(
        body, name="mix_out_fwd", grid=(SEQ // TOK_TILE,),
        in_specs=[pl.BlockSpec((TOK_TILE, POOL_WIDTH), lambda i: (i, 0)), tile,
                  pl.BlockSpec((TOK_TILE, GATE_W), lambda i: (i, C_GATE // GATE_W)), tile,
                  resident((POOL_WIDTH, D_MODEL)), resident((D_MODEL, D_MODEL)), resident((D_MODEL, D_MODEL)),
                  _const_spec((1, GATE_W)), _const_spec((1, D_MODEL)), pl.BlockSpec(memory_space=pl.ANY)],
        out_specs=[tile] * 5, out_shape=[f32, f32, bf16, f32, bf16], compiler_params=_params("parallel"),
    )(*map(_in_hbm, (ps, og, zcat, x, w_pool_proj, w_gla_proj, w_out, b_gate, g_ffn)), after)


def _mix_bwd(dx1, w_out, zcat, b_gate, y_pool, y_gla):
    def body(dx_ref, w_ref, zg_ref, b_ref, yp_ref, yg_ref, dz_ref, dyp_ref, dyg_ref, db_ref):
        dm = _dot(dx_ref[...], w_ref[...], tb=True)
        g0 = _sigmoid(zg_ref[:, :D_MODEL] + b_ref[:, :D_MODEL])
        g1 = _sigmoid(zg_ref[:, D_MODEL:] + b_ref[:, D_MODEL:])
        dyp_ref[...] = (dm * g0).astype(BF)
        dyg_ref[...] = (dm * g1).astype(BF)
        dz0 = dm * yp_ref[...] * g0 * (1.0 - g0)
        dz1 = dm * yg_ref[...] * g1 * (1.0 - g1)
        dz_ref[:, :D_MODEL] = dz0.astype(BF)
        dz_ref[:, D_MODEL:] = dz1.astype(BF)
        b0 = jnp.sum(dz0, axis=0, keepdims=True)
        b1 = jnp.sum(dz1, axis=0, keepdims=True)

        @pl.when(pl.program_id(0) == 0)
        def _():
            db_ref[:, :D_MODEL] = b0
            db_ref[:, D_MODEL:] = b1

        @pl.when(pl.program_id(0) > 0)
        def _():
            db_ref[:, :D_MODEL] += b0
            db_ref[:, D_MODEL:] += b1

    tile = pl.BlockSpec((TOK_TILE, D_MODEL), lambda i: (i, 0))
    gspec = pl.BlockSpec((TOK_TILE, GATE_W), lambda i: (i, C_GATE // GATE_W))
    return pl.pallas_call(
        body, name="mix_bwd", grid=(SEQ // TOK_TILE,),
        in_specs=[tile, _const_spec((D_MODEL, D_MODEL)), gspec, _const_spec((1, GATE_W)), tile, tile],
        out_specs=[gspec, tile, tile, _const_spec((1, GATE_W))],
        out_shape=[jax.ShapeDtypeStruct((SEQ, N_CAT), BF), jax.ShapeDtypeStruct((SEQ, D_MODEL), BF),
                   jax.ShapeDtypeStruct((SEQ, D_MODEL), BF), jax.ShapeDtypeStruct((1, GATE_W), F32)],
        compiler_params=_params("arbitrary"),
    )(*map(_in_hbm, (dx1, w_out, zcat, b_gate, y_pool, y_gla)))


N_TOK_TILES = SEQ // TOK_TILE
HALO_PER_TILE = TOK_TILE // HALO


LANE_TILES = tuple((lo, min(128, FF_BLK - lo)) for lo in range(0, FF_BLK, 128))


def _taps(w_ref, b_ref, half, lanes, rows):
    shape = (rows, lanes.stop - lanes.start)
    return ([jnp.broadcast_to(w_ref[half, j:j + 1, lanes], shape) for j in range(3)],
            jnp.broadcast_to(b_ref[half, :, lanes], shape))


def _conv_strips(u_ref, ub_ref, ua_ref, taps, lanes, width, n_strips, first):
    row = lax.broadcasted_iota(jnp.int32, (HALO, width), 0)
    prev = [[pltpu.roll(jnp.where(first, 0.0, ub_ref[half, :, lanes]), k, 0) for k in (1, 2)] for half in range(2)]
    for s in range(n_strips + (ua_ref is not None)):
        u3, conv = [], []
        for half in range(2):
            cur = u_ref[half, s * HALO:(s + 1) * HALO, lanes] if s < n_strips else ua_ref[half, :, lanes]
            rolled = [pltpu.roll(cur, k, 0) for k in (1, 2)]
            frames = [jnp.where(row >= 2, rolled[1], prev[half][1]), jnp.where(row >= 1, rolled[0], prev[half][0]), cur]
            prev[half] = rolled
            w3, bias = taps[half]
            u3.append(frames)
            conv.append(bias + frames[0] * w3[0] + frames[1] * w3[1] + frames[2] * w3[2])
        yield s, u3, conv


def _pair_specs(pairs):
    tile = pl.BlockSpec((pairs, None, TOK_TILE, FF_BLK), lambda b, i: (0, b, i, 0))
    before = pl.BlockSpec((pairs, None, HALO, FF_BLK), lambda b, i: (0, b, jnp.maximum(i * HALO_PER_TILE - 1, 0), 0))
    after = pl.BlockSpec((pairs, None, HALO, FF_BLK),
                         lambda b, i: (0, b, jnp.minimum((i + 1) * HALO_PER_TILE, SEQ // HALO - 1), 0))

    def vec(rows):
        return pl.BlockSpec((2, None, rows, FF_BLK), lambda b, i: (0, b, 0, 0))

    return tile, before, after, vec


N_STRIPS = TOK_TILE // HALO


def _up_conv_fwd(h2, wt_up, w_conv, b_conv):
    steps = N_TOK_TILES // 2

    def body(h_ref, h_next, wg_ref, wv_ref, w_ref, b_ref, u_ref, a_ref, buf_a, buf_b, carry):
        j = pl.program_id(1)

        def project(hv, buf):
            buf[0] = _dot(hv, wg_ref[...], tb=True)
            buf[1] = _dot(hv, wv_ref[...], tb=True)

        def conv(buf, row0):
            u_ref[:, row0:row0 + TOK_TILE, :] = buf[...]
            for lo, width in LANE_TILES:
                lanes = slice(lo, lo + width)
                taps = [_taps(w_ref, b_ref, half, lanes, HALO) for half in range(2)]
                pending = None
                for s, _, (cg, cv) in _conv_strips(buf, carry, None, taps, lanes, width, N_STRIPS, False):
                    act = cg * _sigmoid(cg) * cv
                    if s % 2 == 0:
                        pending = act
                    else:
                        a_ref[0, row0 + (s - 1) * HALO:row0 + (s + 1) * HALO, lanes] = (
                            jnp.concatenate([pending, act], axis=0).astype(BF))
            carry[...] = buf[:, TOK_TILE - HALO:, :]

        @pl.when(j == 0)
        def _():
            project(h_ref[0:TOK_TILE, :], buf_a)
            carry[...] = jnp.zeros_like(carry)

        project(h_ref[TOK_TILE:, :], buf_b)
        conv(buf_a, 0)
        project(h_next[...], buf_a)
        conv(buf_b, TOK_TILE)

    w_blk = lambda half: pl.BlockSpec((FF_BLK, D_MODEL), lambda b, j: (b + 4 * half, 0))
    vec = lambda rows: pl.BlockSpec((2, None, rows, FF_BLK), lambda b, j: (0, b, 0, 0))
    u_buf = pltpu.VMEM((2, TOK_TILE, FF_BLK), F32)
    return pl.pallas_call(
        body, name="up_conv_fwd", grid=(4, steps),
        in_specs=[pl.BlockSpec((2 * TOK_TILE, D_MODEL), lambda b, j: (j, 0)),
                  pl.BlockSpec((TOK_TILE, D_MODEL), lambda b, j: (jnp.minimum(2 * j + 2, N_TOK_TILES - 1), 0)),
                  w_blk(0), w_blk(1), vec(3), vec(1)],
        out_specs=[pl.BlockSpec((2, None, 2 * TOK_TILE, FF_BLK), lambda b, j: (0, b, j, 0)),
                   pl.BlockSpec((1, None, 2 * TOK_TILE, FF_BLK), lambda b, j: (0, b, j, 0))],
        out_shape=[jax.ShapeDtypeStruct((2, 4, SEQ, FF_BLK), F32), jax.ShapeDtypeStruct((1, 4, SEQ, FF_BLK), BF)],
        scratch_shapes=[u_buf, u_buf, pltpu.VMEM((2, HALO, FF_BLK), F32)],
        compiler_params=_params("parallel", "arbitrary"),
    )(*map(_in_hbm, (h2, h2, wt_up, wt_up, w_conv, b_conv)))


def _conv_bwd(u, da, w_conv, b_conv):
    def body(u_ref, ub_ref, ua_ref, da_ref, daa_ref, w_ref, b_ref, du_ref, dw_ref, db_ref):
        i = pl.program_id(1)

        @pl.when(i == 0)
        def _():
            dw_ref[...] = jnp.zeros_like(dw_ref)
            db_ref[...] = jnp.zeros_like(db_ref)

        for lo, width in LANE_TILES:
            lanes = slice(lo, lo + width)
            row = lax.broadcasted_iota(jnp.int32, (HALO, width), 0)
            taps = [_taps(w_ref, b_ref, half, lanes, HALO) for half in range(2)]
            acc_w = [[jnp.zeros((HALO, width), F32) for _ in range(3)] for _ in range(2)]
            acc_b = [jnp.zeros((HALO, width), F32) for _ in range(2)]
            da_pair, pending = None, [None, None]
            dc_prev, up_prev = [None, None], [None, None]
            for s, u3, (cg, cv) in _conv_strips(u_ref, ub_ref, ua_ref, taps, lanes, width, N_STRIPS, i == 0):
                act, dact = _silu_parts(cg)
                if s == N_STRIPS:
                    da = jnp.where(i < N_TOK_TILES - 1, daa_ref[0, :, lanes].astype(F32), 0.0)
                elif s % 2 == 0:
                    da_pair = da_ref[0, s * HALO:(s + 2) * HALO, lanes].astype(F32)
                    da = da_pair[:HALO]
                else:
                    da = da_pair[HALO:]
                dc = (da * cv * dact, da * act)
                for half in range(2):
                    up = [pltpu.roll(dc[half], HALO - k, 0) for k in (1, 2)]
                    if s < N_STRIPS:
                        for j in range(3):
                            acc_w[half][j] = acc_w[half][j] + dc[half] * u3[half][j]
                        acc_b[half] = acc_b[half] + dc[half]
                    if s >= 1:
                        w3 = taps[half][0]
                        du = (dc_prev[half] * w3[2] + jnp.where(row < HALO - 1, up_prev[half][0], up[0]) * w3[1]
                              + jnp.where(row < HALO - 2, up_prev[half][1], up[1]) * w3[0])
                        if (s - 1) % 2 == 0:
                            pending[half] = du
                        else:
                            du_ref[half, (s - 2) * HALO:s * HALO, lanes] = jnp.concatenate([pending[half], du],
                                                                                           axis=0).astype(BF)
                    dc_prev[half], up_prev[half] = dc[half], up
            for half in range(2):
                for j in range(3):
                    dw_ref[half, j:j + 1, lanes] += jnp.sum(acc_w[half][j], axis=0, keepdims=True)
                db_ref[half, :, lanes] += jnp.sum(acc_b[half], axis=0, keepdims=True)

    tile, before, after, vec = _pair_specs(2)
    da_tile, _, da_after_spec, _ = _pair_specs(1)
    return pl.pallas_call(
        body, name="conv_bwd", grid=(4, N_TOK_TILES),
        in_specs=[tile, before, after, da_tile, da_after_spec, vec(3), vec(1)],
        out_specs=[tile, vec(3), vec(1)],
        out_shape=[jax.ShapeDtypeStruct((2, 4, SEQ, FF_BLK), BF), jax.ShapeDtypeStruct((2, 4, 3, FF_BLK), F32),
                   jax.ShapeDtypeStruct((2, 4, 1, FF_BLK), F32)],
        compiler_params=_params("parallel", "arbitrary"),
    )(*map(_in_hbm, (u, u, u, da, da, w_conv, b_conv)))


W_IN_SEGMENTS = ((R_POOL, POOL_WIDTH, "cat", C_POOL), (R_QKV, QKV_W, "cat", C_QKV), (R_OG, D_MODEL, "cat", C_OG),
                 (R_GK, GATE_RANK, "gk", 0), (R_GATE, GATE_W, "cat", C_GATE))


def _slab_pieces(d):
    lo, hi = d * IN_SHARD, (d + 1) * IN_SHARD
    pieces = []
    for start, n, dest, at in W_IN_SEGMENTS:
        a, b = max(lo, start), min(hi, start + n)
        if a < b:
            assert (a - lo) % 2 == 0 and (b - a) % 2 == 0 and (at + a - start) % 2 == 0
            pieces.append(((a - lo) // 2, (b - a) // 2, dest, (at + a - start) // 2))
    return pieces


def _unshard_w_in(slabs):
    def body(slab_ref, cat_ref, gk_ref):
        d = pl.program_id(0)
        src = slab_ref.bitcast(jnp.uint32)
        dst = dict(cat=cat_ref.bitcast(jnp.uint32), gk=gk_ref.bitcast(jnp.uint32))

        @pl.when(d == 0)
        def _():
            gk_ref[...] = jnp.zeros_like(gk_ref)

        for dd in range(N_DEV):
            @pl.when(d == dd)
            def _():
                for a, n, dest, at in _slab_pieces(dd):
                    dst[dest][pl.ds(at, n), :] = src[0, pl.ds(a, n), :]

    return pl.pallas_call(
        body, name="unshard_w_in", grid=(N_DEV,),
        in_specs=[pl.BlockSpec((1, IN_SHARD, D_MODEL), lambda d: (d, 0, 0))],
        out_specs=[_const_spec((N_CAT, D_MODEL)), _const_spec((GK_PAD, D_MODEL))],
        out_shape=[jax.ShapeDtypeStruct((N_CAT, D_MODEL), BF), jax.ShapeDtypeStruct((GK_PAD, D_MODEL), BF)],
        compiler_params=_params("arbitrary"),
    )(_in_hbm(slabs))


def _shard_d_w_in(d_cat, d_gk):
    def body(cat_ref, gk_ref, slab_ref):
        d = pl.program_id(0)
        cat = cat_ref.bitcast(jnp.uint32)
        gk = pltpu.bitcast(gk_ref[0:GATE_RANK, :].astype(BF), jnp.uint32)
        dst = slab_ref.bitcast(jnp.uint32)
        for dd in range(N_DEV):
            @pl.when(d == dd)
            def _():
                for a, n, source, at in _slab_pieces(dd):
                    dst[0, pl.ds(a, n), :] = gk[at:at + n] if source == "gk" else cat[pl.ds(at, n), :]

    return pl.pallas_call(
        body, name="shard_d_w_in", grid=(N_DEV,),
        in_specs=[_const_spec((N_CAT, D_MODEL)), _const_spec((GK_PAD, D_MODEL))],
        out_specs=pl.BlockSpec((1, IN_SHARD, D_MODEL), lambda d: (d, 0, 0)),
        out_shape=jax.ShapeDtypeStruct((N_DEV, IN_SHARD, D_MODEL), BF), compiler_params=_params("parallel"),
    )(_in_hbm(d_cat), _in_hbm(d_gk))


ANY = pl.BlockSpec(memory_space=pl.ANY)


def _place():
    x, y, c = lax.axis_index("x"), lax.axis_index("y"), lax.axis_index("c")
    other_chips = [(1 - x, y), (x, 1 - y), (1 - x, 1 - y)]
    return x, y, c, other_chips


SEM = pl.BlockSpec(memory_space=pltpu.SEMAPHORE)
IN_HBM = pl.BlockSpec(memory_space=pltpu.HBM)
SPLIT_PARAMS = pltpu.CompilerParams(has_side_effects=pltpu.SideEffectType.DATAFLOW_SIDE_EFFECTING)


def _gather_first(refs, send_sems, recv_sems):
    x, y, c, chips = _place()
    targets = [(x, y, 1 - c)] + [(px, py, c) for px, py in chips]
    return [pltpu.make_async_remote_copy(src_ref=refs[2 * a], dst_ref=refs[2 * a + 1].at[4 * x + 2 * y + c],
                                         send_sem=send_sems.at[4 * a + k], recv_sem=recv_sems.at[4 * a + k],
                                         device_id=to, device_id_type=MESH)
            for a in range(len(refs) // 2) for k, to in enumerate(targets)]


def _gather_direct(refs, send_sems, recv_sems):
    x, y, c, _ = _place()
    flips = [(dx, dy, dc) for dx in (0, 1) for dy in (0, 1) for dc in (0, 1) if dx + dy + dc]
    targets = [(1 - x if dx else x, 1 - y if dy else y, 1 - c if dc else c) for dx, dy, dc in flips]
    return [pltpu.make_async_remote_copy(src_ref=refs[2 * a], dst_ref=refs[2 * a + 1].at[4 * x + 2 * y + c],
                                         send_sem=send_sems.at[7 * a + k], recv_sem=recv_sems.at[7 * a + k],
                                         device_id=to, device_id_type=MESH)
            for a in range(len(refs) // 2) for k, to in enumerate(targets)]


def _gather_second(refs, send_sems, recv_sems):
    x, y, c, chips = _place()
    copies = []
    for a, land in enumerate(refs):
        for j, (px, py) in enumerate(chips):
            block = land.at[4 * px + 2 * py + c]
            copies.append(pltpu.make_async_remote_copy(src_ref=block, dst_ref=block, send_sem=send_sems.at[3 * a + j],
                                                       recv_sem=recv_sems.at[3 * a + j], device_id=(x, y, 1 - c),
                                                       device_id_type=MESH))
    return copies


def _reduce_first(refs, send_sems, recv_sems):
    x, y, c, _ = _place()
    return [pltpu.make_async_remote_copy(src_ref=refs[2 * a].at[j, 1 - c], dst_ref=refs[2 * a + 1].at[j],
                                         send_sem=send_sems.at[4 * a + j], recv_sem=recv_sems.at[4 * a + j],
                                         device_id=(x, y, 1 - c), device_id_type=MESH)
            for a in range(len(refs) // 2) for j in range(4)]


def _reduce_second(refs, send_sems, recv_sems):
    _, _, c, chips = _place()
    return [pltpu.make_async_remote_copy(src_ref=refs[2 * a].at[2 * px + py], dst_ref=refs[2 * a + 1].at[k],
                                         send_sem=send_sems.at[3 * a + k], recv_sem=recv_sems.at[3 * a + k],
                                         device_id=(px, py, c), device_id_type=MESH)
            for a in range(len(refs) // 2) for k, (px, py) in enumerate(chips)]


def _split_start(name, groups):
    arrays = [a for g in groups for a in g[0]]
    n = len(arrays)

    def body(*refs):
        sems = refs[n:n + 2 * len(groups)]
        at = 0
        for gi, (members, _, build) in enumerate(groups):
            for cp in build(refs[at:at + len(members)], sems[2 * gi], sems[2 * gi + 1]):
                cp.start()
            at += len(members)
        refs[-1][...] = jnp.zeros_like(refs[-1])

    sem_shapes = [pltpu.SemaphoreType.DMA((g[1],)) for g in groups for _ in range(2)]
    outs = pl.pallas_call(
        body, name=name, in_specs=[IN_HBM] * n,
        out_shape=(*sem_shapes, *[pltpu.HBM(a.shape, a.dtype) for a in arrays], jax.ShapeDtypeStruct((8, 128), F32)),
        out_specs=(*[SEM] * len(sem_shapes), *[IN_HBM] * n, pl.BlockSpec(memory_space=pltpu.VMEM)),
        input_output_aliases={i: len(sem_shapes) + i for i in range(n)}, compiler_params=SPLIT_PARAMS,
    )(*[pltpu.with_memory_space_constraint(a, pltpu.HBM) for a in arrays])
    per_group, at = [], len(sem_shapes)
    for gi, (members, _, _) in enumerate(groups):
        per_group.append((outs[2 * gi], outs[2 * gi + 1], list(outs[at:at + len(members)])))
        at += len(members)
    return per_group, outs[-1]


def _split_wait(name, started, build, after):
    send_sems, recv_sems, arrays = started
    n = len(arrays)
    after = after if isinstance(after, (tuple, list)) else (after,)

    def body(*refs):
        for cp in build(refs[:n], refs[n], refs[n + 1]):
            cp.wait_send()
            cp.wait_recv()

    return pl.pallas_call(
        body, name=name, in_specs=[IN_HBM] * n + [SEM, SEM] + [ANY] * len(after),
        out_shape=tuple(pltpu.HBM(a.shape, a.dtype) for a in arrays), out_specs=tuple([IN_HBM] * n),
        input_output_aliases={i: i for i in range(n)}, compiler_params=SPLIT_PARAMS,
    )(*arrays, send_sems, recv_sems, *after)


def _gather_landing(shard, me):
    return lax.dynamic_update_slice(lax.empty((N_DEV,) + shard.shape, shard.dtype), shard[None],
                                    (me,) + (0,) * shard.ndim)


def _tile_2d(rows, cols):
    for t in (256, 176, 128):
        if rows % t == 0:
            return t, cols
    return rows, 256


def _pair_sum(part, recv, core, name):
    _, rows, cols = recv.shape
    tr, tc = rows, cols

    def body(c_ref, p_ref, r_ref, o_ref):
        del c_ref
        o_ref[...] = (p_ref[...].astype(F32) + r_ref[...].astype(F32)).astype(BF)

    grid_spec = pltpu.PrefetchScalarGridSpec(
        num_scalar_prefetch=1, grid=(4, rows // tr, cols // tc),
        in_specs=[pl.BlockSpec((None, None, tr, tc), lambda j, i, k, c_ref: (j, c_ref[0], i, k)),
                  pl.BlockSpec((None, tr, tc), lambda j, i, k, c_ref: (j, i, k))],
        out_specs=pl.BlockSpec((None, tr, tc), lambda j, i, k, c_ref: (j, i, k)))
    return pl.pallas_call(
        body, name=name, grid_spec=grid_spec, out_shape=jax.ShapeDtypeStruct(recv.shape, BF),
        compiler_params=_params("parallel", "parallel", "parallel"),
    )(core, *map(_in_hbm, (part, recv)))


def _adamw(w, g, m, v):
    m = ADAM_B1 * m + (1.0 - ADAM_B1) * g
    v = ADAM_B2 * v + (1.0 - ADAM_B2) * (g * g)
    delta = -ADAM_LR * ((m / ADAM_C1) / (jnp.sqrt(v / ADAM_C2) + ADAM_EPS) + ADAM_WD * w)
    return delta, m, v


def _chip_sum_adamw(sums, recv, w, m, v, chip, name):
    rows, cols = w.shape
    tr, tc = _tile_2d(rows, cols)

    def body(chip_ref, s_ref, r_ref, w_ref, m_ref, v_ref, g_out, d_out, m_out, v_out):
        del chip_ref
        g = s_ref[...].astype(F32)
        for k in range(3):
            g = g + r_ref[k].astype(F32)
        g_out[...] = g
        d_out[...], m_out[...], v_out[...] = _adamw(w_ref[...], g, m_ref[...], v_ref[...])

    tile = pl.BlockSpec((tr, tc), lambda i, k, chip_ref: (i, k))
    grid_spec = pltpu.PrefetchScalarGridSpec(
        num_scalar_prefetch=1, grid=(rows // tr, cols // tc),
        in_specs=[pl.BlockSpec((None, tr, tc), lambda i, k, chip_ref: (chip_ref[0], i, k)),
                  pl.BlockSpec((3, tr, tc), lambda i, k, chip_ref: (0, i, k)), tile, tile, tile],
        out_specs=[tile] * 4)
    return pl.pallas_call(
        body, name=name, grid_spec=grid_spec, out_shape=[jax.ShapeDtypeStruct((rows, cols), F32)] * 4,
        compiler_params=_params("parallel", "parallel"),
    )(chip, *map(_in_hbm, (sums, recv, w, m, v)))


def _small_sum_adamw(me, entries, loss_parts):
    def whole(shape, squeeze=0, pick=False):
        blk = (None,) * squeeze + tuple(shape[squeeze:])
        if pick:
            blk = (shape[0], None) + tuple(shape[2:])
            return pl.BlockSpec(blk, lambda i, me_ref: (0, me_ref[0]) + (0,) * (len(shape) - 2))
        return pl.BlockSpec(blk, lambda i, me_ref: (0,) * len(shape))

    in_specs, out_specs, out_shape, args = [], [], [], []
    for parts, w, m, v, sharded in entries:
        lead = w.ndim - (parts.ndim - (2 if sharded else 1))
        in_specs += [whole(parts.shape, pick=sharded)] + [whole(w.shape, squeeze=lead)] * 3
        out_specs += [whole(w.shape, squeeze=lead)] * 4
        out_shape += [jax.ShapeDtypeStruct(w.shape, F32)] * 4
        args += [parts, w, m, v]
    in_specs.append(whole(loss_parts.shape))
    out_specs.append(whole(loss_parts.shape[1:]))
    out_shape.append(jax.ShapeDtypeStruct(loss_parts.shape[1:], F32))
    n = len(entries)

    def added(p_ref):
        total = p_ref[0]
        for d in range(1, N_DEV):
            total = total + p_ref[d]
        return total

    def body(me_ref, *refs):
        del me_ref
        ins, outs = refs[:4 * n + 1], refs[4 * n + 1:]
        for e in range(n):
            p_ref, w_ref, m_ref, v_ref = ins[4 * e:4 * e + 4]
            g_out, d_out, m_out, v_out = outs[4 * e:4 * e + 4]
            g = added(p_ref)
            g_out[...] = g
            d_out[...], m_out[...], v_out[...] = _adamw(w_ref[...], g, m_ref[...], v_ref[...])
        outs[4 * n][...] = added(ins[4 * n])

    grid_spec = pltpu.PrefetchScalarGridSpec(num_scalar_prefetch=1, grid=(1,), in_specs=in_specs, out_specs=out_specs)
    outs = pl.pallas_call(body, name="small_sum_adamw", grid_spec=grid_spec, out_shape=out_shape,
                          compiler_params=_params("arbitrary"))(me, *map(_in_hbm, args + [loss_parts]))
    return [outs[4 * e:4 * e + 4] for e in range(n)], outs[4 * n]


MM_TILE = 512
N_MM_TILES = SEQ // MM_TILE
CAT_TILE = 512
N_CAT_TILES = N_CAT // CAT_TILE


def kernel(x, g_mix, w_in, b_gate, w_gk_up, b_gk, w_pool_grp, pool_scale, g_gla_head, w_pool_proj, w_gla_proj, w_out, g_ffn, w_up, w_conv, b_conv, w_down, g_final, loss_target, m_g_mix, m_w_in, m_b_gate, m_w_gk_up, m_b_gk, m_w_pool_grp, m_pool_scale, m_g_gla_head, m_w_pool_proj, m_w_gla_proj, m_w_out, m_g_ffn, m_w_up, m_w_conv, m_b_conv, m_w_down, m_g_final, v_g_mix, v_w_in, v_b_gate, v_w_gk_up, v_b_gk, v_w_pool_grp, v_pool_scale, v_g_gla_head, v_w_pool_proj, v_w_gla_proj, v_w_out, v_g_ffn, v_w_up, v_w_conv, v_b_conv, v_w_down, v_g_final):
    xi, yi, ci = lax.axis_index("x"), lax.axis_index("y"), lax.axis_index("c")
    me = 4 * xi + 2 * yi + ci
    core = jnp.reshape(ci, (1,)).astype(jnp.int32)
    chip = jnp.reshape(2 * xi + yi, (1,)).astype(jnp.int32)
    xs, target = x[0], loss_target[0]

    big = dict(w_in=w_in[0].T, w_pool_proj=w_pool_proj[0], w_gla_proj=w_gla_proj[0], w_out=w_out[0], w_up=w_up[0].T,
               w_down=w_down[0])
    moments = dict(w_in=(m_w_in[0].T, v_w_in[0].T), w_pool_proj=(m_w_pool_proj[0], v_w_pool_proj[0]),
                   w_gla_proj=(m_w_gla_proj[0], v_w_gla_proj[0]), w_out=(m_w_out[0], v_w_out[0]),
                   w_up=(m_w_up[0].T, v_w_up[0].T), w_down=(m_w_down[0], v_w_down[0]))
    names = list(big)
    shards = {k: big[k].astype(BF) for k in names}
    shards["w_gk_up"], shards["w_conv"] = w_gk_up[0], w_conv[0]
    gather_groups = (("w_in", "w_gk_up"), ("w_pool_proj", "w_gla_proj", "w_out"), ("w_up", "w_down", "w_conv"))
    started, token = _split_start("gather_start", [
        ([t for k in g for t in (shards[k], _gather_landing(shards[k], me))], 4 * len(g), _gather_first)
        for g in gather_groups])

    def gather_pass(gi, after):
        lands = list(_split_wait(f"gather_wait_{gi}", started[gi], _gather_first, after)[1::2])
        passed, tkn = _split_start(f"gather_pass_{gi}", [(lands, 3 * len(lands), _gather_second)])
        return passed[0], tkn

    def gather_done(gi, passed, after):
        return dict(zip(gather_groups[gi], _split_wait(f"gather_pass_wait_{gi}", passed, _gather_second, after)))

    tok = lambda i, j, k: (i, 0)
    whole = lambda i, j, k: (0, 0)
    kblk = lambda i, j, k: (k, 0)
    ff_seq = (None, None, SEQ, FF_BLK)

    h = _rms_fwd(xs, g_mix + token[:1, :1], "rms_mix")
    wg = gather_done(0, gather_pass(0, h)[0], h)
    wt_cat, wt_gk = _unshard_w_in(wg["w_in"])
    wgk_pad = jnp.pad(wg["w_gk_up"].transpose(1, 0, 2).reshape(GATE_RANK, GLA_DK), ((0, GK_PAD - GATE_RANK), (0, 0)))
    zcat = _mm(h, wt_cat, out_shape=(SEQ, N_CAT), out_dtype=F32, grid=(N_CAT_TILES, 1, 1),
               blk_a=(SEQ, D_MODEL), blk_b=(CAT_TILE, D_MODEL), blk_o=(SEQ, CAT_TILE),
               map_a=whole, map_b=lambda j, i, k: (j, 0), map_o=lambda j, i, k: (0, j), tb=True, name="mm_in")
    la = _gk_fwd(h, wt_gk, wgk_pad, b_gk)
    passed, tkn = gather_pass(1, la)
    o, states = _gla_fwd(zcat, la, tkn)
    wg = gather_done(1, passed, o)
    wpp = wg["w_pool_proj"].transpose(1, 0, 2).reshape(POOL_WIDTH, D_MODEL)
    wgp = wg["w_gla_proj"].reshape(D_MODEL, D_MODEL)
    wout = wg["w_out"].reshape(D_MODEL, D_MODEL)
    og = _post_gla_fwd(o, zcat, g_gla_head)
    ps = _pool_fwd(zcat, w_pool_grp[0], pool_scale)
    passed, tkn = gather_pass(2, (og, ps))
    y_pool, y_gla, mixed, x1, h2 = _mix_out_fwd(ps, og, zcat, xs, wpp, wgp, wout, b_gate, g_ffn, tkn)
    wg = gather_done(2, passed, h2)
    wt_up = wg["w_up"].reshape(2 * D_FF, D_MODEL)
    wdown = wg["w_down"].reshape(D_FF, D_MODEL)
    wconv4 = wg["w_conv"].reshape(2, 4, 3, FF_BLK)
    bconv4 = b_conv.reshape(2, 4, 1, FF_BLK)
    blk4 = lambda b, i, k: (b // 4, b % 4, 0, 0)
    u4, act = _up_conv_fwd(h2, wt_up, wconv4, bconv4)
    loss_part, dx2, dx2_bf, dg_final = _mm_tokens(
        act, wdown, blk_a=(None, 4, TOK_MM_TILE, FF_BLK), map_a=lambda i: (0, 0, i, 0),
        pieces=[(b, b * FF_BLK, FF_BLK) for b in range(4)], res=x1, then=("loss", g_final.reshape(1, D_MODEL), target),
        name="mm_down_loss")

    da = _mm(dx2_bf, wdown, out_shape=(1, 4, SEQ, FF_BLK), out_dtype=BF, grid=(4, 1, 1),
             blk_a=(SEQ, D_MODEL), blk_b=(FF_BLK, D_MODEL), blk_o=ff_seq,
             map_a=whole, map_b=lambda b, i, k: (b, 0), map_o=lambda b, i, k: (0, b, 0, 0), tb=True, name="mm_d_act")
    d_wdown = _mm(act, dx2_bf, out_shape=(D_FF, D_MODEL), out_dtype=BF, grid=(4, 1, 1),
                  blk_a=ff_seq, blk_b=(SEQ, D_MODEL), blk_o=(FF_BLK, D_MODEL),
                  map_a=lambda b, i, k: (0, b, 0, 0), map_b=whole, map_o=lambda b, i, k: (b, 0), ta=True,
                  name="mm_d_wdown")
    du4, d_wconv, d_bconv = _conv_bwd(u4, da, wconv4, bconv4)
    d_wt_up = _mm(du4, h2, out_shape=(2 * D_FF, D_MODEL), out_dtype=BF, grid=(N_DEV, 1, 1),
                  blk_a=ff_seq, blk_b=(SEQ, D_MODEL), blk_o=(FF_BLK, D_MODEL),
                  map_a=blk4, map_b=whole, map_o=lambda b, i, k: (b, 0), ta=True, name="mm_d_wup")
    res = {}

    def reduce_start(keys, parts):
        arrays = [t for k in keys for t in (parts[k], lax.empty((4,) + parts[k].shape[2:], BF))]
        st, tkn = _split_start("reduce_start_" + keys[0], [(arrays, 4 * len(keys), _reduce_first)])
        return st[0], tkn

    def reduce_cross(keys, st, after):
        arrays = _split_wait("reduce_wait_" + keys[0], st, _reduce_first, after)
        sums = [_pair_sum(p, r, core, "pair_sum_" + k) for k, p, r in zip(keys, arrays[0::2], arrays[1::2])]
        arrays = [t for s in sums for t in (s, lax.empty((3,) + s.shape[1:], BF))]
        st2, tkn = _split_start("reduce_cross_" + keys[0], [(arrays, 3 * len(keys), _reduce_second)])
        return st2[0], tkn

    def reduce_done(keys, st2, after):
        arrays = _split_wait("reduce_cross_wait_" + keys[0], st2, _reduce_second, after)
        for k, s, r in zip(keys, arrays[0::2], arrays[1::2]):
            outs = _chip_sum_adamw(s, r, big[k], moments[k][0], moments[k][1], chip, "adamw_" + k)
            res[k] = [(t.T if k in ("w_in", "w_up") else t)[None] for t in outs]

    ffn_keys = ("w_down", "w_up")
    ffn_red, tkn = reduce_start(ffn_keys, dict(w_down=d_wdown.reshape(4, 2, D_FF // N_DEV, D_MODEL),
                                               w_up=d_wt_up.reshape(4, 2, FF_BLK, D_MODEL)))
    dx1, dg_ffn = _mm_tokens(
        du4, wt_up, blk_a=(2, 4, TOK_MM_TILE, FF_BLK), map_a=lambda i: (0, 0, i, 0),
        pieces=[((b // 4, b % 4), b * FF_BLK, FF_BLK) for b in range(N_DEV)], after=tkn, then=("rms_bwd", x1, g_ffn, dx2),
        name="mm_d_h2_rms")

    sq_t = dict(out_shape=(D_MODEL, D_MODEL), grid=(1, 1, N_MM_TILES), blk_a=(MM_TILE, D_MODEL),
                blk_b=(MM_TILE, D_MODEL), blk_o=(D_MODEL, D_MODEL), map_a=kblk, map_b=kblk, map_o=whole, ta=True)
    d_wout = _mm(mixed, dx1, out_dtype=BF, name="mm_d_wout", **sq_t)
    dzcat, dy_pool, dy_gla, db_gate = _mix_bwd(dx1, wout, zcat, b_gate, y_pool, y_gla)
    ffn_red, tkn = reduce_cross(ffn_keys, ffn_red, db_gate)
    d_wgp = _mm(og, dy_gla, out_dtype=BF, after=tkn, name="mm_d_wgp", **sq_t)
    mix_keys = ("w_out", "w_gla_proj")
    mix_red, tkn = reduce_start(mix_keys, dict(w_out=d_wout.reshape(4, 2, D_MODEL // N_DEV, D_MODEL),
                                               w_gla_proj=d_wgp.reshape(4, 2, D_MODEL // N_DEV, D_MODEL)))
    dzcat, d_o, dg_head = _post_gla_bwd(dzcat, dy_gla, wgp, o, zcat, g_gla_head + tkn[:1, :1])
    dzcat, dla = _gla_bwd(dzcat, zcat, la, d_o, states)
    mix_red, tkn = reduce_cross(mix_keys, mix_red, dla)
    dh_gk, d_wt_gk, d_wgk, db_gk = _gk_bwd(dla, h, wt_gk, wgk_pad, b_gk + tkn[:1, :1])
    dps = _mm(dy_pool, wpp, out_shape=(SEQ, POOL_WIDTH), out_dtype=F32, grid=(N_MM_TILES, 1, 1),
              blk_a=(MM_TILE, D_MODEL), blk_b=(POOL_WIDTH, D_MODEL), blk_o=(MM_TILE, POOL_WIDTH),
              map_a=tok, map_b=whole, map_o=tok, tb=True, name="mm_d_ps")
    d_wpp = _mm(ps, dy_pool, out_shape=(POOL_WIDTH, D_MODEL), out_dtype=F32, grid=(1, 1, N_MM_TILES),
                blk_a=(MM_TILE, POOL_WIDTH), blk_b=(MM_TILE, D_MODEL), blk_o=(POOL_WIDTH, D_MODEL),
                map_a=kblk, map_b=kblk, map_o=whole, ta=True, name="mm_d_wpp")
    dzcat, d_wgrp, d_scale = _pool_bwd(dzcat, zcat, dps, w_pool_grp[0], pool_scale)
    row = lambda t: t.reshape(1, D_MODEL)
    conv_vec = lambda t: t.reshape(2, 4, 1, FF_BLK)
    small = [("b_gate", db_gate, b_gate, m_b_gate, v_b_gate, False),
             ("w_gk_up", d_wgk.reshape(GATE_RANK, N_DEV, GLA_DK // N_DEV).transpose(1, 0, 2), w_gk_up, m_w_gk_up,
              v_w_gk_up, True),
             ("b_gk", db_gk, b_gk, m_b_gk, v_b_gk, False),
             ("w_pool_grp", d_wgrp, w_pool_grp, m_w_pool_grp, v_w_pool_grp, False),
             ("pool_scale", d_scale, pool_scale, m_pool_scale, v_pool_scale, False),
             ("g_gla_head", dg_head, g_gla_head, m_g_gla_head, v_g_gla_head, False),
             ("g_ffn", dg_ffn, g_ffn, m_g_ffn, v_g_ffn, False),
             ("w_conv", d_wconv.reshape(N_DEV, 3, FF_BLK), w_conv, m_w_conv, v_w_conv, True),
             ("b_conv", d_bconv, conv_vec(b_conv), conv_vec(m_b_conv), conv_vec(v_b_conv), False),
             ("g_final", dg_final, row(g_final), row(m_g_final), row(v_g_final), False)]

    def small_start(parts, name):
        arrays = [t for p in parts for t in (p, _gather_landing(p, me))]
        st, tkn = _split_start(name, [(arrays, 7 * len(parts), _gather_direct)])
        return st[0], tkn

    small_sent, tkn = small_start([t[1] for t in small] + [loss_part], "small_start")
    d_wt_cat = _mm(dzcat, h, out_shape=(N_CAT, D_MODEL), out_dtype=BF, grid=(N_CAT_TILES, 1, 1),
                   blk_a=(SEQ, CAT_TILE), blk_b=(SEQ, D_MODEL), blk_o=(CAT_TILE, D_MODEL),
                   map_a=lambda j, i, k: (0, j), map_b=whole, map_o=lambda j, i, k: (j, 0), ta=True, after=tkn,
                   name="mm_d_wcat")
    in_keys = ("w_in", "w_pool_proj")
    in_red, tkn = reduce_start(in_keys, dict(
        w_in=_shard_d_w_in(d_wt_cat, d_wt_gk).reshape(4, 2, IN_SHARD, D_MODEL),
        w_pool_proj=d_wpp.reshape(POOL_WIDTH, N_DEV, D_MODEL // N_DEV).transpose(1, 0, 2).astype(BF)
        .reshape(4, 2, POOL_WIDTH, D_MODEL // N_DEV)))
    reduce_done(mix_keys, mix_red, tkn)
    in_red, tkn = reduce_cross(in_keys, in_red, res["w_out"][0])
    grad_x, dg_mix = _mm_tokens(dzcat, wt_cat, blk_a=(TOK_MM_TILE, N_CAT), map_a=lambda i: (i, 0),
                                pieces=[(None, 0, N_CAT)], res=dh_gk, after=tkn, then=("rms_bwd", xs, g_mix, dx1),
                                name="mm_d_h_rms")
    g_mix_sent, tkn = small_start([dg_mix], "g_mix_start")
    reduce_done(ffn_keys, ffn_red, (grad_x, tkn))
    gathered = _split_wait("small_wait", small_sent, _gather_direct, res["w_down"][0])[1::2]
    small.append(("g_mix", dg_mix, g_mix, m_g_mix, v_g_mix, False))
    gathered = list(gathered[:-1]) + [_split_wait("g_mix_wait", g_mix_sent, _gather_direct, gathered[0])[1], gathered[-1]]
    small_out, loss_sum = _small_sum_adamw(jnp.reshape(me, (1,)).astype(jnp.int32),
                                           [(p,) + t[2:] for p, t in zip(gathered, small)], gathered[-1])
    for t, outs in zip(small, small_out):
        res[t[0]] = list(outs)
    res["b_conv"] = [t.reshape(b_conv.shape) for t in res["b_conv"]]
    res["g_final"] = [t.reshape(g_final.shape) for t in res["g_final"]]

    reduce_done(in_keys, in_red, loss_sum)
    loss = loss_sum[0, 0]
    order =["g_mix", "w_in", "b_gate", "w_gk_up", "b_gk", "w_pool_grp", "pool_scale", "g_gla_head", "w_pool_proj",
             "w_gla_proj", "w_out", "g_ffn", "w_up", "w_conv", "b_conv", "w_down", "g_final"]
    return (loss, grad_x[None], *[res[k][0] for k in order], *[res[k][1] for k in order],
            *[res[k][2] for k in order], *[res[k][3] for k in order])
```

```python
import jax
import jax.numpy as jnp
from jax import lax
from jax.experimental import pallas as pl
from jax.experimental.pallas import tpu as pltpu

F32 = jnp.float32
BF = jnp.bfloat16
HIGHEST = lax.Precision.HIGHEST
MESH = pl.DeviceIdType.MESH

N_DEV = 8
SEQ = 2048
D_MODEL = 1024
CHUNK = 64
EPS = 1e-6
POOL_WIDTH = 512
POOL_WINDOWS = (2, 4, 8, 16)
POOL_GD = 128
POOL_HALO = 16
HEADS = 4
HK = 128
HV = 256
GLA_DK = 512
GATE_RANK = 16
GATE_NORM = 16.0
D_FF = 2816
FF_BLK = 704
IN_SHARD = 706
C_QKV, C_GATE, C_OG, C_POOL = 0, 2048, 4096, 5120
N_CAT = 5632
R_POOL, R_QKV, R_OG, R_GK, R_GATE = 0, 512, 2560, 3584, 3600
GK_PAD = 128

ADAM_LR, ADAM_B1, ADAM_B2, ADAM_EPS, ADAM_WD, ADAM_STEP = 0.001, 0.9, 0.999, 1e-08, 0.01, 10
ADAM_C1 = 1.0 - ADAM_B1 ** ADAM_STEP
ADAM_C2 = 1.0 - ADAM_B2 ** ADAM_STEP

VMEM_BYTES_V7X = 64 * 1024 * 1024
VMEM_LIMIT = VMEM_BYTES_V7X * 3 // 4

TOK_TILE = 256
HALO = 8
GLA_CPS = 4


def _params(*sem):
    return pltpu.CompilerParams(dimension_semantics=sem, vmem_limit_bytes=VMEM_LIMIT)


def _const_spec(shape):
    nd = len(shape)
    return pl.BlockSpec(shape, lambda *_: (0,) * nd)


def _in_hbm(t):
    return pltpu.with_memory_space_constraint(t, pltpu.HBM)


def _dot(a, b, ta=False, tb=False):
    dims = (((0 if ta else 1,), (1 if tb else 0,)), ((), ()))
    return lax.dot_general(a.astype(BF), b.astype(BF), dims, preferred_element_type=F32)


def _dot_exact(a, b):
    return jnp.dot(a, b, precision=HIGHEST, preferred_element_type=F32)


def _sigmoid(x):
    return 0.5 * jnp.tanh(0.5 * x) + 0.5


def _mm(a, b, *, out_shape, out_dtype, grid, blk_a, blk_b, blk_o, map_a, map_b, map_o, ta=False, tb=False,
        after=None, name):
    gk = grid[2]
    n_in = 2 + (after is not None)

    def body(*refs):
        a_ref, b_ref, o_ref = refs[0], refs[1], refs[n_in]
        prod = _dot(a_ref[...], b_ref[...], ta, tb)
        if gk == 1:
            o_ref[...] = prod.astype(out_dtype)
        else:
            acc = refs[n_in + 1]
            k = pl.program_id(2)

            @pl.when(k == 0)
            def _():
                acc[...] = prod

            @pl.when(k > 0)
            def _():
                acc[...] += prod

            @pl.when(k == gk - 1)
            def _():
                o_ref[...] = acc[...].astype(out_dtype)

    in_specs = [pl.BlockSpec(blk_a, map_a), pl.BlockSpec(blk_b, map_b)]
    args = [_in_hbm(a), _in_hbm(b)]
    if after is not None:
        in_specs.append(pl.BlockSpec(memory_space=pl.ANY))
        args.append(after)
    return pl.pallas_call(
        body, name=name, grid=grid, in_specs=in_specs, out_specs=pl.BlockSpec(blk_o, map_o),
        out_shape=jax.ShapeDtypeStruct(out_shape, out_dtype),
        scratch_shapes=[] if gk == 1 else [pltpu.VMEM(tuple(d for d in blk_o if d is not None), F32)],
        compiler_params=_params("parallel", "parallel", "arbitrary"),
    )(*args)


TOK_MM_TILE = 256


def _mm_tokens(a, w, *, blk_a, map_a, pieces, res=None, after=None, then=None, name):
    n_in = 2 + (res is not None) + (after is not None) + (0 if then is None else len(then) - 1)

    def accumulate(ref, part):
        @pl.when(pl.program_id(0) == 0)
        def _():
            ref[...] = part

        @pl.when(pl.program_id(0) > 0)
        def _():
            ref[...] += part

    def body(*refs):
        a_ref, w_ref = refs[:2]
        extra, outs = refs[n_in - (0 if then is None else len(then) - 1):n_in], refs[n_in:]
        total = None
        for idx, row, n in pieces:
            av = a_ref[...] if idx is None else a_ref[idx]
            prod = _dot(av, w_ref[row:row + n, :])
            total = prod if total is None else total + prod
        if res is not None:
            total = total + refs[2][...]
        if then is None:
            outs[0][...] = total
        elif then[0] == "rms_bwd":
            dx, part = _rms_bwd_tile(total, extra[0][...], extra[1][...], extra[2][...])
            outs[0][...] = dx
            accumulate(outs[1], part)
        else:
            lpart, dx, part = _loss_tile(total, extra[0][...], extra[1][...])
            outs[1][...] = dx
            outs[2][...] = dx.astype(BF)
            accumulate(outs[0], lpart)
            accumulate(outs[3], part)

    tile = pl.BlockSpec((TOK_MM_TILE, D_MODEL), lambda i: (i, 0))
    vec = _const_spec((1, D_MODEL))
    big = jax.ShapeDtypeStruct((SEQ, D_MODEL), F32)
    small = jax.ShapeDtypeStruct((1, D_MODEL), F32)
    in_specs = [pl.BlockSpec(blk_a, map_a), pl.BlockSpec(w.shape, lambda i: (0, 0), pipeline_mode=pl.Buffered(1))]
    args = [a, w]
    if res is not None:
        in_specs.append(tile)
        args.append(res)
    if after is not None:
        in_specs.append(pl.BlockSpec(memory_space=pl.ANY))
        args.append(after)
    if then is None:
        out_specs, out_shape = tile, big
    elif then[0] == "rms_bwd":
        in_specs += [tile, vec, tile]
        out_specs, out_shape = [tile, vec], [big, small]
    else:
        in_specs += [vec, tile]
        out_specs = [_const_spec((1, 128)), tile, tile, vec]
        out_shape = [jax.ShapeDtypeStruct((1, 128), F32), big, jax.ShapeDtypeStruct((SEQ, D_MODEL), BF), small]
    if then is not None:
        args += list(then[1:])
    return pl.pallas_call(
        body, name=name, grid=(SEQ // TOK_MM_TILE,), in_specs=in_specs, out_specs=out_specs, out_shape=out_shape,
        compiler_params=_params("parallel" if then is None else "arbitrary"),
    )(*[_in_hbm(t) for t in args])


def _rms_fwd(x, g, name):
    def body(x_ref, g_ref, o_ref):
        xv = x_ref[...]
        r = lax.rsqrt(jnp.mean(xv * xv, axis=-1, keepdims=True) + EPS)
        o_ref[...] = (xv * r * g_ref[...]).astype(BF)

    tile = pl.BlockSpec((TOK_TILE, D_MODEL), lambda i: (i, 0))
    return pl.pallas_call(
        body, name=name, grid=(SEQ // TOK_TILE,), in_specs=[tile, _const_spec((1, D_MODEL))], out_specs=tile,
        out_shape=jax.ShapeDtypeStruct((SEQ, D_MODEL), BF), compiler_params=_params("parallel"),
    )(*map(_in_hbm, (x, g)))


def _rms_bwd_tile(dyv, xv, gv, dresv):
    r = lax.rsqrt(jnp.mean(xv * xv, axis=-1, keepdims=True) + EPS)
    xn = xv * r
    dxn = dyv * gv
    return dresv + r * (dxn - xn * jnp.mean(dxn * xn, axis=-1, keepdims=True)), jnp.sum(dyv * xn, axis=0, keepdims=True)


def _loss_tile(xv, gv, tv):
    r = lax.rsqrt(jnp.mean(xv * xv, axis=-1, keepdims=True) + EPS)
    xn = xv * r
    err = xn * gv - tv
    lpart = jnp.full((1, 128), 0.5 * jnp.sum(jnp.mean(err * err, axis=-1, keepdims=True)), F32)
    dyv = err * (1.0 / D_MODEL)
    dxn = dyv * gv
    return lpart, r * (dxn - xn * jnp.mean(dxn * xn, axis=-1, keepdims=True)), jnp.sum(dyv * xn, axis=0, keepdims=True)


def _pool_counts(w):
    pos = lax.broadcasted_iota(jnp.int32, (SEQ, 1), 0).astype(F32)
    return jnp.minimum(pos + 1.0, float(w))


def _pool_window(u, w, ext):
    ext[pl.ds(POOL_HALO, SEQ), :] = u
    win = u
    for j in range(1, w):
        win = win + ext[pl.ds(POOL_HALO - j, SEQ), :]
    return win / _pool_counts(w) - u


def _pool_fwd(zcat, w_grp, scale):
    def body(z_ref, w_ref, s_ref, o_ref, ext):
        ext[pl.ds(0, POOL_HALO), :] = jnp.zeros((POOL_HALO, POOL_GD), F32)
        for g, w in enumerate(POOL_WINDOWS):
            cols = slice(g * POOL_GD, (g + 1) * POOL_GD)
            p = _pool_window(z_ref[:, cols].astype(F32), w, ext)
            o_ref[:, cols] = (_dot(p, w_ref[g]) * s_ref[:, cols]).astype(BF)

    return pl.pallas_call(
        body, name="pool_fwd", grid=(1,),
        in_specs=[pl.BlockSpec((SEQ, POOL_WIDTH), lambda i: (0, C_POOL // POOL_WIDTH)),
                  _const_spec((4, POOL_GD, POOL_GD)), _const_spec((1, POOL_WIDTH))],
        out_specs=_const_spec((SEQ, POOL_WIDTH)), out_shape=jax.ShapeDtypeStruct((SEQ, POOL_WIDTH), BF),
        scratch_shapes=[pltpu.VMEM((POOL_HALO + SEQ, POOL_GD), F32)], compiler_params=_params("arbitrary"),
    )(*map(_in_hbm, (zcat, w_grp, scale)))


def _pool_bwd(dzcat, zcat, dps, w_grp, scale):
    def body(dz_in, z_ref, dps_ref, w_ref, s_ref, dz_ref, dw_ref, dsc_ref, ext, ext2):
        del dz_in
        ext[pl.ds(0, POOL_HALO), :] = jnp.zeros((POOL_HALO, POOL_GD), F32)
        ext2[pl.ds(SEQ, POOL_HALO), :] = jnp.zeros((POOL_HALO, POOL_GD), F32)
        for g, w in enumerate(POOL_WINDOWS):
            cols = slice(g * POOL_GD, (g + 1) * POOL_GD)
            p = _pool_window(z_ref[:, cols].astype(F32), w, ext)
            wg = w_ref[g]
            pg = _dot(p, wg)
            dpsv = dps_ref[:, cols]
            dsc_ref[:, cols] = jnp.sum(dpsv * pg, axis=0, keepdims=True)
            dpg = dpsv * s_ref[:, cols]
            dw_ref[g] = _dot(p, dpg, ta=True)
            dp = _dot(dpg, wg, tb=True)
            dpc = dp / _pool_counts(w)
            ext2[pl.ds(0, SEQ), :] = dpc
            du = dpc
            for j in range(1, w):
                du = du + ext2[pl.ds(j, SEQ), :]
            dz_ref[:, cols] = (du - dp).astype(BF)

    return pl.pallas_call(
        body, name="pool_bwd", grid=(1,),
        in_specs=[pl.BlockSpec(memory_space=pl.ANY),
                  pl.BlockSpec((SEQ, POOL_WIDTH), lambda i: (0, C_POOL // POOL_WIDTH)),
                  _const_spec((SEQ, POOL_WIDTH)), _const_spec((4, POOL_GD, POOL_GD)), _const_spec((1, POOL_WIDTH))],
        out_specs=[pl.BlockSpec((SEQ, POOL_WIDTH), lambda i: (0, C_POOL // POOL_WIDTH)),
                   _const_spec((4, POOL_GD, POOL_GD)), _const_spec((1, POOL_WIDTH))],
        out_shape=[jax.ShapeDtypeStruct((SEQ, N_CAT), BF), jax.ShapeDtypeStruct((4, POOL_GD, POOL_GD), F32),
                   jax.ShapeDtypeStruct((1, POOL_WIDTH), F32)],
        scratch_shapes=[pltpu.VMEM((POOL_HALO + SEQ, POOL_GD), F32), pltpu.VMEM((SEQ + POOL_HALO, POOL_GD), F32)],
        input_output_aliases={0: 0}, compiler_params=_params("arbitrary"),
    )(*map(_in_hbm, (dzcat, zcat, dps, w_grp, scale)))


GK_TILE = 512


def _gk_fwd(h, wt_gk, wgk_pad, b_gk):
    def body(h_ref, wt_ref, w_ref, b_ref, la_ref):
        z_gk = _dot(h_ref[...], wt_ref[...], tb=True)
        pre = _dot(z_gk, w_ref[...]) + b_ref[...]
        la_ref[...] = (jnp.minimum(pre, 0.0) - jnp.log(1.0 + jnp.exp(-jnp.abs(pre)))) * (1.0 / GATE_NORM)

    return pl.pallas_call(
        body, name="gk_fwd", grid=(SEQ // GK_TILE,),
        in_specs=[pl.BlockSpec((GK_TILE, D_MODEL), lambda i: (i, 0)), _const_spec((GK_PAD, D_MODEL)),
                  _const_spec((GK_PAD, GLA_DK)), _const_spec((1, GLA_DK))],
        out_specs=pl.BlockSpec((GK_TILE, GLA_DK), lambda i: (i, 0)),
        out_shape=jax.ShapeDtypeStruct((SEQ, GLA_DK), F32), compiler_params=_params("parallel"),
    )(*map(_in_hbm, (h, wt_gk, wgk_pad, b_gk)))


def _gk_bwd(dla, h, wt_gk, wgk_pad, b_gk):
    def body(dla_ref, h_ref, wt_ref, w_ref, b_ref, dh_ref, dwt_ref, dw_ref, db_ref):
        hv = h_ref[...]
        wtv = wt_ref[...]
        wv = w_ref[...]
        z_gk = _dot(hv, wtv, tb=True)
        pre = _dot(z_gk, wv) + b_ref[...]
        dpre = dla_ref[...] * (1.0 / GATE_NORM) * (1.0 - _sigmoid(pre))
        dz_gk = _dot(dpre, wv, tb=True)
        dh_ref[...] = _dot(dz_gk, wtv)
        dwtp = _dot(dz_gk, hv, ta=True)
        dwp = _dot(z_gk, dpre, ta=True)[:GATE_RANK]
        dbp = jnp.sum(dpre, axis=0, keepdims=True)

        @pl.when(pl.program_id(0) == 0)
        def _():
            dwt_ref[...] = dwtp
            dw_ref[...] = dwp
            db_ref[...] = dbp

        @pl.when(pl.program_id(0) > 0)
        def _():
            dwt_ref[...] += dwtp
            dw_ref[...] += dwp
            db_ref[...] += dbp

    tile = pl.BlockSpec((GK_TILE, D_MODEL), lambda i: (i, 0))
    return pl.pallas_call(
        body, name="gk_bwd", grid=(SEQ // GK_TILE,),
        in_specs=[pl.BlockSpec((GK_TILE, GLA_DK), lambda i: (i, 0)), tile, _const_spec((GK_PAD, D_MODEL)),
                  _const_spec((GK_PAD, GLA_DK)), _const_spec((1, GLA_DK))],
        out_specs=[tile, _const_spec((GK_PAD, D_MODEL)), _const_spec((GATE_RANK, GLA_DK)), _const_spec((1, GLA_DK))],
        out_shape=[jax.ShapeDtypeStruct((SEQ, D_MODEL), F32), jax.ShapeDtypeStruct((GK_PAD, D_MODEL), F32),
                   jax.ShapeDtypeStruct((GATE_RANK, GLA_DK), F32), jax.ShapeDtypeStruct((1, GLA_DK), F32)],
        compiler_params=_params("arbitrary"),
    )(*map(_in_hbm, (dla, h, wt_gk, wgk_pad, b_gk)))


GLA_ROWS = GLA_CPS * CHUNK
GLA_STEPS = SEQ // GLA_ROWS
QKV_W = 2048


def _tri():
    return lax.broadcasted_iota(jnp.int32, (CHUNK, CHUNK), 0) >= lax.broadcasted_iota(jnp.int32, (CHUNK, CHUNK), 1)


def _chunk_cumsum(la_ref, rows):
    return _dot_exact(_tri().astype(F32), la_ref[rows, :])


def _gla_chunk(qkv_ref, la_ref, rows, h, bc_all):
    tri = _tri()
    q = qkv_ref[rows, h * HK:(h + 1) * HK].astype(F32) * (HK ** -0.5)
    k = qkv_ref[rows, GLA_DK + h * HK:GLA_DK + (h + 1) * HK].astype(F32)
    v = qkv_ref[rows, 2 * GLA_DK + h * HV:2 * GLA_DK + (h + 1) * HV].astype(BF)
    la = la_ref[rows, h * HK:(h + 1) * HK]
    bc = bc_all[:, h * HK:(h + 1) * HK]
    e_pos, e_neg = jnp.exp(bc), jnp.exp(-bc)
    dl = jnp.exp(jnp.sum(la, axis=0, keepdims=True))
    q_fw, q_bw, k_fw, k_bw = q * e_pos, q * e_neg, k * e_neg, k * e_pos
    scores = jnp.where(tri, _dot(q_fw, k_fw, tb=True), _dot(q_bw, k_bw, tb=True))
    return tri, v, e_pos, e_neg, dl, q_fw, q_bw, k_fw, k_bw, scores


def _gla_fwd(zcat, la, after):
    def body(qkv_ref, la_ref, after_ref, o_ref, st_ref, state):
        del after_ref

        @pl.when(pl.program_id(0) == 0)
        def _():
            state[...] = jnp.zeros_like(state)

        for c in range(GLA_CPS):
            rows = slice(c * CHUNK, (c + 1) * CHUNK)
            bc_all = _chunk_cumsum(la_ref, rows)
            for h in range(HEADS):
                _, v, _, _, dl, q_fw, _, k_fw, _, scores = _gla_chunk(qkv_ref, la_ref, rows, h, bc_all)
                st = state[h]
                st_ref[c, h] = st
                o_ref[rows, h * HV:(h + 1) * HV] = _dot(scores, v) + _dot(q_fw, st, tb=True)
                state[h] = st * dl + _dot(v, k_fw * dl, ta=True)

    return pl.pallas_call(
        body, name="gla_fwd", grid=(GLA_STEPS,),
        in_specs=[pl.BlockSpec((GLA_ROWS, QKV_W), lambda i: (i, 0)), pl.BlockSpec((GLA_ROWS, GLA_DK), lambda i: (i, 0)),
                  pl.BlockSpec(memory_space=pl.ANY)],
        out_specs=[pl.BlockSpec((GLA_ROWS, D_MODEL), lambda i: (i, 0)),
                   pl.BlockSpec((GLA_CPS, HEADS, HV, HK), lambda i: (i, 0, 0, 0))],
        out_shape=[jax.ShapeDtypeStruct((SEQ, D_MODEL), F32),
                   jax.ShapeDtypeStruct((SEQ // CHUNK, HEADS, HV, HK), F32)],
        scratch_shapes=[pltpu.VMEM((HEADS, HV, HK), F32)], compiler_params=_params("arbitrary"),
    )(*map(_in_hbm, (zcat, la)), after)


def _gla_bwd(dzcat, zcat, la, d_o, states):
    def body(dz_in, qkv_ref, la_ref, do_ref, st_ref, dqkv_ref, dla_ref, dstate):
        del dz_in

        @pl.when(pl.program_id(0) == 0)
        def _():
            dstate[...] = jnp.zeros_like(dstate)

        last_row = lax.broadcasted_iota(jnp.int32, (CHUNK, HK), 0) == CHUNK - 1
        upper = (lax.broadcasted_iota(jnp.int32, (CHUNK, CHUNK), 0)
                 <= lax.broadcasted_iota(jnp.int32, (CHUNK, CHUNK), 1)).astype(F32)
        for c in reversed(range(GLA_CPS)):
            rows = slice(c * CHUNK, (c + 1) * CHUNK)
            bc_all = _chunk_cumsum(la_ref, rows)
            dbs = []
            for h in range(HEADS):
                tri, v, e_pos, e_neg, dl, q_fw, q_bw, k_fw, k_bw, scores = _gla_chunk(qkv_ref, la_ref, rows, h, bc_all)
                st = st_ref[c, h]
                dst = dstate[h]
                d_out = do_ref[rows, h * HV:(h + 1) * HV].astype(BF)
                k_dec = k_fw * dl
                dp = _dot(d_out, v, tb=True)
                dp_fw = jnp.where(tri, dp, 0.0)
                dp_bw = jnp.where(tri, 0.0, dp)
                dv = _dot(scores, d_out, ta=True) + _dot(k_dec, dst, tb=True)
                dk_dec = _dot(v, dst)
                dq_fw = _dot(dp_fw, k_fw) + _dot(d_out, st)
                dk_fw = _dot(dp_fw, q_fw, ta=True) + dk_dec * dl
                dq_bw = _dot(dp_bw, k_bw)
                dk_bw = _dot(dp_bw, q_bw, ta=True)
                ddl = jnp.sum(st * dst, axis=0, keepdims=True) + jnp.sum(k_fw * dk_dec, axis=0, keepdims=True)
                dstate[h] = dst * dl + _dot(d_out, q_fw, ta=True)
                dq = (dq_fw * e_pos + dq_bw * e_neg) * (HK ** -0.5)
                dk = dk_fw * e_neg + dk_bw * e_pos
                dbs.append(dq_fw * q_fw - dk_fw * k_fw - dq_bw * q_bw + dk_bw * k_bw + jnp.where(last_row, ddl * dl, 0.0))
                dqkv_ref[rows, h * HK:(h + 1) * HK] = dq.astype(BF)
                dqkv_ref[rows, GLA_DK + h * HK:GLA_DK + (h + 1) * HK] = dk.astype(BF)
                dqkv_ref[rows, 2 * GLA_DK + h * HV:2 * GLA_DK + (h + 1) * HV] = dv.astype(BF)
            dla_ref[rows, :] = _dot_exact(upper, jnp.concatenate(dbs, axis=1))

    rev = lambda i: (GLA_STEPS - 1 - i, 0)
    return pl.pallas_call(
        body, name="gla_bwd", grid=(GLA_STEPS,),
        in_specs=[pl.BlockSpec(memory_space=pl.ANY), pl.BlockSpec((GLA_ROWS, QKV_W), rev),
                  pl.BlockSpec((GLA_ROWS, GLA_DK), rev), pl.BlockSpec((GLA_ROWS, D_MODEL), rev),
                  pl.BlockSpec((GLA_CPS, HEADS, HV, HK), lambda i: (GLA_STEPS - 1 - i, 0, 0, 0))],
        out_specs=[pl.BlockSpec((GLA_ROWS, QKV_W), rev), pl.BlockSpec((GLA_ROWS, GLA_DK), rev)],
        out_shape=[jax.ShapeDtypeStruct((SEQ, N_CAT), BF), jax.ShapeDtypeStruct((SEQ, GLA_DK), F32)],
        scratch_shapes=[pltpu.VMEM((HEADS, HV, HK), F32)], input_output_aliases={0: 0},
        compiler_params=_params("arbitrary"),
    )(*map(_in_hbm, (dzcat, zcat, la, d_o, states)))


def _silu_parts(x):
    s = _sigmoid(x)
    return x * s, s * (1.0 + x * (1.0 - s))


def _post_gla_fwd(o, zcat, g_head):
    def body(o_ref, zog_ref, g_ref, out_ref):
        for h in range(HEADS):
            cols = slice(h * HV, (h + 1) * HV)
            ov = o_ref[:, cols]
            r = lax.rsqrt(jnp.mean(ov * ov, axis=-1, keepdims=True) + EPS)
            act, _ = _silu_parts(zog_ref[:, cols].astype(F32))
            out_ref[:, cols] = (ov * r * g_ref[...] * act).astype(BF)

    tile = pl.BlockSpec((TOK_TILE, D_MODEL), lambda i: (i, 0))
    return pl.pallas_call(
        body, name="post_gla_fwd", grid=(SEQ // TOK_TILE,),
        in_specs=[tile, pl.BlockSpec((TOK_TILE, D_MODEL), lambda i: (i, C_OG // D_MODEL)), _const_spec((1, HV))],
        out_specs=tile, out_shape=jax.ShapeDtypeStruct((SEQ, D_MODEL), BF), compiler_params=_params("parallel"),
    )(*map(_in_hbm, (o, zcat, g_head)))


def _post_gla_bwd(dzcat, dy_gla, w_gla_proj, o, zcat, g_head):
    def body(dz_in, dyg_ref, w_ref, o_ref, zog_ref, g_ref, dz_ref, do_ref, dg_ref):
        del dz_in
        dog = _dot(dyg_ref[...], w_ref[...], tb=True)
        gpart = jnp.zeros((1, HV), F32)
        gv = g_ref[...]
        for h in range(HEADS):
            cols = slice(h * HV, (h + 1) * HV)
            ov = o_ref[:, cols]
            r = lax.rsqrt(jnp.mean(ov * ov, axis=-1, keepdims=True) + EPS)
            on = ov * r
            act, dact = _silu_parts(zog_ref[:, cols].astype(F32))
            dogv = dog[:, cols]
            dz_ref[:, cols] = (dogv * on * gv * dact).astype(BF)
            d_on_g = dogv * act
            gpart = gpart + jnp.sum(d_on_g * on, axis=0, keepdims=True)
            dxn = d_on_g * gv
            do_ref[:, cols] = r * (dxn - on * jnp.mean(dxn * on, axis=-1, keepdims=True))

        @pl.when(pl.program_id(0) == 0)
        def _():
            dg_ref[...] = gpart

        @pl.when(pl.program_id(0) > 0)
        def _():
            dg_ref[...] += gpart

    tile = pl.BlockSpec((TOK_TILE, D_MODEL), lambda i: (i, 0))
    ogspec = pl.BlockSpec((TOK_TILE, D_MODEL), lambda i: (i, C_OG // D_MODEL))
    return pl.pallas_call(
        body, name="post_gla_bwd", grid=(SEQ // TOK_TILE,),
        in_specs=[pl.BlockSpec(memory_space=pl.ANY), tile, _const_spec((D_MODEL, D_MODEL)), tile, ogspec,
                  _const_spec((1, HV))],
        out_specs=[ogspec, tile, _const_spec((1, HV))],
        out_shape=[jax.ShapeDtypeStruct((SEQ, N_CAT), BF), jax.ShapeDtypeStruct((SEQ, D_MODEL), F32),
                   jax.ShapeDtypeStruct((1, HV), F32)],
        input_output_aliases={0: 0}, compiler_params=_params("arbitrary"),
    )(*map(_in_hbm, (dzcat, dy_gla, w_gla_proj, o, zcat, g_head)))


GATE_W = 2 * D_MODEL


def _mix_out_fwd(ps, og, zcat, x, w_pool_proj, w_gla_proj, w_out, b_gate, g_ffn, after):
    def body(ps_ref, og_ref, zg_ref, x_ref, wpp_ref, wgp_ref, wout_ref, b_ref, g_ref, after_ref,
             yp_ref, yg_ref, mixed_ref, x1_ref, h2_ref):
        del after_ref
        y_pool = _dot(ps_ref[...], wpp_ref[...])
        y_gla = _dot(og_ref[...], wgp_ref[...])
        yp_ref[...] = y_pool
        yg_ref[...] = y_gla
        g0 = _sigmoid(zg_ref[:, :D_MODEL].astype(F32) + b_ref[:, :D_MODEL])
        g1 = _sigmoid(zg_ref[:, D_MODEL:].astype(F32) + b_ref[:, D_MODEL:])
        mixed = (g0 * y_pool + g1 * y_gla).astype(BF)
        mixed_ref[...] = mixed
        x1 = x_ref[...] + _dot(mixed, wout_ref[...])
        x1_ref[...] = x1
        r = lax.rsqrt(jnp.mean(x1 * x1, axis=-1, keepdims=True) + EPS)
        h2_ref[...] = (x1 * r * g_ref[...]).astype(BF)

    tile = pl.BlockSpec((TOK_TILE, D_MODEL), lambda i: (i, 0))
    resident = lambda shape: pl.BlockSpec(shape, lambda i: (0, 0), pipeline_mode=pl.Buffered(1))
    f32, bf16 = jax.ShapeDtypeStruct((SEQ, D_MODEL), F32), jax.ShapeDtypeStruct((SEQ, D_MODEL), BF)
    return pl.pallas_call(
        body, name="mix_out_fwd", grid=(SEQ // TOK_TILE,),
        in_specs=[pl.BlockSpec((TOK_TILE, POOL_WIDTH), lambda i: (i, 0)), tile,
                  pl.BlockSpec((TOK_TILE, GATE_W), lambda i: (i, C_GATE // GATE_W)), tile,
                  resident((POOL_WIDTH, D_MODEL)), resident((D_MODEL, D_MODEL)), resident((D_MODEL, D_MODEL)),
                  _const_spec((1, GATE_W)), _const_spec((1, D_MODEL)), pl.BlockSpec(memory_space=pl.ANY)],
        out_specs=[tile] * 5, out_shape=[f32, f32, bf16, f32, bf16], compiler_params=_params("parallel"),
    )(*map(_in_hbm, (ps, og, zcat, x, w_pool_proj, w_gla_proj, w_out, b_gate, g_ffn)), after)


def _mix_bwd(dx1, w_out, zcat, b_gate, y_pool, y_gla):
    def body(dx_ref, w_ref, zg_ref, b_ref, yp_ref, yg_ref, dz_ref, dyp_ref, dyg_ref, db_ref):
        dm = _dot(dx_ref[...], w_ref[...], tb=True)
        g0 = _sigmoid(zg_ref[:, :D_MODEL].astype(F32) + b_ref[:, :D_MODEL])
        g1 = _sigmoid(zg_ref[:, D_MODEL:].astype(F32) + b_ref[:, D_MODEL:])
        dyp_ref[...] = (dm * g0).astype(BF)
        dyg_ref[...] = (dm * g1).astype(BF)
        dz0 = dm * yp_ref[...] * g0 * (1.0 - g0)
        dz1 = dm * yg_ref[...] * g1 * (1.0 - g1)
        dz_ref[:, :D_MODEL] = dz0.astype(BF)
        dz_ref[:, D_MODEL:] = dz1.astype(BF)
        b0 = jnp.sum(dz0, axis=0, keepdims=True)
        b1 = jnp.sum(dz1, axis=0, keepdims=True)

        @pl.when(pl.program_id(0) == 0)
        def _():
            db_ref[:, :D_MODEL] = b0
            db_ref[:, D_MODEL:] = b1

        @pl.when(pl.program_id(0) > 0)
        def _():
            db_ref[:, :D_MODEL] += b0
            db_ref[:, D_MODEL:] += b1

    tile = pl.BlockSpec((TOK_TILE, D_MODEL), lambda i: (i, 0))
    gspec = pl.BlockSpec((TOK_TILE, GATE_W), lambda i: (i, C_GATE // GATE_W))
    return pl.pallas_call(
        body, name="mix_bwd", grid=(SEQ // TOK_TILE,),
        in_specs=[tile, _const_spec((D_MODEL, D_MODEL)), gspec, _const_spec((1, GATE_W)), tile, tile],
        out_specs=[gspec, tile, tile, _const_spec((1, GATE_W))],
        out_shape=[jax.ShapeDtypeStruct((SEQ, N_CAT), BF), jax.ShapeDtypeStruct((SEQ, D_MODEL), BF),
                   jax.ShapeDtypeStruct((SEQ, D_MODEL), BF), jax.ShapeDtypeStruct((1, GATE_W), F32)],
        compiler_params=_params("arbitrary"),
    )(*map(_in_hbm, (dx1, w_out, zcat, b_gate, y_pool, y_gla)))


N_TOK_TILES = SEQ // TOK_TILE
HALO_PER_TILE = TOK_TILE // HALO


LANE_TILES = tuple((lo, min(128, FF_BLK - lo)) for lo in range(0, FF_BLK, 128))


def _taps(w_ref, b_ref, half, lanes, rows):
    shape = (rows, lanes.stop - lanes.start)
    return ([jnp.broadcast_to(w_ref[half, j:j + 1, lanes], shape) for j in range(3)],
            jnp.broadcast_to(b_ref[half, :, lanes], shape))


def _conv_strips(u_ref, ub_ref, ua_ref, taps, lanes, width, n_strips, first):
    row = lax.broadcasted_iota(jnp.int32, (HALO, width), 0)
    prev = [[pltpu.roll(jnp.where(first, 0.0, ub_ref[half, :, lanes]), k, 0) for k in (1, 2)] for half in range(2)]
    for s in range(n_strips + (ua_ref is not None)):
        u3, conv = [], []
        for half in range(2):
            cur = u_ref[half, s * HALO:(s + 1) * HALO, lanes] if s < n_strips else ua_ref[half, :, lanes]
            rolled = [pltpu.roll(cur, k, 0) for k in (1, 2)]
            frames = [jnp.where(row >= 2, rolled[1], prev[half][1]), jnp.where(row >= 1, rolled[0], prev[half][0]), cur]
            prev[half] = rolled
            w3, bias = taps[half]
            u3.append(frames)
            conv.append(bias + frames[0] * w3[0] + frames[1] * w3[1] + frames[2] * w3[2])
        yield s, u3, conv


def _pair_specs(pairs):
    tile = pl.BlockSpec((pairs, None, TOK_TILE, FF_BLK), lambda b, i: (0, b, i, 0))
    before = pl.BlockSpec((pairs, None, HALO, FF_BLK), lambda b, i: (0, b, jnp.maximum(i * HALO_PER_TILE - 1, 0), 0))
    after = pl.BlockSpec((pairs, None, HALO, FF_BLK),
                         lambda b, i: (0, b, jnp.minimum((i + 1) * HALO_PER_TILE, SEQ // HALO - 1), 0))

    def vec(rows):
        return pl.BlockSpec((2, None, rows, FF_BLK), lambda b, i: (0, b, 0, 0))

    return tile, before, after, vec


N_STRIPS = TOK_TILE // HALO


def _up_conv_fwd(h2, wt_up, w_conv, b_conv):
    steps = N_TOK_TILES // 2

    def body(h_ref, h_next, wg_ref, wv_ref, w_ref, b_ref, u_ref, a_ref, buf_a, buf_b, carry):
        j = pl.program_id(1)

        def project(hv, buf):
            buf[0] = _dot(hv, wg_ref[...], tb=True)
            buf[1] = _dot(hv, wv_ref[...], tb=True)

        def conv(buf, row0):
            u_ref[:, row0:row0 + TOK_TILE, :] = buf[...]
            for lo, width in LANE_TILES:
                lanes = slice(lo, lo + width)
                taps = [_taps(w_ref, b_ref, half, lanes, HALO) for half in range(2)]
                pending = None
                for s, _, (cg, cv) in _conv_strips(buf, carry, None, taps, lanes, width, N_STRIPS, False):
                    act = cg * _sigmoid(cg) * cv
                    if s % 2 == 0:
                        pending = act
                    else:
                        a_ref[0, row0 + (s - 1) * HALO:row0 + (s + 1) * HALO, lanes] = (
                            jnp.concatenate([pending, act], axis=0).astype(BF))
            carry[...] = buf[:, TOK_TILE - HALO:, :]

        @pl.when(j == 0)
        def _():
            project(h_ref[0:TOK_TILE, :], buf_a)
            carry[...] = jnp.zeros_like(carry)

        project(h_ref[TOK_TILE:, :], buf_b)
        conv(buf_a, 0)
        project(h_next[...], buf_a)
        conv(buf_b, TOK_TILE)

    w_blk = lambda half: pl.BlockSpec((FF_BLK, D_MODEL), lambda b, j: (b + 4 * half, 0))
    vec = lambda rows: pl.BlockSpec((2, None, rows, FF_BLK), lambda b, j: (0, b, 0, 0))
    u_buf = pltpu.VMEM((2, TOK_TILE, FF_BLK), F32)
    return pl.pallas_call(
        body, name="up_conv_fwd", grid=(4, steps),
        in_specs=[pl.BlockSpec((2 * TOK_TILE, D_MODEL), lambda b, j: (j, 0)),
                  pl.BlockSpec((TOK_TILE, D_MODEL), lambda b, j: (jnp.minimum(2 * j + 2, N_TOK_TILES - 1), 0)),
                  w_blk(0), w_blk(1), vec(3), vec(1)],
        out_specs=[pl.BlockSpec((2, None, 2 * TOK_TILE, FF_BLK), lambda b, j: (0, b, j, 0)),
                   pl.BlockSpec((1, None, 2 * TOK_TILE, FF_BLK), lambda b, j: (0, b, j, 0))],
        out_shape=[jax.ShapeDtypeStruct((2, 4, SEQ, FF_BLK), F32), jax.ShapeDtypeStruct((1, 4, SEQ, FF_BLK), BF)],
        scratch_shapes=[u_buf, u_buf, pltpu.VMEM((2, HALO, FF_BLK), F32)],
        compiler_params=_params("parallel", "arbitrary"),
    )(*map(_in_hbm, (h2, h2, wt_up, wt_up, w_conv, b_conv)))


def _conv_bwd(u, da, w_conv, b_conv):
    def body(u_ref, ub_ref, ua_ref, da_ref, daa_ref, w_ref, b_ref, du_ref, dw_ref, db_ref):
        i = pl.program_id(1)

        @pl.when(i == 0)
        def _():
            dw_ref[...] = jnp.zeros_like(dw_ref)
            db_ref[...] = jnp.zeros_like(db_ref)

        for lo, width in LANE_TILES:
            lanes = slice(lo, lo + width)
            row = lax.broadcasted_iota(jnp.int32, (HALO, width), 0)
            taps = [_taps(w_ref, b_ref, half, lanes, HALO) for half in range(2)]
            acc_w = [[jnp.zeros((HALO, width), F32) for _ in range(3)] for _ in range(2)]
            acc_b = [jnp.zeros((HALO, width), F32) for _ in range(2)]
            da_pair, pending = None, [None, None]
            dc_prev, up_prev = [None, None], [None, None]
            for s, u3, (cg, cv) in _conv_strips(u_ref, ub_ref, ua_ref, taps, lanes, width, N_STRIPS, i == 0):
                act, dact = _silu_parts(cg)
                if s == N_STRIPS:
                    da = jnp.where(i < N_TOK_TILES - 1, daa_ref[0, :, lanes].astype(F32), 0.0)
                elif s % 2 == 0:
                    da_pair = da_ref[0, s * HALO:(s + 2) * HALO, lanes].astype(F32)
                    da = da_pair[:HALO]
                else:
                    da = da_pair[HALO:]
                dc = (da * cv * dact, da * act)
                for half in range(2):
                    up = [pltpu.roll(dc[half], HALO - k, 0) for k in (1, 2)]
                    if s < N_STRIPS:
                        for j in range(3):
                            acc_w[half][j] = acc_w[half][j] + dc[half] * u3[half][j]
                        acc_b[half] = acc_b[half] + dc[half]
                    if s >= 1:
                        w3 = taps[half][0]
                        du = (dc_prev[half] * w3[2] + jnp.where(row < HALO - 1, up_prev[half][0], up[0]) * w3[1]
                              + jnp.where(row < HALO - 2, up_prev[half][1], up[1]) * w3[0])
                        if (s - 1) % 2 == 0:
                            pending[half] = du
                        else:
                            du_ref[half, (s - 2) * HALO:s * HALO, lanes] = jnp.concatenate([pending[half], du],
                                                                                           axis=0).astype(BF)
                    dc_prev[half], up_prev[half] = dc[half], up
            for half in range(2):
                for j in range(3):
                    dw_ref[half, j:j + 1, lanes] += jnp.sum(acc_w[half][j], axis=0, keepdims=True)
                db_ref[half, :, lanes] += jnp.sum(acc_b[half], axis=0, keepdims=True)

    tile, before, after, vec = _pair_specs(2)
    da_tile, _, da_after_spec, _ = _pair_specs(1)
    return pl.pallas_call(
        body, name="conv_bwd", grid=(4, N_TOK_TILES),
        in_specs=[tile, before, after, da_tile, da_after_spec, vec(3), vec(1)],
        out_specs=[tile, vec(3), vec(1)],
        out_shape=[jax.ShapeDtypeStruct((2, 4, SEQ, FF_BLK), BF), jax.ShapeDtypeStruct((2, 4, 3, FF_BLK), F32),
                   jax.ShapeDtypeStruct((2, 4, 1, FF_BLK), F32)],
        compiler_params=_params("parallel", "arbitrary"),
    )(*map(_in_hbm, (u, u, u, da, da, w_conv, b_conv)))


W_IN_SEGMENTS = ((R_POOL, POOL_WIDTH, "cat", C_POOL), (R_QKV, QKV_W, "cat", C_QKV), (R_OG, D_MODEL, "cat", C_OG),
                 (R_GK, GATE_RANK, "gk", 0), (R_GATE, GATE_W, "cat", C_GATE))


def _slab_pieces(d):
    lo, hi = d * IN_SHARD, (d + 1) * IN_SHARD
    pieces = []
    for start, n, dest, at in W_IN_SEGMENTS:
        a, b = max(lo, start), min(hi, start + n)
        if a < b:
            assert (a - lo) % 2 == 0 and (b - a) % 2 == 0 and (at + a - start) % 2 == 0
            pieces.append(((a - lo) // 2, (b - a) // 2, dest, (at + a - start) // 2))
    return pieces


def _unshard_w_in(slabs):
    def body(slab_ref, cat_ref, gk_ref):
        d = pl.program_id(0)
        src = slab_ref.bitcast(jnp.uint32)
        dst = dict(cat=cat_ref.bitcast(jnp.uint32), gk=gk_ref.bitcast(jnp.uint32))

        @pl.when(d == 0)
        def _():
            gk_ref[...] = jnp.zeros_like(gk_ref)

        for dd in range(N_DEV):
            @pl.when(d == dd)
            def _():
                for a, n, dest, at in _slab_pieces(dd):
                    dst[dest][pl.ds(at, n), :] = src[0, pl.ds(a, n), :]

    return pl.pallas_call(
        body, name="unshard_w_in", grid=(N_DEV,),
        in_specs=[pl.BlockSpec((1, IN_SHARD, D_MODEL), lambda d: (d, 0, 0))],
        out_specs=[_const_spec((N_CAT, D_MODEL)), _const_spec((GK_PAD, D_MODEL))],
        out_shape=[jax.ShapeDtypeStruct((N_CAT, D_MODEL), BF), jax.ShapeDtypeStruct((GK_PAD, D_MODEL), BF)],
        compiler_params=_params("arbitrary"),
    )(_in_hbm(slabs))


def _shard_d_w_in(d_cat, d_gk):
    def body(cat_ref, gk_ref, slab_ref):
        d = pl.program_id(0)
        cat = cat_ref.bitcast(jnp.uint32)
        gk = pltpu.bitcast(gk_ref[0:GATE_RANK, :].astype(BF), jnp.uint32)
        dst = slab_ref.bitcast(jnp.uint32)
        for dd in range(N_DEV):
            @pl.when(d == dd)
            def _():
                for a, n, source, at in _slab_pieces(dd):
                    dst[0, pl.ds(a, n), :] = gk[at:at + n] if source == "gk" else cat[pl.ds(at, n), :]

    return pl.pallas_call(
        body, name="shard_d_w_in", grid=(N_DEV,),
        in_specs=[_const_spec((N_CAT, D_MODEL)), _const_spec((GK_PAD, D_MODEL))],
        out_specs=pl.BlockSpec((1, IN_SHARD, D_MODEL), lambda d: (d, 0, 0)),
        out_shape=jax.ShapeDtypeStruct((N_DEV, IN_SHARD, D_MODEL), BF), compiler_params=_params("parallel"),
    )(_in_hbm(d_cat), _in_hbm(d_gk))


ANY = pl.BlockSpec(memory_space=pl.ANY)


def _place():
    x, y, c = lax.axis_index("x"), lax.axis_index("y"), lax.axis_index("c")
    other_chips = [(1 - x, y), (x, 1 - y), (1 - x, 1 - y)]
    return x, y, c, other_chips


SEM = pl.BlockSpec(memory_space=pltpu.SEMAPHORE)
IN_HBM = pl.BlockSpec(memory_space=pltpu.HBM)
SPLIT_PARAMS = pltpu.CompilerParams(has_side_effects=pltpu.SideEffectType.DATAFLOW_SIDE_EFFECTING)


def _gather_first(refs, send_sems, recv_sems):
    x, y, c, chips = _place()
    targets = [(x, y, 1 - c)] + [(px, py, c) for px, py in chips]
    return [pltpu.make_async_remote_copy(src_ref=refs[2 * a], dst_ref=refs[2 * a + 1].at[4 * x + 2 * y + c],
                                         send_sem=send_sems.at[4 * a + k], recv_sem=recv_sems.at[4 * a + k],
                                         device_id=to, device_id_type=MESH)
            for a in range(len(refs) // 2) for k, to in enumerate(targets)]


def _gather_direct(refs, send_sems, recv_sems):
    x, y, c, _ = _place()
    flips = [(dx, dy, dc) for dx in (0, 1) for dy in (0, 1) for dc in (0, 1) if dx + dy + dc]
    targets = [(1 - x if dx else x, 1 - y if dy else y, 1 - c if dc else c) for dx, dy, dc in flips]
    return [pltpu.make_async_remote_copy(src_ref=refs[2 * a], dst_ref=refs[2 * a + 1].at[4 * x + 2 * y + c],
                                         send_sem=send_sems.at[7 * a + k], recv_sem=recv_sems.at[7 * a + k],
                                         device_id=to, device_id_type=MESH)
            for a in range(len(refs) // 2) for k, to in enumerate(targets)]


def _gather_second(refs, send_sems, recv_sems):
    x, y, c, chips = _place()
    copies = []
    for a, land in enumerate(refs):
        for j, (px, py) in enumerate(chips):
            block = land.at[4 * px + 2 * py + c]
            copies.append(pltpu.make_async_remote_copy(src_ref=block, dst_ref=block, send_sem=send_sems.at[3 * a + j],
                                                       recv_sem=recv_sems.at[3 * a + j], device_id=(x, y, 1 - c),
                                                       device_id_type=MESH))
    return copies


def _reduce_first(refs, send_sems, recv_sems):
    x, y, c, _ = _place()
    return [pltpu.make_async_remote_copy(src_ref=refs[2 * a].at[j, 1 - c], dst_ref=refs[2 * a + 1].at[j],
                                         send_sem=send_sems.at[4 * a + j], recv_sem=recv_sems.at[4 * a + j],
                                         device_id=(x, y, 1 - c), device_id_type=MESH)
            for a in range(len(refs) // 2) for j in range(4)]


def _reduce_second(refs, send_sems, recv_sems):
    _, _, c, chips = _place()
    return [pltpu.make_async_remote_copy(src_ref=refs[2 * a].at[2 * px + py], dst_ref=refs[2 * a + 1].at[k],
                                         send_sem=send_sems.at[3 * a + k], recv_sem=recv_sems.at[3 * a + k],
                                         device_id=(px, py, c), device_id_type=MESH)
            for a in range(len(refs) // 2) for k, (px, py) in enumerate(chips)]


def _split_start(name, groups):
    arrays = [a for g in groups for a in g[0]]
    n = len(arrays)

    def body(*refs):
        sems = refs[n:n + 2 * len(groups)]
        at = 0
        for gi, (members, _, build) in enumerate(groups):
            for cp in build(refs[at:at + len(members)], sems[2 * gi], sems[2 * gi + 1]):
                cp.start()
            at += len(members)
        refs[-1][...] = jnp.zeros_like(refs[-1])

    sem_shapes = [pltpu.SemaphoreType.DMA((g[1],)) for g in groups for _ in range(2)]
    outs = pl.pallas_call(
        body, name=name, in_specs=[IN_HBM] * n,
        out_shape=(*sem_shapes, *[pltpu.HBM(a.shape, a.dtype) for a in arrays], jax.ShapeDtypeStruct((8, 128), F32)),
        out_specs=(*[SEM] * len(sem_shapes), *[IN_HBM] * n, pl.BlockSpec(memory_space=pltpu.VMEM)),
        input_output_aliases={i: len(sem_shapes) + i for i in range(n)}, compiler_params=SPLIT_PARAMS,
    )(*[pltpu.with_memory_space_constraint(a, pltpu.HBM) for a in arrays])
    per_group, at = [], len(sem_shapes)
    for gi, (members, _, _) in enumerate(groups):
        per_group.append((outs[2 * gi], outs[2 * gi + 1], list(outs[at:at + len(members)])))
        at += len(members)
    return per_group, outs[-1]


def _split_wait(name, started, build, after):
    send_sems, recv_sems, arrays = started
    n = len(arrays)
    after = after if isinstance(after, (tuple, list)) else (after,)

    def body(*refs):
        for cp in build(refs[:n], refs[n], refs[n + 1]):
            cp.wait_send()
            cp.wait_recv()

    return pl.pallas_call(
        body, name=name, in_specs=[IN_HBM] * n + [SEM, SEM] + [ANY] * len(after),
        out_shape=tuple(pltpu.HBM(a.shape, a.dtype) for a in arrays), out_specs=tuple([IN_HBM] * n),
        input_output_aliases={i: i for i in range(n)}, compiler_params=SPLIT_PARAMS,
    )(*arrays, send_sems, recv_sems, *after)


def _gather_landing(shard, me):
    return lax.dynamic_update_slice(lax.empty((N_DEV,) + shard.shape, shard.dtype), shard[None],
                                    (me,) + (0,) * shard.ndim)


def _tile_2d(rows, cols):
    for t in (256, 176, 128):
        if rows % t == 0:
            return t, cols
    return rows, 256


def _pair_sum(part, recv, core, name):
    _, rows, cols = recv.shape
    tr, tc = rows, cols

    def body(c_ref, p_ref, r_ref, o_ref):
        del c_ref
        o_ref[...] = (p_ref[...].astype(F32) + r_ref[...].astype(F32)).astype(BF)

    grid_spec = pltpu.PrefetchScalarGridSpec(
        num_scalar_prefetch=1, grid=(4, rows // tr, cols // tc),
        in_specs=[pl.BlockSpec((None, None, tr, tc), lambda j, i, k, c_ref: (j, c_ref[0], i, k)),
                  pl.BlockSpec((None, tr, tc), lambda j, i, k, c_ref: (j, i, k))],
        out_specs=pl.BlockSpec((None, tr, tc), lambda j, i, k, c_ref: (j, i, k)))
    return pl.pallas_call(
        body, name=name, grid_spec=grid_spec, out_shape=jax.ShapeDtypeStruct(recv.shape, BF),
        compiler_params=_params("parallel", "parallel", "parallel"),
    )(core, *map(_in_hbm, (part, recv)))


def _adamw(w, g, m, v):
    m = ADAM_B1 * m + (1.0 - ADAM_B1) * g
    v = ADAM_B2 * v + (1.0 - ADAM_B2) * (g * g)
    delta = -ADAM_LR * ((m / ADAM_C1) / (jnp.sqrt(v / ADAM_C2) + ADAM_EPS) + ADAM_WD * w)
    return delta, m, v


def _chip_sum_adamw(sums, recv, w, m, v, chip, name):
    rows, cols = w.shape
    tr, tc = _tile_2d(rows, cols)

    def body(chip_ref, s_ref, r_ref, w_ref, m_ref, v_ref, g_out, d_out, m_out, v_out):
        del chip_ref
        g = s_ref[...].astype(F32)
        for k in range(3):
            g = g + r_ref[k].astype(F32)
        g_out[...] = g
        d_out[...], m_out[...], v_out[...] = _adamw(w_ref[...], g, m_ref[...], v_ref[...])

    tile = pl.BlockSpec((tr, tc), lambda i, k, chip_ref: (i, k))
    grid_spec = pltpu.PrefetchScalarGridSpec(
        num_scalar_prefetch=1, grid=(rows // tr, cols // tc),
        in_specs=[pl.BlockSpec((None, tr, tc), lambda i, k, chip_ref: (chip_ref[0], i, k)),
                  pl.BlockSpec((3, tr, tc), lambda i, k, chip_ref: (0, i, k)), tile, tile, tile],
        out_specs=[tile] * 4)
    return pl.pallas_call(
        body, name=name, grid_spec=grid_spec, out_shape=[jax.ShapeDtypeStruct((rows, cols), F32)] * 4,
        compiler_params=_params("parallel", "parallel"),
    )(chip, *map(_in_hbm, (sums, recv, w, m, v)))


def _small_sum_adamw(me, entries, loss_parts):
    def whole(shape, squeeze=0, pick=False):
        blk = (None,) * squeeze + tuple(shape[squeeze:])
        if pick:
            blk = (shape[0], None) + tuple(shape[2:])
            return pl.BlockSpec(blk, lambda i, me_ref: (0, me_ref[0]) + (0,) * (len(shape) - 2))
        return pl.BlockSpec(blk, lambda i, me_ref: (0,) * len(shape))

    in_specs, out_specs, out_shape, args = [], [], [], []
    for parts, w, m, v, sharded in entries:
        lead = w.ndim - (parts.ndim - (2 if sharded else 1))
        in_specs += [whole(parts.shape, pick=sharded)] + [whole(w.shape, squeeze=lead)] * 3
        out_specs += [whole(w.shape, squeeze=lead)] * 4
        out_shape += [jax.ShapeDtypeStruct(w.shape, F32)] * 4
        args += [parts, w, m, v]
    in_specs.append(whole(loss_parts.shape))
    out_specs.append(whole(loss_parts.shape[1:]))
    out_shape.append(jax.ShapeDtypeStruct(loss_parts.shape[1:], F32))
    n = len(entries)

    def added(p_ref):
        total = p_ref[0]
        for d in range(1, N_DEV):
            total = total + p_ref[d]
        return total

    def body(me_ref, *refs):
        del me_ref
        ins, outs = refs[:4 * n + 1], refs[4 * n + 1:]
        for e in range(n):
            p_ref, w_ref, m_ref, v_ref = ins[4 * e:4 * e + 4]
            g_out, d_out, m_out, v_out = outs[4 * e:4 * e + 4]
            g = added(p_ref)
            g_out[...] = g
            d_out[...], m_out[...], v_out[...] = _adamw(w_ref[...], g, m_ref[...], v_ref[...])
        outs[4 * n][...] = added(ins[4 * n])

    grid_spec = pltpu.PrefetchScalarGridSpec(num_scalar_prefetch=1, grid=(1,), in_specs=in_specs, out_specs=out_specs)
    outs = pl.pallas_call(body, name="small_sum_adamw", grid_spec=grid_spec, out_shape=out_shape,
                          compiler_params=_params("arbitrary"))(me, *map(_in_hbm, args + [loss_parts]))
    return [outs[4 * e:4 * e + 4] for e in range(n)], outs[4 * n]


MM_TILE = 512
N_MM_TILES = SEQ // MM_TILE
CAT_TILE = 512
N_CAT_TILES = N_CAT // CAT_TILE


def kernel(x, g_mix, w_in, b_gate, w_gk_up, b_gk, w_pool_grp, pool_scale, g_gla_head, w_pool_proj, w_gla_proj, w_out, g_ffn, w_up, w_conv, b_conv, w_down, g_final, loss_target, m_g_mix, m_w_in, m_b_gate, m_w_gk_up, m_b_gk, m_w_pool_grp, m_pool_scale, m_g_gla_head, m_w_pool_proj, m_w_gla_proj, m_w_out, m_g_ffn, m_w_up, m_w_conv, m_b_conv, m_w_down, m_g_final, v_g_mix, v_w_in, v_b_gate, v_w_gk_up, v_b_gk, v_w_pool_grp, v_pool_scale, v_g_gla_head, v_w_pool_proj, v_w_gla_proj, v_w_out, v_g_ffn, v_w_up, v_w_conv, v_b_conv, v_w_down, v_g_final):
    xi, yi, ci = lax.axis_index("x"), lax.axis_index("y"), lax.axis_index("c")
    me = 4 * xi + 2 * yi + ci
    core = jnp.reshape(ci, (1,)).astype(jnp.int32)
    chip = jnp.reshape(2 * xi + yi, (1,)).astype(jnp.int32)
    xs, target = x[0], loss_target[0]

    big = dict(w_in=w_in[0].T, w_pool_proj=w_pool_proj[0], w_gla_proj=w_gla_proj[0], w_out=w_out[0], w_up=w_up[0].T,
               w_down=w_down[0])
    moments = dict(w_in=(m_w_in[0].T, v_w_in[0].T), w_pool_proj=(m_w_pool_proj[0], v_w_pool_proj[0]),
                   w_gla_proj=(m_w_gla_proj[0], v_w_gla_proj[0]), w_out=(m_w_out[0], v_w_out[0]),
                   w_up=(m_w_up[0].T, v_w_up[0].T), w_down=(m_w_down[0], v_w_down[0]))
    names = list(big)
    shards = {k: big[k].astype(BF) for k in names}
    shards["w_gk_up"], shards["w_conv"] = w_gk_up[0], w_conv[0]
    gather_groups = (("w_in", "w_gk_up"), ("w_pool_proj", "w_gla_proj", "w_out"), ("w_up", "w_down", "w_conv"))
    started, token = _split_start("gather_start", [
        ([t for k in g for t in (shards[k], _gather_landing(shards[k], me))], 4 * len(g), _gather_first)
        for g in gather_groups])

    def gather_pass(gi, after):
        lands = list(_split_wait(f"gather_wait_{gi}", started[gi], _gather_first, after)[1::2])
        passed, tkn = _split_start(f"gather_pass_{gi}", [(lands, 3 * len(lands), _gather_second)])
        return passed[0], tkn

    def gather_done(gi, passed, after):
        return dict(zip(gather_groups[gi], _split_wait(f"gather_pass_wait_{gi}", passed, _gather_second, after)))

    tok = lambda i, j, k: (i, 0)
    whole = lambda i, j, k: (0, 0)
    kblk = lambda i, j, k: (k, 0)
    ff_seq = (None, None, SEQ, FF_BLK)

    h = _rms_fwd(xs, g_mix + token[:1, :1], "rms_mix")
    wg = gather_done(0, gather_pass(0, h)[0], h)
    wt_cat, wt_gk = _unshard_w_in(wg["w_in"])
    wgk_pad = jnp.pad(wg["w_gk_up"].transpose(1, 0, 2).reshape(GATE_RANK, GLA_DK), ((0, GK_PAD - GATE_RANK), (0, 0)))
    zcat = _mm(h, wt_cat, out_shape=(SEQ, N_CAT), out_dtype=BF, grid=(N_CAT_TILES, 1, 1),
               blk_a=(SEQ, D_MODEL), blk_b=(CAT_TILE, D_MODEL), blk_o=(SEQ, CAT_TILE),
               map_a=whole, map_b=lambda j, i, k: (j, 0), map_o=lambda j, i, k: (0, j), tb=True, name="mm_in")
    la = _gk_fwd(h, wt_gk, wgk_pad, b_gk)
    passed, tkn = gather_pass(1, la)
    o, states = _gla_fwd(zcat, la, tkn)
    wg = gather_done(1, passed, o)
    wpp = wg["w_pool_proj"].transpose(1, 0, 2).reshape(POOL_WIDTH, D_MODEL)
    wgp = wg["w_gla_proj"].reshape(D_MODEL, D_MODEL)
    wout = wg["w_out"].reshape(D_MODEL, D_MODEL)
    og = _post_gla_fwd(o, zcat, g_gla_head)
    ps = _pool_fwd(zcat, w_pool_grp[0], pool_scale)
    passed, tkn = gather_pass(2, (og, ps))
    y_pool, y_gla, mixed, x1, h2 = _mix_out_fwd(ps, og, zcat, xs, wpp, wgp, wout, b_gate, g_ffn, tkn)
    wg = gather_done(2, passed, h2)
    wt_up = wg["w_up"].reshape(2 * D_FF, D_MODEL)
    wdown = wg["w_down"].reshape(D_FF, D_MODEL)
    wconv4 = wg["w_conv"].reshape(2, 4, 3, FF_BLK)
    bconv4 = b_conv.reshape(2, 4, 1, FF_BLK)
    blk4 = lambda b, i, k: (b // 4, b % 4, 0, 0)
    u4, act = _up_conv_fwd(h2, wt_up, wconv4, bconv4)
    loss_part, dx2, dx2_bf, dg_final = _mm_tokens(
        act, wdown, blk_a=(None, 4, TOK_MM_TILE, FF_BLK), map_a=lambda i: (0, 0, i, 0),
        pieces=[(b, b * FF_BLK, FF_BLK) for b in range(4)], res=x1, then=("loss", g_final.reshape(1, D_MODEL), target),
        name="mm_down_loss")

    da = _mm(dx2_bf, wdown, out_shape=(1, 4, SEQ, FF_BLK), out_dtype=BF, grid=(4, 1, 1),
             blk_a=(SEQ, D_MODEL), blk_b=(FF_BLK, D_MODEL), blk_o=ff_seq,
             map_a=whole, map_b=lambda b, i, k: (b, 0), map_o=lambda b, i, k: (0, b, 0, 0), tb=True, name="mm_d_act")
    d_wdown = _mm(act, dx2_bf, out_shape=(D_FF, D_MODEL), out_dtype=BF, grid=(4, 1, 1),
                  blk_a=ff_seq, blk_b=(SEQ, D_MODEL), blk_o=(FF_BLK, D_MODEL),
                  map_a=lambda b, i, k: (0, b, 0, 0), map_b=whole, map_o=lambda b, i, k: (b, 0), ta=True,
                  name="mm_d_wdown")
    du4, d_wconv, d_bconv = _conv_bwd(u4, da, wconv4, bconv4)
    d_wt_up = _mm(du4, h2, out_shape=(2 * D_FF, D_MODEL), out_dtype=BF, grid=(N_DEV, 1, 1),
                  blk_a=ff_seq, blk_b=(SEQ, D_MODEL), blk_o=(FF_BLK, D_MODEL),
                  map_a=blk4, map_b=whole, map_o=lambda b, i, k: (b, 0), ta=True, name="mm_d_wup")
    res = {}

    def reduce_start(keys, parts):
        arrays = [t for k in keys for t in (parts[k], lax.empty((4,) + parts[k].shape[2:], BF))]
        st, tkn = _split_start("reduce_start_" + keys[0], [(arrays, 4 * len(keys), _reduce_first)])
        return st[0], tkn

    def reduce_cross(keys, st, after):
        arrays = _split_wait("reduce_wait_" + keys[0], st, _reduce_first, after)
        sums = [_pair_sum(p, r, core, "pair_sum_" + k) for k, p, r in zip(keys, arrays[0::2], arrays[1::2])]
        arrays = [t for s in sums for t in (s, lax.empty((3,) + s.shape[1:], BF))]
        st2, tkn = _split_start("reduce_cross_" + keys[0], [(arrays, 3 * len(keys), _reduce_second)])
        return st2[0], tkn

    def reduce_done(keys, st2, after):
        arrays = _split_wait("reduce_cross_wait_" + keys[0], st2, _reduce_second, after)
        for k, s, r in zip(keys, arrays[0::2], arrays[1::2]):
            outs = _chip_sum_adamw(s, r, big[k], moments[k][0], moments[k][1], chip, "adamw_" + k)
            res[k] = [(t.T if k in ("w_in", "w_up") else t)[None] for t in outs]

    ffn_keys = ("w_down", "w_up")
    ffn_red, tkn = reduce_start(ffn_keys, dict(w_down=d_wdown.reshape(4, 2, D_FF // N_DEV, D_MODEL),
                                               w_up=d_wt_up.reshape(4, 2, FF_BLK, D_MODEL)))
    dx1, dg_ffn = _mm_tokens(
        du4, wt_up, blk_a=(2, 4, TOK_MM_TILE, FF_BLK), map_a=lambda i: (0, 0, i, 0),
        pieces=[((b // 4, b % 4), b * FF_BLK, FF_BLK) for b in range(N_DEV)], after=tkn, then=("rms_bwd", x1, g_ffn, dx2),
        name="mm_d_h2_rms")

    sq_t = dict(out_shape=(D_MODEL, D_MODEL), grid=(1, 1, N_MM_TILES), blk_a=(MM_TILE, D_MODEL),
                blk_b=(MM_TILE, D_MODEL), blk_o=(D_MODEL, D_MODEL), map_a=kblk, map_b=kblk, map_o=whole, ta=True)
    d_wout = _mm(mixed, dx1, out_dtype=BF, name="mm_d_wout", **sq_t)
    dzcat, dy_pool, dy_gla, db_gate = _mix_bwd(dx1, wout, zcat, b_gate, y_pool, y_gla)
    ffn_red, tkn = reduce_cross(ffn_keys, ffn_red, db_gate)
    d_wgp = _mm(og, dy_gla, out_dtype=BF, after=tkn, name="mm_d_wgp", **sq_t)
    mix_keys = ("w_out", "w_gla_proj")
    mix_red, tkn = reduce_start(mix_keys, dict(w_out=d_wout.reshape(4, 2, D_MODEL // N_DEV, D_MODEL),
                                               w_gla_proj=d_wgp.reshape(4, 2, D_MODEL // N_DEV, D_MODEL)))
    dzcat, d_o, dg_head = _post_gla_bwd(dzcat, dy_gla, wgp, o, zcat, g_gla_head + tkn[:1, :1])
    dzcat, dla = _gla_bwd(dzcat, zcat, la, d_o, states)
    mix_red, tkn = reduce_cross(mix_keys, mix_red, dla)
    dh_gk, d_wt_gk, d_wgk, db_gk = _gk_bwd(dla, h, wt_gk, wgk_pad, b_gk + tkn[:1, :1])
    dps = _mm(dy_pool, wpp, out_shape=(SEQ, POOL_WIDTH), out_dtype=F32, grid=(N_MM_TILES, 1, 1),
              blk_a=(MM_TILE, D_MODEL), blk_b=(POOL_WIDTH, D_MODEL), blk_o=(MM_TILE, POOL_WIDTH),
              map_a=tok, map_b=whole, map_o=tok, tb=True, name="mm_d_ps")
    d_wpp = _mm(ps, dy_pool, out_shape=(POOL_WIDTH, D_MODEL), out_dtype=F32, grid=(1, 1, N_MM_TILES),
                blk_a=(MM_TILE, POOL_WIDTH), blk_b=(MM_TILE, D_MODEL), blk_o=(POOL_WIDTH, D_MODEL),
                map_a=kblk, map_b=kblk, map_o=whole, ta=True, name="mm_d_wpp")
    dzcat, d_wgrp, d_scale = _pool_bwd(dzcat, zcat, dps, w_pool_grp[0], pool_scale)
    row = lambda t: t.reshape(1, D_MODEL)
    conv_vec = lambda t: t.reshape(2, 4, 1, FF_BLK)
    small = [("b_gate", db_gate, b_gate, m_b_gate, v_b_gate, False),
             ("w_gk_up", d_wgk.reshape(GATE_RANK, N_DEV, GLA_DK // N_DEV).transpose(1, 0, 2), w_gk_up, m_w_gk_up,
              v_w_gk_up, True),
             ("b_gk", db_gk, b_gk, m_b_gk, v_b_gk, False),
             ("w_pool_grp", d_wgrp, w_pool_grp, m_w_pool_grp, v_w_pool_grp, False),
             ("pool_scale", d_scale, pool_scale, m_pool_scale, v_pool_scale, False),
             ("g_gla_head", dg_head, g_gla_head, m_g_gla_head, v_g_gla_head, False),
             ("g_ffn", dg_ffn, g_ffn, m_g_ffn, v_g_ffn, False),
             ("w_conv", d_wconv.reshape(N_DEV, 3, FF_BLK), w_conv, m_w_conv, v_w_conv, True),
             ("b_conv", d_bconv, conv_vec(b_conv), conv_vec(m_b_conv), conv_vec(v_b_conv), False),
             ("g_final", dg_final, row(g_final), row(m_g_final), row(v_g_final), False)]

    def small_start(parts, name):
        arrays = [t for p in parts for t in (p, _gather_landing(p, me))]
        st, tkn = _split_start(name, [(arrays, 7 * len(parts), _gather_direct)])
        return st[0], tkn

    small_sent, tkn = small_start([t[1] for t in small] + [loss_part], "small_start")
    d_wt_cat = _mm(dzcat, h, out_shape=(N_CAT, D_MODEL), out_dtype=BF, grid=(N_CAT_TILES, 1, 1),
                   blk_a=(SEQ, CAT_TILE), blk_b=(SEQ, D_MODEL), blk_o=(CAT_TILE, D_MODEL),
                   map_a=lambda j, i, k: (0, j), map_b=whole, map_o=lambda j, i, k: (j, 0), ta=True, after=tkn,
                   name="mm_d_wcat")
    in_keys = ("w_in", "w_pool_proj")
    in_red, tkn = reduce_start(in_keys, dict(
        w_in=_shard_d_w_in(d_wt_cat, d_wt_gk).reshape(4, 2, IN_SHARD, D_MODEL),
        w_pool_proj=d_wpp.reshape(POOL_WIDTH, N_DEV, D_MODEL // N_DEV).transpose(1, 0, 2).astype(BF)
        .reshape(4, 2, POOL_WIDTH, D_MODEL // N_DEV)))
    reduce_done(mix_keys, mix_red, tkn)
    in_red, tkn = reduce_cross(in_keys, in_red, res["w_out"][0])
    grad_x, dg_mix = _mm_tokens(dzcat, wt_cat, blk_a=(TOK_MM_TILE, N_CAT), map_a=lambda i: (i, 0),
                                pieces=[(None, 0, N_CAT)], res=dh_gk, after=tkn, then=("rms_bwd", xs, g_mix, dx1),
                                name="mm_d_h_rms")
    g_mix_sent, tkn = small_start([dg_mix], "g_mix_start")
    reduce_done(ffn_keys, ffn_red, (grad_x, tkn))
    gathered = _split_wait("small_wait", small_sent, _gather_direct, res["w_down"][0])[1::2]
    small.append(("g_mix", dg_mix, g_mix, m_g_mix, v_g_mix, False))
    gathered = list(gathered[:-1]) + [_split_wait("g_mix_wait", g_mix_sent, _gather_direct, gathered[0])[1], gathered[-1]]
    small_out, loss_sum = _small_sum_adamw(jnp.reshape(me, (1,)).astype(jnp.int32),
                                           [(p,) + t[2:] for p, t in zip(gathered, small)], gathered[-1])
    for t, outs in zip(small, small_out):
        res[t[0]] = list(outs)
    res["b_conv"] = [t.reshape(b_conv.shape) for t in res["b_conv"]]
    res["g_final"] = [t.reshape(g_final.shape) for t in res["g_final"]]

    reduce_done(in_keys, in_red, loss_sum)
    loss = loss_sum[0, 0]
    order =["g_mix", "w_in", "b_gate", "w_gk_up", "b_gk", "w_pool_grp", "pool_scale", "g_gla_head", "w_pool_proj",
             "w_gla_proj", "w_out", "g_ffn", "w_up", "w_conv", "b_conv", "w_down", "g_final"]
    return (loss, grad_x[None], *[res[k][0] for k in order], *[res[k][1] for k in order],
            *[res[k][2] for k in order], *[res[k][3] for k in order])
```

```python
import jax
import jax.numpy as jnp
from jax import lax
from jax.experimental import pallas as pl
from jax.experimental.pallas import tpu as pltpu

F32 = jnp.float32
BF = jnp.bfloat16
HIGHEST = lax.Precision.HIGHEST
MESH = pl.DeviceIdType.MESH

N_DEV = 8
SEQ = 2048
D_MODEL = 1024
CHUNK = 64
EPS = 1e-6
POOL_WIDTH = 512
POOL_WINDOWS = (2, 4, 8, 16)
POOL_GD = 128
POOL_HALO = 16
HEADS = 4
HK = 128
HV = 256
GLA_DK = 512
GATE_RANK = 16
GATE_NORM = 16.0
D_FF = 2816
FF_BLK = 704
IN_SHARD = 706
C_QKV, C_GATE, C_OG, C_POOL = 0, 2048, 4096, 5120
N_CAT = 5632
R_POOL, R_QKV, R_OG, R_GK, R_GATE = 0, 512, 2560, 3584, 3600
GK_PAD = 128

ADAM_LR, ADAM_B1, ADAM_B2, ADAM_EPS, ADAM_WD, ADAM_STEP = 0.001, 0.9, 0.999, 1e-08, 0.01, 10
ADAM_C1 = 1.0 - ADAM_B1 ** ADAM_STEP
ADAM_C2 = 1.0 - ADAM_B2 ** ADAM_STEP

VMEM_BYTES_V7X = 64 * 1024 * 1024
VMEM_LIMIT = VMEM_BYTES_V7X * 3 // 4

TOK_TILE = 256
HALO = 8
GLA_CPS = 4


def _params(*sem):
    return pltpu.CompilerParams(dimension_semantics=sem, vmem_limit_bytes=VMEM_LIMIT)


def _const_spec(shape):
    nd = len(shape)
    return pl.BlockSpec(shape, lambda *_: (0,) * nd)


def _in_hbm(t):
    return pltpu.with_memory_space_constraint(t, pltpu.HBM)


def _dot(a, b, ta=False, tb=False):
    dims = (((0 if ta else 1,), (1 if tb else 0,)), ((), ()))
    return lax.dot_general(a.astype(BF), b.astype(BF), dims, preferred_element_type=F32)


def _dot_exact(a, b):
    return jnp.dot(a, b, precision=HIGHEST, preferred_element_type=F32)


def _sigmoid(x):
    return 0.5 * jnp.tanh(0.5 * x) + 0.5


def _mm(a, b, *, out_shape, out_dtype, grid, blk_a, blk_b, blk_o, map_a, map_b, map_o, ta=False, tb=False,
        after=None, name):
    gk = grid[2]
    n_in = 2 + (after is not None)

    def body(*refs):
        a_ref, b_ref, o_ref = refs[0], refs[1], refs[n_in]
        prod = _dot(a_ref[...], b_ref[...], ta, tb)
        if gk == 1:
            o_ref[...] = prod.astype(out_dtype)
        else:
            acc = refs[n_in + 1]
            k = pl.program_id(2)

            @pl.when(k == 0)
            def _():
                acc[...] = prod

            @pl.when(k > 0)
            def _():
                acc[...] += prod

            @pl.when(k == gk - 1)
            def _():
                o_ref[...] = acc[...].astype(out_dtype)

    in_specs = [pl.BlockSpec(blk_a, map_a), pl.BlockSpec(blk_b, map_b)]
    args = [_in_hbm(a), _in_hbm(b)]
    if after is not None:
        in_specs.append(pl.BlockSpec(memory_space=pl.ANY))
        args.append(after)
    return pl.pallas_call(
        body, name=name, grid=grid, in_specs=in_specs, out_specs=pl.BlockSpec(blk_o, map_o),
        out_shape=jax.ShapeDtypeStruct(out_shape, out_dtype),
        scratch_shapes=[] if gk == 1 else [pltpu.VMEM(tuple(d for d in blk_o if d is not None), F32)],
        compiler_params=_params("parallel", "parallel", "arbitrary"),
    )(*args)


TOK_MM_TILE = 256


def _mm_tokens(a, w, *, blk_a, map_a, pieces, res=None, after=None, then=None, name):
    n_in = 2 + (res is not None) + (after is not None) + (0 if then is None else len(then) - 1)

    def accumulate(ref, part):
        @pl.when(pl.program_id(0) == 0)
        def _():
            ref[...] = part

        @pl.when(pl.program_id(0) > 0)
        def _():
            ref[...] += part

    def body(*refs):
        a_ref, w_ref = refs[:2]
        extra, outs = refs[n_in - (0 if then is None else len(then) - 1):n_in], refs[n_in:]
        total = None
        for idx, row, n in pieces:
            av = a_ref[...] if idx is None else a_ref[idx]
            prod = _dot(av, w_ref[row:row + n, :])
            total = prod if total is None else total + prod
        if res is not None:
            total = total + refs[2][...]
        if then is None:
            outs[0][...] = total
        elif then[0] == "rms_bwd":
            dx, part = _rms_bwd_tile(total, extra[0][...], extra[1][...], extra[2][...])
            outs[0][...] = dx
            accumulate(outs[1], part)
        else:
            lpart, dx, part = _loss_tile(total, extra[0][...], extra[1][...])
            outs[1][...] = dx
            outs[2][...] = dx.astype(BF)
            accumulate(outs[0], lpart)
            accumulate(outs[3], part)

    tile = pl.BlockSpec((TOK_MM_TILE, D_MODEL), lambda i: (i, 0))
    vec = _const_spec((1, D_MODEL))
    big = jax.ShapeDtypeStruct((SEQ, D_MODEL), F32)
    small = jax.ShapeDtypeStruct((1, D_MODEL), F32)
    in_specs = [pl.BlockSpec(blk_a, map_a), pl.BlockSpec(w.shape, lambda i: (0, 0), pipeline_mode=pl.Buffered(1))]
    args = [a, w]
    if res is not None:
        in_specs.append(tile)
        args.append(res)
    if after is not None:
        in_specs.append(pl.BlockSpec(memory_space=pl.ANY))
        args.append(after)
    if then is None:
        out_specs, out_shape = tile, big
    elif then[0] == "rms_bwd":
        in_specs += [tile, vec, tile]
        out_specs, out_shape = [tile, vec], [big, small]
    else:
        in_specs += [vec, tile]
        out_specs = [_const_spec((1, 128)), tile, tile, vec]
        out_shape = [jax.ShapeDtypeStruct((1, 128), F32), big, jax.ShapeDtypeStruct((SEQ, D_MODEL), BF), small]
    if then is not None:
        args += list(then[1:])
    return pl.pallas_call(
        body, name=name, grid=(SEQ // TOK_MM_TILE,), in_specs=in_specs, out_specs=out_specs, out_shape=out_shape,
        compiler_params=_params("parallel" if then is None else "arbitrary"),
    )(*[_in_hbm(t) for t in args])


def _rms_fwd(x, g, name):
    def body(x_ref, g_ref, o_ref):
        xv = x_ref[...]
        r = lax.rsqrt(jnp.mean(xv * xv, axis=-1, keepdims=True) + EPS)
        o_ref[...] = (xv * r * g_ref[...]).astype(BF)

    tile = pl.BlockSpec((TOK_TILE, D_MODEL), lambda i: (i, 0))
    return pl.pallas_call(
        body, name=name, grid=(SEQ // TOK_TILE,), in_specs=[tile, _const_spec((1, D_MODEL))], out_specs=tile,
        out_shape=jax.ShapeDtypeStruct((SEQ, D_MODEL), BF), compiler_params=_params("parallel"),
    )(*map(_in_hbm, (x, g)))


def _rms_bwd_tile(dyv, xv, gv, dresv):
    r = lax.rsqrt(jnp.mean(xv * xv, axis=-1, keepdims=True) + EPS)
    xn = xv * r
    dxn = dyv * gv
    return dresv + r * (dxn - xn * jnp.mean(dxn * xn, axis=-1, keepdims=True)), jnp.sum(dyv * xn, axis=0, keepdims=True)


def _loss_tile(xv, gv, tv):
    r = lax.rsqrt(jnp.mean(xv * xv, axis=-1, keepdims=True) + EPS)
    xn = xv * r
    err = xn * gv - tv
    lpart = jnp.full((1, 128), 0.5 * jnp.sum(jnp.mean(err * err, axis=-1, keepdims=True)), F32)
    dyv = err * (1.0 / D_MODEL)
    dxn = dyv * gv
    return lpart, r * (dxn - xn * jnp.mean(dxn * xn, axis=-1, keepdims=True)), jnp.sum(dyv * xn, axis=0, keepdims=True)


def _pool_counts(w):
    pos = lax.broadcasted_iota(jnp.int32, (SEQ, 1), 0).astype(F32)
    return jnp.minimum(pos + 1.0, float(w))


def _pool_window(u, w, ext):
    ext[pl.ds(POOL_HALO, SEQ), :] = u
    win = u
    for j in range(1, w):
        win = win + ext[pl.ds(POOL_HALO - j, SEQ), :]
    return win / _pool_counts(w) - u


def _pool_fwd(zcat, w_grp, scale):
    def body(z_ref, w_ref, s_ref, o_ref, ext):
        ext[pl.ds(0, POOL_HALO), :] = jnp.zeros((POOL_HALO, POOL_GD), F32)
        for g, w in enumerate(POOL_WINDOWS):
            cols = slice(g * POOL_GD, (g + 1) * POOL_GD)
            p = _pool_window(z_ref[:, cols].astype(F32), w, ext)
            o_ref[:, cols] = (_dot(p, w_ref[g]) * s_ref[:, cols]).astype(BF)

    return pl.pallas_call(
        body, name="pool_fwd", grid=(1,),
        in_specs=[pl.BlockSpec((SEQ, POOL_WIDTH), lambda i: (0, C_POOL // POOL_WIDTH)),
                  _const_spec((4, POOL_GD, POOL_GD)), _const_spec((1, POOL_WIDTH))],
        out_specs=_const_spec((SEQ, POOL_WIDTH)), out_shape=jax.ShapeDtypeStruct((SEQ, POOL_WIDTH), BF),
        scratch_shapes=[pltpu.VMEM((POOL_HALO + SEQ, POOL_GD), F32)], compiler_params=_params("arbitrary"),
    )(*map(_in_hbm, (zcat, w_grp, scale)))


def _pool_bwd(dzcat, zcat, dps, w_grp, scale):
    def body(dz_in, z_ref, dps_ref, w_ref, s_ref, dz_ref, dw_ref, dsc_ref, ext, ext2):
        del dz_in
        ext[pl.ds(0, POOL_HALO), :] = jnp.zeros((POOL_HALO, POOL_GD), F32)
        ext2[pl.ds(SEQ, POOL_HALO), :] = jnp.zeros((POOL_HALO, POOL_GD), F32)
        for g, w in enumerate(POOL_WINDOWS):
            cols = slice(g * POOL_GD, (g + 1) * POOL_GD)
            p = _pool_window(z_ref[:, cols].astype(F32), w, ext)
            wg = w_ref[g]
            pg = _dot(p, wg)
            dpsv = dps_ref[:, cols]
            dsc_ref[:, cols] = jnp.sum(dpsv * pg, axis=0, keepdims=True)
            dpg = dpsv * s_ref[:, cols]
            dw_ref[g] = _dot(p, dpg, ta=True)
            dp = _dot(dpg, wg, tb=True)
            dpc = dp / _pool_counts(w)
            ext2[pl.ds(0, SEQ), :] = dpc
            du = dpc
            for j in range(1, w):
                du = du + ext2[pl.ds(j, SEQ), :]
            dz_ref[:, cols] = (du - dp).astype(BF)

    return pl.pallas_call(
        body, name="pool_bwd", grid=(1,),
        in_specs=[pl.BlockSpec(memory_space=pl.ANY),
                  pl.BlockSpec((SEQ, POOL_WIDTH), lambda i: (0, C_POOL // POOL_WIDTH)),
                  _const_spec((SEQ, POOL_WIDTH)), _const_spec((4, POOL_GD, POOL_GD)), _const_spec((1, POOL_WIDTH))],
        out_specs=[pl.BlockSpec((SEQ, POOL_WIDTH), lambda i: (0, C_POOL // POOL_WIDTH)),
                   _const_spec((4, POOL_GD, POOL_GD)), _const_spec((1, POOL_WIDTH))],
        out_shape=[jax.ShapeDtypeStruct((SEQ, N_CAT), BF), jax.ShapeDtypeStruct((4, POOL_GD, POOL_GD), F32),
                   jax.ShapeDtypeStruct((1, POOL_WIDTH), F32)],
        scratch_shapes=[pltpu.VMEM((POOL_HALO + SEQ, POOL_GD), F32), pltpu.VMEM((SEQ + POOL_HALO, POOL_GD), F32)],
        input_output_aliases={0: 0}, compiler_params=_params("arbitrary"),
    )(*map(_in_hbm, (dzcat, zcat, dps, w_grp, scale)))


GK_TILE = 512


def _gk_fwd(h, wt_gk, wgk_pad, b_gk):
    def body(h_ref, wt_ref, w_ref, b_ref, la_ref):
        z_gk = _dot(h_ref[...], wt_ref[...], tb=True)
        pre = _dot(z_gk, w_ref[...]) + b_ref[...]
        la_ref[...] = (jnp.minimum(pre, 0.0) - jnp.log(1.0 + jnp.exp(-jnp.abs(pre)))) * (1.0 / GATE_NORM)

    return pl.pallas_call(
        body, name="gk_fwd", grid=(SEQ // GK_TILE,),
        in_specs=[pl.BlockSpec((GK_TILE, D_MODEL), lambda i: (i, 0)), _const_spec((GK_PAD, D_MODEL)),
                  _const_spec((GK_PAD, GLA_DK)), _const_spec((1, GLA_DK))],
        out_specs=pl.BlockSpec((GK_TILE, GLA_DK), lambda i: (i, 0)),
        out_shape=jax.ShapeDtypeStruct((SEQ, GLA_DK), F32), compiler_params=_params("parallel"),
    )(*map(_in_hbm, (h, wt_gk, wgk_pad, b_gk)))


def _gk_bwd(dla, h, wt_gk, wgk_pad, b_gk):
    def body(dla_ref, h_ref, wt_ref, w_ref, b_ref, dh_ref, dwt_ref, dw_ref, db_ref):
        hv = h_ref[...]
        wtv = wt_ref[...]
        wv = w_ref[...]
        z_gk = _dot(hv, wtv, tb=True)
        pre = _dot(z_gk, wv) + b_ref[...]
        dpre = dla_ref[...] * (1.0 / GATE_NORM) * (1.0 - _sigmoid(pre))
        dz_gk = _dot(dpre, wv, tb=True)
        dh_ref[...] = _dot(dz_gk, wtv)
        dwtp = _dot(dz_gk, hv, ta=True)
        dwp = _dot(z_gk, dpre, ta=True)[:GATE_RANK]
        dbp = jnp.sum(dpre, axis=0, keepdims=True)

        @pl.when(pl.program_id(0) == 0)
        def _():
            dwt_ref[...] = dwtp
            dw_ref[...] = dwp
            db_ref[...] = dbp

        @pl.when(pl.program_id(0) > 0)
        def _():
            dwt_ref[...] += dwtp
            dw_ref[...] += dwp
            db_ref[...] += dbp

    tile = pl.BlockSpec((GK_TILE, D_MODEL), lambda i: (i, 0))
    return pl.pallas_call(
        body, name="gk_bwd", grid=(SEQ // GK_TILE,),
        in_specs=[pl.BlockSpec((GK_TILE, GLA_DK), lambda i: (i, 0)), tile, _const_spec((GK_PAD, D_MODEL)),
                  _const_spec((GK_PAD, GLA_DK)), _const_spec((1, GLA_DK))],
        out_specs=[tile, _const_spec((GK_PAD, D_MODEL)), _const_spec((GATE_RANK, GLA_DK)), _const_spec((1, GLA_DK))],
        out_shape=[jax.ShapeDtypeStruct((SEQ, D_MODEL), F32), jax.ShapeDtypeStruct((GK_PAD, D_MODEL), F32),
                   jax.ShapeDtypeStruct((GATE_RANK, GLA_DK), F32), jax.ShapeDtypeStruct((1, GLA_DK), F32)],
        compiler_params=_params("arbitrary"),
    )(*map(_in_hbm, (dla, h, wt_gk, wgk_pad, b_gk)))


GLA_ROWS = GLA_CPS * CHUNK
GLA_STEPS = SEQ // GLA_ROWS
QKV_W = 2048


def _tri():
    return lax.broadcasted_iota(jnp.int32, (CHUNK, CHUNK), 0) >= lax.broadcasted_iota(jnp.int32, (CHUNK, CHUNK), 1)


def _chunk_cumsum(la_ref, rows):
    return _dot_exact(_tri().astype(F32), la_ref[rows, :])


def _gla_chunk(qkv_ref, la_ref, rows, h, bc_all):
    tri = _tri()
    q = qkv_ref[rows, h * HK:(h + 1) * HK].astype(F32) * (HK ** -0.5)
    k = qkv_ref[rows, GLA_DK + h * HK:GLA_DK + (h + 1) * HK].astype(F32)
    v = qkv_ref[rows, 2 * GLA_DK + h * HV:2 * GLA_DK + (h + 1) * HV].astype(BF)
    la = la_ref[rows, h * HK:(h + 1) * HK]
    bc = bc_all[:, h * HK:(h + 1) * HK]
    e_pos, e_neg = jnp.exp(bc), jnp.exp(-bc)
    dl = jnp.exp(jnp.sum(la, axis=0, keepdims=True))
    q_fw, q_bw, k_fw, k_bw = q * e_pos, q * e_neg, k * e_neg, k * e_pos
    scores = jnp.where(tri, _dot(q_fw, k_fw, tb=True), _dot(q_bw, k_bw, tb=True))
    return tri, v, e_pos, e_neg, dl, q_fw, q_bw, k_fw, k_bw, scores


def _gla_fwd(zcat, la, after):
    def body(qkv_ref, la_ref, after_ref, o_ref, st_ref, state):
        del after_ref

        @pl.when(pl.program_id(0) == 0)
        def _():
            state[...] = jnp.zeros_like(state)

        for c in range(GLA_CPS):
            rows = slice(c * CHUNK, (c + 1) * CHUNK)
            bc_all = _chunk_cumsum(la_ref, rows)
            for h in range(HEADS):
                _, v, _, _, dl, q_fw, _, k_fw, _, scores = _gla_chunk(qkv_ref, la_ref, rows, h, bc_all)
                st = state[h]
                st_ref[c, h] = st
                o_ref[rows, h * HV:(h + 1) * HV] = _dot(scores, v) + _dot(q_fw, st, tb=True)
                state[h] = st * dl + _dot(v, k_fw * dl, ta=True)

    return pl.pallas_call(
        body, name="gla_fwd", grid=(GLA_STEPS,),
        in_specs=[pl.BlockSpec((GLA_ROWS, QKV_W), lambda i: (i, 0)), pl.BlockSpec((GLA_ROWS, GLA_DK), lambda i: (i, 0)),
                  pl.BlockSpec(memory_space=pl.ANY)],
        out_specs=[pl.BlockSpec((GLA_ROWS, D_MODEL), lambda i: (i, 0)),
                   pl.BlockSpec((GLA_CPS, HEADS, HV, HK), lambda i: (i, 0, 0, 0))],
        out_shape=[jax.ShapeDtypeStruct((SEQ, D_MODEL), F32),
                   jax.ShapeDtypeStruct((SEQ // CHUNK, HEADS, HV, HK), F32)],
        scratch_shapes=[pltpu.VMEM((HEADS, HV, HK), F32)], compiler_params=_params("arbitrary"),
    )(*map(_in_hbm, (zcat, la)), after)


def _gla_bwd(dzcat, zcat, la, d_o, states):
    def body(dz_in, qkv_ref, la_ref, do_ref, st_ref, dqkv_ref, dla_ref, dstate):
        del dz_in

        @pl.when(pl.program_id(0) == 0)
        def _():
            dstate[...] = jnp.zeros_like(dstate)

        last_row = lax.broadcasted_iota(jnp.int32, (CHUNK, HK), 0) == CHUNK - 1
        upper = (lax.broadcasted_iota(jnp.int32, (CHUNK, CHUNK), 0)
                 <= lax.broadcasted_iota(jnp.int32, (CHUNK, CHUNK), 1)).astype(F32)
        for c in reversed(range(GLA_CPS)):
            rows = slice(c * CHUNK, (c + 1) * CHUNK)
            bc_all = _chunk_cumsum(la_ref, rows)
            dbs = []
            for h in range(HEADS):
                tri, v, e_pos, e_neg, dl, q_fw, q_bw, k_fw, k_bw, scores = _gla_chunk(qkv_ref, la_ref, rows, h, bc_all)
                st = st_ref[c, h]
                dst = dstate[h]
                d_out = do_ref[rows, h * HV:(h + 1) * HV].astype(BF)
                k_dec = k_fw * dl
                dp = _dot(d_out, v, tb=True)
                dp_fw = jnp.where(tri, dp, 0.0)
                dp_bw = jnp.where(tri, 0.0, dp)
                dv = _dot(scores, d_out, ta=True) + _dot(k_dec, dst, tb=True)
                dk_dec = _dot(v, dst)
                dq_fw = _dot(dp_fw, k_fw) + _dot(d_out, st)
                dk_fw = _dot(dp_fw, q_fw, ta=True) + dk_dec * dl
                dq_bw = _dot(dp_bw, k_bw)
                dk_bw = _dot(dp_bw, q_bw, ta=True)
                ddl = jnp.sum(st * dst, axis=0, keepdims=True) + jnp.sum(k_fw * dk_dec, axis=0, keepdims=True)
                dstate[h] = dst * dl + _dot(d_out, q_fw, ta=True)
                dq = (dq_fw * e_pos + dq_bw * e_neg) * (HK ** -0.5)
                dk = dk_fw * e_neg + dk_bw * e_pos
                dbs.append(dq_fw * q_fw - dk_fw * k_fw - dq_bw * q_bw + dk_bw * k_bw + jnp.where(last_row, ddl * dl, 0.0))
                dqkv_ref[rows, h * HK:(h + 1) * HK] = dq.astype(BF)
                dqkv_ref[rows, GLA_DK + h * HK:GLA_DK + (h + 1) * HK] = dk.astype(BF)
                dqkv_ref[rows, 2 * GLA_DK + h * HV:2 * GLA_DK + (h + 1) * HV] = dv.astype(BF)
            dla_ref[rows, :] = _dot_exact(upper, jnp.concatenate(dbs, axis=1))

    rev = lambda i: (GLA_STEPS - 1 - i, 0)
    return pl.pallas_call(
        body, name="gla_bwd", grid=(GLA_STEPS,),
        in_specs=[pl.BlockSpec(memory_space=pl.ANY), pl.BlockSpec((GLA_ROWS, QKV_W), rev),
                  pl.BlockSpec((GLA_ROWS, GLA_DK), rev), pl.BlockSpec((GLA_ROWS, D_MODEL), rev),
                  pl.BlockSpec((GLA_CPS, HEADS, HV, HK), lambda i: (GLA_STEPS - 1 - i, 0, 0, 0))],
        out_specs=[pl.BlockSpec((GLA_ROWS, QKV_W), rev), pl.BlockSpec((GLA_ROWS, GLA_DK), rev)],
        out_shape=[jax.ShapeDtypeStruct((SEQ, N_CAT), BF), jax.ShapeDtypeStruct((SEQ, GLA_DK), F32)],
        scratch_shapes=[pltpu.VMEM((HEADS, HV, HK), F32)], input_output_aliases={0: 0},
        compiler_params=_params("arbitrary"),
    )(*map(_in_hbm, (dzcat, zcat, la, d_o, states)))


def _silu_parts(x):
    s = _sigmoid(x)
    return x * s, s * (1.0 + x * (1.0 - s))


def _post_gla_fwd(o, zcat, g_head):
    def body(o_ref, zog_ref, g_ref, out_ref):
        for h in range(HEADS):
            cols = slice(h * HV, (h + 1) * HV)
            ov = o_ref[:, cols]
            r = lax.rsqrt(jnp.mean(ov * ov, axis=-1, keepdims=True) + EPS)
            act, _ = _silu_parts(zog_ref[:, cols].astype(F32))
            out_ref[:, cols] = (ov * r * g_ref[...] * act).astype(BF)

    tile = pl.BlockSpec((TOK_TILE, D_MODEL), lambda i: (i, 0))
    return pl.pallas_call(
        body, name="post_gla_fwd", grid=(SEQ // TOK_TILE,),
        in_specs=[tile, pl.BlockSpec((TOK_TILE, D_MODEL), lambda i: (i, C_OG // D_MODEL)), _const_spec((1, HV))],
        out_specs=tile, out_shape=jax.ShapeDtypeStruct((SEQ, D_MODEL), BF), compiler_params=_params("parallel"),
    )(*map(_in_hbm, (o, zcat, g_head)))


def _post_gla_bwd(dzcat, dy_gla, w_gla_proj, o, zcat, g_head):
    def body(dz_in, dyg_ref, w_ref, o_ref, zog_ref, g_ref, dz_ref, do_ref, dg_ref):
        del dz_in
        dog = _dot(dyg_ref[...], w_ref[...], tb=True)
        gpart = jnp.zeros((1, HV), F32)
        gv = g_ref[...]
        for h in range(HEADS):
            cols = slice(h * HV, (h + 1) * HV)
            ov = o_ref[:, cols]
            r = lax.rsqrt(jnp.mean(ov * ov, axis=-1, keepdims=True) + EPS)
            on = ov * r
            act, dact = _silu_parts(zog_ref[:, cols].astype(F32))
            dogv = dog[:, cols]
            dz_ref[:, cols] = (dogv * on * gv * dact).astype(BF)
            d_on_g = dogv * act
            gpart = gpart + jnp.sum(d_on_g * on, axis=0, keepdims=True)
            dxn = d_on_g * gv
            do_ref[:, cols] = (r * (dxn - on * jnp.mean(dxn * on, axis=-1, keepdims=True))).astype(BF)

        @pl.when(pl.program_id(0) == 0)
        def _():
            dg_ref[...] = gpart

        @pl.when(pl.program_id(0) > 0)
        def _():
            dg_ref[...] += gpart

    tile = pl.BlockSpec((TOK_TILE, D_MODEL), lambda i: (i, 0))
    ogspec = pl.BlockSpec((TOK_TILE, D_MODEL), lambda i: (i, C_OG // D_MODEL))
    return pl.pallas_call(
        body, name="post_gla_bwd", grid=(SEQ // TOK_TILE,),
        in_specs=[pl.BlockSpec(memory_space=pl.ANY), tile, _const_spec((D_MODEL, D_MODEL)), tile, ogspec,
                  _const_spec((1, HV))],
        out_specs=[ogspec, tile, _const_spec((1, HV))],
        out_shape=[jax.ShapeDtypeStruct((SEQ, N_CAT), BF), jax.ShapeDtypeStruct((SEQ, D_MODEL), BF),
                   jax.ShapeDtypeStruct((1, HV), F32)],
        input_output_aliases={0: 0}, compiler_params=_params("arbitrary"),
    )(*map(_in_hbm, (dzcat, dy_gla, w_gla_proj, o, zcat, g_head)))


GATE_W = 2 * D_MODEL


def _mix_out_fwd(ps, og, zcat, x, w_pool_proj, w_gla_proj, w_out, b_gate, g_ffn, after):
    def body(ps_ref, og_ref, zg_ref, x_ref, wpp_ref, wgp_ref, wout_ref, b_ref, g_ref, after_ref,
             yp_ref, yg_ref, mixed_ref, x1_ref, h2_ref):
        del after_ref
        y_pool = _dot(ps_ref[...], wpp_ref[...])
        y_gla = _dot(og_ref[...], wgp_ref[...])
        yp_ref[...] = y_pool.astype(BF)
        yg_ref[...] = y_gla.astype(BF)
        g0 = _sigmoid(zg_ref[:, :D_MODEL].astype(F32) + b_ref[:, :D_MODEL])
        g1 = _sigmoid(zg_ref[:, D_MODEL:].astype(F32) + b_ref[:, D_MODEL:])
        mixed = (g0 * y_pool + g1 * y_gla).astype(BF)
        mixed_ref[...] = mixed
        x1 = x_ref[...] + _dot(mixed, wout_ref[...])
        x1_ref[...] = x1
        r = lax.rsqrt(jnp.mean(x1 * x1, axis=-1, keepdims=True) + EPS)
        h2_ref[...] = (x1 * r * g_ref[...]).astype(BF)

    tile = pl.BlockSpec((TOK_TILE, D_MODEL), lambda i: (i, 0))
    resident = lambda shape: pl.BlockSpec(shape, lambda i: (0, 0), pipeline_mode=pl.Buffered(1))
    f32, bf16 = jax.ShapeDtypeStruct((SEQ, D_MODEL), F32), jax.ShapeDtypeStruct((SEQ, D_MODEL), BF)
    return pl.pallas_call(
        body, name="mix_out_fwd", grid=(SEQ // TOK_TILE,),
        in_specs=[pl.BlockSpec((TOK_TILE, POOL_WIDTH), lambda i: (i, 0)), tile,
                  pl.BlockSpec((TOK_TILE, GATE_W), lambda i: (i, C_GATE // GATE_W)), tile,
                  resident((POOL_WIDTH, D_MODEL)), resident((D_MODEL, D_MODEL)), resident((D_MODEL, D_MODEL)),
                  _const_spec((1, GATE_W)), _const_spec((1, D_MODEL)), pl.BlockSpec(memory_space=pl.ANY)],
        out_specs=[tile] * 5, out_shape=[bf16, bf16, bf16, f32, bf16], compiler_params=_params("parallel"),
    )(*map(_in_hbm, (ps, og, zcat, x, w_pool_proj, w_gla_proj, w_out, b_gate, g_ffn)), after)


def _mix_bwd(dx1, w_out, zcat, b_gate, y_pool, y_gla):
    def body(dx_ref, w_ref, zg_ref, b_ref, yp_ref, yg_ref, dz_ref, dyp_ref, dyg_ref, db_ref):
        dm = _dot(dx_ref[...], w_ref[...], tb=True)
        g0 = _sigmoid(zg_ref[:, :D_MODEL].astype(F32) + b_ref[:, :D_MODEL])
        g1 = _sigmoid(zg_ref[:, D_MODEL:].astype(F32) + b_ref[:, D_MODEL:])
        dyp_ref[...] = (dm * g0).astype(BF)
        dyg_ref[...] = (dm * g1).astype(BF)
        dz0 = dm * yp_ref[...].astype(F32) * g0 * (1.0 - g0)
        dz1 = dm * yg_ref[...].astype(F32) * g1 * (1.0 - g1)
        dz_ref[:, :D_MODEL] = dz0.astype(BF)
        dz_ref[:, D_MODEL:] = dz1.astype(BF)
        b0 = jnp.sum(dz0, axis=0, keepdims=True)
        b1 = jnp.sum(dz1, axis=0, keepdims=True)

        @pl.when(pl.program_id(0) == 0)
        def _():
            db_ref[:, :D_MODEL] = b0
            db_ref[:, D_MODEL:] = b1

        @pl.when(pl.program_id(0) > 0)
        def _():
            db_ref[:, :D_MODEL] += b0
            db_ref[:, D_MODEL:] += b1

    tile = pl.BlockSpec((TOK_TILE, D_MODEL), lambda i: (i, 0))
    gspec = pl.BlockSpec((TOK_TILE, GATE_W), lambda i: (i, C_GATE // GATE_W))
    return pl.pallas_call(
        body, name="mix_bwd", grid=(SEQ // TOK_TILE,),
        in_specs=[tile, _const_spec((D_MODEL, D_MODEL)), gspec, _const_spec((1, GATE_W)), tile, tile],
        out_specs=[gspec, tile, tile, _const_spec((1, GATE_W))],
        out_shape=[jax.ShapeDtypeStruct((SEQ, N_CAT), BF), jax.ShapeDtypeStruct((SEQ, D_MODEL), BF),
                   jax.ShapeDtypeStruct((SEQ, D_MODEL), BF), jax.ShapeDtypeStruct((1, GATE_W), F32)],
        compiler_params=_params("arbitrary"),
    )(*map(_in_hbm, (dx1, w_out, zcat, b_gate, y_pool, y_gla)))


N_TOK_TILES = SEQ // TOK_TILE
HALO_PER_TILE = TOK_TILE // HALO


LANE_TILES = tuple((lo, min(128, FF_BLK - lo)) for lo in range(0, FF_BLK, 128))


def _taps(w_ref, b_ref, half, lanes, rows):
    shape = (rows, lanes.stop - lanes.start)
    return ([jnp.broadcast_to(w_ref[half, j:j + 1, lanes], shape) for j in range(3)],
            jnp.broadcast_to(b_ref[half, :, lanes], shape))


def _conv_strips(u_ref, ub_ref, ua_ref, taps, lanes, width, n_strips, first):
    row = lax.broadcasted_iota(jnp.int32, (HALO, width), 0)
    prev = [[pltpu.roll(jnp.where(first, 0.0, ub_ref[half, :, lanes]), k, 0) for k in (1, 2)] for half in range(2)]
    for s in range(n_strips + (ua_ref is not None)):
        u3, conv = [], []
        for half in range(2):
            cur = u_ref[half, s * HALO:(s + 1) * HALO, lanes] if s < n_strips else ua_ref[half, :, lanes]
            rolled = [pltpu.roll(cur, k, 0) for k in (1, 2)]
            frames = [jnp.where(row >= 2, rolled[1], prev[half][1]), jnp.where(row >= 1, rolled[0], prev[half][0]), cur]
            prev[half] = rolled
            w3, bias = taps[half]
            u3.append(frames)
            conv.append(bias + frames[0] * w3[0] + frames[1] * w3[1] + frames[2] * w3[2])
        yield s, u3, conv


def _pair_specs(pairs):
    tile = pl.BlockSpec((pairs, None, TOK_TILE, FF_BLK), lambda b, i: (0, b, i, 0))
    before = pl.BlockSpec((pairs, None, HALO, FF_BLK), lambda b, i: (0, b, jnp.maximum(i * HALO_PER_TILE - 1, 0), 0))
    after = pl.BlockSpec((pairs, None, HALO, FF_BLK),
                         lambda b, i: (0, b, jnp.minimum((i + 1) * HALO_PER_TILE, SEQ // HALO - 1), 0))

    def vec(rows):
        return pl.BlockSpec((2, None, rows, FF_BLK), lambda b, i: (0, b, 0, 0))

    return tile, before, after, vec


N_STRIPS = TOK_TILE // HALO


def _up_conv_fwd(h2, wt_up, w_conv, b_conv):
    steps = N_TOK_TILES // 2

    def body(h_ref, h_next, wg_ref, wv_ref, w_ref, b_ref, u_ref, a_ref, buf_a, buf_b, carry):
        j = pl.program_id(1)

        def project(hv, buf):
            buf[0] = _dot(hv, wg_ref[...], tb=True)
            buf[1] = _dot(hv, wv_ref[...], tb=True)

        def conv(buf, row0):
            u_ref[:, row0:row0 + TOK_TILE, :] = buf[...]
            for lo, width in LANE_TILES:
                lanes = slice(lo, lo + width)
                taps = [_taps(w_ref, b_ref, half, lanes, HALO) for half in range(2)]
                pending = None
                for s, _, (cg, cv) in _conv_strips(buf, carry, None, taps, lanes, width, N_STRIPS, False):
                    act = cg * _sigmoid(cg) * cv
                    if s % 2 == 0:
                        pending = act
                    else:
                        a_ref[0, row0 + (s - 1) * HALO:row0 + (s + 1) * HALO, lanes] = (
                            jnp.concatenate([pending, act], axis=0).astype(BF))
            carry[...] = buf[:, TOK_TILE - HALO:, :]

        @pl.when(j == 0)
        def _():
            project(h_ref[0:TOK_TILE, :], buf_a)
            carry[...] = jnp.zeros_like(carry)

        project(h_ref[TOK_TILE:, :], buf_b)
        conv(buf_a, 0)
        project(h_next[...], buf_a)
        conv(buf_b, TOK_TILE)

    w_blk = lambda half: pl.BlockSpec((FF_BLK, D_MODEL), lambda b, j: (b + 4 * half, 0))
    vec = lambda rows: pl.BlockSpec((2, None, rows, FF_BLK), lambda b, j: (0, b, 0, 0))
    u_buf = pltpu.VMEM((2, TOK_TILE, FF_BLK), F32)
    return pl.pallas_call(
        body, name="up_conv_fwd", grid=(4, steps),
        in_specs=[pl.BlockSpec((2 * TOK_TILE, D_MODEL), lambda b, j: (j, 0)),
                  pl.BlockSpec((TOK_TILE, D_MODEL), lambda b, j: (jnp.minimum(2 * j + 2, N_TOK_TILES - 1), 0)),
                  w_blk(0), w_blk(1), vec(3), vec(1)],
        out_specs=[pl.BlockSpec((2, None, 2 * TOK_TILE, FF_BLK), lambda b, j: (0, b, j, 0)),
                   pl.BlockSpec((1, None, 2 * TOK_TILE, FF_BLK), lambda b, j: (0, b, j, 0))],
        out_shape=[jax.ShapeDtypeStruct((2, 4, SEQ, FF_BLK), F32), jax.ShapeDtypeStruct((1, 4, SEQ, FF_BLK), BF)],
        scratch_shapes=[u_buf, u_buf, pltpu.VMEM((2, HALO, FF_BLK), F32)],
        compiler_params=_params("parallel", "arbitrary"),
    )(*map(_in_hbm, (h2, h2, wt_up, wt_up, w_conv, b_conv)))


def _conv_bwd(u, da, w_conv, b_conv):
    def body(u_ref, ub_ref, ua_ref, da_ref, daa_ref, w_ref, b_ref, du_ref, dw_ref, db_ref):
        i = pl.program_id(1)

        @pl.when(i == 0)
        def _():
            dw_ref[...] = jnp.zeros_like(dw_ref)
            db_ref[...] = jnp.zeros_like(db_ref)

        for lo, width in LANE_TILES:
            lanes = slice(lo, lo + width)
            row = lax.broadcasted_iota(jnp.int32, (HALO, width), 0)
            taps = [_taps(w_ref, b_ref, half, lanes, HALO) for half in range(2)]
            acc_w = [[jnp.zeros((HALO, width), F32) for _ in range(3)] for _ in range(2)]
            acc_b = [jnp.zeros((HALO, width), F32) for _ in range(2)]
            da_pair, pending = None, [None, None]
            dc_prev, up_prev = [None, None], [None, None]
            for s, u3, (cg, cv) in _conv_strips(u_ref, ub_ref, ua_ref, taps, lanes, width, N_STRIPS, i == 0):
                act, dact = _silu_parts(cg)
                if s == N_STRIPS:
                    da = jnp.where(i < N_TOK_TILES - 1, daa_ref[0, :, lanes].astype(F32), 0.0)
                elif s % 2 == 0:
                    da_pair = da_ref[0, s * HALO:(s + 2) * HALO, lanes].astype(F32)
                    da = da_pair[:HALO]
                else:
                    da = da_pair[HALO:]
                dc = (da * cv * dact, da * act)
                for half in range(2):
                    up = [pltpu.roll(dc[half], HALO - k, 0) for k in (1, 2)]
                    if s < N_STRIPS:
                        for j in range(3):
                            acc_w[half][j] = acc_w[half][j] + dc[half] * u3[half][j]
                        acc_b[half] = acc_b[half] + dc[half]
                    if s >= 1:
                        w3 = taps[half][0]
                        du = (dc_prev[half] * w3[2] + jnp.where(row < HALO - 1, up_prev[half][0], up[0]) * w3[1]
                              + jnp.where(row < HALO - 2, up_prev[half][1], up[1]) * w3[0])
                        if (s - 1) % 2 == 0:
                            pending[half] = du
                        else:
                            du_ref[half, (s - 2) * HALO:s * HALO, lanes] = jnp.concatenate([pending[half], du],
                                                                                           axis=0).astype(BF)
                    dc_prev[half], up_prev[half] = dc[half], up
            for half in range(2):
                for j in range(3):
                    dw_ref[half, j:j + 1, lanes] += jnp.sum(acc_w[half][j], axis=0, keepdims=True)
                db_ref[half, :, lanes] += jnp.sum(acc_b[half], axis=0, keepdims=True)

    tile, before, after, vec = _pair_specs(2)
    da_tile, _, da_after_spec, _ = _pair_specs(1)
    return pl.pallas_call(
        body, name="conv_bwd", grid=(4, N_TOK_TILES),
        in_specs=[tile, before, after, da_tile, da_after_spec, vec(3), vec(1)],
        out_specs=[tile, vec(3), vec(1)],
        out_shape=[jax.ShapeDtypeStruct((2, 4, SEQ, FF_BLK), BF), jax.ShapeDtypeStruct((2, 4, 3, FF_BLK), F32),
                   jax.ShapeDtypeStruct((2, 4, 1, FF_BLK), F32)],
        compiler_params=_params("parallel", "arbitrary"),
    )(*map(_in_hbm, (u, u, u, da, da, w_conv, b_conv)))


W_IN_SEGMENTS = ((R_POOL, POOL_WIDTH, "cat", C_POOL), (R_QKV, QKV_W, "cat", C_QKV), (R_OG, D_MODEL, "cat", C_OG),
                 (R_GK, GATE_RANK, "gk", 0), (R_GATE, GATE_W, "cat", C_GATE))


def _slab_pieces(d):
    lo, hi = d * IN_SHARD, (d + 1) * IN_SHARD
    pieces = []
    for start, n, dest, at in W_IN_SEGMENTS:
        a, b = max(lo, start), min(hi, start + n)
        if a < b:
            assert (a - lo) % 2 == 0 and (b - a) % 2 == 0 and (at + a - start) % 2 == 0
            pieces.append(((a - lo) // 2, (b - a) // 2, dest, (at + a - start) // 2))
    return pieces


def _unshard_w_in(slabs):
    def body(slab_ref, cat_ref, gk_ref):
        d = pl.program_id(0)
        src = slab_ref.bitcast(jnp.uint32)
        dst = dict(cat=cat_ref.bitcast(jnp.uint32), gk=gk_ref.bitcast(jnp.uint32))

        @pl.when(d == 0)
        def _():
            gk_ref[...] = jnp.zeros_like(gk_ref)

        for dd in range(N_DEV):
            @pl.when(d == dd)
            def _():
                for a, n, dest, at in _slab_pieces(dd):
                    dst[dest][pl.ds(at, n), :] = src[0, pl.ds(a, n), :]

    return pl.pallas_call(
        body, name="unshard_w_in", grid=(N_DEV,),
        in_specs=[pl.BlockSpec((1, IN_SHARD, D_MODEL), lambda d: (d, 0, 0))],
        out_specs=[_const_spec((N_CAT, D_MODEL)), _const_spec((GK_PAD, D_MODEL))],
        out_shape=[jax.ShapeDtypeStruct((N_CAT, D_MODEL), BF), jax.ShapeDtypeStruct((GK_PAD, D_MODEL), BF)],
        compiler_params=_params("arbitrary"),
    )(_in_hbm(slabs))


def _shard_d_w_in(d_cat, d_gk):
    def body(cat_ref, gk_ref, slab_ref):
        d = pl.program_id(0)
        cat = cat_ref.bitcast(jnp.uint32)
        gk = pltpu.bitcast(gk_ref[0:GATE_RANK, :].astype(BF), jnp.uint32)
        dst = slab_ref.bitcast(jnp.uint32)
        for dd in range(N_DEV):
            @pl.when(d == dd)
            def _():
                for a, n, source, at in _slab_pieces(dd):
                    dst[0, pl.ds(a, n), :] = gk[at:at + n] if source == "gk" else cat[pl.ds(at, n), :]

    return pl.pallas_call(
        body, name="shard_d_w_in", grid=(N_DEV,),
        in_specs=[_const_spec((N_CAT, D_MODEL)), _const_spec((GK_PAD, D_MODEL))],
        out_specs=pl.BlockSpec((1, IN_SHARD, D_MODEL), lambda d: (d, 0, 0)),
        out_shape=jax.ShapeDtypeStruct((N_DEV, IN_SHARD, D_MODEL), BF), compiler_params=_params("parallel"),
    )(_in_hbm(d_cat), _in_hbm(d_gk))


ANY = pl.BlockSpec(memory_space=pl.ANY)


def _place():
    x, y, c = lax.axis_index("x"), lax.axis_index("y"), lax.axis_index("c")
    other_chips = [(1 - x, y), (x, 1 - y), (1 - x, 1 - y)]
    return x, y, c, other_chips


SEM = pl.BlockSpec(memory_space=pltpu.SEMAPHORE)
IN_HBM = pl.BlockSpec(memory_space=pltpu.HBM)
SPLIT_PARAMS = pltpu.CompilerParams(has_side_effects=pltpu.SideEffectType.DATAFLOW_SIDE_EFFECTING)


def _gather_first(refs, send_sems, recv_sems):
    x, y, c, chips = _place()
    targets = [(x, y, 1 - c)] + [(px, py, c) for px, py in chips]
    return [pltpu.make_async_remote_copy(src_ref=refs[2 * a], dst_ref=refs[2 * a + 1].at[4 * x + 2 * y + c],
                                         send_sem=send_sems.at[4 * a + k], recv_sem=recv_sems.at[4 * a + k],
                                         device_id=to, device_id_type=MESH)
            for a in range(len(refs) // 2) for k, to in enumerate(targets)]


def _gather_direct(refs, send_sems, recv_sems):
    x, y, c, _ = _place()
    flips = [(dx, dy, dc) for dx in (0, 1) for dy in (0, 1) for dc in (0, 1) if dx + dy + dc]
    targets = [(1 - x if dx else x, 1 - y if dy else y, 1 - c if dc else c) for dx, dy, dc in flips]
    return [pltpu.make_async_remote_copy(src_ref=refs[2 * a], dst_ref=refs[2 * a + 1].at[4 * x + 2 * y + c],
                                         send_sem=send_sems.at[7 * a + k], recv_sem=recv_sems.at[7 * a + k],
                                         device_id=to, device_id_type=MESH)
            for a in range(len(refs) // 2) for k, to in enumerate(targets)]


def _gather_second(refs, send_sems, recv_sems):
    x, y, c, chips = _place()
    copies = []
    for a, land in enumerate(refs):
        for j, (px, py) in enumerate(chips):
            block = land.at[4 * px + 2 * py + c]
            copies.append(pltpu.make_async_remote_copy(src_ref=block, dst_ref=block, send_sem=send_sems.at[3 * a + j],
                                                       recv_sem=recv_sems.at[3 * a + j], device_id=(x, y, 1 - c),
                                                       device_id_type=MESH))
    return copies


def _reduce_first(refs, send_sems, recv_sems):
    x, y, c, _ = _place()
    return [pltpu.make_async_remote_copy(src_ref=refs[2 * a].at[j, 1 - c], dst_ref=refs[2 * a + 1].at[j],
                                         send_sem=send_sems.at[4 * a + j], recv_sem=recv_sems.at[4 * a + j],
                                         device_id=(x, y, 1 - c), device_id_type=MESH)
            for a in range(len(refs) // 2) for j in range(4)]


def _reduce_second(refs, send_sems, recv_sems):
    _, _, c, chips = _place()
    return [pltpu.make_async_remote_copy(src_ref=refs[2 * a].at[2 * px + py], dst_ref=refs[2 * a + 1].at[k],
                                         send_sem=send_sems.at[3 * a + k], recv_sem=recv_sems.at[3 * a + k],
                                         device_id=(px, py, c), device_id_type=MESH)
            for a in range(len(refs) // 2) for k, (px, py) in enumerate(chips)]


def _split_start(name, groups):
    arrays = [a for g in groups for a in g[0]]
    n = len(arrays)

    def body(*refs):
        sems = refs[n:n + 2 * len(groups)]
        at = 0
        for gi, (members, _, build) in enumerate(groups):
            for cp in build(refs[at:at + len(members)], sems[2 * gi], sems[2 * gi + 1]):
                cp.start()
            at += len(members)
        refs[-1][...] = jnp.zeros_like(refs[-1])

    sem_shapes = [pltpu.SemaphoreType.DMA((g[1],)) for g in groups for _ in range(2)]
    outs = pl.pallas_call(
        body, name=name, in_specs=[IN_HBM] * n,
        out_shape=(*sem_shapes, *[pltpu.HBM(a.shape, a.dtype) for a in arrays], jax.ShapeDtypeStruct((8, 128), F32)),
        out_specs=(*[SEM] * len(sem_shapes), *[IN_HBM] * n, pl.BlockSpec(memory_space=pltpu.VMEM)),
        input_output_aliases={i: len(sem_shapes) + i for i in range(n)}, compiler_params=SPLIT_PARAMS,
    )(*[pltpu.with_memory_space_constraint(a, pltpu.HBM) for a in arrays])
    per_group, at = [], len(sem_shapes)
    for gi, (members, _, _) in enumerate(groups):
        per_group.append((outs[2 * gi], outs[2 * gi + 1], list(outs[at:at + len(members)])))
        at += len(members)
    return per_group, outs[-1]


def _split_wait(name, started, build, after):
    send_sems, recv_sems, arrays = started
    n = len(arrays)
    after = after if isinstance(after, (tuple, list)) else (after,)

    def body(*refs):
        for cp in build(refs[:n], refs[n], refs[n + 1]):
            cp.wait_send()
            cp.wait_recv()

    return pl.pallas_call(
        body, name=name, in_specs=[IN_HBM] * n + [SEM, SEM] + [ANY] * len(after),
        out_shape=tuple(pltpu.HBM(a.shape, a.dtype) for a in arrays), out_specs=tuple([IN_HBM] * n),
        input_output_aliases={i: i for i in range(n)}, compiler_params=SPLIT_PARAMS,
    )(*arrays, send_sems, recv_sems, *after)


def _gather_landing(shard, me):
    return lax.dynamic_update_slice(lax.empty((N_DEV,) + shard.shape, shard.dtype), shard[None],
                                    (me,) + (0,) * shard.ndim)


def _tile_2d(rows, cols):
    for t in (256, 176, 128):
        if rows % t == 0:
            return t, cols
    return rows, 256


def _pair_sum(part, recv, core, name):
    _, rows, cols = recv.shape
    tr, tc = rows, cols

    def body(c_ref, p_ref, r_ref, o_ref):
        del c_ref
        o_ref[...] = (p_ref[...].astype(F32) + r_ref[...].astype(F32)).astype(BF)

    grid_spec = pltpu.PrefetchScalarGridSpec(
        num_scalar_prefetch=1, grid=(4, rows // tr, cols // tc),
        in_specs=[pl.BlockSpec((None, None, tr, tc), lambda j, i, k, c_ref: (j, c_ref[0], i, k)),
                  pl.BlockSpec((None, tr, tc), lambda j, i, k, c_ref: (j, i, k))],
        out_specs=pl.BlockSpec((None, tr, tc), lambda j, i, k, c_ref: (j, i, k)))
    return pl.pallas_call(
        body, name=name, grid_spec=grid_spec, out_shape=jax.ShapeDtypeStruct(recv.shape, BF),
        compiler_params=_params("parallel", "parallel", "parallel"),
    )(core, *map(_in_hbm, (part, recv)))


def _adamw(w, g, m, v):
    m = ADAM_B1 * m + (1.0 - ADAM_B1) * g
    v = ADAM_B2 * v + (1.0 - ADAM_B2) * (g * g)
    delta = -ADAM_LR * ((m / ADAM_C1) / (jnp.sqrt(v / ADAM_C2) + ADAM_EPS) + ADAM_WD * w)
    return delta, m, v


def _chip_sum_adamw(sums, recv, w, m, v, chip, name):
    rows, cols = w.shape
    tr, tc = _tile_2d(rows, cols)

    def body(chip_ref, s_ref, r_ref, w_ref, m_ref, v_ref, g_out, d_out, m_out, v_out):
        del chip_ref
        g = s_ref[...].astype(F32)
        for k in range(3):
            g = g + r_ref[k].astype(F32)
        g_out[...] = g
        d_out[...], m_out[...], v_out[...] = _adamw(w_ref[...], g, m_ref[...], v_ref[...])

    tile = pl.BlockSpec((tr, tc), lambda i, k, chip_ref: (i, k))
    grid_spec = pltpu.PrefetchScalarGridSpec(
        num_scalar_prefetch=1, grid=(rows // tr, cols // tc),
        in_specs=[pl.BlockSpec((None, tr, tc), lambda i, k, chip_ref: (chip_ref[0], i, k)),
                  pl.BlockSpec((3, tr, tc), lambda i, k, chip_ref: (0, i, k)), tile, tile, tile],
        out_specs=[tile] * 4)
    return pl.pallas_call(
        body, name=name, grid_spec=grid_spec, out_shape=[jax.ShapeDtypeStruct((rows, cols), F32)] * 4,
        compiler_params=_params("parallel", "parallel"),
    )(chip, *map(_in_hbm, (sums, recv, w, m, v)))


def _small_sum_adamw(me, entries, loss_parts):
    def whole(shape, squeeze=0, pick=False):
        blk = (None,) * squeeze + tuple(shape[squeeze:])
        if pick:
            blk = (shape[0], None) + tuple(shape[2:])
            return pl.BlockSpec(blk, lambda i, me_ref: (0, me_ref[0]) + (0,) * (len(shape) - 2))
        return pl.BlockSpec(blk, lambda i, me_ref: (0,) * len(shape))

    in_specs, out_specs, out_shape, args = [], [], [], []
    for parts, w, m, v, sharded in entries:
        lead = w.ndim - (parts.ndim - (2 if sharded else 1))
        in_specs += [whole(parts.shape, pick=sharded)] + [whole(w.shape, squeeze=lead)] * 3
        out_specs += [whole(w.shape, squeeze=lead)] * 4
        out_shape += [jax.ShapeDtypeStruct(w.shape, F32)] * 4
        args += [parts, w, m, v]
    in_specs.append(whole(loss_parts.shape))
    out_specs.append(whole(loss_parts.shape[1:]))
    out_shape.append(jax.ShapeDtypeStruct(loss_parts.shape[1:], F32))
    n = len(entries)

    def added(p_ref):
        total = p_ref[0]
        for d in range(1, N_DEV):
            total = total + p_ref[d]
        return total

    def body(me_ref, *refs):
        del me_ref
        ins, outs = refs[:4 * n + 1], refs[4 * n + 1:]
        for e in range(n):
            p_ref, w_ref, m_ref, v_ref = ins[4 * e:4 * e + 4]
            g_out, d_out, m_out, v_out = outs[4 * e:4 * e + 4]
            g = added(p_ref)
            g_out[...] = g
            d_out[...], m_out[...], v_out[...] = _adamw(w_ref[...], g, m_ref[...], v_ref[...])
        outs[4 * n][...] = added(ins[4 * n])

    grid_spec = pltpu.PrefetchScalarGridSpec(num_scalar_prefetch=1, grid=(1,), in_specs=in_specs, out_specs=out_specs)
    outs = pl.pallas_call(body, name="small_sum_adamw", grid_spec=grid_spec, out_shape=out_shape,
                          compiler_params=_params("arbitrary"))(me, *map(_in_hbm, args + [loss_parts]))
    return [outs[4 * e:4 * e + 4] for e in range(n)], outs[4 * n]


MM_TILE = 512
N_MM_TILES = SEQ // MM_TILE
CAT_TILE = 512
N_CAT_TILES = N_CAT // CAT_TILE


def kernel(x, g_mix, w_in, b_gate, w_gk_up, b_gk, w_pool_grp, pool_scale, g_gla_head, w_pool_proj, w_gla_proj, w_out, g_ffn, w_up, w_conv, b_conv, w_down, g_final, loss_target, m_g_mix, m_w_in, m_b_gate, m_w_gk_up, m_b_gk, m_w_pool_grp, m_pool_scale, m_g_gla_head, m_w_pool_proj, m_w_gla_proj, m_w_out, m_g_ffn, m_w_up, m_w_conv, m_b_conv, m_w_down, m_g_final, v_g_mix, v_w_in, v_b_gate, v_w_gk_up, v_b_gk, v_w_pool_grp, v_pool_scale, v_g_gla_head, v_w_pool_proj, v_w_gla_proj, v_w_out, v_g_ffn, v_w_up, v_w_conv, v_b_conv, v_w_down, v_g_final):
    xi, yi, ci = lax.axis_index("x"), lax.axis_index("y"), lax.axis_index("c")
    me = 4 * xi + 2 * yi + ci
    core = jnp.reshape(ci, (1,)).astype(jnp.int32)
    chip = jnp.reshape(2 * xi + yi, (1,)).astype(jnp.int32)
    xs, target = x[0], loss_target[0]

    big = dict(w_in=w_in[0].T, w_pool_proj=w_pool_proj[0], w_gla_proj=w_gla_proj[0], w_out=w_out[0], w_up=w_up[0].T,
               w_down=w_down[0])
    moments = dict(w_in=(m_w_in[0].T, v_w_in[0].T), w_pool_proj=(m_w_pool_proj[0], v_w_pool_proj[0]),
                   w_gla_proj=(m_w_gla_proj[0], v_w_gla_proj[0]), w_out=(m_w_out[0], v_w_out[0]),
                   w_up=(m_w_up[0].T, v_w_up[0].T), w_down=(m_w_down[0], v_w_down[0]))
    names = list(big)
    shards = {k: big[k].astype(BF) for k in names}
    shards["w_gk_up"], shards["w_conv"] = w_gk_up[0], w_conv[0]
    gather_groups = (("w_in", "w_gk_up"), ("w_pool_proj", "w_gla_proj", "w_out"), ("w_up", "w_down", "w_conv"))
    started, token = _split_start("gather_start", [
        ([t for k in g for t in (shards[k], _gather_landing(shards[k], me))], 4 * len(g), _gather_first)
        for g in gather_groups])

    def gather_pass(gi, after):
        lands = list(_split_wait(f"gather_wait_{gi}", started[gi], _gather_first, after)[1::2])
        passed, tkn = _split_start(f"gather_pass_{gi}", [(lands, 3 * len(lands), _gather_second)])
        return passed[0], tkn

    def gather_done(gi, passed, after):
        return dict(zip(gather_groups[gi], _split_wait(f"gather_pass_wait_{gi}", passed, _gather_second, after)))

    tok = lambda i, j, k: (i, 0)
    whole = lambda i, j, k: (0, 0)
    kblk = lambda i, j, k: (k, 0)
    ff_seq = (None, None, SEQ, FF_BLK)

    h = _rms_fwd(xs, g_mix + token[:1, :1], "rms_mix")
    wg = gather_done(0, gather_pass(0, h)[0], h)
    wt_cat, wt_gk = _unshard_w_in(wg["w_in"])
    wgk_pad = jnp.pad(wg["w_gk_up"].transpose(1, 0, 2).reshape(GATE_RANK, GLA_DK), ((0, GK_PAD - GATE_RANK), (0, 0)))
    zcat = _mm(h, wt_cat, out_shape=(SEQ, N_CAT), out_dtype=BF, grid=(N_CAT_TILES, 1, 1),
               blk_a=(SEQ, D_MODEL), blk_b=(CAT_TILE, D_MODEL), blk_o=(SEQ, CAT_TILE),
               map_a=whole, map_b=lambda j, i, k: (j, 0), map_o=lambda j, i, k: (0, j), tb=True, name="mm_in")
    la = _gk_fwd(h, wt_gk, wgk_pad, b_gk)
    passed, tkn = gather_pass(1, la)
    o, states = _gla_fwd(zcat, la, tkn)
    wg = gather_done(1, passed, o)
    wpp = wg["w_pool_proj"].transpose(1, 0, 2).reshape(POOL_WIDTH, D_MODEL)
    wgp = wg["w_gla_proj"].reshape(D_MODEL, D_MODEL)
    wout = wg["w_out"].reshape(D_MODEL, D_MODEL)
    og = _post_gla_fwd(o, zcat, g_gla_head)
    ps = _pool_fwd(zcat, w_pool_grp[0], pool_scale)
    passed, tkn = gather_pass(2, (og, ps))
    y_pool, y_gla, mixed, x1, h2 = _mix_out_fwd(ps, og, zcat, xs, wpp, wgp, wout, b_gate, g_ffn, tkn)
    wg = gather_done(2, passed, h2)
    wt_up = wg["w_up"].reshape(2 * D_FF, D_MODEL)
    wdown = wg["w_down"].reshape(D_FF, D_MODEL)
    wconv4 = wg["w_conv"].reshape(2, 4, 3, FF_BLK)
    bconv4 = b_conv.reshape(2, 4, 1, FF_BLK)
    blk4 = lambda b, i, k: (b // 4, b % 4, 0, 0)
    u4, act = _up_conv_fwd(h2, wt_up, wconv4, bconv4)
    loss_part, dx2, dx2_bf, dg_final = _mm_tokens(
        act, wdown, blk_a=(None, 4, TOK_MM_TILE, FF_BLK), map_a=lambda i: (0, 0, i, 0),
        pieces=[(b, b * FF_BLK, FF_BLK) for b in range(4)], res=x1, then=("loss", g_final.reshape(1, D_MODEL), target),
        name="mm_down_loss")

    da = _mm(dx2_bf, wdown, out_shape=(1, 4, SEQ, FF_BLK), out_dtype=BF, grid=(4, 1, 1),
             blk_a=(SEQ, D_MODEL), blk_b=(FF_BLK, D_MODEL), blk_o=ff_seq,
             map_a=whole, map_b=lambda b, i, k: (b, 0), map_o=lambda b, i, k: (0, b, 0, 0), tb=True, name="mm_d_act")
    d_wdown = _mm(act, dx2_bf, out_shape=(D_FF, D_MODEL), out_dtype=BF, grid=(4, 1, 1),
                  blk_a=ff_seq, blk_b=(SEQ, D_MODEL), blk_o=(FF_BLK, D_MODEL),
                  map_a=lambda b, i, k: (0, b, 0, 0), map_b=whole, map_o=lambda b, i, k: (b, 0), ta=True,
                  name="mm_d_wdown")
    du4, d_wconv, d_bconv = _conv_bwd(u4, da, wconv4, bconv4)
    d_wt_up = _mm(du4, h2, out_shape=(2 * D_FF, D_MODEL), out_dtype=BF, grid=(N_DEV, 1, 1),
                  blk_a=ff_seq, blk_b=(SEQ, D_MODEL), blk_o=(FF_BLK, D_MODEL),
                  map_a=blk4, map_b=whole, map_o=lambda b, i, k: (b, 0), ta=True, name="mm_d_wup")
    res = {}

    def reduce_start(keys, parts):
        arrays = [t for k in keys for t in (parts[k], lax.empty((4,) + parts[k].shape[2:], BF))]
        st, tkn = _split_start("reduce_start_" + keys[0], [(arrays, 4 * len(keys), _reduce_first)])
        return st[0], tkn

    def reduce_cross(keys, st, after):
        arrays = _split_wait("reduce_wait_" + keys[0], st, _reduce_first, after)
        sums = [_pair_sum(p, r, core, "pair_sum_" + k) for k, p, r in zip(keys, arrays[0::2], arrays[1::2])]
        arrays = [t for s in sums for t in (s, lax.empty((3,) + s.shape[1:], BF))]
        st2, tkn = _split_start("reduce_cross_" + keys[0], [(arrays, 3 * len(keys), _reduce_second)])
        return st2[0], tkn

    def reduce_done(keys, st2, after):
        arrays = _split_wait("reduce_cross_wait_" + keys[0], st2, _reduce_second, after)
        for k, s, r in zip(keys, arrays[0::2], arrays[1::2]):
            outs = _chip_sum_adamw(s, r, big[k], moments[k][0], moments[k][1], chip, "adamw_" + k)
            res[k] = [(t.T if k in ("w_in", "w_up") else t)[None] for t in outs]

    ffn_keys = ("w_down", "w_up")
    ffn_red, tkn = reduce_start(ffn_keys, dict(w_down=d_wdown.reshape(4, 2, D_FF // N_DEV, D_MODEL),
                                               w_up=d_wt_up.reshape(4, 2, FF_BLK, D_MODEL)))
    dx1, dg_ffn = _mm_tokens(
        du4, wt_up, blk_a=(2, 4, TOK_MM_TILE, FF_BLK), map_a=lambda i: (0, 0, i, 0),
        pieces=[((b // 4, b % 4), b * FF_BLK, FF_BLK) for b in range(N_DEV)], after=tkn, then=("rms_bwd", x1, g_ffn, dx2),
        name="mm_d_h2_rms")

    sq_t = dict(out_shape=(D_MODEL, D_MODEL), grid=(1, 1, N_MM_TILES), blk_a=(MM_TILE, D_MODEL),
                blk_b=(MM_TILE, D_MODEL), blk_o=(D_MODEL, D_MODEL), map_a=kblk, map_b=kblk, map_o=whole, ta=True)
    d_wout = _mm(mixed, dx1, out_dtype=BF, name="mm_d_wout", **sq_t)
    dzcat, dy_pool, dy_gla, db_gate = _mix_bwd(dx1, wout, zcat, b_gate, y_pool, y_gla)
    ffn_red, tkn = reduce_cross(ffn_keys, ffn_red, db_gate)
    d_wgp = _mm(og, dy_gla, out_dtype=BF, after=tkn, name="mm_d_wgp", **sq_t)
    mix_keys = ("w_out", "w_gla_proj")
    mix_red, tkn = reduce_start(mix_keys, dict(w_out=d_wout.reshape(4, 2, D_MODEL // N_DEV, D_MODEL),
                                               w_gla_proj=d_wgp.reshape(4, 2, D_MODEL // N_DEV, D_MODEL)))
    dzcat, d_o, dg_head = _post_gla_bwd(dzcat, dy_gla, wgp, o, zcat, g_gla_head + tkn[:1, :1])
    dzcat, dla = _gla_bwd(dzcat, zcat, la, d_o, states)
    mix_red, tkn = reduce_cross(mix_keys, mix_red, dla)
    dh_gk, d_wt_gk, d_wgk, db_gk = _gk_bwd(dla, h, wt_gk, wgk_pad, b_gk + tkn[:1, :1])
    dps = _mm(dy_pool, wpp, out_shape=(SEQ, POOL_WIDTH), out_dtype=F32, grid=(N_MM_TILES, 1, 1),
              blk_a=(MM_TILE, D_MODEL), blk_b=(POOL_WIDTH, D_MODEL), blk_o=(MM_TILE, POOL_WIDTH),
              map_a=tok, map_b=whole, map_o=tok, tb=True, name="mm_d_ps")
    d_wpp = _mm(ps, dy_pool, out_shape=(POOL_WIDTH, D_MODEL), out_dtype=F32, grid=(1, 1, N_MM_TILES),
                blk_a=(MM_TILE, POOL_WIDTH), blk_b=(MM_TILE, D_MODEL), blk_o=(POOL_WIDTH, D_MODEL),
                map_a=kblk, map_b=kblk, map_o=whole, ta=True, name="mm_d_wpp")
    dzcat, d_wgrp, d_scale = _pool_bwd(dzcat, zcat, dps, w_pool_grp[0], pool_scale)
    row = lambda t: t.reshape(1, D_MODEL)
    conv_vec = lambda t: t.reshape(2, 4, 1, FF_BLK)
    small = [("b_gate", db_gate, b_gate, m_b_gate, v_b_gate, False),
             ("w_gk_up", d_wgk.reshape(GATE_RANK, N_DEV, GLA_DK // N_DEV).transpose(1, 0, 2), w_gk_up, m_w_gk_up,
              v_w_gk_up, True),
             ("b_gk", db_gk, b_gk, m_b_gk, v_b_gk, False),
             ("w_pool_grp", d_wgrp, w_pool_grp, m_w_pool_grp, v_w_pool_grp, False),
             ("pool_scale", d_scale, pool_scale, m_pool_scale, v_pool_scale, False),
             ("g_gla_head", dg_head, g_gla_head, m_g_gla_head, v_g_gla_head, False),
             ("g_ffn", dg_ffn, g_ffn, m_g_ffn, v_g_ffn, False),
             ("w_conv", d_wconv.reshape(N_DEV, 3, FF_BLK), w_conv, m_w_conv, v_w_conv, True),
             ("b_conv", d_bconv, conv_vec(b_conv), conv_vec(m_b_conv), conv_vec(v_b_conv), False),
             ("g_final", dg_final, row(g_final), row(m_g_final), row(v_g_final), False)]

    def small_start(parts, name):
        arrays = [t for p in parts for t in (p, _gather_landing(p, me))]
        st, tkn = _split_start(name, [(arrays, 7 * len(parts), _gather_direct)])
        return st[0], tkn

    small_sent, tkn = small_start([t[1] for t in small] + [loss_part], "small_start")
    d_wt_cat = _mm(dzcat, h, out_shape=(N_CAT, D_MODEL), out_dtype=BF, grid=(N_CAT_TILES, 1, 1),
                   blk_a=(SEQ, CAT_TILE), blk_b=(SEQ, D_MODEL), blk_o=(CAT_TILE, D_MODEL),
                   map_a=lambda j, i, k: (0, j), map_b=whole, map_o=lambda j, i, k: (j, 0), ta=True, after=tkn,
                   name="mm_d_wcat")
    in_keys = ("w_in", "w_pool_proj")
    in_red, tkn = reduce_start(in_keys, dict(
        w_in=_shard_d_w_in(d_wt_cat, d_wt_gk).reshape(4, 2, IN_SHARD, D_MODEL),
        w_pool_proj=d_wpp.reshape(POOL_WIDTH, N_DEV, D_MODEL // N_DEV).transpose(1, 0, 2).astype(BF)
        .reshape(4, 2, POOL_WIDTH, D_MODEL // N_DEV)))
    reduce_done(mix_keys, mix_red, tkn)
    in_red, tkn = reduce_cross(in_keys, in_red, res["w_out"][0])
    grad_x, dg_mix = _mm_tokens(dzcat, wt_cat, blk_a=(TOK_MM_TILE, N_CAT), map_a=lambda i: (i, 0),
                                pieces=[(None, 0, N_CAT)], res=dh_gk, after=tkn, then=("rms_bwd", xs, g_mix, dx1),
                                name="mm_d_h_rms")
    g_mix_sent, tkn = small_start([dg_mix], "g_mix_start")
    reduce_done(ffn_keys, ffn_red, (grad_x, tkn))
    gathered = _split_wait("small_wait", small_sent, _gather_direct, res["w_down"][0])[1::2]
    small.append(("g_mix", dg_mix, g_mix, m_g_mix, v_g_mix, False))
    gathered = list(gathered[:-1]) + [_split_wait("g_mix_wait", g_mix_sent, _gather_direct, gathered[0])[1], gathered[-1]]
    small_out, loss_sum = _small_sum_adamw(jnp.reshape(me, (1,)).astype(jnp.int32),
                                           [(p,) + t[2:] for p, t in zip(gathered, small)], gathered[-1])
    for t, outs in zip(small, small_out):
        res[t[0]] = list(outs)
    res["b_conv"] = [t.reshape(b_conv.shape) for t in res["b_conv"]]
    res["g_final"] = [t.reshape(g_final.shape) for t in res["g_final"]]

    reduce_done(in_keys, in_red, loss_sum)
    loss = loss_sum[0, 0]
    order =["g_mix", "w_in", "b_gate", "w_gk_up", "b_gk", "w_pool_grp", "pool_scale", "g_gla_head", "w_pool_proj",
             "w_gla_proj", "w_out", "g_ffn", "w_up", "w_conv", "b_conv", "w_down", "g_final"]
    return (loss, grad_x[None], *[res[k][0] for k in order], *[res[k][1] for k in order],
            *[res[k][2] for k in order], *[res[k][3] for k in order])
```

```python
import jax
import jax.numpy as jnp
from jax import lax
from jax.experimental import pallas as pl
from jax.experimental.pallas import tpu as pltpu

F32 = jnp.float32
BF = jnp.bfloat16
HIGHEST = lax.Precision.HIGHEST
MESH = pl.DeviceIdType.MESH

N_DEV = 8
SEQ = 2048
D_MODEL = 1024
CHUNK = 64
EPS = 1e-6
POOL_WIDTH = 512
POOL_WINDOWS = (2, 4, 8, 16)
POOL_GD = 128
POOL_HALO = 16
HEADS = 4
HK = 128
HV = 256
GLA_DK = 512
GATE_RANK = 16
GATE_NORM = 16.0
D_FF = 2816
FF_BLK = 704
IN_SHARD = 706
C_QKV, C_GATE, C_OG, C_POOL = 0, 2048, 4096, 5120
N_CAT = 5632
R_POOL, R_QKV, R_OG, R_GK, R_GATE = 0, 512, 2560, 3584, 3600
GK_PAD = 128

ADAM_LR, ADAM_B1, ADAM_B2, ADAM_EPS, ADAM_WD, ADAM_STEP = 0.001, 0.9, 0.999, 1e-08, 0.01, 10
ADAM_C1 = 1.0 - ADAM_B1 ** ADAM_STEP
ADAM_C2 = 1.0 - ADAM_B2 ** ADAM_STEP

VMEM_BYTES_V7X = 64 * 1024 * 1024
VMEM_LIMIT = VMEM_BYTES_V7X * 3 // 4

TOK_TILE = 256
HALO = 8
GLA_CPS = 4


def _params(*sem):
    return pltpu.CompilerParams(dimension_semantics=sem, vmem_limit_bytes=VMEM_LIMIT)


def _const_spec(shape):
    nd = len(shape)
    return pl.BlockSpec(shape, lambda *_: (0,) * nd)


def _in_hbm(t):
    return pltpu.with_memory_space_constraint(t, pltpu.HBM)


def _dot(a, b, ta=False, tb=False):
    dims = (((0 if ta else 1,), (1 if tb else 0,)), ((), ()))
    return lax.dot_general(a.astype(BF), b.astype(BF), dims, preferred_element_type=F32)


def _dot_exact(a, b):
    return jnp.dot(a, b, precision=HIGHEST, preferred_element_type=F32)


def _sigmoid(x):
    return 0.5 * jnp.tanh(0.5 * x) + 0.5


def _mm(a, b, *, out_shape, out_dtype, grid, blk_a, blk_b, blk_o, map_a, map_b, map_o, ta=False, tb=False,
        after=None, name):
    gk = grid[2]
    n_in = 2 + (after is not None)

    def body(*refs):
        a_ref, b_ref, o_ref = refs[0], refs[1], refs[n_in]
        prod = _dot(a_ref[...], b_ref[...], ta, tb)
        if gk == 1:
            o_ref[...] = prod.astype(out_dtype)
        else:
            acc = refs[n_in + 1]
            k = pl.program_id(2)

            @pl.when(k == 0)
            def _():
                acc[...] = prod

            @pl.when(k > 0)
            def _():
                acc[...] += prod

            @pl.when(k == gk - 1)
            def _():
                o_ref[...] = acc[...].astype(out_dtype)

    in_specs = [pl.BlockSpec(blk_a, map_a), pl.BlockSpec(blk_b, map_b)]
    args = [_in_hbm(a), _in_hbm(b)]
    if after is not None:
        in_specs.append(pl.BlockSpec(memory_space=pl.ANY))
        args.append(after)
    return pl.pallas_call(
        body, name=name, grid=grid, in_specs=in_specs, out_specs=pl.BlockSpec(blk_o, map_o),
        out_shape=jax.ShapeDtypeStruct(out_shape, out_dtype),
        scratch_shapes=[] if gk == 1 else [pltpu.VMEM(tuple(d for d in blk_o if d is not None), F32)],
        compiler_params=_params("parallel", "parallel", "arbitrary"),
    )(*args)


TOK_MM_TILE = 256


def _mm_tokens(a, w, *, blk_a, map_a, pieces, res=None, after=None, then=None, name):
    n_in = 2 + (res is not None) + (after is not None) + (0 if then is None else len(then) - 1)

    def accumulate(ref, part):
        @pl.when(pl.program_id(0) == 0)
        def _():
            ref[...] = part

        @pl.when(pl.program_id(0) > 0)
        def _():
            ref[...] += part

    def body(*refs):
        a_ref, w_ref = refs[:2]
        extra, outs = refs[n_in - (0 if then is None else len(then) - 1):n_in], refs[n_in:]
        total = None
        for idx, row, n in pieces:
            av = a_ref[...] if idx is None else a_ref[idx]
            prod = _dot(av, w_ref[row:row + n, :])
            total = prod if total is None else total + prod
        if res is not None:
            total = total + refs[2][...]
        if then is None:
            outs[0][...] = total
        elif then[0] == "rms_bwd":
            dx, part = _rms_bwd_tile(total, extra[0][...], extra[1][...], extra[2][...])
            outs[0][...] = dx
            accumulate(outs[1], part)
        else:
            lpart, dx, part = _loss_tile(total, extra[0][...], extra[1][...])
            outs[1][...] = dx
            outs[2][...] = dx.astype(BF)
            accumulate(outs[0], lpart)
            accumulate(outs[3], part)

    tile = pl.BlockSpec((TOK_MM_TILE, D_MODEL), lambda i: (i, 0))
    vec = _const_spec((1, D_MODEL))
    big = jax.ShapeDtypeStruct((SEQ, D_MODEL), F32)
    small = jax.ShapeDtypeStruct((1, D_MODEL), F32)
    in_specs = [pl.BlockSpec(blk_a, map_a), pl.BlockSpec(w.shape, lambda i: (0, 0), pipeline_mode=pl.Buffered(1))]
    args = [a, w]
    if res is not None:
        in_specs.append(tile)
        args.append(res)
    if after is not None:
        in_specs.append(pl.BlockSpec(memory_space=pl.ANY))
        args.append(after)
    if then is None:
        out_specs, out_shape = tile, big
    elif then[0] == "rms_bwd":
        in_specs += [tile, vec, tile]
        out_specs, out_shape = [tile, vec], [big, small]
    else:
        in_specs += [vec, tile]
        out_specs = [_const_spec((1, 128)), tile, tile, vec]
        out_shape = [jax.ShapeDtypeStruct((1, 128), F32), big, jax.ShapeDtypeStruct((SEQ, D_MODEL), BF), small]
    if then is not None:
        args += list(then[1:])
    return pl.pallas_call(
        body, name=name, grid=(SEQ // TOK_MM_TILE,), in_specs=in_specs, out_specs=out_specs, out_shape=out_shape,
        compiler_params=_params("parallel" if then is None else "arbitrary"),
    )(*[_in_hbm(t) for t in args])


def _rms_fwd(x, g, name):
    def body(x_ref, g_ref, o_ref):
        xv = x_ref[...]
        r = lax.rsqrt(jnp.mean(xv * xv, axis=-1, keepdims=True) + EPS)
        o_ref[...] = (xv * r * g_ref[...]).astype(BF)

    tile = pl.BlockSpec((TOK_TILE, D_MODEL), lambda i: (i, 0))
    return pl.pallas_call(
        body, name=name, grid=(SEQ // TOK_TILE,), in_specs=[tile, _const_spec((1, D_MODEL))], out_specs=tile,
        out_shape=jax.ShapeDtypeStruct((SEQ, D_MODEL), BF), compiler_params=_params("parallel"),
    )(*map(_in_hbm, (x, g)))


def _rms_bwd_tile(dyv, xv, gv, dresv):
    r = lax.rsqrt(jnp.mean(xv * xv, axis=-1, keepdims=True) + EPS)
    xn = xv * r
    dxn = dyv * gv
    return dresv + r * (dxn - xn * jnp.mean(dxn * xn, axis=-1, keepdims=True)), jnp.sum(dyv * xn, axis=0, keepdims=True)


def _loss_tile(xv, gv, tv):
    r = lax.rsqrt(jnp.mean(xv * xv, axis=-1, keepdims=True) + EPS)
    xn = xv * r
    err = xn * gv - tv
    lpart = jnp.full((1, 128), 0.5 * jnp.sum(jnp.mean(err * err, axis=-1, keepdims=True)), F32)
    dyv = err * (1.0 / D_MODEL)
    dxn = dyv * gv
    return lpart, r * (dxn - xn * jnp.mean(dxn * xn, axis=-1, keepdims=True)), jnp.sum(dyv * xn, axis=0, keepdims=True)


def _pool_counts(w):
    pos = lax.broadcasted_iota(jnp.int32, (SEQ, 1), 0).astype(F32)
    return jnp.minimum(pos + 1.0, float(w))


def _pool_window(u, w, ext):
    ext[pl.ds(POOL_HALO, SEQ), :] = u
    win = u
    for j in range(1, w):
        win = win + ext[pl.ds(POOL_HALO - j, SEQ), :]
    return win / _pool_counts(w) - u


def _pool_fwd(zcat, w_grp, scale):
    def body(z_ref, w_ref, s_ref, o_ref, ext):
        ext[pl.ds(0, POOL_HALO), :] = jnp.zeros((POOL_HALO, POOL_GD), F32)
        for g, w in enumerate(POOL_WINDOWS):
            cols = slice(g * POOL_GD, (g + 1) * POOL_GD)
            p = _pool_window(z_ref[:, cols].astype(F32), w, ext)
            o_ref[:, cols] = (_dot(p, w_ref[g]) * s_ref[:, cols]).astype(BF)

    return pl.pallas_call(
        body, name="pool_fwd", grid=(1,),
        in_specs=[pl.BlockSpec((SEQ, POOL_WIDTH), lambda i: (0, C_POOL // POOL_WIDTH)),
                  _const_spec((4, POOL_GD, POOL_GD)), _const_spec((1, POOL_WIDTH))],
        out_specs=_const_spec((SEQ, POOL_WIDTH)), out_shape=jax.ShapeDtypeStruct((SEQ, POOL_WIDTH), BF),
        scratch_shapes=[pltpu.VMEM((POOL_HALO + SEQ, POOL_GD), F32)], compiler_params=_params("arbitrary"),
    )(*map(_in_hbm, (zcat, w_grp, scale)))


def _pool_bwd(dzcat, zcat, dps, w_grp, scale):
    def body(dz_in, z_ref, dps_ref, w_ref, s_ref, dz_ref, dw_ref, dsc_ref, ext, ext2):
        del dz_in
        ext[pl.ds(0, POOL_HALO), :] = jnp.zeros((POOL_HALO, POOL_GD), F32)
        ext2[pl.ds(SEQ, POOL_HALO), :] = jnp.zeros((POOL_HALO, POOL_GD), F32)
        for g, w in enumerate(POOL_WINDOWS):
            cols = slice(g * POOL_GD, (g + 1) * POOL_GD)
            p = _pool_window(z_ref[:, cols].astype(F32), w, ext)
            wg = w_ref[g]
            pg = _dot(p, wg)
            dpsv = dps_ref[:, cols]
            dsc_ref[:, cols] = jnp.sum(dpsv * pg, axis=0, keepdims=True)
            dpg = dpsv * s_ref[:, cols]
            dw_ref[g] = _dot(p, dpg, ta=True)
            dp = _dot(dpg, wg, tb=True)
            dpc = dp / _pool_counts(w)
            ext2[pl.ds(0, SEQ), :] = dpc
            du = dpc
            for j in range(1, w):
                du = du + ext2[pl.ds(j, SEQ), :]
            dz_ref[:, cols] = (du - dp).astype(BF)

    return pl.pallas_call(
        body, name="pool_bwd", grid=(1,),
        in_specs=[pl.BlockSpec(memory_space=pl.ANY),
                  pl.BlockSpec((SEQ, POOL_WIDTH), lambda i: (0, C_POOL // POOL_WIDTH)),
                  _const_spec((SEQ, POOL_WIDTH)), _const_spec((4, POOL_GD, POOL_GD)), _const_spec((1, POOL_WIDTH))],
        out_specs=[pl.BlockSpec((SEQ, POOL_WIDTH), lambda i: (0, C_POOL // POOL_WIDTH)),
                   _const_spec((4, POOL_GD, POOL_GD)), _const_spec((1, POOL_WIDTH))],
        out_shape=[jax.ShapeDtypeStruct((SEQ, N_CAT), BF), jax.ShapeDtypeStruct((4, POOL_GD, POOL_GD), F32),
                   jax.ShapeDtypeStruct((1, POOL_WIDTH), F32)],
        scratch_shapes=[pltpu.VMEM((POOL_HALO + SEQ, POOL_GD), F32), pltpu.VMEM((SEQ + POOL_HALO, POOL_GD), F32)],
        input_output_aliases={0: 0}, compiler_params=_params("arbitrary"),
    )(*map(_in_hbm, (dzcat, zcat, dps, w_grp, scale)))


GK_TILE = 512


def _gk_fwd(h, wt_gk, wgk_pad, b_gk):
    def body(h_ref, wt_ref, w_ref, b_ref, la_ref):
        z_gk = _dot(h_ref[...], wt_ref[...], tb=True)
        pre = _dot(z_gk, w_ref[...]) + b_ref[...]
        la_ref[...] = (jnp.minimum(pre, 0.0) - jnp.log(1.0 + jnp.exp(-jnp.abs(pre)))) * (1.0 / GATE_NORM)

    return pl.pallas_call(
        body, name="gk_fwd", grid=(SEQ // GK_TILE,),
        in_specs=[pl.BlockSpec((GK_TILE, D_MODEL), lambda i: (i, 0)), _const_spec((GK_PAD, D_MODEL)),
                  _const_spec((GK_PAD, GLA_DK)), _const_spec((1, GLA_DK))],
        out_specs=pl.BlockSpec((GK_TILE, GLA_DK), lambda i: (i, 0)),
        out_shape=jax.ShapeDtypeStruct((SEQ, GLA_DK), F32), compiler_params=_params("parallel"),
    )(*map(_in_hbm, (h, wt_gk, wgk_pad, b_gk)))


def _gk_bwd(dla, h, wt_gk, wgk_pad, b_gk):
    def body(dla_ref, h_ref, wt_ref, w_ref, b_ref, dh_ref, dwt_ref, dw_ref, db_ref):
        hv = h_ref[...]
        wtv = wt_ref[...]
        wv = w_ref[...]
        z_gk = _dot(hv, wtv, tb=True)
        pre = _dot(z_gk, wv) + b_ref[...]
        dpre = dla_ref[...] * (1.0 / GATE_NORM) * (1.0 - _sigmoid(pre))
        dz_gk = _dot(dpre, wv, tb=True)
        dh_ref[...] = _dot(dz_gk, wtv)
        dwtp = _dot(dz_gk, hv, ta=True)
        dwp = _dot(z_gk, dpre, ta=True)[:GATE_RANK]
        dbp = jnp.sum(dpre, axis=0, keepdims=True)

        @pl.when(pl.program_id(0) == 0)
        def _():
            dwt_ref[...] = dwtp
            dw_ref[...] = dwp
            db_ref[...] = dbp

        @pl.when(pl.program_id(0) > 0)
        def _():
            dwt_ref[...] += dwtp
            dw_ref[...] += dwp
            db_ref[...] += dbp

    tile = pl.BlockSpec((GK_TILE, D_MODEL), lambda i: (i, 0))
    return pl.pallas_call(
        body, name="gk_bwd", grid=(SEQ // GK_TILE,),
        in_specs=[pl.BlockSpec((GK_TILE, GLA_DK), lambda i: (i, 0)), tile, _const_spec((GK_PAD, D_MODEL)),
                  _const_spec((GK_PAD, GLA_DK)), _const_spec((1, GLA_DK))],
        out_specs=[tile, _const_spec((GK_PAD, D_MODEL)), _const_spec((GATE_RANK, GLA_DK)), _const_spec((1, GLA_DK))],
        out_shape=[jax.ShapeDtypeStruct((SEQ, D_MODEL), F32), jax.ShapeDtypeStruct((GK_PAD, D_MODEL), F32),
                   jax.ShapeDtypeStruct((GATE_RANK, GLA_DK), F32), jax.ShapeDtypeStruct((1, GLA_DK), F32)],
        compiler_params=_params("arbitrary"),
    )(*map(_in_hbm, (dla, h, wt_gk, wgk_pad, b_gk)))


GLA_ROWS = GLA_CPS * CHUNK
GLA_STEPS = SEQ // GLA_ROWS
QKV_W = 2048


def _tri():
    return lax.broadcasted_iota(jnp.int32, (CHUNK, CHUNK), 0) >= lax.broadcasted_iota(jnp.int32, (CHUNK, CHUNK), 1)


def _chunk_cumsum(la_ref, rows):
    return _dot_exact(_tri().astype(F32), la_ref[rows, :])


def _gla_chunk(qkv_ref, la_ref, rows, h, bc_all):
    tri = _tri()
    q = qkv_ref[rows, h * HK:(h + 1) * HK].astype(F32) * (HK ** -0.5)
    k = qkv_ref[rows, GLA_DK + h * HK:GLA_DK + (h + 1) * HK].astype(F32)
    v = qkv_ref[rows, 2 * GLA_DK + h * HV:2 * GLA_DK + (h + 1) * HV].astype(BF)
    la = la_ref[rows, h * HK:(h + 1) * HK]
    bc = bc_all[:, h * HK:(h + 1) * HK]
    e_pos, e_neg = jnp.exp(bc), jnp.exp(-bc)
    dl = jnp.exp(jnp.sum(la, axis=0, keepdims=True))
    q_fw, q_bw, k_fw, k_bw = q * e_pos, q * e_neg, k * e_neg, k * e_pos
    scores = jnp.where(tri, _dot(q_fw, k_fw, tb=True), _dot(q_bw, k_bw, tb=True))
    return tri, v, e_pos, e_neg, dl, q_fw, q_bw, k_fw, k_bw, scores


def _gla_fwd(zcat, la, after):
    def body(qkv_ref, la_ref, after_ref, o_ref, st_ref, state):
        del after_ref

        @pl.when(pl.program_id(0) == 0)
        def _():
            state[...] = jnp.zeros_like(state)

        for c in range(GLA_CPS):
            rows = slice(c * CHUNK, (c + 1) * CHUNK)
            bc_all = _chunk_cumsum(la_ref, rows)
            for h in range(HEADS):
                _, v, _, _, dl, q_fw, _, k_fw, _, scores = _gla_chunk(qkv_ref, la_ref, rows, h, bc_all)
                st = state[h]
                st_ref[c, h] = st
                o_ref[rows, h * HV:(h + 1) * HV] = _dot(scores, v) + _dot(q_fw, st, tb=True)
                state[h] = st * dl + _dot(v, k_fw * dl, ta=True)

    return pl.pallas_call(
        body, name="gla_fwd", grid=(GLA_STEPS,),
        in_specs=[pl.BlockSpec((GLA_ROWS, QKV_W), lambda i: (i, 0)), pl.BlockSpec((GLA_ROWS, GLA_DK), lambda i: (i, 0)),
                  pl.BlockSpec(memory_space=pl.ANY)],
        out_specs=[pl.BlockSpec((GLA_ROWS, D_MODEL), lambda i: (i, 0)),
                   pl.BlockSpec((GLA_CPS, HEADS, HV, HK), lambda i: (i, 0, 0, 0))],
        out_shape=[jax.ShapeDtypeStruct((SEQ, D_MODEL), F32),
                   jax.ShapeDtypeStruct((SEQ // CHUNK, HEADS, HV, HK), F32)],
        scratch_shapes=[pltpu.VMEM((HEADS, HV, HK), F32)], compiler_params=_params("arbitrary"),
    )(*map(_in_hbm, (zcat, la)), after)


def _gla_bwd(dzcat, zcat, la, d_o, states):
    def body(dz_in, qkv_ref, la_ref, do_ref, st_ref, dqkv_ref, dla_ref, dstate):
        del dz_in

        @pl.when(pl.program_id(0) == 0)
        def _():
            dstate[...] = jnp.zeros_like(dstate)

        last_row = lax.broadcasted_iota(jnp.int32, (CHUNK, HK), 0) == CHUNK - 1
        upper = (lax.broadcasted_iota(jnp.int32, (CHUNK, CHUNK), 0)
                 <= lax.broadcasted_iota(jnp.int32, (CHUNK, CHUNK), 1)).astype(F32)
        for c in reversed(range(GLA_CPS)):
            rows = slice(c * CHUNK, (c + 1) * CHUNK)
            bc_all = _chunk_cumsum(la_ref, rows)
            dbs = []
            for h in range(HEADS):
                tri, v, e_pos, e_neg, dl, q_fw, q_bw, k_fw, k_bw, scores = _gla_chunk(qkv_ref, la_ref, rows, h, bc_all)
                st = st_ref[c, h]
                dst = dstate[h]
                d_out = do_ref[rows, h * HV:(h + 1) * HV].astype(BF)
                k_dec = k_fw * dl
                dp = _dot(d_out, v, tb=True)
                dp_fw = jnp.where(tri, dp, 0.0)
                dp_bw = jnp.where(tri, 0.0, dp)
                dv = _dot(scores, d_out, ta=True) + _dot(k_dec, dst, tb=True)
                dk_dec = _dot(v, dst)
                dq_fw = _dot(dp_fw, k_fw) + _dot(d_out, st)
                dk_fw = _dot(dp_fw, q_fw, ta=True) + dk_dec * dl
                dq_bw = _dot(dp_bw, k_bw)
                dk_bw = _dot(dp_bw, q_bw, ta=True)
                ddl = jnp.sum(st * dst, axis=0, keepdims=True) + jnp.sum(k_fw * dk_dec, axis=0, keepdims=True)
                dstate[h] = dst * dl + _dot(d_out, q_fw, ta=True)
                dq = (dq_fw * e_pos + dq_bw * e_neg) * (HK ** -0.5)
                dk = dk_fw * e_neg + dk_bw * e_pos
                dbs.append(dq_fw * q_fw - dk_fw * k_fw - dq_bw * q_bw + dk_bw * k_bw + jnp.where(last_row, ddl * dl, 0.0))
                dqkv_ref[rows, h * HK:(h + 1) * HK] = dq.astype(BF)
                dqkv_ref[rows, GLA_DK + h * HK:GLA_DK + (h + 1) * HK] = dk.astype(BF)
                dqkv_ref[rows, 2 * GLA_DK + h * HV:2 * GLA_DK + (h + 1) * HV] = dv.astype(BF)
            dla_ref[rows, :] = _dot_exact(upper, jnp.concatenate(dbs, axis=1))

    rev = lambda i: (GLA_STEPS - 1 - i, 0)
    return pl.pallas_call(
        body, name="gla_bwd", grid=(GLA_STEPS,),
        in_specs=[pl.BlockSpec(memory_space=pl.ANY), pl.BlockSpec((GLA_ROWS, QKV_W), rev),
                  pl.BlockSpec((GLA_ROWS, GLA_DK), rev), pl.BlockSpec((GLA_ROWS, D_MODEL), rev),
                  pl.BlockSpec((GLA_CPS, HEADS, HV, HK), lambda i: (GLA_STEPS - 1 - i, 0, 0, 0))],
        out_specs=[pl.BlockSpec((GLA_ROWS, QKV_W), rev), pl.BlockSpec((GLA_ROWS, GLA_DK), rev)],
        out_shape=[jax.ShapeDtypeStruct((SEQ, N_CAT), BF), jax.ShapeDtypeStruct((SEQ, GLA_DK), F32)],
        scratch_shapes=[pltpu.VMEM((HEADS, HV, HK), F32)], input_output_aliases={0: 0},
        compiler_params=_params("arbitrary"),
    )(*map(_in_hbm, (dzcat, zcat, la, d_o, states)))


def _silu_parts(x):
    s = _sigmoid(x)
    return x * s, s * (1.0 + x * (1.0 - s))


def _post_gla_fwd(o, zcat, g_head):
    def body(o_ref, zog_ref, g_ref, out_ref):
        for h in range(HEADS):
            cols = slice(h * HV, (h + 1) * HV)
            ov = o_ref[:, cols]
            r = lax.rsqrt(jnp.mean(ov * ov, axis=-1, keepdims=True) + EPS)
            act, _ = _silu_parts(zog_ref[:, cols].astype(F32))
            out_ref[:, cols] = (ov * r * g_ref[...] * act).astype(BF)

    tile = pl.BlockSpec((TOK_TILE, D_MODEL), lambda i: (i, 0))
    return pl.pallas_call(
        body, name="post_gla_fwd", grid=(SEQ // TOK_TILE,),
        in_specs=[tile, pl.BlockSpec((TOK_TILE, D_MODEL), lambda i: (i, C_OG // D_MODEL)), _const_spec((1, HV))],
        out_specs=tile, out_shape=jax.ShapeDtypeStruct((SEQ, D_MODEL), BF), compiler_params=_params("parallel"),
    )(*map(_in_hbm, (o, zcat, g_head)))


def _post_gla_bwd(dzcat, dy_gla, w_gla_proj, o, zcat, g_head):
    def body(dz_in, dyg_ref, w_ref, o_ref, zog_ref, g_ref, dz_ref, do_ref, dg_ref):
        del dz_in
        dog = _dot(dyg_ref[...], w_ref[...], tb=True)
        gpart = jnp.zeros((1, HV), F32)
        gv = g_ref[...]
        for h in range(HEADS):
            cols = slice(h * HV, (h + 1) * HV)
            ov = o_ref[:, cols]
            r = lax.rsqrt(jnp.mean(ov * ov, axis=-1, keepdims=True) + EPS)
            on = ov * r
            act, dact = _silu_parts(zog_ref[:, cols].astype(F32))
            dogv = dog[:, cols]
            dz_ref[:, cols] = (dogv * on * gv * dact).astype(BF)
            d_on_g = dogv * act
            gpart = gpart + jnp.sum(d_on_g * on, axis=0, keepdims=True)
            dxn = d_on_g * gv
            do_ref[:, cols] = (r * (dxn - on * jnp.mean(dxn * on, axis=-1, keepdims=True))).astype(BF)

        @pl.when(pl.program_id(0) == 0)
        def _():
            dg_ref[...] = gpart

        @pl.when(pl.program_id(0) > 0)
        def _():
            dg_ref[...] += gpart

    tile = pl.BlockSpec((TOK_TILE, D_MODEL), lambda i: (i, 0))
    ogspec = pl.BlockSpec((TOK_TILE, D_MODEL), lambda i: (i, C_OG // D_MODEL))
    return pl.pallas_call(
        body, name="post_gla_bwd", grid=(SEQ // TOK_TILE,),
        in_specs=[pl.BlockSpec(memory_space=pl.ANY), tile, _const_spec((D_MODEL, D_MODEL)), tile, ogspec,
                  _const_spec((1, HV))],
        out_specs=[ogspec, tile, _const_spec((1, HV))],
        out_shape=[jax.ShapeDtypeStruct((SEQ, N_CAT), BF), jax.ShapeDtypeStruct((SEQ, D_MODEL), BF),
                   jax.ShapeDtypeStruct((1, HV), F32)],
        input_output_aliases={0: 0}, compiler_params=_params("arbitrary"),
    )(*map(_in_hbm, (dzcat, dy_gla, w_gla_proj, o, zcat, g_head)))


GATE_W = 2 * D_MODEL


def _mix_out_fwd(ps, og, zcat, x, w_pool_proj, w_gla_proj, w_out, b_gate, g_ffn, after):
    def body(ps_ref, og_ref, zg_ref, x_ref, wpp_ref, wgp_ref, wout_ref, b_ref, g_ref, after_ref,
             yp_ref, yg_ref, mixed_ref, x1_ref, h2_ref):
        del after_ref
        y_pool = _dot(ps_ref[...], wpp_ref[...])
        y_gla = _dot(og_ref[...], wgp_ref[...])
        yp_ref[...] = y_pool.astype(BF)
        yg_ref[...] = y_gla.astype(BF)
        g0 = _sigmoid(zg_ref[:, :D_MODEL].astype(F32) + b_ref[:, :D_MODEL])
        g1 = _sigmoid(zg_ref[:, D_MODEL:].astype(F32) + b_ref[:, D_MODEL:])
        mixed = (g0 * y_pool + g1 * y_gla).astype(BF)
        mixed_ref[...] = mixed
        x1 = x_ref[...] + _dot(mixed, wout_ref[...])
        x1_ref[...] = x1
        r = lax.rsqrt(jnp.mean(x1 * x1, axis=-1, keepdims=True) + EPS)
        h2_ref[...] = (x1 * r * g_ref[...]).astype(BF)

    tile = pl.BlockSpec((TOK_TILE, D_MODEL), lambda i: (i, 0))
    resident = lambda shape: pl.BlockSpec(shape, lambda i: (0, 0), pipeline_mode=pl.Buffered(1))
    f32, bf16 = jax.ShapeDtypeStruct((SEQ, D_MODEL), F32), jax.ShapeDtypeStruct((SEQ, D_MODEL), BF)
    return pl.pallas_call(
        body, name="mix_out_fwd", grid=(SEQ // TOK_TILE,),
        in_specs=[pl.BlockSpec((TOK_TILE, POOL_WIDTH), lambda i: (i, 0)), tile,
                  pl.BlockSpec((TOK_TILE, GATE_W), lambda i: (i, C_GATE // GATE_W)), tile,
                  resident((POOL_WIDTH, D_MODEL)), resident((D_MODEL, D_MODEL)), resident((D_MODEL, D_MODEL)),
                  _const_spec((1, GATE_W)), _const_spec((1, D_MODEL)), pl.BlockSpec(memory_space=pl.ANY)],
        out_specs=[tile] * 5, out_shape=[bf16, bf16, bf16, f32, bf16], compiler_params=_params("parallel"),
    )(*map(_in_hbm, (ps, og, zcat, x, w_pool_proj, w_gla_proj, w_out, b_gate, g_ffn)), after)


def _mix_bwd(dx1, w_out, zcat, b_gate, y_pool, y_gla):
    def body(dx_ref, w_ref, zg_ref, b_ref, yp_ref, yg_ref, dz_ref, dyp_ref, dyg_ref, db_ref):
        dm = _dot(dx_ref[...], w_ref[...], tb=True)
        g0 = _sigmoid(zg_ref[:, :D_MODEL].astype(F32) + b_ref[:, :D_MODEL])
        g1 = _sigmoid(zg_ref[:, D_MODEL:].astype(F32) + b_ref[:, D_MODEL:])
        dyp_ref[...] = (dm * g0).astype(BF)
        dyg_ref[...] = (dm * g1).astype(BF)
        dz0 = dm * yp_ref[...].astype(F32) * g0 * (1.0 - g0)
        dz1 = dm * yg_ref[...].astype(F32) * g1 * (1.0 - g1)
        dz_ref[:, :D_MODEL] = dz0.astype(BF)
        dz_ref[:, D_MODEL:] = dz1.astype(BF)
        b0 = jnp.sum(dz0, axis=0, keepdims=True)
        b1 = jnp.sum(dz1, axis=0, keepdims=True)

        @pl.when(pl.program_id(0) == 0)
        def _():
            db_ref[:, :D_MODEL] = b0
            db_ref[:, D_MODEL:] = b1

        @pl.when(pl.program_id(0) > 0)
        def _():
            db_ref[:, :D_MODEL] += b0
            db_ref[:, D_MODEL:] += b1

    tile = pl.BlockSpec((TOK_TILE, D_MODEL), lambda i: (i, 0))
    gspec = pl.BlockSpec((TOK_TILE, GATE_W), lambda i: (i, C_GATE // GATE_W))
    return pl.pallas_call(
        body, name="mix_bwd", grid=(SEQ // TOK_TILE,),
        in_specs=[tile, _const_spec((D_MODEL, D_MODEL)), gspec, _const_spec((1, GATE_W)), tile, tile],
        out_specs=[gspec, tile, tile, _const_spec((1, GATE_W))],
        out_shape=[jax.ShapeDtypeStruct((SEQ, N_CAT), BF), jax.ShapeDtypeStruct((SEQ, D_MODEL), BF),
                   jax.ShapeDtypeStruct((SEQ, D_MODEL), BF), jax.ShapeDtypeStruct((1, GATE_W), F32)],
        compiler_params=_params("arbitrary"),
    )(*map(_in_hbm, (dx1, w_out, zcat, b_gate, y_pool, y_gla)))


N_TOK_TILES = SEQ // TOK_TILE
HALO_PER_TILE = TOK_TILE // HALO


LANE_TILES = tuple((lo, min(128, FF_BLK - lo)) for lo in range(0, FF_BLK, 128))


def _taps(w_ref, b_ref, half, lanes, rows):
    shape = (rows, lanes.stop - lanes.start)
    return ([jnp.broadcast_to(w_ref[half, j:j + 1, lanes], shape) for j in range(3)],
            jnp.broadcast_to(b_ref[half, :, lanes], shape))


def _conv_strips(u_ref, ub_ref, ua_ref, taps, lanes, width, n_strips, first):
    row = lax.broadcasted_iota(jnp.int32, (HALO, width), 0)
    prev = [[pltpu.roll(jnp.where(first, 0.0, ub_ref[half, :, lanes]), k, 0) for k in (1, 2)] for half in range(2)]
    for s in range(n_strips + (ua_ref is not None)):
        u3, conv = [], []
        for half in range(2):
            cur = u_ref[half, s * HALO:(s + 1) * HALO, lanes] if s < n_strips else ua_ref[half, :, lanes]
            rolled = [pltpu.roll(cur, k, 0) for k in (1, 2)]
            frames = [jnp.where(row >= 2, rolled[1], prev[half][1]), jnp.where(row >= 1, rolled[0], prev[half][0]), cur]
            prev[half] = rolled
            w3, bias = taps[half]
            u3.append(frames)
            conv.append(bias + frames[0] * w3[0] + frames[1] * w3[1] + frames[2] * w3[2])
        yield s, u3, conv


def _pair_specs(pairs):
    tile = pl.BlockSpec((pairs, None, TOK_TILE, FF_BLK), lambda b, i: (0, b, i, 0))
    before = pl.BlockSpec((pairs, None, HALO, FF_BLK), lambda b, i: (0, b, jnp.maximum(i * HALO_PER_TILE - 1, 0), 0))
    after = pl.BlockSpec((pairs, None, HALO, FF_BLK),
                         lambda b, i: (0, b, jnp.minimum((i + 1) * HALO_PER_TILE, SEQ // HALO - 1), 0))

    def vec(rows):
        return pl.BlockSpec((2, None, rows, FF_BLK), lambda b, i: (0, b, 0, 0))

    return tile, before, after, vec


N_STRIPS = TOK_TILE // HALO


def _up_conv_fwd(h2, wt_up, w_conv, b_conv):
    steps = N_TOK_TILES // 2

    def body(h_ref, h_next, wg_ref, wv_ref, w_ref, b_ref, u_ref, a_ref, buf_a, buf_b, carry):
        j = pl.program_id(1)

        def project(hv, buf):
            buf[0] = _dot(hv, wg_ref[...], tb=True)
            buf[1] = _dot(hv, wv_ref[...], tb=True)

        def conv(buf, row0):
            u_ref[:, row0:row0 + TOK_TILE, :] = buf[...]
            for lo, width in LANE_TILES:
                lanes = slice(lo, lo + width)
                taps = [_taps(w_ref, b_ref, half, lanes, HALO) for half in range(2)]
                pending = None
                for s, _, (cg, cv) in _conv_strips(buf, carry, None, taps, lanes, width, N_STRIPS, False):
                    act = cg * _sigmoid(cg) * cv
                    if s % 2 == 0:
                        pending = act
                    else:
                        a_ref[0, row0 + (s - 1) * HALO:row0 + (s + 1) * HALO, lanes] = (
                            jnp.concatenate([pending, act], axis=0).astype(BF))
            carry[...] = buf[:, TOK_TILE - HALO:, :]

        @pl.when(j == 0)
        def _():
            project(h_ref[0:TOK_TILE, :], buf_a)
            carry[...] = jnp.zeros_like(carry)

        project(h_ref[TOK_TILE:, :], buf_b)
        conv(buf_a, 0)
        project(h_next[...], buf_a)
        conv(buf_b, TOK_TILE)

    w_blk = lambda half: pl.BlockSpec((FF_BLK, D_MODEL), lambda b, j: (b + 4 * half, 0))
    vec = lambda rows: pl.BlockSpec((2, None, rows, FF_BLK), lambda b, j: (0, b, 0, 0))
    u_buf = pltpu.VMEM((2, TOK_TILE, FF_BLK), F32)
    return pl.pallas_call(
        body, name="up_conv_fwd", grid=(4, steps),
        in_specs=[pl.BlockSpec((2 * TOK_TILE, D_MODEL), lambda b, j: (j, 0)),
                  pl.BlockSpec((TOK_TILE, D_MODEL), lambda b, j: (jnp.minimum(2 * j + 2, N_TOK_TILES - 1), 0)),
                  w_blk(0), w_blk(1), vec(3), vec(1)],
        out_specs=[pl.BlockSpec((2, None, 2 * TOK_TILE, FF_BLK), lambda b, j: (0, b, j, 0)),
                   pl.BlockSpec((1, None, 2 * TOK_TILE, FF_BLK), lambda b, j: (0, b, j, 0))],
        out_shape=[jax.ShapeDtypeStruct((2, 4, SEQ, FF_BLK), F32), jax.ShapeDtypeStruct((1, 4, SEQ, FF_BLK), BF)],
        scratch_shapes=[u_buf, u_buf, pltpu.VMEM((2, HALO, FF_BLK), F32)],
        compiler_params=_params("parallel", "arbitrary"),
    )(*map(_in_hbm, (h2, h2, wt_up, wt_up, w_conv, b_conv)))


def _conv_bwd(u, da, w_conv, b_conv):
    def body(u_ref, ub_ref, ua_ref, da_ref, daa_ref, w_ref, b_ref, du_ref, dw_ref, db_ref):
        i = pl.program_id(1)

        @pl.when(i == 0)
        def _():
            dw_ref[...] = jnp.zeros_like(dw_ref)
            db_ref[...] = jnp.zeros_like(db_ref)

        for lo, width in LANE_TILES:
            lanes = slice(lo, lo + width)
            row = lax.broadcasted_iota(jnp.int32, (HALO, width), 0)
            taps = [_taps(w_ref, b_ref, half, lanes, HALO) for half in range(2)]
            acc_w = [[jnp.zeros((HALO, width), F32) for _ in range(3)] for _ in range(2)]
            acc_b = [jnp.zeros((HALO, width), F32) for _ in range(2)]
            da_pair, pending = None, [None, None]
            dc_prev, up_prev = [None, None], [None, None]
            for s, u3, (cg, cv) in _conv_strips(u_ref, ub_ref, ua_ref, taps, lanes, width, N_STRIPS, i == 0):
                act, dact = _silu_parts(cg)
                if s == N_STRIPS:
                    da = jnp.where(i < N_TOK_TILES - 1, daa_ref[0, :, lanes].astype(F32), 0.0)
                elif s % 2 == 0:
                    da_pair = da_ref[0, s * HALO:(s + 2) * HALO, lanes].astype(F32)
                    da = da_pair[:HALO]
                else:
                    da = da_pair[HALO:]
                dc = (da * cv * dact, da * act)
                for half in range(2):
                    up = [pltpu.roll(dc[half], HALO - k, 0) for k in (1, 2)]
                    if s < N_STRIPS:
                        for j in range(3):
                            acc_w[half][j] = acc_w[half][j] + dc[half] * u3[half][j]
                        acc_b[half] = acc_b[half] + dc[half]
                    if s >= 1:
                        w3 = taps[half][0]
                        du = (dc_prev[half] * w3[2] + jnp.where(row < HALO - 1, up_prev[half][0], up[0]) * w3[1]
                              + jnp.where(row < HALO - 2, up_prev[half][1], up[1]) * w3[0])
                        if (s - 1) % 2 == 0:
                            pending[half] = du
                        else:
                            du_ref[half, (s - 2) * HALO:s * HALO, lanes] = jnp.concatenate([pending[half], du],
                                                                                           axis=0).astype(BF)
                    dc_prev[half], up_prev[half] = dc[half], up
            for half in range(2):
                for j in range(3):
                    dw_ref[half, j:j + 1, lanes] += jnp.sum(acc_w[half][j], axis=0, keepdims=True)
                db_ref[half, :, lanes] += jnp.sum(acc_b[half], axis=0, keepdims=True)

    tile, before, after, vec = _pair_specs(2)
    da_tile, _, da_after_spec, _ = _pair_specs(1)
    return pl.pallas_call(
        body, name="conv_bwd", grid=(4, N_TOK_TILES),
        in_specs=[tile, before, after, da_tile, da_after_spec, vec(3), vec(1)],
        out_specs=[tile, vec(3), vec(1)],
        out_shape=[jax.ShapeDtypeStruct((2, 4, SEQ, FF_BLK), BF), jax.ShapeDtypeStruct((2, 4, 3, FF_BLK), F32),
                   jax.ShapeDtypeStruct((2, 4, 1, FF_BLK), F32)],
        compiler_params=_params("parallel", "arbitrary"),
    )(*map(_in_hbm, (u, u, u, da, da, w_conv, b_conv)))


W_IN_SEGMENTS = ((R_POOL, POOL_WIDTH, "cat", C_POOL), (R_QKV, QKV_W, "cat", C_QKV), (R_OG, D_MODEL, "cat", C_OG),
                 (R_GK, GATE_RANK, "gk", 0), (R_GATE, GATE_W, "cat", C_GATE))


def _slab_pieces(d):
    lo, hi = d * IN_SHARD, (d + 1) * IN_SHARD
    pieces = []
    for start, n, dest, at in W_IN_SEGMENTS:
        a, b = max(lo, start), min(hi, start + n)
        if a < b:
            assert (a - lo) % 2 == 0 and (b - a) % 2 == 0 and (at + a - start) % 2 == 0
            pieces.append(((a - lo) // 2, (b - a) // 2, dest, (at + a - start) // 2))
    return pieces


def _unshard_w_in(slabs):
    def body(slab_ref, cat_ref, gk_ref):
        d = pl.program_id(0)
        src = slab_ref.bitcast(jnp.uint32)
        dst = dict(cat=cat_ref.bitcast(jnp.uint32), gk=gk_ref.bitcast(jnp.uint32))

        @pl.when(d == 0)
        def _():
            gk_ref[...] = jnp.zeros_like(gk_ref)

        for dd in range(N_DEV):
            @pl.when(d == dd)
            def _():
                for a, n, dest, at in _slab_pieces(dd):
                    dst[dest][pl.ds(at, n), :] = src[0, pl.ds(a, n), :]

    return pl.pallas_call(
        body, name="unshard_w_in", grid=(N_DEV,),
        in_specs=[pl.BlockSpec((1, IN_SHARD, D_MODEL), lambda d: (d, 0, 0))],
        out_specs=[_const_spec((N_CAT, D_MODEL)), _const_spec((GK_PAD, D_MODEL))],
        out_shape=[jax.ShapeDtypeStruct((N_CAT, D_MODEL), BF), jax.ShapeDtypeStruct((GK_PAD, D_MODEL), BF)],
        compiler_params=_params("arbitrary"),
    )(_in_hbm(slabs))


def _shard_d_w_in(d_cat, d_gk):
    def body(cat_ref, gk_ref, slab_ref):
        d = pl.program_id(0)
        cat = cat_ref.bitcast(jnp.uint32)
        gk = pltpu.bitcast(gk_ref[0:GATE_RANK, :].astype(BF), jnp.uint32)
        dst = slab_ref.bitcast(jnp.uint32)
        for dd in range(N_DEV):
            @pl.when(d == dd)
            def _():
                for a, n, source, at in _slab_pieces(dd):
                    dst[0, pl.ds(a, n), :] = gk[at:at + n] if source == "gk" else cat[pl.ds(at, n), :]

    return pl.pallas_call(
        body, name="shard_d_w_in", grid=(N_DEV,),
        in_specs=[_const_spec((N_CAT, D_MODEL)), _const_spec((GK_PAD, D_MODEL))],
        out_specs=pl.BlockSpec((1, IN_SHARD, D_MODEL), lambda d: (d, 0, 0)),
        out_shape=jax.ShapeDtypeStruct((N_DEV, IN_SHARD, D_MODEL), BF), compiler_params=_params("parallel"),
    )(_in_hbm(d_cat), _in_hbm(d_gk))


ANY = pl.BlockSpec(memory_space=pl.ANY)


def _place():
    x, y, c = lax.axis_index("x"), lax.axis_index("y"), lax.axis_index("c")
    other_chips = [(1 - x, y), (x, 1 - y), (1 - x, 1 - y)]
    return x, y, c, other_chips


SEM = pl.BlockSpec(memory_space=pltpu.SEMAPHORE)
IN_HBM = pl.BlockSpec(memory_space=pltpu.HBM)
SPLIT_PARAMS = pltpu.CompilerParams(has_side_effects=pltpu.SideEffectType.DATAFLOW_SIDE_EFFECTING)


def _gather_first(refs, send_sems, recv_sems):
    x, y, c, chips = _place()
    targets = [(x, y, 1 - c)] + [(px, py, c) for px, py in chips]
    return [pltpu.make_async_remote_copy(src_ref=refs[2 * a], dst_ref=refs[2 * a + 1].at[4 * x + 2 * y + c],
                                         send_sem=send_sems.at[4 * a + k], recv_sem=recv_sems.at[4 * a + k],
                                         device_id=to, device_id_type=MESH)
            for a in range(len(refs) // 2) for k, to in enumerate(targets)]


def _gather_direct(refs, send_sems, recv_sems):
    x, y, c, _ = _place()
    flips = [(dx, dy, dc) for dx in (0, 1) for dy in (0, 1) for dc in (0, 1) if dx + dy + dc]
    targets = [(1 - x if dx else x, 1 - y if dy else y, 1 - c if dc else c) for dx, dy, dc in flips]
    return [pltpu.make_async_remote_copy(src_ref=refs[2 * a], dst_ref=refs[2 * a + 1].at[4 * x + 2 * y + c],
                                         send_sem=send_sems.at[7 * a + k], recv_sem=recv_sems.at[7 * a + k],
                                         device_id=to, device_id_type=MESH)
            for a in range(len(refs) // 2) for k, to in enumerate(targets)]


def _gather_second(refs, send_sems, recv_sems):
    x, y, c, chips = _place()
    copies = []
    for a, land in enumerate(refs):
        for j, (px, py) in enumerate(chips):
            block = land.at[4 * px + 2 * py + c]
            copies.append(pltpu.make_async_remote_copy(src_ref=block, dst_ref=block, send_sem=send_sems.at[3 * a + j],
                                                       recv_sem=recv_sems.at[3 * a + j], device_id=(x, y, 1 - c),
                                                       device_id_type=MESH))
    return copies


def _reduce_first(refs, send_sems, recv_sems):
    x, y, c, _ = _place()
    return [pltpu.make_async_remote_copy(src_ref=refs[2 * a].at[j, 1 - c], dst_ref=refs[2 * a + 1].at[j],
                                         send_sem=send_sems.at[4 * a + j], recv_sem=recv_sems.at[4 * a + j],
                                         device_id=(x, y, 1 - c), device_id_type=MESH)
            for a in range(len(refs) // 2) for j in range(4)]


def _reduce_second(refs, send_sems, recv_sems):
    _, _, c, chips = _place()
    return [pltpu.make_async_remote_copy(src_ref=refs[2 * a].at[2 * px + py], dst_ref=refs[2 * a + 1].at[k],
                                         send_sem=send_sems.at[3 * a + k], recv_sem=recv_sems.at[3 * a + k],
                                         device_id=(px, py, c), device_id_type=MESH)
            for a in range(len(refs) // 2) for k, (px, py) in enumerate(chips)]


def _split_start(name, groups):
    arrays = [a for g in groups for a in g[0]]
    n = len(arrays)

    def body(*refs):
        sems = refs[n:n + 2 * len(groups)]
        at = 0
        for gi, (members, _, build) in enumerate(groups):
            for cp in build(refs[at:at + len(members)], sems[2 * gi], sems[2 * gi + 1]):
                cp.start()
            at += len(members)
        refs[-1][...] = jnp.zeros_like(refs[-1])

    sem_shapes = [pltpu.SemaphoreType.DMA((g[1],)) for g in groups for _ in range(2)]
    outs = pl.pallas_call(
        body, name=name, in_specs=[IN_HBM] * n,
        out_shape=(*sem_shapes, *[pltpu.HBM(a.shape, a.dtype) for a in arrays], jax.ShapeDtypeStruct((8, 128), F32)),
        out_specs=(*[SEM] * len(sem_shapes), *[IN_HBM] * n, pl.BlockSpec(memory_space=pltpu.VMEM)),
        input_output_aliases={i: len(sem_shapes) + i for i in range(n)}, compiler_params=SPLIT_PARAMS,
    )(*[pltpu.with_memory_space_constraint(a, pltpu.HBM) for a in arrays])
    per_group, at = [], len(sem_shapes)
    for gi, (members, _, _) in enumerate(groups):
        per_group.append((outs[2 * gi], outs[2 * gi + 1], list(outs[at:at + len(members)])))
        at += len(members)
    return per_group, outs[-1]


def _split_wait(name, started, build, after):
    send_sems, recv_sems, arrays = started
    n = len(arrays)
    after = after if isinstance(after, (tuple, list)) else (after,)

    def body(*refs):
        for cp in build(refs[:n], refs[n], refs[n + 1]):
            cp.wait_send()
            cp.wait_recv()

    return pl.pallas_call(
        body, name=name, in_specs=[IN_HBM] * n + [SEM, SEM] + [ANY] * len(after),
        out_shape=tuple(pltpu.HBM(a.shape, a.dtype) for a in arrays), out_specs=tuple([IN_HBM] * n),
        input_output_aliases={i: i for i in range(n)}, compiler_params=SPLIT_PARAMS,
    )(*arrays, send_sems, recv_sems, *after)


def _gather_landing(shard, me):
    return lax.dynamic_update_slice(lax.empty((N_DEV,) + shard.shape, shard.dtype), shard[None],
                                    (me,) + (0,) * shard.ndim)


def _tile_2d(rows, cols):
    for t in (256, 176, 128):
        if rows % t == 0:
            return t, cols
    return rows, 256


def _pair_sum(part, recv, core, name):
    _, rows, cols = recv.shape
    tr, tc = rows, cols

    def body(c_ref, p_ref, r_ref, o_ref):
        del c_ref
        o_ref[...] = (p_ref[...].astype(F32) + r_ref[...].astype(F32)).astype(BF)

    grid_spec = pltpu.PrefetchScalarGridSpec(
        num_scalar_prefetch=1, grid=(4, rows // tr, cols // tc),
        in_specs=[pl.BlockSpec((None, None, tr, tc), lambda j, i, k, c_ref: (j, c_ref[0], i, k)),
                  pl.BlockSpec((None, tr, tc), lambda j, i, k, c_ref: (j, i, k))],
        out_specs=pl.BlockSpec((None, tr, tc), lambda j, i, k, c_ref: (j, i, k)))
    return pl.pallas_call(
        body, name=name, grid_spec=grid_spec, out_shape=jax.ShapeDtypeStruct(recv.shape, BF),
        compiler_params=_params("parallel", "parallel", "parallel"),
    )(core, *map(_in_hbm, (part, recv)))


def _adamw(w, g, m, v):
    m = ADAM_B1 * m + (1.0 - ADAM_B1) * g
    v = ADAM_B2 * v + (1.0 - ADAM_B2) * (g * g)
    delta = -ADAM_LR * ((m / ADAM_C1) / (jnp.sqrt(v / ADAM_C2) + ADAM_EPS) + ADAM_WD * w)
    return delta, m, v


def _chip_sum_adamw(sums, recv, w, m, v, chip, name):
    rows, cols = w.shape
    tr, tc = _tile_2d(rows, cols)

    def body(chip_ref, s_ref, r_ref, w_ref, m_ref, v_ref, g_out, d_out, m_out, v_out):
        del chip_ref
        g = s_ref[...].astype(F32)
        for k in range(3):
            g = g + r_ref[k].astype(F32)
        g_out[...] = g
        d_out[...], m_out[...], v_out[...] = _adamw(w_ref[...], g, m_ref[...], v_ref[...])

    tile = pl.BlockSpec((tr, tc), lambda i, k, chip_ref: (i, k))
    grid_spec = pltpu.PrefetchScalarGridSpec(
        num_scalar_prefetch=1, grid=(rows // tr, cols // tc),
        in_specs=[pl.BlockSpec((None, tr, tc), lambda i, k, chip_ref: (chip_ref[0], i, k)),
                  pl.BlockSpec((3, tr, tc), lambda i, k, chip_ref: (0, i, k)), tile, tile, tile],
        out_specs=[tile] * 4)
    return pl.pallas_call(
        body, name=name, grid_spec=grid_spec, out_shape=[jax.ShapeDtypeStruct((rows, cols), F32)] * 4,
        compiler_params=_params("parallel", "parallel"),
    )(chip, *map(_in_hbm, (sums, recv, w, m, v)))


def _small_sum_adamw(me, entries, loss_parts):
    def whole(shape, squeeze=0, pick=False):
        blk = (None,) * squeeze + tuple(shape[squeeze:])
        if pick:
            blk = (shape[0], None) + tuple(shape[2:])
            return pl.BlockSpec(blk, lambda i, me_ref: (0, me_ref[0]) + (0,) * (len(shape) - 2))
        return pl.BlockSpec(blk, lambda i, me_ref: (0,) * len(shape))

    in_specs, out_specs, out_shape, args = [], [], [], []
    for parts, w, m, v, sharded in entries:
        lead = w.ndim - (parts.ndim - (2 if sharded else 1))
        in_specs += [whole(parts.shape, pick=sharded)] + [whole(w.shape, squeeze=lead)] * 3
        out_specs += [whole(w.shape, squeeze=lead)] * 4
        out_shape += [jax.ShapeDtypeStruct(w.shape, F32)] * 4
        args += [parts, w, m, v]
    in_specs.append(whole(loss_parts.shape))
    out_specs.append(whole(loss_parts.shape[1:]))
    out_shape.append(jax.ShapeDtypeStruct(loss_parts.shape[1:], F32))
    n = len(entries)

    def added(p_ref):
        total = p_ref[0]
        for d in range(1, N_DEV):
            total = total + p_ref[d]
        return total

    def body(me_ref, *refs):
        del me_ref
        ins, outs = refs[:4 * n + 1], refs[4 * n + 1:]
        for e in range(n):
            p_ref, w_ref, m_ref, v_ref = ins[4 * e:4 * e + 4]
            g_out, d_out, m_out, v_out = outs[4 * e:4 * e + 4]
            g = added(p_ref)
            g_out[...] = g
            d_out[...], m_out[...], v_out[...] = _adamw(w_ref[...], g, m_ref[...], v_ref[...])
        outs[4 * n][...] = added(ins[4 * n])

    grid_spec = pltpu.PrefetchScalarGridSpec(num_scalar_prefetch=1, grid=(1,), in_specs=in_specs, out_specs=out_specs)
    outs = pl.pallas_call(body, name="small_sum_adamw", grid_spec=grid_spec, out_shape=out_shape,
                          compiler_params=_params("arbitrary"))(me, *map(_in_hbm, args + [loss_parts]))
    return [outs[4 * e:4 * e + 4] for e in range(n)], outs[4 * n]


MM_TILE = 512
N_MM_TILES = SEQ // MM_TILE
CAT_TILE = 512
N_CAT_TILES = N_CAT // CAT_TILE


def kernel(x, g_mix, w_in, b_gate, w_gk_up, b_gk, w_pool_grp, pool_scale, g_gla_head, w_pool_proj, w_gla_proj, w_out, g_ffn, w_up, w_conv, b_conv, w_down, g_final, loss_target, m_g_mix, m_w_in, m_b_gate, m_w_gk_up, m_b_gk, m_w_pool_grp, m_pool_scale, m_g_gla_head, m_w_pool_proj, m_w_gla_proj, m_w_out, m_g_ffn, m_w_up, m_w_conv, m_b_conv, m_w_down, m_g_final, v_g_mix, v_w_in, v_b_gate, v_w_gk_up, v_b_gk, v_w_pool_grp, v_pool_scale, v_g_gla_head, v_w_pool_proj, v_w_gla_proj, v_w_out, v_g_ffn, v_w_up, v_w_conv, v_b_conv, v_w_down, v_g_final):
    xi, yi, ci = lax.axis_index("x"), lax.axis_index("y"), lax.axis_index("c")
    me = 4 * xi + 2 * yi + ci
    core = jnp.reshape(ci, (1,)).astype(jnp.int32)
    chip = jnp.reshape(2 * xi + yi, (1,)).astype(jnp.int32)
    xs, target = x[0], loss_target[0]

    big = dict(w_in=w_in[0].T, w_pool_proj=w_pool_proj[0], w_gla_proj=w_gla_proj[0], w_out=w_out[0], w_up=w_up[0].T,
               w_down=w_down[0])
    moments = dict(w_in=(m_w_in[0].T, v_w_in[0].T), w_pool_proj=(m_w_pool_proj[0], v_w_pool_proj[0]),
                   w_gla_proj=(m_w_gla_proj[0], v_w_gla_proj[0]), w_out=(m_w_out[0], v_w_out[0]),
                   w_up=(m_w_up[0].T, v_w_up[0].T), w_down=(m_w_down[0], v_w_down[0]))
    names = list(big)
    shards = {k: big[k].astype(BF) for k in names}
    shards["w_gk_up"], shards["w_conv"] = w_gk_up[0], w_conv[0]
    gather_groups = (("w_in", "w_gk_up"), ("w_pool_proj", "w_gla_proj", "w_out"), ("w_up", "w_down", "w_conv"))
    group = lambda g: ([t for k in g for t in (shards[k], _gather_landing(shards[k], me))], 4 * len(g), _gather_first)
    started, token = _split_start("gather_start_0", [group(gather_groups[0])])
    later, _ = _split_start("gather_start_12", [group(g) for g in gather_groups[1:]])
    started = started + later

    def gather_pass(gi, after):
        lands = list(_split_wait(f"gather_wait_{gi}", started[gi], _gather_first, after)[1::2])
        passed, tkn = _split_start(f"gather_pass_{gi}", [(lands, 3 * len(lands), _gather_second)])
        return passed[0], tkn

    def gather_done(gi, passed, after):
        return dict(zip(gather_groups[gi], _split_wait(f"gather_pass_wait_{gi}", passed, _gather_second, after)))

    tok = lambda i, j, k: (i, 0)
    whole = lambda i, j, k: (0, 0)
    kblk = lambda i, j, k: (k, 0)
    ff_seq = (None, None, SEQ, FF_BLK)

    h = _rms_fwd(xs, g_mix + token[:1, :1], "rms_mix")
    wg = gather_done(0, gather_pass(0, h)[0], h)
    wt_cat, wt_gk = _unshard_w_in(wg["w_in"])
    wgk_pad = jnp.pad(wg["w_gk_up"].transpose(1, 0, 2).reshape(GATE_RANK, GLA_DK), ((0, GK_PAD - GATE_RANK), (0, 0)))
    zcat = _mm(h, wt_cat, out_shape=(SEQ, N_CAT), out_dtype=BF, grid=(N_CAT_TILES, 1, 1),
               blk_a=(SEQ, D_MODEL), blk_b=(CAT_TILE, D_MODEL), blk_o=(SEQ, CAT_TILE),
               map_a=whole, map_b=lambda j, i, k: (j, 0), map_o=lambda j, i, k: (0, j), tb=True, name="mm_in")
    la = _gk_fwd(h, wt_gk, wgk_pad, b_gk)
    passed, tkn = gather_pass(1, la)
    o, states = _gla_fwd(zcat, la, tkn)
    wg = gather_done(1, passed, o)
    wpp = wg["w_pool_proj"].transpose(1, 0, 2).reshape(POOL_WIDTH, D_MODEL)
    wgp = wg["w_gla_proj"].reshape(D_MODEL, D_MODEL)
    wout = wg["w_out"].reshape(D_MODEL, D_MODEL)
    og = _post_gla_fwd(o, zcat, g_gla_head)
    ps = _pool_fwd(zcat, w_pool_grp[0], pool_scale)
    passed, tkn = gather_pass(2, (og, ps))
    y_pool, y_gla, mixed, x1, h2 = _mix_out_fwd(ps, og, zcat, xs, wpp, wgp, wout, b_gate, g_ffn, tkn)
    wg = gather_done(2, passed, h2)
    wt_up = wg["w_up"].reshape(2 * D_FF, D_MODEL)
    wdown = wg["w_down"].reshape(D_FF, D_MODEL)
    wconv4 = wg["w_conv"].reshape(2, 4, 3, FF_BLK)
    bconv4 = b_conv.reshape(2, 4, 1, FF_BLK)
    blk4 = lambda b, i, k: (b // 4, b % 4, 0, 0)
    u4, act = _up_conv_fwd(h2, wt_up, wconv4, bconv4)
    loss_part, dx2, dx2_bf, dg_final = _mm_tokens(
        act, wdown, blk_a=(None, 4, TOK_MM_TILE, FF_BLK), map_a=lambda i: (0, 0, i, 0),
        pieces=[(b, b * FF_BLK, FF_BLK) for b in range(4)], res=x1, then=("loss", g_final.reshape(1, D_MODEL), target),
        name="mm_down_loss")

    da = _mm(dx2_bf, wdown, out_shape=(1, 4, SEQ, FF_BLK), out_dtype=BF, grid=(4, 1, 1),
             blk_a=(SEQ, D_MODEL), blk_b=(FF_BLK, D_MODEL), blk_o=ff_seq,
             map_a=whole, map_b=lambda b, i, k: (b, 0), map_o=lambda b, i, k: (0, b, 0, 0), tb=True, name="mm_d_act")
    d_wdown = _mm(act, dx2_bf, out_shape=(D_FF, D_MODEL), out_dtype=BF, grid=(4, 1, 1),
                  blk_a=ff_seq, blk_b=(SEQ, D_MODEL), blk_o=(FF_BLK, D_MODEL),
                  map_a=lambda b, i, k: (0, b, 0, 0), map_b=whole, map_o=lambda b, i, k: (b, 0), ta=True,
                  name="mm_d_wdown")
    du4, d_wconv, d_bconv = _conv_bwd(u4, da, wconv4, bconv4)
    d_wt_up = _mm(du4, h2, out_shape=(2 * D_FF, D_MODEL), out_dtype=BF, grid=(N_DEV, 1, 1),
                  blk_a=ff_seq, blk_b=(SEQ, D_MODEL), blk_o=(FF_BLK, D_MODEL),
                  map_a=blk4, map_b=whole, map_o=lambda b, i, k: (b, 0), ta=True, name="mm_d_wup")
    res = {}

    def reduce_start(keys, parts):
        arrays = [t for k in keys for t in (parts[k], lax.empty((4,) + parts[k].shape[2:], BF))]
        st, tkn = _split_start("reduce_start_" + keys[0], [(arrays, 4 * len(keys), _reduce_first)])
        return st[0], tkn

    def reduce_cross(keys, st, after):
        arrays = _split_wait("reduce_wait_" + keys[0], st, _reduce_first, after)
        sums = [_pair_sum(p, r, core, "pair_sum_" + k) for k, p, r in zip(keys, arrays[0::2], arrays[1::2])]
        arrays = [t for s in sums for t in (s, lax.empty((3,) + s.shape[1:], BF))]
        st2, tkn = _split_start("reduce_cross_" + keys[0], [(arrays, 3 * len(keys), _reduce_second)])
        return st2[0], tkn

    def reduce_done(keys, st2, after):
        arrays = _split_wait("reduce_cross_wait_" + keys[0], st2, _reduce_second, after)
        for k, s, r in zip(keys, arrays[0::2], arrays[1::2]):
            outs = _chip_sum_adamw(s, r, big[k], moments[k][0], moments[k][1], chip, "adamw_" + k)
            res[k] = [(t.T if k in ("w_in", "w_up") else t)[None] for t in outs]

    ffn_keys = ("w_down", "w_up")
    ffn_red, tkn = reduce_start(ffn_keys, dict(w_down=d_wdown.reshape(4, 2, D_FF // N_DEV, D_MODEL),
                                               w_up=d_wt_up.reshape(4, 2, FF_BLK, D_MODEL)))
    dx1, dg_ffn = _mm_tokens(
        du4, wt_up, blk_a=(2, 4, TOK_MM_TILE, FF_BLK), map_a=lambda i: (0, 0, i, 0),
        pieces=[((b // 4, b % 4), b * FF_BLK, FF_BLK) for b in range(N_DEV)], after=tkn, then=("rms_bwd", x1, g_ffn, dx2),
        name="mm_d_h2_rms")

    sq_t = dict(out_shape=(D_MODEL, D_MODEL), grid=(1, 1, N_MM_TILES), blk_a=(MM_TILE, D_MODEL),
                blk_b=(MM_TILE, D_MODEL), blk_o=(D_MODEL, D_MODEL), map_a=kblk, map_b=kblk, map_o=whole, ta=True)
    d_wout = _mm(mixed, dx1, out_dtype=BF, name="mm_d_wout", **sq_t)
    dzcat, dy_pool, dy_gla, db_gate = _mix_bwd(dx1, wout, zcat, b_gate, y_pool, y_gla)
    ffn_red, tkn = reduce_cross(ffn_keys, ffn_red, db_gate)
    d_wgp = _mm(og, dy_gla, out_dtype=BF, after=tkn, name="mm_d_wgp", **sq_t)
    mix_keys = ("w_out", "w_gla_proj")
    mix_red, tkn = reduce_start(mix_keys, dict(w_out=d_wout.reshape(4, 2, D_MODEL // N_DEV, D_MODEL),
                                               w_gla_proj=d_wgp.reshape(4, 2, D_MODEL // N_DEV, D_MODEL)))
    dzcat, d_o, dg_head = _post_gla_bwd(dzcat, dy_gla, wgp, o, zcat, g_gla_head + tkn[:1, :1])
    dzcat, dla = _gla_bwd(dzcat, zcat, la, d_o, states)
    mix_red, tkn = reduce_cross(mix_keys, mix_red, dla)
    dh_gk, d_wt_gk, d_wgk, db_gk = _gk_bwd(dla, h, wt_gk, wgk_pad, b_gk + tkn[:1, :1])
    dps = _mm(dy_pool, wpp, out_shape=(SEQ, POOL_WIDTH), out_dtype=F32, grid=(N_MM_TILES, 1, 1),
              blk_a=(MM_TILE, D_MODEL), blk_b=(POOL_WIDTH, D_MODEL), blk_o=(MM_TILE, POOL_WIDTH),
              map_a=tok, map_b=whole, map_o=tok, tb=True, name="mm_d_ps")
    d_wpp = _mm(ps, dy_pool, out_shape=(POOL_WIDTH, D_MODEL), out_dtype=F32, grid=(1, 1, N_MM_TILES),
                blk_a=(MM_TILE, POOL_WIDTH), blk_b=(MM_TILE, D_MODEL), blk_o=(POOL_WIDTH, D_MODEL),
                map_a=kblk, map_b=kblk, map_o=whole, ta=True, name="mm_d_wpp")
    dzcat, d_wgrp, d_scale = _pool_bwd(dzcat, zcat, dps, w_pool_grp[0], pool_scale)
    row = lambda t: t.reshape(1, D_MODEL)
    conv_vec = lambda t: t.reshape(2, 4, 1, FF_BLK)
    small = [("b_gate", db_gate, b_gate, m_b_gate, v_b_gate, False),
             ("w_gk_up", d_wgk.reshape(GATE_RANK, N_DEV, GLA_DK // N_DEV).transpose(1, 0, 2), w_gk_up, m_w_gk_up,
              v_w_gk_up, True),
             ("b_gk", db_gk, b_gk, m_b_gk, v_b_gk, False),
             ("w_pool_grp", d_wgrp, w_pool_grp, m_w_pool_grp, v_w_pool_grp, False),
             ("pool_scale", d_scale, pool_scale, m_pool_scale, v_pool_scale, False),
             ("g_gla_head", dg_head, g_gla_head, m_g_gla_head, v_g_gla_head, False),
             ("g_ffn", dg_ffn, g_ffn, m_g_ffn, v_g_ffn, False),
             ("w_conv", d_wconv.reshape(N_DEV, 3, FF_BLK), w_conv, m_w_conv, v_w_conv, True),
             ("b_conv", d_bconv, conv_vec(b_conv), conv_vec(m_b_conv), conv_vec(v_b_conv), False),
             ("g_final", dg_final, row(g_final), row(m_g_final), row(v_g_final), False)]

    def small_start(parts, name):
        arrays = [t for p in parts for t in (p, _gather_landing(p, me))]
        st, tkn = _split_start(name, [(arrays, 7 * len(parts), _gather_direct)])
        return st[0], tkn

    small_sent, tkn = small_start([t[1] for t in small] + [loss_part], "small_start")
    d_wt_cat = _mm(dzcat, h, out_shape=(N_CAT, D_MODEL), out_dtype=BF, grid=(N_CAT_TILES, 1, 1),
                   blk_a=(SEQ, CAT_TILE), blk_b=(SEQ, D_MODEL), blk_o=(CAT_TILE, D_MODEL),
                   map_a=lambda j, i, k: (0, j), map_b=whole, map_o=lambda j, i, k: (j, 0), ta=True, after=tkn,
                   name="mm_d_wcat")
    in_keys = ("w_in", "w_pool_proj")
    in_red, tkn = reduce_start(in_keys, dict(
        w_in=_shard_d_w_in(d_wt_cat, d_wt_gk).reshape(4, 2, IN_SHARD, D_MODEL),
        w_pool_proj=d_wpp.reshape(POOL_WIDTH, N_DEV, D_MODEL // N_DEV).transpose(1, 0, 2).astype(BF)
        .reshape(4, 2, POOL_WIDTH, D_MODEL // N_DEV)))
    reduce_done(mix_keys, mix_red, tkn)
    in_red, tkn = reduce_cross(in_keys, in_red, res["w_out"][0])
    grad_x, dg_mix = _mm_tokens(dzcat, wt_cat, blk_a=(TOK_MM_TILE, N_CAT), map_a=lambda i: (i, 0),
                                pieces=[(None, 0, N_CAT)], res=dh_gk, after=tkn, then=("rms_bwd", xs, g_mix, dx1),
                                name="mm_d_h_rms")
    g_mix_sent, tkn = small_start([dg_mix], "g_mix_start")
    reduce_done(ffn_keys, ffn_red, (grad_x, tkn))
    gathered = _split_wait("small_wait", small_sent, _gather_direct, res["w_down"][0])[1::2]
    small.append(("g_mix", dg_mix, g_mix, m_g_mix, v_g_mix, False))
    gathered = list(gathered[:-1]) + [_split_wait("g_mix_wait", g_mix_sent, _gather_direct, gathered[0])[1], gathered[-1]]
    small_out, loss_sum = _small_sum_adamw(jnp.reshape(me, (1,)).astype(jnp.int32),
                                           [(p,) + t[2:] for p, t in zip(gathered, small)], gathered[-1])
    for t, outs in zip(small, small_out):
        res[t[0]] = list(outs)
    res["b_conv"] = [t.reshape(b_conv.shape) for t in res["b_conv"]]
    res["g_final"] = [t.reshape(g_final.shape) for t in res["g_final"]]

    reduce_done(in_keys, in_red, loss_sum)
    loss = loss_sum[0, 0]
    order =["g_mix", "w_in", "b_gate", "w_gk_up", "b_gk", "w_pool_grp", "pool_scale", "g_gla_head", "w_pool_proj",
             "w_gla_proj", "w_out", "g_ffn", "w_up", "w_conv", "b_conv", "w_down", "g_final"]
    return (loss, grad_x[None], *[res[k][0] for k in order], *[res[k][1] for k in order],
            *[res[k][2] for k in order], *[res[k][3] for k in order])
```

```python
import jax
import jax.numpy as jnp
from jax import lax
from jax.experimental import pallas as pl
from jax.experimental.pallas import tpu as pltpu

F32 = jnp.float32
BF = jnp.bfloat16
HIGHEST = lax.Precision.HIGHEST
MESH = pl.DeviceIdType.MESH

N_DEV = 8
SEQ = 2048
D_MODEL = 1024
CHUNK = 64
EPS = 1e-6
POOL_WIDTH = 512
POOL_WINDOWS = (2, 4, 8, 16)
POOL_GD = 128
POOL_HALO = 16
HEADS = 4
HK = 128
HV = 256
GLA_DK = 512
GATE_RANK = 16
GATE_NORM = 16.0
D_FF = 2816
FF_BLK = 704
IN_SHARD = 706
C_QKV, C_GATE, C_OG, C_POOL, C_GK = 0, 2048, 4096, 5120, 5632
N_CAT = 5632
GK_PAD = 128
N_DZ = N_CAT + GK_PAD
R_POOL, R_QKV, R_OG, R_GK, R_GATE = 0, 512, 2560, 3584, 3600

ADAM_LR, ADAM_B1, ADAM_B2, ADAM_EPS, ADAM_WD, ADAM_STEP = 0.001, 0.9, 0.999, 1e-08, 0.01, 10
ADAM_C1 = 1.0 - ADAM_B1 ** ADAM_STEP
ADAM_C2 = 1.0 - ADAM_B2 ** ADAM_STEP

VMEM_BYTES_V7X = 64 * 1024 * 1024
VMEM_LIMIT = VMEM_BYTES_V7X * 3 // 4

TOK_TILE = 256
HALO = 8
GLA_CPS = 4


def _params(*sem):
    return pltpu.CompilerParams(dimension_semantics=sem, vmem_limit_bytes=VMEM_LIMIT)


def _const_spec(shape):
    nd = len(shape)
    return pl.BlockSpec(shape, lambda *_: (0,) * nd)


def _in_hbm(t):
    return pltpu.with_memory_space_constraint(t, pltpu.HBM)


def _dot(a, b, ta=False, tb=False):
    dims = (((0 if ta else 1,), (1 if tb else 0,)), ((), ()))
    return lax.dot_general(a.astype(BF), b.astype(BF), dims, preferred_element_type=F32)


def _dot_exact(a, b):
    return jnp.dot(a, b, precision=HIGHEST, preferred_element_type=F32)


def _sigmoid(x):
    return 0.5 * jnp.tanh(0.5 * x) + 0.5


def _mm(a, b, *, out_shape, out_dtype, grid, blk_a, blk_b, blk_o, map_a, map_b, map_o, ta=False, tb=False,
        after=None, name):
    gk = grid[2]
    n_in = 2 + (after is not None)

    def body(*refs):
        a_ref, b_ref, o_ref = refs[0], refs[1], refs[n_in]
        prod = _dot(a_ref[...], b_ref[...], ta, tb)
        if gk == 1:
            o_ref[...] = prod.astype(out_dtype)
        else:
            acc = refs[n_in + 1]
            k = pl.program_id(2)

            @pl.when(k == 0)
            def _():
                acc[...] = prod

            @pl.when(k > 0)
            def _():
                acc[...] += prod

            @pl.when(k == gk - 1)
            def _():
                o_ref[...] = acc[...].astype(out_dtype)

    in_specs = [pl.BlockSpec(blk_a, map_a), pl.BlockSpec(blk_b, map_b)]
    args = [_in_hbm(a), _in_hbm(b)]
    if after is not None:
        in_specs.append(pl.BlockSpec(memory_space=pl.ANY))
        args.append(after)
    return pl.pallas_call(
        body, name=name, grid=grid, in_specs=in_specs, out_specs=pl.BlockSpec(blk_o, map_o),
        out_shape=jax.ShapeDtypeStruct(out_shape, out_dtype),
        scratch_shapes=[] if gk == 1 else [pltpu.VMEM(tuple(d for d in blk_o if d is not None), F32)],
        compiler_params=_params("parallel", "parallel", "arbitrary"),
    )(*args)


TOK_MM_TILE = 256


def _mm_tokens(a, w, *, blk_a, map_a, pieces, res=None, after=None, then=None, name):
    n_in = 2 + (res is not None) + (after is not None) + (0 if then is None else len(then) - 1)

    def accumulate(ref, part):
        @pl.when(pl.program_id(0) == 0)
        def _():
            ref[...] = part

        @pl.when(pl.program_id(0) > 0)
        def _():
            ref[...] += part

    def body(*refs):
        a_ref, w_ref = refs[:2]
        extra, outs = refs[n_in - (0 if then is None else len(then) - 1):n_in], refs[n_in:]
        total = None
        for idx, row, n in pieces:
            av = a_ref[...] if idx is None else a_ref[idx]
            prod = _dot(av, w_ref[row:row + n, :])
            total = prod if total is None else total + prod
        if res is not None:
            total = total + refs[2][...]
        if then is None:
            outs[0][...] = total
        elif then[0] == "rms_bwd":
            dx, part = _rms_bwd_tile(total, extra[0][...], extra[1][...], extra[2][...])
            outs[0][...] = dx
            accumulate(outs[1], part)
        else:
            lpart, dx, part = _loss_tile(total, extra[0][...], extra[1][...])
            outs[1][...] = dx
            outs[2][...] = dx.astype(BF)
            accumulate(outs[0], lpart)
            accumulate(outs[3], part)

    tile = pl.BlockSpec((TOK_MM_TILE, D_MODEL), lambda i: (i, 0))
    vec = _const_spec((1, D_MODEL))
    big = jax.ShapeDtypeStruct((SEQ, D_MODEL), F32)
    small = jax.ShapeDtypeStruct((1, D_MODEL), F32)
    in_specs = [pl.BlockSpec(blk_a, map_a), pl.BlockSpec(w.shape, lambda i: (0, 0), pipeline_mode=pl.Buffered(1))]
    args = [a, w]
    if res is not None:
        in_specs.append(tile)
        args.append(res)
    if after is not None:
        in_specs.append(pl.BlockSpec(memory_space=pl.ANY))
        args.append(after)
    if then is None:
        out_specs, out_shape = tile, big
    elif then[0] == "rms_bwd":
        in_specs += [tile, vec, tile]
        out_specs, out_shape = [tile, vec], [big, small]
    else:
        in_specs += [vec, tile]
        out_specs = [_const_spec((1, 128)), tile, tile, vec]
        out_shape = [jax.ShapeDtypeStruct((1, 128), F32), big, jax.ShapeDtypeStruct((SEQ, D_MODEL), BF), small]
    if then is not None:
        args += list(then[1:])
    return pl.pallas_call(
        body, name=name, grid=(SEQ // TOK_MM_TILE,), in_specs=in_specs, out_specs=out_specs, out_shape=out_shape,
        compiler_params=_params("parallel" if then is None else "arbitrary"),
    )(*[_in_hbm(t) for t in args])


def _rms_fwd(x, g, name):
    def body(x_ref, g_ref, o_ref):
        xv = x_ref[...]
        r = lax.rsqrt(jnp.mean(xv * xv, axis=-1, keepdims=True) + EPS)
        o_ref[...] = (xv * r * g_ref[...]).astype(BF)

    tile = pl.BlockSpec((TOK_TILE, D_MODEL), lambda i: (i, 0))
    return pl.pallas_call(
        body, name=name, grid=(SEQ // TOK_TILE,), in_specs=[tile, _const_spec((1, D_MODEL))], out_specs=tile,
        out_shape=jax.ShapeDtypeStruct((SEQ, D_MODEL), BF), compiler_params=_params("parallel"),
    )(*map(_in_hbm, (x, g)))


def _rms_bwd_tile(dyv, xv, gv, dresv):
    r = lax.rsqrt(jnp.mean(xv * xv, axis=-1, keepdims=True) + EPS)
    xn = xv * r
    dxn = dyv * gv
    return dresv + r * (dxn - xn * jnp.mean(dxn * xn, axis=-1, keepdims=True)), jnp.sum(dyv * xn, axis=0, keepdims=True)


def _loss_tile(xv, gv, tv):
    r = lax.rsqrt(jnp.mean(xv * xv, axis=-1, keepdims=True) + EPS)
    xn = xv * r
    err = xn * gv - tv
    lpart = jnp.full((1, 128), 0.5 * jnp.sum(jnp.mean(err * err, axis=-1, keepdims=True)), F32)
    dyv = err * (1.0 / D_MODEL)
    dxn = dyv * gv
    return lpart, r * (dxn - xn * jnp.mean(dxn * xn, axis=-1, keepdims=True)), jnp.sum(dyv * xn, axis=0, keepdims=True)


def _pool_counts(w):
    pos = lax.broadcasted_iota(jnp.int32, (SEQ, 1), 0).astype(F32)
    return jnp.minimum(pos + 1.0, float(w))


def _pool_window(u, w, ext):
    ext[pl.ds(POOL_HALO, SEQ), :] = u
    win = u
    for j in range(1, w):
        win = win + ext[pl.ds(POOL_HALO - j, SEQ), :]
    return win / _pool_counts(w) - u


def _pool_fwd(zcat, w_grp, scale):
    def body(z_ref, w_ref, s_ref, o_ref, ext):
        ext[pl.ds(0, POOL_HALO), :] = jnp.zeros((POOL_HALO, POOL_GD), F32)
        for g, w in enumerate(POOL_WINDOWS):
            cols = slice(g * POOL_GD, (g + 1) * POOL_GD)
            p = _pool_window(z_ref[:, cols].astype(F32), w, ext)
            o_ref[:, cols] = (_dot(p, w_ref[g]) * s_ref[:, cols]).astype(BF)

    return pl.pallas_call(
        body, name="pool_fwd", grid=(1,),
        in_specs=[pl.BlockSpec((SEQ, POOL_WIDTH), lambda i: (0, C_POOL // POOL_WIDTH)),
                  _const_spec((4, POOL_GD, POOL_GD)), _const_spec((1, POOL_WIDTH))],
        out_specs=_const_spec((SEQ, POOL_WIDTH)), out_shape=jax.ShapeDtypeStruct((SEQ, POOL_WIDTH), BF),
        scratch_shapes=[pltpu.VMEM((POOL_HALO + SEQ, POOL_GD), F32)], compiler_params=_params("arbitrary"),
    )(*map(_in_hbm, (zcat, w_grp, scale)))


def _pool_bwd(dzcat, zcat, dps, w_grp, scale):
    def body(dz_in, z_ref, dps_ref, w_ref, s_ref, dz_ref, dw_ref, dsc_ref, ext, ext2):
        del dz_in
        ext[pl.ds(0, POOL_HALO), :] = jnp.zeros((POOL_HALO, POOL_GD), F32)
        ext2[pl.ds(SEQ, POOL_HALO), :] = jnp.zeros((POOL_HALO, POOL_GD), F32)
        for g, w in enumerate(POOL_WINDOWS):
            cols = slice(g * POOL_GD, (g + 1) * POOL_GD)
            p = _pool_window(z_ref[:, cols].astype(F32), w, ext)
            wg = w_ref[g]
            pg = _dot(p, wg)
            dpsv = dps_ref[:, cols]
            dsc_ref[:, cols] = jnp.sum(dpsv * pg, axis=0, keepdims=True)
            dpg = dpsv * s_ref[:, cols]
            dw_ref[g] = _dot(p, dpg, ta=True)
            dp = _dot(dpg, wg, tb=True)
            dpc = dp / _pool_counts(w)
            ext2[pl.ds(0, SEQ), :] = dpc
            du = dpc
            for j in range(1, w):
                du = du + ext2[pl.ds(j, SEQ), :]
            dz_ref[:, cols] = (du - dp).astype(BF)

    return pl.pallas_call(
        body, name="pool_bwd", grid=(1,),
        in_specs=[pl.BlockSpec(memory_space=pl.ANY),
                  pl.BlockSpec((SEQ, POOL_WIDTH), lambda i: (0, C_POOL // POOL_WIDTH)),
                  _const_spec((SEQ, POOL_WIDTH)), _const_spec((4, POOL_GD, POOL_GD)), _const_spec((1, POOL_WIDTH))],
        out_specs=[pl.BlockSpec((SEQ, POOL_WIDTH), lambda i: (0, C_POOL // POOL_WIDTH)),
                   _const_spec((4, POOL_GD, POOL_GD)), _const_spec((1, POOL_WIDTH))],
        out_shape=[jax.ShapeDtypeStruct((SEQ, N_DZ), BF), jax.ShapeDtypeStruct((4, POOL_GD, POOL_GD), F32),
                   jax.ShapeDtypeStruct((1, POOL_WIDTH), F32)],
        scratch_shapes=[pltpu.VMEM((POOL_HALO + SEQ, POOL_GD), F32), pltpu.VMEM((SEQ + POOL_HALO, POOL_GD), F32)],
        input_output_aliases={0: 0}, compiler_params=_params("arbitrary"),
    )(*map(_in_hbm, (dzcat, zcat, dps, w_grp, scale)))


GK_TILE = 512


GK_ROWS = pl.BlockSpec((GK_PAD, D_MODEL), lambda i: (C_GK // GK_PAD, 0))


def _gk_fwd(h, wt_cat, wgk_pad, b_gk):
    def body(h_ref, wt_ref, w_ref, b_ref, la_ref):
        z_gk = _dot(h_ref[...], wt_ref[...], tb=True)
        pre = _dot(z_gk, w_ref[...]) + b_ref[...]
        la_ref[...] = (jnp.minimum(pre, 0.0) - jnp.log(1.0 + jnp.exp(-jnp.abs(pre)))) * (1.0 / GATE_NORM)

    return pl.pallas_call(
        body, name="gk_fwd", grid=(SEQ // GK_TILE,),
        in_specs=[pl.BlockSpec((GK_TILE, D_MODEL), lambda i: (i, 0)), GK_ROWS,
                  _const_spec((GK_PAD, GLA_DK)), _const_spec((1, GLA_DK))],
        out_specs=pl.BlockSpec((GK_TILE, GLA_DK), lambda i: (i, 0)),
        out_shape=jax.ShapeDtypeStruct((SEQ, GLA_DK), F32), compiler_params=_params("parallel"),
    )(*map(_in_hbm, (h, wt_cat, wgk_pad, b_gk)))


def _gk_bwd(dzcat, dla, h, wt_cat, wgk_pad, b_gk):
    def body(dz_in, dla_ref, h_ref, wt_ref, w_ref, b_ref, dz_ref, dw_ref, db_ref):
        del dz_in
        wv = w_ref[...]
        z_gk = _dot(h_ref[...], wt_ref[...], tb=True)
        pre = _dot(z_gk, wv) + b_ref[...]
        dpre = dla_ref[...] * (1.0 / GATE_NORM) * (1.0 - _sigmoid(pre))
        dz_ref[...] = _dot(dpre, wv, tb=True).astype(BF)
        dwp = _dot(z_gk, dpre, ta=True)[:GATE_RANK]
        dbp = jnp.sum(dpre, axis=0, keepdims=True)

        @pl.when(pl.program_id(0) == 0)
        def _():
            dw_ref[...] = dwp
            db_ref[...] = dbp

        @pl.when(pl.program_id(0) > 0)
        def _():
            dw_ref[...] += dwp
            db_ref[...] += dbp

    return pl.pallas_call(
        body, name="gk_bwd", grid=(SEQ // GK_TILE,),
        in_specs=[pl.BlockSpec(memory_space=pl.ANY), pl.BlockSpec((GK_TILE, GLA_DK), lambda i: (i, 0)),
                  pl.BlockSpec((GK_TILE, D_MODEL), lambda i: (i, 0)), GK_ROWS, _const_spec((GK_PAD, GLA_DK)),
                  _const_spec((1, GLA_DK))],
        out_specs=[pl.BlockSpec((GK_TILE, GK_PAD), lambda i: (i, C_GK // GK_PAD)), _const_spec((GATE_RANK, GLA_DK)),
                   _const_spec((1, GLA_DK))],
        out_shape=[jax.ShapeDtypeStruct((SEQ, N_DZ), BF), jax.ShapeDtypeStruct((GATE_RANK, GLA_DK), F32),
                   jax.ShapeDtypeStruct((1, GLA_DK), F32)],
        input_output_aliases={0: 0}, compiler_params=_params("arbitrary"),
    )(*map(_in_hbm, (dzcat, dla, h, wt_cat, wgk_pad, b_gk)))


GLA_ROWS = GLA_CPS * CHUNK
GLA_STEPS = SEQ // GLA_ROWS
QKV_W = 2048


def _tri():
    return lax.broadcasted_iota(jnp.int32, (CHUNK, CHUNK), 0) >= lax.broadcasted_iota(jnp.int32, (CHUNK, CHUNK), 1)


def _chunk_cumsum(la_ref, rows):
    return _dot_exact(_tri().astype(F32), la_ref[rows, :])


def _gla_chunk(qkv_ref, la_ref, rows, h, bc_all):
    tri = _tri()
    q = qkv_ref[rows, h * HK:(h + 1) * HK].astype(F32) * (HK ** -0.5)
    k = qkv_ref[rows, GLA_DK + h * HK:GLA_DK + (h + 1) * HK].astype(F32)
    v = qkv_ref[rows, 2 * GLA_DK + h * HV:2 * GLA_DK + (h + 1) * HV].astype(BF)
    la = la_ref[rows, h * HK:(h + 1) * HK]
    bc = bc_all[:, h * HK:(h + 1) * HK]
    e_pos, e_neg = jnp.exp(bc), jnp.exp(-bc)
    dl = jnp.exp(jnp.sum(la, axis=0, keepdims=True))
    q_fw, q_bw, k_fw, k_bw = q * e_pos, q * e_neg, k * e_neg, k * e_pos
    scores = jnp.where(tri, _dot(q_fw, k_fw, tb=True), _dot(q_bw, k_bw, tb=True))
    return tri, v, e_pos, e_neg, dl, q_fw, q_bw, k_fw, k_bw, scores


def _gla_fwd(zcat, la, after):
    def body(qkv_ref, la_ref, after_ref, o_ref, st_ref, state):
        del after_ref

        @pl.when(pl.program_id(0) == 0)
        def _():
            state[...] = jnp.zeros_like(state)

        for c in range(GLA_CPS):
            rows = slice(c * CHUNK, (c + 1) * CHUNK)
            bc_all = _chunk_cumsum(la_ref, rows)
            for h in range(HEADS):
                _, v, _, _, dl, q_fw, _, k_fw, _, scores = _gla_chunk(qkv_ref, la_ref, rows, h, bc_all)
                st = state[h]
                st_ref[c, h] = st
                o_ref[rows, h * HV:(h + 1) * HV] = _dot(scores, v) + _dot(q_fw, st, tb=True)
                state[h] = st * dl + _dot(v, k_fw * dl, ta=True)

    return pl.pallas_call(
        body, name="gla_fwd", grid=(GLA_STEPS,),
        in_specs=[pl.BlockSpec((GLA_ROWS, QKV_W), lambda i: (i, 0)), pl.BlockSpec((GLA_ROWS, GLA_DK), lambda i: (i, 0)),
                  pl.BlockSpec(memory_space=pl.ANY)],
        out_specs=[pl.BlockSpec((GLA_ROWS, D_MODEL), lambda i: (i, 0)),
                   pl.BlockSpec((GLA_CPS, HEADS, HV, HK), lambda i: (i, 0, 0, 0))],
        out_shape=[jax.ShapeDtypeStruct((SEQ, D_MODEL), F32),
                   jax.ShapeDtypeStruct((SEQ // CHUNK, HEADS, HV, HK), F32)],
        scratch_shapes=[pltpu.VMEM((HEADS, HV, HK), F32)], compiler_params=_params("arbitrary"),
    )(*map(_in_hbm, (zcat, la)), after)


def _gla_bwd(dzcat, zcat, la, d_o, states):
    def body(dz_in, qkv_ref, la_ref, do_ref, st_ref, dqkv_ref, dla_ref, dstate):
        del dz_in

        @pl.when(pl.program_id(0) == 0)
        def _():
            dstate[...] = jnp.zeros_like(dstate)

        last_row = lax.broadcasted_iota(jnp.int32, (CHUNK, HK), 0) == CHUNK - 1
        upper = (lax.broadcasted_iota(jnp.int32, (CHUNK, CHUNK), 0)
                 <= lax.broadcasted_iota(jnp.int32, (CHUNK, CHUNK), 1)).astype(F32)
        for c in reversed(range(GLA_CPS)):
            rows = slice(c * CHUNK, (c + 1) * CHUNK)
            bc_all = _chunk_cumsum(la_ref, rows)
            dbs = []
            for h in range(HEADS):
                tri, v, e_pos, e_neg, dl, q_fw, q_bw, k_fw, k_bw, scores = _gla_chunk(qkv_ref, la_ref, rows, h, bc_all)
                st = st_ref[c, h]
                dst = dstate[h]
                d_out = do_ref[rows, h * HV:(h + 1) * HV].astype(BF)
                k_dec = k_fw * dl
                dp = _dot(d_out, v, tb=True)
                dp_fw = jnp.where(tri, dp, 0.0)
                dp_bw = jnp.where(tri, 0.0, dp)
                dv = _dot(scores, d_out, ta=True) + _dot(k_dec, dst, tb=True)
                dk_dec = _dot(v, dst)
                dq_fw = _dot(dp_fw, k_fw) + _dot(d_out, st)
                dk_fw = _dot(dp_fw, q_fw, ta=True) + dk_dec * dl
                dq_bw = _dot(dp_bw, k_bw)
                dk_bw = _dot(dp_bw, q_bw, ta=True)
                ddl = jnp.sum(st * dst, axis=0, keepdims=True) + jnp.sum(k_fw * dk_dec, axis=0, keepdims=True)
                dstate[h] = dst * dl + _dot(d_out, q_fw, ta=True)
                dq = (dq_fw * e_pos + dq_bw * e_neg) * (HK ** -0.5)
                dk = dk_fw * e_neg + dk_bw * e_pos
                dbs.append(dq_fw * q_fw - dk_fw * k_fw - dq_bw * q_bw + dk_bw * k_bw + jnp.where(last_row, ddl * dl, 0.0))
                dqkv_ref[rows, h * HK:(h + 1) * HK] = dq.astype(BF)
                dqkv_ref[rows, GLA_DK + h * HK:GLA_DK + (h + 1) * HK] = dk.astype(BF)
                dqkv_ref[rows, 2 * GLA_DK + h * HV:2 * GLA_DK + (h + 1) * HV] = dv.astype(BF)
            dla_ref[rows, :] = _dot_exact(upper, jnp.concatenate(dbs, axis=1))

    rev = lambda i: (GLA_STEPS - 1 - i, 0)
    return pl.pallas_call(
        body, name="gla_bwd", grid=(GLA_STEPS,),
        in_specs=[pl.BlockSpec(memory_space=pl.ANY), pl.BlockSpec((GLA_ROWS, QKV_W), rev),
                  pl.BlockSpec((GLA_ROWS, GLA_DK), rev), pl.BlockSpec((GLA_ROWS, D_MODEL), rev),
                  pl.BlockSpec((GLA_CPS, HEADS, HV, HK), lambda i: (GLA_STEPS - 1 - i, 0, 0, 0))],
        out_specs=[pl.BlockSpec((GLA_ROWS, QKV_W), rev), pl.BlockSpec((GLA_ROWS, GLA_DK), rev)],
        out_shape=[jax.ShapeDtypeStruct((SEQ, N_DZ), BF), jax.ShapeDtypeStruct((SEQ, GLA_DK), F32)],
        scratch_shapes=[pltpu.VMEM((HEADS, HV, HK), F32)], input_output_aliases={0: 0},
        compiler_params=_params("arbitrary"),
    )(*map(_in_hbm, (dzcat, zcat, la, d_o, states)))


def _silu_parts(x):
    s = _sigmoid(x)
    return x * s, s * (1.0 + x * (1.0 - s))


def _post_gla_fwd(o, zcat, g_head):
    def body(o_ref, zog_ref, g_ref, out_ref):
        for h in range(HEADS):
            cols = slice(h * HV, (h + 1) * HV)
            ov = o_ref[:, cols]
            r = lax.rsqrt(jnp.mean(ov * ov, axis=-1, keepdims=True) + EPS)
            act, _ = _silu_parts(zog_ref[:, cols].astype(F32))
            out_ref[:, cols] = (ov * r * g_ref[...] * act).astype(BF)

    tile = pl.BlockSpec((TOK_TILE, D_MODEL), lambda i: (i, 0))
    return pl.pallas_call(
        body, name="post_gla_fwd", grid=(SEQ // TOK_TILE,),
        in_specs=[tile, pl.BlockSpec((TOK_TILE, D_MODEL), lambda i: (i, C_OG // D_MODEL)), _const_spec((1, HV))],
        out_specs=tile, out_shape=jax.ShapeDtypeStruct((SEQ, D_MODEL), BF), compiler_params=_params("parallel"),
    )(*map(_in_hbm, (o, zcat, g_head)))


def _post_gla_bwd(dzcat, dy_gla, w_gla_proj, o, zcat, g_head):
    def body(dz_in, dyg_ref, w_ref, o_ref, zog_ref, g_ref, dz_ref, do_ref, dg_ref):
        del dz_in
        dog = _dot(dyg_ref[...], w_ref[...], tb=True)
        gpart = jnp.zeros((1, HV), F32)
        gv = g_ref[...]
        for h in range(HEADS):
            cols = slice(h * HV, (h + 1) * HV)
            ov = o_ref[:, cols]
            r = lax.rsqrt(jnp.mean(ov * ov, axis=-1, keepdims=True) + EPS)
            on = ov * r
            act, dact = _silu_parts(zog_ref[:, cols].astype(F32))
            dogv = dog[:, cols]
            dz_ref[:, cols] = (dogv * on * gv * dact).astype(BF)
            d_on_g = dogv * act
            gpart = gpart + jnp.sum(d_on_g * on, axis=0, keepdims=True)
            dxn = d_on_g * gv
            do_ref[:, cols] = (r * (dxn - on * jnp.mean(dxn * on, axis=-1, keepdims=True))).astype(BF)

        @pl.when(pl.program_id(0) == 0)
        def _():
            dg_ref[...] = gpart

        @pl.when(pl.program_id(0) > 0)
        def _():
            dg_ref[...] += gpart

    tile = pl.BlockSpec((TOK_TILE, D_MODEL), lambda i: (i, 0))
    ogspec = pl.BlockSpec((TOK_TILE, D_MODEL), lambda i: (i, C_OG // D_MODEL))
    return pl.pallas_call(
        body, name="post_gla_bwd", grid=(SEQ // TOK_TILE,),
        in_specs=[pl.BlockSpec(memory_space=pl.ANY), tile, _const_spec((D_MODEL, D_MODEL)), tile, ogspec,
                  _const_spec((1, HV))],
        out_specs=[ogspec, tile, _const_spec((1, HV))],
        out_shape=[jax.ShapeDtypeStruct((SEQ, N_DZ), BF), jax.ShapeDtypeStruct((SEQ, D_MODEL), BF),
                   jax.ShapeDtypeStruct((1, HV), F32)],
        input_output_aliases={0: 0}, compiler_params=_params("arbitrary"),
    )(*map(_in_hbm, (dzcat, dy_gla, w_gla_proj, o, zcat, g_head)))


GATE_W = 2 * D_MODEL


def _mix_out_fwd(ps, og, zcat, x, w_pool_proj, w_gla_proj, w_out, b_gate, g_ffn, after):
    def body(ps_ref, og_ref, zg_ref, x_ref, wpp_ref, wgp_ref, wout_ref, b_ref, g_ref, after_ref,
             yp_ref, yg_ref, mixed_ref, x1_ref, h2_ref):
        del after_ref
        y_pool = _dot(ps_ref[...], wpp_ref[...])
        y_gla = _dot(og_ref[...], wgp_ref[...])
        yp_ref[...] = y_pool.astype(BF)
        yg_ref[...] = y_gla.astype(BF)
        g0 = _sigmoid(zg_ref[:, :D_MODEL].astype(F32) + b_ref[:, :D_MODEL])
        g1 = _sigmoid(zg_ref[:, D_MODEL:].astype(F32) + b_ref[:, D_MODEL:])
        mixed = (g0 * y_pool + g1 * y_gla).astype(BF)
        mixed_ref[...] = mixed
        x1 = x_ref[...] + _dot(mixed, wout_ref[...])
        x1_ref[...] = x1
        r = lax.rsqrt(jnp.mean(x1 * x1, axis=-1, keepdims=True) + EPS)
        h2_ref[...] = (x1 * r * g_ref[...]).astype(BF)

    tile = pl.BlockSpec((TOK_TILE, D_MODEL), lambda i: (i, 0))
    resident = lambda shape: pl.BlockSpec(shape, lambda i: (0, 0), pipeline_mode=pl.Buffered(1))
    f32, bf16 = jax.ShapeDtypeStruct((SEQ, D_MODEL), F32), jax.ShapeDtypeStruct((SEQ, D_MODEL), BF)
    return pl.pallas_call(
        body, name="mix_out_fwd", grid=(SEQ // TOK_TILE,),
        in_specs=[pl.BlockSpec((TOK_TILE, POOL_WIDTH), lambda i: (i, 0)), tile,
                  pl.BlockSpec((TOK_TILE, GATE_W), lambda i: (i, C_GATE // GATE_W)), tile,
                  resident((POOL_WIDTH, D_MODEL)), resident((D_MODEL, D_MODEL)), resident((D_MODEL, D_MODEL)),
                  _const_spec((1, GATE_W)), _const_spec((1, D_MODEL)), pl.BlockSpec(memory_space=pl.ANY)],
        out_specs=[tile] * 5, out_shape=[bf16, bf16, bf16, f32, bf16], compiler_params=_params("parallel"),
    )(*map(_in_hbm, (ps, og, zcat, x, w_pool_proj, w_gla_proj, w_out, b_gate, g_ffn)), after)


def _mix_bwd(dx1, w_out, zcat, b_gate, y_pool, y_gla):
    def body(dx_ref, w_ref, zg_ref, b_ref, yp_ref, yg_ref, dz_ref, dyp_ref, dyg_ref, db_ref):
        dm = _dot(dx_ref[...], w_ref[...], tb=True)
        g0 = _sigmoid(zg_ref[:, :D_MODEL].astype(F32) + b_ref[:, :D_MODEL])
        g1 = _sigmoid(zg_ref[:, D_MODEL:].astype(F32) + b_ref[:, D_MODEL:])
        dyp_ref[...] = (dm * g0).astype(BF)
        dyg_ref[...] = (dm * g1).astype(BF)
        dz0 = dm * yp_ref[...].astype(F32) * g0 * (1.0 - g0)
        dz1 = dm * yg_ref[...].astype(F32) * g1 * (1.0 - g1)
        dz_ref[:, :D_MODEL] = dz0.astype(BF)
        dz_ref[:, D_MODEL:] = dz1.astype(BF)
        b0 = jnp.sum(dz0, axis=0, keepdims=True)
        b1 = jnp.sum(dz1, axis=0, keepdims=True)

        @pl.when(pl.program_id(0) == 0)
        def _():
            db_ref[:, :D_MODEL] = b0
            db_ref[:, D_MODEL:] = b1

        @pl.when(pl.program_id(0) > 0)
        def _():
            db_ref[:, :D_MODEL] += b0
            db_ref[:, D_MODEL:] += b1

    tile = pl.BlockSpec((TOK_TILE, D_MODEL), lambda i: (i, 0))
    gspec = pl.BlockSpec((TOK_TILE, GATE_W), lambda i: (i, C_GATE // GATE_W))
    return pl.pallas_call(
        body, name="mix_bwd", grid=(SEQ // TOK_TILE,),
        in_specs=[tile, _const_spec((D_MODEL, D_MODEL)), gspec, _const_spec((1, GATE_W)), tile, tile],
        out_specs=[gspec, tile, tile, _const_spec((1, GATE_W))],
        out_shape=[jax.ShapeDtypeStruct((SEQ, N_DZ), BF), jax.ShapeDtypeStruct((SEQ, D_MODEL), BF),
                   jax.ShapeDtypeStruct((SEQ, D_MODEL), BF), jax.ShapeDtypeStruct((1, GATE_W), F32)],
        compiler_params=_params("arbitrary"),
    )(*map(_in_hbm, (dx1, w_out, zcat, b_gate, y_pool, y_gla)))


N_TOK_TILES = SEQ // TOK_TILE
HALO_PER_TILE = TOK_TILE // HALO


LANE_TILES = tuple((lo, min(128, FF_BLK - lo)) for lo in range(0, FF_BLK, 128))


def _taps(w_ref, b_ref, half, lanes, rows):
    shape = (rows, lanes.stop - lanes.start)
    return ([jnp.broadcast_to(w_ref[half, j:j + 1, lanes], shape) for j in range(3)],
            jnp.broadcast_to(b_ref[half, :, lanes], shape))


def _conv_strips(u_ref, ub_ref, ua_ref, taps, lanes, width, n_strips, first):
    row = lax.broadcasted_iota(jnp.int32, (HALO, width), 0)
    prev = [[pltpu.roll(jnp.where(first, 0.0, ub_ref[half, :, lanes]), k, 0) for k in (1, 2)] for half in range(2)]
    for s in range(n_strips + (ua_ref is not None)):
        u3, conv = [], []
        for half in range(2):
            cur = u_ref[half, s * HALO:(s + 1) * HALO, lanes] if s < n_strips else ua_ref[half, :, lanes]
            rolled = [pltpu.roll(cur, k, 0) for k in (1, 2)]
            frames = [jnp.where(row >= 2, rolled[1], prev[half][1]), jnp.where(row >= 1, rolled[0], prev[half][0]), cur]
            prev[half] = rolled
            w3, bias = taps[half]
            u3.append(frames)
            conv.append(bias + frames[0] * w3[0] + frames[1] * w3[1] + frames[2] * w3[2])
        yield s, u3, conv


def _pair_specs(pairs):
    tile = pl.BlockSpec((pairs, None, TOK_TILE, FF_BLK), lambda b, i: (0, b, i, 0))
    before = pl.BlockSpec((pairs, None, HALO, FF_BLK), lambda b, i: (0, b, jnp.maximum(i * HALO_PER_TILE - 1, 0), 0))
    after = pl.BlockSpec((pairs, None, HALO, FF_BLK),
                         lambda b, i: (0, b, jnp.minimum((i + 1) * HALO_PER_TILE, SEQ // HALO - 1), 0))

    def vec(rows):
        return pl.BlockSpec((2, None, rows, FF_BLK), lambda b, i: (0, b, 0, 0))

    return tile, before, after, vec


N_STRIPS = TOK_TILE // HALO


def _up_conv_fwd(h2, wt_up, w_conv, b_conv):
    steps = N_TOK_TILES // 2

    def body(h_ref, h_next, wg_ref, wv_ref, w_ref, b_ref, u_ref, a_ref, buf_a, buf_b, carry):
        j = pl.program_id(1)

        def project(hv, buf):
            buf[0] = _dot(hv, wg_ref[...], tb=True)
            buf[1] = _dot(hv, wv_ref[...], tb=True)

        def conv(buf, row0):
            u_ref[:, row0:row0 + TOK_TILE, :] = buf[...]
            for lo, width in LANE_TILES:
                lanes = slice(lo, lo + width)
                taps = [_taps(w_ref, b_ref, half, lanes, HALO) for half in range(2)]
                pending = None
                for s, _, (cg, cv) in _conv_strips(buf, carry, None, taps, lanes, width, N_STRIPS, False):
                    act = cg * _sigmoid(cg) * cv
                    if s % 2 == 0:
                        pending = act
                    else:
                        a_ref[0, row0 + (s - 1) * HALO:row0 + (s + 1) * HALO, lanes] = (
                            jnp.concatenate([pending, act], axis=0).astype(BF))
            carry[...] = buf[:, TOK_TILE - HALO:, :]

        @pl.when(j == 0)
        def _():
            project(h_ref[0:TOK_TILE, :], buf_a)
            carry[...] = jnp.zeros_like(carry)

        project(h_ref[TOK_TILE:, :], buf_b)
        conv(buf_a, 0)
        project(h_next[...], buf_a)
        conv(buf_b, TOK_TILE)

    w_blk = lambda half: pl.BlockSpec((FF_BLK, D_MODEL), lambda b, j: (b + 4 * half, 0))
    vec = lambda rows: pl.BlockSpec((2, None, rows, FF_BLK), lambda b, j: (0, b, 0, 0))
    u_buf = pltpu.VMEM((2, TOK_TILE, FF_BLK), F32)
    return pl.pallas_call(
        body, name="up_conv_fwd", grid=(4, steps),
        in_specs=[pl.BlockSpec((2 * TOK_TILE, D_MODEL), lambda b, j: (j, 0)),
                  pl.BlockSpec((TOK_TILE, D_MODEL), lambda b, j: (jnp.minimum(2 * j + 2, N_TOK_TILES - 1), 0)),
                  w_blk(0), w_blk(1), vec(3), vec(1)],
        out_specs=[pl.BlockSpec((2, None, 2 * TOK_TILE, FF_BLK), lambda b, j: (0, b, j, 0)),
                   pl.BlockSpec((1, None, 2 * TOK_TILE, FF_BLK), lambda b, j: (0, b, j, 0))],
        out_shape=[jax.ShapeDtypeStruct((2, 4, SEQ, FF_BLK), F32), jax.ShapeDtypeStruct((1, 4, SEQ, FF_BLK), BF)],
        scratch_shapes=[u_buf, u_buf, pltpu.VMEM((2, HALO, FF_BLK), F32)],
        compiler_params=_params("parallel", "arbitrary"),
    )(*map(_in_hbm, (h2, h2, wt_up, wt_up, w_conv, b_conv)))


def _conv_bwd(u, da, w_conv, b_conv):
    def body(u_ref, ub_ref, ua_ref, da_ref, daa_ref, w_ref, b_ref, du_ref, dw_ref, db_ref):
        i = pl.program_id(1)

        @pl.when(i == 0)
        def _():
            dw_ref[...] = jnp.zeros_like(dw_ref)
            db_ref[...] = jnp.zeros_like(db_ref)

        for lo, width in LANE_TILES:
            lanes = slice(lo, lo + width)
            row = lax.broadcasted_iota(jnp.int32, (HALO, width), 0)
            taps = [_taps(w_ref, b_ref, half, lanes, HALO) for half in range(2)]
            acc_w = [[jnp.zeros((HALO, width), F32) for _ in range(3)] for _ in range(2)]
            acc_b = [jnp.zeros((HALO, width), F32) for _ in range(2)]
            da_pair, pending = None, [None, None]
            dc_prev, up_prev = [None, None], [None, None]
            for s, u3, (cg, cv) in _conv_strips(u_ref, ub_ref, ua_ref, taps, lanes, width, N_STRIPS, i == 0):
                act, dact = _silu_parts(cg)
                if s == N_STRIPS:
                    da = jnp.where(i < N_TOK_TILES - 1, daa_ref[0, :, lanes].astype(F32), 0.0)
                elif s % 2 == 0:
                    da_pair = da_ref[0, s * HALO:(s + 2) * HALO, lanes].astype(F32)
                    da = da_pair[:HALO]
                else:
                    da = da_pair[HALO:]
                dc = (da * cv * dact, da * act)
                for half in range(2):
                    up = [pltpu.roll(dc[half], HALO - k, 0) for k in (1, 2)]
                    if s < N_STRIPS:
                        for j in range(3):
                            acc_w[half][j] = acc_w[half][j] + dc[half] * u3[half][j]
                        acc_b[half] = acc_b[half] + dc[half]
                    if s >= 1:
                        w3 = taps[half][0]
                        du = (dc_prev[half] * w3[2] + jnp.where(row < HALO - 1, up_prev[half][0], up[0]) * w3[1]
                              + jnp.where(row < HALO - 2, up_prev[half][1], up[1]) * w3[0])
                        if (s - 1) % 2 == 0:
                            pending[half] = du
                        else:
                            du_ref[half, (s - 2) * HALO:s * HALO, lanes] = jnp.concatenate([pending[half], du],
                                                                                           axis=0).astype(BF)
                    dc_prev[half], up_prev[half] = dc[half], up
            for half in range(2):
                for j in range(3):
                    dw_ref[half, j:j + 1, lanes] += jnp.sum(acc_w[half][j], axis=0, keepdims=True)
                db_ref[half, :, lanes] += jnp.sum(acc_b[half], axis=0, keepdims=True)

    tile, before, after, vec = _pair_specs(2)
    da_tile, _, da_after_spec, _ = _pair_specs(1)
    return pl.pallas_call(
        body, name="conv_bwd", grid=(4, N_TOK_TILES),
        in_specs=[tile, before, after, da_tile, da_after_spec, vec(3), vec(1)],
        out_specs=[tile, vec(3), vec(1)],
        out_shape=[jax.ShapeDtypeStruct((2, 4, SEQ, FF_BLK), BF), jax.ShapeDtypeStruct((2, 4, 3, FF_BLK), F32),
                   jax.ShapeDtypeStruct((2, 4, 1, FF_BLK), F32)],
        compiler_params=_params("parallel", "arbitrary"),
    )(*map(_in_hbm, (u, u, u, da, da, w_conv, b_conv)))


W_IN_SEGMENTS = ((R_POOL, POOL_WIDTH, C_POOL), (R_QKV, QKV_W, C_QKV), (R_OG, D_MODEL, C_OG), (R_GK, GATE_RANK, C_GK),
                 (R_GATE, GATE_W, C_GATE))


def _slab_pieces(d):
    lo, hi = d * IN_SHARD, (d + 1) * IN_SHARD
    pieces = []
    for start, n, at in W_IN_SEGMENTS:
        a, b = max(lo, start), min(hi, start + n)
        if a < b:
            assert (a - lo) % 2 == 0 and (b - a) % 2 == 0 and (at + a - start) % 2 == 0
            pieces.append(((a - lo) // 2, (b - a) // 2, (at + a - start) // 2))
    return pieces


def _unshard_w_in(slabs):
    def body(slab_ref, cat_ref):
        d = pl.program_id(0)
        src = slab_ref.bitcast(jnp.uint32)
        dst = cat_ref.bitcast(jnp.uint32)

        @pl.when(d == 0)
        def _():
            cat_ref[C_GK:, :] = jnp.zeros((GK_PAD, D_MODEL), BF)

        for dd in range(N_DEV):
            @pl.when(d == dd)
            def _():
                for a, n, at in _slab_pieces(dd):
                    dst[pl.ds(at, n), :] = src[0, pl.ds(a, n), :]

    return pl.pallas_call(
        body, name="unshard_w_in", grid=(N_DEV,),
        in_specs=[pl.BlockSpec((1, IN_SHARD, D_MODEL), lambda d: (d, 0, 0))], out_specs=_const_spec((N_DZ, D_MODEL)),
        out_shape=jax.ShapeDtypeStruct((N_DZ, D_MODEL), BF), compiler_params=_params("arbitrary"),
    )(_in_hbm(slabs))


def _shard_d_w_in(d_cat):
    def body(cat_ref, slab_ref):
        d = pl.program_id(0)
        cat = cat_ref.bitcast(jnp.uint32)
        dst = slab_ref.bitcast(jnp.uint32)
        for dd in range(N_DEV):
            @pl.when(d == dd)
            def _():
                for a, n, at in _slab_pieces(dd):
                    dst[0, pl.ds(a, n), :] = cat[pl.ds(at, n), :]

    return pl.pallas_call(
        body, name="shard_d_w_in", grid=(N_DEV,), in_specs=[_const_spec((N_DZ, D_MODEL))],
        out_specs=pl.BlockSpec((1, IN_SHARD, D_MODEL), lambda d: (d, 0, 0)),
        out_shape=jax.ShapeDtypeStruct((N_DEV, IN_SHARD, D_MODEL), BF), compiler_params=_params("parallel"),
    )(_in_hbm(d_cat))


ANY = pl.BlockSpec(memory_space=pl.ANY)


def _place():
    x, y, c = lax.axis_index("x"), lax.axis_index("y"), lax.axis_index("c")
    other_chips = [(1 - x, y), (x, 1 - y), (1 - x, 1 - y)]
    return x, y, c, other_chips


SEM = pl.BlockSpec(memory_space=pltpu.SEMAPHORE)
IN_HBM = pl.BlockSpec(memory_space=pltpu.HBM)
SPLIT_PARAMS = pltpu.CompilerParams(has_side_effects=pltpu.SideEffectType.DATAFLOW_SIDE_EFFECTING)


def _gather_first(refs, send_sems, recv_sems):
    x, y, c, chips = _place()
    targets = [(x, y, 1 - c)] + [(px, py, c) for px, py in chips]
    return [pltpu.make_async_remote_copy(src_ref=refs[2 * a], dst_ref=refs[2 * a + 1].at[4 * x + 2 * y + c],
                                         send_sem=send_sems.at[4 * a + k], recv_sem=recv_sems.at[4 * a + k],
                                         device_id=to, device_id_type=MESH)
            for a in range(len(refs) // 2) for k, to in enumerate(targets)]


def _gather_direct(refs, send_sems, recv_sems):
    x, y, c, _ = _place()
    flips = [(dx, dy, dc) for dx in (0, 1) for dy in (0, 1) for dc in (0, 1) if dx + dy + dc]
    targets = [(1 - x if dx else x, 1 - y if dy else y, 1 - c if dc else c) for dx, dy, dc in flips]
    return [pltpu.make_async_remote_copy(src_ref=refs[2 * a], dst_ref=refs[2 * a + 1].at[4 * x + 2 * y + c],
                                         send_sem=send_sems.at[7 * a + k], recv_sem=recv_sems.at[7 * a + k],
                                         device_id=to, device_id_type=MESH)
            for a in range(len(refs) // 2) for k, to in enumerate(targets)]


def _gather_second(refs, send_sems, recv_sems):
    x, y, c, chips = _place()
    copies = []
    for a, land in enumerate(refs):
        for j, (px, py) in enumerate(chips):
            block = land.at[4 * px + 2 * py + c]
            copies.append(pltpu.make_async_remote_copy(src_ref=block, dst_ref=block, send_sem=send_sems.at[3 * a + j],
                                                       recv_sem=recv_sems.at[3 * a + j], device_id=(x, y, 1 - c),
                                                       device_id_type=MESH))
    return copies


def _reduce_first(refs, send_sems, recv_sems):
    x, y, c, _ = _place()
    return [pltpu.make_async_remote_copy(src_ref=refs[2 * a].at[j, 1 - c], dst_ref=refs[2 * a + 1].at[j],
                                         send_sem=send_sems.at[4 * a + j], recv_sem=recv_sems.at[4 * a + j],
                                         device_id=(x, y, 1 - c), device_id_type=MESH)
            for a in range(len(refs) // 2) for j in range(4)]


def _reduce_second(refs, send_sems, recv_sems):
    _, _, c, chips = _place()
    return [pltpu.make_async_remote_copy(src_ref=refs[2 * a].at[2 * px + py], dst_ref=refs[2 * a + 1].at[k],
                                         send_sem=send_sems.at[3 * a + k], recv_sem=recv_sems.at[3 * a + k],
                                         device_id=(px, py, c), device_id_type=MESH)
            for a in range(len(refs) // 2) for k, (px, py) in enumerate(chips)]


def _split_start(name, groups):
    arrays = [a for g in groups for a in g[0]]
    n = len(arrays)

    def body(*refs):
        sems = refs[n:n + 2 * len(groups)]
        at = 0
        for gi, (members, _, build) in enumerate(groups):
            for cp in build(refs[at:at + len(members)], sems[2 * gi], sems[2 * gi + 1]):
                cp.start()
            at += len(members)
        refs[-1][...] = jnp.zeros_like(refs[-1])

    sem_shapes = [pltpu.SemaphoreType.DMA((g[1],)) for g in groups for _ in range(2)]
    outs = pl.pallas_call(
        body, name=name, in_specs=[IN_HBM] * n,
        out_shape=(*sem_shapes, *[pltpu.HBM(a.shape, a.dtype) for a in arrays], jax.ShapeDtypeStruct((8, 128), F32)),
        out_specs=(*[SEM] * len(sem_shapes), *[IN_HBM] * n, pl.BlockSpec(memory_space=pltpu.VMEM)),
        input_output_aliases={i: len(sem_shapes) + i for i in range(n)}, compiler_params=SPLIT_PARAMS,
    )(*[pltpu.with_memory_space_constraint(a, pltpu.HBM) for a in arrays])
    per_group, at = [], len(sem_shapes)
    for gi, (members, _, _) in enumerate(groups):
        per_group.append((outs[2 * gi], outs[2 * gi + 1], list(outs[at:at + len(members)])))
        at += len(members)
    return per_group, outs[-1]


def _split_wait(name, started, build, after):
    send_sems, recv_sems, arrays = started
    n = len(arrays)
    after = after if isinstance(after, (tuple, list)) else (after,)

    def body(*refs):
        for cp in build(refs[:n], refs[n], refs[n + 1]):
            cp.wait_send()
            cp.wait_recv()

    return pl.pallas_call(
        body, name=name, in_specs=[IN_HBM] * n + [SEM, SEM] + [ANY] * len(after),
        out_shape=tuple(pltpu.HBM(a.shape, a.dtype) for a in arrays), out_specs=tuple([IN_HBM] * n),
        input_output_aliases={i: i for i in range(n)}, compiler_params=SPLIT_PARAMS,
    )(*arrays, send_sems, recv_sems, *after)


def _gather_landing(shard, me):
    return lax.dynamic_update_slice(lax.empty((N_DEV,) + shard.shape, shard.dtype), shard[None],
                                    (me,) + (0,) * shard.ndim)


ADAM_LANE_TILE = 256


def _tile_2d(rows, cols):
    for t in (256, 176, 128):
        if rows % t == 0:
            return t, cols
    return rows, ADAM_LANE_TILE


def _pair_sum(part, recv, core, name):
    _, rows, cols = recv.shape
    tr, tc = rows, cols

    def body(c_ref, p_ref, r_ref, o_ref):
        del c_ref
        o_ref[...] = (p_ref[...].astype(F32) + r_ref[...].astype(F32)).astype(BF)

    grid_spec = pltpu.PrefetchScalarGridSpec(
        num_scalar_prefetch=1, grid=(4, rows // tr, cols // tc),
        in_specs=[pl.BlockSpec((None, None, tr, tc), lambda j, i, k, c_ref: (j, c_ref[0], i, k)),
                  pl.BlockSpec((None, tr, tc), lambda j, i, k, c_ref: (j, i, k))],
        out_specs=pl.BlockSpec((None, tr, tc), lambda j, i, k, c_ref: (j, i, k)))
    return pl.pallas_call(
        body, name=name, grid_spec=grid_spec, out_shape=jax.ShapeDtypeStruct(recv.shape, BF),
        compiler_params=_params("parallel", "parallel", "parallel"),
    )(core, *map(_in_hbm, (part, recv)))


def _adamw(w, g, m, v):
    m = ADAM_B1 * m + (1.0 - ADAM_B1) * g
    v = ADAM_B2 * v + (1.0 - ADAM_B2) * (g * g)
    delta = -ADAM_LR * ((m / ADAM_C1) / (jnp.sqrt(v / ADAM_C2) + ADAM_EPS) + ADAM_WD * w)
    return delta, m, v


def _chip_sum_adamw(sums, recv, w, m, v, chip, name):
    rows, cols = w.shape
    tr, tc = _tile_2d(rows, cols)

    def body(chip_ref, s_ref, r_ref, w_ref, m_ref, v_ref, g_out, d_out, m_out, v_out):
        del chip_ref
        g = s_ref[...].astype(F32)
        for k in range(3):
            g = g + r_ref[k].astype(F32)
        g_out[...] = g
        d_out[...], m_out[...], v_out[...] = _adamw(w_ref[...], g, m_ref[...], v_ref[...])

    tile = pl.BlockSpec((tr, tc), lambda i, k, chip_ref: (i, k))
    grid_spec = pltpu.PrefetchScalarGridSpec(
        num_scalar_prefetch=1, grid=(rows // tr, cols // tc),
        in_specs=[pl.BlockSpec((None, tr, tc), lambda i, k, chip_ref: (chip_ref[0], i, k)),
                  pl.BlockSpec((3, tr, tc), lambda i, k, chip_ref: (0, i, k)), tile, tile, tile],
        out_specs=[tile] * 4)
    return pl.pallas_call(
        body, name=name, grid_spec=grid_spec, out_shape=[jax.ShapeDtypeStruct((rows, cols), F32)] * 4,
        compiler_params=_params("parallel", "parallel"),
    )(chip, *map(_in_hbm, (sums, recv, w, m, v)))


def _small_sum_adamw(me, entries, loss_parts):
    def whole(shape, squeeze=0, pick=False):
        blk = (None,) * squeeze + tuple(shape[squeeze:])
        if pick:
            blk = (shape[0], None) + tuple(shape[2:])
            return pl.BlockSpec(blk, lambda i, me_ref: (0, me_ref[0]) + (0,) * (len(shape) - 2))
        return pl.BlockSpec(blk, lambda i, me_ref: (0,) * len(shape))

    in_specs, out_specs, out_shape, args = [], [], [], []
    for parts, w, m, v, sharded in entries:
        lead = w.ndim - (parts.ndim - (2 if sharded else 1))
        in_specs += [whole(parts.shape, pick=sharded)] + [whole(w.shape, squeeze=lead)] * 3
        out_specs += [whole(w.shape, squeeze=lead)] * 4
        out_shape += [jax.ShapeDtypeStruct(w.shape, F32)] * 4
        args += [parts, w, m, v]
    in_specs.append(whole(loss_parts.shape))
    out_specs.append(whole(loss_parts.shape[1:]))
    out_shape.append(jax.ShapeDtypeStruct(loss_parts.shape[1:], F32))
    n = len(entries)

    def added(p_ref):
        total = p_ref[0]
        for d in range(1, N_DEV):
            total = total + p_ref[d]
        return total

    def body(me_ref, *refs):
        del me_ref
        ins, outs = refs[:4 * n + 1], refs[4 * n + 1:]
        for e in range(n):
            p_ref, w_ref, m_ref, v_ref = ins[4 * e:4 * e + 4]
            g_out, d_out, m_out, v_out = outs[4 * e:4 * e + 4]
            g = added(p_ref)
            g_out[...] = g
            d_out[...], m_out[...], v_out[...] = _adamw(w_ref[...], g, m_ref[...], v_ref[...])
        outs[4 * n][...] = added(ins[4 * n])

    grid_spec = pltpu.PrefetchScalarGridSpec(num_scalar_prefetch=1, grid=(1,), in_specs=in_specs, out_specs=out_specs)
    outs = pl.pallas_call(body, name="small_sum_adamw", grid_spec=grid_spec, out_shape=out_shape,
                          compiler_params=_params("arbitrary"))(me, *map(_in_hbm, args + [loss_parts]))
    return [outs[4 * e:4 * e + 4] for e in range(n)], outs[4 * n]


MM_TILE = 512
N_MM_TILES = SEQ // MM_TILE
CAT_TILE = 512
N_CAT_TILES = N_CAT // CAT_TILE
DZ_TILE = 640


def kernel(x, g_mix, w_in, b_gate, w_gk_up, b_gk, w_pool_grp, pool_scale, g_gla_head, w_pool_proj, w_gla_proj, w_out, g_ffn, w_up, w_conv, b_conv, w_down, g_final, loss_target, m_g_mix, m_w_in, m_b_gate, m_w_gk_up, m_b_gk, m_w_pool_grp, m_pool_scale, m_g_gla_head, m_w_pool_proj, m_w_gla_proj, m_w_out, m_g_ffn, m_w_up, m_w_conv, m_b_conv, m_w_down, m_g_final, v_g_mix, v_w_in, v_b_gate, v_w_gk_up, v_b_gk, v_w_pool_grp, v_pool_scale, v_g_gla_head, v_w_pool_proj, v_w_gla_proj, v_w_out, v_g_ffn, v_w_up, v_w_conv, v_b_conv, v_w_down, v_g_final):
    xi, yi, ci = lax.axis_index("x"), lax.axis_index("y"), lax.axis_index("c")
    me = 4 * xi + 2 * yi + ci
    core = jnp.reshape(ci, (1,)).astype(jnp.int32)
    chip = jnp.reshape(2 * xi + yi, (1,)).astype(jnp.int32)
    xs, target = x[0], loss_target[0]

    big = dict(w_in=w_in[0].T, w_pool_proj=w_pool_proj[0], w_gla_proj=w_gla_proj[0], w_out=w_out[0], w_up=w_up[0].T,
               w_down=w_down[0])
    moments = dict(w_in=(m_w_in[0].T, v_w_in[0].T), w_pool_proj=(m_w_pool_proj[0], v_w_pool_proj[0]),
                   w_gla_proj=(m_w_gla_proj[0], v_w_gla_proj[0]), w_out=(m_w_out[0], v_w_out[0]),
                   w_up=(m_w_up[0].T, v_w_up[0].T), w_down=(m_w_down[0], v_w_down[0]))
    names = list(big)
    shards = {k: big[k].astype(BF) for k in names}
    shards["w_gk_up"], shards["w_conv"] = w_gk_up[0], w_conv[0]
    gather_groups = (("w_in", "w_gk_up"), ("w_pool_proj", "w_gla_proj", "w_out"), ("w_up", "w_down", "w_conv"))
    started, token = _split_start("gather_start", [
        ([t for k in g for t in (shards[k], _gather_landing(shards[k], me))], 4 * len(g), _gather_first)
        for g in gather_groups])

    def gather_pass(gi, after):
        lands = list(_split_wait(f"gather_wait_{gi}", started[gi], _gather_first, after)[1::2])
        passed, tkn = _split_start(f"gather_pass_{gi}", [(lands, 3 * len(lands), _gather_second)])
        return passed[0], tkn

    def gather_done(gi, passed, after):
        return dict(zip(gather_groups[gi], _split_wait(f"gather_pass_wait_{gi}", passed, _gather_second, after)))

    tok = lambda i, j, k: (i, 0)
    whole = lambda i, j, k: (0, 0)
    kblk = lambda i, j, k: (k, 0)
    ff_seq = (None, None, SEQ, FF_BLK)

    h = _rms_fwd(xs, g_mix + token[:1, :1], "rms_mix")
    wg = gather_done(0, gather_pass(0, h)[0], h)
    wt_cat = _unshard_w_in(wg["w_in"])
    wgk_pad = jnp.pad(wg["w_gk_up"].transpose(1, 0, 2).reshape(GATE_RANK, GLA_DK), ((0, GK_PAD - GATE_RANK), (0, 0)))
    zcat = _mm(h, wt_cat, out_shape=(SEQ, N_CAT), out_dtype=BF, grid=(N_CAT_TILES, 1, 1),
               blk_a=(SEQ, D_MODEL), blk_b=(CAT_TILE, D_MODEL), blk_o=(SEQ, CAT_TILE),
               map_a=whole, map_b=lambda j, i, k: (j, 0), map_o=lambda j, i, k: (0, j), tb=True, name="mm_in")
    la = _gk_fwd(h, wt_cat, wgk_pad, b_gk)
    passed, tkn = gather_pass(1, la)
    o, states = _gla_fwd(zcat, la, tkn)
    wg = gather_done(1, passed, o)
    wpp = wg["w_pool_proj"].transpose(1, 0, 2).reshape(POOL_WIDTH, D_MODEL)
    wgp = wg["w_gla_proj"].reshape(D_MODEL, D_MODEL)
    wout = wg["w_out"].reshape(D_MODEL, D_MODEL)
    og = _post_gla_fwd(o, zcat, g_gla_head)
    ps = _pool_fwd(zcat, w_pool_grp[0], pool_scale)
    passed, tkn = gather_pass(2, (og, ps))
    y_pool, y_gla, mixed, x1, h2 = _mix_out_fwd(ps, og, zcat, xs, wpp, wgp, wout, b_gate, g_ffn, tkn)
    wg = gather_done(2, passed, h2)
    wt_up = wg["w_up"].reshape(2 * D_FF, D_MODEL)
    wdown = wg["w_down"].reshape(D_FF, D_MODEL)
    wconv4 = wg["w_conv"].reshape(2, 4, 3, FF_BLK)
    bconv4 = b_conv.reshape(2, 4, 1, FF_BLK)
    blk4 = lambda b, i, k: (b // 4, b % 4, 0, 0)
    u4, act = _up_conv_fwd(h2, wt_up, wconv4, bconv4)
    loss_part, dx2, dx2_bf, dg_final = _mm_tokens(
        act, wdown, blk_a=(None, 4, TOK_MM_TILE, FF_BLK), map_a=lambda i: (0, 0, i, 0),
        pieces=[(b, b * FF_BLK, FF_BLK) for b in range(4)], res=x1, then=("loss", g_final.reshape(1, D_MODEL), target),
        name="mm_down_loss")

    da = _mm(dx2_bf, wdown, out_shape=(1, 4, SEQ, FF_BLK), out_dtype=BF, grid=(4, 1, 1),
             blk_a=(SEQ, D_MODEL), blk_b=(FF_BLK, D_MODEL), blk_o=ff_seq,
             map_a=whole, map_b=lambda b, i, k: (b, 0), map_o=lambda b, i, k: (0, b, 0, 0), tb=True, name="mm_d_act")
    d_wdown = _mm(act, dx2_bf, out_shape=(D_FF, D_MODEL), out_dtype=BF, grid=(4, 1, 1),
                  blk_a=ff_seq, blk_b=(SEQ, D_MODEL), blk_o=(FF_BLK, D_MODEL),
                  map_a=lambda b, i, k: (0, b, 0, 0), map_b=whole, map_o=lambda b, i, k: (b, 0), ta=True,
                  name="mm_d_wdown")
    du4, d_wconv, d_bconv = _conv_bwd(u4, da, wconv4, bconv4)
    d_wt_up = _mm(du4, h2, out_shape=(2 * D_FF, D_MODEL), out_dtype=BF, grid=(N_DEV, 1, 1),
                  blk_a=ff_seq, blk_b=(SEQ, D_MODEL), blk_o=(FF_BLK, D_MODEL),
                  map_a=blk4, map_b=whole, map_o=lambda b, i, k: (b, 0), ta=True, name="mm_d_wup")
    res = {}

    def reduce_start(keys, parts):
        arrays = [t for k in keys for t in (parts[k], lax.empty((4,) + parts[k].shape[2:], BF))]
        st, tkn = _split_start("reduce_start_" + keys[0], [(arrays, 4 * len(keys), _reduce_first)])
        return st[0], tkn

    def reduce_cross(keys, st, after):
        arrays = _split_wait("reduce_wait_" + keys[0], st, _reduce_first, after)
        sums = [_pair_sum(p, r, core, "pair_sum_" + k) for k, p, r in zip(keys, arrays[0::2], arrays[1::2])]
        arrays = [t for s in sums for t in (s, lax.empty((3,) + s.shape[1:], BF))]
        st2, tkn = _split_start("reduce_cross_" + keys[0], [(arrays, 3 * len(keys), _reduce_second)])
        return st2[0], tkn

    def reduce_done(keys, st2, after):
        arrays = _split_wait("reduce_cross_wait_" + keys[0], st2, _reduce_second, after)
        for k, s, r in zip(keys, arrays[0::2], arrays[1::2]):
            outs = _chip_sum_adamw(s, r, big[k], moments[k][0], moments[k][1], chip, "adamw_" + k)
            res[k] = [(t.T if k in ("w_in", "w_up") else t)[None] for t in outs]

    ffn_keys = ("w_down", "w_up")
    ffn_red, tkn = reduce_start(ffn_keys, dict(w_down=d_wdown.reshape(4, 2, D_FF // N_DEV, D_MODEL),
                                               w_up=d_wt_up.reshape(4, 2, FF_BLK, D_MODEL)))
    dx1, dg_ffn = _mm_tokens(
        du4, wt_up, blk_a=(2, 4, TOK_MM_TILE, FF_BLK), map_a=lambda i: (0, 0, i, 0),
        pieces=[((b // 4, b % 4), b * FF_BLK, FF_BLK) for b in range(N_DEV)], after=tkn, then=("rms_bwd", x1, g_ffn, dx2),
        name="mm_d_h2_rms")

    sq_t = dict(out_shape=(D_MODEL, D_MODEL), grid=(1, 1, N_MM_TILES), blk_a=(MM_TILE, D_MODEL),
                blk_b=(MM_TILE, D_MODEL), blk_o=(D_MODEL, D_MODEL), map_a=kblk, map_b=kblk, map_o=whole, ta=True)
    d_wout = _mm(mixed, dx1, out_dtype=BF, name="mm_d_wout", **sq_t)
    dzcat, dy_pool, dy_gla, db_gate = _mix_bwd(dx1, wout, zcat, b_gate, y_pool, y_gla)
    ffn_red, tkn = reduce_cross(ffn_keys, ffn_red, db_gate)
    d_wgp = _mm(og, dy_gla, out_dtype=BF, after=tkn, name="mm_d_wgp", **sq_t)
    mix_keys = ("w_out", "w_gla_proj")
    mix_red, tkn = reduce_start(mix_keys, dict(w_out=d_wout.reshape(4, 2, D_MODEL // N_DEV, D_MODEL),
                                               w_gla_proj=d_wgp.reshape(4, 2, D_MODEL // N_DEV, D_MODEL)))
    dzcat, d_o, dg_head = _post_gla_bwd(dzcat, dy_gla, wgp, o, zcat, g_gla_head + tkn[:1, :1])
    dzcat, dla = _gla_bwd(dzcat, zcat, la, d_o, states)
    mix_red, tkn = reduce_cross(mix_keys, mix_red, dla)
    dzcat, d_wgk, db_gk = _gk_bwd(dzcat, dla, h, wt_cat, wgk_pad, b_gk + tkn[:1, :1])
    dps = _mm(dy_pool, wpp, out_shape=(SEQ, POOL_WIDTH), out_dtype=F32, grid=(N_MM_TILES, 1, 1),
              blk_a=(MM_TILE, D_MODEL), blk_b=(POOL_WIDTH, D_MODEL), blk_o=(MM_TILE, POOL_WIDTH),
              map_a=tok, map_b=whole, map_o=tok, tb=True, name="mm_d_ps")
    d_wpp = _mm(ps, dy_pool, out_shape=(POOL_WIDTH, D_MODEL), out_dtype=F32, grid=(1, 1, N_MM_TILES),
                blk_a=(MM_TILE, POOL_WIDTH), blk_b=(MM_TILE, D_MODEL), blk_o=(POOL_WIDTH, D_MODEL),
                map_a=kblk, map_b=kblk, map_o=whole, ta=True, name="mm_d_wpp")
    dzcat, d_wgrp, d_scale = _pool_bwd(dzcat, zcat, dps, w_pool_grp[0], pool_scale)
    row = lambda t: t.reshape(1, D_MODEL)
    conv_vec = lambda t: t.reshape(2, 4, 1, FF_BLK)
    small = [("b_gate", db_gate, b_gate, m_b_gate, v_b_gate, False),
             ("w_gk_up", d_wgk.reshape(GATE_RANK, N_DEV, GLA_DK // N_DEV).transpose(1, 0, 2), w_gk_up, m_w_gk_up,
              v_w_gk_up, True),
             ("b_gk", db_gk, b_gk, m_b_gk, v_b_gk, False),
             ("w_pool_grp", d_wgrp, w_pool_grp, m_w_pool_grp, v_w_pool_grp, False),
             ("pool_scale", d_scale, pool_scale, m_pool_scale, v_pool_scale, False),
             ("g_gla_head", dg_head, g_gla_head, m_g_gla_head, v_g_gla_head, False),
             ("g_ffn", dg_ffn, g_ffn, m_g_ffn, v_g_ffn, False),
             ("w_conv", d_wconv.reshape(N_DEV, 3, FF_BLK), w_conv, m_w_conv, v_w_conv, True),
             ("b_conv", d_bconv, conv_vec(b_conv), conv_vec(m_b_conv), conv_vec(v_b_conv), False),
             ("g_final", dg_final, row(g_final), row(m_g_final), row(v_g_final), False)]

    def small_start(parts, name):
        arrays = [t for p in parts for t in (p, _gather_landing(p, me))]
        st, tkn = _split_start(name, [(arrays, 7 * len(parts), _gather_direct)])
        return st[0], tkn

    small_sent, tkn = small_start([t[1] for t in small] + [loss_part], "small_start")
    d_wt_cat = _mm(dzcat, h, out_shape=(N_DZ, D_MODEL), out_dtype=BF, grid=(N_DZ // DZ_TILE, 1, 1),
                   blk_a=(SEQ, DZ_TILE), blk_b=(SEQ, D_MODEL), blk_o=(DZ_TILE, D_MODEL),
                   map_a=lambda j, i, k: (0, j), map_b=whole, map_o=lambda j, i, k: (j, 0), ta=True, after=tkn,
                   name="mm_d_wcat")
    in_keys = ("w_in", "w_pool_proj")
    in_red, tkn = reduce_start(in_keys, dict(
        w_in=_shard_d_w_in(d_wt_cat).reshape(4, 2, IN_SHARD, D_MODEL),
        w_pool_proj=d_wpp.reshape(POOL_WIDTH, N_DEV, D_MODEL // N_DEV).transpose(1, 0, 2).astype(BF)
        .reshape(4, 2, POOL_WIDTH, D_MODEL // N_DEV)))
    reduce_done(mix_keys, mix_red, tkn)
    in_red, tkn = reduce_cross(in_keys, in_red, res["w_out"][0])
    grad_x, dg_mix = _mm_tokens(dzcat, wt_cat, blk_a=(TOK_MM_TILE, N_DZ), map_a=lambda i: (i, 0),
                                pieces=[(None, 0, N_DZ)], after=tkn, then=("rms_bwd", xs, g_mix, dx1),
                                name="mm_d_h_rms")
    g_mix_sent, tkn = small_start([dg_mix], "g_mix_start")
    reduce_done(ffn_keys, ffn_red, (grad_x, tkn))
    gathered = _split_wait("small_wait", small_sent, _gather_direct, res["w_down"][0])[1::2]
    small.append(("g_mix", dg_mix, g_mix, m_g_mix, v_g_mix, False))
    gathered = list(gathered[:-1]) + [_split_wait("g_mix_wait", g_mix_sent, _gather_direct, gathered[0])[1], gathered[-1]]
    small_out, loss_sum = _small_sum_adamw(jnp.reshape(me, (1,)).astype(jnp.int32),
                                           [(p,) + t[2:] for p, t in zip(gathered, small)], gathered[-1])
    for t, outs in zip(small, small_out):
        res[t[0]] = list(outs)
    res["b_conv"] = [t.reshape(b_conv.shape) for t in res["b_conv"]]
    res["g_final"] = [t.reshape(g_final.shape) for t in res["g_final"]]

    reduce_done(in_keys, in_red, loss_sum)
    loss = loss_sum[0, 0]
    order =["g_mix", "w_in", "b_gate", "w_gk_up", "b_gk", "w_pool_grp", "pool_scale", "g_gla_head", "w_pool_proj",
             "w_gla_proj", "w_out", "g_ffn", "w_up", "w_conv", "b_conv", "w_down", "g_final"]
    return (loss, grad_x[None], *[res[k][0] for k in order], *[res[k][1] for k in order],
            *[res[k][2] for k in order], *[res[k][3] for k in order])
```

```python
import jax
import jax.numpy as jnp
from jax import lax
from jax.experimental import pallas as pl
from jax.experimental.pallas import tpu as pltpu

F32 = jnp.float32
BF = jnp.bfloat16
HIGHEST = lax.Precision.HIGHEST
MESH = pl.DeviceIdType.MESH

N_DEV = 8
SEQ = 2048
D_MODEL = 1024
CHUNK = 64
EPS = 1e-6
POOL_WIDTH = 512
POOL_WINDOWS = (2, 4, 8, 16)
POOL_GD = 128
POOL_HALO = 16
HEADS = 4
HK = 128
HV = 256
GLA_DK = 512
GATE_RANK = 16
GATE_NORM = 16.0
D_FF = 2816
FF_BLK = 704
IN_SHARD = 706
C_QKV, C_GATE, C_OG, C_POOL, C_GK = 0, 2048, 4096, 5120, 5632
N_CAT = 5632
GK_PAD = 128
N_DZ = N_CAT + GK_PAD
R_POOL, R_QKV, R_OG, R_GK, R_GATE = 0, 512, 2560, 3584, 3600

ADAM_LR, ADAM_B1, ADAM_B2, ADAM_EPS, ADAM_WD, ADAM_STEP = 0.001, 0.9, 0.999, 1e-08, 0.01, 10
ADAM_C1 = 1.0 - ADAM_B1 ** ADAM_STEP
ADAM_C2 = 1.0 - ADAM_B2 ** ADAM_STEP

VMEM_BYTES_V7X = 64 * 1024 * 1024
VMEM_LIMIT = VMEM_BYTES_V7X * 3 // 4

TOK_TILE = 256
HALO = 8
GLA_CPS = 4


def _params(*sem):
    return pltpu.CompilerParams(dimension_semantics=sem, vmem_limit_bytes=VMEM_LIMIT)


def _const_spec(shape):
    nd = len(shape)
    return pl.BlockSpec(shape, lambda *_: (0,) * nd)


def _in_hbm(t):
    return pltpu.with_memory_space_constraint(t, pltpu.HBM)


def _dot(a, b, ta=False, tb=False):
    dims = (((0 if ta else 1,), (1 if tb else 0,)), ((), ()))
    return lax.dot_general(a.astype(BF), b.astype(BF), dims, preferred_element_type=F32)


def _dot_exact(a, b):
    return jnp.dot(a, b, precision=HIGHEST, preferred_element_type=F32)


def _sigmoid(x):
    return 0.5 * jnp.tanh(0.5 * x) + 0.5


def _mm(a, b, *, out_shape, out_dtype, grid, blk_a, blk_b, blk_o, map_a, map_b, map_o, ta=False, tb=False,
        after=None, name):
    gk = grid[2]
    n_in = 2 + (after is not None)

    def body(*refs):
        a_ref, b_ref, o_ref = refs[0], refs[1], refs[n_in]
        prod = _dot(a_ref[...], b_ref[...], ta, tb)
        if gk == 1:
            o_ref[...] = prod.astype(out_dtype)
        else:
            acc = refs[n_in + 1]
            k = pl.program_id(2)

            @pl.when(k == 0)
            def _():
                acc[...] = prod

            @pl.when(k > 0)
            def _():
                acc[...] += prod

            @pl.when(k == gk - 1)
            def _():
                o_ref[...] = acc[...].astype(out_dtype)

    in_specs = [pl.BlockSpec(blk_a, map_a), pl.BlockSpec(blk_b, map_b)]
    args = [_in_hbm(a), _in_hbm(b)]
    if after is not None:
        in_specs.append(pl.BlockSpec(memory_space=pl.ANY))
        args.append(after)
    return pl.pallas_call(
        body, name=name, grid=grid, in_specs=in_specs, out_specs=pl.BlockSpec(blk_o, map_o),
        out_shape=jax.ShapeDtypeStruct(out_shape, out_dtype),
        scratch_shapes=[] if gk == 1 else [pltpu.VMEM(tuple(d for d in blk_o if d is not None), F32)],
        compiler_params=_params("parallel", "parallel", "arbitrary"),
    )(*args)


TOK_MM_TILE = 256


def _mm_tokens(a, w, *, blk_a, map_a, pieces, res=None, after=None, then=None, name):
    n_in = 2 + (res is not None) + (after is not None) + (0 if then is None else len(then) - 1)

    def accumulate(ref, part):
        @pl.when(pl.program_id(0) == 0)
        def _():
            ref[...] = part

        @pl.when(pl.program_id(0) > 0)
        def _():
            ref[...] += part

    def body(*refs):
        a_ref, w_ref = refs[:2]
        extra, outs = refs[n_in - (0 if then is None else len(then) - 1):n_in], refs[n_in:]
        total = None
        for idx, row, n in pieces:
            av = a_ref[...] if idx is None else a_ref[idx]
            prod = _dot(av, w_ref[row:row + n, :])
            total = prod if total is None else total + prod
        if res is not None:
            total = total + refs[2][...]
        if then is None:
            outs[0][...] = total
        elif then[0] == "rms_bwd":
            dx, part = _rms_bwd_tile(total, extra[0][...], extra[1][...], extra[2][...])
            outs[0][...] = dx
            accumulate(outs[1], part)
        else:
            lpart, dx, part = _loss_tile(total, extra[0][...], extra[1][...])
            outs[1][...] = dx
            outs[2][...] = dx.astype(BF)
            accumulate(outs[0], lpart)
            accumulate(outs[3], part)

    tile = pl.BlockSpec((TOK_MM_TILE, D_MODEL), lambda i: (i, 0))
    vec = _const_spec((1, D_MODEL))
    big = jax.ShapeDtypeStruct((SEQ, D_MODEL), F32)
    small = jax.ShapeDtypeStruct((1, D_MODEL), F32)
    in_specs = [pl.BlockSpec(blk_a, map_a), pl.BlockSpec(w.shape, lambda i: (0, 0), pipeline_mode=pl.Buffered(1))]
    args = [a, w]
    if res is not None:
        in_specs.append(tile)
        args.append(res)
    if after is not None:
        in_specs.append(pl.BlockSpec(memory_space=pl.ANY))
        args.append(after)
    if then is None:
        out_specs, out_shape = tile, big
    elif then[0] == "rms_bwd":
        in_specs += [tile, vec, tile]
        out_specs, out_shape = [tile, vec], [big, small]
    else:
        in_specs += [vec, tile]
        out_specs = [_const_spec((1, 128)), tile, tile, vec]
        out_shape = [jax.ShapeDtypeStruct((1, 128), F32), big, jax.ShapeDtypeStruct((SEQ, D_MODEL), BF), small]
    if then is not None:
        args += list(then[1:])
    return pl.pallas_call(
        body, name=name, grid=(SEQ // TOK_MM_TILE,), in_specs=in_specs, out_specs=out_specs, out_shape=out_shape,
        compiler_params=_params("parallel" if then is None else "arbitrary"),
    )(*[_in_hbm(t) for t in args])


def _rms_fwd(x, g, name):
    def body(x_ref, g_ref, o_ref):
        xv = x_ref[...]
        r = lax.rsqrt(jnp.mean(xv * xv, axis=-1, keepdims=True) + EPS)
        o_ref[...] = (xv * r * g_ref[...]).astype(BF)

    tile = pl.BlockSpec((TOK_TILE, D_MODEL), lambda i: (i, 0))
    return pl.pallas_call(
        body, name=name, grid=(SEQ // TOK_TILE,), in_specs=[tile, _const_spec((1, D_MODEL))], out_specs=tile,
        out_shape=jax.ShapeDtypeStruct((SEQ, D_MODEL), BF), compiler_params=_params("parallel"),
    )(*map(_in_hbm, (x, g)))


def _rms_bwd_tile(dyv, xv, gv, dresv):
    r = lax.rsqrt(jnp.mean(xv * xv, axis=-1, keepdims=True) + EPS)
    xn = xv * r
    dxn = dyv * gv
    return dresv + r * (dxn - xn * jnp.mean(dxn * xn, axis=-1, keepdims=True)), jnp.sum(dyv * xn, axis=0, keepdims=True)


def _loss_tile(xv, gv, tv):
    r = lax.rsqrt(jnp.mean(xv * xv, axis=-1, keepdims=True) + EPS)
    xn = xv * r
    err = xn * gv - tv
    lpart = jnp.full((1, 128), 0.5 * jnp.sum(jnp.mean(err * err, axis=-1, keepdims=True)), F32)
    dyv = err * (1.0 / D_MODEL)
    dxn = dyv * gv
    return lpart, r * (dxn - xn * jnp.mean(dxn * xn, axis=-1, keepdims=True)), jnp.sum(dyv * xn, axis=0, keepdims=True)


def _pool_counts(w):
    pos = lax.broadcasted_iota(jnp.int32, (SEQ, 1), 0).astype(F32)
    return jnp.minimum(pos + 1.0, float(w))


def _pool_window(u, w, ext):
    ext[pl.ds(POOL_HALO, SEQ), :] = u
    win = u
    for j in range(1, w):
        win = win + ext[pl.ds(POOL_HALO - j, SEQ), :]
    return win / _pool_counts(w) - u


def _pool_fwd(zcat, w_grp, scale):
    def body(z_ref, w_ref, s_ref, o_ref, ext):
        ext[pl.ds(0, POOL_HALO), :] = jnp.zeros((POOL_HALO, POOL_GD), F32)
        for g, w in enumerate(POOL_WINDOWS):
            cols = slice(g * POOL_GD, (g + 1) * POOL_GD)
            p = _pool_window(z_ref[:, cols].astype(F32), w, ext)
            o_ref[:, cols] = (_dot(p, w_ref[g]) * s_ref[:, cols]).astype(BF)

    return pl.pallas_call(
        body, name="pool_fwd", grid=(1,),
        in_specs=[pl.BlockSpec((SEQ, POOL_WIDTH), lambda i: (0, C_POOL // POOL_WIDTH)),
                  _const_spec((4, POOL_GD, POOL_GD)), _const_spec((1, POOL_WIDTH))],
        out_specs=_const_spec((SEQ, POOL_WIDTH)), out_shape=jax.ShapeDtypeStruct((SEQ, POOL_WIDTH), BF),
        scratch_shapes=[pltpu.VMEM((POOL_HALO + SEQ, POOL_GD), F32)], compiler_params=_params("arbitrary"),
    )(*map(_in_hbm, (zcat, w_grp, scale)))


def _pool_bwd(dzcat, zcat, dps, w_grp, scale):
    def body(dz_in, z_ref, dps_ref, w_ref, s_ref, dz_ref, dw_ref, dsc_ref, ext, ext2):
        del dz_in
        ext[pl.ds(0, POOL_HALO), :] = jnp.zeros((POOL_HALO, POOL_GD), F32)
        ext2[pl.ds(SEQ, POOL_HALO), :] = jnp.zeros((POOL_HALO, POOL_GD), F32)
        for g, w in enumerate(POOL_WINDOWS):
            cols = slice(g * POOL_GD, (g + 1) * POOL_GD)
            p = _pool_window(z_ref[:, cols].astype(F32), w, ext)
            wg = w_ref[g]
            pg = _dot(p, wg)
            dpsv = dps_ref[:, cols]
            dsc_ref[:, cols] = jnp.sum(dpsv * pg, axis=0, keepdims=True)
            dpg = dpsv * s_ref[:, cols]
            dw_ref[g] = _dot(p, dpg, ta=True)
            dp = _dot(dpg, wg, tb=True)
            dpc = dp / _pool_counts(w)
            ext2[pl.ds(0, SEQ), :] = dpc
            du = dpc
            for j in range(1, w):
                du = du + ext2[pl.ds(j, SEQ), :]
            dz_ref[:, cols] = (du - dp).astype(BF)

    return pl.pallas_call(
        body, name="pool_bwd", grid=(1,),
        in_specs=[pl.BlockSpec(memory_space=pl.ANY),
                  pl.BlockSpec((SEQ, POOL_WIDTH), lambda i: (0, C_POOL // POOL_WIDTH)),
                  _const_spec((SEQ, POOL_WIDTH)), _const_spec((4, POOL_GD, POOL_GD)), _const_spec((1, POOL_WIDTH))],
        out_specs=[pl.BlockSpec((SEQ, POOL_WIDTH), lambda i: (0, C_POOL // POOL_WIDTH)),
                   _const_spec((4, POOL_GD, POOL_GD)), _const_spec((1, POOL_WIDTH))],
        out_shape=[jax.ShapeDtypeStruct((SEQ, N_DZ), BF), jax.ShapeDtypeStruct((4, POOL_GD, POOL_GD), F32),
                   jax.ShapeDtypeStruct((1, POOL_WIDTH), F32)],
        scratch_shapes=[pltpu.VMEM((POOL_HALO + SEQ, POOL_GD), F32), pltpu.VMEM((SEQ + POOL_HALO, POOL_GD), F32)],
        input_output_aliases={0: 0}, compiler_params=_params("arbitrary"),
    )(*map(_in_hbm, (dzcat, zcat, dps, w_grp, scale)))


GK_TILE = 512


GK_ROWS = pl.BlockSpec((GK_PAD, D_MODEL), lambda i: (C_GK // GK_PAD, 0))


def _gk_fwd(h, wt_cat, wgk_pad, b_gk):
    def body(h_ref, wt_ref, w_ref, b_ref, la_ref):
        z_gk = _dot(h_ref[...], wt_ref[...], tb=True)
        pre = _dot(z_gk, w_ref[...]) + b_ref[...]
        la_ref[...] = (jnp.minimum(pre, 0.0) - jnp.log(1.0 + jnp.exp(-jnp.abs(pre)))) * (1.0 / GATE_NORM)

    return pl.pallas_call(
        body, name="gk_fwd", grid=(SEQ // GK_TILE,),
        in_specs=[pl.BlockSpec((GK_TILE, D_MODEL), lambda i: (i, 0)), GK_ROWS,
                  _const_spec((GK_PAD, GLA_DK)), _const_spec((1, GLA_DK))],
        out_specs=pl.BlockSpec((GK_TILE, GLA_DK), lambda i: (i, 0)),
        out_shape=jax.ShapeDtypeStruct((SEQ, GLA_DK), F32), compiler_params=_params("parallel"),
    )(*map(_in_hbm, (h, wt_cat, wgk_pad, b_gk)))


def _gk_bwd(dzcat, dla, h, wt_cat, wgk_pad, b_gk):
    def body(dz_in, dla_ref, h_ref, wt_ref, w_ref, b_ref, dz_ref, dw_ref, db_ref):
        del dz_in
        wv = w_ref[...]
        z_gk = _dot(h_ref[...], wt_ref[...], tb=True)
        pre = _dot(z_gk, wv) + b_ref[...]
        dpre = dla_ref[...] * (1.0 / GATE_NORM) * (1.0 - _sigmoid(pre))
        dz_ref[...] = _dot(dpre, wv, tb=True).astype(BF)
        dwp = _dot(z_gk, dpre, ta=True)[:GATE_RANK]
        dbp = jnp.sum(dpre, axis=0, keepdims=True)

        @pl.when(pl.program_id(0) == 0)
        def _():
            dw_ref[...] = dwp
            db_ref[...] = dbp

        @pl.when(pl.program_id(0) > 0)
        def _():
            dw_ref[...] += dwp
            db_ref[...] += dbp

    return pl.pallas_call(
        body, name="gk_bwd", grid=(SEQ // GK_TILE,),
        in_specs=[pl.BlockSpec(memory_space=pl.ANY), pl.BlockSpec((GK_TILE, GLA_DK), lambda i: (i, 0)),
                  pl.BlockSpec((GK_TILE, D_MODEL), lambda i: (i, 0)), GK_ROWS, _const_spec((GK_PAD, GLA_DK)),
                  _const_spec((1, GLA_DK))],
        out_specs=[pl.BlockSpec((GK_TILE, GK_PAD), lambda i: (i, C_GK // GK_PAD)), _const_spec((GATE_RANK, GLA_DK)),
                   _const_spec((1, GLA_DK))],
        out_shape=[jax.ShapeDtypeStruct((SEQ, N_DZ), BF), jax.ShapeDtypeStruct((GATE_RANK, GLA_DK), F32),
                   jax.ShapeDtypeStruct((1, GLA_DK), F32)],
        input_output_aliases={0: 0}, compiler_params=_params("arbitrary"),
    )(*map(_in_hbm, (dzcat, dla, h, wt_cat, wgk_pad, b_gk)))


GLA_ROWS = GLA_CPS * CHUNK
GLA_STEPS = SEQ // GLA_ROWS
QKV_W = 2048


def _tri():
    return lax.broadcasted_iota(jnp.int32, (CHUNK, CHUNK), 0) >= lax.broadcasted_iota(jnp.int32, (CHUNK, CHUNK), 1)


def _chunk_cumsum(la_ref, rows):
    return _dot_exact(_tri().astype(F32), la_ref[rows, :])


def _gla_chunk(qkv_ref, la_ref, rows, h, bc_all):
    tri = _tri()
    q = qkv_ref[rows, h * HK:(h + 1) * HK].astype(F32) * (HK ** -0.5)
    k = qkv_ref[rows, GLA_DK + h * HK:GLA_DK + (h + 1) * HK].astype(F32)
    v = qkv_ref[rows, 2 * GLA_DK + h * HV:2 * GLA_DK + (h + 1) * HV].astype(BF)
    la = la_ref[rows, h * HK:(h + 1) * HK]
    bc = bc_all[:, h * HK:(h + 1) * HK]
    e_pos, e_neg = jnp.exp(bc), jnp.exp(-bc)
    dl = jnp.exp(jnp.sum(la, axis=0, keepdims=True))
    q_fw, q_bw, k_fw, k_bw = q * e_pos, q * e_neg, k * e_neg, k * e_pos
    scores = jnp.where(tri, _dot(q_fw, k_fw, tb=True), _dot(q_bw, k_bw, tb=True))
    return tri, v, e_pos, e_neg, dl, q_fw, q_bw, k_fw, k_bw, scores


def _gla_fwd(zcat, la, after):
    def body(qkv_ref, la_ref, after_ref, o_ref, st_ref, state):
        del after_ref

        @pl.when(pl.program_id(0) == 0)
        def _():
            state[...] = jnp.zeros_like(state)

        for c in range(GLA_CPS):
            rows = slice(c * CHUNK, (c + 1) * CHUNK)
            bc_all = _chunk_cumsum(la_ref, rows)
            for h in range(HEADS):
                _, v, _, _, dl, q_fw, _, k_fw, _, scores = _gla_chunk(qkv_ref, la_ref, rows, h, bc_all)
                st = state[h]
                st_ref[c, h] = st
                o_ref[rows, h * HV:(h + 1) * HV] = _dot(scores, v) + _dot(q_fw, st, tb=True)
                state[h] = st * dl + _dot(v, k_fw * dl, ta=True)

    return pl.pallas_call(
        body, name="gla_fwd", grid=(GLA_STEPS,),
        in_specs=[pl.BlockSpec((GLA_ROWS, QKV_W), lambda i: (i, 0)), pl.BlockSpec((GLA_ROWS, GLA_DK), lambda i: (i, 0)),
                  pl.BlockSpec(memory_space=pl.ANY)],
        out_specs=[pl.BlockSpec((GLA_ROWS, D_MODEL), lambda i: (i, 0)),
                   pl.BlockSpec((GLA_CPS, HEADS, HV, HK), lambda i: (i, 0, 0, 0))],
        out_shape=[jax.ShapeDtypeStruct((SEQ, D_MODEL), F32),
                   jax.ShapeDtypeStruct((SEQ // CHUNK, HEADS, HV, HK), F32)],
        scratch_shapes=[pltpu.VMEM((HEADS, HV, HK), F32)], compiler_params=_params("arbitrary"),
    )(*map(_in_hbm, (zcat, la)), after)


def _gla_bwd(dzcat, zcat, la, d_o, states):
    def body(dz_in, qkv_ref, la_ref, do_ref, st_ref, dqkv_ref, dla_ref, dstate):
        del dz_in

        @pl.when(pl.program_id(0) == 0)
        def _():
            dstate[...] = jnp.zeros_like(dstate)

        last_row = lax.broadcasted_iota(jnp.int32, (CHUNK, HK), 0) == CHUNK - 1
        upper = (lax.broadcasted_iota(jnp.int32, (CHUNK, CHUNK), 0)
                 <= lax.broadcasted_iota(jnp.int32, (CHUNK, CHUNK), 1)).astype(F32)
        for c in reversed(range(GLA_CPS)):
            rows = slice(c * CHUNK, (c + 1) * CHUNK)
            bc_all = _chunk_cumsum(la_ref, rows)
            dbs = []
            for h in range(HEADS):
                tri, v, e_pos, e_neg, dl, q_fw, q_bw, k_fw, k_bw, scores = _gla_chunk(qkv_ref, la_ref, rows, h, bc_all)
                st = st_ref[c, h]
                dst = dstate[h]
                d_out = do_ref[rows, h * HV:(h + 1) * HV].astype(BF)
                k_dec = k_fw * dl
                dp = _dot(d_out, v, tb=True)
                dp_fw = jnp.where(tri, dp, 0.0)
                dp_bw = jnp.where(tri, 0.0, dp)
                dv = _dot(scores, d_out, ta=True) + _dot(k_dec, dst, tb=True)
                dk_dec = _dot(v, dst)
                dq_fw = _dot(dp_fw, k_fw) + _dot(d_out, st)
                dk_fw = _dot(dp_fw, q_fw, ta=True) + dk_dec * dl
                dq_bw = _dot(dp_bw, k_bw)
                dk_bw = _dot(dp_bw, q_bw, ta=True)
                ddl = jnp.sum(st * dst, axis=0, keepdims=True) + jnp.sum(k_fw * dk_dec, axis=0, keepdims=True)
                dstate[h] = dst * dl + _dot(d_out, q_fw, ta=True)
                dq = (dq_fw * e_pos + dq_bw * e_neg) * (HK ** -0.5)
                dk = dk_fw * e_neg + dk_bw * e_pos
                dbs.append(dq_fw * q_fw - dk_fw * k_fw - dq_bw * q_bw + dk_bw * k_bw + jnp.where(last_row, ddl * dl, 0.0))
                dqkv_ref[rows, h * HK:(h + 1) * HK] = dq.astype(BF)
                dqkv_ref[rows, GLA_DK + h * HK:GLA_DK + (h + 1) * HK] = dk.astype(BF)
                dqkv_ref[rows, 2 * GLA_DK + h * HV:2 * GLA_DK + (h + 1) * HV] = dv.astype(BF)
            dla_ref[rows, :] = _dot_exact(upper, jnp.concatenate(dbs, axis=1))

    rev = lambda i: (GLA_STEPS - 1 - i, 0)
    return pl.pallas_call(
        body, name="gla_bwd", grid=(GLA_STEPS,),
        in_specs=[pl.BlockSpec(memory_space=pl.ANY), pl.BlockSpec((GLA_ROWS, QKV_W), rev),
                  pl.BlockSpec((GLA_ROWS, GLA_DK), rev), pl.BlockSpec((GLA_ROWS, D_MODEL), rev),
                  pl.BlockSpec((GLA_CPS, HEADS, HV, HK), lambda i: (GLA_STEPS - 1 - i, 0, 0, 0))],
        out_specs=[pl.BlockSpec((GLA_ROWS, QKV_W), rev), pl.BlockSpec((GLA_ROWS, GLA_DK), rev)],
        out_shape=[jax.ShapeDtypeStruct((SEQ, N_DZ), BF), jax.ShapeDtypeStruct((SEQ, GLA_DK), F32)],
        scratch_shapes=[pltpu.VMEM((HEADS, HV, HK), F32)], input_output_aliases={0: 0},
        compiler_params=_params("arbitrary"),
    )(*map(_in_hbm, (dzcat, zcat, la, d_o, states)))


def _silu_parts(x):
    s = _sigmoid(x)
    return x * s, s * (1.0 + x * (1.0 - s))


def _post_gla_fwd(o, zcat, g_head):
    def body(o_ref, zog_ref, g_ref, out_ref):
        for h in range(HEADS):
            cols = slice(h * HV, (h + 1) * HV)
            ov = o_ref[:, cols]
            r = lax.rsqrt(jnp.mean(ov * ov, axis=-1, keepdims=True) + EPS)
            act, _ = _silu_parts(zog_ref[:, cols].astype(F32))
            out_ref[:, cols] = (ov * r * g_ref[...] * act).astype(BF)

    tile = pl.BlockSpec((TOK_TILE, D_MODEL), lambda i: (i, 0))
    return pl.pallas_call(
        body, name="post_gla_fwd", grid=(SEQ // TOK_TILE,),
        in_specs=[tile, pl.BlockSpec((TOK_TILE, D_MODEL), lambda i: (i, C_OG // D_MODEL)), _const_spec((1, HV))],
        out_specs=tile, out_shape=jax.ShapeDtypeStruct((SEQ, D_MODEL), BF), compiler_params=_params("parallel"),
    )(*map(_in_hbm, (o, zcat, g_head)))


def _post_gla_bwd(dzcat, dy_gla, w_gla_proj, o, zcat, g_head):
    def body(dz_in, dyg_ref, w_ref, o_ref, zog_ref, g_ref, dz_ref, do_ref, dg_ref):
        del dz_in
        dog = _dot(dyg_ref[...], w_ref[...], tb=True)
        gpart = jnp.zeros((1, HV), F32)
        gv = g_ref[...]
        for h in range(HEADS):
            cols = slice(h * HV, (h + 1) * HV)
            ov = o_ref[:, cols]
            r = lax.rsqrt(jnp.mean(ov * ov, axis=-1, keepdims=True) + EPS)
            on = ov * r
            act, dact = _silu_parts(zog_ref[:, cols].astype(F32))
            dogv = dog[:, cols]
            dz_ref[:, cols] = (dogv * on * gv * dact).astype(BF)
            d_on_g = dogv * act
            gpart = gpart + jnp.sum(d_on_g * on, axis=0, keepdims=True)
            dxn = d_on_g * gv
            do_ref[:, cols] = (r * (dxn - on * jnp.mean(dxn * on, axis=-1, keepdims=True))).astype(BF)

        @pl.when(pl.program_id(0) == 0)
        def _():
            dg_ref[...] = gpart

        @pl.when(pl.program_id(0) > 0)
        def _():
            dg_ref[...] += gpart

    tile = pl.BlockSpec((TOK_TILE, D_MODEL), lambda i: (i, 0))
    ogspec = pl.BlockSpec((TOK_TILE, D_MODEL), lambda i: (i, C_OG // D_MODEL))
    return pl.pallas_call(
        body, name="post_gla_bwd", grid=(SEQ // TOK_TILE,),
        in_specs=[pl.BlockSpec(memory_space=pl.ANY), tile, _const_spec((D_MODEL, D_MODEL)), tile, ogspec,
                  _const_spec((1, HV))],
        out_specs=[ogspec, tile, _const_spec((1, HV))],
        out_shape=[jax.ShapeDtypeStruct((SEQ, N_DZ), BF), jax.ShapeDtypeStruct((SEQ, D_MODEL), BF),
                   jax.ShapeDtypeStruct((1, HV), F32)],
        input_output_aliases={0: 0}, compiler_params=_params("arbitrary"),
    )(*map(_in_hbm, (dzcat, dy_gla, w_gla_proj, o, zcat, g_head)))


GATE_W = 2 * D_MODEL


def _mix_out_fwd(ps, og, zcat, x, w_pool_proj, w_gla_proj, w_out, b_gate, g_ffn, after):
    def body(ps_ref, og_ref, zg_ref, x_ref, wpp_ref, wgp_ref, wout_ref, b_ref, g_ref, after_ref,
             yp_ref, yg_ref, mixed_ref, x1_ref, h2_ref):
        del after_ref
        y_pool = _dot(ps_ref[...], wpp_ref[...])
        y_gla = _dot(og_ref[...], wgp_ref[...])
        yp_ref[...] = y_pool.astype(BF)
        yg_ref[...] = y_gla.astype(BF)
        g0 = _sigmoid(zg_ref[:, :D_MODEL].astype(F32) + b_ref[:, :D_MODEL])
        g1 = _sigmoid(zg_ref[:, D_MODEL:].astype(F32) + b_ref[:, D_MODEL:])
        mixed = (g0 * y_pool + g1 * y_gla).astype(BF)
        mixed_ref[...] = mixed
        x1 = x_ref[...] + _dot(mixed, wout_ref[...])
        x1_ref[...] = x1
        r = lax.rsqrt(jnp.mean(x1 * x1, axis=-1, keepdims=True) + EPS)
        h2_ref[...] = (x1 * r * g_ref[...]).astype(BF)

    tile = pl.BlockSpec((TOK_TILE, D_MODEL), lambda i: (i, 0))
    resident = lambda shape: pl.BlockSpec(shape, lambda i: (0, 0), pipeline_mode=pl.Buffered(1))
    f32, bf16 = jax.ShapeDtypeStruct((SEQ, D_MODEL), F32), jax.ShapeDtypeStruct((SEQ, D_MODEL), BF)
    return pl.pallas_call(
        body, name="mix_out_fwd", grid=(SEQ // TOK_TILE,),
        in_specs=[pl.BlockSpec((TOK_TILE, POOL_WIDTH), lambda i: (i, 0)), tile,
                  pl.BlockSpec((TOK_TILE, GATE_W), lambda i: (i, C_GATE // GATE_W)), tile,
                  resident((POOL_WIDTH, D_MODEL)), resident((D_MODEL, D_MODEL)), resident((D_MODEL, D_MODEL)),
                  _const_spec((1, GATE_W)), _const_spec((1, D_MODEL)), pl.BlockSpec(memory_space=pl.ANY)],
        out_specs=[tile] * 5, out_shape=[bf16, bf16, bf16, f32, bf16], compiler_params=_params("parallel"),
    )(*map(_in_hbm, (ps, og, zcat, x, w_pool_proj, w_gla_proj, w_out, b_gate, g_ffn)), after)


def _mix_bwd(dx1, w_out, zcat, b_gate, y_pool, y_gla):
    def body(dx_ref, w_ref, zg_ref, b_ref, yp_ref, yg_ref, dz_ref, dyp_ref, dyg_ref, db_ref):
        dm = _dot(dx_ref[...], w_ref[...], tb=True)
        g0 = _sigmoid(zg_ref[:, :D_MODEL].astype(F32) + b_ref[:, :D_MODEL])
        g1 = _sigmoid(zg_ref[:, D_MODEL:].astype(F32) + b_ref[:, D_MODEL:])
        dyp_ref[...] = (dm * g0).astype(BF)
        dyg_ref[...] = (dm * g1).astype(BF)
        dz0 = dm * yp_ref[...].astype(F32) * g0 * (1.0 - g0)
        dz1 = dm * yg_ref[...].astype(F32) * g1 * (1.0 - g1)
        dz_ref[:, :D_MODEL] = dz0.astype(BF)
        dz_ref[:, D_MODEL:] = dz1.astype(BF)
        b0 = jnp.sum(dz0, axis=0, keepdims=True)
        b1 = jnp.sum(dz1, axis=0, keepdims=True)

        @pl.when(pl.program_id(0) == 0)
        def _():
            db_ref[:, :D_MODEL] = b0
            db_ref[:, D_MODEL:] = b1

        @pl.when(pl.program_id(0) > 0)
        def _():
            db_ref[:, :D_MODEL] += b0
            db_ref[:, D_MODEL:] += b1

    tile = pl.BlockSpec((TOK_TILE, D_MODEL), lambda i: (i, 0))
    gspec = pl.BlockSpec((TOK_TILE, GATE_W), lambda i: (i, C_GATE // GATE_W))
    return pl.pallas_call(
        body, name="mix_bwd", grid=(SEQ // TOK_TILE,),
        in_specs=[tile, _const_spec((D_MODEL, D_MODEL)), gspec, _const_spec((1, GATE_W)), tile, tile],
        out_specs=[gspec, tile, tile, _const_spec((1, GATE_W))],
        out_shape=[jax.ShapeDtypeStruct((SEQ, N_DZ), BF), jax.ShapeDtypeStruct((SEQ, D_MODEL), BF),
                   jax.ShapeDtypeStruct((SEQ, D_MODEL), BF), jax.ShapeDtypeStruct((1, GATE_W), F32)],
        compiler_params=_params("arbitrary"),
    )(*map(_in_hbm, (dx1, w_out, zcat, b_gate, y_pool, y_gla)))


N_TOK_TILES = SEQ // TOK_TILE
HALO_PER_TILE = TOK_TILE // HALO


LANE_TILES = tuple((lo, min(128, FF_BLK - lo)) for lo in range(0, FF_BLK, 128))


def _taps(w_ref, b_ref, half, lanes, rows):
    shape = (rows, lanes.stop - lanes.start)
    return ([jnp.broadcast_to(w_ref[half, j:j + 1, lanes], shape) for j in range(3)],
            jnp.broadcast_to(b_ref[half, :, lanes], shape))


def _conv_strips(u_ref, ub_ref, ua_ref, taps, lanes, width, n_strips, first):
    row = lax.broadcasted_iota(jnp.int32, (HALO, width), 0)
    prev = [[pltpu.roll(jnp.where(first, 0.0, ub_ref[half, :, lanes]), k, 0) for k in (1, 2)] for half in range(2)]
    for s in range(n_strips + (ua_ref is not None)):
        u3, conv = [], []
        for half in range(2):
            cur = u_ref[half, s * HALO:(s + 1) * HALO, lanes] if s < n_strips else ua_ref[half, :, lanes]
            rolled = [pltpu.roll(cur, k, 0) for k in (1, 2)]
            frames = [jnp.where(row >= 2, rolled[1], prev[half][1]), jnp.where(row >= 1, rolled[0], prev[half][0]), cur]
            prev[half] = rolled
            w3, bias = taps[half]
            u3.append(frames)
            conv.append(bias + frames[0] * w3[0] + frames[1] * w3[1] + frames[2] * w3[2])
        yield s, u3, conv


def _pair_specs(pairs):
    tile = pl.BlockSpec((pairs, None, TOK_TILE, FF_BLK), lambda b, i: (0, b, i, 0))
    before = pl.BlockSpec((pairs, None, HALO, FF_BLK), lambda b, i: (0, b, jnp.maximum(i * HALO_PER_TILE - 1, 0), 0))
    after = pl.BlockSpec((pairs, None, HALO, FF_BLK),
                         lambda b, i: (0, b, jnp.minimum((i + 1) * HALO_PER_TILE, SEQ // HALO - 1), 0))

    def vec(rows):
        return pl.BlockSpec((2, None, rows, FF_BLK), lambda b, i: (0, b, 0, 0))

    return tile, before, after, vec


N_STRIPS = TOK_TILE // HALO


def _up_conv_fwd(h2, wt_up, w_conv, b_conv):
    steps = N_TOK_TILES // 2

    def body(h_ref, h_next, wg_ref, wv_ref, w_ref, b_ref, u_ref, a_ref, buf_a, buf_b, carry):
        j = pl.program_id(1)

        def project(hv, buf):
            buf[0] = _dot(hv, wg_ref[...], tb=True)
            buf[1] = _dot(hv, wv_ref[...], tb=True)

        def conv(buf, row0):
            u_ref[:, row0:row0 + TOK_TILE, :] = buf[...]
            for lo, width in LANE_TILES:
                lanes = slice(lo, lo + width)
                taps = [_taps(w_ref, b_ref, half, lanes, HALO) for half in range(2)]
                pending = None
                for s, _, (cg, cv) in _conv_strips(buf, carry, None, taps, lanes, width, N_STRIPS, False):
                    act = cg * _sigmoid(cg) * cv
                    if s % 2 == 0:
                        pending = act
                    else:
                        a_ref[0, row0 + (s - 1) * HALO:row0 + (s + 1) * HALO, lanes] = (
                            jnp.concatenate([pending, act], axis=0).astype(BF))
            carry[...] = buf[:, TOK_TILE - HALO:, :]

        @pl.when(j == 0)
        def _():
            project(h_ref[0:TOK_TILE, :], buf_a)
            carry[...] = jnp.zeros_like(carry)

        project(h_ref[TOK_TILE:, :], buf_b)
        conv(buf_a, 0)
        project(h_next[...], buf_a)
        conv(buf_b, TOK_TILE)

    w_blk = lambda half: pl.BlockSpec((FF_BLK, D_MODEL), lambda b, j: (b + 4 * half, 0))
    vec = lambda rows: pl.BlockSpec((2, None, rows, FF_BLK), lambda b, j: (0, b, 0, 0))
    u_buf = pltpu.VMEM((2, TOK_TILE, FF_BLK), F32)
    return pl.pallas_call(
        body, name="up_conv_fwd", grid=(4, steps),
        in_specs=[pl.BlockSpec((2 * TOK_TILE, D_MODEL), lambda b, j: (j, 0)),
                  pl.BlockSpec((TOK_TILE, D_MODEL), lambda b, j: (jnp.minimum(2 * j + 2, N_TOK_TILES - 1), 0)),
                  w_blk(0), w_blk(1), vec(3), vec(1)],
        out_specs=[pl.BlockSpec((2, None, 2 * TOK_TILE, FF_BLK), lambda b, j: (0, b, j, 0)),
                   pl.BlockSpec((1, None, 2 * TOK_TILE, FF_BLK), lambda b, j: (0, b, j, 0))],
        out_shape=[jax.ShapeDtypeStruct((2, 4, SEQ, FF_BLK), F32), jax.ShapeDtypeStruct((1, 4, SEQ, FF_BLK), BF)],
        scratch_shapes=[u_buf, u_buf, pltpu.VMEM((2, HALO, FF_BLK), F32)],
        compiler_params=_params("parallel", "arbitrary"),
    )(*map(_in_hbm, (h2, h2, wt_up, wt_up, w_conv, b_conv)))


def _conv_bwd(u, da, w_conv, b_conv):
    def body(u_ref, ub_ref, ua_ref, da_ref, daa_ref, w_ref, b_ref, du_ref, dw_ref, db_ref):
        i = pl.program_id(1)

        @pl.when(i == 0)
        def _():
            dw_ref[...] = jnp.zeros_like(dw_ref)
            db_ref[...] = jnp.zeros_like(db_ref)

        for lo, width in LANE_TILES:
            lanes = slice(lo, lo + width)
            row = lax.broadcasted_iota(jnp.int32, (HALO, width), 0)
            taps = [_taps(w_ref, b_ref, half, lanes, HALO) for half in range(2)]
            acc_w = [[jnp.zeros((HALO, width), F32) for _ in range(3)] for _ in range(2)]
            acc_b = [jnp.zeros((HALO, width), F32) for _ in range(2)]
            da_pair, pending = None, [None, None]
            dc_prev, up_prev = [None, None], [None, None]
            for s, u3, (cg, cv) in _conv_strips(u_ref, ub_ref, ua_ref, taps, lanes, width, N_STRIPS, i == 0):
                act, dact = _silu_parts(cg)
                if s == N_STRIPS:
                    da = jnp.where(i < N_TOK_TILES - 1, daa_ref[0, :, lanes].astype(F32), 0.0)
                elif s % 2 == 0:
                    da_pair = da_ref[0, s * HALO:(s + 2) * HALO, lanes].astype(F32)
                    da = da_pair[:HALO]
                else:
                    da = da_pair[HALO:]
                dc = (da * cv * dact, da * act)
                for half in range(2):
                    up = [pltpu.roll(dc[half], HALO - k, 0) for k in (1, 2)]
                    if s < N_STRIPS:
                        for j in range(3):
                            acc_w[half][j] = acc_w[half][j] + dc[half] * u3[half][j]
                        acc_b[half] = acc_b[half] + dc[half]
                    if s >= 1:
                        w3 = taps[half][0]
                        du = (dc_prev[half] * w3[2] + jnp.where(row < HALO - 1, up_prev[half][0], up[0]) * w3[1]
                              + jnp.where(row < HALO - 2, up_prev[half][1], up[1]) * w3[0])
                        if (s - 1) % 2 == 0:
                            pending[half] = du
                        else:
                            du_ref[half, (s - 2) * HALO:s * HALO, lanes] = jnp.concatenate([pending[half], du],
                                                                                           axis=0).astype(BF)
                    dc_prev[half], up_prev[half] = dc[half], up
            for half in range(2):
                for j in range(3):
                    dw_ref[half, j:j + 1, lanes] += jnp.sum(acc_w[half][j], axis=0, keepdims=True)
                db_ref[half, :, lanes] += jnp.sum(acc_b[half], axis=0, keepdims=True)

    tile, before, after, vec = _pair_specs(2)
    da_tile, _, da_after_spec, _ = _pair_specs(1)
    return pl.pallas_call(
        body, name="conv_bwd", grid=(4, N_TOK_TILES),
        in_specs=[tile, before, after, da_tile, da_after_spec, vec(3), vec(1)],
        out_specs=[tile, vec(3), vec(1)],
        out_shape=[jax.ShapeDtypeStruct((2, 4, SEQ, FF_BLK), BF), jax.ShapeDtypeStruct((2, 4, 3, FF_BLK), F32),
                   jax.ShapeDtypeStruct((2, 4, 1, FF_BLK), F32)],
        compiler_params=_params("parallel", "arbitrary"),
    )(*map(_in_hbm, (u, u, u, da, da, w_conv, b_conv)))


W_IN_SEGMENTS = ((R_POOL, POOL_WIDTH, C_POOL), (R_QKV, QKV_W, C_QKV), (R_OG, D_MODEL, C_OG), (R_GK, GATE_RANK, C_GK),
                 (R_GATE, GATE_W, C_GATE))


def _slab_pieces(d):
    lo, hi = d * IN_SHARD, (d + 1) * IN_SHARD
    pieces = []
    for start, n, at in W_IN_SEGMENTS:
        a, b = max(lo, start), min(hi, start + n)
        if a < b:
            assert (a - lo) % 2 == 0 and (b - a) % 2 == 0 and (at + a - start) % 2 == 0
            pieces.append(((a - lo) // 2, (b - a) // 2, (at + a - start) // 2))
    return pieces


def _unshard_w_in(slabs):
    def body(slab_ref, cat_ref):
        d = pl.program_id(0)
        src = slab_ref.bitcast(jnp.uint32)
        dst = cat_ref.bitcast(jnp.uint32)

        @pl.when(d == 0)
        def _():
            cat_ref[C_GK:, :] = jnp.zeros((GK_PAD, D_MODEL), BF)

        for dd in range(N_DEV):
            @pl.when(d == dd)
            def _():
                for a, n, at in _slab_pieces(dd):
                    dst[pl.ds(at, n), :] = src[0, pl.ds(a, n), :]

    return pl.pallas_call(
        body, name="unshard_w_in", grid=(N_DEV,),
        in_specs=[pl.BlockSpec((1, IN_SHARD, D_MODEL), lambda d: (d, 0, 0))], out_specs=_const_spec((N_DZ, D_MODEL)),
        out_shape=jax.ShapeDtypeStruct((N_DZ, D_MODEL), BF), compiler_params=_params("arbitrary"),
    )(_in_hbm(slabs))


def _shard_d_w_in(d_cat):
    def body(cat_ref, slab_ref):
        d = pl.program_id(0)
        cat = cat_ref.bitcast(jnp.uint32)
        dst = slab_ref.bitcast(jnp.uint32)
        for dd in range(N_DEV):
            @pl.when(d == dd)
            def _():
                for a, n, at in _slab_pieces(dd):
                    dst[0, pl.ds(a, n), :] = cat[pl.ds(at, n), :]

    return pl.pallas_call(
        body, name="shard_d_w_in", grid=(N_DEV,), in_specs=[_const_spec((N_DZ, D_MODEL))],
        out_specs=pl.BlockSpec((1, IN_SHARD, D_MODEL), lambda d: (d, 0, 0)),
        out_shape=jax.ShapeDtypeStruct((N_DEV, IN_SHARD, D_MODEL), BF), compiler_params=_params("parallel"),
    )(_in_hbm(d_cat))


ANY = pl.BlockSpec(memory_space=pl.ANY)


def _place():
    x, y, c = lax.axis_index("x"), lax.axis_index("y"), lax.axis_index("c")
    other_chips = [(1 - x, y), (x, 1 - y), (1 - x, 1 - y)]
    return x, y, c, other_chips


SEM = pl.BlockSpec(memory_space=pltpu.SEMAPHORE)
IN_HBM = pl.BlockSpec(memory_space=pltpu.HBM)
SPLIT_PARAMS = pltpu.CompilerParams(has_side_effects=pltpu.SideEffectType.DATAFLOW_SIDE_EFFECTING)


def _gather_first(refs, send_sems, recv_sems):
    x, y, c, chips = _place()
    targets = [(x, y, 1 - c)] + [(px, py, c) for px, py in chips]
    return [pltpu.make_async_remote_copy(src_ref=refs[2 * a], dst_ref=refs[2 * a + 1].at[4 * x + 2 * y + c],
                                         send_sem=send_sems.at[4 * a + k], recv_sem=recv_sems.at[4 * a + k],
                                         device_id=to, device_id_type=MESH)
            for a in range(len(refs) // 2) for k, to in enumerate(targets)]


def _gather_direct(refs, send_sems, recv_sems):
    x, y, c, _ = _place()
    flips = [(dx, dy, dc) for dx in (0, 1) for dy in (0, 1) for dc in (0, 1) if dx + dy + dc]
    targets = [(1 - x if dx else x, 1 - y if dy else y, 1 - c if dc else c) for dx, dy, dc in flips]
    return [pltpu.make_async_remote_copy(src_ref=refs[2 * a], dst_ref=refs[2 * a + 1].at[4 * x + 2 * y + c],
                                         send_sem=send_sems.at[7 * a + k], recv_sem=recv_sems.at[7 * a + k],
                                         device_id=to, device_id_type=MESH)
            for a in range(len(refs) // 2) for k, to in enumerate(targets)]


def _gather_second(refs, send_sems, recv_sems):
    x, y, c, chips = _place()
    copies = []
    for a, land in enumerate(refs):
        for j, (px, py) in enumerate(chips):
            block = land.at[4 * px + 2 * py + c]
            copies.append(pltpu.make_async_remote_copy(src_ref=block, dst_ref=block, send_sem=send_sems.at[3 * a + j],
                                                       recv_sem=recv_sems.at[3 * a + j], device_id=(x, y, 1 - c),
                                                       device_id_type=MESH))
    return copies


def _reduce_first(refs, send_sems, recv_sems):
    x, y, c, _ = _place()
    return [pltpu.make_async_remote_copy(src_ref=refs[2 * a].at[j, 1 - c], dst_ref=refs[2 * a + 1].at[j],
                                         send_sem=send_sems.at[4 * a + j], recv_sem=recv_sems.at[4 * a + j],
                                         device_id=(x, y, 1 - c), device_id_type=MESH)
            for a in range(len(refs) // 2) for j in range(4)]


def _reduce_second(refs, send_sems, recv_sems):
    _, _, c, chips = _place()
    return [pltpu.make_async_remote_copy(src_ref=refs[2 * a].at[2 * px + py], dst_ref=refs[2 * a + 1].at[k],
                                         send_sem=send_sems.at[3 * a + k], recv_sem=recv_sems.at[3 * a + k],
                                         device_id=(px, py, c), device_id_type=MESH)
            for a in range(len(refs) // 2) for k, (px, py) in enumerate(chips)]


def _split_start(name, groups):
    arrays = [a for g in groups for a in g[0]]
    n = len(arrays)

    def body(*refs):
        sems = refs[n:n + 2 * len(groups)]
        at = 0
        for gi, (members, _, build) in enumerate(groups):
            for cp in build(refs[at:at + len(members)], sems[2 * gi], sems[2 * gi + 1]):
                cp.start()
            at += len(members)
        refs[-1][...] = jnp.zeros_like(refs[-1])

    sem_shapes = [pltpu.SemaphoreType.DMA((g[1],)) for g in groups for _ in range(2)]
    outs = pl.pallas_call(
        body, name=name, in_specs=[IN_HBM] * n,
        out_shape=(*sem_shapes, *[pltpu.HBM(a.shape, a.dtype) for a in arrays], jax.ShapeDtypeStruct((8, 128), F32)),
        out_specs=(*[SEM] * len(sem_shapes), *[IN_HBM] * n, pl.BlockSpec(memory_space=pltpu.VMEM)),
        input_output_aliases={i: len(sem_shapes) + i for i in range(n)}, compiler_params=SPLIT_PARAMS,
    )(*[pltpu.with_memory_space_constraint(a, pltpu.HBM) for a in arrays])
    per_group, at = [], len(sem_shapes)
    for gi, (members, _, _) in enumerate(groups):
        per_group.append((outs[2 * gi], outs[2 * gi + 1], list(outs[at:at + len(members)])))
        at += len(members)
    return per_group, outs[-1]


def _split_wait(name, started, build, after):
    send_sems, recv_sems, arrays = started
    n = len(arrays)
    after = after if isinstance(after, (tuple, list)) else (after,)

    def body(*refs):
        for cp in build(refs[:n], refs[n], refs[n + 1]):
            cp.wait_send()
            cp.wait_recv()

    return pl.pallas_call(
        body, name=name, in_specs=[IN_HBM] * n + [SEM, SEM] + [ANY] * len(after),
        out_shape=tuple(pltpu.HBM(a.shape, a.dtype) for a in arrays), out_specs=tuple([IN_HBM] * n),
        input_output_aliases={i: i for i in range(n)}, compiler_params=SPLIT_PARAMS,
    )(*arrays, send_sems, recv_sems, *after)


def _gather_landing(shard, me):
    return lax.dynamic_update_slice(lax.empty((N_DEV,) + shard.shape, shard.dtype), shard[None],
                                    (me,) + (0,) * shard.ndim)


ADAM_LANE_TILE = 256


def _tile_2d(rows, cols):
    for t in (256, 176, 128):
        if rows % t == 0:
            return t, cols
    return rows, ADAM_LANE_TILE


def _pair_sum(part, recv, core, name):
    _, rows, cols = recv.shape
    tr, tc = rows, cols

    def body(c_ref, p_ref, r_ref, o_ref):
        del c_ref
        o_ref[...] = (p_ref[...].astype(F32) + r_ref[...].astype(F32)).astype(BF)

    grid_spec = pltpu.PrefetchScalarGridSpec(
        num_scalar_prefetch=1, grid=(4, rows // tr, cols // tc),
        in_specs=[pl.BlockSpec((None, None, tr, tc), lambda j, i, k, c_ref: (j, c_ref[0], i, k)),
                  pl.BlockSpec((None, tr, tc), lambda j, i, k, c_ref: (j, i, k))],
        out_specs=pl.BlockSpec((None, tr, tc), lambda j, i, k, c_ref: (j, i, k)))
    return pl.pallas_call(
        body, name=name, grid_spec=grid_spec, out_shape=jax.ShapeDtypeStruct(recv.shape, BF),
        compiler_params=_params("parallel", "parallel", "parallel"),
    )(core, *map(_in_hbm, (part, recv)))


def _adamw(w, g, m, v):
    m = ADAM_B1 * m + (1.0 - ADAM_B1) * g
    v = ADAM_B2 * v + (1.0 - ADAM_B2) * (g * g)
    delta = -ADAM_LR * ((m / ADAM_C1) / (jnp.sqrt(v / ADAM_C2) + ADAM_EPS) + ADAM_WD * w)
    return delta, m, v


def _chip_sum_adamw(sums, recv, w, m, v, chip, name):
    rows, cols = w.shape
    tr, tc = _tile_2d(rows, cols)

    def body(chip_ref, s_ref, r_ref, w_ref, m_ref, v_ref, g_out, d_out, m_out, v_out):
        del chip_ref
        g = s_ref[...].astype(F32)
        for k in range(3):
            g = g + r_ref[k].astype(F32)
        g_out[...] = g
        d_out[...], m_out[...], v_out[...] = _adamw(w_ref[...], g, m_ref[...], v_ref[...])

    tile = pl.BlockSpec((tr, tc), lambda i, k, chip_ref: (i, k))
    grid_spec = pltpu.PrefetchScalarGridSpec(
        num_scalar_prefetch=1, grid=(rows // tr, cols // tc),
        in_specs=[pl.BlockSpec((None, tr, tc), lambda i, k, chip_ref: (chip_ref[0], i, k)),
                  pl.BlockSpec((3, tr, tc), lambda i, k, chip_ref: (0, i, k)), tile, tile, tile],
        out_specs=[tile] * 4)
    return pl.pallas_call(
        body, name=name, grid_spec=grid_spec, out_shape=[jax.ShapeDtypeStruct((rows, cols), F32)] * 4,
        compiler_params=_params("parallel", "parallel"),
    )(chip, *map(_in_hbm, (sums, recv, w, m, v)))


def _small_sum_adamw(me, entries, loss_parts):
    def whole(shape, squeeze=0, pick=False):
        blk = (None,) * squeeze + tuple(shape[squeeze:])
        if pick:
            blk = (shape[0], None) + tuple(shape[2:])
            return pl.BlockSpec(blk, lambda i, me_ref: (0, me_ref[0]) + (0,) * (len(shape) - 2))
        return pl.BlockSpec(blk, lambda i, me_ref: (0,) * len(shape))

    in_specs, out_specs, out_shape, args = [], [], [], []
    for parts, w, m, v, sharded in entries:
        lead = w.ndim - (parts.ndim - (2 if sharded else 1))
        in_specs += [whole(parts.shape, pick=sharded)] + [whole(w.shape, squeeze=lead)] * 3
        out_specs += [whole(w.shape, squeeze=lead)] * 4
        out_shape += [jax.ShapeDtypeStruct(w.shape, F32)] * 4
        args += [parts, w, m, v]
    in_specs.append(whole(loss_parts.shape))
    out_specs.append(whole(loss_parts.shape[1:]))
    out_shape.append(jax.ShapeDtypeStruct(loss_parts.shape[1:], F32))
    n = len(entries)

    def added(p_ref):
        total = p_ref[0]
        for d in range(1, N_DEV):
            total = total + p_ref[d]
        return total

    def body(me_ref, *refs):
        del me_ref
        ins, outs = refs[:4 * n + 1], refs[4 * n + 1:]
        for e in range(n):
            p_ref, w_ref, m_ref, v_ref = ins[4 * e:4 * e + 4]
            g_out, d_out, m_out, v_out = outs[4 * e:4 * e + 4]
            g = added(p_ref)
            g_out[...] = g
            d_out[...], m_out[...], v_out[...] = _adamw(w_ref[...], g, m_ref[...], v_ref[...])
        outs[4 * n][...] = added(ins[4 * n])

    grid_spec = pltpu.PrefetchScalarGridSpec(num_scalar_prefetch=1, grid=(1,), in_specs=in_specs, out_specs=out_specs)
    outs = pl.pallas_call(body, name="small_sum_adamw", grid_spec=grid_spec, out_shape=out_shape,
                          compiler_params=_params("arbitrary"))(me, *map(_in_hbm, args + [loss_parts]))
    return [outs[4 * e:4 * e + 4] for e in range(n)], outs[4 * n]


MM_TILE = 512
N_MM_TILES = SEQ // MM_TILE
CAT_TILE = 512
N_CAT_TILES = N_CAT // CAT_TILE
DZ_TILE = 640


def kernel(x, g_mix, w_in, b_gate, w_gk_up, b_gk, w_pool_grp, pool_scale, g_gla_head, w_pool_proj, w_gla_proj, w_out, g_ffn, w_up, w_conv, b_conv, w_down, g_final, loss_target, m_g_mix, m_w_in, m_b_gate, m_w_gk_up, m_b_gk, m_w_pool_grp, m_pool_scale, m_g_gla_head, m_w_pool_proj, m_w_gla_proj, m_w_out, m_g_ffn, m_w_up, m_w_conv, m_b_conv, m_w_down, m_g_final, v_g_mix, v_w_in, v_b_gate, v_w_gk_up, v_b_gk, v_w_pool_grp, v_pool_scale, v_g_gla_head, v_w_pool_proj, v_w_gla_proj, v_w_out, v_g_ffn, v_w_up, v_w_conv, v_b_conv, v_w_down, v_g_final):
    xi, yi, ci = lax.axis_index("x"), lax.axis_index("y"), lax.axis_index("c")
    me = 4 * xi + 2 * yi + ci
    core = jnp.reshape(ci, (1,)).astype(jnp.int32)
    chip = jnp.reshape(2 * xi + yi, (1,)).astype(jnp.int32)
    xs, target = x[0], loss_target[0]

    big = dict(w_in=w_in[0].T, w_pool_proj=w_pool_proj[0], w_gla_proj=w_gla_proj[0], w_out=w_out[0], w_up=w_up[0].T,
               w_down=w_down[0])
    moments = dict(w_in=(m_w_in[0].T, v_w_in[0].T), w_pool_proj=(m_w_pool_proj[0], v_w_pool_proj[0]),
                   w_gla_proj=(m_w_gla_proj[0], v_w_gla_proj[0]), w_out=(m_w_out[0], v_w_out[0]),
                   w_up=(m_w_up[0].T, v_w_up[0].T), w_down=(m_w_down[0], v_w_down[0]))
    names = list(big)
    shards = {k: big[k].astype(BF) for k in names}
    shards["w_gk_up"], shards["w_conv"] = w_gk_up[0], w_conv[0]
    gather_groups = (("w_in", "w_gk_up"), ("w_pool_proj", "w_gla_proj", "w_out"), ("w_up", "w_down", "w_conv"))
    started, token = _split_start("gather_start", [
        ([t for k in g for t in (shards[k], _gather_landing(shards[k], me))], 4 * len(g), _gather_first)
        for g in gather_groups])

    def gather_pass(gi, after):
        lands = list(_split_wait(f"gather_wait_{gi}", started[gi], _gather_first, after)[1::2])
        passed, tkn = _split_start(f"gather_pass_{gi}", [(lands, 3 * len(lands), _gather_second)])
        return passed[0], tkn

    def gather_done(gi, passed, after):
        return dict(zip(gather_groups[gi], _split_wait(f"gather_pass_wait_{gi}", passed, _gather_second, after)))

    tok = lambda i, j, k: (i, 0)
    whole = lambda i, j, k: (0, 0)
    kblk = lambda i, j, k: (k, 0)
    ff_seq = (None, None, SEQ, FF_BLK)

    h = _rms_fwd(xs, g_mix + token[:1, :1], "rms_mix")
    wg = gather_done(0, gather_pass(0, h)[0], h)
    wt_cat = _unshard_w_in(wg["w_in"])
    wgk_pad = jnp.pad(wg["w_gk_up"].transpose(1, 0, 2).reshape(GATE_RANK, GLA_DK), ((0, GK_PAD - GATE_RANK), (0, 0)))
    zcat = _mm(h, wt_cat, out_shape=(SEQ, N_CAT), out_dtype=BF, grid=(N_CAT_TILES, 1, 1),
               blk_a=(SEQ, D_MODEL), blk_b=(CAT_TILE, D_MODEL), blk_o=(SEQ, CAT_TILE),
               map_a=whole, map_b=lambda j, i, k: (j, 0), map_o=lambda j, i, k: (0, j), tb=True, name="mm_in")
    la = _gk_fwd(h, wt_cat, wgk_pad, b_gk)
    passed, tkn = gather_pass(1, la)
    o, states = _gla_fwd(zcat, la, tkn)
    wg = gather_done(1, passed, o)
    wpp = wg["w_pool_proj"].transpose(1, 0, 2).reshape(POOL_WIDTH, D_MODEL)
    wgp = wg["w_gla_proj"].reshape(D_MODEL, D_MODEL)
    wout = wg["w_out"].reshape(D_MODEL, D_MODEL)
    og = _post_gla_fwd(o, zcat, g_gla_head)
    ps = _pool_fwd(zcat, w_pool_grp[0], pool_scale)
    passed, tkn = gather_pass(2, (og, ps))
    y_pool, y_gla, mixed, x1, h2 = _mix_out_fwd(ps, og, zcat, xs, wpp, wgp, wout, b_gate, g_ffn, tkn)
    wg = gather_done(2, passed, h2)
    wt_up = wg["w_up"].reshape(2 * D_FF, D_MODEL)
    wdown = wg["w_down"].reshape(D_FF, D_MODEL)
    wconv4 = wg["w_conv"].reshape(2, 4, 3, FF_BLK)
    bconv4 = b_conv.reshape(2, 4, 1, FF_BLK)
    blk4 = lambda b, i, k: (b // 4, b % 4, 0, 0)
    u4, act = _up_conv_fwd(h2, wt_up, wconv4, bconv4)
    loss_part, dx2, dx2_bf, dg_final = _mm_tokens(
        act, wdown, blk_a=(None, 4, TOK_MM_TILE, FF_BLK), map_a=lambda i: (0, 0, i, 0),
        pieces=[(b, b * FF_BLK, FF_BLK) for b in range(4)], res=x1, then=("loss", g_final.reshape(1, D_MODEL), target),
        name="mm_down_loss")

    da = _mm(dx2_bf, wdown, out_shape=(1, 4, SEQ, FF_BLK), out_dtype=BF, grid=(4, 1, 1),
             blk_a=(SEQ, D_MODEL), blk_b=(FF_BLK, D_MODEL), blk_o=ff_seq,
             map_a=whole, map_b=lambda b, i, k: (b, 0), map_o=lambda b, i, k: (0, b, 0, 0), tb=True, name="mm_d_act")
    d_wdown = _mm(act, dx2_bf, out_shape=(D_FF, D_MODEL), out_dtype=BF, grid=(4, 1, 1),
                  blk_a=ff_seq, blk_b=(SEQ, D_MODEL), blk_o=(FF_BLK, D_MODEL),
                  map_a=lambda b, i, k: (0, b, 0, 0), map_b=whole, map_o=lambda b, i, k: (b, 0), ta=True,
                  name="mm_d_wdown")
    du4, d_wconv, d_bconv = _conv_bwd(u4, da, wconv4, bconv4)
    d_wt_up = _mm(du4, h2, out_shape=(2 * D_FF, D_MODEL), out_dtype=BF, grid=(N_DEV, 1, 1),
                  blk_a=ff_seq, blk_b=(SEQ, D_MODEL), blk_o=(FF_BLK, D_MODEL),
                  map_a=blk4, map_b=whole, map_o=lambda b, i, k: (b, 0), ta=True, name="mm_d_wup")
    res = {}

    def to_sibling(keys, parts):
        return [t for k in keys for t in (parts[k], lax.empty((4,) + parts[k].shape[2:], BF))], 4 * len(keys), _reduce_first

    def to_chips(keys, st, after):
        arrays = _split_wait("reduce_wait_" + keys[0], st, _reduce_first, after)
        sums = [_pair_sum(p, r, core, "pair_sum_" + k) for k, p, r in zip(keys, arrays[0::2], arrays[1::2])]
        return [t for s in sums for t in (s, lax.empty((3,) + s.shape[1:], BF))], 3 * len(keys), _reduce_second

    def reduce_start(keys, parts):
        st, tkn = _split_start("reduce_start_" + keys[0], [to_sibling(keys, parts)])
        return st[0], tkn

    def reduce_cross(keys, st, after):
        st2, tkn = _split_start("reduce_cross_" + keys[0], [to_chips(keys, st, after)])
        return st2[0], tkn

    def reduce_done(keys, st2, after):
        arrays = _split_wait("reduce_cross_wait_" + keys[0], st2, _reduce_second, after)
        for k, s, r in zip(keys, arrays[0::2], arrays[1::2]):
            outs = _chip_sum_adamw(s, r, big[k], moments[k][0], moments[k][1], chip, "adamw_" + k)
            res[k] = [(t.T if k in ("w_in", "w_up") else t)[None] for t in outs]

    ffn_keys = ("w_down", "w_up")
    ffn_red, tkn = reduce_start(ffn_keys, dict(w_down=d_wdown.reshape(4, 2, D_FF // N_DEV, D_MODEL),
                                               w_up=d_wt_up.reshape(4, 2, FF_BLK, D_MODEL)))
    dx1, dg_ffn = _mm_tokens(
        du4, wt_up, blk_a=(2, 4, TOK_MM_TILE, FF_BLK), map_a=lambda i: (0, 0, i, 0),
        pieces=[((b // 4, b % 4), b * FF_BLK, FF_BLK) for b in range(N_DEV)], after=tkn, then=("rms_bwd", x1, g_ffn, dx2),
        name="mm_d_h2_rms")

    sq_t = dict(out_shape=(D_MODEL, D_MODEL), grid=(1, 1, N_MM_TILES), blk_a=(MM_TILE, D_MODEL),
                blk_b=(MM_TILE, D_MODEL), blk_o=(D_MODEL, D_MODEL), map_a=kblk, map_b=kblk, map_o=whole, ta=True)
    d_wout = _mm(mixed, dx1, out_dtype=BF, name="mm_d_wout", **sq_t)
    dzcat, dy_pool, dy_gla, db_gate = _mix_bwd(dx1, wout, zcat, b_gate, y_pool, y_gla)
    d_wgp = _mm(og, dy_gla, out_dtype=BF, name="mm_d_wgp", **sq_t)
    mix_keys = ("w_out", "w_gla_proj")
    (ffn_red, mix_red), tkn = _split_start("reduce_cross_w_down", [
        to_chips(ffn_keys, ffn_red, db_gate),
        to_sibling(mix_keys, dict(w_out=d_wout.reshape(4, 2, D_MODEL // N_DEV, D_MODEL),
                                  w_gla_proj=d_wgp.reshape(4, 2, D_MODEL // N_DEV, D_MODEL)))])
    dzcat, d_o, dg_head = _post_gla_bwd(dzcat, dy_gla, wgp, o, zcat, g_gla_head + tkn[:1, :1])
    dzcat, dla = _gla_bwd(dzcat, zcat, la, d_o, states)
    dzcat, d_wgk, db_gk = _gk_bwd(dzcat, dla, h, wt_cat, wgk_pad, b_gk)
    dps = _mm(dy_pool, wpp, out_shape=(SEQ, POOL_WIDTH), out_dtype=F32, grid=(N_MM_TILES, 1, 1),
              blk_a=(MM_TILE, D_MODEL), blk_b=(POOL_WIDTH, D_MODEL), blk_o=(MM_TILE, POOL_WIDTH),
              map_a=tok, map_b=whole, map_o=tok, tb=True, name="mm_d_ps")
    d_wpp = _mm(ps, dy_pool, out_shape=(POOL_WIDTH, D_MODEL), out_dtype=F32, grid=(1, 1, N_MM_TILES),
                blk_a=(MM_TILE, POOL_WIDTH), blk_b=(MM_TILE, D_MODEL), blk_o=(POOL_WIDTH, D_MODEL),
                map_a=kblk, map_b=kblk, map_o=whole, ta=True, name="mm_d_wpp")
    dzcat, d_wgrp, d_scale = _pool_bwd(dzcat, zcat, dps, w_pool_grp[0], pool_scale)
    row = lambda t: t.reshape(1, D_MODEL)
    conv_vec = lambda t: t.reshape(2, 4, 1, FF_BLK)
    small = [("b_gate", db_gate, b_gate, m_b_gate, v_b_gate, False),
             ("w_gk_up", d_wgk.reshape(GATE_RANK, N_DEV, GLA_DK // N_DEV).transpose(1, 0, 2), w_gk_up, m_w_gk_up,
              v_w_gk_up, True),
             ("b_gk", db_gk, b_gk, m_b_gk, v_b_gk, False),
             ("w_pool_grp", d_wgrp, w_pool_grp, m_w_pool_grp, v_w_pool_grp, False),
             ("pool_scale", d_scale, pool_scale, m_pool_scale, v_pool_scale, False),
             ("g_gla_head", dg_head, g_gla_head, m_g_gla_head, v_g_gla_head, False),
             ("g_ffn", dg_ffn, g_ffn, m_g_ffn, v_g_ffn, False),
             ("w_conv", d_wconv.reshape(N_DEV, 3, FF_BLK), w_conv, m_w_conv, v_w_conv, True),
             ("b_conv", d_bconv, conv_vec(b_conv), conv_vec(m_b_conv), conv_vec(v_b_conv), False),
             ("g_final", dg_final, row(g_final), row(m_g_final), row(v_g_final), False)]

    def to_all(parts):
        return [t for p in parts for t in (p, _gather_landing(p, me))], 7 * len(parts), _gather_direct

    (small_sent, mix_red), tkn = _split_start("small_start", [to_all([t[1] for t in small] + [loss_part]),
                                                              to_chips(mix_keys, mix_red, dla)])
    d_wt_cat = _mm(dzcat, h, out_shape=(N_DZ, D_MODEL), out_dtype=BF, grid=(N_DZ // DZ_TILE, 1, 1),
                   blk_a=(SEQ, DZ_TILE), blk_b=(SEQ, D_MODEL), blk_o=(DZ_TILE, D_MODEL),
                   map_a=lambda j, i, k: (0, j), map_b=whole, map_o=lambda j, i, k: (j, 0), ta=True, after=tkn,
                   name="mm_d_wcat")
    in_keys = ("w_in", "w_pool_proj")
    in_red, tkn = reduce_start(in_keys, dict(
        w_in=_shard_d_w_in(d_wt_cat).reshape(4, 2, IN_SHARD, D_MODEL),
        w_pool_proj=d_wpp.reshape(POOL_WIDTH, N_DEV, D_MODEL // N_DEV).transpose(1, 0, 2).astype(BF)
        .reshape(4, 2, POOL_WIDTH, D_MODEL // N_DEV)))
    reduce_done(mix_keys, mix_red, tkn)
    in_red, tkn = reduce_cross(in_keys, in_red, res["w_out"][0])
    grad_x, dg_mix = _mm_tokens(dzcat, wt_cat, blk_a=(TOK_MM_TILE, N_DZ), map_a=lambda i: (i, 0),
                                pieces=[(None, 0, N_DZ)], after=tkn, then=("rms_bwd", xs, g_mix, dx1),
                                name="mm_d_h_rms")
    (g_mix_sent,), tkn = _split_start("g_mix_start", [to_all([dg_mix])])
    reduce_done(ffn_keys, ffn_red, (grad_x, tkn))
    gathered = _split_wait("small_wait", small_sent, _gather_direct, res["w_down"][0])[1::2]
    small.append(("g_mix", dg_mix, g_mix, m_g_mix, v_g_mix, False))
    gathered = list(gathered[:-1]) + [_split_wait("g_mix_wait", g_mix_sent, _gather_direct, gathered[0])[1], gathered[-1]]
    small_out, loss_sum = _small_sum_adamw(jnp.reshape(me, (1,)).astype(jnp.int32),
                                           [(p,) + t[2:] for p, t in zip(gathered, small)], gathered[-1])
    for t, outs in zip(small, small_out):
        res[t[0]] = list(outs)
    res["b_conv"] = [t.reshape(b_conv.shape) for t in res["b_conv"]]
    res["g_final"] = [t.reshape(g_final.shape) for t in res["g_final"]]

    reduce_done(in_keys, in_red, loss_sum)
    loss = loss_sum[0, 0]
    order =["g_mix", "w_in", "b_gate", "w_gk_up", "b_gk", "w_pool_grp", "pool_scale", "g_gla_head", "w_pool_proj",
             "w_gla_proj", "w_out", "g_ffn", "w_up", "w_conv", "b_conv", "w_down", "g_final"]
    return (loss, grad_x[None], *[res[k][0] for k in order], *[res[k][1] for k in order],
            *[res[k][2] for k in order], *[res[k][3] for k in order])
```

```python
import jax
import jax.numpy as jnp
from jax import lax
from jax.experimental import pallas as pl
from jax.experimental.pallas import tpu as pltpu

F32 = jnp.float32
BF = jnp.bfloat16
HIGHEST = lax.Precision.HIGHEST
MESH = pl.DeviceIdType.MESH

N_DEV = 8
SEQ = 2048
D_MODEL = 1024
CHUNK = 64
EPS = 1e-6
POOL_WIDTH = 512
POOL_WINDOWS = (2, 4, 8, 16)
POOL_GD = 128
POOL_HALO = 16
HEADS = 4
HK = 128
HV = 256
GLA_DK = 512
GATE_RANK = 16
GATE_NORM = 16.0
D_FF = 2816
FF_BLK = 704
IN_SHARD = 706
C_QKV, C_GATE, C_OG, C_POOL, C_GK = 0, 2048, 4096, 5120, 5632
N_CAT = 5632
GK_PAD = 128
N_DZ = N_CAT + GK_PAD
R_POOL, R_QKV, R_OG, R_GK, R_GATE = 0, 512, 2560, 3584, 3600

ADAM_LR, ADAM_B1, ADAM_B2, ADAM_EPS, ADAM_WD, ADAM_STEP = 0.001, 0.9, 0.999, 1e-08, 0.01, 10
ADAM_C1 = 1.0 - ADAM_B1 ** ADAM_STEP
ADAM_C2 = 1.0 - ADAM_B2 ** ADAM_STEP

VMEM_BYTES_V7X = 64 * 1024 * 1024
VMEM_LIMIT = VMEM_BYTES_V7X * 3 // 4

TOK_TILE = 256
HALO = 8
GLA_CPS = 4


def _params(*sem):
    return pltpu.CompilerParams(dimension_semantics=sem, vmem_limit_bytes=VMEM_LIMIT)


def _const_spec(shape):
    nd = len(shape)
    return pl.BlockSpec(shape, lambda *_: (0,) * nd)


def _in_hbm(t):
    return pltpu.with_memory_space_constraint(t, pltpu.HBM)


def _dot(a, b, ta=False, tb=False):
    dims = (((0 if ta else 1,), (1 if tb else 0,)), ((), ()))
    return lax.dot_general(a.astype(BF), b.astype(BF), dims, preferred_element_type=F32)


def _dot_exact(a, b):
    return jnp.dot(a, b, precision=HIGHEST, preferred_element_type=F32)


def _sigmoid(x):
    return 0.5 * jnp.tanh(0.5 * x) + 0.5


def _mm(a, b, *, out_shape, out_dtype, grid, blk_a, blk_b, blk_o, map_a, map_b, map_o, ta=False, tb=False,
        after=None, name):
    gk = grid[2]
    n_in = 2 + (after is not None)

    def body(*refs):
        a_ref, b_ref, o_ref = refs[0], refs[1], refs[n_in]
        prod = _dot(a_ref[...], b_ref[...], ta, tb)
        if gk == 1:
            o_ref[...] = prod.astype(out_dtype)
        else:
            acc = refs[n_in + 1]
            k = pl.program_id(2)

            @pl.when(k == 0)
            def _():
                acc[...] = prod

            @pl.when(k > 0)
            def _():
                acc[...] += prod

            @pl.when(k == gk - 1)
            def _():
                o_ref[...] = acc[...].astype(out_dtype)

    in_specs = [pl.BlockSpec(blk_a, map_a), pl.BlockSpec(blk_b, map_b)]
    args = [_in_hbm(a), _in_hbm(b)]
    if after is not None:
        in_specs.append(pl.BlockSpec(memory_space=pl.ANY))
        args.append(after)
    return pl.pallas_call(
        body, name=name, grid=grid, in_specs=in_specs, out_specs=pl.BlockSpec(blk_o, map_o),
        out_shape=jax.ShapeDtypeStruct(out_shape, out_dtype),
        scratch_shapes=[] if gk == 1 else [pltpu.VMEM(tuple(d for d in blk_o if d is not None), F32)],
        compiler_params=_params("parallel", "parallel", "arbitrary"),
    )(*args)


TOK_MM_TILE = 256


def _mm_tokens(a, w, *, blk_a, map_a, pieces, res=None, after=None, then=None, name):
    n_in = 2 + (res is not None) + (after is not None) + (0 if then is None else len(then) - 1)

    def accumulate(ref, part):
        @pl.when(pl.program_id(0) == 0)
        def _():
            ref[...] = part

        @pl.when(pl.program_id(0) > 0)
        def _():
            ref[...] += part

    def body(*refs):
        a_ref, w_ref = refs[:2]
        extra, outs = refs[n_in - (0 if then is None else len(then) - 1):n_in], refs[n_in:]
        total = None
        for idx, row, n in pieces:
            av = a_ref[...] if idx is None else a_ref[idx]
            prod = _dot(av, w_ref[row:row + n, :])
            total = prod if total is None else total + prod
        if res is not None:
            total = total + refs[2][...]
        if then is None:
            outs[0][...] = total
        elif then[0] == "rms_bwd":
            dx, part = _rms_bwd_tile(total, extra[0][...], extra[1][...], extra[2][...])
            outs[0][...] = dx
            accumulate(outs[1], part)
        else:
            lpart, dx, part = _loss_tile(total, extra[0][...], extra[1][...])
            outs[1][...] = dx
            outs[2][...] = dx.astype(BF)
            accumulate(outs[0], lpart)
            accumulate(outs[3], part)

    tile = pl.BlockSpec((TOK_MM_TILE, D_MODEL), lambda i: (i, 0))
    vec = _const_spec((1, D_MODEL))
    big = jax.ShapeDtypeStruct((SEQ, D_MODEL), F32)
    small = jax.ShapeDtypeStruct((1, D_MODEL), F32)
    in_specs = [pl.BlockSpec(blk_a, map_a), pl.BlockSpec(w.shape, lambda i: (0, 0), pipeline_mode=pl.Buffered(1))]
    args = [a, w]
    if res is not None:
        in_specs.append(tile)
        args.append(res)
    if after is not None:
        in_specs.append(pl.BlockSpec(memory_space=pl.ANY))
        args.append(after)
    if then is None:
        out_specs, out_shape = tile, big
    elif then[0] == "rms_bwd":
        in_specs += [tile, vec, tile]
        out_specs, out_shape = [tile, vec], [big, small]
    else:
        in_specs += [vec, tile]
        out_specs = [_const_spec((1, 128)), tile, tile, vec]
        out_shape = [jax.ShapeDtypeStruct((1, 128), F32), big, jax.ShapeDtypeStruct((SEQ, D_MODEL), BF), small]
    if then is not None:
        args += list(then[1:])
    return pl.pallas_call(
        body, name=name, grid=(SEQ // TOK_MM_TILE,), in_specs=in_specs, out_specs=out_specs, out_shape=out_shape,
        compiler_params=_params("parallel" if then is None else "arbitrary"),
    )(*[_in_hbm(t) for t in args])


def _rms_fwd(x, g, name):
    def body(x_ref, g_ref, o_ref):
        xv = x_ref[...]
        r = lax.rsqrt(jnp.mean(xv * xv, axis=-1, keepdims=True) + EPS)
        o_ref[...] = (xv * r * g_ref[...]).astype(BF)

    tile = pl.BlockSpec((TOK_TILE, D_MODEL), lambda i: (i, 0))
    return pl.pallas_call(
        body, name=name, grid=(SEQ // TOK_TILE,), in_specs=[tile, _const_spec((1, D_MODEL))], out_specs=tile,
        out_shape=jax.ShapeDtypeStruct((SEQ, D_MODEL), BF), compiler_params=_params("parallel"),
    )(*map(_in_hbm, (x, g)))


def _rms_bwd_tile(dyv, xv, gv, dresv):
    r = lax.rsqrt(jnp.mean(xv * xv, axis=-1, keepdims=True) + EPS)
    xn = xv * r
    dxn = dyv * gv
    return dresv + r * (dxn - xn * jnp.mean(dxn * xn, axis=-1, keepdims=True)), jnp.sum(dyv * xn, axis=0, keepdims=True)


def _loss_tile(xv, gv, tv):
    r = lax.rsqrt(jnp.mean(xv * xv, axis=-1, keepdims=True) + EPS)
    xn = xv * r
    err = xn * gv - tv
    lpart = jnp.full((1, 128), 0.5 * jnp.sum(jnp.mean(err * err, axis=-1, keepdims=True)), F32)
    dyv = err * (1.0 / D_MODEL)
    dxn = dyv * gv
    return lpart, r * (dxn - xn * jnp.mean(dxn * xn, axis=-1, keepdims=True)), jnp.sum(dyv * xn, axis=0, keepdims=True)


def _pool_counts(w):
    pos = lax.broadcasted_iota(jnp.int32, (SEQ, 1), 0).astype(F32)
    return jnp.minimum(pos + 1.0, float(w))


def _pool_window(u, w, ext):
    ext[pl.ds(POOL_HALO, SEQ), :] = u
    win = u
    for j in range(1, w):
        win = win + ext[pl.ds(POOL_HALO - j, SEQ), :]
    return win / _pool_counts(w) - u


def _pool_fwd(zcat, w_grp, scale):
    def body(z_ref, w_ref, s_ref, o_ref, ext):
        ext[pl.ds(0, POOL_HALO), :] = jnp.zeros((POOL_HALO, POOL_GD), F32)
        for g, w in enumerate(POOL_WINDOWS):
            cols = slice(g * POOL_GD, (g + 1) * POOL_GD)
            p = _pool_window(z_ref[:, cols].astype(F32), w, ext)
            o_ref[:, cols] = (_dot(p, w_ref[g]) * s_ref[:, cols]).astype(BF)

    return pl.pallas_call(
        body, name="pool_fwd", grid=(1,),
        in_specs=[pl.BlockSpec((SEQ, POOL_WIDTH), lambda i: (0, C_POOL // POOL_WIDTH)),
                  _const_spec((4, POOL_GD, POOL_GD)), _const_spec((1, POOL_WIDTH))],
        out_specs=_const_spec((SEQ, POOL_WIDTH)), out_shape=jax.ShapeDtypeStruct((SEQ, POOL_WIDTH), BF),
        scratch_shapes=[pltpu.VMEM((POOL_HALO + SEQ, POOL_GD), F32)], compiler_params=_params("arbitrary"),
    )(*map(_in_hbm, (zcat, w_grp, scale)))


def _pool_bwd(dzcat, zcat, dps, w_grp, scale):
    def body(dz_in, z_ref, dps_ref, w_ref, s_ref, dz_ref, dw_ref, dsc_ref, ext, ext2):
        del dz_in
        ext[pl.ds(0, POOL_HALO), :] = jnp.zeros((POOL_HALO, POOL_GD), F32)
        ext2[pl.ds(SEQ, POOL_HALO), :] = jnp.zeros((POOL_HALO, POOL_GD), F32)
        for g, w in enumerate(POOL_WINDOWS):
            cols = slice(g * POOL_GD, (g + 1) * POOL_GD)
            p = _pool_window(z_ref[:, cols].astype(F32), w, ext)
            wg = w_ref[g]
            pg = _dot(p, wg)
            dpsv = dps_ref[:, cols]
            dsc_ref[:, cols] = jnp.sum(dpsv * pg, axis=0, keepdims=True)
            dpg = dpsv * s_ref[:, cols]
            dw_ref[g] = _dot(p, dpg, ta=True)
            dp = _dot(dpg, wg, tb=True)
            dpc = dp / _pool_counts(w)
            ext2[pl.ds(0, SEQ), :] = dpc
            du = dpc
            for j in range(1, w):
                du = du + ext2[pl.ds(j, SEQ), :]
            dz_ref[:, cols] = (du - dp).astype(BF)

    return pl.pallas_call(
        body, name="pool_bwd", grid=(1,),
        in_specs=[pl.BlockSpec(memory_space=pl.ANY),
                  pl.BlockSpec((SEQ, POOL_WIDTH), lambda i: (0, C_POOL // POOL_WIDTH)),
                  _const_spec((SEQ, POOL_WIDTH)), _const_spec((4, POOL_GD, POOL_GD)), _const_spec((1, POOL_WIDTH))],
        out_specs=[pl.BlockSpec((SEQ, POOL_WIDTH), lambda i: (0, C_POOL // POOL_WIDTH)),
                   _const_spec((4, POOL_GD, POOL_GD)), _const_spec((1, POOL_WIDTH))],
        out_shape=[jax.ShapeDtypeStruct((SEQ, N_DZ), BF), jax.ShapeDtypeStruct((4, POOL_GD, POOL_GD), F32),
                   jax.ShapeDtypeStruct((1, POOL_WIDTH), F32)],
        scratch_shapes=[pltpu.VMEM((POOL_HALO + SEQ, POOL_GD), F32), pltpu.VMEM((SEQ + POOL_HALO, POOL_GD), F32)],
        input_output_aliases={0: 0}, compiler_params=_params("arbitrary"),
    )(*map(_in_hbm, (dzcat, zcat, dps, w_grp, scale)))


GK_TILE = 512


GK_ROWS = pl.BlockSpec((GK_PAD, D_MODEL), lambda i: (C_GK // GK_PAD, 0))


def _gk_fwd(h, wt_cat, wgk_pad, b_gk):
    def body(h_ref, wt_ref, w_ref, b_ref, la_ref):
        z_gk = _dot(h_ref[...], wt_ref[...], tb=True)
        pre = _dot(z_gk, w_ref[...]) + b_ref[...]
        la_ref[...] = (jnp.minimum(pre, 0.0) - jnp.log(1.0 + jnp.exp(-jnp.abs(pre)))) * (1.0 / GATE_NORM)

    return pl.pallas_call(
        body, name="gk_fwd", grid=(SEQ // GK_TILE,),
        in_specs=[pl.BlockSpec((GK_TILE, D_MODEL), lambda i: (i, 0)), GK_ROWS,
                  _const_spec((GK_PAD, GLA_DK)), _const_spec((1, GLA_DK))],
        out_specs=pl.BlockSpec((GK_TILE, GLA_DK), lambda i: (i, 0)),
        out_shape=jax.ShapeDtypeStruct((SEQ, GLA_DK), F32), compiler_params=_params("parallel"),
    )(*map(_in_hbm, (h, wt_cat, wgk_pad, b_gk)))


def _gk_bwd(dzcat, dla, h, wt_cat, wgk_pad, b_gk):
    def body(dz_in, dla_ref, h_ref, wt_ref, w_ref, b_ref, dz_ref, dw_ref, db_ref):
        del dz_in
        wv = w_ref[...]
        z_gk = _dot(h_ref[...], wt_ref[...], tb=True)
        pre = _dot(z_gk, wv) + b_ref[...]
        dpre = dla_ref[...] * (1.0 / GATE_NORM) * (1.0 - _sigmoid(pre))
        dz_ref[...] = _dot(dpre, wv, tb=True).astype(BF)
        dwp = _dot(z_gk, dpre, ta=True)[:GATE_RANK]
        dbp = jnp.sum(dpre, axis=0, keepdims=True)

        @pl.when(pl.program_id(0) == 0)
        def _():
            dw_ref[...] = dwp
            db_ref[...] = dbp

        @pl.when(pl.program_id(0) > 0)
        def _():
            dw_ref[...] += dwp
            db_ref[...] += dbp

    return pl.pallas_call(
        body, name="gk_bwd", grid=(SEQ // GK_TILE,),
        in_specs=[pl.BlockSpec(memory_space=pl.ANY), pl.BlockSpec((GK_TILE, GLA_DK), lambda i: (i, 0)),
                  pl.BlockSpec((GK_TILE, D_MODEL), lambda i: (i, 0)), GK_ROWS, _const_spec((GK_PAD, GLA_DK)),
                  _const_spec((1, GLA_DK))],
        out_specs=[pl.BlockSpec((GK_TILE, GK_PAD), lambda i: (i, C_GK // GK_PAD)), _const_spec((GATE_RANK, GLA_DK)),
                   _const_spec((1, GLA_DK))],
        out_shape=[jax.ShapeDtypeStruct((SEQ, N_DZ), BF), jax.ShapeDtypeStruct((GATE_RANK, GLA_DK), F32),
                   jax.ShapeDtypeStruct((1, GLA_DK), F32)],
        input_output_aliases={0: 0}, compiler_params=_params("arbitrary"),
    )(*map(_in_hbm, (dzcat, dla, h, wt_cat, wgk_pad, b_gk)))


GLA_ROWS = GLA_CPS * CHUNK
GLA_STEPS = SEQ // GLA_ROWS
QKV_W = 2048


def _tri():
    return lax.broadcasted_iota(jnp.int32, (CHUNK, CHUNK), 0) >= lax.broadcasted_iota(jnp.int32, (CHUNK, CHUNK), 1)


def _chunk_cumsum(la_ref, rows):
    return _dot_exact(_tri().astype(F32), la_ref[rows, :])


def _gla_chunk(qkv_ref, la_ref, rows, h, bc_all):
    tri = _tri()
    q = qkv_ref[rows, h * HK:(h + 1) * HK].astype(F32) * (HK ** -0.5)
    k = qkv_ref[rows, GLA_DK + h * HK:GLA_DK + (h + 1) * HK].astype(F32)
    v = qkv_ref[rows, 2 * GLA_DK + h * HV:2 * GLA_DK + (h + 1) * HV].astype(BF)
    la = la_ref[rows, h * HK:(h + 1) * HK]
    bc = bc_all[:, h * HK:(h + 1) * HK]
    e_pos, e_neg = jnp.exp(bc), jnp.exp(-bc)
    dl = jnp.exp(jnp.sum(la, axis=0, keepdims=True))
    q_fw, q_bw, k_fw, k_bw = q * e_pos, q * e_neg, k * e_neg, k * e_pos
    scores = jnp.where(tri, _dot(q_fw, k_fw, tb=True), _dot(q_bw, k_bw, tb=True))
    return tri, v, e_pos, e_neg, dl, q_fw, q_bw, k_fw, k_bw, scores


def _gla_fwd(zcat, la, after):
    def body(qkv_ref, la_ref, after_ref, o_ref, st_ref, state):
        del after_ref

        @pl.when(pl.program_id(0) == 0)
        def _():
            state[...] = jnp.zeros_like(state)

        for c in range(GLA_CPS):
            rows = slice(c * CHUNK, (c + 1) * CHUNK)
            bc_all = _chunk_cumsum(la_ref, rows)
            for h in range(HEADS):
                _, v, _, _, dl, q_fw, _, k_fw, _, scores = _gla_chunk(qkv_ref, la_ref, rows, h, bc_all)
                st = state[h]
                st_ref[c, h] = st
                o_ref[rows, h * HV:(h + 1) * HV] = _dot(scores, v) + _dot(q_fw, st, tb=True)
                state[h] = st * dl + _dot(v, k_fw * dl, ta=True)

    return pl.pallas_call(
        body, name="gla_fwd", grid=(GLA_STEPS,),
        in_specs=[pl.BlockSpec((GLA_ROWS, QKV_W), lambda i: (i, 0)), pl.BlockSpec((GLA_ROWS, GLA_DK), lambda i: (i, 0)),
                  pl.BlockSpec(memory_space=pl.ANY)],
        out_specs=[pl.BlockSpec((GLA_ROWS, D_MODEL), lambda i: (i, 0)),
                   pl.BlockSpec((GLA_CPS, HEADS, HV, HK), lambda i: (i, 0, 0, 0))],
        out_shape=[jax.ShapeDtypeStruct((SEQ, D_MODEL), F32),
                   jax.ShapeDtypeStruct((SEQ // CHUNK, HEADS, HV, HK), F32)],
        scratch_shapes=[pltpu.VMEM((HEADS, HV, HK), F32)], compiler_params=_params("arbitrary"),
    )(*map(_in_hbm, (zcat, la)), after)


def _gla_bwd(dzcat, zcat, la, d_o, states):
    def body(dz_in, qkv_ref, la_ref, do_ref, st_ref, dqkv_ref, dla_ref, dstate):
        del dz_in

        @pl.when(pl.program_id(0) == 0)
        def _():
            dstate[...] = jnp.zeros_like(dstate)

        last_row = lax.broadcasted_iota(jnp.int32, (CHUNK, HK), 0) == CHUNK - 1
        upper = (lax.broadcasted_iota(jnp.int32, (CHUNK, CHUNK), 0)
                 <= lax.broadcasted_iota(jnp.int32, (CHUNK, CHUNK), 1)).astype(F32)
        for c in reversed(range(GLA_CPS)):
            rows = slice(c * CHUNK, (c + 1) * CHUNK)
            bc_all = _chunk_cumsum(la_ref, rows)
            dbs = []
            for h in range(HEADS):
                tri, v, e_pos, e_neg, dl, q_fw, q_bw, k_fw, k_bw, scores = _gla_chunk(qkv_ref, la_ref, rows, h, bc_all)
                st = st_ref[c, h]
                dst = dstate[h]
                d_out = do_ref[rows, h * HV:(h + 1) * HV].astype(BF)
                k_dec = k_fw * dl
                dp = _dot(d_out, v, tb=True)
                dp_fw = jnp.where(tri, dp, 0.0)
                dp_bw = jnp.where(tri, 0.0, dp)
                dv = _dot(scores, d_out, ta=True) + _dot(k_dec, dst, tb=True)
                dk_dec = _dot(v, dst)
                dq_fw = _dot(dp_fw, k_fw) + _dot(d_out, st)
                dk_fw = _dot(dp_fw, q_fw, ta=True) + dk_dec * dl
                dq_bw = _dot(dp_bw, k_bw)
                dk_bw = _dot(dp_bw, q_bw, ta=True)
                ddl = jnp.sum(st * dst, axis=0, keepdims=True) + jnp.sum(k_fw * dk_dec, axis=0, keepdims=True)
                dstate[h] = dst * dl + _dot(d_out, q_fw, ta=True)
                dq = (dq_fw * e_pos + dq_bw * e_neg) * (HK ** -0.5)
                dk = dk_fw * e_neg + dk_bw * e_pos
                dbs.append(dq_fw * q_fw - dk_fw * k_fw - dq_bw * q_bw + dk_bw * k_bw + jnp.where(last_row, ddl * dl, 0.0))
                dqkv_ref[rows, h * HK:(h + 1) * HK] = dq.astype(BF)
                dqkv_ref[rows, GLA_DK + h * HK:GLA_DK + (h + 1) * HK] = dk.astype(BF)
                dqkv_ref[rows, 2 * GLA_DK + h * HV:2 * GLA_DK + (h + 1) * HV] = dv.astype(BF)
            dla_ref[rows, :] = _dot_exact(upper, jnp.concatenate(dbs, axis=1))

    rev = lambda i: (GLA_STEPS - 1 - i, 0)
    return pl.pallas_call(
        body, name="gla_bwd", grid=(GLA_STEPS,),
        in_specs=[pl.BlockSpec(memory_space=pl.ANY), pl.BlockSpec((GLA_ROWS, QKV_W), rev),
                  pl.BlockSpec((GLA_ROWS, GLA_DK), rev), pl.BlockSpec((GLA_ROWS, D_MODEL), rev),
                  pl.BlockSpec((GLA_CPS, HEADS, HV, HK), lambda i: (GLA_STEPS - 1 - i, 0, 0, 0))],
        out_specs=[pl.BlockSpec((GLA_ROWS, QKV_W), rev), pl.BlockSpec((GLA_ROWS, GLA_DK), rev)],
        out_shape=[jax.ShapeDtypeStruct((SEQ, N_DZ), BF), jax.ShapeDtypeStruct((SEQ, GLA_DK), F32)],
        scratch_shapes=[pltpu.VMEM((HEADS, HV, HK), F32)], input_output_aliases={0: 0},
        compiler_params=_params("arbitrary"),
    )(*map(_in_hbm, (dzcat, zcat, la, d_o, states)))


def _silu_parts(x):
    s = _sigmoid(x)
    return x * s, s * (1.0 + x * (1.0 - s))


def _post_gla_fwd(o, zcat, g_head):
    def body(o_ref, zog_ref, g_ref, out_ref):
        for h in range(HEADS):
            cols = slice(h * HV, (h + 1) * HV)
            ov = o_ref[:, cols]
            r = lax.rsqrt(jnp.mean(ov * ov, axis=-1, keepdims=True) + EPS)
            act, _ = _silu_parts(zog_ref[:, cols].astype(F32))
            out_ref[:, cols] = (ov * r * g_ref[...] * act).astype(BF)

    tile = pl.BlockSpec((TOK_TILE, D_MODEL), lambda i: (i, 0))
    return pl.pallas_call(
        body, name="post_gla_fwd", grid=(SEQ // TOK_TILE,),
        in_specs=[tile, pl.BlockSpec((TOK_TILE, D_MODEL), lambda i: (i, C_OG // D_MODEL)), _const_spec((1, HV))],
        out_specs=tile, out_shape=jax.ShapeDtypeStruct((SEQ, D_MODEL), BF), compiler_params=_params("parallel"),
    )(*map(_in_hbm, (o, zcat, g_head)))


def _post_gla_bwd(dzcat, dy_gla, w_gla_proj, o, zcat, g_head):
    def body(dz_in, dyg_ref, w_ref, o_ref, zog_ref, g_ref, dz_ref, do_ref, dg_ref):
        del dz_in
        dog = _dot(dyg_ref[...], w_ref[...], tb=True)
        gpart = jnp.zeros((1, HV), F32)
        gv = g_ref[...]
        for h in range(HEADS):
            cols = slice(h * HV, (h + 1) * HV)
            ov = o_ref[:, cols]
            r = lax.rsqrt(jnp.mean(ov * ov, axis=-1, keepdims=True) + EPS)
            on = ov * r
            act, dact = _silu_parts(zog_ref[:, cols].astype(F32))
            dogv = dog[:, cols]
            dz_ref[:, cols] = (dogv * on * gv * dact).astype(BF)
            d_on_g = dogv * act
            gpart = gpart + jnp.sum(d_on_g * on, axis=0, keepdims=True)
            dxn = d_on_g * gv
            do_ref[:, cols] = (r * (dxn - on * jnp.mean(dxn * on, axis=-1, keepdims=True))).astype(BF)

        @pl.when(pl.program_id(0) == 0)
        def _():
            dg_ref[...] = gpart

        @pl.when(pl.program_id(0) > 0)
        def _():
            dg_ref[...] += gpart

    tile = pl.BlockSpec((TOK_TILE, D_MODEL), lambda i: (i, 0))
    ogspec = pl.BlockSpec((TOK_TILE, D_MODEL), lambda i: (i, C_OG // D_MODEL))
    return pl.pallas_call(
        body, name="post_gla_bwd", grid=(SEQ // TOK_TILE,),
        in_specs=[pl.BlockSpec(memory_space=pl.ANY), tile, _const_spec((D_MODEL, D_MODEL)), tile, ogspec,
                  _const_spec((1, HV))],
        out_specs=[ogspec, tile, _const_spec((1, HV))],
        out_shape=[jax.ShapeDtypeStruct((SEQ, N_DZ), BF), jax.ShapeDtypeStruct((SEQ, D_MODEL), BF),
                   jax.ShapeDtypeStruct((1, HV), F32)],
        input_output_aliases={0: 0}, compiler_params=_params("arbitrary"),
    )(*map(_in_hbm, (dzcat, dy_gla, w_gla_proj, o, zcat, g_head)))


GATE_W = 2 * D_MODEL


def _mix_out_fwd(ps, og, zcat, x, w_pool_proj, w_gla_proj, w_out, b_gate, g_ffn, after):
    def body(ps_ref, og_ref, zg_ref, x_ref, wpp_ref, wgp_ref, wout_ref, b_ref, g_ref, after_ref,
             yp_ref, yg_ref, mixed_ref, x1_ref, h2_ref):
        del after_ref
        y_pool = _dot(ps_ref[...], wpp_ref[...])
        y_gla = _dot(og_ref[...], wgp_ref[...])
        yp_ref[...] = y_pool.astype(BF)
        yg_ref[...] = y_gla.astype(BF)
        g0 = _sigmoid(zg_ref[:, :D_MODEL].astype(F32) + b_ref[:, :D_MODEL])
        g1 = _sigmoid(zg_ref[:, D_MODEL:].astype(F32) + b_ref[:, D_MODEL:])
        mixed = (g0 * y_pool + g1 * y_gla).astype(BF)
        mixed_ref[...] = mixed
        x1 = x_ref[...] + _dot(mixed, wout_ref[...])
        x1_ref[...] = x1
        r = lax.rsqrt(jnp.mean(x1 * x1, axis=-1, keepdims=True) + EPS)
        h2_ref[...] = (x1 * r * g_ref[...]).astype(BF)

    tile = pl.BlockSpec((TOK_TILE, D_MODEL), lambda i: (i, 0))
    resident = lambda shape: pl.BlockSpec(shape, lambda i: (0, 0), pipeline_mode=pl.Buffered(1))
    f32, bf16 = jax.ShapeDtypeStruct((SEQ, D_MODEL), F32), jax.ShapeDtypeStruct((SEQ, D_MODEL), BF)
    return pl.pallas_call(
        body, name="mix_out_fwd", grid=(SEQ // TOK_TILE,),
        in_specs=[pl.BlockSpec((TOK_TILE, POOL_WIDTH), lambda i: (i, 0)), tile,
                  pl.BlockSpec((TOK_TILE, GATE_W), lambda i: (i, C_GATE // GATE_W)), tile,
                  resident((POOL_WIDTH, D_MODEL)), resident((D_MODEL, D_MODEL)), resident((D_MODEL, D_MODEL)),
                  _const_spec((1, GATE_W)), _const_spec((1, D_MODEL)), pl.BlockSpec(memory_space=pl.ANY)],
        out_specs=[tile] * 5, out_shape=[bf16, bf16, bf16, f32, bf16], compiler_params=_params("parallel"),
    )(*map(_in_hbm, (ps, og, zcat, x, w_pool_proj, w_gla_proj, w_out, b_gate, g_ffn)), after)


def _mix_bwd(dx1, w_out, zcat, b_gate, y_pool, y_gla):
    def body(dx_ref, w_ref, zg_ref, b_ref, yp_ref, yg_ref, dz_ref, dyp_ref, dyg_ref, db_ref):
        dm = _dot(dx_ref[...], w_ref[...], tb=True)
        g0 = _sigmoid(zg_ref[:, :D_MODEL].astype(F32) + b_ref[:, :D_MODEL])
        g1 = _sigmoid(zg_ref[:, D_MODEL:].astype(F32) + b_ref[:, D_MODEL:])
        dyp_ref[...] = (dm * g0).astype(BF)
        dyg_ref[...] = (dm * g1).astype(BF)
        dz0 = dm * yp_ref[...].astype(F32) * g0 * (1.0 - g0)
        dz1 = dm * yg_ref[...].astype(F32) * g1 * (1.0 - g1)
        dz_ref[:, :D_MODEL] = dz0.astype(BF)
        dz_ref[:, D_MODEL:] = dz1.astype(BF)
        b0 = jnp.sum(dz0, axis=0, keepdims=True)
        b1 = jnp.sum(dz1, axis=0, keepdims=True)

        @pl.when(pl.program_id(0) == 0)
        def _():
            db_ref[:, :D_MODEL] = b0
            db_ref[:, D_MODEL:] = b1

        @pl.when(pl.program_id(0) > 0)
        def _():
            db_ref[:, :D_MODEL] += b0
            db_ref[:, D_MODEL:] += b1

    tile = pl.BlockSpec((TOK_TILE, D_MODEL), lambda i: (i, 0))
    gspec = pl.BlockSpec((TOK_TILE, GATE_W), lambda i: (i, C_GATE // GATE_W))
    return pl.pallas_call(
        body, name="mix_bwd", grid=(SEQ // TOK_TILE,),
        in_specs=[tile, _const_spec((D_MODEL, D_MODEL)), gspec, _const_spec((1, GATE_W)), tile, tile],
        out_specs=[gspec, tile, tile, _const_spec((1, GATE_W))],
        out_shape=[jax.ShapeDtypeStruct((SEQ, N_DZ), BF), jax.ShapeDtypeStruct((SEQ, D_MODEL), BF),
                   jax.ShapeDtypeStruct((SEQ, D_MODEL), BF), jax.ShapeDtypeStruct((1, GATE_W), F32)],
        compiler_params=_params("arbitrary"),
    )(*map(_in_hbm, (dx1, w_out, zcat, b_gate, y_pool, y_gla)))


N_TOK_TILES = SEQ // TOK_TILE
HALO_PER_TILE = TOK_TILE // HALO


LANE_TILES = tuple((lo, min(128, FF_BLK - lo)) for lo in range(0, FF_BLK, 128))


def _taps(w_ref, b_ref, half, lanes, rows):
    shape = (rows, lanes.stop - lanes.start)
    return ([jnp.broadcast_to(w_ref[half, j:j + 1, lanes], shape) for j in range(3)],
            jnp.broadcast_to(b_ref[half, :, lanes], shape))


def _conv_strips(u_ref, ub_ref, ua_ref, taps, lanes, width, n_strips, first):
    row = lax.broadcasted_iota(jnp.int32, (HALO, width), 0)
    prev = [[pltpu.roll(jnp.where(first, 0.0, ub_ref[half, :, lanes]), k, 0) for k in (1, 2)] for half in range(2)]
    for s in range(n_strips + (ua_ref is not None)):
        u3, conv = [], []
        for half in range(2):
            cur = u_ref[half, s * HALO:(s + 1) * HALO, lanes] if s < n_strips else ua_ref[half, :, lanes]
            rolled = [pltpu.roll(cur, k, 0) for k in (1, 2)]
            frames = [jnp.where(row >= 2, rolled[1], prev[half][1]), jnp.where(row >= 1, rolled[0], prev[half][0]), cur]
            prev[half] = rolled
            w3, bias = taps[half]
            u3.append(frames)
            conv.append(bias + frames[0] * w3[0] + frames[1] * w3[1] + frames[2] * w3[2])
        yield s, u3, conv


def _pair_specs(pairs):
    tile = pl.BlockSpec((pairs, None, TOK_TILE, FF_BLK), lambda b, i: (0, b, i, 0))
    before = pl.BlockSpec((pairs, None, HALO, FF_BLK), lambda b, i: (0, b, jnp.maximum(i * HALO_PER_TILE - 1, 0), 0))
    after = pl.BlockSpec((pairs, None, HALO, FF_BLK),
                         lambda b, i: (0, b, jnp.minimum((i + 1) * HALO_PER_TILE, SEQ // HALO - 1), 0))

    def vec(rows):
        return pl.BlockSpec((2, None, rows, FF_BLK), lambda b, i: (0, b, 0, 0))

    return tile, before, after, vec


N_STRIPS = TOK_TILE // HALO


def _up_conv_fwd(h2, wt_up, w_conv, b_conv):
    steps = N_TOK_TILES // 2

    def body(h_ref, h_next, wg_ref, wv_ref, w_ref, b_ref, u_ref, a_ref, buf_a, buf_b, carry):
        j = pl.program_id(1)

        def project(hv, buf):
            buf[0] = _dot(hv, wg_ref[...], tb=True)
            buf[1] = _dot(hv, wv_ref[...], tb=True)

        def conv(buf, row0):
            u_ref[:, row0:row0 + TOK_TILE, :] = buf[...]
            for lo, width in LANE_TILES:
                lanes = slice(lo, lo + width)
                taps = [_taps(w_ref, b_ref, half, lanes, HALO) for half in range(2)]
                pending = None
                for s, _, (cg, cv) in _conv_strips(buf, carry, None, taps, lanes, width, N_STRIPS, False):
                    act = cg * _sigmoid(cg) * cv
                    if s % 2 == 0:
                        pending = act
                    else:
                        a_ref[0, row0 + (s - 1) * HALO:row0 + (s + 1) * HALO, lanes] = (
                            jnp.concatenate([pending, act], axis=0).astype(BF))
            carry[...] = buf[:, TOK_TILE - HALO:, :]

        @pl.when(j == 0)
        def _():
            project(h_ref[0:TOK_TILE, :], buf_a)
            carry[...] = jnp.zeros_like(carry)

        project(h_ref[TOK_TILE:, :], buf_b)
        conv(buf_a, 0)
        project(h_next[...], buf_a)
        conv(buf_b, TOK_TILE)

    w_blk = lambda half: pl.BlockSpec((FF_BLK, D_MODEL), lambda b, j: (b + 4 * half, 0))
    vec = lambda rows: pl.BlockSpec((2, None, rows, FF_BLK), lambda b, j: (0, b, 0, 0))
    u_buf = pltpu.VMEM((2, TOK_TILE, FF_BLK), F32)
    return pl.pallas_call(
        body, name="up_conv_fwd", grid=(4, steps),
        in_specs=[pl.BlockSpec((2 * TOK_TILE, D_MODEL), lambda b, j: (j, 0)),
                  pl.BlockSpec((TOK_TILE, D_MODEL), lambda b, j: (jnp.minimum(2 * j + 2, N_TOK_TILES - 1), 0)),
                  w_blk(0), w_blk(1), vec(3), vec(1)],
        out_specs=[pl.BlockSpec((2, None, 2 * TOK_TILE, FF_BLK), lambda b, j: (0, b, j, 0)),
                   pl.BlockSpec((1, None, 2 * TOK_TILE, FF_BLK), lambda b, j: (0, b, j, 0))],
        out_shape=[jax.ShapeDtypeStruct((2, 4, SEQ, FF_BLK), F32), jax.ShapeDtypeStruct((1, 4, SEQ, FF_BLK), BF)],
        scratch_shapes=[u_buf, u_buf, pltpu.VMEM((2, HALO, FF_BLK), F32)],
        compiler_params=_params("parallel", "arbitrary"),
    )(*map(_in_hbm, (h2, h2, wt_up, wt_up, w_conv, b_conv)))


def _conv_bwd(u, da, w_conv, b_conv):
    def body(u_ref, ub_ref, ua_ref, da_ref, daa_ref, w_ref, b_ref, du_ref, dw_ref, db_ref):
        i = pl.program_id(1)

        @pl.when(i == 0)
        def _():
            dw_ref[...] = jnp.zeros_like(dw_ref)
            db_ref[...] = jnp.zeros_like(db_ref)

        for lo, width in LANE_TILES:
            lanes = slice(lo, lo + width)
            row = lax.broadcasted_iota(jnp.int32, (HALO, width), 0)
            taps = [_taps(w_ref, b_ref, half, lanes, HALO) for half in range(2)]
            acc_w = [[jnp.zeros((HALO, width), F32) for _ in range(3)] for _ in range(2)]
            acc_b = [jnp.zeros((HALO, width), F32) for _ in range(2)]
            da_pair, pending = None, [None, None]
            dc_prev, up_prev = [None, None], [None, None]
            for s, u3, (cg, cv) in _conv_strips(u_ref, ub_ref, ua_ref, taps, lanes, width, N_STRIPS, i == 0):
                act, dact = _silu_parts(cg)
                if s == N_STRIPS:
                    da = jnp.where(i < N_TOK_TILES - 1, daa_ref[0, :, lanes].astype(F32), 0.0)
                elif s % 2 == 0:
                    da_pair = da_ref[0, s * HALO:(s + 2) * HALO, lanes].astype(F32)
                    da = da_pair[:HALO]
                else:
                    da = da_pair[HALO:]
                dc = (da * cv * dact, da * act)
                for half in range(2):
                    up = [pltpu.roll(dc[half], HALO - k, 0) for k in (1, 2)]
                    if s < N_STRIPS:
                        for j in range(3):
                            acc_w[half][j] = acc_w[half][j] + dc[half] * u3[half][j]
                        acc_b[half] = acc_b[half] + dc[half]
                    if s >= 1:
                        w3 = taps[half][0]
                        du = (dc_prev[half] * w3[2] + jnp.where(row < HALO - 1, up_prev[half][0], up[0]) * w3[1]
                              + jnp.where(row < HALO - 2, up_prev[half][1], up[1]) * w3[0])
                        if (s - 1) % 2 == 0:
                            pending[half] = du
                        else:
                            du_ref[half, (s - 2) * HALO:s * HALO, lanes] = jnp.concatenate([pending[half], du],
                                                                                           axis=0).astype(BF)
                    dc_prev[half], up_prev[half] = dc[half], up
            for half in range(2):
                for j in range(3):
                    dw_ref[half, j:j + 1, lanes] += jnp.sum(acc_w[half][j], axis=0, keepdims=True)
                db_ref[half, :, lanes] += jnp.sum(acc_b[half], axis=0, keepdims=True)

    tile, before, after, vec = _pair_specs(2)
    da_tile, _, da_after_spec, _ = _pair_specs(1)
    return pl.pallas_call(
        body, name="conv_bwd", grid=(4, N_TOK_TILES),
        in_specs=[tile, before, after, da_tile, da_after_spec, vec(3), vec(1)],
        out_specs=[tile, vec(3), vec(1)],
        out_shape=[jax.ShapeDtypeStruct((2, 4, SEQ, FF_BLK), BF), jax.ShapeDtypeStruct((2, 4, 3, FF_BLK), F32),
                   jax.ShapeDtypeStruct((2, 4, 1, FF_BLK), F32)],
        compiler_params=_params("parallel", "arbitrary"),
    )(*map(_in_hbm, (u, u, u, da, da, w_conv, b_conv)))


W_IN_SEGMENTS = ((R_POOL, POOL_WIDTH, C_POOL), (R_QKV, QKV_W, C_QKV), (R_OG, D_MODEL, C_OG), (R_GK, GATE_RANK, C_GK),
                 (R_GATE, GATE_W, C_GATE))


def _slab_pieces(d):
    lo, hi = d * IN_SHARD, (d + 1) * IN_SHARD
    pieces = []
    for start, n, at in W_IN_SEGMENTS:
        a, b = max(lo, start), min(hi, start + n)
        if a < b:
            assert (a - lo) % 2 == 0 and (b - a) % 2 == 0 and (at + a - start) % 2 == 0
            pieces.append(((a - lo) // 2, (b - a) // 2, (at + a - start) // 2))
    return pieces


def _unshard_w_in(slabs):
    def body(slab_ref, cat_ref):
        d = pl.program_id(0)
        src = slab_ref.bitcast(jnp.uint32)
        dst = cat_ref.bitcast(jnp.uint32)

        @pl.when(d == 0)
        def _():
            cat_ref[C_GK:, :] = jnp.zeros((GK_PAD, D_MODEL), BF)

        for dd in range(N_DEV):
            @pl.when(d == dd)
            def _():
                for a, n, at in _slab_pieces(dd):
                    dst[pl.ds(at, n), :] = src[0, pl.ds(a, n), :]

    return pl.pallas_call(
        body, name="unshard_w_in", grid=(N_DEV,),
        in_specs=[pl.BlockSpec((1, IN_SHARD, D_MODEL), lambda d: (d, 0, 0))], out_specs=_const_spec((N_DZ, D_MODEL)),
        out_shape=jax.ShapeDtypeStruct((N_DZ, D_MODEL), BF), compiler_params=_params("arbitrary"),
    )(_in_hbm(slabs))


def _shard_d_w_in(d_cat):
    def body(cat_ref, slab_ref):
        d = pl.program_id(0)
        cat = cat_ref.bitcast(jnp.uint32)
        dst = slab_ref.bitcast(jnp.uint32)
        for dd in range(N_DEV):
            @pl.when(d == dd)
            def _():
                for a, n, at in _slab_pieces(dd):
                    dst[0, pl.ds(a, n), :] = cat[pl.ds(at, n), :]

    return pl.pallas_call(
        body, name="shard_d_w_in", grid=(N_DEV,), in_specs=[_const_spec((N_DZ, D_MODEL))],
        out_specs=pl.BlockSpec((1, IN_SHARD, D_MODEL), lambda d: (d, 0, 0)),
        out_shape=jax.ShapeDtypeStruct((N_DEV, IN_SHARD, D_MODEL), BF), compiler_params=_params("parallel"),
    )(_in_hbm(d_cat))


ANY = pl.BlockSpec(memory_space=pl.ANY)


def _place():
    x, y, c = lax.axis_index("x"), lax.axis_index("y"), lax.axis_index("c")
    other_chips = [(1 - x, y), (x, 1 - y), (1 - x, 1 - y)]
    return x, y, c, other_chips


SEM = pl.BlockSpec(memory_space=pltpu.SEMAPHORE)
IN_HBM = pl.BlockSpec(memory_space=pltpu.HBM)
SPLIT_PARAMS = pltpu.CompilerParams(has_side_effects=pltpu.SideEffectType.DATAFLOW_SIDE_EFFECTING)


def _gather_first(refs, send_sems, recv_sems):
    x, y, c, chips = _place()
    targets = [(x, y, 1 - c)] + [(px, py, c) for px, py in chips]
    return [pltpu.make_async_remote_copy(src_ref=refs[2 * a], dst_ref=refs[2 * a + 1].at[4 * x + 2 * y + c],
                                         send_sem=send_sems.at[4 * a + k], recv_sem=recv_sems.at[4 * a + k],
                                         device_id=to, device_id_type=MESH)
            for a in range(len(refs) // 2) for k, to in enumerate(targets)]


def _gather_direct(refs, send_sems, recv_sems):
    x, y, c, _ = _place()
    flips = [(dx, dy, dc) for dx in (0, 1) for dy in (0, 1) for dc in (0, 1) if dx + dy + dc]
    targets = [(1 - x if dx else x, 1 - y if dy else y, 1 - c if dc else c) for dx, dy, dc in flips]
    return [pltpu.make_async_remote_copy(src_ref=refs[2 * a], dst_ref=refs[2 * a + 1].at[4 * x + 2 * y + c],
                                         send_sem=send_sems.at[7 * a + k], recv_sem=recv_sems.at[7 * a + k],
                                         device_id=to, device_id_type=MESH)
            for a in range(len(refs) // 2) for k, to in enumerate(targets)]


def _gather_second(refs, send_sems, recv_sems):
    x, y, c, chips = _place()
    copies = []
    for a, land in enumerate(refs):
        for j, (px, py) in enumerate(chips):
            block = land.at[4 * px + 2 * py + c]
            copies.append(pltpu.make_async_remote_copy(src_ref=block, dst_ref=block, send_sem=send_sems.at[3 * a + j],
                                                       recv_sem=recv_sems.at[3 * a + j], device_id=(x, y, 1 - c),
                                                       device_id_type=MESH))
    return copies


def _reduce_first(refs, send_sems, recv_sems):
    x, y, c, _ = _place()
    return [pltpu.make_async_remote_copy(src_ref=refs[2 * a].at[j, 1 - c], dst_ref=refs[2 * a + 1].at[j],
                                         send_sem=send_sems.at[4 * a + j], recv_sem=recv_sems.at[4 * a + j],
                                         device_id=(x, y, 1 - c), device_id_type=MESH)
            for a in range(len(refs) // 2) for j in range(4)]


def _reduce_second(refs, send_sems, recv_sems):
    _, _, c, chips = _place()
    return [pltpu.make_async_remote_copy(src_ref=refs[2 * a].at[2 * px + py], dst_ref=refs[2 * a + 1].at[k],
                                         send_sem=send_sems.at[3 * a + k], recv_sem=recv_sems.at[3 * a + k],
                                         device_id=(px, py, c), device_id_type=MESH)
            for a in range(len(refs) // 2) for k, (px, py) in enumerate(chips)]


def _split_start(name, groups):
    arrays = [a for g in groups for a in g[0]]
    n = len(arrays)

    def body(*refs):
        sems = refs[n:n + 2 * len(groups)]
        at = 0
        for gi, (members, _, build) in enumerate(groups):
            for cp in build(refs[at:at + len(members)], sems[2 * gi], sems[2 * gi + 1]):
                cp.start()
            at += len(members)
        refs[-1][...] = jnp.zeros_like(refs[-1])

    sem_shapes = [pltpu.SemaphoreType.DMA((g[1],)) for g in groups for _ in range(2)]
    outs = pl.pallas_call(
        body, name=name, in_specs=[IN_HBM] * n,
        out_shape=(*sem_shapes, *[pltpu.HBM(a.shape, a.dtype) for a in arrays], jax.ShapeDtypeStruct((8, 128), F32)),
        out_specs=(*[SEM] * len(sem_shapes), *[IN_HBM] * n, pl.BlockSpec(memory_space=pltpu.VMEM)),
        input_output_aliases={i: len(sem_shapes) + i for i in range(n)}, compiler_params=SPLIT_PARAMS,
    )(*[pltpu.with_memory_space_constraint(a, pltpu.HBM) for a in arrays])
    per_group, at = [], len(sem_shapes)
    for gi, (members, _, _) in enumerate(groups):
        per_group.append((outs[2 * gi], outs[2 * gi + 1], list(outs[at:at + len(members)])))
        at += len(members)
    return per_group, outs[-1]


def _split_wait(name, started, build, after):
    send_sems, recv_sems, arrays = started
    n = len(arrays)
    after = after if isinstance(after, (tuple, list)) else (after,)

    def body(*refs):
        for cp in build(refs[:n], refs[n], refs[n + 1]):
            cp.wait_send()
            cp.wait_recv()

    return pl.pallas_call(
        body, name=name, in_specs=[IN_HBM] * n + [SEM, SEM] + [ANY] * len(after),
        out_shape=tuple(pltpu.HBM(a.shape, a.dtype) for a in arrays), out_specs=tuple([IN_HBM] * n),
        input_output_aliases={i: i for i in range(n)}, compiler_params=SPLIT_PARAMS,
    )(*arrays, send_sems, recv_sems, *after)


def _gather_landing(shard, me):
    return lax.dynamic_update_slice(lax.empty((N_DEV,) + shard.shape, shard.dtype), shard[None],
                                    (me,) + (0,) * shard.ndim)


ADAM_LANE_TILE = 256


def _tile_2d(rows, cols):
    for t in (256, 176, 128):
        if rows % t == 0:
            return t, cols
    return rows, ADAM_LANE_TILE


def _pair_sum(part, recv, core, name):
    _, rows, cols = recv.shape
    tr, tc = rows, cols

    def body(c_ref, p_ref, r_ref, o_ref):
        del c_ref
        o_ref[...] = (p_ref[...].astype(F32) + r_ref[...].astype(F32)).astype(BF)

    grid_spec = pltpu.PrefetchScalarGridSpec(
        num_scalar_prefetch=1, grid=(4, rows // tr, cols // tc),
        in_specs=[pl.BlockSpec((None, None, tr, tc), lambda j, i, k, c_ref: (j, c_ref[0], i, k)),
                  pl.BlockSpec((None, tr, tc), lambda j, i, k, c_ref: (j, i, k))],
        out_specs=pl.BlockSpec((None, tr, tc), lambda j, i, k, c_ref: (j, i, k)))
    return pl.pallas_call(
        body, name=name, grid_spec=grid_spec, out_shape=jax.ShapeDtypeStruct(recv.shape, BF),
        compiler_params=_params("parallel", "parallel", "parallel"),
    )(core, *map(_in_hbm, (part, recv)))


def _adamw(w, g, m, v):
    m = ADAM_B1 * m + (1.0 - ADAM_B1) * g
    v = ADAM_B2 * v + (1.0 - ADAM_B2) * (g * g)
    delta = -ADAM_LR * ((m / ADAM_C1) / (jnp.sqrt(v / ADAM_C2) + ADAM_EPS) + ADAM_WD * w)
    return delta, m, v


def _chip_sum_adamw(sums, recv, w, m, v, chip, after, name):
    rows, cols = w.shape
    tr, tc = _tile_2d(rows, cols)

    def body(chip_ref, s_ref, r_ref, w_ref, m_ref, v_ref, after_ref, g_out, d_out, m_out, v_out):
        del chip_ref, after_ref
        g = s_ref[...].astype(F32)
        for k in range(3):
            g = g + r_ref[k].astype(F32)
        g_out[...] = g
        d_out[...], m_out[...], v_out[...] = _adamw(w_ref[...], g, m_ref[...], v_ref[...])

    tile = pl.BlockSpec((tr, tc), lambda i, k, chip_ref: (i, k))
    grid_spec = pltpu.PrefetchScalarGridSpec(
        num_scalar_prefetch=1, grid=(rows // tr, cols // tc),
        in_specs=[pl.BlockSpec((None, tr, tc), lambda i, k, chip_ref: (chip_ref[0], i, k)),
                  pl.BlockSpec((3, tr, tc), lambda i, k, chip_ref: (0, i, k)), tile, tile, tile,
                  pl.BlockSpec(memory_space=pl.ANY)],
        out_specs=[tile] * 4)
    return pl.pallas_call(
        body, name=name, grid_spec=grid_spec, out_shape=[jax.ShapeDtypeStruct((rows, cols), F32)] * 4,
        compiler_params=_params("parallel", "parallel"),
    )(chip, *map(_in_hbm, (sums, recv, w, m, v)), after)


def _small_sum_adamw(me, entries, loss_parts):
    def whole(shape, squeeze=0, pick=False):
        blk = (None,) * squeeze + tuple(shape[squeeze:])
        if pick:
            blk = (shape[0], None) + tuple(shape[2:])
            return pl.BlockSpec(blk, lambda i, me_ref: (0, me_ref[0]) + (0,) * (len(shape) - 2))
        return pl.BlockSpec(blk, lambda i, me_ref: (0,) * len(shape))

    in_specs, out_specs, out_shape, args = [], [], [], []
    for parts, w, m, v, sharded in entries:
        lead = w.ndim - (parts.ndim - (2 if sharded else 1))
        in_specs += [whole(parts.shape, pick=sharded)] + [whole(w.shape, squeeze=lead)] * 3
        out_specs += [whole(w.shape, squeeze=lead)] * 4
        out_shape += [jax.ShapeDtypeStruct(w.shape, F32)] * 4
        args += [parts, w, m, v]
    in_specs.append(whole(loss_parts.shape))
    out_specs.append(whole(loss_parts.shape[1:]))
    out_shape.append(jax.ShapeDtypeStruct(loss_parts.shape[1:], F32))
    n = len(entries)

    def added(p_ref):
        total = p_ref[0]
        for d in range(1, N_DEV):
            total = total + p_ref[d]
        return total

    def body(me_ref, *refs):
        del me_ref
        ins, outs = refs[:4 * n + 1], refs[4 * n + 1:]
        for e in range(n):
            p_ref, w_ref, m_ref, v_ref = ins[4 * e:4 * e + 4]
            g_out, d_out, m_out, v_out = outs[4 * e:4 * e + 4]
            g = added(p_ref)
            g_out[...] = g
            d_out[...], m_out[...], v_out[...] = _adamw(w_ref[...], g, m_ref[...], v_ref[...])
        outs[4 * n][...] = added(ins[4 * n])

    grid_spec = pltpu.PrefetchScalarGridSpec(num_scalar_prefetch=1, grid=(1,), in_specs=in_specs, out_specs=out_specs)
    outs = pl.pallas_call(body, name="small_sum_adamw", grid_spec=grid_spec, out_shape=out_shape,
                          compiler_params=_params("arbitrary"))(me, *map(_in_hbm, args + [loss_parts]))
    return [outs[4 * e:4 * e + 4] for e in range(n)], outs[4 * n]


MM_TILE = 512
N_MM_TILES = SEQ // MM_TILE
CAT_TILE = 512
N_CAT_TILES = N_CAT // CAT_TILE
DZ_TILE = 640


def kernel(x, g_mix, w_in, b_gate, w_gk_up, b_gk, w_pool_grp, pool_scale, g_gla_head, w_pool_proj, w_gla_proj, w_out, g_ffn, w_up, w_conv, b_conv, w_down, g_final, loss_target, m_g_mix, m_w_in, m_b_gate, m_w_gk_up, m_b_gk, m_w_pool_grp, m_pool_scale, m_g_gla_head, m_w_pool_proj, m_w_gla_proj, m_w_out, m_g_ffn, m_w_up, m_w_conv, m_b_conv, m_w_down, m_g_final, v_g_mix, v_w_in, v_b_gate, v_w_gk_up, v_b_gk, v_w_pool_grp, v_pool_scale, v_g_gla_head, v_w_pool_proj, v_w_gla_proj, v_w_out, v_g_ffn, v_w_up, v_w_conv, v_b_conv, v_w_down, v_g_final):
    xi, yi, ci = lax.axis_index("x"), lax.axis_index("y"), lax.axis_index("c")
    me = 4 * xi + 2 * yi + ci
    core = jnp.reshape(ci, (1,)).astype(jnp.int32)
    chip = jnp.reshape(2 * xi + yi, (1,)).astype(jnp.int32)
    xs, target = x[0], loss_target[0]

    big = dict(w_in=w_in[0].T, w_pool_proj=w_pool_proj[0], w_gla_proj=w_gla_proj[0], w_out=w_out[0], w_up=w_up[0].T,
               w_down=w_down[0])
    moments = dict(w_in=(m_w_in[0].T, v_w_in[0].T), w_pool_proj=(m_w_pool_proj[0], v_w_pool_proj[0]),
                   w_gla_proj=(m_w_gla_proj[0], v_w_gla_proj[0]), w_out=(m_w_out[0], v_w_out[0]),
                   w_up=(m_w_up[0].T, v_w_up[0].T), w_down=(m_w_down[0], v_w_down[0]))
    names = list(big)
    shards = {k: big[k].astype(BF) for k in names}
    shards["w_gk_up"], shards["w_conv"] = w_gk_up[0], w_conv[0]
    gather_groups = (("w_in", "w_gk_up"), ("w_pool_proj", "w_gla_proj", "w_out"), ("w_up", "w_down", "w_conv"))
    started, token = _split_start("gather_start", [
        ([t for k in g for t in (shards[k], _gather_landing(shards[k], me))], 4 * len(g), _gather_first)
        for g in gather_groups])

    def gather_pass(gi, after):
        lands = list(_split_wait(f"gather_wait_{gi}", started[gi], _gather_first, after)[1::2])
        passed, tkn = _split_start(f"gather_pass_{gi}", [(lands, 3 * len(lands), _gather_second)])
        return passed[0], tkn

    def gather_done(gi, passed, after):
        return dict(zip(gather_groups[gi], _split_wait(f"gather_pass_wait_{gi}", passed, _gather_second, after)))

    tok = lambda i, j, k: (i, 0)
    whole = lambda i, j, k: (0, 0)
    kblk = lambda i, j, k: (k, 0)
    ff_seq = (None, None, SEQ, FF_BLK)

    h = _rms_fwd(xs, g_mix + token[:1, :1], "rms_mix")
    wg = gather_done(0, gather_pass(0, h)[0], h)
    wt_cat = _unshard_w_in(wg["w_in"])
    wgk_pad = jnp.pad(wg["w_gk_up"].transpose(1, 0, 2).reshape(GATE_RANK, GLA_DK), ((0, GK_PAD - GATE_RANK), (0, 0)))
    zcat = _mm(h, wt_cat, out_shape=(SEQ, N_CAT), out_dtype=BF, grid=(N_CAT_TILES, 1, 1),
               blk_a=(SEQ, D_MODEL), blk_b=(CAT_TILE, D_MODEL), blk_o=(SEQ, CAT_TILE),
               map_a=whole, map_b=lambda j, i, k: (j, 0), map_o=lambda j, i, k: (0, j), tb=True, name="mm_in")
    la = _gk_fwd(h, wt_cat, wgk_pad, b_gk)
    passed, tkn = gather_pass(1, la)
    o, states = _gla_fwd(zcat, la, tkn)
    wg = gather_done(1, passed, o)
    wpp = wg["w_pool_proj"].transpose(1, 0, 2).reshape(POOL_WIDTH, D_MODEL)
    wgp = wg["w_gla_proj"].reshape(D_MODEL, D_MODEL)
    wout = wg["w_out"].reshape(D_MODEL, D_MODEL)
    og = _post_gla_fwd(o, zcat, g_gla_head)
    ps = _pool_fwd(zcat, w_pool_grp[0], pool_scale)
    passed, tkn = gather_pass(2, (og, ps))
    y_pool, y_gla, mixed, x1, h2 = _mix_out_fwd(ps, og, zcat, xs, wpp, wgp, wout, b_gate, g_ffn, tkn)
    wg = gather_done(2, passed, h2)
    wt_up = wg["w_up"].reshape(2 * D_FF, D_MODEL)
    wdown = wg["w_down"].reshape(D_FF, D_MODEL)
    wconv4 = wg["w_conv"].reshape(2, 4, 3, FF_BLK)
    bconv4 = b_conv.reshape(2, 4, 1, FF_BLK)
    blk4 = lambda b, i, k: (b // 4, b % 4, 0, 0)
    u4, act = _up_conv_fwd(h2, wt_up, wconv4, bconv4)
    loss_part, dx2, dx2_bf, dg_final = _mm_tokens(
        act, wdown, blk_a=(None, 4, TOK_MM_TILE, FF_BLK), map_a=lambda i: (0, 0, i, 0),
        pieces=[(b, b * FF_BLK, FF_BLK) for b in range(4)], res=x1, then=("loss", g_final.reshape(1, D_MODEL), target),
        name="mm_down_loss")

    da = _mm(dx2_bf, wdown, out_shape=(1, 4, SEQ, FF_BLK), out_dtype=BF, grid=(4, 1, 1),
             blk_a=(SEQ, D_MODEL), blk_b=(FF_BLK, D_MODEL), blk_o=ff_seq,
             map_a=whole, map_b=lambda b, i, k: (b, 0), map_o=lambda b, i, k: (0, b, 0, 0), tb=True, name="mm_d_act")
    d_wdown = _mm(act, dx2_bf, out_shape=(D_FF, D_MODEL), out_dtype=BF, grid=(4, 1, 1),
                  blk_a=ff_seq, blk_b=(SEQ, D_MODEL), blk_o=(FF_BLK, D_MODEL),
                  map_a=lambda b, i, k: (0, b, 0, 0), map_b=whole, map_o=lambda b, i, k: (b, 0), ta=True,
                  name="mm_d_wdown")
    du4, d_wconv, d_bconv = _conv_bwd(u4, da, wconv4, bconv4)
    d_wt_up = _mm(du4, h2, out_shape=(2 * D_FF, D_MODEL), out_dtype=BF, grid=(N_DEV, 1, 1),
                  blk_a=ff_seq, blk_b=(SEQ, D_MODEL), blk_o=(FF_BLK, D_MODEL),
                  map_a=blk4, map_b=whole, map_o=lambda b, i, k: (b, 0), ta=True, name="mm_d_wup")
    res = {}

    def to_sibling(keys, parts):
        return [t for k in keys for t in (parts[k], lax.empty((4,) + parts[k].shape[2:], BF))], 4 * len(keys), _reduce_first

    def to_chips(keys, st, after):
        arrays = _split_wait("reduce_wait_" + keys[0], st, _reduce_first, after)
        sums = [_pair_sum(p, r, core, "pair_sum_" + k) for k, p, r in zip(keys, arrays[0::2], arrays[1::2])]
        return [t for s in sums for t in (s, lax.empty((3,) + s.shape[1:], BF))], 3 * len(keys), _reduce_second

    def reduce_start(keys, parts):
        st, tkn = _split_start("reduce_start_" + keys[0], [to_sibling(keys, parts)])
        return st[0], tkn

    def reduce_cross(keys, st, after):
        st2, tkn = _split_start("reduce_cross_" + keys[0], [to_chips(keys, st, after)])
        return st2[0], tkn

    def adamw(k, sums, recv, after):
        outs = _chip_sum_adamw(sums, recv, big[k], moments[k][0], moments[k][1], chip, after, "adamw_" + k)
        res[k] = [(t.T if k in ("w_in", "w_up") else t)[None] for t in outs]

    def reduce_done(keys, st2, after):
        arrays = _split_wait("reduce_cross_wait_" + keys[0], st2, _reduce_second, after)
        for k, s, r in zip(keys, arrays[0::2], arrays[1::2]):
            adamw(k, s, r, s)

    ffn_keys = ("w_down", "w_up")
    ffn_red, tkn = reduce_start(ffn_keys, dict(w_down=d_wdown.reshape(4, 2, D_FF // N_DEV, D_MODEL),
                                               w_up=d_wt_up.reshape(4, 2, FF_BLK, D_MODEL)))
    dx1, dg_ffn = _mm_tokens(
        du4, wt_up, blk_a=(2, 4, TOK_MM_TILE, FF_BLK), map_a=lambda i: (0, 0, i, 0),
        pieces=[((b // 4, b % 4), b * FF_BLK, FF_BLK) for b in range(N_DEV)], after=tkn, then=("rms_bwd", x1, g_ffn, dx2),
        name="mm_d_h2_rms")

    sq_t = dict(out_shape=(D_MODEL, D_MODEL), grid=(1, 1, N_MM_TILES), blk_a=(MM_TILE, D_MODEL),
                blk_b=(MM_TILE, D_MODEL), blk_o=(D_MODEL, D_MODEL), map_a=kblk, map_b=kblk, map_o=whole, ta=True)
    d_wout = _mm(mixed, dx1, out_dtype=BF, name="mm_d_wout", **sq_t)
    dzcat, dy_pool, dy_gla, db_gate = _mix_bwd(dx1, wout, zcat, b_gate, y_pool, y_gla)
    d_wgp = _mm(og, dy_gla, out_dtype=BF, name="mm_d_wgp", **sq_t)
    mix_keys = ("w_out", "w_gla_proj")
    (ffn_red, mix_red), tkn = _split_start("reduce_cross_w_down", [
        to_chips(ffn_keys, ffn_red, db_gate),
        to_sibling(mix_keys, dict(w_out=d_wout.reshape(4, 2, D_MODEL // N_DEV, D_MODEL),
                                  w_gla_proj=d_wgp.reshape(4, 2, D_MODEL // N_DEV, D_MODEL)))])
    dzcat, d_o, dg_head = _post_gla_bwd(dzcat, dy_gla, wgp, o, zcat, g_gla_head + tkn[:1, :1])
    dzcat, dla = _gla_bwd(dzcat, zcat, la, d_o, states)
    dzcat, d_wgk, db_gk = _gk_bwd(dzcat, dla, h, wt_cat, wgk_pad, b_gk)
    dps = _mm(dy_pool, wpp, out_shape=(SEQ, POOL_WIDTH), out_dtype=F32, grid=(N_MM_TILES, 1, 1),
              blk_a=(MM_TILE, D_MODEL), blk_b=(POOL_WIDTH, D_MODEL), blk_o=(MM_TILE, POOL_WIDTH),
              map_a=tok, map_b=whole, map_o=tok, tb=True, name="mm_d_ps")
    d_wpp = _mm(ps, dy_pool, out_shape=(POOL_WIDTH, D_MODEL), out_dtype=F32, grid=(1, 1, N_MM_TILES),
                blk_a=(MM_TILE, POOL_WIDTH), blk_b=(MM_TILE, D_MODEL), blk_o=(POOL_WIDTH, D_MODEL),
                map_a=kblk, map_b=kblk, map_o=whole, ta=True, name="mm_d_wpp")
    dzcat, d_wgrp, d_scale = _pool_bwd(dzcat, zcat, dps, w_pool_grp[0], pool_scale)
    row = lambda t: t.reshape(1, D_MODEL)
    conv_vec = lambda t: t.reshape(2, 4, 1, FF_BLK)
    small = [("b_gate", db_gate, b_gate, m_b_gate, v_b_gate, False),
             ("w_gk_up", d_wgk.reshape(GATE_RANK, N_DEV, GLA_DK // N_DEV).transpose(1, 0, 2), w_gk_up, m_w_gk_up,
              v_w_gk_up, True),
             ("b_gk", db_gk, b_gk, m_b_gk, v_b_gk, False),
             ("w_pool_grp", d_wgrp, w_pool_grp, m_w_pool_grp, v_w_pool_grp, False),
             ("pool_scale", d_scale, pool_scale, m_pool_scale, v_pool_scale, False),
             ("g_gla_head", dg_head, g_gla_head, m_g_gla_head, v_g_gla_head, False),
             ("g_ffn", dg_ffn, g_ffn, m_g_ffn, v_g_ffn, False),
             ("w_conv", d_wconv.reshape(N_DEV, 3, FF_BLK), w_conv, m_w_conv, v_w_conv, True),
             ("b_conv", d_bconv, conv_vec(b_conv), conv_vec(m_b_conv), conv_vec(v_b_conv), False),
             ("g_final", dg_final, row(g_final), row(m_g_final), row(v_g_final), False)]

    def to_all(parts):
        return [t for p in parts for t in (p, _gather_landing(p, me))], 7 * len(parts), _gather_direct

    (small_sent, mix_red), tkn = _split_start("small_start", [to_all([t[1] for t in small] + [loss_part]),
                                                              to_chips(mix_keys, mix_red, dla)])
    d_wt_cat = _mm(dzcat, h, out_shape=(N_DZ, D_MODEL), out_dtype=BF, grid=(N_DZ // DZ_TILE, 1, 1),
                   blk_a=(SEQ, DZ_TILE), blk_b=(SEQ, D_MODEL), blk_o=(DZ_TILE, D_MODEL),
                   map_a=lambda j, i, k: (0, j), map_b=whole, map_o=lambda j, i, k: (j, 0), ta=True, after=tkn,
                   name="mm_d_wcat")
    in_keys = ("w_in", "w_pool_proj")
    in_red, tkn = reduce_start(in_keys, dict(
        w_in=_shard_d_w_in(d_wt_cat).reshape(4, 2, IN_SHARD, D_MODEL),
        w_pool_proj=d_wpp.reshape(POOL_WIDTH, N_DEV, D_MODEL // N_DEV).transpose(1, 0, 2).astype(BF)
        .reshape(4, 2, POOL_WIDTH, D_MODEL // N_DEV)))
    ffn_arrays = _split_wait("reduce_cross_wait_w_down", ffn_red, _reduce_second, tkn)
    adamw("w_down", ffn_arrays[0], ffn_arrays[1], tkn)
    in_red, tkn = reduce_cross(in_keys, in_red, res["w_down"][0])
    grad_x, dg_mix = _mm_tokens(dzcat, wt_cat, blk_a=(TOK_MM_TILE, N_DZ), map_a=lambda i: (i, 0),
                                pieces=[(None, 0, N_DZ)], after=tkn, then=("rms_bwd", xs, g_mix, dx1),
                                name="mm_d_h_rms")
    (g_mix_sent,), tkn = _split_start("g_mix_start", [to_all([dg_mix])])
    adamw("w_up", ffn_arrays[2], ffn_arrays[3], tkn)
    reduce_done(mix_keys, mix_red, res["w_up"][0])
    gathered = _split_wait("small_wait", small_sent, _gather_direct, res["w_out"][0])[1::2]
    small.append(("g_mix", dg_mix, g_mix, m_g_mix, v_g_mix, False))
    gathered = list(gathered[:-1]) + [_split_wait("g_mix_wait", g_mix_sent, _gather_direct, gathered[0])[1], gathered[-1]]
    small_out, loss_sum = _small_sum_adamw(jnp.reshape(me, (1,)).astype(jnp.int32),
                                           [(p,) + t[2:] for p, t in zip(gathered, small)], gathered[-1])
    for t, outs in zip(small, small_out):
        res[t[0]] = list(outs)
    res["b_conv"] = [t.reshape(b_conv.shape) for t in res["b_conv"]]
    res["g_final"] = [t.reshape(g_final.shape) for t in res["g_final"]]

    reduce_done(in_keys, in_red, loss_sum)
    loss = loss_sum[0, 0]
    order =["g_mix", "w_in", "b_gate", "w_gk_up", "b_gk", "w_pool_grp", "pool_scale", "g_gla_head", "w_pool_proj",
             "w_gla_proj", "w_out", "g_ffn", "w_up", "w_conv", "b_conv", "w_down", "g_final"]
    return (loss, grad_x[None], *[res[k][0] for k in order], *[res[k][1] for k in order],
            *[res[k][2] for k in order], *[res[k][3] for k in order])
```

```python
import jax
import jax.numpy as jnp
from jax import lax
from jax.experimental import pallas as pl
from jax.experimental.pallas import tpu as pltpu

F32 = jnp.float32
BF = jnp.bfloat16
HIGHEST = lax.Precision.HIGHEST
MESH = pl.DeviceIdType.MESH

N_DEV = 8
SEQ = 2048
D_MODEL = 1024
CHUNK = 64
EPS = 1e-6
POOL_WIDTH = 512
POOL_WINDOWS = (2, 4, 8, 16)
POOL_GD = 128
POOL_HALO = 16
HEADS = 4
HK = 128
HV = 256
GLA_DK = 512
GATE_RANK = 16
GATE_NORM = 16.0
D_FF = 2816
FF_BLK = 704
IN_SHARD = 706
C_QKV, C_GATE, C_OG, C_POOL, C_GK = 0, 2048, 4096, 5120, 5632
N_CAT = 5632
GK_PAD = 128
N_DZ = N_CAT + GK_PAD
R_POOL, R_QKV, R_OG, R_GK, R_GATE = 0, 512, 2560, 3584, 3600

ADAM_LR, ADAM_B1, ADAM_B2, ADAM_EPS, ADAM_WD, ADAM_STEP = 0.001, 0.9, 0.999, 1e-08, 0.01, 10
ADAM_C1 = 1.0 - ADAM_B1 ** ADAM_STEP
ADAM_C2 = 1.0 - ADAM_B2 ** ADAM_STEP

VMEM_BYTES_V7X = 64 * 1024 * 1024
VMEM_LIMIT = VMEM_BYTES_V7X * 3 // 4

TOK_TILE = 256
HALO = 8
GLA_CPS = 4


def _params(*sem):
    return pltpu.CompilerParams(dimension_semantics=sem, vmem_limit_bytes=VMEM_LIMIT)


def _const_spec(shape):
    nd = len(shape)
    return pl.BlockSpec(shape, lambda *_: (0,) * nd)


def _in_hbm(t):
    return pltpu.with_memory_space_constraint(t, pltpu.HBM)


def _out_hbm(shape, dtype):
    return pltpu.HBM(shape, dtype)


def _dot(a, b, ta=False, tb=False):
    dims = (((0 if ta else 1,), (1 if tb else 0,)), ((), ()))
    return lax.dot_general(a.astype(BF), b.astype(BF), dims, preferred_element_type=F32)


def _dot_exact(a, b):
    return jnp.dot(a, b, precision=HIGHEST, preferred_element_type=F32)


def _sigmoid(x):
    return 0.5 * jnp.tanh(0.5 * x) + 0.5


def _mm(a, b, *, out_shape, out_dtype, grid, blk_a, blk_b, blk_o, map_a, map_b, map_o, ta=False, tb=False,
        after=None, name):
    gk = grid[2]
    n_in = 2 + (after is not None)

    def body(*refs):
        a_ref, b_ref, o_ref = refs[0], refs[1], refs[n_in]
        prod = _dot(a_ref[...], b_ref[...], ta, tb)
        if gk == 1:
            o_ref[...] = prod.astype(out_dtype)
        else:
            acc = refs[n_in + 1]
            k = pl.program_id(2)

            @pl.when(k == 0)
            def _():
                acc[...] = prod

            @pl.when(k > 0)
            def _():
                acc[...] += prod

            @pl.when(k == gk - 1)
            def _():
                o_ref[...] = acc[...].astype(out_dtype)

    in_specs = [pl.BlockSpec(blk_a, map_a), pl.BlockSpec(blk_b, map_b)]
    args = [_in_hbm(a), _in_hbm(b)]
    if after is not None:
        in_specs.append(pl.BlockSpec(memory_space=pl.ANY))
        args.append(after)
    return pl.pallas_call(
        body, name=name, grid=grid, in_specs=in_specs, out_specs=pl.BlockSpec(blk_o, map_o),
        out_shape=_out_hbm(out_shape, out_dtype),
        scratch_shapes=[] if gk == 1 else [pltpu.VMEM(tuple(d for d in blk_o if d is not None), F32)],
        compiler_params=_params("parallel", "parallel", "arbitrary"),
    )(*args)


TOK_MM_TILE = 256


def _mm_tokens(a, w, *, blk_a, map_a, pieces, res=None, after=None, then=None, name):
    n_in = 2 + (res is not None) + (after is not None) + (0 if then is None else len(then) - 1)

    def accumulate(ref, part):
        @pl.when(pl.program_id(0) == 0)
        def _():
            ref[...] = part

        @pl.when(pl.program_id(0) > 0)
        def _():
            ref[...] += part

    def body(*refs):
        a_ref, w_ref = refs[:2]
        extra, outs = refs[n_in - (0 if then is None else len(then) - 1):n_in], refs[n_in:]
        total = None
        for idx, row, n in pieces:
            av = a_ref[...] if idx is None else a_ref[idx]
            prod = _dot(av, w_ref[row:row + n, :])
            total = prod if total is None else total + prod
        if res is not None:
            total = total + refs[2][...]
        if then is None:
            outs[0][...] = total
        elif then[0] == "rms_bwd":
            dx, part = _rms_bwd_tile(total, extra[0][...], extra[1][...], extra[2][...])
            outs[0][...] = dx
            accumulate(outs[1], part)
        else:
            lpart, dx, part = _loss_tile(total, extra[0][...], extra[1][...])
            outs[1][...] = dx
            outs[2][...] = dx.astype(BF)
            accumulate(outs[0], lpart)
            accumulate(outs[3], part)

    tile = pl.BlockSpec((TOK_MM_TILE, D_MODEL), lambda i: (i, 0))
    vec = _const_spec((1, D_MODEL))
    big = _out_hbm((SEQ, D_MODEL), F32)
    small = _out_hbm((1, D_MODEL), F32)
    in_specs = [pl.BlockSpec(blk_a, map_a), pl.BlockSpec(w.shape, lambda i: (0, 0), pipeline_mode=pl.Buffered(1))]
    args = [a, w]
    if res is not None:
        in_specs.append(tile)
        args.append(res)
    if after is not None:
        in_specs.append(pl.BlockSpec(memory_space=pl.ANY))
        args.append(after)
    if then is None:
        out_specs, out_shape = tile, big
    elif then[0] == "rms_bwd":
        in_specs += [tile, vec, tile]
        out_specs, out_shape = [tile, vec], [big, small]
    else:
        in_specs += [vec, tile]
        out_specs = [_const_spec((1, 128)), tile, tile, vec]
        out_shape = [_out_hbm((1, 128), F32), big, _out_hbm((SEQ, D_MODEL), BF), small]
    if then is not None:
        args += list(then[1:])
    return pl.pallas_call(
        body, name=name, grid=(SEQ // TOK_MM_TILE,), in_specs=in_specs, out_specs=out_specs, out_shape=out_shape,
        compiler_params=_params("parallel" if then is None else "arbitrary"),
    )(*[_in_hbm(t) for t in args])


def _rms_fwd(x, g, name):
    def body(x_ref, g_ref, o_ref):
        xv = x_ref[...]
        r = lax.rsqrt(jnp.mean(xv * xv, axis=-1, keepdims=True) + EPS)
        o_ref[...] = (xv * r * g_ref[...]).astype(BF)

    tile = pl.BlockSpec((TOK_TILE, D_MODEL), lambda i: (i, 0))
    return pl.pallas_call(
        body, name=name, grid=(SEQ // TOK_TILE,), in_specs=[tile, _const_spec((1, D_MODEL))], out_specs=tile,
        out_shape=_out_hbm((SEQ, D_MODEL), BF), compiler_params=_params("parallel"),
    )(*map(_in_hbm, (x, g)))


def _rms_bwd_tile(dyv, xv, gv, dresv):
    r = lax.rsqrt(jnp.mean(xv * xv, axis=-1, keepdims=True) + EPS)
    xn = xv * r
    dxn = dyv * gv
    return dresv + r * (dxn - xn * jnp.mean(dxn * xn, axis=-1, keepdims=True)), jnp.sum(dyv * xn, axis=0, keepdims=True)


def _loss_tile(xv, gv, tv):
    r = lax.rsqrt(jnp.mean(xv * xv, axis=-1, keepdims=True) + EPS)
    xn = xv * r
    err = xn * gv - tv
    lpart = jnp.full((1, 128), 0.5 * jnp.sum(jnp.mean(err * err, axis=-1, keepdims=True)), F32)
    dyv = err * (1.0 / D_MODEL)
    dxn = dyv * gv
    return lpart, r * (dxn - xn * jnp.mean(dxn * xn, axis=-1, keepdims=True)), jnp.sum(dyv * xn, axis=0, keepdims=True)


def _pool_counts(w):
    pos = lax.broadcasted_iota(jnp.int32, (SEQ, 1), 0).astype(F32)
    return jnp.minimum(pos + 1.0, float(w))


def _pool_window(u, w, ext):
    ext[pl.ds(POOL_HALO, SEQ), :] = u
    win = u
    for j in range(1, w):
        win = win + ext[pl.ds(POOL_HALO - j, SEQ), :]
    return win / _pool_counts(w) - u


def _pool_fwd(zcat, w_grp, scale):
    def body(z_ref, w_ref, s_ref, o_ref, ext):
        ext[pl.ds(0, POOL_HALO), :] = jnp.zeros((POOL_HALO, POOL_GD), F32)
        for g, w in enumerate(POOL_WINDOWS):
            cols = slice(g * POOL_GD, (g + 1) * POOL_GD)
            p = _pool_window(z_ref[:, cols].astype(F32), w, ext)
            o_ref[:, cols] = (_dot(p, w_ref[g]) * s_ref[:, cols]).astype(BF)

    return pl.pallas_call(
        body, name="pool_fwd", grid=(1,),
        in_specs=[pl.BlockSpec((SEQ, POOL_WIDTH), lambda i: (0, C_POOL // POOL_WIDTH)),
                  _const_spec((4, POOL_GD, POOL_GD)), _const_spec((1, POOL_WIDTH))],
        out_specs=_const_spec((SEQ, POOL_WIDTH)), out_shape=_out_hbm((SEQ, POOL_WIDTH), BF),
        scratch_shapes=[pltpu.VMEM((POOL_HALO + SEQ, POOL_GD), F32)], compiler_params=_params("arbitrary"),
    )(*map(_in_hbm, (zcat, w_grp, scale)))


def _pool_bwd(dzcat, zcat, dps, w_grp, scale):
    def body(dz_in, z_ref, dps_ref, w_ref, s_ref, dz_ref, dw_ref, dsc_ref, ext, ext2):
        del dz_in
        ext[pl.ds(0, POOL_HALO), :] = jnp.zeros((POOL_HALO, POOL_GD), F32)
        ext2[pl.ds(SEQ, POOL_HALO), :] = jnp.zeros((POOL_HALO, POOL_GD), F32)
        for g, w in enumerate(POOL_WINDOWS):
            cols = slice(g * POOL_GD, (g + 1) * POOL_GD)
            p = _pool_window(z_ref[:, cols].astype(F32), w, ext)
            wg = w_ref[g]
            pg = _dot(p, wg)
            dpsv = dps_ref[:, cols]
            dsc_ref[:, cols] = jnp.sum(dpsv * pg, axis=0, keepdims=True)
            dpg = dpsv * s_ref[:, cols]
            dw_ref[g] = _dot(p, dpg, ta=True)
            dp = _dot(dpg, wg, tb=True)
            dpc = dp / _pool_counts(w)
            ext2[pl.ds(0, SEQ), :] = dpc
            du = dpc
            for j in range(1, w):
                du = du + ext2[pl.ds(j, SEQ), :]
            dz_ref[:, cols] = (du - dp).astype(BF)

    return pl.pallas_call(
        body, name="pool_bwd", grid=(1,),
        in_specs=[pl.BlockSpec(memory_space=pl.ANY),
                  pl.BlockSpec((SEQ, POOL_WIDTH), lambda i: (0, C_POOL // POOL_WIDTH)),
                  _const_spec((SEQ, POOL_WIDTH)), _const_spec((4, POOL_GD, POOL_GD)), _const_spec((1, POOL_WIDTH))],
        out_specs=[pl.BlockSpec((SEQ, POOL_WIDTH), lambda i: (0, C_POOL // POOL_WIDTH)),
                   _const_spec((4, POOL_GD, POOL_GD)), _const_spec((1, POOL_WIDTH))],
        out_shape=[_out_hbm((SEQ, N_DZ), BF), _out_hbm((4, POOL_GD, POOL_GD), F32),
                   _out_hbm((1, POOL_WIDTH), F32)],
        scratch_shapes=[pltpu.VMEM((POOL_HALO + SEQ, POOL_GD), F32), pltpu.VMEM((SEQ + POOL_HALO, POOL_GD), F32)],
        input_output_aliases={0: 0}, compiler_params=_params("arbitrary"),
    )(*map(_in_hbm, (dzcat, zcat, dps, w_grp, scale)))


GK_TILE = 512


GK_ROWS = pl.BlockSpec((GK_PAD, D_MODEL), lambda i: (C_GK // GK_PAD, 0))


def _gk_fwd(h, wt_cat, wgk_pad, b_gk):
    def body(h_ref, wt_ref, w_ref, b_ref, la_ref):
        z_gk = _dot(h_ref[...], wt_ref[...], tb=True)
        pre = _dot(z_gk, w_ref[...]) + b_ref[...]
        la_ref[...] = (jnp.minimum(pre, 0.0) - jnp.log(1.0 + jnp.exp(-jnp.abs(pre)))) * (1.0 / GATE_NORM)

    return pl.pallas_call(
        body, name="gk_fwd", grid=(SEQ // GK_TILE,),
        in_specs=[pl.BlockSpec((GK_TILE, D_MODEL), lambda i: (i, 0)), GK_ROWS,
                  _const_spec((GK_PAD, GLA_DK)), _const_spec((1, GLA_DK))],
        out_specs=pl.BlockSpec((GK_TILE, GLA_DK), lambda i: (i, 0)),
        out_shape=_out_hbm((SEQ, GLA_DK), F32), compiler_params=_params("parallel"),
    )(*map(_in_hbm, (h, wt_cat, wgk_pad, b_gk)))


def _gk_bwd(dzcat, dla, h, wt_cat, wgk_pad, b_gk):
    def body(dz_in, dla_ref, h_ref, wt_ref, w_ref, b_ref, dz_ref, dw_ref, db_ref):
        del dz_in
        wv = w_ref[...]
        z_gk = _dot(h_ref[...], wt_ref[...], tb=True)
        pre = _dot(z_gk, wv) + b_ref[...]
        dpre = dla_ref[...] * (1.0 / GATE_NORM) * (1.0 - _sigmoid(pre))
        dz_ref[...] = _dot(dpre, wv, tb=True).astype(BF)
        dwp = _dot(z_gk, dpre, ta=True)[:GATE_RANK]
        dbp = jnp.sum(dpre, axis=0, keepdims=True)

        @pl.when(pl.program_id(0) == 0)
        def _():
            dw_ref[...] = dwp
            db_ref[...] = dbp

        @pl.when(pl.program_id(0) > 0)
        def _():
            dw_ref[...] += dwp
            db_ref[...] += dbp

    return pl.pallas_call(
        body, name="gk_bwd", grid=(SEQ // GK_TILE,),
        in_specs=[pl.BlockSpec(memory_space=pl.ANY), pl.BlockSpec((GK_TILE, GLA_DK), lambda i: (i, 0)),
                  pl.BlockSpec((GK_TILE, D_MODEL), lambda i: (i, 0)), GK_ROWS, _const_spec((GK_PAD, GLA_DK)),
                  _const_spec((1, GLA_DK))],
        out_specs=[pl.BlockSpec((GK_TILE, GK_PAD), lambda i: (i, C_GK // GK_PAD)), _const_spec((GATE_RANK, GLA_DK)),
                   _const_spec((1, GLA_DK))],
        out_shape=[_out_hbm((SEQ, N_DZ), BF), _out_hbm((GATE_RANK, GLA_DK), F32),
                   _out_hbm((1, GLA_DK), F32)],
        input_output_aliases={0: 0}, compiler_params=_params("arbitrary"),
    )(*map(_in_hbm, (dzcat, dla, h, wt_cat, wgk_pad, b_gk)))


GLA_ROWS = GLA_CPS * CHUNK
GLA_STEPS = SEQ // GLA_ROWS
QKV_W = 2048


def _tri():
    return lax.broadcasted_iota(jnp.int32, (CHUNK, CHUNK), 0) >= lax.broadcasted_iota(jnp.int32, (CHUNK, CHUNK), 1)


def _chunk_cumsum(la_ref, rows):
    return _dot_exact(_tri().astype(F32), la_ref[rows, :])


def _gla_chunk(qkv_ref, la_ref, rows, h, bc_all):
    tri = _tri()
    q = qkv_ref[rows, h * HK:(h + 1) * HK].astype(F32) * (HK ** -0.5)
    k = qkv_ref[rows, GLA_DK + h * HK:GLA_DK + (h + 1) * HK].astype(F32)
    v = qkv_ref[rows, 2 * GLA_DK + h * HV:2 * GLA_DK + (h + 1) * HV].astype(BF)
    la = la_ref[rows, h * HK:(h + 1) * HK]
    bc = bc_all[:, h * HK:(h + 1) * HK]
    e_pos, e_neg = jnp.exp(bc), jnp.exp(-bc)
    dl = jnp.exp(jnp.sum(la, axis=0, keepdims=True))
    q_fw, q_bw, k_fw, k_bw = q * e_pos, q * e_neg, k * e_neg, k * e_pos
    scores = jnp.where(tri, _dot(q_fw, k_fw, tb=True), _dot(q_bw, k_bw, tb=True))
    return tri, v, e_pos, e_neg, dl, q_fw, q_bw, k_fw, k_bw, scores


def _gla_fwd(zcat, la, after):
    def body(qkv_ref, la_ref, after_ref, o_ref, st_ref, state):
        del after_ref

        @pl.when(pl.program_id(0) == 0)
        def _():
            state[...] = jnp.zeros_like(state)

        for c in range(GLA_CPS):
            rows = slice(c * CHUNK, (c + 1) * CHUNK)
            bc_all = _chunk_cumsum(la_ref, rows)
            for h in range(HEADS):
                _, v, _, _, dl, q_fw, _, k_fw, _, scores = _gla_chunk(qkv_ref, la_ref, rows, h, bc_all)
                st = state[h]
                st_ref[c, h] = st
                o_ref[rows, h * HV:(h + 1) * HV] = _dot(scores, v) + _dot(q_fw, st, tb=True)
                state[h] = st * dl + _dot(v, k_fw * dl, ta=True)

    return pl.pallas_call(
        body, name="gla_fwd", grid=(GLA_STEPS,),
        in_specs=[pl.BlockSpec((GLA_ROWS, QKV_W), lambda i: (i, 0)), pl.BlockSpec((GLA_ROWS, GLA_DK), lambda i: (i, 0)),
                  pl.BlockSpec(memory_space=pl.ANY)],
        out_specs=[pl.BlockSpec((GLA_ROWS, D_MODEL), lambda i: (i, 0)),
                   pl.BlockSpec((GLA_CPS, HEADS, HV, HK), lambda i: (i, 0, 0, 0))],
        out_shape=[_out_hbm((SEQ, D_MODEL), F32),
                   _out_hbm((SEQ // CHUNK, HEADS, HV, HK), F32)],
        scratch_shapes=[pltpu.VMEM((HEADS, HV, HK), F32)], compiler_params=_params("arbitrary"),
    )(*map(_in_hbm, (zcat, la)), after)


def _gla_bwd(dzcat, zcat, la, d_o, states):
    def body(dz_in, qkv_ref, la_ref, do_ref, st_ref, dqkv_ref, dla_ref, dstate):
        del dz_in

        @pl.when(pl.program_id(0) == 0)
        def _():
            dstate[...] = jnp.zeros_like(dstate)

        last_row = lax.broadcasted_iota(jnp.int32, (CHUNK, HK), 0) == CHUNK - 1
        upper = (lax.broadcasted_iota(jnp.int32, (CHUNK, CHUNK), 0)
                 <= lax.broadcasted_iota(jnp.int32, (CHUNK, CHUNK), 1)).astype(F32)
        for c in reversed(range(GLA_CPS)):
            rows = slice(c * CHUNK, (c + 1) * CHUNK)
            bc_all = _chunk_cumsum(la_ref, rows)
            dbs = []
            for h in range(HEADS):
                tri, v, e_pos, e_neg, dl, q_fw, q_bw, k_fw, k_bw, scores = _gla_chunk(qkv_ref, la_ref, rows, h, bc_all)
                st = st_ref[c, h]
                dst = dstate[h]
                d_out = do_ref[rows, h * HV:(h + 1) * HV].astype(BF)
                k_dec = k_fw * dl
                dp = _dot(d_out, v, tb=True)
                dp_fw = jnp.where(tri, dp, 0.0)
                dp_bw = jnp.where(tri, 0.0, dp)
                dv = _dot(scores, d_out, ta=True) + _dot(k_dec, dst, tb=True)
                dk_dec = _dot(v, dst)
                dq_fw = _dot(dp_fw, k_fw) + _dot(d_out, st)
                dk_fw = _dot(dp_fw, q_fw, ta=True) + dk_dec * dl
                dq_bw = _dot(dp_bw, k_bw)
                dk_bw = _dot(dp_bw, q_bw, ta=True)
                ddl = jnp.sum(st * dst, axis=0, keepdims=True) + jnp.sum(k_fw * dk_dec, axis=0, keepdims=True)
                dstate[h] = dst * dl + _dot(d_out, q_fw, ta=True)
                dq = (dq_fw * e_pos + dq_bw * e_neg) * (HK ** -0.5)
                dk = dk_fw * e_neg + dk_bw * e_pos
                dbs.append(dq_fw * q_fw - dk_fw * k_fw - dq_bw * q_bw + dk_bw * k_bw + jnp.where(last_row, ddl * dl, 0.0))
                dqkv_ref[rows, h * HK:(h + 1) * HK] = dq.astype(BF)
                dqkv_ref[rows, GLA_DK + h * HK:GLA_DK + (h + 1) * HK] = dk.astype(BF)
                dqkv_ref[rows, 2 * GLA_DK + h * HV:2 * GLA_DK + (h + 1) * HV] = dv.astype(BF)
            dla_ref[rows, :] = _dot_exact(upper, jnp.concatenate(dbs, axis=1))

    rev = lambda i: (GLA_STEPS - 1 - i, 0)
    return pl.pallas_call(
        body, name="gla_bwd", grid=(GLA_STEPS,),
        in_specs=[pl.BlockSpec(memory_space=pl.ANY), pl.BlockSpec((GLA_ROWS, QKV_W), rev),
                  pl.BlockSpec((GLA_ROWS, GLA_DK), rev), pl.BlockSpec((GLA_ROWS, D_MODEL), rev),
                  pl.BlockSpec((GLA_CPS, HEADS, HV, HK), lambda i: (GLA_STEPS - 1 - i, 0, 0, 0))],
        out_specs=[pl.BlockSpec((GLA_ROWS, QKV_W), rev), pl.BlockSpec((GLA_ROWS, GLA_DK), rev)],
        out_shape=[_out_hbm((SEQ, N_DZ), BF), _out_hbm((SEQ, GLA_DK), F32)],
        scratch_shapes=[pltpu.VMEM((HEADS, HV, HK), F32)], input_output_aliases={0: 0},
        compiler_params=_params("arbitrary"),
    )(*map(_in_hbm, (dzcat, zcat, la, d_o, states)))


def _silu_parts(x):
    s = _sigmoid(x)
    return x * s, s * (1.0 + x * (1.0 - s))


def _post_gla_fwd(o, zcat, g_head):
    def body(o_ref, zog_ref, g_ref, out_ref):
        for h in range(HEADS):
            cols = slice(h * HV, (h + 1) * HV)
            ov = o_ref[:, cols]
            r = lax.rsqrt(jnp.mean(ov * ov, axis=-1, keepdims=True) + EPS)
            act, _ = _silu_parts(zog_ref[:, cols].astype(F32))
            out_ref[:, cols] = (ov * r * g_ref[...] * act).astype(BF)

    tile = pl.BlockSpec((TOK_TILE, D_MODEL), lambda i: (i, 0))
    return pl.pallas_call(
        body, name="post_gla_fwd", grid=(SEQ // TOK_TILE,),
        in_specs=[tile, pl.BlockSpec((TOK_TILE, D_MODEL), lambda i: (i, C_OG // D_MODEL)), _const_spec((1, HV))],
        out_specs=tile, out_shape=_out_hbm((SEQ, D_MODEL), BF), compiler_params=_params("parallel"),
    )(*map(_in_hbm, (o, zcat, g_head)))


def _post_gla_bwd(dzcat, dy_gla, w_gla_proj, o, zcat, g_head):
    def body(dz_in, dyg_ref, w_ref, o_ref, zog_ref, g_ref, dz_ref, do_ref, dg_ref):
        del dz_in
        dog = _dot(dyg_ref[...], w_ref[...], tb=True)
        gpart = jnp.zeros((1, HV), F32)
        gv = g_ref[...]
        for h in range(HEADS):
            cols = slice(h * HV, (h + 1) * HV)
            ov = o_ref[:, cols]
            r = lax.rsqrt(jnp.mean(ov * ov, axis=-1, keepdims=True) + EPS)
            on = ov * r
            act, dact = _silu_parts(zog_ref[:, cols].astype(F32))
            dogv = dog[:, cols]
            dz_ref[:, cols] = (dogv * on * gv * dact).astype(BF)
            d_on_g = dogv * act
            gpart = gpart + jnp.sum(d_on_g * on, axis=0, keepdims=True)
            dxn = d_on_g * gv
            do_ref[:, cols] = (r * (dxn - on * jnp.mean(dxn * on, axis=-1, keepdims=True))).astype(BF)

        @pl.when(pl.program_id(0) == 0)
        def _():
            dg_ref[...] = gpart

        @pl.when(pl.program_id(0) > 0)
        def _():
            dg_ref[...] += gpart

    tile = pl.BlockSpec((TOK_TILE, D_MODEL), lambda i: (i, 0))
    ogspec = pl.BlockSpec((TOK_TILE, D_MODEL), lambda i: (i, C_OG // D_MODEL))
    return pl.pallas_call(
        body, name="post_gla_bwd", grid=(SEQ // TOK_TILE,),
        in_specs=[pl.BlockSpec(memory_space=pl.ANY), tile, _const_spec((D_MODEL, D_MODEL)), tile, ogspec,
                  _const_spec((1, HV))],
        out_specs=[ogspec, tile, _const_spec((1, HV))],
        out_shape=[_out_hbm((SEQ, N_DZ), BF), _out_hbm((SEQ, D_MODEL), BF),
                   _out_hbm((1, HV), F32)],
        input_output_aliases={0: 0}, compiler_params=_params("arbitrary"),
    )(*map(_in_hbm, (dzcat, dy_gla, w_gla_proj, o, zcat, g_head)))


GATE_W = 2 * D_MODEL


def _mix_out_fwd(ps, og, zcat, x, w_pool_proj, w_gla_proj, w_out, b_gate, g_ffn, after):
    def body(ps_ref, og_ref, zg_ref, x_ref, wpp_ref, wgp_ref, wout_ref, b_ref, g_ref, after_ref,
             yp_ref, yg_ref, mixed_ref, x1_ref, h2_ref):
        del after_ref
        y_pool = _dot(ps_ref[...], wpp_ref[...])
        y_gla = _dot(og_ref[...], wgp_ref[...])
        yp_ref[...] = y_pool.astype(BF)
        yg_ref[...] = y_gla.astype(BF)
        g0 = _sigmoid(zg_ref[:, :D_MODEL].astype(F32) + b_ref[:, :D_MODEL])
        g1 = _sigmoid(zg_ref[:, D_MODEL:].astype(F32) + b_ref[:, D_MODEL:])
        mixed = (g0 * y_pool + g1 * y_gla).astype(BF)
        mixed_ref[...] = mixed
        x1 = x_ref[...] + _dot(mixed, wout_ref[...])
        x1_ref[...] = x1
        r = lax.rsqrt(jnp.mean(x1 * x1, axis=-1, keepdims=True) + EPS)
        h2_ref[...] = (x1 * r * g_ref[...]).astype(BF)

    tile = pl.BlockSpec((TOK_TILE, D_MODEL), lambda i: (i, 0))
    resident = lambda shape: pl.BlockSpec(shape, lambda i: (0, 0), pipeline_mode=pl.Buffered(1))
    f32, bf16 = _out_hbm((SEQ, D_MODEL), F32), _out_hbm((SEQ, D_MODEL), BF)
    return pl.pallas_call(
        body, name="mix_out_fwd", grid=(SEQ // TOK_TILE,),
        in_specs=[pl.BlockSpec((TOK_TILE, POOL_WIDTH), lambda i: (i, 0)), tile,
                  pl.BlockSpec((TOK_TILE, GATE_W), lambda i: (i, C_GATE // GATE_W)), tile,
                  resident((POOL_WIDTH, D_MODEL)), resident((D_MODEL, D_MODEL)), resident((D_MODEL, D_MODEL)),
                  _const_spec((1, GATE_W)), _const_spec((1, D_MODEL)), pl.BlockSpec(memory_space=pl.ANY)],
        out_specs=[tile] * 5, out_shape=[bf16, bf16, bf16, f32, bf16], compiler_params=_params("parallel"),
    )(*map(_in_hbm, (ps, og, zcat, x, w_pool_proj, w_gla_proj, w_out, b_gate, g_ffn)), after)


def _mix_bwd(dx1, w_out, zcat, b_gate, y_pool, y_gla):
    def body(dx_ref, w_ref, zg_ref, b_ref, yp_ref, yg_ref, dz_ref, dyp_ref, dyg_ref, db_ref):
        dm = _dot(dx_ref[...], w_ref[...], tb=True)
        g0 = _sigmoid(zg_ref[:, :D_MODEL].astype(F32) + b_ref[:, :D_MODEL])
        g1 = _sigmoid(zg_ref[:, D_MODEL:].astype(F32) + b_ref[:, D_MODEL:])
        dyp_ref[...] = (dm * g0).astype(BF)
        dyg_ref[...] = (dm * g1).astype(BF)
        dz0 = dm * yp_ref[...].astype(F32) * g0 * (1.0 - g0)
        dz1 = dm * yg_ref[...].astype(F32) * g1 * (1.0 - g1)
        dz_ref[:, :D_MODEL] = dz0.astype(BF)
        dz_ref[:, D_MODEL:] = dz1.astype(BF)
        b0 = jnp.sum(dz0, axis=0, keepdims=True)
        b1 = jnp.sum(dz1, axis=0, keepdims=True)

        @pl.when(pl.program_id(0) == 0)
        def _():
            db_ref[:, :D_MODEL] = b0
            db_ref[:, D_MODEL:] = b1

        @pl.when(pl.program_id(0) > 0)
        def _():
            db_ref[:, :D_MODEL] += b0
            db_ref[:, D_MODEL:] += b1

    tile = pl.BlockSpec((TOK_TILE, D_MODEL), lambda i: (i, 0))
    gspec = pl.BlockSpec((TOK_TILE, GATE_W), lambda i: (i, C_GATE // GATE_W))
    return pl.pallas_call(
        body, name="mix_bwd", grid=(SEQ // TOK_TILE,),
        in_specs=[tile, _const_spec((D_MODEL, D_MODEL)), gspec, _const_spec((1, GATE_W)), tile, tile],
        out_specs=[gspec, tile, tile, _const_spec((1, GATE_W))],
        out_shape=[_out_hbm((SEQ, N_DZ), BF), _out_hbm((SEQ, D_MODEL), BF),
                   _out_hbm((SEQ, D_MODEL), BF), _out_hbm((1, GATE_W), F32)],
        compiler_params=_params("arbitrary"),
    )(*map(_in_hbm, (dx1, w_out, zcat, b_gate, y_pool, y_gla)))


N_TOK_TILES = SEQ // TOK_TILE
HALO_PER_TILE = TOK_TILE // HALO


LANE_TILES = tuple((lo, min(128, FF_BLK - lo)) for lo in range(0, FF_BLK, 128))


def _taps(w_ref, b_ref, half, lanes, rows):
    shape = (rows, lanes.stop - lanes.start)
    return ([jnp.broadcast_to(w_ref[half, j:j + 1, lanes], shape) for j in range(3)],
            jnp.broadcast_to(b_ref[half, :, lanes], shape))


def _conv_strips(u_ref, ub_ref, ua_ref, taps, lanes, width, n_strips, first):
    row = lax.broadcasted_iota(jnp.int32, (HALO, width), 0)
    prev = [[pltpu.roll(jnp.where(first, 0.0, ub_ref[half, :, lanes]), k, 0) for k in (1, 2)] for half in range(2)]
    for s in range(n_strips + (ua_ref is not None)):
        u3, conv = [], []
        for half in range(2):
            cur = u_ref[half, s * HALO:(s + 1) * HALO, lanes] if s < n_strips else ua_ref[half, :, lanes]
            rolled = [pltpu.roll(cur, k, 0) for k in (1, 2)]
            frames = [jnp.where(row >= 2, rolled[1], prev[half][1]), jnp.where(row >= 1, rolled[0], prev[half][0]), cur]
            prev[half] = rolled
            w3, bias = taps[half]
            u3.append(frames)
            conv.append(bias + frames[0] * w3[0] + frames[1] * w3[1] + frames[2] * w3[2])
        yield s, u3, conv


def _pair_specs(pairs):
    tile = pl.BlockSpec((pairs, None, TOK_TILE, FF_BLK), lambda b, i: (0, b, i, 0))
    before = pl.BlockSpec((pairs, None, HALO, FF_BLK), lambda b, i: (0, b, jnp.maximum(i * HALO_PER_TILE - 1, 0), 0))
    after = pl.BlockSpec((pairs, None, HALO, FF_BLK),
                         lambda b, i: (0, b, jnp.minimum((i + 1) * HALO_PER_TILE, SEQ // HALO - 1), 0))

    def vec(rows):
        return pl.BlockSpec((2, None, rows, FF_BLK), lambda b, i: (0, b, 0, 0))

    return tile, before, after, vec


N_STRIPS = TOK_TILE // HALO


def _up_conv_fwd(h2, wt_up, w_conv, b_conv):
    steps = N_TOK_TILES // 2

    def body(h_ref, h_next, wg_ref, wv_ref, w_ref, b_ref, u_ref, a_ref, buf_a, buf_b, carry):
        j = pl.program_id(1)

        def project(hv, buf):
            buf[0] = _dot(hv, wg_ref[...], tb=True)
            buf[1] = _dot(hv, wv_ref[...], tb=True)

        def conv(buf, row0):
            u_ref[:, row0:row0 + TOK_TILE, :] = buf[...]
            for lo, width in LANE_TILES:
                lanes = slice(lo, lo + width)
                taps = [_taps(w_ref, b_ref, half, lanes, HALO) for half in range(2)]
                pending = None
                for s, _, (cg, cv) in _conv_strips(buf, carry, None, taps, lanes, width, N_STRIPS, False):
                    act = cg * _sigmoid(cg) * cv
                    if s % 2 == 0:
                        pending = act
                    else:
                        a_ref[0, row0 + (s - 1) * HALO:row0 + (s + 1) * HALO, lanes] = (
                            jnp.concatenate([pending, act], axis=0).astype(BF))
            carry[...] = buf[:, TOK_TILE - HALO:, :]

        @pl.when(j == 0)
        def _():
            project(h_ref[0:TOK_TILE, :], buf_a)
            carry[...] = jnp.zeros_like(carry)

        project(h_ref[TOK_TILE:, :], buf_b)
        conv(buf_a, 0)
        project(h_next[...], buf_a)
        conv(buf_b, TOK_TILE)

    w_blk = lambda half: pl.BlockSpec((FF_BLK, D_MODEL), lambda b, j: (b + 4 * half, 0))
    vec = lambda rows: pl.BlockSpec((2, None, rows, FF_BLK), lambda b, j: (0, b, 0, 0))
    u_buf = pltpu.VMEM((2, TOK_TILE, FF_BLK), F32)
    return pl.pallas_call(
        body, name="up_conv_fwd", grid=(4, steps),
        in_specs=[pl.BlockSpec((2 * TOK_TILE, D_MODEL), lambda b, j: (j, 0)),
                  pl.BlockSpec((TOK_TILE, D_MODEL), lambda b, j: (jnp.minimum(2 * j + 2, N_TOK_TILES - 1), 0)),
                  w_blk(0), w_blk(1), vec(3), vec(1)],
        out_specs=[pl.BlockSpec((2, None, 2 * TOK_TILE, FF_BLK), lambda b, j: (0, b, j, 0)),
                   pl.BlockSpec((1, None, 2 * TOK_TILE, FF_BLK), lambda b, j: (0, b, j, 0))],
        out_shape=[_out_hbm((2, 4, SEQ, FF_BLK), F32), _out_hbm((1, 4, SEQ, FF_BLK), BF)],
        scratch_shapes=[u_buf, u_buf, pltpu.VMEM((2, HALO, FF_BLK), F32)],
        compiler_params=_params("parallel", "arbitrary"),
    )(*map(_in_hbm, (h2, h2, wt_up, wt_up, w_conv, b_conv)))


def _conv_bwd(u, da, w_conv, b_conv):
    def body(u_ref, ub_ref, ua_ref, da_ref, daa_ref, w_ref, b_ref, du_ref, dw_ref, db_ref):
        i = pl.program_id(1)

        @pl.when(i == 0)
        def _():
            dw_ref[...] = jnp.zeros_like(dw_ref)
            db_ref[...] = jnp.zeros_like(db_ref)

        for lo, width in LANE_TILES:
            lanes = slice(lo, lo + width)
            row = lax.broadcasted_iota(jnp.int32, (HALO, width), 0)
            taps = [_taps(w_ref, b_ref, half, lanes, HALO) for half in range(2)]
            acc_w = [[jnp.zeros((HALO, width), F32) for _ in range(3)] for _ in range(2)]
            acc_b = [jnp.zeros((HALO, width), F32) for _ in range(2)]
            da_pair, pending = None, [None, None]
            dc_prev, up_prev = [None, None], [None, None]
            for s, u3, (cg, cv) in _conv_strips(u_ref, ub_ref, ua_ref, taps, lanes, width, N_STRIPS, i == 0):
                act, dact = _silu_parts(cg)
                if s == N_STRIPS:
                    da = jnp.where(i < N_TOK_TILES - 1, daa_ref[0, :, lanes].astype(F32), 0.0)
                elif s % 2 == 0:
                    da_pair = da_ref[0, s * HALO:(s + 2) * HALO, lanes].astype(F32)
                    da = da_pair[:HALO]
                else:
                    da = da_pair[HALO:]
                dc = (da * cv * dact, da * act)
                for half in range(2):
                    up = [pltpu.roll(dc[half], HALO - k, 0) for k in (1, 2)]
                    if s < N_STRIPS:
                        for j in range(3):
                            acc_w[half][j] = acc_w[half][j] + dc[half] * u3[half][j]
                        acc_b[half] = acc_b[half] + dc[half]
                    if s >= 1:
                        w3 = taps[half][0]
                        du = (dc_prev[half] * w3[2] + jnp.where(row < HALO - 1, up_prev[half][0], up[0]) * w3[1]
                              + jnp.where(row < HALO - 2, up_prev[half][1], up[1]) * w3[0])
                        if (s - 1) % 2 == 0:
                            pending[half] = du
                        else:
                            du_ref[half, (s - 2) * HALO:s * HALO, lanes] = jnp.concatenate([pending[half], du],
                                                                                           axis=0).astype(BF)
                    dc_prev[half], up_prev[half] = dc[half], up
            for half in range(2):
                for j in range(3):
                    dw_ref[half, j:j + 1, lanes] += jnp.sum(acc_w[half][j], axis=0, keepdims=True)
                db_ref[half, :, lanes] += jnp.sum(acc_b[half], axis=0, keepdims=True)

    tile, before, after, vec = _pair_specs(2)
    da_tile, _, da_after_spec, _ = _pair_specs(1)
    return pl.pallas_call(
        body, name="conv_bwd", grid=(4, N_TOK_TILES),
        in_specs=[tile, before, after, da_tile, da_after_spec, vec(3), vec(1)],
        out_specs=[tile, vec(3), vec(1)],
        out_shape=[_out_hbm((2, 4, SEQ, FF_BLK), BF), _out_hbm((2, 4, 3, FF_BLK), F32),
                   _out_hbm((2, 4, 1, FF_BLK), F32)],
        compiler_params=_params("parallel", "arbitrary"),
    )(*map(_in_hbm, (u, u, u, da, da, w_conv, b_conv)))


W_IN_SEGMENTS = ((R_POOL, POOL_WIDTH, C_POOL), (R_QKV, QKV_W, C_QKV), (R_OG, D_MODEL, C_OG), (R_GK, GATE_RANK, C_GK),
                 (R_GATE, GATE_W, C_GATE))


def _slab_pieces(d):
    lo, hi = d * IN_SHARD, (d + 1) * IN_SHARD
    pieces = []
    for start, n, at in W_IN_SEGMENTS:
        a, b = max(lo, start), min(hi, start + n)
        if a < b:
            assert (a - lo) % 2 == 0 and (b - a) % 2 == 0 and (at + a - start) % 2 == 0
            pieces.append(((a - lo) // 2, (b - a) // 2, (at + a - start) // 2))
    return pieces


def _unshard_w_in(slabs):
    def body(slab_ref, cat_ref):
        d = pl.program_id(0)
        src = slab_ref.bitcast(jnp.uint32)
        dst = cat_ref.bitcast(jnp.uint32)

        @pl.when(d == 0)
        def _():
            cat_ref[C_GK:, :] = jnp.zeros((GK_PAD, D_MODEL), BF)

        for dd in range(N_DEV):
            @pl.when(d == dd)
            def _():
                for a, n, at in _slab_pieces(dd):
                    dst[pl.ds(at, n), :] = src[0, pl.ds(a, n), :]

    return pl.pallas_call(
        body, name="unshard_w_in", grid=(N_DEV,),
        in_specs=[pl.BlockSpec((1, IN_SHARD, D_MODEL), lambda d: (d, 0, 0))], out_specs=_const_spec((N_DZ, D_MODEL)),
        out_shape=_out_hbm((N_DZ, D_MODEL), BF), compiler_params=_params("arbitrary"),
    )(_in_hbm(slabs))


def _shard_d_w_in(d_cat):
    def body(cat_ref, slab_ref):
        d = pl.program_id(0)
        cat = cat_ref.bitcast(jnp.uint32)
        dst = slab_ref.bitcast(jnp.uint32)
        for dd in range(N_DEV):
            @pl.when(d == dd)
            def _():
                for a, n, at in _slab_pieces(dd):
                    dst[0, pl.ds(a, n), :] = cat[pl.ds(at, n), :]

    return pl.pallas_call(
        body, name="shard_d_w_in", grid=(N_DEV,), in_specs=[_const_spec((N_DZ, D_MODEL))],
        out_specs=pl.BlockSpec((1, IN_SHARD, D_MODEL), lambda d: (d, 0, 0)),
        out_shape=_out_hbm((N_DEV, IN_SHARD, D_MODEL), BF), compiler_params=_params("parallel"),
    )(_in_hbm(d_cat))


ANY = pl.BlockSpec(memory_space=pl.ANY)


def _place():
    x, y, c = lax.axis_index("x"), lax.axis_index("y"), lax.axis_index("c")
    other_chips = [(1 - x, y), (x, 1 - y), (1 - x, 1 - y)]
    return x, y, c, other_chips


SEM = pl.BlockSpec(memory_space=pltpu.SEMAPHORE)
IN_HBM = pl.BlockSpec(memory_space=pltpu.HBM)
SPLIT_PARAMS = pltpu.CompilerParams(has_side_effects=pltpu.SideEffectType.DATAFLOW_SIDE_EFFECTING)


def _gather_first(refs, send_sems, recv_sems):
    x, y, c, chips = _place()
    targets = [(x, y, 1 - c)] + [(px, py, c) for px, py in chips]
    return [pltpu.make_async_remote_copy(src_ref=refs[2 * a], dst_ref=refs[2 * a + 1].at[4 * x + 2 * y + c],
                                         send_sem=send_sems.at[4 * a + k], recv_sem=recv_sems.at[4 * a + k],
                                         device_id=to, device_id_type=MESH)
            for a in range(len(refs) // 2) for k, to in enumerate(targets)]


def _gather_direct(refs, send_sems, recv_sems):
    x, y, c, _ = _place()
    flips = [(dx, dy, dc) for dx in (0, 1) for dy in (0, 1) for dc in (0, 1) if dx + dy + dc]
    targets = [(1 - x if dx else x, 1 - y if dy else y, 1 - c if dc else c) for dx, dy, dc in flips]
    return [pltpu.make_async_remote_copy(src_ref=refs[2 * a], dst_ref=refs[2 * a + 1].at[4 * x + 2 * y + c],
                                         send_sem=send_sems.at[7 * a + k], recv_sem=recv_sems.at[7 * a + k],
                                         device_id=to, device_id_type=MESH)
            for a in range(len(refs) // 2) for k, to in enumerate(targets)]


def _gather_second(refs, send_sems, recv_sems):
    x, y, c, chips = _place()
    copies = []
    for a, land in enumerate(refs):
        for j, (px, py) in enumerate(chips):
            block = land.at[4 * px + 2 * py + c]
            copies.append(pltpu.make_async_remote_copy(src_ref=block, dst_ref=block, send_sem=send_sems.at[3 * a + j],
                                                       recv_sem=recv_sems.at[3 * a + j], device_id=(x, y, 1 - c),
                                                       device_id_type=MESH))
    return copies


def _reduce_first(refs, send_sems, recv_sems):
    x, y, c, _ = _place()
    return [pltpu.make_async_remote_copy(src_ref=refs[2 * a].at[j, 1 - c], dst_ref=refs[2 * a + 1].at[j],
                                         send_sem=send_sems.at[4 * a + j], recv_sem=recv_sems.at[4 * a + j],
                                         device_id=(x, y, 1 - c), device_id_type=MESH)
            for a in range(len(refs) // 2) for j in range(4)]


def _reduce_second(refs, send_sems, recv_sems):
    _, _, c, chips = _place()
    return [pltpu.make_async_remote_copy(src_ref=refs[2 * a].at[2 * px + py], dst_ref=refs[2 * a + 1].at[k],
                                         send_sem=send_sems.at[3 * a + k], recv_sem=recv_sems.at[3 * a + k],
                                         device_id=(px, py, c), device_id_type=MESH)
            for a in range(len(refs) // 2) for k, (px, py) in enumerate(chips)]


def _split_start(name, groups):
    arrays = [a for g in groups for a in g[0]]
    n = len(arrays)

    def body(*refs):
        sems = refs[n:n + 2 * len(groups)]
        at = 0
        for gi, (members, _, build) in enumerate(groups):
            for cp in build(refs[at:at + len(members)], sems[2 * gi], sems[2 * gi + 1]):
                cp.start()
            at += len(members)
        refs[-1][...] = jnp.zeros_like(refs[-1])

    sem_shapes = [pltpu.SemaphoreType.DMA((g[1],)) for g in groups for _ in range(2)]
    outs = pl.pallas_call(
        body, name=name, in_specs=[IN_HBM] * n,
        out_shape=(*sem_shapes, *[_out_hbm(a.shape, a.dtype) for a in arrays], jax.ShapeDtypeStruct((8, 128), F32)),
        out_specs=(*[SEM] * len(sem_shapes), *[IN_HBM] * n, pl.BlockSpec(memory_space=pltpu.VMEM)),
        input_output_aliases={i: len(sem_shapes) + i for i in range(n)}, compiler_params=SPLIT_PARAMS,
    )(*[pltpu.with_memory_space_constraint(a, pltpu.HBM) for a in arrays])
    per_group, at = [], len(sem_shapes)
    for gi, (members, _, _) in enumerate(groups):
        per_group.append((outs[2 * gi], outs[2 * gi + 1], list(outs[at:at + len(members)])))
        at += len(members)
    return per_group, outs[-1]


def _split_wait(name, started, build, after):
    send_sems, recv_sems, arrays = started
    n = len(arrays)
    after = after if isinstance(after, (tuple, list)) else (after,)

    def body(*refs):
        for cp in build(refs[:n], refs[n], refs[n + 1]):
            cp.wait_send()
            cp.wait_recv()

    return pl.pallas_call(
        body, name=name, in_specs=[IN_HBM] * n + [SEM, SEM] + [ANY] * len(after),
        out_shape=tuple(_out_hbm(a.shape, a.dtype) for a in arrays), out_specs=tuple([IN_HBM] * n),
        input_output_aliases={i: i for i in range(n)}, compiler_params=SPLIT_PARAMS,
    )(*arrays, send_sems, recv_sems, *after)


def _gather_landing(shard, me):
    return lax.dynamic_update_slice(lax.empty((N_DEV,) + shard.shape, shard.dtype), shard[None],
                                    (me,) + (0,) * shard.ndim)


ADAM_LANE_TILE = 256


def _tile_2d(rows, cols):
    for t in (256, 176, 128):
        if rows % t == 0:
            return t, cols
    return rows, ADAM_LANE_TILE


def _pair_sum(part, recv, core, name):
    _, rows, cols = recv.shape
    tr, tc = rows, cols

    def body(c_ref, p_ref, r_ref, o_ref):
        del c_ref
        o_ref[...] = (p_ref[...].astype(F32) + r_ref[...].astype(F32)).astype(BF)

    grid_spec = pltpu.PrefetchScalarGridSpec(
        num_scalar_prefetch=1, grid=(4, rows // tr, cols // tc),
        in_specs=[pl.BlockSpec((None, None, tr, tc), lambda j, i, k, c_ref: (j, c_ref[0], i, k)),
                  pl.BlockSpec((None, tr, tc), lambda j, i, k, c_ref: (j, i, k))],
        out_specs=pl.BlockSpec((None, tr, tc), lambda j, i, k, c_ref: (j, i, k)))
    return pl.pallas_call(
        body, name=name, grid_spec=grid_spec, out_shape=_out_hbm(recv.shape, BF),
        compiler_params=_params("parallel", "parallel", "parallel"),
    )(core, *map(_in_hbm, (part, recv)))


def _adamw(w, g, m, v):
    m = ADAM_B1 * m + (1.0 - ADAM_B1) * g
    v = ADAM_B2 * v + (1.0 - ADAM_B2) * (g * g)
    delta = -ADAM_LR * ((m / ADAM_C1) / (jnp.sqrt(v / ADAM_C2) + ADAM_EPS) + ADAM_WD * w)
    return delta, m, v


def _chip_sum_adamw(sums, recv, w, m, v, chip, name):
    rows, cols = w.shape
    tr, tc = _tile_2d(rows, cols)

    def body(chip_ref, s_ref, r_ref, w_ref, m_ref, v_ref, g_out, d_out, m_out, v_out):
        del chip_ref
        g = s_ref[...].astype(F32)
        for k in range(3):
            g = g + r_ref[k].astype(F32)
        g_out[...] = g
        d_out[...], m_out[...], v_out[...] = _adamw(w_ref[...], g, m_ref[...], v_ref[...])

    tile = pl.BlockSpec((tr, tc), lambda i, k, chip_ref: (i, k))
    grid_spec = pltpu.PrefetchScalarGridSpec(
        num_scalar_prefetch=1, grid=(rows // tr, cols // tc),
        in_specs=[pl.BlockSpec((None, tr, tc), lambda i, k, chip_ref: (chip_ref[0], i, k)),
                  pl.BlockSpec((3, tr, tc), lambda i, k, chip_ref: (0, i, k)), tile, tile, tile],
        out_specs=[tile] * 4)
    return pl.pallas_call(
        body, name=name, grid_spec=grid_spec, out_shape=[_out_hbm((rows, cols), F32)] * 4,
        compiler_params=_params("parallel", "parallel"),
    )(chip, *map(_in_hbm, (sums, recv, w, m, v)))


def _small_sum_adamw(me, entries, loss_parts):
    def whole(shape, squeeze=0, pick=False):
        blk = (None,) * squeeze + tuple(shape[squeeze:])
        if pick:
            blk = (shape[0], None) + tuple(shape[2:])
            return pl.BlockSpec(blk, lambda i, me_ref: (0, me_ref[0]) + (0,) * (len(shape) - 2))
        return pl.BlockSpec(blk, lambda i, me_ref: (0,) * len(shape))

    in_specs, out_specs, out_shape, args = [], [], [], []
    for parts, w, m, v, sharded in entries:
        lead = w.ndim - (parts.ndim - (2 if sharded else 1))
        in_specs += [whole(parts.shape, pick=sharded)] + [whole(w.shape, squeeze=lead)] * 3
        out_specs += [whole(w.shape, squeeze=lead)] * 4
        out_shape += [_out_hbm(w.shape, F32)] * 4
        args += [parts, w, m, v]
    in_specs.append(whole(loss_parts.shape))
    out_specs.append(whole(loss_parts.shape[1:]))
    out_shape.append(_out_hbm(loss_parts.shape[1:], F32))
    n = len(entries)

    def added(p_ref):
        total = p_ref[0]
        for d in range(1, N_DEV):
            total = total + p_ref[d]
        return total

    def body(me_ref, *refs):
        del me_ref
        ins, outs = refs[:4 * n + 1], refs[4 * n + 1:]
        for e in range(n):
            p_ref, w_ref, m_ref, v_ref = ins[4 * e:4 * e + 4]
            g_out, d_out, m_out, v_out = outs[4 * e:4 * e + 4]
            g = added(p_ref)
            g_out[...] = g
            d_out[...], m_out[...], v_out[...] = _adamw(w_ref[...], g, m_ref[...], v_ref[...])
        outs[4 * n][...] = added(ins[4 * n])

    grid_spec = pltpu.PrefetchScalarGridSpec(num_scalar_prefetch=1, grid=(1,), in_specs=in_specs, out_specs=out_specs)
    outs = pl.pallas_call(body, name="small_sum_adamw", grid_spec=grid_spec, out_shape=out_shape,
                          compiler_params=_params("arbitrary"))(me, *map(_in_hbm, args + [loss_parts]))
    return [outs[4 * e:4 * e + 4] for e in range(n)], outs[4 * n]


MM_TILE = 512
N_MM_TILES = SEQ // MM_TILE
CAT_TILE = 512
N_CAT_TILES = N_CAT // CAT_TILE
DZ_TILE = 640


def kernel(x, g_mix, w_in, b_gate, w_gk_up, b_gk, w_pool_grp, pool_scale, g_gla_head, w_pool_proj, w_gla_proj, w_out, g_ffn, w_up, w_conv, b_conv, w_down, g_final, loss_target, m_g_mix, m_w_in, m_b_gate, m_w_gk_up, m_b_gk, m_w_pool_grp, m_pool_scale, m_g_gla_head, m_w_pool_proj, m_w_gla_proj, m_w_out, m_g_ffn, m_w_up, m_w_conv, m_b_conv, m_w_down, m_g_final, v_g_mix, v_w_in, v_b_gate, v_w_gk_up, v_b_gk, v_w_pool_grp, v_pool_scale, v_g_gla_head, v_w_pool_proj, v_w_gla_proj, v_w_out, v_g_ffn, v_w_up, v_w_conv, v_b_conv, v_w_down, v_g_final):
    xi, yi, ci = lax.axis_index("x"), lax.axis_index("y"), lax.axis_index("c")
    me = 4 * xi + 2 * yi + ci
    core = jnp.reshape(ci, (1,)).astype(jnp.int32)
    chip = jnp.reshape(2 * xi + yi, (1,)).astype(jnp.int32)
    xs, target = x[0], loss_target[0]

    big = dict(w_in=w_in[0].T, w_pool_proj=w_pool_proj[0], w_gla_proj=w_gla_proj[0], w_out=w_out[0], w_up=w_up[0].T,
               w_down=w_down[0])
    moments = dict(w_in=(m_w_in[0].T, v_w_in[0].T), w_pool_proj=(m_w_pool_proj[0], v_w_pool_proj[0]),
                   w_gla_proj=(m_w_gla_proj[0], v_w_gla_proj[0]), w_out=(m_w_out[0], v_w_out[0]),
                   w_up=(m_w_up[0].T, v_w_up[0].T), w_down=(m_w_down[0], v_w_down[0]))
    names = list(big)
    shards = {k: big[k].astype(BF) for k in names}
    shards["w_gk_up"], shards["w_conv"] = w_gk_up[0], w_conv[0]
    gather_groups = (("w_in", "w_gk_up"), ("w_pool_proj", "w_gla_proj", "w_out"), ("w_up", "w_down", "w_conv"))
    started, token = _split_start("gather_start", [
        ([t for k in g for t in (shards[k], _gather_landing(shards[k], me))], 4 * len(g), _gather_first)
        for g in gather_groups])

    def gather_pass(gi, after):
        lands = list(_split_wait(f"gather_wait_{gi}", started[gi], _gather_first, after)[1::2])
        passed, tkn = _split_start(f"gather_pass_{gi}", [(lands, 3 * len(lands), _gather_second)])
        return passed[0], tkn

    def gather_done(gi, passed, after):
        return dict(zip(gather_groups[gi], _split_wait(f"gather_pass_wait_{gi}", passed, _gather_second, after)))

    tok = lambda i, j, k: (i, 0)
    whole = lambda i, j, k: (0, 0)
    kblk = lambda i, j, k: (k, 0)
    ff_seq = (None, None, SEQ, FF_BLK)

    h = _rms_fwd(xs, g_mix + token[:1, :1], "rms_mix")
    wg = gather_done(0, gather_pass(0, h)[0], h)
    wt_cat = _unshard_w_in(wg["w_in"])
    wgk_pad = jnp.pad(wg["w_gk_up"].transpose(1, 0, 2).reshape(GATE_RANK, GLA_DK), ((0, GK_PAD - GATE_RANK), (0, 0)))
    zcat = _mm(h, wt_cat, out_shape=(SEQ, N_CAT), out_dtype=BF, grid=(N_CAT_TILES, 1, 1),
               blk_a=(SEQ, D_MODEL), blk_b=(CAT_TILE, D_MODEL), blk_o=(SEQ, CAT_TILE),
               map_a=whole, map_b=lambda j, i, k: (j, 0), map_o=lambda j, i, k: (0, j), tb=True, name="mm_in")
    la = _gk_fwd(h, wt_cat, wgk_pad, b_gk)
    passed, tkn = gather_pass(1, la)
    o, states = _gla_fwd(zcat, la, tkn)
    wg = gather_done(1, passed, o)
    wpp = wg["w_pool_proj"].transpose(1, 0, 2).reshape(POOL_WIDTH, D_MODEL)
    wgp = wg["w_gla_proj"].reshape(D_MODEL, D_MODEL)
    wout = wg["w_out"].reshape(D_MODEL, D_MODEL)
    og = _post_gla_fwd(o, zcat, g_gla_head)
    ps = _pool_fwd(zcat, w_pool_grp[0], pool_scale)
    passed, tkn = gather_pass(2, (og, ps))
    y_pool, y_gla, mixed, x1, h2 = _mix_out_fwd(ps, og, zcat, xs, wpp, wgp, wout, b_gate, g_ffn, tkn)
    wg = gather_done(2, passed, h2)
    wt_up = wg["w_up"].reshape(2 * D_FF, D_MODEL)
    wdown = wg["w_down"].reshape(D_FF, D_MODEL)
    wconv4 = wg["w_conv"].reshape(2, 4, 3, FF_BLK)
    bconv4 = b_conv.reshape(2, 4, 1, FF_BLK)
    blk4 = lambda b, i, k: (b // 4, b % 4, 0, 0)
    u4, act = _up_conv_fwd(h2, wt_up, wconv4, bconv4)
    loss_part, dx2, dx2_bf, dg_final = _mm_tokens(
        act, wdown, blk_a=(None, 4, TOK_MM_TILE, FF_BLK), map_a=lambda i: (0, 0, i, 0),
        pieces=[(b, b * FF_BLK, FF_BLK) for b in range(4)], res=x1, then=("loss", g_final.reshape(1, D_MODEL), target),
        name="mm_down_loss")

    da = _mm(dx2_bf, wdown, out_shape=(1, 4, SEQ, FF_BLK), out_dtype=BF, grid=(4, 1, 1),
             blk_a=(SEQ, D_MODEL), blk_b=(FF_BLK, D_MODEL), blk_o=ff_seq,
             map_a=whole, map_b=lambda b, i, k: (b, 0), map_o=lambda b, i, k: (0, b, 0, 0), tb=True, name="mm_d_act")
    d_wdown = _mm(act, dx2_bf, out_shape=(D_FF, D_MODEL), out_dtype=BF, grid=(4, 1, 1),
                  blk_a=ff_seq, blk_b=(SEQ, D_MODEL), blk_o=(FF_BLK, D_MODEL),
                  map_a=lambda b, i, k: (0, b, 0, 0), map_b=whole, map_o=lambda b, i, k: (b, 0), ta=True,
                  name="mm_d_wdown")
    du4, d_wconv, d_bconv = _conv_bwd(u4, da, wconv4, bconv4)
    d_wt_up = _mm(du4, h2, out_shape=(2 * D_FF, D_MODEL), out_dtype=BF, grid=(N_DEV, 1, 1),
                  blk_a=ff_seq, blk_b=(SEQ, D_MODEL), blk_o=(FF_BLK, D_MODEL),
                  map_a=blk4, map_b=whole, map_o=lambda b, i, k: (b, 0), ta=True, name="mm_d_wup")
    res = {}

    def to_sibling(keys, parts):
        return [t for k in keys for t in (parts[k], lax.empty((4,) + parts[k].shape[2:], BF))], 4 * len(keys), _reduce_first

    def to_chips(keys, st, after):
        arrays = _split_wait("reduce_wait_" + keys[0], st, _reduce_first, after)
        sums = [_pair_sum(p, r, core, "pair_sum_" + k) for k, p, r in zip(keys, arrays[0::2], arrays[1::2])]
        return [t for s in sums for t in (s, lax.empty((3,) + s.shape[1:], BF))], 3 * len(keys), _reduce_second

    def reduce_start(keys, parts):
        st, tkn = _split_start("reduce_start_" + keys[0], [to_sibling(keys, parts)])
        return st[0], tkn

    def reduce_cross(keys, st, after):
        st2, tkn = _split_start("reduce_cross_" + keys[0], [to_chips(keys, st, after)])
        return st2[0], tkn

    def reduce_done(keys, st2, after):
        arrays = _split_wait("reduce_cross_wait_" + keys[0], st2, _reduce_second, after)
        for k, s, r in zip(keys, arrays[0::2], arrays[1::2]):
            outs = _chip_sum_adamw(s, r, big[k], moments[k][0], moments[k][1], chip, "adamw_" + k)
            res[k] = [(t.T if k in ("w_in", "w_up") else t)[None] for t in outs]

    ffn_keys = ("w_down", "w_up")
    ffn_red, tkn = reduce_start(ffn_keys, dict(w_down=d_wdown.reshape(4, 2, D_FF // N_DEV, D_MODEL),
                                               w_up=d_wt_up.reshape(4, 2, FF_BLK, D_MODEL)))
    dx1, dg_ffn = _mm_tokens(
        du4, wt_up, blk_a=(2, 4, TOK_MM_TILE, FF_BLK), map_a=lambda i: (0, 0, i, 0),
        pieces=[((b // 4, b % 4), b * FF_BLK, FF_BLK) for b in range(N_DEV)], after=tkn, then=("rms_bwd", x1, g_ffn, dx2),
        name="mm_d_h2_rms")

    sq_t = dict(out_shape=(D_MODEL, D_MODEL), grid=(1, 1, N_MM_TILES), blk_a=(MM_TILE, D_MODEL),
                blk_b=(MM_TILE, D_MODEL), blk_o=(D_MODEL, D_MODEL), map_a=kblk, map_b=kblk, map_o=whole, ta=True)
    d_wout = _mm(mixed, dx1, out_dtype=BF, name="mm_d_wout", **sq_t)
    dzcat, dy_pool, dy_gla, db_gate = _mix_bwd(dx1, wout, zcat, b_gate, y_pool, y_gla)
    d_wgp = _mm(og, dy_gla, out_dtype=BF, name="mm_d_wgp", **sq_t)
    mix_keys = ("w_out", "w_gla_proj")
    (ffn_red, mix_red), tkn = _split_start("reduce_cross_w_down", [
        to_chips(ffn_keys, ffn_red, db_gate),
        to_sibling(mix_keys, dict(w_out=d_wout.reshape(4, 2, D_MODEL // N_DEV, D_MODEL),
                                  w_gla_proj=d_wgp.reshape(4, 2, D_MODEL // N_DEV, D_MODEL)))])
    dzcat, d_o, dg_head = _post_gla_bwd(dzcat, dy_gla, wgp, o, zcat, g_gla_head + tkn[:1, :1])
    dzcat, dla = _gla_bwd(dzcat, zcat, la, d_o, states)
    dzcat, d_wgk, db_gk = _gk_bwd(dzcat, dla, h, wt_cat, wgk_pad, b_gk)
    dps = _mm(dy_pool, wpp, out_shape=(SEQ, POOL_WIDTH), out_dtype=F32, grid=(N_MM_TILES, 1, 1),
              blk_a=(MM_TILE, D_MODEL), blk_b=(POOL_WIDTH, D_MODEL), blk_o=(MM_TILE, POOL_WIDTH),
              map_a=tok, map_b=whole, map_o=tok, tb=True, name="mm_d_ps")
    d_wpp = _mm(ps, dy_pool, out_shape=(POOL_WIDTH, D_MODEL), out_dtype=F32, grid=(1, 1, N_MM_TILES),
                blk_a=(MM_TILE, POOL_WIDTH), blk_b=(MM_TILE, D_MODEL), blk_o=(POOL_WIDTH, D_MODEL),
                map_a=kblk, map_b=kblk, map_o=whole, ta=True, name="mm_d_wpp")
    dzcat, d_wgrp, d_scale = _pool_bwd(dzcat, zcat, dps, w_pool_grp[0], pool_scale)
    row = lambda t: t.reshape(1, D_MODEL)
    conv_vec = lambda t: t.reshape(2, 4, 1, FF_BLK)
    small = [("b_gate", db_gate, b_gate, m_b_gate, v_b_gate, False),
             ("w_gk_up", d_wgk.reshape(GATE_RANK, N_DEV, GLA_DK // N_DEV).transpose(1, 0, 2), w_gk_up, m_w_gk_up,
              v_w_gk_up, True),
             ("b_gk", db_gk, b_gk, m_b_gk, v_b_gk, False),
             ("w_pool_grp", d_wgrp, w_pool_grp, m_w_pool_grp, v_w_pool_grp, False),
             ("pool_scale", d_scale, pool_scale, m_pool_scale, v_pool_scale, False),
             ("g_gla_head", dg_head, g_gla_head, m_g_gla_head, v_g_gla_head, False),
             ("g_ffn", dg_ffn, g_ffn, m_g_ffn, v_g_ffn, False),
             ("w_conv", d_wconv.reshape(N_DEV, 3, FF_BLK), w_conv, m_w_conv, v_w_conv, True),
             ("b_conv", d_bconv, conv_vec(b_conv), conv_vec(m_b_conv), conv_vec(v_b_conv), False),
             ("g_final", dg_final, row(g_final), row(m_g_final), row(v_g_final), False)]

    def to_all(parts):
        return [t for p in parts for t in (p, _gather_landing(p, me))], 7 * len(parts), _gather_direct

    (small_sent, mix_red), tkn = _split_start("small_start", [to_all([t[1] for t in small] + [loss_part]),
                                                              to_chips(mix_keys, mix_red, dla)])
    d_wt_cat = _mm(dzcat, h, out_shape=(N_DZ, D_MODEL), out_dtype=BF, grid=(N_DZ // DZ_TILE, 1, 1),
                   blk_a=(SEQ, DZ_TILE), blk_b=(SEQ, D_MODEL), blk_o=(DZ_TILE, D_MODEL),
                   map_a=lambda j, i, k: (0, j), map_b=whole, map_o=lambda j, i, k: (j, 0), ta=True, after=tkn,
                   name="mm_d_wcat")
    in_keys = ("w_in", "w_pool_proj")
    in_red, tkn = reduce_start(in_keys, dict(
        w_in=_shard_d_w_in(d_wt_cat).reshape(4, 2, IN_SHARD, D_MODEL),
        w_pool_proj=d_wpp.reshape(POOL_WIDTH, N_DEV, D_MODEL // N_DEV).transpose(1, 0, 2).astype(BF)
        .reshape(4, 2, POOL_WIDTH, D_MODEL // N_DEV)))
    reduce_done(mix_keys, mix_red, tkn)
    in_red, tkn = reduce_cross(in_keys, in_red, res["w_out"][0])
    grad_x, dg_mix = _mm_tokens(dzcat, wt_cat, blk_a=(TOK_MM_TILE, N_DZ), map_a=lambda i: (i, 0),
                                pieces=[(None, 0, N_DZ)], after=tkn, then=("rms_bwd", xs, g_mix, dx1),
                                name="mm_d_h_rms")
    (g_mix_sent,), tkn = _split_start("g_mix_start", [to_all([dg_mix])])
    reduce_done(ffn_keys, ffn_red, (grad_x, tkn))
    gathered = _split_wait("small_wait", small_sent, _gather_direct, res["w_down"][0])[1::2]
    small.append(("g_mix", dg_mix, g_mix, m_g_mix, v_g_mix, False))
    gathered = list(gathered[:-1]) + [_split_wait("g_mix_wait", g_mix_sent, _gather_direct, gathered[0])[1], gathered[-1]]
    small_out, loss_sum = _small_sum_adamw(jnp.reshape(me, (1,)).astype(jnp.int32),
                                           [(p,) + t[2:] for p, t in zip(gathered, small)], gathered[-1])
    for t, outs in zip(small, small_out):
        res[t[0]] = list(outs)
    res["b_conv"] = [t.reshape(b_conv.shape) for t in res["b_conv"]]
    res["g_final"] = [t.reshape(g_final.shape) for t in res["g_final"]]

    reduce_done(in_keys, in_red, loss_sum)
    loss = loss_sum[0, 0]
    order =["g_mix", "w_in", "b_gate", "w_gk_up", "b_gk", "w_pool_grp", "pool_scale", "g_gla_head", "w_pool_proj",
             "w_gla_proj", "w_out", "g_ffn", "w_up", "w_conv", "b_conv", "w_down", "g_final"]
    return (loss, grad_x[None], *[res[k][0] for k in order], *[res[k][1] for k in order],
            *[res[k][2] for k in order], *[res[k][3] for k in order])
```

```python
import jax
import jax.numpy as jnp
from jax import lax
from jax.experimental import pallas as pl
from jax.experimental.pallas import tpu as pltpu

F32 = jnp.float32
BF = jnp.bfloat16
HIGHEST = lax.Precision.HIGHEST
MESH = pl.DeviceIdType.MESH

N_DEV = 8
SEQ = 2048
D_MODEL = 1024
CHUNK = 64
EPS = 1e-6
POOL_WIDTH = 512
POOL_WINDOWS = (2, 4, 8, 16)
POOL_GD = 128
POOL_HALO = 16
HEADS = 4
HK = 128
HV = 256
GLA_DK = 512
GATE_RANK = 16
GATE_NORM = 16.0
D_FF = 2816
FF_BLK = 704
IN_SHARD = 706
C_QKV, C_GATE, C_OG, C_POOL, C_GK = 0, 2048, 4096, 5120, 5632
N_CAT = 5632
GK_PAD = 128
N_DZ = N_CAT + GK_PAD
R_POOL, R_QKV, R_OG, R_GK, R_GATE = 0, 512, 2560, 3584, 3600

ADAM_LR, ADAM_B1, ADAM_B2, ADAM_EPS, ADAM_WD, ADAM_STEP = 0.001, 0.9, 0.999, 1e-08, 0.01, 10
ADAM_C1 = 1.0 - ADAM_B1 ** ADAM_STEP
ADAM_C2 = 1.0 - ADAM_B2 ** ADAM_STEP

VMEM_BYTES_V7X = 64 * 1024 * 1024
VMEM_LIMIT = VMEM_BYTES_V7X * 3 // 4

TOK_TILE = 256
HALO = 8
GLA_CPS = 4


def _params(*sem):
    return pltpu.CompilerParams(dimension_semantics=sem, vmem_limit_bytes=VMEM_LIMIT)


def _const_spec(shape):
    nd = len(shape)
    return pl.BlockSpec(shape, lambda *_: (0,) * nd)


def _in_hbm(t):
    return pltpu.with_memory_space_constraint(t, pltpu.HBM)


def _out_hbm(shape, dtype):
    return pltpu.HBM(shape, dtype)


def _dot(a, b, ta=False, tb=False):
    dims = (((0 if ta else 1,), (1 if tb else 0,)), ((), ()))
    return lax.dot_general(a.astype(BF), b.astype(BF), dims, preferred_element_type=F32)


def _dot_exact(a, b):
    return jnp.dot(a, b, precision=HIGHEST, preferred_element_type=F32)


def _sigmoid(x):
    return 0.5 * jnp.tanh(0.5 * x) + 0.5


def _mm(a, b, *, out_shape, out_dtype, grid, blk_a, blk_b, blk_o, map_a, map_b, map_o, ta=False, tb=False,
        after=None, name):
    gk = grid[2]
    n_in = 2 + (after is not None)

    def body(*refs):
        a_ref, b_ref, o_ref = refs[0], refs[1], refs[n_in]
        prod = _dot(a_ref[...], b_ref[...], ta, tb)
        if gk == 1:
            o_ref[...] = prod.astype(out_dtype)
        else:
            acc = refs[n_in + 1]
            k = pl.program_id(2)

            @pl.when(k == 0)
            def _():
                acc[...] = prod

            @pl.when(k > 0)
            def _():
                acc[...] += prod

            @pl.when(k == gk - 1)
            def _():
                o_ref[...] = acc[...].astype(out_dtype)

    in_specs = [pl.BlockSpec(blk_a, map_a), pl.BlockSpec(blk_b, map_b)]
    args = [_in_hbm(a), _in_hbm(b)]
    if after is not None:
        in_specs.append(pl.BlockSpec(memory_space=pl.ANY))
        args.append(after)
    return pl.pallas_call(
        body, name=name, grid=grid, in_specs=in_specs, out_specs=pl.BlockSpec(blk_o, map_o),
        out_shape=_out_hbm(out_shape, out_dtype),
        scratch_shapes=[] if gk == 1 else [pltpu.VMEM(tuple(d for d in blk_o if d is not None), F32)],
        compiler_params=_params("parallel", "parallel", "arbitrary"),
    )(*args)


TOK_MM_TILE = 256


def _mm_tokens(a, w, *, blk_a, map_a, pieces, res=None, after=None, then=None, name):
    n_in = 2 + (res is not None) + (after is not None) + (0 if then is None else len(then) - 1)

    def accumulate(ref, part):
        @pl.when(pl.program_id(0) == 0)
        def _():
            ref[...] = part

        @pl.when(pl.program_id(0) > 0)
        def _():
            ref[...] += part

    def body(*refs):
        a_ref, w_ref = refs[:2]
        extra, outs = refs[n_in - (0 if then is None else len(then) - 1):n_in], refs[n_in:]
        total = None
        for idx, row, n in pieces:
            av = a_ref[...] if idx is None else a_ref[idx]
            prod = _dot(av, w_ref[row:row + n, :])
            total = prod if total is None else total + prod
        if res is not None:
            total = total + refs[2][...]
        if then is None:
            outs[0][...] = total
        elif then[0] == "rms_bwd":
            dx, part = _rms_bwd_tile(total, extra[0][...], extra[1][...], extra[2][...])
            outs[0][...] = dx
            accumulate(outs[1], part)
        else:
            lpart, dx, part = _loss_tile(total, extra[0][...], extra[1][...])
            outs[1][...] = dx
            outs[2][...] = dx.astype(BF)
            accumulate(outs[0], lpart)
            accumulate(outs[3], part)

    tile = pl.BlockSpec((TOK_MM_TILE, D_MODEL), lambda i: (i, 0))
    vec = _const_spec((1, D_MODEL))
    big = _out_hbm((SEQ, D_MODEL), F32)
    small = _out_hbm((1, D_MODEL), F32)
    in_specs = [pl.BlockSpec(blk_a, map_a), pl.BlockSpec(w.shape, lambda i: (0, 0), pipeline_mode=pl.Buffered(1))]
    args = [a, w]
    if res is not None:
        in_specs.append(tile)
        args.append(res)
    if after is not None:
        in_specs.append(pl.BlockSpec(memory_space=pl.ANY))
        args.append(after)
    if then is None:
        out_specs, out_shape = tile, big
    elif then[0] == "rms_bwd":
        in_specs += [tile, vec, tile]
        out_specs, out_shape = [tile, vec], [big, small]
    else:
        in_specs += [vec, tile]
        out_specs = [_const_spec((1, 128)), tile, tile, vec]
        out_shape = [_out_hbm((1, 128), F32), big, _out_hbm((SEQ, D_MODEL), BF), small]
    if then is not None:
        args += list(then[1:])
    return pl.pallas_call(
        body, name=name, grid=(SEQ // TOK_MM_TILE,), in_specs=in_specs, out_specs=out_specs, out_shape=out_shape,
        compiler_params=_params("parallel" if then is None else "arbitrary"),
    )(*[_in_hbm(t) for t in args])


def _rms_fwd(x, g, name):
    def body(x_ref, g_ref, o_ref):
        xv = x_ref[...]
        r = lax.rsqrt(jnp.mean(xv * xv, axis=-1, keepdims=True) + EPS)
        o_ref[...] = (xv * r * g_ref[...]).astype(BF)

    tile = pl.BlockSpec((TOK_TILE, D_MODEL), lambda i: (i, 0))
    return pl.pallas_call(
        body, name=name, grid=(SEQ // TOK_TILE,), in_specs=[tile, _const_spec((1, D_MODEL))], out_specs=tile,
        out_shape=_out_hbm((SEQ, D_MODEL), BF), compiler_params=_params("parallel"),
    )(*map(_in_hbm, (x, g)))


def _rms_bwd_tile(dyv, xv, gv, dresv):
    r = lax.rsqrt(jnp.mean(xv * xv, axis=-1, keepdims=True) + EPS)
    xn = xv * r
    dxn = dyv * gv
    return dresv + r * (dxn - xn * jnp.mean(dxn * xn, axis=-1, keepdims=True)), jnp.sum(dyv * xn, axis=0, keepdims=True)


def _loss_tile(xv, gv, tv):
    r = lax.rsqrt(jnp.mean(xv * xv, axis=-1, keepdims=True) + EPS)
    xn = xv * r
    err = xn * gv - tv
    lpart = jnp.full((1, 128), 0.5 * jnp.sum(jnp.mean(err * err, axis=-1, keepdims=True)), F32)
    dyv = err * (1.0 / D_MODEL)
    dxn = dyv * gv
    return lpart, r * (dxn - xn * jnp.mean(dxn * xn, axis=-1, keepdims=True)), jnp.sum(dyv * xn, axis=0, keepdims=True)


def _pool_counts(w):
    pos = lax.broadcasted_iota(jnp.int32, (SEQ, 1), 0).astype(F32)
    return jnp.minimum(pos + 1.0, float(w))


def _pool_window(u, w, ext):
    ext[pl.ds(POOL_HALO, SEQ), :] = u
    win = u
    for j in range(1, w):
        win = win + ext[pl.ds(POOL_HALO - j, SEQ), :]
    return win / _pool_counts(w) - u


def _pool_fwd(zcat, w_grp, scale):
    def body(z_ref, w_ref, s_ref, o_ref, ext):
        ext[pl.ds(0, POOL_HALO), :] = jnp.zeros((POOL_HALO, POOL_GD), F32)
        for g, w in enumerate(POOL_WINDOWS):
            cols = slice(g * POOL_GD, (g + 1) * POOL_GD)
            p = _pool_window(z_ref[:, cols].astype(F32), w, ext)
            o_ref[:, cols] = (_dot(p, w_ref[g]) * s_ref[:, cols]).astype(BF)

    return pl.pallas_call(
        body, name="pool_fwd", grid=(1,),
        in_specs=[pl.BlockSpec((SEQ, POOL_WIDTH), lambda i: (0, C_POOL // POOL_WIDTH)),
                  _const_spec((4, POOL_GD, POOL_GD)), _const_spec((1, POOL_WIDTH))],
        out_specs=_const_spec((SEQ, POOL_WIDTH)), out_shape=_out_hbm((SEQ, POOL_WIDTH), BF),
        scratch_shapes=[pltpu.VMEM((POOL_HALO + SEQ, POOL_GD), F32)], compiler_params=_params("arbitrary"),
    )(*map(_in_hbm, (zcat, w_grp, scale)))


def _pool_bwd(dzcat, zcat, dps, w_grp, scale):
    def body(dz_in, z_ref, dps_ref, w_ref, s_ref, dz_ref, dw_ref, dsc_ref, ext, ext2):
        del dz_in
        ext[pl.ds(0, POOL_HALO), :] = jnp.zeros((POOL_HALO, POOL_GD), F32)
        ext2[pl.ds(SEQ, POOL_HALO), :] = jnp.zeros((POOL_HALO, POOL_GD), F32)
        for g, w in enumerate(POOL_WINDOWS):
            cols = slice(g * POOL_GD, (g + 1) * POOL_GD)
            p = _pool_window(z_ref[:, cols].astype(F32), w, ext)
            wg = w_ref[g]
            pg = _dot(p, wg)
            dpsv = dps_ref[:, cols]
            dsc_ref[:, cols] = jnp.sum(dpsv * pg, axis=0, keepdims=True)
            dpg = dpsv * s_ref[:, cols]
            dw_ref[g] = _dot(p, dpg, ta=True)
            dp = _dot(dpg, wg, tb=True)
            dpc = dp / _pool_counts(w)
            ext2[pl.ds(0, SEQ), :] = dpc
            du = dpc
            for j in range(1, w):
                du = du + ext2[pl.ds(j, SEQ), :]
            dz_ref[:, cols] = (du - dp).astype(BF)

    return pl.pallas_call(
        body, name="pool_bwd", grid=(1,),
        in_specs=[pl.BlockSpec(memory_space=pl.ANY),
                  pl.BlockSpec((SEQ, POOL_WIDTH), lambda i: (0, C_POOL // POOL_WIDTH)),
                  _const_spec((SEQ, POOL_WIDTH)), _const_spec((4, POOL_GD, POOL_GD)), _const_spec((1, POOL_WIDTH))],
        out_specs=[pl.BlockSpec((SEQ, POOL_WIDTH), lambda i: (0, C_POOL // POOL_WIDTH)),
                   _const_spec((4, POOL_GD, POOL_GD)), _const_spec((1, POOL_WIDTH))],
        out_shape=[_out_hbm((SEQ, N_DZ), BF), _out_hbm((4, POOL_GD, POOL_GD), F32),
                   _out_hbm((1, POOL_WIDTH), F32)],
        scratch_shapes=[pltpu.VMEM((POOL_HALO + SEQ, POOL_GD), F32), pltpu.VMEM((SEQ + POOL_HALO, POOL_GD), F32)],
        input_output_aliases={0: 0}, compiler_params=_params("arbitrary"),
    )(*map(_in_hbm, (dzcat, zcat, dps, w_grp, scale)))


GK_TILE = 512


GK_ROWS = pl.BlockSpec((GK_PAD, D_MODEL), lambda i: (C_GK // GK_PAD, 0))


def _gk_fwd(h, wt_cat, wgk_pad, b_gk):
    def body(h_ref, wt_ref, w_ref, b_ref, la_ref):
        z_gk = _dot(h_ref[...], wt_ref[...], tb=True)
        pre = _dot(z_gk, w_ref[...]) + b_ref[...]
        la_ref[...] = (jnp.minimum(pre, 0.0) - jnp.log(1.0 + jnp.exp(-jnp.abs(pre)))) * (1.0 / GATE_NORM)

    return pl.pallas_call(
        body, name="gk_fwd", grid=(SEQ // GK_TILE,),
        in_specs=[pl.BlockSpec((GK_TILE, D_MODEL), lambda i: (i, 0)), GK_ROWS,
                  _const_spec((GK_PAD, GLA_DK)), _const_spec((1, GLA_DK))],
        out_specs=pl.BlockSpec((GK_TILE, GLA_DK), lambda i: (i, 0)),
        out_shape=_out_hbm((SEQ, GLA_DK), F32), compiler_params=_params("parallel"),
    )(*map(_in_hbm, (h, wt_cat, wgk_pad, b_gk)))


def _gk_bwd(dzcat, dla, h, wt_cat, wgk_pad, b_gk):
    def body(dz_in, dla_ref, h_ref, wt_ref, w_ref, b_ref, dz_ref, dw_ref, db_ref):
        del dz_in
        wv = w_ref[...]
        z_gk = _dot(h_ref[...], wt_ref[...], tb=True)
        pre = _dot(z_gk, wv) + b_ref[...]
        dpre = dla_ref[...] * (1.0 / GATE_NORM) * (1.0 - _sigmoid(pre))
        dz_ref[...] = _dot(dpre, wv, tb=True).astype(BF)
        dwp = _dot(z_gk, dpre, ta=True)[:GATE_RANK]
        dbp = jnp.sum(dpre, axis=0, keepdims=True)

        @pl.when(pl.program_id(0) == 0)
        def _():
            dw_ref[...] = dwp
            db_ref[...] = dbp

        @pl.when(pl.program_id(0) > 0)
        def _():
            dw_ref[...] += dwp
            db_ref[...] += dbp

    return pl.pallas_call(
        body, name="gk_bwd", grid=(SEQ // GK_TILE,),
        in_specs=[pl.BlockSpec(memory_space=pl.ANY), pl.BlockSpec((GK_TILE, GLA_DK), lambda i: (i, 0)),
                  pl.BlockSpec((GK_TILE, D_MODEL), lambda i: (i, 0)), GK_ROWS, _const_spec((GK_PAD, GLA_DK)),
                  _const_spec((1, GLA_DK))],
        out_specs=[pl.BlockSpec((GK_TILE, GK_PAD), lambda i: (i, C_GK // GK_PAD)), _const_spec((GATE_RANK, GLA_DK)),
                   _const_spec((1, GLA_DK))],
        out_shape=[_out_hbm((SEQ, N_DZ), BF), _out_hbm((GATE_RANK, GLA_DK), F32),
                   _out_hbm((1, GLA_DK), F32)],
        input_output_aliases={0: 0}, compiler_params=_params("arbitrary"),
    )(*map(_in_hbm, (dzcat, dla, h, wt_cat, wgk_pad, b_gk)))


GLA_ROWS = GLA_CPS * CHUNK
GLA_STEPS = SEQ // GLA_ROWS
QKV_W = 2048


def _tri():
    return lax.broadcasted_iota(jnp.int32, (CHUNK, CHUNK), 0) >= lax.broadcasted_iota(jnp.int32, (CHUNK, CHUNK), 1)


def _chunk_cumsum(la_ref, rows):
    return _dot_exact(_tri().astype(F32), la_ref[rows, :])


def _gla_chunk(qkv_ref, la_ref, rows, h, bc_all):
    tri = _tri()
    q = qkv_ref[rows, h * HK:(h + 1) * HK].astype(F32) * (HK ** -0.5)
    k = qkv_ref[rows, GLA_DK + h * HK:GLA_DK + (h + 1) * HK].astype(F32)
    v = qkv_ref[rows, 2 * GLA_DK + h * HV:2 * GLA_DK + (h + 1) * HV].astype(BF)
    la = la_ref[rows, h * HK:(h + 1) * HK]
    bc = bc_all[:, h * HK:(h + 1) * HK]
    e_pos, e_neg = jnp.exp(bc), jnp.exp(-bc)
    dl = jnp.exp(jnp.sum(la, axis=0, keepdims=True))
    q_fw, q_bw, k_fw, k_bw = q * e_pos, q * e_neg, k * e_neg, k * e_pos
    scores = jnp.where(tri, _dot(q_fw, k_fw, tb=True), _dot(q_bw, k_bw, tb=True))
    return tri, v, e_pos, e_neg, dl, q_fw, q_bw, k_fw, k_bw, scores


def _gla_fwd(zcat, la, after):
    def body(qkv_ref, la_ref, after_ref, o_ref, st_ref, state):
        del after_ref

        @pl.when(pl.program_id(0) == 0)
        def _():
            state[...] = jnp.zeros_like(state)

        for c in range(GLA_CPS):
            rows = slice(c * CHUNK, (c + 1) * CHUNK)
            bc_all = _chunk_cumsum(la_ref, rows)
            for h in range(HEADS):
                _, v, _, _, dl, q_fw, _, k_fw, _, scores = _gla_chunk(qkv_ref, la_ref, rows, h, bc_all)
                st = state[h]
                st_ref[c, h] = st
                o_ref[rows, h * HV:(h + 1) * HV] = _dot(scores, v) + _dot(q_fw, st, tb=True)
                state[h] = st * dl + _dot(v, k_fw * dl, ta=True)

    return pl.pallas_call(
        body, name="gla_fwd", grid=(GLA_STEPS,),
        in_specs=[pl.BlockSpec((GLA_ROWS, QKV_W), lambda i: (i, 0)), pl.BlockSpec((GLA_ROWS, GLA_DK), lambda i: (i, 0)),
                  pl.BlockSpec(memory_space=pl.ANY)],
        out_specs=[pl.BlockSpec((GLA_ROWS, D_MODEL), lambda i: (i, 0)),
                   pl.BlockSpec((GLA_CPS, HEADS, HV, HK), lambda i: (i, 0, 0, 0))],
        out_shape=[_out_hbm((SEQ, D_MODEL), F32),
                   _out_hbm((SEQ // CHUNK, HEADS, HV, HK), F32)],
        scratch_shapes=[pltpu.VMEM((HEADS, HV, HK), F32)], compiler_params=_params("arbitrary"),
    )(*map(_in_hbm, (zcat, la)), after)


def _gla_bwd(dzcat, zcat, la, d_o, states):
    def body(dz_in, qkv_ref, la_ref, do_ref, st_ref, dqkv_ref, dla_ref, dstate):
        del dz_in

        @pl.when(pl.program_id(0) == 0)
        def _():
            dstate[...] = jnp.zeros_like(dstate)

        last_row = lax.broadcasted_iota(jnp.int32, (CHUNK, HK), 0) == CHUNK - 1
        upper = (lax.broadcasted_iota(jnp.int32, (CHUNK, CHUNK), 0)
                 <= lax.broadcasted_iota(jnp.int32, (CHUNK, CHUNK), 1)).astype(F32)
        for c in reversed(range(GLA_CPS)):
            rows = slice(c * CHUNK, (c + 1) * CHUNK)
            bc_all = _chunk_cumsum(la_ref, rows)
            dbs = []
            for h in range(HEADS):
                tri, v, e_pos, e_neg, dl, q_fw, q_bw, k_fw, k_bw, scores = _gla_chunk(qkv_ref, la_ref, rows, h, bc_all)
                st = st_ref[c, h]
                dst = dstate[h]
                d_out = do_ref[rows, h * HV:(h + 1) * HV].astype(BF)
                k_dec = k_fw * dl
                dp = _dot(d_out, v, tb=True)
                dp_fw = jnp.where(tri, dp, 0.0)
                dp_bw = jnp.where(tri, 0.0, dp)
                dv = _dot(scores, d_out, ta=True) + _dot(k_dec, dst, tb=True)
                dk_dec = _dot(v, dst)
                dq_fw = _dot(dp_fw, k_fw) + _dot(d_out, st)
                dk_fw = _dot(dp_fw, q_fw, ta=True) + dk_dec * dl
                dq_bw = _dot(dp_bw, k_bw)
                dk_bw = _dot(dp_bw, q_bw, ta=True)
                ddl = jnp.sum(st * dst, axis=0, keepdims=True) + jnp.sum(k_fw * dk_dec, axis=0, keepdims=True)
                dstate[h] = dst * dl + _dot(d_out, q_fw, ta=True)
                dq = (dq_fw * e_pos + dq_bw * e_neg) * (HK ** -0.5)
                dk = dk_fw * e_neg + dk_bw * e_pos
                dbs.append(dq_fw * q_fw - dk_fw * k_fw - dq_bw * q_bw + dk_bw * k_bw + jnp.where(last_row, ddl * dl, 0.0))
                dqkv_ref[rows, h * HK:(h + 1) * HK] = dq.astype(BF)
                dqkv_ref[rows, GLA_DK + h * HK:GLA_DK + (h + 1) * HK] = dk.astype(BF)
                dqkv_ref[rows, 2 * GLA_DK + h * HV:2 * GLA_DK + (h + 1) * HV] = dv.astype(BF)
            dla_ref[rows, :] = _dot_exact(upper, jnp.concatenate(dbs, axis=1))

    rev = lambda i: (GLA_STEPS - 1 - i, 0)
    return pl.pallas_call(
        body, name="gla_bwd", grid=(GLA_STEPS,),
        in_specs=[pl.BlockSpec(memory_space=pl.ANY), pl.BlockSpec((GLA_ROWS, QKV_W), rev),
                  pl.BlockSpec((GLA_ROWS, GLA_DK), rev), pl.BlockSpec((GLA_ROWS, D_MODEL), rev),
                  pl.BlockSpec((GLA_CPS, HEADS, HV, HK), lambda i: (GLA_STEPS - 1 - i, 0, 0, 0))],
        out_specs=[pl.BlockSpec((GLA_ROWS, QKV_W), rev), pl.BlockSpec((GLA_ROWS, GLA_DK), rev)],
        out_shape=[_out_hbm((SEQ, N_DZ), BF), _out_hbm((SEQ, GLA_DK), F32)],
        scratch_shapes=[pltpu.VMEM((HEADS, HV, HK), F32)], input_output_aliases={0: 0},
        compiler_params=_params("arbitrary"),
    )(*map(_in_hbm, (dzcat, zcat, la, d_o, states)))


def _silu_parts(x):
    s = _sigmoid(x)
    return x * s, s * (1.0 + x * (1.0 - s))


def _post_gla_fwd(o, zcat, g_head):
    def body(o_ref, zog_ref, g_ref, out_ref):
        for h in range(HEADS):
            cols = slice(h * HV, (h + 1) * HV)
            ov = o_ref[:, cols]
            r = lax.rsqrt(jnp.mean(ov * ov, axis=-1, keepdims=True) + EPS)
            act, _ = _silu_parts(zog_ref[:, cols].astype(F32))
            out_ref[:, cols] = (ov * r * g_ref[...] * act).astype(BF)

    tile = pl.BlockSpec((TOK_TILE, D_MODEL), lambda i: (i, 0))
    return pl.pallas_call(
        body, name="post_gla_fwd", grid=(SEQ // TOK_TILE,),
        in_specs=[tile, pl.BlockSpec((TOK_TILE, D_MODEL), lambda i: (i, C_OG // D_MODEL)), _const_spec((1, HV))],
        out_specs=tile, out_shape=_out_hbm((SEQ, D_MODEL), BF), compiler_params=_params("parallel"),
    )(*map(_in_hbm, (o, zcat, g_head)))


def _post_gla_bwd(dzcat, dy_gla, w_gla_proj, o, zcat, g_head):
    def body(dz_in, dyg_ref, w_ref, o_ref, zog_ref, g_ref, dz_ref, do_ref, dg_ref):
        del dz_in
        dog = _dot(dyg_ref[...], w_ref[...], tb=True)
        gpart = jnp.zeros((1, HV), F32)
        gv = g_ref[...]
        for h in range(HEADS):
            cols = slice(h * HV, (h + 1) * HV)
            ov = o_ref[:, cols]
            r = lax.rsqrt(jnp.mean(ov * ov, axis=-1, keepdims=True) + EPS)
            on = ov * r
            act, dact = _silu_parts(zog_ref[:, cols].astype(F32))
            dogv = dog[:, cols]
            dz_ref[:, cols] = (dogv * on * gv * dact).astype(BF)
            d_on_g = dogv * act
            gpart = gpart + jnp.sum(d_on_g * on, axis=0, keepdims=True)
            dxn = d_on_g * gv
            do_ref[:, cols] = (r * (dxn - on * jnp.mean(dxn * on, axis=-1, keepdims=True))).astype(BF)

        @pl.when(pl.program_id(0) == 0)
        def _():
            dg_ref[...] = gpart

        @pl.when(pl.program_id(0) > 0)
        def _():
            dg_ref[...] += gpart

    tile = pl.BlockSpec((TOK_TILE, D_MODEL), lambda i: (i, 0))
    ogspec = pl.BlockSpec((TOK_TILE, D_MODEL), lambda i: (i, C_OG // D_MODEL))
    return pl.pallas_call(
        body, name="post_gla_bwd", grid=(SEQ // TOK_TILE,),
        in_specs=[pl.BlockSpec(memory_space=pl.ANY), tile, _const_spec((D_MODEL, D_MODEL)), tile, ogspec,
                  _const_spec((1, HV))],
        out_specs=[ogspec, tile, _const_spec((1, HV))],
        out_shape=[_out_hbm((SEQ, N_DZ), BF), _out_hbm((SEQ, D_MODEL), BF),
                   _out_hbm((1, HV), F32)],
        input_output_aliases={0: 0}, compiler_params=_params("arbitrary"),
    )(*map(_in_hbm, (dzcat, dy_gla, w_gla_proj, o, zcat, g_head)))


GATE_W = 2 * D_MODEL


def _mix_out_fwd(ps, og, zcat, x, w_pool_proj, w_gla_proj, w_out, b_gate, g_ffn, after):
    def body(ps_ref, og_ref, zg_ref, x_ref, wpp_ref, wgp_ref, wout_ref, b_ref, g_ref, after_ref,
             yp_ref, yg_ref, mixed_ref, x1_ref, h2_ref):
        del after_ref
        y_pool = _dot(ps_ref[...], wpp_ref[...])
        y_gla = _dot(og_ref[...], wgp_ref[...])
        yp_ref[...] = y_pool.astype(BF)
        yg_ref[...] = y_gla.astype(BF)
        g0 = _sigmoid(zg_ref[:, :D_MODEL].astype(F32) + b_ref[:, :D_MODEL])
        g1 = _sigmoid(zg_ref[:, D_MODEL:].astype(F32) + b_ref[:, D_MODEL:])
        mixed = (g0 * y_pool + g1 * y_gla).astype(BF)
        mixed_ref[...] = mixed
        x1 = x_ref[...] + _dot(mixed, wout_ref[...])
        x1_ref[...] = x1
        r = lax.rsqrt(jnp.mean(x1 * x1, axis=-1, keepdims=True) + EPS)
        h2_ref[...] = (x1 * r * g_ref[...]).astype(BF)

    tile = pl.BlockSpec((TOK_TILE, D_MODEL), lambda i: (i, 0))
    resident = lambda shape: pl.BlockSpec(shape, lambda i: (0, 0), pipeline_mode=pl.Buffered(1))
    f32, bf16 = _out_hbm((SEQ, D_MODEL), F32), _out_hbm((SEQ, D_MODEL), BF)
    return pl.pallas_call(
        body, name="mix_out_fwd", grid=(SEQ // TOK_TILE,),
        in_specs=[pl.BlockSpec((TOK_TILE, POOL_WIDTH), lambda i: (i, 0)), tile,
                  pl.BlockSpec((TOK_TILE, GATE_W), lambda i: (i, C_GATE // GATE_W)), tile,
                  resident((POOL_WIDTH, D_MODEL)), resident((D_MODEL, D_MODEL)), resident((D_MODEL, D_MODEL)),
                  _const_spec((1, GATE_W)), _const_spec((1, D_MODEL)), pl.BlockSpec(memory_space=pl.ANY)],
        out_specs=[tile] * 5, out_shape=[bf16, bf16, bf16, f32, bf16], compiler_params=_params("parallel"),
    )(*map(_in_hbm, (ps, og, zcat, x, w_pool_proj, w_gla_proj, w_out, b_gate, g_ffn)), after)


def _mix_bwd(dx1, w_out, zcat, b_gate, y_pool, y_gla):
    def body(dx_ref, w_ref, zg_ref, b_ref, yp_ref, yg_ref, dz_ref, dyp_ref, dyg_ref, db_ref):
        dm = _dot(dx_ref[...], w_ref[...], tb=True)
        g0 = _sigmoid(zg_ref[:, :D_MODEL].astype(F32) + b_ref[:, :D_MODEL])
        g1 = _sigmoid(zg_ref[:, D_MODEL:].astype(F32) + b_ref[:, D_MODEL:])
        dyp_ref[...] = (dm * g0).astype(BF)
        dyg_ref[...] = (dm * g1).astype(BF)
        dz0 = dm * yp_ref[...].astype(F32) * g0 * (1.0 - g0)
        dz1 = dm * yg_ref[...].astype(F32) * g1 * (1.0 - g1)
        dz_ref[:, :D_MODEL] = dz0.astype(BF)
        dz_ref[:, D_MODEL:] = dz1.astype(BF)
        b0 = jnp.sum(dz0, axis=0, keepdims=True)
        b1 = jnp.sum(dz1, axis=0, keepdims=True)

        @pl.when(pl.program_id(0) == 0)
        def _():
            db_ref[:, :D_MODEL] = b0
            db_ref[:, D_MODEL:] = b1

        @pl.when(pl.program_id(0) > 0)
        def _():
            db_ref[:, :D_MODEL] += b0
            db_ref[:, D_MODEL:] += b1

    tile = pl.BlockSpec((TOK_TILE, D_MODEL), lambda i: (i, 0))
    gspec = pl.BlockSpec((TOK_TILE, GATE_W), lambda i: (i, C_GATE // GATE_W))
    return pl.pallas_call(
        body, name="mix_bwd", grid=(SEQ // TOK_TILE,),
        in_specs=[tile, _const_spec((D_MODEL, D_MODEL)), gspec, _const_spec((1, GATE_W)), tile, tile],
        out_specs=[gspec, tile, tile, _const_spec((1, GATE_W))],
        out_shape=[_out_hbm((SEQ, N_DZ), BF), _out_hbm((SEQ, D_MODEL), BF),
                   _out_hbm((SEQ, D_MODEL), BF), _out_hbm((1, GATE_W), F32)],
        compiler_params=_params("arbitrary"),
    )(*map(_in_hbm, (dx1, w_out, zcat, b_gate, y_pool, y_gla)))


N_TOK_TILES = SEQ // TOK_TILE
HALO_PER_TILE = TOK_TILE // HALO


LANE_TILES = tuple((lo, min(128, FF_BLK - lo)) for lo in range(0, FF_BLK, 128))


def _taps(w_ref, b_ref, half, lanes, rows):
    shape = (rows, lanes.stop - lanes.start)
    return ([jnp.broadcast_to(w_ref[half, j:j + 1, lanes], shape) for j in range(3)],
            jnp.broadcast_to(b_ref[half, :, lanes], shape))


def _conv_strips(u_ref, ub_ref, ua_ref, taps, lanes, width, n_strips, first):
    row = lax.broadcasted_iota(jnp.int32, (HALO, width), 0)
    prev = [[pltpu.roll(jnp.where(first, 0.0, ub_ref[half, :, lanes]), k, 0) for k in (1, 2)] for half in range(2)]
    for s in range(n_strips + (ua_ref is not None)):
        u3, conv = [], []
        for half in range(2):
            cur = u_ref[half, s * HALO:(s + 1) * HALO, lanes] if s < n_strips else ua_ref[half, :, lanes]
            rolled = [pltpu.roll(cur, k, 0) for k in (1, 2)]
            frames = [jnp.where(row >= 2, rolled[1], prev[half][1]), jnp.where(row >= 1, rolled[0], prev[half][0]), cur]
            prev[half] = rolled
            w3, bias = taps[half]
            u3.append(frames)
            conv.append(bias + frames[0] * w3[0] + frames[1] * w3[1] + frames[2] * w3[2])
        yield s, u3, conv


def _pair_specs(pairs):
    tile = pl.BlockSpec((pairs, None, TOK_TILE, FF_BLK), lambda b, i: (0, b, i, 0))
    before = pl.BlockSpec((pairs, None, HALO, FF_BLK), lambda b, i: (0, b, jnp.maximum(i * HALO_PER_TILE - 1, 0), 0))
    after = pl.BlockSpec((pairs, None, HALO, FF_BLK),
                         lambda b, i: (0, b, jnp.minimum((i + 1) * HALO_PER_TILE, SEQ // HALO - 1), 0))

    def vec(rows):
        return pl.BlockSpec((2, None, rows, FF_BLK), lambda b, i: (0, b, 0, 0))

    return tile, before, after, vec


N_STRIPS = TOK_TILE // HALO


def _up_conv_fwd(h2, wt_up, w_conv, b_conv):
    steps = N_TOK_TILES // 2

    def body(h_ref, h_next, wg_ref, wv_ref, w_ref, b_ref, u_ref, a_ref, buf_a, buf_b, carry):
        j = pl.program_id(1)

        def project(hv, buf):
            buf[0] = _dot(hv, wg_ref[...], tb=True)
            buf[1] = _dot(hv, wv_ref[...], tb=True)

        def conv(buf, row0):
            u_ref[:, row0:row0 + TOK_TILE, :] = buf[...]
            for lo, width in LANE_TILES:
                lanes = slice(lo, lo + width)
                taps = [_taps(w_ref, b_ref, half, lanes, HALO) for half in range(2)]
                pending = None
                for s, _, (cg, cv) in _conv_strips(buf, carry, None, taps, lanes, width, N_STRIPS, False):
                    act = cg * _sigmoid(cg) * cv
                    if s % 2 == 0:
                        pending = act
                    else:
                        a_ref[0, row0 + (s - 1) * HALO:row0 + (s + 1) * HALO, lanes] = (
                            jnp.concatenate([pending, act], axis=0).astype(BF))
            carry[...] = buf[:, TOK_TILE - HALO:, :]

        @pl.when(j == 0)
        def _():
            project(h_ref[0:TOK_TILE, :], buf_a)
            carry[...] = jnp.zeros_like(carry)

        project(h_ref[TOK_TILE:, :], buf_b)
        conv(buf_a, 0)
        project(h_next[...], buf_a)
        conv(buf_b, TOK_TILE)

    w_blk = lambda half: pl.BlockSpec((FF_BLK, D_MODEL), lambda b, j: (b + 4 * half, 0))
    vec = lambda rows: pl.BlockSpec((2, None, rows, FF_BLK), lambda b, j: (0, b, 0, 0))
    u_buf = pltpu.VMEM((2, TOK_TILE, FF_BLK), F32)
    return pl.pallas_call(
        body, name="up_conv_fwd", grid=(4, steps),
        in_specs=[pl.BlockSpec((2 * TOK_TILE, D_MODEL), lambda b, j: (j, 0)),
                  pl.BlockSpec((TOK_TILE, D_MODEL), lambda b, j: (jnp.minimum(2 * j + 2, N_TOK_TILES - 1), 0)),
                  w_blk(0), w_blk(1), vec(3), vec(1)],
        out_specs=[pl.BlockSpec((2, None, 2 * TOK_TILE, FF_BLK), lambda b, j: (0, b, j, 0)),
                   pl.BlockSpec((1, None, 2 * TOK_TILE, FF_BLK), lambda b, j: (0, b, j, 0))],
        out_shape=[_out_hbm((2, 4, SEQ, FF_BLK), F32), _out_hbm((1, 4, SEQ, FF_BLK), BF)],
        scratch_shapes=[u_buf, u_buf, pltpu.VMEM((2, HALO, FF_BLK), F32)],
        compiler_params=_params("parallel", "arbitrary"),
    )(*map(_in_hbm, (h2, h2, wt_up, wt_up, w_conv, b_conv)))


def _conv_bwd(u, da, w_conv, b_conv):
    def body(u_ref, ub_ref, ua_ref, da_ref, daa_ref, w_ref, b_ref, du_ref, dw_ref, db_ref):
        i = pl.program_id(1)

        @pl.when(i == 0)
        def _():
            dw_ref[...] = jnp.zeros_like(dw_ref)
            db_ref[...] = jnp.zeros_like(db_ref)

        for lo, width in LANE_TILES:
            lanes = slice(lo, lo + width)
            row = lax.broadcasted_iota(jnp.int32, (HALO, width), 0)
            taps = [_taps(w_ref, b_ref, half, lanes, HALO) for half in range(2)]
            acc_w = [[jnp.zeros((HALO, width), F32) for _ in range(3)] for _ in range(2)]
            acc_b = [jnp.zeros((HALO, width), F32) for _ in range(2)]
            da_pair, pending = None, [None, None]
            dc_prev, up_prev = [None, None], [None, None]
            for s, u3, (cg, cv) in _conv_strips(u_ref, ub_ref, ua_ref, taps, lanes, width, N_STRIPS, i == 0):
                act, dact = _silu_parts(cg)
                if s == N_STRIPS:
                    da = jnp.where(i < N_TOK_TILES - 1, daa_ref[0, :, lanes].astype(F32), 0.0)
                elif s % 2 == 0:
                    da_pair = da_ref[0, s * HALO:(s + 2) * HALO, lanes].astype(F32)
                    da = da_pair[:HALO]
                else:
                    da = da_pair[HALO:]
                dc = (da * cv * dact, da * act)
                for half in range(2):
                    up = [pltpu.roll(dc[half], HALO - k, 0) for k in (1, 2)]
                    if s < N_STRIPS:
                        for j in range(3):
                            acc_w[half][j] = acc_w[half][j] + dc[half] * u3[half][j]
                        acc_b[half] = acc_b[half] + dc[half]
                    if s >= 1:
                        w3 = taps[half][0]
                        du = (dc_prev[half] * w3[2] + jnp.where(row < HALO - 1, up_prev[half][0], up[0]) * w3[1]
                              + jnp.where(row < HALO - 2, up_prev[half][1], up[1]) * w3[0])
                        if (s - 1) % 2 == 0:
                            pending[half] = du
                        else:
                            du_ref[half, (s - 2) * HALO:s * HALO, lanes] = jnp.concatenate([pending[half], du],
                                                                                           axis=0).astype(BF)
                    dc_prev[half], up_prev[half] = dc[half], up
            for half in range(2):
                for j in range(3):
                    dw_ref[half, j:j + 1, lanes] += jnp.sum(acc_w[half][j], axis=0, keepdims=True)
                db_ref[half, :, lanes] += jnp.sum(acc_b[half], axis=0, keepdims=True)

    tile, before, after, vec = _pair_specs(2)
    da_tile, _, da_after_spec, _ = _pair_specs(1)
    return pl.pallas_call(
        body, name="conv_bwd", grid=(4, N_TOK_TILES),
        in_specs=[tile, before, after, da_tile, da_after_spec, vec(3), vec(1)],
        out_specs=[tile, vec(3), vec(1)],
        out_shape=[_out_hbm((2, 4, SEQ, FF_BLK), BF), _out_hbm((2, 4, 3, FF_BLK), F32),
                   _out_hbm((2, 4, 1, FF_BLK), F32)],
        compiler_params=_params("parallel", "arbitrary"),
    )(*map(_in_hbm, (u, u, u, da, da, w_conv, b_conv)))


W_IN_SEGMENTS = ((R_POOL, POOL_WIDTH, C_POOL), (R_QKV, QKV_W, C_QKV), (R_OG, D_MODEL, C_OG), (R_GK, GATE_RANK, C_GK),
                 (R_GATE, GATE_W, C_GATE))


def _slab_pieces(d):
    lo, hi = d * IN_SHARD, (d + 1) * IN_SHARD
    pieces = []
    for start, n, at in W_IN_SEGMENTS:
        a, b = max(lo, start), min(hi, start + n)
        if a < b:
            assert (a - lo) % 2 == 0 and (b - a) % 2 == 0 and (at + a - start) % 2 == 0
            pieces.append(((a - lo) // 2, (b - a) // 2, (at + a - start) // 2))
    return pieces


def _unshard_w_in(slabs):
    def body(slab_ref, cat_ref):
        d = pl.program_id(0)
        src = slab_ref.bitcast(jnp.uint32)
        dst = cat_ref.bitcast(jnp.uint32)

        @pl.when(d == 0)
        def _():
            cat_ref[C_GK:, :] = jnp.zeros((GK_PAD, D_MODEL), BF)

        for dd in range(N_DEV):
            @pl.when(d == dd)
            def _():
                for a, n, at in _slab_pieces(dd):
                    dst[pl.ds(at, n), :] = src[0, pl.ds(a, n), :]

    return pl.pallas_call(
        body, name="unshard_w_in", grid=(N_DEV,),
        in_specs=[pl.BlockSpec((1, IN_SHARD, D_MODEL), lambda d: (d, 0, 0))], out_specs=_const_spec((N_DZ, D_MODEL)),
        out_shape=_out_hbm((N_DZ, D_MODEL), BF), compiler_params=_params("arbitrary"),
    )(_in_hbm(slabs))


def _shard_d_w_in(d_cat):
    def body(cat_ref, slab_ref):
        d = pl.program_id(0)
        cat = cat_ref.bitcast(jnp.uint32)
        dst = slab_ref.bitcast(jnp.uint32)
        for dd in range(N_DEV):
            @pl.when(d == dd)
            def _():
                for a, n, at in _slab_pieces(dd):
                    dst[0, pl.ds(a, n), :] = cat[pl.ds(at, n), :]

    return pl.pallas_call(
        body, name="shard_d_w_in", grid=(N_DEV,), in_specs=[_const_spec((N_DZ, D_MODEL))],
        out_specs=pl.BlockSpec((1, IN_SHARD, D_MODEL), lambda d: (d, 0, 0)),
        out_shape=_out_hbm((N_DEV, IN_SHARD, D_MODEL), BF), compiler_params=_params("parallel"),
    )(_in_hbm(d_cat))


ANY = pl.BlockSpec(memory_space=pl.ANY)


def _place():
    x, y, c = lax.axis_index("x"), lax.axis_index("y"), lax.axis_index("c")
    other_chips = [(1 - x, y), (x, 1 - y), (1 - x, 1 - y)]
    return x, y, c, other_chips


SEM = pl.BlockSpec(memory_space=pltpu.SEMAPHORE)
IN_HBM = pl.BlockSpec(memory_space=pltpu.HBM)
SPLIT_PARAMS = pltpu.CompilerParams(has_side_effects=pltpu.SideEffectType.DATAFLOW_SIDE_EFFECTING)


def _gather_first(refs, send_sems, recv_sems):
    x, y, c, chips = _place()
    targets = [(x, y, 1 - c)] + [(px, py, c) for px, py in chips]
    return [pltpu.make_async_remote_copy(src_ref=refs[2 * a], dst_ref=refs[2 * a + 1].at[4 * x + 2 * y + c],
                                         send_sem=send_sems.at[4 * a + k], recv_sem=recv_sems.at[4 * a + k],
                                         device_id=to, device_id_type=MESH)
            for a in range(len(refs) // 2) for k, to in enumerate(targets)]


def _gather_direct(refs, send_sems, recv_sems):
    x, y, c, _ = _place()
    flips = [(dx, dy, dc) for dx in (0, 1) for dy in (0, 1) for dc in (0, 1) if dx + dy + dc]
    targets = [(1 - x if dx else x, 1 - y if dy else y, 1 - c if dc else c) for dx, dy, dc in flips]
    return [pltpu.make_async_remote_copy(src_ref=refs[2 * a], dst_ref=refs[2 * a + 1].at[4 * x + 2 * y + c],
                                         send_sem=send_sems.at[7 * a + k], recv_sem=recv_sems.at[7 * a + k],
                                         device_id=to, device_id_type=MESH)
            for a in range(len(refs) // 2) for k, to in enumerate(targets)]


def _gather_second(refs, send_sems, recv_sems):
    x, y, c, chips = _place()
    copies = []
    for a, land in enumerate(refs):
        for j, (px, py) in enumerate(chips):
            block = land.at[4 * px + 2 * py + c]
            copies.append(pltpu.make_async_remote_copy(src_ref=block, dst_ref=block, send_sem=send_sems.at[3 * a + j],
                                                       recv_sem=recv_sems.at[3 * a + j], device_id=(x, y, 1 - c),
                                                       device_id_type=MESH))
    return copies


def _reduce_first(refs, send_sems, recv_sems):
    x, y, c, _ = _place()
    return [pltpu.make_async_remote_copy(src_ref=refs[2 * a].at[j, 1 - c], dst_ref=refs[2 * a + 1].at[j],
                                         send_sem=send_sems.at[4 * a + j], recv_sem=recv_sems.at[4 * a + j],
                                         device_id=(x, y, 1 - c), device_id_type=MESH)
            for a in range(len(refs) // 2) for j in range(4)]


def _reduce_second(refs, send_sems, recv_sems):
    _, _, c, chips = _place()
    return [pltpu.make_async_remote_copy(src_ref=refs[2 * a].at[2 * px + py], dst_ref=refs[2 * a + 1].at[k],
                                         send_sem=send_sems.at[3 * a + k], recv_sem=recv_sems.at[3 * a + k],
                                         device_id=(px, py, c), device_id_type=MESH)
            for a in range(len(refs) // 2) for k, (px, py) in enumerate(chips)]


def _split_start(name, groups):
    arrays = [a for g in groups for a in g[0]]
    n = len(arrays)

    def body(*refs):
        sems = refs[n:n + 2 * len(groups)]
        at = 0
        for gi, (members, _, build) in enumerate(groups):
            for cp in build(refs[at:at + len(members)], sems[2 * gi], sems[2 * gi + 1]):
                cp.start()
            at += len(members)
        refs[-1][...] = jnp.zeros_like(refs[-1])

    sem_shapes = [pltpu.SemaphoreType.DMA((g[1],)) for g in groups for _ in range(2)]
    outs = pl.pallas_call(
        body, name=name, in_specs=[IN_HBM] * n,
        out_shape=(*sem_shapes, *[_out_hbm(a.shape, a.dtype) for a in arrays], jax.ShapeDtypeStruct((8, 128), F32)),
        out_specs=(*[SEM] * len(sem_shapes), *[IN_HBM] * n, pl.BlockSpec(memory_space=pltpu.VMEM)),
        input_output_aliases={i: len(sem_shapes) + i for i in range(n)}, compiler_params=SPLIT_PARAMS,
    )(*[pltpu.with_memory_space_constraint(a, pltpu.HBM) for a in arrays])
    per_group, at = [], len(sem_shapes)
    for gi, (members, _, _) in enumerate(groups):
        per_group.append((outs[2 * gi], outs[2 * gi + 1], list(outs[at:at + len(members)])))
        at += len(members)
    return per_group, outs[-1]


def _split_wait(name, started, build, after):
    send_sems, recv_sems, arrays = started
    n = len(arrays)
    after = after if isinstance(after, (tuple, list)) else (after,)

    def body(*refs):
        for cp in build(refs[:n], refs[n], refs[n + 1]):
            cp.wait_send()
            cp.wait_recv()

    return pl.pallas_call(
        body, name=name, in_specs=[IN_HBM] * n + [SEM, SEM] + [ANY] * len(after),
        out_shape=tuple(_out_hbm(a.shape, a.dtype) for a in arrays), out_specs=tuple([IN_HBM] * n),
        input_output_aliases={i: i for i in range(n)}, compiler_params=SPLIT_PARAMS,
    )(*arrays, send_sems, recv_sems, *after)


def _gather_landing(shard, me):
    return lax.dynamic_update_slice(lax.empty((N_DEV,) + shard.shape, shard.dtype), shard[None],
                                    (me,) + (0,) * shard.ndim)


ADAM_LANE_TILE = 256


def _tile_2d(rows, cols):
    for t in (256, 176, 128):
        if rows % t == 0:
            return t, cols
    return rows, ADAM_LANE_TILE


def _pair_sum(part, recv, core, name):
    _, rows, cols = recv.shape
    tr, tc = rows, cols

    def body(c_ref, p_ref, r_ref, o_ref):
        del c_ref
        o_ref[...] = (p_ref[...].astype(F32) + r_ref[...].astype(F32)).astype(BF)

    grid_spec = pltpu.PrefetchScalarGridSpec(
        num_scalar_prefetch=1, grid=(4, rows // tr, cols // tc),
        in_specs=[pl.BlockSpec((None, None, tr, tc), lambda j, i, k, c_ref: (j, c_ref[0], i, k)),
                  pl.BlockSpec((None, tr, tc), lambda j, i, k, c_ref: (j, i, k))],
        out_specs=pl.BlockSpec((None, tr, tc), lambda j, i, k, c_ref: (j, i, k)))
    return pl.pallas_call(
        body, name=name, grid_spec=grid_spec, out_shape=_out_hbm(recv.shape, BF),
        compiler_params=_params("parallel", "parallel", "parallel"),
    )(core, *map(_in_hbm, (part, recv)))


def _adamw(w, g, m, v):
    m = ADAM_B1 * m + (1.0 - ADAM_B1) * g
    v = ADAM_B2 * v + (1.0 - ADAM_B2) * (g * g)
    delta = -ADAM_LR * ((m / ADAM_C1) / (jnp.sqrt(v / ADAM_C2) + ADAM_EPS) + ADAM_WD * w)
    return delta, m, v


def _chip_sum_adamw(sums, recv, w, m, v, chip, name):
    rows, cols = w.shape
    tr, tc = _tile_2d(rows, cols)

    def body(chip_ref, s_ref, r_ref, w_ref, m_ref, v_ref, g_out, d_out, m_out, v_out):
        del chip_ref
        g = s_ref[...].astype(F32)
        for k in range(3):
            g = g + r_ref[k].astype(F32)
        g_out[...] = g
        d_out[...], m_out[...], v_out[...] = _adamw(w_ref[...], g, m_ref[...], v_ref[...])

    tile = pl.BlockSpec((tr, tc), lambda i, k, chip_ref: (i, k))
    grid_spec = pltpu.PrefetchScalarGridSpec(
        num_scalar_prefetch=1, grid=(rows // tr, cols // tc),
        in_specs=[pl.BlockSpec((None, tr, tc), lambda i, k, chip_ref: (chip_ref[0], i, k)),
                  pl.BlockSpec((3, tr, tc), lambda i, k, chip_ref: (0, i, k)), tile, tile, tile],
        out_specs=[tile] * 4)
    return pl.pallas_call(
        body, name=name, grid_spec=grid_spec, out_shape=[_out_hbm((rows, cols), F32)] * 4,
        compiler_params=_params("parallel", "parallel"),
    )(chip, *map(_in_hbm, (sums, recv, w, m, v)))


def _small_sum_adamw(me, entries, loss_parts):
    def whole(shape, squeeze=0, pick=False):
        blk = (None,) * squeeze + tuple(shape[squeeze:])
        if pick:
            blk = (shape[0], None) + tuple(shape[2:])
            return pl.BlockSpec(blk, lambda i, me_ref: (0, me_ref[0]) + (0,) * (len(shape) - 2))
        return pl.BlockSpec(blk, lambda i, me_ref: (0,) * len(shape))

    in_specs, out_specs, out_shape, args = [], [], [], []
    for parts, w, m, v, sharded in entries:
        lead = w.ndim - (parts.ndim - (2 if sharded else 1))
        in_specs += [whole(parts.shape, pick=sharded)] + [whole(w.shape, squeeze=lead)] * 3
        out_specs += [whole(w.shape, squeeze=lead)] * 4
        out_shape += [_out_hbm(w.shape, F32)] * 4
        args += [parts, w, m, v]
    in_specs.append(whole(loss_parts.shape))
    out_specs.append(whole(loss_parts.shape[1:]))
    out_shape.append(_out_hbm(loss_parts.shape[1:], F32))
    n = len(entries)

    def added(p_ref):
        total = p_ref[0]
        for d in range(1, N_DEV):
            total = total + p_ref[d]
        return total

    def body(me_ref, *refs):
        del me_ref
        ins, outs = refs[:4 * n + 1], refs[4 * n + 1:]
        for e in range(n):
            p_ref, w_ref, m_ref, v_ref = ins[4 * e:4 * e + 4]
            g_out, d_out, m_out, v_out = outs[4 * e:4 * e + 4]
            g = added(p_ref)
            g_out[...] = g
            d_out[...], m_out[...], v_out[...] = _adamw(w_ref[...], g, m_ref[...], v_ref[...])
        outs[4 * n][...] = added(ins[4 * n])

    grid_spec = pltpu.PrefetchScalarGridSpec(num_scalar_prefetch=1, grid=(1,), in_specs=in_specs, out_specs=out_specs)
    outs = pl.pallas_call(body, name="small_sum_adamw", grid_spec=grid_spec, out_shape=out_shape,
                          compiler_params=_params("arbitrary"))(me, *map(_in_hbm, args + [loss_parts]))
    return [outs[4 * e:4 * e + 4] for e in range(n)], outs[4 * n]


MM_TILE = 512
N_MM_TILES = SEQ // MM_TILE
CAT_TILE = 512
N_CAT_TILES = N_CAT // CAT_TILE
DZ_TILE = 640


def kernel(x, g_mix, w_in, b_gate, w_gk_up, b_gk, w_pool_grp, pool_scale, g_gla_head, w_pool_proj, w_gla_proj, w_out, g_ffn, w_up, w_conv, b_conv, w_down, g_final, loss_target, m_g_mix, m_w_in, m_b_gate, m_w_gk_up, m_b_gk, m_w_pool_grp, m_pool_scale, m_g_gla_head, m_w_pool_proj, m_w_gla_proj, m_w_out, m_g_ffn, m_w_up, m_w_conv, m_b_conv, m_w_down, m_g_final, v_g_mix, v_w_in, v_b_gate, v_w_gk_up, v_b_gk, v_w_pool_grp, v_pool_scale, v_g_gla_head, v_w_pool_proj, v_w_gla_proj, v_w_out, v_g_ffn, v_w_up, v_w_conv, v_b_conv, v_w_down, v_g_final):
    xi, yi, ci = lax.axis_index("x"), lax.axis_index("y"), lax.axis_index("c")
    me = 4 * xi + 2 * yi + ci
    core = jnp.reshape(ci, (1,)).astype(jnp.int32)
    chip = jnp.reshape(2 * xi + yi, (1,)).astype(jnp.int32)
    xs, target = x[0], loss_target[0]

    big = dict(w_in=w_in[0].T, w_pool_proj=w_pool_proj[0], w_gla_proj=w_gla_proj[0], w_out=w_out[0], w_up=w_up[0].T,
               w_down=w_down[0])
    moments = dict(w_in=(m_w_in[0].T, v_w_in[0].T), w_pool_proj=(m_w_pool_proj[0], v_w_pool_proj[0]),
                   w_gla_proj=(m_w_gla_proj[0], v_w_gla_proj[0]), w_out=(m_w_out[0], v_w_out[0]),
                   w_up=(m_w_up[0].T, v_w_up[0].T), w_down=(m_w_down[0], v_w_down[0]))
    names = list(big)
    shards = {k: big[k].astype(BF) for k in names}
    shards["w_gk_up"], shards["w_conv"] = w_gk_up[0], w_conv[0]
    gather_groups = (("w_in", "w_gk_up"), ("w_pool_proj", "w_gla_proj", "w_out"), ("w_up", "w_down", "w_conv"))
    over_ici = lambda g: ([t for k in g for t in (shards[k], _gather_landing(shards[k], me))], 4 * len(g), _gather_first)
    started, token = _split_start("gather_start", [over_ici(gather_groups[0])])

    def gather_pass(gi, after):
        lands = list(_split_wait(f"gather_wait_{gi}", started[gi], _gather_first, after)[1::2])
        later = [over_ici(g) for g in gather_groups[1:]] if gi == 0 else []
        passed, tkn = _split_start(f"gather_pass_{gi}", [(lands, 3 * len(lands), _gather_second)] + later)
        started.extend(passed[1:])
        return passed[0], tkn

    def gather_done(gi, passed, after):
        return dict(zip(gather_groups[gi], _split_wait(f"gather_pass_wait_{gi}", passed, _gather_second, after)))

    tok = lambda i, j, k: (i, 0)
    whole = lambda i, j, k: (0, 0)
    kblk = lambda i, j, k: (k, 0)
    ff_seq = (None, None, SEQ, FF_BLK)

    h = _rms_fwd(xs, g_mix + token[:1, :1], "rms_mix")
    wg = gather_done(0, gather_pass(0, h)[0], h)
    wt_cat = _unshard_w_in(wg["w_in"])
    wgk_pad = jnp.pad(wg["w_gk_up"].transpose(1, 0, 2).reshape(GATE_RANK, GLA_DK), ((0, GK_PAD - GATE_RANK), (0, 0)))
    zcat = _mm(h, wt_cat, out_shape=(SEQ, N_CAT), out_dtype=BF, grid=(N_CAT_TILES, 1, 1),
               blk_a=(SEQ, D_MODEL), blk_b=(CAT_TILE, D_MODEL), blk_o=(SEQ, CAT_TILE),
               map_a=whole, map_b=lambda j, i, k: (j, 0), map_o=lambda j, i, k: (0, j), tb=True, name="mm_in")
    la = _gk_fwd(h, wt_cat, wgk_pad, b_gk)
    passed, tkn = gather_pass(1, la)
    o, states = _gla_fwd(zcat, la, tkn)
    wg = gather_done(1, passed, o)
    wpp = wg["w_pool_proj"].transpose(1, 0, 2).reshape(POOL_WIDTH, D_MODEL)
    wgp = wg["w_gla_proj"].reshape(D_MODEL, D_MODEL)
    wout = wg["w_out"].reshape(D_MODEL, D_MODEL)
    og = _post_gla_fwd(o, zcat, g_gla_head)
    ps = _pool_fwd(zcat, w_pool_grp[0], pool_scale)
    passed, tkn = gather_pass(2, (og, ps))
    y_pool, y_gla, mixed, x1, h2 = _mix_out_fwd(ps, og, zcat, xs, wpp, wgp, wout, b_gate, g_ffn, tkn)
    wg = gather_done(2, passed, h2)
    wt_up = wg["w_up"].reshape(2 * D_FF, D_MODEL)
    wdown = wg["w_down"].reshape(D_FF, D_MODEL)
    wconv4 = wg["w_conv"].reshape(2, 4, 3, FF_BLK)
    bconv4 = b_conv.reshape(2, 4, 1, FF_BLK)
    blk4 = lambda b, i, k: (b // 4, b % 4, 0, 0)
    u4, act = _up_conv_fwd(h2, wt_up, wconv4, bconv4)
    loss_part, dx2, dx2_bf, dg_final = _mm_tokens(
        act, wdown, blk_a=(None, 4, TOK_MM_TILE, FF_BLK), map_a=lambda i: (0, 0, i, 0),
        pieces=[(b, b * FF_BLK, FF_BLK) for b in range(4)], res=x1, then=("loss", g_final.reshape(1, D_MODEL), target),
        name="mm_down_loss")

    da = _mm(dx2_bf, wdown, out_shape=(1, 4, SEQ, FF_BLK), out_dtype=BF, grid=(4, 1, 1),
             blk_a=(SEQ, D_MODEL), blk_b=(FF_BLK, D_MODEL), blk_o=ff_seq,
             map_a=whole, map_b=lambda b, i, k: (b, 0), map_o=lambda b, i, k: (0, b, 0, 0), tb=True, name="mm_d_act")
    d_wdown = _mm(act, dx2_bf, out_shape=(D_FF, D_MODEL), out_dtype=BF, grid=(4, 1, 1),
                  blk_a=ff_seq, blk_b=(SEQ, D_MODEL), blk_o=(FF_BLK, D_MODEL),
                  map_a=lambda b, i, k: (0, b, 0, 0), map_b=whole, map_o=lambda b, i, k: (b, 0), ta=True,
                  name="mm_d_wdown")
    du4, d_wconv, d_bconv = _conv_bwd(u4, da, wconv4, bconv4)
    d_wt_up = _mm(du4, h2, out_shape=(2 * D_FF, D_MODEL), out_dtype=BF, grid=(N_DEV, 1, 1),
                  blk_a=ff_seq, blk_b=(SEQ, D_MODEL), blk_o=(FF_BLK, D_MODEL),
                  map_a=blk4, map_b=whole, map_o=lambda b, i, k: (b, 0), ta=True, name="mm_d_wup")
    res = {}

    def to_sibling(keys, parts):
        return [t for k in keys for t in (parts[k], lax.empty((4,) + parts[k].shape[2:], BF))], 4 * len(keys), _reduce_first

    def to_chips(keys, st, after):
        arrays = _split_wait("reduce_wait_" + keys[0], st, _reduce_first, after)
        sums = [_pair_sum(p, r, core, "pair_sum_" + k) for k, p, r in zip(keys, arrays[0::2], arrays[1::2])]
        return [t for s in sums for t in (s, lax.empty((3,) + s.shape[1:], BF))], 3 * len(keys), _reduce_second

    def reduce_start(keys, parts):
        st, tkn = _split_start("reduce_start_" + keys[0], [to_sibling(keys, parts)])
        return st[0], tkn

    def reduce_cross(keys, st, after):
        st2, tkn = _split_start("reduce_cross_" + keys[0], [to_chips(keys, st, after)])
        return st2[0], tkn

    def reduce_done(keys, st2, after):
        arrays = _split_wait("reduce_cross_wait_" + keys[0], st2, _reduce_second, after)
        for k, s, r in zip(keys, arrays[0::2], arrays[1::2]):
            outs = _chip_sum_adamw(s, r, big[k], moments[k][0], moments[k][1], chip, "adamw_" + k)
            res[k] = [(t.T if k in ("w_in", "w_up") else t)[None] for t in outs]

    ffn_keys = ("w_down", "w_up")
    ffn_red, tkn = reduce_start(ffn_keys, dict(w_down=d_wdown.reshape(4, 2, D_FF // N_DEV, D_MODEL),
                                               w_up=d_wt_up.reshape(4, 2, FF_BLK, D_MODEL)))
    dx1, dg_ffn = _mm_tokens(
        du4, wt_up, blk_a=(2, 4, TOK_MM_TILE, FF_BLK), map_a=lambda i: (0, 0, i, 0),
        pieces=[((b // 4, b % 4), b * FF_BLK, FF_BLK) for b in range(N_DEV)], after=tkn, then=("rms_bwd", x1, g_ffn, dx2),
        name="mm_d_h2_rms")

    sq_t = dict(out_shape=(D_MODEL, D_MODEL), grid=(1, 1, N_MM_TILES), blk_a=(MM_TILE, D_MODEL),
                blk_b=(MM_TILE, D_MODEL), blk_o=(D_MODEL, D_MODEL), map_a=kblk, map_b=kblk, map_o=whole, ta=True)
    d_wout = _mm(mixed, dx1, out_dtype=BF, name="mm_d_wout", **sq_t)
    dzcat, dy_pool, dy_gla, db_gate = _mix_bwd(dx1, wout, zcat, b_gate, y_pool, y_gla)
    d_wgp = _mm(og, dy_gla, out_dtype=BF, name="mm_d_wgp", **sq_t)
    mix_keys = ("w_out", "w_gla_proj")
    (ffn_red, mix_red), tkn = _split_start("reduce_cross_w_down", [
        to_chips(ffn_keys, ffn_red, db_gate),
        to_sibling(mix_keys, dict(w_out=d_wout.reshape(4, 2, D_MODEL // N_DEV, D_MODEL),
                                  w_gla_proj=d_wgp.reshape(4, 2, D_MODEL // N_DEV, D_MODEL)))])
    dzcat, d_o, dg_head = _post_gla_bwd(dzcat, dy_gla, wgp, o, zcat, g_gla_head + tkn[:1, :1])
    dzcat, dla = _gla_bwd(dzcat, zcat, la, d_o, states)
    dzcat, d_wgk, db_gk = _gk_bwd(dzcat, dla, h, wt_cat, wgk_pad, b_gk)
    dps = _mm(dy_pool, wpp, out_shape=(SEQ, POOL_WIDTH), out_dtype=F32, grid=(N_MM_TILES, 1, 1),
              blk_a=(MM_TILE, D_MODEL), blk_b=(POOL_WIDTH, D_MODEL), blk_o=(MM_TILE, POOL_WIDTH),
              map_a=tok, map_b=whole, map_o=tok, tb=True, name="mm_d_ps")
    d_wpp = _mm(ps, dy_pool, out_shape=(POOL_WIDTH, D_MODEL), out_dtype=F32, grid=(1, 1, N_MM_TILES),
                blk_a=(MM_TILE, POOL_WIDTH), blk_b=(MM_TILE, D_MODEL), blk_o=(POOL_WIDTH, D_MODEL),
                map_a=kblk, map_b=kblk, map_o=whole, ta=True, name="mm_d_wpp")
    dzcat, d_wgrp, d_scale = _pool_bwd(dzcat, zcat, dps, w_pool_grp[0], pool_scale)
    row = lambda t: t.reshape(1, D_MODEL)
    conv_vec = lambda t: t.reshape(2, 4, 1, FF_BLK)
    small = [("b_gate", db_gate, b_gate, m_b_gate, v_b_gate, False),
             ("w_gk_up", d_wgk.reshape(GATE_RANK, N_DEV, GLA_DK // N_DEV).transpose(1, 0, 2), w_gk_up, m_w_gk_up,
              v_w_gk_up, True),
             ("b_gk", db_gk, b_gk, m_b_gk, v_b_gk, False),
             ("w_pool_grp", d_wgrp, w_pool_grp, m_w_pool_grp, v_w_pool_grp, False),
             ("pool_scale", d_scale, pool_scale, m_pool_scale, v_pool_scale, False),
             ("g_gla_head", dg_head, g_gla_head, m_g_gla_head, v_g_gla_head, False),
             ("g_ffn", dg_ffn, g_ffn, m_g_ffn, v_g_ffn, False),
             ("w_conv", d_wconv.reshape(N_DEV, 3, FF_BLK), w_conv, m_w_conv, v_w_conv, True),
             ("b_conv", d_bconv, conv_vec(b_conv), conv_vec(m_b_conv), conv_vec(v_b_conv), False),
             ("g_final", dg_final, row(g_final), row(m_g_final), row(v_g_final), False)]

    def to_all(parts):
        return [t for p in parts for t in (p, _gather_landing(p, me))], 7 * len(parts), _gather_direct

    (small_sent, mix_red), tkn = _split_start("small_start", [to_all([t[1] for t in small] + [loss_part]),
                                                              to_chips(mix_keys, mix_red, dla)])
    d_wt_cat = _mm(dzcat, h, out_shape=(N_DZ, D_MODEL), out_dtype=BF, grid=(N_DZ // DZ_TILE, 1, 1),
                   blk_a=(SEQ, DZ_TILE), blk_b=(SEQ, D_MODEL), blk_o=(DZ_TILE, D_MODEL),
                   map_a=lambda j, i, k: (0, j), map_b=whole, map_o=lambda j, i, k: (j, 0), ta=True, after=tkn,
                   name="mm_d_wcat")
    in_keys = ("w_in", "w_pool_proj")
    in_red, tkn = reduce_start(in_keys, dict(
        w_in=_shard_d_w_in(d_wt_cat).reshape(4, 2, IN_SHARD, D_MODEL),
        w_pool_proj=d_wpp.reshape(POOL_WIDTH, N_DEV, D_MODEL // N_DEV).transpose(1, 0, 2).astype(BF)
        .reshape(4, 2, POOL_WIDTH, D_MODEL // N_DEV)))
    reduce_done(mix_keys, mix_red, tkn)
    in_red, tkn = reduce_cross(in_keys, in_red, res["w_out"][0])
    grad_x, dg_mix = _mm_tokens(dzcat, wt_cat, blk_a=(TOK_MM_TILE, N_DZ), map_a=lambda i: (i, 0),
                                pieces=[(None, 0, N_DZ)], after=tkn, then=("rms_bwd", xs, g_mix, dx1),
                                name="mm_d_h_rms")
    (g_mix_sent,), tkn = _split_start("g_mix_start", [to_all([dg_mix])])
    reduce_done(ffn_keys, ffn_red, (grad_x, tkn))
    gathered = _split_wait("small_wait", small_sent, _gather_direct, res["w_down"][0])[1::2]
    small.append(("g_mix", dg_mix, g_mix, m_g_mix, v_g_mix, False))
    gathered = list(gathered[:-1]) + [_split_wait("g_mix_wait", g_mix_sent, _gather_direct, gathered[0])[1], gathered[-1]]
    small_out, loss_sum = _small_sum_adamw(jnp.reshape(me, (1,)).astype(jnp.int32),
                                           [(p,) + t[2:] for p, t in zip(gathered, small)], gathered[-1])
    for t, outs in zip(small, small_out):
        res[t[0]] = list(outs)
    res["b_conv"] = [t.reshape(b_conv.shape) for t in res["b_conv"]]
    res["g_final"] = [t.reshape(g_final.shape) for t in res["g_final"]]

    reduce_done(in_keys, in_red, loss_sum)
    loss = loss_sum[0, 0]
    order =["g_mix", "w_in", "b_gate", "w_gk_up", "b_gk", "w_pool_grp", "pool_scale", "g_gla_head", "w_pool_proj",
             "w_gla_proj", "w_out", "g_ffn", "w_up", "w_conv", "b_conv", "w_down", "g_final"]
    return (loss, grad_x[None], *[res[k][0] for k in order], *[res[k][1] for k in order],
            *[res[k][2] for k in order], *[res[k][3] for k in order])
```

```python
import jax
import jax.numpy as jnp
from jax import lax
from jax.experimental import pallas as pl
from jax.experimental.pallas import tpu as pltpu

F32 = jnp.float32
BF = jnp.bfloat16
HIGHEST = lax.Precision.HIGHEST
MESH = pl.DeviceIdType.MESH

N_DEV = 8
SEQ = 2048
D_MODEL = 1024
CHUNK = 64
EPS = 1e-6
POOL_WIDTH = 512
POOL_WINDOWS = (2, 4, 8, 16)
POOL_GD = 128
POOL_HALO = 16
HEADS = 4
HK = 128
HV = 256
GLA_DK = 512
GATE_RANK = 16
GATE_NORM = 16.0
D_FF = 2816
FF_BLK = 704
IN_SHARD = 706
C_QKV, C_GATE, C_OG, C_POOL, C_GK = 0, 2048, 4096, 5120, 5632
N_CAT = 5632
GK_PAD = 128
N_DZ = N_CAT + GK_PAD
R_POOL, R_QKV, R_OG, R_GK, R_GATE = 0, 512, 2560, 3584, 3600

ADAM_LR, ADAM_B1, ADAM_B2, ADAM_EPS, ADAM_WD, ADAM_STEP = 0.001, 0.9, 0.999, 1e-08, 0.01, 10
ADAM_C1 = 1.0 - ADAM_B1 ** ADAM_STEP
ADAM_C2 = 1.0 - ADAM_B2 ** ADAM_STEP

VMEM_BYTES_V7X = 64 * 1024 * 1024
VMEM_LIMIT = VMEM_BYTES_V7X * 3 // 4

TOK_TILE = 256
HALO = 8
GLA_CPS = 4


def _params(*sem):
    return pltpu.CompilerParams(dimension_semantics=sem, vmem_limit_bytes=VMEM_LIMIT)


def _const_spec(shape):
    nd = len(shape)
    return pl.BlockSpec(shape, lambda *_: (0,) * nd)


def _in_hbm(t):
    return pltpu.with_memory_space_constraint(t, pltpu.HBM)


def _out_hbm(shape, dtype):
    return pltpu.HBM(shape, dtype)


def _dot(a, b, ta=False, tb=False):
    dims = (((0 if ta else 1,), (1 if tb else 0,)), ((), ()))
    return lax.dot_general(a.astype(BF), b.astype(BF), dims, preferred_element_type=F32)


def _dot_exact(a, b):
    return jnp.dot(a, b, precision=HIGHEST, preferred_element_type=F32)


def _sigmoid(x):
    return 0.5 * jnp.tanh(0.5 * x) + 0.5


def _mm(a, b, *, out_shape, out_dtype, grid, blk_a, blk_b, blk_o, map_a, map_b, map_o, ta=False, tb=False,
        after=None, name):
    gk = grid[2]
    n_in = 2 + (after is not None)

    def body(*refs):
        a_ref, b_ref, o_ref = refs[0], refs[1], refs[n_in]
        prod = _dot(a_ref[...], b_ref[...], ta, tb)
        if gk == 1:
            o_ref[...] = prod.astype(out_dtype)
        else:
            acc = refs[n_in + 1]
            k = pl.program_id(2)

            @pl.when(k == 0)
            def _():
                acc[...] = prod

            @pl.when(k > 0)
            def _():
                acc[...] += prod

            @pl.when(k == gk - 1)
            def _():
                o_ref[...] = acc[...].astype(out_dtype)

    in_specs = [pl.BlockSpec(blk_a, map_a), pl.BlockSpec(blk_b, map_b)]
    args = [_in_hbm(a), _in_hbm(b)]
    if after is not None:
        in_specs.append(pl.BlockSpec(memory_space=pl.ANY))
        args.append(after)
    return pl.pallas_call(
        body, name=name, grid=grid, in_specs=in_specs, out_specs=pl.BlockSpec(blk_o, map_o),
        out_shape=_out_hbm(out_shape, out_dtype),
        scratch_shapes=[] if gk == 1 else [pltpu.VMEM(tuple(d for d in blk_o if d is not None), F32)],
        compiler_params=_params("parallel", "parallel", "arbitrary"),
    )(*args)


TOK_MM_TILE = 256


def _mm_tokens(a, w, *, blk_a, map_a, pieces, res=None, after=None, then=None, name):
    n_in = 2 + (res is not None) + (after is not None) + (0 if then is None else len(then) - 1)

    def accumulate(ref, part):
        @pl.when(pl.program_id(0) == 0)
        def _():
            ref[...] = part

        @pl.when(pl.program_id(0) > 0)
        def _():
            ref[...] += part

    def body(*refs):
        a_ref, w_ref = refs[:2]
        extra, outs = refs[n_in - (0 if then is None else len(then) - 1):n_in], refs[n_in:]
        total = None
        for idx, row, n in pieces:
            av = a_ref[...] if idx is None else a_ref[idx]
            prod = _dot(av, w_ref[row:row + n, :])
            total = prod if total is None else total + prod
        if res is not None:
            total = total + refs[2][...]
        if then is None:
            outs[0][...] = total
        elif then[0] == "rms_bwd":
            dx, part = _rms_bwd_tile(total, extra[0][...], extra[1][...], extra[2][...])
            outs[0][...] = dx
            accumulate(outs[1], part)
        else:
            lpart, dx, part = _loss_tile(total, extra[0][...], extra[1][...])
            outs[1][...] = dx
            outs[2][...] = dx.astype(BF)
            accumulate(outs[0], lpart)
            accumulate(outs[3], part)

    tile = pl.BlockSpec((TOK_MM_TILE, D_MODEL), lambda i: (i, 0))
    vec = _const_spec((1, D_MODEL))
    big = _out_hbm((SEQ, D_MODEL), F32)
    small = _out_hbm((1, D_MODEL), F32)
    in_specs = [pl.BlockSpec(blk_a, map_a), pl.BlockSpec(w.shape, lambda i: (0, 0), pipeline_mode=pl.Buffered(1))]
    args = [a, w]
    if res is not None:
        in_specs.append(tile)
        args.append(res)
    if after is not None:
        in_specs.append(pl.BlockSpec(memory_space=pl.ANY))
        args.append(after)
    if then is None:
        out_specs, out_shape = tile, big
    elif then[0] == "rms_bwd":
        in_specs += [tile, vec, tile]
        out_specs, out_shape = [tile, vec], [big, small]
    else:
        in_specs += [vec, tile]
        out_specs = [_const_spec((1, 128)), tile, tile, vec]
        out_shape = [_out_hbm((1, 128), F32), big, _out_hbm((SEQ, D_MODEL), BF), small]
    if then is not None:
        args += list(then[1:])
    return pl.pallas_call(
        body, name=name, grid=(SEQ // TOK_MM_TILE,), in_specs=in_specs, out_specs=out_specs, out_shape=out_shape,
        compiler_params=_params("parallel" if then is None else "arbitrary"),
    )(*[_in_hbm(t) for t in args])


def _rms_fwd(x, g, name):
    def body(x_ref, g_ref, o_ref):
        xv = x_ref[...]
        r = lax.rsqrt(jnp.mean(xv * xv, axis=-1, keepdims=True) + EPS)
        o_ref[...] = (xv * r * g_ref[...]).astype(BF)

    tile = pl.BlockSpec((TOK_TILE, D_MODEL), lambda i: (i, 0))
    return pl.pallas_call(
        body, name=name, grid=(SEQ // TOK_TILE,), in_specs=[tile, _const_spec((1, D_MODEL))], out_specs=tile,
        out_shape=_out_hbm((SEQ, D_MODEL), BF), compiler_params=_params("parallel"),
    )(*map(_in_hbm, (x, g)))


def _rms_bwd_tile(dyv, xv, gv, dresv):
    r = lax.rsqrt(jnp.mean(xv * xv, axis=-1, keepdims=True) + EPS)
    xn = xv * r
    dxn = dyv * gv
    return dresv + r * (dxn - xn * jnp.mean(dxn * xn, axis=-1, keepdims=True)), jnp.sum(dyv * xn, axis=0, keepdims=True)


def _loss_tile(xv, gv, tv):
    r = lax.rsqrt(jnp.mean(xv * xv, axis=-1, keepdims=True) + EPS)
    xn = xv * r
    err = xn * gv - tv
    lpart = jnp.full((1, 128), 0.5 * jnp.sum(jnp.mean(err * err, axis=-1, keepdims=True)), F32)
    dyv = err * (1.0 / D_MODEL)
    dxn = dyv * gv
    return lpart, r * (dxn - xn * jnp.mean(dxn * xn, axis=-1, keepdims=True)), jnp.sum(dyv * xn, axis=0, keepdims=True)


def _pool_counts(w):
    pos = lax.broadcasted_iota(jnp.int32, (SEQ, 1), 0).astype(F32)
    return jnp.minimum(pos + 1.0, float(w))


def _pool_window(u, w, ext):
    ext[pl.ds(POOL_HALO, SEQ), :] = u
    win = u
    for j in range(1, w):
        win = win + ext[pl.ds(POOL_HALO - j, SEQ), :]
    return win / _pool_counts(w) - u


def _pool_fwd(zcat, w_grp, scale):
    def body(z_ref, w_ref, s_ref, o_ref, ext):
        ext[pl.ds(0, POOL_HALO), :] = jnp.zeros((POOL_HALO, POOL_GD), F32)
        for g, w in enumerate(POOL_WINDOWS):
            cols = slice(g * POOL_GD, (g + 1) * POOL_GD)
            p = _pool_window(z_ref[:, cols].astype(F32), w, ext)
            o_ref[:, cols] = (_dot(p, w_ref[g]) * s_ref[:, cols]).astype(BF)

    return pl.pallas_call(
        body, name="pool_fwd", grid=(1,),
        in_specs=[pl.BlockSpec((SEQ, POOL_WIDTH), lambda i: (0, C_POOL // POOL_WIDTH)),
                  _const_spec((4, POOL_GD, POOL_GD)), _const_spec((1, POOL_WIDTH))],
        out_specs=_const_spec((SEQ, POOL_WIDTH)), out_shape=_out_hbm((SEQ, POOL_WIDTH), BF),
        scratch_shapes=[pltpu.VMEM((POOL_HALO + SEQ, POOL_GD), F32)], compiler_params=_params("arbitrary"),
    )(*map(_in_hbm, (zcat, w_grp, scale)))


def _pool_bwd(dzcat, zcat, dps, w_grp, scale):
    def body(dz_in, z_ref, dps_ref, w_ref, s_ref, dz_ref, dw_ref, dsc_ref, ext, ext2):
        del dz_in
        ext[pl.ds(0, POOL_HALO), :] = jnp.zeros((POOL_HALO, POOL_GD), F32)
        ext2[pl.ds(SEQ, POOL_HALO), :] = jnp.zeros((POOL_HALO, POOL_GD), F32)
        for g, w in enumerate(POOL_WINDOWS):
            cols = slice(g * POOL_GD, (g + 1) * POOL_GD)
            p = _pool_window(z_ref[:, cols].astype(F32), w, ext)
            wg = w_ref[g]
            pg = _dot(p, wg)
            dpsv = dps_ref[:, cols]
            dsc_ref[:, cols] = jnp.sum(dpsv * pg, axis=0, keepdims=True)
            dpg = dpsv * s_ref[:, cols]
            dw_ref[g] = _dot(p, dpg, ta=True)
            dp = _dot(dpg, wg, tb=True)
            dpc = dp / _pool_counts(w)
            ext2[pl.ds(0, SEQ), :] = dpc
            du = dpc
            for j in range(1, w):
                du = du + ext2[pl.ds(j, SEQ), :]
            dz_ref[:, cols] = (du - dp).astype(BF)

    return pl.pallas_call(
        body, name="pool_bwd", grid=(1,),
        in_specs=[pl.BlockSpec(memory_space=pl.ANY),
                  pl.BlockSpec((SEQ, POOL_WIDTH), lambda i: (0, C_POOL // POOL_WIDTH)),
                  _const_spec((SEQ, POOL_WIDTH)), _const_spec((4, POOL_GD, POOL_GD)), _const_spec((1, POOL_WIDTH))],
        out_specs=[pl.BlockSpec((SEQ, POOL_WIDTH), lambda i: (0, C_POOL // POOL_WIDTH)),
                   _const_spec((4, POOL_GD, POOL_GD)), _const_spec((1, POOL_WIDTH))],
        out_shape=[_out_hbm((SEQ, N_DZ), BF), _out_hbm((4, POOL_GD, POOL_GD), F32),
                   _out_hbm((1, POOL_WIDTH), F32)],
        scratch_shapes=[pltpu.VMEM((POOL_HALO + SEQ, POOL_GD), F32), pltpu.VMEM((SEQ + POOL_HALO, POOL_GD), F32)],
        input_output_aliases={0: 0}, compiler_params=_params("arbitrary"),
    )(*map(_in_hbm, (dzcat, zcat, dps, w_grp, scale)))


GK_TILE = 512


GK_ROWS = pl.BlockSpec((GK_PAD, D_MODEL), lambda i: (C_GK // GK_PAD, 0))


def _gk_fwd(h, wt_cat, wgk_pad, b_gk):
    def body(h_ref, wt_ref, w_ref, b_ref, la_ref):
        z_gk = _dot(h_ref[...], wt_ref[...], tb=True)
        pre = _dot(z_gk, w_ref[...]) + b_ref[...]
        la_ref[...] = (jnp.minimum(pre, 0.0) - jnp.log(1.0 + jnp.exp(-jnp.abs(pre)))) * (1.0 / GATE_NORM)

    return pl.pallas_call(
        body, name="gk_fwd", grid=(SEQ // GK_TILE,),
        in_specs=[pl.BlockSpec((GK_TILE, D_MODEL), lambda i: (i, 0)), GK_ROWS,
                  _const_spec((GK_PAD, GLA_DK)), _const_spec((1, GLA_DK))],
        out_specs=pl.BlockSpec((GK_TILE, GLA_DK), lambda i: (i, 0)),
        out_shape=_out_hbm((SEQ, GLA_DK), F32), compiler_params=_params("parallel"),
    )(*map(_in_hbm, (h, wt_cat, wgk_pad, b_gk)))


def _gk_bwd(dzcat, dla, h, wt_cat, wgk_pad, b_gk):
    def body(dz_in, dla_ref, h_ref, wt_ref, w_ref, b_ref, dz_ref, dw_ref, db_ref):
        del dz_in
        wv = w_ref[...]
        z_gk = _dot(h_ref[...], wt_ref[...], tb=True)
        pre = _dot(z_gk, wv) + b_ref[...]
        dpre = dla_ref[...] * (1.0 / GATE_NORM) * (1.0 - _sigmoid(pre))
        dz_ref[...] = _dot(dpre, wv, tb=True).astype(BF)
        dwp = _dot(z_gk, dpre, ta=True)[:GATE_RANK]
        dbp = jnp.sum(dpre, axis=0, keepdims=True)

        @pl.when(pl.program_id(0) == 0)
        def _():
            dw_ref[...] = dwp
            db_ref[...] = dbp

        @pl.when(pl.program_id(0) > 0)
        def _():
            dw_ref[...] += dwp
            db_ref[...] += dbp

    return pl.pallas_call(
        body, name="gk_bwd", grid=(SEQ // GK_TILE,),
        in_specs=[pl.BlockSpec(memory_space=pl.ANY), pl.BlockSpec((GK_TILE, GLA_DK), lambda i: (i, 0)),
                  pl.BlockSpec((GK_TILE, D_MODEL), lambda i: (i, 0)), GK_ROWS, _const_spec((GK_PAD, GLA_DK)),
                  _const_spec((1, GLA_DK))],
        out_specs=[pl.BlockSpec((GK_TILE, GK_PAD), lambda i: (i, C_GK // GK_PAD)), _const_spec((GATE_RANK, GLA_DK)),
                   _const_spec((1, GLA_DK))],
        out_shape=[_out_hbm((SEQ, N_DZ), BF), _out_hbm((GATE_RANK, GLA_DK), F32),
                   _out_hbm((1, GLA_DK), F32)],
        input_output_aliases={0: 0}, compiler_params=_params("arbitrary"),
    )(*map(_in_hbm, (dzcat, dla, h, wt_cat, wgk_pad, b_gk)))


GLA_ROWS = GLA_CPS * CHUNK
GLA_STEPS = SEQ // GLA_ROWS
QKV_W = 2048


def _tri():
    return lax.broadcasted_iota(jnp.int32, (CHUNK, CHUNK), 0) >= lax.broadcasted_iota(jnp.int32, (CHUNK, CHUNK), 1)


def _chunk_cumsum(la_ref, rows):
    return _dot_exact(_tri().astype(F32), la_ref[rows, :])


def _gla_chunk(qkv_ref, la_ref, rows, h, bc_all):
    tri = _tri()
    q = qkv_ref[rows, h * HK:(h + 1) * HK].astype(F32) * (HK ** -0.5)
    k = qkv_ref[rows, GLA_DK + h * HK:GLA_DK + (h + 1) * HK].astype(F32)
    v = qkv_ref[rows, 2 * GLA_DK + h * HV:2 * GLA_DK + (h + 1) * HV].astype(BF)
    la = la_ref[rows, h * HK:(h + 1) * HK]
    bc = bc_all[:, h * HK:(h + 1) * HK]
    e_pos, e_neg = jnp.exp(bc), jnp.exp(-bc)
    dl = jnp.exp(jnp.sum(la, axis=0, keepdims=True))
    q_fw, q_bw, k_fw, k_bw = q * e_pos, q * e_neg, k * e_neg, k * e_pos
    scores = jnp.where(tri, _dot(q_fw, k_fw, tb=True), _dot(q_bw, k_bw, tb=True))
    return tri, v, e_pos, e_neg, dl, q_fw, q_bw, k_fw, k_bw, scores


def _gla_fwd(zcat, la, after):
    def body(qkv_ref, la_ref, after_ref, o_ref, st_ref, state):
        del after_ref

        @pl.when(pl.program_id(0) == 0)
        def _():
            state[...] = jnp.zeros_like(state)

        for c in range(GLA_CPS):
            rows = slice(c * CHUNK, (c + 1) * CHUNK)
            bc_all = _chunk_cumsum(la_ref, rows)
            for h in range(HEADS):
                _, v, _, _, dl, q_fw, _, k_fw, _, scores = _gla_chunk(qkv_ref, la_ref, rows, h, bc_all)
                st = state[h]
                st_ref[c, h] = st
                o_ref[rows, h * HV:(h + 1) * HV] = _dot(scores, v) + _dot(q_fw, st, tb=True)
                state[h] = st * dl + _dot(v, k_fw * dl, ta=True)

    return pl.pallas_call(
        body, name="gla_fwd", grid=(GLA_STEPS,),
        in_specs=[pl.BlockSpec((GLA_ROWS, QKV_W), lambda i: (i, 0)), pl.BlockSpec((GLA_ROWS, GLA_DK), lambda i: (i, 0)),
                  pl.BlockSpec(memory_space=pl.ANY)],
        out_specs=[pl.BlockSpec((GLA_ROWS, D_MODEL), lambda i: (i, 0)),
                   pl.BlockSpec((GLA_CPS, HEADS, HV, HK), lambda i: (i, 0, 0, 0))],
        out_shape=[_out_hbm((SEQ, D_MODEL), F32),
                   _out_hbm((SEQ // CHUNK, HEADS, HV, HK), F32)],
        scratch_shapes=[pltpu.VMEM((HEADS, HV, HK), F32)], compiler_params=_params("arbitrary"),
    )(*map(_in_hbm, (zcat, la)), after)


def _gla_bwd(dzcat, zcat, la, d_o, states):
    def body(dz_in, qkv_ref, la_ref, do_ref, st_ref, dqkv_ref, dla_ref, dstate):
        del dz_in

        @pl.when(pl.program_id(0) == 0)
        def _():
            dstate[...] = jnp.zeros_like(dstate)

        last_row = lax.broadcasted_iota(jnp.int32, (CHUNK, HK), 0) == CHUNK - 1
        upper = (lax.broadcasted_iota(jnp.int32, (CHUNK, CHUNK), 0)
                 <= lax.broadcasted_iota(jnp.int32, (CHUNK, CHUNK), 1)).astype(F32)
        for c in reversed(range(GLA_CPS)):
            rows = slice(c * CHUNK, (c + 1) * CHUNK)
            bc_all = _chunk_cumsum(la_ref, rows)
            dbs = []
            for h in range(HEADS):
                tri, v, e_pos, e_neg, dl, q_fw, q_bw, k_fw, k_bw, scores = _gla_chunk(qkv_ref, la_ref, rows, h, bc_all)
                st = st_ref[c, h]
                dst = dstate[h]
                d_out = do_ref[rows, h * HV:(h + 1) * HV].astype(BF)
                k_dec = k_fw * dl
                dp = _dot(d_out, v, tb=True)
                dp_fw = jnp.where(tri, dp, 0.0)
                dp_bw = jnp.where(tri, 0.0, dp)
                dv = _dot(scores, d_out, ta=True) + _dot(k_dec, dst, tb=True)
                dk_dec = _dot(v, dst)
                dq_fw = _dot(dp_fw, k_fw) + _dot(d_out, st)
                dk_fw = _dot(dp_fw, q_fw, ta=True) + dk_dec * dl
                dq_bw = _dot(dp_bw, k_bw)
                dk_bw = _dot(dp_bw, q_bw, ta=True)
                ddl = jnp.sum(st * dst, axis=0, keepdims=True) + jnp.sum(k_fw * dk_dec, axis=0, keepdims=True)
                dstate[h] = dst * dl + _dot(d_out, q_fw, ta=True)
                dq = (dq_fw * e_pos + dq_bw * e_neg) * (HK ** -0.5)
                dk = dk_fw * e_neg + dk_bw * e_pos
                dbs.append(dq_fw * q_fw - dk_fw * k_fw - dq_bw * q_bw + dk_bw * k_bw + jnp.where(last_row, ddl * dl, 0.0))
                dqkv_ref[rows, h * HK:(h + 1) * HK] = dq.astype(BF)
                dqkv_ref[rows, GLA_DK + h * HK:GLA_DK + (h + 1) * HK] = dk.astype(BF)
                dqkv_ref[rows, 2 * GLA_DK + h * HV:2 * GLA_DK + (h + 1) * HV] = dv.astype(BF)
            dla_ref[rows, :] = _dot_exact(upper, jnp.concatenate(dbs, axis=1))

    rev = lambda i: (GLA_STEPS - 1 - i, 0)
    return pl.pallas_call(
        body, name="gla_bwd", grid=(GLA_STEPS,),
        in_specs=[pl.BlockSpec(memory_space=pl.ANY), pl.BlockSpec((GLA_ROWS, QKV_W), rev),
                  pl.BlockSpec((GLA_ROWS, GLA_DK), rev), pl.BlockSpec((GLA_ROWS, D_MODEL), rev),
                  pl.BlockSpec((GLA_CPS, HEADS, HV, HK), lambda i: (GLA_STEPS - 1 - i, 0, 0, 0))],
        out_specs=[pl.BlockSpec((GLA_ROWS, QKV_W), rev), pl.BlockSpec((GLA_ROWS, GLA_DK), rev)],
        out_shape=[_out_hbm((SEQ, N_DZ), BF), _out_hbm((SEQ, GLA_DK), F32)],
        scratch_shapes=[pltpu.VMEM((HEADS, HV, HK), F32)], input_output_aliases={0: 0},
        compiler_params=_params("arbitrary"),
    )(*map(_in_hbm, (dzcat, zcat, la, d_o, states)))


def _silu_parts(x):
    s = _sigmoid(x)
    return x * s, s * (1.0 + x * (1.0 - s))


def _post_gla_fwd(o, zcat, g_head):
    def body(o_ref, zog_ref, g_ref, out_ref):
        for h in range(HEADS):
            cols = slice(h * HV, (h + 1) * HV)
            ov = o_ref[:, cols]
            r = lax.rsqrt(jnp.mean(ov * ov, axis=-1, keepdims=True) + EPS)
            act, _ = _silu_parts(zog_ref[:, cols].astype(F32))
            out_ref[:, cols] = (ov * r * g_ref[...] * act).astype(BF)

    tile = pl.BlockSpec((TOK_TILE, D_MODEL), lambda i: (i, 0))
    return pl.pallas_call(
        body, name="post_gla_fwd", grid=(SEQ // TOK_TILE,),
        in_specs=[tile, pl.BlockSpec((TOK_TILE, D_MODEL), lambda i: (i, C_OG // D_MODEL)), _const_spec((1, HV))],
        out_specs=tile, out_shape=_out_hbm((SEQ, D_MODEL), BF), compiler_params=_params("parallel"),
    )(*map(_in_hbm, (o, zcat, g_head)))


def _post_gla_bwd(dzcat, dy_gla, w_gla_proj, o, zcat, g_head):
    def body(dz_in, dyg_ref, w_ref, o_ref, zog_ref, g_ref, dz_ref, do_ref, dg_ref):
        del dz_in
        dog = _dot(dyg_ref[...], w_ref[...], tb=True)
        gpart = jnp.zeros((1, HV), F32)
        gv = g_ref[...]
        for h in range(HEADS):
            cols = slice(h * HV, (h + 1) * HV)
            ov = o_ref[:, cols]
            r = lax.rsqrt(jnp.mean(ov * ov, axis=-1, keepdims=True) + EPS)
            on = ov * r
            act, dact = _silu_parts(zog_ref[:, cols].astype(F32))
            dogv = dog[:, cols]
            dz_ref[:, cols] = (dogv * on * gv * dact).astype(BF)
            d_on_g = dogv * act
            gpart = gpart + jnp.sum(d_on_g * on, axis=0, keepdims=True)
            dxn = d_on_g * gv
            do_ref[:, cols] = (r * (dxn - on * jnp.mean(dxn * on, axis=-1, keepdims=True))).astype(BF)

        @pl.when(pl.program_id(0) == 0)
        def _():
            dg_ref[...] = gpart

        @pl.when(pl.program_id(0) > 0)
        def _():
            dg_ref[...] += gpart

    tile = pl.BlockSpec((TOK_TILE, D_MODEL), lambda i: (i, 0))
    ogspec = pl.BlockSpec((TOK_TILE, D_MODEL), lambda i: (i, C_OG // D_MODEL))
    return pl.pallas_call(
        body, name="post_gla_bwd", grid=(SEQ // TOK_TILE,),
        in_specs=[pl.BlockSpec(memory_space=pl.ANY), tile, _const_spec((D_MODEL, D_MODEL)), tile, ogspec,
                  _const_spec((1, HV))],
        out_specs=[ogspec, tile, _const_spec((1, HV))],
        out_shape=[_out_hbm((SEQ, N_DZ), BF), _out_hbm((SEQ, D_MODEL), BF),
                   _out_hbm((1, HV), F32)],
        input_output_aliases={0: 0}, compiler_params=_params("arbitrary"),
    )(*map(_in_hbm, (dzcat, dy_gla, w_gla_proj, o, zcat, g_head)))


GATE_W = 2 * D_MODEL


def _mix_out_fwd(ps, og, zcat, x, w_pool_proj, w_gla_proj, w_out, b_gate, g_ffn, after):
    def body(ps_ref, og_ref, zg_ref, x_ref, wpp_ref, wgp_ref, wout_ref, b_ref, g_ref, after_ref,
             yp_ref, yg_ref, mixed_ref, x1_ref, h2_ref):
        del after_ref
        y_pool = _dot(ps_ref[...], wpp_ref[...])
        y_gla = _dot(og_ref[...], wgp_ref[...])
        yp_ref[...] = y_pool.astype(BF)
        yg_ref[...] = y_gla.astype(BF)
        g0 = _sigmoid(zg_ref[:, :D_MODEL].astype(F32) + b_ref[:, :D_MODEL])
        g1 = _sigmoid(zg_ref[:, D_MODEL:].astype(F32) + b_ref[:, D_MODEL:])
        mixed = (g0 * y_pool + g1 * y_gla).astype(BF)
        mixed_ref[...] = mixed
        x1 = x_ref[...] + _dot(mixed, wout_ref[...])
        x1_ref[...] = x1
        r = lax.rsqrt(jnp.mean(x1 * x1, axis=-1, keepdims=True) + EPS)
        h2_ref[...] = (x1 * r * g_ref[...]).astype(BF)

    tile = pl.BlockSpec((TOK_TILE, D_MODEL), lambda i: (i, 0))
    resident = lambda shape: pl.BlockSpec(shape, lambda i: (0, 0), pipeline_mode=pl.Buffered(1))
    f32, bf16 = _out_hbm((SEQ, D_MODEL), F32), _out_hbm((SEQ, D_MODEL), BF)
    return pl.pallas_call(
        body, name="mix_out_fwd", grid=(SEQ // TOK_TILE,),
        in_specs=[pl.BlockSpec((TOK_TILE, POOL_WIDTH), lambda i: (i, 0)), tile,
                  pl.BlockSpec((TOK_TILE, GATE_W), lambda i: (i, C_GATE // GATE_W)), tile,
                  resident((POOL_WIDTH, D_MODEL)), resident((D_MODEL, D_MODEL)), resident((D_MODEL, D_MODEL)),
                  _const_spec((1, GATE_W)), _const_spec((1, D_MODEL)), pl.BlockSpec(memory_space=pl.ANY)],
        out_specs=[tile] * 5, out_shape=[bf16, bf16, bf16, f32, bf16], compiler_params=_params("parallel"),
    )(*map(_in_hbm, (ps, og, zcat, x, w_pool_proj, w_gla_proj, w_out, b_gate, g_ffn)), after)


def _mix_bwd(dx1, w_out, zcat, b_gate, y_pool, y_gla):
    def body(dx_ref, w_ref, zg_ref, b_ref, yp_ref, yg_ref, dz_ref, dyp_ref, dyg_ref, db_ref):
        dm = _dot(dx_ref[...], w_ref[...], tb=True)
        g0 = _sigmoid(zg_ref[:, :D_MODEL].astype(F32) + b_ref[:, :D_MODEL])
        g1 = _sigmoid(zg_ref[:, D_MODEL:].astype(F32) + b_ref[:, D_MODEL:])
        dyp_ref[...] = (dm * g0).astype(BF)
        dyg_ref[...] = (dm * g1).astype(BF)
        dz0 = dm * yp_ref[...].astype(F32) * g0 * (1.0 - g0)
        dz1 = dm * yg_ref[...].astype(F32) * g1 * (1.0 - g1)
        dz_ref[:, :D_MODEL] = dz0.astype(BF)
        dz_ref[:, D_MODEL:] = dz1.astype(BF)
        b0 = jnp.sum(dz0, axis=0, keepdims=True)
        b1 = jnp.sum(dz1, axis=0, keepdims=True)

        @pl.when(pl.program_id(0) == 0)
        def _():
            db_ref[:, :D_MODEL] = b0
            db_ref[:, D_MODEL:] = b1

        @pl.when(pl.program_id(0) > 0)
        def _():
            db_ref[:, :D_MODEL] += b0
            db_ref[:, D_MODEL:] += b1

    tile = pl.BlockSpec((TOK_TILE, D_MODEL), lambda i: (i, 0))
    gspec = pl.BlockSpec((TOK_TILE, GATE_W), lambda i: (i, C_GATE // GATE_W))
    return pl.pallas_call(
        body, name="mix_bwd", grid=(SEQ // TOK_TILE,),
        in_specs=[tile, _const_spec((D_MODEL, D_MODEL)), gspec, _const_spec((1, GATE_W)), tile, tile],
        out_specs=[gspec, tile, tile, _const_spec((1, GATE_W))],
        out_shape=[_out_hbm((SEQ, N_DZ), BF), _out_hbm((SEQ, D_MODEL), BF),
                   _out_hbm((SEQ, D_MODEL), BF), _out_hbm((1, GATE_W), F32)],
        compiler_params=_params("arbitrary"),
    )(*map(_in_hbm, (dx1, w_out, zcat, b_gate, y_pool, y_gla)))


N_TOK_TILES = SEQ // TOK_TILE
HALO_PER_TILE = TOK_TILE // HALO


LANE_TILES = tuple((lo, min(128, FF_BLK - lo)) for lo in range(0, FF_BLK, 128))


def _taps(w_ref, b_ref, half, lanes, rows):
    shape = (rows, lanes.stop - lanes.start)
    return ([jnp.broadcast_to(w_ref[half, j:j + 1, lanes], shape) for j in range(3)],
            jnp.broadcast_to(b_ref[half, :, lanes], shape))


def _conv_strips(u_ref, ub_ref, ua_ref, taps, lanes, width, n_strips, first):
    row = lax.broadcasted_iota(jnp.int32, (HALO, width), 0)
    prev = [[pltpu.roll(jnp.where(first, 0.0, ub_ref[half, :, lanes]), k, 0) for k in (1, 2)] for half in range(2)]
    for s in range(n_strips + (ua_ref is not None)):
        u3, conv = [], []
        for half in range(2):
            cur = u_ref[half, s * HALO:(s + 1) * HALO, lanes] if s < n_strips else ua_ref[half, :, lanes]
            rolled = [pltpu.roll(cur, k, 0) for k in (1, 2)]
            frames = [jnp.where(row >= 2, rolled[1], prev[half][1]), jnp.where(row >= 1, rolled[0], prev[half][0]), cur]
            prev[half] = rolled
            w3, bias = taps[half]
            u3.append(frames)
            conv.append(bias + frames[0] * w3[0] + frames[1] * w3[1] + frames[2] * w3[2])
        yield s, u3, conv


def _pair_specs(pairs):
    tile = pl.BlockSpec((pairs, None, TOK_TILE, FF_BLK), lambda b, i: (0, b, i, 0))
    before = pl.BlockSpec((pairs, None, HALO, FF_BLK), lambda b, i: (0, b, jnp.maximum(i * HALO_PER_TILE - 1, 0), 0))
    after = pl.BlockSpec((pairs, None, HALO, FF_BLK),
                         lambda b, i: (0, b, jnp.minimum((i + 1) * HALO_PER_TILE, SEQ // HALO - 1), 0))

    def vec(rows):
        return pl.BlockSpec((2, None, rows, FF_BLK), lambda b, i: (0, b, 0, 0))

    return tile, before, after, vec


N_STRIPS = TOK_TILE // HALO


def _up_conv_fwd(h2, wt_up, w_conv, b_conv):
    steps = N_TOK_TILES // 2

    def body(h_ref, h_next, wg_ref, wv_ref, w_ref, b_ref, u_ref, a_ref, buf_a, buf_b, carry):
        j = pl.program_id(1)

        def project(hv, buf):
            buf[0] = _dot(hv, wg_ref[...], tb=True)
            buf[1] = _dot(hv, wv_ref[...], tb=True)

        def conv(buf, row0):
            u_ref[:, row0:row0 + TOK_TILE, :] = buf[...]
            for lo, width in LANE_TILES:
                lanes = slice(lo, lo + width)
                taps = [_taps(w_ref, b_ref, half, lanes, HALO) for half in range(2)]
                pending = None
                for s, _, (cg, cv) in _conv_strips(buf, carry, None, taps, lanes, width, N_STRIPS, False):
                    act = cg * _sigmoid(cg) * cv
                    if s % 2 == 0:
                        pending = act
                    else:
                        a_ref[0, row0 + (s - 1) * HALO:row0 + (s + 1) * HALO, lanes] = (
                            jnp.concatenate([pending, act], axis=0).astype(BF))
            carry[...] = buf[:, TOK_TILE - HALO:, :]

        @pl.when(j == 0)
        def _():
            project(h_ref[0:TOK_TILE, :], buf_a)
            carry[...] = jnp.zeros_like(carry)

        project(h_ref[TOK_TILE:, :], buf_b)
        conv(buf_a, 0)
        project(h_next[...], buf_a)
        conv(buf_b, TOK_TILE)

    w_blk = lambda half: pl.BlockSpec((FF_BLK, D_MODEL), lambda b, j: (b + 4 * half, 0))
    vec = lambda rows: pl.BlockSpec((2, None, rows, FF_BLK), lambda b, j: (0, b, 0, 0))
    u_buf = pltpu.VMEM((2, TOK_TILE, FF_BLK), F32)
    return pl.pallas_call(
        body, name="up_conv_fwd", grid=(4, steps),
        in_specs=[pl.BlockSpec((2 * TOK_TILE, D_MODEL), lambda b, j: (j, 0)),
                  pl.BlockSpec((TOK_TILE, D_MODEL), lambda b, j: (jnp.minimum(2 * j + 2, N_TOK_TILES - 1), 0)),
                  w_blk(0), w_blk(1), vec(3), vec(1)],
        out_specs=[pl.BlockSpec((2, None, 2 * TOK_TILE, FF_BLK), lambda b, j: (0, b, j, 0)),
                   pl.BlockSpec((1, None, 2 * TOK_TILE, FF_BLK), lambda b, j: (0, b, j, 0))],
        out_shape=[_out_hbm((2, 4, SEQ, FF_BLK), F32), _out_hbm((1, 4, SEQ, FF_BLK), BF)],
        scratch_shapes=[u_buf, u_buf, pltpu.VMEM((2, HALO, FF_BLK), F32)],
        compiler_params=_params("parallel", "arbitrary"),
    )(*map(_in_hbm, (h2, h2, wt_up, wt_up, w_conv, b_conv)))


def _conv_bwd(u, da, w_conv, b_conv):
    def body(u_ref, ub_ref, ua_ref, da_ref, daa_ref, w_ref, b_ref, du_ref, dw_ref, db_ref):
        i = pl.program_id(1)

        @pl.when(i == 0)
        def _():
            dw_ref[...] = jnp.zeros_like(dw_ref)
            db_ref[...] = jnp.zeros_like(db_ref)

        for lo, width in LANE_TILES:
            lanes = slice(lo, lo + width)
            row = lax.broadcasted_iota(jnp.int32, (HALO, width), 0)
            taps = [_taps(w_ref, b_ref, half, lanes, HALO) for half in range(2)]
            acc_w = [[jnp.zeros((HALO, width), F32) for _ in range(3)] for _ in range(2)]
            acc_b = [jnp.zeros((HALO, width), F32) for _ in range(2)]
            da_pair, pending = None, [None, None]
            dc_prev, up_prev = [None, None], [None, None]
            for s, u3, (cg, cv) in _conv_strips(u_ref, ub_ref, ua_ref, taps, lanes, width, N_STRIPS, i == 0):
                act, dact = _silu_parts(cg)
                if s == N_STRIPS:
                    da = jnp.where(i < N_TOK_TILES - 1, daa_ref[0, :, lanes].astype(F32), 0.0)
                elif s % 2 == 0:
                    da_pair = da_ref[0, s * HALO:(s + 2) * HALO, lanes].astype(F32)
                    da = da_pair[:HALO]
                else:
                    da = da_pair[HALO:]
                dc = (da * cv * dact, da * act)
                for half in range(2):
                    up = [pltpu.roll(dc[half], HALO - k, 0) for k in (1, 2)]
                    if s < N_STRIPS:
                        for j in range(3):
                            acc_w[half][j] = acc_w[half][j] + dc[half] * u3[half][j]
                        acc_b[half] = acc_b[half] + dc[half]
                    if s >= 1:
                        w3 = taps[half][0]
                        du = (dc_prev[half] * w3[2] + jnp.where(row < HALO - 1, up_prev[half][0], up[0]) * w3[1]
                              + jnp.where(row < HALO - 2, up_prev[half][1], up[1]) * w3[0])
                        if (s - 1) % 2 == 0:
                            pending[half] = du
                        else:
                            du_ref[half, (s - 2) * HALO:s * HALO, lanes] = jnp.concatenate([pending[half], du],
                                                                                           axis=0).astype(BF)
                    dc_prev[half], up_prev[half] = dc[half], up
            for half in range(2):
                for j in range(3):
                    dw_ref[half, j:j + 1, lanes] += jnp.sum(acc_w[half][j], axis=0, keepdims=True)
                db_ref[half, :, lanes] += jnp.sum(acc_b[half], axis=0, keepdims=True)

    tile, before, after, vec = _pair_specs(2)
    da_tile, _, da_after_spec, _ = _pair_specs(1)
    return pl.pallas_call(
        body, name="conv_bwd", grid=(4, N_TOK_TILES),
        in_specs=[tile, before, after, da_tile, da_after_spec, vec(3), vec(1)],
        out_specs=[tile, vec(3), vec(1)],
        out_shape=[_out_hbm((2, 4, SEQ, FF_BLK), BF), _out_hbm((2, 4, 3, FF_BLK), F32),
                   _out_hbm((2, 4, 1, FF_BLK), F32)],
        compiler_params=_params("parallel", "arbitrary"),
    )(*map(_in_hbm, (u, u, u, da, da, w_conv, b_conv)))


W_IN_SEGMENTS = ((R_POOL, POOL_WIDTH, C_POOL), (R_QKV, QKV_W, C_QKV), (R_OG, D_MODEL, C_OG), (R_GK, GATE_RANK, C_GK),
                 (R_GATE, GATE_W, C_GATE))


def _slab_pieces(d):
    lo, hi = d * IN_SHARD, (d + 1) * IN_SHARD
    pieces = []
    for start, n, at in W_IN_SEGMENTS:
        a, b = max(lo, start), min(hi, start + n)
        if a < b:
            assert (a - lo) % 2 == 0 and (b - a) % 2 == 0 and (at + a - start) % 2 == 0
            pieces.append(((a - lo) // 2, (b - a) // 2, (at + a - start) // 2))
    return pieces


def _unshard_w_in(slabs):
    def body(slab_ref, cat_ref):
        d = pl.program_id(0)
        src = slab_ref.bitcast(jnp.uint32)
        dst = cat_ref.bitcast(jnp.uint32)

        @pl.when(d == 0)
        def _():
            cat_ref[C_GK:, :] = jnp.zeros((GK_PAD, D_MODEL), BF)

        for dd in range(N_DEV):
            @pl.when(d == dd)
            def _():
                for a, n, at in _slab_pieces(dd):
                    dst[pl.ds(at, n), :] = src[0, pl.ds(a, n), :]

    return pl.pallas_call(
        body, name="unshard_w_in", grid=(N_DEV,),
        in_specs=[pl.BlockSpec((1, IN_SHARD, D_MODEL), lambda d: (d, 0, 0))], out_specs=_const_spec((N_DZ, D_MODEL)),
        out_shape=_out_hbm((N_DZ, D_MODEL), BF), compiler_params=_params("arbitrary"),
    )(_in_hbm(slabs))


def _shard_d_w_in(d_cat):
    def body(cat_ref, slab_ref):
        d = pl.program_id(0)
        cat = cat_ref.bitcast(jnp.uint32)
        dst = slab_ref.bitcast(jnp.uint32)
        for dd in range(N_DEV):
            @pl.when(d == dd)
            def _():
                for a, n, at in _slab_pieces(dd):
                    dst[0, pl.ds(a, n), :] = cat[pl.ds(at, n), :]

    return pl.pallas_call(
        body, name="shard_d_w_in", grid=(N_DEV,), in_specs=[_const_spec((N_DZ, D_MODEL))],
        out_specs=pl.BlockSpec((1, IN_SHARD, D_MODEL), lambda d: (d, 0, 0)),
        out_shape=_out_hbm((N_DEV, IN_SHARD, D_MODEL), BF), compiler_params=_params("parallel"),
    )(_in_hbm(d_cat))


ANY = pl.BlockSpec(memory_space=pl.ANY)


def _place():
    x, y, c = lax.axis_index("x"), lax.axis_index("y"), lax.axis_index("c")
    other_chips = [(1 - x, y), (x, 1 - y), (1 - x, 1 - y)]
    return x, y, c, other_chips


SEM = pl.BlockSpec(memory_space=pltpu.SEMAPHORE)
IN_HBM = pl.BlockSpec(memory_space=pltpu.HBM)
SPLIT_PARAMS = pltpu.CompilerParams(has_side_effects=pltpu.SideEffectType.DATAFLOW_SIDE_EFFECTING)


def _gather_first(refs, send_sems, recv_sems):
    x, y, c, chips = _place()
    targets = [(x, y, 1 - c)] + [(px, py, c) for px, py in chips]
    return [pltpu.make_async_remote_copy(src_ref=refs[2 * a], dst_ref=refs[2 * a + 1].at[4 * x + 2 * y + c],
                                         send_sem=send_sems.at[4 * a + k], recv_sem=recv_sems.at[4 * a + k],
                                         device_id=to, device_id_type=MESH)
            for a in range(len(refs) // 2) for k, to in enumerate(targets)]


def _gather_direct(refs, send_sems, recv_sems):
    x, y, c, _ = _place()
    flips = [(dx, dy, dc) for dx in (0, 1) for dy in (0, 1) for dc in (0, 1) if dx + dy + dc]
    targets = [(1 - x if dx else x, 1 - y if dy else y, 1 - c if dc else c) for dx, dy, dc in flips]
    return [pltpu.make_async_remote_copy(src_ref=refs[2 * a], dst_ref=refs[2 * a + 1].at[4 * x + 2 * y + c],
                                         send_sem=send_sems.at[7 * a + k], recv_sem=recv_sems.at[7 * a + k],
                                         device_id=to, device_id_type=MESH)
            for a in range(len(refs) // 2) for k, to in enumerate(targets)]


def _gather_second(refs, send_sems, recv_sems):
    x, y, c, chips = _place()
    copies = []
    for a, land in enumerate(refs):
        for j, (px, py) in enumerate(chips):
            block = land.at[4 * px + 2 * py + c]
            copies.append(pltpu.make_async_remote_copy(src_ref=block, dst_ref=block, send_sem=send_sems.at[3 * a + j],
                                                       recv_sem=recv_sems.at[3 * a + j], device_id=(x, y, 1 - c),
                                                       device_id_type=MESH))
    return copies


def _reduce_first(refs, send_sems, recv_sems):
    x, y, c, _ = _place()
    return [pltpu.make_async_remote_copy(src_ref=refs[2 * a].at[j, 1 - c], dst_ref=refs[2 * a + 1].at[j],
                                         send_sem=send_sems.at[4 * a + j], recv_sem=recv_sems.at[4 * a + j],
                                         device_id=(x, y, 1 - c), device_id_type=MESH)
            for a in range(len(refs) // 2) for j in range(4)]


def _reduce_second(refs, send_sems, recv_sems):
    _, _, c, chips = _place()
    return [pltpu.make_async_remote_copy(src_ref=refs[2 * a].at[2 * px + py], dst_ref=refs[2 * a + 1].at[k],
                                         send_sem=send_sems.at[3 * a + k], recv_sem=recv_sems.at[3 * a + k],
                                         device_id=(px, py, c), device_id_type=MESH)
            for a in range(len(refs) // 2) for k, (px, py) in enumerate(chips)]


def _split_start(name, groups):
    arrays = [a for g in groups for a in g[0]]
    n = len(arrays)

    def body(*refs):
        sems = refs[n:n + 2 * len(groups)]
        at = 0
        for gi, (members, _, build) in enumerate(groups):
            for cp in build(refs[at:at + len(members)], sems[2 * gi], sems[2 * gi + 1]):
                cp.start()
            at += len(members)
        refs[-1][...] = jnp.zeros_like(refs[-1])

    sem_shapes = [pltpu.SemaphoreType.DMA((g[1],)) for g in groups for _ in range(2)]
    outs = pl.pallas_call(
        body, name=name, in_specs=[IN_HBM] * n,
        out_shape=(*sem_shapes, *[_out_hbm(a.shape, a.dtype) for a in arrays], jax.ShapeDtypeStruct((8, 128), F32)),
        out_specs=(*[SEM] * len(sem_shapes), *[IN_HBM] * n, pl.BlockSpec(memory_space=pltpu.VMEM)),
        input_output_aliases={i: len(sem_shapes) + i for i in range(n)}, compiler_params=SPLIT_PARAMS,
    )(*[pltpu.with_memory_space_constraint(a, pltpu.HBM) for a in arrays])
    per_group, at = [], len(sem_shapes)
    for gi, (members, _, _) in enumerate(groups):
        per_group.append((outs[2 * gi], outs[2 * gi + 1], list(outs[at:at + len(members)])))
        at += len(members)
    return per_group, outs[-1]


def _split_wait(name, started, build, after):
    send_sems, recv_sems, arrays = started
    n = len(arrays)
    after = after if isinstance(after, (tuple, list)) else (after,)

    def body(*refs):
        for cp in build(refs[:n], refs[n], refs[n + 1]):
            cp.wait_send()
            cp.wait_recv()

    return pl.pallas_call(
        body, name=name, in_specs=[IN_HBM] * n + [SEM, SEM] + [ANY] * len(after),
        out_shape=tuple(_out_hbm(a.shape, a.dtype) for a in arrays), out_specs=tuple([IN_HBM] * n),
        input_output_aliases={i: i for i in range(n)}, compiler_params=SPLIT_PARAMS,
    )(*arrays, send_sems, recv_sems, *after)


def _placed_behind(token, arrays, name):
    n = len(arrays)

    def body(*refs):
        refs[-1][...] = jnp.zeros_like(refs[-1])

    outs = pl.pallas_call(
        body, name=name, in_specs=[IN_HBM] * n + [ANY],
        out_shape=(*[_out_hbm(a.shape, a.dtype) for a in arrays], jax.ShapeDtypeStruct((8, 128), F32)),
        out_specs=(*[IN_HBM] * n, pl.BlockSpec(memory_space=pltpu.VMEM)),
        input_output_aliases={i: i for i in range(n)},
    )(*map(_in_hbm, arrays), token)
    return outs[:n], outs[-1]


def _gather_landing(shard, me):
    return lax.dynamic_update_slice(lax.empty((N_DEV,) + shard.shape, shard.dtype), shard[None],
                                    (me,) + (0,) * shard.ndim)


ADAM_LANE_TILE = 256


def _tile_2d(rows, cols):
    for t in (256, 176, 128):
        if rows % t == 0:
            return t, cols
    return rows, ADAM_LANE_TILE


def _pair_sum(part, recv, core, name):
    _, rows, cols = recv.shape
    tr, tc = rows, cols

    def body(c_ref, p_ref, r_ref, o_ref):
        del c_ref
        o_ref[...] = (p_ref[...].astype(F32) + r_ref[...].astype(F32)).astype(BF)

    grid_spec = pltpu.PrefetchScalarGridSpec(
        num_scalar_prefetch=1, grid=(4, rows // tr, cols // tc),
        in_specs=[pl.BlockSpec((None, None, tr, tc), lambda j, i, k, c_ref: (j, c_ref[0], i, k)),
                  pl.BlockSpec((None, tr, tc), lambda j, i, k, c_ref: (j, i, k))],
        out_specs=pl.BlockSpec((None, tr, tc), lambda j, i, k, c_ref: (j, i, k)))
    return pl.pallas_call(
        body, name=name, grid_spec=grid_spec, out_shape=_out_hbm(recv.shape, BF),
        compiler_params=_params("parallel", "parallel", "parallel"),
    )(core, *map(_in_hbm, (part, recv)))


def _adamw(w, g, m, v):
    m = ADAM_B1 * m + (1.0 - ADAM_B1) * g
    v = ADAM_B2 * v + (1.0 - ADAM_B2) * (g * g)
    delta = -ADAM_LR * ((m / ADAM_C1) / (jnp.sqrt(v / ADAM_C2) + ADAM_EPS) + ADAM_WD * w)
    return delta, m, v


def _chip_sum_adamw(sums, recv, w, m, v, chip, name):
    rows, cols = w.shape
    tr, tc = _tile_2d(rows, cols)

    def body(chip_ref, s_ref, r_ref, w_ref, m_ref, v_ref, g_out, d_out, m_out, v_out):
        del chip_ref
        g = s_ref[...].astype(F32)
        for k in range(3):
            g = g + r_ref[k].astype(F32)
        g_out[...] = g
        d_out[...], m_out[...], v_out[...] = _adamw(w_ref[...], g, m_ref[...], v_ref[...])

    tile = pl.BlockSpec((tr, tc), lambda i, k, chip_ref: (i, k))
    grid_spec = pltpu.PrefetchScalarGridSpec(
        num_scalar_prefetch=1, grid=(rows // tr, cols // tc),
        in_specs=[pl.BlockSpec((None, tr, tc), lambda i, k, chip_ref: (chip_ref[0], i, k)),
                  pl.BlockSpec((3, tr, tc), lambda i, k, chip_ref: (0, i, k)), tile, tile, tile],
        out_specs=[tile] * 4)
    return pl.pallas_call(
        body, name=name, grid_spec=grid_spec, out_shape=[_out_hbm((rows, cols), F32)] * 4,
        compiler_params=_params("parallel", "parallel"),
    )(chip, *map(_in_hbm, (sums, recv, w, m, v)))


def _small_sum_adamw(me, entries, loss_parts):
    def whole(shape, squeeze=0, pick=False):
        blk = (None,) * squeeze + tuple(shape[squeeze:])
        if pick:
            blk = (shape[0], None) + tuple(shape[2:])
            return pl.BlockSpec(blk, lambda i, me_ref: (0, me_ref[0]) + (0,) * (len(shape) - 2))
        return pl.BlockSpec(blk, lambda i, me_ref: (0,) * len(shape))

    in_specs, out_specs, out_shape, args = [], [], [], []
    for parts, w, m, v, sharded in entries:
        lead = w.ndim - (parts.ndim - (2 if sharded else 1))
        in_specs += [whole(parts.shape, pick=sharded)] + [whole(w.shape, squeeze=lead)] * 3
        out_specs += [whole(w.shape, squeeze=lead)] * 4
        out_shape += [_out_hbm(w.shape, F32)] * 4
        args += [parts, w, m, v]
    in_specs.append(whole(loss_parts.shape))
    out_specs.append(whole(loss_parts.shape[1:]))
    out_shape.append(_out_hbm(loss_parts.shape[1:], F32))
    n = len(entries)

    def added(p_ref):
        total = p_ref[0]
        for d in range(1, N_DEV):
            total = total + p_ref[d]
        return total

    def body(me_ref, *refs):
        del me_ref
        ins, outs = refs[:4 * n + 1], refs[4 * n + 1:]
        for e in range(n):
            p_ref, w_ref, m_ref, v_ref = ins[4 * e:4 * e + 4]
            g_out, d_out, m_out, v_out = outs[4 * e:4 * e + 4]
            g = added(p_ref)
            g_out[...] = g
            d_out[...], m_out[...], v_out[...] = _adamw(w_ref[...], g, m_ref[...], v_ref[...])
        outs[4 * n][...] = added(ins[4 * n])

    grid_spec = pltpu.PrefetchScalarGridSpec(num_scalar_prefetch=1, grid=(1,), in_specs=in_specs, out_specs=out_specs)
    outs = pl.pallas_call(body, name="small_sum_adamw", grid_spec=grid_spec, out_shape=out_shape,
                          compiler_params=_params("arbitrary"))(me, *map(_in_hbm, args + [loss_parts]))
    return [outs[4 * e:4 * e + 4] for e in range(n)], outs[4 * n]


MM_TILE = 512
N_MM_TILES = SEQ // MM_TILE
CAT_TILE = 512
N_CAT_TILES = N_CAT // CAT_TILE
DZ_TILE = 640


def kernel(x, g_mix, w_in, b_gate, w_gk_up, b_gk, w_pool_grp, pool_scale, g_gla_head, w_pool_proj, w_gla_proj, w_out, g_ffn, w_up, w_conv, b_conv, w_down, g_final, loss_target, m_g_mix, m_w_in, m_b_gate, m_w_gk_up, m_b_gk, m_w_pool_grp, m_pool_scale, m_g_gla_head, m_w_pool_proj, m_w_gla_proj, m_w_out, m_g_ffn, m_w_up, m_w_conv, m_b_conv, m_w_down, m_g_final, v_g_mix, v_w_in, v_b_gate, v_w_gk_up, v_b_gk, v_w_pool_grp, v_pool_scale, v_g_gla_head, v_w_pool_proj, v_w_gla_proj, v_w_out, v_g_ffn, v_w_up, v_w_conv, v_b_conv, v_w_down, v_g_final):
    xi, yi, ci = lax.axis_index("x"), lax.axis_index("y"), lax.axis_index("c")
    me = 4 * xi + 2 * yi + ci
    core = jnp.reshape(ci, (1,)).astype(jnp.int32)
    chip = jnp.reshape(2 * xi + yi, (1,)).astype(jnp.int32)
    xs, target = x[0], loss_target[0]

    big = dict(w_in=w_in[0].T, w_pool_proj=w_pool_proj[0], w_gla_proj=w_gla_proj[0], w_out=w_out[0], w_up=w_up[0].T,
               w_down=w_down[0])
    moments = dict(w_in=(m_w_in[0].T, v_w_in[0].T), w_pool_proj=(m_w_pool_proj[0], v_w_pool_proj[0]),
                   w_gla_proj=(m_w_gla_proj[0], v_w_gla_proj[0]), w_out=(m_w_out[0], v_w_out[0]),
                   w_up=(m_w_up[0].T, v_w_up[0].T), w_down=(m_w_down[0], v_w_down[0]))
    names = list(big)
    shards = {k: big[k].astype(BF) for k in names}
    shards["w_gk_up"], shards["w_conv"] = w_gk_up[0], w_conv[0]
    gather_groups = (("w_in", "w_gk_up"), ("w_pool_proj", "w_gla_proj", "w_out"), ("w_up", "w_down", "w_conv"))
    started, token = _split_start("gather_start", [
        ([t for k in g for t in (shards[k], _gather_landing(shards[k], me))], 4 * len(g), _gather_first)
        for g in gather_groups])
    (big["w_in"], *moments["w_in"]), token = _placed_behind(token, [big["w_in"], *moments["w_in"]], "place_w_in")

    def gather_pass(gi, after):
        lands = list(_split_wait(f"gather_wait_{gi}", started[gi], _gather_first, after)[1::2])
        passed, tkn = _split_start(f"gather_pass_{gi}", [(lands, 3 * len(lands), _gather_second)])
        return passed[0], tkn

    def gather_done(gi, passed, after):
        return dict(zip(gather_groups[gi], _split_wait(f"gather_pass_wait_{gi}", passed, _gather_second, after)))

    tok = lambda i, j, k: (i, 0)
    whole = lambda i, j, k: (0, 0)
    kblk = lambda i, j, k: (k, 0)
    ff_seq = (None, None, SEQ, FF_BLK)

    h = _rms_fwd(xs, g_mix + token[:1, :1], "rms_mix")
    wg = gather_done(0, gather_pass(0, h)[0], h)
    wt_cat = _unshard_w_in(wg["w_in"])
    wgk_pad = jnp.pad(wg["w_gk_up"].transpose(1, 0, 2).reshape(GATE_RANK, GLA_DK), ((0, GK_PAD - GATE_RANK), (0, 0)))
    zcat = _mm(h, wt_cat, out_shape=(SEQ, N_CAT), out_dtype=BF, grid=(N_CAT_TILES, 1, 1),
               blk_a=(SEQ, D_MODEL), blk_b=(CAT_TILE, D_MODEL), blk_o=(SEQ, CAT_TILE),
               map_a=whole, map_b=lambda j, i, k: (j, 0), map_o=lambda j, i, k: (0, j), tb=True, name="mm_in")
    la = _gk_fwd(h, wt_cat, wgk_pad, b_gk)
    passed, tkn = gather_pass(1, la)
    o, states = _gla_fwd(zcat, la, tkn)
    wg = gather_done(1, passed, o)
    wpp = wg["w_pool_proj"].transpose(1, 0, 2).reshape(POOL_WIDTH, D_MODEL)
    wgp = wg["w_gla_proj"].reshape(D_MODEL, D_MODEL)
    wout = wg["w_out"].reshape(D_MODEL, D_MODEL)
    og = _post_gla_fwd(o, zcat, g_gla_head)
    ps = _pool_fwd(zcat, w_pool_grp[0], pool_scale)
    passed, tkn = gather_pass(2, (og, ps))
    y_pool, y_gla, mixed, x1, h2 = _mix_out_fwd(ps, og, zcat, xs, wpp, wgp, wout, b_gate, g_ffn, tkn)
    wg = gather_done(2, passed, h2)
    wt_up = wg["w_up"].reshape(2 * D_FF, D_MODEL)
    wdown = wg["w_down"].reshape(D_FF, D_MODEL)
    wconv4 = wg["w_conv"].reshape(2, 4, 3, FF_BLK)
    bconv4 = b_conv.reshape(2, 4, 1, FF_BLK)
    blk4 = lambda b, i, k: (b // 4, b % 4, 0, 0)
    u4, act = _up_conv_fwd(h2, wt_up, wconv4, bconv4)
    loss_part, dx2, dx2_bf, dg_final = _mm_tokens(
        act, wdown, blk_a=(None, 4, TOK_MM_TILE, FF_BLK), map_a=lambda i: (0, 0, i, 0),
        pieces=[(b, b * FF_BLK, FF_BLK) for b in range(4)], res=x1, then=("loss", g_final.reshape(1, D_MODEL), target),
        name="mm_down_loss")

    da = _mm(dx2_bf, wdown, out_shape=(1, 4, SEQ, FF_BLK), out_dtype=BF, grid=(4, 1, 1),
             blk_a=(SEQ, D_MODEL), blk_b=(FF_BLK, D_MODEL), blk_o=ff_seq,
             map_a=whole, map_b=lambda b, i, k: (b, 0), map_o=lambda b, i, k: (0, b, 0, 0), tb=True, name="mm_d_act")
    d_wdown = _mm(act, dx2_bf, out_shape=(D_FF, D_MODEL), out_dtype=BF, grid=(4, 1, 1),
                  blk_a=ff_seq, blk_b=(SEQ, D_MODEL), blk_o=(FF_BLK, D_MODEL),
                  map_a=lambda b, i, k: (0, b, 0, 0), map_b=whole, map_o=lambda b, i, k: (b, 0), ta=True,
                  name="mm_d_wdown")
    du4, d_wconv, d_bconv = _conv_bwd(u4, da, wconv4, bconv4)
    d_wt_up = _mm(du4, h2, out_shape=(2 * D_FF, D_MODEL), out_dtype=BF, grid=(N_DEV, 1, 1),
                  blk_a=ff_seq, blk_b=(SEQ, D_MODEL), blk_o=(FF_BLK, D_MODEL),
                  map_a=blk4, map_b=whole, map_o=lambda b, i, k: (b, 0), ta=True, name="mm_d_wup")
    res = {}

    def to_sibling(keys, parts):
        return [t for k in keys for t in (parts[k], lax.empty((4,) + parts[k].shape[2:], BF))], 4 * len(keys), _reduce_first

    def to_chips(keys, st, after):
        arrays = _split_wait("reduce_wait_" + keys[0], st, _reduce_first, after)
        sums = [_pair_sum(p, r, core, "pair_sum_" + k) for k, p, r in zip(keys, arrays[0::2], arrays[1::2])]
        return [t for s in sums for t in (s, lax.empty((3,) + s.shape[1:], BF))], 3 * len(keys), _reduce_second

    def reduce_start(keys, parts):
        st, tkn = _split_start("reduce_start_" + keys[0], [to_sibling(keys, parts)])
        return st[0], tkn

    def reduce_cross(keys, st, after):
        st2, tkn = _split_start("reduce_cross_" + keys[0], [to_chips(keys, st, after)])
        return st2[0], tkn

    def reduce_done(keys, st2, after):
        arrays = _split_wait("reduce_cross_wait_" + keys[0], st2, _reduce_second, after)
        for k, s, r in zip(keys, arrays[0::2], arrays[1::2]):
            outs = _chip_sum_adamw(s, r, big[k], moments[k][0], moments[k][1], chip, "adamw_" + k)
            res[k] = [(t.T if k in ("w_in", "w_up") else t)[None] for t in outs]

    ffn_keys = ("w_down", "w_up")
    ffn_red, tkn = reduce_start(ffn_keys, dict(w_down=d_wdown.reshape(4, 2, D_FF // N_DEV, D_MODEL),
                                               w_up=d_wt_up.reshape(4, 2, FF_BLK, D_MODEL)))
    dx1, dg_ffn = _mm_tokens(
        du4, wt_up, blk_a=(2, 4, TOK_MM_TILE, FF_BLK), map_a=lambda i: (0, 0, i, 0),
        pieces=[((b // 4, b % 4), b * FF_BLK, FF_BLK) for b in range(N_DEV)], after=tkn, then=("rms_bwd", x1, g_ffn, dx2),
        name="mm_d_h2_rms")

    sq_t = dict(out_shape=(D_MODEL, D_MODEL), grid=(1, 1, N_MM_TILES), blk_a=(MM_TILE, D_MODEL),
                blk_b=(MM_TILE, D_MODEL), blk_o=(D_MODEL, D_MODEL), map_a=kblk, map_b=kblk, map_o=whole, ta=True)
    d_wout = _mm(mixed, dx1, out_dtype=BF, name="mm_d_wout", **sq_t)
    dzcat, dy_pool, dy_gla, db_gate = _mix_bwd(dx1, wout, zcat, b_gate, y_pool, y_gla)
    d_wgp = _mm(og, dy_gla, out_dtype=BF, name="mm_d_wgp", **sq_t)
    mix_keys = ("w_out", "w_gla_proj")
    (ffn_red, mix_red), tkn = _split_start("reduce_cross_w_down", [
        to_chips(ffn_keys, ffn_red, db_gate),
        to_sibling(mix_keys, dict(w_out=d_wout.reshape(4, 2, D_MODEL // N_DEV, D_MODEL),
                                  w_gla_proj=d_wgp.reshape(4, 2, D_MODEL // N_DEV, D_MODEL)))])
    dzcat, d_o, dg_head = _post_gla_bwd(dzcat, dy_gla, wgp, o, zcat, g_gla_head + tkn[:1, :1])
    dzcat, dla = _gla_bwd(dzcat, zcat, la, d_o, states)
    dzcat, d_wgk, db_gk = _gk_bwd(dzcat, dla, h, wt_cat, wgk_pad, b_gk)
    dps = _mm(dy_pool, wpp, out_shape=(SEQ, POOL_WIDTH), out_dtype=F32, grid=(N_MM_TILES, 1, 1),
              blk_a=(MM_TILE, D_MODEL), blk_b=(POOL_WIDTH, D_MODEL), blk_o=(MM_TILE, POOL_WIDTH),
              map_a=tok, map_b=whole, map_o=tok, tb=True, name="mm_d_ps")
    d_wpp = _mm(ps, dy_pool, out_shape=(POOL_WIDTH, D_MODEL), out_dtype=F32, grid=(1, 1, N_MM_TILES),
                blk_a=(MM_TILE, POOL_WIDTH), blk_b=(MM_TILE, D_MODEL), blk_o=(POOL_WIDTH, D_MODEL),
                map_a=kblk, map_b=kblk, map_o=whole, ta=True, name="mm_d_wpp")
    dzcat, d_wgrp, d_scale = _pool_bwd(dzcat, zcat, dps, w_pool_grp[0], pool_scale)
    row = lambda t: t.reshape(1, D_MODEL)
    conv_vec = lambda t: t.reshape(2, 4, 1, FF_BLK)
    small = [("b_gate", db_gate, b_gate, m_b_gate, v_b_gate, False),
             ("w_gk_up", d_wgk.reshape(GATE_RANK, N_DEV, GLA_DK // N_DEV).transpose(1, 0, 2), w_gk_up, m_w_gk_up,
              v_w_gk_up, True),
             ("b_gk", db_gk, b_gk, m_b_gk, v_b_gk, False),
             ("w_pool_grp", d_wgrp, w_pool_grp, m_w_pool_grp, v_w_pool_grp, False),
             ("pool_scale", d_scale, pool_scale, m_pool_scale, v_pool_scale, False),
             ("g_gla_head", dg_head, g_gla_head, m_g_gla_head, v_g_gla_head, False),
             ("g_ffn", dg_ffn, g_ffn, m_g_ffn, v_g_ffn, False),
             ("w_conv", d_wconv.reshape(N_DEV, 3, FF_BLK), w_conv, m_w_conv, v_w_conv, True),
             ("b_conv", d_bconv, conv_vec(b_conv), conv_vec(m_b_conv), conv_vec(v_b_conv), False),
             ("g_final", dg_final, row(g_final), row(m_g_final), row(v_g_final), False)]

    def to_all(parts):
        return [t for p in parts for t in (p, _gather_landing(p, me))], 7 * len(parts), _gather_direct

    (small_sent, mix_red), tkn = _split_start("small_start", [to_all([t[1] for t in small] + [loss_part]),
                                                              to_chips(mix_keys, mix_red, dla)])
    d_wt_cat = _mm(dzcat, h, out_shape=(N_DZ, D_MODEL), out_dtype=BF, grid=(N_DZ // DZ_TILE, 1, 1),
                   blk_a=(SEQ, DZ_TILE), blk_b=(SEQ, D_MODEL), blk_o=(DZ_TILE, D_MODEL),
                   map_a=lambda j, i, k: (0, j), map_b=whole, map_o=lambda j, i, k: (j, 0), ta=True, after=tkn,
                   name="mm_d_wcat")
    in_keys = ("w_in", "w_pool_proj")
    in_red, tkn = reduce_start(in_keys, dict(
        w_in=_shard_d_w_in(d_wt_cat).reshape(4, 2, IN_SHARD, D_MODEL),
        w_pool_proj=d_wpp.reshape(POOL_WIDTH, N_DEV, D_MODEL // N_DEV).transpose(1, 0, 2).astype(BF)
        .reshape(4, 2, POOL_WIDTH, D_MODEL // N_DEV)))
    reduce_done(mix_keys, mix_red, tkn)
    in_red, tkn = reduce_cross(in_keys, in_red, res["w_out"][0])
    grad_x, dg_mix = _mm_tokens(dzcat, wt_cat, blk_a=(TOK_MM_TILE, N_DZ), map_a=lambda i: (i, 0),
                                pieces=[(None, 0, N_DZ)], after=tkn, then=("rms_bwd", xs, g_mix, dx1),
                                name="mm_d_h_rms")
    (g_mix_sent,), tkn = _split_start("g_mix_start", [to_all([dg_mix])])
    reduce_done(ffn_keys, ffn_red, (grad_x, tkn))
    gathered = _split_wait("small_wait", small_sent, _gather_direct, res["w_down"][0])[1::2]
    small.append(("g_mix", dg_mix, g_mix, m_g_mix, v_g_mix, False))
    gathered = list(gathered[:-1]) + [_split_wait("g_mix_wait", g_mix_sent, _gather_direct, gathered[0])[1], gathered[-1]]
    small_out, loss_sum = _small_sum_adamw(jnp.reshape(me, (1,)).astype(jnp.int32),
                                           [(p,) + t[2:] for p, t in zip(gathered, small)], gathered[-1])
    for t, outs in zip(small, small_out):
        res[t[0]] = list(outs)
    res["b_conv"] = [t.reshape(b_conv.shape) for t in res["b_conv"]]
    res["g_final"] = [t.reshape(g_final.shape) for t in res["g_final"]]

    reduce_done(in_keys, in_red, loss_sum)
    loss = loss_sum[0, 0]
    order =["g_mix", "w_in", "b_gate", "w_gk_up", "b_gk", "w_pool_grp", "pool_scale", "g_gla_head", "w_pool_proj",
             "w_gla_proj", "w_out", "g_ffn", "w_up", "w_conv", "b_conv", "w_down", "g_final"]
    return (loss, grad_x[None], *[res[k][0] for k in order], *[res[k][1] for k in order],
            *[res[k][2] for k in order], *[res[k][3] for k in order])
```

```python
import jax
import jax.numpy as jnp
from jax import lax
from jax.experimental import pallas as pl
from jax.experimental.pallas import tpu as pltpu

F32 = jnp.float32
BF = jnp.bfloat16
HIGHEST = lax.Precision.HIGHEST
MESH = pl.DeviceIdType.MESH

N_DEV = 8
SEQ = 2048
D_MODEL = 1024
CHUNK = 64
EPS = 1e-6
POOL_WIDTH = 512
POOL_WINDOWS = (2, 4, 8, 16)
POOL_GD = 128
POOL_HALO = 16
HEADS = 4
HK = 128
HV = 256
GLA_DK = 512
GATE_RANK = 16
GATE_NORM = 16.0
D_FF = 2816
FF_BLK = 704
IN_SHARD = 706
C_QKV, C_GATE, C_OG, C_POOL, C_GK = 0, 2048, 4096, 5120, 5632
N_CAT = 5632
GK_PAD = 128
N_DZ = N_CAT + GK_PAD
R_POOL, R_QKV, R_OG, R_GK, R_GATE = 0, 512, 2560, 3584, 3600

ADAM_LR, ADAM_B1, ADAM_B2, ADAM_EPS, ADAM_WD, ADAM_STEP = 0.001, 0.9, 0.999, 1e-08, 0.01, 10
ADAM_C1 = 1.0 - ADAM_B1 ** ADAM_STEP
ADAM_C2 = 1.0 - ADAM_B2 ** ADAM_STEP

VMEM_BYTES_V7X = 64 * 1024 * 1024
VMEM_LIMIT = VMEM_BYTES_V7X * 3 // 4

TOK_TILE = 256
HALO = 8
GLA_CPS = 4


def _params(*sem):
    return pltpu.CompilerParams(dimension_semantics=sem, vmem_limit_bytes=VMEM_LIMIT)


def _const_spec(shape):
    nd = len(shape)
    return pl.BlockSpec(shape, lambda *_: (0,) * nd)


def _in_hbm(t):
    return pltpu.with_memory_space_constraint(t, pltpu.HBM)


def _out_hbm(shape, dtype):
    return pltpu.HBM(shape, dtype)


def _dot(a, b, ta=False, tb=False):
    dims = (((0 if ta else 1,), (1 if tb else 0,)), ((), ()))
    return lax.dot_general(a.astype(BF), b.astype(BF), dims, preferred_element_type=F32)


def _dot_exact(a, b):
    return jnp.dot(a, b, precision=HIGHEST, preferred_element_type=F32)


def _sigmoid(x):
    return 0.5 * jnp.tanh(0.5 * x) + 0.5


def _mm(a, b, *, out_shape, out_dtype, grid, blk_a, blk_b, blk_o, map_a, map_b, map_o, ta=False, tb=False,
        after=None, name):
    gk = grid[2]
    n_in = 2 + (after is not None)

    def body(*refs):
        a_ref, b_ref, o_ref = refs[0], refs[1], refs[n_in]
        prod = _dot(a_ref[...], b_ref[...], ta, tb)
        if gk == 1:
            o_ref[...] = prod.astype(out_dtype)
        else:
            acc = refs[n_in + 1]
            k = pl.program_id(2)

            @pl.when(k == 0)
            def _():
                acc[...] = prod

            @pl.when(k > 0)
            def _():
                acc[...] += prod

            @pl.when(k == gk - 1)
            def _():
                o_ref[...] = acc[...].astype(out_dtype)

    in_specs = [pl.BlockSpec(blk_a, map_a), pl.BlockSpec(blk_b, map_b)]
    args = [_in_hbm(a), _in_hbm(b)]
    if after is not None:
        in_specs.append(pl.BlockSpec(memory_space=pl.ANY))
        args.append(after)
    return pl.pallas_call(
        body, name=name, grid=grid, in_specs=in_specs, out_specs=pl.BlockSpec(blk_o, map_o),
        out_shape=_out_hbm(out_shape, out_dtype),
        scratch_shapes=[] if gk == 1 else [pltpu.VMEM(tuple(d for d in blk_o if d is not None), F32)],
        compiler_params=_params("parallel", "parallel", "arbitrary"),
    )(*args)


TOK_MM_TILE = 256


def _mm_tokens(a, w, *, blk_a, map_a, pieces, res=None, after=None, then=None, name):
    n_in = 2 + (res is not None) + (after is not None) + (0 if then is None else len(then) - 1)

    def accumulate(ref, part):
        @pl.when(pl.program_id(0) == 0)
        def _():
            ref[...] = part

        @pl.when(pl.program_id(0) > 0)
        def _():
            ref[...] += part

    def body(*refs):
        a_ref, w_ref = refs[:2]
        extra, outs = refs[n_in - (0 if then is None else len(then) - 1):n_in], refs[n_in:]
        total = None
        for idx, row, n in pieces:
            av = a_ref[...] if idx is None else a_ref[idx]
            prod = _dot(av, w_ref[row:row + n, :])
            total = prod if total is None else total + prod
        if res is not None:
            total = total + refs[2][...]
        if then is None:
            outs[0][...] = total
        elif then[0] == "rms_bwd":
            dx, part = _rms_bwd_tile(total, extra[0][...], extra[1][...], extra[2][...])
            outs[0][...] = dx
            accumulate(outs[1], part)
        else:
            lpart, dx, part = _loss_tile(total, extra[0][...], extra[1][...])
            outs[1][...] = dx
            outs[2][...] = dx.astype(BF)
            accumulate(outs[0], lpart)
            accumulate(outs[3], part)

    tile = pl.BlockSpec((TOK_MM_TILE, D_MODEL), lambda i: (i, 0))
    vec = _const_spec((1, D_MODEL))
    big = _out_hbm((SEQ, D_MODEL), F32)
    small = _out_hbm((1, D_MODEL), F32)
    in_specs = [pl.BlockSpec(blk_a, map_a), pl.BlockSpec(w.shape, lambda i: (0, 0), pipeline_mode=pl.Buffered(1))]
    args = [a, w]
    if res is not None:
        in_specs.append(tile)
        args.append(res)
    if after is not None:
        in_specs.append(pl.BlockSpec(memory_space=pl.ANY))
        args.append(after)
    if then is None:
        out_specs, out_shape = tile, big
    elif then[0] == "rms_bwd":
        in_specs += [tile, vec, tile]
        out_specs, out_shape = [tile, vec], [big, small]
    else:
        in_specs += [vec, tile]
        out_specs = [_const_spec((1, 128)), tile, tile, vec]
        out_shape = [_out_hbm((1, 128), F32), big, _out_hbm((SEQ, D_MODEL), BF), small]
    if then is not None:
        args += list(then[1:])
    return pl.pallas_call(
        body, name=name, grid=(SEQ // TOK_MM_TILE,), in_specs=in_specs, out_specs=out_specs, out_shape=out_shape,
        compiler_params=_params("parallel" if then is None else "arbitrary"),
    )(*[_in_hbm(t) for t in args])


def _rms_fwd(x, g, after, name):
    def body(x_ref, g_ref, after_ref, o_ref):
        del after_ref
        xv = x_ref[...]
        r = lax.rsqrt(jnp.mean(xv * xv, axis=-1, keepdims=True) + EPS)
        o_ref[...] = (xv * r * g_ref[...]).astype(BF)

    tile = pl.BlockSpec((TOK_TILE, D_MODEL), lambda i: (i, 0))
    return pl.pallas_call(
        body, name=name, grid=(SEQ // TOK_TILE,),
        in_specs=[tile, _const_spec((1, D_MODEL)), pl.BlockSpec(memory_space=pl.ANY)], out_specs=tile,
        out_shape=_out_hbm((SEQ, D_MODEL), BF), compiler_params=_params("parallel"),
    )(*map(_in_hbm, (x, g)), after)


def _rms_bwd_tile(dyv, xv, gv, dresv):
    r = lax.rsqrt(jnp.mean(xv * xv, axis=-1, keepdims=True) + EPS)
    xn = xv * r
    dxn = dyv * gv
    return dresv + r * (dxn - xn * jnp.mean(dxn * xn, axis=-1, keepdims=True)), jnp.sum(dyv * xn, axis=0, keepdims=True)


def _loss_tile(xv, gv, tv):
    r = lax.rsqrt(jnp.mean(xv * xv, axis=-1, keepdims=True) + EPS)
    xn = xv * r
    err = xn * gv - tv
    lpart = jnp.full((1, 128), 0.5 * jnp.sum(jnp.mean(err * err, axis=-1, keepdims=True)), F32)
    dyv = err * (1.0 / D_MODEL)
    dxn = dyv * gv
    return lpart, r * (dxn - xn * jnp.mean(dxn * xn, axis=-1, keepdims=True)), jnp.sum(dyv * xn, axis=0, keepdims=True)


def _pool_counts(w):
    pos = lax.broadcasted_iota(jnp.int32, (SEQ, 1), 0).astype(F32)
    return jnp.minimum(pos + 1.0, float(w))


def _pool_window(u, w, ext):
    ext[pl.ds(POOL_HALO, SEQ), :] = u
    win = u
    for j in range(1, w):
        win = win + ext[pl.ds(POOL_HALO - j, SEQ), :]
    return win / _pool_counts(w) - u


def _pool_fwd(zcat, w_grp, scale):
    def body(z_ref, w_ref, s_ref, o_ref, ext):
        ext[pl.ds(0, POOL_HALO), :] = jnp.zeros((POOL_HALO, POOL_GD), F32)
        for g, w in enumerate(POOL_WINDOWS):
            cols = slice(g * POOL_GD, (g + 1) * POOL_GD)
            p = _pool_window(z_ref[:, cols].astype(F32), w, ext)
            o_ref[:, cols] = (_dot(p, w_ref[g]) * s_ref[:, cols]).astype(BF)

    return pl.pallas_call(
        body, name="pool_fwd", grid=(1,),
        in_specs=[pl.BlockSpec((SEQ, POOL_WIDTH), lambda i: (0, C_POOL // POOL_WIDTH)),
                  _const_spec((4, POOL_GD, POOL_GD)), _const_spec((1, POOL_WIDTH))],
        out_specs=_const_spec((SEQ, POOL_WIDTH)), out_shape=_out_hbm((SEQ, POOL_WIDTH), BF),
        scratch_shapes=[pltpu.VMEM((POOL_HALO + SEQ, POOL_GD), F32)], compiler_params=_params("arbitrary"),
    )(*map(_in_hbm, (zcat, w_grp, scale)))


def _pool_bwd(dzcat, zcat, dps, w_grp, scale):
    def body(dz_in, z_ref, dps_ref, w_ref, s_ref, dz_ref, dw_ref, dsc_ref, ext, ext2):
        del dz_in
        ext[pl.ds(0, POOL_HALO), :] = jnp.zeros((POOL_HALO, POOL_GD), F32)
        ext2[pl.ds(SEQ, POOL_HALO), :] = jnp.zeros((POOL_HALO, POOL_GD), F32)
        for g, w in enumerate(POOL_WINDOWS):
            cols = slice(g * POOL_GD, (g + 1) * POOL_GD)
            p = _pool_window(z_ref[:, cols].astype(F32), w, ext)
            wg = w_ref[g]
            pg = _dot(p, wg)
            dpsv = dps_ref[:, cols]
            dsc_ref[:, cols] = jnp.sum(dpsv * pg, axis=0, keepdims=True)
            dpg = dpsv * s_ref[:, cols]
            dw_ref[g] = _dot(p, dpg, ta=True)
            dp = _dot(dpg, wg, tb=True)
            dpc = dp / _pool_counts(w)
            ext2[pl.ds(0, SEQ), :] = dpc
            du = dpc
            for j in range(1, w):
                du = du + ext2[pl.ds(j, SEQ), :]
            dz_ref[:, cols] = (du - dp).astype(BF)

    return pl.pallas_call(
        body, name="pool_bwd", grid=(1,),
        in_specs=[pl.BlockSpec(memory_space=pl.ANY),
                  pl.BlockSpec((SEQ, POOL_WIDTH), lambda i: (0, C_POOL // POOL_WIDTH)),
                  _const_spec((SEQ, POOL_WIDTH)), _const_spec((4, POOL_GD, POOL_GD)), _const_spec((1, POOL_WIDTH))],
        out_specs=[pl.BlockSpec((SEQ, POOL_WIDTH), lambda i: (0, C_POOL // POOL_WIDTH)),
                   _const_spec((4, POOL_GD, POOL_GD)), _const_spec((1, POOL_WIDTH))],
        out_shape=[_out_hbm((SEQ, N_DZ), BF), _out_hbm((4, POOL_GD, POOL_GD), F32),
                   _out_hbm((1, POOL_WIDTH), F32)],
        scratch_shapes=[pltpu.VMEM((POOL_HALO + SEQ, POOL_GD), F32), pltpu.VMEM((SEQ + POOL_HALO, POOL_GD), F32)],
        input_output_aliases={0: 0}, compiler_params=_params("arbitrary"),
    )(*map(_in_hbm, (dzcat, zcat, dps, w_grp, scale)))


GK_TILE = 512


GK_ROWS = pl.BlockSpec((GK_PAD, D_MODEL), lambda i: (C_GK // GK_PAD, 0))


def _gk_fwd(h, wt_cat, wgk_pad, b_gk):
    def body(h_ref, wt_ref, w_ref, b_ref, la_ref):
        z_gk = _dot(h_ref[...], wt_ref[...], tb=True)
        pre = _dot(z_gk, w_ref[...]) + b_ref[...]
        la_ref[...] = (jnp.minimum(pre, 0.0) - jnp.log(1.0 + jnp.exp(-jnp.abs(pre)))) * (1.0 / GATE_NORM)

    return pl.pallas_call(
        body, name="gk_fwd", grid=(SEQ // GK_TILE,),
        in_specs=[pl.BlockSpec((GK_TILE, D_MODEL), lambda i: (i, 0)), GK_ROWS,
                  _const_spec((GK_PAD, GLA_DK)), _const_spec((1, GLA_DK))],
        out_specs=pl.BlockSpec((GK_TILE, GLA_DK), lambda i: (i, 0)),
        out_shape=_out_hbm((SEQ, GLA_DK), F32), compiler_params=_params("parallel"),
    )(*map(_in_hbm, (h, wt_cat, wgk_pad, b_gk)))


def _gk_bwd(dzcat, dla, h, wt_cat, wgk_pad, b_gk):
    def body(dz_in, dla_ref, h_ref, wt_ref, w_ref, b_ref, dz_ref, dw_ref, db_ref):
        del dz_in
        wv = w_ref[...]
        z_gk = _dot(h_ref[...], wt_ref[...], tb=True)
        pre = _dot(z_gk, wv) + b_ref[...]
        dpre = dla_ref[...] * (1.0 / GATE_NORM) * (1.0 - _sigmoid(pre))
        dz_ref[...] = _dot(dpre, wv, tb=True).astype(BF)
        dwp = _dot(z_gk, dpre, ta=True)[:GATE_RANK]
        dbp = jnp.sum(dpre, axis=0, keepdims=True)

        @pl.when(pl.program_id(0) == 0)
        def _():
            dw_ref[...] = dwp
            db_ref[...] = dbp

        @pl.when(pl.program_id(0) > 0)
        def _():
            dw_ref[...] += dwp
            db_ref[...] += dbp

    return pl.pallas_call(
        body, name="gk_bwd", grid=(SEQ // GK_TILE,),
        in_specs=[pl.BlockSpec(memory_space=pl.ANY), pl.BlockSpec((GK_TILE, GLA_DK), lambda i: (i, 0)),
                  pl.BlockSpec((GK_TILE, D_MODEL), lambda i: (i, 0)), GK_ROWS, _const_spec((GK_PAD, GLA_DK)),
                  _const_spec((1, GLA_DK))],
        out_specs=[pl.BlockSpec((GK_TILE, GK_PAD), lambda i: (i, C_GK // GK_PAD)), _const_spec((GATE_RANK, GLA_DK)),
                   _const_spec((1, GLA_DK))],
        out_shape=[_out_hbm((SEQ, N_DZ), BF), _out_hbm((GATE_RANK, GLA_DK), F32),
                   _out_hbm((1, GLA_DK), F32)],
        input_output_aliases={0: 0}, compiler_params=_params("arbitrary"),
    )(*map(_in_hbm, (dzcat, dla, h, wt_cat, wgk_pad, b_gk)))


GLA_ROWS = GLA_CPS * CHUNK
GLA_STEPS = SEQ // GLA_ROWS
QKV_W = 2048


def _tri():
    return lax.broadcasted_iota(jnp.int32, (CHUNK, CHUNK), 0) >= lax.broadcasted_iota(jnp.int32, (CHUNK, CHUNK), 1)


def _chunk_cumsum(la_ref, rows):
    return _dot_exact(_tri().astype(F32), la_ref[rows, :])


def _gla_chunk(qkv_ref, la_ref, rows, h, bc_all):
    tri = _tri()
    q = qkv_ref[rows, h * HK:(h + 1) * HK].astype(F32) * (HK ** -0.5)
    k = qkv_ref[rows, GLA_DK + h * HK:GLA_DK + (h + 1) * HK].astype(F32)
    v = qkv_ref[rows, 2 * GLA_DK + h * HV:2 * GLA_DK + (h + 1) * HV].astype(BF)
    la = la_ref[rows, h * HK:(h + 1) * HK]
    bc = bc_all[:, h * HK:(h + 1) * HK]
    e_pos, e_neg = jnp.exp(bc), jnp.exp(-bc)
    dl = jnp.exp(jnp.sum(la, axis=0, keepdims=True))
    q_fw, q_bw, k_fw, k_bw = q * e_pos, q * e_neg, k * e_neg, k * e_pos
    scores = jnp.where(tri, _dot(q_fw, k_fw, tb=True), _dot(q_bw, k_bw, tb=True))
    return tri, v, e_pos, e_neg, dl, q_fw, q_bw, k_fw, k_bw, scores


def _gla_fwd(zcat, la, after):
    def body(qkv_ref, la_ref, after_ref, o_ref, st_ref, state):
        del after_ref

        @pl.when(pl.program_id(0) == 0)
        def _():
            state[...] = jnp.zeros_like(state)

        for c in range(GLA_CPS):
            rows = slice(c * CHUNK, (c + 1) * CHUNK)
            bc_all = _chunk_cumsum(la_ref, rows)
            for h in range(HEADS):
                _, v, _, _, dl, q_fw, _, k_fw, _, scores = _gla_chunk(qkv_ref, la_ref, rows, h, bc_all)
                st = state[h]
                st_ref[c, h] = st
                o_ref[rows, h * HV:(h + 1) * HV] = _dot(scores, v) + _dot(q_fw, st, tb=True)
                state[h] = st * dl + _dot(v, k_fw * dl, ta=True)

    return pl.pallas_call(
        body, name="gla_fwd", grid=(GLA_STEPS,),
        in_specs=[pl.BlockSpec((GLA_ROWS, QKV_W), lambda i: (i, 0)), pl.BlockSpec((GLA_ROWS, GLA_DK), lambda i: (i, 0)),
                  pl.BlockSpec(memory_space=pl.ANY)],
        out_specs=[pl.BlockSpec((GLA_ROWS, D_MODEL), lambda i: (i, 0)),
                   pl.BlockSpec((GLA_CPS, HEADS, HV, HK), lambda i: (i, 0, 0, 0))],
        out_shape=[_out_hbm((SEQ, D_MODEL), F32),
                   _out_hbm((SEQ // CHUNK, HEADS, HV, HK), F32)],
        scratch_shapes=[pltpu.VMEM((HEADS, HV, HK), F32)], compiler_params=_params("arbitrary"),
    )(*map(_in_hbm, (zcat, la)), after)


def _gla_bwd(dzcat, zcat, la, d_o, states):
    def body(dz_in, qkv_ref, la_ref, do_ref, st_ref, dqkv_ref, dla_ref, dstate):
        del dz_in

        @pl.when(pl.program_id(0) == 0)
        def _():
            dstate[...] = jnp.zeros_like(dstate)

        last_row = lax.broadcasted_iota(jnp.int32, (CHUNK, HK), 0) == CHUNK - 1
        upper = (lax.broadcasted_iota(jnp.int32, (CHUNK, CHUNK), 0)
                 <= lax.broadcasted_iota(jnp.int32, (CHUNK, CHUNK), 1)).astype(F32)
        for c in reversed(range(GLA_CPS)):
            rows = slice(c * CHUNK, (c + 1) * CHUNK)
            bc_all = _chunk_cumsum(la_ref, rows)
            dbs = []
            for h in range(HEADS):
                tri, v, e_pos, e_neg, dl, q_fw, q_bw, k_fw, k_bw, scores = _gla_chunk(qkv_ref, la_ref, rows, h, bc_all)
                st = st_ref[c, h]
                dst = dstate[h]
                d_out = do_ref[rows, h * HV:(h + 1) * HV].astype(BF)
                k_dec = k_fw * dl
                dp = _dot(d_out, v, tb=True)
                dp_fw = jnp.where(tri, dp, 0.0)
                dp_bw = jnp.where(tri, 0.0, dp)
                dv = _dot(scores, d_out, ta=True) + _dot(k_dec, dst, tb=True)
                dk_dec = _dot(v, dst)
                dq_fw = _dot(dp_fw, k_fw) + _dot(d_out, st)
                dk_fw = _dot(dp_fw, q_fw, ta=True) + dk_dec * dl
                dq_bw = _dot(dp_bw, k_bw)
                dk_bw = _dot(dp_bw, q_bw, ta=True)
                ddl = jnp.sum(st * dst, axis=0, keepdims=True) + jnp.sum(k_fw * dk_dec, axis=0, keepdims=True)
                dstate[h] = dst * dl + _dot(d_out, q_fw, ta=True)
                dq = (dq_fw * e_pos + dq_bw * e_neg) * (HK ** -0.5)
                dk = dk_fw * e_neg + dk_bw * e_pos
                dbs.append(dq_fw * q_fw - dk_fw * k_fw - dq_bw * q_bw + dk_bw * k_bw + jnp.where(last_row, ddl * dl, 0.0))
                dqkv_ref[rows, h * HK:(h + 1) * HK] = dq.astype(BF)
                dqkv_ref[rows, GLA_DK + h * HK:GLA_DK + (h + 1) * HK] = dk.astype(BF)
                dqkv_ref[rows, 2 * GLA_DK + h * HV:2 * GLA_DK + (h + 1) * HV] = dv.astype(BF)
            dla_ref[rows, :] = _dot_exact(upper, jnp.concatenate(dbs, axis=1))

    rev = lambda i: (GLA_STEPS - 1 - i, 0)
    return pl.pallas_call(
        body, name="gla_bwd", grid=(GLA_STEPS,),
        in_specs=[pl.BlockSpec(memory_space=pl.ANY), pl.BlockSpec((GLA_ROWS, QKV_W), rev),
                  pl.BlockSpec((GLA_ROWS, GLA_DK), rev), pl.BlockSpec((GLA_ROWS, D_MODEL), rev),
                  pl.BlockSpec((GLA_CPS, HEADS, HV, HK), lambda i: (GLA_STEPS - 1 - i, 0, 0, 0))],
        out_specs=[pl.BlockSpec((GLA_ROWS, QKV_W), rev), pl.BlockSpec((GLA_ROWS, GLA_DK), rev)],
        out_shape=[_out_hbm((SEQ, N_DZ), BF), _out_hbm((SEQ, GLA_DK), F32)],
        scratch_shapes=[pltpu.VMEM((HEADS, HV, HK), F32)], input_output_aliases={0: 0},
        compiler_params=_params("arbitrary"),
    )(*map(_in_hbm, (dzcat, zcat, la, d_o, states)))


def _silu_parts(x):
    s = _sigmoid(x)
    return x * s, s * (1.0 + x * (1.0 - s))


def _post_gla_fwd(o, zcat, g_head):
    def body(o_ref, zog_ref, g_ref, out_ref):
        for h in range(HEADS):
            cols = slice(h * HV, (h + 1) * HV)
            ov = o_ref[:, cols]
            r = lax.rsqrt(jnp.mean(ov * ov, axis=-1, keepdims=True) + EPS)
            act, _ = _silu_parts(zog_ref[:, cols].astype(F32))
            out_ref[:, cols] = (ov * r * g_ref[...] * act).astype(BF)

    tile = pl.BlockSpec((TOK_TILE, D_MODEL), lambda i: (i, 0))
    return pl.pallas_call(
        body, name="post_gla_fwd", grid=(SEQ // TOK_TILE,),
        in_specs=[tile, pl.BlockSpec((TOK_TILE, D_MODEL), lambda i: (i, C_OG // D_MODEL)), _const_spec((1, HV))],
        out_specs=tile, out_shape=_out_hbm((SEQ, D_MODEL), BF), compiler_params=_params("parallel"),
    )(*map(_in_hbm, (o, zcat, g_head)))


def _post_gla_bwd(dzcat, dy_gla, w_gla_proj, o, zcat, g_head, after):
    def body(dz_in, dyg_ref, w_ref, o_ref, zog_ref, g_ref, after_ref, dz_ref, do_ref, dg_ref):
        del dz_in, after_ref
        dog = _dot(dyg_ref[...], w_ref[...], tb=True)
        gpart = jnp.zeros((1, HV), F32)
        gv = g_ref[...]
        for h in range(HEADS):
            cols = slice(h * HV, (h + 1) * HV)
            ov = o_ref[:, cols]
            r = lax.rsqrt(jnp.mean(ov * ov, axis=-1, keepdims=True) + EPS)
            on = ov * r
            act, dact = _silu_parts(zog_ref[:, cols].astype(F32))
            dogv = dog[:, cols]
            dz_ref[:, cols] = (dogv * on * gv * dact).astype(BF)
            d_on_g = dogv * act
            gpart = gpart + jnp.sum(d_on_g * on, axis=0, keepdims=True)
            dxn = d_on_g * gv
            do_ref[:, cols] = (r * (dxn - on * jnp.mean(dxn * on, axis=-1, keepdims=True))).astype(BF)

        @pl.when(pl.program_id(0) == 0)
        def _():
            dg_ref[...] = gpart

        @pl.when(pl.program_id(0) > 0)
        def _():
            dg_ref[...] += gpart

    tile = pl.BlockSpec((TOK_TILE, D_MODEL), lambda i: (i, 0))
    ogspec = pl.BlockSpec((TOK_TILE, D_MODEL), lambda i: (i, C_OG // D_MODEL))
    return pl.pallas_call(
        body, name="post_gla_bwd", grid=(SEQ // TOK_TILE,),
        in_specs=[pl.BlockSpec(memory_space=pl.ANY), tile, _const_spec((D_MODEL, D_MODEL)), tile, ogspec,
                  _const_spec((1, HV)), pl.BlockSpec(memory_space=pl.ANY)],
        out_specs=[ogspec, tile, _const_spec((1, HV))],
        out_shape=[_out_hbm((SEQ, N_DZ), BF), _out_hbm((SEQ, D_MODEL), BF),
                   _out_hbm((1, HV), F32)],
        input_output_aliases={0: 0}, compiler_params=_params("arbitrary"),
    )(*map(_in_hbm, (dzcat, dy_gla, w_gla_proj, o, zcat, g_head)), after)


GATE_W = 2 * D_MODEL


def _mix_out_fwd(ps, og, zcat, x, w_pool_proj, w_gla_proj, w_out, b_gate, g_ffn, after):
    def body(ps_ref, og_ref, zg_ref, x_ref, wpp_ref, wgp_ref, wout_ref, b_ref, g_ref, after_ref,
             yp_ref, yg_ref, mixed_ref, x1_ref, h2_ref):
        del after_ref
        y_pool = _dot(ps_ref[...], wpp_ref[...])
        y_gla = _dot(og_ref[...], wgp_ref[...])
        yp_ref[...] = y_pool.astype(BF)
        yg_ref[...] = y_gla.astype(BF)
        g0 = _sigmoid(zg_ref[:, :D_MODEL].astype(F32) + b_ref[:, :D_MODEL])
        g1 = _sigmoid(zg_ref[:, D_MODEL:].astype(F32) + b_ref[:, D_MODEL:])
        mixed = (g0 * y_pool + g1 * y_gla).astype(BF)
        mixed_ref[...] = mixed
        x1 = x_ref[...] + _dot(mixed, wout_ref[...])
        x1_ref[...] = x1
        r = lax.rsqrt(jnp.mean(x1 * x1, axis=-1, keepdims=True) + EPS)
        h2_ref[...] = (x1 * r * g_ref[...]).astype(BF)

    tile = pl.BlockSpec((TOK_TILE, D_MODEL), lambda i: (i, 0))
    resident = lambda shape: pl.BlockSpec(shape, lambda i: (0, 0), pipeline_mode=pl.Buffered(1))
    f32, bf16 = _out_hbm((SEQ, D_MODEL), F32), _out_hbm((SEQ, D_MODEL), BF)
    return pl.pallas_call(
        body, name="mix_out_fwd", grid=(SEQ // TOK_TILE,),
        in_specs=[pl.BlockSpec((TOK_TILE, POOL_WIDTH), lambda i: (i, 0)), tile,
                  pl.BlockSpec((TOK_TILE, GATE_W), lambda i: (i, C_GATE // GATE_W)), tile,
                  resident((POOL_WIDTH, D_MODEL)), resident((D_MODEL, D_MODEL)), resident((D_MODEL, D_MODEL)),
                  _const_spec((1, GATE_W)), _const_spec((1, D_MODEL)), pl.BlockSpec(memory_space=pl.ANY)],
        out_specs=[tile] * 5, out_shape=[bf16, bf16, bf16, f32, bf16], compiler_params=_params("parallel"),
    )(*map(_in_hbm, (ps, og, zcat, x, w_pool_proj, w_gla_proj, w_out, b_gate, g_ffn)), after)


def _mix_bwd(dx1, w_out, zcat, b_gate, y_pool, y_gla):
    def body(dx_ref, w_ref, zg_ref, b_ref, yp_ref, yg_ref, dz_ref, dyp_ref, dyg_ref, db_ref):
        dm = _dot(dx_ref[...], w_ref[...], tb=True)
        g0 = _sigmoid(zg_ref[:, :D_MODEL].astype(F32) + b_ref[:, :D_MODEL])
        g1 = _sigmoid(zg_ref[:, D_MODEL:].astype(F32) + b_ref[:, D_MODEL:])
        dyp_ref[...] = (dm * g0).astype(BF)
        dyg_ref[...] = (dm * g1).astype(BF)
        dz0 = dm * yp_ref[...].astype(F32) * g0 * (1.0 - g0)
        dz1 = dm * yg_ref[...].astype(F32) * g1 * (1.0 - g1)
        dz_ref[:, :D_MODEL] = dz0.astype(BF)
        dz_ref[:, D_MODEL:] = dz1.astype(BF)
        b0 = jnp.sum(dz0, axis=0, keepdims=True)
        b1 = jnp.sum(dz1, axis=0, keepdims=True)

        @pl.when(pl.program_id(0) == 0)
        def _():
            db_ref[:, :D_MODEL] = b0
            db_ref[:, D_MODEL:] = b1

        @pl.when(pl.program_id(0) > 0)
        def _():
            db_ref[:, :D_MODEL] += b0
            db_ref[:, D_MODEL:] += b1

    tile = pl.BlockSpec((TOK_TILE, D_MODEL), lambda i: (i, 0))
    gspec = pl.BlockSpec((TOK_TILE, GATE_W), lambda i: (i, C_GATE // GATE_W))
    return pl.pallas_call(
        body, name="mix_bwd", grid=(SEQ // TOK_TILE,),
        in_specs=[tile, _const_spec((D_MODEL, D_MODEL)), gspec, _const_spec((1, GATE_W)), tile, tile],
        out_specs=[gspec, tile, tile, _const_spec((1, GATE_W))],
        out_shape=[_out_hbm((SEQ, N_DZ), BF), _out_hbm((SEQ, D_MODEL), BF),
                   _out_hbm((SEQ, D_MODEL), BF), _out_hbm((1, GATE_W), F32)],
        compiler_params=_params("arbitrary"),
    )(*map(_in_hbm, (dx1, w_out, zcat, b_gate, y_pool, y_gla)))


N_TOK_TILES = SEQ // TOK_TILE
HALO_PER_TILE = TOK_TILE // HALO


LANE_TILES = tuple((lo, min(128, FF_BLK - lo)) for lo in range(0, FF_BLK, 128))


def _taps(w_ref, b_ref, half, lanes, rows):
    shape = (rows, lanes.stop - lanes.start)
    return ([jnp.broadcast_to(w_ref[half, j:j + 1, lanes], shape) for j in range(3)],
            jnp.broadcast_to(b_ref[half, :, lanes], shape))


def _conv_strips(u_ref, ub_ref, ua_ref, taps, lanes, width, n_strips, first):
    row = lax.broadcasted_iota(jnp.int32, (HALO, width), 0)
    prev = [[pltpu.roll(jnp.where(first, 0.0, ub_ref[half, :, lanes]), k, 0) for k in (1, 2)] for half in range(2)]
    for s in range(n_strips + (ua_ref is not None)):
        u3, conv = [], []
        for half in range(2):
            cur = u_ref[half, s * HALO:(s + 1) * HALO, lanes] if s < n_strips else ua_ref[half, :, lanes]
            rolled = [pltpu.roll(cur, k, 0) for k in (1, 2)]
            frames = [jnp.where(row >= 2, rolled[1], prev[half][1]), jnp.where(row >= 1, rolled[0], prev[half][0]), cur]
            prev[half] = rolled
            w3, bias = taps[half]
            u3.append(frames)
            conv.append(bias + frames[0] * w3[0] + frames[1] * w3[1] + frames[2] * w3[2])
        yield s, u3, conv


def _pair_specs(pairs):
    tile = pl.BlockSpec((pairs, None, TOK_TILE, FF_BLK), lambda b, i: (0, b, i, 0))
    before = pl.BlockSpec((pairs, None, HALO, FF_BLK), lambda b, i: (0, b, jnp.maximum(i * HALO_PER_TILE - 1, 0), 0))
    after = pl.BlockSpec((pairs, None, HALO, FF_BLK),
                         lambda b, i: (0, b, jnp.minimum((i + 1) * HALO_PER_TILE, SEQ // HALO - 1), 0))

    def vec(rows):
        return pl.BlockSpec((2, None, rows, FF_BLK), lambda b, i: (0, b, 0, 0))

    return tile, before, after, vec


N_STRIPS = TOK_TILE // HALO


def _up_conv_fwd(h2, wt_up, w_conv, b_conv):
    steps = N_TOK_TILES // 2

    def body(h_ref, h_next, wg_ref, wv_ref, w_ref, b_ref, u_ref, a_ref, buf_a, buf_b, carry):
        j = pl.program_id(1)

        def project(hv, buf):
            buf[0] = _dot(hv, wg_ref[...], tb=True)
            buf[1] = _dot(hv, wv_ref[...], tb=True)

        def conv(buf, row0):
            u_ref[:, row0:row0 + TOK_TILE, :] = buf[...]
            for lo, width in LANE_TILES:
                lanes = slice(lo, lo + width)
                taps = [_taps(w_ref, b_ref, half, lanes, HALO) for half in range(2)]
                pending = None
                for s, _, (cg, cv) in _conv_strips(buf, carry, None, taps, lanes, width, N_STRIPS, False):
                    act = cg * _sigmoid(cg) * cv
                    if s % 2 == 0:
                        pending = act
                    else:
                        a_ref[0, row0 + (s - 1) * HALO:row0 + (s + 1) * HALO, lanes] = (
                            jnp.concatenate([pending, act], axis=0).astype(BF))
            carry[...] = buf[:, TOK_TILE - HALO:, :]

        @pl.when(j == 0)
        def _():
            project(h_ref[0:TOK_TILE, :], buf_a)
            carry[...] = jnp.zeros_like(carry)

        project(h_ref[TOK_TILE:, :], buf_b)
        conv(buf_a, 0)
        project(h_next[...], buf_a)
        conv(buf_b, TOK_TILE)

    w_blk = lambda half: pl.BlockSpec((FF_BLK, D_MODEL), lambda b, j: (b + 4 * half, 0))
    vec = lambda rows: pl.BlockSpec((2, None, rows, FF_BLK), lambda b, j: (0, b, 0, 0))
    u_buf = pltpu.VMEM((2, TOK_TILE, FF_BLK), F32)
    return pl.pallas_call(
        body, name="up_conv_fwd", grid=(4, steps),
        in_specs=[pl.BlockSpec((2 * TOK_TILE, D_MODEL), lambda b, j: (j, 0)),
                  pl.BlockSpec((TOK_TILE, D_MODEL), lambda b, j: (jnp.minimum(2 * j + 2, N_TOK_TILES - 1), 0)),
                  w_blk(0), w_blk(1), vec(3), vec(1)],
        out_specs=[pl.BlockSpec((2, None, 2 * TOK_TILE, FF_BLK), lambda b, j: (0, b, j, 0)),
                   pl.BlockSpec((1, None, 2 * TOK_TILE, FF_BLK), lambda b, j: (0, b, j, 0))],
        out_shape=[_out_hbm((2, 4, SEQ, FF_BLK), F32), _out_hbm((1, 4, SEQ, FF_BLK), BF)],
        scratch_shapes=[u_buf, u_buf, pltpu.VMEM((2, HALO, FF_BLK), F32)],
        compiler_params=_params("parallel", "arbitrary"),
    )(*map(_in_hbm, (h2, h2, wt_up, wt_up, w_conv, b_conv)))


def _conv_bwd(u, da, w_conv, b_conv):
    def body(u_ref, ub_ref, ua_ref, da_ref, daa_ref, w_ref, b_ref, du_ref, dw_ref, db_ref):
        i = pl.program_id(1)

        @pl.when(i == 0)
        def _():
            dw_ref[...] = jnp.zeros_like(dw_ref)
            db_ref[...] = jnp.zeros_like(db_ref)

        for lo, width in LANE_TILES:
            lanes = slice(lo, lo + width)
            row = lax.broadcasted_iota(jnp.int32, (HALO, width), 0)
            taps = [_taps(w_ref, b_ref, half, lanes, HALO) for half in range(2)]
            acc_w = [[jnp.zeros((HALO, width), F32) for _ in range(3)] for _ in range(2)]
            acc_b = [jnp.zeros((HALO, width), F32) for _ in range(2)]
            da_pair, pending = None, [None, None]
            dc_prev, up_prev = [None, None], [None, None]
            for s, u3, (cg, cv) in _conv_strips(u_ref, ub_ref, ua_ref, taps, lanes, width, N_STRIPS, i == 0):
                act, dact = _silu_parts(cg)
                if s == N_STRIPS:
                    da = jnp.where(i < N_TOK_TILES - 1, daa_ref[0, :, lanes].astype(F32), 0.0)
                elif s % 2 == 0:
                    da_pair = da_ref[0, s * HALO:(s + 2) * HALO, lanes].astype(F32)
                    da = da_pair[:HALO]
                else:
                    da = da_pair[HALO:]
                dc = (da * cv * dact, da * act)
                for half in range(2):
                    up = [pltpu.roll(dc[half], HALO - k, 0) for k in (1, 2)]
                    if s < N_STRIPS:
                        for j in range(3):
                            acc_w[half][j] = acc_w[half][j] + dc[half] * u3[half][j]
                        acc_b[half] = acc_b[half] + dc[half]
                    if s >= 1:
                        w3 = taps[half][0]
                        du = (dc_prev[half] * w3[2] + jnp.where(row < HALO - 1, up_prev[half][0], up[0]) * w3[1]
                              + jnp.where(row < HALO - 2, up_prev[half][1], up[1]) * w3[0])
                        if (s - 1) % 2 == 0:
                            pending[half] = du
                        else:
                            du_ref[half, (s - 2) * HALO:s * HALO, lanes] = jnp.concatenate([pending[half], du],
                                                                                           axis=0).astype(BF)
                    dc_prev[half], up_prev[half] = dc[half], up
            for half in range(2):
                for j in range(3):
                    dw_ref[half, j:j + 1, lanes] += jnp.sum(acc_w[half][j], axis=0, keepdims=True)
                db_ref[half, :, lanes] += jnp.sum(acc_b[half], axis=0, keepdims=True)

    tile, before, after, vec = _pair_specs(2)
    da_tile, _, da_after_spec, _ = _pair_specs(1)
    return pl.pallas_call(
        body, name="conv_bwd", grid=(4, N_TOK_TILES),
        in_specs=[tile, before, after, da_tile, da_after_spec, vec(3), vec(1)],
        out_specs=[tile, vec(3), vec(1)],
        out_shape=[_out_hbm((2, 4, SEQ, FF_BLK), BF), _out_hbm((2, 4, 3, FF_BLK), F32),
                   _out_hbm((2, 4, 1, FF_BLK), F32)],
        compiler_params=_params("parallel", "arbitrary"),
    )(*map(_in_hbm, (u, u, u, da, da, w_conv, b_conv)))


W_IN_SEGMENTS = ((R_POOL, POOL_WIDTH, C_POOL), (R_QKV, QKV_W, C_QKV), (R_OG, D_MODEL, C_OG), (R_GK, GATE_RANK, C_GK),
                 (R_GATE, GATE_W, C_GATE))


def _slab_pieces(d):
    lo, hi = d * IN_SHARD, (d + 1) * IN_SHARD
    pieces = []
    for start, n, at in W_IN_SEGMENTS:
        a, b = max(lo, start), min(hi, start + n)
        if a < b:
            assert (a - lo) % 2 == 0 and (b - a) % 2 == 0 and (at + a - start) % 2 == 0
            pieces.append(((a - lo) // 2, (b - a) // 2, (at + a - start) // 2))
    return pieces


def _unshard_w_in(slabs):
    def body(slab_ref, cat_ref):
        d = pl.program_id(0)
        src = slab_ref.bitcast(jnp.uint32)
        dst = cat_ref.bitcast(jnp.uint32)

        @pl.when(d == 0)
        def _():
            cat_ref[C_GK:, :] = jnp.zeros((GK_PAD, D_MODEL), BF)

        for dd in range(N_DEV):
            @pl.when(d == dd)
            def _():
                for a, n, at in _slab_pieces(dd):
                    dst[pl.ds(at, n), :] = src[0, pl.ds(a, n), :]

    return pl.pallas_call(
        body, name="unshard_w_in", grid=(N_DEV,),
        in_specs=[pl.BlockSpec((1, IN_SHARD, D_MODEL), lambda d: (d, 0, 0))], out_specs=_const_spec((N_DZ, D_MODEL)),
        out_shape=_out_hbm((N_DZ, D_MODEL), BF), compiler_params=_params("arbitrary"),
    )(_in_hbm(slabs))


def _shard_d_w_in(d_cat):
    def body(cat_ref, slab_ref):
        d = pl.program_id(0)
        cat = cat_ref.bitcast(jnp.uint32)
        dst = slab_ref.bitcast(jnp.uint32)
        for dd in range(N_DEV):
            @pl.when(d == dd)
            def _():
                for a, n, at in _slab_pieces(dd):
                    dst[0, pl.ds(a, n), :] = cat[pl.ds(at, n), :]

    return pl.pallas_call(
        body, name="shard_d_w_in", grid=(N_DEV,), in_specs=[_const_spec((N_DZ, D_MODEL))],
        out_specs=pl.BlockSpec((1, IN_SHARD, D_MODEL), lambda d: (d, 0, 0)),
        out_shape=_out_hbm((N_DEV, IN_SHARD, D_MODEL), BF), compiler_params=_params("parallel"),
    )(_in_hbm(d_cat))


ANY = pl.BlockSpec(memory_space=pl.ANY)


def _place():
    x, y, c = lax.axis_index("x"), lax.axis_index("y"), lax.axis_index("c")
    other_chips = [(1 - x, y), (x, 1 - y), (1 - x, 1 - y)]
    return x, y, c, other_chips


SEM = pl.BlockSpec(memory_space=pltpu.SEMAPHORE)
IN_HBM = pl.BlockSpec(memory_space=pltpu.HBM)
SPLIT_PARAMS = pltpu.CompilerParams(has_side_effects=pltpu.SideEffectType.DATAFLOW_SIDE_EFFECTING)


def _gather_first(refs, send_sems, recv_sems):
    x, y, c, chips = _place()
    targets = [(x, y, 1 - c)] + [(px, py, c) for px, py in chips]
    return [pltpu.make_async_remote_copy(src_ref=refs[2 * a], dst_ref=refs[2 * a + 1].at[4 * x + 2 * y + c],
                                         send_sem=send_sems.at[4 * a + k], recv_sem=recv_sems.at[4 * a + k],
                                         device_id=to, device_id_type=MESH)
            for a in range(len(refs) // 2) for k, to in enumerate(targets)]


def _gather_direct(refs, send_sems, recv_sems):
    x, y, c, _ = _place()
    flips = [(dx, dy, dc) for dx in (0, 1) for dy in (0, 1) for dc in (0, 1) if dx + dy + dc]
    targets = [(1 - x if dx else x, 1 - y if dy else y, 1 - c if dc else c) for dx, dy, dc in flips]
    return [pltpu.make_async_remote_copy(src_ref=refs[2 * a], dst_ref=refs[2 * a + 1].at[4 * x + 2 * y + c],
                                         send_sem=send_sems.at[7 * a + k], recv_sem=recv_sems.at[7 * a + k],
                                         device_id=to, device_id_type=MESH)
            for a in range(len(refs) // 2) for k, to in enumerate(targets)]


def _gather_second(refs, send_sems, recv_sems):
    x, y, c, chips = _place()
    copies = []
    for a, land in enumerate(refs):
        for j, (px, py) in enumerate(chips):
            block = land.at[4 * px + 2 * py + c]
            copies.append(pltpu.make_async_remote_copy(src_ref=block, dst_ref=block, send_sem=send_sems.at[3 * a + j],
                                                       recv_sem=recv_sems.at[3 * a + j], device_id=(x, y, 1 - c),
                                                       device_id_type=MESH))
    return copies


def _reduce_first(refs, send_sems, recv_sems):
    x, y, c, _ = _place()
    return [pltpu.make_async_remote_copy(src_ref=refs[2 * a].at[j, 1 - c], dst_ref=refs[2 * a + 1].at[j],
                                         send_sem=send_sems.at[4 * a + j], recv_sem=recv_sems.at[4 * a + j],
                                         device_id=(x, y, 1 - c), device_id_type=MESH)
            for a in range(len(refs) // 2) for j in range(4)]


def _reduce_second(refs, send_sems, recv_sems):
    _, _, c, chips = _place()
    return [pltpu.make_async_remote_copy(src_ref=refs[2 * a].at[2 * px + py], dst_ref=refs[2 * a + 1].at[k],
                                         send_sem=send_sems.at[3 * a + k], recv_sem=recv_sems.at[3 * a + k],
                                         device_id=(px, py, c), device_id_type=MESH)
            for a in range(len(refs) // 2) for k, (px, py) in enumerate(chips)]


def _split_start(name, groups):
    arrays = [a for g in groups for a in g[0]]
    n = len(arrays)

    def body(*refs):
        sems = refs[n:n + 2 * len(groups)]
        at = 0
        for gi, (members, _, build) in enumerate(groups):
            for cp in build(refs[at:at + len(members)], sems[2 * gi], sems[2 * gi + 1]):
                cp.start()
            at += len(members)
        refs[-1][...] = jnp.zeros_like(refs[-1])

    sem_shapes = [pltpu.SemaphoreType.DMA((g[1],)) for g in groups for _ in range(2)]
    outs = pl.pallas_call(
        body, name=name, in_specs=[IN_HBM] * n,
        out_shape=(*sem_shapes, *[_out_hbm(a.shape, a.dtype) for a in arrays], jax.ShapeDtypeStruct((8, 128), F32)),
        out_specs=(*[SEM] * len(sem_shapes), *[IN_HBM] * n, pl.BlockSpec(memory_space=pltpu.VMEM)),
        input_output_aliases={i: len(sem_shapes) + i for i in range(n)}, compiler_params=SPLIT_PARAMS,
    )(*[pltpu.with_memory_space_constraint(a, pltpu.HBM) for a in arrays])
    per_group, at = [], len(sem_shapes)
    for gi, (members, _, _) in enumerate(groups):
        per_group.append((outs[2 * gi], outs[2 * gi + 1], list(outs[at:at + len(members)])))
        at += len(members)
    return per_group, outs[-1]


def _split_wait(name, started, build, after):
    send_sems, recv_sems, arrays = started
    n = len(arrays)
    after = after if isinstance(after, (tuple, list)) else (after,)

    def body(*refs):
        for cp in build(refs[:n], refs[n], refs[n + 1]):
            cp.wait_send()
            cp.wait_recv()

    return pl.pallas_call(
        body, name=name, in_specs=[IN_HBM] * n + [SEM, SEM] + [ANY] * len(after),
        out_shape=tuple(_out_hbm(a.shape, a.dtype) for a in arrays), out_specs=tuple([IN_HBM] * n),
        input_output_aliases={i: i for i in range(n)}, compiler_params=SPLIT_PARAMS,
    )(*arrays, send_sems, recv_sems, *after)


def _placed_behind(token, arrays, name):
    n = len(arrays)

    def body(*refs):
        refs[-1][...] = jnp.zeros_like(refs[-1])

    outs = pl.pallas_call(
        body, name=name, in_specs=[IN_HBM] * n + [ANY],
        out_shape=(*[_out_hbm(a.shape, a.dtype) for a in arrays], jax.ShapeDtypeStruct((8, 128), F32)),
        out_specs=(*[IN_HBM] * n, pl.BlockSpec(memory_space=pltpu.VMEM)),
        input_output_aliases={i: i for i in range(n)},
    )(*map(_in_hbm, arrays), token)
    return outs[:n], outs[-1]


def _gather_landing(shard, me):
    return lax.dynamic_update_slice(lax.empty((N_DEV,) + shard.shape, shard.dtype), shard[None],
                                    (me,) + (0,) * shard.ndim)


ADAM_LANE_TILE = 256


def _tile_2d(rows, cols):
    for t in (256, 176, 128):
        if rows % t == 0:
            return t, cols
    return rows, ADAM_LANE_TILE


def _pair_sum(part, recv, core, name):
    _, rows, cols = recv.shape
    tr, tc = rows, cols

    def body(c_ref, p_ref, r_ref, o_ref):
        del c_ref
        o_ref[...] = (p_ref[...].astype(F32) + r_ref[...].astype(F32)).astype(BF)

    grid_spec = pltpu.PrefetchScalarGridSpec(
        num_scalar_prefetch=1, grid=(4, rows // tr, cols // tc),
        in_specs=[pl.BlockSpec((None, None, tr, tc), lambda j, i, k, c_ref: (j, c_ref[0], i, k)),
                  pl.BlockSpec((None, tr, tc), lambda j, i, k, c_ref: (j, i, k))],
        out_specs=pl.BlockSpec((None, tr, tc), lambda j, i, k, c_ref: (j, i, k)))
    return pl.pallas_call(
        body, name=name, grid_spec=grid_spec, out_shape=_out_hbm(recv.shape, BF),
        compiler_params=_params("parallel", "parallel", "parallel"),
    )(core, *map(_in_hbm, (part, recv)))


def _adamw(w, g, m, v):
    m = ADAM_B1 * m + (1.0 - ADAM_B1) * g
    v = ADAM_B2 * v + (1.0 - ADAM_B2) * (g * g)
    delta = -ADAM_LR * ((m / ADAM_C1) / (jnp.sqrt(v / ADAM_C2) + ADAM_EPS) + ADAM_WD * w)
    return delta, m, v


def _chip_sum_adamw(sums, recv, w, m, v, chip, name):
    rows, cols = w.shape
    tr, tc = _tile_2d(rows, cols)

    def body(chip_ref, s_ref, r_ref, w_ref, m_ref, v_ref, g_out, d_out, m_out, v_out):
        del chip_ref
        g = s_ref[...].astype(F32)
        for k in range(3):
            g = g + r_ref[k].astype(F32)
        g_out[...] = g
        d_out[...], m_out[...], v_out[...] = _adamw(w_ref[...], g, m_ref[...], v_ref[...])

    tile = pl.BlockSpec((tr, tc), lambda i, k, chip_ref: (i, k))
    grid_spec = pltpu.PrefetchScalarGridSpec(
        num_scalar_prefetch=1, grid=(rows // tr, cols // tc),
        in_specs=[pl.BlockSpec((None, tr, tc), lambda i, k, chip_ref: (chip_ref[0], i, k)),
                  pl.BlockSpec((3, tr, tc), lambda i, k, chip_ref: (0, i, k)), tile, tile, tile],
        out_specs=[tile] * 4)
    return pl.pallas_call(
        body, name=name, grid_spec=grid_spec, out_shape=[_out_hbm((rows, cols), F32)] * 4,
        compiler_params=_params("parallel", "parallel"),
    )(chip, *map(_in_hbm, (sums, recv, w, m, v)))


def _small_sum_adamw(me, entries, loss_parts):
    def whole(shape, squeeze=0, pick=False):
        blk = (None,) * squeeze + tuple(shape[squeeze:])
        if pick:
            blk = (shape[0], None) + tuple(shape[2:])
            return pl.BlockSpec(blk, lambda i, me_ref: (0, me_ref[0]) + (0,) * (len(shape) - 2))
        return pl.BlockSpec(blk, lambda i, me_ref: (0,) * len(shape))

    in_specs, out_specs, out_shape, args = [], [], [], []
    for parts, w, m, v, sharded in entries:
        lead = w.ndim - (parts.ndim - (2 if sharded else 1))
        in_specs += [whole(parts.shape, pick=sharded)] + [whole(w.shape, squeeze=lead)] * 3
        out_specs += [whole(w.shape, squeeze=lead)] * 4
        out_shape += [_out_hbm(w.shape, F32)] * 4
        args += [parts, w, m, v]
    in_specs.append(whole(loss_parts.shape))
    out_specs.append(whole(loss_parts.shape[1:]))
    out_shape.append(_out_hbm(loss_parts.shape[1:], F32))
    n = len(entries)

    def added(p_ref):
        total = p_ref[0]
        for d in range(1, N_DEV):
            total = total + p_ref[d]
        return total

    def body(me_ref, *refs):
        del me_ref
        ins, outs = refs[:4 * n + 1], refs[4 * n + 1:]
        for e in range(n):
            p_ref, w_ref, m_ref, v_ref = ins[4 * e:4 * e + 4]
            g_out, d_out, m_out, v_out = outs[4 * e:4 * e + 4]
            g = added(p_ref)
            g_out[...] = g
            d_out[...], m_out[...], v_out[...] = _adamw(w_ref[...], g, m_ref[...], v_ref[...])
        outs[4 * n][...] = added(ins[4 * n])

    grid_spec = pltpu.PrefetchScalarGridSpec(num_scalar_prefetch=1, grid=(1,), in_specs=in_specs, out_specs=out_specs)
    outs = pl.pallas_call(body, name="small_sum_adamw", grid_spec=grid_spec, out_shape=out_shape,
                          compiler_params=_params("arbitrary"))(me, *map(_in_hbm, args + [loss_parts]))
    return [outs[4 * e:4 * e + 4] for e in range(n)], outs[4 * n]


MM_TILE = 512
N_MM_TILES = SEQ // MM_TILE
CAT_TILE = 512
N_CAT_TILES = N_CAT // CAT_TILE
DZ_TILE = 640


def kernel(x, g_mix, w_in, b_gate, w_gk_up, b_gk, w_pool_grp, pool_scale, g_gla_head, w_pool_proj, w_gla_proj, w_out, g_ffn, w_up, w_conv, b_conv, w_down, g_final, loss_target, m_g_mix, m_w_in, m_b_gate, m_w_gk_up, m_b_gk, m_w_pool_grp, m_pool_scale, m_g_gla_head, m_w_pool_proj, m_w_gla_proj, m_w_out, m_g_ffn, m_w_up, m_w_conv, m_b_conv, m_w_down, m_g_final, v_g_mix, v_w_in, v_b_gate, v_w_gk_up, v_b_gk, v_w_pool_grp, v_pool_scale, v_g_gla_head, v_w_pool_proj, v_w_gla_proj, v_w_out, v_g_ffn, v_w_up, v_w_conv, v_b_conv, v_w_down, v_g_final):
    xi, yi, ci = lax.axis_index("x"), lax.axis_index("y"), lax.axis_index("c")
    me = 4 * xi + 2 * yi + ci
    core = jnp.reshape(ci, (1,)).astype(jnp.int32)
    chip = jnp.reshape(2 * xi + yi, (1,)).astype(jnp.int32)
    xs, target = x[0], loss_target[0]

    big = dict(w_in=w_in[0].T, w_pool_proj=w_pool_proj[0], w_gla_proj=w_gla_proj[0], w_out=w_out[0], w_up=w_up[0].T,
               w_down=w_down[0])
    moments = dict(w_in=(m_w_in[0].T, v_w_in[0].T), w_pool_proj=(m_w_pool_proj[0], v_w_pool_proj[0]),
                   w_gla_proj=(m_w_gla_proj[0], v_w_gla_proj[0]), w_out=(m_w_out[0], v_w_out[0]),
                   w_up=(m_w_up[0].T, v_w_up[0].T), w_down=(m_w_down[0], v_w_down[0]))
    names = list(big)
    shards = {k: big[k].astype(BF) for k in names}
    shards["w_gk_up"], shards["w_conv"] = w_gk_up[0], w_conv[0]
    gather_groups = (("w_in", "w_gk_up"), ("w_pool_proj", "w_gla_proj", "w_out"), ("w_up", "w_down", "w_conv"))
    started, token = _split_start("gather_start", [
        ([t for k in g for t in (shards[k], _gather_landing(shards[k], me))], 4 * len(g), _gather_first)
        for g in gather_groups])
    (big["w_in"], *moments["w_in"]), token = _placed_behind(token, [big["w_in"], *moments["w_in"]], "place_w_in")

    def gather_pass(gi, after):
        lands = list(_split_wait(f"gather_wait_{gi}", started[gi], _gather_first, after)[1::2])
        passed, tkn = _split_start(f"gather_pass_{gi}", [(lands, 3 * len(lands), _gather_second)])
        return passed[0], tkn

    def gather_done(gi, passed, after):
        return dict(zip(gather_groups[gi], _split_wait(f"gather_pass_wait_{gi}", passed, _gather_second, after)))

    tok = lambda i, j, k: (i, 0)
    whole = lambda i, j, k: (0, 0)
    kblk = lambda i, j, k: (k, 0)
    ff_seq = (None, None, SEQ, FF_BLK)

    h = _rms_fwd(xs, g_mix, token, "rms_mix")
    wg = gather_done(0, gather_pass(0, h)[0], h)
    wt_cat = _unshard_w_in(wg["w_in"])
    wgk_pad = jnp.pad(wg["w_gk_up"].transpose(1, 0, 2).reshape(GATE_RANK, GLA_DK), ((0, GK_PAD - GATE_RANK), (0, 0)))
    zcat = _mm(h, wt_cat, out_shape=(SEQ, N_CAT), out_dtype=BF, grid=(N_CAT_TILES, 1, 1),
               blk_a=(SEQ, D_MODEL), blk_b=(CAT_TILE, D_MODEL), blk_o=(SEQ, CAT_TILE),
               map_a=whole, map_b=lambda j, i, k: (j, 0), map_o=lambda j, i, k: (0, j), tb=True, name="mm_in")
    la = _gk_fwd(h, wt_cat, wgk_pad, b_gk)
    passed, tkn = gather_pass(1, la)
    o, states = _gla_fwd(zcat, la, tkn)
    wg = gather_done(1, passed, o)
    wpp = wg["w_pool_proj"].transpose(1, 0, 2).reshape(POOL_WIDTH, D_MODEL)
    wgp = wg["w_gla_proj"].reshape(D_MODEL, D_MODEL)
    wout = wg["w_out"].reshape(D_MODEL, D_MODEL)
    og = _post_gla_fwd(o, zcat, g_gla_head)
    ps = _pool_fwd(zcat, w_pool_grp[0], pool_scale)
    passed, tkn = gather_pass(2, (og, ps))
    y_pool, y_gla, mixed, x1, h2 = _mix_out_fwd(ps, og, zcat, xs, wpp, wgp, wout, b_gate, g_ffn, tkn)
    wg = gather_done(2, passed, h2)
    wt_up = wg["w_up"].reshape(2 * D_FF, D_MODEL)
    wdown = wg["w_down"].reshape(D_FF, D_MODEL)
    wconv4 = wg["w_conv"].reshape(2, 4, 3, FF_BLK)
    bconv4 = b_conv.reshape(2, 4, 1, FF_BLK)
    blk4 = lambda b, i, k: (b // 4, b % 4, 0, 0)
    u4, act = _up_conv_fwd(h2, wt_up, wconv4, bconv4)
    loss_part, dx2, dx2_bf, dg_final = _mm_tokens(
        act, wdown, blk_a=(None, 4, TOK_MM_TILE, FF_BLK), map_a=lambda i: (0, 0, i, 0),
        pieces=[(b, b * FF_BLK, FF_BLK) for b in range(4)], res=x1, then=("loss", g_final.reshape(1, D_MODEL), target),
        name="mm_down_loss")

    da = _mm(dx2_bf, wdown, out_shape=(1, 4, SEQ, FF_BLK), out_dtype=BF, grid=(4, 1, 1),
             blk_a=(SEQ, D_MODEL), blk_b=(FF_BLK, D_MODEL), blk_o=ff_seq,
             map_a=whole, map_b=lambda b, i, k: (b, 0), map_o=lambda b, i, k: (0, b, 0, 0), tb=True, name="mm_d_act")
    d_wdown = _mm(act, dx2_bf, out_shape=(D_FF, D_MODEL), out_dtype=BF, grid=(4, 1, 1),
                  blk_a=ff_seq, blk_b=(SEQ, D_MODEL), blk_o=(FF_BLK, D_MODEL),
                  map_a=lambda b, i, k: (0, b, 0, 0), map_b=whole, map_o=lambda b, i, k: (b, 0), ta=True,
                  name="mm_d_wdown")
    du4, d_wconv, d_bconv = _conv_bwd(u4, da, wconv4, bconv4)
    d_wt_up = _mm(du4, h2, out_shape=(2 * D_FF, D_MODEL), out_dtype=BF, grid=(N_DEV, 1, 1),
                  blk_a=ff_seq, blk_b=(SEQ, D_MODEL), blk_o=(FF_BLK, D_MODEL),
                  map_a=blk4, map_b=whole, map_o=lambda b, i, k: (b, 0), ta=True, name="mm_d_wup")
    res = {}

    def to_sibling(keys, parts):
        return [t for k in keys for t in (parts[k], lax.empty((4,) + parts[k].shape[2:], BF))], 4 * len(keys), _reduce_first

    def to_chips(keys, st, after):
        arrays = _split_wait("reduce_wait_" + keys[0], st, _reduce_first, after)
        sums = [_pair_sum(p, r, core, "pair_sum_" + k) for k, p, r in zip(keys, arrays[0::2], arrays[1::2])]
        return [t for s in sums for t in (s, lax.empty((3,) + s.shape[1:], BF))], 3 * len(keys), _reduce_second

    def reduce_start(keys, parts):
        st, tkn = _split_start("reduce_start_" + keys[0], [to_sibling(keys, parts)])
        return st[0], tkn

    def reduce_cross(keys, st, after):
        st2, tkn = _split_start("reduce_cross_" + keys[0], [to_chips(keys, st, after)])
        return st2[0], tkn

    def reduce_done(keys, st2, after):
        arrays = _split_wait("reduce_cross_wait_" + keys[0], st2, _reduce_second, after)
        for k, s, r in zip(keys, arrays[0::2], arrays[1::2]):
            outs = _chip_sum_adamw(s, r, big[k], moments[k][0], moments[k][1], chip, "adamw_" + k)
            res[k] = [(t.T if k in ("w_in", "w_up") else t)[None] for t in outs]

    ffn_keys = ("w_down", "w_up")
    ffn_red, tkn = reduce_start(ffn_keys, dict(w_down=d_wdown.reshape(4, 2, D_FF // N_DEV, D_MODEL),
                                               w_up=d_wt_up.reshape(4, 2, FF_BLK, D_MODEL)))
    dx1, dg_ffn = _mm_tokens(
        du4, wt_up, blk_a=(2, 4, TOK_MM_TILE, FF_BLK), map_a=lambda i: (0, 0, i, 0),
        pieces=[((b // 4, b % 4), b * FF_BLK, FF_BLK) for b in range(N_DEV)], after=tkn, then=("rms_bwd", x1, g_ffn, dx2),
        name="mm_d_h2_rms")

    sq_t = dict(out_shape=(D_MODEL, D_MODEL), grid=(1, 1, N_MM_TILES), blk_a=(MM_TILE, D_MODEL),
                blk_b=(MM_TILE, D_MODEL), blk_o=(D_MODEL, D_MODEL), map_a=kblk, map_b=kblk, map_o=whole, ta=True)
    d_wout = _mm(mixed, dx1, out_dtype=BF, name="mm_d_wout", **sq_t)
    dzcat, dy_pool, dy_gla, db_gate = _mix_bwd(dx1, wout, zcat, b_gate, y_pool, y_gla)
    d_wgp = _mm(og, dy_gla, out_dtype=BF, name="mm_d_wgp", **sq_t)
    mix_keys = ("w_out", "w_gla_proj")
    (ffn_red, mix_red), tkn = _split_start("reduce_cross_w_down", [
        to_chips(ffn_keys, ffn_red, db_gate),
        to_sibling(mix_keys, dict(w_out=d_wout.reshape(4, 2, D_MODEL // N_DEV, D_MODEL),
                                  w_gla_proj=d_wgp.reshape(4, 2, D_MODEL // N_DEV, D_MODEL)))])
    dzcat, d_o, dg_head = _post_gla_bwd(dzcat, dy_gla, wgp, o, zcat, g_gla_head, tkn)
    dzcat, dla = _gla_bwd(dzcat, zcat, la, d_o, states)
    dzcat, d_wgk, db_gk = _gk_bwd(dzcat, dla, h, wt_cat, wgk_pad, b_gk)
    dps = _mm(dy_pool, wpp, out_shape=(SEQ, POOL_WIDTH), out_dtype=F32, grid=(N_MM_TILES, 1, 1),
              blk_a=(MM_TILE, D_MODEL), blk_b=(POOL_WIDTH, D_MODEL), blk_o=(MM_TILE, POOL_WIDTH),
              map_a=tok, map_b=whole, map_o=tok, tb=True, name="mm_d_ps")
    d_wpp = _mm(ps, dy_pool, out_shape=(POOL_WIDTH, D_MODEL), out_dtype=F32, grid=(1, 1, N_MM_TILES),
                blk_a=(MM_TILE, POOL_WIDTH), blk_b=(MM_TILE, D_MODEL), blk_o=(POOL_WIDTH, D_MODEL),
                map_a=kblk, map_b=kblk, map_o=whole, ta=True, name="mm_d_wpp")
    dzcat, d_wgrp, d_scale = _pool_bwd(dzcat, zcat, dps, w_pool_grp[0], pool_scale)
    row = lambda t: t.reshape(1, D_MODEL)
    conv_vec = lambda t: t.reshape(2, 4, 1, FF_BLK)
    small = [("b_gate", db_gate, b_gate, m_b_gate, v_b_gate, False),
             ("w_gk_up", d_wgk.reshape(GATE_RANK, N_DEV, GLA_DK // N_DEV).transpose(1, 0, 2), w_gk_up, m_w_gk_up,
              v_w_gk_up, True),
             ("b_gk", db_gk, b_gk, m_b_gk, v_b_gk, False),
             ("w_pool_grp", d_wgrp, w_pool_grp, m_w_pool_grp, v_w_pool_grp, False),
             ("pool_scale", d_scale, pool_scale, m_pool_scale, v_pool_scale, False),
             ("g_gla_head", dg_head, g_gla_head, m_g_gla_head, v_g_gla_head, False),
             ("g_ffn", dg_ffn, g_ffn, m_g_ffn, v_g_ffn, False),
             ("w_conv", d_wconv.reshape(N_DEV, 3, FF_BLK), w_conv, m_w_conv, v_w_conv, True),
             ("b_conv", d_bconv, conv_vec(b_conv), conv_vec(m_b_conv), conv_vec(v_b_conv), False),
             ("g_final", dg_final, row(g_final), row(m_g_final), row(v_g_final), False)]

    def to_all(parts):
        return [t for p in parts for t in (p, _gather_landing(p, me))], 7 * len(parts), _gather_direct

    (small_sent, mix_red), tkn = _split_start("small_start", [to_all([t[1] for t in small] + [loss_part]),
                                                              to_chips(mix_keys, mix_red, dla)])
    d_wt_cat = _mm(dzcat, h, out_shape=(N_DZ, D_MODEL), out_dtype=BF, grid=(N_DZ // DZ_TILE, 1, 1),
                   blk_a=(SEQ, DZ_TILE), blk_b=(SEQ, D_MODEL), blk_o=(DZ_TILE, D_MODEL),
                   map_a=lambda j, i, k: (0, j), map_b=whole, map_o=lambda j, i, k: (j, 0), ta=True, after=tkn,
                   name="mm_d_wcat")
    in_keys = ("w_in", "w_pool_proj")
    in_red, tkn = reduce_start(in_keys, dict(
        w_in=_shard_d_w_in(d_wt_cat).reshape(4, 2, IN_SHARD, D_MODEL),
        w_pool_proj=d_wpp.reshape(POOL_WIDTH, N_DEV, D_MODEL // N_DEV).transpose(1, 0, 2).astype(BF)
        .reshape(4, 2, POOL_WIDTH, D_MODEL // N_DEV)))
    reduce_done(mix_keys, mix_red, tkn)
    in_red, tkn = reduce_cross(in_keys, in_red, res["w_out"][0])
    grad_x, dg_mix = _mm_tokens(dzcat, wt_cat, blk_a=(TOK_MM_TILE, N_DZ), map_a=lambda i: (i, 0),
                                pieces=[(None, 0, N_DZ)], after=tkn, then=("rms_bwd", xs, g_mix, dx1),
                                name="mm_d_h_rms")
    (g_mix_sent,), tkn = _split_start("g_mix_start", [to_all([dg_mix])])
    reduce_done(ffn_keys, ffn_red, (grad_x, tkn))
    gathered = _split_wait("small_wait", small_sent, _gather_direct, res["w_down"][0])[1::2]
    small.append(("g_mix", dg_mix, g_mix, m_g_mix, v_g_mix, False))
    gathered = list(gathered[:-1]) + [_split_wait("g_mix_wait", g_mix_sent, _gather_direct, gathered[0])[1], gathered[-1]]
    small_out, loss_sum = _small_sum_adamw(jnp.reshape(me, (1,)).astype(jnp.int32),
                                           [(p,) + t[2:] for p, t in zip(gathered, small)], gathered[-1])
    for t, outs in zip(small, small_out):
        res[t[0]] = list(outs)
    res["b_conv"] = [t.reshape(b_conv.shape) for t in res["b_conv"]]
    res["g_final"] = [t.reshape(g_final.shape) for t in res["g_final"]]

    reduce_done(in_keys, in_red, loss_sum)
    loss = loss_sum[0, 0]
    order =["g_mix", "w_in", "b_gate", "w_gk_up", "b_gk", "w_pool_grp", "pool_scale", "g_gla_head", "w_pool_proj",
             "w_gla_proj", "w_out", "g_ffn", "w_up", "w_conv", "b_conv", "w_down", "g_final"]
    return (loss, grad_x[None], *[res[k][0] for k in order], *[res[k][1] for k in order],
            *[res[k][2] for k in order], *[res[k][3] for k in order])
```

```python
import jax
import jax.numpy as jnp
from jax import lax
from jax.experimental import pallas as pl
from jax.experimental.pallas import tpu as pltpu

F32 = jnp.float32
BF = jnp.bfloat16
HIGHEST = lax.Precision.HIGHEST
MESH = pl.DeviceIdType.MESH

N_DEV = 8
SEQ = 2048
D_MODEL = 1024
CHUNK = 64
EPS = 1e-6
POOL_WIDTH = 512
POOL_WINDOWS = (2, 4, 8, 16)
POOL_GD = 128
POOL_HALO = 16
HEADS = 4
HK = 128
HV = 256
GLA_DK = 512
GATE_RANK = 16
GATE_NORM = 16.0
D_FF = 2816
FF_BLK = 704
IN_SHARD = 706
C_QKV, C_GATE, C_OG, C_POOL, C_GK = 0, 2048, 4096, 5120, 5632
N_CAT = 5632
GK_PAD = 128
N_DZ = N_CAT + GK_PAD
R_POOL, R_QKV, R_OG, R_GK, R_GATE = 0, 512, 2560, 3584, 3600

ADAM_LR, ADAM_B1, ADAM_B2, ADAM_EPS, ADAM_WD, ADAM_STEP = 0.001, 0.9, 0.999, 1e-08, 0.01, 10
ADAM_C1 = 1.0 - ADAM_B1 ** ADAM_STEP
ADAM_C2 = 1.0 - ADAM_B2 ** ADAM_STEP

VMEM_BYTES_V7X = 64 * 1024 * 1024
VMEM_LIMIT = VMEM_BYTES_V7X * 3 // 4

TOK_TILE = 256
HALO = 8
GLA_CPS = 4


def _params(*sem):
    return pltpu.CompilerParams(dimension_semantics=sem, vmem_limit_bytes=VMEM_LIMIT)


def _const_spec(shape):
    nd = len(shape)
    return pl.BlockSpec(shape, lambda *_: (0,) * nd)


def _in_hbm(t):
    return pltpu.with_memory_space_constraint(t, pltpu.HBM)


def _out_hbm(shape, dtype):
    return pltpu.HBM(shape, dtype)


def _dot(a, b, ta=False, tb=False):
    dims = (((0 if ta else 1,), (1 if tb else 0,)), ((), ()))
    return lax.dot_general(a.astype(BF), b.astype(BF), dims, preferred_element_type=F32)


def _dot_exact(a, b):
    return jnp.dot(a, b, precision=HIGHEST, preferred_element_type=F32)


def _sigmoid(x):
    return 0.5 * jnp.tanh(0.5 * x) + 0.5


def _mm(a, b, *, out_shape, out_dtype, grid, blk_a, blk_b, blk_o, map_a, map_b, map_o, ta=False, tb=False,
        after=None, name):
    gk = grid[2]
    n_in = 2 + (after is not None)

    def body(*refs):
        a_ref, b_ref, o_ref = refs[0], refs[1], refs[n_in]
        prod = _dot(a_ref[...], b_ref[...], ta, tb)
        if gk == 1:
            o_ref[...] = prod.astype(out_dtype)
        else:
            acc = refs[n_in + 1]
            k = pl.program_id(2)

            @pl.when(k == 0)
            def _():
                acc[...] = prod

            @pl.when(k > 0)
            def _():
                acc[...] += prod

            @pl.when(k == gk - 1)
            def _():
                o_ref[...] = acc[...].astype(out_dtype)

    in_specs = [pl.BlockSpec(blk_a, map_a), pl.BlockSpec(blk_b, map_b)]
    args = [_in_hbm(a), _in_hbm(b)]
    if after is not None:
        in_specs.append(pl.BlockSpec(memory_space=pl.ANY))
        args.append(after)
    return pl.pallas_call(
        body, name=name, grid=grid, in_specs=in_specs, out_specs=pl.BlockSpec(blk_o, map_o),
        out_shape=_out_hbm(out_shape, out_dtype),
        scratch_shapes=[] if gk == 1 else [pltpu.VMEM(tuple(d for d in blk_o if d is not None), F32)],
        compiler_params=_params("parallel", "parallel", "arbitrary"),
    )(*args)


TOK_MM_TILE = 256


def _mm_tokens(a, w, *, blk_a, map_a, pieces, res=None, after=None, then=None, name):
    n_in = 2 + (res is not None) + (after is not None) + (0 if then is None else len(then) - 1)

    def accumulate(ref, part):
        @pl.when(pl.program_id(0) == 0)
        def _():
            ref[...] = part

        @pl.when(pl.program_id(0) > 0)
        def _():
            ref[...] += part

    def body(*refs):
        a_ref, w_ref = refs[:2]
        extra, outs = refs[n_in - (0 if then is None else len(then) - 1):n_in], refs[n_in:]
        total = None
        for idx, row, n in pieces:
            av = a_ref[...] if idx is None else a_ref[idx]
            prod = _dot(av, w_ref[row:row + n, :])
            total = prod if total is None else total + prod
        if res is not None:
            total = total + refs[2][...]
        if then is None:
            outs[0][...] = total
        elif then[0] == "rms_bwd":
            dx, part = _rms_bwd_tile(total, extra[0][...], extra[1][...], extra[2][...])
            outs[0][...] = dx
            accumulate(outs[1], part)
        else:
            lpart, dx, part = _loss_tile(total, extra[0][...], extra[1][...])
            outs[1][...] = dx
            outs[2][...] = dx.astype(BF)
            accumulate(outs[0], lpart)
            accumulate(outs[3], part)

    tile = pl.BlockSpec((TOK_MM_TILE, D_MODEL), lambda i: (i, 0))
    vec = _const_spec((1, D_MODEL))
    big = _out_hbm((SEQ, D_MODEL), F32)
    small = _out_hbm((1, D_MODEL), F32)
    in_specs = [pl.BlockSpec(blk_a, map_a), pl.BlockSpec(w.shape, lambda i: (0, 0), pipeline_mode=pl.Buffered(1))]
    args = [a, w]
    if res is not None:
        in_specs.append(tile)
        args.append(res)
    if after is not None:
        in_specs.append(pl.BlockSpec(memory_space=pl.ANY))
        args.append(after)
    if then is None:
        out_specs, out_shape = tile, big
    elif then[0] == "rms_bwd":
        in_specs += [tile, vec, tile]
        out_specs, out_shape = [tile, vec], [big, small]
    else:
        in_specs += [vec, tile]
        out_specs = [_const_spec((1, 128)), tile, tile, vec]
        out_shape = [_out_hbm((1, 128), F32), big, _out_hbm((SEQ, D_MODEL), BF), small]
    if then is not None:
        args += list(then[1:])
    return pl.pallas_call(
        body, name=name, grid=(SEQ // TOK_MM_TILE,), in_specs=in_specs, out_specs=out_specs, out_shape=out_shape,
        compiler_params=_params("parallel" if then is None else "arbitrary"),
    )(*[_in_hbm(t) for t in args])


def _rms_fwd(x, g, after, name):
    def body(x_ref, g_ref, after_ref, o_ref):
        del after_ref
        xv = x_ref[...]
        r = lax.rsqrt(jnp.mean(xv * xv, axis=-1, keepdims=True) + EPS)
        o_ref[...] = (xv * r * g_ref[...]).astype(BF)

    tile = pl.BlockSpec((TOK_TILE, D_MODEL), lambda i: (i, 0))
    return pl.pallas_call(
        body, name=name, grid=(SEQ // TOK_TILE,),
        in_specs=[tile, _const_spec((1, D_MODEL)), pl.BlockSpec(memory_space=pl.ANY)], out_specs=tile,
        out_shape=_out_hbm((SEQ, D_MODEL), BF), compiler_params=_params("parallel"),
    )(*map(_in_hbm, (x, g)), after)


def _rms_bwd_tile(dyv, xv, gv, dresv):
    r = lax.rsqrt(jnp.mean(xv * xv, axis=-1, keepdims=True) + EPS)
    xn = xv * r
    dxn = dyv * gv
    return dresv + r * (dxn - xn * jnp.mean(dxn * xn, axis=-1, keepdims=True)), jnp.sum(dyv * xn, axis=0, keepdims=True)


def _loss_tile(xv, gv, tv):
    r = lax.rsqrt(jnp.mean(xv * xv, axis=-1, keepdims=True) + EPS)
    xn = xv * r
    err = xn * gv - tv
    lpart = jnp.full((1, 128), 0.5 * jnp.sum(jnp.mean(err * err, axis=-1, keepdims=True)), F32)
    dyv = err * (1.0 / D_MODEL)
    dxn = dyv * gv
    return lpart, r * (dxn - xn * jnp.mean(dxn * xn, axis=-1, keepdims=True)), jnp.sum(dyv * xn, axis=0, keepdims=True)


def _pool_counts(w):
    pos = lax.broadcasted_iota(jnp.int32, (SEQ, 1), 0).astype(F32)
    return jnp.minimum(pos + 1.0, float(w))


def _pool_window(u, w, ext):
    ext[pl.ds(POOL_HALO, SEQ), :] = u
    win = u
    for j in range(1, w):
        win = win + ext[pl.ds(POOL_HALO - j, SEQ), :]
    return win / _pool_counts(w) - u


def _pool_fwd(zcat, w_grp, scale):
    def body(z_ref, w_ref, s_ref, o_ref, ext):
        ext[pl.ds(0, POOL_HALO), :] = jnp.zeros((POOL_HALO, POOL_GD), F32)
        for g, w in enumerate(POOL_WINDOWS):
            cols = slice(g * POOL_GD, (g + 1) * POOL_GD)
            p = _pool_window(z_ref[:, cols].astype(F32), w, ext)
            o_ref[:, cols] = (_dot(p, w_ref[g]) * s_ref[:, cols]).astype(BF)

    return pl.pallas_call(
        body, name="pool_fwd", grid=(1,),
        in_specs=[pl.BlockSpec((SEQ, POOL_WIDTH), lambda i: (0, C_POOL // POOL_WIDTH)),
                  _const_spec((4, POOL_GD, POOL_GD)), _const_spec((1, POOL_WIDTH))],
        out_specs=_const_spec((SEQ, POOL_WIDTH)), out_shape=_out_hbm((SEQ, POOL_WIDTH), BF),
        scratch_shapes=[pltpu.VMEM((POOL_HALO + SEQ, POOL_GD), F32)], compiler_params=_params("arbitrary"),
    )(*map(_in_hbm, (zcat, w_grp, scale)))


def _pool_bwd(dzcat, zcat, dps, w_grp, scale):
    def body(dz_in, z_ref, dps_ref, w_ref, s_ref, dz_ref, dw_ref, dsc_ref, ext, ext2):
        del dz_in
        ext[pl.ds(0, POOL_HALO), :] = jnp.zeros((POOL_HALO, POOL_GD), F32)
        ext2[pl.ds(SEQ, POOL_HALO), :] = jnp.zeros((POOL_HALO, POOL_GD), F32)
        for g, w in enumerate(POOL_WINDOWS):
            cols = slice(g * POOL_GD, (g + 1) * POOL_GD)
            p = _pool_window(z_ref[:, cols].astype(F32), w, ext)
            wg = w_ref[g]
            pg = _dot(p, wg)
            dpsv = dps_ref[:, cols]
            dsc_ref[:, cols] = jnp.sum(dpsv * pg, axis=0, keepdims=True)
            dpg = dpsv * s_ref[:, cols]
            dw_ref[g] = _dot(p, dpg, ta=True)
            dp = _dot(dpg, wg, tb=True)
            dpc = dp / _pool_counts(w)
            ext2[pl.ds(0, SEQ), :] = dpc
            du = dpc
            for j in range(1, w):
                du = du + ext2[pl.ds(j, SEQ), :]
            dz_ref[:, cols] = (du - dp).astype(BF)

    return pl.pallas_call(
        body, name="pool_bwd", grid=(1,),
        in_specs=[pl.BlockSpec(memory_space=pl.ANY),
                  pl.BlockSpec((SEQ, POOL_WIDTH), lambda i: (0, C_POOL // POOL_WIDTH)),
                  _const_spec((SEQ, POOL_WIDTH)), _const_spec((4, POOL_GD, POOL_GD)), _const_spec((1, POOL_WIDTH))],
        out_specs=[pl.BlockSpec((SEQ, POOL_WIDTH), lambda i: (0, C_POOL // POOL_WIDTH)),
                   _const_spec((4, POOL_GD, POOL_GD)), _const_spec((1, POOL_WIDTH))],
        out_shape=[_out_hbm((SEQ, N_DZ), BF), _out_hbm((4, POOL_GD, POOL_GD), F32),
                   _out_hbm((1, POOL_WIDTH), F32)],
        scratch_shapes=[pltpu.VMEM((POOL_HALO + SEQ, POOL_GD), F32), pltpu.VMEM((SEQ + POOL_HALO, POOL_GD), F32)],
        input_output_aliases={0: 0}, compiler_params=_params("arbitrary"),
    )(*map(_in_hbm, (dzcat, zcat, dps, w_grp, scale)))


GK_TILE = 512


GK_ROWS = pl.BlockSpec((GK_PAD, D_MODEL), lambda i: (C_GK // GK_PAD, 0))


def _gk_fwd(h, wt_cat, wgk_pad, b_gk):
    def body(h_ref, wt_ref, w_ref, b_ref, la_ref):
        z_gk = _dot(h_ref[...], wt_ref[...], tb=True)
        pre = _dot(z_gk, w_ref[...]) + b_ref[...]
        la_ref[...] = (jnp.minimum(pre, 0.0) - jnp.log(1.0 + jnp.exp(-jnp.abs(pre)))) * (1.0 / GATE_NORM)

    return pl.pallas_call(
        body, name="gk_fwd", grid=(SEQ // GK_TILE,),
        in_specs=[pl.BlockSpec((GK_TILE, D_MODEL), lambda i: (i, 0)), GK_ROWS,
                  _const_spec((GK_PAD, GLA_DK)), _const_spec((1, GLA_DK))],
        out_specs=pl.BlockSpec((GK_TILE, GLA_DK), lambda i: (i, 0)),
        out_shape=_out_hbm((SEQ, GLA_DK), F32), compiler_params=_params("parallel"),
    )(*map(_in_hbm, (h, wt_cat, wgk_pad, b_gk)))


def _gk_bwd(dzcat, dla, h, wt_cat, wgk_pad, b_gk):
    def body(dz_in, dla_ref, h_ref, wt_ref, w_ref, b_ref, dz_ref, dw_ref, db_ref):
        del dz_in
        wv = w_ref[...]
        z_gk = _dot(h_ref[...], wt_ref[...], tb=True)
        pre = _dot(z_gk, wv) + b_ref[...]
        dpre = dla_ref[...] * (1.0 / GATE_NORM) * (1.0 - _sigmoid(pre))
        dz_ref[...] = _dot(dpre, wv, tb=True).astype(BF)
        dwp = _dot(z_gk, dpre, ta=True)[:GATE_RANK]
        dbp = jnp.sum(dpre, axis=0, keepdims=True)

        @pl.when(pl.program_id(0) == 0)
        def _():
            dw_ref[...] = dwp
            db_ref[...] = dbp

        @pl.when(pl.program_id(0) > 0)
        def _():
            dw_ref[...] += dwp
            db_ref[...] += dbp

    return pl.pallas_call(
        body, name="gk_bwd", grid=(SEQ // GK_TILE,),
        in_specs=[pl.BlockSpec(memory_space=pl.ANY), pl.BlockSpec((GK_TILE, GLA_DK), lambda i: (i, 0)),
                  pl.BlockSpec((GK_TILE, D_MODEL), lambda i: (i, 0)), GK_ROWS, _const_spec((GK_PAD, GLA_DK)),
                  _const_spec((1, GLA_DK))],
        out_specs=[pl.BlockSpec((GK_TILE, GK_PAD), lambda i: (i, C_GK // GK_PAD)), _const_spec((GATE_RANK, GLA_DK)),
                   _const_spec((1, GLA_DK))],
        out_shape=[_out_hbm((SEQ, N_DZ), BF), _out_hbm((GATE_RANK, GLA_DK), F32),
                   _out_hbm((1, GLA_DK), F32)],
        input_output_aliases={0: 0}, compiler_params=_params("arbitrary"),
    )(*map(_in_hbm, (dzcat, dla, h, wt_cat, wgk_pad, b_gk)))


GLA_ROWS = GLA_CPS * CHUNK
GLA_STEPS = SEQ // GLA_ROWS
QKV_W = 2048


def _tri():
    return lax.broadcasted_iota(jnp.int32, (CHUNK, CHUNK), 0) >= lax.broadcasted_iota(jnp.int32, (CHUNK, CHUNK), 1)


def _chunk_cumsum(la_ref, rows):
    return _dot_exact(_tri().astype(F32), la_ref[rows, :])


def _gla_chunk(qkv_ref, la_ref, rows, h, bc_all):
    tri = _tri()
    q = qkv_ref[rows, h * HK:(h + 1) * HK].astype(F32) * (HK ** -0.5)
    k = qkv_ref[rows, GLA_DK + h * HK:GLA_DK + (h + 1) * HK].astype(F32)
    v = qkv_ref[rows, 2 * GLA_DK + h * HV:2 * GLA_DK + (h + 1) * HV].astype(BF)
    la = la_ref[rows, h * HK:(h + 1) * HK]
    bc = bc_all[:, h * HK:(h + 1) * HK]
    e_pos, e_neg = jnp.exp(bc), jnp.exp(-bc)
    dl = jnp.exp(jnp.sum(la, axis=0, keepdims=True))
    q_fw, q_bw, k_fw, k_bw = q * e_pos, q * e_neg, k * e_neg, k * e_pos
    scores = jnp.where(tri, _dot(q_fw, k_fw, tb=True), _dot(q_bw, k_bw, tb=True))
    return tri, v, e_pos, e_neg, dl, q_fw, q_bw, k_fw, k_bw, scores


def _gla_fwd(zcat, la, after):
    def body(qkv_ref, la_ref, after_ref, o_ref, st_ref, state):
        del after_ref

        @pl.when(pl.program_id(0) == 0)
        def _():
            state[...] = jnp.zeros_like(state)

        for c in range(GLA_CPS):
            rows = slice(c * CHUNK, (c + 1) * CHUNK)
            bc_all = _chunk_cumsum(la_ref, rows)
            for h in range(HEADS):
                _, v, _, _, dl, q_fw, _, k_fw, _, scores = _gla_chunk(qkv_ref, la_ref, rows, h, bc_all)
                st = state[h]
                st_ref[c, h] = st
                o_ref[rows, h * HV:(h + 1) * HV] = _dot(scores, v) + _dot(q_fw, st, tb=True)
                state[h] = st * dl + _dot(v, k_fw * dl, ta=True)

    return pl.pallas_call(
        body, name="gla_fwd", grid=(GLA_STEPS,),
        in_specs=[pl.BlockSpec((GLA_ROWS, QKV_W), lambda i: (i, 0)), pl.BlockSpec((GLA_ROWS, GLA_DK), lambda i: (i, 0)),
                  pl.BlockSpec(memory_space=pl.ANY)],
        out_specs=[pl.BlockSpec((GLA_ROWS, D_MODEL), lambda i: (i, 0)),
                   pl.BlockSpec((GLA_CPS, HEADS, HV, HK), lambda i: (i, 0, 0, 0))],
        out_shape=[_out_hbm((SEQ, D_MODEL), F32),
                   _out_hbm((SEQ // CHUNK, HEADS, HV, HK), F32)],
        scratch_shapes=[pltpu.VMEM((HEADS, HV, HK), F32)], compiler_params=_params("arbitrary"),
    )(*map(_in_hbm, (zcat, la)), after)


def _gla_bwd(dzcat, zcat, la, d_o, states):
    def body(dz_in, qkv_ref, la_ref, do_ref, st_ref, dqkv_ref, dla_ref, dstate):
        del dz_in

        @pl.when(pl.program_id(0) == 0)
        def _():
            dstate[...] = jnp.zeros_like(dstate)

        last_row = lax.broadcasted_iota(jnp.int32, (CHUNK, HK), 0) == CHUNK - 1
        upper = (lax.broadcasted_iota(jnp.int32, (CHUNK, CHUNK), 0)
                 <= lax.broadcasted_iota(jnp.int32, (CHUNK, CHUNK), 1)).astype(F32)
        for c in reversed(range(GLA_CPS)):
            rows = slice(c * CHUNK, (c + 1) * CHUNK)
            bc_all = _chunk_cumsum(la_ref, rows)
            dbs = []
            for h in range(HEADS):
                tri, v, e_pos, e_neg, dl, q_fw, q_bw, k_fw, k_bw, scores = _gla_chunk(qkv_ref, la_ref, rows, h, bc_all)
                st = st_ref[c, h]
                dst = dstate[h]
                d_out = do_ref[rows, h * HV:(h + 1) * HV].astype(BF)
                k_dec = k_fw * dl
                dp = _dot(d_out, v, tb=True)
                dp_fw = jnp.where(tri, dp, 0.0)
                dp_bw = jnp.where(tri, 0.0, dp)
                dv = _dot(scores, d_out, ta=True) + _dot(k_dec, dst, tb=True)
                dk_dec = _dot(v, dst)
                dq_fw = _dot(dp_fw, k_fw) + _dot(d_out, st)
                dk_fw = _dot(dp_fw, q_fw, ta=True) + dk_dec * dl
                dq_bw = _dot(dp_bw, k_bw)
                dk_bw = _dot(dp_bw, q_bw, ta=True)
                ddl = jnp.sum(st * dst, axis=0, keepdims=True) + jnp.sum(k_fw * dk_dec, axis=0, keepdims=True)
                dstate[h] = dst * dl + _dot(d_out, q_fw, ta=True)
                dq = (dq_fw * e_pos + dq_bw * e_neg) * (HK ** -0.5)
                dk = dk_fw * e_neg + dk_bw * e_pos
                dbs.append(dq_fw * q_fw - dk_fw * k_fw - dq_bw * q_bw + dk_bw * k_bw + jnp.where(last_row, ddl * dl, 0.0))
                dqkv_ref[rows, h * HK:(h + 1) * HK] = dq.astype(BF)
                dqkv_ref[rows, GLA_DK + h * HK:GLA_DK + (h + 1) * HK] = dk.astype(BF)
                dqkv_ref[rows, 2 * GLA_DK + h * HV:2 * GLA_DK + (h + 1) * HV] = dv.astype(BF)
            dla_ref[rows, :] = _dot_exact(upper, jnp.concatenate(dbs, axis=1))

    rev = lambda i: (GLA_STEPS - 1 - i, 0)
    return pl.pallas_call(
        body, name="gla_bwd", grid=(GLA_STEPS,),
        in_specs=[pl.BlockSpec(memory_space=pl.ANY), pl.BlockSpec((GLA_ROWS, QKV_W), rev),
                  pl.BlockSpec((GLA_ROWS, GLA_DK), rev), pl.BlockSpec((GLA_ROWS, D_MODEL), rev),
                  pl.BlockSpec((GLA_CPS, HEADS, HV, HK), lambda i: (GLA_STEPS - 1 - i, 0, 0, 0))],
        out_specs=[pl.BlockSpec((GLA_ROWS, QKV_W), rev), pl.BlockSpec((GLA_ROWS, GLA_DK), rev)],
        out_shape=[_out_hbm((SEQ, N_DZ), BF), _out_hbm((SEQ, GLA_DK), F32)],
        scratch_shapes=[pltpu.VMEM((HEADS, HV, HK), F32)], input_output_aliases={0: 0},
        compiler_params=_params("arbitrary"),
    )(*map(_in_hbm, (dzcat, zcat, la, d_o, states)))


def _silu_parts(x):
    s = _sigmoid(x)
    return x * s, s * (1.0 + x * (1.0 - s))


def _post_gla_fwd(o, zcat, g_head):
    def body(o_ref, zog_ref, g_ref, out_ref):
        for h in range(HEADS):
            cols = slice(h * HV, (h + 1) * HV)
            ov = o_ref[:, cols]
            r = lax.rsqrt(jnp.mean(ov * ov, axis=-1, keepdims=True) + EPS)
            act, _ = _silu_parts(zog_ref[:, cols].astype(F32))
            out_ref[:, cols] = (ov * r * g_ref[...] * act).astype(BF)

    tile = pl.BlockSpec((TOK_TILE, D_MODEL), lambda i: (i, 0))
    return pl.pallas_call(
        body, name="post_gla_fwd", grid=(SEQ // TOK_TILE,),
        in_specs=[tile, pl.BlockSpec((TOK_TILE, D_MODEL), lambda i: (i, C_OG // D_MODEL)), _const_spec((1, HV))],
        out_specs=tile, out_shape=_out_hbm((SEQ, D_MODEL), BF), compiler_params=_params("parallel"),
    )(*map(_in_hbm, (o, zcat, g_head)))


def _post_gla_bwd(dzcat, dy_gla, w_gla_proj, o, zcat, g_head, after):
    def body(dz_in, dyg_ref, w_ref, o_ref, zog_ref, g_ref, after_ref, dz_ref, do_ref, dg_ref):
        del dz_in, after_ref
        dog = _dot(dyg_ref[...], w_ref[...], tb=True)
        gpart = jnp.zeros((1, HV), F32)
        gv = g_ref[...]
        for h in range(HEADS):
            cols = slice(h * HV, (h + 1) * HV)
            ov = o_ref[:, cols]
            r = lax.rsqrt(jnp.mean(ov * ov, axis=-1, keepdims=True) + EPS)
            on = ov * r
            act, dact = _silu_parts(zog_ref[:, cols].astype(F32))
            dogv = dog[:, cols]
            dz_ref[:, cols] = (dogv * on * gv * dact).astype(BF)
            d_on_g = dogv * act
            gpart = gpart + jnp.sum(d_on_g * on, axis=0, keepdims=True)
            dxn = d_on_g * gv
            do_ref[:, cols] = (r * (dxn - on * jnp.mean(dxn * on, axis=-1, keepdims=True))).astype(BF)

        @pl.when(pl.program_id(0) == 0)
        def _():
            dg_ref[...] = gpart

        @pl.when(pl.program_id(0) > 0)
        def _():
            dg_ref[...] += gpart

    tile = pl.BlockSpec((TOK_TILE, D_MODEL), lambda i: (i, 0))
    ogspec = pl.BlockSpec((TOK_TILE, D_MODEL), lambda i: (i, C_OG // D_MODEL))
    return pl.pallas_call(
        body, name="post_gla_bwd", grid=(SEQ // TOK_TILE,),
        in_specs=[pl.BlockSpec(memory_space=pl.ANY), tile, _const_spec((D_MODEL, D_MODEL)), tile, ogspec,
                  _const_spec((1, HV)), pl.BlockSpec(memory_space=pl.ANY)],
        out_specs=[ogspec, tile, _const_spec((1, HV))],
        out_shape=[_out_hbm((SEQ, N_DZ), BF), _out_hbm((SEQ, D_MODEL), BF),
                   _out_hbm((1, HV), F32)],
        input_output_aliases={0: 0}, compiler_params=_params("arbitrary"),
    )(*map(_in_hbm, (dzcat, dy_gla, w_gla_proj, o, zcat, g_head)), after)


GATE_W = 2 * D_MODEL


def _mix_out_fwd(ps, og, zcat, x, w_pool_proj, w_gla_proj, w_out, b_gate, g_ffn, after):
    def body(ps_ref, og_ref, zg_ref, x_ref, wpp_ref, wgp_ref, wout_ref, b_ref, g_ref, after_ref,
             yp_ref, yg_ref, mixed_ref, x1_ref, h2_ref):
        del after_ref
        y_pool = _dot(ps_ref[...], wpp_ref[...])
        y_gla = _dot(og_ref[...], wgp_ref[...])
        yp_ref[...] = y_pool.astype(BF)
        yg_ref[...] = y_gla.astype(BF)
        g0 = _sigmoid(zg_ref[:, :D_MODEL].astype(F32) + b_ref[:, :D_MODEL])
        g1 = _sigmoid(zg_ref[:, D_MODEL:].astype(F32) + b_ref[:, D_MODEL:])
        mixed = (g0 * y_pool + g1 * y_gla).astype(BF)
        mixed_ref[...] = mixed
        x1 = x_ref[...] + _dot(mixed, wout_ref[...])
        x1_ref[...] = x1
        r = lax.rsqrt(jnp.mean(x1 * x1, axis=-1, keepdims=True) + EPS)
        h2_ref[...] = (x1 * r * g_ref[...]).astype(BF)

    tile = pl.BlockSpec((TOK_TILE, D_MODEL), lambda i: (i, 0))
    resident = lambda shape: pl.BlockSpec(shape, lambda i: (0, 0), pipeline_mode=pl.Buffered(1))
    f32, bf16 = _out_hbm((SEQ, D_MODEL), F32), _out_hbm((SEQ, D_MODEL), BF)
    return pl.pallas_call(
        body, name="mix_out_fwd", grid=(SEQ // TOK_TILE,),
        in_specs=[pl.BlockSpec((TOK_TILE, POOL_WIDTH), lambda i: (i, 0)), tile,
                  pl.BlockSpec((TOK_TILE, GATE_W), lambda i: (i, C_GATE // GATE_W)), tile,
                  resident((POOL_WIDTH, D_MODEL)), resident((D_MODEL, D_MODEL)), resident((D_MODEL, D_MODEL)),
                  _const_spec((1, GATE_W)), _const_spec((1, D_MODEL)), pl.BlockSpec(memory_space=pl.ANY)],
        out_specs=[tile] * 5, out_shape=[bf16, bf16, bf16, f32, bf16], compiler_params=_params("parallel"),
    )(*map(_in_hbm, (ps, og, zcat, x, w_pool_proj, w_gla_proj, w_out, b_gate, g_ffn)), after)


def _mix_bwd(dx1, w_out, zcat, b_gate, y_pool, y_gla):
    def body(dx_ref, w_ref, zg_ref, b_ref, yp_ref, yg_ref, dz_ref, dyp_ref, dyg_ref, db_ref):
        dm = _dot(dx_ref[...], w_ref[...], tb=True)
        g0 = _sigmoid(zg_ref[:, :D_MODEL].astype(F32) + b_ref[:, :D_MODEL])
        g1 = _sigmoid(zg_ref[:, D_MODEL:].astype(F32) + b_ref[:, D_MODEL:])
        dyp_ref[...] = (dm * g0).astype(BF)
        dyg_ref[...] = (dm * g1).astype(BF)
        dz0 = dm * yp_ref[...].astype(F32) * g0 * (1.0 - g0)
        dz1 = dm * yg_ref[...].astype(F32) * g1 * (1.0 - g1)
        dz_ref[:, :D_MODEL] = dz0.astype(BF)
        dz_ref[:, D_MODEL:] = dz1.astype(BF)
        b0 = jnp.sum(dz0, axis=0, keepdims=True)
        b1 = jnp.sum(dz1, axis=0, keepdims=True)

        @pl.when(pl.program_id(0) == 0)
        def _():
            db_ref[:, :D_MODEL] = b0
            db_ref[:, D_MODEL:] = b1

        @pl.when(pl.program_id(0) > 0)
        def _():
            db_ref[:, :D_MODEL] += b0
            db_ref[:, D_MODEL:] += b1

    tile = pl.BlockSpec((TOK_TILE, D_MODEL), lambda i: (i, 0))
    gspec = pl.BlockSpec((TOK_TILE, GATE_W), lambda i: (i, C_GATE // GATE_W))
    return pl.pallas_call(
        body, name="mix_bwd", grid=(SEQ // TOK_TILE,),
        in_specs=[tile, _const_spec((D_MODEL, D_MODEL)), gspec, _const_spec((1, GATE_W)), tile, tile],
        out_specs=[gspec, tile, tile, _const_spec((1, GATE_W))],
        out_shape=[_out_hbm((SEQ, N_DZ), BF), _out_hbm((SEQ, D_MODEL), BF),
                   _out_hbm((SEQ, D_MODEL), BF), _out_hbm((1, GATE_W), F32)],
        compiler_params=_params("arbitrary"),
    )(*map(_in_hbm, (dx1, w_out, zcat, b_gate, y_pool, y_gla)))


N_TOK_TILES = SEQ // TOK_TILE
HALO_PER_TILE = TOK_TILE // HALO


LANE_TILES = tuple((lo, min(128, FF_BLK - lo)) for lo in range(0, FF_BLK, 128))


def _taps(w_ref, b_ref, half, lanes, rows):
    shape = (rows, lanes.stop - lanes.start)
    return ([jnp.broadcast_to(w_ref[half, j:j + 1, lanes], shape) for j in range(3)],
            jnp.broadcast_to(b_ref[half, :, lanes], shape))


def _conv_strips(u_ref, ub_ref, ua_ref, taps, lanes, width, n_strips, first):
    row = lax.broadcasted_iota(jnp.int32, (HALO, width), 0)
    prev = [[pltpu.roll(jnp.where(first, 0.0, ub_ref[half, :, lanes]), k, 0) for k in (1, 2)] for half in range(2)]
    for s in range(n_strips + (ua_ref is not None)):
        u3, conv = [], []
        for half in range(2):
            cur = u_ref[half, s * HALO:(s + 1) * HALO, lanes] if s < n_strips else ua_ref[half, :, lanes]
            rolled = [pltpu.roll(cur, k, 0) for k in (1, 2)]
            frames = [jnp.where(row >= 2, rolled[1], prev[half][1]), jnp.where(row >= 1, rolled[0], prev[half][0]), cur]
            prev[half] = rolled
            w3, bias = taps[half]
            u3.append(frames)
            conv.append(bias + frames[0] * w3[0] + frames[1] * w3[1] + frames[2] * w3[2])
        yield s, u3, conv


def _pair_specs(pairs):
    tile = pl.BlockSpec((pairs, None, TOK_TILE, FF_BLK), lambda b, i: (0, b, i, 0))
    before = pl.BlockSpec((pairs, None, HALO, FF_BLK), lambda b, i: (0, b, jnp.maximum(i * HALO_PER_TILE - 1, 0), 0))
    after = pl.BlockSpec((pairs, None, HALO, FF_BLK),
                         lambda b, i: (0, b, jnp.minimum((i + 1) * HALO_PER_TILE, SEQ // HALO - 1), 0))

    def vec(rows):
        return pl.BlockSpec((2, None, rows, FF_BLK), lambda b, i: (0, b, 0, 0))

    return tile, before, after, vec


N_STRIPS = TOK_TILE // HALO


def _up_conv_fwd(h2, wt_up, w_conv, b_conv):
    steps = N_TOK_TILES // 2

    def body(h_ref, h_next, wg_ref, wv_ref, w_ref, b_ref, u_ref, a_ref, buf_a, buf_b, carry):
        j = pl.program_id(1)

        def project(hv, buf):
            buf[0] = _dot(hv, wg_ref[...], tb=True)
            buf[1] = _dot(hv, wv_ref[...], tb=True)

        def conv(buf, row0):
            u_ref[:, row0:row0 + TOK_TILE, :] = buf[...]
            for lo, width in LANE_TILES:
                lanes = slice(lo, lo + width)
                taps = [_taps(w_ref, b_ref, half, lanes, HALO) for half in range(2)]
                pending = None
                for s, _, (cg, cv) in _conv_strips(buf, carry, None, taps, lanes, width, N_STRIPS, False):
                    act = cg * _sigmoid(cg) * cv
                    if s % 2 == 0:
                        pending = act
                    else:
                        a_ref[0, row0 + (s - 1) * HALO:row0 + (s + 1) * HALO, lanes] = (
                            jnp.concatenate([pending, act], axis=0).astype(BF))
            carry[...] = buf[:, TOK_TILE - HALO:, :]

        @pl.when(j == 0)
        def _():
            project(h_ref[0:TOK_TILE, :], buf_a)
            carry[...] = jnp.zeros_like(carry)

        project(h_ref[TOK_TILE:, :], buf_b)
        conv(buf_a, 0)
        project(h_next[...], buf_a)
        conv(buf_b, TOK_TILE)

    w_blk = lambda half: pl.BlockSpec((FF_BLK, D_MODEL), lambda b, j: (b + 4 * half, 0))
    vec = lambda rows: pl.BlockSpec((2, None, rows, FF_BLK), lambda b, j: (0, b, 0, 0))
    u_buf = pltpu.VMEM((2, TOK_TILE, FF_BLK), F32)
    return pl.pallas_call(
        body, name="up_conv_fwd", grid=(4, steps),
        in_specs=[pl.BlockSpec((2 * TOK_TILE, D_MODEL), lambda b, j: (j, 0)),
                  pl.BlockSpec((TOK_TILE, D_MODEL), lambda b, j: (jnp.minimum(2 * j + 2, N_TOK_TILES - 1), 0)),
                  w_blk(0), w_blk(1), vec(3), vec(1)],
        out_specs=[pl.BlockSpec((2, None, 2 * TOK_TILE, FF_BLK), lambda b, j: (0, b, j, 0)),
                   pl.BlockSpec((1, None, 2 * TOK_TILE, FF_BLK), lambda b, j: (0, b, j, 0))],
        out_shape=[_out_hbm((2, 4, SEQ, FF_BLK), F32), _out_hbm((1, 4, SEQ, FF_BLK), BF)],
        scratch_shapes=[u_buf, u_buf, pltpu.VMEM((2, HALO, FF_BLK), F32)],
        compiler_params=_params("parallel", "arbitrary"),
    )(*map(_in_hbm, (h2, h2, wt_up, wt_up, w_conv, b_conv)))


def _conv_bwd(u, da, w_conv, b_conv):
    def body(u_ref, ub_ref, ua_ref, da_ref, daa_ref, w_ref, b_ref, du_ref, dw_ref, db_ref):
        i = pl.program_id(1)

        @pl.when(i == 0)
        def _():
            dw_ref[...] = jnp.zeros_like(dw_ref)
            db_ref[...] = jnp.zeros_like(db_ref)

        for lo, width in LANE_TILES:
            lanes = slice(lo, lo + width)
            row = lax.broadcasted_iota(jnp.int32, (HALO, width), 0)
            taps = [_taps(w_ref, b_ref, half, lanes, HALO) for half in range(2)]
            acc_w = [[jnp.zeros((HALO, width), F32) for _ in range(3)] for _ in range(2)]
            acc_b = [jnp.zeros((HALO, width), F32) for _ in range(2)]
            da_pair, pending = None, [None, None]
            dc_prev, up_prev = [None, None], [None, None]
            for s, u3, (cg, cv) in _conv_strips(u_ref, ub_ref, ua_ref, taps, lanes, width, N_STRIPS, i == 0):
                act, dact = _silu_parts(cg)
                if s == N_STRIPS:
                    da = jnp.where(i < N_TOK_TILES - 1, daa_ref[0, :, lanes].astype(F32), 0.0)
                elif s % 2 == 0:
                    da_pair = da_ref[0, s * HALO:(s + 2) * HALO, lanes].astype(F32)
                    da = da_pair[:HALO]
                else:
                    da = da_pair[HALO:]
                dc = (da * cv * dact, da * act)
                for half in range(2):
                    up = [pltpu.roll(dc[half], HALO - k, 0) for k in (1, 2)]
                    if s < N_STRIPS:
                        for j in range(3):
                            acc_w[half][j] = acc_w[half][j] + dc[half] * u3[half][j]
                        acc_b[half] = acc_b[half] + dc[half]
                    if s >= 1:
                        w3 = taps[half][0]
                        du = (dc_prev[half] * w3[2] + jnp.where(row < HALO - 1, up_prev[half][0], up[0]) * w3[1]
                              + jnp.where(row < HALO - 2, up_prev[half][1], up[1]) * w3[0])
                        if (s - 1) % 2 == 0:
                            pending[half] = du
                        else:
                            du_ref[half, (s - 2) * HALO:s * HALO, lanes] = jnp.concatenate([pending[half], du],
                                                                                           axis=0).astype(BF)
                    dc_prev[half], up_prev[half] = dc[half], up
            for half in range(2):
                for j in range(3):
                    dw_ref[half, j:j + 1, lanes] += jnp.sum(acc_w[half][j], axis=0, keepdims=True)
                db_ref[half, :, lanes] += jnp.sum(acc_b[half], axis=0, keepdims=True)

    tile, before, after, vec = _pair_specs(2)
    da_tile, _, da_after_spec, _ = _pair_specs(1)
    return pl.pallas_call(
        body, name="conv_bwd", grid=(4, N_TOK_TILES),
        in_specs=[tile, before, after, da_tile, da_after_spec, vec(3), vec(1)],
        out_specs=[tile, vec(3), vec(1)],
        out_shape=[_out_hbm((2, 4, SEQ, FF_BLK), BF), _out_hbm((2, 4, 3, FF_BLK), F32),
                   _out_hbm((2, 4, 1, FF_BLK), F32)],
        compiler_params=_params("parallel", "arbitrary"),
    )(*map(_in_hbm, (u, u, u, da, da, w_conv, b_conv)))


W_IN_SEGMENTS = ((R_POOL, POOL_WIDTH, C_POOL), (R_QKV, QKV_W, C_QKV), (R_OG, D_MODEL, C_OG), (R_GK, GATE_RANK, C_GK),
                 (R_GATE, GATE_W, C_GATE))


def _slab_pieces(d):
    lo, hi = d * IN_SHARD, (d + 1) * IN_SHARD
    pieces = []
    for start, n, at in W_IN_SEGMENTS:
        a, b = max(lo, start), min(hi, start + n)
        if a < b:
            assert (a - lo) % 2 == 0 and (b - a) % 2 == 0 and (at + a - start) % 2 == 0
            pieces.append(((a - lo) // 2, (b - a) // 2, (at + a - start) // 2))
    return pieces


def _unshard_w_in(slabs):
    def body(slab_ref, cat_ref):
        d = pl.program_id(0)
        src = slab_ref.bitcast(jnp.uint32)
        dst = cat_ref.bitcast(jnp.uint32)

        @pl.when(d == 0)
        def _():
            cat_ref[C_GK:, :] = jnp.zeros((GK_PAD, D_MODEL), BF)

        for dd in range(N_DEV):
            @pl.when(d == dd)
            def _():
                for a, n, at in _slab_pieces(dd):
                    dst[pl.ds(at, n), :] = src[0, pl.ds(a, n), :]

    return pl.pallas_call(
        body, name="unshard_w_in", grid=(N_DEV,),
        in_specs=[pl.BlockSpec((1, IN_SHARD, D_MODEL), lambda d: (d, 0, 0))], out_specs=_const_spec((N_DZ, D_MODEL)),
        out_shape=_out_hbm((N_DZ, D_MODEL), BF), compiler_params=_params("arbitrary"),
    )(_in_hbm(slabs))


def _shard_d_w_in(d_cat):
    def body(cat_ref, slab_ref):
        d = pl.program_id(0)
        cat = cat_ref.bitcast(jnp.uint32)
        dst = slab_ref.bitcast(jnp.uint32)
        for dd in range(N_DEV):
            @pl.when(d == dd)
            def _():
                for a, n, at in _slab_pieces(dd):
                    dst[0, pl.ds(a, n), :] = cat[pl.ds(at, n), :]

    return pl.pallas_call(
        body, name="shard_d_w_in", grid=(N_DEV,), in_specs=[_const_spec((N_DZ, D_MODEL))],
        out_specs=pl.BlockSpec((1, IN_SHARD, D_MODEL), lambda d: (d, 0, 0)),
        out_shape=_out_hbm((N_DEV, IN_SHARD, D_MODEL), BF), compiler_params=_params("parallel"),
    )(_in_hbm(d_cat))


ANY = pl.BlockSpec(memory_space=pl.ANY)


def _place():
    x, y, c = lax.axis_index("x"), lax.axis_index("y"), lax.axis_index("c")
    other_chips = [(1 - x, y), (x, 1 - y), (1 - x, 1 - y)]
    return x, y, c, other_chips


SEM = pl.BlockSpec(memory_space=pltpu.SEMAPHORE)
IN_HBM = pl.BlockSpec(memory_space=pltpu.HBM)
SPLIT_PARAMS = pltpu.CompilerParams(has_side_effects=pltpu.SideEffectType.DATAFLOW_SIDE_EFFECTING)


def _gather_first(refs, send_sems, recv_sems):
    x, y, c, chips = _place()
    targets = [(x, y, 1 - c)] + [(px, py, c) for px, py in chips]
    return [pltpu.make_async_remote_copy(src_ref=refs[2 * a], dst_ref=refs[2 * a + 1].at[4 * x + 2 * y + c],
                                         send_sem=send_sems.at[4 * a + k], recv_sem=recv_sems.at[4 * a + k],
                                         device_id=to, device_id_type=MESH)
            for a in range(len(refs) // 2) for k, to in enumerate(targets)]


def _gather_direct(refs, send_sems, recv_sems):
    x, y, c, _ = _place()
    flips = [(dx, dy, dc) for dx in (0, 1) for dy in (0, 1) for dc in (0, 1) if dx + dy + dc]
    targets = [(1 - x if dx else x, 1 - y if dy else y, 1 - c if dc else c) for dx, dy, dc in flips]
    return [pltpu.make_async_remote_copy(src_ref=refs[2 * a], dst_ref=refs[2 * a + 1].at[4 * x + 2 * y + c],
                                         send_sem=send_sems.at[7 * a + k], recv_sem=recv_sems.at[7 * a + k],
                                         device_id=to, device_id_type=MESH)
            for a in range(len(refs) // 2) for k, to in enumerate(targets)]


def _gather_second(refs, send_sems, recv_sems):
    x, y, c, chips = _place()
    copies = []
    for a, land in enumerate(refs):
        for j, (px, py) in enumerate(chips):
            block = land.at[4 * px + 2 * py + c]
            copies.append(pltpu.make_async_remote_copy(src_ref=block, dst_ref=block, send_sem=send_sems.at[3 * a + j],
                                                       recv_sem=recv_sems.at[3 * a + j], device_id=(x, y, 1 - c),
                                                       device_id_type=MESH))
    return copies


def _reduce_first(refs, send_sems, recv_sems):
    x, y, c, _ = _place()
    return [pltpu.make_async_remote_copy(src_ref=refs[2 * a].at[j, 1 - c], dst_ref=refs[2 * a + 1].at[j],
                                         send_sem=send_sems.at[4 * a + j], recv_sem=recv_sems.at[4 * a + j],
                                         device_id=(x, y, 1 - c), device_id_type=MESH)
            for a in range(len(refs) // 2) for j in range(4)]


def _reduce_second(refs, send_sems, recv_sems):
    _, _, c, chips = _place()
    return [pltpu.make_async_remote_copy(src_ref=refs[2 * a].at[2 * px + py], dst_ref=refs[2 * a + 1].at[k],
                                         send_sem=send_sems.at[3 * a + k], recv_sem=recv_sems.at[3 * a + k],
                                         device_id=(px, py, c), device_id_type=MESH)
            for a in range(len(refs) // 2) for k, (px, py) in enumerate(chips)]


def _split_start(name, groups):
    arrays = [a for g in groups for a in g[0]]
    n = len(arrays)

    def body(*refs):
        sems = refs[n:n + 2 * len(groups)]
        at = 0
        for gi, (members, _, build) in enumerate(groups):
            for cp in build(refs[at:at + len(members)], sems[2 * gi], sems[2 * gi + 1]):
                cp.start()
            at += len(members)
        refs[-1][...] = jnp.zeros_like(refs[-1])

    sem_shapes = [pltpu.SemaphoreType.DMA((g[1],)) for g in groups for _ in range(2)]
    outs = pl.pallas_call(
        body, name=name, in_specs=[IN_HBM] * n,
        out_shape=(*sem_shapes, *[_out_hbm(a.shape, a.dtype) for a in arrays], jax.ShapeDtypeStruct((8, 128), F32)),
        out_specs=(*[SEM] * len(sem_shapes), *[IN_HBM] * n, pl.BlockSpec(memory_space=pltpu.VMEM)),
        input_output_aliases={i: len(sem_shapes) + i for i in range(n)}, compiler_params=SPLIT_PARAMS,
    )(*[pltpu.with_memory_space_constraint(a, pltpu.HBM) for a in arrays])
    per_group, at = [], len(sem_shapes)
    for gi, (members, _, _) in enumerate(groups):
        per_group.append((outs[2 * gi], outs[2 * gi + 1], list(outs[at:at + len(members)])))
        at += len(members)
    return per_group, outs[-1]


def _split_wait(name, started, build, after):
    send_sems, recv_sems, arrays = started
    n = len(arrays)
    after = after if isinstance(after, (tuple, list)) else (after,)

    def body(*refs):
        for cp in build(refs[:n], refs[n], refs[n + 1]):
            cp.wait_send()
            cp.wait_recv()

    return pl.pallas_call(
        body, name=name, in_specs=[IN_HBM] * n + [SEM, SEM] + [ANY] * len(after),
        out_shape=tuple(_out_hbm(a.shape, a.dtype) for a in arrays), out_specs=tuple([IN_HBM] * n),
        input_output_aliases={i: i for i in range(n)}, compiler_params=SPLIT_PARAMS,
    )(*arrays, send_sems, recv_sems, *after)


def _placed_behind(token, arrays, name):
    n = len(arrays)

    def body(*refs):
        refs[-1][...] = jnp.zeros_like(refs[-1])

    outs = pl.pallas_call(
        body, name=name, in_specs=[IN_HBM] * n + [ANY],
        out_shape=(*[_out_hbm(a.shape, a.dtype) for a in arrays], jax.ShapeDtypeStruct((8, 128), F32)),
        out_specs=(*[IN_HBM] * n, pl.BlockSpec(memory_space=pltpu.VMEM)),
        input_output_aliases={i: i for i in range(n)},
    )(*map(_in_hbm, arrays), token)
    return outs[:n], outs[-1]


def _gather_landing(shard, me):
    return lax.dynamic_update_slice(lax.empty((N_DEV,) + shard.shape, shard.dtype), shard[None],
                                    (me,) + (0,) * shard.ndim)


ADAM_LANE_TILE = 256


def _tile_2d(rows, cols):
    for t in (256, 176, 128):
        if rows % t == 0:
            return t, cols
    return rows, ADAM_LANE_TILE


def _pair_sum(part, recv, core, name):
    _, rows, cols = recv.shape
    tr, tc = rows, cols

    def body(c_ref, p_ref, r_ref, o_ref):
        del c_ref
        o_ref[...] = (p_ref[...].astype(F32) + r_ref[...].astype(F32)).astype(BF)

    grid_spec = pltpu.PrefetchScalarGridSpec(
        num_scalar_prefetch=1, grid=(4, rows // tr, cols // tc),
        in_specs=[pl.BlockSpec((None, None, tr, tc), lambda j, i, k, c_ref: (j, c_ref[0], i, k)),
                  pl.BlockSpec((None, tr, tc), lambda j, i, k, c_ref: (j, i, k))],
        out_specs=pl.BlockSpec((None, tr, tc), lambda j, i, k, c_ref: (j, i, k)))
    return pl.pallas_call(
        body, name=name, grid_spec=grid_spec, out_shape=_out_hbm(recv.shape, BF),
        compiler_params=_params("parallel", "parallel", "parallel"),
    )(core, *map(_in_hbm, (part, recv)))


def _adamw(w, g, m, v):
    m = ADAM_B1 * m + (1.0 - ADAM_B1) * g
    v = ADAM_B2 * v + (1.0 - ADAM_B2) * (g * g)
    delta = -ADAM_LR * ((m / ADAM_C1) / (jnp.sqrt(v / ADAM_C2) + ADAM_EPS) + ADAM_WD * w)
    return delta, m, v


def _chip_sum_adamw(sums, recv, w, m, v, chip, name):
    rows, cols = w.shape
    tr, tc = _tile_2d(rows, cols)

    def body(chip_ref, s_ref, r_ref, w_ref, m_ref, v_ref, g_out, d_out, m_out, v_out):
        del chip_ref
        g = s_ref[...].astype(F32)
        for k in range(3):
            g = g + r_ref[k].astype(F32)
        g_out[...] = g
        d_out[...], m_out[...], v_out[...] = _adamw(w_ref[...], g, m_ref[...], v_ref[...])

    tile = pl.BlockSpec((tr, tc), lambda i, k, chip_ref: (i, k))
    grid_spec = pltpu.PrefetchScalarGridSpec(
        num_scalar_prefetch=1, grid=(rows // tr, cols // tc),
        in_specs=[pl.BlockSpec((None, tr, tc), lambda i, k, chip_ref: (chip_ref[0], i, k)),
                  pl.BlockSpec((3, tr, tc), lambda i, k, chip_ref: (0, i, k)), tile, tile, tile],
        out_specs=[tile] * 4)
    return pl.pallas_call(
        body, name=name, grid_spec=grid_spec, out_shape=[_out_hbm((rows, cols), F32)] * 4,
        compiler_params=_params("parallel", "parallel"),
    )(chip, *map(_in_hbm, (sums, recv, w, m, v)))


def _small_sum_adamw(me, entries, loss_parts):
    def whole(shape, squeeze=0, pick=False):
        blk = (None,) * squeeze + tuple(shape[squeeze:])
        if pick:
            blk = (shape[0], None) + tuple(shape[2:])
            return pl.BlockSpec(blk, lambda i, me_ref: (0, me_ref[0]) + (0,) * (len(shape) - 2))
        return pl.BlockSpec(blk, lambda i, me_ref: (0,) * len(shape))

    in_specs, out_specs, out_shape, args = [], [], [], []
    for parts, w, m, v, sharded in entries:
        lead = w.ndim - (parts.ndim - (2 if sharded else 1))
        in_specs += [whole(parts.shape, pick=sharded)] + [whole(w.shape, squeeze=lead)] * 3
        out_specs += [whole(w.shape, squeeze=lead)] * 4
        out_shape += [_out_hbm(w.shape, F32)] * 4
        args += [parts, w, m, v]
    in_specs.append(whole(loss_parts.shape))
    out_specs.append(whole(loss_parts.shape[1:]))
    out_shape.append(_out_hbm(loss_parts.shape[1:], F32))
    n = len(entries)

    def added(p_ref):
        total = p_ref[0]
        for d in range(1, N_DEV):
            total = total + p_ref[d]
        return total

    def body(me_ref, *refs):
        del me_ref
        ins, outs = refs[:4 * n + 1], refs[4 * n + 1:]
        for e in range(n):
            p_ref, w_ref, m_ref, v_ref = ins[4 * e:4 * e + 4]
            g_out, d_out, m_out, v_out = outs[4 * e:4 * e + 4]
            g = added(p_ref)
            g_out[...] = g
            d_out[...], m_out[...], v_out[...] = _adamw(w_ref[...], g, m_ref[...], v_ref[...])
        outs[4 * n][...] = added(ins[4 * n])

    grid_spec = pltpu.PrefetchScalarGridSpec(num_scalar_prefetch=1, grid=(1,), in_specs=in_specs, out_specs=out_specs)
    outs = pl.pallas_call(body, name="small_sum_adamw", grid_spec=grid_spec, out_shape=out_shape,
                          compiler_params=_params("arbitrary"))(me, *map(_in_hbm, args + [loss_parts]))
    return [outs[4 * e:4 * e + 4] for e in range(n)], outs[4 * n]


MM_TILE = 512
N_MM_TILES = SEQ // MM_TILE
CAT_TILE = 512
N_CAT_TILES = N_CAT // CAT_TILE
DZ_TILE = 640


def kernel(x, g_mix, w_in, b_gate, w_gk_up, b_gk, w_pool_grp, pool_scale, g_gla_head, w_pool_proj, w_gla_proj, w_out, g_ffn, w_up, w_conv, b_conv, w_down, g_final, loss_target, m_g_mix, m_w_in, m_b_gate, m_w_gk_up, m_b_gk, m_w_pool_grp, m_pool_scale, m_g_gla_head, m_w_pool_proj, m_w_gla_proj, m_w_out, m_g_ffn, m_w_up, m_w_conv, m_b_conv, m_w_down, m_g_final, v_g_mix, v_w_in, v_b_gate, v_w_gk_up, v_b_gk, v_w_pool_grp, v_pool_scale, v_g_gla_head, v_w_pool_proj, v_w_gla_proj, v_w_out, v_g_ffn, v_w_up, v_w_conv, v_b_conv, v_w_down, v_g_final):
    xi, yi, ci = lax.axis_index("x"), lax.axis_index("y"), lax.axis_index("c")
    me = 4 * xi + 2 * yi + ci
    core = jnp.reshape(ci, (1,)).astype(jnp.int32)
    chip = jnp.reshape(2 * xi + yi, (1,)).astype(jnp.int32)
    xs, target = x[0], loss_target[0]

    big = dict(w_in=w_in[0].T, w_pool_proj=w_pool_proj[0], w_gla_proj=w_gla_proj[0], w_out=w_out[0], w_up=w_up[0].T,
               w_down=w_down[0])
    moments = dict(w_in=(m_w_in[0].T, v_w_in[0].T), w_pool_proj=(m_w_pool_proj[0], v_w_pool_proj[0]),
                   w_gla_proj=(m_w_gla_proj[0], v_w_gla_proj[0]), w_out=(m_w_out[0], v_w_out[0]),
                   w_up=(m_w_up[0].T, v_w_up[0].T), w_down=(m_w_down[0], v_w_down[0]))
    names = list(big)
    shards = {k: big[k].astype(BF) for k in names}
    shards["w_gk_up"], shards["w_conv"] = w_gk_up[0], w_conv[0]
    gather_groups = (("w_in", "w_gk_up"), ("w_pool_proj", "w_gla_proj", "w_out"), ("w_up", "w_down", "w_conv"))
    started, token = _split_start("gather_start", [
        ([t for k in g for t in (shards[k], _gather_landing(shards[k], me))], 4 * len(g), _gather_first)
        for g in gather_groups])
    conv_vec = lambda t: t.reshape(2, 4, 1, FF_BLK)
    (big["w_in"], *moments["w_in"], bconv4, m_bconv4, v_bconv4, m_w_conv, v_w_conv), token = _placed_behind(
        token, [big["w_in"], *moments["w_in"], *map(conv_vec, (b_conv, m_b_conv, v_b_conv)), m_w_conv, v_w_conv],
        "place_adamw_operands")

    def gather_pass(gi, after):
        lands = list(_split_wait(f"gather_wait_{gi}", started[gi], _gather_first, after)[1::2])
        passed, tkn = _split_start(f"gather_pass_{gi}", [(lands, 3 * len(lands), _gather_second)])
        return passed[0], tkn

    def gather_done(gi, passed, after):
        return dict(zip(gather_groups[gi], _split_wait(f"gather_pass_wait_{gi}", passed, _gather_second, after)))

    tok = lambda i, j, k: (i, 0)
    whole = lambda i, j, k: (0, 0)
    kblk = lambda i, j, k: (k, 0)
    ff_seq = (None, None, SEQ, FF_BLK)

    h = _rms_fwd(xs, g_mix, token, "rms_mix")
    wg = gather_done(0, gather_pass(0, h)[0], h)
    wt_cat = _unshard_w_in(wg["w_in"])
    wgk_pad = jnp.pad(wg["w_gk_up"].transpose(1, 0, 2).reshape(GATE_RANK, GLA_DK), ((0, GK_PAD - GATE_RANK), (0, 0)))
    zcat = _mm(h, wt_cat, out_shape=(SEQ, N_CAT), out_dtype=BF, grid=(N_CAT_TILES, 1, 1),
               blk_a=(SEQ, D_MODEL), blk_b=(CAT_TILE, D_MODEL), blk_o=(SEQ, CAT_TILE),
               map_a=whole, map_b=lambda j, i, k: (j, 0), map_o=lambda j, i, k: (0, j), tb=True, name="mm_in")
    la = _gk_fwd(h, wt_cat, wgk_pad, b_gk)
    passed, tkn = gather_pass(1, la)
    o, states = _gla_fwd(zcat, la, tkn)
    wg = gather_done(1, passed, o)
    wpp = wg["w_pool_proj"].transpose(1, 0, 2).reshape(POOL_WIDTH, D_MODEL)
    wgp = wg["w_gla_proj"].reshape(D_MODEL, D_MODEL)
    wout = wg["w_out"].reshape(D_MODEL, D_MODEL)
    og = _post_gla_fwd(o, zcat, g_gla_head)
    ps = _pool_fwd(zcat, w_pool_grp[0], pool_scale)
    passed, tkn = gather_pass(2, (og, ps))
    y_pool, y_gla, mixed, x1, h2 = _mix_out_fwd(ps, og, zcat, xs, wpp, wgp, wout, b_gate, g_ffn, tkn)
    wg = gather_done(2, passed, h2)
    wt_up = wg["w_up"].reshape(2 * D_FF, D_MODEL)
    wdown = wg["w_down"].reshape(D_FF, D_MODEL)
    wconv4 = wg["w_conv"].reshape(2, 4, 3, FF_BLK)
    blk4 = lambda b, i, k: (b // 4, b % 4, 0, 0)
    u4, act = _up_conv_fwd(h2, wt_up, wconv4, bconv4)
    loss_part, dx2, dx2_bf, dg_final = _mm_tokens(
        act, wdown, blk_a=(None, 4, TOK_MM_TILE, FF_BLK), map_a=lambda i: (0, 0, i, 0),
        pieces=[(b, b * FF_BLK, FF_BLK) for b in range(4)], res=x1, then=("loss", g_final.reshape(1, D_MODEL), target),
        name="mm_down_loss")

    da = _mm(dx2_bf, wdown, out_shape=(1, 4, SEQ, FF_BLK), out_dtype=BF, grid=(4, 1, 1),
             blk_a=(SEQ, D_MODEL), blk_b=(FF_BLK, D_MODEL), blk_o=ff_seq,
             map_a=whole, map_b=lambda b, i, k: (b, 0), map_o=lambda b, i, k: (0, b, 0, 0), tb=True, name="mm_d_act")
    d_wdown = _mm(act, dx2_bf, out_shape=(D_FF, D_MODEL), out_dtype=BF, grid=(4, 1, 1),
                  blk_a=ff_seq, blk_b=(SEQ, D_MODEL), blk_o=(FF_BLK, D_MODEL),
                  map_a=lambda b, i, k: (0, b, 0, 0), map_b=whole, map_o=lambda b, i, k: (b, 0), ta=True,
                  name="mm_d_wdown")
    du4, d_wconv, d_bconv = _conv_bwd(u4, da, wconv4, bconv4)
    d_wt_up = _mm(du4, h2, out_shape=(2 * D_FF, D_MODEL), out_dtype=BF, grid=(N_DEV, 1, 1),
                  blk_a=ff_seq, blk_b=(SEQ, D_MODEL), blk_o=(FF_BLK, D_MODEL),
                  map_a=blk4, map_b=whole, map_o=lambda b, i, k: (b, 0), ta=True, name="mm_d_wup")
    res = {}

    def to_sibling(keys, parts):
        return [t for k in keys for t in (parts[k], lax.empty((4,) + parts[k].shape[2:], BF))], 4 * len(keys), _reduce_first

    def to_chips(keys, st, after):
        arrays = _split_wait("reduce_wait_" + keys[0], st, _reduce_first, after)
        sums = [_pair_sum(p, r, core, "pair_sum_" + k) for k, p, r in zip(keys, arrays[0::2], arrays[1::2])]
        return [t for s in sums for t in (s, lax.empty((3,) + s.shape[1:], BF))], 3 * len(keys), _reduce_second

    def reduce_start(keys, parts):
        st, tkn = _split_start("reduce_start_" + keys[0], [to_sibling(keys, parts)])
        return st[0], tkn

    def reduce_cross(keys, st, after):
        st2, tkn = _split_start("reduce_cross_" + keys[0], [to_chips(keys, st, after)])
        return st2[0], tkn

    def reduce_done(keys, st2, after):
        arrays = _split_wait("reduce_cross_wait_" + keys[0], st2, _reduce_second, after)
        for k, s, r in zip(keys, arrays[0::2], arrays[1::2]):
            outs = _chip_sum_adamw(s, r, big[k], moments[k][0], moments[k][1], chip, "adamw_" + k)
            res[k] = [(t.T if k in ("w_in", "w_up") else t)[None] for t in outs]

    ffn_keys = ("w_down", "w_up")
    ffn_red, tkn = reduce_start(ffn_keys, dict(w_down=d_wdown.reshape(4, 2, D_FF // N_DEV, D_MODEL),
                                               w_up=d_wt_up.reshape(4, 2, FF_BLK, D_MODEL)))
    dx1, dg_ffn = _mm_tokens(
        du4, wt_up, blk_a=(2, 4, TOK_MM_TILE, FF_BLK), map_a=lambda i: (0, 0, i, 0),
        pieces=[((b // 4, b % 4), b * FF_BLK, FF_BLK) for b in range(N_DEV)], after=tkn, then=("rms_bwd", x1, g_ffn, dx2),
        name="mm_d_h2_rms")

    sq_t = dict(out_shape=(D_MODEL, D_MODEL), grid=(1, 1, N_MM_TILES), blk_a=(MM_TILE, D_MODEL),
                blk_b=(MM_TILE, D_MODEL), blk_o=(D_MODEL, D_MODEL), map_a=kblk, map_b=kblk, map_o=whole, ta=True)
    d_wout = _mm(mixed, dx1, out_dtype=BF, name="mm_d_wout", **sq_t)
    dzcat, dy_pool, dy_gla, db_gate = _mix_bwd(dx1, wout, zcat, b_gate, y_pool, y_gla)
    d_wgp = _mm(og, dy_gla, out_dtype=BF, name="mm_d_wgp", **sq_t)
    mix_keys = ("w_out", "w_gla_proj")
    (ffn_red, mix_red), tkn = _split_start("reduce_cross_w_down", [
        to_chips(ffn_keys, ffn_red, db_gate),
        to_sibling(mix_keys, dict(w_out=d_wout.reshape(4, 2, D_MODEL // N_DEV, D_MODEL),
                                  w_gla_proj=d_wgp.reshape(4, 2, D_MODEL // N_DEV, D_MODEL)))])
    dzcat, d_o, dg_head = _post_gla_bwd(dzcat, dy_gla, wgp, o, zcat, g_gla_head, tkn)
    dzcat, dla = _gla_bwd(dzcat, zcat, la, d_o, states)
    dzcat, d_wgk, db_gk = _gk_bwd(dzcat, dla, h, wt_cat, wgk_pad, b_gk)
    dps = _mm(dy_pool, wpp, out_shape=(SEQ, POOL_WIDTH), out_dtype=F32, grid=(N_MM_TILES, 1, 1),
              blk_a=(MM_TILE, D_MODEL), blk_b=(POOL_WIDTH, D_MODEL), blk_o=(MM_TILE, POOL_WIDTH),
              map_a=tok, map_b=whole, map_o=tok, tb=True, name="mm_d_ps")
    d_wpp = _mm(ps, dy_pool, out_shape=(POOL_WIDTH, D_MODEL), out_dtype=F32, grid=(1, 1, N_MM_TILES),
                blk_a=(MM_TILE, POOL_WIDTH), blk_b=(MM_TILE, D_MODEL), blk_o=(POOL_WIDTH, D_MODEL),
                map_a=kblk, map_b=kblk, map_o=whole, ta=True, name="mm_d_wpp")
    dzcat, d_wgrp, d_scale = _pool_bwd(dzcat, zcat, dps, w_pool_grp[0], pool_scale)
    row = lambda t: t.reshape(1, D_MODEL)
    small = [("b_gate", db_gate, b_gate, m_b_gate, v_b_gate, False),
             ("w_gk_up", d_wgk.reshape(GATE_RANK, N_DEV, GLA_DK // N_DEV).transpose(1, 0, 2), w_gk_up, m_w_gk_up,
              v_w_gk_up, True),
             ("b_gk", db_gk, b_gk, m_b_gk, v_b_gk, False),
             ("w_pool_grp", d_wgrp, w_pool_grp, m_w_pool_grp, v_w_pool_grp, False),
             ("pool_scale", d_scale, pool_scale, m_pool_scale, v_pool_scale, False),
             ("g_gla_head", dg_head, g_gla_head, m_g_gla_head, v_g_gla_head, False),
             ("g_ffn", dg_ffn, g_ffn, m_g_ffn, v_g_ffn, False),
             ("w_conv", d_wconv.reshape(N_DEV, 3, FF_BLK), w_conv, m_w_conv, v_w_conv, True),
             ("b_conv", d_bconv, bconv4, m_bconv4, v_bconv4, False),
             ("g_final", dg_final, row(g_final), row(m_g_final), row(v_g_final), False)]

    def to_all(parts):
        return [t for p in parts for t in (p, _gather_landing(p, me))], 7 * len(parts), _gather_direct

    (small_sent, mix_red), tkn = _split_start("small_start", [to_all([t[1] for t in small] + [loss_part]),
                                                              to_chips(mix_keys, mix_red, dla)])
    d_wt_cat = _mm(dzcat, h, out_shape=(N_DZ, D_MODEL), out_dtype=BF, grid=(N_DZ // DZ_TILE, 1, 1),
                   blk_a=(SEQ, DZ_TILE), blk_b=(SEQ, D_MODEL), blk_o=(DZ_TILE, D_MODEL),
                   map_a=lambda j, i, k: (0, j), map_b=whole, map_o=lambda j, i, k: (j, 0), ta=True, after=tkn,
                   name="mm_d_wcat")
    in_keys = ("w_in", "w_pool_proj")
    in_red, tkn = reduce_start(in_keys, dict(
        w_in=_shard_d_w_in(d_wt_cat).reshape(4, 2, IN_SHARD, D_MODEL),
        w_pool_proj=d_wpp.reshape(POOL_WIDTH, N_DEV, D_MODEL // N_DEV).transpose(1, 0, 2).astype(BF)
        .reshape(4, 2, POOL_WIDTH, D_MODEL // N_DEV)))
    reduce_done(mix_keys, mix_red, tkn)
    in_red, tkn = reduce_cross(in_keys, in_red, res["w_out"][0])
    grad_x, dg_mix = _mm_tokens(dzcat, wt_cat, blk_a=(TOK_MM_TILE, N_DZ), map_a=lambda i: (i, 0),
                                pieces=[(None, 0, N_DZ)], after=tkn, then=("rms_bwd", xs, g_mix, dx1),
                                name="mm_d_h_rms")
    (g_mix_sent,), tkn = _split_start("g_mix_start", [to_all([dg_mix])])
    reduce_done(ffn_keys, ffn_red, (grad_x, tkn))
    gathered = _split_wait("small_wait", small_sent, _gather_direct, res["w_down"][0])[1::2]
    small.append(("g_mix", dg_mix, g_mix, m_g_mix, v_g_mix, False))
    gathered = list(gathered[:-1]) + [_split_wait("g_mix_wait", g_mix_sent, _gather_direct, gathered[0])[1], gathered[-1]]
    small_out, loss_sum = _small_sum_adamw(jnp.reshape(me, (1,)).astype(jnp.int32),
                                           [(p,) + t[2:] for p, t in zip(gathered, small)], gathered[-1])
    for t, outs in zip(small, small_out):
        res[t[0]] = list(outs)
    res["b_conv"] = [t.reshape(b_conv.shape) for t in res["b_conv"]]
    res["g_final"] = [t.reshape(g_final.shape) for t in res["g_final"]]

    reduce_done(in_keys, in_red, loss_sum)
    loss = loss_sum[0, 0]
    order =["g_mix", "w_in", "b_gate", "w_gk_up", "b_gk", "w_pool_grp", "pool_scale", "g_gla_head", "w_pool_proj",
             "w_gla_proj", "w_out", "g_ffn", "w_up", "w_conv", "b_conv", "w_down", "g_final"]
    return (loss, grad_x[None], *[res[k][0] for k in order], *[res[k][1] for k in order],
            *[res[k][2] for k in order], *[res[k][3] for k in order])
```

```python
import jax
import jax.numpy as jnp
from jax import lax
from jax.experimental import pallas as pl
from jax.experimental.pallas import tpu as pltpu

F32 = jnp.float32
BF = jnp.bfloat16
HIGHEST = lax.Precision.HIGHEST
MESH = pl.DeviceIdType.MESH

N_DEV = 8
SEQ = 2048
D_MODEL = 1024
CHUNK = 64
EPS = 1e-6
POOL_WIDTH = 512
POOL_WINDOWS = (2, 4, 8, 16)
POOL_GD = 128
POOL_HALO = 16
HEADS = 4
HK = 128
HV = 256
GLA_DK = 512
GATE_RANK = 16
GATE_NORM = 16.0
D_FF = 2816
FF_BLK = 704
IN_SHARD = 706
C_QKV, C_GATE, C_OG, C_POOL, C_GK = 0, 2048, 4096, 5120, 5632
N_CAT = 5632
GK_PAD = 128
N_DZ = N_CAT + GK_PAD
R_POOL, R_QKV, R_OG, R_GK, R_GATE = 0, 512, 2560, 3584, 3600

ADAM_LR, ADAM_B1, ADAM_B2, ADAM_EPS, ADAM_WD, ADAM_STEP = 0.001, 0.9, 0.999, 1e-08, 0.01, 10
ADAM_C1 = 1.0 - ADAM_B1 ** ADAM_STEP
ADAM_C2 = 1.0 - ADAM_B2 ** ADAM_STEP

VMEM_BYTES_V7X = 64 * 1024 * 1024
VMEM_LIMIT = VMEM_BYTES_V7X * 3 // 4

TOK_TILE = 256
HALO = 8
GLA_CPS = 4


def _params(*sem):
    return pltpu.CompilerParams(dimension_semantics=sem, vmem_limit_bytes=VMEM_LIMIT)


def _const_spec(shape):
    nd = len(shape)
    return pl.BlockSpec(shape, lambda *_: (0,) * nd)


def _in_hbm(t):
    return pltpu.with_memory_space_constraint(t, pltpu.HBM)


def _out_hbm(shape, dtype):
    return pltpu.HBM(shape, dtype)


def _dot(a, b, ta=False, tb=False):
    dims = (((0 if ta else 1,), (1 if tb else 0,)), ((), ()))
    return lax.dot_general(a.astype(BF), b.astype(BF), dims, preferred_element_type=F32)


def _dot_exact(a, b):
    return jnp.dot(a, b, precision=HIGHEST, preferred_element_type=F32)


def _sigmoid(x):
    return 0.5 * jnp.tanh(0.5 * x) + 0.5


def _mm(a, b, *, out_shape, out_dtype, grid, blk_a, blk_b, blk_o, map_a, map_b, map_o, ta=False, tb=False,
        after=None, name):
    gk = grid[2]
    n_in = 2 + (after is not None)

    def body(*refs):
        a_ref, b_ref, o_ref = refs[0], refs[1], refs[n_in]
        prod = _dot(a_ref[...], b_ref[...], ta, tb)
        if gk == 1:
            o_ref[...] = prod.astype(out_dtype)
        else:
            acc = refs[n_in + 1]
            k = pl.program_id(2)

            @pl.when(k == 0)
            def _():
                acc[...] = prod

            @pl.when(k > 0)
            def _():
                acc[...] += prod

            @pl.when(k == gk - 1)
            def _():
                o_ref[...] = acc[...].astype(out_dtype)

    in_specs = [pl.BlockSpec(blk_a, map_a), pl.BlockSpec(blk_b, map_b)]
    args = [_in_hbm(a), _in_hbm(b)]
    if after is not None:
        in_specs.append(pl.BlockSpec(memory_space=pl.ANY))
        args.append(after)
    return pl.pallas_call(
        body, name=name, grid=grid, in_specs=in_specs, out_specs=pl.BlockSpec(blk_o, map_o),
        out_shape=_out_hbm(out_shape, out_dtype),
        scratch_shapes=[] if gk == 1 else [pltpu.VMEM(tuple(d for d in blk_o if d is not None), F32)],
        compiler_params=_params("parallel", "parallel", "arbitrary"),
    )(*args)


TOK_MM_TILE = 256


def _mm_tokens(a, w, *, blk_a, map_a, pieces, res=None, after=None, then=None, name):
    n_in = 2 + (res is not None) + (after is not None) + (0 if then is None else len(then) - 1)

    def accumulate(ref, part):
        @pl.when(pl.program_id(0) == 0)
        def _():
            ref[...] = part

        @pl.when(pl.program_id(0) > 0)
        def _():
            ref[...] += part

    def body(*refs):
        a_ref, w_ref = refs[:2]
        extra, outs = refs[n_in - (0 if then is None else len(then) - 1):n_in], refs[n_in:]
        total = None
        for idx, row, n in pieces:
            av = a_ref[...] if idx is None else a_ref[idx]
            prod = _dot(av, w_ref[row:row + n, :])
            total = prod if total is None else total + prod
        if res is not None:
            total = total + refs[2][...]
        if then is None:
            outs[0][...] = total
        elif then[0] == "rms_bwd":
            dx, part = _rms_bwd_tile(total, extra[0][...], extra[1][...], extra[2][...])
            outs[0][...] = dx
            accumulate(outs[1], part)
        else:
            lpart, dx, part = _loss_tile(total, extra[0][...], extra[1][...])
            outs[1][...] = dx
            outs[2][...] = dx.astype(BF)
            accumulate(outs[0], lpart)
            accumulate(outs[3], part)

    tile = pl.BlockSpec((TOK_MM_TILE, D_MODEL), lambda i: (i, 0))
    vec = _const_spec((1, D_MODEL))
    big = _out_hbm((SEQ, D_MODEL), F32)
    small = _out_hbm((1, D_MODEL), F32)
    in_specs = [pl.BlockSpec(blk_a, map_a), pl.BlockSpec(w.shape, lambda i: (0, 0), pipeline_mode=pl.Buffered(1))]
    args = [a, w]
    if res is not None:
        in_specs.append(tile)
        args.append(res)
    if after is not None:
        in_specs.append(pl.BlockSpec(memory_space=pl.ANY))
        args.append(after)
    if then is None:
        out_specs, out_shape = tile, big
    elif then[0] == "rms_bwd":
        in_specs += [tile, vec, tile]
        out_specs, out_shape = [tile, vec], [big, small]
    else:
        in_specs += [vec, tile]
        out_specs = [_const_spec((1, 128)), tile, tile, vec]
        out_shape = [_out_hbm((1, 128), F32), big, _out_hbm((SEQ, D_MODEL), BF), small]
    if then is not None:
        args += list(then[1:])
    return pl.pallas_call(
        body, name=name, grid=(SEQ // TOK_MM_TILE,), in_specs=in_specs, out_specs=out_specs, out_shape=out_shape,
        compiler_params=_params("parallel" if then is None else "arbitrary"),
    )(*[_in_hbm(t) for t in args])


def _rms_fwd(x, g, after, name):
    def body(x_ref, g_ref, after_ref, o_ref):
        del after_ref
        xv = x_ref[...]
        r = lax.rsqrt(jnp.mean(xv * xv, axis=-1, keepdims=True) + EPS)
        o_ref[...] = (xv * r * g_ref[...]).astype(BF)

    tile = pl.BlockSpec((TOK_TILE, D_MODEL), lambda i: (i, 0))
    return pl.pallas_call(
        body, name=name, grid=(SEQ // TOK_TILE,),
        in_specs=[tile, _const_spec((1, D_MODEL)), pl.BlockSpec(memory_space=pl.ANY)], out_specs=tile,
        out_shape=_out_hbm((SEQ, D_MODEL), BF), compiler_params=_params("parallel"),
    )(*map(_in_hbm, (x, g)), after)


def _rms_bwd_tile(dyv, xv, gv, dresv):
    r = lax.rsqrt(jnp.mean(xv * xv, axis=-1, keepdims=True) + EPS)
    xn = xv * r
    dxn = dyv * gv
    return dresv + r * (dxn - xn * jnp.mean(dxn * xn, axis=-1, keepdims=True)), jnp.sum(dyv * xn, axis=0, keepdims=True)


def _loss_tile(xv, gv, tv):
    r = lax.rsqrt(jnp.mean(xv * xv, axis=-1, keepdims=True) + EPS)
    xn = xv * r
    err = xn * gv - tv
    lpart = jnp.full((1, 128), 0.5 * jnp.sum(jnp.mean(err * err, axis=-1, keepdims=True)), F32)
    dyv = err * (1.0 / D_MODEL)
    dxn = dyv * gv
    return lpart, r * (dxn - xn * jnp.mean(dxn * xn, axis=-1, keepdims=True)), jnp.sum(dyv * xn, axis=0, keepdims=True)


def _pool_counts(w):
    pos = lax.broadcasted_iota(jnp.int32, (SEQ, 1), 0).astype(F32)
    return jnp.minimum(pos + 1.0, float(w))


def _pool_window(u, w, ext):
    ext[pl.ds(POOL_HALO, SEQ), :] = u
    win = u
    for j in range(1, w):
        win = win + ext[pl.ds(POOL_HALO - j, SEQ), :]
    return win / _pool_counts(w) - u


def _pool_fwd(zcat, w_grp, scale):
    def body(z_ref, w_ref, s_ref, o_ref, ext):
        ext[pl.ds(0, POOL_HALO), :] = jnp.zeros((POOL_HALO, POOL_GD), F32)
        for g, w in enumerate(POOL_WINDOWS):
            cols = slice(g * POOL_GD, (g + 1) * POOL_GD)
            p = _pool_window(z_ref[:, cols].astype(F32), w, ext)
            o_ref[:, cols] = (_dot(p, w_ref[g]) * s_ref[:, cols]).astype(BF)

    return pl.pallas_call(
        body, name="pool_fwd", grid=(1,),
        in_specs=[pl.BlockSpec((SEQ, POOL_WIDTH), lambda i: (0, C_POOL // POOL_WIDTH)),
                  _const_spec((4, POOL_GD, POOL_GD)), _const_spec((1, POOL_WIDTH))],
        out_specs=_const_spec((SEQ, POOL_WIDTH)), out_shape=_out_hbm((SEQ, POOL_WIDTH), BF),
        scratch_shapes=[pltpu.VMEM((POOL_HALO + SEQ, POOL_GD), F32)], compiler_params=_params("arbitrary"),
    )(*map(_in_hbm, (zcat, w_grp, scale)))


def _pool_bwd(dzcat, zcat, dps, w_grp, scale):
    def body(dz_in, z_ref, dps_ref, w_ref, s_ref, dz_ref, dw_ref, dsc_ref, ext, ext2):
        del dz_in
        ext[pl.ds(0, POOL_HALO), :] = jnp.zeros((POOL_HALO, POOL_GD), F32)
        ext2[pl.ds(SEQ, POOL_HALO), :] = jnp.zeros((POOL_HALO, POOL_GD), F32)
        for g, w in enumerate(POOL_WINDOWS):
            cols = slice(g * POOL_GD, (g + 1) * POOL_GD)
            p = _pool_window(z_ref[:, cols].astype(F32), w, ext)
            wg = w_ref[g]
            pg = _dot(p, wg)
            dpsv = dps_ref[:, cols]
            dsc_ref[:, cols] = jnp.sum(dpsv * pg, axis=0, keepdims=True)
            dpg = dpsv * s_ref[:, cols]
            dw_ref[g] = _dot(p, dpg, ta=True)
            dp = _dot(dpg, wg, tb=True)
            dpc = dp / _pool_counts(w)
            ext2[pl.ds(0, SEQ), :] = dpc
            du = dpc
            for j in range(1, w):
                du = du + ext2[pl.ds(j, SEQ), :]
            dz_ref[:, cols] = (du - dp).astype(BF)

    return pl.pallas_call(
        body, name="pool_bwd", grid=(1,),
        in_specs=[pl.BlockSpec(memory_space=pl.ANY),
                  pl.BlockSpec((SEQ, POOL_WIDTH), lambda i: (0, C_POOL // POOL_WIDTH)),
                  _const_spec((SEQ, POOL_WIDTH)), _const_spec((4, POOL_GD, POOL_GD)), _const_spec((1, POOL_WIDTH))],
        out_specs=[pl.BlockSpec((SEQ, POOL_WIDTH), lambda i: (0, C_POOL // POOL_WIDTH)),
                   _const_spec((4, POOL_GD, POOL_GD)), _const_spec((1, POOL_WIDTH))],
        out_shape=[_out_hbm((SEQ, N_DZ), BF), _out_hbm((4, POOL_GD, POOL_GD), F32),
                   _out_hbm((1, POOL_WIDTH), F32)],
        scratch_shapes=[pltpu.VMEM((POOL_HALO + SEQ, POOL_GD), F32), pltpu.VMEM((SEQ + POOL_HALO, POOL_GD), F32)],
        input_output_aliases={0: 0}, compiler_params=_params("arbitrary"),
    )(*map(_in_hbm, (dzcat, zcat, dps, w_grp, scale)))


GK_TILE = 512


GK_ROWS = pl.BlockSpec((GK_PAD, D_MODEL), lambda i: (C_GK // GK_PAD, 0))


def _gk_fwd(h, wt_cat, wgk_pad, b_gk):
    def body(h_ref, wt_ref, w_ref, b_ref, la_ref):
        z_gk = _dot(h_ref[...], wt_ref[...], tb=True)
        pre = _dot(z_gk, w_ref[...]) + b_ref[...]
        la_ref[...] = (jnp.minimum(pre, 0.0) - jnp.log(1.0 + jnp.exp(-jnp.abs(pre)))) * (1.0 / GATE_NORM)

    return pl.pallas_call(
        body, name="gk_fwd", grid=(SEQ // GK_TILE,),
        in_specs=[pl.BlockSpec((GK_TILE, D_MODEL), lambda i: (i, 0)), GK_ROWS,
                  _const_spec((GK_PAD, GLA_DK)), _const_spec((1, GLA_DK))],
        out_specs=pl.BlockSpec((GK_TILE, GLA_DK), lambda i: (i, 0)),
        out_shape=_out_hbm((SEQ, GLA_DK), F32), compiler_params=_params("parallel"),
    )(*map(_in_hbm, (h, wt_cat, wgk_pad, b_gk)))


def _gk_bwd(dzcat, dla, h, wt_cat, wgk_pad, b_gk):
    def body(dz_in, dla_ref, h_ref, wt_ref, w_ref, b_ref, dz_ref, dw_ref, db_ref):
        del dz_in
        wv = w_ref[...]
        z_gk = _dot(h_ref[...], wt_ref[...], tb=True)
        pre = _dot(z_gk, wv) + b_ref[...]
        dpre = dla_ref[...] * (1.0 / GATE_NORM) * (1.0 - _sigmoid(pre))
        dz_ref[...] = _dot(dpre, wv, tb=True).astype(BF)
        dwp = _dot(z_gk, dpre, ta=True)[:GATE_RANK]
        dbp = jnp.sum(dpre, axis=0, keepdims=True)

        @pl.when(pl.program_id(0) == 0)
        def _():
            dw_ref[...] = dwp
            db_ref[...] = dbp

        @pl.when(pl.program_id(0) > 0)
        def _():
            dw_ref[...] += dwp
            db_ref[...] += dbp

    return pl.pallas_call(
        body, name="gk_bwd", grid=(SEQ // GK_TILE,),
        in_specs=[pl.BlockSpec(memory_space=pl.ANY), pl.BlockSpec((GK_TILE, GLA_DK), lambda i: (i, 0)),
                  pl.BlockSpec((GK_TILE, D_MODEL), lambda i: (i, 0)), GK_ROWS, _const_spec((GK_PAD, GLA_DK)),
                  _const_spec((1, GLA_DK))],
        out_specs=[pl.BlockSpec((GK_TILE, GK_PAD), lambda i: (i, C_GK // GK_PAD)), _const_spec((GATE_RANK, GLA_DK)),
                   _const_spec((1, GLA_DK))],
        out_shape=[_out_hbm((SEQ, N_DZ), BF), _out_hbm((GATE_RANK, GLA_DK), F32),
                   _out_hbm((1, GLA_DK), F32)],
        input_output_aliases={0: 0}, compiler_params=_params("arbitrary"),
    )(*map(_in_hbm, (dzcat, dla, h, wt_cat, wgk_pad, b_gk)))


GLA_ROWS = GLA_CPS * CHUNK
GLA_STEPS = SEQ // GLA_ROWS
QKV_W = 2048


def _tri():
    return lax.broadcasted_iota(jnp.int32, (CHUNK, CHUNK), 0) >= lax.broadcasted_iota(jnp.int32, (CHUNK, CHUNK), 1)


def _chunk_cumsum(la_ref, rows):
    return _dot_exact(_tri().astype(F32), la_ref[rows, :])


def _gla_chunk(qkv_ref, la_ref, rows, h, bc_all):
    tri = _tri()
    q = qkv_ref[rows, h * HK:(h + 1) * HK].astype(F32) * (HK ** -0.5)
    k = qkv_ref[rows, GLA_DK + h * HK:GLA_DK + (h + 1) * HK].astype(F32)
    v = qkv_ref[rows, 2 * GLA_DK + h * HV:2 * GLA_DK + (h + 1) * HV].astype(BF)
    la = la_ref[rows, h * HK:(h + 1) * HK]
    bc = bc_all[:, h * HK:(h + 1) * HK]
    e_pos, e_neg = jnp.exp(bc), jnp.exp(-bc)
    dl = jnp.exp(jnp.sum(la, axis=0, keepdims=True))
    q_fw, q_bw, k_fw, k_bw = q * e_pos, q * e_neg, k * e_neg, k * e_pos
    scores = jnp.where(tri, _dot(q_fw, k_fw, tb=True), _dot(q_bw, k_bw, tb=True))
    return tri, v, e_pos, e_neg, dl, q_fw, q_bw, k_fw, k_bw, scores


def _gla_fwd(zcat, la, after):
    def body(qkv_ref, la_ref, after_ref, o_ref, st_ref, state):
        del after_ref

        @pl.when(pl.program_id(0) == 0)
        def _():
            state[...] = jnp.zeros_like(state)

        for c in range(GLA_CPS):
            rows = slice(c * CHUNK, (c + 1) * CHUNK)
            bc_all = _chunk_cumsum(la_ref, rows)
            for h in range(HEADS):
                _, v, _, _, dl, q_fw, _, k_fw, _, scores = _gla_chunk(qkv_ref, la_ref, rows, h, bc_all)
                st = state[h]
                st_ref[c, h] = st
                o_ref[rows, h * HV:(h + 1) * HV] = _dot(scores, v) + _dot(q_fw, st, tb=True)
                state[h] = st * dl + _dot(v, k_fw * dl, ta=True)

    return pl.pallas_call(
        body, name="gla_fwd", grid=(GLA_STEPS,),
        in_specs=[pl.BlockSpec((GLA_ROWS, QKV_W), lambda i: (i, 0)), pl.BlockSpec((GLA_ROWS, GLA_DK), lambda i: (i, 0)),
                  pl.BlockSpec(memory_space=pl.ANY)],
        out_specs=[pl.BlockSpec((GLA_ROWS, D_MODEL), lambda i: (i, 0)),
                   pl.BlockSpec((GLA_CPS, HEADS, HV, HK), lambda i: (i, 0, 0, 0))],
        out_shape=[_out_hbm((SEQ, D_MODEL), F32),
                   _out_hbm((SEQ // CHUNK, HEADS, HV, HK), F32)],
        scratch_shapes=[pltpu.VMEM((HEADS, HV, HK), F32)], compiler_params=_params("arbitrary"),
    )(*map(_in_hbm, (zcat, la)), after)


def _gla_bwd(dzcat, zcat, la, d_o, states):
    def body(dz_in, qkv_ref, la_ref, do_ref, st_ref, dqkv_ref, dla_ref, dstate):
        del dz_in

        @pl.when(pl.program_id(0) == 0)
        def _():
            dstate[...] = jnp.zeros_like(dstate)

        last_row = lax.broadcasted_iota(jnp.int32, (CHUNK, HK), 0) == CHUNK - 1
        upper = (lax.broadcasted_iota(jnp.int32, (CHUNK, CHUNK), 0)
                 <= lax.broadcasted_iota(jnp.int32, (CHUNK, CHUNK), 1)).astype(F32)
        for c in reversed(range(GLA_CPS)):
            rows = slice(c * CHUNK, (c + 1) * CHUNK)
            bc_all = _chunk_cumsum(la_ref, rows)
            dbs = []
            for h in range(HEADS):
                tri, v, e_pos, e_neg, dl, q_fw, q_bw, k_fw, k_bw, scores = _gla_chunk(qkv_ref, la_ref, rows, h, bc_all)
                st = st_ref[c, h]
                dst = dstate[h]
                d_out = do_ref[rows, h * HV:(h + 1) * HV].astype(BF)
                k_dec = k_fw * dl
                dp = _dot(d_out, v, tb=True)
                dp_fw = jnp.where(tri, dp, 0.0)
                dp_bw = jnp.where(tri, 0.0, dp)
                dv = _dot(scores, d_out, ta=True) + _dot(k_dec, dst, tb=True)
                dk_dec = _dot(v, dst)
                dq_fw = _dot(dp_fw, k_fw) + _dot(d_out, st)
                dk_fw = _dot(dp_fw, q_fw, ta=True) + dk_dec * dl
                dq_bw = _dot(dp_bw, k_bw)
                dk_bw = _dot(dp_bw, q_bw, ta=True)
                ddl = jnp.sum(st * dst, axis=0, keepdims=True) + jnp.sum(k_fw * dk_dec, axis=0, keepdims=True)
                dstate[h] = dst * dl + _dot(d_out, q_fw, ta=True)
                dq = (dq_fw * e_pos + dq_bw * e_neg) * (HK ** -0.5)
                dk = dk_fw * e_neg + dk_bw * e_pos
                dbs.append(dq_fw * q_fw - dk_fw * k_fw - dq_bw * q_bw + dk_bw * k_bw + jnp.where(last_row, ddl * dl, 0.0))
                dqkv_ref[rows, h * HK:(h + 1) * HK] = dq.astype(BF)
                dqkv_ref[rows, GLA_DK + h * HK:GLA_DK + (h + 1) * HK] = dk.astype(BF)
                dqkv_ref[rows, 2 * GLA_DK + h * HV:2 * GLA_DK + (h + 1) * HV] = dv.astype(BF)
            dla_ref[rows, :] = _dot_exact(upper, jnp.concatenate(dbs, axis=1))

    rev = lambda i: (GLA_STEPS - 1 - i, 0)
    return pl.pallas_call(
        body, name="gla_bwd", grid=(GLA_STEPS,),
        in_specs=[pl.BlockSpec(memory_space=pl.ANY), pl.BlockSpec((GLA_ROWS, QKV_W), rev),
                  pl.BlockSpec((GLA_ROWS, GLA_DK), rev), pl.BlockSpec((GLA_ROWS, D_MODEL), rev),
                  pl.BlockSpec((GLA_CPS, HEADS, HV, HK), lambda i: (GLA_STEPS - 1 - i, 0, 0, 0))],
        out_specs=[pl.BlockSpec((GLA_ROWS, QKV_W), rev), pl.BlockSpec((GLA_ROWS, GLA_DK), rev)],
        out_shape=[_out_hbm((SEQ, N_DZ), BF), _out_hbm((SEQ, GLA_DK), F32)],
        scratch_shapes=[pltpu.VMEM((HEADS, HV, HK), F32)], input_output_aliases={0: 0},
        compiler_params=_params("arbitrary"),
    )(*map(_in_hbm, (dzcat, zcat, la, d_o, states)))


def _silu_parts(x):
    s = _sigmoid(x)
    return x * s, s * (1.0 + x * (1.0 - s))


def _post_gla_fwd(o, zcat, g_head):
    def body(o_ref, zog_ref, g_ref, out_ref):
        for h in range(HEADS):
            cols = slice(h * HV, (h + 1) * HV)
            ov = o_ref[:, cols]
            r = lax.rsqrt(jnp.mean(ov * ov, axis=-1, keepdims=True) + EPS)
            act, _ = _silu_parts(zog_ref[:, cols].astype(F32))
            out_ref[:, cols] = (ov * r * g_ref[...] * act).astype(BF)

    tile = pl.BlockSpec((TOK_TILE, D_MODEL), lambda i: (i, 0))
    return pl.pallas_call(
        body, name="post_gla_fwd", grid=(SEQ // TOK_TILE,),
        in_specs=[tile, pl.BlockSpec((TOK_TILE, D_MODEL), lambda i: (i, C_OG // D_MODEL)), _const_spec((1, HV))],
        out_specs=tile, out_shape=_out_hbm((SEQ, D_MODEL), BF), compiler_params=_params("parallel"),
    )(*map(_in_hbm, (o, zcat, g_head)))


def _post_gla_bwd(dzcat, dy_gla, w_gla_proj, o, zcat, g_head, after):
    def body(dz_in, dyg_ref, w_ref, o_ref, zog_ref, g_ref, after_ref, dz_ref, do_ref, dg_ref):
        del dz_in, after_ref
        dog = _dot(dyg_ref[...], w_ref[...], tb=True)
        gpart = jnp.zeros((1, HV), F32)
        gv = g_ref[...]
        for h in range(HEADS):
            cols = slice(h * HV, (h + 1) * HV)
            ov = o_ref[:, cols]
            r = lax.rsqrt(jnp.mean(ov * ov, axis=-1, keepdims=True) + EPS)
            on = ov * r
            act, dact = _silu_parts(zog_ref[:, cols].astype(F32))
            dogv = dog[:, cols]
            dz_ref[:, cols] = (dogv * on * gv * dact).astype(BF)
            d_on_g = dogv * act
            gpart = gpart + jnp.sum(d_on_g * on, axis=0, keepdims=True)
            dxn = d_on_g * gv
            do_ref[:, cols] = (r * (dxn - on * jnp.mean(dxn * on, axis=-1, keepdims=True))).astype(BF)

        @pl.when(pl.program_id(0) == 0)
        def _():
            dg_ref[...] = gpart

        @pl.when(pl.program_id(0) > 0)
        def _():
            dg_ref[...] += gpart

    tile = pl.BlockSpec((TOK_TILE, D_MODEL), lambda i: (i, 0))
    ogspec = pl.BlockSpec((TOK_TILE, D_MODEL), lambda i: (i, C_OG // D_MODEL))
    return pl.pallas_call(
        body, name="post_gla_bwd", grid=(SEQ // TOK_TILE,),
        in_specs=[pl.BlockSpec(memory_space=pl.ANY), tile, _const_spec((D_MODEL, D_MODEL)), tile, ogspec,
                  _const_spec((1, HV)), pl.BlockSpec(memory_space=pl.ANY)],
        out_specs=[ogspec, tile, _const_spec((1, HV))],
        out_shape=[_out_hbm((SEQ, N_DZ), BF), _out_hbm((SEQ, D_MODEL), BF),
                   _out_hbm((1, HV), F32)],
        input_output_aliases={0: 0}, compiler_params=_params("arbitrary"),
    )(*map(_in_hbm, (dzcat, dy_gla, w_gla_proj, o, zcat, g_head)), after)


GATE_W = 2 * D_MODEL


def _mix_out_fwd(ps, og, zcat, x, w_pool_proj, w_gla_proj, w_out, b_gate, g_ffn, after):
    def body(ps_ref, og_ref, zg_ref, x_ref, wpp_ref, wgp_ref, wout_ref, b_ref, g_ref, after_ref,
             yp_ref, yg_ref, mixed_ref, x1_ref, h2_ref):
        del after_ref
        y_pool = _dot(ps_ref[...], wpp_ref[...])
        y_gla = _dot(og_ref[...], wgp_ref[...])
        yp_ref[...] = y_pool.astype(BF)
        yg_ref[...] = y_gla.astype(BF)
        g0 = _sigmoid(zg_ref[:, :D_MODEL].astype(F32) + b_ref[:, :D_MODEL])
        g1 = _sigmoid(zg_ref[:, D_MODEL:].astype(F32) + b_ref[:, D_MODEL:])
        mixed = (g0 * y_pool + g1 * y_gla).astype(BF)
        mixed_ref[...] = mixed
        x1 = x_ref[...] + _dot(mixed, wout_ref[...])
        x1_ref[...] = x1
        r = lax.rsqrt(jnp.mean(x1 * x1, axis=-1, keepdims=True) + EPS)
        h2_ref[...] = (x1 * r * g_ref[...]).astype(BF)

    tile = pl.BlockSpec((TOK_TILE, D_MODEL), lambda i: (i, 0))
    resident = lambda shape: pl.BlockSpec(shape, lambda i: (0, 0), pipeline_mode=pl.Buffered(1))
    f32, bf16 = _out_hbm((SEQ, D_MODEL), F32), _out_hbm((SEQ, D_MODEL), BF)
    return pl.pallas_call(
        body, name="mix_out_fwd", grid=(SEQ // TOK_TILE,),
        in_specs=[pl.BlockSpec((TOK_TILE, POOL_WIDTH), lambda i: (i, 0)), tile,
                  pl.BlockSpec((TOK_TILE, GATE_W), lambda i: (i, C_GATE // GATE_W)), tile,
                  resident((POOL_WIDTH, D_MODEL)), resident((D_MODEL, D_MODEL)), resident((D_MODEL, D_MODEL)),
                  _const_spec((1, GATE_W)), _const_spec((1, D_MODEL)), pl.BlockSpec(memory_space=pl.ANY)],
        out_specs=[tile] * 5, out_shape=[bf16, bf16, bf16, f32, bf16], compiler_params=_params("parallel"),
    )(*map(_in_hbm, (ps, og, zcat, x, w_pool_proj, w_gla_proj, w_out, b_gate, g_ffn)), after)


def _mix_bwd(dx1, w_out, zcat, b_gate, y_pool, y_gla):
    def body(dx_ref, w_ref, zg_ref, b_ref, yp_ref, yg_ref, dz_ref, dyp_ref, dyg_ref, db_ref):
        dm = _dot(dx_ref[...], w_ref[...], tb=True)
        g0 = _sigmoid(zg_ref[:, :D_MODEL].astype(F32) + b_ref[:, :D_MODEL])
        g1 = _sigmoid(zg_ref[:, D_MODEL:].astype(F32) + b_ref[:, D_MODEL:])
        dyp_ref[...] = (dm * g0).astype(BF)
        dyg_ref[...] = (dm * g1).astype(BF)
        dz0 = dm * yp_ref[...].astype(F32) * g0 * (1.0 - g0)
        dz1 = dm * yg_ref[...].astype(F32) * g1 * (1.0 - g1)
        dz_ref[:, :D_MODEL] = dz0.astype(BF)
        dz_ref[:, D_MODEL:] = dz1.astype(BF)
        b0 = jnp.sum(dz0, axis=0, keepdims=True)
        b1 = jnp.sum(dz1, axis=0, keepdims=True)

        @pl.when(pl.program_id(0) == 0)
        def _():
            db_ref[:, :D_MODEL] = b0
            db_ref[:, D_MODEL:] = b1

        @pl.when(pl.program_id(0) > 0)
        def _():
            db_ref[:, :D_MODEL] += b0
            db_ref[:, D_MODEL:] += b1

    tile = pl.BlockSpec((TOK_TILE, D_MODEL), lambda i: (i, 0))
    gspec = pl.BlockSpec((TOK_TILE, GATE_W), lambda i: (i, C_GATE // GATE_W))
    return pl.pallas_call(
        body, name="mix_bwd", grid=(SEQ // TOK_TILE,),
        in_specs=[tile, _const_spec((D_MODEL, D_MODEL)), gspec, _const_spec((1, GATE_W)), tile, tile],
        out_specs=[gspec, tile, tile, _const_spec((1, GATE_W))],
        out_shape=[_out_hbm((SEQ, N_DZ), BF), _out_hbm((SEQ, D_MODEL), BF),
                   _out_hbm((SEQ, D_MODEL), BF), _out_hbm((1, GATE_W), F32)],
        compiler_params=_params("arbitrary"),
    )(*map(_in_hbm, (dx1, w_out, zcat, b_gate, y_pool, y_gla)))


N_TOK_TILES = SEQ // TOK_TILE
HALO_PER_TILE = TOK_TILE // HALO


LANE_TILES = tuple((lo, min(128, FF_BLK - lo)) for lo in range(0, FF_BLK, 128))


def _taps(w_ref, b_ref, half, lanes, rows):
    shape = (rows, lanes.stop - lanes.start)
    return ([jnp.broadcast_to(w_ref[half, j:j + 1, lanes], shape) for j in range(3)],
            jnp.broadcast_to(b_ref[half, :, lanes], shape))


def _conv_strips(u_ref, ub_ref, ua_ref, taps, lanes, width, n_strips, first):
    row = lax.broadcasted_iota(jnp.int32, (HALO, width), 0)
    prev = [[pltpu.roll(jnp.where(first, 0.0, ub_ref[half, :, lanes]), k, 0) for k in (1, 2)] for half in range(2)]
    for s in range(n_strips + (ua_ref is not None)):
        u3, conv = [], []
        for half in range(2):
            cur = u_ref[half, s * HALO:(s + 1) * HALO, lanes] if s < n_strips else ua_ref[half, :, lanes]
            rolled = [pltpu.roll(cur, k, 0) for k in (1, 2)]
            frames = [jnp.where(row >= 2, rolled[1], prev[half][1]), jnp.where(row >= 1, rolled[0], prev[half][0]), cur]
            prev[half] = rolled
            w3, bias = taps[half]
            u3.append(frames)
            conv.append(bias + frames[0] * w3[0] + frames[1] * w3[1] + frames[2] * w3[2])
        yield s, u3, conv


def _pair_specs(pairs):
    tile = pl.BlockSpec((pairs, None, TOK_TILE, FF_BLK), lambda b, i: (0, b, i, 0))
    before = pl.BlockSpec((pairs, None, HALO, FF_BLK), lambda b, i: (0, b, jnp.maximum(i * HALO_PER_TILE - 1, 0), 0))
    after = pl.BlockSpec((pairs, None, HALO, FF_BLK),
                         lambda b, i: (0, b, jnp.minimum((i + 1) * HALO_PER_TILE, SEQ // HALO - 1), 0))

    def vec(rows):
        return pl.BlockSpec((2, None, rows, FF_BLK), lambda b, i: (0, b, 0, 0))

    return tile, before, after, vec


N_STRIPS = TOK_TILE // HALO


def _up_conv_fwd(h2, wt_up, w_conv, b_conv):
    steps = N_TOK_TILES // 2

    def body(h_ref, h_next, wg_ref, wv_ref, w_ref, b_ref, u_ref, a_ref, buf_a, buf_b, carry):
        j = pl.program_id(1)

        def project(hv, buf):
            buf[0] = _dot(hv, wg_ref[...], tb=True)
            buf[1] = _dot(hv, wv_ref[...], tb=True)

        def conv(buf, row0):
            u_ref[:, row0:row0 + TOK_TILE, :] = buf[...]
            for lo, width in LANE_TILES:
                lanes = slice(lo, lo + width)
                taps = [_taps(w_ref, b_ref, half, lanes, HALO) for half in range(2)]
                pending = None
                for s, _, (cg, cv) in _conv_strips(buf, carry, None, taps, lanes, width, N_STRIPS, False):
                    act = cg * _sigmoid(cg) * cv
                    if s % 2 == 0:
                        pending = act
                    else:
                        a_ref[0, row0 + (s - 1) * HALO:row0 + (s + 1) * HALO, lanes] = (
                            jnp.concatenate([pending, act], axis=0).astype(BF))
            carry[...] = buf[:, TOK_TILE - HALO:, :]

        @pl.when(j == 0)
        def _():
            project(h_ref[0:TOK_TILE, :], buf_a)
            carry[...] = jnp.zeros_like(carry)

        project(h_ref[TOK_TILE:, :], buf_b)
        conv(buf_a, 0)
        project(h_next[...], buf_a)
        conv(buf_b, TOK_TILE)

    w_blk = lambda half: pl.BlockSpec((FF_BLK, D_MODEL), lambda b, j: (b + 4 * half, 0))
    vec = lambda rows: pl.BlockSpec((2, None, rows, FF_BLK), lambda b, j: (0, b, 0, 0))
    u_buf = pltpu.VMEM((2, TOK_TILE, FF_BLK), F32)
    return pl.pallas_call(
        body, name="up_conv_fwd", grid=(4, steps),
        in_specs=[pl.BlockSpec((2 * TOK_TILE, D_MODEL), lambda b, j: (j, 0)),
                  pl.BlockSpec((TOK_TILE, D_MODEL), lambda b, j: (jnp.minimum(2 * j + 2, N_TOK_TILES - 1), 0)),
                  w_blk(0), w_blk(1), vec(3), vec(1)],
        out_specs=[pl.BlockSpec((2, None, 2 * TOK_TILE, FF_BLK), lambda b, j: (0, b, j, 0)),
                   pl.BlockSpec((1, None, 2 * TOK_TILE, FF_BLK), lambda b, j: (0, b, j, 0))],
        out_shape=[_out_hbm((2, 4, SEQ, FF_BLK), F32), _out_hbm((1, 4, SEQ, FF_BLK), BF)],
        scratch_shapes=[u_buf, u_buf, pltpu.VMEM((2, HALO, FF_BLK), F32)],
        compiler_params=_params("parallel", "arbitrary"),
    )(*map(_in_hbm, (h2, h2, wt_up, wt_up, w_conv, b_conv)))


def _conv_bwd(u, da, w_conv, b_conv):
    def body(u_ref, ub_ref, ua_ref, da_ref, daa_ref, w_ref, b_ref, du_ref, dw_ref, db_ref):
        i = pl.program_id(1)

        @pl.when(i == 0)
        def _():
            dw_ref[...] = jnp.zeros_like(dw_ref)
            db_ref[...] = jnp.zeros_like(db_ref)

        for lo, width in LANE_TILES:
            lanes = slice(lo, lo + width)
            row = lax.broadcasted_iota(jnp.int32, (HALO, width), 0)
            taps = [_taps(w_ref, b_ref, half, lanes, HALO) for half in range(2)]
            acc_w = [[jnp.zeros((HALO, width), F32) for _ in range(3)] for _ in range(2)]
            acc_b = [jnp.zeros((HALO, width), F32) for _ in range(2)]
            da_pair, pending = None, [None, None]
            dc_prev, up_prev = [None, None], [None, None]
            for s, u3, (cg, cv) in _conv_strips(u_ref, ub_ref, ua_ref, taps, lanes, width, N_STRIPS, i == 0):
                act, dact = _silu_parts(cg)
                if s == N_STRIPS:
                    da = jnp.where(i < N_TOK_TILES - 1, daa_ref[0, :, lanes].astype(F32), 0.0)
                elif s % 2 == 0:
                    da_pair = da_ref[0, s * HALO:(s + 2) * HALO, lanes].astype(F32)
                    da = da_pair[:HALO]
                else:
                    da = da_pair[HALO:]
                dc = (da * cv * dact, da * act)
                for half in range(2):
                    up = [pltpu.roll(dc[half], HALO - k, 0) for k in (1, 2)]
                    if s < N_STRIPS:
                        for j in range(3):
                            acc_w[half][j] = acc_w[half][j] + dc[half] * u3[half][j]
                        acc_b[half] = acc_b[half] + dc[half]
                    if s >= 1:
                        w3 = taps[half][0]
                        du = (dc_prev[half] * w3[2] + jnp.where(row < HALO - 1, up_prev[half][0], up[0]) * w3[1]
                              + jnp.where(row < HALO - 2, up_prev[half][1], up[1]) * w3[0])
                        if (s - 1) % 2 == 0:
                            pending[half] = du
                        else:
                            du_ref[half, (s - 2) * HALO:s * HALO, lanes] = jnp.concatenate([pending[half], du],
                                                                                           axis=0).astype(BF)
                    dc_prev[half], up_prev[half] = dc[half], up
            for half in range(2):
                for j in range(3):
                    dw_ref[half, j:j + 1, lanes] += jnp.sum(acc_w[half][j], axis=0, keepdims=True)
                db_ref[half, :, lanes] += jnp.sum(acc_b[half], axis=0, keepdims=True)

    tile, before, after, vec = _pair_specs(2)
    da_tile, _, da_after_spec, _ = _pair_specs(1)
    return pl.pallas_call(
        body, name="conv_bwd", grid=(4, N_TOK_TILES),
        in_specs=[tile, before, after, da_tile, da_after_spec, vec(3), vec(1)],
        out_specs=[tile, vec(3), vec(1)],
        out_shape=[_out_hbm((2, 4, SEQ, FF_BLK), BF), _out_hbm((2, 4, 3, FF_BLK), F32),
                   _out_hbm((2, 4, 1, FF_BLK), F32)],
        compiler_params=_params("parallel", "arbitrary"),
    )(*map(_in_hbm, (u, u, u, da, da, w_conv, b_conv)))


W_IN_SEGMENTS = ((R_POOL, POOL_WIDTH, C_POOL), (R_QKV, QKV_W, C_QKV), (R_OG, D_MODEL, C_OG), (R_GK, GATE_RANK, C_GK),
                 (R_GATE, GATE_W, C_GATE))


def _slab_pieces(d):
    lo, hi = d * IN_SHARD, (d + 1) * IN_SHARD
    pieces = []
    for start, n, at in W_IN_SEGMENTS:
        a, b = max(lo, start), min(hi, start + n)
        if a < b:
            assert (a - lo) % 2 == 0 and (b - a) % 2 == 0 and (at + a - start) % 2 == 0
            pieces.append(((a - lo) // 2, (b - a) // 2, (at + a - start) // 2))
    return pieces


def _unshard_w_in(slabs):
    def body(slab_ref, cat_ref):
        d = pl.program_id(0)
        src = slab_ref.bitcast(jnp.uint32)
        dst = cat_ref.bitcast(jnp.uint32)

        @pl.when(d == 0)
        def _():
            cat_ref[C_GK:, :] = jnp.zeros((GK_PAD, D_MODEL), BF)

        for dd in range(N_DEV):
            @pl.when(d == dd)
            def _():
                for a, n, at in _slab_pieces(dd):
                    dst[pl.ds(at, n), :] = src[0, pl.ds(a, n), :]

    return pl.pallas_call(
        body, name="unshard_w_in", grid=(N_DEV,),
        in_specs=[pl.BlockSpec((1, IN_SHARD, D_MODEL), lambda d: (d, 0, 0))], out_specs=_const_spec((N_DZ, D_MODEL)),
        out_shape=_out_hbm((N_DZ, D_MODEL), BF), compiler_params=_params("arbitrary"),
    )(_in_hbm(slabs))


def _shard_d_w_in(d_cat):
    def body(cat_ref, slab_ref):
        d = pl.program_id(0)
        cat = cat_ref.bitcast(jnp.uint32)
        dst = slab_ref.bitcast(jnp.uint32)
        for dd in range(N_DEV):
            @pl.when(d == dd)
            def _():
                for a, n, at in _slab_pieces(dd):
                    dst[0, pl.ds(a, n), :] = cat[pl.ds(at, n), :]

    return pl.pallas_call(
        body, name="shard_d_w_in", grid=(N_DEV,), in_specs=[_const_spec((N_DZ, D_MODEL))],
        out_specs=pl.BlockSpec((1, IN_SHARD, D_MODEL), lambda d: (d, 0, 0)),
        out_shape=_out_hbm((N_DEV, IN_SHARD, D_MODEL), BF), compiler_params=_params("parallel"),
    )(_in_hbm(d_cat))


ANY = pl.BlockSpec(memory_space=pl.ANY)


def _place():
    x, y, c = lax.axis_index("x"), lax.axis_index("y"), lax.axis_index("c")
    other_chips = [(1 - x, y), (x, 1 - y), (1 - x, 1 - y)]
    return x, y, c, other_chips


SEM = pl.BlockSpec(memory_space=pltpu.SEMAPHORE)
IN_HBM = pl.BlockSpec(memory_space=pltpu.HBM)
SPLIT_PARAMS = pltpu.CompilerParams(has_side_effects=pltpu.SideEffectType.DATAFLOW_SIDE_EFFECTING)


def _gather_first(refs, send_sems, recv_sems):
    x, y, c, chips = _place()
    targets = [(x, y, 1 - c)] + [(px, py, c) for px, py in chips]
    return [pltpu.make_async_remote_copy(src_ref=refs[2 * a], dst_ref=refs[2 * a + 1].at[4 * x + 2 * y + c],
                                         send_sem=send_sems.at[4 * a + k], recv_sem=recv_sems.at[4 * a + k],
                                         device_id=to, device_id_type=MESH)
            for a in range(len(refs) // 2) for k, to in enumerate(targets)]


def _gather_direct(refs, send_sems, recv_sems):
    x, y, c, _ = _place()
    flips = [(dx, dy, dc) for dx in (0, 1) for dy in (0, 1) for dc in (0, 1) if dx + dy + dc]
    targets = [(1 - x if dx else x, 1 - y if dy else y, 1 - c if dc else c) for dx, dy, dc in flips]
    return [pltpu.make_async_remote_copy(src_ref=refs[2 * a], dst_ref=refs[2 * a + 1].at[4 * x + 2 * y + c],
                                         send_sem=send_sems.at[7 * a + k], recv_sem=recv_sems.at[7 * a + k],
                                         device_id=to, device_id_type=MESH)
            for a in range(len(refs) // 2) for k, to in enumerate(targets)]


def _gather_second(refs, send_sems, recv_sems):
    x, y, c, chips = _place()
    copies = []
    for a, land in enumerate(refs):
        for j, (px, py) in enumerate(chips):
            block = land.at[4 * px + 2 * py + c]
            copies.append(pltpu.make_async_remote_copy(src_ref=block, dst_ref=block, send_sem=send_sems.at[3 * a + j],
                                                       recv_sem=recv_sems.at[3 * a + j], device_id=(x, y, 1 - c),
                                                       device_id_type=MESH))
    return copies


def _reduce_first(refs, send_sems, recv_sems):
    x, y, c, _ = _place()
    return [pltpu.make_async_remote_copy(src_ref=refs[2 * a].at[j, 1 - c], dst_ref=refs[2 * a + 1].at[j],
                                         send_sem=send_sems.at[4 * a + j], recv_sem=recv_sems.at[4 * a + j],
                                         device_id=(x, y, 1 - c), device_id_type=MESH)
            for a in range(len(refs) // 2) for j in range(4)]


def _reduce_second(refs, send_sems, recv_sems):
    _, _, c, chips = _place()
    return [pltpu.make_async_remote_copy(src_ref=refs[2 * a].at[2 * px + py], dst_ref=refs[2 * a + 1].at[k],
                                         send_sem=send_sems.at[3 * a + k], recv_sem=recv_sems.at[3 * a + k],
                                         device_id=(px, py, c), device_id_type=MESH)
            for a in range(len(refs) // 2) for k, (px, py) in enumerate(chips)]


def _split_start(name, groups):
    arrays = [a for g in groups for a in g[0]]
    n = len(arrays)

    def body(*refs):
        sems = refs[n:n + 2 * len(groups)]
        at = 0
        for gi, (members, _, build) in enumerate(groups):
            for cp in build(refs[at:at + len(members)], sems[2 * gi], sems[2 * gi + 1]):
                cp.start()
            at += len(members)
        refs[-1][...] = jnp.zeros_like(refs[-1])

    sem_shapes = [pltpu.SemaphoreType.DMA((g[1],)) for g in groups for _ in range(2)]
    outs = pl.pallas_call(
        body, name=name, in_specs=[IN_HBM] * n,
        out_shape=(*sem_shapes, *[_out_hbm(a.shape, a.dtype) for a in arrays], jax.ShapeDtypeStruct((8, 128), F32)),
        out_specs=(*[SEM] * len(sem_shapes), *[IN_HBM] * n, pl.BlockSpec(memory_space=pltpu.VMEM)),
        input_output_aliases={i: len(sem_shapes) + i for i in range(n)}, compiler_params=SPLIT_PARAMS,
    )(*[pltpu.with_memory_space_constraint(a, pltpu.HBM) for a in arrays])
    per_group, at = [], len(sem_shapes)
    for gi, (members, _, _) in enumerate(groups):
        per_group.append((outs[2 * gi], outs[2 * gi + 1], list(outs[at:at + len(members)])))
        at += len(members)
    return per_group, outs[-1]


def _split_wait(name, started, build, after):
    send_sems, recv_sems, arrays = started
    n = len(arrays)
    after = after if isinstance(after, (tuple, list)) else (after,)

    def body(*refs):
        for cp in build(refs[:n], refs[n], refs[n + 1]):
            cp.wait_send()
            cp.wait_recv()

    return pl.pallas_call(
        body, name=name, in_specs=[IN_HBM] * n + [SEM, SEM] + [ANY] * len(after),
        out_shape=tuple(_out_hbm(a.shape, a.dtype) for a in arrays), out_specs=tuple([IN_HBM] * n),
        input_output_aliases={i: i for i in range(n)}, compiler_params=SPLIT_PARAMS,
    )(*arrays, send_sems, recv_sems, *after)


def _placed_behind(token, arrays, name):
    n = len(arrays)

    def body(*refs):
        refs[-1][...] = jnp.zeros_like(refs[-1])

    outs = pl.pallas_call(
        body, name=name, in_specs=[IN_HBM] * n + [ANY],
        out_shape=(*[_out_hbm(a.shape, a.dtype) for a in arrays], jax.ShapeDtypeStruct((8, 128), F32)),
        out_specs=(*[IN_HBM] * n, pl.BlockSpec(memory_space=pltpu.VMEM)),
        input_output_aliases={i: i for i in range(n)},
    )(*map(_in_hbm, arrays), token)
    return outs[:n], outs[-1]


def _gather_landing(shard, me):
    return lax.dynamic_update_slice(lax.empty((N_DEV,) + shard.shape, shard.dtype), shard[None],
                                    (me,) + (0,) * shard.ndim)


ADAM_LANE_TILE = 256


def _tile_2d(rows, cols):
    for t in (256, 176, 128):
        if rows % t == 0:
            return t, cols
    return rows, ADAM_LANE_TILE


def _pair_sum(part, recv, core, name):
    _, rows, cols = recv.shape
    tr, tc = rows, cols

    def body(c_ref, p_ref, r_ref, o_ref):
        del c_ref
        o_ref[...] = (p_ref[...].astype(F32) + r_ref[...].astype(F32)).astype(BF)

    grid_spec = pltpu.PrefetchScalarGridSpec(
        num_scalar_prefetch=1, grid=(4, rows // tr, cols // tc),
        in_specs=[pl.BlockSpec((None, None, tr, tc), lambda j, i, k, c_ref: (j, c_ref[0], i, k)),
                  pl.BlockSpec((None, tr, tc), lambda j, i, k, c_ref: (j, i, k))],
        out_specs=pl.BlockSpec((None, tr, tc), lambda j, i, k, c_ref: (j, i, k)))
    return pl.pallas_call(
        body, name=name, grid_spec=grid_spec, out_shape=_out_hbm(recv.shape, BF),
        compiler_params=_params("parallel", "parallel", "parallel"),
    )(core, *map(_in_hbm, (part, recv)))


def _adamw(w, g, m, v):
    m = ADAM_B1 * m + (1.0 - ADAM_B1) * g
    v = ADAM_B2 * v + (1.0 - ADAM_B2) * (g * g)
    delta = -ADAM_LR * ((m / ADAM_C1) / (jnp.sqrt(v / ADAM_C2) + ADAM_EPS) + ADAM_WD * w)
    return delta, m, v


def _chip_sum_adamw(sums, recv, w, m, v, chip, name):
    rows, cols = w.shape[0], w.shape[-1]
    lone_rows = w.ndim == 3
    tr, tc = _tile_2d(rows, cols)

    def body(chip_ref, s_ref, r_ref, w_ref, m_ref, v_ref, g_out, d_out, m_out, v_out):
        del chip_ref
        g = s_ref[...].astype(F32)
        for k in range(3):
            g = g + r_ref[k].astype(F32)
        g = g[:, None, :] if lone_rows else g
        g_out[...] = g
        d_out[...], m_out[...], v_out[...] = _adamw(w_ref[...], g, m_ref[...], v_ref[...])

    tile = pl.BlockSpec((tr, tc), lambda i, k, chip_ref: (i, k))
    if lone_rows:
        tile = pl.BlockSpec((tr, 1, tc), lambda i, k, chip_ref: (i, 0, k))
    grid_spec = pltpu.PrefetchScalarGridSpec(
        num_scalar_prefetch=1, grid=(rows // tr, cols // tc),
        in_specs=[pl.BlockSpec((None, tr, tc), lambda i, k, chip_ref: (chip_ref[0], i, k)),
                  pl.BlockSpec((3, tr, tc), lambda i, k, chip_ref: (0, i, k)), tile, tile, tile],
        out_specs=[tile] * 4)
    return pl.pallas_call(
        body, name=name, grid_spec=grid_spec, out_shape=[_out_hbm(w.shape, F32)] * 4,
        compiler_params=_params("parallel", "parallel"),
    )(chip, *map(_in_hbm, (sums, recv, w, m, v)))


def _small_sum_adamw(me, entries, loss_parts):
    def whole(shape, squeeze=0, pick=False):
        blk = (None,) * squeeze + tuple(shape[squeeze:])
        if pick:
            blk = (shape[0], None) + tuple(shape[2:])
            return pl.BlockSpec(blk, lambda i, me_ref: (0, me_ref[0]) + (0,) * (len(shape) - 2))
        return pl.BlockSpec(blk, lambda i, me_ref: (0,) * len(shape))

    in_specs, out_specs, out_shape, args = [], [], [], []
    for parts, w, m, v, sharded in entries:
        lead = w.ndim - (parts.ndim - (2 if sharded else 1))
        in_specs += [whole(parts.shape, pick=sharded)] + [whole(w.shape, squeeze=lead)] * 3
        out_specs += [whole(w.shape, squeeze=lead)] * 4
        out_shape += [_out_hbm(w.shape, F32)] * 4
        args += [parts, w, m, v]
    in_specs.append(whole(loss_parts.shape))
    out_specs.append(whole(loss_parts.shape[1:]))
    out_shape.append(_out_hbm(loss_parts.shape[1:], F32))
    n = len(entries)

    def added(p_ref):
        total = p_ref[0]
        for d in range(1, N_DEV):
            total = total + p_ref[d]
        return total

    def body(me_ref, *refs):
        del me_ref
        ins, outs = refs[:4 * n + 1], refs[4 * n + 1:]
        for e in range(n):
            p_ref, w_ref, m_ref, v_ref = ins[4 * e:4 * e + 4]
            g_out, d_out, m_out, v_out = outs[4 * e:4 * e + 4]
            g = added(p_ref)
            g_out[...] = g
            d_out[...], m_out[...], v_out[...] = _adamw(w_ref[...], g, m_ref[...], v_ref[...])
        outs[4 * n][...] = added(ins[4 * n])

    grid_spec = pltpu.PrefetchScalarGridSpec(num_scalar_prefetch=1, grid=(1,), in_specs=in_specs, out_specs=out_specs)
    outs = pl.pallas_call(body, name="small_sum_adamw", grid_spec=grid_spec, out_shape=out_shape,
                          compiler_params=_params("arbitrary"))(me, *map(_in_hbm, args + [loss_parts]))
    return [outs[4 * e:4 * e + 4] for e in range(n)], outs[4 * n]


MM_TILE = 512
N_MM_TILES = SEQ // MM_TILE
CAT_TILE = 512
N_CAT_TILES = N_CAT // CAT_TILE
DZ_TILE = 640


def kernel(x, g_mix, w_in, b_gate, w_gk_up, b_gk, w_pool_grp, pool_scale, g_gla_head, w_pool_proj, w_gla_proj, w_out, g_ffn, w_up, w_conv, b_conv, w_down, g_final, loss_target, m_g_mix, m_w_in, m_b_gate, m_w_gk_up, m_b_gk, m_w_pool_grp, m_pool_scale, m_g_gla_head, m_w_pool_proj, m_w_gla_proj, m_w_out, m_g_ffn, m_w_up, m_w_conv, m_b_conv, m_w_down, m_g_final, v_g_mix, v_w_in, v_b_gate, v_w_gk_up, v_b_gk, v_w_pool_grp, v_pool_scale, v_g_gla_head, v_w_pool_proj, v_w_gla_proj, v_w_out, v_g_ffn, v_w_up, v_w_conv, v_b_conv, v_w_down, v_g_final):
    xi, yi, ci = lax.axis_index("x"), lax.axis_index("y"), lax.axis_index("c")
    me = 4 * xi + 2 * yi + ci
    core = jnp.reshape(ci, (1,)).astype(jnp.int32)
    chip = jnp.reshape(2 * xi + yi, (1,)).astype(jnp.int32)
    xs, target = x[0], loss_target[0]

    big = dict(w_in=w_in[0].T, w_pool_proj=w_pool_proj[0], w_gla_proj=w_gla_proj[0], w_out=w_out[0], w_up=w_up[0].T,
               w_down=w_down[0])
    moments = dict(w_in=(m_w_in[0].T, v_w_in[0].T), w_pool_proj=(m_w_pool_proj[0], v_w_pool_proj[0]),
                   w_gla_proj=(m_w_gla_proj[0], v_w_gla_proj[0]), w_out=(m_w_out[0], v_w_out[0]),
                   w_up=(m_w_up[0].T, v_w_up[0].T), w_down=(m_w_down[0], v_w_down[0]))
    names = list(big)
    shards = {k: big[k].astype(BF) for k in names}
    shards["w_gk_up"], shards["w_conv"] = w_gk_up[0], w_conv[0]
    gather_groups = (("w_in", "w_gk_up"), ("w_pool_proj", "w_gla_proj", "w_out"), ("w_up", "w_down", "w_conv"))
    started, token = _split_start("gather_start", [
        ([t for k in g for t in (shards[k], _gather_landing(shards[k], me))], 4 * len(g), _gather_first)
        for g in gather_groups])
    lone_rows = lambda t: jnp.transpose(t, (2, 0, 1))
    big["w_in"], moments["w_in"] = lone_rows(w_in), (lone_rows(m_w_in), lone_rows(v_w_in))
    conv_vec = lambda t: t.reshape(2, 4, 1, FF_BLK)
    (bconv4, m_bconv4, v_bconv4, m_w_conv, v_w_conv), token = _placed_behind(
        token, [*map(conv_vec, (b_conv, m_b_conv, v_b_conv)), m_w_conv, v_w_conv], "place_adamw_operands")

    def gather_pass(gi, after):
        lands = list(_split_wait(f"gather_wait_{gi}", started[gi], _gather_first, after)[1::2])
        passed, tkn = _split_start(f"gather_pass_{gi}", [(lands, 3 * len(lands), _gather_second)])
        return passed[0], tkn

    def gather_done(gi, passed, after):
        return dict(zip(gather_groups[gi], _split_wait(f"gather_pass_wait_{gi}", passed, _gather_second, after)))

    tok = lambda i, j, k: (i, 0)
    whole = lambda i, j, k: (0, 0)
    kblk = lambda i, j, k: (k, 0)
    ff_seq = (None, None, SEQ, FF_BLK)

    h = _rms_fwd(xs, g_mix, token, "rms_mix")
    wg = gather_done(0, gather_pass(0, h)[0], h)
    wt_cat = _unshard_w_in(wg["w_in"])
    wgk_pad = jnp.pad(wg["w_gk_up"].transpose(1, 0, 2).reshape(GATE_RANK, GLA_DK), ((0, GK_PAD - GATE_RANK), (0, 0)))
    zcat = _mm(h, wt_cat, out_shape=(SEQ, N_CAT), out_dtype=BF, grid=(N_CAT_TILES, 1, 1),
               blk_a=(SEQ, D_MODEL), blk_b=(CAT_TILE, D_MODEL), blk_o=(SEQ, CAT_TILE),
               map_a=whole, map_b=lambda j, i, k: (j, 0), map_o=lambda j, i, k: (0, j), tb=True, name="mm_in")
    la = _gk_fwd(h, wt_cat, wgk_pad, b_gk)
    passed, tkn = gather_pass(1, la)
    o, states = _gla_fwd(zcat, la, tkn)
    wg = gather_done(1, passed, o)
    wpp = wg["w_pool_proj"].transpose(1, 0, 2).reshape(POOL_WIDTH, D_MODEL)
    wgp = wg["w_gla_proj"].reshape(D_MODEL, D_MODEL)
    wout = wg["w_out"].reshape(D_MODEL, D_MODEL)
    og = _post_gla_fwd(o, zcat, g_gla_head)
    ps = _pool_fwd(zcat, w_pool_grp[0], pool_scale)
    passed, tkn = gather_pass(2, (og, ps))
    y_pool, y_gla, mixed, x1, h2 = _mix_out_fwd(ps, og, zcat, xs, wpp, wgp, wout, b_gate, g_ffn, tkn)
    wg = gather_done(2, passed, h2)
    wt_up = wg["w_up"].reshape(2 * D_FF, D_MODEL)
    wdown = wg["w_down"].reshape(D_FF, D_MODEL)
    wconv4 = wg["w_conv"].reshape(2, 4, 3, FF_BLK)
    blk4 = lambda b, i, k: (b // 4, b % 4, 0, 0)
    u4, act = _up_conv_fwd(h2, wt_up, wconv4, bconv4)
    loss_part, dx2, dx2_bf, dg_final = _mm_tokens(
        act, wdown, blk_a=(None, 4, TOK_MM_TILE, FF_BLK), map_a=lambda i: (0, 0, i, 0),
        pieces=[(b, b * FF_BLK, FF_BLK) for b in range(4)], res=x1, then=("loss", g_final.reshape(1, D_MODEL), target),
        name="mm_down_loss")

    da = _mm(dx2_bf, wdown, out_shape=(1, 4, SEQ, FF_BLK), out_dtype=BF, grid=(4, 1, 1),
             blk_a=(SEQ, D_MODEL), blk_b=(FF_BLK, D_MODEL), blk_o=ff_seq,
             map_a=whole, map_b=lambda b, i, k: (b, 0), map_o=lambda b, i, k: (0, b, 0, 0), tb=True, name="mm_d_act")
    d_wdown = _mm(act, dx2_bf, out_shape=(D_FF, D_MODEL), out_dtype=BF, grid=(4, 1, 1),
                  blk_a=ff_seq, blk_b=(SEQ, D_MODEL), blk_o=(FF_BLK, D_MODEL),
                  map_a=lambda b, i, k: (0, b, 0, 0), map_b=whole, map_o=lambda b, i, k: (b, 0), ta=True,
                  name="mm_d_wdown")
    du4, d_wconv, d_bconv = _conv_bwd(u4, da, wconv4, bconv4)
    d_wt_up = _mm(du4, h2, out_shape=(2 * D_FF, D_MODEL), out_dtype=BF, grid=(N_DEV, 1, 1),
                  blk_a=ff_seq, blk_b=(SEQ, D_MODEL), blk_o=(FF_BLK, D_MODEL),
                  map_a=blk4, map_b=whole, map_o=lambda b, i, k: (b, 0), ta=True, name="mm_d_wup")
    res = {}

    def to_sibling(keys, parts):
        return [t for k in keys for t in (parts[k], lax.empty((4,) + parts[k].shape[2:], BF))], 4 * len(keys), _reduce_first

    def to_chips(keys, st, after):
        arrays = _split_wait("reduce_wait_" + keys[0], st, _reduce_first, after)
        sums = [_pair_sum(p, r, core, "pair_sum_" + k) for k, p, r in zip(keys, arrays[0::2], arrays[1::2])]
        return [t for s in sums for t in (s, lax.empty((3,) + s.shape[1:], BF))], 3 * len(keys), _reduce_second

    def reduce_start(keys, parts):
        st, tkn = _split_start("reduce_start_" + keys[0], [to_sibling(keys, parts)])
        return st[0], tkn

    def reduce_cross(keys, st, after):
        st2, tkn = _split_start("reduce_cross_" + keys[0], [to_chips(keys, st, after)])
        return st2[0], tkn

    def reduce_done(keys, st2, after):
        arrays = _split_wait("reduce_cross_wait_" + keys[0], st2, _reduce_second, after)
        for k, s, r in zip(keys, arrays[0::2], arrays[1::2]):
            outs = _chip_sum_adamw(s, r, big[k], moments[k][0], moments[k][1], chip, "adamw_" + k)
            res[k] = [jnp.transpose(t, (1, 2, 0)) if k == "w_in" else (t.T if k == "w_up" else t)[None] for t in outs]

    ffn_keys = ("w_down", "w_up")
    ffn_red, tkn = reduce_start(ffn_keys, dict(w_down=d_wdown.reshape(4, 2, D_FF // N_DEV, D_MODEL),
                                               w_up=d_wt_up.reshape(4, 2, FF_BLK, D_MODEL)))
    dx1, dg_ffn = _mm_tokens(
        du4, wt_up, blk_a=(2, 4, TOK_MM_TILE, FF_BLK), map_a=lambda i: (0, 0, i, 0),
        pieces=[((b // 4, b % 4), b * FF_BLK, FF_BLK) for b in range(N_DEV)], after=tkn, then=("rms_bwd", x1, g_ffn, dx2),
        name="mm_d_h2_rms")

    sq_t = dict(out_shape=(D_MODEL, D_MODEL), grid=(1, 1, N_MM_TILES), blk_a=(MM_TILE, D_MODEL),
                blk_b=(MM_TILE, D_MODEL), blk_o=(D_MODEL, D_MODEL), map_a=kblk, map_b=kblk, map_o=whole, ta=True)
    d_wout = _mm(mixed, dx1, out_dtype=BF, name="mm_d_wout", **sq_t)
    dzcat, dy_pool, dy_gla, db_gate = _mix_bwd(dx1, wout, zcat, b_gate, y_pool, y_gla)
    d_wgp = _mm(og, dy_gla, out_dtype=BF, name="mm_d_wgp", **sq_t)
    mix_keys = ("w_out", "w_gla_proj")
    (ffn_red, mix_red), tkn = _split_start("reduce_cross_w_down", [
        to_chips(ffn_keys, ffn_red, db_gate),
        to_sibling(mix_keys, dict(w_out=d_wout.reshape(4, 2, D_MODEL // N_DEV, D_MODEL),
                                  w_gla_proj=d_wgp.reshape(4, 2, D_MODEL // N_DEV, D_MODEL)))])
    dzcat, d_o, dg_head = _post_gla_bwd(dzcat, dy_gla, wgp, o, zcat, g_gla_head, tkn)
    dzcat, dla = _gla_bwd(dzcat, zcat, la, d_o, states)
    dzcat, d_wgk, db_gk = _gk_bwd(dzcat, dla, h, wt_cat, wgk_pad, b_gk)
    dps = _mm(dy_pool, wpp, out_shape=(SEQ, POOL_WIDTH), out_dtype=F32, grid=(N_MM_TILES, 1, 1),
              blk_a=(MM_TILE, D_MODEL), blk_b=(POOL_WIDTH, D_MODEL), blk_o=(MM_TILE, POOL_WIDTH),
              map_a=tok, map_b=whole, map_o=tok, tb=True, name="mm_d_ps")
    d_wpp = _mm(ps, dy_pool, out_shape=(POOL_WIDTH, D_MODEL), out_dtype=F32, grid=(1, 1, N_MM_TILES),
                blk_a=(MM_TILE, POOL_WIDTH), blk_b=(MM_TILE, D_MODEL), blk_o=(POOL_WIDTH, D_MODEL),
                map_a=kblk, map_b=kblk, map_o=whole, ta=True, name="mm_d_wpp")
    dzcat, d_wgrp, d_scale = _pool_bwd(dzcat, zcat, dps, w_pool_grp[0], pool_scale)
    row = lambda t: t.reshape(1, D_MODEL)
    small = [("b_gate", db_gate, b_gate, m_b_gate, v_b_gate, False),
             ("w_gk_up", d_wgk.reshape(GATE_RANK, N_DEV, GLA_DK // N_DEV).transpose(1, 0, 2), w_gk_up, m_w_gk_up,
              v_w_gk_up, True),
             ("b_gk", db_gk, b_gk, m_b_gk, v_b_gk, False),
             ("w_pool_grp", d_wgrp, w_pool_grp, m_w_pool_grp, v_w_pool_grp, False),
             ("pool_scale", d_scale, pool_scale, m_pool_scale, v_pool_scale, False),
             ("g_gla_head", dg_head, g_gla_head, m_g_gla_head, v_g_gla_head, False),
             ("g_ffn", dg_ffn, g_ffn, m_g_ffn, v_g_ffn, False),
             ("w_conv", d_wconv.reshape(N_DEV, 3, FF_BLK), w_conv, m_w_conv, v_w_conv, True),
             ("b_conv", d_bconv, bconv4, m_bconv4, v_bconv4, False),
             ("g_final", dg_final, row(g_final), row(m_g_final), row(v_g_final), False)]

    def to_all(parts):
        return [t for p in parts for t in (p, _gather_landing(p, me))], 7 * len(parts), _gather_direct

    (small_sent, mix_red), tkn = _split_start("small_start", [to_all([t[1] for t in small] + [loss_part]),
                                                              to_chips(mix_keys, mix_red, dla)])
    d_wt_cat = _mm(dzcat, h, out_shape=(N_DZ, D_MODEL), out_dtype=BF, grid=(N_DZ // DZ_TILE, 1, 1),
                   blk_a=(SEQ, DZ_TILE), blk_b=(SEQ, D_MODEL), blk_o=(DZ_TILE, D_MODEL),
                   map_a=lambda j, i, k: (0, j), map_b=whole, map_o=lambda j, i, k: (j, 0), ta=True, after=tkn,
                   name="mm_d_wcat")
    in_keys = ("w_in", "w_pool_proj")
    in_red, tkn = reduce_start(in_keys, dict(
        w_in=_shard_d_w_in(d_wt_cat).reshape(4, 2, IN_SHARD, D_MODEL),
        w_pool_proj=d_wpp.reshape(POOL_WIDTH, N_DEV, D_MODEL // N_DEV).transpose(1, 0, 2).astype(BF)
        .reshape(4, 2, POOL_WIDTH, D_MODEL // N_DEV)))
    reduce_done(mix_keys, mix_red, tkn)
    in_red, tkn = reduce_cross(in_keys, in_red, res["w_out"][0])
    grad_x, dg_mix = _mm_tokens(dzcat, wt_cat, blk_a=(TOK_MM_TILE, N_DZ), map_a=lambda i: (i, 0),
                                pieces=[(None, 0, N_DZ)], after=tkn, then=("rms_bwd", xs, g_mix, dx1),
                                name="mm_d_h_rms")
    (g_mix_sent,), tkn = _split_start("g_mix_start", [to_all([dg_mix])])
    reduce_done(ffn_keys, ffn_red, (grad_x, tkn))
    gathered = _split_wait("small_wait", small_sent, _gather_direct, res["w_down"][0])[1::2]
    small.append(("g_mix", dg_mix, g_mix, m_g_mix, v_g_mix, False))
    gathered = list(gathered[:-1]) + [_split_wait("g_mix_wait", g_mix_sent, _gather_direct, gathered[0])[1], gathered[-1]]
    small_out, loss_sum = _small_sum_adamw(jnp.reshape(me, (1,)).astype(jnp.int32),
                                           [(p,) + t[2:] for p, t in zip(gathered, small)], gathered[-1])
    for t, outs in zip(small, small_out):
        res[t[0]] = list(outs)
    res["b_conv"] = [t.reshape(b_conv.shape) for t in res["b_conv"]]
    res["g_final"] = [t.reshape(g_final.shape) for t in res["g_final"]]

    reduce_done(in_keys, in_red, loss_sum)
    loss = loss_sum[0, 0]
    order =["g_mix", "w_in", "b_gate", "w_gk_up", "b_gk", "w_pool_grp", "pool_scale", "g_gla_head", "w_pool_proj",
             "w_gla_proj", "w_out", "g_ffn", "w_up", "w_conv", "b_conv", "w_down", "g_final"]
    return (loss, grad_x[None], *[res[k][0] for k in order], *[res[k][1] for k in order],
            *[res[k][2] for k in order], *[res[k][3] for k in order])
```

```python
import jax
import jax.numpy as jnp
from jax import lax
from jax.experimental import pallas as pl
from jax.experimental.pallas import tpu as pltpu

F32 = jnp.float32
BF = jnp.bfloat16
HIGHEST = lax.Precision.HIGHEST
MESH = pl.DeviceIdType.MESH

N_DEV = 8
SEQ = 2048
D_MODEL = 1024
CHUNK = 64
EPS = 1e-6
POOL_WIDTH = 512
POOL_WINDOWS = (2, 4, 8, 16)
POOL_GD = 128
POOL_HALO = 16
HEADS = 4
HK = 128
HV = 256
GLA_DK = 512
GATE_RANK = 16
GATE_NORM = 16.0
D_FF = 2816
FF_BLK = 704
IN_SHARD = 706
C_QKV, C_GATE, C_OG, C_POOL, C_GK = 0, 2048, 4096, 5120, 5632
N_CAT = 5632
GK_PAD = 128
N_DZ = N_CAT + GK_PAD
R_POOL, R_QKV, R_OG, R_GK, R_GATE = 0, 512, 2560, 3584, 3600

ADAM_LR, ADAM_B1, ADAM_B2, ADAM_EPS, ADAM_WD, ADAM_STEP = 0.001, 0.9, 0.999, 1e-08, 0.01, 10
ADAM_C1 = 1.0 - ADAM_B1 ** ADAM_STEP
ADAM_C2 = 1.0 - ADAM_B2 ** ADAM_STEP

VMEM_BYTES_V7X = 64 * 1024 * 1024
VMEM_LIMIT = VMEM_BYTES_V7X * 3 // 4

TOK_TILE = 256
HALO = 8
GLA_CPS = 4


def _params(*sem):
    return pltpu.CompilerParams(dimension_semantics=sem, vmem_limit_bytes=VMEM_LIMIT)


def _const_spec(shape):
    nd = len(shape)
    return pl.BlockSpec(shape, lambda *_: (0,) * nd)


def _in_hbm(t):
    return pltpu.with_memory_space_constraint(t, pltpu.HBM)


def _out_hbm(shape, dtype):
    return pltpu.HBM(shape, dtype)


def _dot(a, b, ta=False, tb=False):
    dims = (((0 if ta else 1,), (1 if tb else 0,)), ((), ()))
    return lax.dot_general(a.astype(BF), b.astype(BF), dims, preferred_element_type=F32)


def _dot_exact(a, b):
    return jnp.dot(a, b, precision=HIGHEST, preferred_element_type=F32)


def _sigmoid(x):
    return 0.5 * jnp.tanh(0.5 * x) + 0.5


def _mm(a, b, *, out_shape, out_dtype, grid, blk_a, blk_b, blk_o, map_a, map_b, map_o, ta=False, tb=False,
        after=None, name):
    gk = grid[2]
    n_in = 2 + (after is not None)

    def body(*refs):
        a_ref, b_ref, o_ref = refs[0], refs[1], refs[n_in]
        prod = _dot(a_ref[...], b_ref[...], ta, tb)
        if gk == 1:
            o_ref[...] = prod.astype(out_dtype)
        else:
            acc = refs[n_in + 1]
            k = pl.program_id(2)

            @pl.when(k == 0)
            def _():
                acc[...] = prod

            @pl.when(k > 0)
            def _():
                acc[...] += prod

            @pl.when(k == gk - 1)
            def _():
                o_ref[...] = acc[...].astype(out_dtype)

    in_specs = [pl.BlockSpec(blk_a, map_a), pl.BlockSpec(blk_b, map_b)]
    args = [_in_hbm(a), _in_hbm(b)]
    if after is not None:
        in_specs.append(pl.BlockSpec(memory_space=pl.ANY))
        args.append(after)
    return pl.pallas_call(
        body, name=name, grid=grid, in_specs=in_specs, out_specs=pl.BlockSpec(blk_o, map_o),
        out_shape=_out_hbm(out_shape, out_dtype),
        scratch_shapes=[] if gk == 1 else [pltpu.VMEM(tuple(d for d in blk_o if d is not None), F32)],
        compiler_params=_params("parallel", "parallel", "arbitrary"),
    )(*args)


TOK_MM_TILE = 256


def _mm_tokens(a, w, *, blk_a, map_a, pieces, res=None, after=None, then=None, name):
    n_in = 2 + (res is not None) + (after is not None) + (0 if then is None else len(then) - 1)

    def accumulate(ref, part):
        @pl.when(pl.program_id(0) == 0)
        def _():
            ref[...] = part

        @pl.when(pl.program_id(0) > 0)
        def _():
            ref[...] += part

    def body(*refs):
        a_ref, w_ref = refs[:2]
        extra, outs = refs[n_in - (0 if then is None else len(then) - 1):n_in], refs[n_in:]
        total = None
        for idx, row, n in pieces:
            av = a_ref[...] if idx is None else a_ref[idx]
            prod = _dot(av, w_ref[row:row + n, :])
            total = prod if total is None else total + prod
        if res is not None:
            total = total + refs[2][...]
        if then is None:
            outs[0][...] = total
        elif then[0] == "rms_bwd":
            dx, part = _rms_bwd_tile(total, extra[0][...], extra[1][...], extra[2][...])
            outs[0][...] = dx
            accumulate(outs[1], part)
        else:
            lpart, dx, part = _loss_tile(total, extra[0][...], extra[1][...])
            outs[1][...] = dx
            outs[2][...] = dx.astype(BF)
            accumulate(outs[0], lpart)
            accumulate(outs[3], part)

    tile = pl.BlockSpec((TOK_MM_TILE, D_MODEL), lambda i: (i, 0))
    vec = _const_spec((1, D_MODEL))
    big = _out_hbm((SEQ, D_MODEL), F32)
    small = _out_hbm((1, D_MODEL), F32)
    in_specs = [pl.BlockSpec(blk_a, map_a), pl.BlockSpec(w.shape, lambda i: (0, 0), pipeline_mode=pl.Buffered(1))]
    args = [a, w]
    if res is not None:
        in_specs.append(tile)
        args.append(res)
    if after is not None:
        in_specs.append(pl.BlockSpec(memory_space=pl.ANY))
        args.append(after)
    if then is None:
        out_specs, out_shape = tile, big
    elif then[0] == "rms_bwd":
        in_specs += [tile, vec, tile]
        out_specs, out_shape = [tile, vec], [big, small]
    else:
        in_specs += [vec, tile]
        out_specs = [_const_spec((1, 128)), tile, tile, vec]
        out_shape = [_out_hbm((1, 128), F32), big, _out_hbm((SEQ, D_MODEL), BF), small]
    if then is not None:
        args += list(then[1:])
    return pl.pallas_call(
        body, name=name, grid=(SEQ // TOK_MM_TILE,), in_specs=in_specs, out_specs=out_specs, out_shape=out_shape,
        compiler_params=_params("parallel" if then is None else "arbitrary"),
    )(*[_in_hbm(t) for t in args])


def _rms_fwd(x, g, after, name):
    def body(x_ref, g_ref, after_ref, o_ref):
        del after_ref
        xv = x_ref[...]
        r = lax.rsqrt(jnp.mean(xv * xv, axis=-1, keepdims=True) + EPS)
        o_ref[...] = (xv * r * g_ref[...]).astype(BF)

    tile = pl.BlockSpec((TOK_TILE, D_MODEL), lambda i: (i, 0))
    return pl.pallas_call(
        body, name=name, grid=(SEQ // TOK_TILE,),
        in_specs=[tile, _const_spec((1, D_MODEL)), pl.BlockSpec(memory_space=pl.ANY)], out_specs=tile,
        out_shape=_out_hbm((SEQ, D_MODEL), BF), compiler_params=_params("parallel"),
    )(*map(_in_hbm, (x, g)), after)


def _rms_bwd_tile(dyv, xv, gv, dresv):
    r = lax.rsqrt(jnp.mean(xv * xv, axis=-1, keepdims=True) + EPS)
    xn = xv * r
    dxn = dyv * gv
    return dresv + r * (dxn - xn * jnp.mean(dxn * xn, axis=-1, keepdims=True)), jnp.sum(dyv * xn, axis=0, keepdims=True)


def _loss_tile(xv, gv, tv):
    r = lax.rsqrt(jnp.mean(xv * xv, axis=-1, keepdims=True) + EPS)
    xn = xv * r
    err = xn * gv - tv
    lpart = jnp.full((1, 128), 0.5 * jnp.sum(jnp.mean(err * err, axis=-1, keepdims=True)), F32)
    dyv = err * (1.0 / D_MODEL)
    dxn = dyv * gv
    return lpart, r * (dxn - xn * jnp.mean(dxn * xn, axis=-1, keepdims=True)), jnp.sum(dyv * xn, axis=0, keepdims=True)


def _pool_counts(w):
    pos = lax.broadcasted_iota(jnp.int32, (SEQ, 1), 0).astype(F32)
    return jnp.minimum(pos + 1.0, float(w))


def _pool_window(u, w, ext):
    ext[pl.ds(POOL_HALO, SEQ), :] = u
    win = u
    for j in range(1, w):
        win = win + ext[pl.ds(POOL_HALO - j, SEQ), :]
    return win / _pool_counts(w) - u


def _pool_fwd(zcat, w_grp, scale):
    def body(z_ref, w_ref, s_ref, o_ref, ext):
        ext[pl.ds(0, POOL_HALO), :] = jnp.zeros((POOL_HALO, POOL_GD), F32)
        for g, w in enumerate(POOL_WINDOWS):
            cols = slice(g * POOL_GD, (g + 1) * POOL_GD)
            p = _pool_window(z_ref[:, cols].astype(F32), w, ext)
            o_ref[:, cols] = (_dot(p, w_ref[g]) * s_ref[:, cols]).astype(BF)

    return pl.pallas_call(
        body, name="pool_fwd", grid=(1,),
        in_specs=[pl.BlockSpec((SEQ, POOL_WIDTH), lambda i: (0, C_POOL // POOL_WIDTH)),
                  _const_spec((4, POOL_GD, POOL_GD)), _const_spec((1, POOL_WIDTH))],
        out_specs=_const_spec((SEQ, POOL_WIDTH)), out_shape=_out_hbm((SEQ, POOL_WIDTH), BF),
        scratch_shapes=[pltpu.VMEM((POOL_HALO + SEQ, POOL_GD), F32)], compiler_params=_params("arbitrary"),
    )(*map(_in_hbm, (zcat, w_grp, scale)))


def _pool_bwd(dzcat, zcat, dps, w_grp, scale):
    def body(dz_in, z_ref, dps_ref, w_ref, s_ref, dz_ref, dw_ref, dsc_ref, ext, ext2):
        del dz_in
        ext[pl.ds(0, POOL_HALO), :] = jnp.zeros((POOL_HALO, POOL_GD), F32)
        ext2[pl.ds(SEQ, POOL_HALO), :] = jnp.zeros((POOL_HALO, POOL_GD), F32)
        for g, w in enumerate(POOL_WINDOWS):
            cols = slice(g * POOL_GD, (g + 1) * POOL_GD)
            p = _pool_window(z_ref[:, cols].astype(F32), w, ext)
            wg = w_ref[g]
            pg = _dot(p, wg)
            dpsv = dps_ref[:, cols]
            dsc_ref[:, cols] = jnp.sum(dpsv * pg, axis=0, keepdims=True)
            dpg = dpsv * s_ref[:, cols]
            dw_ref[g] = _dot(p, dpg, ta=True)
            dp = _dot(dpg, wg, tb=True)
            dpc = dp / _pool_counts(w)
            ext2[pl.ds(0, SEQ), :] = dpc
            du = dpc
            for j in range(1, w):
                du = du + ext2[pl.ds(j, SEQ), :]
            dz_ref[:, cols] = (du - dp).astype(BF)

    return pl.pallas_call(
        body, name="pool_bwd", grid=(1,),
        in_specs=[pl.BlockSpec(memory_space=pl.ANY),
                  pl.BlockSpec((SEQ, POOL_WIDTH), lambda i: (0, C_POOL // POOL_WIDTH)),
                  _const_spec((SEQ, POOL_WIDTH)), _const_spec((4, POOL_GD, POOL_GD)), _const_spec((1, POOL_WIDTH))],
        out_specs=[pl.BlockSpec((SEQ, POOL_WIDTH), lambda i: (0, C_POOL // POOL_WIDTH)),
                   _const_spec((4, POOL_GD, POOL_GD)), _const_spec((1, POOL_WIDTH))],
        out_shape=[_out_hbm((SEQ, N_DZ), BF), _out_hbm((4, POOL_GD, POOL_GD), F32),
                   _out_hbm((1, POOL_WIDTH), F32)],
        scratch_shapes=[pltpu.VMEM((POOL_HALO + SEQ, POOL_GD), F32), pltpu.VMEM((SEQ + POOL_HALO, POOL_GD), F32)],
        input_output_aliases={0: 0}, compiler_params=_params("arbitrary"),
    )(*map(_in_hbm, (dzcat, zcat, dps, w_grp, scale)))


GK_TILE = 512


GK_ROWS = pl.BlockSpec((GK_PAD, D_MODEL), lambda i: (C_GK // GK_PAD, 0))


def _gk_fwd(h, wt_cat, wgk_pad, b_gk):
    def body(h_ref, wt_ref, w_ref, b_ref, la_ref):
        z_gk = _dot(h_ref[...], wt_ref[...], tb=True)
        pre = _dot(z_gk, w_ref[...]) + b_ref[...]
        la_ref[...] = (jnp.minimum(pre, 0.0) - jnp.log(1.0 + jnp.exp(-jnp.abs(pre)))) * (1.0 / GATE_NORM)

    return pl.pallas_call(
        body, name="gk_fwd", grid=(SEQ // GK_TILE,),
        in_specs=[pl.BlockSpec((GK_TILE, D_MODEL), lambda i: (i, 0)), GK_ROWS,
                  _const_spec((GK_PAD, GLA_DK)), _const_spec((1, GLA_DK))],
        out_specs=pl.BlockSpec((GK_TILE, GLA_DK), lambda i: (i, 0)),
        out_shape=_out_hbm((SEQ, GLA_DK), F32), compiler_params=_params("parallel"),
    )(*map(_in_hbm, (h, wt_cat, wgk_pad, b_gk)))


def _gk_bwd(dzcat, dla, h, wt_cat, wgk_pad, b_gk):
    def body(dz_in, dla_ref, h_ref, wt_ref, w_ref, b_ref, dz_ref, dw_ref, db_ref):
        del dz_in
        wv = w_ref[...]
        z_gk = _dot(h_ref[...], wt_ref[...], tb=True)
        pre = _dot(z_gk, wv) + b_ref[...]
        dpre = dla_ref[...] * (1.0 / GATE_NORM) * (1.0 - _sigmoid(pre))
        dz_ref[...] = _dot(dpre, wv, tb=True).astype(BF)
        dwp = _dot(z_gk, dpre, ta=True)[:GATE_RANK]
        dbp = jnp.sum(dpre, axis=0, keepdims=True)

        @pl.when(pl.program_id(0) == 0)
        def _():
            dw_ref[...] = dwp
            db_ref[...] = dbp

        @pl.when(pl.program_id(0) > 0)
        def _():
            dw_ref[...] += dwp
            db_ref[...] += dbp

    return pl.pallas_call(
        body, name="gk_bwd", grid=(SEQ // GK_TILE,),
        in_specs=[pl.BlockSpec(memory_space=pl.ANY), pl.BlockSpec((GK_TILE, GLA_DK), lambda i: (i, 0)),
                  pl.BlockSpec((GK_TILE, D_MODEL), lambda i: (i, 0)), GK_ROWS, _const_spec((GK_PAD, GLA_DK)),
                  _const_spec((1, GLA_DK))],
        out_specs=[pl.BlockSpec((GK_TILE, GK_PAD), lambda i: (i, C_GK // GK_PAD)), _const_spec((GATE_RANK, GLA_DK)),
                   _const_spec((1, GLA_DK))],
        out_shape=[_out_hbm((SEQ, N_DZ), BF), _out_hbm((GATE_RANK, GLA_DK), F32),
                   _out_hbm((1, GLA_DK), F32)],
        input_output_aliases={0: 0}, compiler_params=_params("arbitrary"),
    )(*map(_in_hbm, (dzcat, dla, h, wt_cat, wgk_pad, b_gk)))


GLA_ROWS = GLA_CPS * CHUNK
GLA_STEPS = SEQ // GLA_ROWS
QKV_W = 2048


def _tri():
    return lax.broadcasted_iota(jnp.int32, (CHUNK, CHUNK), 0) >= lax.broadcasted_iota(jnp.int32, (CHUNK, CHUNK), 1)


def _chunk_cumsum(la_ref, rows):
    return _dot_exact(_tri().astype(F32), la_ref[rows, :])


def _gla_chunk(qkv_ref, la_ref, rows, h, bc_all):
    tri = _tri()
    q = qkv_ref[rows, h * HK:(h + 1) * HK].astype(F32) * (HK ** -0.5)
    k = qkv_ref[rows, GLA_DK + h * HK:GLA_DK + (h + 1) * HK].astype(F32)
    v = qkv_ref[rows, 2 * GLA_DK + h * HV:2 * GLA_DK + (h + 1) * HV].astype(BF)
    la = la_ref[rows, h * HK:(h + 1) * HK]
    bc = bc_all[:, h * HK:(h + 1) * HK]
    e_pos, e_neg = jnp.exp(bc), jnp.exp(-bc)
    dl = jnp.exp(jnp.sum(la, axis=0, keepdims=True))
    q_fw, q_bw, k_fw, k_bw = q * e_pos, q * e_neg, k * e_neg, k * e_pos
    scores = jnp.where(tri, _dot(q_fw, k_fw, tb=True), _dot(q_bw, k_bw, tb=True))
    return tri, v, e_pos, e_neg, dl, q_fw, q_bw, k_fw, k_bw, scores


def _gla_fwd(zcat, la, after):
    def body(qkv_ref, la_ref, after_ref, o_ref, st_ref, state):
        del after_ref

        @pl.when(pl.program_id(0) == 0)
        def _():
            state[...] = jnp.zeros_like(state)

        for c in range(GLA_CPS):
            rows = slice(c * CHUNK, (c + 1) * CHUNK)
            bc_all = _chunk_cumsum(la_ref, rows)
            for h in range(HEADS):
                _, v, _, _, dl, q_fw, _, k_fw, _, scores = _gla_chunk(qkv_ref, la_ref, rows, h, bc_all)
                st = state[h]
                st_ref[c, h] = st
                o_ref[rows, h * HV:(h + 1) * HV] = _dot(scores, v) + _dot(q_fw, st, tb=True)
                state[h] = st * dl + _dot(v, k_fw * dl, ta=True)

    return pl.pallas_call(
        body, name="gla_fwd", grid=(GLA_STEPS,),
        in_specs=[pl.BlockSpec((GLA_ROWS, QKV_W), lambda i: (i, 0)), pl.BlockSpec((GLA_ROWS, GLA_DK), lambda i: (i, 0)),
                  pl.BlockSpec(memory_space=pl.ANY)],
        out_specs=[pl.BlockSpec((GLA_ROWS, D_MODEL), lambda i: (i, 0)),
                   pl.BlockSpec((GLA_CPS, HEADS, HV, HK), lambda i: (i, 0, 0, 0))],
        out_shape=[_out_hbm((SEQ, D_MODEL), F32),
                   _out_hbm((SEQ // CHUNK, HEADS, HV, HK), F32)],
        scratch_shapes=[pltpu.VMEM((HEADS, HV, HK), F32)], compiler_params=_params("arbitrary"),
    )(*map(_in_hbm, (zcat, la)), after)


def _gla_bwd(dzcat, zcat, la, d_o, states):
    def body(dz_in, qkv_ref, la_ref, do_ref, st_ref, dqkv_ref, dla_ref, dstate):
        del dz_in

        @pl.when(pl.program_id(0) == 0)
        def _():
            dstate[...] = jnp.zeros_like(dstate)

        last_row = lax.broadcasted_iota(jnp.int32, (CHUNK, HK), 0) == CHUNK - 1
        upper = (lax.broadcasted_iota(jnp.int32, (CHUNK, CHUNK), 0)
                 <= lax.broadcasted_iota(jnp.int32, (CHUNK, CHUNK), 1)).astype(F32)
        for c in reversed(range(GLA_CPS)):
            rows = slice(c * CHUNK, (c + 1) * CHUNK)
            bc_all = _chunk_cumsum(la_ref, rows)
            dbs = []
            for h in range(HEADS):
                tri, v, e_pos, e_neg, dl, q_fw, q_bw, k_fw, k_bw, scores = _gla_chunk(qkv_ref, la_ref, rows, h, bc_all)
                st = st_ref[c, h]
                dst = dstate[h]
                d_out = do_ref[rows, h * HV:(h + 1) * HV].astype(BF)
                k_dec = k_fw * dl
                dp = _dot(d_out, v, tb=True)
                dp_fw = jnp.where(tri, dp, 0.0)
                dp_bw = jnp.where(tri, 0.0, dp)
                dv = _dot(scores, d_out, ta=True) + _dot(k_dec, dst, tb=True)
                dk_dec = _dot(v, dst)
                dq_fw = _dot(dp_fw, k_fw) + _dot(d_out, st)
                dk_fw = _dot(dp_fw, q_fw, ta=True) + dk_dec * dl
                dq_bw = _dot(dp_bw, k_bw)
                dk_bw = _dot(dp_bw, q_bw, ta=True)
                ddl = jnp.sum(st * dst, axis=0, keepdims=True) + jnp.sum(k_fw * dk_dec, axis=0, keepdims=True)
                dstate[h] = dst * dl + _dot(d_out, q_fw, ta=True)
                dq = (dq_fw * e_pos + dq_bw * e_neg) * (HK ** -0.5)
                dk = dk_fw * e_neg + dk_bw * e_pos
                dbs.append(dq_fw * q_fw - dk_fw * k_fw - dq_bw * q_bw + dk_bw * k_bw + jnp.where(last_row, ddl * dl, 0.0))
                dqkv_ref[rows, h * HK:(h + 1) * HK] = dq.astype(BF)
                dqkv_ref[rows, GLA_DK + h * HK:GLA_DK + (h + 1) * HK] = dk.astype(BF)
                dqkv_ref[rows, 2 * GLA_DK + h * HV:2 * GLA_DK + (h + 1) * HV] = dv.astype(BF)
            dla_ref[rows, :] = _dot_exact(upper, jnp.concatenate(dbs, axis=1))

    rev = lambda i: (GLA_STEPS - 1 - i, 0)
    return pl.pallas_call(
        body, name="gla_bwd", grid=(GLA_STEPS,),
        in_specs=[pl.BlockSpec(memory_space=pl.ANY), pl.BlockSpec((GLA_ROWS, QKV_W), rev),
                  pl.BlockSpec((GLA_ROWS, GLA_DK), rev), pl.BlockSpec((GLA_ROWS, D_MODEL), rev),
                  pl.BlockSpec((GLA_CPS, HEADS, HV, HK), lambda i: (GLA_STEPS - 1 - i, 0, 0, 0))],
        out_specs=[pl.BlockSpec((GLA_ROWS, QKV_W), rev), pl.BlockSpec((GLA_ROWS, GLA_DK), rev)],
        out_shape=[_out_hbm((SEQ, N_DZ), BF), _out_hbm((SEQ, GLA_DK), F32)],
        scratch_shapes=[pltpu.VMEM((HEADS, HV, HK), F32)], input_output_aliases={0: 0},
        compiler_params=_params("arbitrary"),
    )(*map(_in_hbm, (dzcat, zcat, la, d_o, states)))


def _silu_parts(x):
    s = _sigmoid(x)
    return x * s, s * (1.0 + x * (1.0 - s))


def _post_gla_fwd(o, zcat, g_head):
    def body(o_ref, zog_ref, g_ref, out_ref):
        for h in range(HEADS):
            cols = slice(h * HV, (h + 1) * HV)
            ov = o_ref[:, cols]
            r = lax.rsqrt(jnp.mean(ov * ov, axis=-1, keepdims=True) + EPS)
            act, _ = _silu_parts(zog_ref[:, cols].astype(F32))
            out_ref[:, cols] = (ov * r * g_ref[...] * act).astype(BF)

    tile = pl.BlockSpec((TOK_TILE, D_MODEL), lambda i: (i, 0))
    return pl.pallas_call(
        body, name="post_gla_fwd", grid=(SEQ // TOK_TILE,),
        in_specs=[tile, pl.BlockSpec((TOK_TILE, D_MODEL), lambda i: (i, C_OG // D_MODEL)), _const_spec((1, HV))],
        out_specs=tile, out_shape=_out_hbm((SEQ, D_MODEL), BF), compiler_params=_params("parallel"),
    )(*map(_in_hbm, (o, zcat, g_head)))


def _post_gla_bwd(dzcat, dy_gla, w_gla_proj, o, zcat, g_head, after):
    def body(dz_in, dyg_ref, w_ref, o_ref, zog_ref, g_ref, after_ref, dz_ref, do_ref, dg_ref):
        del dz_in, after_ref
        dog = _dot(dyg_ref[...], w_ref[...], tb=True)
        gpart = jnp.zeros((1, HV), F32)
        gv = g_ref[...]
        for h in range(HEADS):
            cols = slice(h * HV, (h + 1) * HV)
            ov = o_ref[:, cols]
            r = lax.rsqrt(jnp.mean(ov * ov, axis=-1, keepdims=True) + EPS)
            on = ov * r
            act, dact = _silu_parts(zog_ref[:, cols].astype(F32))
            dogv = dog[:, cols]
            dz_ref[:, cols] = (dogv * on * gv * dact).astype(BF)
            d_on_g = dogv * act
            gpart = gpart + jnp.sum(d_on_g * on, axis=0, keepdims=True)
            dxn = d_on_g * gv
            do_ref[:, cols] = (r * (dxn - on * jnp.mean(dxn * on, axis=-1, keepdims=True))).astype(BF)

        @pl.when(pl.program_id(0) == 0)
        def _():
            dg_ref[...] = gpart

        @pl.when(pl.program_id(0) > 0)
        def _():
            dg_ref[...] += gpart

    tile = pl.BlockSpec((TOK_TILE, D_MODEL), lambda i: (i, 0))
    ogspec = pl.BlockSpec((TOK_TILE, D_MODEL), lambda i: (i, C_OG // D_MODEL))
    return pl.pallas_call(
        body, name="post_gla_bwd", grid=(SEQ // TOK_TILE,),
        in_specs=[pl.BlockSpec(memory_space=pl.ANY), tile, _const_spec((D_MODEL, D_MODEL)), tile, ogspec,
                  _const_spec((1, HV)), pl.BlockSpec(memory_space=pl.ANY)],
        out_specs=[ogspec, tile, _const_spec((1, HV))],
        out_shape=[_out_hbm((SEQ, N_DZ), BF), _out_hbm((SEQ, D_MODEL), BF),
                   _out_hbm((1, HV), F32)],
        input_output_aliases={0: 0}, compiler_params=_params("arbitrary"),
    )(*map(_in_hbm, (dzcat, dy_gla, w_gla_proj, o, zcat, g_head)), after)


GATE_W = 2 * D_MODEL


def _mix_out_fwd(ps, og, zcat, x, w_pool_proj, w_gla_proj, w_out, b_gate, g_ffn, after):
    def body(ps_ref, og_ref, zg_ref, x_ref, wpp_ref, wgp_ref, wout_ref, b_ref, g_ref, after_ref,
             yp_ref, yg_ref, mixed_ref, x1_ref, h2_ref):
        del after_ref
        y_pool = _dot(ps_ref[...], wpp_ref[...])
        y_gla = _dot(og_ref[...], wgp_ref[...])
        yp_ref[...] = y_pool.astype(BF)
        yg_ref[...] = y_gla.astype(BF)
        g0 = _sigmoid(zg_ref[:, :D_MODEL].astype(F32) + b_ref[:, :D_MODEL])
        g1 = _sigmoid(zg_ref[:, D_MODEL:].astype(F32) + b_ref[:, D_MODEL:])
        mixed = (g0 * y_pool + g1 * y_gla).astype(BF)
        mixed_ref[...] = mixed
        x1 = x_ref[...] + _dot(mixed, wout_ref[...])
        x1_ref[...] = x1
        r = lax.rsqrt(jnp.mean(x1 * x1, axis=-1, keepdims=True) + EPS)
        h2_ref[...] = (x1 * r * g_ref[...]).astype(BF)

    tile = pl.BlockSpec((TOK_TILE, D_MODEL), lambda i: (i, 0))
    resident = lambda shape: pl.BlockSpec(shape, lambda i: (0, 0), pipeline_mode=pl.Buffered(1))
    f32, bf16 = _out_hbm((SEQ, D_MODEL), F32), _out_hbm((SEQ, D_MODEL), BF)
    return pl.pallas_call(
        body, name="mix_out_fwd", grid=(SEQ // TOK_TILE,),
        in_specs=[pl.BlockSpec((TOK_TILE, POOL_WIDTH), lambda i: (i, 0)), tile,
                  pl.BlockSpec((TOK_TILE, GATE_W), lambda i: (i, C_GATE // GATE_W)), tile,
                  resident((POOL_WIDTH, D_MODEL)), resident((D_MODEL, D_MODEL)), resident((D_MODEL, D_MODEL)),
                  _const_spec((1, GATE_W)), _const_spec((1, D_MODEL)), pl.BlockSpec(memory_space=pl.ANY)],
        out_specs=[tile] * 5, out_shape=[bf16, bf16, bf16, f32, bf16], compiler_params=_params("parallel"),
    )(*map(_in_hbm, (ps, og, zcat, x, w_pool_proj, w_gla_proj, w_out, b_gate, g_ffn)), after)


def _mix_bwd(dx1, w_out, zcat, b_gate, y_pool, y_gla):
    def body(dx_ref, w_ref, zg_ref, b_ref, yp_ref, yg_ref, dz_ref, dyp_ref, dyg_ref, db_ref):
        dm = _dot(dx_ref[...], w_ref[...], tb=True)
        g0 = _sigmoid(zg_ref[:, :D_MODEL].astype(F32) + b_ref[:, :D_MODEL])
        g1 = _sigmoid(zg_ref[:, D_MODEL:].astype(F32) + b_ref[:, D_MODEL:])
        dyp_ref[...] = (dm * g0).astype(BF)
        dyg_ref[...] = (dm * g1).astype(BF)
        dz0 = dm * yp_ref[...].astype(F32) * g0 * (1.0 - g0)
        dz1 = dm * yg_ref[...].astype(F32) * g1 * (1.0 - g1)
        dz_ref[:, :D_MODEL] = dz0.astype(BF)
        dz_ref[:, D_MODEL:] = dz1.astype(BF)
        b0 = jnp.sum(dz0, axis=0, keepdims=True)
        b1 = jnp.sum(dz1, axis=0, keepdims=True)

        @pl.when(pl.program_id(0) == 0)
        def _():
            db_ref[:, :D_MODEL] = b0
            db_ref[:, D_MODEL:] = b1

        @pl.when(pl.program_id(0) > 0)
        def _():
            db_ref[:, :D_MODEL] += b0
            db_ref[:, D_MODEL:] += b1

    tile = pl.BlockSpec((TOK_TILE, D_MODEL), lambda i: (i, 0))
    gspec = pl.BlockSpec((TOK_TILE, GATE_W), lambda i: (i, C_GATE // GATE_W))
    return pl.pallas_call(
        body, name="mix_bwd", grid=(SEQ // TOK_TILE,),
        in_specs=[tile, _const_spec((D_MODEL, D_MODEL)), gspec, _const_spec((1, GATE_W)), tile, tile],
        out_specs=[gspec, tile, tile, _const_spec((1, GATE_W))],
        out_shape=[_out_hbm((SEQ, N_DZ), BF), _out_hbm((SEQ, D_MODEL), BF),
                   _out_hbm((SEQ, D_MODEL), BF), _out_hbm((1, GATE_W), F32)],
        compiler_params=_params("arbitrary"),
    )(*map(_in_hbm, (dx1, w_out, zcat, b_gate, y_pool, y_gla)))


N_TOK_TILES = SEQ // TOK_TILE
HALO_PER_TILE = TOK_TILE // HALO


LANE_TILES = tuple((lo, min(128, FF_BLK - lo)) for lo in range(0, FF_BLK, 128))


def _taps(w_ref, b_ref, half, lanes, rows):
    shape = (rows, lanes.stop - lanes.start)
    return ([jnp.broadcast_to(w_ref[half, j:j + 1, lanes], shape) for j in range(3)],
            jnp.broadcast_to(b_ref[half, :, lanes], shape))


def _conv_strips(u_ref, ub_ref, ua_ref, taps, lanes, width, n_strips, first):
    row = lax.broadcasted_iota(jnp.int32, (HALO, width), 0)
    prev = [[pltpu.roll(jnp.where(first, 0.0, ub_ref[half, :, lanes]), k, 0) for k in (1, 2)] for half in range(2)]
    for s in range(n_strips + (ua_ref is not None)):
        u3, conv = [], []
        for half in range(2):
            cur = u_ref[half, s * HALO:(s + 1) * HALO, lanes] if s < n_strips else ua_ref[half, :, lanes]
            rolled = [pltpu.roll(cur, k, 0) for k in (1, 2)]
            frames = [jnp.where(row >= 2, rolled[1], prev[half][1]), jnp.where(row >= 1, rolled[0], prev[half][0]), cur]
            prev[half] = rolled
            w3, bias = taps[half]
            u3.append(frames)
            conv.append(bias + frames[0] * w3[0] + frames[1] * w3[1] + frames[2] * w3[2])
        yield s, u3, conv


def _pair_specs(pairs):
    tile = pl.BlockSpec((pairs, None, TOK_TILE, FF_BLK), lambda b, i: (0, b, i, 0))
    before = pl.BlockSpec((pairs, None, HALO, FF_BLK), lambda b, i: (0, b, jnp.maximum(i * HALO_PER_TILE - 1, 0), 0))
    after = pl.BlockSpec((pairs, None, HALO, FF_BLK),
                         lambda b, i: (0, b, jnp.minimum((i + 1) * HALO_PER_TILE, SEQ // HALO - 1), 0))

    def vec(rows):
        return pl.BlockSpec((2, None, rows, FF_BLK), lambda b, i: (0, b, 0, 0))

    return tile, before, after, vec


N_STRIPS = TOK_TILE // HALO


def _up_conv_fwd(h2, wt_up, w_conv, b_conv):
    steps = N_TOK_TILES // 2

    def body(h_ref, h_next, wg_ref, wv_ref, w_ref, b_ref, u_ref, a_ref, buf_a, buf_b, carry):
        j = pl.program_id(1)

        def project(hv, buf):
            buf[0] = _dot(hv, wg_ref[...], tb=True)
            buf[1] = _dot(hv, wv_ref[...], tb=True)

        def conv(buf, row0):
            u_ref[:, row0:row0 + TOK_TILE, :] = buf[...]
            for lo, width in LANE_TILES:
                lanes = slice(lo, lo + width)
                taps = [_taps(w_ref, b_ref, half, lanes, HALO) for half in range(2)]
                pending = None
                for s, _, (cg, cv) in _conv_strips(buf, carry, None, taps, lanes, width, N_STRIPS, False):
                    act = cg * _sigmoid(cg) * cv
                    if s % 2 == 0:
                        pending = act
                    else:
                        a_ref[0, row0 + (s - 1) * HALO:row0 + (s + 1) * HALO, lanes] = (
                            jnp.concatenate([pending, act], axis=0).astype(BF))
            carry[...] = buf[:, TOK_TILE - HALO:, :]

        @pl.when(j == 0)
        def _():
            project(h_ref[0:TOK_TILE, :], buf_a)
            carry[...] = jnp.zeros_like(carry)

        project(h_ref[TOK_TILE:, :], buf_b)
        conv(buf_a, 0)
        project(h_next[...], buf_a)
        conv(buf_b, TOK_TILE)

    w_blk = lambda half: pl.BlockSpec((FF_BLK, D_MODEL), lambda b, j: (b + 4 * half, 0))
    vec = lambda rows: pl.BlockSpec((2, None, rows, FF_BLK), lambda b, j: (0, b, 0, 0))
    u_buf = pltpu.VMEM((2, TOK_TILE, FF_BLK), F32)
    return pl.pallas_call(
        body, name="up_conv_fwd", grid=(4, steps),
        in_specs=[pl.BlockSpec((2 * TOK_TILE, D_MODEL), lambda b, j: (j, 0)),
                  pl.BlockSpec((TOK_TILE, D_MODEL), lambda b, j: (jnp.minimum(2 * j + 2, N_TOK_TILES - 1), 0)),
                  w_blk(0), w_blk(1), vec(3), vec(1)],
        out_specs=[pl.BlockSpec((2, None, 2 * TOK_TILE, FF_BLK), lambda b, j: (0, b, j, 0)),
                   pl.BlockSpec((1, None, 2 * TOK_TILE, FF_BLK), lambda b, j: (0, b, j, 0))],
        out_shape=[_out_hbm((2, 4, SEQ, FF_BLK), F32), _out_hbm((1, 4, SEQ, FF_BLK), BF)],
        scratch_shapes=[u_buf, u_buf, pltpu.VMEM((2, HALO, FF_BLK), F32)],
        compiler_params=_params("parallel", "arbitrary"),
    )(*map(_in_hbm, (h2, h2, wt_up, wt_up, w_conv, b_conv)))


def _conv_bwd(u, da, w_conv, b_conv):
    def body(u_ref, ub_ref, ua_ref, da_ref, daa_ref, w_ref, b_ref, du_ref, dw_ref, db_ref):
        i = pl.program_id(1)

        @pl.when(i == 0)
        def _():
            dw_ref[...] = jnp.zeros_like(dw_ref)
            db_ref[...] = jnp.zeros_like(db_ref)

        for lo, width in LANE_TILES:
            lanes = slice(lo, lo + width)
            row = lax.broadcasted_iota(jnp.int32, (HALO, width), 0)
            taps = [_taps(w_ref, b_ref, half, lanes, HALO) for half in range(2)]
            acc_w = [[jnp.zeros((HALO, width), F32) for _ in range(3)] for _ in range(2)]
            acc_b = [jnp.zeros((HALO, width), F32) for _ in range(2)]
            da_pair, pending = None, [None, None]
            dc_prev, up_prev = [None, None], [None, None]
            for s, u3, (cg, cv) in _conv_strips(u_ref, ub_ref, ua_ref, taps, lanes, width, N_STRIPS, i == 0):
                act, dact = _silu_parts(cg)
                if s == N_STRIPS:
                    da = jnp.where(i < N_TOK_TILES - 1, daa_ref[0, :, lanes].astype(F32), 0.0)
                elif s % 2 == 0:
                    da_pair = da_ref[0, s * HALO:(s + 2) * HALO, lanes].astype(F32)
                    da = da_pair[:HALO]
                else:
                    da = da_pair[HALO:]
                dc = (da * cv * dact, da * act)
                for half in range(2):
                    up = [pltpu.roll(dc[half], HALO - k, 0) for k in (1, 2)]
                    if s < N_STRIPS:
                        for j in range(3):
                            acc_w[half][j] = acc_w[half][j] + dc[half] * u3[half][j]
                        acc_b[half] = acc_b[half] + dc[half]
                    if s >= 1:
                        w3 = taps[half][0]
                        du = (dc_prev[half] * w3[2] + jnp.where(row < HALO - 1, up_prev[half][0], up[0]) * w3[1]
                              + jnp.where(row < HALO - 2, up_prev[half][1], up[1]) * w3[0])
                        if (s - 1) % 2 == 0:
                            pending[half] = du
                        else:
                            du_ref[half, (s - 2) * HALO:s * HALO, lanes] = jnp.concatenate([pending[half], du],
                                                                                           axis=0).astype(BF)
                    dc_prev[half], up_prev[half] = dc[half], up
            for half in range(2):
                for j in range(3):
                    dw_ref[half, j:j + 1, lanes] += jnp.sum(acc_w[half][j], axis=0, keepdims=True)
                db_ref[half, :, lanes] += jnp.sum(acc_b[half], axis=0, keepdims=True)

    tile, before, after, vec = _pair_specs(2)
    da_tile, _, da_after_spec, _ = _pair_specs(1)
    return pl.pallas_call(
        body, name="conv_bwd", grid=(4, N_TOK_TILES),
        in_specs=[tile, before, after, da_tile, da_after_spec, vec(3), vec(1)],
        out_specs=[tile, vec(3), vec(1)],
        out_shape=[_out_hbm((2, 4, SEQ, FF_BLK), BF), _out_hbm((2, 4, 3, FF_BLK), F32),
                   _out_hbm((2, 4, 1, FF_BLK), F32)],
        compiler_params=_params("parallel", "arbitrary"),
    )(*map(_in_hbm, (u, u, u, da, da, w_conv, b_conv)))


W_IN_SEGMENTS = ((R_POOL, POOL_WIDTH, C_POOL), (R_QKV, QKV_W, C_QKV), (R_OG, D_MODEL, C_OG), (R_GK, GATE_RANK, C_GK),
                 (R_GATE, GATE_W, C_GATE))


def _slab_pieces(d):
    lo, hi = d * IN_SHARD, (d + 1) * IN_SHARD
    pieces = []
    for start, n, at in W_IN_SEGMENTS:
        a, b = max(lo, start), min(hi, start + n)
        if a < b:
            assert (a - lo) % 2 == 0 and (b - a) % 2 == 0 and (at + a - start) % 2 == 0
            pieces.append(((a - lo) // 2, (b - a) // 2, (at + a - start) // 2))
    return pieces


def _unshard_w_in(slabs):
    def body(slab_ref, cat_ref):
        d = pl.program_id(0)
        src = slab_ref.bitcast(jnp.uint32)
        dst = cat_ref.bitcast(jnp.uint32)

        @pl.when(d == 0)
        def _():
            cat_ref[C_GK:, :] = jnp.zeros((GK_PAD, D_MODEL), BF)

        for dd in range(N_DEV):
            @pl.when(d == dd)
            def _():
                for a, n, at in _slab_pieces(dd):
                    dst[pl.ds(at, n), :] = src[0, pl.ds(a, n), :]

    return pl.pallas_call(
        body, name="unshard_w_in", grid=(N_DEV,),
        in_specs=[pl.BlockSpec((1, IN_SHARD, D_MODEL), lambda d: (d, 0, 0))], out_specs=_const_spec((N_DZ, D_MODEL)),
        out_shape=_out_hbm((N_DZ, D_MODEL), BF), compiler_params=_params("arbitrary"),
    )(_in_hbm(slabs))


def _shard_d_w_in(d_cat):
    def body(cat_ref, slab_ref):
        d = pl.program_id(0)
        cat = cat_ref.bitcast(jnp.uint32)
        dst = slab_ref.bitcast(jnp.uint32)
        for dd in range(N_DEV):
            @pl.when(d == dd)
            def _():
                for a, n, at in _slab_pieces(dd):
                    dst[0, pl.ds(a, n), :] = cat[pl.ds(at, n), :]

    return pl.pallas_call(
        body, name="shard_d_w_in", grid=(N_DEV,), in_specs=[_const_spec((N_DZ, D_MODEL))],
        out_specs=pl.BlockSpec((1, IN_SHARD, D_MODEL), lambda d: (d, 0, 0)),
        out_shape=_out_hbm((N_DEV, IN_SHARD, D_MODEL), BF), compiler_params=_params("parallel"),
    )(_in_hbm(d_cat))


ANY = pl.BlockSpec(memory_space=pl.ANY)


def _place():
    x, y, c = lax.axis_index("x"), lax.axis_index("y"), lax.axis_index("c")
    other_chips = [(1 - x, y), (x, 1 - y), (1 - x, 1 - y)]
    return x, y, c, other_chips


SEM = pl.BlockSpec(memory_space=pltpu.SEMAPHORE)
IN_HBM = pl.BlockSpec(memory_space=pltpu.HBM)
SPLIT_PARAMS = pltpu.CompilerParams(has_side_effects=pltpu.SideEffectType.DATAFLOW_SIDE_EFFECTING)


def _gather_first(refs, send_sems, recv_sems):
    x, y, c, chips = _place()
    targets = [(x, y, 1 - c)] + [(px, py, c) for px, py in chips]
    return [pltpu.make_async_remote_copy(src_ref=refs[2 * a], dst_ref=refs[2 * a + 1].at[4 * x + 2 * y + c],
                                         send_sem=send_sems.at[4 * a + k], recv_sem=recv_sems.at[4 * a + k],
                                         device_id=to, device_id_type=MESH)
            for a in range(len(refs) // 2) for k, to in enumerate(targets)]


def _gather_direct(refs, send_sems, recv_sems):
    x, y, c, _ = _place()
    flips = [(dx, dy, dc) for dx in (0, 1) for dy in (0, 1) for dc in (0, 1) if dx + dy + dc]
    targets = [(1 - x if dx else x, 1 - y if dy else y, 1 - c if dc else c) for dx, dy, dc in flips]
    return [pltpu.make_async_remote_copy(src_ref=refs[2 * a], dst_ref=refs[2 * a + 1].at[4 * x + 2 * y + c],
                                         send_sem=send_sems.at[7 * a + k], recv_sem=recv_sems.at[7 * a + k],
                                         device_id=to, device_id_type=MESH)
            for a in range(len(refs) // 2) for k, to in enumerate(targets)]


def _gather_second(refs, send_sems, recv_sems):
    x, y, c, chips = _place()
    copies = []
    for a, land in enumerate(refs):
        for j, (px, py) in enumerate(chips):
            block = land.at[4 * px + 2 * py + c]
            copies.append(pltpu.make_async_remote_copy(src_ref=block, dst_ref=block, send_sem=send_sems.at[3 * a + j],
                                                       recv_sem=recv_sems.at[3 * a + j], device_id=(x, y, 1 - c),
                                                       device_id_type=MESH))
    return copies


def _reduce_first(refs, send_sems, recv_sems):
    x, y, c, _ = _place()
    return [pltpu.make_async_remote_copy(src_ref=refs[2 * a].at[j, 1 - c], dst_ref=refs[2 * a + 1].at[j],
                                         send_sem=send_sems.at[4 * a + j], recv_sem=recv_sems.at[4 * a + j],
                                         device_id=(x, y, 1 - c), device_id_type=MESH)
            for a in range(len(refs) // 2) for j in range(4)]


def _reduce_second(refs, send_sems, recv_sems):
    _, _, c, chips = _place()
    return [pltpu.make_async_remote_copy(src_ref=refs[2 * a].at[2 * px + py], dst_ref=refs[2 * a + 1].at[k],
                                         send_sem=send_sems.at[3 * a + k], recv_sem=recv_sems.at[3 * a + k],
                                         device_id=(px, py, c), device_id_type=MESH)
            for a in range(len(refs) // 2) for k, (px, py) in enumerate(chips)]


def _split_start(name, groups):
    arrays = [a for g in groups for a in g[0]]
    n = len(arrays)

    def body(*refs):
        sems = refs[n:n + 2 * len(groups)]
        at = 0
        for gi, (members, _, build) in enumerate(groups):
            for cp in build(refs[at:at + len(members)], sems[2 * gi], sems[2 * gi + 1]):
                cp.start()
            at += len(members)
        refs[-1][...] = jnp.zeros_like(refs[-1])

    sem_shapes = [pltpu.SemaphoreType.DMA((g[1],)) for g in groups for _ in range(2)]
    outs = pl.pallas_call(
        body, name=name, in_specs=[IN_HBM] * n,
        out_shape=(*sem_shapes, *[_out_hbm(a.shape, a.dtype) for a in arrays], jax.ShapeDtypeStruct((8, 128), F32)),
        out_specs=(*[SEM] * len(sem_shapes), *[IN_HBM] * n, pl.BlockSpec(memory_space=pltpu.VMEM)),
        input_output_aliases={i: len(sem_shapes) + i for i in range(n)}, compiler_params=SPLIT_PARAMS,
    )(*[pltpu.with_memory_space_constraint(a, pltpu.HBM) for a in arrays])
    per_group, at = [], len(sem_shapes)
    for gi, (members, _, _) in enumerate(groups):
        per_group.append((outs[2 * gi], outs[2 * gi + 1], list(outs[at:at + len(members)])))
        at += len(members)
    return per_group, outs[-1]


def _split_wait(name, started, build, after):
    send_sems, recv_sems, arrays = started
    n = len(arrays)
    after = after if isinstance(after, (tuple, list)) else (after,)

    def body(*refs):
        for cp in build(refs[:n], refs[n], refs[n + 1]):
            cp.wait_send()
            cp.wait_recv()

    return pl.pallas_call(
        body, name=name, in_specs=[IN_HBM] * n + [SEM, SEM] + [ANY] * len(after),
        out_shape=tuple(_out_hbm(a.shape, a.dtype) for a in arrays), out_specs=tuple([IN_HBM] * n),
        input_output_aliases={i: i for i in range(n)}, compiler_params=SPLIT_PARAMS,
    )(*arrays, send_sems, recv_sems, *after)


def _placed_behind(token, arrays, name):
    n = len(arrays)

    def body(*refs):
        refs[-1][...] = jnp.zeros_like(refs[-1])

    outs = pl.pallas_call(
        body, name=name, in_specs=[IN_HBM] * n + [ANY],
        out_shape=(*[_out_hbm(a.shape, a.dtype) for a in arrays], jax.ShapeDtypeStruct((8, 128), F32)),
        out_specs=(*[IN_HBM] * n, pl.BlockSpec(memory_space=pltpu.VMEM)),
        input_output_aliases={i: i for i in range(n)},
    )(*map(_in_hbm, arrays), token)
    return outs[:n], outs[-1]


def _gather_landing(shard, me):
    return lax.dynamic_update_slice(lax.empty((N_DEV,) + shard.shape, shard.dtype), shard[None],
                                    (me,) + (0,) * shard.ndim)


ADAM_LANE_TILE = 256


def _tile_2d(rows, cols):
    for t in (256, 176, 128):
        if rows % t == 0:
            return t, cols
    return rows, ADAM_LANE_TILE


def _pair_sum(part, recv, core, name):
    _, rows, cols = recv.shape
    tr, tc = rows, cols

    def body(c_ref, p_ref, r_ref, o_ref):
        del c_ref
        o_ref[...] = (p_ref[...].astype(F32) + r_ref[...].astype(F32)).astype(BF)

    grid_spec = pltpu.PrefetchScalarGridSpec(
        num_scalar_prefetch=1, grid=(4, rows // tr, cols // tc),
        in_specs=[pl.BlockSpec((None, None, tr, tc), lambda j, i, k, c_ref: (j, c_ref[0], i, k)),
                  pl.BlockSpec((None, tr, tc), lambda j, i, k, c_ref: (j, i, k))],
        out_specs=pl.BlockSpec((None, tr, tc), lambda j, i, k, c_ref: (j, i, k)))
    return pl.pallas_call(
        body, name=name, grid_spec=grid_spec, out_shape=_out_hbm(recv.shape, BF),
        compiler_params=_params("parallel", "parallel", "parallel"),
    )(core, *map(_in_hbm, (part, recv)))


def _adamw(w, g, m, v):
    m = ADAM_B1 * m + (1.0 - ADAM_B1) * g
    v = ADAM_B2 * v + (1.0 - ADAM_B2) * (g * g)
    delta = -ADAM_LR * ((m / ADAM_C1) / (jnp.sqrt(v / ADAM_C2) + ADAM_EPS) + ADAM_WD * w)
    return delta, m, v


def _chip_sum_adamw(sums, recv, w, m, v, chip, name):
    rows, cols = w.shape[0], w.shape[-1]
    lone_rows = w.ndim == 3
    tr, tc = _tile_2d(rows, cols)

    def body(chip_ref, s_ref, r_ref, w_ref, m_ref, v_ref, g_out, d_out, m_out, v_out):
        del chip_ref
        g = s_ref[...].astype(F32)
        for k in range(3):
            g = g + r_ref[k].astype(F32)
        g = g[:, None, :] if lone_rows else g
        g_out[...] = g
        d_out[...], m_out[...], v_out[...] = _adamw(w_ref[...], g, m_ref[...], v_ref[...])

    tile = pl.BlockSpec((tr, tc), lambda i, k, chip_ref: (i, k))
    if lone_rows:
        tile = pl.BlockSpec((tr, 1, tc), lambda i, k, chip_ref: (i, 0, k))
    grid_spec = pltpu.PrefetchScalarGridSpec(
        num_scalar_prefetch=1, grid=(rows // tr, cols // tc),
        in_specs=[pl.BlockSpec((None, tr, tc), lambda i, k, chip_ref: (chip_ref[0], i, k)),
                  pl.BlockSpec((3, tr, tc), lambda i, k, chip_ref: (0, i, k)), tile, tile, tile],
        out_specs=[tile] * 4)
    return pl.pallas_call(
        body, name=name, grid_spec=grid_spec, out_shape=[_out_hbm(w.shape, F32)] * 4,
        compiler_params=_params("parallel", "parallel"),
    )(chip, *map(_in_hbm, (sums, recv, w, m, v)))


def _small_sum_adamw(me, entries, loss):
    def whole(shape, squeeze=0, pick=None):
        blk = (None,) * squeeze + tuple(shape[squeeze:])
        if pick is not None:
            blk = tuple(shape[:pick]) + (None,) + tuple(shape[pick + 1:])
            return pl.BlockSpec(blk, lambda i, me_ref: (0,) * pick + (me_ref[0],) + (0,) * (len(shape) - pick - 1))
        return pl.BlockSpec(blk, lambda i, me_ref: (0,) * len(shape))

    in_specs, out_specs, out_shape, args = [], [], [], []
    for own, parts, w, m, v, sharded in entries + [loss + (None, None, None, False)]:
        in_specs += [whole(own.shape, pick=0 if sharded else None), whole(parts.shape, pick=1 if sharded else None)]
        args += [own, parts]
        if w is not None:
            lead = w.ndim - (parts.ndim - (2 if sharded else 1))
            in_specs += [whole(w.shape, squeeze=lead)] * 3
            out_specs += [whole(w.shape, squeeze=lead)] * 4
            out_shape += [_out_hbm(w.shape, F32)] * 4
            args += [w, m, v]
    out_specs.append(whole(loss[0].shape))
    out_shape.append(_out_hbm(loss[0].shape, F32))
    n = len(entries)

    def added(own_ref, p_ref, me):
        total = None
        for d in range(N_DEV):
            part = jnp.where(me == d, own_ref[...], p_ref[d])
            total = part if total is None else total + part
        return total

    def body(me_ref, *refs):
        ins, outs = refs[:5 * n + 2], refs[5 * n + 2:]
        for e in range(n):
            own_ref, p_ref, w_ref, m_ref, v_ref = ins[5 * e:5 * e + 5]
            g_out, d_out, m_out, v_out = outs[4 * e:4 * e + 4]
            g = added(own_ref, p_ref, me_ref[0])
            g_out[...] = g
            d_out[...], m_out[...], v_out[...] = _adamw(w_ref[...], g, m_ref[...], v_ref[...])
        outs[4 * n][...] = added(ins[5 * n], ins[5 * n + 1], me_ref[0])

    grid_spec = pltpu.PrefetchScalarGridSpec(num_scalar_prefetch=1, grid=(1,), in_specs=in_specs, out_specs=out_specs)
    outs = pl.pallas_call(body, name="small_sum_adamw", grid_spec=grid_spec, out_shape=out_shape,
                          compiler_params=_params("arbitrary"))(me, *map(_in_hbm, args))
    return [outs[4 * e:4 * e + 4] for e in range(n)], outs[4 * n]


MM_TILE = 512
N_MM_TILES = SEQ // MM_TILE
CAT_TILE = 512
N_CAT_TILES = N_CAT // CAT_TILE
DZ_TILE = 640


def kernel(x, g_mix, w_in, b_gate, w_gk_up, b_gk, w_pool_grp, pool_scale, g_gla_head, w_pool_proj, w_gla_proj, w_out, g_ffn, w_up, w_conv, b_conv, w_down, g_final, loss_target, m_g_mix, m_w_in, m_b_gate, m_w_gk_up, m_b_gk, m_w_pool_grp, m_pool_scale, m_g_gla_head, m_w_pool_proj, m_w_gla_proj, m_w_out, m_g_ffn, m_w_up, m_w_conv, m_b_conv, m_w_down, m_g_final, v_g_mix, v_w_in, v_b_gate, v_w_gk_up, v_b_gk, v_w_pool_grp, v_pool_scale, v_g_gla_head, v_w_pool_proj, v_w_gla_proj, v_w_out, v_g_ffn, v_w_up, v_w_conv, v_b_conv, v_w_down, v_g_final):
    xi, yi, ci = lax.axis_index("x"), lax.axis_index("y"), lax.axis_index("c")
    me = 4 * xi + 2 * yi + ci
    core = jnp.reshape(ci, (1,)).astype(jnp.int32)
    chip = jnp.reshape(2 * xi + yi, (1,)).astype(jnp.int32)
    xs, target = x[0], loss_target[0]

    big = dict(w_in=w_in[0].T, w_pool_proj=w_pool_proj[0], w_gla_proj=w_gla_proj[0], w_out=w_out[0], w_up=w_up[0].T,
               w_down=w_down[0])
    moments = dict(w_in=(m_w_in[0].T, v_w_in[0].T), w_pool_proj=(m_w_pool_proj[0], v_w_pool_proj[0]),
                   w_gla_proj=(m_w_gla_proj[0], v_w_gla_proj[0]), w_out=(m_w_out[0], v_w_out[0]),
                   w_up=(m_w_up[0].T, v_w_up[0].T), w_down=(m_w_down[0], v_w_down[0]))
    names = list(big)
    shards = {k: big[k].astype(BF) for k in names}
    shards["w_gk_up"], shards["w_conv"] = w_gk_up[0], w_conv[0]
    gather_groups = (("w_in", "w_gk_up"), ("w_pool_proj", "w_gla_proj", "w_out"), ("w_up", "w_down", "w_conv"))
    started, token = _split_start("gather_start", [
        ([t for k in g for t in (shards[k], _gather_landing(shards[k], me))], 4 * len(g), _gather_first)
        for g in gather_groups])
    lone_rows = lambda t: jnp.transpose(t, (2, 0, 1))
    big["w_in"], moments["w_in"] = lone_rows(w_in), (lone_rows(m_w_in), lone_rows(v_w_in))
    (bconv4, m_w_conv, v_w_conv), token = _placed_behind(
        token, [b_conv.reshape(2, 4, 1, FF_BLK), m_w_conv, v_w_conv], "place_conv_operands")

    def gather_pass(gi, after):
        lands = list(_split_wait(f"gather_wait_{gi}", started[gi], _gather_first, after)[1::2])
        passed, tkn = _split_start(f"gather_pass_{gi}", [(lands, 3 * len(lands), _gather_second)])
        return passed[0], tkn

    def gather_done(gi, passed, after):
        return dict(zip(gather_groups[gi], _split_wait(f"gather_pass_wait_{gi}", passed, _gather_second, after)))

    tok = lambda i, j, k: (i, 0)
    whole = lambda i, j, k: (0, 0)
    kblk = lambda i, j, k: (k, 0)
    ff_seq = (None, None, SEQ, FF_BLK)

    h = _rms_fwd(xs, g_mix, token, "rms_mix")
    wg = gather_done(0, gather_pass(0, h)[0], h)
    wt_cat = _unshard_w_in(wg["w_in"])
    wgk_pad = jnp.pad(wg["w_gk_up"].transpose(1, 0, 2).reshape(GATE_RANK, GLA_DK), ((0, GK_PAD - GATE_RANK), (0, 0)))
    zcat = _mm(h, wt_cat, out_shape=(SEQ, N_CAT), out_dtype=BF, grid=(N_CAT_TILES, 1, 1),
               blk_a=(SEQ, D_MODEL), blk_b=(CAT_TILE, D_MODEL), blk_o=(SEQ, CAT_TILE),
               map_a=whole, map_b=lambda j, i, k: (j, 0), map_o=lambda j, i, k: (0, j), tb=True, name="mm_in")
    la = _gk_fwd(h, wt_cat, wgk_pad, b_gk)
    passed, tkn = gather_pass(1, la)
    o, states = _gla_fwd(zcat, la, tkn)
    wg = gather_done(1, passed, o)
    wpp = wg["w_pool_proj"].transpose(1, 0, 2).reshape(POOL_WIDTH, D_MODEL)
    wgp = wg["w_gla_proj"].reshape(D_MODEL, D_MODEL)
    wout = wg["w_out"].reshape(D_MODEL, D_MODEL)
    og = _post_gla_fwd(o, zcat, g_gla_head)
    ps = _pool_fwd(zcat, w_pool_grp[0], pool_scale)
    passed, tkn = gather_pass(2, (og, ps))
    y_pool, y_gla, mixed, x1, h2 = _mix_out_fwd(ps, og, zcat, xs, wpp, wgp, wout, b_gate, g_ffn, tkn)
    wg = gather_done(2, passed, h2)
    wt_up = wg["w_up"].reshape(2 * D_FF, D_MODEL)
    wdown = wg["w_down"].reshape(D_FF, D_MODEL)
    wconv4 = wg["w_conv"].reshape(2, 4, 3, FF_BLK)
    blk4 = lambda b, i, k: (b // 4, b % 4, 0, 0)
    u4, act = _up_conv_fwd(h2, wt_up, wconv4, bconv4)
    loss_part, dx2, dx2_bf, dg_final = _mm_tokens(
        act, wdown, blk_a=(None, 4, TOK_MM_TILE, FF_BLK), map_a=lambda i: (0, 0, i, 0),
        pieces=[(b, b * FF_BLK, FF_BLK) for b in range(4)], res=x1, then=("loss", g_final.reshape(1, D_MODEL), target),
        name="mm_down_loss")

    da = _mm(dx2_bf, wdown, out_shape=(1, 4, SEQ, FF_BLK), out_dtype=BF, grid=(4, 1, 1),
             blk_a=(SEQ, D_MODEL), blk_b=(FF_BLK, D_MODEL), blk_o=ff_seq,
             map_a=whole, map_b=lambda b, i, k: (b, 0), map_o=lambda b, i, k: (0, b, 0, 0), tb=True, name="mm_d_act")
    d_wdown = _mm(act, dx2_bf, out_shape=(D_FF, D_MODEL), out_dtype=BF, grid=(4, 1, 1),
                  blk_a=ff_seq, blk_b=(SEQ, D_MODEL), blk_o=(FF_BLK, D_MODEL),
                  map_a=lambda b, i, k: (0, b, 0, 0), map_b=whole, map_o=lambda b, i, k: (b, 0), ta=True,
                  name="mm_d_wdown")
    du4, d_wconv, d_bconv = _conv_bwd(u4, da, wconv4, bconv4)
    d_wt_up = _mm(du4, h2, out_shape=(2 * D_FF, D_MODEL), out_dtype=BF, grid=(N_DEV, 1, 1),
                  blk_a=ff_seq, blk_b=(SEQ, D_MODEL), blk_o=(FF_BLK, D_MODEL),
                  map_a=blk4, map_b=whole, map_o=lambda b, i, k: (b, 0), ta=True, name="mm_d_wup")
    res = {}

    def to_sibling(keys, parts):
        return [t for k in keys for t in (parts[k], lax.empty((4,) + parts[k].shape[2:], BF))], 4 * len(keys), _reduce_first

    def to_chips(keys, st, after):
        arrays = _split_wait("reduce_wait_" + keys[0], st, _reduce_first, after)
        sums = [_pair_sum(p, r, core, "pair_sum_" + k) for k, p, r in zip(keys, arrays[0::2], arrays[1::2])]
        return [t for s in sums for t in (s, lax.empty((3,) + s.shape[1:], BF))], 3 * len(keys), _reduce_second

    def reduce_start(keys, parts):
        st, tkn = _split_start("reduce_start_" + keys[0], [to_sibling(keys, parts)])
        return st[0], tkn

    def reduce_cross(keys, st, after):
        st2, tkn = _split_start("reduce_cross_" + keys[0], [to_chips(keys, st, after)])
        return st2[0], tkn

    def reduce_done(keys, st2, after):
        arrays = _split_wait("reduce_cross_wait_" + keys[0], st2, _reduce_second, after)
        for k, s, r in zip(keys, arrays[0::2], arrays[1::2]):
            outs = _chip_sum_adamw(s, r, big[k], moments[k][0], moments[k][1], chip, "adamw_" + k)
            res[k] = [jnp.transpose(t, (1, 2, 0)) if k == "w_in" else (t.T if k == "w_up" else t)[None] for t in outs]

    ffn_keys = ("w_down", "w_up")
    ffn_red, tkn = reduce_start(ffn_keys, dict(w_down=d_wdown.reshape(4, 2, D_FF // N_DEV, D_MODEL),
                                               w_up=d_wt_up.reshape(4, 2, FF_BLK, D_MODEL)))
    dx1, dg_ffn = _mm_tokens(
        du4, wt_up, blk_a=(2, 4, TOK_MM_TILE, FF_BLK), map_a=lambda i: (0, 0, i, 0),
        pieces=[((b // 4, b % 4), b * FF_BLK, FF_BLK) for b in range(N_DEV)], after=tkn, then=("rms_bwd", x1, g_ffn, dx2),
        name="mm_d_h2_rms")

    sq_t = dict(out_shape=(D_MODEL, D_MODEL), grid=(1, 1, N_MM_TILES), blk_a=(MM_TILE, D_MODEL),
                blk_b=(MM_TILE, D_MODEL), blk_o=(D_MODEL, D_MODEL), map_a=kblk, map_b=kblk, map_o=whole, ta=True)
    d_wout = _mm(mixed, dx1, out_dtype=BF, name="mm_d_wout", **sq_t)
    dzcat, dy_pool, dy_gla, db_gate = _mix_bwd(dx1, wout, zcat, b_gate, y_pool, y_gla)
    d_wgp = _mm(og, dy_gla, out_dtype=BF, name="mm_d_wgp", **sq_t)
    mix_keys = ("w_out", "w_gla_proj")
    (ffn_red, mix_red), tkn = _split_start("reduce_cross_w_down", [
        to_chips(ffn_keys, ffn_red, db_gate),
        to_sibling(mix_keys, dict(w_out=d_wout.reshape(4, 2, D_MODEL // N_DEV, D_MODEL),
                                  w_gla_proj=d_wgp.reshape(4, 2, D_MODEL // N_DEV, D_MODEL)))])
    dzcat, d_o, dg_head = _post_gla_bwd(dzcat, dy_gla, wgp, o, zcat, g_gla_head, tkn)
    dzcat, dla = _gla_bwd(dzcat, zcat, la, d_o, states)
    dzcat, d_wgk, db_gk = _gk_bwd(dzcat, dla, h, wt_cat, wgk_pad, b_gk)
    dps = _mm(dy_pool, wpp, out_shape=(SEQ, POOL_WIDTH), out_dtype=F32, grid=(N_MM_TILES, 1, 1),
              blk_a=(MM_TILE, D_MODEL), blk_b=(POOL_WIDTH, D_MODEL), blk_o=(MM_TILE, POOL_WIDTH),
              map_a=tok, map_b=whole, map_o=tok, tb=True, name="mm_d_ps")
    d_wpp = _mm(ps, dy_pool, out_shape=(POOL_WIDTH, D_MODEL), out_dtype=F32, grid=(1, 1, N_MM_TILES),
                blk_a=(MM_TILE, POOL_WIDTH), blk_b=(MM_TILE, D_MODEL), blk_o=(POOL_WIDTH, D_MODEL),
                map_a=kblk, map_b=kblk, map_o=whole, ta=True, name="mm_d_wpp")
    dzcat, d_wgrp, d_scale = _pool_bwd(dzcat, zcat, dps, w_pool_grp[0], pool_scale)
    row = lambda t: t.reshape(1, D_MODEL)
    small = [("b_gate", db_gate, b_gate, m_b_gate, v_b_gate, False),
             ("w_gk_up", d_wgk.reshape(GATE_RANK, N_DEV, GLA_DK // N_DEV).transpose(1, 0, 2), w_gk_up, m_w_gk_up,
              v_w_gk_up, True),
             ("b_gk", db_gk, b_gk, m_b_gk, v_b_gk, False),
             ("w_pool_grp", d_wgrp, w_pool_grp, m_w_pool_grp, v_w_pool_grp, False),
             ("pool_scale", d_scale, pool_scale, m_pool_scale, v_pool_scale, False),
             ("g_gla_head", dg_head, g_gla_head, m_g_gla_head, v_g_gla_head, False),
             ("g_ffn", dg_ffn, g_ffn, m_g_ffn, v_g_ffn, False),
             ("w_conv", d_wconv.reshape(N_DEV, 3, FF_BLK), w_conv, m_w_conv, v_w_conv, True),
             ("b_conv", d_bconv.reshape(b_conv.shape), b_conv, m_b_conv, v_b_conv, False),
             ("g_final", dg_final, row(g_final), row(m_g_final), row(v_g_final), False)]

    def to_all(parts):
        return ([t for p in parts for t in (p, lax.empty((N_DEV,) + p.shape, p.dtype))], 7 * len(parts),
                _gather_direct)

    (small_sent, mix_red), tkn = _split_start("small_start", [to_all([t[1] for t in small] + [loss_part]),
                                                              to_chips(mix_keys, mix_red, dla)])
    d_wt_cat = _mm(dzcat, h, out_shape=(N_DZ, D_MODEL), out_dtype=BF, grid=(N_DZ // DZ_TILE, 1, 1),
                   blk_a=(SEQ, DZ_TILE), blk_b=(SEQ, D_MODEL), blk_o=(DZ_TILE, D_MODEL),
                   map_a=lambda j, i, k: (0, j), map_b=whole, map_o=lambda j, i, k: (j, 0), ta=True, after=tkn,
                   name="mm_d_wcat")
    in_keys = ("w_in", "w_pool_proj")
    in_red, tkn = reduce_start(in_keys, dict(
        w_in=_shard_d_w_in(d_wt_cat).reshape(4, 2, IN_SHARD, D_MODEL),
        w_pool_proj=d_wpp.reshape(POOL_WIDTH, N_DEV, D_MODEL // N_DEV).transpose(1, 0, 2).astype(BF)
        .reshape(4, 2, POOL_WIDTH, D_MODEL // N_DEV)))
    reduce_done(mix_keys, mix_red, tkn)
    in_red, tkn = reduce_cross(in_keys, in_red, res["w_out"][0])
    grad_x, dg_mix = _mm_tokens(dzcat, wt_cat, blk_a=(TOK_MM_TILE, N_DZ), map_a=lambda i: (i, 0),
                                pieces=[(None, 0, N_DZ)], after=tkn, then=("rms_bwd", xs, g_mix, dx1),
                                name="mm_d_h_rms")
    (g_mix_sent,), tkn = _split_start("g_mix_start", [to_all([dg_mix])])
    reduce_done(ffn_keys, ffn_red, (grad_x, tkn))
    sent = list(_split_wait("small_wait", small_sent, _gather_direct, res["w_down"][0]))
    small.append(("g_mix", dg_mix, g_mix, m_g_mix, v_g_mix, False))
    sent[-2:-2] = _split_wait("g_mix_wait", g_mix_sent, _gather_direct, sent[1])
    own, gathered = sent[0::2], sent[1::2]
    small_out, loss_sum = _small_sum_adamw(jnp.reshape(me, (1,)).astype(jnp.int32),
                                           [(o, p) + t[2:] for o, p, t in zip(own, gathered, small)],
                                           (own[-1], gathered[-1]))
    for t, outs in zip(small, small_out):
        res[t[0]] = list(outs)
    res["g_final"] = [t.reshape(g_final.shape) for t in res["g_final"]]

    reduce_done(in_keys, in_red, loss_sum)
    loss = loss_sum[0, 0]
    order =["g_mix", "w_in", "b_gate", "w_gk_up", "b_gk", "w_pool_grp", "pool_scale", "g_gla_head", "w_pool_proj",
             "w_gla_proj", "w_out", "g_ffn", "w_up", "w_conv", "b_conv", "w_down", "g_final"]
    return (loss, grad_x[None], *[res[k][0] for k in order], *[res[k][1] for k in order],
            *[res[k][2] for k in order], *[res[k][3] for k in order])
```

```python
import jax
import jax.numpy as jnp
from jax import lax
from jax.experimental import pallas as pl
from jax.experimental.pallas import tpu as pltpu

F32 = jnp.float32
BF = jnp.bfloat16
HIGHEST = lax.Precision.HIGHEST
MESH = pl.DeviceIdType.MESH

N_DEV = 8
SEQ = 2048
D_MODEL = 1024
CHUNK = 64
EPS = 1e-6
POOL_WIDTH = 512
POOL_WINDOWS = (2, 4, 8, 16)
POOL_GD = 128
POOL_HALO = 16
HEADS = 4
HK = 128
HV = 256
GLA_DK = 512
GATE_RANK = 16
GATE_NORM = 16.0
D_FF = 2816
FF_BLK = 704
IN_SHARD = 706
C_QKV, C_GATE, C_OG, C_POOL, C_GK = 0, 2048, 4096, 5120, 5632
N_CAT = 5632
GK_PAD = 128
N_DZ = N_CAT + GK_PAD
R_POOL, R_QKV, R_OG, R_GK, R_GATE = 0, 512, 2560, 3584, 3600

ADAM_LR, ADAM_B1, ADAM_B2, ADAM_EPS, ADAM_WD, ADAM_STEP = 0.001, 0.9, 0.999, 1e-08, 0.01, 10
ADAM_C1 = 1.0 - ADAM_B1 ** ADAM_STEP
ADAM_C2 = 1.0 - ADAM_B2 ** ADAM_STEP

VMEM_BYTES_V7X = 64 * 1024 * 1024
VMEM_LIMIT = VMEM_BYTES_V7X * 3 // 4

TOK_TILE = 256
HALO = 8
GLA_CPS = 4


def _params(*sem):
    return pltpu.CompilerParams(dimension_semantics=sem, vmem_limit_bytes=VMEM_LIMIT)


def _const_spec(shape):
    nd = len(shape)
    return pl.BlockSpec(shape, lambda *_: (0,) * nd)


def _in_hbm(t):
    return pltpu.with_memory_space_constraint(t, pltpu.HBM)


def _out_hbm(shape, dtype):
    return pltpu.HBM(shape, dtype)


def _dot(a, b, ta=False, tb=False):
    dims = (((0 if ta else 1,), (1 if tb else 0,)), ((), ()))
    return lax.dot_general(a.astype(BF), b.astype(BF), dims, preferred_element_type=F32)


def _dot_exact(a, b):
    return jnp.dot(a, b, precision=HIGHEST, preferred_element_type=F32)


def _sigmoid(x):
    return 0.5 * jnp.tanh(0.5 * x) + 0.5


def _mm(a, b, *, out_shape, out_dtype, grid, blk_a, blk_b, blk_o, map_a, map_b, map_o, ta=False, tb=False,
        after=None, name):
    gk = grid[2]
    n_in = 2 + (after is not None)

    def body(*refs):
        a_ref, b_ref, o_ref = refs[0], refs[1], refs[n_in]
        prod = _dot(a_ref[...], b_ref[...], ta, tb)
        if gk == 1:
            o_ref[...] = prod.astype(out_dtype)
        else:
            acc = refs[n_in + 1]
            k = pl.program_id(2)

            @pl.when(k == 0)
            def _():
                acc[...] = prod

            @pl.when(k > 0)
            def _():
                acc[...] += prod

            @pl.when(k == gk - 1)
            def _():
                o_ref[...] = acc[...].astype(out_dtype)

    in_specs = [pl.BlockSpec(blk_a, map_a), pl.BlockSpec(blk_b, map_b)]
    args = [_in_hbm(a), _in_hbm(b)]
    if after is not None:
        in_specs.append(pl.BlockSpec(memory_space=pl.ANY))
        args.append(after)
    return pl.pallas_call(
        body, name=name, grid=grid, in_specs=in_specs, out_specs=pl.BlockSpec(blk_o, map_o),
        out_shape=_out_hbm(out_shape, out_dtype),
        scratch_shapes=[] if gk == 1 else [pltpu.VMEM(tuple(d for d in blk_o if d is not None), F32)],
        compiler_params=_params("parallel", "parallel", "arbitrary"),
    )(*args)


TOK_MM_TILE = 256


def _mm_tokens(a, w, *, blk_a, map_a, pieces, res=None, after=None, then=None, name):
    n_in = 2 + (res is not None) + (after is not None) + (0 if then is None else len(then) - 1)

    def accumulate(ref, part):
        @pl.when(pl.program_id(0) == 0)
        def _():
            ref[...] = part

        @pl.when(pl.program_id(0) > 0)
        def _():
            ref[...] += part

    def body(*refs):
        a_ref, w_ref = refs[:2]
        extra, outs = refs[n_in - (0 if then is None else len(then) - 1):n_in], refs[n_in:]
        total = None
        for idx, row, n in pieces:
            av = a_ref[...] if idx is None else a_ref[idx]
            prod = _dot(av, w_ref[row:row + n, :])
            total = prod if total is None else total + prod
        if res is not None:
            total = total + refs[2][...]
        if then is None:
            outs[0][...] = total
        elif then[0] == "rms_bwd":
            dx, part = _rms_bwd_tile(total, extra[0][...], extra[1][...], extra[2][...])
            outs[0][...] = dx
            accumulate(outs[1], part)
        else:
            lpart, dx, part = _loss_tile(total, extra[0][...], extra[1][...])
            outs[1][...] = dx
            outs[2][...] = dx.astype(BF)
            accumulate(outs[0], lpart)
            accumulate(outs[3], part)

    tile = pl.BlockSpec((TOK_MM_TILE, D_MODEL), lambda i: (i, 0))
    vec = _const_spec((1, D_MODEL))
    big = _out_hbm((SEQ, D_MODEL), F32)
    small = _out_hbm((1, D_MODEL), F32)
    in_specs = [pl.BlockSpec(blk_a, map_a), pl.BlockSpec(w.shape, lambda i: (0, 0), pipeline_mode=pl.Buffered(1))]
    args = [a, w]
    if res is not None:
        in_specs.append(tile)
        args.append(res)
    if after is not None:
        in_specs.append(pl.BlockSpec(memory_space=pl.ANY))
        args.append(after)
    if then is None:
        out_specs, out_shape = tile, big
    elif then[0] == "rms_bwd":
        in_specs += [tile, vec, tile]
        out_specs, out_shape = [tile, vec], [big, small]
    else:
        in_specs += [vec, tile]
        out_specs = [_const_spec((1, 128)), tile, tile, vec]
        out_shape = [_out_hbm((1, 128), F32), big, _out_hbm((SEQ, D_MODEL), BF), small]
    if then is not None:
        args += list(then[1:])
    return pl.pallas_call(
        body, name=name, grid=(SEQ // TOK_MM_TILE,), in_specs=in_specs, out_specs=out_specs, out_shape=out_shape,
        compiler_params=_params("parallel" if then is None else "arbitrary"),
    )(*[_in_hbm(t) for t in args])


def _rms_fwd(x, g, after, name):
    def body(x_ref, g_ref, after_ref, o_ref):
        del after_ref
        xv = x_ref[...]
        r = lax.rsqrt(jnp.mean(xv * xv, axis=-1, keepdims=True) + EPS)
        o_ref[...] = (xv * r * g_ref[...]).astype(BF)

    tile = pl.BlockSpec((TOK_TILE, D_MODEL), lambda i: (i, 0))
    return pl.pallas_call(
        body, name=name, grid=(SEQ // TOK_TILE,),
        in_specs=[tile, _const_spec((1, D_MODEL)), pl.BlockSpec(memory_space=pl.ANY)], out_specs=tile,
        out_shape=_out_hbm((SEQ, D_MODEL), BF), compiler_params=_params("parallel"),
    )(*map(_in_hbm, (x, g)), after)


def _rms_bwd_tile(dyv, xv, gv, dresv):
    r = lax.rsqrt(jnp.mean(xv * xv, axis=-1, keepdims=True) + EPS)
    xn = xv * r
    dxn = dyv * gv
    return dresv + r * (dxn - xn * jnp.mean(dxn * xn, axis=-1, keepdims=True)), jnp.sum(dyv * xn, axis=0, keepdims=True)


def _loss_tile(xv, gv, tv):
    r = lax.rsqrt(jnp.mean(xv * xv, axis=-1, keepdims=True) + EPS)
    xn = xv * r
    err = xn * gv - tv
    lpart = jnp.full((1, 128), 0.5 * jnp.sum(jnp.mean(err * err, axis=-1, keepdims=True)), F32)
    dyv = err * (1.0 / D_MODEL)
    dxn = dyv * gv
    return lpart, r * (dxn - xn * jnp.mean(dxn * xn, axis=-1, keepdims=True)), jnp.sum(dyv * xn, axis=0, keepdims=True)


def _pool_counts(w):
    pos = lax.broadcasted_iota(jnp.int32, (SEQ, 1), 0).astype(F32)
    return jnp.minimum(pos + 1.0, float(w))


def _pool_window(u, w, ext):
    ext[pl.ds(POOL_HALO, SEQ), :] = u
    win = u
    for j in range(1, w):
        win = win + ext[pl.ds(POOL_HALO - j, SEQ), :]
    return win / _pool_counts(w) - u


def _pool_fwd(zcat, w_grp, scale):
    def body(z_ref, w_ref, s_ref, o_ref, ext):
        ext[pl.ds(0, POOL_HALO), :] = jnp.zeros((POOL_HALO, POOL_GD), F32)
        for g, w in enumerate(POOL_WINDOWS):
            cols = slice(g * POOL_GD, (g + 1) * POOL_GD)
            p = _pool_window(z_ref[:, cols].astype(F32), w, ext)
            o_ref[:, cols] = (_dot(p, w_ref[g]) * s_ref[:, cols]).astype(BF)

    return pl.pallas_call(
        body, name="pool_fwd", grid=(1,),
        in_specs=[pl.BlockSpec((SEQ, POOL_WIDTH), lambda i: (0, C_POOL // POOL_WIDTH)),
                  _const_spec((4, POOL_GD, POOL_GD)), _const_spec((1, POOL_WIDTH))],
        out_specs=_const_spec((SEQ, POOL_WIDTH)), out_shape=_out_hbm((SEQ, POOL_WIDTH), BF),
        scratch_shapes=[pltpu.VMEM((POOL_HALO + SEQ, POOL_GD), F32)], compiler_params=_params("arbitrary"),
    )(*map(_in_hbm, (zcat, w_grp, scale)))


def _pool_bwd(dzcat, zcat, dps, w_grp, scale):
    def body(dz_in, z_ref, dps_ref, w_ref, s_ref, dz_ref, dw_ref, dsc_ref, ext, ext2):
        del dz_in
        ext[pl.ds(0, POOL_HALO), :] = jnp.zeros((POOL_HALO, POOL_GD), F32)
        ext2[pl.ds(SEQ, POOL_HALO), :] = jnp.zeros((POOL_HALO, POOL_GD), F32)
        for g, w in enumerate(POOL_WINDOWS):
            cols = slice(g * POOL_GD, (g + 1) * POOL_GD)
            p = _pool_window(z_ref[:, cols].astype(F32), w, ext)
            wg = w_ref[g]
            pg = _dot(p, wg)
            dpsv = dps_ref[:, cols]
            dsc_ref[:, cols] = jnp.sum(dpsv * pg, axis=0, keepdims=True)
            dpg = dpsv * s_ref[:, cols]
            dw_ref[g] = _dot(p, dpg, ta=True)
            dp = _dot(dpg, wg, tb=True)
            dpc = dp / _pool_counts(w)
            ext2[pl.ds(0, SEQ), :] = dpc
            du = dpc
            for j in range(1, w):
                du = du + ext2[pl.ds(j, SEQ), :]
            dz_ref[:, cols] = (du - dp).astype(BF)

    return pl.pallas_call(
        body, name="pool_bwd", grid=(1,),
        in_specs=[pl.BlockSpec(memory_space=pl.ANY),
                  pl.BlockSpec((SEQ, POOL_WIDTH), lambda i: (0, C_POOL // POOL_WIDTH)),
                  _const_spec((SEQ, POOL_WIDTH)), _const_spec((4, POOL_GD, POOL_GD)), _const_spec((1, POOL_WIDTH))],
        out_specs=[pl.BlockSpec((SEQ, POOL_WIDTH), lambda i: (0, C_POOL // POOL_WIDTH)),
                   _const_spec((4, POOL_GD, POOL_GD)), _const_spec((1, POOL_WIDTH))],
        out_shape=[_out_hbm((SEQ, N_DZ), BF), _out_hbm((4, POOL_GD, POOL_GD), F32),
                   _out_hbm((1, POOL_WIDTH), F32)],
        scratch_shapes=[pltpu.VMEM((POOL_HALO + SEQ, POOL_GD), F32), pltpu.VMEM((SEQ + POOL_HALO, POOL_GD), F32)],
        input_output_aliases={0: 0}, compiler_params=_params("arbitrary"),
    )(*map(_in_hbm, (dzcat, zcat, dps, w_grp, scale)))


GK_TILE = 512


GK_ROWS = pl.BlockSpec((GK_PAD, D_MODEL), lambda i: (C_GK // GK_PAD, 0))


def _gk_fwd(h, wt_cat, wgk_pad, b_gk):
    def body(h_ref, wt_ref, w_ref, b_ref, la_ref):
        z_gk = _dot(h_ref[...], wt_ref[...], tb=True)
        pre = _dot(z_gk, w_ref[...]) + b_ref[...]
        la_ref[...] = (jnp.minimum(pre, 0.0) - jnp.log(1.0 + jnp.exp(-jnp.abs(pre)))) * (1.0 / GATE_NORM)

    return pl.pallas_call(
        body, name="gk_fwd", grid=(SEQ // GK_TILE,),
        in_specs=[pl.BlockSpec((GK_TILE, D_MODEL), lambda i: (i, 0)), GK_ROWS,
                  _const_spec((GK_PAD, GLA_DK)), _const_spec((1, GLA_DK))],
        out_specs=pl.BlockSpec((GK_TILE, GLA_DK), lambda i: (i, 0)),
        out_shape=_out_hbm((SEQ, GLA_DK), F32), compiler_params=_params("parallel"),
    )(*map(_in_hbm, (h, wt_cat, wgk_pad, b_gk)))


def _gk_bwd(dzcat, dla, h, wt_cat, wgk_pad, b_gk):
    def body(dz_in, dla_ref, h_ref, wt_ref, w_ref, b_ref, dz_ref, dw_ref, db_ref):
        del dz_in
        wv = w_ref[...]
        z_gk = _dot(h_ref[...], wt_ref[...], tb=True)
        pre = _dot(z_gk, wv) + b_ref[...]
        dpre = dla_ref[...] * (1.0 / GATE_NORM) * (1.0 - _sigmoid(pre))
        dz_ref[...] = _dot(dpre, wv, tb=True).astype(BF)
        dwp = _dot(z_gk, dpre, ta=True)[:GATE_RANK]
        dbp = jnp.sum(dpre, axis=0, keepdims=True)

        @pl.when(pl.program_id(0) == 0)
        def _():
            dw_ref[...] = dwp
            db_ref[...] = dbp

        @pl.when(pl.program_id(0) > 0)
        def _():
            dw_ref[...] += dwp
            db_ref[...] += dbp

    return pl.pallas_call(
        body, name="gk_bwd", grid=(SEQ // GK_TILE,),
        in_specs=[pl.BlockSpec(memory_space=pl.ANY), pl.BlockSpec((GK_TILE, GLA_DK), lambda i: (i, 0)),
                  pl.BlockSpec((GK_TILE, D_MODEL), lambda i: (i, 0)), GK_ROWS, _const_spec((GK_PAD, GLA_DK)),
                  _const_spec((1, GLA_DK))],
        out_specs=[pl.BlockSpec((GK_TILE, GK_PAD), lambda i: (i, C_GK // GK_PAD)), _const_spec((GATE_RANK, GLA_DK)),
                   _const_spec((1, GLA_DK))],
        out_shape=[_out_hbm((SEQ, N_DZ), BF), _out_hbm((GATE_RANK, GLA_DK), F32),
                   _out_hbm((1, GLA_DK), F32)],
        input_output_aliases={0: 0}, compiler_params=_params("arbitrary"),
    )(*map(_in_hbm, (dzcat, dla, h, wt_cat, wgk_pad, b_gk)))


GLA_ROWS = GLA_CPS * CHUNK
GLA_STEPS = SEQ // GLA_ROWS
QKV_W = 2048


def _tri():
    return lax.broadcasted_iota(jnp.int32, (CHUNK, CHUNK), 0) >= lax.broadcasted_iota(jnp.int32, (CHUNK, CHUNK), 1)


def _chunk_cumsum(la_ref, rows):
    return _dot_exact(_tri().astype(F32), la_ref[rows, :])


def _gla_chunk(qkv_ref, la_ref, rows, h, bc_all):
    tri = _tri()
    q = qkv_ref[rows, h * HK:(h + 1) * HK].astype(F32) * (HK ** -0.5)
    k = qkv_ref[rows, GLA_DK + h * HK:GLA_DK + (h + 1) * HK].astype(F32)
    v = qkv_ref[rows, 2 * GLA_DK + h * HV:2 * GLA_DK + (h + 1) * HV].astype(BF)
    la = la_ref[rows, h * HK:(h + 1) * HK]
    bc = bc_all[:, h * HK:(h + 1) * HK]
    e_pos, e_neg = jnp.exp(bc), jnp.exp(-bc)
    dl = jnp.exp(jnp.sum(la, axis=0, keepdims=True))
    q_fw, q_bw, k_fw, k_bw = q * e_pos, q * e_neg, k * e_neg, k * e_pos
    scores = jnp.where(tri, _dot(q_fw, k_fw, tb=True), _dot(q_bw, k_bw, tb=True))
    return tri, v, e_pos, e_neg, dl, q_fw, q_bw, k_fw, k_bw, scores


def _gla_fwd(zcat, la, after):
    def body(qkv_ref, la_ref, after_ref, o_ref, st_ref, state):
        del after_ref

        @pl.when(pl.program_id(0) == 0)
        def _():
            state[...] = jnp.zeros_like(state)

        for c in range(GLA_CPS):
            rows = slice(c * CHUNK, (c + 1) * CHUNK)
            bc_all = _chunk_cumsum(la_ref, rows)
            for h in range(HEADS):
                _, v, _, _, dl, q_fw, _, k_fw, _, scores = _gla_chunk(qkv_ref, la_ref, rows, h, bc_all)
                st = state[h]
                st_ref[c, h] = st
                o_ref[rows, h * HV:(h + 1) * HV] = _dot(scores, v) + _dot(q_fw, st, tb=True)
                state[h] = st * dl + _dot(v, k_fw * dl, ta=True)

    return pl.pallas_call(
        body, name="gla_fwd", grid=(GLA_STEPS,),
        in_specs=[pl.BlockSpec((GLA_ROWS, QKV_W), lambda i: (i, 0)), pl.BlockSpec((GLA_ROWS, GLA_DK), lambda i: (i, 0)),
                  pl.BlockSpec(memory_space=pl.ANY)],
        out_specs=[pl.BlockSpec((GLA_ROWS, D_MODEL), lambda i: (i, 0)),
                   pl.BlockSpec((GLA_CPS, HEADS, HV, HK), lambda i: (i, 0, 0, 0))],
        out_shape=[_out_hbm((SEQ, D_MODEL), F32),
                   _out_hbm((SEQ // CHUNK, HEADS, HV, HK), F32)],
        scratch_shapes=[pltpu.VMEM((HEADS, HV, HK), F32)], compiler_params=_params("arbitrary"),
    )(*map(_in_hbm, (zcat, la)), after)


def _gla_bwd(dzcat, zcat, la, d_o, states):
    def body(dz_in, qkv_ref, la_ref, do_ref, st_ref, dqkv_ref, dla_ref, dstate):
        del dz_in

        @pl.when(pl.program_id(0) == 0)
        def _():
            dstate[...] = jnp.zeros_like(dstate)

        last_row = lax.broadcasted_iota(jnp.int32, (CHUNK, HK), 0) == CHUNK - 1
        upper = (lax.broadcasted_iota(jnp.int32, (CHUNK, CHUNK), 0)
                 <= lax.broadcasted_iota(jnp.int32, (CHUNK, CHUNK), 1)).astype(F32)
        for c in reversed(range(GLA_CPS)):
            rows = slice(c * CHUNK, (c + 1) * CHUNK)
            bc_all = _chunk_cumsum(la_ref, rows)
            dbs = []
            for h in range(HEADS):
                tri, v, e_pos, e_neg, dl, q_fw, q_bw, k_fw, k_bw, scores = _gla_chunk(qkv_ref, la_ref, rows, h, bc_all)
                st = st_ref[c, h]
                dst = dstate[h]
                d_out = do_ref[rows, h * HV:(h + 1) * HV].astype(BF)
                k_dec = k_fw * dl
                dp = _dot(d_out, v, tb=True)
                dp_fw = jnp.where(tri, dp, 0.0)
                dp_bw = jnp.where(tri, 0.0, dp)
                dv = _dot(scores, d_out, ta=True) + _dot(k_dec, dst, tb=True)
                dk_dec = _dot(v, dst)
                dq_fw = _dot(dp_fw, k_fw) + _dot(d_out, st)
                dk_fw = _dot(dp_fw, q_fw, ta=True) + dk_dec * dl
                dq_bw = _dot(dp_bw, k_bw)
                dk_bw = _dot(dp_bw, q_bw, ta=True)
                ddl = jnp.sum(st * dst, axis=0, keepdims=True) + jnp.sum(k_fw * dk_dec, axis=0, keepdims=True)
                dstate[h] = dst * dl + _dot(d_out, q_fw, ta=True)
                dq = (dq_fw * e_pos + dq_bw * e_neg) * (HK ** -0.5)
                dk = dk_fw * e_neg + dk_bw * e_pos
                dbs.append(dq_fw * q_fw - dk_fw * k_fw - dq_bw * q_bw + dk_bw * k_bw + jnp.where(last_row, ddl * dl, 0.0))
                dqkv_ref[rows, h * HK:(h + 1) * HK] = dq.astype(BF)
                dqkv_ref[rows, GLA_DK + h * HK:GLA_DK + (h + 1) * HK] = dk.astype(BF)
                dqkv_ref[rows, 2 * GLA_DK + h * HV:2 * GLA_DK + (h + 1) * HV] = dv.astype(BF)
            dla_ref[rows, :] = _dot_exact(upper, jnp.concatenate(dbs, axis=1))

    rev = lambda i: (GLA_STEPS - 1 - i, 0)
    return pl.pallas_call(
        body, name="gla_bwd", grid=(GLA_STEPS,),
        in_specs=[pl.BlockSpec(memory_space=pl.ANY), pl.BlockSpec((GLA_ROWS, QKV_W), rev),
                  pl.BlockSpec((GLA_ROWS, GLA_DK), rev), pl.BlockSpec((GLA_ROWS, D_MODEL), rev),
                  pl.BlockSpec((GLA_CPS, HEADS, HV, HK), lambda i: (GLA_STEPS - 1 - i, 0, 0, 0))],
        out_specs=[pl.BlockSpec((GLA_ROWS, QKV_W), rev), pl.BlockSpec((GLA_ROWS, GLA_DK), rev)],
        out_shape=[_out_hbm((SEQ, N_DZ), BF), _out_hbm((SEQ, GLA_DK), F32)],
        scratch_shapes=[pltpu.VMEM((HEADS, HV, HK), F32)], input_output_aliases={0: 0},
        compiler_params=_params("arbitrary"),
    )(*map(_in_hbm, (dzcat, zcat, la, d_o, states)))


def _silu_parts(x):
    s = _sigmoid(x)
    return x * s, s * (1.0 + x * (1.0 - s))


def _post_gla_fwd(o, zcat, g_head):
    def body(o_ref, zog_ref, g_ref, out_ref):
        for h in range(HEADS):
            cols = slice(h * HV, (h + 1) * HV)
            ov = o_ref[:, cols]
            r = lax.rsqrt(jnp.mean(ov * ov, axis=-1, keepdims=True) + EPS)
            act, _ = _silu_parts(zog_ref[:, cols].astype(F32))
            out_ref[:, cols] = (ov * r * g_ref[...] * act).astype(BF)

    tile = pl.BlockSpec((TOK_TILE, D_MODEL), lambda i: (i, 0))
    return pl.pallas_call(
        body, name="post_gla_fwd", grid=(SEQ // TOK_TILE,),
        in_specs=[tile, pl.BlockSpec((TOK_TILE, D_MODEL), lambda i: (i, C_OG // D_MODEL)), _const_spec((1, HV))],
        out_specs=tile, out_shape=_out_hbm((SEQ, D_MODEL), BF), compiler_params=_params("parallel"),
    )(*map(_in_hbm, (o, zcat, g_head)))


def _post_gla_bwd(dzcat, dy_gla, w_gla_proj, o, zcat, g_head, after):
    def body(dz_in, dyg_ref, w_ref, o_ref, zog_ref, g_ref, after_ref, dz_ref, do_ref, dg_ref):
        del dz_in, after_ref
        dog = _dot(dyg_ref[...], w_ref[...], tb=True)
        gpart = jnp.zeros((1, HV), F32)
        gv = g_ref[...]
        for h in range(HEADS):
            cols = slice(h * HV, (h + 1) * HV)
            ov = o_ref[:, cols]
            r = lax.rsqrt(jnp.mean(ov * ov, axis=-1, keepdims=True) + EPS)
            on = ov * r
            act, dact = _silu_parts(zog_ref[:, cols].astype(F32))
            dogv = dog[:, cols]
            dz_ref[:, cols] = (dogv * on * gv * dact).astype(BF)
            d_on_g = dogv * act
            gpart = gpart + jnp.sum(d_on_g * on, axis=0, keepdims=True)
            dxn = d_on_g * gv
            do_ref[:, cols] = (r * (dxn - on * jnp.mean(dxn * on, axis=-1, keepdims=True))).astype(BF)

        @pl.when(pl.program_id(0) == 0)
        def _():
            dg_ref[...] = gpart

        @pl.when(pl.program_id(0) > 0)
        def _():
            dg_ref[...] += gpart

    tile = pl.BlockSpec((TOK_TILE, D_MODEL), lambda i: (i, 0))
    ogspec = pl.BlockSpec((TOK_TILE, D_MODEL), lambda i: (i, C_OG // D_MODEL))
    return pl.pallas_call(
        body, name="post_gla_bwd", grid=(SEQ // TOK_TILE,),
        in_specs=[pl.BlockSpec(memory_space=pl.ANY), tile, _const_spec((D_MODEL, D_MODEL)), tile, ogspec,
                  _const_spec((1, HV)), pl.BlockSpec(memory_space=pl.ANY)],
        out_specs=[ogspec, tile, _const_spec((1, HV))],
        out_shape=[_out_hbm((SEQ, N_DZ), BF), _out_hbm((SEQ, D_MODEL), BF),
                   _out_hbm((1, HV), F32)],
        input_output_aliases={0: 0}, compiler_params=_params("arbitrary"),
    )(*map(_in_hbm, (dzcat, dy_gla, w_gla_proj, o, zcat, g_head)), after)


GATE_W = 2 * D_MODEL


def _mix_out_fwd(ps, og, zcat, x, w_pool_proj, w_gla_proj, w_out, b_gate, g_ffn, after):
    def body(ps_ref, og_ref, zg_ref, x_ref, wpp_ref, wgp_ref, wout_ref, b_ref, g_ref, after_ref,
             yp_ref, yg_ref, mixed_ref, x1_ref, h2_ref):
        del after_ref
        y_pool = _dot(ps_ref[...], wpp_ref[...])
        y_gla = _dot(og_ref[...], wgp_ref[...])
        yp_ref[...] = y_pool.astype(BF)
        yg_ref[...] = y_gla.astype(BF)
        g0 = _sigmoid(zg_ref[:, :D_MODEL].astype(F32) + b_ref[:, :D_MODEL])
        g1 = _sigmoid(zg_ref[:, D_MODEL:].astype(F32) + b_ref[:, D_MODEL:])
        mixed = (g0 * y_pool + g1 * y_gla).astype(BF)
        mixed_ref[...] = mixed
        x1 = x_ref[...] + _dot(mixed, wout_ref[...])
        x1_ref[...] = x1
        r = lax.rsqrt(jnp.mean(x1 * x1, axis=-1, keepdims=True) + EPS)
        h2_ref[...] = (x1 * r * g_ref[...]).astype(BF)

    tile = pl.BlockSpec((TOK_TILE, D_MODEL), lambda i: (i, 0))
    resident = lambda shape: pl.BlockSpec(shape, lambda i: (0, 0), pipeline_mode=pl.Buffered(1))
    f32, bf16 = _out_hbm((SEQ, D_MODEL), F32), _out_hbm((SEQ, D_MODEL), BF)
    return pl.pallas_call(
        body, name="mix_out_fwd", grid=(SEQ // TOK_TILE,),
        in_specs=[pl.BlockSpec((TOK_TILE, POOL_WIDTH), lambda i: (i, 0)), tile,
                  pl.BlockSpec((TOK_TILE, GATE_W), lambda i: (i, C_GATE // GATE_W)), tile,
                  resident((POOL_WIDTH, D_MODEL)), resident((D_MODEL, D_MODEL)), resident((D_MODEL, D_MODEL)),
                  _const_spec((1, GATE_W)), _const_spec((1, D_MODEL)), pl.BlockSpec(memory_space=pl.ANY)],
        out_specs=[tile] * 5, out_shape=[bf16, bf16, bf16, f32, bf16], compiler_params=_params("parallel"),
    )(*map(_in_hbm, (ps, og, zcat, x, w_pool_proj, w_gla_proj, w_out, b_gate, g_ffn)), after)


def _mix_bwd(dx1, w_out, zcat, b_gate, y_pool, y_gla):
    def body(dx_ref, w_ref, zg_ref, b_ref, yp_ref, yg_ref, dz_ref, dyp_ref, dyg_ref, db_ref):
        dm = _dot(dx_ref[...], w_ref[...], tb=True)
        g0 = _sigmoid(zg_ref[:, :D_MODEL].astype(F32) + b_ref[:, :D_MODEL])
        g1 = _sigmoid(zg_ref[:, D_MODEL:].astype(F32) + b_ref[:, D_MODEL:])
        dyp_ref[...] = (dm * g0).astype(BF)
        dyg_ref[...] = (dm * g1).astype(BF)
        dz0 = dm * yp_ref[...].astype(F32) * g0 * (1.0 - g0)
        dz1 = dm * yg_ref[...].astype(F32) * g1 * (1.0 - g1)
        dz_ref[:, :D_MODEL] = dz0.astype(BF)
        dz_ref[:, D_MODEL:] = dz1.astype(BF)
        b0 = jnp.sum(dz0, axis=0, keepdims=True)
        b1 = jnp.sum(dz1, axis=0, keepdims=True)

        @pl.when(pl.program_id(0) == 0)
        def _():
            db_ref[:, :D_MODEL] = b0
            db_ref[:, D_MODEL:] = b1

        @pl.when(pl.program_id(0) > 0)
        def _():
            db_ref[:, :D_MODEL] += b0
            db_ref[:, D_MODEL:] += b1

    tile = pl.BlockSpec((TOK_TILE, D_MODEL), lambda i: (i, 0))
    gspec = pl.BlockSpec((TOK_TILE, GATE_W), lambda i: (i, C_GATE // GATE_W))
    return pl.pallas_call(
        body, name="mix_bwd", grid=(SEQ // TOK_TILE,),
        in_specs=[tile, _const_spec((D_MODEL, D_MODEL)), gspec, _const_spec((1, GATE_W)), tile, tile],
        out_specs=[gspec, tile, tile, _const_spec((1, GATE_W))],
        out_shape=[_out_hbm((SEQ, N_DZ), BF), _out_hbm((SEQ, D_MODEL), BF),
                   _out_hbm((SEQ, D_MODEL), BF), _out_hbm((1, GATE_W), F32)],
        compiler_params=_params("arbitrary"),
    )(*map(_in_hbm, (dx1, w_out, zcat, b_gate, y_pool, y_gla)))


N_TOK_TILES = SEQ // TOK_TILE
HALO_PER_TILE = TOK_TILE // HALO


LANE_TILES = tuple((lo, min(128, FF_BLK - lo)) for lo in range(0, FF_BLK, 128))


def _taps(w_ref, b_ref, half, lanes, rows):
    shape = (rows, lanes.stop - lanes.start)
    return ([jnp.broadcast_to(w_ref[half, j:j + 1, lanes], shape) for j in range(3)],
            jnp.broadcast_to(b_ref[half, :, lanes], shape))


def _conv_strips(u_ref, ub_ref, ua_ref, taps, lanes, width, n_strips, first):
    row = lax.broadcasted_iota(jnp.int32, (HALO, width), 0)
    prev = [[pltpu.roll(jnp.where(first, 0.0, ub_ref[half, :, lanes]), k, 0) for k in (1, 2)] for half in range(2)]
    for s in range(n_strips + (ua_ref is not None)):
        u3, conv = [], []
        for half in range(2):
            cur = u_ref[half, s * HALO:(s + 1) * HALO, lanes] if s < n_strips else ua_ref[half, :, lanes]
            rolled = [pltpu.roll(cur, k, 0) for k in (1, 2)]
            frames = [jnp.where(row >= 2, rolled[1], prev[half][1]), jnp.where(row >= 1, rolled[0], prev[half][0]), cur]
            prev[half] = rolled
            w3, bias = taps[half]
            u3.append(frames)
            conv.append(bias + frames[0] * w3[0] + frames[1] * w3[1] + frames[2] * w3[2])
        yield s, u3, conv


def _pair_specs(pairs):
    tile = pl.BlockSpec((pairs, None, TOK_TILE, FF_BLK), lambda b, i: (0, b, i, 0))
    before = pl.BlockSpec((pairs, None, HALO, FF_BLK), lambda b, i: (0, b, jnp.maximum(i * HALO_PER_TILE - 1, 0), 0))
    after = pl.BlockSpec((pairs, None, HALO, FF_BLK),
                         lambda b, i: (0, b, jnp.minimum((i + 1) * HALO_PER_TILE, SEQ // HALO - 1), 0))

    def vec(rows):
        return pl.BlockSpec((2, None, rows, FF_BLK), lambda b, i: (0, b, 0, 0))

    return tile, before, after, vec


N_STRIPS = TOK_TILE // HALO


def _up_conv_fwd(h2, wt_up, w_conv, b_conv):
    steps = N_TOK_TILES // 2

    def body(h_ref, h_next, wg_ref, wv_ref, w_ref, b_ref, u_ref, a_ref, buf_a, buf_b, carry):
        j = pl.program_id(1)

        def project(hv, buf):
            buf[0] = _dot(hv, wg_ref[...], tb=True)
            buf[1] = _dot(hv, wv_ref[...], tb=True)

        def conv(buf, row0):
            u_ref[:, row0:row0 + TOK_TILE, :] = buf[...]
            for lo, width in LANE_TILES:
                lanes = slice(lo, lo + width)
                taps = [_taps(w_ref, b_ref, half, lanes, HALO) for half in range(2)]
                pending = None
                for s, _, (cg, cv) in _conv_strips(buf, carry, None, taps, lanes, width, N_STRIPS, False):
                    act = cg * _sigmoid(cg) * cv
                    if s % 2 == 0:
                        pending = act
                    else:
                        a_ref[0, row0 + (s - 1) * HALO:row0 + (s + 1) * HALO, lanes] = (
                            jnp.concatenate([pending, act], axis=0).astype(BF))
            carry[...] = buf[:, TOK_TILE - HALO:, :]

        @pl.when(j == 0)
        def _():
            project(h_ref[0:TOK_TILE, :], buf_a)
            carry[...] = jnp.zeros_like(carry)

        project(h_ref[TOK_TILE:, :], buf_b)
        conv(buf_a, 0)
        project(h_next[...], buf_a)
        conv(buf_b, TOK_TILE)

    w_blk = lambda half: pl.BlockSpec((FF_BLK, D_MODEL), lambda b, j: (b + 4 * half, 0))
    vec = lambda rows: pl.BlockSpec((2, None, rows, FF_BLK), lambda b, j: (0, b, 0, 0))
    u_buf = pltpu.VMEM((2, TOK_TILE, FF_BLK), F32)
    return pl.pallas_call(
        body, name="up_conv_fwd", grid=(4, steps),
        in_specs=[pl.BlockSpec((2 * TOK_TILE, D_MODEL), lambda b, j: (j, 0)),
                  pl.BlockSpec((TOK_TILE, D_MODEL), lambda b, j: (jnp.minimum(2 * j + 2, N_TOK_TILES - 1), 0)),
                  w_blk(0), w_blk(1), vec(3), vec(1)],
        out_specs=[pl.BlockSpec((2, None, 2 * TOK_TILE, FF_BLK), lambda b, j: (0, b, j, 0)),
                   pl.BlockSpec((1, None, 2 * TOK_TILE, FF_BLK), lambda b, j: (0, b, j, 0))],
        out_shape=[_out_hbm((2, 4, SEQ, FF_BLK), F32), _out_hbm((1, 4, SEQ, FF_BLK), BF)],
        scratch_shapes=[u_buf, u_buf, pltpu.VMEM((2, HALO, FF_BLK), F32)],
        compiler_params=_params("parallel", "arbitrary"),
    )(*map(_in_hbm, (h2, h2, wt_up, wt_up, w_conv, b_conv)))


def _conv_bwd(u, da, w_conv, b_conv):
    def body(u_ref, ub_ref, ua_ref, da_ref, daa_ref, w_ref, b_ref, du_ref, dw_ref, db_ref):
        i = pl.program_id(1)

        @pl.when(i == 0)
        def _():
            dw_ref[...] = jnp.zeros_like(dw_ref)
            db_ref[...] = jnp.zeros_like(db_ref)

        for lo, width in LANE_TILES:
            lanes = slice(lo, lo + width)
            row = lax.broadcasted_iota(jnp.int32, (HALO, width), 0)
            taps = [_taps(w_ref, b_ref, half, lanes, HALO) for half in range(2)]
            acc_w = [[jnp.zeros((HALO, width), F32) for _ in range(3)] for _ in range(2)]
            acc_b = [jnp.zeros((HALO, width), F32) for _ in range(2)]
            da_pair, pending = None, [None, None]
            dc_prev, up_prev = [None, None], [None, None]
            for s, u3, (cg, cv) in _conv_strips(u_ref, ub_ref, ua_ref, taps, lanes, width, N_STRIPS, i == 0):
                act, dact = _silu_parts(cg)
                if s == N_STRIPS:
                    da = jnp.where(i < N_TOK_TILES - 1, daa_ref[0, :, lanes].astype(F32), 0.0)
                elif s % 2 == 0:
                    da_pair = da_ref[0, s * HALO:(s + 2) * HALO, lanes].astype(F32)
                    da = da_pair[:HALO]
                else:
                    da = da_pair[HALO:]
                dc = (da * cv * dact, da * act)
                for half in range(2):
                    up = [pltpu.roll(dc[half], HALO - k, 0) for k in (1, 2)]
                    if s < N_STRIPS:
                        for j in range(3):
                            acc_w[half][j] = acc_w[half][j] + dc[half] * u3[half][j]
                        acc_b[half] = acc_b[half] + dc[half]
                    if s >= 1:
                        w3 = taps[half][0]
                        du = (dc_prev[half] * w3[2] + jnp.where(row < HALO - 1, up_prev[half][0], up[0]) * w3[1]
                              + jnp.where(row < HALO - 2, up_prev[half][1], up[1]) * w3[0])
                        if (s - 1) % 2 == 0:
                            pending[half] = du
                        else:
                            du_ref[half, (s - 2) * HALO:s * HALO, lanes] = jnp.concatenate([pending[half], du],
                                                                                           axis=0).astype(BF)
                    dc_prev[half], up_prev[half] = dc[half], up
            for half in range(2):
                for j in range(3):
                    dw_ref[half, j:j + 1, lanes] += jnp.sum(acc_w[half][j], axis=0, keepdims=True)
                db_ref[half, :, lanes] += jnp.sum(acc_b[half], axis=0, keepdims=True)

    tile, before, after, vec = _pair_specs(2)
    da_tile, _, da_after_spec, _ = _pair_specs(1)
    return pl.pallas_call(
        body, name="conv_bwd", grid=(4, N_TOK_TILES),
        in_specs=[tile, before, after, da_tile, da_after_spec, vec(3), vec(1)],
        out_specs=[tile, vec(3), vec(1)],
        out_shape=[_out_hbm((2, 4, SEQ, FF_BLK), BF), _out_hbm((2, 4, 3, FF_BLK), F32),
                   _out_hbm((2, 4, 1, FF_BLK), F32)],
        compiler_params=_params("parallel", "arbitrary"),
    )(*map(_in_hbm, (u, u, u, da, da, w_conv, b_conv)))


W_IN_SEGMENTS = ((R_POOL, POOL_WIDTH, C_POOL), (R_QKV, QKV_W, C_QKV), (R_OG, D_MODEL, C_OG), (R_GK, GATE_RANK, C_GK),
                 (R_GATE, GATE_W, C_GATE))


def _slab_pieces(d):
    lo, hi = d * IN_SHARD, (d + 1) * IN_SHARD
    pieces = []
    for start, n, at in W_IN_SEGMENTS:
        a, b = max(lo, start), min(hi, start + n)
        if a < b:
            assert (a - lo) % 2 == 0 and (b - a) % 2 == 0 and (at + a - start) % 2 == 0
            pieces.append(((a - lo) // 2, (b - a) // 2, (at + a - start) // 2))
    return pieces


def _unshard_w_in(slabs):
    def body(slab_ref, cat_ref):
        d = pl.program_id(0)
        src = slab_ref.bitcast(jnp.uint32)
        dst = cat_ref.bitcast(jnp.uint32)

        @pl.when(d == 0)
        def _():
            cat_ref[C_GK:, :] = jnp.zeros((GK_PAD, D_MODEL), BF)

        for dd in range(N_DEV):
            @pl.when(d == dd)
            def _():
                for a, n, at in _slab_pieces(dd):
                    dst[pl.ds(at, n), :] = src[0, pl.ds(a, n), :]

    return pl.pallas_call(
        body, name="unshard_w_in", grid=(N_DEV,),
        in_specs=[pl.BlockSpec((1, IN_SHARD, D_MODEL), lambda d: (d, 0, 0))], out_specs=_const_spec((N_DZ, D_MODEL)),
        out_shape=_out_hbm((N_DZ, D_MODEL), BF), compiler_params=_params("arbitrary"),
    )(_in_hbm(slabs))


def _shard_d_w_in(d_cat):
    def body(cat_ref, slab_ref):
        d = pl.program_id(0)
        cat = cat_ref.bitcast(jnp.uint32)
        dst = slab_ref.bitcast(jnp.uint32)
        for dd in range(N_DEV):
            @pl.when(d == dd)
            def _():
                for a, n, at in _slab_pieces(dd):
                    dst[0, pl.ds(a, n), :] = cat[pl.ds(at, n), :]

    return pl.pallas_call(
        body, name="shard_d_w_in", grid=(N_DEV,), in_specs=[_const_spec((N_DZ, D_MODEL))],
        out_specs=pl.BlockSpec((1, IN_SHARD, D_MODEL), lambda d: (d, 0, 0)),
        out_shape=_out_hbm((N_DEV, IN_SHARD, D_MODEL), BF), compiler_params=_params("parallel"),
    )(_in_hbm(d_cat))


ANY = pl.BlockSpec(memory_space=pl.ANY)


def _place():
    x, y, c = lax.axis_index("x"), lax.axis_index("y"), lax.axis_index("c")
    other_chips = [(1 - x, y), (x, 1 - y), (1 - x, 1 - y)]
    return x, y, c, other_chips


SEM = pl.BlockSpec(memory_space=pltpu.SEMAPHORE)
IN_HBM = pl.BlockSpec(memory_space=pltpu.HBM)
SPLIT_PARAMS = pltpu.CompilerParams(has_side_effects=pltpu.SideEffectType.DATAFLOW_SIDE_EFFECTING)


def _gather_first(refs, send_sems, recv_sems):
    x, y, c, chips = _place()
    targets = [(x, y, 1 - c)] + [(px, py, c) for px, py in chips]
    return [pltpu.make_async_remote_copy(src_ref=refs[2 * a], dst_ref=refs[2 * a + 1].at[4 * x + 2 * y + c],
                                         send_sem=send_sems.at[4 * a + k], recv_sem=recv_sems.at[4 * a + k],
                                         device_id=to, device_id_type=MESH)
            for a in range(len(refs) // 2) for k, to in enumerate(targets)]


def _gather_direct(refs, send_sems, recv_sems):
    x, y, c, _ = _place()
    flips = [(dx, dy, dc) for dx in (0, 1) for dy in (0, 1) for dc in (0, 1) if dx + dy + dc]
    targets = [(1 - x if dx else x, 1 - y if dy else y, 1 - c if dc else c) for dx, dy, dc in flips]
    return [pltpu.make_async_remote_copy(src_ref=refs[2 * a], dst_ref=refs[2 * a + 1].at[4 * x + 2 * y + c],
                                         send_sem=send_sems.at[7 * a + k], recv_sem=recv_sems.at[7 * a + k],
                                         device_id=to, device_id_type=MESH)
            for a in range(len(refs) // 2) for k, to in enumerate(targets)]


def _gather_second(refs, send_sems, recv_sems):
    x, y, c, chips = _place()
    copies = []
    for a, land in enumerate(refs):
        for j, (px, py) in enumerate(chips):
            block = land.at[4 * px + 2 * py + c]
            copies.append(pltpu.make_async_remote_copy(src_ref=block, dst_ref=block, send_sem=send_sems.at[3 * a + j],
                                                       recv_sem=recv_sems.at[3 * a + j], device_id=(x, y, 1 - c),
                                                       device_id_type=MESH))
    return copies


def _reduce_first(refs, send_sems, recv_sems):
    x, y, c, _ = _place()
    return [pltpu.make_async_remote_copy(src_ref=refs[2 * a].at[j, 1 - c], dst_ref=refs[2 * a + 1].at[j],
                                         send_sem=send_sems.at[4 * a + j], recv_sem=recv_sems.at[4 * a + j],
                                         device_id=(x, y, 1 - c), device_id_type=MESH)
            for a in range(len(refs) // 2) for j in range(4)]


def _reduce_second(refs, send_sems, recv_sems):
    _, _, c, chips = _place()
    return [pltpu.make_async_remote_copy(src_ref=refs[2 * a].at[2 * px + py], dst_ref=refs[2 * a + 1].at[k],
                                         send_sem=send_sems.at[3 * a + k], recv_sem=recv_sems.at[3 * a + k],
                                         device_id=(px, py, c), device_id_type=MESH)
            for a in range(len(refs) // 2) for k, (px, py) in enumerate(chips)]


def _split_start(name, groups):
    arrays = [a for g in groups for a in g[0]]
    n = len(arrays)

    def body(*refs):
        sems = refs[n:n + 2 * len(groups)]
        at = 0
        for gi, (members, _, build) in enumerate(groups):
            for cp in build(refs[at:at + len(members)], sems[2 * gi], sems[2 * gi + 1]):
                cp.start()
            at += len(members)
        refs[-1][...] = jnp.zeros_like(refs[-1])

    sem_shapes = [pltpu.SemaphoreType.DMA((g[1],)) for g in groups for _ in range(2)]
    outs = pl.pallas_call(
        body, name=name, in_specs=[IN_HBM] * n,
        out_shape=(*sem_shapes, *[_out_hbm(a.shape, a.dtype) for a in arrays], jax.ShapeDtypeStruct((8, 128), F32)),
        out_specs=(*[SEM] * len(sem_shapes), *[IN_HBM] * n, pl.BlockSpec(memory_space=pltpu.VMEM)),
        input_output_aliases={i: len(sem_shapes) + i for i in range(n)}, compiler_params=SPLIT_PARAMS,
    )(*[pltpu.with_memory_space_constraint(a, pltpu.HBM) for a in arrays])
    per_group, at = [], len(sem_shapes)
    for gi, (members, _, _) in enumerate(groups):
        per_group.append((outs[2 * gi], outs[2 * gi + 1], list(outs[at:at + len(members)])))
        at += len(members)
    return per_group, outs[-1]


def _split_wait(name, started, build, after):
    send_sems, recv_sems, arrays = started
    n = len(arrays)
    after = after if isinstance(after, (tuple, list)) else (after,)

    def body(*refs):
        for cp in build(refs[:n], refs[n], refs[n + 1]):
            cp.wait_send()
            cp.wait_recv()

    return pl.pallas_call(
        body, name=name, in_specs=[IN_HBM] * n + [SEM, SEM] + [ANY] * len(after),
        out_shape=tuple(_out_hbm(a.shape, a.dtype) for a in arrays), out_specs=tuple([IN_HBM] * n),
        input_output_aliases={i: i for i in range(n)}, compiler_params=SPLIT_PARAMS,
    )(*arrays, send_sems, recv_sems, *after)


def _placed_behind(token, arrays, name):
    n = len(arrays)

    def body(*refs):
        refs[-1][...] = jnp.zeros_like(refs[-1])

    outs = pl.pallas_call(
        body, name=name, in_specs=[IN_HBM] * n + [ANY],
        out_shape=(*[_out_hbm(a.shape, a.dtype) for a in arrays], jax.ShapeDtypeStruct((8, 128), F32)),
        out_specs=(*[IN_HBM] * n, pl.BlockSpec(memory_space=pltpu.VMEM)),
        input_output_aliases={i: i for i in range(n)},
    )(*map(_in_hbm, arrays), token)
    return outs[:n], outs[-1]


def _gather_landing(shard, me):
    return lax.dynamic_update_slice(lax.empty((N_DEV,) + shard.shape, shard.dtype), shard[None],
                                    (me,) + (0,) * shard.ndim)


ADAM_LANE_TILE = 256


def _tile_2d(rows, cols):
    for t in (256, 176, 128):
        if rows % t == 0:
            return t, cols
    return rows, ADAM_LANE_TILE


def _pair_sum(part, recv, core, name):
    _, rows, cols = recv.shape
    tr, tc = rows, cols

    def body(c_ref, p_ref, r_ref, o_ref):
        del c_ref
        o_ref[...] = (p_ref[...].astype(F32) + r_ref[...].astype(F32)).astype(BF)

    grid_spec = pltpu.PrefetchScalarGridSpec(
        num_scalar_prefetch=1, grid=(4, rows // tr, cols // tc),
        in_specs=[pl.BlockSpec((None, None, tr, tc), lambda j, i, k, c_ref: (j, c_ref[0], i, k)),
                  pl.BlockSpec((None, tr, tc), lambda j, i, k, c_ref: (j, i, k))],
        out_specs=pl.BlockSpec((None, tr, tc), lambda j, i, k, c_ref: (j, i, k)))
    return pl.pallas_call(
        body, name=name, grid_spec=grid_spec, out_shape=_out_hbm(recv.shape, BF),
        compiler_params=_params("parallel", "parallel", "parallel"),
    )(core, *map(_in_hbm, (part, recv)))


def _adamw(w, g, m, v):
    m = ADAM_B1 * m + (1.0 - ADAM_B1) * g
    v = ADAM_B2 * v + (1.0 - ADAM_B2) * (g * g)
    delta = -ADAM_LR * ((m / ADAM_C1) / (jnp.sqrt(v / ADAM_C2) + ADAM_EPS) + ADAM_WD * w)
    return delta, m, v


def _chip_sum_adamw(sums, recv, w, m, v, chip, name, lone_rows=False):
    rows, cols = w.shape
    tr, tc = _tile_2d(rows, cols)

    def body(chip_ref, s_ref, r_ref, w_ref, m_ref, v_ref, *out_refs):
        del chip_ref
        g = s_ref[...].astype(F32)
        for k in range(3):
            g = g + r_ref[k].astype(F32)
        for o_ref, t in zip(out_refs, (g,) + _adamw(w_ref[...], g, m_ref[...], v_ref[...])):
            o_ref[...] = t[:, None, :] if lone_rows else t

    tile = pl.BlockSpec((tr, tc), lambda i, k, chip_ref: (i, k))
    out_tile = pl.BlockSpec((tr, 1, tc), lambda i, k, chip_ref: (i, 0, k)) if lone_rows else tile
    grid_spec = pltpu.PrefetchScalarGridSpec(
        num_scalar_prefetch=1, grid=(rows // tr, cols // tc),
        in_specs=[pl.BlockSpec((None, tr, tc), lambda i, k, chip_ref: (chip_ref[0], i, k)),
                  pl.BlockSpec((3, tr, tc), lambda i, k, chip_ref: (0, i, k)), tile, tile, tile],
        out_specs=[out_tile] * 4)
    return pl.pallas_call(
        body, name=name, grid_spec=grid_spec,
        out_shape=[_out_hbm((rows, 1, cols) if lone_rows else (rows, cols), F32)] * 4,
        compiler_params=_params("parallel", "parallel"),
    )(chip, *map(_in_hbm, (sums, recv, w, m, v)))


def _small_sum_adamw(me, entries, loss):
    def whole(shape, squeeze=0, pick=None):
        blk = (None,) * squeeze + tuple(shape[squeeze:])
        if pick is not None:
            blk = tuple(shape[:pick]) + (None,) + tuple(shape[pick + 1:])
            return pl.BlockSpec(blk, lambda i, me_ref: (0,) * pick + (me_ref[0],) + (0,) * (len(shape) - pick - 1))
        return pl.BlockSpec(blk, lambda i, me_ref: (0,) * len(shape))

    in_specs, out_specs, out_shape, args = [], [], [], []
    for own, parts, w, m, v, sharded in entries + [loss + (None, None, None, False)]:
        in_specs += [whole(own.shape, pick=0 if sharded else None), whole(parts.shape, pick=1 if sharded else None)]
        args += [own, parts]
        if w is not None:
            lead = w.ndim - (parts.ndim - (2 if sharded else 1))
            in_specs += [whole(w.shape, squeeze=lead)] * 3
            out_specs += [whole(w.shape, squeeze=lead)] * 4
            out_shape += [_out_hbm(w.shape, F32)] * 4
            args += [w, m, v]
    out_specs.append(whole(loss[0].shape))
    out_shape.append(_out_hbm(loss[0].shape, F32))
    n = len(entries)

    def added(own_ref, p_ref, me):
        total = None
        for d in range(N_DEV):
            part = jnp.where(me == d, own_ref[...], p_ref[d])
            total = part if total is None else total + part
        return total

    def body(me_ref, *refs):
        ins, outs = refs[:5 * n + 2], refs[5 * n + 2:]
        for e in range(n):
            own_ref, p_ref, w_ref, m_ref, v_ref = ins[5 * e:5 * e + 5]
            g_out, d_out, m_out, v_out = outs[4 * e:4 * e + 4]
            g = added(own_ref, p_ref, me_ref[0])
            g_out[...] = g
            d_out[...], m_out[...], v_out[...] = _adamw(w_ref[...], g, m_ref[...], v_ref[...])
        outs[4 * n][...] = added(ins[5 * n], ins[5 * n + 1], me_ref[0])

    grid_spec = pltpu.PrefetchScalarGridSpec(num_scalar_prefetch=1, grid=(1,), in_specs=in_specs, out_specs=out_specs)
    outs = pl.pallas_call(body, name="small_sum_adamw", grid_spec=grid_spec, out_shape=out_shape,
                          compiler_params=_params("arbitrary"))(me, *map(_in_hbm, args))
    return [outs[4 * e:4 * e + 4] for e in range(n)], outs[4 * n]


MM_TILE = 512
N_MM_TILES = SEQ // MM_TILE
CAT_TILE = 512
N_CAT_TILES = N_CAT // CAT_TILE
DZ_TILE = 640


def kernel(x, g_mix, w_in, b_gate, w_gk_up, b_gk, w_pool_grp, pool_scale, g_gla_head, w_pool_proj, w_gla_proj, w_out, g_ffn, w_up, w_conv, b_conv, w_down, g_final, loss_target, m_g_mix, m_w_in, m_b_gate, m_w_gk_up, m_b_gk, m_w_pool_grp, m_pool_scale, m_g_gla_head, m_w_pool_proj, m_w_gla_proj, m_w_out, m_g_ffn, m_w_up, m_w_conv, m_b_conv, m_w_down, m_g_final, v_g_mix, v_w_in, v_b_gate, v_w_gk_up, v_b_gk, v_w_pool_grp, v_pool_scale, v_g_gla_head, v_w_pool_proj, v_w_gla_proj, v_w_out, v_g_ffn, v_w_up, v_w_conv, v_b_conv, v_w_down, v_g_final):
    xi, yi, ci = lax.axis_index("x"), lax.axis_index("y"), lax.axis_index("c")
    me = 4 * xi + 2 * yi + ci
    core = jnp.reshape(ci, (1,)).astype(jnp.int32)
    chip = jnp.reshape(2 * xi + yi, (1,)).astype(jnp.int32)
    xs, target = x[0], loss_target[0]

    big = dict(w_in=w_in[0].T, w_pool_proj=w_pool_proj[0], w_gla_proj=w_gla_proj[0], w_out=w_out[0], w_up=w_up[0].T,
               w_down=w_down[0])
    moments = dict(w_in=(m_w_in[0].T, v_w_in[0].T), w_pool_proj=(m_w_pool_proj[0], v_w_pool_proj[0]),
                   w_gla_proj=(m_w_gla_proj[0], v_w_gla_proj[0]), w_out=(m_w_out[0], v_w_out[0]),
                   w_up=(m_w_up[0].T, v_w_up[0].T), w_down=(m_w_down[0], v_w_down[0]))
    names = list(big)
    shards = {k: big[k].astype(BF) for k in names}
    shards["w_gk_up"], shards["w_conv"] = w_gk_up[0], w_conv[0]
    gather_groups = (("w_in", "w_gk_up"), ("w_pool_proj", "w_gla_proj", "w_out"), ("w_up", "w_down", "w_conv"))
    started, token = _split_start("gather_start", [
        ([t for k in g for t in (shards[k], _gather_landing(shards[k], me))], 4 * len(g), _gather_first)
        for g in gather_groups])
    (big["w_in"], *moments["w_in"], bconv4, m_w_conv, v_w_conv), token = _placed_behind(
        token, [big["w_in"], *moments["w_in"], b_conv.reshape(2, 4, 1, FF_BLK), m_w_conv, v_w_conv],
        "place_adamw_operands")

    def gather_pass(gi, after):
        lands = list(_split_wait(f"gather_wait_{gi}", started[gi], _gather_first, after)[1::2])
        passed, tkn = _split_start(f"gather_pass_{gi}", [(lands, 3 * len(lands), _gather_second)])
        return passed[0], tkn

    def gather_done(gi, passed, after):
        return dict(zip(gather_groups[gi], _split_wait(f"gather_pass_wait_{gi}", passed, _gather_second, after)))

    tok = lambda i, j, k: (i, 0)
    whole = lambda i, j, k: (0, 0)
    kblk = lambda i, j, k: (k, 0)
    ff_seq = (None, None, SEQ, FF_BLK)

    h = _rms_fwd(xs, g_mix, token, "rms_mix")
    wg = gather_done(0, gather_pass(0, h)[0], h)
    wt_cat = _unshard_w_in(wg["w_in"])
    wgk_pad = jnp.pad(wg["w_gk_up"].transpose(1, 0, 2).reshape(GATE_RANK, GLA_DK), ((0, GK_PAD - GATE_RANK), (0, 0)))
    zcat = _mm(h, wt_cat, out_shape=(SEQ, N_CAT), out_dtype=BF, grid=(N_CAT_TILES, 1, 1),
               blk_a=(SEQ, D_MODEL), blk_b=(CAT_TILE, D_MODEL), blk_o=(SEQ, CAT_TILE),
               map_a=whole, map_b=lambda j, i, k: (j, 0), map_o=lambda j, i, k: (0, j), tb=True, name="mm_in")
    la = _gk_fwd(h, wt_cat, wgk_pad, b_gk)
    passed, tkn = gather_pass(1, la)
    o, states = _gla_fwd(zcat, la, tkn)
    wg = gather_done(1, passed, o)
    wpp = wg["w_pool_proj"].transpose(1, 0, 2).reshape(POOL_WIDTH, D_MODEL)
    wgp = wg["w_gla_proj"].reshape(D_MODEL, D_MODEL)
    wout = wg["w_out"].reshape(D_MODEL, D_MODEL)
    og = _post_gla_fwd(o, zcat, g_gla_head)
    ps = _pool_fwd(zcat, w_pool_grp[0], pool_scale)
    passed, tkn = gather_pass(2, (og, ps))
    y_pool, y_gla, mixed, x1, h2 = _mix_out_fwd(ps, og, zcat, xs, wpp, wgp, wout, b_gate, g_ffn, tkn)
    wg = gather_done(2, passed, h2)
    wt_up = wg["w_up"].reshape(2 * D_FF, D_MODEL)
    wdown = wg["w_down"].reshape(D_FF, D_MODEL)
    wconv4 = wg["w_conv"].reshape(2, 4, 3, FF_BLK)
    blk4 = lambda b, i, k: (b // 4, b % 4, 0, 0)
    u4, act = _up_conv_fwd(h2, wt_up, wconv4, bconv4)
    loss_part, dx2, dx2_bf, dg_final = _mm_tokens(
        act, wdown, blk_a=(None, 4, TOK_MM_TILE, FF_BLK), map_a=lambda i: (0, 0, i, 0),
        pieces=[(b, b * FF_BLK, FF_BLK) for b in range(4)], res=x1, then=("loss", g_final.reshape(1, D_MODEL), target),
        name="mm_down_loss")

    da = _mm(dx2_bf, wdown, out_shape=(1, 4, SEQ, FF_BLK), out_dtype=BF, grid=(4, 1, 1),
             blk_a=(SEQ, D_MODEL), blk_b=(FF_BLK, D_MODEL), blk_o=ff_seq,
             map_a=whole, map_b=lambda b, i, k: (b, 0), map_o=lambda b, i, k: (0, b, 0, 0), tb=True, name="mm_d_act")
    d_wdown = _mm(act, dx2_bf, out_shape=(D_FF, D_MODEL), out_dtype=BF, grid=(4, 1, 1),
                  blk_a=ff_seq, blk_b=(SEQ, D_MODEL), blk_o=(FF_BLK, D_MODEL),
                  map_a=lambda b, i, k: (0, b, 0, 0), map_b=whole, map_o=lambda b, i, k: (b, 0), ta=True,
                  name="mm_d_wdown")
    du4, d_wconv, d_bconv = _conv_bwd(u4, da, wconv4, bconv4)
    d_wt_up = _mm(du4, h2, out_shape=(2 * D_FF, D_MODEL), out_dtype=BF, grid=(N_DEV, 1, 1),
                  blk_a=ff_seq, blk_b=(SEQ, D_MODEL), blk_o=(FF_BLK, D_MODEL),
                  map_a=blk4, map_b=whole, map_o=lambda b, i, k: (b, 0), ta=True, name="mm_d_wup")
    res = {}

    def to_sibling(keys, parts):
        return [t for k in keys for t in (parts[k], lax.empty((4,) + parts[k].shape[2:], BF))], 4 * len(keys), _reduce_first

    def to_chips(keys, st, after):
        arrays = _split_wait("reduce_wait_" + keys[0], st, _reduce_first, after)
        sums = [_pair_sum(p, r, core, "pair_sum_" + k) for k, p, r in zip(keys, arrays[0::2], arrays[1::2])]
        return [t for s in sums for t in (s, lax.empty((3,) + s.shape[1:], BF))], 3 * len(keys), _reduce_second

    def reduce_start(keys, parts):
        st, tkn = _split_start("reduce_start_" + keys[0], [to_sibling(keys, parts)])
        return st[0], tkn

    def reduce_cross(keys, st, after):
        st2, tkn = _split_start("reduce_cross_" + keys[0], [to_chips(keys, st, after)])
        return st2[0], tkn

    def reduce_done(keys, st2, after):
        arrays = _split_wait("reduce_cross_wait_" + keys[0], st2, _reduce_second, after)
        for k, s, r in zip(keys, arrays[0::2], arrays[1::2]):
            outs = _chip_sum_adamw(s, r, big[k], moments[k][0], moments[k][1], chip, "adamw_" + k,
                                   lone_rows=k == "w_in")
            res[k] = [jnp.transpose(t, (1, 2, 0)) if k == "w_in" else (t.T if k == "w_up" else t)[None] for t in outs]

    ffn_keys = ("w_down", "w_up")
    ffn_red, tkn = reduce_start(ffn_keys, dict(w_down=d_wdown.reshape(4, 2, D_FF // N_DEV, D_MODEL),
                                               w_up=d_wt_up.reshape(4, 2, FF_BLK, D_MODEL)))
    dx1, dg_ffn = _mm_tokens(
        du4, wt_up, blk_a=(2, 4, TOK_MM_TILE, FF_BLK), map_a=lambda i: (0, 0, i, 0),
        pieces=[((b // 4, b % 4), b * FF_BLK, FF_BLK) for b in range(N_DEV)], after=tkn, then=("rms_bwd", x1, g_ffn, dx2),
        name="mm_d_h2_rms")

    sq_t = dict(out_shape=(D_MODEL, D_MODEL), grid=(1, 1, N_MM_TILES), blk_a=(MM_TILE, D_MODEL),
                blk_b=(MM_TILE, D_MODEL), blk_o=(D_MODEL, D_MODEL), map_a=kblk, map_b=kblk, map_o=whole, ta=True)
    d_wout = _mm(mixed, dx1, out_dtype=BF, name="mm_d_wout", **sq_t)
    dzcat, dy_pool, dy_gla, db_gate = _mix_bwd(dx1, wout, zcat, b_gate, y_pool, y_gla)
    d_wgp = _mm(og, dy_gla, out_dtype=BF, name="mm_d_wgp", **sq_t)
    mix_keys = ("w_out", "w_gla_proj")
    (ffn_red, mix_red), tkn = _split_start("reduce_cross_w_down", [
        to_chips(ffn_keys, ffn_red, db_gate),
        to_sibling(mix_keys, dict(w_out=d_wout.reshape(4, 2, D_MODEL // N_DEV, D_MODEL),
                                  w_gla_proj=d_wgp.reshape(4, 2, D_MODEL // N_DEV, D_MODEL)))])
    dzcat, d_o, dg_head = _post_gla_bwd(dzcat, dy_gla, wgp, o, zcat, g_gla_head, tkn)
    dzcat, dla = _gla_bwd(dzcat, zcat, la, d_o, states)
    dzcat, d_wgk, db_gk = _gk_bwd(dzcat, dla, h, wt_cat, wgk_pad, b_gk)
    dps = _mm(dy_pool, wpp, out_shape=(SEQ, POOL_WIDTH), out_dtype=F32, grid=(N_MM_TILES, 1, 1),
              blk_a=(MM_TILE, D_MODEL), blk_b=(POOL_WIDTH, D_MODEL), blk_o=(MM_TILE, POOL_WIDTH),
              map_a=tok, map_b=whole, map_o=tok, tb=True, name="mm_d_ps")
    d_wpp = _mm(ps, dy_pool, out_shape=(POOL_WIDTH, D_MODEL), out_dtype=F32, grid=(1, 1, N_MM_TILES),
                blk_a=(MM_TILE, POOL_WIDTH), blk_b=(MM_TILE, D_MODEL), blk_o=(POOL_WIDTH, D_MODEL),
                map_a=kblk, map_b=kblk, map_o=whole, ta=True, name="mm_d_wpp")
    dzcat, d_wgrp, d_scale = _pool_bwd(dzcat, zcat, dps, w_pool_grp[0], pool_scale)
    row = lambda t: t.reshape(1, D_MODEL)
    small = [("b_gate", db_gate, b_gate, m_b_gate, v_b_gate, False),
             ("w_gk_up", d_wgk.reshape(GATE_RANK, N_DEV, GLA_DK // N_DEV).transpose(1, 0, 2), w_gk_up, m_w_gk_up,
              v_w_gk_up, True),
             ("b_gk", db_gk, b_gk, m_b_gk, v_b_gk, False),
             ("w_pool_grp", d_wgrp, w_pool_grp, m_w_pool_grp, v_w_pool_grp, False),
             ("pool_scale", d_scale, pool_scale, m_pool_scale, v_pool_scale, False),
             ("g_gla_head", dg_head, g_gla_head, m_g_gla_head, v_g_gla_head, False),
             ("g_ffn", dg_ffn, g_ffn, m_g_ffn, v_g_ffn, False),
             ("w_conv", d_wconv.reshape(N_DEV, 3, FF_BLK), w_conv, m_w_conv, v_w_conv, True),
             ("b_conv", d_bconv.reshape(b_conv.shape), b_conv, m_b_conv, v_b_conv, False),
             ("g_final", dg_final, row(g_final), row(m_g_final), row(v_g_final), False)]

    def to_all(parts):
        return ([t for p in parts for t in (p, lax.empty((N_DEV,) + p.shape, p.dtype))], 7 * len(parts),
                _gather_direct)

    (small_sent, mix_red), tkn = _split_start("small_start", [to_all([t[1] for t in small] + [loss_part]),
                                                              to_chips(mix_keys, mix_red, dla)])
    d_wt_cat = _mm(dzcat, h, out_shape=(N_DZ, D_MODEL), out_dtype=BF, grid=(N_DZ // DZ_TILE, 1, 1),
                   blk_a=(SEQ, DZ_TILE), blk_b=(SEQ, D_MODEL), blk_o=(DZ_TILE, D_MODEL),
                   map_a=lambda j, i, k: (0, j), map_b=whole, map_o=lambda j, i, k: (j, 0), ta=True, after=tkn,
                   name="mm_d_wcat")
    in_keys = ("w_in", "w_pool_proj")
    in_red, tkn = reduce_start(in_keys, dict(
        w_in=_shard_d_w_in(d_wt_cat).reshape(4, 2, IN_SHARD, D_MODEL),
        w_pool_proj=d_wpp.reshape(POOL_WIDTH, N_DEV, D_MODEL // N_DEV).transpose(1, 0, 2).astype(BF)
        .reshape(4, 2, POOL_WIDTH, D_MODEL // N_DEV)))
    reduce_done(mix_keys, mix_red, tkn)
    in_red, tkn = reduce_cross(in_keys, in_red, res["w_out"][0])
    grad_x, dg_mix = _mm_tokens(dzcat, wt_cat, blk_a=(TOK_MM_TILE, N_DZ), map_a=lambda i: (i, 0),
                                pieces=[(None, 0, N_DZ)], after=tkn, then=("rms_bwd", xs, g_mix, dx1),
                                name="mm_d_h_rms")
    (g_mix_sent,), tkn = _split_start("g_mix_start", [to_all([dg_mix])])
    reduce_done(ffn_keys, ffn_red, (grad_x, tkn))
    sent = list(_split_wait("small_wait", small_sent, _gather_direct, res["w_down"][0]))
    small.append(("g_mix", dg_mix, g_mix, m_g_mix, v_g_mix, False))
    sent[-2:-2] = _split_wait("g_mix_wait", g_mix_sent, _gather_direct, sent[1])
    own, gathered = sent[0::2], sent[1::2]
    small_out, loss_sum = _small_sum_adamw(jnp.reshape(me, (1,)).astype(jnp.int32),
                                           [(o, p) + t[2:] for o, p, t in zip(own, gathered, small)],
                                           (own[-1], gathered[-1]))
    for t, outs in zip(small, small_out):
        res[t[0]] = list(outs)
    res["g_final"] = [t.reshape(g_final.shape) for t in res["g_final"]]

    reduce_done(in_keys, in_red, loss_sum)
    loss = loss_sum[0, 0]
    order =["g_mix", "w_in", "b_gate", "w_gk_up", "b_gk", "w_pool_grp", "pool_scale", "g_gla_head", "w_pool_proj",
             "w_gla_proj", "w_out", "g_ffn", "w_up", "w_conv", "b_conv", "w_down", "g_final"]
    return (loss, grad_x[None], *[res[k][0] for k in order], *[res[k][1] for k in order],
            *[res[k][2] for k in order], *[res[k][3] for k in order])
```

```python
import jax
import jax.numpy as jnp
from jax import lax
from jax.experimental import pallas as pl
from jax.experimental.pallas import tpu as pltpu

F32 = jnp.float32
BF = jnp.bfloat16
HIGHEST = lax.Precision.HIGHEST
MESH = pl.DeviceIdType.MESH

N_DEV = 8
SEQ = 2048
D_MODEL = 1024
CHUNK = 64
EPS = 1e-6
POOL_WIDTH = 512
POOL_WINDOWS = (2, 4, 8, 16)
POOL_GD = 128
POOL_HALO = 16
HEADS = 4
HK = 128
HV = 256
GLA_DK = 512
GATE_RANK = 16
GATE_NORM = 16.0
D_FF = 2816
FF_BLK = 704
IN_SHARD = 706
C_QKV, C_GATE, C_OG, C_POOL, C_GK = 0, 2048, 4096, 5120, 5632
N_CAT = 5632
GK_PAD = 128
N_DZ = N_CAT + GK_PAD
R_POOL, R_QKV, R_OG, R_GK, R_GATE = 0, 512, 2560, 3584, 3600

ADAM_LR, ADAM_B1, ADAM_B2, ADAM_EPS, ADAM_WD, ADAM_STEP = 0.001, 0.9, 0.999, 1e-08, 0.01, 10
ADAM_C1 = 1.0 - ADAM_B1 ** ADAM_STEP
ADAM_C2 = 1.0 - ADAM_B2 ** ADAM_STEP

VMEM_BYTES_V7X = 64 * 1024 * 1024
VMEM_LIMIT = VMEM_BYTES_V7X * 3 // 4

TOK_TILE = 256
HALO = 8
GLA_CPS = 4


def _params(*sem):
    return pltpu.CompilerParams(dimension_semantics=sem, vmem_limit_bytes=VMEM_LIMIT)


def _const_spec(shape):
    nd = len(shape)
    return pl.BlockSpec(shape, lambda *_: (0,) * nd)


def _in_hbm(t):
    return pltpu.with_memory_space_constraint(t, pltpu.HBM)


def _out_hbm(shape, dtype):
    return pltpu.HBM(shape, dtype)


def _dot(a, b, ta=False, tb=False):
    dims = (((0 if ta else 1,), (1 if tb else 0,)), ((), ()))
    return lax.dot_general(a.astype(BF), b.astype(BF), dims, preferred_element_type=F32)


def _dot_exact(a, b):
    return jnp.dot(a, b, precision=HIGHEST, preferred_element_type=F32)


def _sigmoid(x):
    return 0.5 * jnp.tanh(0.5 * x) + 0.5


def _mm(a, b, *, out_shape, out_dtype, grid, blk_a, blk_b, blk_o, map_a, map_b, map_o, ta=False, tb=False,
        after=None, name):
    gk = grid[2]
    n_in = 2 + (after is not None)

    def body(*refs):
        a_ref, b_ref, o_ref = refs[0], refs[1], refs[n_in]
        prod = _dot(a_ref[...], b_ref[...], ta, tb)
        if gk == 1:
            o_ref[...] = prod.astype(out_dtype)
        else:
            acc = refs[n_in + 1]
            k = pl.program_id(2)

            @pl.when(k == 0)
            def _():
                acc[...] = prod

            @pl.when(k > 0)
            def _():
                acc[...] += prod

            @pl.when(k == gk - 1)
            def _():
                o_ref[...] = acc[...].astype(out_dtype)

    in_specs = [pl.BlockSpec(blk_a, map_a), pl.BlockSpec(blk_b, map_b)]
    args = [_in_hbm(a), _in_hbm(b)]
    if after is not None:
        in_specs.append(pl.BlockSpec(memory_space=pl.ANY))
        args.append(after)
    return pl.pallas_call(
        body, name=name, grid=grid, in_specs=in_specs, out_specs=pl.BlockSpec(blk_o, map_o),
        out_shape=_out_hbm(out_shape, out_dtype),
        scratch_shapes=[] if gk == 1 else [pltpu.VMEM(tuple(d for d in blk_o if d is not None), F32)],
        compiler_params=_params("parallel", "parallel", "arbitrary"),
    )(*args)


TOK_MM_TILE = 256


def _mm_tokens(a, w, *, blk_a, map_a, pieces, res=None, after=None, then=None, name):
    n_in = 2 + (res is not None) + (after is not None) + (0 if then is None else len(then) - 1)

    def accumulate(ref, part):
        @pl.when(pl.program_id(0) == 0)
        def _():
            ref[...] = part

        @pl.when(pl.program_id(0) > 0)
        def _():
            ref[...] += part

    def body(*refs):
        a_ref, w_ref = refs[:2]
        extra, outs = refs[n_in - (0 if then is None else len(then) - 1):n_in], refs[n_in:]
        total = None
        for idx, row, n in pieces:
            av = a_ref[...] if idx is None else a_ref[idx]
            prod = _dot(av, w_ref[row:row + n, :])
            total = prod if total is None else total + prod
        if res is not None:
            total = total + refs[2][...]
        if then is None:
            outs[0][...] = total
        elif then[0] == "rms_bwd":
            dx, part = _rms_bwd_tile(total, extra[0][...], extra[1][...], extra[2][...])
            outs[0][...] = dx
            accumulate(outs[1], part)
        else:
            lpart, dx, part = _loss_tile(total, extra[0][...], extra[1][...])
            outs[1][...] = dx
            outs[2][...] = dx.astype(BF)
            accumulate(outs[0], lpart)
            accumulate(outs[3], part)

    tile = pl.BlockSpec((TOK_MM_TILE, D_MODEL), lambda i: (i, 0))
    vec = _const_spec((1, D_MODEL))
    big = _out_hbm((SEQ, D_MODEL), F32)
    small = _out_hbm((1, D_MODEL), F32)
    in_specs = [pl.BlockSpec(blk_a, map_a), pl.BlockSpec(w.shape, lambda i: (0, 0), pipeline_mode=pl.Buffered(1))]
    args = [a, w]
    if res is not None:
        in_specs.append(tile)
        args.append(res)
    if after is not None:
        in_specs.append(pl.BlockSpec(memory_space=pl.ANY))
        args.append(after)
    if then is None:
        out_specs, out_shape = tile, big
    elif then[0] == "rms_bwd":
        in_specs += [tile, vec, tile]
        out_specs, out_shape = [tile, vec], [big, small]
    else:
        in_specs += [vec, tile]
        out_specs = [_const_spec((1, 128)), tile, tile, vec]
        out_shape = [_out_hbm((1, 128), F32), big, _out_hbm((SEQ, D_MODEL), BF), small]
    if then is not None:
        args += list(then[1:])
    return pl.pallas_call(
        body, name=name, grid=(SEQ // TOK_MM_TILE,), in_specs=in_specs, out_specs=out_specs, out_shape=out_shape,
        compiler_params=_params("parallel" if then is None else "arbitrary"),
    )(*[_in_hbm(t) for t in args])


def _rms_fwd(x, g, after, name):
    def body(x_ref, g_ref, after_ref, o_ref):
        del after_ref
        xv = x_ref[...]
        r = lax.rsqrt(jnp.mean(xv * xv, axis=-1, keepdims=True) + EPS)
        o_ref[...] = (xv * r * g_ref[...]).astype(BF)

    tile = pl.BlockSpec((TOK_TILE, D_MODEL), lambda i: (i, 0))
    return pl.pallas_call(
        body, name=name, grid=(SEQ // TOK_TILE,),
        in_specs=[tile, _const_spec((1, D_MODEL)), pl.BlockSpec(memory_space=pl.ANY)], out_specs=tile,
        out_shape=_out_hbm((SEQ, D_MODEL), BF), compiler_params=_params("parallel"),
    )(*map(_in_hbm, (x, g)), after)


def _rms_bwd_tile(dyv, xv, gv, dresv):
    r = lax.rsqrt(jnp.mean(xv * xv, axis=-1, keepdims=True) + EPS)
    xn = xv * r
    dxn = dyv * gv
    return dresv + r * (dxn - xn * jnp.mean(dxn * xn, axis=-1, keepdims=True)), jnp.sum(dyv * xn, axis=0, keepdims=True)


def _loss_tile(xv, gv, tv):
    r = lax.rsqrt(jnp.mean(xv * xv, axis=-1, keepdims=True) + EPS)
    xn = xv * r
    err = xn * gv - tv
    lpart = jnp.full((1, 128), 0.5 * jnp.sum(jnp.mean(err * err, axis=-1, keepdims=True)), F32)
    dyv = err * (1.0 / D_MODEL)
    dxn = dyv * gv
    return lpart, r * (dxn - xn * jnp.mean(dxn * xn, axis=-1, keepdims=True)), jnp.sum(dyv * xn, axis=0, keepdims=True)


def _pool_counts(w):
    pos = lax.broadcasted_iota(jnp.int32, (SEQ, 1), 0).astype(F32)
    return jnp.minimum(pos + 1.0, float(w))


def _pool_window(u, w, ext):
    ext[pl.ds(POOL_HALO, SEQ), :] = u
    win = u
    for j in range(1, w):
        win = win + ext[pl.ds(POOL_HALO - j, SEQ), :]
    return win / _pool_counts(w) - u


def _pool_fwd(zcat, w_grp, scale):
    def body(z_ref, w_ref, s_ref, o_ref, ext):
        ext[pl.ds(0, POOL_HALO), :] = jnp.zeros((POOL_HALO, POOL_GD), F32)
        for g, w in enumerate(POOL_WINDOWS):
            cols = slice(g * POOL_GD, (g + 1) * POOL_GD)
            p = _pool_window(z_ref[:, cols].astype(F32), w, ext)
            o_ref[:, cols] = (_dot(p, w_ref[g]) * s_ref[:, cols]).astype(BF)

    return pl.pallas_call(
        body, name="pool_fwd", grid=(1,),
        in_specs=[pl.BlockSpec((SEQ, POOL_WIDTH), lambda i: (0, C_POOL // POOL_WIDTH)),
                  _const_spec((4, POOL_GD, POOL_GD)), _const_spec((1, POOL_WIDTH))],
        out_specs=_const_spec((SEQ, POOL_WIDTH)), out_shape=_out_hbm((SEQ, POOL_WIDTH), BF),
        scratch_shapes=[pltpu.VMEM((POOL_HALO + SEQ, POOL_GD), F32)], compiler_params=_params("arbitrary"),
    )(*map(_in_hbm, (zcat, w_grp, scale)))


def _pool_bwd(dzcat, zcat, dps, w_grp, scale):
    def body(dz_in, z_ref, dps_ref, w_ref, s_ref, dz_ref, dw_ref, dsc_ref, ext, ext2):
        del dz_in
        ext[pl.ds(0, POOL_HALO), :] = jnp.zeros((POOL_HALO, POOL_GD), F32)
        ext2[pl.ds(SEQ, POOL_HALO), :] = jnp.zeros((POOL_HALO, POOL_GD), F32)
        for g, w in enumerate(POOL_WINDOWS):
            cols = slice(g * POOL_GD, (g + 1) * POOL_GD)
            p = _pool_window(z_ref[:, cols].astype(F32), w, ext)
            wg = w_ref[g]
            pg = _dot(p, wg)
            dpsv = dps_ref[:, cols]
            dsc_ref[:, cols] = jnp.sum(dpsv * pg, axis=0, keepdims=True)
            dpg = dpsv * s_ref[:, cols]
            dw_ref[g] = _dot(p, dpg, ta=True)
            dp = _dot(dpg, wg, tb=True)
            dpc = dp / _pool_counts(w)
            ext2[pl.ds(0, SEQ), :] = dpc
            du = dpc
            for j in range(1, w):
                du = du + ext2[pl.ds(j, SEQ), :]
            dz_ref[:, cols] = (du - dp).astype(BF)

    return pl.pallas_call(
        body, name="pool_bwd", grid=(1,),
        in_specs=[pl.BlockSpec(memory_space=pl.ANY),
                  pl.BlockSpec((SEQ, POOL_WIDTH), lambda i: (0, C_POOL // POOL_WIDTH)),
                  _const_spec((SEQ, POOL_WIDTH)), _const_spec((4, POOL_GD, POOL_GD)), _const_spec((1, POOL_WIDTH))],
        out_specs=[pl.BlockSpec((SEQ, POOL_WIDTH), lambda i: (0, C_POOL // POOL_WIDTH)),
                   _const_spec((4, POOL_GD, POOL_GD)), _const_spec((1, POOL_WIDTH))],
        out_shape=[_out_hbm((SEQ, N_DZ), BF), _out_hbm((4, POOL_GD, POOL_GD), F32),
                   _out_hbm((1, POOL_WIDTH), F32)],
        scratch_shapes=[pltpu.VMEM((POOL_HALO + SEQ, POOL_GD), F32), pltpu.VMEM((SEQ + POOL_HALO, POOL_GD), F32)],
        input_output_aliases={0: 0}, compiler_params=_params("arbitrary"),
    )(*map(_in_hbm, (dzcat, zcat, dps, w_grp, scale)))


GK_TILE = 512


GK_ROWS = pl.BlockSpec((GK_PAD, D_MODEL), lambda i: (C_GK // GK_PAD, 0))


def _gk_fwd(h, wt_cat, wgk_pad, b_gk):
    def body(h_ref, wt_ref, w_ref, b_ref, la_ref):
        z_gk = _dot(h_ref[...], wt_ref[...], tb=True)
        pre = _dot(z_gk, w_ref[...]) + b_ref[...]
        la_ref[...] = (jnp.minimum(pre, 0.0) - jnp.log(1.0 + jnp.exp(-jnp.abs(pre)))) * (1.0 / GATE_NORM)

    return pl.pallas_call(
        body, name="gk_fwd", grid=(SEQ // GK_TILE,),
        in_specs=[pl.BlockSpec((GK_TILE, D_MODEL), lambda i: (i, 0)), GK_ROWS,
                  _const_spec((GK_PAD, GLA_DK)), _const_spec((1, GLA_DK))],
        out_specs=pl.BlockSpec((GK_TILE, GLA_DK), lambda i: (i, 0)),
        out_shape=_out_hbm((SEQ, GLA_DK), F32), compiler_params=_params("parallel"),
    )(*map(_in_hbm, (h, wt_cat, wgk_pad, b_gk)))


def _gk_bwd(dzcat, dla, h, wt_cat, wgk_pad, b_gk):
    def body(dz_in, dla_ref, h_ref, wt_ref, w_ref, b_ref, dz_ref, dw_ref, db_ref):
        del dz_in
        wv = w_ref[...]
        z_gk = _dot(h_ref[...], wt_ref[...], tb=True)
        pre = _dot(z_gk, wv) + b_ref[...]
        dpre = dla_ref[...] * (1.0 / GATE_NORM) * (1.0 - _sigmoid(pre))
        dz_ref[...] = _dot(dpre, wv, tb=True).astype(BF)
        dwp = _dot(z_gk, dpre, ta=True)[:GATE_RANK]
        dbp = jnp.sum(dpre, axis=0, keepdims=True)

        @pl.when(pl.program_id(0) == 0)
        def _():
            dw_ref[...] = dwp
            db_ref[...] = dbp

        @pl.when(pl.program_id(0) > 0)
        def _():
            dw_ref[...] += dwp
            db_ref[...] += dbp

    return pl.pallas_call(
        body, name="gk_bwd", grid=(SEQ // GK_TILE,),
        in_specs=[pl.BlockSpec(memory_space=pl.ANY), pl.BlockSpec((GK_TILE, GLA_DK), lambda i: (i, 0)),
                  pl.BlockSpec((GK_TILE, D_MODEL), lambda i: (i, 0)), GK_ROWS, _const_spec((GK_PAD, GLA_DK)),
                  _const_spec((1, GLA_DK))],
        out_specs=[pl.BlockSpec((GK_TILE, GK_PAD), lambda i: (i, C_GK // GK_PAD)), _const_spec((GATE_RANK, GLA_DK)),
                   _const_spec((1, GLA_DK))],
        out_shape=[_out_hbm((SEQ, N_DZ), BF), _out_hbm((GATE_RANK, GLA_DK), F32),
                   _out_hbm((1, GLA_DK), F32)],
        input_output_aliases={0: 0}, compiler_params=_params("arbitrary"),
    )(*map(_in_hbm, (dzcat, dla, h, wt_cat, wgk_pad, b_gk)))


GLA_ROWS = GLA_CPS * CHUNK
GLA_STEPS = SEQ // GLA_ROWS
QKV_W = 2048


def _tri():
    return lax.broadcasted_iota(jnp.int32, (CHUNK, CHUNK), 0) >= lax.broadcasted_iota(jnp.int32, (CHUNK, CHUNK), 1)


def _chunk_cumsum(la_ref, rows):
    return _dot_exact(_tri().astype(F32), la_ref[rows, :])


def _gla_chunk(qkv_ref, la_ref, rows, h, bc_all):
    tri = _tri()
    q = qkv_ref[rows, h * HK:(h + 1) * HK].astype(F32) * (HK ** -0.5)
    k = qkv_ref[rows, GLA_DK + h * HK:GLA_DK + (h + 1) * HK].astype(F32)
    v = qkv_ref[rows, 2 * GLA_DK + h * HV:2 * GLA_DK + (h + 1) * HV].astype(BF)
    la = la_ref[rows, h * HK:(h + 1) * HK]
    bc = bc_all[:, h * HK:(h + 1) * HK]
    e_pos, e_neg = jnp.exp(bc), jnp.exp(-bc)
    dl = jnp.exp(jnp.sum(la, axis=0, keepdims=True))
    q_fw, q_bw, k_fw, k_bw = q * e_pos, q * e_neg, k * e_neg, k * e_pos
    scores = jnp.where(tri, _dot(q_fw, k_fw, tb=True), _dot(q_bw, k_bw, tb=True))
    return tri, v, e_pos, e_neg, dl, q_fw, q_bw, k_fw, k_bw, scores


def _gla_fwd(zcat, la, after):
    def body(qkv_ref, la_ref, after_ref, o_ref, st_ref, state):
        del after_ref

        @pl.when(pl.program_id(0) == 0)
        def _():
            state[...] = jnp.zeros_like(state)

        for c in range(GLA_CPS):
            rows = slice(c * CHUNK, (c + 1) * CHUNK)
            bc_all = _chunk_cumsum(la_ref, rows)
            for h in range(HEADS):
                _, v, _, _, dl, q_fw, _, k_fw, _, scores = _gla_chunk(qkv_ref, la_ref, rows, h, bc_all)
                st = state[h]
                st_ref[c, h] = st
                o_ref[rows, h * HV:(h + 1) * HV] = _dot(scores, v) + _dot(q_fw, st, tb=True)
                state[h] = st * dl + _dot(v, k_fw * dl, ta=True)

    return pl.pallas_call(
        body, name="gla_fwd", grid=(GLA_STEPS,),
        in_specs=[pl.BlockSpec((GLA_ROWS, QKV_W), lambda i: (i, 0)), pl.BlockSpec((GLA_ROWS, GLA_DK), lambda i: (i, 0)),
                  pl.BlockSpec(memory_space=pl.ANY)],
        out_specs=[pl.BlockSpec((GLA_ROWS, D_MODEL), lambda i: (i, 0)),
                   pl.BlockSpec((GLA_CPS, HEADS, HV, HK), lambda i: (i, 0, 0, 0))],
        out_shape=[_out_hbm((SEQ, D_MODEL), F32),
                   _out_hbm((SEQ // CHUNK, HEADS, HV, HK), F32)],
        scratch_shapes=[pltpu.VMEM((HEADS, HV, HK), F32)], compiler_params=_params("arbitrary"),
    )(*map(_in_hbm, (zcat, la)), after)


def _gla_bwd(dzcat, zcat, la, d_o, states):
    def body(dz_in, qkv_ref, la_ref, do_ref, st_ref, dqkv_ref, dla_ref, dstate):
        del dz_in

        @pl.when(pl.program_id(0) == 0)
        def _():
            dstate[...] = jnp.zeros_like(dstate)

        last_row = lax.broadcasted_iota(jnp.int32, (CHUNK, HK), 0) == CHUNK - 1
        upper = (lax.broadcasted_iota(jnp.int32, (CHUNK, CHUNK), 0)
                 <= lax.broadcasted_iota(jnp.int32, (CHUNK, CHUNK), 1)).astype(F32)
        for c in reversed(range(GLA_CPS)):
            rows = slice(c * CHUNK, (c + 1) * CHUNK)
            bc_all = _chunk_cumsum(la_ref, rows)
            dbs = []
            for h in range(HEADS):
                tri, v, e_pos, e_neg, dl, q_fw, q_bw, k_fw, k_bw, scores = _gla_chunk(qkv_ref, la_ref, rows, h, bc_all)
                st = st_ref[c, h]
                dst = dstate[h]
                d_out = do_ref[rows, h * HV:(h + 1) * HV].astype(BF)
                k_dec = k_fw * dl
                dp = _dot(d_out, v, tb=True)
                dp_fw = jnp.where(tri, dp, 0.0)
                dp_bw = jnp.where(tri, 0.0, dp)
                dv = _dot(scores, d_out, ta=True) + _dot(k_dec, dst, tb=True)
                dk_dec = _dot(v, dst)
                dq_fw = _dot(dp_fw, k_fw) + _dot(d_out, st)
                dk_fw = _dot(dp_fw, q_fw, ta=True) + dk_dec * dl
                dq_bw = _dot(dp_bw, k_bw)
                dk_bw = _dot(dp_bw, q_bw, ta=True)
                ddl = jnp.sum(st * dst, axis=0, keepdims=True) + jnp.sum(k_fw * dk_dec, axis=0, keepdims=True)
                dstate[h] = dst * dl + _dot(d_out, q_fw, ta=True)
                dq = (dq_fw * e_pos + dq_bw * e_neg) * (HK ** -0.5)
                dk = dk_fw * e_neg + dk_bw * e_pos
                dbs.append(dq_fw * q_fw - dk_fw * k_fw - dq_bw * q_bw + dk_bw * k_bw + jnp.where(last_row, ddl * dl, 0.0))
                dqkv_ref[rows, h * HK:(h + 1) * HK] = dq.astype(BF)
                dqkv_ref[rows, GLA_DK + h * HK:GLA_DK + (h + 1) * HK] = dk.astype(BF)
                dqkv_ref[rows, 2 * GLA_DK + h * HV:2 * GLA_DK + (h + 1) * HV] = dv.astype(BF)
            dla_ref[rows, :] = _dot_exact(upper, jnp.concatenate(dbs, axis=1))

    rev = lambda i: (GLA_STEPS - 1 - i, 0)
    return pl.pallas_call(
        body, name="gla_bwd", grid=(GLA_STEPS,),
        in_specs=[pl.BlockSpec(memory_space=pl.ANY), pl.BlockSpec((GLA_ROWS, QKV_W), rev),
                  pl.BlockSpec((GLA_ROWS, GLA_DK), rev), pl.BlockSpec((GLA_ROWS, D_MODEL), rev),
                  pl.BlockSpec((GLA_CPS, HEADS, HV, HK), lambda i: (GLA_STEPS - 1 - i, 0, 0, 0))],
        out_specs=[pl.BlockSpec((GLA_ROWS, QKV_W), rev), pl.BlockSpec((GLA_ROWS, GLA_DK), rev)],
        out_shape=[_out_hbm((SEQ, N_DZ), BF), _out_hbm((SEQ, GLA_DK), F32)],
        scratch_shapes=[pltpu.VMEM((HEADS, HV, HK), F32)], input_output_aliases={0: 0},
        compiler_params=_params("arbitrary"),
    )(*map(_in_hbm, (dzcat, zcat, la, d_o, states)))


def _silu_parts(x):
    s = _sigmoid(x)
    return x * s, s * (1.0 + x * (1.0 - s))


def _post_gla_fwd(o, zcat, g_head):
    def body(o_ref, zog_ref, g_ref, out_ref):
        for h in range(HEADS):
            cols = slice(h * HV, (h + 1) * HV)
            ov = o_ref[:, cols]
            r = lax.rsqrt(jnp.mean(ov * ov, axis=-1, keepdims=True) + EPS)
            act, _ = _silu_parts(zog_ref[:, cols].astype(F32))
            out_ref[:, cols] = (ov * r * g_ref[...] * act).astype(BF)

    tile = pl.BlockSpec((TOK_TILE, D_MODEL), lambda i: (i, 0))
    return pl.pallas_call(
        body, name="post_gla_fwd", grid=(SEQ // TOK_TILE,),
        in_specs=[tile, pl.BlockSpec((TOK_TILE, D_MODEL), lambda i: (i, C_OG // D_MODEL)), _const_spec((1, HV))],
        out_specs=tile, out_shape=_out_hbm((SEQ, D_MODEL), BF), compiler_params=_params("parallel"),
    )(*map(_in_hbm, (o, zcat, g_head)))


def _post_gla_bwd(dzcat, dy_gla, w_gla_proj, o, zcat, g_head, after):
    def body(dz_in, dyg_ref, w_ref, o_ref, zog_ref, g_ref, after_ref, dz_ref, do_ref, dg_ref):
        del dz_in, after_ref
        dog = _dot(dyg_ref[...], w_ref[...], tb=True)
        gpart = jnp.zeros((1, HV), F32)
        gv = g_ref[...]
        for h in range(HEADS):
            cols = slice(h * HV, (h + 1) * HV)
            ov = o_ref[:, cols]
            r = lax.rsqrt(jnp.mean(ov * ov, axis=-1, keepdims=True) + EPS)
            on = ov * r
            act, dact = _silu_parts(zog_ref[:, cols].astype(F32))
            dogv = dog[:, cols]
            dz_ref[:, cols] = (dogv * on * gv * dact).astype(BF)
            d_on_g = dogv * act
            gpart = gpart + jnp.sum(d_on_g * on, axis=0, keepdims=True)
            dxn = d_on_g * gv
            do_ref[:, cols] = (r * (dxn - on * jnp.mean(dxn * on, axis=-1, keepdims=True))).astype(BF)

        @pl.when(pl.program_id(0) == 0)
        def _():
            dg_ref[...] = gpart

        @pl.when(pl.program_id(0) > 0)
        def _():
            dg_ref[...] += gpart

    tile = pl.BlockSpec((TOK_TILE, D_MODEL), lambda i: (i, 0))
    ogspec = pl.BlockSpec((TOK_TILE, D_MODEL), lambda i: (i, C_OG // D_MODEL))
    return pl.pallas_call(
        body, name="post_gla_bwd", grid=(SEQ // TOK_TILE,),
        in_specs=[pl.BlockSpec(memory_space=pl.ANY), tile, _const_spec((D_MODEL, D_MODEL)), tile, ogspec,
                  _const_spec((1, HV)), pl.BlockSpec(memory_space=pl.ANY)],
        out_specs=[ogspec, tile, _const_spec((1, HV))],
        out_shape=[_out_hbm((SEQ, N_DZ), BF), _out_hbm((SEQ, D_MODEL), BF),
                   _out_hbm((1, HV), F32)],
        input_output_aliases={0: 0}, compiler_params=_params("arbitrary"),
    )(*map(_in_hbm, (dzcat, dy_gla, w_gla_proj, o, zcat, g_head)), after)


GATE_W = 2 * D_MODEL


def _mix_out_fwd(ps, og, zcat, x, w_pool_proj, w_gla_proj, w_out, b_gate, g_ffn, after):
    def body(ps_ref, og_ref, zg_ref, x_ref, wpp_ref, wgp_ref, wout_ref, b_ref, g_ref, after_ref,
             yp_ref, yg_ref, mixed_ref, x1_ref, h2_ref):
        del after_ref
        y_pool = _dot(ps_ref[...], wpp_ref[...])
        y_gla = _dot(og_ref[...], wgp_ref[...])
        yp_ref[...] = y_pool.astype(BF)
        yg_ref[...] = y_gla.astype(BF)
        g0 = _sigmoid(zg_ref[:, :D_MODEL].astype(F32) + b_ref[:, :D_MODEL])
        g1 = _sigmoid(zg_ref[:, D_MODEL:].astype(F32) + b_ref[:, D_MODEL:])
        mixed = (g0 * y_pool + g1 * y_gla).astype(BF)
        mixed_ref[...] = mixed
        x1 = x_ref[...] + _dot(mixed, wout_ref[...])
        x1_ref[...] = x1
        r = lax.rsqrt(jnp.mean(x1 * x1, axis=-1, keepdims=True) + EPS)
        h2_ref[...] = (x1 * r * g_ref[...]).astype(BF)

    tile = pl.BlockSpec((TOK_TILE, D_MODEL), lambda i: (i, 0))
    resident = lambda shape: pl.BlockSpec(shape, lambda i: (0, 0), pipeline_mode=pl.Buffered(1))
    f32, bf16 = _out_hbm((SEQ, D_MODEL), F32), _out_hbm((SEQ, D_MODEL), BF)
    return pl.pallas_call(
        body, name="mix_out_fwd", grid=(SEQ // TOK_TILE,),
        in_specs=[pl.BlockSpec((TOK_TILE, POOL_WIDTH), lambda i: (i, 0)), tile,
                  pl.BlockSpec((TOK_TILE, GATE_W), lambda i: (i, C_GATE // GATE_W)), tile,
                  resident((POOL_WIDTH, D_MODEL)), resident((D_MODEL, D_MODEL)), resident((D_MODEL, D_MODEL)),
                  _const_spec((1, GATE_W)), _const_spec((1, D_MODEL)), pl.BlockSpec(memory_space=pl.ANY)],
        out_specs=[tile] * 5, out_shape=[bf16, bf16, bf16, f32, bf16], compiler_params=_params("parallel"),
    )(*map(_in_hbm, (ps, og, zcat, x, w_pool_proj, w_gla_proj, w_out, b_gate, g_ffn)), after)


def _mix_bwd(dx1, w_out, zcat, b_gate, y_pool, y_gla):
    def body(dx_ref, w_ref, zg_ref, b_ref, yp_ref, yg_ref, dz_ref, dyp_ref, dyg_ref, db_ref):
        dm = _dot(dx_ref[...], w_ref[...], tb=True)
        g0 = _sigmoid(zg_ref[:, :D_MODEL].astype(F32) + b_ref[:, :D_MODEL])
        g1 = _sigmoid(zg_ref[:, D_MODEL:].astype(F32) + b_ref[:, D_MODEL:])
        dyp_ref[...] = (dm * g0).astype(BF)
        dyg_ref[...] = (dm * g1).astype(BF)
        dz0 = dm * yp_ref[...].astype(F32) * g0 * (1.0 - g0)
        dz1 = dm * yg_ref[...].astype(F32) * g1 * (1.0 - g1)
        dz_ref[:, :D_MODEL] = dz0.astype(BF)
        dz_ref[:, D_MODEL:] = dz1.astype(BF)
        b0 = jnp.sum(dz0, axis=0, keepdims=True)
        b1 = jnp.sum(dz1, axis=0, keepdims=True)

        @pl.when(pl.program_id(0) == 0)
        def _():
            db_ref[:, :D_MODEL] = b0
            db_ref[:, D_MODEL:] = b1

        @pl.when(pl.program_id(0) > 0)
        def _():
            db_ref[:, :D_MODEL] += b0
            db_ref[:, D_MODEL:] += b1

    tile = pl.BlockSpec((TOK_TILE, D_MODEL), lambda i: (i, 0))
    gspec = pl.BlockSpec((TOK_TILE, GATE_W), lambda i: (i, C_GATE // GATE_W))
    return pl.pallas_call(
        body, name="mix_bwd", grid=(SEQ // TOK_TILE,),
        in_specs=[tile, _const_spec((D_MODEL, D_MODEL)), gspec, _const_spec((1, GATE_W)), tile, tile],
        out_specs=[gspec, tile, tile, _const_spec((1, GATE_W))],
        out_shape=[_out_hbm((SEQ, N_DZ), BF), _out_hbm((SEQ, D_MODEL), BF),
                   _out_hbm((SEQ, D_MODEL), BF), _out_hbm((1, GATE_W), F32)],
        compiler_params=_params("arbitrary"),
    )(*map(_in_hbm, (dx1, w_out, zcat, b_gate, y_pool, y_gla)))


N_TOK_TILES = SEQ // TOK_TILE
HALO_PER_TILE = TOK_TILE // HALO


LANE_TILES = tuple((lo, min(128, FF_BLK - lo)) for lo in range(0, FF_BLK, 128))


def _taps(w_ref, b_ref, half, lanes, rows):
    shape = (rows, lanes.stop - lanes.start)
    return ([jnp.broadcast_to(w_ref[half, j:j + 1, lanes], shape) for j in range(3)],
            jnp.broadcast_to(b_ref[half, :, lanes], shape))


def _conv_strips(u_ref, ub_ref, ua_ref, taps, lanes, width, n_strips, first):
    row = lax.broadcasted_iota(jnp.int32, (HALO, width), 0)
    prev = [[pltpu.roll(jnp.where(first, 0.0, ub_ref[half, :, lanes]), k, 0) for k in (1, 2)] for half in range(2)]
    for s in range(n_strips + (ua_ref is not None)):
        u3, conv = [], []
        for half in range(2):
            cur = u_ref[half, s * HALO:(s + 1) * HALO, lanes] if s < n_strips else ua_ref[half, :, lanes]
            rolled = [pltpu.roll(cur, k, 0) for k in (1, 2)]
            frames = [jnp.where(row >= 2, rolled[1], prev[half][1]), jnp.where(row >= 1, rolled[0], prev[half][0]), cur]
            prev[half] = rolled
            w3, bias = taps[half]
            u3.append(frames)
            conv.append(bias + frames[0] * w3[0] + frames[1] * w3[1] + frames[2] * w3[2])
        yield s, u3, conv


def _pair_specs(pairs):
    tile = pl.BlockSpec((pairs, None, TOK_TILE, FF_BLK), lambda b, i: (0, b, i, 0))
    before = pl.BlockSpec((pairs, None, HALO, FF_BLK), lambda b, i: (0, b, jnp.maximum(i * HALO_PER_TILE - 1, 0), 0))
    after = pl.BlockSpec((pairs, None, HALO, FF_BLK),
                         lambda b, i: (0, b, jnp.minimum((i + 1) * HALO_PER_TILE, SEQ // HALO - 1), 0))

    def vec(rows):
        return pl.BlockSpec((2, None, rows, FF_BLK), lambda b, i: (0, b, 0, 0))

    return tile, before, after, vec


N_STRIPS = TOK_TILE // HALO


def _up_conv_fwd(h2, wt_up, w_conv, b_conv):
    steps = N_TOK_TILES // 2

    def body(h_ref, h_next, wg_ref, wv_ref, w_ref, b_ref, u_ref, a_ref, buf_a, buf_b, carry):
        j = pl.program_id(1)

        def project(hv, buf):
            buf[0] = _dot(hv, wg_ref[...], tb=True)
            buf[1] = _dot(hv, wv_ref[...], tb=True)

        def conv(buf, row0):
            u_ref[:, row0:row0 + TOK_TILE, :] = buf[...]
            for lo, width in LANE_TILES:
                lanes = slice(lo, lo + width)
                taps = [_taps(w_ref, b_ref, half, lanes, HALO) for half in range(2)]
                pending = None
                for s, _, (cg, cv) in _conv_strips(buf, carry, None, taps, lanes, width, N_STRIPS, False):
                    act = cg * _sigmoid(cg) * cv
                    if s % 2 == 0:
                        pending = act
                    else:
                        a_ref[0, row0 + (s - 1) * HALO:row0 + (s + 1) * HALO, lanes] = (
                            jnp.concatenate([pending, act], axis=0).astype(BF))
            carry[...] = buf[:, TOK_TILE - HALO:, :]

        @pl.when(j == 0)
        def _():
            project(h_ref[0:TOK_TILE, :], buf_a)
            carry[...] = jnp.zeros_like(carry)

        project(h_ref[TOK_TILE:, :], buf_b)
        conv(buf_a, 0)
        project(h_next[...], buf_a)
        conv(buf_b, TOK_TILE)

    w_blk = lambda half: pl.BlockSpec((FF_BLK, D_MODEL), lambda b, j: (b + 4 * half, 0))
    vec = lambda rows: pl.BlockSpec((2, None, rows, FF_BLK), lambda b, j: (0, b, 0, 0))
    u_buf = pltpu.VMEM((2, TOK_TILE, FF_BLK), F32)
    return pl.pallas_call(
        body, name="up_conv_fwd", grid=(4, steps),
        in_specs=[pl.BlockSpec((2 * TOK_TILE, D_MODEL), lambda b, j: (j, 0)),
                  pl.BlockSpec((TOK_TILE, D_MODEL), lambda b, j: (jnp.minimum(2 * j + 2, N_TOK_TILES - 1), 0)),
                  w_blk(0), w_blk(1), vec(3), vec(1)],
        out_specs=[pl.BlockSpec((2, None, 2 * TOK_TILE, FF_BLK), lambda b, j: (0, b, j, 0)),
                   pl.BlockSpec((1, None, 2 * TOK_TILE, FF_BLK), lambda b, j: (0, b, j, 0))],
        out_shape=[_out_hbm((2, 4, SEQ, FF_BLK), F32), _out_hbm((1, 4, SEQ, FF_BLK), BF)],
        scratch_shapes=[u_buf, u_buf, pltpu.VMEM((2, HALO, FF_BLK), F32)],
        compiler_params=_params("parallel", "arbitrary"),
    )(*map(_in_hbm, (h2, h2, wt_up, wt_up, w_conv, b_conv)))


def _conv_bwd(u, da, w_conv, b_conv):
    def body(u_ref, ub_ref, ua_ref, da_ref, daa_ref, w_ref, b_ref, du_ref, dw_ref, db_ref):
        i = pl.program_id(1)

        @pl.when(i == 0)
        def _():
            dw_ref[...] = jnp.zeros_like(dw_ref)
            db_ref[...] = jnp.zeros_like(db_ref)

        for lo, width in LANE_TILES:
            lanes = slice(lo, lo + width)
            row = lax.broadcasted_iota(jnp.int32, (HALO, width), 0)
            taps = [_taps(w_ref, b_ref, half, lanes, HALO) for half in range(2)]
            acc_w = [[jnp.zeros((HALO, width), F32) for _ in range(3)] for _ in range(2)]
            acc_b = [jnp.zeros((HALO, width), F32) for _ in range(2)]
            da_pair, pending = None, [None, None]
            dc_prev, up_prev = [None, None], [None, None]
            for s, u3, (cg, cv) in _conv_strips(u_ref, ub_ref, ua_ref, taps, lanes, width, N_STRIPS, i == 0):
                act, dact = _silu_parts(cg)
                if s == N_STRIPS:
                    da = jnp.where(i < N_TOK_TILES - 1, daa_ref[0, :, lanes].astype(F32), 0.0)
                elif s % 2 == 0:
                    da_pair = da_ref[0, s * HALO:(s + 2) * HALO, lanes].astype(F32)
                    da = da_pair[:HALO]
                else:
                    da = da_pair[HALO:]
                dc = (da * cv * dact, da * act)
                for half in range(2):
                    up = [pltpu.roll(dc[half], HALO - k, 0) for k in (1, 2)]
                    if s < N_STRIPS:
                        for j in range(3):
                            acc_w[half][j] = acc_w[half][j] + dc[half] * u3[half][j]
                        acc_b[half] = acc_b[half] + dc[half]
                    if s >= 1:
                        w3 = taps[half][0]
                        du = (dc_prev[half] * w3[2] + jnp.where(row < HALO - 1, up_prev[half][0], up[0]) * w3[1]
                              + jnp.where(row < HALO - 2, up_prev[half][1], up[1]) * w3[0])
                        if (s - 1) % 2 == 0:
                            pending[half] = du
                        else:
                            du_ref[half, (s - 2) * HALO:s * HALO, lanes] = jnp.concatenate([pending[half], du],
                                                                                           axis=0).astype(BF)
                    dc_prev[half], up_prev[half] = dc[half], up
            for half in range(2):
                for j in range(3):
                    dw_ref[half, j:j + 1, lanes] += jnp.sum(acc_w[half][j], axis=0, keepdims=True)
                db_ref[half, :, lanes] += jnp.sum(acc_b[half], axis=0, keepdims=True)

    tile, before, after, vec = _pair_specs(2)
    da_tile, _, da_after_spec, _ = _pair_specs(1)
    return pl.pallas_call(
        body, name="conv_bwd", grid=(4, N_TOK_TILES),
        in_specs=[tile, before, after, da_tile, da_after_spec, vec(3), vec(1)],
        out_specs=[tile, vec(3), vec(1)],
        out_shape=[_out_hbm((2, 4, SEQ, FF_BLK), BF), _out_hbm((2, 4, 3, FF_BLK), F32),
                   _out_hbm((2, 4, 1, FF_BLK), F32)],
        compiler_params=_params("parallel", "arbitrary"),
    )(*map(_in_hbm, (u, u, u, da, da, w_conv, b_conv)))


W_IN_SEGMENTS = ((R_POOL, POOL_WIDTH, C_POOL), (R_QKV, QKV_W, C_QKV), (R_OG, D_MODEL, C_OG), (R_GK, GATE_RANK, C_GK),
                 (R_GATE, GATE_W, C_GATE))


def _slab_pieces(d):
    lo, hi = d * IN_SHARD, (d + 1) * IN_SHARD
    pieces = []
    for start, n, at in W_IN_SEGMENTS:
        a, b = max(lo, start), min(hi, start + n)
        if a < b:
            assert (a - lo) % 2 == 0 and (b - a) % 2 == 0 and (at + a - start) % 2 == 0
            pieces.append(((a - lo) // 2, (b - a) // 2, (at + a - start) // 2))
    return pieces


def _unshard_w_in(slabs):
    def body(slab_ref, cat_ref):
        d = pl.program_id(0)
        src = slab_ref.bitcast(jnp.uint32)
        dst = cat_ref.bitcast(jnp.uint32)

        @pl.when(d == 0)
        def _():
            cat_ref[C_GK:, :] = jnp.zeros((GK_PAD, D_MODEL), BF)

        for dd in range(N_DEV):
            @pl.when(d == dd)
            def _():
                for a, n, at in _slab_pieces(dd):
                    dst[pl.ds(at, n), :] = src[0, pl.ds(a, n), :]

    return pl.pallas_call(
        body, name="unshard_w_in", grid=(N_DEV,),
        in_specs=[pl.BlockSpec((1, IN_SHARD, D_MODEL), lambda d: (d, 0, 0))], out_specs=_const_spec((N_DZ, D_MODEL)),
        out_shape=_out_hbm((N_DZ, D_MODEL), BF), compiler_params=_params("arbitrary"),
    )(_in_hbm(slabs))


def _shard_d_w_in(d_cat):
    def body(cat_ref, slab_ref):
        d = pl.program_id(0)
        cat = cat_ref.bitcast(jnp.uint32)
        dst = slab_ref.bitcast(jnp.uint32)
        for dd in range(N_DEV):
            @pl.when(d == dd)
            def _():
                for a, n, at in _slab_pieces(dd):
                    dst[0, pl.ds(a, n), :] = cat[pl.ds(at, n), :]

    return pl.pallas_call(
        body, name="shard_d_w_in", grid=(N_DEV,), in_specs=[_const_spec((N_DZ, D_MODEL))],
        out_specs=pl.BlockSpec((1, IN_SHARD, D_MODEL), lambda d: (d, 0, 0)),
        out_shape=_out_hbm((N_DEV, IN_SHARD, D_MODEL), BF), compiler_params=_params("parallel"),
    )(_in_hbm(d_cat))


ANY = pl.BlockSpec(memory_space=pl.ANY)


def _place():
    x, y, c = lax.axis_index("x"), lax.axis_index("y"), lax.axis_index("c")
    other_chips = [(1 - x, y), (x, 1 - y), (1 - x, 1 - y)]
    return x, y, c, other_chips


SEM = pl.BlockSpec(memory_space=pltpu.SEMAPHORE)
IN_HBM = pl.BlockSpec(memory_space=pltpu.HBM)
SPLIT_PARAMS = pltpu.CompilerParams(has_side_effects=pltpu.SideEffectType.DATAFLOW_SIDE_EFFECTING)


def _gather_first(refs, send_sems, recv_sems):
    x, y, c, chips = _place()
    targets = [(x, y, 1 - c)] + [(px, py, c) for px, py in chips]
    copies = []
    for a, land in enumerate(refs):
        mine = land.at[4 * x + 2 * y + c]
        copies += [pltpu.make_async_remote_copy(src_ref=mine, dst_ref=mine, send_sem=send_sems.at[4 * a + k],
                                                recv_sem=recv_sems.at[4 * a + k], device_id=to, device_id_type=MESH)
                   for k, to in enumerate(targets)]
    return copies


def _gather_direct(refs, send_sems, recv_sems):
    x, y, c, _ = _place()
    flips = [(dx, dy, dc) for dx in (0, 1) for dy in (0, 1) for dc in (0, 1) if dx + dy + dc]
    targets = [(1 - x if dx else x, 1 - y if dy else y, 1 - c if dc else c) for dx, dy, dc in flips]
    return [pltpu.make_async_remote_copy(src_ref=refs[2 * a], dst_ref=refs[2 * a + 1].at[4 * x + 2 * y + c],
                                         send_sem=send_sems.at[7 * a + k], recv_sem=recv_sems.at[7 * a + k],
                                         device_id=to, device_id_type=MESH)
            for a in range(len(refs) // 2) for k, to in enumerate(targets)]


def _gather_second(refs, send_sems, recv_sems):
    x, y, c, chips = _place()
    copies = []
    for a, land in enumerate(refs):
        for j, (px, py) in enumerate(chips):
            block = land.at[4 * px + 2 * py + c]
            copies.append(pltpu.make_async_remote_copy(src_ref=block, dst_ref=block, send_sem=send_sems.at[3 * a + j],
                                                       recv_sem=recv_sems.at[3 * a + j], device_id=(x, y, 1 - c),
                                                       device_id_type=MESH))
    return copies


def _reduce_first(refs, send_sems, recv_sems):
    x, y, c, _ = _place()
    return [pltpu.make_async_remote_copy(src_ref=refs[2 * a].at[j, 1 - c], dst_ref=refs[2 * a + 1].at[j],
                                         send_sem=send_sems.at[4 * a + j], recv_sem=recv_sems.at[4 * a + j],
                                         device_id=(x, y, 1 - c), device_id_type=MESH)
            for a in range(len(refs) // 2) for j in range(4)]


def _reduce_second(refs, send_sems, recv_sems):
    _, _, c, chips = _place()
    return [pltpu.make_async_remote_copy(src_ref=refs[2 * a].at[2 * px + py], dst_ref=refs[2 * a + 1].at[k],
                                         send_sem=send_sems.at[3 * a + k], recv_sem=recv_sems.at[3 * a + k],
                                         device_id=(px, py, c), device_id_type=MESH)
            for a in range(len(refs) // 2) for k, (px, py) in enumerate(chips)]


def _split_start(name, groups):
    arrays = [a for g in groups for a in g[0]]
    n = len(arrays)

    def body(*refs):
        sems = refs[n:n + 2 * len(groups)]
        at = 0
        for gi, (members, _, build) in enumerate(groups):
            for cp in build(refs[at:at + len(members)], sems[2 * gi], sems[2 * gi + 1]):
                cp.start()
            at += len(members)
        refs[-1][...] = jnp.zeros_like(refs[-1])

    sem_shapes = [pltpu.SemaphoreType.DMA((g[1],)) for g in groups for _ in range(2)]
    outs = pl.pallas_call(
        body, name=name, in_specs=[IN_HBM] * n,
        out_shape=(*sem_shapes, *[_out_hbm(a.shape, a.dtype) for a in arrays], jax.ShapeDtypeStruct((8, 128), F32)),
        out_specs=(*[SEM] * len(sem_shapes), *[IN_HBM] * n, pl.BlockSpec(memory_space=pltpu.VMEM)),
        input_output_aliases={i: len(sem_shapes) + i for i in range(n)}, compiler_params=SPLIT_PARAMS,
    )(*[pltpu.with_memory_space_constraint(a, pltpu.HBM) for a in arrays])
    per_group, at = [], len(sem_shapes)
    for gi, (members, _, _) in enumerate(groups):
        per_group.append((outs[2 * gi], outs[2 * gi + 1], list(outs[at:at + len(members)])))
        at += len(members)
    return per_group, outs[-1]


def _split_wait(name, started, build, after):
    send_sems, recv_sems, arrays = started
    n = len(arrays)
    after = after if isinstance(after, (tuple, list)) else (after,)

    def body(*refs):
        for cp in build(refs[:n], refs[n], refs[n + 1]):
            cp.wait_send()
            cp.wait_recv()

    return pl.pallas_call(
        body, name=name, in_specs=[IN_HBM] * n + [SEM, SEM] + [ANY] * len(after),
        out_shape=tuple(_out_hbm(a.shape, a.dtype) for a in arrays), out_specs=tuple([IN_HBM] * n),
        input_output_aliases={i: i for i in range(n)}, compiler_params=SPLIT_PARAMS,
    )(*arrays, send_sems, recv_sems, *after)


def _placed_behind(token, arrays, name):
    n = len(arrays)

    def body(*refs):
        refs[-1][...] = jnp.zeros_like(refs[-1])

    outs = pl.pallas_call(
        body, name=name, in_specs=[IN_HBM] * n + [ANY],
        out_shape=(*[_out_hbm(a.shape, a.dtype) for a in arrays], jax.ShapeDtypeStruct((8, 128), F32)),
        out_specs=(*[IN_HBM] * n, pl.BlockSpec(memory_space=pltpu.VMEM)),
        input_output_aliases={i: i for i in range(n)},
    )(*map(_in_hbm, arrays), token)
    return outs[:n], outs[-1]


def _gather_landing(shard, me):
    return lax.dynamic_update_slice(lax.empty((N_DEV,) + shard.shape, shard.dtype), shard[None],
                                    (me,) + (0,) * shard.ndim)


ADAM_LANE_TILE = 256


def _tile_2d(rows, cols):
    for t in (256, 176, 128):
        if rows % t == 0:
            return t, cols
    return rows, ADAM_LANE_TILE


def _pair_sum(part, recv, core, name):
    _, rows, cols = recv.shape
    tr, tc = rows, cols

    def body(c_ref, p_ref, r_ref, o_ref):
        del c_ref
        o_ref[...] = (p_ref[...].astype(F32) + r_ref[...].astype(F32)).astype(BF)

    grid_spec = pltpu.PrefetchScalarGridSpec(
        num_scalar_prefetch=1, grid=(4, rows // tr, cols // tc),
        in_specs=[pl.BlockSpec((None, None, tr, tc), lambda j, i, k, c_ref: (j, c_ref[0], i, k)),
                  pl.BlockSpec((None, tr, tc), lambda j, i, k, c_ref: (j, i, k))],
        out_specs=pl.BlockSpec((None, tr, tc), lambda j, i, k, c_ref: (j, i, k)))
    return pl.pallas_call(
        body, name=name, grid_spec=grid_spec, out_shape=_out_hbm(recv.shape, BF),
        compiler_params=_params("parallel", "parallel", "parallel"),
    )(core, *map(_in_hbm, (part, recv)))


def _adamw(w, g, m, v):
    m = ADAM_B1 * m + (1.0 - ADAM_B1) * g
    v = ADAM_B2 * v + (1.0 - ADAM_B2) * (g * g)
    delta = -ADAM_LR * ((m / ADAM_C1) / (jnp.sqrt(v / ADAM_C2) + ADAM_EPS) + ADAM_WD * w)
    return delta, m, v


def _chip_sum_adamw(sums, recv, w, m, v, chip, name, lone_rows=False):
    rows, cols = w.shape
    tr, tc = _tile_2d(rows, cols)

    def body(chip_ref, s_ref, r_ref, w_ref, m_ref, v_ref, *out_refs):
        del chip_ref
        g = s_ref[...].astype(F32)
        for k in range(3):
            g = g + r_ref[k].astype(F32)
        for o_ref, t in zip(out_refs, (g,) + _adamw(w_ref[...], g, m_ref[...], v_ref[...])):
            o_ref[...] = t[:, None, :] if lone_rows else t

    tile = pl.BlockSpec((tr, tc), lambda i, k, chip_ref: (i, k))
    out_tile = pl.BlockSpec((tr, 1, tc), lambda i, k, chip_ref: (i, 0, k)) if lone_rows else tile
    grid_spec = pltpu.PrefetchScalarGridSpec(
        num_scalar_prefetch=1, grid=(rows // tr, cols // tc),
        in_specs=[pl.BlockSpec((None, tr, tc), lambda i, k, chip_ref: (chip_ref[0], i, k)),
                  pl.BlockSpec((3, tr, tc), lambda i, k, chip_ref: (0, i, k)), tile, tile, tile],
        out_specs=[out_tile] * 4)
    return pl.pallas_call(
        body, name=name, grid_spec=grid_spec,
        out_shape=[_out_hbm((rows, 1, cols) if lone_rows else (rows, cols), F32)] * 4,
        compiler_params=_params("parallel", "parallel"),
    )(chip, *map(_in_hbm, (sums, recv, w, m, v)))


def _small_sum_adamw(me, entries, loss):
    def whole(shape, squeeze=0, pick=None):
        blk = (None,) * squeeze + tuple(shape[squeeze:])
        if pick is not None:
            blk = tuple(shape[:pick]) + (None,) + tuple(shape[pick + 1:])
            return pl.BlockSpec(blk, lambda i, me_ref: (0,) * pick + (me_ref[0],) + (0,) * (len(shape) - pick - 1))
        return pl.BlockSpec(blk, lambda i, me_ref: (0,) * len(shape))

    in_specs, out_specs, out_shape, args = [], [], [], []
    for own, parts, w, m, v, sharded in entries + [loss + (None, None, None, False)]:
        in_specs += [whole(own.shape, pick=0 if sharded else None), whole(parts.shape, pick=1 if sharded else None)]
        args += [own, parts]
        if w is not None:
            lead = w.ndim - (parts.ndim - (2 if sharded else 1))
            in_specs += [whole(w.shape, squeeze=lead)] * 3
            out_specs += [whole(w.shape, squeeze=lead)] * 4
            out_shape += [_out_hbm(w.shape, F32)] * 4
            args += [w, m, v]
    out_specs.append(whole(loss[0].shape))
    out_shape.append(_out_hbm(loss[0].shape, F32))
    n = len(entries)

    def added(own_ref, p_ref, me):
        total = None
        for d in range(N_DEV):
            part = jnp.where(me == d, own_ref[...], p_ref[d])
            total = part if total is None else total + part
        return total

    def body(me_ref, *refs):
        ins, outs = refs[:5 * n + 2], refs[5 * n + 2:]
        for e in range(n):
            own_ref, p_ref, w_ref, m_ref, v_ref = ins[5 * e:5 * e + 5]
            g_out, d_out, m_out, v_out = outs[4 * e:4 * e + 4]
            g = added(own_ref, p_ref, me_ref[0])
            g_out[...] = g
            d_out[...], m_out[...], v_out[...] = _adamw(w_ref[...], g, m_ref[...], v_ref[...])
        outs[4 * n][...] = added(ins[5 * n], ins[5 * n + 1], me_ref[0])

    grid_spec = pltpu.PrefetchScalarGridSpec(num_scalar_prefetch=1, grid=(1,), in_specs=in_specs, out_specs=out_specs)
    outs = pl.pallas_call(body, name="small_sum_adamw", grid_spec=grid_spec, out_shape=out_shape,
                          compiler_params=_params("arbitrary"))(me, *map(_in_hbm, args))
    return [outs[4 * e:4 * e + 4] for e in range(n)], outs[4 * n]


MM_TILE = 512
N_MM_TILES = SEQ // MM_TILE
CAT_TILE = 512
N_CAT_TILES = N_CAT // CAT_TILE
DZ_TILE = 640


def kernel(x, g_mix, w_in, b_gate, w_gk_up, b_gk, w_pool_grp, pool_scale, g_gla_head, w_pool_proj, w_gla_proj, w_out, g_ffn, w_up, w_conv, b_conv, w_down, g_final, loss_target, m_g_mix, m_w_in, m_b_gate, m_w_gk_up, m_b_gk, m_w_pool_grp, m_pool_scale, m_g_gla_head, m_w_pool_proj, m_w_gla_proj, m_w_out, m_g_ffn, m_w_up, m_w_conv, m_b_conv, m_w_down, m_g_final, v_g_mix, v_w_in, v_b_gate, v_w_gk_up, v_b_gk, v_w_pool_grp, v_pool_scale, v_g_gla_head, v_w_pool_proj, v_w_gla_proj, v_w_out, v_g_ffn, v_w_up, v_w_conv, v_b_conv, v_w_down, v_g_final):
    xi, yi, ci = lax.axis_index("x"), lax.axis_index("y"), lax.axis_index("c")
    me = 4 * xi + 2 * yi + ci
    core = jnp.reshape(ci, (1,)).astype(jnp.int32)
    chip = jnp.reshape(2 * xi + yi, (1,)).astype(jnp.int32)
    xs, target = x[0], loss_target[0]

    big = dict(w_in=w_in[0].T, w_pool_proj=w_pool_proj[0], w_gla_proj=w_gla_proj[0], w_out=w_out[0], w_up=w_up[0].T,
               w_down=w_down[0])
    moments = dict(w_in=(m_w_in[0].T, v_w_in[0].T), w_pool_proj=(m_w_pool_proj[0], v_w_pool_proj[0]),
                   w_gla_proj=(m_w_gla_proj[0], v_w_gla_proj[0]), w_out=(m_w_out[0], v_w_out[0]),
                   w_up=(m_w_up[0].T, v_w_up[0].T), w_down=(m_w_down[0], v_w_down[0]))
    names = list(big)
    shards = {k: big[k].astype(BF) for k in names}
    shards["w_gk_up"], shards["w_conv"] = w_gk_up[0], w_conv[0]
    gather_groups = (("w_in", "w_gk_up"), ("w_pool_proj", "w_gla_proj", "w_out"), ("w_up", "w_down", "w_conv"))
    started, token = _split_start("gather_start", [
        ([_gather_landing(shards[k], me) for k in g], 4 * len(g), _gather_first) for g in gather_groups])
    (big["w_in"], *moments["w_in"], bconv4, m_w_conv, v_w_conv), token = _placed_behind(
        token, [big["w_in"], *moments["w_in"], b_conv.reshape(2, 4, 1, FF_BLK), m_w_conv, v_w_conv],
        "place_adamw_operands")

    def gather_pass(gi, after):
        lands = list(_split_wait(f"gather_wait_{gi}", started[gi], _gather_first, after))
        passed, tkn = _split_start(f"gather_pass_{gi}", [(lands, 3 * len(lands), _gather_second)])
        return passed[0], tkn

    def gather_done(gi, passed, after):
        return dict(zip(gather_groups[gi], _split_wait(f"gather_pass_wait_{gi}", passed, _gather_second, after)))

    tok = lambda i, j, k: (i, 0)
    whole = lambda i, j, k: (0, 0)
    kblk = lambda i, j, k: (k, 0)
    ff_seq = (None, None, SEQ, FF_BLK)

    h = _rms_fwd(xs, g_mix, token, "rms_mix")
    wg = gather_done(0, gather_pass(0, h)[0], h)
    wt_cat = _unshard_w_in(wg["w_in"])
    wgk_pad = jnp.pad(wg["w_gk_up"].transpose(1, 0, 2).reshape(GATE_RANK, GLA_DK), ((0, GK_PAD - GATE_RANK), (0, 0)))
    zcat = _mm(h, wt_cat, out_shape=(SEQ, N_CAT), out_dtype=BF, grid=(N_CAT_TILES, 1, 1),
               blk_a=(SEQ, D_MODEL), blk_b=(CAT_TILE, D_MODEL), blk_o=(SEQ, CAT_TILE),
               map_a=whole, map_b=lambda j, i, k: (j, 0), map_o=lambda j, i, k: (0, j), tb=True, name="mm_in")
    la = _gk_fwd(h, wt_cat, wgk_pad, b_gk)
    passed, tkn = gather_pass(1, la)
    o, states = _gla_fwd(zcat, la, tkn)
    wg = gather_done(1, passed, o)
    wpp = wg["w_pool_proj"].transpose(1, 0, 2).reshape(POOL_WIDTH, D_MODEL)
    wgp = wg["w_gla_proj"].reshape(D_MODEL, D_MODEL)
    wout = wg["w_out"].reshape(D_MODEL, D_MODEL)
    og = _post_gla_fwd(o, zcat, g_gla_head)
    ps = _pool_fwd(zcat, w_pool_grp[0], pool_scale)
    passed, tkn = gather_pass(2, (og, ps))
    y_pool, y_gla, mixed, x1, h2 = _mix_out_fwd(ps, og, zcat, xs, wpp, wgp, wout, b_gate, g_ffn, tkn)
    wg = gather_done(2, passed, h2)
    wt_up = wg["w_up"].reshape(2 * D_FF, D_MODEL)
    wdown = wg["w_down"].reshape(D_FF, D_MODEL)
    wconv4 = wg["w_conv"].reshape(2, 4, 3, FF_BLK)
    blk4 = lambda b, i, k: (b // 4, b % 4, 0, 0)
    u4, act = _up_conv_fwd(h2, wt_up, wconv4, bconv4)
    loss_part, dx2, dx2_bf, dg_final = _mm_tokens(
        act, wdown, blk_a=(None, 4, TOK_MM_TILE, FF_BLK), map_a=lambda i: (0, 0, i, 0),
        pieces=[(b, b * FF_BLK, FF_BLK) for b in range(4)], res=x1, then=("loss", g_final.reshape(1, D_MODEL), target),
        name="mm_down_loss")

    da = _mm(dx2_bf, wdown, out_shape=(1, 4, SEQ, FF_BLK), out_dtype=BF, grid=(4, 1, 1),
             blk_a=(SEQ, D_MODEL), blk_b=(FF_BLK, D_MODEL), blk_o=ff_seq,
             map_a=whole, map_b=lambda b, i, k: (b, 0), map_o=lambda b, i, k: (0, b, 0, 0), tb=True, name="mm_d_act")
    d_wdown = _mm(act, dx2_bf, out_shape=(D_FF, D_MODEL), out_dtype=BF, grid=(4, 1, 1),
                  blk_a=ff_seq, blk_b=(SEQ, D_MODEL), blk_o=(FF_BLK, D_MODEL),
                  map_a=lambda b, i, k: (0, b, 0, 0), map_b=whole, map_o=lambda b, i, k: (b, 0), ta=True,
                  name="mm_d_wdown")
    du4, d_wconv, d_bconv = _conv_bwd(u4, da, wconv4, bconv4)
    d_wt_up = _mm(du4, h2, out_shape=(2 * D_FF, D_MODEL), out_dtype=BF, grid=(N_DEV, 1, 1),
                  blk_a=ff_seq, blk_b=(SEQ, D_MODEL), blk_o=(FF_BLK, D_MODEL),
                  map_a=blk4, map_b=whole, map_o=lambda b, i, k: (b, 0), ta=True, name="mm_d_wup")
    res = {}

    def to_sibling(keys, parts):
        return [t for k in keys for t in (parts[k], lax.empty((4,) + parts[k].shape[2:], BF))], 4 * len(keys), _reduce_first

    def to_chips(keys, st, after):
        arrays = _split_wait("reduce_wait_" + keys[0], st, _reduce_first, after)
        sums = [_pair_sum(p, r, core, "pair_sum_" + k) for k, p, r in zip(keys, arrays[0::2], arrays[1::2])]
        return [t for s in sums for t in (s, lax.empty((3,) + s.shape[1:], BF))], 3 * len(keys), _reduce_second

    def reduce_start(keys, parts):
        st, tkn = _split_start("reduce_start_" + keys[0], [to_sibling(keys, parts)])
        return st[0], tkn

    def reduce_cross(keys, st, after):
        st2, tkn = _split_start("reduce_cross_" + keys[0], [to_chips(keys, st, after)])
        return st2[0], tkn

    def reduce_done(keys, st2, after):
        arrays = _split_wait("reduce_cross_wait_" + keys[0], st2, _reduce_second, after)
        for k, s, r in zip(keys, arrays[0::2], arrays[1::2]):
            outs = _chip_sum_adamw(s, r, big[k], moments[k][0], moments[k][1], chip, "adamw_" + k,
                                   lone_rows=k == "w_in")
            res[k] = [jnp.transpose(t, (1, 2, 0)) if k == "w_in" else (t.T if k == "w_up" else t)[None] for t in outs]

    ffn_keys = ("w_down", "w_up")
    ffn_red, tkn = reduce_start(ffn_keys, dict(w_down=d_wdown.reshape(4, 2, D_FF // N_DEV, D_MODEL),
                                               w_up=d_wt_up.reshape(4, 2, FF_BLK, D_MODEL)))
    dx1, dg_ffn = _mm_tokens(
        du4, wt_up, blk_a=(2, 4, TOK_MM_TILE, FF_BLK), map_a=lambda i: (0, 0, i, 0),
        pieces=[((b // 4, b % 4), b * FF_BLK, FF_BLK) for b in range(N_DEV)], after=tkn, then=("rms_bwd", x1, g_ffn, dx2),
        name="mm_d_h2_rms")

    sq_t = dict(out_shape=(D_MODEL, D_MODEL), grid=(1, 1, N_MM_TILES), blk_a=(MM_TILE, D_MODEL),
                blk_b=(MM_TILE, D_MODEL), blk_o=(D_MODEL, D_MODEL), map_a=kblk, map_b=kblk, map_o=whole, ta=True)
    d_wout = _mm(mixed, dx1, out_dtype=BF, name="mm_d_wout", **sq_t)
    dzcat, dy_pool, dy_gla, db_gate = _mix_bwd(dx1, wout, zcat, b_gate, y_pool, y_gla)
    d_wgp = _mm(og, dy_gla, out_dtype=BF, name="mm_d_wgp", **sq_t)
    mix_keys = ("w_out", "w_gla_proj")
    (ffn_red, mix_red), tkn = _split_start("reduce_cross_w_down", [
        to_chips(ffn_keys, ffn_red, db_gate),
        to_sibling(mix_keys, dict(w_out=d_wout.reshape(4, 2, D_MODEL // N_DEV, D_MODEL),
                                  w_gla_proj=d_wgp.reshape(4, 2, D_MODEL // N_DEV, D_MODEL)))])
    dzcat, d_o, dg_head = _post_gla_bwd(dzcat, dy_gla, wgp, o, zcat, g_gla_head, tkn)
    dzcat, dla = _gla_bwd(dzcat, zcat, la, d_o, states)
    dzcat, d_wgk, db_gk = _gk_bwd(dzcat, dla, h, wt_cat, wgk_pad, b_gk)
    dps = _mm(dy_pool, wpp, out_shape=(SEQ, POOL_WIDTH), out_dtype=F32, grid=(N_MM_TILES, 1, 1),
              blk_a=(MM_TILE, D_MODEL), blk_b=(POOL_WIDTH, D_MODEL), blk_o=(MM_TILE, POOL_WIDTH),
              map_a=tok, map_b=whole, map_o=tok, tb=True, name="mm_d_ps")
    d_wpp = _mm(ps, dy_pool, out_shape=(POOL_WIDTH, D_MODEL), out_dtype=F32, grid=(1, 1, N_MM_TILES),
                blk_a=(MM_TILE, POOL_WIDTH), blk_b=(MM_TILE, D_MODEL), blk_o=(POOL_WIDTH, D_MODEL),
                map_a=kblk, map_b=kblk, map_o=whole, ta=True, name="mm_d_wpp")
    dzcat, d_wgrp, d_scale = _pool_bwd(dzcat, zcat, dps, w_pool_grp[0], pool_scale)
    row = lambda t: t.reshape(1, D_MODEL)
    small = [("b_gate", db_gate, b_gate, m_b_gate, v_b_gate, False),
             ("w_gk_up", d_wgk.reshape(GATE_RANK, N_DEV, GLA_DK // N_DEV).transpose(1, 0, 2), w_gk_up, m_w_gk_up,
              v_w_gk_up, True),
             ("b_gk", db_gk, b_gk, m_b_gk, v_b_gk, False),
             ("w_pool_grp", d_wgrp, w_pool_grp, m_w_pool_grp, v_w_pool_grp, False),
             ("pool_scale", d_scale, pool_scale, m_pool_scale, v_pool_scale, False),
             ("g_gla_head", dg_head, g_gla_head, m_g_gla_head, v_g_gla_head, False),
             ("g_ffn", dg_ffn, g_ffn, m_g_ffn, v_g_ffn, False),
             ("w_conv", d_wconv.reshape(N_DEV, 3, FF_BLK), w_conv, m_w_conv, v_w_conv, True),
             ("b_conv", d_bconv.reshape(b_conv.shape), b_conv, m_b_conv, v_b_conv, False),
             ("g_final", dg_final, row(g_final), row(m_g_final), row(v_g_final), False)]

    def to_all(parts):
        return ([t for p in parts for t in (p, lax.empty((N_DEV,) + p.shape, p.dtype))], 7 * len(parts),
                _gather_direct)

    (small_sent, mix_red), tkn = _split_start("small_start", [to_all([t[1] for t in small] + [loss_part]),
                                                              to_chips(mix_keys, mix_red, dla)])
    d_wt_cat = _mm(dzcat, h, out_shape=(N_DZ, D_MODEL), out_dtype=BF, grid=(N_DZ // DZ_TILE, 1, 1),
                   blk_a=(SEQ, DZ_TILE), blk_b=(SEQ, D_MODEL), blk_o=(DZ_TILE, D_MODEL),
                   map_a=lambda j, i, k: (0, j), map_b=whole, map_o=lambda j, i, k: (j, 0), ta=True, after=tkn,
                   name="mm_d_wcat")
    in_keys = ("w_in", "w_pool_proj")
    in_red, tkn = reduce_start(in_keys, dict(
        w_in=_shard_d_w_in(d_wt_cat).reshape(4, 2, IN_SHARD, D_MODEL),
        w_pool_proj=d_wpp.reshape(POOL_WIDTH, N_DEV, D_MODEL // N_DEV).transpose(1, 0, 2).astype(BF)
        .reshape(4, 2, POOL_WIDTH, D_MODEL // N_DEV)))
    reduce_done(mix_keys, mix_red, tkn)
    in_red, tkn = reduce_cross(in_keys, in_red, res["w_out"][0])
    grad_x, dg_mix = _mm_tokens(dzcat, wt_cat, blk_a=(TOK_MM_TILE, N_DZ), map_a=lambda i: (i, 0),
                                pieces=[(None, 0, N_DZ)], after=tkn, then=("rms_bwd", xs, g_mix, dx1),
                                name="mm_d_h_rms")
    (g_mix_sent,), tkn = _split_start("g_mix_start", [to_all([dg_mix])])
    reduce_done(ffn_keys, ffn_red, (grad_x, tkn))
    sent = list(_split_wait("small_wait", small_sent, _gather_direct, res["w_down"][0]))
    small.append(("g_mix", dg_mix, g_mix, m_g_mix, v_g_mix, False))
    sent[-2:-2] = _split_wait("g_mix_wait", g_mix_sent, _gather_direct, sent[1])
    own, gathered = sent[0::2], sent[1::2]
    small_out, loss_sum = _small_sum_adamw(jnp.reshape(me, (1,)).astype(jnp.int32),
                                           [(o, p) + t[2:] for o, p, t in zip(own, gathered, small)],
                                           (own[-1], gathered[-1]))
    for t, outs in zip(small, small_out):
        res[t[0]] = list(outs)
    res["g_final"] = [t.reshape(g_final.shape) for t in res["g_final"]]

    reduce_done(in_keys, in_red, loss_sum)
    loss = loss_sum[0, 0]
    order =["g_mix", "w_in", "b_gate", "w_gk_up", "b_gk", "w_pool_grp", "pool_scale", "g_gla_head", "w_pool_proj",
             "w_gla_proj", "w_out", "g_ffn", "w_up", "w_conv", "b_conv", "w_down", "g_final"]
    return (loss, grad_x[None], *[res[k][0] for k in order], *[res[k][1] for k in order],
            *[res[k][2] for k in order], *[res[k][3] for k in order])
```

```python
import jax
import jax.numpy as jnp
from jax import lax
from jax.experimental import pallas as pl
from jax.experimental.pallas import tpu as pltpu

F32 = jnp.float32
BF = jnp.bfloat16
HIGHEST = lax.Precision.HIGHEST
MESH = pl.DeviceIdType.MESH

N_DEV = 8
SEQ = 2048
D_MODEL = 1024
CHUNK = 64
EPS = 1e-6
POOL_WIDTH = 512
POOL_WINDOWS = (2, 4, 8, 16)
POOL_GD = 128
POOL_HALO = 16
HEADS = 4
HK = 128
HV = 256
GLA_DK = 512
GATE_RANK = 16
GATE_NORM = 16.0
D_FF = 2816
FF_BLK = 704
IN_SHARD = 706
C_QKV, C_GATE, C_OG, C_POOL, C_GK = 0, 2048, 4096, 5120, 5632
N_CAT = 5632
GK_PAD = 128
N_DZ = N_CAT + GK_PAD
R_POOL, R_QKV, R_OG, R_GK, R_GATE = 0, 512, 2560, 3584, 3600

ADAM_LR, ADAM_B1, ADAM_B2, ADAM_EPS, ADAM_WD, ADAM_STEP = 0.001, 0.9, 0.999, 1e-08, 0.01, 10
ADAM_C1 = 1.0 - ADAM_B1 ** ADAM_STEP
ADAM_C2 = 1.0 - ADAM_B2 ** ADAM_STEP

VMEM_BYTES_V7X = 64 * 1024 * 1024
VMEM_LIMIT = VMEM_BYTES_V7X * 3 // 4

TOK_TILE = 256
HALO = 8
GLA_CPS = 4


def _params(*sem):
    return pltpu.CompilerParams(dimension_semantics=sem, vmem_limit_bytes=VMEM_LIMIT)


def _const_spec(shape):
    nd = len(shape)
    return pl.BlockSpec(shape, lambda *_: (0,) * nd)


def _in_hbm(t):
    return pltpu.with_memory_space_constraint(t, pltpu.HBM)


def _out_hbm(shape, dtype):
    return pltpu.HBM(shape, dtype)


def _dot(a, b, ta=False, tb=False):
    dims = (((0 if ta else 1,), (1 if tb else 0,)), ((), ()))
    return lax.dot_general(a.astype(BF), b.astype(BF), dims, preferred_element_type=F32)


def _dot_exact(a, b):
    return jnp.dot(a, b, precision=HIGHEST, preferred_element_type=F32)


def _sigmoid(x):
    return 0.5 * jnp.tanh(0.5 * x) + 0.5


def _mm(a, b, *, out_shape, out_dtype, grid, blk_a, blk_b, blk_o, map_a, map_b, map_o, ta=False, tb=False,
        after=None, name):
    gk = grid[2]
    n_in = 2 + (after is not None)

    def body(*refs):
        a_ref, b_ref, o_ref = refs[0], refs[1], refs[n_in]
        prod = _dot(a_ref[...], b_ref[...], ta, tb)
        if gk == 1:
            o_ref[...] = prod.astype(out_dtype)
        else:
            acc = refs[n_in + 1]
            k = pl.program_id(2)

            @pl.when(k == 0)
            def _():
                acc[...] = prod

            @pl.when(k > 0)
            def _():
                acc[...] += prod

            @pl.when(k == gk - 1)
            def _():
                o_ref[...] = acc[...].astype(out_dtype)

    in_specs = [pl.BlockSpec(blk_a, map_a), pl.BlockSpec(blk_b, map_b)]
    args = [_in_hbm(a), _in_hbm(b)]
    if after is not None:
        in_specs.append(pl.BlockSpec(memory_space=pl.ANY))
        args.append(after)
    return pl.pallas_call(
        body, name=name, grid=grid, in_specs=in_specs, out_specs=pl.BlockSpec(blk_o, map_o),
        out_shape=_out_hbm(out_shape, out_dtype),
        scratch_shapes=[] if gk == 1 else [pltpu.VMEM(tuple(d for d in blk_o if d is not None), F32)],
        compiler_params=_params("parallel", "parallel", "arbitrary"),
    )(*args)


TOK_MM_TILE = 256


def _mm_tokens(a, w, *, blk_a, map_a, pieces, res=None, after=None, then=None, name):
    n_in = 2 + (res is not None) + (after is not None) + (0 if then is None else len(then) - 1)

    def accumulate(ref, part):
        @pl.when(pl.program_id(0) == 0)
        def _():
            ref[...] = part

        @pl.when(pl.program_id(0) > 0)
        def _():
            ref[...] += part

    def body(*refs):
        a_ref, w_ref = refs[:2]
        extra, outs = refs[n_in - (0 if then is None else len(then) - 1):n_in], refs[n_in:]
        total = None
        for idx, row, n in pieces:
            av = a_ref[...] if idx is None else a_ref[idx]
            prod = _dot(av, w_ref[row:row + n, :])
            total = prod if total is None else total + prod
        if res is not None:
            total = total + refs[2][...]
        if then is None:
            outs[0][...] = total
        elif then[0] == "rms_bwd":
            dx, part = _rms_bwd_tile(total, extra[0][...], extra[1][...], extra[2][...])
            outs[0][...] = dx
            accumulate(outs[1], part)
        else:
            lpart, dx, part = _loss_tile(total, extra[0][...], extra[1][...])
            outs[1][...] = dx
            outs[2][...] = dx.astype(BF)
            accumulate(outs[0], lpart)
            accumulate(outs[3], part)

    tile = pl.BlockSpec((TOK_MM_TILE, D_MODEL), lambda i: (i, 0))
    vec = _const_spec((1, D_MODEL))
    big = _out_hbm((SEQ, D_MODEL), F32)
    small = _out_hbm((1, D_MODEL), F32)
    in_specs = [pl.BlockSpec(blk_a, map_a), pl.BlockSpec(w.shape, lambda i: (0, 0), pipeline_mode=pl.Buffered(1))]
    args = [a, w]
    if res is not None:
        in_specs.append(tile)
        args.append(res)
    if after is not None:
        in_specs.append(pl.BlockSpec(memory_space=pl.ANY))
        args.append(after)
    if then is None:
        out_specs, out_shape = tile, big
    elif then[0] == "rms_bwd":
        in_specs += [tile, vec, tile]
        out_specs, out_shape = [tile, vec], [big, small]
    else:
        in_specs += [vec, tile]
        out_specs = [_const_spec((1, 128)), tile, tile, vec]
        out_shape = [_out_hbm((1, 128), F32), big, _out_hbm((SEQ, D_MODEL), BF), small]
    if then is not None:
        args += list(then[1:])
    return pl.pallas_call(
        body, name=name, grid=(SEQ // TOK_MM_TILE,), in_specs=in_specs, out_specs=out_specs, out_shape=out_shape,
        compiler_params=_params("parallel" if then is None else "arbitrary"),
    )(*[_in_hbm(t) for t in args])


def _rms_fwd(x, g, after, name):
    def body(x_ref, g_ref, after_ref, o_ref):
        del after_ref
        xv = x_ref[...]
        r = lax.rsqrt(jnp.mean(xv * xv, axis=-1, keepdims=True) + EPS)
        o_ref[...] = (xv * r * g_ref[...]).astype(BF)

    tile = pl.BlockSpec((TOK_TILE, D_MODEL), lambda i: (i, 0))
    return pl.pallas_call(
        body, name=name, grid=(SEQ // TOK_TILE,),
        in_specs=[tile, _const_spec((1, D_MODEL)), pl.BlockSpec(memory_space=pl.ANY)], out_specs=tile,
        out_shape=_out_hbm((SEQ, D_MODEL), BF), compiler_params=_params("parallel"),
    )(*map(_in_hbm, (x, g)), after)


def _rms_bwd_tile(dyv, xv, gv, dresv):
    r = lax.rsqrt(jnp.mean(xv * xv, axis=-1, keepdims=True) + EPS)
    xn = xv * r
    dxn = dyv * gv
    return dresv + r * (dxn - xn * jnp.mean(dxn * xn, axis=-1, keepdims=True)), jnp.sum(dyv * xn, axis=0, keepdims=True)


def _loss_tile(xv, gv, tv):
    r = lax.rsqrt(jnp.mean(xv * xv, axis=-1, keepdims=True) + EPS)
    xn = xv * r
    err = xn * gv - tv
    lpart = jnp.full((1, 128), 0.5 * jnp.sum(jnp.mean(err * err, axis=-1, keepdims=True)), F32)
    dyv = err * (1.0 / D_MODEL)
    dxn = dyv * gv
    return lpart, r * (dxn - xn * jnp.mean(dxn * xn, axis=-1, keepdims=True)), jnp.sum(dyv * xn, axis=0, keepdims=True)


def _pool_counts(w):
    pos = lax.broadcasted_iota(jnp.int32, (SEQ, 1), 0).astype(F32)
    return jnp.minimum(pos + 1.0, float(w))


def _pool_window(u, w, ext):
    ext[pl.ds(POOL_HALO, SEQ), :] = u
    win = u
    for j in range(1, w):
        win = win + ext[pl.ds(POOL_HALO - j, SEQ), :]
    return win / _pool_counts(w) - u


def _pool_fwd(zcat, w_grp, scale):
    def body(z_ref, w_ref, s_ref, o_ref, ext):
        ext[pl.ds(0, POOL_HALO), :] = jnp.zeros((POOL_HALO, POOL_GD), F32)
        for g, w in enumerate(POOL_WINDOWS):
            cols = slice(g * POOL_GD, (g + 1) * POOL_GD)
            p = _pool_window(z_ref[:, cols].astype(F32), w, ext)
            o_ref[:, cols] = (_dot(p, w_ref[g]) * s_ref[:, cols]).astype(BF)

    return pl.pallas_call(
        body, name="pool_fwd", grid=(1,),
        in_specs=[pl.BlockSpec((SEQ, POOL_WIDTH), lambda i: (0, C_POOL // POOL_WIDTH)),
                  _const_spec((4, POOL_GD, POOL_GD)), _const_spec((1, POOL_WIDTH))],
        out_specs=_const_spec((SEQ, POOL_WIDTH)), out_shape=_out_hbm((SEQ, POOL_WIDTH), BF),
        scratch_shapes=[pltpu.VMEM((POOL_HALO + SEQ, POOL_GD), F32)], compiler_params=_params("arbitrary"),
    )(*map(_in_hbm, (zcat, w_grp, scale)))


def _pool_bwd(dzcat, zcat, dps, w_grp, scale):
    def body(dz_in, z_ref, dps_ref, w_ref, s_ref, dz_ref, dw_ref, dsc_ref, ext, ext2):
        del dz_in
        ext[pl.ds(0, POOL_HALO), :] = jnp.zeros((POOL_HALO, POOL_GD), F32)
        ext2[pl.ds(SEQ, POOL_HALO), :] = jnp.zeros((POOL_HALO, POOL_GD), F32)
        for g, w in enumerate(POOL_WINDOWS):
            cols = slice(g * POOL_GD, (g + 1) * POOL_GD)
            p = _pool_window(z_ref[:, cols].astype(F32), w, ext)
            wg = w_ref[g]
            pg = _dot(p, wg)
            dpsv = dps_ref[:, cols]
            dsc_ref[:, cols] = jnp.sum(dpsv * pg, axis=0, keepdims=True)
            dpg = dpsv * s_ref[:, cols]
            dw_ref[g] = _dot(p, dpg, ta=True)
            dp = _dot(dpg, wg, tb=True)
            dpc = dp / _pool_counts(w)
            ext2[pl.ds(0, SEQ), :] = dpc
            du = dpc
            for j in range(1, w):
                du = du + ext2[pl.ds(j, SEQ), :]
            dz_ref[:, cols] = (du - dp).astype(BF)

    return pl.pallas_call(
        body, name="pool_bwd", grid=(1,),
        in_specs=[pl.BlockSpec(memory_space=pl.ANY),
                  pl.BlockSpec((SEQ, POOL_WIDTH), lambda i: (0, C_POOL // POOL_WIDTH)),
                  _const_spec((SEQ, POOL_WIDTH)), _const_spec((4, POOL_GD, POOL_GD)), _const_spec((1, POOL_WIDTH))],
        out_specs=[pl.BlockSpec((SEQ, POOL_WIDTH), lambda i: (0, C_POOL // POOL_WIDTH)),
                   _const_spec((4, POOL_GD, POOL_GD)), _const_spec((1, POOL_WIDTH))],
        out_shape=[_out_hbm((SEQ, N_DZ), BF), _out_hbm((4, POOL_GD, POOL_GD), F32),
                   _out_hbm((1, POOL_WIDTH), F32)],
        scratch_shapes=[pltpu.VMEM((POOL_HALO + SEQ, POOL_GD), F32), pltpu.VMEM((SEQ + POOL_HALO, POOL_GD), F32)],
        input_output_aliases={0: 0}, compiler_params=_params("arbitrary"),
    )(*map(_in_hbm, (dzcat, zcat, dps, w_grp, scale)))


GK_TILE = 512


GK_ROWS = pl.BlockSpec((GK_PAD, D_MODEL), lambda i: (C_GK // GK_PAD, 0))


def _gk_fwd(h, wt_cat, wgk_pad, b_gk):
    def body(h_ref, wt_ref, w_ref, b_ref, la_ref):
        z_gk = _dot(h_ref[...], wt_ref[...], tb=True)
        pre = _dot(z_gk, w_ref[...]) + b_ref[...]
        la_ref[...] = (jnp.minimum(pre, 0.0) - jnp.log(1.0 + jnp.exp(-jnp.abs(pre)))) * (1.0 / GATE_NORM)

    return pl.pallas_call(
        body, name="gk_fwd", grid=(SEQ // GK_TILE,),
        in_specs=[pl.BlockSpec((GK_TILE, D_MODEL), lambda i: (i, 0)), GK_ROWS,
                  _const_spec((GK_PAD, GLA_DK)), _const_spec((1, GLA_DK))],
        out_specs=pl.BlockSpec((GK_TILE, GLA_DK), lambda i: (i, 0)),
        out_shape=_out_hbm((SEQ, GLA_DK), F32), compiler_params=_params("parallel"),
    )(*map(_in_hbm, (h, wt_cat, wgk_pad, b_gk)))


def _gk_bwd(dzcat, dla, h, wt_cat, wgk_pad, b_gk):
    def body(dz_in, dla_ref, h_ref, wt_ref, w_ref, b_ref, dz_ref, dw_ref, db_ref):
        del dz_in
        wv = w_ref[...]
        z_gk = _dot(h_ref[...], wt_ref[...], tb=True)
        pre = _dot(z_gk, wv) + b_ref[...]
        dpre = dla_ref[...] * (1.0 / GATE_NORM) * (1.0 - _sigmoid(pre))
        dz_ref[...] = _dot(dpre, wv, tb=True).astype(BF)
        dwp = _dot(z_gk, dpre, ta=True)[:GATE_RANK]
        dbp = jnp.sum(dpre, axis=0, keepdims=True)

        @pl.when(pl.program_id(0) == 0)
        def _():
            dw_ref[...] = dwp
            db_ref[...] = dbp

        @pl.when(pl.program_id(0) > 0)
        def _():
            dw_ref[...] += dwp
            db_ref[...] += dbp

    return pl.pallas_call(
        body, name="gk_bwd", grid=(SEQ // GK_TILE,),
        in_specs=[pl.BlockSpec(memory_space=pl.ANY), pl.BlockSpec((GK_TILE, GLA_DK), lambda i: (i, 0)),
                  pl.BlockSpec((GK_TILE, D_MODEL), lambda i: (i, 0)), GK_ROWS, _const_spec((GK_PAD, GLA_DK)),
                  _const_spec((1, GLA_DK))],
        out_specs=[pl.BlockSpec((GK_TILE, GK_PAD), lambda i: (i, C_GK // GK_PAD)), _const_spec((GATE_RANK, GLA_DK)),
                   _const_spec((1, GLA_DK))],
        out_shape=[_out_hbm((SEQ, N_DZ), BF), _out_hbm((GATE_RANK, GLA_DK), F32),
                   _out_hbm((1, GLA_DK), F32)],
        input_output_aliases={0: 0}, compiler_params=_params("arbitrary"),
    )(*map(_in_hbm, (dzcat, dla, h, wt_cat, wgk_pad, b_gk)))


GLA_ROWS = GLA_CPS * CHUNK
GLA_STEPS = SEQ // GLA_ROWS
QKV_W = 2048


def _tri():
    return lax.broadcasted_iota(jnp.int32, (CHUNK, CHUNK), 0) >= lax.broadcasted_iota(jnp.int32, (CHUNK, CHUNK), 1)


def _chunk_cumsum(la_ref, rows):
    return _dot_exact(_tri().astype(F32), la_ref[rows, :])


def _gla_chunk(qkv_ref, la_ref, rows, h, bc_all):
    tri = _tri()
    q = qkv_ref[rows, h * HK:(h + 1) * HK].astype(F32) * (HK ** -0.5)
    k = qkv_ref[rows, GLA_DK + h * HK:GLA_DK + (h + 1) * HK].astype(F32)
    v = qkv_ref[rows, 2 * GLA_DK + h * HV:2 * GLA_DK + (h + 1) * HV].astype(BF)
    la = la_ref[rows, h * HK:(h + 1) * HK]
    bc = bc_all[:, h * HK:(h + 1) * HK]
    e_pos, e_neg = jnp.exp(bc), jnp.exp(-bc)
    dl = jnp.exp(jnp.sum(la, axis=0, keepdims=True))
    q_fw, q_bw, k_fw, k_bw = q * e_pos, q * e_neg, k * e_neg, k * e_pos
    scores = jnp.where(tri, _dot(q_fw, k_fw, tb=True), _dot(q_bw, k_bw, tb=True))
    return tri, v, e_pos, e_neg, dl, q_fw, q_bw, k_fw, k_bw, scores


def _gla_fwd(zcat, la, after):
    def body(qkv_ref, la_ref, after_ref, o_ref, st_ref, state):
        del after_ref

        @pl.when(pl.program_id(0) == 0)
        def _():
            state[...] = jnp.zeros_like(state)

        for c in range(GLA_CPS):
            rows = slice(c * CHUNK, (c + 1) * CHUNK)
            bc_all = _chunk_cumsum(la_ref, rows)
            for h in range(HEADS):
                _, v, _, _, dl, q_fw, _, k_fw, _, scores = _gla_chunk(qkv_ref, la_ref, rows, h, bc_all)
                st = state[h]
                st_ref[c, h] = st
                o_ref[rows, h * HV:(h + 1) * HV] = _dot(scores, v) + _dot(q_fw, st, tb=True)
                state[h] = st * dl + _dot(v, k_fw * dl, ta=True)

    return pl.pallas_call(
        body, name="gla_fwd", grid=(GLA_STEPS,),
        in_specs=[pl.BlockSpec((GLA_ROWS, QKV_W), lambda i: (i, 0)), pl.BlockSpec((GLA_ROWS, GLA_DK), lambda i: (i, 0)),
                  pl.BlockSpec(memory_space=pl.ANY)],
        out_specs=[pl.BlockSpec((GLA_ROWS, D_MODEL), lambda i: (i, 0)),
                   pl.BlockSpec((GLA_CPS, HEADS, HV, HK), lambda i: (i, 0, 0, 0))],
        out_shape=[_out_hbm((SEQ, D_MODEL), F32),
                   _out_hbm((SEQ // CHUNK, HEADS, HV, HK), F32)],
        scratch_shapes=[pltpu.VMEM((HEADS, HV, HK), F32)], compiler_params=_params("arbitrary"),
    )(*map(_in_hbm, (zcat, la)), after)


def _gla_bwd(dzcat, zcat, la, d_o, states):
    def body(dz_in, qkv_ref, la_ref, do_ref, st_ref, dqkv_ref, dla_ref, dstate):
        del dz_in

        @pl.when(pl.program_id(0) == 0)
        def _():
            dstate[...] = jnp.zeros_like(dstate)

        last_row = lax.broadcasted_iota(jnp.int32, (CHUNK, HK), 0) == CHUNK - 1
        upper = (lax.broadcasted_iota(jnp.int32, (CHUNK, CHUNK), 0)
                 <= lax.broadcasted_iota(jnp.int32, (CHUNK, CHUNK), 1)).astype(F32)
        for c in reversed(range(GLA_CPS)):
            rows = slice(c * CHUNK, (c + 1) * CHUNK)
            bc_all = _chunk_cumsum(la_ref, rows)
            dbs = []
            for h in range(HEADS):
                tri, v, e_pos, e_neg, dl, q_fw, q_bw, k_fw, k_bw, scores = _gla_chunk(qkv_ref, la_ref, rows, h, bc_all)
                st = st_ref[c, h]
                dst = dstate[h]
                d_out = do_ref[rows, h * HV:(h + 1) * HV].astype(BF)
                k_dec = k_fw * dl
                dp = _dot(d_out, v, tb=True)
                dp_fw = jnp.where(tri, dp, 0.0)
                dp_bw = jnp.where(tri, 0.0, dp)
                dv = _dot(scores, d_out, ta=True) + _dot(k_dec, dst, tb=True)
                dk_dec = _dot(v, dst)
                dq_fw = _dot(dp_fw, k_fw) + _dot(d_out, st)
                dk_fw = _dot(dp_fw, q_fw, ta=True) + dk_dec * dl
                dq_bw = _dot(dp_bw, k_bw)
                dk_bw = _dot(dp_bw, q_bw, ta=True)
                ddl = jnp.sum(st * dst, axis=0, keepdims=True) + jnp.sum(k_fw * dk_dec, axis=0, keepdims=True)
                dstate[h] = dst * dl + _dot(d_out, q_fw, ta=True)
                dq = (dq_fw * e_pos + dq_bw * e_neg) * (HK ** -0.5)
                dk = dk_fw * e_neg + dk_bw * e_pos
                dbs.append(dq_fw * q_fw - dk_fw * k_fw - dq_bw * q_bw + dk_bw * k_bw + jnp.where(last_row, ddl * dl, 0.0))
                dqkv_ref[rows, h * HK:(h + 1) * HK] = dq.astype(BF)
                dqkv_ref[rows, GLA_DK + h * HK:GLA_DK + (h + 1) * HK] = dk.astype(BF)
                dqkv_ref[rows, 2 * GLA_DK + h * HV:2 * GLA_DK + (h + 1) * HV] = dv.astype(BF)
            dla_ref[rows, :] = _dot_exact(upper, jnp.concatenate(dbs, axis=1))

    rev = lambda i: (GLA_STEPS - 1 - i, 0)
    return pl.pallas_call(
        body, name="gla_bwd", grid=(GLA_STEPS,),
        in_specs=[pl.BlockSpec(memory_space=pl.ANY), pl.BlockSpec((GLA_ROWS, QKV_W), rev),
                  pl.BlockSpec((GLA_ROWS, GLA_DK), rev), pl.BlockSpec((GLA_ROWS, D_MODEL), rev),
                  pl.BlockSpec((GLA_CPS, HEADS, HV, HK), lambda i: (GLA_STEPS - 1 - i, 0, 0, 0))],
        out_specs=[pl.BlockSpec((GLA_ROWS, QKV_W), rev), pl.BlockSpec((GLA_ROWS, GLA_DK), rev)],
        out_shape=[_out_hbm((SEQ, N_DZ), BF), _out_hbm((SEQ, GLA_DK), F32)],
        scratch_shapes=[pltpu.VMEM((HEADS, HV, HK), F32)], input_output_aliases={0: 0},
        compiler_params=_params("arbitrary"),
    )(*map(_in_hbm, (dzcat, zcat, la, d_o, states)))


def _silu_parts(x):
    s = _sigmoid(x)
    return x * s, s * (1.0 + x * (1.0 - s))


def _post_gla_fwd(o, zcat, g_head):
    def body(o_ref, zog_ref, g_ref, out_ref):
        for h in range(HEADS):
            cols = slice(h * HV, (h + 1) * HV)
            ov = o_ref[:, cols]
            r = lax.rsqrt(jnp.mean(ov * ov, axis=-1, keepdims=True) + EPS)
            act, _ = _silu_parts(zog_ref[:, cols].astype(F32))
            out_ref[:, cols] = (ov * r * g_ref[...] * act).astype(BF)

    tile = pl.BlockSpec((TOK_TILE, D_MODEL), lambda i: (i, 0))
    return pl.pallas_call(
        body, name="post_gla_fwd", grid=(SEQ // TOK_TILE,),
        in_specs=[tile, pl.BlockSpec((TOK_TILE, D_MODEL), lambda i: (i, C_OG // D_MODEL)), _const_spec((1, HV))],
        out_specs=tile, out_shape=_out_hbm((SEQ, D_MODEL), BF), compiler_params=_params("parallel"),
    )(*map(_in_hbm, (o, zcat, g_head)))


def _post_gla_bwd(dzcat, dy_gla, w_gla_proj, o, zcat, g_head, after):
    def body(dz_in, dyg_ref, w_ref, o_ref, zog_ref, g_ref, after_ref, dz_ref, do_ref, dg_ref):
        del dz_in, after_ref
        dog = _dot(dyg_ref[...], w_ref[...], tb=True)
        gpart = jnp.zeros((1, HV), F32)
        gv = g_ref[...]
        for h in range(HEADS):
            cols = slice(h * HV, (h + 1) * HV)
            ov = o_ref[:, cols]
            r = lax.rsqrt(jnp.mean(ov * ov, axis=-1, keepdims=True) + EPS)
            on = ov * r
            act, dact = _silu_parts(zog_ref[:, cols].astype(F32))
            dogv = dog[:, cols]
            dz_ref[:, cols] = (dogv * on * gv * dact).astype(BF)
            d_on_g = dogv * act
            gpart = gpart + jnp.sum(d_on_g * on, axis=0, keepdims=True)
            dxn = d_on_g * gv
            do_ref[:, cols] = (r * (dxn - on * jnp.mean(dxn * on, axis=-1, keepdims=True))).astype(BF)

        @pl.when(pl.program_id(0) == 0)
        def _():
            dg_ref[...] = gpart

        @pl.when(pl.program_id(0) > 0)
        def _():
            dg_ref[...] += gpart

    tile = pl.BlockSpec((TOK_TILE, D_MODEL), lambda i: (i, 0))
    ogspec = pl.BlockSpec((TOK_TILE, D_MODEL), lambda i: (i, C_OG // D_MODEL))
    return pl.pallas_call(
        body, name="post_gla_bwd", grid=(SEQ // TOK_TILE,),
        in_specs=[pl.BlockSpec(memory_space=pl.ANY), tile, _const_spec((D_MODEL, D_MODEL)), tile, ogspec,
                  _const_spec((1, HV)), pl.BlockSpec(memory_space=pl.ANY)],
        out_specs=[ogspec, tile, _const_spec((1, HV))],
        out_shape=[_out_hbm((SEQ, N_DZ), BF), _out_hbm((SEQ, D_MODEL), BF),
                   _out_hbm((1, HV), F32)],
        input_output_aliases={0: 0}, compiler_params=_params("arbitrary"),
    )(*map(_in_hbm, (dzcat, dy_gla, w_gla_proj, o, zcat, g_head)), after)


GATE_W = 2 * D_MODEL


def _mix_out_fwd(ps, og, zcat, x, w_pool_proj, w_gla_proj, w_out, b_gate, g_ffn, after):
    def body(ps_ref, og_ref, zg_ref, x_ref, wpp_ref, wgp_ref, wout_ref, b_ref, g_ref, after_ref,
             yp_ref, yg_ref, mixed_ref, x1_ref, h2_ref):
        del after_ref
        y_pool = _dot(ps_ref[...], wpp_ref[...])
        y_gla = _dot(og_ref[...], wgp_ref[...])
        yp_ref[...] = y_pool.astype(BF)
        yg_ref[...] = y_gla.astype(BF)
        g0 = _sigmoid(zg_ref[:, :D_MODEL].astype(F32) + b_ref[:, :D_MODEL])
        g1 = _sigmoid(zg_ref[:, D_MODEL:].astype(F32) + b_ref[:, D_MODEL:])
        mixed = (g0 * y_pool + g1 * y_gla).astype(BF)
        mixed_ref[...] = mixed
        x1 = x_ref[...] + _dot(mixed, wout_ref[...])
        x1_ref[...] = x1
        r = lax.rsqrt(jnp.mean(x1 * x1, axis=-1, keepdims=True) + EPS)
        h2_ref[...] = (x1 * r * g_ref[...]).astype(BF)

    tile = pl.BlockSpec((TOK_TILE, D_MODEL), lambda i: (i, 0))
    resident = lambda shape: pl.BlockSpec(shape, lambda i: (0, 0), pipeline_mode=pl.Buffered(1))
    f32, bf16 = _out_hbm((SEQ, D_MODEL), F32), _out_hbm((SEQ, D_MODEL), BF)
    return pl.pallas_call(
        body, name="mix_out_fwd", grid=(SEQ // TOK_TILE,),
        in_specs=[pl.BlockSpec((TOK_TILE, POOL_WIDTH), lambda i: (i, 0)), tile,
                  pl.BlockSpec((TOK_TILE, GATE_W), lambda i: (i, C_GATE // GATE_W)), tile,
                  resident((POOL_WIDTH, D_MODEL)), resident((D_MODEL, D_MODEL)), resident((D_MODEL, D_MODEL)),
                  _const_spec((1, GATE_W)), _const_spec((1, D_MODEL)), pl.BlockSpec(memory_space=pl.ANY)],
        out_specs=[tile] * 5, out_shape=[bf16, bf16, bf16, f32, bf16], compiler_params=_params("parallel"),
    )(*map(_in_hbm, (ps, og, zcat, x, w_pool_proj, w_gla_proj, w_out, b_gate, g_ffn)), after)


def _mix_bwd(dx1, w_out, zcat, b_gate, y_pool, y_gla):
    def body(dx_ref, w_ref, zg_ref, b_ref, yp_ref, yg_ref, dz_ref, dyp_ref, dyg_ref, db_ref):
        dm = _dot(dx_ref[...], w_ref[...], tb=True)
        g0 = _sigmoid(zg_ref[:, :D_MODEL].astype(F32) + b_ref[:, :D_MODEL])
        g1 = _sigmoid(zg_ref[:, D_MODEL:].astype(F32) + b_ref[:, D_MODEL:])
        dyp_ref[...] = (dm * g0).astype(BF)
        dyg_ref[...] = (dm * g1).astype(BF)
        dz0 = dm * yp_ref[...].astype(F32) * g0 * (1.0 - g0)
        dz1 = dm * yg_ref[...].astype(F32) * g1 * (1.0 - g1)
        dz_ref[:, :D_MODEL] = dz0.astype(BF)
        dz_ref[:, D_MODEL:] = dz1.astype(BF)
        b0 = jnp.sum(dz0, axis=0, keepdims=True)
        b1 = jnp.sum(dz1, axis=0, keepdims=True)

        @pl.when(pl.program_id(0) == 0)
        def _():
            db_ref[:, :D_MODEL] = b0
            db_ref[:, D_MODEL:] = b1

        @pl.when(pl.program_id(0) > 0)
        def _():
            db_ref[:, :D_MODEL] += b0
            db_ref[:, D_MODEL:] += b1

    tile = pl.BlockSpec((TOK_TILE, D_MODEL), lambda i: (i, 0))
    gspec = pl.BlockSpec((TOK_TILE, GATE_W), lambda i: (i, C_GATE // GATE_W))
    return pl.pallas_call(
        body, name="mix_bwd", grid=(SEQ // TOK_TILE,),
        in_specs=[tile, _const_spec((D_MODEL, D_MODEL)), gspec, _const_spec((1, GATE_W)), tile, tile],
        out_specs=[gspec, tile, tile, _const_spec((1, GATE_W))],
        out_shape=[_out_hbm((SEQ, N_DZ), BF), _out_hbm((SEQ, D_MODEL), BF),
                   _out_hbm((SEQ, D_MODEL), BF), _out_hbm((1, GATE_W), F32)],
        compiler_params=_params("arbitrary"),
    )(*map(_in_hbm, (dx1, w_out, zcat, b_gate, y_pool, y_gla)))


N_TOK_TILES = SEQ // TOK_TILE
HALO_PER_TILE = TOK_TILE // HALO


LANE_TILES = tuple((lo, min(128, FF_BLK - lo)) for lo in range(0, FF_BLK, 128))


def _taps(w_ref, b_ref, half, lanes, rows):
    shape = (rows, lanes.stop - lanes.start)
    return ([jnp.broadcast_to(w_ref[half, j:j + 1, lanes], shape) for j in range(3)],
            jnp.broadcast_to(b_ref[half, :, lanes], shape))


def _conv_strips(u_ref, ub_ref, ua_ref, taps, lanes, width, n_strips, first):
    row = lax.broadcasted_iota(jnp.int32, (HALO, width), 0)
    prev = [[pltpu.roll(jnp.where(first, 0.0, ub_ref[half, :, lanes]), k, 0) for k in (1, 2)] for half in range(2)]
    for s in range(n_strips + (ua_ref is not None)):
        u3, conv = [], []
        for half in range(2):
            cur = u_ref[half, s * HALO:(s + 1) * HALO, lanes] if s < n_strips else ua_ref[half, :, lanes]
            rolled = [pltpu.roll(cur, k, 0) for k in (1, 2)]
            frames = [jnp.where(row >= 2, rolled[1], prev[half][1]), jnp.where(row >= 1, rolled[0], prev[half][0]), cur]
            prev[half] = rolled
            w3, bias = taps[half]
            u3.append(frames)
            conv.append(bias + frames[0] * w3[0] + frames[1] * w3[1] + frames[2] * w3[2])
        yield s, u3, conv


def _pair_specs(pairs):
    tile = pl.BlockSpec((pairs, None, TOK_TILE, FF_BLK), lambda b, i: (0, b, i, 0))
    before = pl.BlockSpec((pairs, None, HALO, FF_BLK), lambda b, i: (0, b, jnp.maximum(i * HALO_PER_TILE - 1, 0), 0))
    after = pl.BlockSpec((pairs, None, HALO, FF_BLK),
                         lambda b, i: (0, b, jnp.minimum((i + 1) * HALO_PER_TILE, SEQ // HALO - 1), 0))

    def vec(rows):
        return pl.BlockSpec((2, None, rows, FF_BLK), lambda b, i: (0, b, 0, 0))

    return tile, before, after, vec


N_STRIPS = TOK_TILE // HALO


def _up_conv_fwd(h2, wt_up, w_conv, b_conv):
    steps = N_TOK_TILES // 2

    def body(h_ref, h_next, wg_ref, wv_ref, w_ref, b_ref, u_ref, a_ref, buf_a, buf_b, carry):
        j = pl.program_id(1)

        def project(hv, buf):
            buf[0] = _dot(hv, wg_ref[...], tb=True)
            buf[1] = _dot(hv, wv_ref[...], tb=True)

        def conv(buf, row0):
            u_ref[:, row0:row0 + TOK_TILE, :] = buf[...]
            for lo, width in LANE_TILES:
                lanes = slice(lo, lo + width)
                taps = [_taps(w_ref, b_ref, half, lanes, HALO) for half in range(2)]
                pending = None
                for s, _, (cg, cv) in _conv_strips(buf, carry, None, taps, lanes, width, N_STRIPS, False):
                    act = cg * _sigmoid(cg) * cv
                    if s % 2 == 0:
                        pending = act
                    else:
                        a_ref[0, row0 + (s - 1) * HALO:row0 + (s + 1) * HALO, lanes] = (
                            jnp.concatenate([pending, act], axis=0).astype(BF))
            carry[...] = buf[:, TOK_TILE - HALO:, :]

        @pl.when(j == 0)
        def _():
            project(h_ref[0:TOK_TILE, :], buf_a)
            carry[...] = jnp.zeros_like(carry)

        project(h_ref[TOK_TILE:, :], buf_b)
        conv(buf_a, 0)
        project(h_next[...], buf_a)
        conv(buf_b, TOK_TILE)

    w_blk = lambda half: pl.BlockSpec((FF_BLK, D_MODEL), lambda b, j: (b + 4 * half, 0))
    vec = lambda rows: pl.BlockSpec((2, None, rows, FF_BLK), lambda b, j: (0, b, 0, 0))
    u_buf = pltpu.VMEM((2, TOK_TILE, FF_BLK), F32)
    return pl.pallas_call(
        body, name="up_conv_fwd", grid=(4, steps),
        in_specs=[pl.BlockSpec((2 * TOK_TILE, D_MODEL), lambda b, j: (j, 0)),
                  pl.BlockSpec((TOK_TILE, D_MODEL), lambda b, j: (jnp.minimum(2 * j + 2, N_TOK_TILES - 1), 0)),
                  w_blk(0), w_blk(1), vec(3), vec(1)],
        out_specs=[pl.BlockSpec((2, None, 2 * TOK_TILE, FF_BLK), lambda b, j: (0, b, j, 0)),
                   pl.BlockSpec((1, None, 2 * TOK_TILE, FF_BLK), lambda b, j: (0, b, j, 0))],
        out_shape=[_out_hbm((2, 4, SEQ, FF_BLK), F32), _out_hbm((1, 4, SEQ, FF_BLK), BF)],
        scratch_shapes=[u_buf, u_buf, pltpu.VMEM((2, HALO, FF_BLK), F32)],
        compiler_params=_params("parallel", "arbitrary"),
    )(*map(_in_hbm, (h2, h2, wt_up, wt_up, w_conv, b_conv)))


def _conv_bwd(u, da, w_conv, b_conv):
    def body(u_ref, ub_ref, ua_ref, da_ref, daa_ref, w_ref, b_ref, du_ref, dw_ref, db_ref):
        i = pl.program_id(1)

        @pl.when(i == 0)
        def _():
            dw_ref[...] = jnp.zeros_like(dw_ref)
            db_ref[...] = jnp.zeros_like(db_ref)

        for lo, width in LANE_TILES:
            lanes = slice(lo, lo + width)
            row = lax.broadcasted_iota(jnp.int32, (HALO, width), 0)
            taps = [_taps(w_ref, b_ref, half, lanes, HALO) for half in range(2)]
            acc_w = [[jnp.zeros((HALO, width), F32) for _ in range(3)] for _ in range(2)]
            acc_b = [jnp.zeros((HALO, width), F32) for _ in range(2)]
            da_pair, pending = None, [None, None]
            dc_prev, up_prev = [None, None], [None, None]
            for s, u3, (cg, cv) in _conv_strips(u_ref, ub_ref, ua_ref, taps, lanes, width, N_STRIPS, i == 0):
                act, dact = _silu_parts(cg)
                if s == N_STRIPS:
                    da = jnp.where(i < N_TOK_TILES - 1, daa_ref[0, :, lanes].astype(F32), 0.0)
                elif s % 2 == 0:
                    da_pair = da_ref[0, s * HALO:(s + 2) * HALO, lanes].astype(F32)
                    da = da_pair[:HALO]
                else:
                    da = da_pair[HALO:]
                dc = (da * cv * dact, da * act)
                for half in range(2):
                    up = [pltpu.roll(dc[half], HALO - k, 0) for k in (1, 2)]
                    if s < N_STRIPS:
                        for j in range(3):
                            acc_w[half][j] = acc_w[half][j] + dc[half] * u3[half][j]
                        acc_b[half] = acc_b[half] + dc[half]
                    if s >= 1:
                        w3 = taps[half][0]
                        du = (dc_prev[half] * w3[2] + jnp.where(row < HALO - 1, up_prev[half][0], up[0]) * w3[1]
                              + jnp.where(row < HALO - 2, up_prev[half][1], up[1]) * w3[0])
                        if (s - 1) % 2 == 0:
                            pending[half] = du
                        else:
                            du_ref[half, (s - 2) * HALO:s * HALO, lanes] = jnp.concatenate([pending[half], du],
                                                                                           axis=0).astype(BF)
                    dc_prev[half], up_prev[half] = dc[half], up
            for half in range(2):
                for j in range(3):
                    dw_ref[half, j:j + 1, lanes] += jnp.sum(acc_w[half][j], axis=0, keepdims=True)
                db_ref[half, :, lanes] += jnp.sum(acc_b[half], axis=0, keepdims=True)

    tile, before, after, vec = _pair_specs(2)
    da_tile, _, da_after_spec, _ = _pair_specs(1)
    return pl.pallas_call(
        body, name="conv_bwd", grid=(4, N_TOK_TILES),
        in_specs=[tile, before, after, da_tile, da_after_spec, vec(3), vec(1)],
        out_specs=[tile, vec(3), vec(1)],
        out_shape=[_out_hbm((2, 4, SEQ, FF_BLK), BF), _out_hbm((2, 4, 3, FF_BLK), F32),
                   _out_hbm((2, 4, 1, FF_BLK), F32)],
        compiler_params=_params("parallel", "arbitrary"),
    )(*map(_in_hbm, (u, u, u, da, da, w_conv, b_conv)))


W_IN_SEGMENTS = ((R_POOL, POOL_WIDTH, C_POOL), (R_QKV, QKV_W, C_QKV), (R_OG, D_MODEL, C_OG), (R_GK, GATE_RANK, C_GK),
                 (R_GATE, GATE_W, C_GATE))


def _slab_pieces(d):
    lo, hi = d * IN_SHARD, (d + 1) * IN_SHARD
    pieces = []
    for start, n, at in W_IN_SEGMENTS:
        a, b = max(lo, start), min(hi, start + n)
        if a < b:
            assert (a - lo) % 2 == 0 and (b - a) % 2 == 0 and (at + a - start) % 2 == 0
            pieces.append(((a - lo) // 2, (b - a) // 2, (at + a - start) // 2))
    return pieces


def _unshard_w_in(slabs):
    def body(slab_ref, cat_ref):
        d = pl.program_id(0)
        src = slab_ref.bitcast(jnp.uint32)
        dst = cat_ref.bitcast(jnp.uint32)

        @pl.when(d == 0)
        def _():
            cat_ref[C_GK:, :] = jnp.zeros((GK_PAD, D_MODEL), BF)

        for dd in range(N_DEV):
            @pl.when(d == dd)
            def _():
                for a, n, at in _slab_pieces(dd):
                    dst[pl.ds(at, n), :] = src[0, pl.ds(a, n), :]

    return pl.pallas_call(
        body, name="unshard_w_in", grid=(N_DEV,),
        in_specs=[pl.BlockSpec((1, IN_SHARD, D_MODEL), lambda d: (d, 0, 0))], out_specs=_const_spec((N_DZ, D_MODEL)),
        out_shape=_out_hbm((N_DZ, D_MODEL), BF), compiler_params=_params("arbitrary"),
    )(_in_hbm(slabs))


def _shard_d_w_in(d_cat):
    def body(cat_ref, slab_ref):
        d = pl.program_id(0)
        cat = cat_ref.bitcast(jnp.uint32)
        dst = slab_ref.bitcast(jnp.uint32)
        for dd in range(N_DEV):
            @pl.when(d == dd)
            def _():
                for a, n, at in _slab_pieces(dd):
                    dst[0, pl.ds(a, n), :] = cat[pl.ds(at, n), :]

    return pl.pallas_call(
        body, name="shard_d_w_in", grid=(N_DEV,), in_specs=[_const_spec((N_DZ, D_MODEL))],
        out_specs=pl.BlockSpec((1, IN_SHARD, D_MODEL), lambda d: (d, 0, 0)),
        out_shape=_out_hbm((N_DEV, IN_SHARD, D_MODEL), BF), compiler_params=_params("parallel"),
    )(_in_hbm(d_cat))


ANY = pl.BlockSpec(memory_space=pl.ANY)


def _place():
    x, y, c = lax.axis_index("x"), lax.axis_index("y"), lax.axis_index("c")
    other_chips = [(1 - x, y), (x, 1 - y), (1 - x, 1 - y)]
    return x, y, c, other_chips


SEM = pl.BlockSpec(memory_space=pltpu.SEMAPHORE)
IN_HBM = pl.BlockSpec(memory_space=pltpu.HBM)
SPLIT_PARAMS = pltpu.CompilerParams(has_side_effects=pltpu.SideEffectType.DATAFLOW_SIDE_EFFECTING)


def _gather_first(refs, send_sems, recv_sems):
    x, y, c, chips = _place()
    targets = [(x, y, 1 - c)] + [(px, py, c) for px, py in chips]
    copies = []
    for a, land in enumerate(refs):
        mine = land.at[4 * x + 2 * y + c]
        copies += [pltpu.make_async_remote_copy(src_ref=mine, dst_ref=mine, send_sem=send_sems.at[4 * a + k],
                                                recv_sem=recv_sems.at[4 * a + k], device_id=to, device_id_type=MESH)
                   for k, to in enumerate(targets)]
    return copies


def _gather_direct(refs, send_sems, recv_sems):
    x, y, c, _ = _place()
    flips = [(dx, dy, dc) for dx in (0, 1) for dy in (0, 1) for dc in (0, 1) if dx + dy + dc]
    targets = [(1 - x if dx else x, 1 - y if dy else y, 1 - c if dc else c) for dx, dy, dc in flips]
    return [pltpu.make_async_remote_copy(src_ref=refs[2 * a], dst_ref=refs[2 * a + 1].at[4 * x + 2 * y + c],
                                         send_sem=send_sems.at[7 * a + k], recv_sem=recv_sems.at[7 * a + k],
                                         device_id=to, device_id_type=MESH)
            for a in range(len(refs) // 2) for k, to in enumerate(targets)]


def _gather_second(refs, send_sems, recv_sems):
    x, y, c, chips = _place()
    copies = []
    for a, land in enumerate(refs):
        for j, (px, py) in enumerate(chips):
            block = land.at[4 * px + 2 * py + c]
            copies.append(pltpu.make_async_remote_copy(src_ref=block, dst_ref=block, send_sem=send_sems.at[3 * a + j],
                                                       recv_sem=recv_sems.at[3 * a + j], device_id=(x, y, 1 - c),
                                                       device_id_type=MESH))
    return copies


def _reduce_first(refs, send_sems, recv_sems):
    x, y, c, _ = _place()
    return [pltpu.make_async_remote_copy(src_ref=refs[2 * a].at[j, 1 - c], dst_ref=refs[2 * a + 1].at[j],
                                         send_sem=send_sems.at[4 * a + j], recv_sem=recv_sems.at[4 * a + j],
                                         device_id=(x, y, 1 - c), device_id_type=MESH)
            for a in range(len(refs) // 2) for j in range(4)]


def _reduce_second(refs, send_sems, recv_sems):
    _, _, c, chips = _place()
    return [pltpu.make_async_remote_copy(src_ref=refs[2 * a].at[2 * px + py], dst_ref=refs[2 * a + 1].at[k],
                                         send_sem=send_sems.at[3 * a + k], recv_sem=recv_sems.at[3 * a + k],
                                         device_id=(px, py, c), device_id_type=MESH)
            for a in range(len(refs) // 2) for k, (px, py) in enumerate(chips)]


def _split_start(name, groups):
    arrays = [a for g in groups for a in g[0]]
    n = len(arrays)

    def body(*refs):
        sems = refs[n:n + 2 * len(groups)]
        at = 0
        for gi, (members, _, build) in enumerate(groups):
            for cp in build(refs[at:at + len(members)], sems[2 * gi], sems[2 * gi + 1]):
                cp.start()
            at += len(members)
        refs[-1][...] = jnp.zeros_like(refs[-1])

    sem_shapes = [pltpu.SemaphoreType.DMA((g[1],)) for g in groups for _ in range(2)]
    outs = pl.pallas_call(
        body, name=name, in_specs=[IN_HBM] * n,
        out_shape=(*sem_shapes, *[_out_hbm(a.shape, a.dtype) for a in arrays], jax.ShapeDtypeStruct((8, 128), F32)),
        out_specs=(*[SEM] * len(sem_shapes), *[IN_HBM] * n, pl.BlockSpec(memory_space=pltpu.VMEM)),
        input_output_aliases={i: len(sem_shapes) + i for i in range(n)}, compiler_params=SPLIT_PARAMS,
    )(*[pltpu.with_memory_space_constraint(a, pltpu.HBM) for a in arrays])
    per_group, at = [], len(sem_shapes)
    for gi, (members, _, _) in enumerate(groups):
        per_group.append((outs[2 * gi], outs[2 * gi + 1], list(outs[at:at + len(members)])))
        at += len(members)
    return per_group, outs[-1]


def _split_wait(name, started, build, after):
    send_sems, recv_sems, arrays = started
    n = len(arrays)
    after = after if isinstance(after, (tuple, list)) else (after,)

    def body(*refs):
        for cp in build(refs[:n], refs[n], refs[n + 1]):
            cp.wait_send()
            cp.wait_recv()

    return pl.pallas_call(
        body, name=name, in_specs=[IN_HBM] * n + [SEM, SEM] + [ANY] * len(after),
        out_shape=tuple(_out_hbm(a.shape, a.dtype) for a in arrays), out_specs=tuple([IN_HBM] * n),
        input_output_aliases={i: i for i in range(n)}, compiler_params=SPLIT_PARAMS,
    )(*arrays, send_sems, recv_sems, *after)


def _placed_behind(token, arrays, name):
    n = len(arrays)

    def body(*refs):
        refs[-1][...] = jnp.zeros_like(refs[-1])

    outs = pl.pallas_call(
        body, name=name, in_specs=[IN_HBM] * n + [ANY],
        out_shape=(*[_out_hbm(a.shape, a.dtype) for a in arrays], jax.ShapeDtypeStruct((8, 128), F32)),
        out_specs=(*[IN_HBM] * n, pl.BlockSpec(memory_space=pltpu.VMEM)),
        input_output_aliases={i: i for i in range(n)},
    )(*map(_in_hbm, arrays), token)
    return outs[:n], outs[-1]


def _gather_landing(shard, me):
    return lax.dynamic_update_slice(lax.empty((N_DEV,) + shard.shape, shard.dtype), shard[None],
                                    (me,) + (0,) * shard.ndim)


ADAM_LANE_TILE = 256


def _tile_2d(rows, cols):
    for t in (256, 176, 128):
        if rows % t == 0:
            return t, cols
    return rows, ADAM_LANE_TILE


def _pair_sum(part, recv, core, name):
    _, rows, cols = recv.shape
    tr, tc = rows, cols

    def body(c_ref, p_ref, r_ref, o_ref):
        del c_ref
        o_ref[...] = (p_ref[...].astype(F32) + r_ref[...].astype(F32)).astype(BF)

    grid_spec = pltpu.PrefetchScalarGridSpec(
        num_scalar_prefetch=1, grid=(4, rows // tr, cols // tc),
        in_specs=[pl.BlockSpec((None, None, tr, tc), lambda j, i, k, c_ref: (j, c_ref[0], i, k)),
                  pl.BlockSpec((None, tr, tc), lambda j, i, k, c_ref: (j, i, k))],
        out_specs=pl.BlockSpec((None, tr, tc), lambda j, i, k, c_ref: (j, i, k)))
    return pl.pallas_call(
        body, name=name, grid_spec=grid_spec, out_shape=_out_hbm(recv.shape, BF),
        compiler_params=_params("parallel", "parallel", "parallel"),
    )(core, *map(_in_hbm, (part, recv)))


def _adamw(w, g, m, v):
    m = ADAM_B1 * m + (1.0 - ADAM_B1) * g
    v = ADAM_B2 * v + (1.0 - ADAM_B2) * (g * g)
    delta = -ADAM_LR * ((m / ADAM_C1) / (jnp.sqrt(v / ADAM_C2) + ADAM_EPS) + ADAM_WD * w)
    return delta, m, v


def _chip_sum_adamw(sums, recv, w, m, v, chip, name, lone_rows=False):
    rows, cols = w.shape
    tr, tc = _tile_2d(rows, cols)

    def body(chip_ref, s_ref, r_ref, w_ref, m_ref, v_ref, *out_refs):
        del chip_ref
        g = s_ref[...].astype(F32)
        for k in range(3):
            g = g + r_ref[k].astype(F32)
        for o_ref, t in zip(out_refs, (g,) + _adamw(w_ref[...], g, m_ref[...], v_ref[...])):
            o_ref[...] = t[:, None, :] if lone_rows else t

    tile = pl.BlockSpec((tr, tc), lambda i, k, chip_ref: (i, k))
    out_tile = pl.BlockSpec((tr, 1, tc), lambda i, k, chip_ref: (i, 0, k)) if lone_rows else tile
    grid_spec = pltpu.PrefetchScalarGridSpec(
        num_scalar_prefetch=1, grid=(rows // tr, cols // tc),
        in_specs=[pl.BlockSpec((None, tr, tc), lambda i, k, chip_ref: (chip_ref[0], i, k)),
                  pl.BlockSpec((3, tr, tc), lambda i, k, chip_ref: (0, i, k)), tile, tile, tile],
        out_specs=[out_tile] * 4)
    return pl.pallas_call(
        body, name=name, grid_spec=grid_spec,
        out_shape=[_out_hbm((rows, 1, cols) if lone_rows else (rows, cols), F32)] * 4,
        compiler_params=_params("parallel", "parallel"),
    )(chip, *map(_in_hbm, (sums, recv, w, m, v)))


def _small_sum_adamw(me, entries, loss):
    def whole(shape, squeeze=0, pick=None):
        blk = (None,) * squeeze + tuple(shape[squeeze:])
        if pick is not None:
            blk = tuple(shape[:pick]) + (None,) + tuple(shape[pick + 1:])
            return pl.BlockSpec(blk, lambda i, me_ref: (0,) * pick + (me_ref[0],) + (0,) * (len(shape) - pick - 1))
        return pl.BlockSpec(blk, lambda i, me_ref: (0,) * len(shape))

    in_specs, out_specs, out_shape, args = [], [], [], []
    for own, parts, w, m, v, sharded in entries + [loss + (None, None, None, False)]:
        in_specs += [whole(own.shape, pick=0 if sharded else None), whole(parts.shape, pick=1 if sharded else None)]
        args += [own, parts]
        if w is not None:
            lead = w.ndim - (parts.ndim - (2 if sharded else 1))
            in_specs += [whole(w.shape, squeeze=lead)] * 3
            out_specs += [whole(w.shape, squeeze=lead)] * 4
            out_shape += [_out_hbm(w.shape, F32)] * 4
            args += [w, m, v]
    out_specs.append(whole(loss[0].shape))
    out_shape.append(_out_hbm(loss[0].shape, F32))
    n = len(entries)

    def added(own_ref, p_ref, me):
        total = None
        for d in range(N_DEV):
            part = jnp.where(me == d, own_ref[...], p_ref[d]).astype(F32)
            total = part if total is None else total + part
        return total

    def body(me_ref, *refs):
        ins, outs = refs[:5 * n + 2], refs[5 * n + 2:]
        for e in range(n):
            own_ref, p_ref, w_ref, m_ref, v_ref = ins[5 * e:5 * e + 5]
            g_out, d_out, m_out, v_out = outs[4 * e:4 * e + 4]
            g = added(own_ref, p_ref, me_ref[0])
            g_out[...] = g
            d_out[...], m_out[...], v_out[...] = _adamw(w_ref[...], g, m_ref[...], v_ref[...])
        outs[4 * n][...] = added(ins[5 * n], ins[5 * n + 1], me_ref[0])

    grid_spec = pltpu.PrefetchScalarGridSpec(num_scalar_prefetch=1, grid=(1,), in_specs=in_specs, out_specs=out_specs)
    outs = pl.pallas_call(body, name="small_sum_adamw", grid_spec=grid_spec, out_shape=out_shape,
                          compiler_params=_params("arbitrary"))(me, *map(_in_hbm, args))
    return [outs[4 * e:4 * e + 4] for e in range(n)], outs[4 * n]


MM_TILE = 512
N_MM_TILES = SEQ // MM_TILE
CAT_TILE = 512
N_CAT_TILES = N_CAT // CAT_TILE
DZ_TILE = 640


def kernel(x, g_mix, w_in, b_gate, w_gk_up, b_gk, w_pool_grp, pool_scale, g_gla_head, w_pool_proj, w_gla_proj, w_out, g_ffn, w_up, w_conv, b_conv, w_down, g_final, loss_target, m_g_mix, m_w_in, m_b_gate, m_w_gk_up, m_b_gk, m_w_pool_grp, m_pool_scale, m_g_gla_head, m_w_pool_proj, m_w_gla_proj, m_w_out, m_g_ffn, m_w_up, m_w_conv, m_b_conv, m_w_down, m_g_final, v_g_mix, v_w_in, v_b_gate, v_w_gk_up, v_b_gk, v_w_pool_grp, v_pool_scale, v_g_gla_head, v_w_pool_proj, v_w_gla_proj, v_w_out, v_g_ffn, v_w_up, v_w_conv, v_b_conv, v_w_down, v_g_final):
    xi, yi, ci = lax.axis_index("x"), lax.axis_index("y"), lax.axis_index("c")
    me = 4 * xi + 2 * yi + ci
    core = jnp.reshape(ci, (1,)).astype(jnp.int32)
    chip = jnp.reshape(2 * xi + yi, (1,)).astype(jnp.int32)
    xs, target = x[0], loss_target[0]

    big = dict(w_in=w_in[0].T, w_pool_proj=w_pool_proj[0], w_gla_proj=w_gla_proj[0], w_out=w_out[0], w_up=w_up[0].T,
               w_down=w_down[0])
    moments = dict(w_in=(m_w_in[0].T, v_w_in[0].T), w_pool_proj=(m_w_pool_proj[0], v_w_pool_proj[0]),
                   w_gla_proj=(m_w_gla_proj[0], v_w_gla_proj[0]), w_out=(m_w_out[0], v_w_out[0]),
                   w_up=(m_w_up[0].T, v_w_up[0].T), w_down=(m_w_down[0], v_w_down[0]))
    names = list(big)
    shards = {k: big[k].astype(BF) for k in names}
    shards["w_gk_up"], shards["w_conv"] = w_gk_up[0], w_conv[0]
    gather_groups = (("w_in", "w_gk_up"), ("w_pool_proj", "w_gla_proj", "w_out"), ("w_up", "w_down", "w_conv"))
    started, token = _split_start("gather_start", [
        ([_gather_landing(shards[k], me) for k in g], 4 * len(g), _gather_first) for g in gather_groups])
    (big["w_in"], *moments["w_in"], bconv4, m_w_conv, v_w_conv), token = _placed_behind(
        token, [big["w_in"], *moments["w_in"], b_conv.reshape(2, 4, 1, FF_BLK), m_w_conv, v_w_conv],
        "place_adamw_operands")

    def gather_pass(gi, after):
        lands = list(_split_wait(f"gather_wait_{gi}", started[gi], _gather_first, after))
        passed, tkn = _split_start(f"gather_pass_{gi}", [(lands, 3 * len(lands), _gather_second)])
        return passed[0], tkn

    def gather_done(gi, passed, after):
        return dict(zip(gather_groups[gi], _split_wait(f"gather_pass_wait_{gi}", passed, _gather_second, after)))

    tok = lambda i, j, k: (i, 0)
    whole = lambda i, j, k: (0, 0)
    kblk = lambda i, j, k: (k, 0)
    ff_seq = (None, None, SEQ, FF_BLK)

    h = _rms_fwd(xs, g_mix, token, "rms_mix")
    wg = gather_done(0, gather_pass(0, h)[0], h)
    wt_cat = _unshard_w_in(wg["w_in"])
    wgk_pad = jnp.pad(wg["w_gk_up"].transpose(1, 0, 2).reshape(GATE_RANK, GLA_DK), ((0, GK_PAD - GATE_RANK), (0, 0)))
    zcat = _mm(h, wt_cat, out_shape=(SEQ, N_CAT), out_dtype=BF, grid=(N_CAT_TILES, 1, 1),
               blk_a=(SEQ, D_MODEL), blk_b=(CAT_TILE, D_MODEL), blk_o=(SEQ, CAT_TILE),
               map_a=whole, map_b=lambda j, i, k: (j, 0), map_o=lambda j, i, k: (0, j), tb=True, name="mm_in")
    la = _gk_fwd(h, wt_cat, wgk_pad, b_gk)
    passed, tkn = gather_pass(1, la)
    o, states = _gla_fwd(zcat, la, tkn)
    wg = gather_done(1, passed, o)
    wpp = wg["w_pool_proj"].transpose(1, 0, 2).reshape(POOL_WIDTH, D_MODEL)
    wgp = wg["w_gla_proj"].reshape(D_MODEL, D_MODEL)
    wout = wg["w_out"].reshape(D_MODEL, D_MODEL)
    og = _post_gla_fwd(o, zcat, g_gla_head)
    ps = _pool_fwd(zcat, w_pool_grp[0], pool_scale)
    passed, tkn = gather_pass(2, (og, ps))
    y_pool, y_gla, mixed, x1, h2 = _mix_out_fwd(ps, og, zcat, xs, wpp, wgp, wout, b_gate, g_ffn, tkn)
    wg = gather_done(2, passed, h2)
    wt_up = wg["w_up"].reshape(2 * D_FF, D_MODEL)
    wdown = wg["w_down"].reshape(D_FF, D_MODEL)
    wconv4 = wg["w_conv"].reshape(2, 4, 3, FF_BLK)
    blk4 = lambda b, i, k: (b // 4, b % 4, 0, 0)
    u4, act = _up_conv_fwd(h2, wt_up, wconv4, bconv4)
    loss_part, dx2, dx2_bf, dg_final = _mm_tokens(
        act, wdown, blk_a=(None, 4, TOK_MM_TILE, FF_BLK), map_a=lambda i: (0, 0, i, 0),
        pieces=[(b, b * FF_BLK, FF_BLK) for b in range(4)], res=x1, then=("loss", g_final.reshape(1, D_MODEL), target),
        name="mm_down_loss")

    da = _mm(dx2_bf, wdown, out_shape=(1, 4, SEQ, FF_BLK), out_dtype=BF, grid=(4, 1, 1),
             blk_a=(SEQ, D_MODEL), blk_b=(FF_BLK, D_MODEL), blk_o=ff_seq,
             map_a=whole, map_b=lambda b, i, k: (b, 0), map_o=lambda b, i, k: (0, b, 0, 0), tb=True, name="mm_d_act")
    d_wdown = _mm(act, dx2_bf, out_shape=(D_FF, D_MODEL), out_dtype=BF, grid=(4, 1, 1),
                  blk_a=ff_seq, blk_b=(SEQ, D_MODEL), blk_o=(FF_BLK, D_MODEL),
                  map_a=lambda b, i, k: (0, b, 0, 0), map_b=whole, map_o=lambda b, i, k: (b, 0), ta=True,
                  name="mm_d_wdown")
    du4, d_wconv, d_bconv = _conv_bwd(u4, da, wconv4, bconv4)
    d_wt_up = _mm(du4, h2, out_shape=(2 * D_FF, D_MODEL), out_dtype=BF, grid=(N_DEV, 1, 1),
                  blk_a=ff_seq, blk_b=(SEQ, D_MODEL), blk_o=(FF_BLK, D_MODEL),
                  map_a=blk4, map_b=whole, map_o=lambda b, i, k: (b, 0), ta=True, name="mm_d_wup")
    res = {}

    def to_sibling(keys, parts):
        return [t for k in keys for t in (parts[k], lax.empty((4,) + parts[k].shape[2:], BF))], 4 * len(keys), _reduce_first

    def to_chips(keys, st, after):
        arrays = _split_wait("reduce_wait_" + keys[0], st, _reduce_first, after)
        sums = [_pair_sum(p, r, core, "pair_sum_" + k) for k, p, r in zip(keys, arrays[0::2], arrays[1::2])]
        return [t for s in sums for t in (s, lax.empty((3,) + s.shape[1:], BF))], 3 * len(keys), _reduce_second

    def reduce_start(keys, parts):
        st, tkn = _split_start("reduce_start_" + keys[0], [to_sibling(keys, parts)])
        return st[0], tkn

    def reduce_cross(keys, st, after):
        st2, tkn = _split_start("reduce_cross_" + keys[0], [to_chips(keys, st, after)])
        return st2[0], tkn

    def reduce_done(keys, st2, after):
        arrays = _split_wait("reduce_cross_wait_" + keys[0], st2, _reduce_second, after)
        for k, s, r in zip(keys, arrays[0::2], arrays[1::2]):
            outs = _chip_sum_adamw(s, r, big[k], moments[k][0], moments[k][1], chip, "adamw_" + k,
                                   lone_rows=k == "w_in")
            res[k] = [jnp.transpose(t, (1, 2, 0)) if k == "w_in" else (t.T if k == "w_up" else t)[None] for t in outs]

    ffn_keys = ("w_down", "w_up")
    ffn_red, tkn = reduce_start(ffn_keys, dict(w_down=d_wdown.reshape(4, 2, D_FF // N_DEV, D_MODEL),
                                               w_up=d_wt_up.reshape(4, 2, FF_BLK, D_MODEL)))
    dx1, dg_ffn = _mm_tokens(
        du4, wt_up, blk_a=(2, 4, TOK_MM_TILE, FF_BLK), map_a=lambda i: (0, 0, i, 0),
        pieces=[((b // 4, b % 4), b * FF_BLK, FF_BLK) for b in range(N_DEV)], after=tkn, then=("rms_bwd", x1, g_ffn, dx2),
        name="mm_d_h2_rms")

    sq_t = dict(out_shape=(D_MODEL, D_MODEL), grid=(1, 1, N_MM_TILES), blk_a=(MM_TILE, D_MODEL),
                blk_b=(MM_TILE, D_MODEL), blk_o=(D_MODEL, D_MODEL), map_a=kblk, map_b=kblk, map_o=whole, ta=True)
    d_wout = _mm(mixed, dx1, out_dtype=BF, name="mm_d_wout", **sq_t)
    dzcat, dy_pool, dy_gla, db_gate = _mix_bwd(dx1, wout, zcat, b_gate, y_pool, y_gla)
    d_wgp = _mm(og, dy_gla, out_dtype=BF, name="mm_d_wgp", **sq_t)
    mix_keys = ("w_out", "w_gla_proj")
    (ffn_red, mix_red), tkn = _split_start("reduce_cross_w_down", [
        to_chips(ffn_keys, ffn_red, db_gate),
        to_sibling(mix_keys, dict(w_out=d_wout.reshape(4, 2, D_MODEL // N_DEV, D_MODEL),
                                  w_gla_proj=d_wgp.reshape(4, 2, D_MODEL // N_DEV, D_MODEL)))])
    dzcat, d_o, dg_head = _post_gla_bwd(dzcat, dy_gla, wgp, o, zcat, g_gla_head, tkn)
    dzcat, dla = _gla_bwd(dzcat, zcat, la, d_o, states)
    dzcat, d_wgk, db_gk = _gk_bwd(dzcat, dla, h, wt_cat, wgk_pad, b_gk)
    dps = _mm(dy_pool, wpp, out_shape=(SEQ, POOL_WIDTH), out_dtype=F32, grid=(N_MM_TILES, 1, 1),
              blk_a=(MM_TILE, D_MODEL), blk_b=(POOL_WIDTH, D_MODEL), blk_o=(MM_TILE, POOL_WIDTH),
              map_a=tok, map_b=whole, map_o=tok, tb=True, name="mm_d_ps")
    d_wpp = _mm(ps, dy_pool, out_shape=(POOL_WIDTH, D_MODEL), out_dtype=F32, grid=(1, 1, N_MM_TILES),
                blk_a=(MM_TILE, POOL_WIDTH), blk_b=(MM_TILE, D_MODEL), blk_o=(POOL_WIDTH, D_MODEL),
                map_a=kblk, map_b=kblk, map_o=whole, ta=True, name="mm_d_wpp")
    dzcat, d_wgrp, d_scale = _pool_bwd(dzcat, zcat, dps, w_pool_grp[0], pool_scale)
    row = lambda t: t.reshape(1, D_MODEL)
    small = [("b_gate", db_gate, b_gate, m_b_gate, v_b_gate, False),
             ("w_gk_up", d_wgk.reshape(GATE_RANK, N_DEV, GLA_DK // N_DEV).transpose(1, 0, 2), w_gk_up, m_w_gk_up,
              v_w_gk_up, True),
             ("b_gk", db_gk, b_gk, m_b_gk, v_b_gk, False),
             ("w_pool_grp", d_wgrp.astype(BF), w_pool_grp, m_w_pool_grp, v_w_pool_grp, False),
             ("pool_scale", d_scale, pool_scale, m_pool_scale, v_pool_scale, False),
             ("g_gla_head", dg_head, g_gla_head, m_g_gla_head, v_g_gla_head, False),
             ("g_ffn", dg_ffn, g_ffn, m_g_ffn, v_g_ffn, False),
             ("w_conv", d_wconv.reshape(N_DEV, 3, FF_BLK), w_conv, m_w_conv, v_w_conv, True),
             ("b_conv", d_bconv.reshape(b_conv.shape), b_conv, m_b_conv, v_b_conv, False),
             ("g_final", dg_final, row(g_final), row(m_g_final), row(v_g_final), False)]

    def to_all(parts):
        return ([t for p in parts for t in (p, lax.empty((N_DEV,) + p.shape, p.dtype))], 7 * len(parts),
                _gather_direct)

    (small_sent, mix_red), tkn = _split_start("small_start", [to_all([t[1] for t in small] + [loss_part]),
                                                              to_chips(mix_keys, mix_red, dla)])
    d_wt_cat = _mm(dzcat, h, out_shape=(N_DZ, D_MODEL), out_dtype=BF, grid=(N_DZ // DZ_TILE, 1, 1),
                   blk_a=(SEQ, DZ_TILE), blk_b=(SEQ, D_MODEL), blk_o=(DZ_TILE, D_MODEL),
                   map_a=lambda j, i, k: (0, j), map_b=whole, map_o=lambda j, i, k: (j, 0), ta=True, after=tkn,
                   name="mm_d_wcat")
    in_keys = ("w_in", "w_pool_proj")
    in_red, tkn = reduce_start(in_keys, dict(
        w_in=_shard_d_w_in(d_wt_cat).reshape(4, 2, IN_SHARD, D_MODEL),
        w_pool_proj=d_wpp.reshape(POOL_WIDTH, N_DEV, D_MODEL // N_DEV).transpose(1, 0, 2).astype(BF)
        .reshape(4, 2, POOL_WIDTH, D_MODEL // N_DEV)))
    reduce_done(mix_keys, mix_red, tkn)
    in_red, tkn = reduce_cross(in_keys, in_red, res["w_out"][0])
    grad_x, dg_mix = _mm_tokens(dzcat, wt_cat, blk_a=(TOK_MM_TILE, N_DZ), map_a=lambda i: (i, 0),
                                pieces=[(None, 0, N_DZ)], after=tkn, then=("rms_bwd", xs, g_mix, dx1),
                                name="mm_d_h_rms")
    (g_mix_sent,), tkn = _split_start("g_mix_start", [to_all([dg_mix])])
    reduce_done(ffn_keys, ffn_red, (grad_x, tkn))
    sent = list(_split_wait("small_wait", small_sent, _gather_direct, res["w_down"][0]))
    small.append(("g_mix", dg_mix, g_mix, m_g_mix, v_g_mix, False))
    sent[-2:-2] = _split_wait("g_mix_wait", g_mix_sent, _gather_direct, sent[1])
    own, gathered = sent[0::2], sent[1::2]
    small_out, loss_sum = _small_sum_adamw(jnp.reshape(me, (1,)).astype(jnp.int32),
                                           [(o, p) + t[2:] for o, p, t in zip(own, gathered, small)],
                                           (own[-1], gathered[-1]))
    for t, outs in zip(small, small_out):
        res[t[0]] = list(outs)
    res["g_final"] = [t.reshape(g_final.shape) for t in res["g_final"]]

    reduce_done(in_keys, in_red, loss_sum)
    loss = loss_sum[0, 0]
    order =["g_mix", "w_in", "b_gate", "w_gk_up", "b_gk", "w_pool_grp", "pool_scale", "g_gla_head", "w_pool_proj",
             "w_gla_proj", "w_out", "g_ffn", "w_up", "w_conv", "b_conv", "w_down", "g_final"]
    return (loss, grad_x[None], *[res[k][0] for k in order], *[res[k][1] for k in order],
            *[res[k][2] for k in order], *[res[k][3] for k in order])
```

```python
import jax
import jax.numpy as jnp
from jax import lax
from jax.experimental import pallas as pl
from jax.experimental.pallas import tpu as pltpu

F32 = jnp.float32
BF = jnp.bfloat16
HIGHEST = lax.Precision.HIGHEST
MESH = pl.DeviceIdType.MESH

N_DEV = 8
SEQ = 2048
D_MODEL = 1024
CHUNK = 64
EPS = 1e-6
POOL_WIDTH = 512
POOL_WINDOWS = (2, 4, 8, 16)
POOL_GD = 128
POOL_HALO = 16
HEADS = 4
HK = 128
HV = 256
GLA_DK = 512
GATE_RANK = 16
GATE_NORM = 16.0
D_FF = 2816
FF_BLK = 704
IN_SHARD = 706
C_QKV, C_GATE, C_OG, C_POOL, C_GK = 0, 2048, 4096, 5120, 5632
N_CAT = 5632
GK_PAD = 128
N_DZ = N_CAT + GK_PAD
R_POOL, R_QKV, R_OG, R_GK, R_GATE = 0, 512, 2560, 3584, 3600

ADAM_LR, ADAM_B1, ADAM_B2, ADAM_EPS, ADAM_WD, ADAM_STEP = 0.001, 0.9, 0.999, 1e-08, 0.01, 10
ADAM_C1 = 1.0 - ADAM_B1 ** ADAM_STEP
ADAM_C2 = 1.0 - ADAM_B2 ** ADAM_STEP

VMEM_BYTES_V7X = 64 * 1024 * 1024
VMEM_LIMIT = VMEM_BYTES_V7X * 3 // 4

TOK_TILE = 256
HALO = 8
GLA_CPS = 4


def _params(*sem):
    return pltpu.CompilerParams(dimension_semantics=sem, vmem_limit_bytes=VMEM_LIMIT)


def _const_spec(shape):
    nd = len(shape)
    return pl.BlockSpec(shape, lambda *_: (0,) * nd)


def _in_hbm(t):
    return pltpu.with_memory_space_constraint(t, pltpu.HBM)


def _out_hbm(shape, dtype):
    return pltpu.HBM(shape, dtype)


def _dot(a, b, ta=False, tb=False):
    dims = (((0 if ta else 1,), (1 if tb else 0,)), ((), ()))
    return lax.dot_general(a.astype(BF), b.astype(BF), dims, preferred_element_type=F32)


def _dot_exact(a, b):
    return jnp.dot(a, b, precision=HIGHEST, preferred_element_type=F32)


def _sigmoid(x):
    return 0.5 * jnp.tanh(0.5 * x) + 0.5


def _mm(a, b, *, out_shape, out_dtype, grid, blk_a, blk_b, blk_o, map_a, map_b, map_o, ta=False, tb=False,
        after=None, name):
    gk = grid[2]
    n_in = 2 + (after is not None)

    def body(*refs):
        a_ref, b_ref, o_ref = refs[0], refs[1], refs[n_in]
        prod = _dot(a_ref[...], b_ref[...], ta, tb)
        if gk == 1:
            o_ref[...] = prod.astype(out_dtype)
        else:
            acc = refs[n_in + 1]
            k = pl.program_id(2)

            @pl.when(k == 0)
            def _():
                acc[...] = prod

            @pl.when(k > 0)
            def _():
                acc[...] += prod

            @pl.when(k == gk - 1)
            def _():
                o_ref[...] = acc[...].astype(out_dtype)

    in_specs = [pl.BlockSpec(blk_a, map_a), pl.BlockSpec(blk_b, map_b)]
    args = [_in_hbm(a), _in_hbm(b)]
    if after is not None:
        in_specs.append(pl.BlockSpec(memory_space=pl.ANY))
        args.append(after)
    return pl.pallas_call(
        body, name=name, grid=grid, in_specs=in_specs, out_specs=pl.BlockSpec(blk_o, map_o),
        out_shape=_out_hbm(out_shape, out_dtype),
        scratch_shapes=[] if gk == 1 else [pltpu.VMEM(tuple(d for d in blk_o if d is not None), F32)],
        compiler_params=_params("parallel", "parallel", "arbitrary"),
    )(*args)


TOK_MM_TILE = 256


def _mm_tokens(a, w, *, blk_a, map_a, pieces, res=None, after=None, then=None, name):
    n_in = 2 + (res is not None) + (after is not None) + (0 if then is None else len(then) - 1)

    def accumulate(ref, part):
        @pl.when(pl.program_id(0) == 0)
        def _():
            ref[...] = part

        @pl.when(pl.program_id(0) > 0)
        def _():
            ref[...] += part

    def body(*refs):
        a_ref, w_ref = refs[:2]
        extra, outs = refs[n_in - (0 if then is None else len(then) - 1):n_in], refs[n_in:]
        total = None
        for idx, row, n in pieces:
            av = a_ref[...] if idx is None else a_ref[idx]
            prod = _dot(av, w_ref[row:row + n, :])
            total = prod if total is None else total + prod
        if res is not None:
            total = total + refs[2][...]
        if then is None:
            outs[0][...] = total
        elif then[0] == "rms_bwd":
            dx, part = _rms_bwd_tile(total, extra[0][...], extra[1][...], extra[2][...])
            outs[0][...] = dx
            accumulate(outs[1], part)
        else:
            lpart, dx, part = _loss_tile(total, extra[0][...], extra[1][...])
            outs[1][...] = dx
            outs[2][...] = dx.astype(BF)
            accumulate(outs[0], lpart)
            accumulate(outs[3], part)

    tile = pl.BlockSpec((TOK_MM_TILE, D_MODEL), lambda i: (i, 0))
    vec = _const_spec((1, D_MODEL))
    big = _out_hbm((SEQ, D_MODEL), F32)
    small = _out_hbm((1, D_MODEL), F32)
    in_specs = [pl.BlockSpec(blk_a, map_a), pl.BlockSpec(w.shape, lambda i: (0, 0), pipeline_mode=pl.Buffered(1))]
    args = [a, w]
    if res is not None:
        in_specs.append(tile)
        args.append(res)
    if after is not None:
        in_specs.append(pl.BlockSpec(memory_space=pl.ANY))
        args.append(after)
    if then is None:
        out_specs, out_shape = tile, big
    elif then[0] == "rms_bwd":
        in_specs += [tile, vec, tile]
        out_specs, out_shape = [tile, vec], [big, small]
    else:
        in_specs += [vec, tile]
        out_specs = [_const_spec((1, 128)), tile, tile, vec]
        out_shape = [_out_hbm((1, 128), F32), big, _out_hbm((SEQ, D_MODEL), BF), small]
    if then is not None:
        args += list(then[1:])
    return pl.pallas_call(
        body, name=name, grid=(SEQ // TOK_MM_TILE,), in_specs=in_specs, out_specs=out_specs, out_shape=out_shape,
        compiler_params=_params("parallel" if then is None else "arbitrary"),
    )(*[_in_hbm(t) for t in args])


def _rms_fwd(x, g, after, name):
    def body(x_ref, g_ref, after_ref, o_ref):
        del after_ref
        xv = x_ref[...]
        r = lax.rsqrt(jnp.mean(xv * xv, axis=-1, keepdims=True) + EPS)
        o_ref[...] = (xv * r * g_ref[...]).astype(BF)

    tile = pl.BlockSpec((TOK_TILE, D_MODEL), lambda i: (i, 0))
    return pl.pallas_call(
        body, name=name, grid=(SEQ // TOK_TILE,),
        in_specs=[tile, _const_spec((1, D_MODEL)), pl.BlockSpec(memory_space=pl.ANY)], out_specs=tile,
        out_shape=_out_hbm((SEQ, D_MODEL), BF), compiler_params=_params("parallel"),
    )(*map(_in_hbm, (x, g)), after)


def _rms_bwd_tile(dyv, xv, gv, dresv):
    r = lax.rsqrt(jnp.mean(xv * xv, axis=-1, keepdims=True) + EPS)
    xn = xv * r
    dxn = dyv * gv
    return dresv + r * (dxn - xn * jnp.mean(dxn * xn, axis=-1, keepdims=True)), jnp.sum(dyv * xn, axis=0, keepdims=True)


def _loss_tile(xv, gv, tv):
    r = lax.rsqrt(jnp.mean(xv * xv, axis=-1, keepdims=True) + EPS)
    xn = xv * r
    err = xn * gv - tv
    lpart = jnp.full((1, 128), 0.5 * jnp.sum(jnp.mean(err * err, axis=-1, keepdims=True)), F32)
    dyv = err * (1.0 / D_MODEL)
    dxn = dyv * gv
    return lpart, r * (dxn - xn * jnp.mean(dxn * xn, axis=-1, keepdims=True)), jnp.sum(dyv * xn, axis=0, keepdims=True)


def _pool_counts(w):
    pos = lax.broadcasted_iota(jnp.int32, (SEQ, 1), 0).astype(F32)
    return jnp.minimum(pos + 1.0, float(w))


def _pool_window(u, w, ext):
    ext[pl.ds(POOL_HALO, SEQ), :] = u
    win = u
    for j in range(1, w):
        win = win + ext[pl.ds(POOL_HALO - j, SEQ), :]
    return win / _pool_counts(w) - u


def _pool_fwd(zcat, w_grp, scale):
    def body(z_ref, w_ref, s_ref, o_ref, ext):
        ext[pl.ds(0, POOL_HALO), :] = jnp.zeros((POOL_HALO, POOL_GD), F32)
        for g, w in enumerate(POOL_WINDOWS):
            cols = slice(g * POOL_GD, (g + 1) * POOL_GD)
            p = _pool_window(z_ref[:, cols].astype(F32), w, ext)
            o_ref[:, cols] = (_dot(p, w_ref[g]) * s_ref[:, cols]).astype(BF)

    return pl.pallas_call(
        body, name="pool_fwd", grid=(1,),
        in_specs=[pl.BlockSpec((SEQ, POOL_WIDTH), lambda i: (0, C_POOL // POOL_WIDTH)),
                  _const_spec((4, POOL_GD, POOL_GD)), _const_spec((1, POOL_WIDTH))],
        out_specs=_const_spec((SEQ, POOL_WIDTH)), out_shape=_out_hbm((SEQ, POOL_WIDTH), BF),
        scratch_shapes=[pltpu.VMEM((POOL_HALO + SEQ, POOL_GD), F32)], compiler_params=_params("arbitrary"),
    )(*map(_in_hbm, (zcat, w_grp, scale)))


def _pool_bwd(dzcat, zcat, dps, w_grp, scale):
    def body(dz_in, z_ref, dps_ref, w_ref, s_ref, dz_ref, dw_ref, dsc_ref, ext, ext2):
        del dz_in
        ext[pl.ds(0, POOL_HALO), :] = jnp.zeros((POOL_HALO, POOL_GD), F32)
        ext2[pl.ds(SEQ, POOL_HALO), :] = jnp.zeros((POOL_HALO, POOL_GD), F32)
        for g, w in enumerate(POOL_WINDOWS):
            cols = slice(g * POOL_GD, (g + 1) * POOL_GD)
            p = _pool_window(z_ref[:, cols].astype(F32), w, ext)
            wg = w_ref[g]
            pg = _dot(p, wg)
            dpsv = dps_ref[:, cols]
            dsc_ref[:, cols] = jnp.sum(dpsv * pg, axis=0, keepdims=True)
            dpg = dpsv * s_ref[:, cols]
            dw_ref[g] = _dot(p, dpg, ta=True)
            dp = _dot(dpg, wg, tb=True)
            dpc = dp / _pool_counts(w)
            ext2[pl.ds(0, SEQ), :] = dpc
            du = dpc
            for j in range(1, w):
                du = du + ext2[pl.ds(j, SEQ), :]
            dz_ref[:, cols] = (du - dp).astype(BF)

    return pl.pallas_call(
        body, name="pool_bwd", grid=(1,),
        in_specs=[pl.BlockSpec(memory_space=pl.ANY),
                  pl.BlockSpec((SEQ, POOL_WIDTH), lambda i: (0, C_POOL // POOL_WIDTH)),
                  _const_spec((SEQ, POOL_WIDTH)), _const_spec((4, POOL_GD, POOL_GD)), _const_spec((1, POOL_WIDTH))],
        out_specs=[pl.BlockSpec((SEQ, POOL_WIDTH), lambda i: (0, C_POOL // POOL_WIDTH)),
                   _const_spec((4, POOL_GD, POOL_GD)), _const_spec((1, POOL_WIDTH))],
        out_shape=[_out_hbm((SEQ, N_DZ), BF), _out_hbm((4, POOL_GD, POOL_GD), F32),
                   _out_hbm((1, POOL_WIDTH), F32)],
        scratch_shapes=[pltpu.VMEM((POOL_HALO + SEQ, POOL_GD), F32), pltpu.VMEM((SEQ + POOL_HALO, POOL_GD), F32)],
        input_output_aliases={0: 0}, compiler_params=_params("arbitrary"),
    )(*map(_in_hbm, (dzcat, zcat, dps, w_grp, scale)))


GK_TILE = 512


GK_ROWS = pl.BlockSpec((GK_PAD, D_MODEL), lambda i: (C_GK // GK_PAD, 0))


def _gk_fwd(h, wt_cat, wgk_pad, b_gk):
    def body(h_ref, wt_ref, w_ref, b_ref, la_ref):
        z_gk = _dot(h_ref[...], wt_ref[...], tb=True)
        pre = _dot(z_gk, w_ref[...]) + b_ref[...]
        la_ref[...] = (jnp.minimum(pre, 0.0) - jnp.log(1.0 + jnp.exp(-jnp.abs(pre)))) * (1.0 / GATE_NORM)

    return pl.pallas_call(
        body, name="gk_fwd", grid=(SEQ // GK_TILE,),
        in_specs=[pl.BlockSpec((GK_TILE, D_MODEL), lambda i: (i, 0)), GK_ROWS,
                  _const_spec((GK_PAD, GLA_DK)), _const_spec((1, GLA_DK))],
        out_specs=pl.BlockSpec((GK_TILE, GLA_DK), lambda i: (i, 0)),
        out_shape=_out_hbm((SEQ, GLA_DK), F32), compiler_params=_params("parallel"),
    )(*map(_in_hbm, (h, wt_cat, wgk_pad, b_gk)))


def _gk_bwd(dzcat, dla, h, wt_cat, wgk_pad, b_gk):
    def body(dz_in, dla_ref, h_ref, wt_ref, w_ref, b_ref, dz_ref, dw_ref, db_ref):
        del dz_in
        wv = w_ref[...]
        z_gk = _dot(h_ref[...], wt_ref[...], tb=True)
        pre = _dot(z_gk, wv) + b_ref[...]
        dpre = dla_ref[...] * (1.0 / GATE_NORM) * (1.0 - _sigmoid(pre))
        dz_ref[...] = _dot(dpre, wv, tb=True).astype(BF)
        dwp = _dot(z_gk, dpre, ta=True)[:GATE_RANK]
        dbp = jnp.sum(dpre, axis=0, keepdims=True)

        @pl.when(pl.program_id(0) == 0)
        def _():
            dw_ref[...] = dwp
            db_ref[...] = dbp

        @pl.when(pl.program_id(0) > 0)
        def _():
            dw_ref[...] += dwp
            db_ref[...] += dbp

    return pl.pallas_call(
        body, name="gk_bwd", grid=(SEQ // GK_TILE,),
        in_specs=[pl.BlockSpec(memory_space=pl.ANY), pl.BlockSpec((GK_TILE, GLA_DK), lambda i: (i, 0)),
                  pl.BlockSpec((GK_TILE, D_MODEL), lambda i: (i, 0)), GK_ROWS, _const_spec((GK_PAD, GLA_DK)),
                  _const_spec((1, GLA_DK))],
        out_specs=[pl.BlockSpec((GK_TILE, GK_PAD), lambda i: (i, C_GK // GK_PAD)), _const_spec((GATE_RANK, GLA_DK)),
                   _const_spec((1, GLA_DK))],
        out_shape=[_out_hbm((SEQ, N_DZ), BF), _out_hbm((GATE_RANK, GLA_DK), F32),
                   _out_hbm((1, GLA_DK), F32)],
        input_output_aliases={0: 0}, compiler_params=_params("arbitrary"),
    )(*map(_in_hbm, (dzcat, dla, h, wt_cat, wgk_pad, b_gk)))


GLA_ROWS = GLA_CPS * CHUNK
GLA_STEPS = SEQ // GLA_ROWS
QKV_W = 2048


def _tri():
    return lax.broadcasted_iota(jnp.int32, (CHUNK, CHUNK), 0) >= lax.broadcasted_iota(jnp.int32, (CHUNK, CHUNK), 1)


def _chunk_cumsum(la_ref, rows):
    return _dot_exact(_tri().astype(F32), la_ref[rows, :])


def _gla_chunk(qkv_ref, la_ref, rows, h, bc_all):
    tri = _tri()
    q = qkv_ref[rows, h * HK:(h + 1) * HK].astype(F32) * (HK ** -0.5)
    k = qkv_ref[rows, GLA_DK + h * HK:GLA_DK + (h + 1) * HK].astype(F32)
    v = qkv_ref[rows, 2 * GLA_DK + h * HV:2 * GLA_DK + (h + 1) * HV].astype(BF)
    la = la_ref[rows, h * HK:(h + 1) * HK]
    bc = bc_all[:, h * HK:(h + 1) * HK]
    e_pos, e_neg = jnp.exp(bc), jnp.exp(-bc)
    dl = jnp.exp(jnp.sum(la, axis=0, keepdims=True))
    q_fw, q_bw, k_fw, k_bw = q * e_pos, q * e_neg, k * e_neg, k * e_pos
    scores = jnp.where(tri, _dot(q_fw, k_fw, tb=True), _dot(q_bw, k_bw, tb=True))
    return tri, v, e_pos, e_neg, dl, q_fw, q_bw, k_fw, k_bw, scores


def _gla_fwd(zcat, la, after):
    def body(qkv_ref, la_ref, after_ref, o_ref, st_ref, state):
        del after_ref

        @pl.when(pl.program_id(0) == 0)
        def _():
            state[...] = jnp.zeros_like(state)

        for c in range(GLA_CPS):
            rows = slice(c * CHUNK, (c + 1) * CHUNK)
            bc_all = _chunk_cumsum(la_ref, rows)
            for h in range(HEADS):
                _, v, _, _, dl, q_fw, _, k_fw, _, scores = _gla_chunk(qkv_ref, la_ref, rows, h, bc_all)
                st = state[h]
                st_ref[c, h] = st
                o_ref[rows, h * HV:(h + 1) * HV] = _dot(scores, v) + _dot(q_fw, st, tb=True)
                state[h] = st * dl + _dot(v, k_fw * dl, ta=True)

    return pl.pallas_call(
        body, name="gla_fwd", grid=(GLA_STEPS,),
        in_specs=[pl.BlockSpec((GLA_ROWS, QKV_W), lambda i: (i, 0)), pl.BlockSpec((GLA_ROWS, GLA_DK), lambda i: (i, 0)),
                  pl.BlockSpec(memory_space=pl.ANY)],
        out_specs=[pl.BlockSpec((GLA_ROWS, D_MODEL), lambda i: (i, 0)),
                   pl.BlockSpec((GLA_CPS, HEADS, HV, HK), lambda i: (i, 0, 0, 0))],
        out_shape=[_out_hbm((SEQ, D_MODEL), F32),
                   _out_hbm((SEQ // CHUNK, HEADS, HV, HK), F32)],
        scratch_shapes=[pltpu.VMEM((HEADS, HV, HK), F32)], compiler_params=_params("arbitrary"),
    )(*map(_in_hbm, (zcat, la)), after)


def _gla_bwd(dzcat, zcat, la, d_o, states):
    def body(dz_in, qkv_ref, la_ref, do_ref, st_ref, dqkv_ref, dla_ref, dstate):
        del dz_in

        @pl.when(pl.program_id(0) == 0)
        def _():
            dstate[...] = jnp.zeros_like(dstate)

        last_row = lax.broadcasted_iota(jnp.int32, (CHUNK, HK), 0) == CHUNK - 1
        upper = (lax.broadcasted_iota(jnp.int32, (CHUNK, CHUNK), 0)
                 <= lax.broadcasted_iota(jnp.int32, (CHUNK, CHUNK), 1)).astype(F32)
        for c in reversed(range(GLA_CPS)):
            rows = slice(c * CHUNK, (c + 1) * CHUNK)
            bc_all = _chunk_cumsum(la_ref, rows)
            dbs = []
            for h in range(HEADS):
                tri, v, e_pos, e_neg, dl, q_fw, q_bw, k_fw, k_bw, scores = _gla_chunk(qkv_ref, la_ref, rows, h, bc_all)
                st = st_ref[c, h]
                dst = dstate[h]
                d_out = do_ref[rows, h * HV:(h + 1) * HV].astype(BF)
                k_dec = k_fw * dl
                dp = _dot(d_out, v, tb=True)
                dp_fw = jnp.where(tri, dp, 0.0)
                dp_bw = jnp.where(tri, 0.0, dp)
                dv = _dot(scores, d_out, ta=True) + _dot(k_dec, dst, tb=True)
                dk_dec = _dot(v, dst)
                dq_fw = _dot(dp_fw, k_fw) + _dot(d_out, st)
                dk_fw = _dot(dp_fw, q_fw, ta=True) + dk_dec * dl
                dq_bw = _dot(dp_bw, k_bw)
                dk_bw = _dot(dp_bw, q_bw, ta=True)
                ddl = jnp.sum(st * dst, axis=0, keepdims=True) + jnp.sum(k_fw * dk_dec, axis=0, keepdims=True)
                dstate[h] = dst * dl + _dot(d_out, q_fw, ta=True)
                dq = (dq_fw * e_pos + dq_bw * e_neg) * (HK ** -0.5)
                dk = dk_fw * e_neg + dk_bw * e_pos
                dbs.append(dq_fw * q_fw - dk_fw * k_fw - dq_bw * q_bw + dk_bw * k_bw + jnp.where(last_row, ddl * dl, 0.0))
                dqkv_ref[rows, h * HK:(h + 1) * HK] = dq.astype(BF)
                dqkv_ref[rows, GLA_DK + h * HK:GLA_DK + (h + 1) * HK] = dk.astype(BF)
                dqkv_ref[rows, 2 * GLA_DK + h * HV:2 * GLA_DK + (h + 1) * HV] = dv.astype(BF)
            dla_ref[rows, :] = _dot_exact(upper, jnp.concatenate(dbs, axis=1))

    rev = lambda i: (GLA_STEPS - 1 - i, 0)
    return pl.pallas_call(
        body, name="gla_bwd", grid=(GLA_STEPS,),
        in_specs=[pl.BlockSpec(memory_space=pl.ANY), pl.BlockSpec((GLA_ROWS, QKV_W), rev),
                  pl.BlockSpec((GLA_ROWS, GLA_DK), rev), pl.BlockSpec((GLA_ROWS, D_MODEL), rev),
                  pl.BlockSpec((GLA_CPS, HEADS, HV, HK), lambda i: (GLA_STEPS - 1 - i, 0, 0, 0))],
        out_specs=[pl.BlockSpec((GLA_ROWS, QKV_W), rev), pl.BlockSpec((GLA_ROWS, GLA_DK), rev)],
        out_shape=[_out_hbm((SEQ, N_DZ), BF), _out_hbm((SEQ, GLA_DK), F32)],
        scratch_shapes=[pltpu.VMEM((HEADS, HV, HK), F32)], input_output_aliases={0: 0},
        compiler_params=_params("arbitrary"),
    )(*map(_in_hbm, (dzcat, zcat, la, d_o, states)))


def _silu_parts(x):
    s = _sigmoid(x)
    return x * s, s * (1.0 + x * (1.0 - s))


def _post_gla_fwd(o, zcat, g_head):
    def body(o_ref, zog_ref, g_ref, out_ref):
        for h in range(HEADS):
            cols = slice(h * HV, (h + 1) * HV)
            ov = o_ref[:, cols]
            r = lax.rsqrt(jnp.mean(ov * ov, axis=-1, keepdims=True) + EPS)
            act, _ = _silu_parts(zog_ref[:, cols].astype(F32))
            out_ref[:, cols] = (ov * r * g_ref[...] * act).astype(BF)

    tile = pl.BlockSpec((TOK_TILE, D_MODEL), lambda i: (i, 0))
    return pl.pallas_call(
        body, name="post_gla_fwd", grid=(SEQ // TOK_TILE,),
        in_specs=[tile, pl.BlockSpec((TOK_TILE, D_MODEL), lambda i: (i, C_OG // D_MODEL)), _const_spec((1, HV))],
        out_specs=tile, out_shape=_out_hbm((SEQ, D_MODEL), BF), compiler_params=_params("parallel"),
    )(*map(_in_hbm, (o, zcat, g_head)))


def _post_gla_bwd(dzcat, dy_gla, w_gla_proj, o, zcat, g_head, after):
    def body(dz_in, dyg_ref, w_ref, o_ref, zog_ref, g_ref, after_ref, dz_ref, do_ref, dg_ref):
        del dz_in, after_ref
        dog = _dot(dyg_ref[...], w_ref[...], tb=True)
        gpart = jnp.zeros((1, HV), F32)
        gv = g_ref[...]
        for h in range(HEADS):
            cols = slice(h * HV, (h + 1) * HV)
            ov = o_ref[:, cols]
            r = lax.rsqrt(jnp.mean(ov * ov, axis=-1, keepdims=True) + EPS)
            on = ov * r
            act, dact = _silu_parts(zog_ref[:, cols].astype(F32))
            dogv = dog[:, cols]
            dz_ref[:, cols] = (dogv * on * gv * dact).astype(BF)
            d_on_g = dogv * act
            gpart = gpart + jnp.sum(d_on_g * on, axis=0, keepdims=True)
            dxn = d_on_g * gv
            do_ref[:, cols] = (r * (dxn - on * jnp.mean(dxn * on, axis=-1, keepdims=True))).astype(BF)

        @pl.when(pl.program_id(0) == 0)
        def _():
            dg_ref[...] = gpart

        @pl.when(pl.program_id(0) > 0)
        def _():
            dg_ref[...] += gpart

    tile = pl.BlockSpec((TOK_TILE, D_MODEL), lambda i: (i, 0))
    ogspec = pl.BlockSpec((TOK_TILE, D_MODEL), lambda i: (i, C_OG // D_MODEL))
    return pl.pallas_call(
        body, name="post_gla_bwd", grid=(SEQ // TOK_TILE,),
        in_specs=[pl.BlockSpec(memory_space=pl.ANY), tile, _const_spec((D_MODEL, D_MODEL)), tile, ogspec,
                  _const_spec((1, HV)), pl.BlockSpec(memory_space=pl.ANY)],
        out_specs=[ogspec, tile, _const_spec((1, HV))],
        out_shape=[_out_hbm((SEQ, N_DZ), BF), _out_hbm((SEQ, D_MODEL), BF),
                   _out_hbm((1, HV), F32)],
        input_output_aliases={0: 0}, compiler_params=_params("arbitrary"),
    )(*map(_in_hbm, (dzcat, dy_gla, w_gla_proj, o, zcat, g_head)), after)


GATE_W = 2 * D_MODEL


def _mix_out_fwd(ps, og, zcat, x, w_pool_proj, w_gla_proj, w_out, b_gate, g_ffn, after):
    def body(ps_ref, og_ref, zg_ref, x_ref, wpp_ref, wgp_ref, wout_ref, b_ref, g_ref, after_ref,
             yp_ref, yg_ref, mixed_ref, x1_ref, h2_ref):
        del after_ref
        y_pool = _dot(ps_ref[...], wpp_ref[...])
        y_gla = _dot(og_ref[...], wgp_ref[...])
        yp_ref[...] = y_pool.astype(BF)
        yg_ref[...] = y_gla.astype(BF)
        g0 = _sigmoid(zg_ref[:, :D_MODEL].astype(F32) + b_ref[:, :D_MODEL])
        g1 = _sigmoid(zg_ref[:, D_MODEL:].astype(F32) + b_ref[:, D_MODEL:])
        mixed = (g0 * y_pool + g1 * y_gla).astype(BF)
        mixed_ref[...] = mixed
        x1 = x_ref[...] + _dot(mixed, wout_ref[...])
        x1_ref[...] = x1
        r = lax.rsqrt(jnp.mean(x1 * x1, axis=-1, keepdims=True) + EPS)
        h2_ref[...] = (x1 * r * g_ref[...]).astype(BF)

    tile = pl.BlockSpec((TOK_TILE, D_MODEL), lambda i: (i, 0))
    resident = lambda shape: pl.BlockSpec(shape, lambda i: (0, 0), pipeline_mode=pl.Buffered(1))
    f32, bf16 = _out_hbm((SEQ, D_MODEL), F32), _out_hbm((SEQ, D_MODEL), BF)
    return pl.pallas_call(
        body, name="mix_out_fwd", grid=(SEQ // TOK_TILE,),
        in_specs=[pl.BlockSpec((TOK_TILE, POOL_WIDTH), lambda i: (i, 0)), tile,
                  pl.BlockSpec((TOK_TILE, GATE_W), lambda i: (i, C_GATE // GATE_W)), tile,
                  resident((POOL_WIDTH, D_MODEL)), resident((D_MODEL, D_MODEL)), resident((D_MODEL, D_MODEL)),
                  _const_spec((1, GATE_W)), _const_spec((1, D_MODEL)), pl.BlockSpec(memory_space=pl.ANY)],
        out_specs=[tile] * 5, out_shape=[bf16, bf16, bf16, f32, bf16], compiler_params=_params("parallel"),
    )(*map(_in_hbm, (ps, og, zcat, x, w_pool_proj, w_gla_proj, w_out, b_gate, g_ffn)), after)


def _mix_bwd(dx1, w_out, zcat, b_gate, y_pool, y_gla):
    def body(dx_ref, w_ref, zg_ref, b_ref, yp_ref, yg_ref, dz_ref, dyp_ref, dyg_ref, db_ref):
        dm = _dot(dx_ref[...], w_ref[...], tb=True)
        g0 = _sigmoid(zg_ref[:, :D_MODEL].astype(F32) + b_ref[:, :D_MODEL])
        g1 = _sigmoid(zg_ref[:, D_MODEL:].astype(F32) + b_ref[:, D_MODEL:])
        dyp_ref[...] = (dm * g0).astype(BF)
        dyg_ref[...] = (dm * g1).astype(BF)
        dz0 = dm * yp_ref[...].astype(F32) * g0 * (1.0 - g0)
        dz1 = dm * yg_ref[...].astype(F32) * g1 * (1.0 - g1)
        dz_ref[:, :D_MODEL] = dz0.astype(BF)
        dz_ref[:, D_MODEL:] = dz1.astype(BF)
        b0 = jnp.sum(dz0, axis=0, keepdims=True)
        b1 = jnp.sum(dz1, axis=0, keepdims=True)

        @pl.when(pl.program_id(0) == 0)
        def _():
            db_ref[:, :D_MODEL] = b0
            db_ref[:, D_MODEL:] = b1

        @pl.when(pl.program_id(0) > 0)
        def _():
            db_ref[:, :D_MODEL] += b0
            db_ref[:, D_MODEL:] += b1

    tile = pl.BlockSpec((TOK_TILE, D_MODEL), lambda i: (i, 0))
    gspec = pl.BlockSpec((TOK_TILE, GATE_W), lambda i: (i, C_GATE // GATE_W))
    return pl.pallas_call(
        body, name="mix_bwd", grid=(SEQ // TOK_TILE,),
        in_specs=[tile, _const_spec((D_MODEL, D_MODEL)), gspec, _const_spec((1, GATE_W)), tile, tile],
        out_specs=[gspec, tile, tile, _const_spec((1, GATE_W))],
        out_shape=[_out_hbm((SEQ, N_DZ), BF), _out_hbm((SEQ, D_MODEL), BF),
                   _out_hbm((SEQ, D_MODEL), BF), _out_hbm((1, GATE_W), F32)],
        compiler_params=_params("arbitrary"),
    )(*map(_in_hbm, (dx1, w_out, zcat, b_gate, y_pool, y_gla)))


N_TOK_TILES = SEQ // TOK_TILE
HALO_PER_TILE = TOK_TILE // HALO


LANE_TILES = tuple((lo, min(128, FF_BLK - lo)) for lo in range(0, FF_BLK, 128))


def _taps(w_ref, b_ref, half, lanes, rows):
    shape = (rows, lanes.stop - lanes.start)
    return ([jnp.broadcast_to(w_ref[half, j:j + 1, lanes], shape) for j in range(3)],
            jnp.broadcast_to(b_ref[half, :, lanes], shape))


def _conv_strips(u_ref, ub_ref, ua_ref, taps, lanes, width, n_strips, first):
    row = lax.broadcasted_iota(jnp.int32, (HALO, width), 0)
    prev = [[pltpu.roll(jnp.where(first, 0.0, ub_ref[half, :, lanes]), k, 0) for k in (1, 2)] for half in range(2)]
    for s in range(n_strips + (ua_ref is not None)):
        u3, conv = [], []
        for half in range(2):
            cur = u_ref[half, s * HALO:(s + 1) * HALO, lanes] if s < n_strips else ua_ref[half, :, lanes]
            rolled = [pltpu.roll(cur, k, 0) for k in (1, 2)]
            frames = [jnp.where(row >= 2, rolled[1], prev[half][1]), jnp.where(row >= 1, rolled[0], prev[half][0]), cur]
            prev[half] = rolled
            w3, bias = taps[half]
            u3.append(frames)
            conv.append(bias + frames[0] * w3[0] + frames[1] * w3[1] + frames[2] * w3[2])
        yield s, u3, conv


def _pair_specs(pairs):
    tile = pl.BlockSpec((pairs, None, TOK_TILE, FF_BLK), lambda b, i: (0, b, i, 0))
    before = pl.BlockSpec((pairs, None, HALO, FF_BLK), lambda b, i: (0, b, jnp.maximum(i * HALO_PER_TILE - 1, 0), 0))
    after = pl.BlockSpec((pairs, None, HALO, FF_BLK),
                         lambda b, i: (0, b, jnp.minimum((i + 1) * HALO_PER_TILE, SEQ // HALO - 1), 0))

    def vec(rows):
        return pl.BlockSpec((2, None, rows, FF_BLK), lambda b, i: (0, b, 0, 0))

    return tile, before, after, vec


N_STRIPS = TOK_TILE // HALO


def _up_conv_fwd(h2, wt_up, w_conv, b_conv):
    steps = N_TOK_TILES // 2

    def body(h_ref, h_next, wg_ref, wv_ref, w_ref, b_ref, u_ref, a_ref, buf_a, buf_b, carry):
        j = pl.program_id(1)

        def project(hv, buf):
            buf[0] = _dot(hv, wg_ref[...], tb=True)
            buf[1] = _dot(hv, wv_ref[...], tb=True)

        def conv(buf, row0):
            u_ref[:, row0:row0 + TOK_TILE, :] = buf[...]
            for lo, width in LANE_TILES:
                lanes = slice(lo, lo + width)
                taps = [_taps(w_ref, b_ref, half, lanes, HALO) for half in range(2)]
                pending = None
                for s, _, (cg, cv) in _conv_strips(buf, carry, None, taps, lanes, width, N_STRIPS, False):
                    act = cg * _sigmoid(cg) * cv
                    if s % 2 == 0:
                        pending = act
                    else:
                        a_ref[0, row0 + (s - 1) * HALO:row0 + (s + 1) * HALO, lanes] = (
                            jnp.concatenate([pending, act], axis=0).astype(BF))
            carry[...] = buf[:, TOK_TILE - HALO:, :]

        @pl.when(j == 0)
        def _():
            project(h_ref[0:TOK_TILE, :], buf_a)
            carry[...] = jnp.zeros_like(carry)

        project(h_ref[TOK_TILE:, :], buf_b)
        conv(buf_a, 0)
        project(h_next[...], buf_a)
        conv(buf_b, TOK_TILE)

    w_blk = lambda half: pl.BlockSpec((FF_BLK, D_MODEL), lambda b, j: (b + 4 * half, 0))
    vec = lambda rows: pl.BlockSpec((2, None, rows, FF_BLK), lambda b, j: (0, b, 0, 0))
    u_buf = pltpu.VMEM((2, TOK_TILE, FF_BLK), F32)
    return pl.pallas_call(
        body, name="up_conv_fwd", grid=(4, steps),
        in_specs=[pl.BlockSpec((2 * TOK_TILE, D_MODEL), lambda b, j: (j, 0)),
                  pl.BlockSpec((TOK_TILE, D_MODEL), lambda b, j: (jnp.minimum(2 * j + 2, N_TOK_TILES - 1), 0)),
                  w_blk(0), w_blk(1), vec(3), vec(1)],
        out_specs=[pl.BlockSpec((2, None, 2 * TOK_TILE, FF_BLK), lambda b, j: (0, b, j, 0)),
                   pl.BlockSpec((1, None, 2 * TOK_TILE, FF_BLK), lambda b, j: (0, b, j, 0))],
        out_shape=[_out_hbm((2, 4, SEQ, FF_BLK), F32), _out_hbm((1, 4, SEQ, FF_BLK), BF)],
        scratch_shapes=[u_buf, u_buf, pltpu.VMEM((2, HALO, FF_BLK), F32)],
        compiler_params=_params("parallel", "arbitrary"),
    )(*map(_in_hbm, (h2, h2, wt_up, wt_up, w_conv, b_conv)))


def _conv_bwd(u, da, w_conv, b_conv):
    def body(u_ref, ub_ref, ua_ref, da_ref, daa_ref, w_ref, b_ref, du_ref, dw_ref, db_ref):
        i = pl.program_id(1)

        @pl.when(i == 0)
        def _():
            dw_ref[...] = jnp.zeros_like(dw_ref)
            db_ref[...] = jnp.zeros_like(db_ref)

        for lo, width in LANE_TILES:
            lanes = slice(lo, lo + width)
            row = lax.broadcasted_iota(jnp.int32, (HALO, width), 0)
            taps = [_taps(w_ref, b_ref, half, lanes, HALO) for half in range(2)]
            acc_w = [[jnp.zeros((HALO, width), F32) for _ in range(3)] for _ in range(2)]
            acc_b = [jnp.zeros((HALO, width), F32) for _ in range(2)]
            da_pair, pending = None, [None, None]
            dc_prev, up_prev = [None, None], [None, None]
            for s, u3, (cg, cv) in _conv_strips(u_ref, ub_ref, ua_ref, taps, lanes, width, N_STRIPS, i == 0):
                act, dact = _silu_parts(cg)
                if s == N_STRIPS:
                    da = jnp.where(i < N_TOK_TILES - 1, daa_ref[0, :, lanes].astype(F32), 0.0)
                elif s % 2 == 0:
                    da_pair = da_ref[0, s * HALO:(s + 2) * HALO, lanes].astype(F32)
                    da = da_pair[:HALO]
                else:
                    da = da_pair[HALO:]
                dc = (da * cv * dact, da * act)
                for half in range(2):
                    up = [pltpu.roll(dc[half], HALO - k, 0) for k in (1, 2)]
                    if s < N_STRIPS:
                        for j in range(3):
                            acc_w[half][j] = acc_w[half][j] + dc[half] * u3[half][j]
                        acc_b[half] = acc_b[half] + dc[half]
                    if s >= 1:
                        w3 = taps[half][0]
                        du = (dc_prev[half] * w3[2] + jnp.where(row < HALO - 1, up_prev[half][0], up[0]) * w3[1]
                              + jnp.where(row < HALO - 2, up_prev[half][1], up[1]) * w3[0])
                        if (s - 1) % 2 == 0:
                            pending[half] = du
                        else:
                            du_ref[half, (s - 2) * HALO:s * HALO, lanes] = jnp.concatenate([pending[half], du],
                                                                                           axis=0).astype(BF)
                    dc_prev[half], up_prev[half] = dc[half], up
            for half in range(2):
                for j in range(3):
                    dw_ref[half, j:j + 1, lanes] += jnp.sum(acc_w[half][j], axis=0, keepdims=True)
                db_ref[half, :, lanes] += jnp.sum(acc_b[half], axis=0, keepdims=True)

    tile, before, after, vec = _pair_specs(2)
    da_tile, _, da_after_spec, _ = _pair_specs(1)
    return pl.pallas_call(
        body, name="conv_bwd", grid=(4, N_TOK_TILES),
        in_specs=[tile, before, after, da_tile, da_after_spec, vec(3), vec(1)],
        out_specs=[tile, vec(3), vec(1)],
        out_shape=[_out_hbm((2, 4, SEQ, FF_BLK), BF), _out_hbm((2, 4, 3, FF_BLK), F32),
                   _out_hbm((2, 4, 1, FF_BLK), F32)],
        compiler_params=_params("parallel", "arbitrary"),
    )(*map(_in_hbm, (u, u, u, da, da, w_conv, b_conv)))


W_IN_SEGMENTS = ((R_POOL, POOL_WIDTH, C_POOL), (R_QKV, QKV_W, C_QKV), (R_OG, D_MODEL, C_OG), (R_GK, GATE_RANK, C_GK),
                 (R_GATE, GATE_W, C_GATE))


def _slab_pieces(d):
    lo, hi = d * IN_SHARD, (d + 1) * IN_SHARD
    pieces = []
    for start, n, at in W_IN_SEGMENTS:
        a, b = max(lo, start), min(hi, start + n)
        if a < b:
            assert (a - lo) % 2 == 0 and (b - a) % 2 == 0 and (at + a - start) % 2 == 0
            pieces.append(((a - lo) // 2, (b - a) // 2, (at + a - start) // 2))
    return pieces


def _unshard_w_in(slabs):
    def body(slab_ref, cat_ref):
        d = pl.program_id(0)
        src = slab_ref.bitcast(jnp.uint32)
        dst = cat_ref.bitcast(jnp.uint32)

        @pl.when(d == 0)
        def _():
            cat_ref[C_GK:, :] = jnp.zeros((GK_PAD, D_MODEL), BF)

        for dd in range(N_DEV):
            @pl.when(d == dd)
            def _():
                for a, n, at in _slab_pieces(dd):
                    dst[pl.ds(at, n), :] = src[0, pl.ds(a, n), :]

    return pl.pallas_call(
        body, name="unshard_w_in", grid=(N_DEV,),
        in_specs=[pl.BlockSpec((1, IN_SHARD, D_MODEL), lambda d: (d, 0, 0))], out_specs=_const_spec((N_DZ, D_MODEL)),
        out_shape=_out_hbm((N_DZ, D_MODEL), BF), compiler_params=_params("arbitrary"),
    )(_in_hbm(slabs))


def _shard_d_w_in(d_cat):
    def body(cat_ref, slab_ref):
        d = pl.program_id(0)
        cat = cat_ref.bitcast(jnp.uint32)
        dst = slab_ref.bitcast(jnp.uint32)
        for dd in range(N_DEV):
            @pl.when(d == dd)
            def _():
                for a, n, at in _slab_pieces(dd):
                    dst[0, pl.ds(a, n), :] = cat[pl.ds(at, n), :]

    return pl.pallas_call(
        body, name="shard_d_w_in", grid=(N_DEV,), in_specs=[_const_spec((N_DZ, D_MODEL))],
        out_specs=pl.BlockSpec((1, IN_SHARD, D_MODEL), lambda d: (d, 0, 0)),
        out_shape=_out_hbm((N_DEV, IN_SHARD, D_MODEL), BF), compiler_params=_params("parallel"),
    )(_in_hbm(d_cat))


ANY = pl.BlockSpec(memory_space=pl.ANY)


def _place():
    x, y, c = lax.axis_index("x"), lax.axis_index("y"), lax.axis_index("c")
    other_chips = [(1 - x, y), (x, 1 - y), (1 - x, 1 - y)]
    return x, y, c, other_chips


SEM = pl.BlockSpec(memory_space=pltpu.SEMAPHORE)
IN_HBM = pl.BlockSpec(memory_space=pltpu.HBM)
SPLIT_PARAMS = pltpu.CompilerParams(has_side_effects=pltpu.SideEffectType.DATAFLOW_SIDE_EFFECTING)


def _gather_first(refs, send_sems, recv_sems):
    x, y, c, chips = _place()
    targets = [(x, y, 1 - c)] + [(px, py, c) for px, py in chips]
    copies = []
    for a, land in enumerate(refs):
        mine = land.at[4 * x + 2 * y + c]
        copies += [pltpu.make_async_remote_copy(src_ref=mine, dst_ref=mine, send_sem=send_sems.at[4 * a + k],
                                                recv_sem=recv_sems.at[4 * a + k], device_id=to, device_id_type=MESH)
                   for k, to in enumerate(targets)]
    return copies


def _gather_direct(refs, send_sems, recv_sems):
    x, y, c, _ = _place()
    flips = [(dx, dy, dc) for dx in (0, 1) for dy in (0, 1) for dc in (0, 1) if dx + dy + dc]
    targets = [(1 - x if dx else x, 1 - y if dy else y, 1 - c if dc else c) for dx, dy, dc in flips]
    return [pltpu.make_async_remote_copy(src_ref=refs[2 * a], dst_ref=refs[2 * a + 1].at[4 * x + 2 * y + c],
                                         send_sem=send_sems.at[7 * a + k], recv_sem=recv_sems.at[7 * a + k],
                                         device_id=to, device_id_type=MESH)
            for a in range(len(refs) // 2) for k, to in enumerate(targets)]


def _gather_second(refs, send_sems, recv_sems):
    x, y, c, chips = _place()
    copies = []
    for a, land in enumerate(refs):
        for j, (px, py) in enumerate(chips):
            block = land.at[4 * px + 2 * py + c]
            copies.append(pltpu.make_async_remote_copy(src_ref=block, dst_ref=block, send_sem=send_sems.at[3 * a + j],
                                                       recv_sem=recv_sems.at[3 * a + j], device_id=(x, y, 1 - c),
                                                       device_id_type=MESH))
    return copies


def _reduce_first(refs, send_sems, recv_sems):
    x, y, c, _ = _place()
    return [pltpu.make_async_remote_copy(src_ref=refs[2 * a].at[j, 1 - c], dst_ref=refs[2 * a + 1].at[j],
                                         send_sem=send_sems.at[4 * a + j], recv_sem=recv_sems.at[4 * a + j],
                                         device_id=(x, y, 1 - c), device_id_type=MESH)
            for a in range(len(refs) // 2) for j in range(4)]


def _reduce_second(refs, send_sems, recv_sems):
    _, _, c, chips = _place()
    return [pltpu.make_async_remote_copy(src_ref=refs[2 * a].at[2 * px + py], dst_ref=refs[2 * a + 1].at[k],
                                         send_sem=send_sems.at[3 * a + k], recv_sem=recv_sems.at[3 * a + k],
                                         device_id=(px, py, c), device_id_type=MESH)
            for a in range(len(refs) // 2) for k, (px, py) in enumerate(chips)]


def _split_start(name, groups):
    arrays = [a for g in groups for a in g[0]]
    n = len(arrays)

    def body(*refs):
        sems = refs[n:n + 2 * len(groups)]
        at = 0
        for gi, (members, _, build) in enumerate(groups):
            for cp in build(refs[at:at + len(members)], sems[2 * gi], sems[2 * gi + 1]):
                cp.start()
            at += len(members)
        refs[-1][...] = jnp.zeros_like(refs[-1])

    sem_shapes = [pltpu.SemaphoreType.DMA((g[1],)) for g in groups for _ in range(2)]
    outs = pl.pallas_call(
        body, name=name, in_specs=[IN_HBM] * n,
        out_shape=(*sem_shapes, *[_out_hbm(a.shape, a.dtype) for a in arrays], jax.ShapeDtypeStruct((8, 128), F32)),
        out_specs=(*[SEM] * len(sem_shapes), *[IN_HBM] * n, pl.BlockSpec(memory_space=pltpu.VMEM)),
        input_output_aliases={i: len(sem_shapes) + i for i in range(n)}, compiler_params=SPLIT_PARAMS,
    )(*[pltpu.with_memory_space_constraint(a, pltpu.HBM) for a in arrays])
    per_group, at = [], len(sem_shapes)
    for gi, (members, _, _) in enumerate(groups):
        per_group.append((outs[2 * gi], outs[2 * gi + 1], list(outs[at:at + len(members)])))
        at += len(members)
    return per_group, outs[-1]


def _split_wait(name, started, build, after):
    send_sems, recv_sems, arrays = started
    n = len(arrays)
    after = after if isinstance(after, (tuple, list)) else (after,)

    def body(*refs):
        for cp in build(refs[:n], refs[n], refs[n + 1]):
            cp.wait_send()
            cp.wait_recv()

    return pl.pallas_call(
        body, name=name, in_specs=[IN_HBM] * n + [SEM, SEM] + [ANY] * len(after),
        out_shape=tuple(_out_hbm(a.shape, a.dtype) for a in arrays), out_specs=tuple([IN_HBM] * n),
        input_output_aliases={i: i for i in range(n)}, compiler_params=SPLIT_PARAMS,
    )(*arrays, send_sems, recv_sems, *after)


def _placed_behind(token, arrays, name):
    n = len(arrays)

    def body(*refs):
        refs[-1][...] = jnp.zeros_like(refs[-1])

    outs = pl.pallas_call(
        body, name=name, in_specs=[IN_HBM] * n + [ANY],
        out_shape=(*[_out_hbm(a.shape, a.dtype) for a in arrays], jax.ShapeDtypeStruct((8, 128), F32)),
        out_specs=(*[IN_HBM] * n, pl.BlockSpec(memory_space=pltpu.VMEM)),
        input_output_aliases={i: i for i in range(n)},
    )(*map(_in_hbm, arrays), token)
    return outs[:n], outs[-1]


def _gather_landing(shard, me):
    return lax.dynamic_update_slice(lax.empty((N_DEV,) + shard.shape, shard.dtype), shard[None],
                                    (me,) + (0,) * shard.ndim)


ADAM_LANE_TILE = 256


def _tile_2d(rows, cols):
    for t in (256, 176, 128):
        if rows % t == 0:
            return t, cols
    return rows, ADAM_LANE_TILE


def _pair_sum(part, recv, core, name):
    _, rows, cols = recv.shape
    tr, tc = rows, cols

    def body(c_ref, p_ref, r_ref, o_ref):
        del c_ref
        o_ref[...] = (p_ref[...].astype(F32) + r_ref[...].astype(F32)).astype(BF)

    grid_spec = pltpu.PrefetchScalarGridSpec(
        num_scalar_prefetch=1, grid=(4, rows // tr, cols // tc),
        in_specs=[pl.BlockSpec((None, None, tr, tc), lambda j, i, k, c_ref: (j, c_ref[0], i, k)),
                  pl.BlockSpec((None, tr, tc), lambda j, i, k, c_ref: (j, i, k))],
        out_specs=pl.BlockSpec((None, tr, tc), lambda j, i, k, c_ref: (j, i, k)))
    return pl.pallas_call(
        body, name=name, grid_spec=grid_spec, out_shape=_out_hbm(recv.shape, BF),
        compiler_params=_params("parallel", "parallel", "parallel"),
    )(core, *map(_in_hbm, (part, recv)))


def _adamw(w, g, m, v):
    m = ADAM_B1 * m + (1.0 - ADAM_B1) * g
    v = ADAM_B2 * v + (1.0 - ADAM_B2) * (g * g)
    delta = -ADAM_LR * ((m / ADAM_C1) / (jnp.sqrt(v / ADAM_C2) + ADAM_EPS) + ADAM_WD * w)
    return delta, m, v


def _chip_sum_adamw(sums, recv, w, m, v, chip, name, lone_rows=False):
    rows, cols = w.shape
    tr, tc = _tile_2d(rows, cols)

    def body(chip_ref, s_ref, r_ref, w_ref, m_ref, v_ref, *out_refs):
        del chip_ref
        g = s_ref[...].astype(F32)
        for k in range(3):
            g = g + r_ref[k].astype(F32)
        for o_ref, t in zip(out_refs, (g,) + _adamw(w_ref[...], g, m_ref[...], v_ref[...])):
            o_ref[...] = t[:, None, :] if lone_rows else t

    tile = pl.BlockSpec((tr, tc), lambda i, k, chip_ref: (i, k))
    out_tile = pl.BlockSpec((tr, 1, tc), lambda i, k, chip_ref: (i, 0, k)) if lone_rows else tile
    grid_spec = pltpu.PrefetchScalarGridSpec(
        num_scalar_prefetch=1, grid=(rows // tr, cols // tc),
        in_specs=[pl.BlockSpec((None, tr, tc), lambda i, k, chip_ref: (chip_ref[0], i, k)),
                  pl.BlockSpec((3, tr, tc), lambda i, k, chip_ref: (0, i, k)), tile, tile, tile],
        out_specs=[out_tile] * 4)
    return pl.pallas_call(
        body, name=name, grid_spec=grid_spec,
        out_shape=[_out_hbm((rows, 1, cols) if lone_rows else (rows, cols), F32)] * 4,
        compiler_params=_params("parallel", "parallel"),
    )(chip, *map(_in_hbm, (sums, recv, w, m, v)))


def _small_sum_adamw(me, entries, loss):
    def whole(shape, squeeze=0, pick=None):
        blk = (None,) * squeeze + tuple(shape[squeeze:])
        if pick is not None:
            blk = tuple(shape[:pick]) + (None,) + tuple(shape[pick + 1:])
            return pl.BlockSpec(blk, lambda i, me_ref: (0,) * pick + (me_ref[0],) + (0,) * (len(shape) - pick - 1))
        return pl.BlockSpec(blk, lambda i, me_ref: (0,) * len(shape))

    in_specs, out_specs, out_shape, args = [], [], [], []
    for own, parts, w, m, v, sharded in entries + [loss + (None, None, None, False)]:
        in_specs += [whole(own.shape, pick=0 if sharded else None), whole(parts.shape, pick=1 if sharded else None)]
        args += [own, parts]
        if w is not None:
            lead = w.ndim - (parts.ndim - (2 if sharded else 1))
            in_specs += [whole(w.shape, squeeze=lead)] * 3
            out_specs += [whole(w.shape, squeeze=lead)] * 4
            out_shape += [_out_hbm(w.shape, F32)] * 4
            args += [w, m, v]
    out_specs.append(whole(loss[0].shape))
    out_shape.append(_out_hbm(loss[0].shape, F32))
    n = len(entries)

    def added(own_ref, p_ref, me):
        total = None
        for d in range(N_DEV):
            part = jnp.where(me == d, own_ref[...], p_ref[d])
            total = part if total is None else total + part
        return total

    def body(me_ref, *refs):
        ins, outs = refs[:5 * n + 2], refs[5 * n + 2:]
        for e in range(n):
            own_ref, p_ref, w_ref, m_ref, v_ref = ins[5 * e:5 * e + 5]
            g_out, d_out, m_out, v_out = outs[4 * e:4 * e + 4]
            g = added(own_ref, p_ref, me_ref[0])
            g_out[...] = g
            d_out[...], m_out[...], v_out[...] = _adamw(w_ref[...], g, m_ref[...], v_ref[...])
        outs[4 * n][...] = added(ins[5 * n], ins[5 * n + 1], me_ref[0])

    grid_spec = pltpu.PrefetchScalarGridSpec(num_scalar_prefetch=1, grid=(1,), in_specs=in_specs, out_specs=out_specs)
    outs = pl.pallas_call(body, name="small_sum_adamw", grid_spec=grid_spec, out_shape=out_shape,
                          compiler_params=_params("arbitrary"))(me, *map(_in_hbm, args))
    return [outs[4 * e:4 * e + 4] for e in range(n)], outs[4 * n]


MM_TILE = 512
N_MM_TILES = SEQ // MM_TILE
CAT_TILE = 512
N_CAT_TILES = N_CAT // CAT_TILE
DZ_TILE = 640


def kernel(x, g_mix, w_in, b_gate, w_gk_up, b_gk, w_pool_grp, pool_scale, g_gla_head, w_pool_proj, w_gla_proj, w_out, g_ffn, w_up, w_conv, b_conv, w_down, g_final, loss_target, m_g_mix, m_w_in, m_b_gate, m_w_gk_up, m_b_gk, m_w_pool_grp, m_pool_scale, m_g_gla_head, m_w_pool_proj, m_w_gla_proj, m_w_out, m_g_ffn, m_w_up, m_w_conv, m_b_conv, m_w_down, m_g_final, v_g_mix, v_w_in, v_b_gate, v_w_gk_up, v_b_gk, v_w_pool_grp, v_pool_scale, v_g_gla_head, v_w_pool_proj, v_w_gla_proj, v_w_out, v_g_ffn, v_w_up, v_w_conv, v_b_conv, v_w_down, v_g_final):
    xi, yi, ci = lax.axis_index("x"), lax.axis_index("y"), lax.axis_index("c")
    me = 4 * xi + 2 * yi + ci
    core = jnp.reshape(ci, (1,)).astype(jnp.int32)
    chip = jnp.reshape(2 * xi + yi, (1,)).astype(jnp.int32)
    xs, target = x[0], loss_target[0]

    big = dict(w_in=w_in[0].T, w_pool_proj=w_pool_proj[0], w_gla_proj=w_gla_proj[0], w_out=w_out[0], w_up=w_up[0].T,
               w_down=w_down[0])
    moments = dict(w_in=(m_w_in[0].T, v_w_in[0].T), w_pool_proj=(m_w_pool_proj[0], v_w_pool_proj[0]),
                   w_gla_proj=(m_w_gla_proj[0], v_w_gla_proj[0]), w_out=(m_w_out[0], v_w_out[0]),
                   w_up=(m_w_up[0].T, v_w_up[0].T), w_down=(m_w_down[0], v_w_down[0]))
    names = list(big)
    shards = {k: big[k].astype(BF) for k in names}
    shards["w_gk_up"], shards["w_conv"] = w_gk_up[0], w_conv[0]
    gather_groups = (("w_in", "w_gk_up"), ("w_pool_proj", "w_gla_proj", "w_out"), ("w_up", "w_down", "w_conv"))
    started, token = _split_start("gather_start", [
        ([_gather_landing(shards[k], me) for k in g], 4 * len(g), _gather_first) for g in gather_groups])
    (big["w_in"], *moments["w_in"], bconv4, m_w_conv, v_w_conv), token = _placed_behind(
        token, [big["w_in"], *moments["w_in"], b_conv.reshape(2, 4, 1, FF_BLK), m_w_conv, v_w_conv],
        "place_adamw_operands")

    def gather_pass(gi, after):
        lands = list(_split_wait(f"gather_wait_{gi}", started[gi], _gather_first, after))
        passed, tkn = _split_start(f"gather_pass_{gi}", [(lands, 3 * len(lands), _gather_second)])
        return passed[0], tkn

    def gather_done(gi, passed, after):
        return dict(zip(gather_groups[gi], _split_wait(f"gather_pass_wait_{gi}", passed, _gather_second, after)))

    tok = lambda i, j, k: (i, 0)
    whole = lambda i, j, k: (0, 0)
    kblk = lambda i, j, k: (k, 0)
    ff_seq = (None, None, SEQ, FF_BLK)

    h = _rms_fwd(xs, g_mix, token, "rms_mix")
    wg = gather_done(0, gather_pass(0, h)[0], h)
    wt_cat = _unshard_w_in(wg["w_in"])
    wgk_pad = jnp.pad(wg["w_gk_up"].transpose(1, 0, 2).reshape(GATE_RANK, GLA_DK), ((0, GK_PAD - GATE_RANK), (0, 0)))
    zcat = _mm(h, wt_cat, out_shape=(SEQ, N_CAT), out_dtype=BF, grid=(N_CAT_TILES, 1, 1),
               blk_a=(SEQ, D_MODEL), blk_b=(CAT_TILE, D_MODEL), blk_o=(SEQ, CAT_TILE),
               map_a=whole, map_b=lambda j, i, k: (j, 0), map_o=lambda j, i, k: (0, j), tb=True, name="mm_in")
    la = _gk_fwd(h, wt_cat, wgk_pad, b_gk)
    passed, tkn = gather_pass(1, la)
    o, states = _gla_fwd(zcat, la, tkn)
    wg = gather_done(1, passed, o)
    wpp = wg["w_pool_proj"].transpose(1, 0, 2).reshape(POOL_WIDTH, D_MODEL)
    wgp = wg["w_gla_proj"].reshape(D_MODEL, D_MODEL)
    wout = wg["w_out"].reshape(D_MODEL, D_MODEL)
    og = _post_gla_fwd(o, zcat, g_gla_head)
    ps = _pool_fwd(zcat, w_pool_grp[0], pool_scale)
    passed, tkn = gather_pass(2, (og, ps, wpp))
    y_pool, y_gla, mixed, x1, h2 = _mix_out_fwd(ps, og, zcat, xs, wpp, wgp, wout, b_gate, g_ffn, tkn)
    wg = gather_done(2, passed, h2)
    wt_up = wg["w_up"].reshape(2 * D_FF, D_MODEL)
    wdown = wg["w_down"].reshape(D_FF, D_MODEL)
    wconv4 = wg["w_conv"].reshape(2, 4, 3, FF_BLK)
    blk4 = lambda b, i, k: (b // 4, b % 4, 0, 0)
    u4, act = _up_conv_fwd(h2, wt_up, wconv4, bconv4)
    loss_part, dx2, dx2_bf, dg_final = _mm_tokens(
        act, wdown, blk_a=(None, 4, TOK_MM_TILE, FF_BLK), map_a=lambda i: (0, 0, i, 0),
        pieces=[(b, b * FF_BLK, FF_BLK) for b in range(4)], res=x1, then=("loss", g_final.reshape(1, D_MODEL), target),
        name="mm_down_loss")

    da = _mm(dx2_bf, wdown, out_shape=(1, 4, SEQ, FF_BLK), out_dtype=BF, grid=(4, 1, 1),
             blk_a=(SEQ, D_MODEL), blk_b=(FF_BLK, D_MODEL), blk_o=ff_seq,
             map_a=whole, map_b=lambda b, i, k: (b, 0), map_o=lambda b, i, k: (0, b, 0, 0), tb=True, name="mm_d_act")
    d_wdown = _mm(act, dx2_bf, out_shape=(D_FF, D_MODEL), out_dtype=BF, grid=(4, 1, 1),
                  blk_a=ff_seq, blk_b=(SEQ, D_MODEL), blk_o=(FF_BLK, D_MODEL),
                  map_a=lambda b, i, k: (0, b, 0, 0), map_b=whole, map_o=lambda b, i, k: (b, 0), ta=True,
                  name="mm_d_wdown")
    du4, d_wconv, d_bconv = _conv_bwd(u4, da, wconv4, bconv4)
    d_wt_up = _mm(du4, h2, out_shape=(2 * D_FF, D_MODEL), out_dtype=BF, grid=(N_DEV, 1, 1),
                  blk_a=ff_seq, blk_b=(SEQ, D_MODEL), blk_o=(FF_BLK, D_MODEL),
                  map_a=blk4, map_b=whole, map_o=lambda b, i, k: (b, 0), ta=True, name="mm_d_wup")
    res = {}

    def to_sibling(keys, parts):
        return [t for k in keys for t in (parts[k], lax.empty((4,) + parts[k].shape[2:], BF))], 4 * len(keys), _reduce_first

    def to_chips(keys, st, after):
        arrays = _split_wait("reduce_wait_" + keys[0], st, _reduce_first, after)
        sums = [_pair_sum(p, r, core, "pair_sum_" + k) for k, p, r in zip(keys, arrays[0::2], arrays[1::2])]
        return [t for s in sums for t in (s, lax.empty((3,) + s.shape[1:], BF))], 3 * len(keys), _reduce_second

    def reduce_start(keys, parts):
        st, tkn = _split_start("reduce_start_" + keys[0], [to_sibling(keys, parts)])
        return st[0], tkn

    def reduce_cross(keys, st, after):
        st2, tkn = _split_start("reduce_cross_" + keys[0], [to_chips(keys, st, after)])
        return st2[0], tkn

    def reduce_done(keys, st2, after):
        arrays = _split_wait("reduce_cross_wait_" + keys[0], st2, _reduce_second, after)
        for k, s, r in zip(keys, arrays[0::2], arrays[1::2]):
            outs = _chip_sum_adamw(s, r, big[k], moments[k][0], moments[k][1], chip, "adamw_" + k,
                                   lone_rows=k == "w_in")
            res[k] = [jnp.transpose(t, (1, 2, 0)) if k == "w_in" else (t.T if k == "w_up" else t)[None] for t in outs]

    ffn_keys = ("w_down", "w_up")
    ffn_red, tkn = reduce_start(ffn_keys, dict(w_down=d_wdown.reshape(4, 2, D_FF // N_DEV, D_MODEL),
                                               w_up=d_wt_up.reshape(4, 2, FF_BLK, D_MODEL)))
    dx1, dg_ffn = _mm_tokens(
        du4, wt_up, blk_a=(2, 4, TOK_MM_TILE, FF_BLK), map_a=lambda i: (0, 0, i, 0),
        pieces=[((b // 4, b % 4), b * FF_BLK, FF_BLK) for b in range(N_DEV)], after=tkn, then=("rms_bwd", x1, g_ffn, dx2),
        name="mm_d_h2_rms")

    sq_t = dict(out_shape=(D_MODEL, D_MODEL), grid=(1, 1, N_MM_TILES), blk_a=(MM_TILE, D_MODEL),
                blk_b=(MM_TILE, D_MODEL), blk_o=(D_MODEL, D_MODEL), map_a=kblk, map_b=kblk, map_o=whole, ta=True)
    d_wout = _mm(mixed, dx1, out_dtype=BF, name="mm_d_wout", **sq_t)
    dzcat, dy_pool, dy_gla, db_gate = _mix_bwd(dx1, wout, zcat, b_gate, y_pool, y_gla)
    d_wgp = _mm(og, dy_gla, out_dtype=BF, name="mm_d_wgp", **sq_t)
    mix_keys = ("w_out", "w_gla_proj")
    (ffn_red, mix_red), tkn = _split_start("reduce_cross_w_down", [
        to_chips(ffn_keys, ffn_red, db_gate),
        to_sibling(mix_keys, dict(w_out=d_wout.reshape(4, 2, D_MODEL // N_DEV, D_MODEL),
                                  w_gla_proj=d_wgp.reshape(4, 2, D_MODEL // N_DEV, D_MODEL)))])
    dzcat, d_o, dg_head = _post_gla_bwd(dzcat, dy_gla, wgp, o, zcat, g_gla_head, tkn)
    dzcat, dla = _gla_bwd(dzcat, zcat, la, d_o, states)
    dzcat, d_wgk, db_gk = _gk_bwd(dzcat, dla, h, wt_cat, wgk_pad, b_gk)
    dps = _mm(dy_pool, wpp, out_shape=(SEQ, POOL_WIDTH), out_dtype=F32, grid=(N_MM_TILES, 1, 1),
              blk_a=(MM_TILE, D_MODEL), blk_b=(POOL_WIDTH, D_MODEL), blk_o=(MM_TILE, POOL_WIDTH),
              map_a=tok, map_b=whole, map_o=tok, tb=True, name="mm_d_ps")
    d_wpp = _mm(ps, dy_pool, out_shape=(POOL_WIDTH, D_MODEL), out_dtype=F32, grid=(1, 1, N_MM_TILES),
                blk_a=(MM_TILE, POOL_WIDTH), blk_b=(MM_TILE, D_MODEL), blk_o=(POOL_WIDTH, D_MODEL),
                map_a=kblk, map_b=kblk, map_o=whole, ta=True, name="mm_d_wpp")
    dzcat, d_wgrp, d_scale = _pool_bwd(dzcat, zcat, dps, w_pool_grp[0], pool_scale)
    row = lambda t: t.reshape(1, D_MODEL)
    small = [("b_gate", db_gate, b_gate, m_b_gate, v_b_gate, False),
             ("w_gk_up", d_wgk.reshape(GATE_RANK, N_DEV, GLA_DK // N_DEV).transpose(1, 0, 2), w_gk_up, m_w_gk_up,
              v_w_gk_up, True),
             ("b_gk", db_gk, b_gk, m_b_gk, v_b_gk, False),
             ("w_pool_grp", d_wgrp, w_pool_grp, m_w_pool_grp, v_w_pool_grp, False),
             ("pool_scale", d_scale, pool_scale, m_pool_scale, v_pool_scale, False),
             ("g_gla_head", dg_head, g_gla_head, m_g_gla_head, v_g_gla_head, False),
             ("g_ffn", dg_ffn, g_ffn, m_g_ffn, v_g_ffn, False),
             ("w_conv", d_wconv.reshape(N_DEV, 3, FF_BLK), w_conv, m_w_conv, v_w_conv, True),
             ("b_conv", d_bconv.reshape(b_conv.shape), b_conv, m_b_conv, v_b_conv, False),
             ("g_final", dg_final, row(g_final), row(m_g_final), row(v_g_final), False)]

    def to_all(parts):
        return ([t for p in parts for t in (p, lax.empty((N_DEV,) + p.shape, p.dtype))], 7 * len(parts),
                _gather_direct)

    (small_sent, mix_red), tkn = _split_start("small_start", [to_all([t[1] for t in small] + [loss_part]),
                                                              to_chips(mix_keys, mix_red, dla)])
    d_wt_cat = _mm(dzcat, h, out_shape=(N_DZ, D_MODEL), out_dtype=BF, grid=(N_DZ // DZ_TILE, 1, 1),
                   blk_a=(SEQ, DZ_TILE), blk_b=(SEQ, D_MODEL), blk_o=(DZ_TILE, D_MODEL),
                   map_a=lambda j, i, k: (0, j), map_b=whole, map_o=lambda j, i, k: (j, 0), ta=True, after=tkn,
                   name="mm_d_wcat")
    in_keys = ("w_in", "w_pool_proj")
    in_red, tkn = reduce_start(in_keys, dict(
        w_in=_shard_d_w_in(d_wt_cat).reshape(4, 2, IN_SHARD, D_MODEL),
        w_pool_proj=d_wpp.reshape(POOL_WIDTH, N_DEV, D_MODEL // N_DEV).transpose(1, 0, 2).astype(BF)
        .reshape(4, 2, POOL_WIDTH, D_MODEL // N_DEV)))
    reduce_done(mix_keys, mix_red, tkn)
    in_red, tkn = reduce_cross(in_keys, in_red, res["w_out"][0])
    grad_x, dg_mix = _mm_tokens(dzcat, wt_cat, blk_a=(TOK_MM_TILE, N_DZ), map_a=lambda i: (i, 0),
                                pieces=[(None, 0, N_DZ)], after=tkn, then=("rms_bwd", xs, g_mix, dx1),
                                name="mm_d_h_rms")
    (g_mix_sent,), tkn = _split_start("g_mix_start", [to_all([dg_mix])])
    reduce_done(ffn_keys, ffn_red, (grad_x, tkn))
    sent = list(_split_wait("small_wait", small_sent, _gather_direct, res["w_down"][0]))
    small.append(("g_mix", dg_mix, g_mix, m_g_mix, v_g_mix, False))
    sent[-2:-2] = _split_wait("g_mix_wait", g_mix_sent, _gather_direct, sent[1])
    own, gathered = sent[0::2], sent[1::2]
    small_out, loss_sum = _small_sum_adamw(jnp.reshape(me, (1,)).astype(jnp.int32),
                                           [(o, p) + t[2:] for o, p, t in zip(own, gathered, small)],
                                           (own[-1], gathered[-1]))
    for t, outs in zip(small, small_out):
        res[t[0]] = list(outs)
    res["g_final"] = [t.reshape(g_final.shape) for t in res["g_final"]]

    reduce_done(in_keys, in_red, loss_sum)
    loss = loss_sum[0, 0]
    order =["g_mix", "w_in", "b_gate", "w_gk_up", "b_gk", "w_pool_grp", "pool_scale", "g_gla_head", "w_pool_proj",
             "w_gla_proj", "w_out", "g_ffn", "w_up", "w_conv", "b_conv", "w_down", "g_final"]
    return (loss, grad_x[None], *[res[k][0] for k in order], *[res[k][1] for k in order],
            *[res[k][2] for k in order], *[res[k][3] for k in order])
```

```python
import jax
import jax.numpy as jnp
from jax import lax
from jax.experimental import pallas as pl
from jax.experimental.pallas import tpu as pltpu

F32 = jnp.float32
BF = jnp.bfloat16
HIGHEST = lax.Precision.HIGHEST
MESH = pl.DeviceIdType.MESH

N_DEV = 8
SEQ = 2048
D_MODEL = 1024
CHUNK = 64
EPS = 1e-6
POOL_WIDTH = 512
POOL_WINDOWS = (2, 4, 8, 16)
POOL_GD = 128
POOL_HALO = 16
HEADS = 4
HK = 128
HV = 256
GLA_DK = 512
GATE_RANK = 16
GATE_NORM = 16.0
D_FF = 2816
FF_BLK = 704
IN_SHARD = 706
C_QKV, C_GATE, C_OG, C_POOL, C_GK = 0, 2048, 4096, 5120, 5632
N_CAT = 5632
GK_PAD = 128
N_DZ = N_CAT + GK_PAD
R_POOL, R_QKV, R_OG, R_GK, R_GATE = 0, 512, 2560, 3584, 3600

ADAM_LR, ADAM_B1, ADAM_B2, ADAM_EPS, ADAM_WD, ADAM_STEP = 0.001, 0.9, 0.999, 1e-08, 0.01, 10
ADAM_C1 = 1.0 - ADAM_B1 ** ADAM_STEP
ADAM_C2 = 1.0 - ADAM_B2 ** ADAM_STEP

VMEM_BYTES_V7X = 64 * 1024 * 1024
VMEM_LIMIT = VMEM_BYTES_V7X * 3 // 4

TOK_TILE = 256
HALO = 8
GLA_CPS = 4


def _params(*sem):
    return pltpu.CompilerParams(dimension_semantics=sem, vmem_limit_bytes=VMEM_LIMIT)


def _const_spec(shape):
    nd = len(shape)
    return pl.BlockSpec(shape, lambda *_: (0,) * nd)


def _in_hbm(t):
    return pltpu.with_memory_space_constraint(t, pltpu.HBM)


def _out_hbm(shape, dtype):
    return pltpu.HBM(shape, dtype)


def _dot(a, b, ta=False, tb=False):
    dims = (((0 if ta else 1,), (1 if tb else 0,)), ((), ()))
    return lax.dot_general(a.astype(BF), b.astype(BF), dims, preferred_element_type=F32)


def _dot_exact(a, b):
    return jnp.dot(a, b, precision=HIGHEST, preferred_element_type=F32)


def _sigmoid(x):
    return 0.5 * jnp.tanh(0.5 * x) + 0.5


def _mm(a, b, *, out_shape, out_dtype, grid, blk_a, blk_b, blk_o, map_a, map_b, map_o, ta=False, tb=False,
        after=None, name):
    gk = grid[2]
    n_in = 2 + (after is not None)

    def body(*refs):
        a_ref, b_ref, o_ref = refs[0], refs[1], refs[n_in]
        prod = _dot(a_ref[...], b_ref[...], ta, tb)
        if gk == 1:
            o_ref[...] = prod.astype(out_dtype)
        else:
            acc = refs[n_in + 1]
            k = pl.program_id(2)

            @pl.when(k == 0)
            def _():
                acc[...] = prod

            @pl.when(k > 0)
            def _():
                acc[...] += prod

            @pl.when(k == gk - 1)
            def _():
                o_ref[...] = acc[...].astype(out_dtype)

    in_specs = [pl.BlockSpec(blk_a, map_a), pl.BlockSpec(blk_b, map_b)]
    args = [_in_hbm(a), _in_hbm(b)]
    if after is not None:
        in_specs.append(pl.BlockSpec(memory_space=pl.ANY))
        args.append(after)
    return pl.pallas_call(
        body, name=name, grid=grid, in_specs=in_specs, out_specs=pl.BlockSpec(blk_o, map_o),
        out_shape=_out_hbm(out_shape, out_dtype),
        scratch_shapes=[] if gk == 1 else [pltpu.VMEM(tuple(d for d in blk_o if d is not None), F32)],
        compiler_params=_params("parallel", "parallel", "arbitrary"),
    )(*args)


TOK_MM_TILE = 256


def _mm_tokens(a, w, *, blk_a, map_a, pieces, res=None, after=None, then=None, name):
    n_in = 2 + (res is not None) + (after is not None) + (0 if then is None else len(then) - 1)

    def accumulate(ref, part):
        @pl.when(pl.program_id(0) == 0)
        def _():
            ref[...] = part

        @pl.when(pl.program_id(0) > 0)
        def _():
            ref[...] += part

    def body(*refs):
        a_ref, w_ref = refs[:2]
        extra, outs = refs[n_in - (0 if then is None else len(then) - 1):n_in], refs[n_in:]
        total = None
        for idx, row, n in pieces:
            av = a_ref[...] if idx is None else a_ref[idx]
            prod = _dot(av, w_ref[row:row + n, :])
            total = prod if total is None else total + prod
        if res is not None:
            total = total + refs[2][...]
        if then is None:
            outs[0][...] = total
        elif then[0] == "rms_bwd":
            dx, part = _rms_bwd_tile(total, extra[0][...], extra[1][...], extra[2][...])
            outs[0][...] = dx
            accumulate(outs[1], part)
        else:
            lpart, dx, part = _loss_tile(total, extra[0][...], extra[1][...])
            outs[1][...] = dx
            outs[2][...] = dx.astype(BF)
            accumulate(outs[0], lpart)
            accumulate(outs[3], part)

    tile = pl.BlockSpec((TOK_MM_TILE, D_MODEL), lambda i: (i, 0))
    vec = _const_spec((1, D_MODEL))
    big = _out_hbm((SEQ, D_MODEL), F32)
    small = _out_hbm((1, D_MODEL), F32)
    in_specs = [pl.BlockSpec(blk_a, map_a), pl.BlockSpec(w.shape, lambda i: (0, 0), pipeline_mode=pl.Buffered(1))]
    args = [a, w]
    if res is not None:
        in_specs.append(tile)
        args.append(res)
    if after is not None:
        in_specs.append(pl.BlockSpec(memory_space=pl.ANY))
        args.append(after)
    if then is None:
        out_specs, out_shape = tile, big
    elif then[0] == "rms_bwd":
        in_specs += [tile, vec, tile]
        out_specs, out_shape = [tile, vec], [big, small]
    else:
        in_specs += [vec, tile]
        out_specs = [_const_spec((1, 128)), tile, tile, vec]
        out_shape = [_out_hbm((1, 128), F32), big, _out_hbm((SEQ, D_MODEL), BF), small]
    if then is not None:
        args += list(then[1:])
    return pl.pallas_call(
        body, name=name, grid=(SEQ // TOK_MM_TILE,), in_specs=in_specs, out_specs=out_specs, out_shape=out_shape,
        compiler_params=_params("parallel" if then is None else "arbitrary"),
    )(*[_in_hbm(t) for t in args])


def _rms_fwd(x, g, after, name):
    def body(x_ref, g_ref, after_ref, o_ref):
        del after_ref
        xv = x_ref[...]
        r = lax.rsqrt(jnp.mean(xv * xv, axis=-1, keepdims=True) + EPS)
        o_ref[...] = (xv * r * g_ref[...]).astype(BF)

    tile = pl.BlockSpec((TOK_TILE, D_MODEL), lambda i: (i, 0))
    return pl.pallas_call(
        body, name=name, grid=(SEQ // TOK_TILE,),
        in_specs=[tile, _const_spec((1, D_MODEL)), pl.BlockSpec(memory_space=pl.ANY)], out_specs=tile,
        out_shape=_out_hbm((SEQ, D_MODEL), BF), compiler_params=_params("parallel"),
    )(*map(_in_hbm, (x, g)), after)


def _rms_bwd_tile(dyv, xv, gv, dresv):
    r = lax.rsqrt(jnp.mean(xv * xv, axis=-1, keepdims=True) + EPS)
    xn = xv * r
    dxn = dyv * gv
    return dresv + r * (dxn - xn * jnp.mean(dxn * xn, axis=-1, keepdims=True)), jnp.sum(dyv * xn, axis=0, keepdims=True)


def _loss_tile(xv, gv, tv):
    r = lax.rsqrt(jnp.mean(xv * xv, axis=-1, keepdims=True) + EPS)
    xn = xv * r
    err = xn * gv - tv
    lpart = jnp.full((1, 128), 0.5 * jnp.sum(jnp.mean(err * err, axis=-1, keepdims=True)), F32)
    dyv = err * (1.0 / D_MODEL)
    dxn = dyv * gv
    return lpart, r * (dxn - xn * jnp.mean(dxn * xn, axis=-1, keepdims=True)), jnp.sum(dyv * xn, axis=0, keepdims=True)


def _pool_counts(w):
    pos = lax.broadcasted_iota(jnp.int32, (SEQ, 1), 0).astype(F32)
    return jnp.minimum(pos + 1.0, float(w))


def _pool_window(u, w, ext):
    ext[pl.ds(POOL_HALO, SEQ), :] = u
    win = u
    for j in range(1, w):
        win = win + ext[pl.ds(POOL_HALO - j, SEQ), :]
    return win / _pool_counts(w) - u


def _pool_fwd(zcat, w_grp, scale):
    def body(z_ref, w_ref, s_ref, o_ref, ext):
        ext[pl.ds(0, POOL_HALO), :] = jnp.zeros((POOL_HALO, POOL_GD), F32)
        for g, w in enumerate(POOL_WINDOWS):
            cols = slice(g * POOL_GD, (g + 1) * POOL_GD)
            p = _pool_window(z_ref[:, cols].astype(F32), w, ext)
            o_ref[:, cols] = (_dot(p, w_ref[g]) * s_ref[:, cols]).astype(BF)

    return pl.pallas_call(
        body, name="pool_fwd", grid=(1,),
        in_specs=[pl.BlockSpec((SEQ, POOL_WIDTH), lambda i: (0, C_POOL // POOL_WIDTH)),
                  _const_spec((4, POOL_GD, POOL_GD)), _const_spec((1, POOL_WIDTH))],
        out_specs=_const_spec((SEQ, POOL_WIDTH)), out_shape=_out_hbm((SEQ, POOL_WIDTH), BF),
        scratch_shapes=[pltpu.VMEM((POOL_HALO + SEQ, POOL_GD), F32)], compiler_params=_params("arbitrary"),
    )(*map(_in_hbm, (zcat, w_grp, scale)))


def _pool_bwd(dzcat, zcat, dps, w_grp, scale):
    def body(dz_in, z_ref, dps_ref, w_ref, s_ref, dz_ref, dw_ref, dsc_ref, ext, ext2):
        del dz_in
        ext[pl.ds(0, POOL_HALO), :] = jnp.zeros((POOL_HALO, POOL_GD), F32)
        ext2[pl.ds(SEQ, POOL_HALO), :] = jnp.zeros((POOL_HALO, POOL_GD), F32)
        for g, w in enumerate(POOL_WINDOWS):
            cols = slice(g * POOL_GD, (g + 1) * POOL_GD)
            p = _pool_window(z_ref[:, cols].astype(F32), w, ext)
            wg = w_ref[g]
            pg = _dot(p, wg)
            dpsv = dps_ref[:, cols]
            dsc_ref[:, cols] = jnp.sum(dpsv * pg, axis=0, keepdims=True)
            dpg = dpsv * s_ref[:, cols]
            dw_ref[g] = _dot(p, dpg, ta=True)
            dp = _dot(dpg, wg, tb=True)
            dpc = dp / _pool_counts(w)
            ext2[pl.ds(0, SEQ), :] = dpc
            du = dpc
            for j in range(1, w):
                du = du + ext2[pl.ds(j, SEQ), :]
            dz_ref[:, cols] = (du - dp).astype(BF)

    return pl.pallas_call(
        body, name="pool_bwd", grid=(1,),
        in_specs=[pl.BlockSpec(memory_space=pl.ANY),
                  pl.BlockSpec((SEQ, POOL_WIDTH), lambda i: (0, C_POOL // POOL_WIDTH)),
                  _const_spec((SEQ, POOL_WIDTH)), _const_spec((4, POOL_GD, POOL_GD)), _const_spec((1, POOL_WIDTH))],
        out_specs=[pl.BlockSpec((SEQ, POOL_WIDTH), lambda i: (0, C_POOL // POOL_WIDTH)),
                   _const_spec((4, POOL_GD, POOL_GD)), _const_spec((1, POOL_WIDTH))],
        out_shape=[_out_hbm((SEQ, N_DZ), BF), _out_hbm((4, POOL_GD, POOL_GD), F32),
                   _out_hbm((1, POOL_WIDTH), F32)],
        scratch_shapes=[pltpu.VMEM((POOL_HALO + SEQ, POOL_GD), F32), pltpu.VMEM((SEQ + POOL_HALO, POOL_GD), F32)],
        input_output_aliases={0: 0}, compiler_params=_params("arbitrary"),
    )(*map(_in_hbm, (dzcat, zcat, dps, w_grp, scale)))


GK_TILE = 512


GK_ROWS = pl.BlockSpec((GK_PAD, D_MODEL), lambda i: (C_GK // GK_PAD, 0))


def _gk_fwd(h, wt_cat, wgk_pad, b_gk):
    def body(h_ref, wt_ref, w_ref, b_ref, la_ref):
        z_gk = _dot(h_ref[...], wt_ref[...], tb=True)
        pre = _dot(z_gk, w_ref[...]) + b_ref[...]
        la_ref[...] = (jnp.minimum(pre, 0.0) - jnp.log(1.0 + jnp.exp(-jnp.abs(pre)))) * (1.0 / GATE_NORM)

    return pl.pallas_call(
        body, name="gk_fwd", grid=(SEQ // GK_TILE,),
        in_specs=[pl.BlockSpec((GK_TILE, D_MODEL), lambda i: (i, 0)), GK_ROWS,
                  _const_spec((GK_PAD, GLA_DK)), _const_spec((1, GLA_DK))],
        out_specs=pl.BlockSpec((GK_TILE, GLA_DK), lambda i: (i, 0)),
        out_shape=_out_hbm((SEQ, GLA_DK), F32), compiler_params=_params("parallel"),
    )(*map(_in_hbm, (h, wt_cat, wgk_pad, b_gk)))


def _gk_bwd(dzcat, dla, h, wt_cat, wgk_pad, b_gk):
    def body(dz_in, dla_ref, h_ref, wt_ref, w_ref, b_ref, dz_ref, dw_ref, db_ref):
        del dz_in
        wv = w_ref[...]
        z_gk = _dot(h_ref[...], wt_ref[...], tb=True)
        pre = _dot(z_gk, wv) + b_ref[...]
        dpre = dla_ref[...] * (1.0 / GATE_NORM) * (1.0 - _sigmoid(pre))
        dz_ref[...] = _dot(dpre, wv, tb=True).astype(BF)
        dwp = _dot(z_gk, dpre, ta=True)[:GATE_RANK]
        dbp = jnp.sum(dpre, axis=0, keepdims=True)

        @pl.when(pl.program_id(0) == 0)
        def _():
            dw_ref[...] = dwp
            db_ref[...] = dbp

        @pl.when(pl.program_id(0) > 0)
        def _():
            dw_ref[...] += dwp
            db_ref[...] += dbp

    return pl.pallas_call(
        body, name="gk_bwd", grid=(SEQ // GK_TILE,),
        in_specs=[pl.BlockSpec(memory_space=pl.ANY), pl.BlockSpec((GK_TILE, GLA_DK), lambda i: (i, 0)),
                  pl.BlockSpec((GK_TILE, D_MODEL), lambda i: (i, 0)), GK_ROWS, _const_spec((GK_PAD, GLA_DK)),
                  _const_spec((1, GLA_DK))],
        out_specs=[pl.BlockSpec((GK_TILE, GK_PAD), lambda i: (i, C_GK // GK_PAD)), _const_spec((GATE_RANK, GLA_DK)),
                   _const_spec((1, GLA_DK))],
        out_shape=[_out_hbm((SEQ, N_DZ), BF), _out_hbm((GATE_RANK, GLA_DK), F32),
                   _out_hbm((1, GLA_DK), F32)],
        input_output_aliases={0: 0}, compiler_params=_params("arbitrary"),
    )(*map(_in_hbm, (dzcat, dla, h, wt_cat, wgk_pad, b_gk)))


GLA_ROWS = GLA_CPS * CHUNK
GLA_STEPS = SEQ // GLA_ROWS
QKV_W = 2048


def _tri():
    return lax.broadcasted_iota(jnp.int32, (CHUNK, CHUNK), 0) >= lax.broadcasted_iota(jnp.int32, (CHUNK, CHUNK), 1)


def _chunk_cumsum(la_ref, rows):
    return _dot_exact(_tri().astype(F32), la_ref[rows, :])


def _gla_chunk(qkv_ref, la_ref, rows, h, bc_all):
    tri = _tri()
    q = qkv_ref[rows, h * HK:(h + 1) * HK].astype(F32) * (HK ** -0.5)
    k = qkv_ref[rows, GLA_DK + h * HK:GLA_DK + (h + 1) * HK].astype(F32)
    v = qkv_ref[rows, 2 * GLA_DK + h * HV:2 * GLA_DK + (h + 1) * HV].astype(BF)
    la = la_ref[rows, h * HK:(h + 1) * HK]
    bc = bc_all[:, h * HK:(h + 1) * HK]
    e_pos, e_neg = jnp.exp(bc), jnp.exp(-bc)
    dl = jnp.exp(jnp.sum(la, axis=0, keepdims=True))
    q_fw, q_bw, k_fw, k_bw = q * e_pos, q * e_neg, k * e_neg, k * e_pos
    scores = jnp.where(tri, _dot(q_fw, k_fw, tb=True), _dot(q_bw, k_bw, tb=True))
    return tri, v, e_pos, e_neg, dl, q_fw, q_bw, k_fw, k_bw, scores


def _gla_fwd(zcat, la, after):
    def body(qkv_ref, la_ref, after_ref, o_ref, st_ref, state):
        del after_ref

        @pl.when(pl.program_id(0) == 0)
        def _():
            state[...] = jnp.zeros_like(state)

        for c in range(GLA_CPS):
            rows = slice(c * CHUNK, (c + 1) * CHUNK)
            bc_all = _chunk_cumsum(la_ref, rows)
            for h in range(HEADS):
                _, v, _, _, dl, q_fw, _, k_fw, _, scores = _gla_chunk(qkv_ref, la_ref, rows, h, bc_all)
                st = state[h]
                st_ref[c, h] = st
                o_ref[rows, h * HV:(h + 1) * HV] = _dot(scores, v) + _dot(q_fw, st, tb=True)
                state[h] = st * dl + _dot(v, k_fw * dl, ta=True)

    return pl.pallas_call(
        body, name="gla_fwd", grid=(GLA_STEPS,),
        in_specs=[pl.BlockSpec((GLA_ROWS, QKV_W), lambda i: (i, 0)), pl.BlockSpec((GLA_ROWS, GLA_DK), lambda i: (i, 0)),
                  pl.BlockSpec(memory_space=pl.ANY)],
        out_specs=[pl.BlockSpec((GLA_ROWS, D_MODEL), lambda i: (i, 0)),
                   pl.BlockSpec((GLA_CPS, HEADS, HV, HK), lambda i: (i, 0, 0, 0))],
        out_shape=[_out_hbm((SEQ, D_MODEL), F32),
                   _out_hbm((SEQ // CHUNK, HEADS, HV, HK), F32)],
        scratch_shapes=[pltpu.VMEM((HEADS, HV, HK), F32)], compiler_params=_params("arbitrary"),
    )(*map(_in_hbm, (zcat, la)), after)


def _gla_bwd(dzcat, zcat, la, d_o, states):
    def body(dz_in, qkv_ref, la_ref, do_ref, st_ref, dqkv_ref, dla_ref, dstate):
        del dz_in

        @pl.when(pl.program_id(0) == 0)
        def _():
            dstate[...] = jnp.zeros_like(dstate)

        last_row = lax.broadcasted_iota(jnp.int32, (CHUNK, HK), 0) == CHUNK - 1
        upper = (lax.broadcasted_iota(jnp.int32, (CHUNK, CHUNK), 0)
                 <= lax.broadcasted_iota(jnp.int32, (CHUNK, CHUNK), 1)).astype(F32)
        for c in reversed(range(GLA_CPS)):
            rows = slice(c * CHUNK, (c + 1) * CHUNK)
            bc_all = _chunk_cumsum(la_ref, rows)
            dbs = []
            for h in range(HEADS):
                tri, v, e_pos, e_neg, dl, q_fw, q_bw, k_fw, k_bw, scores = _gla_chunk(qkv_ref, la_ref, rows, h, bc_all)
                st = st_ref[c, h]
                dst = dstate[h]
                d_out = do_ref[rows, h * HV:(h + 1) * HV].astype(BF)
                k_dec = k_fw * dl
                dp = _dot(d_out, v, tb=True)
                dp_fw = jnp.where(tri, dp, 0.0)
                dp_bw = jnp.where(tri, 0.0, dp)
                dv = _dot(scores, d_out, ta=True) + _dot(k_dec, dst, tb=True)
                dk_dec = _dot(v, dst)
                dq_fw = _dot(dp_fw, k_fw) + _dot(d_out, st)
                dk_fw = _dot(dp_fw, q_fw, ta=True) + dk_dec * dl
                dq_bw = _dot(dp_bw, k_bw)
                dk_bw = _dot(dp_bw, q_bw, ta=True)
                ddl = jnp.sum(st * dst, axis=0, keepdims=True) + jnp.sum(k_fw * dk_dec, axis=0, keepdims=True)
                dstate[h] = dst * dl + _dot(d_out, q_fw, ta=True)
                dq = (dq_fw * e_pos + dq_bw * e_neg) * (HK ** -0.5)
                dk = dk_fw * e_neg + dk_bw * e_pos
                dbs.append(dq_fw * q_fw - dk_fw * k_fw - dq_bw * q_bw + dk_bw * k_bw + jnp.where(last_row, ddl * dl, 0.0))
                dqkv_ref[rows, h * HK:(h + 1) * HK] = dq.astype(BF)
                dqkv_ref[rows, GLA_DK + h * HK:GLA_DK + (h + 1) * HK] = dk.astype(BF)
                dqkv_ref[rows, 2 * GLA_DK + h * HV:2 * GLA_DK + (h + 1) * HV] = dv.astype(BF)
            dla_ref[rows, :] = _dot_exact(upper, jnp.concatenate(dbs, axis=1))

    rev = lambda i: (GLA_STEPS - 1 - i, 0)
    return pl.pallas_call(
        body, name="gla_bwd", grid=(GLA_STEPS,),
        in_specs=[pl.BlockSpec(memory_space=pl.ANY), pl.BlockSpec((GLA_ROWS, QKV_W), rev),
                  pl.BlockSpec((GLA_ROWS, GLA_DK), rev), pl.BlockSpec((GLA_ROWS, D_MODEL), rev),
                  pl.BlockSpec((GLA_CPS, HEADS, HV, HK), lambda i: (GLA_STEPS - 1 - i, 0, 0, 0))],
        out_specs=[pl.BlockSpec((GLA_ROWS, QKV_W), rev), pl.BlockSpec((GLA_ROWS, GLA_DK), rev)],
        out_shape=[_out_hbm((SEQ, N_DZ), BF), _out_hbm((SEQ, GLA_DK), F32)],
        scratch_shapes=[pltpu.VMEM((HEADS, HV, HK), F32)], input_output_aliases={0: 0},
        compiler_params=_params("arbitrary"),
    )(*map(_in_hbm, (dzcat, zcat, la, d_o, states)))


def _silu_parts(x):
    s = _sigmoid(x)
    return x * s, s * (1.0 + x * (1.0 - s))


def _post_gla_fwd(o, zcat, g_head):
    def body(o_ref, zog_ref, g_ref, out_ref):
        for h in range(HEADS):
            cols = slice(h * HV, (h + 1) * HV)
            ov = o_ref[:, cols]
            r = lax.rsqrt(jnp.mean(ov * ov, axis=-1, keepdims=True) + EPS)
            act, _ = _silu_parts(zog_ref[:, cols].astype(F32))
            out_ref[:, cols] = (ov * r * g_ref[...] * act).astype(BF)

    tile = pl.BlockSpec((TOK_TILE, D_MODEL), lambda i: (i, 0))
    return pl.pallas_call(
        body, name="post_gla_fwd", grid=(SEQ // TOK_TILE,),
        in_specs=[tile, pl.BlockSpec((TOK_TILE, D_MODEL), lambda i: (i, C_OG // D_MODEL)), _const_spec((1, HV))],
        out_specs=tile, out_shape=_out_hbm((SEQ, D_MODEL), BF), compiler_params=_params("parallel"),
    )(*map(_in_hbm, (o, zcat, g_head)))


def _post_gla_bwd(dzcat, dy_gla, w_gla_proj, o, zcat, g_head, after):
    def body(dz_in, dyg_ref, w_ref, o_ref, zog_ref, g_ref, after_ref, dz_ref, do_ref, dg_ref):
        del dz_in, after_ref
        dog = _dot(dyg_ref[...], w_ref[...], tb=True)
        gpart = jnp.zeros((1, HV), F32)
        gv = g_ref[...]
        for h in range(HEADS):
            cols = slice(h * HV, (h + 1) * HV)
            ov = o_ref[:, cols]
            r = lax.rsqrt(jnp.mean(ov * ov, axis=-1, keepdims=True) + EPS)
            on = ov * r
            act, dact = _silu_parts(zog_ref[:, cols].astype(F32))
            dogv = dog[:, cols]
            dz_ref[:, cols] = (dogv * on * gv * dact).astype(BF)
            d_on_g = dogv * act
            gpart = gpart + jnp.sum(d_on_g * on, axis=0, keepdims=True)
            dxn = d_on_g * gv
            do_ref[:, cols] = (r * (dxn - on * jnp.mean(dxn * on, axis=-1, keepdims=True))).astype(BF)

        @pl.when(pl.program_id(0) == 0)
        def _():
            dg_ref[...] = gpart

        @pl.when(pl.program_id(0) > 0)
        def _():
            dg_ref[...] += gpart

    tile = pl.BlockSpec((TOK_TILE, D_MODEL), lambda i: (i, 0))
    ogspec = pl.BlockSpec((TOK_TILE, D_MODEL), lambda i: (i, C_OG // D_MODEL))
    return pl.pallas_call(
        body, name="post_gla_bwd", grid=(SEQ // TOK_TILE,),
        in_specs=[pl.BlockSpec(memory_space=pl.ANY), tile, _const_spec((D_MODEL, D_MODEL)), tile, ogspec,
                  _const_spec((1, HV)), pl.BlockSpec(memory_space=pl.ANY)],
        out_specs=[ogspec, tile, _const_spec((1, HV))],
        out_shape=[_out_hbm((SEQ, N_DZ), BF), _out_hbm((SEQ, D_MODEL), BF),
                   _out_hbm((1, HV), F32)],
        input_output_aliases={0: 0}, compiler_params=_params("arbitrary"),
    )(*map(_in_hbm, (dzcat, dy_gla, w_gla_proj, o, zcat, g_head)), after)


GATE_W = 2 * D_MODEL


def _mix_out_fwd(ps, og, zcat, x, w_pool_proj, w_gla_proj, w_out, b_gate, g_ffn, after):
    def body(ps_ref, og_ref, zg_ref, x_ref, wpp_ref, wgp_ref, wout_ref, b_ref, g_ref, after_ref,
             yp_ref, yg_ref, mixed_ref, x1_ref, h2_ref):
        del after_ref
        y_pool = _dot(ps_ref[...], wpp_ref[...])
        y_gla = _dot(og_ref[...], wgp_ref[...])
        yp_ref[...] = y_pool.astype(BF)
        yg_ref[...] = y_gla.astype(BF)
        g0 = _sigmoid(zg_ref[:, :D_MODEL].astype(F32) + b_ref[:, :D_MODEL])
        g1 = _sigmoid(zg_ref[:, D_MODEL:].astype(F32) + b_ref[:, D_MODEL:])
        mixed = (g0 * y_pool + g1 * y_gla).astype(BF)
        mixed_ref[...] = mixed
        x1 = x_ref[...] + _dot(mixed, wout_ref[...])
        x1_ref[...] = x1
        r = lax.rsqrt(jnp.mean(x1 * x1, axis=-1, keepdims=True) + EPS)
        h2_ref[...] = (x1 * r * g_ref[...]).astype(BF)

    tile = pl.BlockSpec((TOK_TILE, D_MODEL), lambda i: (i, 0))
    resident = lambda shape: pl.BlockSpec(shape, lambda i: (0, 0), pipeline_mode=pl.Buffered(1))
    f32, bf16 = _out_hbm((SEQ, D_MODEL), F32), _out_hbm((SEQ, D_MODEL), BF)
    return pl.pallas_call(
        body, name="mix_out_fwd", grid=(SEQ // TOK_TILE,),
        in_specs=[pl.BlockSpec((TOK_TILE, POOL_WIDTH), lambda i: (i, 0)), tile,
                  pl.BlockSpec((TOK_TILE, GATE_W), lambda i: (i, C_GATE // GATE_W)), tile,
                  resident((POOL_WIDTH, D_MODEL)), resident((D_MODEL, D_MODEL)), resident((D_MODEL, D_MODEL)),
                  _const_spec((1, GATE_W)), _const_spec((1, D_MODEL)), pl.BlockSpec(memory_space=pl.ANY)],
        out_specs=[tile] * 5, out_shape=[bf16, bf16, bf16, f32, bf16], compiler_params=_params("parallel"),
    )(*map(_in_hbm, (ps, og, zcat, x, w_pool_proj, w_gla_proj, w_out, b_gate, g_ffn)), after)


def _mix_bwd(dx1, w_out, zcat, b_gate, y_pool, y_gla):
    def body(dx_ref, w_ref, zg_ref, b_ref, yp_ref, yg_ref, dz_ref, dyp_ref, dyg_ref, db_ref):
        dm = _dot(dx_ref[...], w_ref[...], tb=True)
        g0 = _sigmoid(zg_ref[:, :D_MODEL].astype(F32) + b_ref[:, :D_MODEL])
        g1 = _sigmoid(zg_ref[:, D_MODEL:].astype(F32) + b_ref[:, D_MODEL:])
        dyp_ref[...] = (dm * g0).astype(BF)
        dyg_ref[...] = (dm * g1).astype(BF)
        dz0 = dm * yp_ref[...].astype(F32) * g0 * (1.0 - g0)
        dz1 = dm * yg_ref[...].astype(F32) * g1 * (1.0 - g1)
        dz_ref[:, :D_MODEL] = dz0.astype(BF)
        dz_ref[:, D_MODEL:] = dz1.astype(BF)
        b0 = jnp.sum(dz0, axis=0, keepdims=True)
        b1 = jnp.sum(dz1, axis=0, keepdims=True)

        @pl.when(pl.program_id(0) == 0)
        def _():
            db_ref[:, :D_MODEL] = b0
            db_ref[:, D_MODEL:] = b1

        @pl.when(pl.program_id(0) > 0)
        def _():
            db_ref[:, :D_MODEL] += b0
            db_ref[:, D_MODEL:] += b1

    tile = pl.BlockSpec((TOK_TILE, D_MODEL), lambda i: (i, 0))
    gspec = pl.BlockSpec((TOK_TILE, GATE_W), lambda i: (i, C_GATE // GATE_W))
    return pl.pallas_call(
        body, name="mix_bwd", grid=(SEQ // TOK_TILE,),
        in_specs=[tile, _const_spec((D_MODEL, D_MODEL)), gspec, _const_spec((1, GATE_W)), tile, tile],
        out_specs=[gspec, tile, tile, _const_spec((1, GATE_W))],
        out_shape=[_out_hbm((SEQ, N_DZ), BF), _out_hbm((SEQ, D_MODEL), BF),
                   _out_hbm((SEQ, D_MODEL), BF), _out_hbm((1, GATE_W), F32)],
        compiler_params=_params("arbitrary"),
    )(*map(_in_hbm, (dx1, w_out, zcat, b_gate, y_pool, y_gla)))


N_TOK_TILES = SEQ // TOK_TILE
HALO_PER_TILE = TOK_TILE // HALO


LANE_TILES = tuple((lo, min(128, FF_BLK - lo)) for lo in range(0, FF_BLK, 128))


def _taps(w_ref, b_ref, half, lanes, rows):
    shape = (rows, lanes.stop - lanes.start)
    return ([jnp.broadcast_to(w_ref[half, j:j + 1, lanes], shape) for j in range(3)],
            jnp.broadcast_to(b_ref[half, :, lanes], shape))


def _conv_strips(u_ref, ub_ref, ua_ref, taps, lanes, width, n_strips, first):
    row = lax.broadcasted_iota(jnp.int32, (HALO, width), 0)
    prev = [[pltpu.roll(jnp.where(first, 0.0, ub_ref[half, :, lanes]), k, 0) for k in (1, 2)] for half in range(2)]
    for s in range(n_strips + (ua_ref is not None)):
        u3, conv = [], []
        for half in range(2):
            cur = u_ref[half, s * HALO:(s + 1) * HALO, lanes] if s < n_strips else ua_ref[half, :, lanes]
            rolled = [pltpu.roll(cur, k, 0) for k in (1, 2)]
            frames = [jnp.where(row >= 2, rolled[1], prev[half][1]), jnp.where(row >= 1, rolled[0], prev[half][0]), cur]
            prev[half] = rolled
            w3, bias = taps[half]
            u3.append(frames)
            conv.append(bias + frames[0] * w3[0] + frames[1] * w3[1] + frames[2] * w3[2])
        yield s, u3, conv


def _pair_specs(pairs):
    tile = pl.BlockSpec((pairs, None, TOK_TILE, FF_BLK), lambda b, i: (0, b, i, 0))
    before = pl.BlockSpec((pairs, None, HALO, FF_BLK), lambda b, i: (0, b, jnp.maximum(i * HALO_PER_TILE - 1, 0), 0))
    after = pl.BlockSpec((pairs, None, HALO, FF_BLK),
                         lambda b, i: (0, b, jnp.minimum((i + 1) * HALO_PER_TILE, SEQ // HALO - 1), 0))

    def vec(rows):
        return pl.BlockSpec((2, None, rows, FF_BLK), lambda b, i: (0, b, 0, 0))

    return tile, before, after, vec


N_STRIPS = TOK_TILE // HALO


def _up_conv_fwd(h2, wt_up, w_conv, b_conv):
    steps = N_TOK_TILES // 2

    def body(h_ref, h_next, wg_ref, wv_ref, w_ref, b_ref, u_ref, a_ref, buf_a, buf_b, carry):
        j = pl.program_id(1)

        def project(hv, buf):
            buf[0] = _dot(hv, wg_ref[...], tb=True)
            buf[1] = _dot(hv, wv_ref[...], tb=True)

        def conv(buf, row0):
            u_ref[:, row0:row0 + TOK_TILE, :] = buf[...]
            for lo, width in LANE_TILES:
                lanes = slice(lo, lo + width)
                taps = [_taps(w_ref, b_ref, half, lanes, HALO) for half in range(2)]
                pending = None
                for s, _, (cg, cv) in _conv_strips(buf, carry, None, taps, lanes, width, N_STRIPS, False):
                    act = cg * _sigmoid(cg) * cv
                    if s % 2 == 0:
                        pending = act
                    else:
                        a_ref[0, row0 + (s - 1) * HALO:row0 + (s + 1) * HALO, lanes] = (
                            jnp.concatenate([pending, act], axis=0).astype(BF))
            carry[...] = buf[:, TOK_TILE - HALO:, :]

        @pl.when(j == 0)
        def _():
            project(h_ref[0:TOK_TILE, :], buf_a)
            carry[...] = jnp.zeros_like(carry)

        project(h_ref[TOK_TILE:, :], buf_b)
        conv(buf_a, 0)
        project(h_next[...], buf_a)
        conv(buf_b, TOK_TILE)

    w_blk = lambda half: pl.BlockSpec((FF_BLK, D_MODEL), lambda b, j: (b + 4 * half, 0))
    vec = lambda rows: pl.BlockSpec((2, None, rows, FF_BLK), lambda b, j: (0, b, 0, 0))
    u_buf = pltpu.VMEM((2, TOK_TILE, FF_BLK), F32)
    return pl.pallas_call(
        body, name="up_conv_fwd", grid=(4, steps),
        in_specs=[pl.BlockSpec((2 * TOK_TILE, D_MODEL), lambda b, j: (j, 0)),
                  pl.BlockSpec((TOK_TILE, D_MODEL), lambda b, j: (jnp.minimum(2 * j + 2, N_TOK_TILES - 1), 0)),
                  w_blk(0), w_blk(1), vec(3), vec(1)],
        out_specs=[pl.BlockSpec((2, None, 2 * TOK_TILE, FF_BLK), lambda b, j: (0, b, j, 0)),
                   pl.BlockSpec((1, None, 2 * TOK_TILE, FF_BLK), lambda b, j: (0, b, j, 0))],
        out_shape=[_out_hbm((2, 4, SEQ, FF_BLK), F32), _out_hbm((1, 4, SEQ, FF_BLK), BF)],
        scratch_shapes=[u_buf, u_buf, pltpu.VMEM((2, HALO, FF_BLK), F32)],
        compiler_params=_params("parallel", "arbitrary"),
    )(*map(_in_hbm, (h2, h2, wt_up, wt_up, w_conv, b_conv)))


def _conv_bwd(u, da, w_conv, b_conv):
    def body(u_ref, ub_ref, ua_ref, da_ref, daa_ref, w_ref, b_ref, du_ref, dw_ref, db_ref):
        i = pl.program_id(1)

        @pl.when(i == 0)
        def _():
            dw_ref[...] = jnp.zeros_like(dw_ref)
            db_ref[...] = jnp.zeros_like(db_ref)

        for lo, width in LANE_TILES:
            lanes = slice(lo, lo + width)
            row = lax.broadcasted_iota(jnp.int32, (HALO, width), 0)
            taps = [_taps(w_ref, b_ref, half, lanes, HALO) for half in range(2)]
            acc_w = [[jnp.zeros((HALO, width), F32) for _ in range(3)] for _ in range(2)]
            acc_b = [jnp.zeros((HALO, width), F32) for _ in range(2)]
            da_pair, pending = None, [None, None]
            dc_prev, up_prev = [None, None], [None, None]
            for s, u3, (cg, cv) in _conv_strips(u_ref, ub_ref, ua_ref, taps, lanes, width, N_STRIPS, i == 0):
                act, dact = _silu_parts(cg)
                if s == N_STRIPS:
                    da = jnp.where(i < N_TOK_TILES - 1, daa_ref[0, :, lanes].astype(F32), 0.0)
                elif s % 2 == 0:
                    da_pair = da_ref[0, s * HALO:(s + 2) * HALO, lanes].astype(F32)
                    da = da_pair[:HALO]
                else:
                    da = da_pair[HALO:]
                dc = (da * cv * dact, da * act)
                for half in range(2):
                    up = [pltpu.roll(dc[half], HALO - k, 0) for k in (1, 2)]
                    if s < N_STRIPS:
                        for j in range(3):
                            acc_w[half][j] = acc_w[half][j] + dc[half] * u3[half][j]
                        acc_b[half] = acc_b[half] + dc[half]
                    if s >= 1:
                        w3 = taps[half][0]
                        du = (dc_prev[half] * w3[2] + jnp.where(row < HALO - 1, up_prev[half][0], up[0]) * w3[1]
                              + jnp.where(row < HALO - 2, up_prev[half][1], up[1]) * w3[0])
                        if (s - 1) % 2 == 0:
                            pending[half] = du
                        else:
                            du_ref[half, (s - 2) * HALO:s * HALO, lanes] = jnp.concatenate([pending[half], du],
                                                                                           axis=0).astype(BF)
                    dc_prev[half], up_prev[half] = dc[half], up
            for half in range(2):
                for j in range(3):
                    dw_ref[half, j:j + 1, lanes] += jnp.sum(acc_w[half][j], axis=0, keepdims=True)
                db_ref[half, :, lanes] += jnp.sum(acc_b[half], axis=0, keepdims=True)

    tile, before, after, vec = _pair_specs(2)
    da_tile, _, da_after_spec, _ = _pair_specs(1)
    return pl.pallas_call(
        body, name="conv_bwd", grid=(4, N_TOK_TILES),
        in_specs=[tile, before, after, da_tile, da_after_spec, vec(3), vec(1)],
        out_specs=[tile, vec(3), vec(1)],
        out_shape=[_out_hbm((2, 4, SEQ, FF_BLK), BF), _out_hbm((2, 4, 3, FF_BLK), F32),
                   _out_hbm((2, 4, 1, FF_BLK), F32)],
        compiler_params=_params("parallel", "arbitrary"),
    )(*map(_in_hbm, (u, u, u, da, da, w_conv, b_conv)))


W_IN_SEGMENTS = ((R_POOL, POOL_WIDTH, C_POOL), (R_QKV, QKV_W, C_QKV), (R_OG, D_MODEL, C_OG), (R_GK, GATE_RANK, C_GK),
                 (R_GATE, GATE_W, C_GATE))


def _slab_pieces(d):
    lo, hi = d * IN_SHARD, (d + 1) * IN_SHARD
    pieces = []
    for start, n, at in W_IN_SEGMENTS:
        a, b = max(lo, start), min(hi, start + n)
        if a < b:
            assert (a - lo) % 2 == 0 and (b - a) % 2 == 0 and (at + a - start) % 2 == 0
            pieces.append(((a - lo) // 2, (b - a) // 2, (at + a - start) // 2))
    return pieces


def _unshard_w_in(slabs):
    def body(slab_ref, cat_ref):
        d = pl.program_id(0)
        src = slab_ref.bitcast(jnp.uint32)
        dst = cat_ref.bitcast(jnp.uint32)

        @pl.when(d == 0)
        def _():
            cat_ref[C_GK:, :] = jnp.zeros((GK_PAD, D_MODEL), BF)

        for dd in range(N_DEV):
            @pl.when(d == dd)
            def _():
                for a, n, at in _slab_pieces(dd):
                    dst[pl.ds(at, n), :] = src[0, pl.ds(a, n), :]

    return pl.pallas_call(
        body, name="unshard_w_in", grid=(N_DEV,),
        in_specs=[pl.BlockSpec((1, IN_SHARD, D_MODEL), lambda d: (d, 0, 0))], out_specs=_const_spec((N_DZ, D_MODEL)),
        out_shape=_out_hbm((N_DZ, D_MODEL), BF), compiler_params=_params("arbitrary"),
    )(_in_hbm(slabs))


def _shard_d_w_in(d_cat):
    def body(cat_ref, slab_ref):
        d = pl.program_id(0)
        cat = cat_ref.bitcast(jnp.uint32)
        dst = slab_ref.bitcast(jnp.uint32)
        for dd in range(N_DEV):
            @pl.when(d == dd)
            def _():
                for a, n, at in _slab_pieces(dd):
                    dst[0, pl.ds(a, n), :] = cat[pl.ds(at, n), :]

    return pl.pallas_call(
        body, name="shard_d_w_in", grid=(N_DEV,), in_specs=[_const_spec((N_DZ, D_MODEL))],
        out_specs=pl.BlockSpec((1, IN_SHARD, D_MODEL), lambda d: (d, 0, 0)),
        out_shape=_out_hbm((N_DEV, IN_SHARD, D_MODEL), BF), compiler_params=_params("parallel"),
    )(_in_hbm(d_cat))


ANY = pl.BlockSpec(memory_space=pl.ANY)


def _place():
    x, y, c = lax.axis_index("x"), lax.axis_index("y"), lax.axis_index("c")
    other_chips = [(1 - x, y), (x, 1 - y), (1 - x, 1 - y)]
    return x, y, c, other_chips


SEM = pl.BlockSpec(memory_space=pltpu.SEMAPHORE)
IN_HBM = pl.BlockSpec(memory_space=pltpu.HBM)
SPLIT_PARAMS = pltpu.CompilerParams(has_side_effects=pltpu.SideEffectType.DATAFLOW_SIDE_EFFECTING)


def _gather_first(refs, send_sems, recv_sems):
    x, y, c, chips = _place()
    targets = [(x, y, 1 - c)] + [(px, py, c) for px, py in chips]
    copies = []
    for a, land in enumerate(refs):
        mine = land.at[4 * x + 2 * y + c]
        copies += [pltpu.make_async_remote_copy(src_ref=mine, dst_ref=mine, send_sem=send_sems.at[4 * a + k],
                                                recv_sem=recv_sems.at[4 * a + k], device_id=to, device_id_type=MESH)
                   for k, to in enumerate(targets)]
    return copies


def _gather_direct(refs, send_sems, recv_sems):
    x, y, c, _ = _place()
    flips = [(dx, dy, dc) for dx in (0, 1) for dy in (0, 1) for dc in (0, 1) if dx + dy + dc]
    targets = [(1 - x if dx else x, 1 - y if dy else y, 1 - c if dc else c) for dx, dy, dc in flips]
    return [pltpu.make_async_remote_copy(src_ref=refs[2 * a], dst_ref=refs[2 * a + 1].at[4 * x + 2 * y + c],
                                         send_sem=send_sems.at[7 * a + k], recv_sem=recv_sems.at[7 * a + k],
                                         device_id=to, device_id_type=MESH)
            for a in range(len(refs) // 2) for k, to in enumerate(targets)]


def _gather_second(refs, send_sems, recv_sems):
    x, y, c, chips = _place()
    copies = []
    for a, land in enumerate(refs):
        for j, (px, py) in enumerate(chips):
            block = land.at[4 * px + 2 * py + c]
            copies.append(pltpu.make_async_remote_copy(src_ref=block, dst_ref=block, send_sem=send_sems.at[3 * a + j],
                                                       recv_sem=recv_sems.at[3 * a + j], device_id=(x, y, 1 - c),
                                                       device_id_type=MESH))
    return copies


def _reduce_first(refs, send_sems, recv_sems):
    x, y, c, _ = _place()
    return [pltpu.make_async_remote_copy(src_ref=refs[2 * a].at[j, 1 - c], dst_ref=refs[2 * a + 1].at[j],
                                         send_sem=send_sems.at[4 * a + j], recv_sem=recv_sems.at[4 * a + j],
                                         device_id=(x, y, 1 - c), device_id_type=MESH)
            for a in range(len(refs) // 2) for j in range(4)]


def _reduce_second(refs, send_sems, recv_sems):
    _, _, c, chips = _place()
    return [pltpu.make_async_remote_copy(src_ref=refs[2 * a].at[2 * px + py], dst_ref=refs[2 * a + 1].at[k],
                                         send_sem=send_sems.at[3 * a + k], recv_sem=recv_sems.at[3 * a + k],
                                         device_id=(px, py, c), device_id_type=MESH)
            for a in range(len(refs) // 2) for k, (px, py) in enumerate(chips)]


def _split_start(name, groups):
    arrays = [a for g in groups for a in g[0]]
    n = len(arrays)

    def body(*refs):
        sems = refs[n:n + 2 * len(groups)]
        at = 0
        for gi, (members, _, build) in enumerate(groups):
            for cp in build(refs[at:at + len(members)], sems[2 * gi], sems[2 * gi + 1]):
                cp.start()
            at += len(members)
        refs[-1][...] = jnp.zeros_like(refs[-1])

    sem_shapes = [pltpu.SemaphoreType.DMA((g[1],)) for g in groups for _ in range(2)]
    outs = pl.pallas_call(
        body, name=name, in_specs=[IN_HBM] * n,
        out_shape=(*sem_shapes, *[_out_hbm(a.shape, a.dtype) for a in arrays], jax.ShapeDtypeStruct((8, 128), F32)),
        out_specs=(*[SEM] * len(sem_shapes), *[IN_HBM] * n, pl.BlockSpec(memory_space=pltpu.VMEM)),
        input_output_aliases={i: len(sem_shapes) + i for i in range(n)}, compiler_params=SPLIT_PARAMS,
    )(*[pltpu.with_memory_space_constraint(a, pltpu.HBM) for a in arrays])
    per_group, at = [], len(sem_shapes)
    for gi, (members, _, _) in enumerate(groups):
        per_group.append((outs[2 * gi], outs[2 * gi + 1], list(outs[at:at + len(members)])))
        at += len(members)
    return per_group, outs[-1]


def _split_wait(name, started, build, after):
    send_sems, recv_sems, arrays = started
    n = len(arrays)
    after = after if isinstance(after, (tuple, list)) else (after,)

    def body(*refs):
        for cp in build(refs[:n], refs[n], refs[n + 1]):
            cp.wait_send()
            cp.wait_recv()

    return pl.pallas_call(
        body, name=name, in_specs=[IN_HBM] * n + [SEM, SEM] + [ANY] * len(after),
        out_shape=tuple(_out_hbm(a.shape, a.dtype) for a in arrays), out_specs=tuple([IN_HBM] * n),
        input_output_aliases={i: i for i in range(n)}, compiler_params=SPLIT_PARAMS,
    )(*arrays, send_sems, recv_sems, *after)


def _gather_forward(name, started, after):
    send_in, recv_in, lands = started
    n = len(lands)
    after = after if isinstance(after, (tuple, list)) else (after,)

    def body(*refs):
        arrived = _gather_first(refs[:n], refs[n], refs[n + 1])
        onward = _gather_second(refs[:n], *refs[n + 2 + len(after):n + 4 + len(after)])
        for a in range(n):
            for j in range(3):
                arrived[4 * a + 1 + j].wait_recv()
                onward[3 * a + j].start()
        for a in range(n):
            arrived[4 * a].wait_recv()
        for cp in arrived:
            cp.wait_send()
        refs[-1][...] = jnp.zeros_like(refs[-1])

    sem_shape = pltpu.SemaphoreType.DMA((3 * n,))
    outs = pl.pallas_call(
        body, name=name, in_specs=[IN_HBM] * n + [SEM, SEM] + [ANY] * len(after),
        out_shape=(sem_shape, sem_shape, *[_out_hbm(a.shape, a.dtype) for a in lands],
                   jax.ShapeDtypeStruct((8, 128), F32)),
        out_specs=(SEM, SEM, *[IN_HBM] * n, pl.BlockSpec(memory_space=pltpu.VMEM)),
        input_output_aliases={i: 2 + i for i in range(n)}, compiler_params=SPLIT_PARAMS,
    )(*lands, send_in, recv_in, *after)
    return (outs[0], outs[1], list(outs[2:2 + n])), outs[-1]


def _placed_behind(token, arrays, name):
    n = len(arrays)

    def body(*refs):
        refs[-1][...] = jnp.zeros_like(refs[-1])

    outs = pl.pallas_call(
        body, name=name, in_specs=[IN_HBM] * n + [ANY],
        out_shape=(*[_out_hbm(a.shape, a.dtype) for a in arrays], jax.ShapeDtypeStruct((8, 128), F32)),
        out_specs=(*[IN_HBM] * n, pl.BlockSpec(memory_space=pltpu.VMEM)),
        input_output_aliases={i: i for i in range(n)},
    )(*map(_in_hbm, arrays), token)
    return outs[:n], outs[-1]


def _gather_landing(shard, me):
    return lax.dynamic_update_slice(lax.empty((N_DEV,) + shard.shape, shard.dtype), shard[None],
                                    (me,) + (0,) * shard.ndim)


ADAM_LANE_TILE = 256


def _tile_2d(rows, cols):
    for t in (256, 176, 128):
        if rows % t == 0:
            return t, cols
    return rows, ADAM_LANE_TILE


def _pair_sum(part, recv, core, name):
    _, rows, cols = recv.shape
    tr, tc = rows, cols

    def body(c_ref, p_ref, r_ref, o_ref):
        del c_ref
        o_ref[...] = (p_ref[...].astype(F32) + r_ref[...].astype(F32)).astype(BF)

    grid_spec = pltpu.PrefetchScalarGridSpec(
        num_scalar_prefetch=1, grid=(4, rows // tr, cols // tc),
        in_specs=[pl.BlockSpec((None, None, tr, tc), lambda j, i, k, c_ref: (j, c_ref[0], i, k)),
                  pl.BlockSpec((None, tr, tc), lambda j, i, k, c_ref: (j, i, k))],
        out_specs=pl.BlockSpec((None, tr, tc), lambda j, i, k, c_ref: (j, i, k)))
    return pl.pallas_call(
        body, name=name, grid_spec=grid_spec, out_shape=_out_hbm(recv.shape, BF),
        compiler_params=_params("parallel", "parallel", "parallel"),
    )(core, *map(_in_hbm, (part, recv)))


def _adamw(w, g, m, v):
    m = ADAM_B1 * m + (1.0 - ADAM_B1) * g
    v = ADAM_B2 * v + (1.0 - ADAM_B2) * (g * g)
    delta = -ADAM_LR * ((m / ADAM_C1) / (jnp.sqrt(v / ADAM_C2) + ADAM_EPS) + ADAM_WD * w)
    return delta, m, v


def _chip_sum_adamw(sums, recv, w, m, v, chip, name, lone_rows=False):
    rows, cols = w.shape
    tr, tc = _tile_2d(rows, cols)

    def body(chip_ref, s_ref, r_ref, w_ref, m_ref, v_ref, *out_refs):
        del chip_ref
        g = s_ref[...].astype(F32)
        for k in range(3):
            g = g + r_ref[k].astype(F32)
        for o_ref, t in zip(out_refs, (g,) + _adamw(w_ref[...], g, m_ref[...], v_ref[...])):
            o_ref[...] = t[:, None, :] if lone_rows else t

    tile = pl.BlockSpec((tr, tc), lambda i, k, chip_ref: (i, k))
    out_tile = pl.BlockSpec((tr, 1, tc), lambda i, k, chip_ref: (i, 0, k)) if lone_rows else tile
    grid_spec = pltpu.PrefetchScalarGridSpec(
        num_scalar_prefetch=1, grid=(rows // tr, cols // tc),
        in_specs=[pl.BlockSpec((None, tr, tc), lambda i, k, chip_ref: (chip_ref[0], i, k)),
                  pl.BlockSpec((3, tr, tc), lambda i, k, chip_ref: (0, i, k)), tile, tile, tile],
        out_specs=[out_tile] * 4)
    return pl.pallas_call(
        body, name=name, grid_spec=grid_spec,
        out_shape=[_out_hbm((rows, 1, cols) if lone_rows else (rows, cols), F32)] * 4,
        compiler_params=_params("parallel", "parallel"),
    )(chip, *map(_in_hbm, (sums, recv, w, m, v)))


def _small_sum_adamw(me, entries, loss):
    def whole(shape, squeeze=0, pick=None):
        blk = (None,) * squeeze + tuple(shape[squeeze:])
        if pick is not None:
            blk = tuple(shape[:pick]) + (None,) + tuple(shape[pick + 1:])
            return pl.BlockSpec(blk, lambda i, me_ref: (0,) * pick + (me_ref[0],) + (0,) * (len(shape) - pick - 1))
        return pl.BlockSpec(blk, lambda i, me_ref: (0,) * len(shape))

    in_specs, out_specs, out_shape, args = [], [], [], []
    for own, parts, w, m, v, sharded in entries + [loss + (None, None, None, False)]:
        in_specs += [whole(own.shape, pick=0 if sharded else None), whole(parts.shape, pick=1 if sharded else None)]
        args += [own, parts]
        if w is not None:
            lead = w.ndim - (parts.ndim - (2 if sharded else 1))
            in_specs += [whole(w.shape, squeeze=lead)] * 3
            out_specs += [whole(w.shape, squeeze=lead)] * 4
            out_shape += [_out_hbm(w.shape, F32)] * 4
            args += [w, m, v]
    out_specs.append(whole(loss[0].shape))
    out_shape.append(_out_hbm(loss[0].shape, F32))
    n = len(entries)

    def added(own_ref, p_ref, me):
        total = None
        for d in range(N_DEV):
            part = jnp.where(me == d, own_ref[...], p_ref[d])
            total = part if total is None else total + part
        return total

    def body(me_ref, *refs):
        ins, outs = refs[:5 * n + 2], refs[5 * n + 2:]
        for e in range(n):
            own_ref, p_ref, w_ref, m_ref, v_ref = ins[5 * e:5 * e + 5]
            g_out, d_out, m_out, v_out = outs[4 * e:4 * e + 4]
            g = added(own_ref, p_ref, me_ref[0])
            g_out[...] = g
            d_out[...], m_out[...], v_out[...] = _adamw(w_ref[...], g, m_ref[...], v_ref[...])
        outs[4 * n][...] = added(ins[5 * n], ins[5 * n + 1], me_ref[0])

    grid_spec = pltpu.PrefetchScalarGridSpec(num_scalar_prefetch=1, grid=(1,), in_specs=in_specs, out_specs=out_specs)
    outs = pl.pallas_call(body, name="small_sum_adamw", grid_spec=grid_spec, out_shape=out_shape,
                          compiler_params=_params("arbitrary"))(me, *map(_in_hbm, args))
    return [outs[4 * e:4 * e + 4] for e in range(n)], outs[4 * n]


MM_TILE = 512
N_MM_TILES = SEQ // MM_TILE
CAT_TILE = 512
N_CAT_TILES = N_CAT // CAT_TILE
DZ_TILE = 640


def kernel(x, g_mix, w_in, b_gate, w_gk_up, b_gk, w_pool_grp, pool_scale, g_gla_head, w_pool_proj, w_gla_proj, w_out, g_ffn, w_up, w_conv, b_conv, w_down, g_final, loss_target, m_g_mix, m_w_in, m_b_gate, m_w_gk_up, m_b_gk, m_w_pool_grp, m_pool_scale, m_g_gla_head, m_w_pool_proj, m_w_gla_proj, m_w_out, m_g_ffn, m_w_up, m_w_conv, m_b_conv, m_w_down, m_g_final, v_g_mix, v_w_in, v_b_gate, v_w_gk_up, v_b_gk, v_w_pool_grp, v_pool_scale, v_g_gla_head, v_w_pool_proj, v_w_gla_proj, v_w_out, v_g_ffn, v_w_up, v_w_conv, v_b_conv, v_w_down, v_g_final):
    xi, yi, ci = lax.axis_index("x"), lax.axis_index("y"), lax.axis_index("c")
    me = 4 * xi + 2 * yi + ci
    core = jnp.reshape(ci, (1,)).astype(jnp.int32)
    chip = jnp.reshape(2 * xi + yi, (1,)).astype(jnp.int32)
    xs, target = x[0], loss_target[0]

    big = dict(w_in=w_in[0].T, w_pool_proj=w_pool_proj[0], w_gla_proj=w_gla_proj[0], w_out=w_out[0], w_up=w_up[0].T,
               w_down=w_down[0])
    moments = dict(w_in=(m_w_in[0].T, v_w_in[0].T), w_pool_proj=(m_w_pool_proj[0], v_w_pool_proj[0]),
                   w_gla_proj=(m_w_gla_proj[0], v_w_gla_proj[0]), w_out=(m_w_out[0], v_w_out[0]),
                   w_up=(m_w_up[0].T, v_w_up[0].T), w_down=(m_w_down[0], v_w_down[0]))
    names = list(big)
    shards = {k: big[k].astype(BF) for k in names}
    shards["w_gk_up"], shards["w_conv"] = w_gk_up[0], w_conv[0]
    gather_groups = (("w_in", "w_gk_up"), ("w_pool_proj", "w_gla_proj", "w_out"), ("w_up", "w_down", "w_conv"))
    started, token = _split_start("gather_start", [
        ([_gather_landing(shards[k], me) for k in g], 4 * len(g), _gather_first) for g in gather_groups])
    (big["w_in"], *moments["w_in"], bconv4, m_w_conv, v_w_conv), token = _placed_behind(
        token, [big["w_in"], *moments["w_in"], b_conv.reshape(2, 4, 1, FF_BLK), m_w_conv, v_w_conv],
        "place_adamw_operands")

    def gather_pass(gi, after):
        return _gather_forward(f"gather_pass_{gi}", started[gi], after)

    def gather_done(gi, passed, after):
        return dict(zip(gather_groups[gi], _split_wait(f"gather_pass_wait_{gi}", passed, _gather_second, after)))

    tok = lambda i, j, k: (i, 0)
    whole = lambda i, j, k: (0, 0)
    kblk = lambda i, j, k: (k, 0)
    ff_seq = (None, None, SEQ, FF_BLK)

    h = _rms_fwd(xs, g_mix, token, "rms_mix")
    wg = gather_done(0, gather_pass(0, h)[0], h)
    wt_cat = _unshard_w_in(wg["w_in"])
    wgk_pad = jnp.pad(wg["w_gk_up"].transpose(1, 0, 2).reshape(GATE_RANK, GLA_DK), ((0, GK_PAD - GATE_RANK), (0, 0)))
    zcat = _mm(h, wt_cat, out_shape=(SEQ, N_CAT), out_dtype=BF, grid=(N_CAT_TILES, 1, 1),
               blk_a=(SEQ, D_MODEL), blk_b=(CAT_TILE, D_MODEL), blk_o=(SEQ, CAT_TILE),
               map_a=whole, map_b=lambda j, i, k: (j, 0), map_o=lambda j, i, k: (0, j), tb=True, name="mm_in")
    la = _gk_fwd(h, wt_cat, wgk_pad, b_gk)
    passed, tkn = gather_pass(1, la)
    o, states = _gla_fwd(zcat, la, tkn)
    wg = gather_done(1, passed, o)
    wpp = wg["w_pool_proj"].transpose(1, 0, 2).reshape(POOL_WIDTH, D_MODEL)
    wgp = wg["w_gla_proj"].reshape(D_MODEL, D_MODEL)
    wout = wg["w_out"].reshape(D_MODEL, D_MODEL)
    og = _post_gla_fwd(o, zcat, g_gla_head)
    ps = _pool_fwd(zcat, w_pool_grp[0], pool_scale)
    passed, tkn = gather_pass(2, (og, ps))
    y_pool, y_gla, mixed, x1, h2 = _mix_out_fwd(ps, og, zcat, xs, wpp, wgp, wout, b_gate, g_ffn, tkn)
    wg = gather_done(2, passed, h2)
    wt_up = wg["w_up"].reshape(2 * D_FF, D_MODEL)
    wdown = wg["w_down"].reshape(D_FF, D_MODEL)
    wconv4 = wg["w_conv"].reshape(2, 4, 3, FF_BLK)
    blk4 = lambda b, i, k: (b // 4, b % 4, 0, 0)
    u4, act = _up_conv_fwd(h2, wt_up, wconv4, bconv4)
    loss_part, dx2, dx2_bf, dg_final = _mm_tokens(
        act, wdown, blk_a=(None, 4, TOK_MM_TILE, FF_BLK), map_a=lambda i: (0, 0, i, 0),
        pieces=[(b, b * FF_BLK, FF_BLK) for b in range(4)], res=x1, then=("loss", g_final.reshape(1, D_MODEL), target),
        name="mm_down_loss")

    da = _mm(dx2_bf, wdown, out_shape=(1, 4, SEQ, FF_BLK), out_dtype=BF, grid=(4, 1, 1),
             blk_a=(SEQ, D_MODEL), blk_b=(FF_BLK, D_MODEL), blk_o=ff_seq,
             map_a=whole, map_b=lambda b, i, k: (b, 0), map_o=lambda b, i, k: (0, b, 0, 0), tb=True, name="mm_d_act")
    d_wdown = _mm(act, dx2_bf, out_shape=(D_FF, D_MODEL), out_dtype=BF, grid=(4, 1, 1),
                  blk_a=ff_seq, blk_b=(SEQ, D_MODEL), blk_o=(FF_BLK, D_MODEL),
                  map_a=lambda b, i, k: (0, b, 0, 0), map_b=whole, map_o=lambda b, i, k: (b, 0), ta=True,
                  name="mm_d_wdown")
    du4, d_wconv, d_bconv = _conv_bwd(u4, da, wconv4, bconv4)
    d_wt_up = _mm(du4, h2, out_shape=(2 * D_FF, D_MODEL), out_dtype=BF, grid=(N_DEV, 1, 1),
                  blk_a=ff_seq, blk_b=(SEQ, D_MODEL), blk_o=(FF_BLK, D_MODEL),
                  map_a=blk4, map_b=whole, map_o=lambda b, i, k: (b, 0), ta=True, name="mm_d_wup")
    res = {}

    def to_sibling(keys, parts):
        return [t for k in keys for t in (parts[k], lax.empty((4,) + parts[k].shape[2:], BF))], 4 * len(keys), _reduce_first

    def to_chips(keys, st, after):
        arrays = _split_wait("reduce_wait_" + keys[0], st, _reduce_first, after)
        sums = [_pair_sum(p, r, core, "pair_sum_" + k) for k, p, r in zip(keys, arrays[0::2], arrays[1::2])]
        return [t for s in sums for t in (s, lax.empty((3,) + s.shape[1:], BF))], 3 * len(keys), _reduce_second

    def reduce_start(keys, parts):
        st, tkn = _split_start("reduce_start_" + keys[0], [to_sibling(keys, parts)])
        return st[0], tkn

    def reduce_cross(keys, st, after):
        st2, tkn = _split_start("reduce_cross_" + keys[0], [to_chips(keys, st, after)])
        return st2[0], tkn

    def reduce_done(keys, st2, after):
        arrays = _split_wait("reduce_cross_wait_" + keys[0], st2, _reduce_second, after)
        for k, s, r in zip(keys, arrays[0::2], arrays[1::2]):
            outs = _chip_sum_adamw(s, r, big[k], moments[k][0], moments[k][1], chip, "adamw_" + k,
                                   lone_rows=k == "w_in")
            res[k] = [jnp.transpose(t, (1, 2, 0)) if k == "w_in" else (t.T if k == "w_up" else t)[None] for t in outs]

    ffn_keys = ("w_down", "w_up")
    ffn_red, tkn = reduce_start(ffn_keys, dict(w_down=d_wdown.reshape(4, 2, D_FF // N_DEV, D_MODEL),
                                               w_up=d_wt_up.reshape(4, 2, FF_BLK, D_MODEL)))
    dx1, dg_ffn = _mm_tokens(
        du4, wt_up, blk_a=(2, 4, TOK_MM_TILE, FF_BLK), map_a=lambda i: (0, 0, i, 0),
        pieces=[((b // 4, b % 4), b * FF_BLK, FF_BLK) for b in range(N_DEV)], after=tkn, then=("rms_bwd", x1, g_ffn, dx2),
        name="mm_d_h2_rms")

    sq_t = dict(out_shape=(D_MODEL, D_MODEL), grid=(1, 1, N_MM_TILES), blk_a=(MM_TILE, D_MODEL),
                blk_b=(MM_TILE, D_MODEL), blk_o=(D_MODEL, D_MODEL), map_a=kblk, map_b=kblk, map_o=whole, ta=True)
    d_wout = _mm(mixed, dx1, out_dtype=BF, name="mm_d_wout", **sq_t)
    dzcat, dy_pool, dy_gla, db_gate = _mix_bwd(dx1, wout, zcat, b_gate, y_pool, y_gla)
    d_wgp = _mm(og, dy_gla, out_dtype=BF, name="mm_d_wgp", **sq_t)
    mix_keys = ("w_out", "w_gla_proj")
    (ffn_red, mix_red), tkn = _split_start("reduce_cross_w_down", [
        to_chips(ffn_keys, ffn_red, db_gate),
        to_sibling(mix_keys, dict(w_out=d_wout.reshape(4, 2, D_MODEL // N_DEV, D_MODEL),
                                  w_gla_proj=d_wgp.reshape(4, 2, D_MODEL // N_DEV, D_MODEL)))])
    dzcat, d_o, dg_head = _post_gla_bwd(dzcat, dy_gla, wgp, o, zcat, g_gla_head, tkn)
    dzcat, dla = _gla_bwd(dzcat, zcat, la, d_o, states)
    dzcat, d_wgk, db_gk = _gk_bwd(dzcat, dla, h, wt_cat, wgk_pad, b_gk)
    dps = _mm(dy_pool, wpp, out_shape=(SEQ, POOL_WIDTH), out_dtype=F32, grid=(N_MM_TILES, 1, 1),
              blk_a=(MM_TILE, D_MODEL), blk_b=(POOL_WIDTH, D_MODEL), blk_o=(MM_TILE, POOL_WIDTH),
              map_a=tok, map_b=whole, map_o=tok, tb=True, name="mm_d_ps")
    d_wpp = _mm(ps, dy_pool, out_shape=(POOL_WIDTH, D_MODEL), out_dtype=F32, grid=(1, 1, N_MM_TILES),
                blk_a=(MM_TILE, POOL_WIDTH), blk_b=(MM_TILE, D_MODEL), blk_o=(POOL_WIDTH, D_MODEL),
                map_a=kblk, map_b=kblk, map_o=whole, ta=True, name="mm_d_wpp")
    dzcat, d_wgrp, d_scale = _pool_bwd(dzcat, zcat, dps, w_pool_grp[0], pool_scale)
    row = lambda t: t.reshape(1, D_MODEL)
    small = [("b_gate", db_gate, b_gate, m_b_gate, v_b_gate, False),
             ("w_gk_up", d_wgk.reshape(GATE_RANK, N_DEV, GLA_DK // N_DEV).transpose(1, 0, 2), w_gk_up, m_w_gk_up,
              v_w_gk_up, True),
             ("b_gk", db_gk, b_gk, m_b_gk, v_b_gk, False),
             ("w_pool_grp", d_wgrp, w_pool_grp, m_w_pool_grp, v_w_pool_grp, False),
             ("pool_scale", d_scale, pool_scale, m_pool_scale, v_pool_scale, False),
             ("g_gla_head", dg_head, g_gla_head, m_g_gla_head, v_g_gla_head, False),
             ("g_ffn", dg_ffn, g_ffn, m_g_ffn, v_g_ffn, False),
             ("w_conv", d_wconv.reshape(N_DEV, 3, FF_BLK), w_conv, m_w_conv, v_w_conv, True),
             ("b_conv", d_bconv.reshape(b_conv.shape), b_conv, m_b_conv, v_b_conv, False),
             ("g_final", dg_final, row(g_final), row(m_g_final), row(v_g_final), False)]

    def to_all(parts):
        return ([t for p in parts for t in (p, lax.empty((N_DEV,) + p.shape, p.dtype))], 7 * len(parts),
                _gather_direct)

    (small_sent, mix_red), tkn = _split_start("small_start", [to_all([t[1] for t in small] + [loss_part]),
                                                              to_chips(mix_keys, mix_red, dla)])
    d_wt_cat = _mm(dzcat, h, out_shape=(N_DZ, D_MODEL), out_dtype=BF, grid=(N_DZ // DZ_TILE, 1, 1),
                   blk_a=(SEQ, DZ_TILE), blk_b=(SEQ, D_MODEL), blk_o=(DZ_TILE, D_MODEL),
                   map_a=lambda j, i, k: (0, j), map_b=whole, map_o=lambda j, i, k: (j, 0), ta=True, after=tkn,
                   name="mm_d_wcat")
    in_keys = ("w_in", "w_pool_proj")
    in_red, tkn = reduce_start(in_keys, dict(
        w_in=_shard_d_w_in(d_wt_cat).reshape(4, 2, IN_SHARD, D_MODEL),
        w_pool_proj=d_wpp.reshape(POOL_WIDTH, N_DEV, D_MODEL // N_DEV).transpose(1, 0, 2).astype(BF)
        .reshape(4, 2, POOL_WIDTH, D_MODEL // N_DEV)))
    reduce_done(mix_keys, mix_red, tkn)
    in_red, tkn = reduce_cross(in_keys, in_red, res["w_out"][0])
    grad_x, dg_mix = _mm_tokens(dzcat, wt_cat, blk_a=(TOK_MM_TILE, N_DZ), map_a=lambda i: (i, 0),
                                pieces=[(None, 0, N_DZ)], after=tkn, then=("rms_bwd", xs, g_mix, dx1),
                                name="mm_d_h_rms")
    (g_mix_sent,), tkn = _split_start("g_mix_start", [to_all([dg_mix])])
    reduce_done(ffn_keys, ffn_red, (grad_x, tkn))
    sent = list(_split_wait("small_wait", small_sent, _gather_direct, res["w_down"][0]))
    small.append(("g_mix", dg_mix, g_mix, m_g_mix, v_g_mix, False))
    sent[-2:-2] = _split_wait("g_mix_wait", g_mix_sent, _gather_direct, sent[1])
    own, gathered = sent[0::2], sent[1::2]
    small_out, loss_sum = _small_sum_adamw(jnp.reshape(me, (1,)).astype(jnp.int32),
                                           [(o, p) + t[2:] for o, p, t in zip(own, gathered, small)],
                                           (own[-1], gathered[-1]))
    for t, outs in zip(small, small_out):
        res[t[0]] = list(outs)
    res["g_final"] = [t.reshape(g_final.shape) for t in res["g_final"]]

    reduce_done(in_keys, in_red, loss_sum)
    loss = loss_sum[0, 0]
    order =["g_mix", "w_in", "b_gate", "w_gk_up", "b_gk", "w_pool_grp", "pool_scale", "g_gla_head", "w_pool_proj",
             "w_gla_proj", "w_out", "g_ffn", "w_up", "w_conv", "b_conv", "w_down", "g_final"]
    return (loss, grad_x[None], *[res[k][0] for k in order], *[res[k][1] for k in order],
            *[res[k][2] for k in order], *[res[k][3] for k in order])
```
